```python
import jax, jax.numpy as jnp
from jax import lax
import numpy as np

D_MODEL = 2048
BATCH = 8
SEQ = 8192
DEPTH = 1

CHUNK = 64
Q_BLOCK = 128
EPS = 1e-6

CONV_WIDTH = 1024
CONV_K = 3

MLA_HEADS = 16
Q_LORA = 512
KV_LORA = 512
QK_NOPE = 128
QK_ROPE = 64
V_HEAD = 128
MLA_WIDTH = MLA_HEADS * V_HEAD
ROPE_THETA = 10000.0

MEM_TOKENS = 256
MEM_HEADS = 4
MEM_HEAD_DIM = 256
MEM_WIDTH = MEM_HEADS * MEM_HEAD_DIM

IN_SPLITS = (
    CONV_WIDTH, CONV_WIDTH, CONV_WIDTH, CONV_WIDTH,
    Q_LORA, KV_LORA, QK_ROPE, MLA_WIDTH,
    MEM_WIDTH, MEM_WIDTH,
    D_MODEL, D_MODEL, D_MODEL,
)
IN_WIDTH = sum(IN_SPLITS)

kernel_name = "hybrid_conv_mla_memory_block"


def rms_norm(x, g):
    xf = x.astype(jnp.float32)
    y = xf * lax.rsqrt(jnp.mean(xf * xf, axis=-1, keepdims=True) + EPS)
    return (y * g.astype(jnp.float32)).astype(x.dtype)


def apply_rope(x, cos, sin):
    x1, x2 = jnp.split(x.astype(jnp.float32), 2, axis=-1)
    return jnp.concatenate([x1 * cos - x2 * sin, x2 * cos + x1 * sin], axis=-1).astype(x.dtype)


def causal_depthwise_conv(u, w):
    k, c = w.shape
    return lax.conv_general_dilated(
        u, w[:, None, :].astype(u.dtype), window_strides=(1,), padding=[(k - 1, 0)],
        dimension_numbers=("NWC", "WIO", "NWC"), feature_group_count=c)


def mla_attention(c_q, c_kv, k_rope_raw, cos, sin, q_norm_g, w_uq, kv_norm_g, w_ukv,
                  qn_nope_g, qn_rope_g, kn_nope_g, kn_rope_g):
    b, s, _ = c_q.shape
    q = (rms_norm(c_q, q_norm_g) @ w_uq).reshape(b, s, MLA_HEADS, QK_NOPE + QK_ROPE)
    q_nope = rms_norm(q[..., :QK_NOPE], qn_nope_g)
    q_rope = apply_rope(rms_norm(q[..., QK_NOPE:], qn_rope_g), cos[:, :, None], sin[:, :, None])
    kv = (rms_norm(c_kv, kv_norm_g) @ w_ukv).reshape(b, s, MLA_HEADS, QK_NOPE + V_HEAD)
    k_nope = rms_norm(kv[..., :QK_NOPE], kn_nope_g)
    v = kv[..., QK_NOPE:]
    k_rope = apply_rope(rms_norm(k_rope_raw, kn_rope_g), cos, sin)
    scale = (QK_NOPE + QK_ROPE) ** -0.5
    n_blk = s // Q_BLOCK
    qn_blocks = q_nope.reshape(b, n_blk, Q_BLOCK, MLA_HEADS, QK_NOPE).transpose(1, 0, 2, 3, 4)
    qr_blocks = q_rope.reshape(b, n_blk, Q_BLOCK, MLA_HEADS, QK_ROPE).transpose(1, 0, 2, 3, 4)
    k_chunk = jnp.arange(s) // CHUNK

    def attend(args):
        qn, qr, blk = args
        sc = (jnp.einsum("bqhd,bkhd->bhqk", qn, k_nope, preferred_element_type=jnp.float32)
              + jnp.einsum("bqhr,bkr->bhqk", qr, k_rope, preferred_element_type=jnp.float32))
        q_chunk = (blk * Q_BLOCK + jnp.arange(Q_BLOCK)) // CHUNK
        allowed = k_chunk[None, :] <= q_chunk[:, None]
        p = jax.nn.softmax(jnp.where(allowed, sc * scale, -jnp.inf), axis=-1)
        return jnp.einsum("bhqk,bkhd->bqhd", p.astype(v.dtype), v)

    o = lax.map(attend, (qn_blocks, qr_blocks, jnp.arange(n_blk)))
    return o.transpose(1, 0, 2, 3, 4).reshape(b, s, MLA_WIDTH)


def memory_attention(q_raw, mem, mem_norm_g, w_mem_kv, qn_g, kn_g):
    b, s, _ = q_raw.shape
    m = mem.shape[1]
    q = rms_norm(q_raw.reshape(b, s, MEM_HEADS, MEM_HEAD_DIM), qn_g)
    k, v = jnp.split(rms_norm(mem, mem_norm_g) @ w_mem_kv, 2, axis=-1)
    k = rms_norm(k.reshape(b, m, MEM_HEADS, MEM_HEAD_DIM), kn_g)
    v = v.reshape(b, m, MEM_HEADS, MEM_HEAD_DIM)
    sc = jnp.einsum("bqhd,bmhd->bhqm", q, k, preferred_element_type=jnp.float32) * (MEM_HEAD_DIM ** -0.5)
    p = jax.nn.softmax(sc, axis=-1)
    return jnp.einsum("bhqm,bmhd->bqhd", p.astype(v.dtype), v).reshape(b, s, MEM_WIDTH)


def _fwd_setup_inputs(seed: int = 0) -> dict:
    key = jax.random.key(seed)
    ks = jax.random.split(key, 24)
    f32 = jnp.float32

    def w(k, shape, fan_in):
        return jax.random.normal(k, shape, f32) * (fan_in ** -0.5)

    def gain(k, shape):
        return 1.0 + 0.02 * jax.random.normal(k, shape, f32)

    x = jax.random.normal(ks[0], (BATCH, SEQ, D_MODEL), f32)
    offsets = jax.random.randint(ks[1], (BATCH, 1), 0, 64, dtype=jnp.int32) * CHUNK
    positions = (offsets + jnp.arange(SEQ, dtype=jnp.int32)[None, :]).astype(jnp.int32)
    mem = jax.random.normal(ks[2], (BATCH, MEM_TOKENS, D_MODEL), f32)
    L = DEPTH
    return {
        "x": x,
        "positions": positions,
        "mem": mem,
        "norm_g": gain(ks[3], (L, D_MODEL)),
        "w_in": w(ks[4], (L, D_MODEL, IN_WIDTH), D_MODEL),
        "conv_w": w(ks[5], (L, CONV_K, CONV_WIDTH), CONV_K),
        "w_conv_out": w(ks[6], (L, CONV_WIDTH, D_MODEL), CONV_WIDTH),
        "mla_q_norm_g": gain(ks[7], (L, Q_LORA)),
        "w_uq": w(ks[8], (L, Q_LORA, MLA_HEADS * (QK_NOPE + QK_ROPE)), Q_LORA),
        "mla_kv_norm_g": gain(ks[9], (L, KV_LORA)),
        "w_ukv": w(ks[10], (L, KV_LORA, MLA_HEADS * (QK_NOPE + V_HEAD)), KV_LORA),
        "mla_qn_nope_g": gain(ks[11], (L, QK_NOPE)),
        "mla_qn_rope_g": gain(ks[12], (L, QK_ROPE)),
        "mla_kn_nope_g": gain(ks[13], (L, QK_NOPE)),
        "mla_kn_rope_g": gain(ks[14], (L, QK_ROPE)),
        "w_mla_out": w(ks[15], (L, MLA_WIDTH, D_MODEL), MLA_WIDTH),
        "mem_norm_g": gain(ks[16], (L, D_MODEL)),
        "w_mem_kv": w(ks[17], (L, D_MODEL, 2 * MEM_WIDTH), D_MODEL),
        "mem_qn_g": gain(ks[18], (L, MEM_HEAD_DIM)),
        "mem_kn_g": gain(ks[19], (L, MEM_HEAD_DIM)),
        "w_mem_out": w(ks[20], (L, MEM_WIDTH, D_MODEL), MEM_WIDTH),
        "w_o": w(ks[21], (L, D_MODEL, D_MODEL), D_MODEL),
    }


def _fwd_reference(x, positions, mem, norm_g, w_in, conv_w, w_conv_out, mla_q_norm_g, w_uq,
              mla_kv_norm_g, w_ukv, mla_qn_nope_g, mla_qn_rope_g, mla_kn_nope_g, mla_kn_rope_g,
              w_mla_out, mem_norm_g, w_mem_kv, mem_qn_g, mem_kn_g, w_mem_out, w_o):
    half = QK_ROPE // 2
    inv_freq = jnp.power(ROPE_THETA, -jnp.arange(half, dtype=jnp.float32) / half)
    ang = positions.astype(jnp.float32)[..., None] * inv_freq
    cos, sin = jnp.cos(ang), jnp.sin(ang)
    split_at = np.cumsum(IN_SPLITS)[:-1].tolist()

    for l in range(DEPTH):
        h = rms_norm(x, norm_g[l])
        proj = h @ w_in[l]
        (c_gate, b_gate, u, conv_z, c_q, c_kv, k_rope_raw, mla_z,
         mem_q, mem_z, g_conv, g_mla, g_mem) = jnp.split(proj, split_at, axis=-1)

        conv_y = b_gate * causal_depthwise_conv(c_gate * u, conv_w[l])
        o_conv = (conv_y * jax.nn.silu(conv_z)) @ w_conv_out[l]

        mla_y = mla_attention(c_q, c_kv, k_rope_raw, cos, sin, mla_q_norm_g[l], w_uq[l],
                              mla_kv_norm_g[l], w_ukv[l], mla_qn_nope_g[l], mla_qn_rope_g[l],
                              mla_kn_nope_g[l], mla_kn_rope_g[l])
        o_mla = (mla_y * jax.nn.silu(mla_z)) @ w_mla_out[l]

        mem_y = memory_attention(mem_q, mem, mem_norm_g[l], w_mem_kv[l], mem_qn_g[l], mem_kn_g[l])
        o_mem = (mem_y * jax.nn.silu(mem_z)) @ w_mem_out[l]

        merged = (jax.nn.sigmoid(g_conv) * o_conv + jax.nn.sigmoid(g_mla) * o_mla
                  + jax.nn.sigmoid(g_mem) * o_mem)
        x = x + merged @ w_o[l]
    return x


import jax as _jax
import jax.numpy as _jnp

TWIN_FORMAT = 'train_step'
FWD_PARAMS = ['x', 'positions', 'mem', 'norm_g', 'w_in', 'conv_w', 'w_conv_out', 'mla_q_norm_g', 'w_uq', 'mla_kv_norm_g', 'w_ukv', 'mla_qn_nope_g', 'mla_qn_rope_g', 'mla_kn_nope_g', 'mla_kn_rope_g', 'w_mla_out', 'mem_norm_g', 'w_mem_kv', 'mem_qn_g', 'mem_kn_g', 'w_mem_out', 'w_o']
TWIN_WEIGHTS = ['norm_g', 'w_in', 'conv_w', 'w_conv_out', 'mla_q_norm_g', 'w_uq', 'mla_kv_norm_g', 'w_ukv', 'mla_qn_nope_g', 'mla_qn_rope_g', 'mla_kn_nope_g', 'mla_kn_rope_g', 'w_mla_out', 'mem_norm_g', 'w_mem_kv', 'mem_qn_g', 'mem_kn_g', 'w_mem_out', 'w_o']
TWIN_DIFF_INPUT = 'x'
TWIN_INPUTS = ['x', 'positions', 'mem', 'norm_g', 'w_in', 'conv_w', 'w_conv_out', 'mla_q_norm_g', 'w_uq', 'mla_kv_norm_g', 'w_ukv', 'mla_qn_nope_g', 'mla_qn_rope_g', 'mla_kn_nope_g', 'mla_kn_rope_g', 'w_mla_out', 'mem_norm_g', 'w_mem_kv', 'mem_qn_g', 'mem_kn_g', 'w_mem_out', 'w_o', 'loss_target', 'm_norm_g', 'm_w_in', 'm_conv_w', 'm_w_conv_out', 'm_mla_q_norm_g', 'm_w_uq', 'm_mla_kv_norm_g', 'm_w_ukv', 'm_mla_qn_nope_g', 'm_mla_qn_rope_g', 'm_mla_kn_nope_g', 'm_mla_kn_rope_g', 'm_w_mla_out', 'm_mem_norm_g', 'm_w_mem_kv', 'm_mem_qn_g', 'm_mem_kn_g', 'm_w_mem_out', 'm_w_o', 'v_norm_g', 'v_w_in', 'v_conv_w', 'v_w_conv_out', 'v_mla_q_norm_g', 'v_w_uq', 'v_mla_kv_norm_g', 'v_w_ukv', 'v_mla_qn_nope_g', 'v_mla_qn_rope_g', 'v_mla_kn_nope_g', 'v_mla_kn_rope_g', 'v_w_mla_out', 'v_mem_norm_g', 'v_w_mem_kv', 'v_mem_qn_g', 'v_mem_kn_g', 'v_w_mem_out', 'v_w_o']
TWIN_OUTPUTS = ['loss', 'grad_x', 'grad_norm_g', 'grad_w_in', 'grad_conv_w', 'grad_w_conv_out', 'grad_mla_q_norm_g', 'grad_w_uq', 'grad_mla_kv_norm_g', 'grad_w_ukv', 'grad_mla_qn_nope_g', 'grad_mla_qn_rope_g', 'grad_mla_kn_nope_g', 'grad_mla_kn_rope_g', 'grad_w_mla_out', 'grad_mem_norm_g', 'grad_w_mem_kv', 'grad_mem_qn_g', 'grad_mem_kn_g', 'grad_w_mem_out', 'grad_w_o', 'delta_norm_g', 'delta_w_in', 'delta_conv_w', 'delta_w_conv_out', 'delta_mla_q_norm_g', 'delta_w_uq', 'delta_mla_kv_norm_g', 'delta_w_ukv', 'delta_mla_qn_nope_g', 'delta_mla_qn_rope_g', 'delta_mla_kn_nope_g', 'delta_mla_kn_rope_g', 'delta_w_mla_out', 'delta_mem_norm_g', 'delta_w_mem_kv', 'delta_mem_qn_g', 'delta_mem_kn_g', 'delta_w_mem_out', 'delta_w_o', 'new_m_norm_g', 'new_m_w_in', 'new_m_conv_w', 'new_m_w_conv_out', 'new_m_mla_q_norm_g', 'new_m_w_uq', 'new_m_mla_kv_norm_g', 'new_m_w_ukv', 'new_m_mla_qn_nope_g', 'new_m_mla_qn_rope_g', 'new_m_mla_kn_nope_g', 'new_m_mla_kn_rope_g', 'new_m_w_mla_out', 'new_m_mem_norm_g', 'new_m_w_mem_kv', 'new_m_mem_qn_g', 'new_m_mem_kn_g', 'new_m_w_mem_out', 'new_m_w_o', 'new_v_norm_g', 'new_v_w_in', 'new_v_conv_w', 'new_v_w_conv_out', 'new_v_mla_q_norm_g', 'new_v_w_uq', 'new_v_mla_kv_norm_g', 'new_v_w_ukv', 'new_v_mla_qn_nope_g', 'new_v_mla_qn_rope_g', 'new_v_mla_kn_nope_g', 'new_v_mla_kn_rope_g', 'new_v_w_mla_out', 'new_v_mem_norm_g', 'new_v_w_mem_kv', 'new_v_mem_qn_g', 'new_v_mem_kn_g', 'new_v_w_mem_out', 'new_v_w_o']
TWIN_LEAF_KINDS = {'loss': 'loss', 'grad_x': 'grad_x', 'grad_norm_g': 'grad_w', 'grad_w_in': 'grad_w', 'grad_conv_w': 'grad_w', 'grad_w_conv_out': 'grad_w', 'grad_mla_q_norm_g': 'grad_w', 'grad_w_uq': 'grad_w', 'grad_mla_kv_norm_g': 'grad_w', 'grad_w_ukv': 'grad_w', 'grad_mla_qn_nope_g': 'grad_w', 'grad_mla_qn_rope_g': 'grad_w', 'grad_mla_kn_nope_g': 'grad_w', 'grad_mla_kn_rope_g': 'grad_w', 'grad_w_mla_out': 'grad_w', 'grad_mem_norm_g': 'grad_w', 'grad_w_mem_kv': 'grad_w', 'grad_mem_qn_g': 'grad_w', 'grad_mem_kn_g': 'grad_w', 'grad_w_mem_out': 'grad_w', 'grad_w_o': 'grad_w', 'delta_norm_g': 'delta_w', 'delta_w_in': 'delta_w', 'delta_conv_w': 'delta_w', 'delta_w_conv_out': 'delta_w', 'delta_mla_q_norm_g': 'delta_w', 'delta_w_uq': 'delta_w', 'delta_mla_kv_norm_g': 'delta_w', 'delta_w_ukv': 'delta_w', 'delta_mla_qn_nope_g': 'delta_w', 'delta_mla_qn_rope_g': 'delta_w', 'delta_mla_kn_nope_g': 'delta_w', 'delta_mla_kn_rope_g': 'delta_w', 'delta_w_mla_out': 'delta_w', 'delta_mem_norm_g': 'delta_w', 'delta_w_mem_kv': 'delta_w', 'delta_mem_qn_g': 'delta_w', 'delta_mem_kn_g': 'delta_w', 'delta_w_mem_out': 'delta_w', 'delta_w_o': 'delta_w', 'new_m_norm_g': 'new_m', 'new_m_w_in': 'new_m', 'new_m_conv_w': 'new_m', 'new_m_w_conv_out': 'new_m', 'new_m_mla_q_norm_g': 'new_m', 'new_m_w_uq': 'new_m', 'new_m_mla_kv_norm_g': 'new_m', 'new_m_w_ukv': 'new_m', 'new_m_mla_qn_nope_g': 'new_m', 'new_m_mla_qn_rope_g': 'new_m', 'new_m_mla_kn_nope_g': 'new_m', 'new_m_mla_kn_rope_g': 'new_m', 'new_m_w_mla_out': 'new_m', 'new_m_mem_norm_g': 'new_m', 'new_m_w_mem_kv': 'new_m', 'new_m_mem_qn_g': 'new_m', 'new_m_mem_kn_g': 'new_m', 'new_m_w_mem_out': 'new_m', 'new_m_w_o': 'new_m', 'new_v_norm_g': 'new_v', 'new_v_w_in': 'new_v', 'new_v_conv_w': 'new_v', 'new_v_w_conv_out': 'new_v', 'new_v_mla_q_norm_g': 'new_v', 'new_v_w_uq': 'new_v', 'new_v_mla_kv_norm_g': 'new_v', 'new_v_w_ukv': 'new_v', 'new_v_mla_qn_nope_g': 'new_v', 'new_v_mla_qn_rope_g': 'new_v', 'new_v_mla_kn_nope_g': 'new_v', 'new_v_mla_kn_rope_g': 'new_v', 'new_v_w_mla_out': 'new_v', 'new_v_mem_norm_g': 'new_v', 'new_v_w_mem_kv': 'new_v', 'new_v_mem_qn_g': 'new_v', 'new_v_mem_kn_g': 'new_v', 'new_v_w_mem_out': 'new_v', 'new_v_w_o': 'new_v'}


def _forward(args):
    return _fwd_reference(*[args[k] for k in FWD_PARAMS])


def _output_shape():
    def fwd():
        inp = _fwd_setup_inputs(0)
        return _fwd_reference(*[inp[k] for k in FWD_PARAMS])
    out = _jax.eval_shape(fwd)
    return out.shape, out.dtype

N_MICROBATCH = 1
ADAM_LR = 0.001
ADAM_B1 = 0.9
ADAM_B2 = 0.999
ADAM_EPS = 1e-08
ADAM_WD = 0.01
ADAM_STEP = 10
PER_EXAMPLE_BATCH_AXIS = {'x': 0, 'positions': 0, 'mem': 0, 'loss_target': 0}
SHARED_INPUTS = []
_WEIGHT_DTYPES = {'norm_g': _jnp.float32, 'w_in': _jnp.float32, 'conv_w': _jnp.float32, 'w_conv_out': _jnp.float32, 'mla_q_norm_g': _jnp.float32, 'w_uq': _jnp.float32, 'mla_kv_norm_g': _jnp.float32, 'w_ukv': _jnp.float32, 'mla_qn_nope_g': _jnp.float32, 'mla_qn_rope_g': _jnp.float32, 'mla_kn_nope_g': _jnp.float32, 'mla_kn_rope_g': _jnp.float32, 'w_mla_out': _jnp.float32, 'mem_norm_g': _jnp.float32, 'w_mem_kv': _jnp.float32, 'mem_qn_g': _jnp.float32, 'mem_kn_g': _jnp.float32, 'w_mem_out': _jnp.float32, 'w_o': _jnp.float32}
MOMENT_SCALE = {'norm_g': 1.368065e+01, 'w_in': 1.343455e-01, 'conv_w': 3.563711e+00, 'w_conv_out': 1.203680e-01, 'mla_q_norm_g': 1.760648e-02, 'w_uq': 7.128941e-03, 'mla_kv_norm_g': 5.842140e-02, 'w_ukv': 8.267772e-03, 'mla_qn_nope_g': 1.287437e-01, 'mla_qn_rope_g': 1.111912e-01, 'mla_kn_nope_g': 1.285032e-01, 'mla_kn_rope_g': 1.107816e-01, 'w_mla_out': 9.020600e-03, 'mem_norm_g': 2.771295e-02, 'w_mem_kv': 9.774532e-03, 'mem_qn_g': 2.544691e-01, 'mem_kn_g': 2.543063e-01, 'w_mem_out': 6.708720e-03, 'w_o': 9.599010e-02}


def _to_microbatches(a, axis):
    t = _jnp.moveaxis(a, axis, 0)
    t = t.reshape((N_MICROBATCH, t.shape[0] // N_MICROBATCH) + t.shape[1:])
    return _jnp.moveaxis(t, 1, axis + 1)


def setup_inputs(seed: int = 0) -> dict:
    inp = _fwd_setup_inputs(seed)
    key = _jax.random.fold_in(_jax.random.key(seed), 7919)
    shape, _ = _output_shape()
    out = dict(inp)
    out["loss_target"] = _jax.random.normal(_jax.random.fold_in(key, 0), shape, _jnp.float32)
    for i, name in enumerate(TWIN_WEIGHTS):
        w = inp[name].astype(_jnp.float32)
        if MOMENT_SCALE is None:
            s = _jnp.sqrt(_jnp.mean(_jnp.square(w)) + 1e-30)
        else:
            s = MOMENT_SCALE[name]
        km, kv = _jax.random.split(_jax.random.fold_in(key, i + 1))
        out[name] = w
        out["m_" + name] = s * _jax.random.normal(km, w.shape, _jnp.float32)
        out["v_" + name] = (s * s) * _jax.random.uniform(kv, w.shape, _jnp.float32, 0.5, 1.5)
    if N_MICROBATCH > 1:
        for name, axis in PER_EXAMPLE_BATCH_AXIS.items():
            out[name] = _to_microbatches(out[name], axis)
    return {'x': out['x'], 'positions': out['positions'], 'mem': out['mem'], 'norm_g': out['norm_g'], 'w_in': out['w_in'], 'conv_w': out['conv_w'], 'w_conv_out': out['w_conv_out'], 'mla_q_norm_g': out['mla_q_norm_g'], 'w_uq': out['w_uq'], 'mla_kv_norm_g': out['mla_kv_norm_g'], 'w_ukv': out['w_ukv'], 'mla_qn_nope_g': out['mla_qn_nope_g'], 'mla_qn_rope_g': out['mla_qn_rope_g'], 'mla_kn_nope_g': out['mla_kn_nope_g'], 'mla_kn_rope_g': out['mla_kn_rope_g'], 'w_mla_out': out['w_mla_out'], 'mem_norm_g': out['mem_norm_g'], 'w_mem_kv': out['w_mem_kv'], 'mem_qn_g': out['mem_qn_g'], 'mem_kn_g': out['mem_kn_g'], 'w_mem_out': out['w_mem_out'], 'w_o': out['w_o'], 'loss_target': out['loss_target'], 'm_norm_g': out['m_norm_g'], 'm_w_in': out['m_w_in'], 'm_conv_w': out['m_conv_w'], 'm_w_conv_out': out['m_w_conv_out'], 'm_mla_q_norm_g': out['m_mla_q_norm_g'], 'm_w_uq': out['m_w_uq'], 'm_mla_kv_norm_g': out['m_mla_kv_norm_g'], 'm_w_ukv': out['m_w_ukv'], 'm_mla_qn_nope_g': out['m_mla_qn_nope_g'], 'm_mla_qn_rope_g': out['m_mla_qn_rope_g'], 'm_mla_kn_nope_g': out['m_mla_kn_nope_g'], 'm_mla_kn_rope_g': out['m_mla_kn_rope_g'], 'm_w_mla_out': out['m_w_mla_out'], 'm_mem_norm_g': out['m_mem_norm_g'], 'm_w_mem_kv': out['m_w_mem_kv'], 'm_mem_qn_g': out['m_mem_qn_g'], 'm_mem_kn_g': out['m_mem_kn_g'], 'm_w_mem_out': out['m_w_mem_out'], 'm_w_o': out['m_w_o'], 'v_norm_g': out['v_norm_g'], 'v_w_in': out['v_w_in'], 'v_conv_w': out['v_conv_w'], 'v_w_conv_out': out['v_w_conv_out'], 'v_mla_q_norm_g': out['v_mla_q_norm_g'], 'v_w_uq': out['v_w_uq'], 'v_mla_kv_norm_g': out['v_mla_kv_norm_g'], 'v_w_ukv': out['v_w_ukv'], 'v_mla_qn_nope_g': out['v_mla_qn_nope_g'], 'v_mla_qn_rope_g': out['v_mla_qn_rope_g'], 'v_mla_kn_nope_g': out['v_mla_kn_nope_g'], 'v_mla_kn_rope_g': out['v_mla_kn_rope_g'], 'v_w_mla_out': out['v_w_mla_out'], 'v_mem_norm_g': out['v_mem_norm_g'], 'v_w_mem_kv': out['v_w_mem_kv'], 'v_mem_qn_g': out['v_mem_qn_g'], 'v_mem_kn_g': out['v_mem_kn_g'], 'v_w_mem_out': out['v_w_mem_out'], 'v_w_o': out['v_w_o']}


def _loss(weights, diff, rest, loss_target):
    with _jax.named_scope("forward"):
        args = {**rest, TWIN_DIFF_INPUT: diff, **{k: w.astype(_WEIGHT_DTYPES[k]) for k, w in weights.items()}}
        y = _forward(args)
    with _jax.named_scope("loss_head"):
        err = _jnp.square(y.astype(_jnp.float32) - loss_target)
        return 0.5 * _jnp.sum(_jnp.mean(err, axis=-1)) if err.ndim else 0.5 * err


def _adamw(w, g, m, v):
    m = ADAM_B1 * m + (1.0 - ADAM_B1) * g
    v = ADAM_B2 * v + (1.0 - ADAM_B2) * _jnp.square(g)
    m_hat = m / (1.0 - ADAM_B1 ** ADAM_STEP)
    v_hat = v / (1.0 - ADAM_B2 ** ADAM_STEP)
    delta = -ADAM_LR * (m_hat / (_jnp.sqrt(v_hat) + ADAM_EPS) + ADAM_WD * w)
    return delta, m, v


def reference(x, positions, mem, norm_g, w_in, conv_w, w_conv_out, mla_q_norm_g, w_uq, mla_kv_norm_g, w_ukv, mla_qn_nope_g, mla_qn_rope_g, mla_kn_nope_g, mla_kn_rope_g, w_mla_out, mem_norm_g, w_mem_kv, mem_qn_g, mem_kn_g, w_mem_out, w_o, loss_target, m_norm_g, m_w_in, m_conv_w, m_w_conv_out, m_mla_q_norm_g, m_w_uq, m_mla_kv_norm_g, m_w_ukv, m_mla_qn_nope_g, m_mla_qn_rope_g, m_mla_kn_nope_g, m_mla_kn_rope_g, m_w_mla_out, m_mem_norm_g, m_w_mem_kv, m_mem_qn_g, m_mem_kn_g, m_w_mem_out, m_w_o, v_norm_g, v_w_in, v_conv_w, v_w_conv_out, v_mla_q_norm_g, v_w_uq, v_mla_kv_norm_g, v_w_ukv, v_mla_qn_nope_g, v_mla_qn_rope_g, v_mla_kn_nope_g, v_mla_kn_rope_g, v_w_mla_out, v_mem_norm_g, v_w_mem_kv, v_mem_qn_g, v_mem_kn_g, v_w_mem_out, v_w_o):
    given = dict(x=x, positions=positions, mem=mem, norm_g=norm_g, w_in=w_in, conv_w=conv_w, w_conv_out=w_conv_out, mla_q_norm_g=mla_q_norm_g, w_uq=w_uq, mla_kv_norm_g=mla_kv_norm_g, w_ukv=w_ukv, mla_qn_nope_g=mla_qn_nope_g, mla_qn_rope_g=mla_qn_rope_g, mla_kn_nope_g=mla_kn_nope_g, mla_kn_rope_g=mla_kn_rope_g, w_mla_out=w_mla_out, mem_norm_g=mem_norm_g, w_mem_kv=w_mem_kv, mem_qn_g=mem_qn_g, mem_kn_g=mem_kn_g, w_mem_out=w_mem_out, w_o=w_o, loss_target=loss_target, m_norm_g=m_norm_g, m_w_in=m_w_in, m_conv_w=m_conv_w, m_w_conv_out=m_w_conv_out, m_mla_q_norm_g=m_mla_q_norm_g, m_w_uq=m_w_uq, m_mla_kv_norm_g=m_mla_kv_norm_g, m_w_ukv=m_w_ukv, m_mla_qn_nope_g=m_mla_qn_nope_g, m_mla_qn_rope_g=m_mla_qn_rope_g, m_mla_kn_nope_g=m_mla_kn_nope_g, m_mla_kn_rope_g=m_mla_kn_rope_g, m_w_mla_out=m_w_mla_out, m_mem_norm_g=m_mem_norm_g, m_w_mem_kv=m_w_mem_kv, m_mem_qn_g=m_mem_qn_g, m_mem_kn_g=m_mem_kn_g, m_w_mem_out=m_w_mem_out, m_w_o=m_w_o, v_norm_g=v_norm_g, v_w_in=v_w_in, v_conv_w=v_conv_w, v_w_conv_out=v_w_conv_out, v_mla_q_norm_g=v_mla_q_norm_g, v_w_uq=v_w_uq, v_mla_kv_norm_g=v_mla_kv_norm_g, v_w_ukv=v_w_ukv, v_mla_qn_nope_g=v_mla_qn_nope_g, v_mla_qn_rope_g=v_mla_qn_rope_g, v_mla_kn_nope_g=v_mla_kn_nope_g, v_mla_kn_rope_g=v_mla_kn_rope_g, v_w_mla_out=v_w_mla_out, v_mem_norm_g=v_mem_norm_g, v_w_mem_kv=v_w_mem_kv, v_mem_qn_g=v_mem_qn_g, v_mem_kn_g=v_mem_kn_g, v_w_mem_out=v_w_mem_out, v_w_o=v_w_o)
    weights = {n: given[n] for n in TWIN_WEIGHTS}
    shared = {n: given[n] for n in SHARED_INPUTS}
    per_example = {n: given[n] for n in ['x', 'positions', 'mem']}
    grad_fn = _jax.value_and_grad(_loss, argnums=(0, 1))

    def one_microbatch(ex, loss_target):
        ex = dict(ex)
        diff = ex.pop(TWIN_DIFF_INPUT)
        return grad_fn(weights, diff, {**shared, **ex}, loss_target)

    if N_MICROBATCH == 1:
        loss, (grad_w, grad_x) = one_microbatch(per_example, given["loss_target"])
    else:
        def body(carry, xs):
            loss_sum, grad_sum = carry
            l_k, (gw_k, gx_k) = one_microbatch(xs[0], xs[1])
            with _jax.named_scope("update"):
                return (loss_sum + l_k, _jax.tree.map(_jnp.add, grad_sum, gw_k)), gx_k

        init = (_jnp.zeros((), _jnp.float32), _jax.tree.map(_jnp.zeros_like, weights))
        (loss, grad_w), grad_x = _jax.lax.scan(body, init, (per_example, given["loss_target"]))
    with _jax.named_scope("update"):
        delta_w, new_m, new_v = {}, {}, {}
        for n in TWIN_WEIGHTS:
            delta_w[n], new_m[n], new_v[n] = _adamw(weights[n], grad_w[n], given["m_" + n], given["v_" + n])
    return (loss, grad_x, *[grad_w[n] for n in TWIN_WEIGHTS], *[delta_w[n] for n in TWIN_WEIGHTS],
            *[new_m[n] for n in TWIN_WEIGHTS], *[new_v[n] for n in TWIN_WEIGHTS])
```

```python
import math

import jax
import jax.numpy as jnp
from jax import lax
from jax.experimental import pallas as pl
from jax.experimental.pallas import tpu as pltpu

F32 = jnp.float32
_BF = jnp.bfloat16
EPS = 1e-6
CHUNK = 64
ROPE_THETA = 10000.0
ADAM_LR, ADAM_B1, ADAM_B2, ADAM_EPS, ADAM_WD, ADAM_STEP = 0.001, 0.9, 0.999, 1e-08, 0.01, 10
NDEV = 8
AXES = ("x", "y", "c")
MESH = pl.DeviceIdType.MESH
LANES = 128
NEG = -1e30
V7X_VMEM_LIMIT = 56 * 1024 * 1024
PACK_C = 1024
PACK_ROWS = 256

WEIGHTS = ['norm_g', 'w_in', 'conv_w', 'w_conv_out', 'mla_q_norm_g', 'w_uq', 'mla_kv_norm_g', 'w_ukv',
           'mla_qn_nope_g', 'mla_qn_rope_g', 'mla_kn_nope_g', 'mla_kn_rope_g', 'w_mla_out', 'mem_norm_g',
           'w_mem_kv', 'mem_qn_g', 'mem_kn_g', 'w_mem_out', 'w_o']
BIG = ['w_in', 'w_conv_out', 'w_uq', 'w_ukv', 'w_mla_out', 'w_mem_kv', 'w_mem_out', 'w_o']
COL_SHARDED = ('w_in', 'w_conv_out', 'w_uq', 'w_ukv', 'w_mem_out')
SMALL = ['norm_g', 'mla_q_norm_g', 'mla_kv_norm_g', 'mla_qn_nope_g', 'mla_qn_rope_g', 'mla_kn_nope_g',
         'mla_kn_rope_g', 'mem_norm_g', 'mem_qn_g', 'mem_kn_g']


def _tile(dim, target, align):
    if dim <= target:
        return dim
    t = target - target % align
    while t > 0:
        if dim % t == 0:
            return t
        t -= align
    raise ValueError(f"no tile for {dim} {target} {align}")


def _cparams(sem, vmem=True):
    return pltpu.CompilerParams(dimension_semantics=sem, vmem_limit_bytes=V7X_VMEM_LIMIT if vmem else None)


def _sig(x):
    return 1.0 / (1.0 + jnp.exp(-x))


def _rms(x, n):
    r = lax.rsqrt(jnp.sum(x * x, axis=-1, keepdims=True) * (1.0 / n) + EPS)
    return x * r, r


def _rms_bwd(xhat, r, g, dy, n):
    dxh = dy * g
    dx = r * (dxh - xhat * (jnp.sum(dxh * xhat, axis=-1, keepdims=True) * (1.0 / n)))
    return dx, dy * xhat


def _rope(x, cosp, sina, sinb):
    return x * cosp + pltpu.roll(x, 96, 1) * sina + pltpu.roll(x, 32, 1) * sinb


def _rope_t(d, cosp, sina, sinb):
    return d * cosp + pltpu.roll(d * sina, 32, 1) + pltpu.roll(d * sinb, 96, 1)


def _dot_nt(a, b):
    return lax.dot_general(a, b, (((1,), (1,)), ((), ())), preferred_element_type=F32)


def _dot_tn(a, b):
    return lax.dot_general(a, b, (((0,), (0,)), ((), ())), preferred_element_type=F32)


def _dot(a, b):
    return jnp.dot(a, b, preferred_element_type=F32)


def _all_gather(x_shard, name):
    def body(x_ref, out_ref, send_sems, recv_sems, local_sem):
        x, y, c = lax.axis_index("x"), lax.axis_index("y"), lax.axis_index("c")
        me, sibling = (x, y, c), (x, y, 1 - c)
        chips = [(1 - x, y), (x, 1 - y), (1 - x, 1 - y)]

        def rows(px, py, pc):
            return out_ref.at[4 * px + 2 * py + pc]

        def copy(k, block, to, src=None):
            return pltpu.make_async_remote_copy(
                src_ref=rows(*block) if src is None else src, dst_ref=rows(*block),
                send_sem=send_sems.at[k], recv_sem=recv_sems.at[k], device_id=to, device_id_type=MESH)

        mine = pltpu.make_async_copy(x_ref, rows(*me), local_sem)
        mine.start()
        first = [copy(0, me, sibling, src=x_ref)]
        first += [copy(1 + j, me, (*chip, c), src=x_ref) for j, chip in enumerate(chips)]
        for cp in first:
            cp.start()
        passed = [copy(4 + j, (*chip, c), sibling) for j, chip in enumerate(chips)]
        for j, chip in enumerate(chips):
            copy(1 + j, (*chip, c), me).wait_recv()
            passed[j].start()
        copy(0, sibling, me).wait_recv()
        for j, chip in enumerate(chips):
            copy(4 + j, (*chip, 1 - c), me).wait_recv()
        for cp in first + passed:
            cp.wait_send()
        mine.wait()

    return pl.pallas_call(
        body, name=name,
        out_shape=jax.ShapeDtypeStruct((NDEV,) + x_shard.shape, x_shard.dtype),
        in_specs=[pl.BlockSpec(memory_space=pl.ANY)],
        out_specs=pl.BlockSpec(memory_space=pl.ANY),
        scratch_shapes=[pltpu.SemaphoreType.DMA((7,)), pltpu.SemaphoreType.DMA((7,)), pltpu.SemaphoreType.DMA],
    )(x_shard)


def _exchange(blocks, name):
    def body(x_ref, out_ref, send_sems, recv_sems, local_sem):
        x, y, c = lax.axis_index("x"), lax.axis_index("y"), lax.axis_index("c")
        me = 4 * x + 2 * y + c

        def copy(k):
            fx, fy, fc = (k + 1) >> 2 & 1, (k + 1) >> 1 & 1, (k + 1) & 1
            tx, ty, tc = x ^ fx, y ^ fy, c ^ fc
            return pltpu.make_async_remote_copy(
                src_ref=x_ref.at[4 * tx + 2 * ty + tc], dst_ref=out_ref.at[me],
                send_sem=send_sems.at[k], recv_sem=recv_sems.at[k], device_id=(tx, ty, tc), device_id_type=MESH)

        mine = pltpu.make_async_copy(x_ref.at[me], out_ref.at[me], local_sem)
        mine.start()
        copies = [copy(k) for k in range(7)]
        for cp in copies:
            cp.start()
        for cp in copies:
            cp.wait_recv()
        for cp in copies:
            cp.wait_send()
        mine.wait()

    return pl.pallas_call(
        body, name=name,
        out_shape=jax.ShapeDtypeStruct(blocks.shape, blocks.dtype),
        in_specs=[pl.BlockSpec(memory_space=pl.ANY)],
        out_specs=pl.BlockSpec(memory_space=pl.ANY),
        scratch_shapes=[pltpu.SemaphoreType.DMA((7,)), pltpu.SemaphoreType.DMA((7,)), pltpu.SemaphoreType.DMA],
    )(blocks)


def _seg_rows(size):
    rows = -(-size // PACK_C)
    return -(-rows // 16) * 16


def _pack(arrs, dtype):
    lead = arrs[0][1]
    parts = []
    for a, nlead in arrs:
        assert nlead == lead
        lshape = a.shape[:lead]
        f = a.reshape(lshape + (-1,)).astype(dtype)
        rows = _seg_rows(f.shape[-1])
        f = jnp.pad(f, [(0, 0)] * lead + [(0, rows * PACK_C - f.shape[-1])])
        parts.append(f.reshape(lshape + (rows, PACK_C)))
    out = jnp.concatenate(parts, axis=lead)
    pad = -out.shape[lead] % PACK_ROWS
    if pad:
        out = jnp.pad(out, [(0, 0)] * lead + [(0, pad), (0, 0)])
    return out


def _unpack(buf, shapes):
    lshape = buf.shape[:-2]
    out, r = [], 0
    for shp in shapes:
        size = math.prod(shp)
        rows = _seg_rows(size)
        seg = buf[..., r:r + rows, :].reshape(lshape + (rows * PACK_C,))[..., :size]
        out.append(seg.reshape(lshape + tuple(shp)))
        r += rows
    return out


def _mm(a, b, *, name, out_dtype, ta=False, bm=1024, bn=1024, bk=2048):
    if ta:
        kdim, m = a.shape
    else:
        m, kdim = a.shape
    k2, n = b.shape
    assert kdim == k2, (a.shape, b.shape)
    bm = _tile(m, bm, LANES if ta else 16)
    bn = _tile(n, bn, LANES)
    bk = _tile(kdim, bk, LANES)
    nk = kdim // bk

    def kern(a_ref, b_ref, o_ref, *scratch):
        part = _dot_tn(a_ref[...], b_ref[...]) if ta else _dot(a_ref[...], b_ref[...])
        if nk == 1:
            o_ref[...] = part.astype(o_ref.dtype)
        else:
            acc = scratch[0]
            k = pl.program_id(2)

            @pl.when(k == 0)
            def _():
                acc[...] = part

            @pl.when(jnp.logical_and(k > 0, k < nk - 1))
            def _():
                acc[...] += part

            @pl.when(k == nk - 1)
            def _():
                o_ref[...] = (acc[...] + part).astype(o_ref.dtype)

    a_spec = pl.BlockSpec((bk, bm), lambda i, j, k: (k, i)) if ta else pl.BlockSpec((bm, bk), lambda i, j, k: (i, k))
    return pl.pallas_call(
        kern, name=name, grid=(m // bm, n // bn, nk),
        in_specs=[a_spec, pl.BlockSpec((bk, bn), lambda i, j, k: (k, j))],
        out_specs=pl.BlockSpec((bm, bn), lambda i, j, k: (i, j)),
        out_shape=jax.ShapeDtypeStruct((m, n), out_dtype),
        scratch_shapes=[pltpu.VMEM((bm, bn), F32)] if nk > 1 else [],
        compiler_params=_cparams(("parallel", "parallel", "arbitrary")),
    )(a, b)


class _Cfg:
    pass


def _config(x, conv_w, w_uq, w_ukv, mla_qn_nope_g, mla_qn_rope_g, mem, mem_qn_g, w_mem_out, w_mla_out):
    c = _Cfg()
    c.N, c.D = x.shape[1], x.shape[2]
    c.CW = conv_w.shape[2] * NDEV
    c.QL, c.KVL = w_uq.shape[1], w_ukv.shape[1]
    c.NOPE, c.ROPE = mla_qn_nope_g.shape[1], mla_qn_rope_g.shape[1]
    c.H = w_uq.shape[2] * NDEV // (c.NOPE + c.ROPE)
    c.V = w_ukv.shape[2] * NDEV // c.H - c.NOPE
    assert c.NOPE == LANES and c.V == LANES and c.ROPE == LANES // 2
    c.HW = 2 * LANES
    c.HV = c.H * c.V
    assert w_mla_out.shape[1] * NDEV == c.HV
    c.M = mem.shape[1]
    c.MHD = mem_qn_g.shape[1]
    c.MW = w_mem_out.shape[1]
    c.MH = c.MW // c.MHD
    c.o_conv = 0
    c.o_mz = 4 * c.CW
    c.o_g = c.o_mz + c.HV
    c.o_mem = c.o_g + 3 * c.D
    c.o_lora = c.o_mem + 2 * c.MW
    c.o_kr = c.o_lora + c.QL + c.KVL
    c.P = c.o_kr + LANES
    assert c.o_mz % c.HV == 0 and c.o_g % (3 * c.D) == 0 and c.o_mem % (2 * c.MW) == 0
    assert c.o_lora % (c.QL + c.KVL) == 0 and c.QL % LANES == 0 and c.KVL % LANES == 0
    c.IN = 4 * c.CW + c.QL + c.KVL + c.ROPE + c.HV + 2 * c.MW + 3 * c.D
    c.R = _tile(c.N, 256, 16)
    c.B = _tile(c.N, min(512, max(c.N // 2, CHUNK)), CHUNK)
    c.scale = float((c.NOPE + c.ROPE) ** -0.5)
    c.mscale = float(c.MHD ** -0.5)
    return c


def _win_to_padded(w, c):
    o = 0
    parts = {}
    for nm, wd in (('conv', 4 * c.CW), ('lora', c.QL + c.KVL), ('kr', c.ROPE), ('mz', c.HV), ('mem', 2 * c.MW), ('g', 3 * c.D)):
        parts[nm] = w[:, o:o + wd]
        o += wd
    kr = jnp.pad(parts['kr'], ((0, 0), (0, LANES - c.ROPE)))
    return jnp.concatenate([parts['conv'], parts['mz'], parts['g'], parts['mem'], parts['lora'], kr], axis=1)


def _win_from_padded(g, c):
    conv = g[:, c.o_conv:c.o_mz]
    mz = g[:, c.o_mz:c.o_g]
    gates = g[:, c.o_g:c.o_mem]
    mem = g[:, c.o_mem:c.o_lora]
    lora = g[:, c.o_lora:c.o_kr]
    kr = g[:, c.o_kr:c.o_kr + c.ROPE]
    return jnp.concatenate([conv, lora, kr, mz, mem, gates], axis=1)


def kernel(x, positions, mem, norm_g, w_in, conv_w, w_conv_out, mla_q_norm_g, w_uq, mla_kv_norm_g, w_ukv, mla_qn_nope_g, mla_qn_rope_g, mla_kn_nope_g, mla_kn_rope_g, w_mla_out, mem_norm_g, w_mem_kv, mem_qn_g, mem_kn_g, w_mem_out, w_o, loss_target, m_norm_g, m_w_in, m_conv_w, m_w_conv_out, m_mla_q_norm_g, m_w_uq, m_mla_kv_norm_g, m_w_ukv, m_mla_qn_nope_g, m_mla_qn_rope_g, m_mla_kn_nope_g, m_mla_kn_rope_g, m_w_mla_out, m_mem_norm_g, m_w_mem_kv, m_mem_qn_g, m_mem_kn_g, m_w_mem_out, m_w_o, v_norm_g, v_w_in, v_conv_w, v_w_conv_out, v_mla_q_norm_g, v_w_uq, v_mla_kv_norm_g, v_w_ukv, v_mla_qn_nope_g, v_mla_qn_rope_g, v_mla_kn_nope_g, v_mla_kn_rope_g, v_w_mla_out, v_mem_norm_g, v_w_mem_kv, v_mem_qn_g, v_mem_kn_g, v_w_mem_out, v_w_o):
    args = dict(locals())
    W = {n: args[n] for n in WEIGHTS}
    Mo = {n: args['m_' + n] for n in WEIGHTS}
    Vo = {n: args['v_' + n] for n in WEIGHTS}
    c = _config(x, conv_w, w_uq, w_ukv, mla_qn_nope_g, mla_qn_rope_g, mem, mem_qn_g, w_mem_out, w_mla_out)
    N, D, R, B, H = c.N, c.D, c.R, c.B, c.H
    assert x.shape[0] == 1
    xs = x[0]
    tgt = loss_target[0]
    memx = mem[0]
    me = 4 * lax.axis_index("x") + 2 * lax.axis_index("y") + lax.axis_index("c")
    nr = N // R

    conv_bits = lax.bitcast_convert_type(conv_w[0], _BF)
    shard_list = [(W[n][0], 0) for n in BIG] + [(conv_bits, 0)]
    gathered = _all_gather(_pack(shard_list, _BF), "ag_weights")
    full = _unpack(gathered, [W[n][0].shape for n in BIG] + [conv_bits.shape])
    Wf = {}
    for n, g in zip(BIG, full[:-1]):
        if n in COL_SHARDED:
            Wf[n] = jnp.transpose(g, (1, 0, 2)).reshape(g.shape[1], -1)
        else:
            Wf[n] = g.reshape(-1, g.shape[2])
    convw = lax.bitcast_convert_type(full[-1], F32)
    convw = jnp.transpose(convw, (1, 0, 2)).reshape(3, c.CW)
    convw8 = jnp.pad(convw, ((0, 5), (0, 0)))

    win_p = _win_to_padded(Wf['w_in'], c)
    wuq = Wf['w_uq'].reshape(c.QL, H, c.NOPE + c.ROPE)
    wuq_p = jnp.pad(wuq, ((0, 0), (0, 0), (0, c.HW - c.NOPE - c.ROPE))).reshape(c.QL, H * c.HW)
    wukv = Wf['w_ukv']
    wco, wmo, wmkv, wmemo, wo = Wf['w_conv_out'], Wf['w_mla_out'], Wf['w_mem_kv'], Wf['w_mem_out'], Wf['w_o']
    win_pT, wuq_pT, wukvT = win_p.T, wuq_p.T, wukv.T
    wcoT, wmoT, wmkvT, wmemoT, woT = wco.T, wmo.T, wmkv.T, wmemo.T, wo.T

    def rowb(width, cidx):
        return pl.BlockSpec((R, width), lambda i, _c=cidx: (i, _c))

    def fullb(shape):
        nd = len(shape)
        return pl.BlockSpec(shape, lambda *_: (0,) * nd)

    def pad_lanes(g, w=LANES):
        return jnp.pad(g, ((0, 0), (0, w - g.shape[1])))

    half = c.ROPE // 2
    inv_freq = jnp.power(ROPE_THETA, -jnp.arange(half, dtype=F32) / half)
    invf = jnp.concatenate([inv_freq, inv_freq, jnp.zeros((LANES - c.ROPE,), F32)])[None, :]
    pos_col = positions[0].astype(F32).reshape(N, 1)

    def rope_tab_kern(pos_ref, invf_ref, o_ref):
        ang = pos_ref[...] * invf_ref[...]
        co, si = jnp.cos(ang), jnp.sin(ang)
        lane = lax.broadcasted_iota(jnp.int32, ang.shape, 1)
        o_ref[0] = jnp.where(lane < c.ROPE, co, 0.0)
        o_ref[1] = jnp.where(lane < half, -si, 0.0)
        o_ref[2] = jnp.where(jnp.logical_and(lane >= half, lane < c.ROPE), si, 0.0)

    tabs = pl.pallas_call(
        rope_tab_kern, name="rope_tab", grid=(nr,),
        in_specs=[pl.BlockSpec((R, 1), lambda i: (i, 0)), fullb((1, LANES))],
        out_specs=pl.BlockSpec((3, R, LANES), lambda i: (0, i, 0)),
        out_shape=jax.ShapeDtypeStruct((3, N, LANES), F32),
        compiler_params=_cparams(("parallel",)),
    )(pos_col, invf)
    tab_spec = pl.BlockSpec((3, R, LANES), lambda i, *_: (0, i, 0))

    def make_rms_kern():
        def rms_fwd_kern(x_ref, g_ref, o_ref):
            xh, _ = _rms(x_ref[...].astype(F32), x_ref.shape[-1])
            o_ref[...] = (xh * g_ref[...]).astype(o_ref.dtype)
        return rms_fwd_kern

    h = pl.pallas_call(
        make_rms_kern(), name="rms_x", grid=(nr,),
        in_specs=[rowb(D, 0), fullb((1, D))], out_specs=rowb(D, 0),
        out_shape=jax.ShapeDtypeStruct((N, D), _BF), compiler_params=_cparams(("parallel",)),
    )(xs, norm_g)

    proj = _mm(h, win_p, name="mm_proj", out_dtype=_BF, bn=1408)

    CW = c.CW
    conv_blk = c.o_conv // (4 * CW)
    HALO = 16
    rh = R // HALO

    def conv_parts(blk):
        blk = blk.astype(F32)
        return blk[:, 0:CW], blk[:, CW:2 * CW], blk[:, 2 * CW:3 * CW], blk[:, 3 * CW:4 * CW]

    def shifted(cu, prev, i):
        prev = jnp.where(i > 0, prev, 0.0)
        rid = lax.broadcasted_iota(jnp.int32, cu.shape, 0)
        last, last2 = prev[HALO - 1:HALO, :], prev[HALO - 2:HALO - 1, :]
        sh1 = jnp.where(rid == 0, last, pltpu.roll(cu, 1, 0))
        sh2 = jnp.where(rid == 0, last2, jnp.where(rid == 1, last, pltpu.roll(cu, 2, 0)))
        return sh1, sh2

    def conv_fwd_kern(p_ref, prev_ref, w_ref, o_ref):
        i = pl.program_id(0)
        cg, bg, u, z = conv_parts(p_ref[...])
        pc, _, pu, _ = conv_parts(prev_ref[...])
        cu = cg * u
        sh1, sh2 = shifted(cu, pc * pu, i)
        w = w_ref[...]
        conv = w[0:1, :] * sh2 + w[1:2, :] * sh1 + w[2:3, :] * cu
        o_ref[...] = (bg * conv * (z * _sig(z))).astype(o_ref.dtype)

    prev_spec = pl.BlockSpec((HALO, 4 * CW), lambda i: (jnp.maximum(i * rh - 1, 0), conv_blk))
    a_conv = pl.pallas_call(
        conv_fwd_kern, name="conv_fwd", grid=(nr,),
        in_specs=[rowb(4 * CW, conv_blk), prev_spec, fullb((8, CW))],
        out_specs=rowb(CW, 0), out_shape=jax.ShapeDtypeStruct((N, CW), _BF),
        compiler_params=_cparams(("parallel",)),
    )(proj, proj, convw8)
    o_conv = _mm(a_conv, wco, name="mm_oconv", out_dtype=_BF)

    QL, KVL, HW = c.QL, c.KVL, c.HW
    lora_blk = c.o_lora // (QL + KVL)

    def lora_fwd_kern(p_ref, gq_ref, gkv_ref, q_ref, kv_ref):
        blk = p_ref[...].astype(F32)
        qh, _ = _rms(blk[:, :QL], QL)
        kh, _ = _rms(blk[:, QL:], KVL)
        q_ref[...] = (qh * gq_ref[...]).astype(q_ref.dtype)
        kv_ref[...] = (kh * gkv_ref[...]).astype(kv_ref.dtype)

    cqn, ckvn = pl.pallas_call(
        lora_fwd_kern, name="lora_fwd", grid=(nr,),
        in_specs=[rowb(QL + KVL, lora_blk), fullb((1, QL)), fullb((1, KVL))],
        out_specs=[rowb(QL, 0), rowb(KVL, 0)],
        out_shape=[jax.ShapeDtypeStruct((N, QL), _BF), jax.ShapeDtypeStruct((N, KVL), _BF)],
        compiler_params=_cparams(("parallel",)),
    )(proj, mla_q_norm_g, mla_kv_norm_g)
    q_p = _mm(cqn, wuq_p, name="mm_q", out_dtype=_BF)
    kv = _mm(ckvn, wukv, name="mm_kv", out_dtype=_BF)

    g_qn, g_qr = mla_qn_nope_g, pad_lanes(mla_qn_rope_g)
    g_kn, g_kr = mla_kn_nope_g, pad_lanes(mla_kn_rope_g)
    kr_blk = c.o_kr // LANES

    def krope_fwd_kern(p_ref, t_ref, g_ref, o_ref):
        xh, _ = _rms(p_ref[...].astype(F32), c.ROPE)
        o_ref[...] = _rope(xh * g_ref[...], t_ref[0], t_ref[1], t_ref[2]).astype(o_ref.dtype)

    k_rope = pl.pallas_call(
        krope_fwd_kern, name="krope_fwd", grid=(nr,),
        in_specs=[rowb(LANES, kr_blk), tab_spec, fullb((1, LANES))],
        out_specs=rowb(LANES, 0), out_shape=jax.ShapeDtypeStruct((N, LANES), _BF),
        compiler_params=_cparams(("parallel",)),
    )(proj, tabs, g_kr)

    def q_prep_kern(q_ref, t_ref, gn_ref, gr_ref, o_ref):
        blk = q_ref[...].astype(F32)
        nh, _ = _rms(blk[:, :LANES], c.NOPE)
        rh, _ = _rms(blk[:, LANES:], c.ROPE)
        rot = _rope(rh * gr_ref[...], t_ref[0], t_ref[1], t_ref[2])
        o_ref[0] = (jnp.concatenate([nh * gn_ref[...], rot], axis=1) * c.scale).astype(o_ref.dtype)

    head_rows = pl.BlockSpec((R, HW), lambda i, hh: (i, hh))
    head_out = pl.BlockSpec((1, R, HW), lambda i, hh: (hh, i, 0))
    q_cat = pl.pallas_call(
        q_prep_kern, name="q_prep", grid=(nr, H),
        in_specs=[head_rows, tab_spec, fullb((1, LANES)), fullb((1, LANES))],
        out_specs=head_out, out_shape=jax.ShapeDtypeStruct((H, N, HW), _BF),
        compiler_params=_cparams(("parallel", "parallel")),
    )(q_p, tabs, g_qn, g_qr)

    def k_prep_kern(kv_ref, kr_ref, gn_ref, o_ref):
        kn, _ = _rms(kv_ref[:, :LANES].astype(F32), c.NOPE)
        o_ref[0] = jnp.concatenate([(kn * gn_ref[...]).astype(o_ref.dtype), kr_ref[...]], axis=1)

    k_cat = pl.pallas_call(
        k_prep_kern, name="k_prep", grid=(nr, H),
        in_specs=[head_rows, pl.BlockSpec((R, LANES), lambda i, hh: (i, 0)), fullb((1, LANES))],
        out_specs=head_out, out_shape=jax.ShapeDtypeStruct((H, N, HW), _BF),
        compiler_params=_cparams(("parallel", "parallel")),
    )(kv, k_rope, g_kn)

    nb = N // B

    def diag_mask(s):
        row = lax.broadcasted_iota(jnp.int32, s.shape, 0)
        col = lax.broadcasted_iota(jnp.int32, s.shape, 1)
        shift = CHUNK.bit_length() - 1
        allowed = jnp.right_shift(col, shift) <= jnp.right_shift(row, shift)
        return jnp.where(allowed, s, NEG)

    assert CHUNK & (CHUNK - 1) == 0

    def attn_fwd_kern(q_ref, k_ref, v_ref, o_ref, lse_ref, m_sc, l_sc, acc_sc):
        i, j = pl.program_id(1), pl.program_id(2)

        @pl.when(j == 0)
        def _():
            m_sc[...] = jnp.full(m_sc.shape, NEG, F32)
            l_sc[...] = jnp.zeros(l_sc.shape, F32)
            acc_sc[...] = jnp.zeros(acc_sc.shape, F32)

        def step(masked):
            s = _dot_nt(q_ref[0], k_ref[0])
            if masked:
                s = diag_mask(s)
            m_prev = m_sc[...]
            m_new = jnp.maximum(m_prev, jnp.max(s, axis=1, keepdims=True))
            alpha = jnp.exp(m_prev - m_new)
            p = jnp.exp(s - m_new[:, :1])
            l_sc[...] = alpha * l_sc[...] + jnp.sum(p, axis=1, keepdims=True)
            acc_sc[...] = alpha * acc_sc[...] + _dot(p.astype(_BF), v_ref[...])
            m_sc[...] = m_new

        @pl.when(j < i)
        def _():
            step(False)

        @pl.when(j == i)
        def _():
            step(True)
            o_ref[...] = (acc_sc[...] / l_sc[...]).astype(o_ref.dtype)
            lse_ref[0] = m_sc[...] + jnp.log(l_sc[...])

    mla_y, lse = pl.pallas_call(
        attn_fwd_kern, name="attn_fwd", grid=(H, nb, nb),
        in_specs=[pl.BlockSpec((1, B, HW), lambda hh, i, j: (hh, i, 0)),
                  pl.BlockSpec((1, B, HW), lambda hh, i, j: (hh, jnp.minimum(i, j), 0)),
                  pl.BlockSpec((B, LANES), lambda hh, i, j: (jnp.minimum(i, j), 2 * hh + 1))],
        out_specs=[pl.BlockSpec((B, LANES), lambda hh, i, j: (i, hh)),
                   pl.BlockSpec((1, B, LANES), lambda hh, i, j: (hh, i, 0))],
        out_shape=[jax.ShapeDtypeStruct((N, c.HV), _BF), jax.ShapeDtypeStruct((H, N, LANES), F32)],
        scratch_shapes=[pltpu.VMEM((B, LANES), F32)] * 3,
        compiler_params=_cparams(("parallel", "parallel", "arbitrary")),
    )(q_cat, k_cat, kv)

    HV = c.HV
    mz_blk = c.o_mz // HV

    def gate_fwd_kern(y_ref, z_ref, o_ref):
        z = z_ref[...].astype(F32)
        o_ref[...] = (y_ref[...].astype(F32) * (z * _sig(z))).astype(o_ref.dtype)

    a_mla = pl.pallas_call(
        gate_fwd_kern, name="gate_mla", grid=(nr,),
        in_specs=[rowb(HV, 0), rowb(HV, mz_blk)], out_specs=rowb(HV, 0),
        out_shape=jax.ShapeDtypeStruct((N, HV), _BF), compiler_params=_cparams(("parallel",)),
    )(mla_y, proj)
    o_mla = _mm(a_mla, wmo, name="mm_omla", out_dtype=_BF)

    M, MW, MH, MHD = c.M, c.MW, c.MH, c.MHD
    memn = pl.pallas_call(
        make_rms_kern(), name="rms_mem",
        grid=(1,), in_specs=[fullb((M, D)), fullb((1, D))], out_specs=fullb((M, D)),
        out_shape=jax.ShapeDtypeStruct((M, D), _BF), compiler_params=_cparams(("arbitrary",)),
    )(memx, mem_norm_g)
    kvm = _mm(memn, wmkv, name="mm_memkv", out_dtype=F32)

    def memk_fwd_kern(kv_ref, g_ref, k_ref, v_ref):
        for hh in range(MH):
            kh, _ = _rms(kv_ref[:, hh * MHD:(hh + 1) * MHD], MHD)
            k_ref[:, hh * MHD:(hh + 1) * MHD] = (kh * g_ref[...]).astype(k_ref.dtype)
        v_ref[...] = kv_ref[:, MW:].astype(v_ref.dtype)

    mem_k, mem_v = pl.pallas_call(
        memk_fwd_kern, name="memk_fwd", grid=(1,),
        in_specs=[fullb((M, 2 * MW)), fullb((1, MHD))], out_specs=[fullb((M, MW)), fullb((M, MW))],
        out_shape=[jax.ShapeDtypeStruct((M, MW), _BF)] * 2, compiler_params=_cparams(("arbitrary",)),
    )(kvm, mem_kn_g)

    mem_blk = c.o_mem // (2 * MW)

    def mem_head(qz_ref, k_ref, v_ref, g_ref, hh):
        sl = slice(hh * MHD, (hh + 1) * MHD)
        qh, r = _rms(qz_ref[:, sl].astype(F32), MHD)
        qn = (qh * g_ref[...]).astype(_BF)
        s = _dot_nt(qn, k_ref[:, sl]) * c.mscale
        e = jnp.exp(s - jnp.max(s, axis=1, keepdims=True))
        p = e / jnp.sum(e, axis=1, keepdims=True)
        y = _dot(p.astype(_BF), v_ref[:, sl])
        z = qz_ref[:, MW + hh * MHD:MW + (hh + 1) * MHD].astype(F32)
        return sl, qh, r, qn, p, y, z

    def mem_fwd_kern(qz_ref, k_ref, v_ref, g_ref, o_ref):
        for hh in range(MH):
            sl, _, _, _, _, y, z = mem_head(qz_ref, k_ref, v_ref, g_ref, hh)
            o_ref[:, sl] = (y * (z * _sig(z))).astype(o_ref.dtype)

    a_mem = pl.pallas_call(
        mem_fwd_kern, name="mem_fwd", grid=(nr,),
        in_specs=[rowb(2 * MW, mem_blk), fullb((M, MW)), fullb((M, MW)), fullb((1, MHD))],
        out_specs=rowb(MW, 0), out_shape=jax.ShapeDtypeStruct((N, MW), _BF),
        compiler_params=_cparams(("parallel",)),
    )(proj, mem_k, mem_v, mem_qn_g)
    o_mem = _mm(a_mem, wmemo, name="mm_omem", out_dtype=_BF)

    g_blk = c.o_g // (3 * D)

    def merge_fwd_kern(g_ref, oc_ref, om_ref, ome_ref, o_ref):
        g = g_ref[...].astype(F32)
        acc = _sig(g[:, :D]) * oc_ref[...].astype(F32)
        acc += _sig(g[:, D:2 * D]) * om_ref[...].astype(F32)
        acc += _sig(g[:, 2 * D:]) * ome_ref[...].astype(F32)
        o_ref[...] = acc.astype(o_ref.dtype)

    merged = pl.pallas_call(
        merge_fwd_kern, name="merge_fwd", grid=(nr,),
        in_specs=[rowb(3 * D, g_blk), rowb(D, 0), rowb(D, 0), rowb(D, 0)], out_specs=rowb(D, 0),
        out_shape=jax.ShapeDtypeStruct((N, D), _BF), compiler_params=_cparams(("parallel",)),
    )(proj, o_conv, o_mla, o_mem)
    y2 = _mm(merged, wo, name="mm_out", out_dtype=F32)

    def loss_kern(x_ref, y_ref, t_ref, dy_ref, dyb_ref, l_ref):
        e = x_ref[...] + y_ref[...] - t_ref[...]
        dy = e * (1.0 / D)
        dy_ref[...] = dy
        dyb_ref[...] = dy.astype(dyb_ref.dtype)

        @pl.when(pl.program_id(0) == 0)
        def _():
            l_ref[...] = jnp.zeros(l_ref.shape, F32)

        l_ref[...] += jnp.sum(e * e, axis=0, keepdims=True)

    dy, dyb, lpart = pl.pallas_call(
        loss_kern, name="loss", grid=(nr,),
        in_specs=[rowb(D, 0)] * 3, out_specs=[rowb(D, 0), rowb(D, 0), fullb((1, D))],
        out_shape=[jax.ShapeDtypeStruct((N, D), F32), jax.ShapeDtypeStruct((N, D), _BF),
                   jax.ShapeDtypeStruct((1, D), F32)],
        compiler_params=_cparams(("arbitrary",)),
    )(xs, y2, tgt)
    loss = lax.psum(jnp.sum(lpart) * (0.5 / D), AXES)

    G = {}
    d_merged = _mm(dyb, woT, name="mm_dmerged", out_dtype=_BF)
    G['w_o'] = _mm(merged, dyb, ta=True, name="mm_dwo", out_dtype=F32)

    dproj0 = lax.empty((N, c.P), _BF)
    any_spec = pl.BlockSpec(memory_space=pl.ANY)

    def merge_bwd_kern(dp_any, g_ref, dm_ref, oc_ref, om_ref, ome_ref, dg_ref, doc_ref, dom_ref, dome_ref):
        g = g_ref[...].astype(F32)
        dm = dm_ref[...].astype(F32)
        for idx, (o_in, d_out) in enumerate(((oc_ref, doc_ref), (om_ref, dom_ref), (ome_ref, dome_ref))):
            sg = _sig(g[:, idx * D:(idx + 1) * D])
            d_out[...] = (sg * dm).astype(d_out.dtype)
            dg_ref[:, idx * D:(idx + 1) * D] = (dm * o_in[...].astype(F32) * sg * (1.0 - sg)).astype(dg_ref.dtype)

    dproj1, d_oconv, d_omla, d_omem = pl.pallas_call(
        merge_bwd_kern, name="merge_bwd", grid=(nr,),
        in_specs=[any_spec, rowb(3 * D, g_blk), rowb(D, 0), rowb(D, 0), rowb(D, 0), rowb(D, 0)],
        out_specs=[rowb(3 * D, g_blk), rowb(D, 0), rowb(D, 0), rowb(D, 0)],
        out_shape=[jax.ShapeDtypeStruct((N, c.P), _BF)] + [jax.ShapeDtypeStruct((N, D), _BF)] * 3,
        input_output_aliases={0: 0}, compiler_params=_cparams(("parallel",)),
    )(dproj0, proj, d_merged, o_conv, o_mla, o_mem)

    G['w_conv_out'] = _mm(a_conv, d_oconv, ta=True, name="mm_dwco", out_dtype=F32)
    d_aconv = _mm(d_oconv, wcoT, name="mm_daconv", out_dtype=_BF)
    G['w_mla_out'] = _mm(a_mla, d_omla, ta=True, name="mm_dwmo", out_dtype=F32)
    d_amla = _mm(d_omla, wmoT, name="mm_damla", out_dtype=_BF)
    G['w_mem_out'] = _mm(a_mem, d_omem, ta=True, name="mm_dwmemo", out_dtype=F32)
    d_amem = _mm(d_omem, wmemoT, name="mm_damem", out_dtype=_BF)

    def conv_bwd_kern(dp_any, p_ref, prev_ref, next_ref, da_ref, dan_ref, w_ref, o_ref, dw_ref):
        i = pl.program_id(0)
        cg, bg, u, z = conv_parts(p_ref[...])
        pc, _, pu, _ = conv_parts(prev_ref[...])
        _, nbg, _, nz = conv_parts(next_ref[...])
        cu = cg * u
        sh1, sh2 = shifted(cu, pc * pu, i)
        w = w_ref[...]
        conv = w[0:1, :] * sh2 + w[1:2, :] * sh1 + w[2:3, :] * cu
        sg = _sig(z)
        sz = z * sg
        da = da_ref[...].astype(F32)
        dcy = da * sz
        d_z = da * (bg * conv) * (sg * (1.0 + z * (1.0 - sg)))
        d_b = dcy * conv
        dconv = dcy * bg
        dnext = dan_ref[...].astype(F32) * (nz * _sig(nz)) * nbg
        dnext = jnp.where(i < nr - 1, dnext, 0.0)
        rid = lax.broadcasted_iota(jnp.int32, cu.shape, 0)
        up1 = jnp.where(rid == R - 1, dnext[0:1, :], pltpu.roll(dconv, R - 1, 0))
        up2 = jnp.where(rid == R - 2, dnext[0:1, :], jnp.where(rid == R - 1, dnext[1:2, :], pltpu.roll(dconv, R - 2, 0)))
        dcu = w[2:3, :] * dconv + w[1:2, :] * up1 + w[0:1, :] * up2
        o_ref[:, 0:CW] = (dcu * u).astype(o_ref.dtype)
        o_ref[:, CW:2 * CW] = d_b.astype(o_ref.dtype)
        o_ref[:, 2 * CW:3 * CW] = (dcu * cg).astype(o_ref.dtype)
        o_ref[:, 3 * CW:4 * CW] = d_z.astype(o_ref.dtype)

        @pl.when(i == 0)
        def _():
            dw_ref[...] = jnp.zeros(dw_ref.shape, F32)

        dw_ref[0:1, :] += jnp.sum(dconv * sh2, axis=0, keepdims=True)
        dw_ref[1:2, :] += jnp.sum(dconv * sh1, axis=0, keepdims=True)
        dw_ref[2:3, :] += jnp.sum(dconv * cu, axis=0, keepdims=True)

    next_spec = pl.BlockSpec((HALO, 4 * CW), lambda i: (jnp.minimum((i + 1) * rh, N // HALO - 1), conv_blk))
    dan_spec = pl.BlockSpec((HALO, CW), lambda i: (jnp.minimum((i + 1) * rh, N // HALO - 1), 0))
    dproj2, g_convw = pl.pallas_call(
        conv_bwd_kern, name="conv_bwd", grid=(nr,),
        in_specs=[any_spec, rowb(4 * CW, conv_blk), prev_spec, next_spec, rowb(CW, 0), dan_spec, fullb((8, CW))],
        out_specs=[rowb(4 * CW, conv_blk), fullb((8, CW))],
        out_shape=[jax.ShapeDtypeStruct((N, c.P), _BF), jax.ShapeDtypeStruct((8, CW), F32)],
        input_output_aliases={0: 0}, compiler_params=_cparams(("arbitrary",)),
    )(dproj1, proj, proj, proj, d_aconv, d_aconv, convw8)

    def mem_bwd_kern(dp_any, qz_ref, da_ref, k_ref, v_ref, g_ref, o_ref, dk_ref, dv_ref, dg_ref):
        @pl.when(pl.program_id(0) == 0)
        def _():
            dk_ref[...] = jnp.zeros(dk_ref.shape, F32)
            dv_ref[...] = jnp.zeros(dv_ref.shape, F32)
            dg_ref[...] = jnp.zeros(dg_ref.shape, F32)

        for hh in range(MH):
            sl, qh, r, qn, p, y, z = mem_head(qz_ref, k_ref, v_ref, g_ref, hh)
            da = da_ref[:, sl].astype(F32)
            sg = _sig(z)
            dyh = da * (z * sg)
            o_ref[:, MW + hh * MHD:MW + (hh + 1) * MHD] = (da * y * (sg * (1.0 + z * (1.0 - sg)))).astype(o_ref.dtype)
            dyb_h = dyh.astype(_BF)
            dpm = _dot_nt(dyb_h, v_ref[:, sl])
            ds = (p * (dpm - jnp.sum(dpm * p, axis=1, keepdims=True)) * c.mscale).astype(_BF)
            dqn = _dot(ds, k_ref[:, sl])
            dk_ref[:, sl] += _dot_tn(ds, qn)
            dv_ref[:, sl] += _dot_tn(p.astype(_BF), dyb_h)
            dq, dgp = _rms_bwd(qh, r, g_ref[...], dqn, MHD)
            o_ref[:, sl] = dq.astype(o_ref.dtype)
            dg_ref[...] += jnp.sum(dgp, axis=0, keepdims=True)

    dproj3, d_memk, d_memv, g_mem_qn = pl.pallas_call(
        mem_bwd_kern, name="mem_bwd", grid=(nr,),
        in_specs=[any_spec, rowb(2 * MW, mem_blk), rowb(MW, 0), fullb((M, MW)), fullb((M, MW)), fullb((1, MHD))],
        out_specs=[rowb(2 * MW, mem_blk), fullb((M, MW)), fullb((M, MW)), fullb((1, MHD))],
        out_shape=[jax.ShapeDtypeStruct((N, c.P), _BF), jax.ShapeDtypeStruct((M, MW), F32),
                   jax.ShapeDtypeStruct((M, MW), F32), jax.ShapeDtypeStruct((1, MHD), F32)],
        input_output_aliases={0: 0}, compiler_params=_cparams(("arbitrary",)),
    )(dproj2, proj, d_amem, mem_k, mem_v, mem_qn_g)

    def memk_bwd_kern(kv_ref, dk_ref, dv_ref, g_ref, o_ref, dg_ref):
        dg = jnp.zeros((1, MHD), F32)
        for hh in range(MH):
            sl = slice(hh * MHD, (hh + 1) * MHD)
            kh, r = _rms(kv_ref[:, sl], MHD)
            dkr, dgp = _rms_bwd(kh, r, g_ref[...], dk_ref[:, sl], MHD)
            o_ref[:, sl] = dkr.astype(o_ref.dtype)
            dg += jnp.sum(dgp, axis=0, keepdims=True)
        o_ref[:, MW:] = dv_ref[...].astype(o_ref.dtype)
        dg_ref[...] = dg

    d_kvm, g_mem_kn = pl.pallas_call(
        memk_bwd_kern, name="memk_bwd", grid=(1,),
        in_specs=[fullb((M, 2 * MW)), fullb((M, MW)), fullb((M, MW)), fullb((1, MHD))],
        out_specs=[fullb((M, 2 * MW)), fullb((1, MHD))],
        out_shape=[jax.ShapeDtypeStruct((M, 2 * MW), _BF), jax.ShapeDtypeStruct((1, MHD), F32)],
        compiler_params=_cparams(("arbitrary",)),
    )(kvm, d_memk, d_memv, mem_kn_g)
    G['w_mem_kv'] = _mm(memn, d_kvm, ta=True, name="mm_dwmkv", out_dtype=F32)
    d_memn = _mm(d_kvm, wmkvT, name="mm_dmemn", out_dtype=F32)

    def memnorm_bwd_kern(x_ref, d_ref, dg_ref):
        xh, _ = _rms(x_ref[...], D)
        dg_ref[...] = jnp.sum(d_ref[...] * xh, axis=0, keepdims=True)

    g_mem_norm = pl.pallas_call(
        memnorm_bwd_kern, name="memnorm_bwd", grid=(1,),
        in_specs=[fullb((M, D)), fullb((M, D))], out_specs=fullb((1, D)),
        out_shape=jax.ShapeDtypeStruct((1, D), F32), compiler_params=_cparams(("arbitrary",)),
    )(memx, d_memn)

    def gate_bwd_kern(dp_any, da_ref, y_ref, z_ref, dy_ref, dz_ref):
        z = z_ref[...].astype(F32)
        da = da_ref[...].astype(F32)
        sg = _sig(z)
        dy_ref[...] = (da * (z * sg)).astype(dy_ref.dtype)
        dz_ref[...] = (da * y_ref[...].astype(F32) * (sg * (1.0 + z * (1.0 - sg)))).astype(dz_ref.dtype)

    d_mlay, dproj4 = pl.pallas_call(
        gate_bwd_kern, name="gate_mla_bwd", grid=(nr,),
        in_specs=[any_spec, rowb(HV, 0), rowb(HV, 0), rowb(HV, mz_blk)],
        out_specs=[rowb(HV, 0), rowb(HV, mz_blk)],
        out_shape=[jax.ShapeDtypeStruct((N, HV), _BF), jax.ShapeDtypeStruct((N, c.P), _BF)],
        input_output_aliases={0: 1}, compiler_params=_cparams(("parallel",)),
    )(dproj3, d_amla, mla_y, proj)

    def attn_bwd_kern(q_ref, k_ref, v_ref, o_ref, do_ref, lse_ref, dq_ref, dk_ref, dv_ref, dk_sc, dv_sc):
        j, i = pl.program_id(1), pl.program_id(2)

        def step(masked):
            q, k, do = q_ref[0], k_ref[0], do_ref[...]
            s = _dot_nt(q, k)
            if masked:
                s = diag_mask(s)
            p = jnp.exp(s - lse_ref[0][:, :1])
            dpm = _dot_nt(do, v_ref[...])
            delta = jnp.sum(do.astype(F32) * o_ref[...].astype(F32), axis=1, keepdims=True)
            ds = (p * (dpm - delta)).astype(_BF)
            dv_sc[...] += _dot_tn(p.astype(_BF), do)
            dk_sc[...] += _dot_tn(ds, q)
            dq_part = _dot(ds, k)
            rows = pl.ds(pl.multiple_of(i * B, B), B)

            @pl.when(j == 0)
            def _():
                dq_ref[0, rows, :] = dq_part

            @pl.when(j > 0)
            def _():
                dq_ref[0, rows, :] += dq_part

        @pl.when(i == j)
        def _():
            dk_sc[...] = jnp.zeros(dk_sc.shape, F32)
            dv_sc[...] = jnp.zeros(dv_sc.shape, F32)
            step(True)

        @pl.when(i > j)
        def _():
            step(False)

        @pl.when(i == nb - 1)
        def _():
            dk_ref[0] = dk_sc[...]
            dv_ref[0] = dv_sc[...]

    d_qcat, d_kcat, d_v = pl.pallas_call(
        attn_bwd_kern, name="attn_bwd", grid=(H, nb, nb),
        in_specs=[pl.BlockSpec((1, B, HW), lambda hh, j, i: (hh, jnp.maximum(i, j), 0)),
                  pl.BlockSpec((1, B, HW), lambda hh, j, i: (hh, j, 0)),
                  pl.BlockSpec((B, LANES), lambda hh, j, i: (j, 2 * hh + 1)),
                  pl.BlockSpec((B, LANES), lambda hh, j, i: (jnp.maximum(i, j), hh)),
                  pl.BlockSpec((B, LANES), lambda hh, j, i: (jnp.maximum(i, j), hh)),
                  pl.BlockSpec((1, B, LANES), lambda hh, j, i: (hh, jnp.maximum(i, j), 0))],
        out_specs=[pl.BlockSpec((1, N, HW), lambda hh, j, i: (hh, 0, 0)),
                   pl.BlockSpec((1, B, HW), lambda hh, j, i: (hh, j, 0)),
                   pl.BlockSpec((1, B, LANES), lambda hh, j, i: (hh, j, 0))],
        out_shape=[jax.ShapeDtypeStruct((H, N, HW), F32), jax.ShapeDtypeStruct((H, N, HW), F32),
                   jax.ShapeDtypeStruct((H, N, LANES), F32)],
        scratch_shapes=[pltpu.VMEM((B, HW), F32), pltpu.VMEM((B, LANES), F32)],
        compiler_params=_cparams(("parallel", "arbitrary", "arbitrary")),
    )(q_cat, k_cat, kv, mla_y, d_mlay, lse)

    def q_prep_bwd_kern(q_ref, dq_ref, t_ref, gn_ref, gr_ref, o_ref, dgn_ref, dgr_ref):
        @pl.when(jnp.logical_and(pl.program_id(0) == 0, pl.program_id(1) == 0))
        def _():
            dgn_ref[...] = jnp.zeros(dgn_ref.shape, F32)
            dgr_ref[...] = jnp.zeros(dgr_ref.shape, F32)

        blk = q_ref[...].astype(F32)
        d = dq_ref[0] * c.scale
        nh, rn = _rms(blk[:, :LANES], c.NOPE)
        rh, rr = _rms(blk[:, LANES:], c.ROPE)
        dn, dgn = _rms_bwd(nh, rn, gn_ref[...], d[:, :LANES], c.NOPE)
        drot = _rope_t(d[:, LANES:], t_ref[0], t_ref[1], t_ref[2])
        dr, dgr = _rms_bwd(rh, rr, gr_ref[...], drot, c.ROPE)
        o_ref[...] = jnp.concatenate([dn, dr], axis=1).astype(o_ref.dtype)
        dgn_ref[...] += jnp.sum(dgn, axis=0, keepdims=True)
        dgr_ref[...] += jnp.sum(dgr, axis=0, keepdims=True)

    d_qp, g_qn_nope, g_qn_rope = pl.pallas_call(
        q_prep_bwd_kern, name="q_prep_bwd", grid=(nr, H),
        in_specs=[head_rows, head_out, tab_spec, fullb((1, LANES)), fullb((1, LANES))],
        out_specs=[head_rows, fullb((1, LANES)), fullb((1, LANES))],
        out_shape=[jax.ShapeDtypeStruct((N, H * HW), _BF), jax.ShapeDtypeStruct((1, LANES), F32),
                   jax.ShapeDtypeStruct((1, LANES), F32)],
        compiler_params=_cparams(("arbitrary", "arbitrary")),
    )(q_p, d_qcat, tabs, g_qn, g_qr)

    def k_prep_bwd_kern(kv_ref, dk_ref, dv_ref, gn_ref, o_ref, dkr_ref, dgn_ref):
        hh = pl.program_id(1)

        @pl.when(jnp.logical_and(pl.program_id(0) == 0, hh == 0))
        def _():
            dgn_ref[...] = jnp.zeros(dgn_ref.shape, F32)

        dk = dk_ref[0]
        kn, r = _rms(kv_ref[:, :LANES].astype(F32), c.NOPE)
        dkn, dgn = _rms_bwd(kn, r, gn_ref[...], dk[:, :LANES], c.NOPE)
        o_ref[...] = jnp.concatenate([dkn, dv_ref[0]], axis=1).astype(o_ref.dtype)
        dgn_ref[...] += jnp.sum(dgn, axis=0, keepdims=True)

        @pl.when(hh == 0)
        def _():
            dkr_ref[...] = dk[:, LANES:]

        @pl.when(hh > 0)
        def _():
            dkr_ref[...] += dk[:, LANES:]

    d_kv, d_krsum, g_kn_nope = pl.pallas_call(
        k_prep_bwd_kern, name="k_prep_bwd", grid=(nr, H),
        in_specs=[head_rows, head_out, pl.BlockSpec((1, R, LANES), lambda i, hh: (hh, i, 0)), fullb((1, LANES))],
        out_specs=[head_rows, pl.BlockSpec((R, LANES), lambda i, hh: (i, 0)), fullb((1, LANES))],
        out_shape=[jax.ShapeDtypeStruct((N, H * HW), _BF), jax.ShapeDtypeStruct((N, LANES), F32),
                   jax.ShapeDtypeStruct((1, LANES), F32)],
        compiler_params=_cparams(("arbitrary", "arbitrary")),
    )(kv, d_kcat, d_v, g_kn)

    def krope_bwd_kern(dp_any, p_ref, d_ref, t_ref, g_ref, o_ref, dg_ref):
        @pl.when(pl.program_id(0) == 0)
        def _():
            dg_ref[...] = jnp.zeros(dg_ref.shape, F32)

        xh, r = _rms(p_ref[...].astype(F32), c.ROPE)
        drot = _rope_t(d_ref[...], t_ref[0], t_ref[1], t_ref[2])
        dx, dg = _rms_bwd(xh, r, g_ref[...], drot, c.ROPE)
        o_ref[...] = dx.astype(o_ref.dtype)
        dg_ref[...] += jnp.sum(dg, axis=0, keepdims=True)

    dproj5, g_kn_rope = pl.pallas_call(
        krope_bwd_kern, name="krope_bwd", grid=(nr,),
        in_specs=[any_spec, rowb(LANES, kr_blk), rowb(LANES, 0), tab_spec, fullb((1, LANES))],
        out_specs=[rowb(LANES, kr_blk), fullb((1, LANES))],
        out_shape=[jax.ShapeDtypeStruct((N, c.P), _BF), jax.ShapeDtypeStruct((1, LANES), F32)],
        input_output_aliases={0: 0}, compiler_params=_cparams(("arbitrary",)),
    )(dproj4, proj, d_krsum, tabs, g_kr)

    g_wuq_p = _mm(cqn, d_qp, ta=True, name="mm_dwuq", out_dtype=F32)
    G['w_uq'] = g_wuq_p.reshape(QL, H, HW)[:, :, :c.NOPE + c.ROPE].reshape(QL, H * (c.NOPE + c.ROPE))
    d_cqn = _mm(d_qp, wuq_pT, name="mm_dcqn", out_dtype=F32)
    G['w_ukv'] = _mm(ckvn, d_kv, ta=True, name="mm_dwukv", out_dtype=F32)
    d_ckvn = _mm(d_kv, wukvT, name="mm_dckvn", out_dtype=F32)

    def lora_bwd_kern(dp_any, p_ref, dq_ref, dkv_ref, gq_ref, gkv_ref, o_ref, dgq_ref, dgkv_ref):
        @pl.when(pl.program_id(0) == 0)
        def _():
            dgq_ref[...] = jnp.zeros(dgq_ref.shape, F32)
            dgkv_ref[...] = jnp.zeros(dgkv_ref.shape, F32)

        blk = p_ref[...].astype(F32)
        qh, rq = _rms(blk[:, :QL], QL)
        kh, rk = _rms(blk[:, QL:], KVL)
        dq, dgq = _rms_bwd(qh, rq, gq_ref[...], dq_ref[...], QL)
        dk, dgk = _rms_bwd(kh, rk, gkv_ref[...], dkv_ref[...], KVL)
        o_ref[:, :QL] = dq.astype(o_ref.dtype)
        o_ref[:, QL:] = dk.astype(o_ref.dtype)
        dgq_ref[...] += jnp.sum(dgq, axis=0, keepdims=True)
        dgkv_ref[...] += jnp.sum(dgk, axis=0, keepdims=True)

    dproj6, g_q_norm, g_kv_norm = pl.pallas_call(
        lora_bwd_kern, name="lora_bwd", grid=(nr,),
        in_specs=[any_spec, rowb(QL + KVL, lora_blk), rowb(QL, 0), rowb(KVL, 0), fullb((1, QL)), fullb((1, KVL))],
        out_specs=[rowb(QL + KVL, lora_blk), fullb((1, QL)), fullb((1, KVL))],
        out_shape=[jax.ShapeDtypeStruct((N, c.P), _BF), jax.ShapeDtypeStruct((1, QL), F32),
                   jax.ShapeDtypeStruct((1, KVL), F32)],
        input_output_aliases={0: 0}, compiler_params=_cparams(("arbitrary",)),
    )(dproj5, proj, d_cqn, d_ckvn, mla_q_norm_g, mla_kv_norm_g)

    g_win_p = _mm(h, dproj6, ta=True, name="mm_dwin", out_dtype=F32, bn=1408)
    G['w_in'] = _win_from_padded(g_win_p, c)
    d_h = _mm(dproj6, win_pT, name="mm_dh", out_dtype=F32, bk=1408)

    def final_bwd_kern(x_ref, g_ref, dh_ref, dy_ref, gx_ref, dg_ref):
        @pl.when(pl.program_id(0) == 0)
        def _():
            dg_ref[...] = jnp.zeros(dg_ref.shape, F32)

        xh, r = _rms(x_ref[...], D)
        dx, dg = _rms_bwd(xh, r, g_ref[...], dh_ref[...], D)
        gx_ref[...] = dy_ref[...] + dx
        dg_ref[...] += jnp.sum(dg, axis=0, keepdims=True)

    grad_x, g_norm = pl.pallas_call(
        final_bwd_kern, name="final_bwd", grid=(nr,),
        in_specs=[rowb(D, 0), fullb((1, D)), rowb(D, 0), rowb(D, 0)],
        out_specs=[rowb(D, 0), fullb((1, D))],
        out_shape=[jax.ShapeDtypeStruct((N, D), F32), jax.ShapeDtypeStruct((1, D), F32)],
        compiler_params=_cparams(("arbitrary",)),
    )(xs, norm_g, d_h, dy)

    def to_blocks(n, g):
        if n in COL_SHARDED:
            return jnp.transpose(g.reshape(g.shape[0], NDEV, -1), (1, 0, 2))
        return g.reshape(NDEV, -1, g.shape[1])

    send = _pack([(to_blocks(n, G[n]), 1) for n in BIG], _BF)
    recv = _exchange(send, "exchange_grads")
    big_shapes = [W[n][0].shape for n in BIG]
    w_pk = _pack([(W[n][0], 0) for n in BIG], F32)
    m_pk = _pack([(Mo[n][0], 0) for n in BIG], F32)
    v_pk = _pack([(Vo[n][0], 0) for n in BIG], F32)

    def adam_call(parts, w_a, m_a, v_a, name):
        rows, cols = w_a.shape
        rb = _tile(rows, PACK_ROWS, 8)
        bc1 = 1.0 - ADAM_B1 ** ADAM_STEP
        bc2 = 1.0 - ADAM_B2 ** ADAM_STEP

        def adam_kern(p_ref, w_ref, m_ref, v_ref, g_ref, d_ref, nm_ref, nv_ref):
            g = p_ref[0].astype(F32)
            for j in range(1, NDEV):
                g = g + p_ref[j].astype(F32)
            m_new = ADAM_B1 * m_ref[...] + (1.0 - ADAM_B1) * g
            v_new = ADAM_B2 * v_ref[...] + (1.0 - ADAM_B2) * (g * g)
            g_ref[...] = g
            nm_ref[...] = m_new
            nv_ref[...] = v_new
            d_ref[...] = -ADAM_LR * ((m_new / bc1) / (jnp.sqrt(v_new / bc2) + ADAM_EPS) + ADAM_WD * w_ref[...])

        blk = pl.BlockSpec((rb, cols), lambda i: (i, 0))
        return pl.pallas_call(
            adam_kern, name=name, grid=(rows // rb,),
            in_specs=[pl.BlockSpec((NDEV, rb, cols), lambda i: (0, i, 0)), blk, blk, blk],
            out_specs=[blk] * 4, out_shape=[jax.ShapeDtypeStruct((rows, cols), F32)] * 4,
            compiler_params=_cparams(("parallel",)),
        )(parts, w_a, m_a, v_a)

    outs_big = [_unpack(o, big_shapes) for o in adam_call(recv, w_pk, m_pk, v_pk, "adam_big")]

    small_g = {'norm_g': g_norm, 'mla_q_norm_g': g_q_norm, 'mla_kv_norm_g': g_kv_norm,
               'mla_qn_nope_g': g_qn_nope, 'mla_qn_rope_g': g_qn_rope[:, :c.ROPE], 'mla_kn_nope_g': g_kn_nope,
               'mla_kn_rope_g': g_kn_rope[:, :c.ROPE], 'mem_norm_g': g_mem_norm, 'mem_qn_g': g_mem_qn,
               'mem_kn_g': g_mem_kn}
    small_part = _pack([(small_g[n], 0) for n in SMALL] + [(g_convw[0:3, :], 0)], F32)
    small_all = _all_gather(small_part, "ag_small_grads")
    small_shapes = [W[n].shape for n in SMALL]
    pieces = _unpack(small_all, small_shapes + [(3, CW)])
    cw8 = CW // NDEV
    conv_mine = lax.dynamic_slice_in_dim(pieces[-1].reshape(NDEV, 3, NDEV, cw8), me, 1, axis=2)[:, :, 0, :]
    sm_parts = _pack([(p_, 1) for p_ in pieces[:-1]] + [(conv_mine, 1)], F32)
    sm_names = SMALL + ['conv_w']
    sm_shapes = small_shapes + [(3, cw8)]
    w_sm = _pack([(W[n], 0) for n in SMALL] + [(conv_w[0], 0)], F32)
    m_sm = _pack([(Mo[n], 0) for n in SMALL] + [(m_conv_w[0], 0)], F32)
    v_sm = _pack([(Vo[n], 0) for n in SMALL] + [(v_conv_w[0], 0)], F32)
    outs_sm = [_unpack(o, sm_shapes) for o in adam_call(sm_parts, w_sm, m_sm, v_sm, "adam_small")]

    res = [{}, {}, {}, {}]
    for k in range(4):
        for n, a in zip(BIG, outs_big[k]):
            res[k][n] = a[None]
        for n, a in zip(sm_names, outs_sm[k]):
            res[k][n] = a[None] if n == 'conv_w' else a
    return (loss, grad_x[None], *[res[0][n] for n in WEIGHTS], *[res[1][n] for n in WEIGHTS],
            *[res[2][n] for n in WEIGHTS], *[res[3][n] for n in WEIGHTS])
```

```python
import math

import jax
import jax.numpy as jnp
from jax import lax
from jax.experimental import pallas as pl
from jax.experimental.pallas import tpu as pltpu

F32 = jnp.float32
_BF = jnp.bfloat16
EPS = 1e-6
CHUNK = 64
ROPE_THETA = 10000.0
ADAM_LR, ADAM_B1, ADAM_B2, ADAM_EPS, ADAM_WD, ADAM_STEP = 0.001, 0.9, 0.999, 1e-08, 0.01, 10
NDEV = 8
AXES = ("x", "y", "c")
MESH = pl.DeviceIdType.MESH
LANES = 128
NEG = -1e30
V7X_VMEM_LIMIT = 56 * 1024 * 1024
PACK_C = 1024
ATT_BLOCK = 512
ADAM_BLOCK_ELEMS = 256 * 1024

WEIGHTS = ['norm_g', 'w_in', 'conv_w', 'w_conv_out', 'mla_q_norm_g', 'w_uq', 'mla_kv_norm_g', 'w_ukv',
           'mla_qn_nope_g', 'mla_qn_rope_g', 'mla_kn_nope_g', 'mla_kn_rope_g', 'w_mla_out', 'mem_norm_g',
           'w_mem_kv', 'mem_qn_g', 'mem_kn_g', 'w_mem_out', 'w_o']
BIG = ['w_in', 'w_conv_out', 'w_uq', 'w_ukv', 'w_mla_out', 'w_mem_kv', 'w_mem_out', 'w_o']
COL_SHARDED = ('w_in', 'w_conv_out', 'w_uq', 'w_ukv', 'w_mem_out')
SMALL = ['norm_g', 'mla_q_norm_g', 'mla_kv_norm_g', 'mla_qn_nope_g', 'mla_qn_rope_g', 'mla_kn_nope_g',
         'mla_kn_rope_g', 'mem_norm_g', 'mem_qn_g', 'mem_kn_g']


def _tile(dim, target, align):
    if dim <= target:
        return dim
    t = target - target % align
    while t > 0:
        if dim % t == 0:
            return t
        t -= align
    raise ValueError(f"no tile for {dim} {target} {align}")


def _cparams(sem):
    return pltpu.CompilerParams(dimension_semantics=sem, vmem_limit_bytes=V7X_VMEM_LIMIT)


def _sig(x):
    return 1.0 / (1.0 + jnp.exp(-x))


def _rms(x, n):
    r = lax.rsqrt(jnp.sum(x * x, axis=-1, keepdims=True) * (1.0 / n) + EPS)
    return x * r, r


def _rms_bwd(xhat, r, g, dy, n):
    dxh = dy * g
    dx = r * (dxh - xhat * (jnp.sum(dxh * xhat, axis=-1, keepdims=True) * (1.0 / n)))
    return dx, dy * xhat


def _rope(x, cosp, sina, sinb):
    return x * cosp + pltpu.roll(x, 96, 1) * sina + pltpu.roll(x, 32, 1) * sinb


def _rope_t(d, cosp, sina, sinb):
    return d * cosp + pltpu.roll(d * sina, 32, 1) + pltpu.roll(d * sinb, 96, 1)


def _dot_nt(a, b):
    return lax.dot_general(a, b, (((1,), (1,)), ((), ())), preferred_element_type=F32)


def _dot_tn(a, b):
    return lax.dot_general(a, b, (((0,), (0,)), ((), ())), preferred_element_type=F32)


def _dot(a, b):
    return jnp.dot(a, b, preferred_element_type=F32)


def _all_gather(shards, name):
    na = len(shards)

    def body(*refs):
        x_refs, out_refs = refs[:na], refs[na:2 * na]
        send_sems, recv_sems, local_sems = refs[2 * na:]
        x, y, c = lax.axis_index("x"), lax.axis_index("y"), lax.axis_index("c")
        me, sibling = (x, y, c), (x, y, 1 - c)
        chips = [(1 - x, y), (x, 1 - y), (1 - x, 1 - y)]

        def rows(a, px, py, pc):
            return out_refs[a].at[4 * px + 2 * py + pc]

        def copy(a, k, block, to, src=None):
            return pltpu.make_async_remote_copy(
                src_ref=rows(a, *block) if src is None else src, dst_ref=rows(a, *block),
                send_sem=send_sems.at[7 * a + k], recv_sem=recv_sems.at[7 * a + k],
                device_id=to, device_id_type=MESH)

        mine = [pltpu.make_async_copy(x_refs[a], rows(a, *me), local_sems.at[a]) for a in range(na)]
        for cp in mine:
            cp.start()
        first = [copy(a, 0, me, sibling, src=x_refs[a]) for a in range(na)]
        first += [copy(a, 1 + j, me, (*chip, c), src=x_refs[a]) for j, chip in enumerate(chips) for a in range(na)]
        for cp in first:
            cp.start()
        passed = []
        for j, chip in enumerate(chips):
            for a in range(na):
                copy(a, 1 + j, (*chip, c), me).wait_recv()
                fwd = copy(a, 4 + j, (*chip, c), sibling)
                fwd.start()
                passed.append(fwd)
        for a in range(na):
            copy(a, 0, sibling, me).wait_recv()
        for j, chip in enumerate(chips):
            for a in range(na):
                copy(a, 4 + j, (*chip, 1 - c), me).wait_recv()
        for cp in first + passed:
            cp.wait_send()
        for cp in mine:
            cp.wait()

    any_spec = pl.BlockSpec(memory_space=pl.ANY)
    return pl.pallas_call(
        body, name=name,
        out_shape=[jax.ShapeDtypeStruct((NDEV,) + s.shape, s.dtype) for s in shards],
        in_specs=[any_spec] * na, out_specs=[any_spec] * na,
        scratch_shapes=[pltpu.SemaphoreType.DMA((7 * na,)), pltpu.SemaphoreType.DMA((7 * na,)),
                        pltpu.SemaphoreType.DMA((na,))],
    )(*shards)


def _exchange(blocks, name):
    na = len(blocks)

    def body(*refs):
        x_refs, out_refs = refs[:na], refs[na:2 * na]
        send_sems, recv_sems, local_sems = refs[2 * na:]
        x, y, c = lax.axis_index("x"), lax.axis_index("y"), lax.axis_index("c")
        me = 4 * x + 2 * y + c

        def copy(a, k):
            fx, fy, fc = (k + 1) >> 2 & 1, (k + 1) >> 1 & 1, (k + 1) & 1
            tx, ty, tc = x ^ fx, y ^ fy, c ^ fc
            return pltpu.make_async_remote_copy(
                src_ref=x_refs[a].at[4 * tx + 2 * ty + tc], dst_ref=out_refs[a].at[me],
                send_sem=send_sems.at[7 * a + k], recv_sem=recv_sems.at[7 * a + k],
                device_id=(tx, ty, tc), device_id_type=MESH)

        mine = [pltpu.make_async_copy(x_refs[a].at[me], out_refs[a].at[me], local_sems.at[a]) for a in range(na)]
        for cp in mine:
            cp.start()
        copies = [copy(a, k) for k in range(7) for a in range(na)]
        for cp in copies:
            cp.start()
        for cp in copies:
            cp.wait_recv()
        for cp in copies:
            cp.wait_send()
        for cp in mine:
            cp.wait()

    any_spec = pl.BlockSpec(memory_space=pl.ANY)
    return pl.pallas_call(
        body, name=name,
        out_shape=[jax.ShapeDtypeStruct(b.shape, b.dtype) for b in blocks],
        in_specs=[any_spec] * na, out_specs=[any_spec] * na,
        scratch_shapes=[pltpu.SemaphoreType.DMA((7 * na,)), pltpu.SemaphoreType.DMA((7 * na,)),
                        pltpu.SemaphoreType.DMA((na,))],
    )(*blocks)


def _seg_rows(size):
    rows = -(-size // PACK_C)
    return -(-rows // 16) * 16


def _pack(arrs, lead):
    parts = []
    for a in arrs:
        lshape = a.shape[:lead]
        f = a.reshape(lshape + (-1,)).astype(F32)
        rows = _seg_rows(f.shape[-1])
        f = jnp.pad(f, [(0, 0)] * lead + [(0, rows * PACK_C - f.shape[-1])])
        parts.append(f.reshape(lshape + (rows, PACK_C)))
    return jnp.concatenate(parts, axis=lead)


def _unpack(buf, shapes):
    lshape = buf.shape[:-2]
    out, r = [], 0
    for shp in shapes:
        size = math.prod(shp)
        rows = _seg_rows(size)
        seg = buf[..., r:r + rows, :].reshape(lshape + (rows * PACK_C,))[..., :size]
        out.append(seg.reshape(lshape + tuple(shp)))
        r += rows
    return out


def _mm(a, b, *, name, out_dtype, ta=False, bm=1024, bn=1024, bk=2048):
    if ta:
        kdim, m = a.shape
    else:
        m, kdim = a.shape
    k2, n = b.shape
    assert kdim == k2, (a.shape, b.shape)
    bm = _tile(m, bm, LANES if ta else 16)
    bn = _tile(n, bn, LANES)
    bk = _tile(kdim, bk, LANES)
    nk = kdim // bk

    def kern(a_ref, b_ref, o_ref, *scratch):
        part = _dot_tn(a_ref[...], b_ref[...]) if ta else _dot(a_ref[...], b_ref[...])
        if nk == 1:
            o_ref[...] = part.astype(o_ref.dtype)
        else:
            acc = scratch[0]
            k = pl.program_id(2)

            @pl.when(k == 0)
            def _():
                acc[...] = part

            @pl.when(jnp.logical_and(k > 0, k < nk - 1))
            def _():
                acc[...] += part

            @pl.when(k == nk - 1)
            def _():
                o_ref[...] = (acc[...] + part).astype(o_ref.dtype)

    a_spec = pl.BlockSpec((bk, bm), lambda i, j, k: (k, i)) if ta else pl.BlockSpec((bm, bk), lambda i, j, k: (i, k))
    return pl.pallas_call(
        kern, name=name, grid=(m // bm, n // bn, nk),
        in_specs=[a_spec, pl.BlockSpec((bk, bn), lambda i, j, k: (k, j))],
        out_specs=pl.BlockSpec((bm, bn), lambda i, j, k: (i, j)),
        out_shape=jax.ShapeDtypeStruct((m, n), out_dtype),
        scratch_shapes=[pltpu.VMEM((bm, bn), F32)] if nk > 1 else [],
        compiler_params=_cparams(("parallel", "parallel", "arbitrary")),
    )(a, b)


def _adam(parts, w_a, m_a, v_a, name):
    rows, cols = w_a.shape
    rb = _tile(rows, max(8, ADAM_BLOCK_ELEMS // cols // 8 * 8), 8)
    bc1 = 1.0 - ADAM_B1 ** ADAM_STEP
    bc2 = 1.0 - ADAM_B2 ** ADAM_STEP

    def adam_kern(p_ref, w_ref, m_ref, v_ref, g_ref, d_ref, nm_ref, nv_ref):
        g = p_ref[0].astype(F32)
        for j in range(1, NDEV):
            g = g + p_ref[j].astype(F32)
        m_new = ADAM_B1 * m_ref[...] + (1.0 - ADAM_B1) * g
        v_new = ADAM_B2 * v_ref[...] + (1.0 - ADAM_B2) * (g * g)
        g_ref[...] = g
        nm_ref[...] = m_new
        nv_ref[...] = v_new
        d_ref[...] = -ADAM_LR * ((m_new / bc1) / (jnp.sqrt(v_new / bc2) + ADAM_EPS) + ADAM_WD * w_ref[...])

    blk = pl.BlockSpec((rb, cols), lambda i: (i, 0))
    return pl.pallas_call(
        adam_kern, name=name, grid=(rows // rb,),
        in_specs=[pl.BlockSpec((NDEV, rb, cols), lambda i: (0, i, 0)), blk, blk, blk],
        out_specs=[blk] * 4, out_shape=[jax.ShapeDtypeStruct((rows, cols), F32)] * 4,
        compiler_params=_cparams(("parallel",)),
    )(parts, w_a, m_a, v_a)


class _Cfg:
    pass


def _config(x, conv_w, w_uq, w_ukv, mla_qn_nope_g, mla_qn_rope_g, mem, mem_qn_g, w_mem_out, w_mla_out):
    c = _Cfg()
    c.N, c.D = x.shape[1], x.shape[2]
    c.CW = conv_w.shape[2] * NDEV
    c.QL, c.KVL = w_uq.shape[1], w_ukv.shape[1]
    c.NOPE, c.ROPE = mla_qn_nope_g.shape[1], mla_qn_rope_g.shape[1]
    c.H = w_uq.shape[2] * NDEV // (c.NOPE + c.ROPE)
    c.V = w_ukv.shape[2] * NDEV // c.H - c.NOPE
    assert c.NOPE == LANES and c.V == LANES and c.ROPE == LANES // 2
    c.HW = 2 * LANES
    c.HV = c.H * c.V
    assert w_mla_out.shape[1] * NDEV == c.HV
    c.M = mem.shape[1]
    c.MHD = mem_qn_g.shape[1]
    c.MW = w_mem_out.shape[1]
    c.MH = c.MW // c.MHD
    c.o_conv = 0
    c.o_mz = 4 * c.CW
    c.o_g = c.o_mz + c.HV
    c.o_mem = c.o_g + 3 * c.D
    c.o_lora = c.o_mem + 2 * c.MW
    c.o_kr = c.o_lora + c.QL + c.KVL
    c.P = c.o_kr + LANES
    assert c.o_mz % c.HV == 0 and c.o_g % (3 * c.D) == 0 and c.o_mem % (2 * c.MW) == 0
    assert c.o_lora % (c.QL + c.KVL) == 0 and c.QL % LANES == 0 and c.KVL % LANES == 0
    c.IN = 4 * c.CW + c.QL + c.KVL + c.ROPE + c.HV + 2 * c.MW + 3 * c.D
    c.R = _tile(c.N, 256, 16)
    c.RP = _tile(c.N, 512, 16)
    c.HG = _tile(c.H, 4, 1)
    c.B = _tile(c.N, ATT_BLOCK, CHUNK)
    c.scale = float((c.NOPE + c.ROPE) ** -0.5)
    c.mscale = float(c.MHD ** -0.5)
    return c


def _win_to_padded(w, c):
    o = 0
    parts = {}
    for nm, wd in (('conv', 4 * c.CW), ('lora', c.QL + c.KVL), ('kr', c.ROPE), ('mz', c.HV), ('mem', 2 * c.MW), ('g', 3 * c.D)):
        parts[nm] = w[:, o:o + wd]
        o += wd
    kr = jnp.pad(parts['kr'], ((0, 0), (0, LANES - c.ROPE)))
    return jnp.concatenate([parts['conv'], parts['mz'], parts['g'], parts['mem'], parts['lora'], kr], axis=1)


def _win_from_padded(g, c):
    conv = g[:, c.o_conv:c.o_mz]
    mz = g[:, c.o_mz:c.o_g]
    gates = g[:, c.o_g:c.o_mem]
    mem = g[:, c.o_mem:c.o_lora]
    lora = g[:, c.o_lora:c.o_kr]
    kr = g[:, c.o_kr:c.o_kr + c.ROPE]
    return jnp.concatenate([conv, lora, kr, mz, mem, gates], axis=1)


def kernel(x, positions, mem, norm_g, w_in, conv_w, w_conv_out, mla_q_norm_g, w_uq, mla_kv_norm_g, w_ukv, mla_qn_nope_g, mla_qn_rope_g, mla_kn_nope_g, mla_kn_rope_g, w_mla_out, mem_norm_g, w_mem_kv, mem_qn_g, mem_kn_g, w_mem_out, w_o, loss_target, m_norm_g, m_w_in, m_conv_w, m_w_conv_out, m_mla_q_norm_g, m_w_uq, m_mla_kv_norm_g, m_w_ukv, m_mla_qn_nope_g, m_mla_qn_rope_g, m_mla_kn_nope_g, m_mla_kn_rope_g, m_w_mla_out, m_mem_norm_g, m_w_mem_kv, m_mem_qn_g, m_mem_kn_g, m_w_mem_out, m_w_o, v_norm_g, v_w_in, v_conv_w, v_w_conv_out, v_mla_q_norm_g, v_w_uq, v_mla_kv_norm_g, v_w_ukv, v_mla_qn_nope_g, v_mla_qn_rope_g, v_mla_kn_nope_g, v_mla_kn_rope_g, v_w_mla_out, v_mem_norm_g, v_w_mem_kv, v_mem_qn_g, v_mem_kn_g, v_w_mem_out, v_w_o):
    args = dict(locals())
    W = {n: args[n] for n in WEIGHTS}
    Mo = {n: args['m_' + n] for n in WEIGHTS}
    Vo = {n: args['v_' + n] for n in WEIGHTS}
    c = _config(x, conv_w, w_uq, w_ukv, mla_qn_nope_g, mla_qn_rope_g, mem, mem_qn_g, w_mem_out, w_mla_out)
    N, D, R, B, H = c.N, c.D, c.R, c.B, c.H
    assert x.shape[0] == 1
    xs = x[0]
    tgt = loss_target[0]
    memx = mem[0]
    me = 4 * lax.axis_index("x") + 2 * lax.axis_index("y") + lax.axis_index("c")
    nr = N // R

    gathered = _all_gather([W[n][0].astype(_BF) for n in BIG] + [conv_w[0]], "ag_weights")
    Wf, WfT = {}, {}
    for n, g in zip(BIG, gathered[:-1]):
        if n in COL_SHARDED:
            Wf[n] = jnp.transpose(g, (1, 0, 2)).reshape(g.shape[1], -1)
            WfT[n] = jnp.transpose(g, (0, 2, 1)).reshape(-1, g.shape[1])
        else:
            Wf[n] = g.reshape(-1, g.shape[2])
            WfT[n] = Wf[n].T
    convw = jnp.transpose(gathered[-1], (1, 0, 2)).reshape(3, c.CW)
    convw8 = jnp.pad(convw, ((0, 5), (0, 0)))

    win_p = _win_to_padded(Wf['w_in'], c)
    win_pT = win_p.T
    wuq = Wf['w_uq'].reshape(c.QL, H, c.NOPE + c.ROPE)
    wuq_p = jnp.pad(wuq, ((0, 0), (0, 0), (0, c.HW - c.NOPE - c.ROPE))).reshape(c.QL, H * c.HW)
    wuq_pT = wuq_p.T
    wukv, wukvT = Wf['w_ukv'], WfT['w_ukv']
    wco, wmo, wmkv, wmemo, wo = Wf['w_conv_out'], Wf['w_mla_out'], Wf['w_mem_kv'], Wf['w_mem_out'], Wf['w_o']
    wcoT, wmoT, wmkvT, wmemoT, woT = WfT['w_conv_out'], WfT['w_mla_out'], WfT['w_mem_kv'], WfT['w_mem_out'], WfT['w_o']

    def rowb(width, cidx):
        return pl.BlockSpec((R, width), lambda i, _c=cidx: (i, _c))

    def fullb(shape):
        nd = len(shape)
        return pl.BlockSpec(shape, lambda *_: (0,) * nd)

    def pad_lanes(g, w=LANES):
        return jnp.pad(g, ((0, 0), (0, w - g.shape[1])))

    def tabs_of(rows):
        return pl.BlockSpec((3, rows, LANES), lambda i, *_: (0, i, 0))

    half = c.ROPE // 2
    inv_freq = jnp.power(ROPE_THETA, -jnp.arange(half, dtype=F32) / half)
    invf = jnp.concatenate([inv_freq, inv_freq, jnp.zeros((LANES - c.ROPE,), F32)])[None, :]
    pos_col = positions[0].astype(F32).reshape(N, 1)

    def rope_tab_kern(pos_ref, invf_ref, o_ref):
        ang = pos_ref[...] * invf_ref[...]
        co, si = jnp.cos(ang), jnp.sin(ang)
        lane = lax.broadcasted_iota(jnp.int32, ang.shape, 1)
        o_ref[0] = jnp.where(lane < c.ROPE, co, 0.0)
        o_ref[1] = jnp.where(lane < half, -si, 0.0)
        o_ref[2] = jnp.where(jnp.logical_and(lane >= half, lane < c.ROPE), si, 0.0)

    tabs = pl.pallas_call(
        rope_tab_kern, name="rope_tab", grid=(nr,),
        in_specs=[pl.BlockSpec((R, 1), lambda i: (i, 0)), fullb((1, LANES))],
        out_specs=tabs_of(R),
        out_shape=jax.ShapeDtypeStruct((3, N, LANES), F32),
        compiler_params=_cparams(("parallel",)),
    )(pos_col, invf)

    def make_rms_kern():
        def rms_fwd_kern(x_ref, g_ref, o_ref):
            xh, _ = _rms(x_ref[...].astype(F32), x_ref.shape[-1])
            o_ref[...] = (xh * g_ref[...]).astype(o_ref.dtype)
        return rms_fwd_kern

    h = pl.pallas_call(
        make_rms_kern(), name="rms_x", grid=(nr,),
        in_specs=[rowb(D, 0), fullb((1, D))], out_specs=rowb(D, 0),
        out_shape=jax.ShapeDtypeStruct((N, D), _BF), compiler_params=_cparams(("parallel",)),
    )(xs, norm_g)

    proj = _mm(h, win_p, name="mm_proj", out_dtype=_BF, bn=1408)

    CW = c.CW
    conv_blk = c.o_conv // (4 * CW)
    HALO = 16
    rh = R // HALO

    def conv_parts(blk):
        blk = blk.astype(F32)
        return blk[:, 0:CW], blk[:, CW:2 * CW], blk[:, 2 * CW:3 * CW], blk[:, 3 * CW:4 * CW]

    def shifted(cu, prev, i):
        prev = jnp.where(i > 0, prev, 0.0)
        rid = lax.broadcasted_iota(jnp.int32, cu.shape, 0)
        last, last2 = prev[HALO - 1:HALO, :], prev[HALO - 2:HALO - 1, :]
        sh1 = jnp.where(rid == 0, last, pltpu.roll(cu, 1, 0))
        sh2 = jnp.where(rid == 0, last2, jnp.where(rid == 1, last, pltpu.roll(cu, 2, 0)))
        return sh1, sh2

    def conv_fwd_kern(p_ref, prev_ref, w_ref, o_ref):
        i = pl.program_id(0)
        cg, bg, u, z = conv_parts(p_ref[...])
        pc, _, pu, _ = conv_parts(prev_ref[...])
        cu = cg * u
        sh1, sh2 = shifted(cu, pc * pu, i)
        w = w_ref[...]
        conv = w[0:1, :] * sh2 + w[1:2, :] * sh1 + w[2:3, :] * cu
        o_ref[...] = (bg * conv * (z * _sig(z))).astype(o_ref.dtype)

    prev_spec = pl.BlockSpec((HALO, 4 * CW), lambda i: (jnp.maximum(i * rh - 1, 0), conv_blk))
    a_conv = pl.pallas_call(
        conv_fwd_kern, name="conv_fwd", grid=(nr,),
        in_specs=[rowb(4 * CW, conv_blk), prev_spec, fullb((8, CW))],
        out_specs=rowb(CW, 0), out_shape=jax.ShapeDtypeStruct((N, CW), _BF),
        compiler_params=_cparams(("parallel",)),
    )(proj, proj, convw8)
    o_conv = _mm(a_conv, wco, name="mm_oconv", out_dtype=_BF)

    QL, KVL, HW = c.QL, c.KVL, c.HW
    lora_blk = c.o_lora // (QL + KVL)

    def lora_fwd_kern(p_ref, gq_ref, gkv_ref, q_ref, kv_ref):
        blk = p_ref[...].astype(F32)
        qh, _ = _rms(blk[:, :QL], QL)
        kh, _ = _rms(blk[:, QL:], KVL)
        q_ref[...] = (qh * gq_ref[...]).astype(q_ref.dtype)
        kv_ref[...] = (kh * gkv_ref[...]).astype(kv_ref.dtype)

    cqn, ckvn = pl.pallas_call(
        lora_fwd_kern, name="lora_fwd", grid=(nr,),
        in_specs=[rowb(QL + KVL, lora_blk), fullb((1, QL)), fullb((1, KVL))],
        out_specs=[rowb(QL, 0), rowb(KVL, 0)],
        out_shape=[jax.ShapeDtypeStruct((N, QL), _BF), jax.ShapeDtypeStruct((N, KVL), _BF)],
        compiler_params=_cparams(("parallel",)),
    )(proj, mla_q_norm_g, mla_kv_norm_g)
    q_p = _mm(cqn, wuq_p, name="mm_q", out_dtype=_BF)
    kv = _mm(ckvn, wukv, name="mm_kv", out_dtype=_BF)

    g_qn, g_qr = mla_qn_nope_g, pad_lanes(mla_qn_rope_g)
    g_kn, g_kr = mla_kn_nope_g, pad_lanes(mla_kn_rope_g)
    kr_blk = c.o_kr // LANES

    def krope_fwd_kern(p_ref, t_ref, g_ref, o_ref):
        xh, _ = _rms(p_ref[...].astype(F32), c.ROPE)
        o_ref[...] = _rope(xh * g_ref[...], t_ref[0], t_ref[1], t_ref[2]).astype(o_ref.dtype)

    k_rope = pl.pallas_call(
        krope_fwd_kern, name="krope_fwd", grid=(nr,),
        in_specs=[rowb(LANES, kr_blk), tabs_of(R), fullb((1, LANES))],
        out_specs=rowb(LANES, 0), out_shape=jax.ShapeDtypeStruct((N, LANES), _BF),
        compiler_params=_cparams(("parallel",)),
    )(proj, tabs, g_kr)

    RP, HG = c.RP, c.HG
    nrp, nhg = N // RP, H // HG
    heads_in = pl.BlockSpec((RP, HG * HW), lambda i, hg: (i, hg))
    heads_out = pl.BlockSpec((HG, RP, HW), lambda i, hg: (hg, i, 0))

    def q_prep_kern(q_ref, t_ref, gn_ref, gr_ref, o_ref):
        for g in range(HG):
            blk = q_ref[:, g * HW:(g + 1) * HW].astype(F32)
            nh, _ = _rms(blk[:, :LANES], c.NOPE)
            rhat, _ = _rms(blk[:, LANES:], c.ROPE)
            rot = _rope(rhat * gr_ref[...], t_ref[0], t_ref[1], t_ref[2])
            o_ref[g] = (jnp.concatenate([nh * gn_ref[...], rot], axis=1) * c.scale).astype(o_ref.dtype)

    q_cat = pl.pallas_call(
        q_prep_kern, name="q_prep", grid=(nrp, nhg),
        in_specs=[heads_in, tabs_of(RP), fullb((1, LANES)), fullb((1, LANES))],
        out_specs=heads_out, out_shape=jax.ShapeDtypeStruct((H, N, HW), _BF),
        compiler_params=_cparams(("parallel", "parallel")),
    )(q_p, tabs, g_qn, g_qr)

    def k_prep_kern(kv_ref, kr_ref, gn_ref, o_ref):
        for g in range(HG):
            kn, _ = _rms(kv_ref[:, g * HW:g * HW + LANES].astype(F32), c.NOPE)
            o_ref[g] = jnp.concatenate([(kn * gn_ref[...]).astype(o_ref.dtype), kr_ref[...]], axis=1)

    k_cat = pl.pallas_call(
        k_prep_kern, name="k_prep", grid=(nrp, nhg),
        in_specs=[heads_in, pl.BlockSpec((RP, LANES), lambda i, hg: (i, 0)), fullb((1, LANES))],
        out_specs=heads_out, out_shape=jax.ShapeDtypeStruct((H, N, HW), _BF),
        compiler_params=_cparams(("parallel", "parallel")),
    )(kv, k_rope, g_kn)

    nb = N // B
    assert CHUNK & (CHUNK - 1) == 0 and B % CHUNK == 0

    def diag_mask(s):
        row = lax.broadcasted_iota(jnp.int32, s.shape, 0)
        col = lax.broadcasted_iota(jnp.int32, s.shape, 1)
        shift = CHUNK.bit_length() - 1
        allowed = jnp.right_shift(col, shift) <= jnp.right_shift(row, shift)
        return jnp.where(allowed, s, NEG)

    k_head = pl.BlockSpec((1, N, HW), lambda hh, i: (hh, 0, 0))
    v_head = pl.BlockSpec((N, LANES), lambda hh, i: (0, 2 * hh + 1))
    q_blk = pl.BlockSpec((1, B, HW), lambda hh, i: (hh, i, 0))
    o_blk = pl.BlockSpec((B, LANES), lambda hh, i: (i, hh))
    lse_blk = pl.BlockSpec((1, B, LANES), lambda hh, i: (hh, i, 0))

    def attn_fwd_kern(q_ref, k_ref, v_ref, o_ref, lse_ref, m_sc, l_sc, acc_sc):
        i = pl.program_id(1)
        q = q_ref[0]
        m_sc[...] = jnp.full(m_sc.shape, NEG, F32)
        l_sc[...] = jnp.zeros(l_sc.shape, F32)
        acc_sc[...] = jnp.zeros(acc_sc.shape, F32)

        def step(off, width, masked):
            s = _dot_nt(q, k_ref[0, pl.ds(off, width), :])
            if masked:
                s = diag_mask(s)
            m_prev = m_sc[...]
            m_new = jnp.maximum(m_prev, jnp.max(s, axis=1, keepdims=True))
            alpha = jnp.exp(m_prev - m_new)
            p = jnp.exp(s - m_new[:, :1])
            l_sc[...] = alpha * l_sc[...] + jnp.sum(p, axis=1, keepdims=True)
            acc_sc[...] = alpha * acc_sc[...] + _dot(p.astype(_BF), v_ref[pl.ds(off, width), :])
            m_sc[...] = m_new

        def pair(t, carry):
            step(pl.multiple_of(t * 2 * B, 2 * B), 2 * B, False)
            return carry

        lax.fori_loop(0, i // 2, pair, 0)

        @pl.when(i % 2 == 1)
        def _():
            step(pl.multiple_of((i - 1) * B, B), B, False)

        step(pl.multiple_of(i * B, B), B, True)
        o_ref[...] = (acc_sc[...] / l_sc[...]).astype(o_ref.dtype)
        lse_ref[0] = m_sc[...] + jnp.log(l_sc[...])

    mla_y, lse = pl.pallas_call(
        attn_fwd_kern, name="attn_fwd", grid=(H, nb),
        in_specs=[q_blk, k_head, v_head], out_specs=[o_blk, lse_blk],
        out_shape=[jax.ShapeDtypeStruct((N, c.HV), _BF), jax.ShapeDtypeStruct((H, N, LANES), F32)],
        scratch_shapes=[pltpu.VMEM((B, LANES), F32)] * 3,
        compiler_params=_cparams(("parallel", "arbitrary")),
    )(q_cat, k_cat, kv)

    HV = c.HV
    mz_blk = c.o_mz // HV

    def gate_fwd_kern(y_ref, z_ref, o_ref):
        z = z_ref[...].astype(F32)
        o_ref[...] = (y_ref[...].astype(F32) * (z * _sig(z))).astype(o_ref.dtype)

    a_mla = pl.pallas_call(
        gate_fwd_kern, name="gate_mla", grid=(nr,),
        in_specs=[rowb(HV, 0), rowb(HV, mz_blk)], out_specs=rowb(HV, 0),
        out_shape=jax.ShapeDtypeStruct((N, HV), _BF), compiler_params=_cparams(("parallel",)),
    )(mla_y, proj)
    o_mla = _mm(a_mla, wmo, name="mm_omla", out_dtype=_BF)

    M, MW, MH, MHD = c.M, c.MW, c.MH, c.MHD
    memn = pl.pallas_call(
        make_rms_kern(), name="rms_mem",
        grid=(1,), in_specs=[fullb((M, D)), fullb((1, D))], out_specs=fullb((M, D)),
        out_shape=jax.ShapeDtypeStruct((M, D), _BF), compiler_params=_cparams(("arbitrary",)),
    )(memx, mem_norm_g)
    kvm = _mm(memn, wmkv, name="mm_memkv", out_dtype=F32)

    def memk_fwd_kern(kv_ref, g_ref, k_ref, v_ref):
        for hh in range(MH):
            kh, _ = _rms(kv_ref[:, hh * MHD:(hh + 1) * MHD], MHD)
            k_ref[:, hh * MHD:(hh + 1) * MHD] = (kh * g_ref[...]).astype(k_ref.dtype)
        v_ref[...] = kv_ref[:, MW:].astype(v_ref.dtype)

    mem_k, mem_v = pl.pallas_call(
        memk_fwd_kern, name="memk_fwd", grid=(1,),
        in_specs=[fullb((M, 2 * MW)), fullb((1, MHD))], out_specs=[fullb((M, MW)), fullb((M, MW))],
        out_shape=[jax.ShapeDtypeStruct((M, MW), _BF)] * 2, compiler_params=_cparams(("arbitrary",)),
    )(kvm, mem_kn_g)

    mem_blk = c.o_mem // (2 * MW)

    def mem_head(qz_ref, k_ref, v_ref, g_ref, hh):
        sl = slice(hh * MHD, (hh + 1) * MHD)
        qh, r = _rms(qz_ref[:, sl].astype(F32), MHD)
        qn = (qh * g_ref[...]).astype(_BF)
        s = _dot_nt(qn, k_ref[:, sl]) * c.mscale
        e = jnp.exp(s - jnp.max(s, axis=1, keepdims=True))
        p = e / jnp.sum(e, axis=1, keepdims=True)
        y = _dot(p.astype(_BF), v_ref[:, sl])
        z = qz_ref[:, MW + hh * MHD:MW + (hh + 1) * MHD].astype(F32)
        return sl, qh, r, qn, p, y, z

    def mem_fwd_kern(qz_ref, k_ref, v_ref, g_ref, o_ref):
        for hh in range(MH):
            sl, _, _, _, _, y, z = mem_head(qz_ref, k_ref, v_ref, g_ref, hh)
            o_ref[:, sl] = (y * (z * _sig(z))).astype(o_ref.dtype)

    a_mem = pl.pallas_call(
        mem_fwd_kern, name="mem_fwd", grid=(nr,),
        in_specs=[rowb(2 * MW, mem_blk), fullb((M, MW)), fullb((M, MW)), fullb((1, MHD))],
        out_specs=rowb(MW, 0), out_shape=jax.ShapeDtypeStruct((N, MW), _BF),
        compiler_params=_cparams(("parallel",)),
    )(proj, mem_k, mem_v, mem_qn_g)
    o_mem = _mm(a_mem, wmemo, name="mm_omem", out_dtype=_BF)

    g_blk = c.o_g // (3 * D)

    def merge_fwd_kern(g_ref, oc_ref, om_ref, ome_ref, o_ref):
        g = g_ref[...].astype(F32)
        acc = _sig(g[:, :D]) * oc_ref[...].astype(F32)
        acc += _sig(g[:, D:2 * D]) * om_ref[...].astype(F32)
        acc += _sig(g[:, 2 * D:]) * ome_ref[...].astype(F32)
        o_ref[...] = acc.astype(o_ref.dtype)

    merged = pl.pallas_call(
        merge_fwd_kern, name="merge_fwd", grid=(nr,),
        in_specs=[rowb(3 * D, g_blk), rowb(D, 0), rowb(D, 0), rowb(D, 0)], out_specs=rowb(D, 0),
        out_shape=jax.ShapeDtypeStruct((N, D), _BF), compiler_params=_cparams(("parallel",)),
    )(proj, o_conv, o_mla, o_mem)
    y2 = _mm(merged, wo, name="mm_out", out_dtype=F32)

    def loss_kern(x_ref, y_ref, t_ref, dy_ref, dyb_ref, l_ref):
        e = x_ref[...] + y_ref[...] - t_ref[...]
        dy = e * (1.0 / D)
        dy_ref[...] = dy
        dyb_ref[...] = dy.astype(dyb_ref.dtype)

        @pl.when(pl.program_id(0) == 0)
        def _():
            l_ref[...] = jnp.zeros(l_ref.shape, F32)

        l_ref[...] += jnp.sum(e * e, axis=0, keepdims=True)

    dy, dyb, lpart = pl.pallas_call(
        loss_kern, name="loss", grid=(nr,),
        in_specs=[rowb(D, 0)] * 3, out_specs=[rowb(D, 0), rowb(D, 0), fullb((1, D))],
        out_shape=[jax.ShapeDtypeStruct((N, D), F32), jax.ShapeDtypeStruct((N, D), _BF),
                   jax.ShapeDtypeStruct((1, D), F32)],
        compiler_params=_cparams(("arbitrary",)),
    )(xs, y2, tgt)
    loss = lax.psum(jnp.sum(lpart) * (0.5 / D), AXES)

    G = {}
    d_merged = _mm(dyb, woT, name="mm_dmerged", out_dtype=_BF)
    G['w_o'] = _mm(merged, dyb, ta=True, name="mm_dwo", out_dtype=_BF)

    dproj0 = lax.empty((N, c.P), _BF)
    any_spec = pl.BlockSpec(memory_space=pl.ANY)

    def merge_bwd_kern(dp_any, g_ref, dm_ref, oc_ref, om_ref, ome_ref, dg_ref, doc_ref, dom_ref, dome_ref):
        g = g_ref[...].astype(F32)
        dm = dm_ref[...].astype(F32)
        for idx, (o_in, d_out) in enumerate(((oc_ref, doc_ref), (om_ref, dom_ref), (ome_ref, dome_ref))):
            sg = _sig(g[:, idx * D:(idx + 1) * D])
            d_out[...] = (sg * dm).astype(d_out.dtype)
            dg_ref[:, idx * D:(idx + 1) * D] = (dm * o_in[...].astype(F32) * sg * (1.0 - sg)).astype(dg_ref.dtype)

    dproj1, d_oconv, d_omla, d_omem = pl.pallas_call(
        merge_bwd_kern, name="merge_bwd", grid=(nr,),
        in_specs=[any_spec, rowb(3 * D, g_blk), rowb(D, 0), rowb(D, 0), rowb(D, 0), rowb(D, 0)],
        out_specs=[rowb(3 * D, g_blk), rowb(D, 0), rowb(D, 0), rowb(D, 0)],
        out_shape=[jax.ShapeDtypeStruct((N, c.P), _BF)] + [jax.ShapeDtypeStruct((N, D), _BF)] * 3,
        input_output_aliases={0: 0}, compiler_params=_cparams(("parallel",)),
    )(dproj0, proj, d_merged, o_conv, o_mla, o_mem)

    G['w_conv_out'] = _mm(a_conv, d_oconv, ta=True, name="mm_dwco", out_dtype=_BF)
    d_aconv = _mm(d_oconv, wcoT, name="mm_daconv", out_dtype=_BF)
    G['w_mla_out'] = _mm(a_mla, d_omla, ta=True, name="mm_dwmo", out_dtype=_BF)
    d_amla = _mm(d_omla, wmoT, name="mm_damla", out_dtype=_BF)
    G['w_mem_out'] = _mm(a_mem, d_omem, ta=True, name="mm_dwmemo", out_dtype=_BF)
    d_amem = _mm(d_omem, wmemoT, name="mm_damem", out_dtype=_BF)

    def conv_bwd_kern(dp_any, p_ref, prev_ref, next_ref, da_ref, dan_ref, w_ref, o_ref, dw_ref):
        i = pl.program_id(0)
        cg, bg, u, z = conv_parts(p_ref[...])
        pc, _, pu, _ = conv_parts(prev_ref[...])
        _, nbg, _, nz = conv_parts(next_ref[...])
        cu = cg * u
        sh1, sh2 = shifted(cu, pc * pu, i)
        w = w_ref[...]
        conv = w[0:1, :] * sh2 + w[1:2, :] * sh1 + w[2:3, :] * cu
        sg = _sig(z)
        sz = z * sg
        da = da_ref[...].astype(F32)
        dcy = da * sz
        d_z = da * (bg * conv) * (sg * (1.0 + z * (1.0 - sg)))
        d_b = dcy * conv
        dconv = dcy * bg
        dnext = dan_ref[...].astype(F32) * (nz * _sig(nz)) * nbg
        dnext = jnp.where(i < nr - 1, dnext, 0.0)
        rid = lax.broadcasted_iota(jnp.int32, cu.shape, 0)
        up1 = jnp.where(rid == R - 1, dnext[0:1, :], pltpu.roll(dconv, R - 1, 0))
        up2 = jnp.where(rid == R - 2, dnext[0:1, :], jnp.where(rid == R - 1, dnext[1:2, :], pltpu.roll(dconv, R - 2, 0)))
        dcu = w[2:3, :] * dconv + w[1:2, :] * up1 + w[0:1, :] * up2
        o_ref[:, 0:CW] = (dcu * u).astype(o_ref.dtype)
        o_ref[:, CW:2 * CW] = d_b.astype(o_ref.dtype)
        o_ref[:, 2 * CW:3 * CW] = (dcu * cg).astype(o_ref.dtype)
        o_ref[:, 3 * CW:4 * CW] = d_z.astype(o_ref.dtype)

        @pl.when(i == 0)
        def _():
            dw_ref[...] = jnp.zeros(dw_ref.shape, F32)

        dw_ref[0:1, :] += jnp.sum(dconv * sh2, axis=0, keepdims=True)
        dw_ref[1:2, :] += jnp.sum(dconv * sh1, axis=0, keepdims=True)
        dw_ref[2:3, :] += jnp.sum(dconv * cu, axis=0, keepdims=True)

    next_spec = pl.BlockSpec((HALO, 4 * CW), lambda i: (jnp.minimum((i + 1) * rh, N // HALO - 1), conv_blk))
    dan_spec = pl.BlockSpec((HALO, CW), lambda i: (jnp.minimum((i + 1) * rh, N // HALO - 1), 0))
    dproj2, g_convw = pl.pallas_call(
        conv_bwd_kern, name="conv_bwd", grid=(nr,),
        in_specs=[any_spec, rowb(4 * CW, conv_blk), prev_spec, next_spec, rowb(CW, 0), dan_spec, fullb((8, CW))],
        out_specs=[rowb(4 * CW, conv_blk), fullb((8, CW))],
        out_shape=[jax.ShapeDtypeStruct((N, c.P), _BF), jax.ShapeDtypeStruct((8, CW), F32)],
        input_output_aliases={0: 0}, compiler_params=_cparams(("arbitrary",)),
    )(dproj1, proj, proj, proj, d_aconv, d_aconv, convw8)

    def mem_bwd_kern(dp_any, qz_ref, da_ref, k_ref, v_ref, g_ref, o_ref, dk_ref, dv_ref, dg_ref):
        @pl.when(pl.program_id(0) == 0)
        def _():
            dk_ref[...] = jnp.zeros(dk_ref.shape, F32)
            dv_ref[...] = jnp.zeros(dv_ref.shape, F32)
            dg_ref[...] = jnp.zeros(dg_ref.shape, F32)

        for hh in range(MH):
            sl, qh, r, qn, p, y, z = mem_head(qz_ref, k_ref, v_ref, g_ref, hh)
            da = da_ref[:, sl].astype(F32)
            sg = _sig(z)
            dyh = da * (z * sg)
            o_ref[:, MW + hh * MHD:MW + (hh + 1) * MHD] = (da * y * (sg * (1.0 + z * (1.0 - sg)))).astype(o_ref.dtype)
            dyb_h = dyh.astype(_BF)
            dpm = _dot_nt(dyb_h, v_ref[:, sl])
            ds = (p * (dpm - jnp.sum(dpm * p, axis=1, keepdims=True)) * c.mscale).astype(_BF)
            dqn = _dot(ds, k_ref[:, sl])
            dk_ref[:, sl] += _dot_tn(ds, qn)
            dv_ref[:, sl] += _dot_tn(p.astype(_BF), dyb_h)
            dq, dgp = _rms_bwd(qh, r, g_ref[...], dqn, MHD)
            o_ref[:, sl] = dq.astype(o_ref.dtype)
            dg_ref[...] += jnp.sum(dgp, axis=0, keepdims=True)

    dproj3, d_memk, d_memv, g_mem_qn = pl.pallas_call(
        mem_bwd_kern, name="mem_bwd", grid=(nr,),
        in_specs=[any_spec, rowb(2 * MW, mem_blk), rowb(MW, 0), fullb((M, MW)), fullb((M, MW)), fullb((1, MHD))],
        out_specs=[rowb(2 * MW, mem_blk), fullb((M, MW)), fullb((M, MW)), fullb((1, MHD))],
        out_shape=[jax.ShapeDtypeStruct((N, c.P), _BF), jax.ShapeDtypeStruct((M, MW), F32),
                   jax.ShapeDtypeStruct((M, MW), F32), jax.ShapeDtypeStruct((1, MHD), F32)],
        input_output_aliases={0: 0}, compiler_params=_cparams(("arbitrary",)),
    )(dproj2, proj, d_amem, mem_k, mem_v, mem_qn_g)

    def memk_bwd_kern(kv_ref, dk_ref, dv_ref, g_ref, o_ref, dg_ref):
        dg = jnp.zeros((1, MHD), F32)
        for hh in range(MH):
            sl = slice(hh * MHD, (hh + 1) * MHD)
            kh, r = _rms(kv_ref[:, sl], MHD)
            dkr, dgp = _rms_bwd(kh, r, g_ref[...], dk_ref[:, sl], MHD)
            o_ref[:, sl] = dkr.astype(o_ref.dtype)
            dg += jnp.sum(dgp, axis=0, keepdims=True)
        o_ref[:, MW:] = dv_ref[...].astype(o_ref.dtype)
        dg_ref[...] = dg

    d_kvm, g_mem_kn = pl.pallas_call(
        memk_bwd_kern, name="memk_bwd", grid=(1,),
        in_specs=[fullb((M, 2 * MW)), fullb((M, MW)), fullb((M, MW)), fullb((1, MHD))],
        out_specs=[fullb((M, 2 * MW)), fullb((1, MHD))],
        out_shape=[jax.ShapeDtypeStruct((M, 2 * MW), _BF), jax.ShapeDtypeStruct((1, MHD), F32)],
        compiler_params=_cparams(("arbitrary",)),
    )(kvm, d_memk, d_memv, mem_kn_g)
    G['w_mem_kv'] = _mm(memn, d_kvm, ta=True, name="mm_dwmkv", out_dtype=_BF)
    d_memn = _mm(d_kvm, wmkvT, name="mm_dmemn", out_dtype=F32)

    def memnorm_bwd_kern(x_ref, d_ref, dg_ref):
        xh, _ = _rms(x_ref[...], D)
        dg_ref[...] = jnp.sum(d_ref[...] * xh, axis=0, keepdims=True)

    g_mem_norm = pl.pallas_call(
        memnorm_bwd_kern, name="memnorm_bwd", grid=(1,),
        in_specs=[fullb((M, D)), fullb((M, D))], out_specs=fullb((1, D)),
        out_shape=jax.ShapeDtypeStruct((1, D), F32), compiler_params=_cparams(("arbitrary",)),
    )(memx, d_memn)

    def gate_bwd_kern(dp_any, da_ref, y_ref, z_ref, dy_ref, dz_ref):
        z = z_ref[...].astype(F32)
        da = da_ref[...].astype(F32)
        sg = _sig(z)
        dy_ref[...] = (da * (z * sg)).astype(dy_ref.dtype)
        dz_ref[...] = (da * y_ref[...].astype(F32) * (sg * (1.0 + z * (1.0 - sg)))).astype(dz_ref.dtype)

    d_mlay, dproj4 = pl.pallas_call(
        gate_bwd_kern, name="gate_mla_bwd", grid=(nr,),
        in_specs=[any_spec, rowb(HV, 0), rowb(HV, 0), rowb(HV, mz_blk)],
        out_specs=[rowb(HV, 0), rowb(HV, mz_blk)],
        out_shape=[jax.ShapeDtypeStruct((N, HV), _BF), jax.ShapeDtypeStruct((N, c.P), _BF)],
        input_output_aliases={0: 1}, compiler_params=_cparams(("parallel",)),
    )(dproj3, d_amla, mla_y, proj)

    def attn_bwd_kern(q_ref, k_ref, v_ref, o_ref, do_ref, lse_ref, dq_ref, dk_ref, dv_ref, dq_sc):
        i = pl.program_id(1)
        q, do = q_ref[0], do_ref[...]
        lse_col = lse_ref[0][:, :1]
        delta = jnp.sum(do.astype(F32) * o_ref[...].astype(F32), axis=1, keepdims=True)
        dq_sc[...] = jnp.zeros(dq_sc.shape, F32)

        def step(off, masked):
            rows = pl.ds(off, B)
            k = k_ref[0, rows, :]
            s = _dot_nt(q, k)
            if masked:
                s = diag_mask(s)
            p = jnp.exp(s - lse_col)
            dpm = _dot_nt(do, v_ref[rows, :])
            ds = (p * (dpm - delta)).astype(_BF)
            dvp = _dot_tn(p.astype(_BF), do)
            dkp = _dot_tn(ds, q)
            if masked:
                dk_ref[0, rows, :] = dkp
                dv_ref[0, rows, :] = dvp
            else:
                dk_ref[0, rows, :] += dkp
                dv_ref[0, rows, :] += dvp
            dq_sc[...] += _dot(ds, k)

        def pair(t, carry):
            step(pl.multiple_of(t * 2 * B, B), False)
            step(pl.multiple_of(t * 2 * B + B, B), False)
            return carry

        lax.fori_loop(0, i // 2, pair, 0)

        @pl.when(i % 2 == 1)
        def _():
            step(pl.multiple_of((i - 1) * B, B), False)

        step(pl.multiple_of(i * B, B), True)
        dq_ref[0] = dq_sc[...]

    d_qcat, d_kcat, d_v = pl.pallas_call(
        attn_bwd_kern, name="attn_bwd", grid=(H, nb),
        in_specs=[q_blk, k_head, v_head, o_blk, o_blk, lse_blk],
        out_specs=[pl.BlockSpec((1, B, HW), lambda hh, i: (hh, i, 0)),
                   pl.BlockSpec((1, N, HW), lambda hh, i: (hh, 0, 0)),
                   pl.BlockSpec((1, N, LANES), lambda hh, i: (hh, 0, 0))],
        out_shape=[jax.ShapeDtypeStruct((H, N, HW), F32), jax.ShapeDtypeStruct((H, N, HW), F32),
                   jax.ShapeDtypeStruct((H, N, LANES), F32)],
        scratch_shapes=[pltpu.VMEM((B, HW), F32)],
        compiler_params=_cparams(("parallel", "arbitrary")),
    )(q_cat, k_cat, kv, mla_y, d_mlay, lse)

    def q_prep_bwd_kern(q_ref, dq_ref, t_ref, gn_ref, gr_ref, o_ref, dgn_ref, dgr_ref):
        @pl.when(jnp.logical_and(pl.program_id(0) == 0, pl.program_id(1) == 0))
        def _():
            dgn_ref[...] = jnp.zeros(dgn_ref.shape, F32)
            dgr_ref[...] = jnp.zeros(dgr_ref.shape, F32)

        for g in range(HG):
            blk = q_ref[:, g * HW:(g + 1) * HW].astype(F32)
            d = dq_ref[g] * c.scale
            nh, rn = _rms(blk[:, :LANES], c.NOPE)
            rhat, rr = _rms(blk[:, LANES:], c.ROPE)
            dn, dgn = _rms_bwd(nh, rn, gn_ref[...], d[:, :LANES], c.NOPE)
            drot = _rope_t(d[:, LANES:], t_ref[0], t_ref[1], t_ref[2])
            dr, dgr = _rms_bwd(rhat, rr, gr_ref[...], drot, c.ROPE)
            o_ref[:, g * HW:(g + 1) * HW] = jnp.concatenate([dn, dr], axis=1).astype(o_ref.dtype)
            dgn_ref[...] += jnp.sum(dgn, axis=0, keepdims=True)
            dgr_ref[...] += jnp.sum(dgr, axis=0, keepdims=True)

    d_qp, g_qn_nope, g_qn_rope = pl.pallas_call(
        q_prep_bwd_kern, name="q_prep_bwd", grid=(nrp, nhg),
        in_specs=[heads_in, heads_out, tabs_of(RP), fullb((1, LANES)), fullb((1, LANES))],
        out_specs=[heads_in, fullb((1, LANES)), fullb((1, LANES))],
        out_shape=[jax.ShapeDtypeStruct((N, H * HW), _BF), jax.ShapeDtypeStruct((1, LANES), F32),
                   jax.ShapeDtypeStruct((1, LANES), F32)],
        compiler_params=_cparams(("arbitrary", "arbitrary")),
    )(q_p, d_qcat, tabs, g_qn, g_qr)

    def k_prep_bwd_kern(kv_ref, dk_ref, dv_ref, gn_ref, o_ref, dkr_ref, dgn_ref):
        hg = pl.program_id(1)

        @pl.when(jnp.logical_and(pl.program_id(0) == 0, hg == 0))
        def _():
            dgn_ref[...] = jnp.zeros(dgn_ref.shape, F32)

        @pl.when(hg == 0)
        def _():
            dkr_ref[...] = jnp.zeros(dkr_ref.shape, F32)

        dkr = jnp.zeros((RP, LANES), F32)
        for g in range(HG):
            dk = dk_ref[g]
            kn, r = _rms(kv_ref[:, g * HW:g * HW + LANES].astype(F32), c.NOPE)
            dkn, dgn = _rms_bwd(kn, r, gn_ref[...], dk[:, :LANES], c.NOPE)
            o_ref[:, g * HW:(g + 1) * HW] = jnp.concatenate([dkn, dv_ref[g]], axis=1).astype(o_ref.dtype)
            dgn_ref[...] += jnp.sum(dgn, axis=0, keepdims=True)
            dkr += dk[:, LANES:]
        dkr_ref[...] += dkr

    d_kv, d_krsum, g_kn_nope = pl.pallas_call(
        k_prep_bwd_kern, name="k_prep_bwd", grid=(nrp, nhg),
        in_specs=[heads_in, heads_out, pl.BlockSpec((HG, RP, LANES), lambda i, hg: (hg, i, 0)), fullb((1, LANES))],
        out_specs=[heads_in, pl.BlockSpec((RP, LANES), lambda i, hg: (i, 0)), fullb((1, LANES))],
        out_shape=[jax.ShapeDtypeStruct((N, H * HW), _BF), jax.ShapeDtypeStruct((N, LANES), F32),
                   jax.ShapeDtypeStruct((1, LANES), F32)],
        compiler_params=_cparams(("arbitrary", "arbitrary")),
    )(kv, d_kcat, d_v, g_kn)

    def krope_bwd_kern(dp_any, p_ref, d_ref, t_ref, g_ref, o_ref, dg_ref):
        @pl.when(pl.program_id(0) == 0)
        def _():
            dg_ref[...] = jnp.zeros(dg_ref.shape, F32)

        xh, r = _rms(p_ref[...].astype(F32), c.ROPE)
        drot = _rope_t(d_ref[...], t_ref[0], t_ref[1], t_ref[2])
        dx, dg = _rms_bwd(xh, r, g_ref[...], drot, c.ROPE)
        o_ref[...] = dx.astype(o_ref.dtype)
        dg_ref[...] += jnp.sum(dg, axis=0, keepdims=True)

    dproj5, g_kn_rope = pl.pallas_call(
        krope_bwd_kern, name="krope_bwd", grid=(nr,),
        in_specs=[any_spec, rowb(LANES, kr_blk), rowb(LANES, 0), tabs_of(R), fullb((1, LANES))],
        out_specs=[rowb(LANES, kr_blk), fullb((1, LANES))],
        out_shape=[jax.ShapeDtypeStruct((N, c.P), _BF), jax.ShapeDtypeStruct((1, LANES), F32)],
        input_output_aliases={0: 0}, compiler_params=_cparams(("arbitrary",)),
    )(dproj4, proj, d_krsum, tabs, g_kr)

    g_wuq_p = _mm(cqn, d_qp, ta=True, name="mm_dwuq", out_dtype=_BF)
    G['w_uq'] = g_wuq_p.reshape(QL, H, HW)[:, :, :c.NOPE + c.ROPE].reshape(QL, H * (c.NOPE + c.ROPE))
    d_cqn = _mm(d_qp, wuq_pT, name="mm_dcqn", out_dtype=F32)
    G['w_ukv'] = _mm(ckvn, d_kv, ta=True, name="mm_dwukv", out_dtype=_BF)
    d_ckvn = _mm(d_kv, wukvT, name="mm_dckvn", out_dtype=F32)

    def lora_bwd_kern(dp_any, p_ref, dq_ref, dkv_ref, gq_ref, gkv_ref, o_ref, dgq_ref, dgkv_ref):
        @pl.when(pl.program_id(0) == 0)
        def _():
            dgq_ref[...] = jnp.zeros(dgq_ref.shape, F32)
            dgkv_ref[...] = jnp.zeros(dgkv_ref.shape, F32)

        blk = p_ref[...].astype(F32)
        qh, rq = _rms(blk[:, :QL], QL)
        kh, rk = _rms(blk[:, QL:], KVL)
        dq, dgq = _rms_bwd(qh, rq, gq_ref[...], dq_ref[...], QL)
        dk, dgk = _rms_bwd(kh, rk, gkv_ref[...], dkv_ref[...], KVL)
        o_ref[:, :QL] = dq.astype(o_ref.dtype)
        o_ref[:, QL:] = dk.astype(o_ref.dtype)
        dgq_ref[...] += jnp.sum(dgq, axis=0, keepdims=True)
        dgkv_ref[...] += jnp.sum(dgk, axis=0, keepdims=True)

    dproj6, g_q_norm, g_kv_norm = pl.pallas_call(
        lora_bwd_kern, name="lora_bwd", grid=(nr,),
        in_specs=[any_spec, rowb(QL + KVL, lora_blk), rowb(QL, 0), rowb(KVL, 0), fullb((1, QL)), fullb((1, KVL))],
        out_specs=[rowb(QL + KVL, lora_blk), fullb((1, QL)), fullb((1, KVL))],
        out_shape=[jax.ShapeDtypeStruct((N, c.P), _BF), jax.ShapeDtypeStruct((1, QL), F32),
                   jax.ShapeDtypeStruct((1, KVL), F32)],
        input_output_aliases={0: 0}, compiler_params=_cparams(("arbitrary",)),
    )(dproj5, proj, d_cqn, d_ckvn, mla_q_norm_g, mla_kv_norm_g)

    g_win_p = _mm(h, dproj6, ta=True, name="mm_dwin", out_dtype=_BF, bn=1408)
    G['w_in'] = _win_from_padded(g_win_p, c)
    d_h = _mm(dproj6, win_pT, name="mm_dh", out_dtype=F32, bk=1408)

    def final_bwd_kern(x_ref, g_ref, dh_ref, dy_ref, gx_ref, dg_ref):
        @pl.when(pl.program_id(0) == 0)
        def _():
            dg_ref[...] = jnp.zeros(dg_ref.shape, F32)

        xh, r = _rms(x_ref[...], D)
        dx, dg = _rms_bwd(xh, r, g_ref[...], dh_ref[...], D)
        gx_ref[...] = dy_ref[...] + dx
        dg_ref[...] += jnp.sum(dg, axis=0, keepdims=True)

    grad_x, g_norm = pl.pallas_call(
        final_bwd_kern, name="final_bwd", grid=(nr,),
        in_specs=[rowb(D, 0), fullb((1, D)), rowb(D, 0), rowb(D, 0)],
        out_specs=[rowb(D, 0), fullb((1, D))],
        out_shape=[jax.ShapeDtypeStruct((N, D), F32), jax.ShapeDtypeStruct((1, D), F32)],
        compiler_params=_cparams(("arbitrary",)),
    )(xs, norm_g, d_h, dy)

    def to_blocks(n, g):
        if n in COL_SHARDED:
            return jnp.transpose(g.reshape(g.shape[0], NDEV, -1), (1, 0, 2))
        return g.reshape(NDEV, -1, g.shape[1])

    recv = _exchange([to_blocks(n, G[n]) for n in BIG], "exchange_grads")
    res = [{}, {}, {}, {}]
    for n, parts in zip(BIG, recv):
        outs = _adam(parts, W[n][0], Mo[n][0], Vo[n][0], "adam_" + n)
        for k in range(4):
            res[k][n] = outs[k][None]

    small_g = {'norm_g': g_norm, 'mla_q_norm_g': g_q_norm, 'mla_kv_norm_g': g_kv_norm,
               'mla_qn_nope_g': g_qn_nope, 'mla_qn_rope_g': g_qn_rope[:, :c.ROPE], 'mla_kn_nope_g': g_kn_nope,
               'mla_kn_rope_g': g_kn_rope[:, :c.ROPE], 'mem_norm_g': g_mem_norm, 'mem_qn_g': g_mem_qn,
               'mem_kn_g': g_mem_kn}
    small_part = _pack([small_g[n] for n in SMALL] + [g_convw[0:3, :]], 0)
    small_all = _all_gather([small_part], "ag_small_grads")[0]
    small_shapes = [W[n].shape for n in SMALL]
    pieces = _unpack(small_all, small_shapes + [(3, CW)])
    cw8 = CW // NDEV
    conv_mine = lax.dynamic_slice_in_dim(pieces[-1].reshape(NDEV, 3, NDEV, cw8), me, 1, axis=2)[:, :, 0, :]
    sm_parts = _pack(pieces[:-1] + [conv_mine], 1)
    sm_names = SMALL + ['conv_w']
    sm_shapes = small_shapes + [(3, cw8)]
    w_sm = _pack([W[n] for n in SMALL] + [conv_w[0]], 0)
    m_sm = _pack([Mo[n] for n in SMALL] + [m_conv_w[0]], 0)
    v_sm = _pack([Vo[n] for n in SMALL] + [v_conv_w[0]], 0)
    outs_sm = [_unpack(o, sm_shapes) for o in _adam(sm_parts, w_sm, m_sm, v_sm, "adam_small")]
    for k in range(4):
        for n, a in zip(sm_names, outs_sm[k]):
            res[k][n] = a[None] if n == 'conv_w' else a
    return (loss, grad_x[None], *[res[0][n] for n in WEIGHTS], *[res[1][n] for n in WEIGHTS],
            *[res[2][n] for n in WEIGHTS], *[res[3][n] for n in WEIGHTS])
```

```python
import math

import jax
import jax.numpy as jnp
from jax import lax
from jax.experimental import pallas as pl
from jax.experimental.pallas import tpu as pltpu

F32 = jnp.float32
_BF = jnp.bfloat16
EPS = 1e-6
CHUNK = 64
ROPE_THETA = 10000.0
ADAM_LR, ADAM_B1, ADAM_B2, ADAM_EPS, ADAM_WD, ADAM_STEP = 0.001, 0.9, 0.999, 1e-08, 0.01, 10
NDEV = 8
AXES = ("x", "y", "c")
MESH = pl.DeviceIdType.MESH
LANES = 128
NEG = -1e30
LOG2E = math.log2(math.e)
V7X_VMEM_LIMIT = 56 * 1024 * 1024
PACK_C = 1024
ATT_BLOCK = 512
ADAM_BLOCK_ELEMS = 256 * 1024

WEIGHTS = ['norm_g', 'w_in', 'conv_w', 'w_conv_out', 'mla_q_norm_g', 'w_uq', 'mla_kv_norm_g', 'w_ukv',
           'mla_qn_nope_g', 'mla_qn_rope_g', 'mla_kn_nope_g', 'mla_kn_rope_g', 'w_mla_out', 'mem_norm_g',
           'w_mem_kv', 'mem_qn_g', 'mem_kn_g', 'w_mem_out', 'w_o']
BIG = ['w_in', 'w_conv_out', 'w_uq', 'w_ukv', 'w_mla_out', 'w_mem_kv', 'w_mem_out', 'w_o']
COL_SHARDED = ('w_in', 'w_conv_out', 'w_uq', 'w_ukv', 'w_mem_out')
SMALL = ['norm_g', 'mla_q_norm_g', 'mla_kv_norm_g', 'mla_qn_nope_g', 'mla_qn_rope_g', 'mla_kn_nope_g',
         'mla_kn_rope_g', 'mem_norm_g', 'mem_qn_g', 'mem_kn_g']


def _tile(dim, target, align):
    if dim <= target:
        return dim
    t = target - target % align
    while t > 0:
        if dim % t == 0:
            return t
        t -= align
    raise ValueError(f"no tile for {dim} {target} {align}")


def _cparams(sem):
    return pltpu.CompilerParams(dimension_semantics=sem, vmem_limit_bytes=V7X_VMEM_LIMIT)


def _sig(x):
    return 1.0 / (1.0 + jnp.exp(-x))


def _rms(x, n):
    r = lax.rsqrt(jnp.sum(x * x, axis=-1, keepdims=True) * (1.0 / n) + EPS)
    return x * r, r


def _rms_bwd(xhat, r, g, dy, n):
    dxh = dy * g
    dx = r * (dxh - xhat * (jnp.sum(dxh * xhat, axis=-1, keepdims=True) * (1.0 / n)))
    return dx, dy * xhat


def _rope(x, cosp, sina, sinb):
    return x * cosp + pltpu.roll(x, 96, 1) * sina + pltpu.roll(x, 32, 1) * sinb


def _rope_t(d, cosp, sina, sinb):
    return d * cosp + pltpu.roll(d * sina, 32, 1) + pltpu.roll(d * sinb, 96, 1)


def _dot_nt(a, b):
    return lax.dot_general(a, b, (((1,), (1,)), ((), ())), preferred_element_type=F32)


def _dot_tn(a, b):
    return lax.dot_general(a, b, (((0,), (0,)), ((), ())), preferred_element_type=F32)


def _dot(a, b):
    return jnp.dot(a, b, preferred_element_type=F32)


def _all_gather(shards, name):
    na = len(shards)

    def body(*refs):
        x_refs, out_refs = refs[:na], refs[na:2 * na]
        send_sems, recv_sems, local_sems = refs[2 * na:]
        x, y, c = lax.axis_index("x"), lax.axis_index("y"), lax.axis_index("c")
        me, sibling = (x, y, c), (x, y, 1 - c)
        chips = [(1 - x, y), (x, 1 - y), (1 - x, 1 - y)]

        def rows(a, px, py, pc):
            return out_refs[a].at[4 * px + 2 * py + pc]

        def copy(a, k, block, to, src=None):
            return pltpu.make_async_remote_copy(
                src_ref=rows(a, *block) if src is None else src, dst_ref=rows(a, *block),
                send_sem=send_sems.at[7 * a + k], recv_sem=recv_sems.at[7 * a + k],
                device_id=to, device_id_type=MESH)

        mine = [pltpu.make_async_copy(x_refs[a], rows(a, *me), local_sems.at[a]) for a in range(na)]
        for cp in mine:
            cp.start()
        first = [copy(a, 0, me, sibling, src=x_refs[a]) for a in range(na)]
        first += [copy(a, 1 + j, me, (*chip, c), src=x_refs[a]) for j, chip in enumerate(chips) for a in range(na)]
        for cp in first:
            cp.start()
        passed = []
        for j, chip in enumerate(chips):
            for a in range(na):
                copy(a, 1 + j, (*chip, c), me).wait_recv()
                fwd = copy(a, 4 + j, (*chip, c), sibling)
                fwd.start()
                passed.append(fwd)
        for a in range(na):
            copy(a, 0, sibling, me).wait_recv()
        for j, chip in enumerate(chips):
            for a in range(na):
                copy(a, 4 + j, (*chip, 1 - c), me).wait_recv()
        for cp in first + passed:
            cp.wait_send()
        for cp in mine:
            cp.wait()

    any_spec = pl.BlockSpec(memory_space=pl.ANY)
    return pl.pallas_call(
        body, name=name,
        out_shape=[jax.ShapeDtypeStruct((NDEV,) + s.shape, s.dtype) for s in shards],
        in_specs=[any_spec] * na, out_specs=[any_spec] * na,
        scratch_shapes=[pltpu.SemaphoreType.DMA((7 * na,)), pltpu.SemaphoreType.DMA((7 * na,)),
                        pltpu.SemaphoreType.DMA((na,))],
    )(*shards)


def _exchange(blocks, name):
    na = len(blocks)

    def body(*refs):
        x_refs, out_refs = refs[:na], refs[na:2 * na]
        send_sems, recv_sems, local_sems = refs[2 * na:]
        x, y, c = lax.axis_index("x"), lax.axis_index("y"), lax.axis_index("c")
        me = 4 * x + 2 * y + c

        def copy(a, k):
            fx, fy, fc = (k + 1) >> 2 & 1, (k + 1) >> 1 & 1, (k + 1) & 1
            tx, ty, tc = x ^ fx, y ^ fy, c ^ fc
            return pltpu.make_async_remote_copy(
                src_ref=x_refs[a].at[4 * tx + 2 * ty + tc], dst_ref=out_refs[a].at[me],
                send_sem=send_sems.at[7 * a + k], recv_sem=recv_sems.at[7 * a + k],
                device_id=(tx, ty, tc), device_id_type=MESH)

        mine = [pltpu.make_async_copy(x_refs[a].at[me], out_refs[a].at[me], local_sems.at[a]) for a in range(na)]
        for cp in mine:
            cp.start()
        copies = [copy(a, k) for k in range(7) for a in range(na)]
        for cp in copies:
            cp.start()
        for cp in copies:
            cp.wait_recv()
        for cp in copies:
            cp.wait_send()
        for cp in mine:
            cp.wait()

    any_spec = pl.BlockSpec(memory_space=pl.ANY)
    return pl.pallas_call(
        body, name=name,
        out_shape=[jax.ShapeDtypeStruct(b.shape, b.dtype) for b in blocks],
        in_specs=[any_spec] * na, out_specs=[any_spec] * na,
        scratch_shapes=[pltpu.SemaphoreType.DMA((7 * na,)), pltpu.SemaphoreType.DMA((7 * na,)),
                        pltpu.SemaphoreType.DMA((na,))],
    )(*blocks)


def _seg_rows(size):
    rows = -(-size // PACK_C)
    return -(-rows // 16) * 16


def _pack(arrs, lead):
    parts = []
    for a in arrs:
        lshape = a.shape[:lead]
        f = a.reshape(lshape + (-1,)).astype(F32)
        rows = _seg_rows(f.shape[-1])
        f = jnp.pad(f, [(0, 0)] * lead + [(0, rows * PACK_C - f.shape[-1])])
        parts.append(f.reshape(lshape + (rows, PACK_C)))
    return jnp.concatenate(parts, axis=lead)


def _unpack(buf, shapes):
    lshape = buf.shape[:-2]
    out, r = [], 0
    for shp in shapes:
        size = math.prod(shp)
        rows = _seg_rows(size)
        seg = buf[..., r:r + rows, :].reshape(lshape + (rows * PACK_C,))[..., :size]
        out.append(seg.reshape(lshape + tuple(shp)))
        r += rows
    return out


def _mm(a, b, *, name, out_dtype, ta=False, bm=1024, bn=1024, bk=2048):
    if ta:
        kdim, m = a.shape
    else:
        m, kdim = a.shape
    k2, n = b.shape
    assert kdim == k2, (a.shape, b.shape)
    bm = _tile(m, bm, LANES if ta else 16)
    bn = _tile(n, bn, LANES)
    bk = _tile(kdim, bk, LANES)
    nk = kdim // bk

    def kern(a_ref, b_ref, o_ref, *scratch):
        part = _dot_tn(a_ref[...], b_ref[...]) if ta else _dot(a_ref[...], b_ref[...])
        if nk == 1:
            o_ref[...] = part.astype(o_ref.dtype)
        else:
            acc = scratch[0]
            k = pl.program_id(2)

            @pl.when(k == 0)
            def _():
                acc[...] = part

            @pl.when(jnp.logical_and(k > 0, k < nk - 1))
            def _():
                acc[...] += part

            @pl.when(k == nk - 1)
            def _():
                o_ref[...] = (acc[...] + part).astype(o_ref.dtype)

    a_spec = pl.BlockSpec((bk, bm), lambda i, j, k: (k, i)) if ta else pl.BlockSpec((bm, bk), lambda i, j, k: (i, k))
    return pl.pallas_call(
        kern, name=name, grid=(m // bm, n // bn, nk),
        in_specs=[a_spec, pl.BlockSpec((bk, bn), lambda i, j, k: (k, j))],
        out_specs=pl.BlockSpec((bm, bn), lambda i, j, k: (i, j)),
        out_shape=jax.ShapeDtypeStruct((m, n), out_dtype),
        scratch_shapes=[pltpu.VMEM((bm, bn), F32)] if nk > 1 else [],
        compiler_params=_cparams(("parallel", "parallel", "arbitrary")),
    )(a, b)


def _adam(parts, w_a, m_a, v_a, name):
    rows, cols = w_a.shape
    rb = _tile(rows, max(8, ADAM_BLOCK_ELEMS // cols // 8 * 8), 8)
    bc1 = 1.0 - ADAM_B1 ** ADAM_STEP
    bc2 = 1.0 - ADAM_B2 ** ADAM_STEP

    def adam_kern(p_ref, w_ref, m_ref, v_ref, g_ref, d_ref, nm_ref, nv_ref):
        g = p_ref[0].astype(F32)
        for j in range(1, NDEV):
            g = g + p_ref[j].astype(F32)
        m_new = ADAM_B1 * m_ref[...] + (1.0 - ADAM_B1) * g
        v_new = ADAM_B2 * v_ref[...] + (1.0 - ADAM_B2) * (g * g)
        g_ref[...] = g
        nm_ref[...] = m_new
        nv_ref[...] = v_new
        d_ref[...] = -ADAM_LR * ((m_new / bc1) / (jnp.sqrt(v_new / bc2) + ADAM_EPS) + ADAM_WD * w_ref[...])

    blk = pl.BlockSpec((rb, cols), lambda i: (i, 0))
    return pl.pallas_call(
        adam_kern, name=name, grid=(rows // rb,),
        in_specs=[pl.BlockSpec((NDEV, rb, cols), lambda i: (0, i, 0)), blk, blk, blk],
        out_specs=[blk] * 4, out_shape=[jax.ShapeDtypeStruct((rows, cols), F32)] * 4,
        compiler_params=_cparams(("parallel",)),
    )(parts, w_a, m_a, v_a)


class _Cfg:
    pass


def _config(x, conv_w, w_uq, w_ukv, mla_qn_nope_g, mla_qn_rope_g, mem, mem_qn_g, w_mem_out, w_mla_out):
    c = _Cfg()
    c.N, c.D = x.shape[1], x.shape[2]
    c.CW = conv_w.shape[2] * NDEV
    c.QL, c.KVL = w_uq.shape[1], w_ukv.shape[1]
    c.NOPE, c.ROPE = mla_qn_nope_g.shape[1], mla_qn_rope_g.shape[1]
    c.H = w_uq.shape[2] * NDEV // (c.NOPE + c.ROPE)
    c.V = w_ukv.shape[2] * NDEV // c.H - c.NOPE
    assert c.NOPE == LANES and c.V == LANES and c.ROPE == LANES // 2
    c.HW = 2 * LANES
    c.HV = c.H * c.V
    assert w_mla_out.shape[1] * NDEV == c.HV
    c.M = mem.shape[1]
    c.MHD = mem_qn_g.shape[1]
    c.MW = w_mem_out.shape[1]
    c.MH = c.MW // c.MHD
    c.o_conv = 0
    c.o_mz = 4 * c.CW
    c.o_g = c.o_mz + c.HV
    c.o_mem = c.o_g + 3 * c.D
    c.o_lora = c.o_mem + 2 * c.MW
    c.o_kr = c.o_lora + c.QL + c.KVL
    c.P = c.o_kr + LANES
    assert c.o_mz % c.HV == 0 and c.o_g % (3 * c.D) == 0 and c.o_mem % (2 * c.MW) == 0
    assert c.o_lora % (c.QL + c.KVL) == 0 and c.QL % LANES == 0 and c.KVL % LANES == 0
    c.IN = 4 * c.CW + c.QL + c.KVL + c.ROPE + c.HV + 2 * c.MW + 3 * c.D
    c.R = _tile(c.N, 256, 16)
    c.RP = _tile(c.N, 512, 16)
    c.HG = _tile(c.H, 4, 1)
    c.B = _tile(c.N, ATT_BLOCK, CHUNK)
    c.scale = float((c.NOPE + c.ROPE) ** -0.5)
    c.mscale = float(c.MHD ** -0.5)
    return c


def _win_to_padded(w, c):
    o = 0
    parts = {}
    for nm, wd in (('conv', 4 * c.CW), ('lora', c.QL + c.KVL), ('kr', c.ROPE), ('mz', c.HV), ('mem', 2 * c.MW), ('g', 3 * c.D)):
        parts[nm] = w[:, o:o + wd]
        o += wd
    kr = jnp.pad(parts['kr'], ((0, 0), (0, LANES - c.ROPE)))
    return jnp.concatenate([parts['conv'], parts['mz'], parts['g'], parts['mem'], parts['lora'], kr], axis=1)


def _win_from_padded(g, c):
    conv = g[:, c.o_conv:c.o_mz]
    mz = g[:, c.o_mz:c.o_g]
    gates = g[:, c.o_g:c.o_mem]
    mem = g[:, c.o_mem:c.o_lora]
    lora = g[:, c.o_lora:c.o_kr]
    kr = g[:, c.o_kr:c.o_kr + c.ROPE]
    return jnp.concatenate([conv, lora, kr, mz, mem, gates], axis=1)


def kernel(x, positions, mem, norm_g, w_in, conv_w, w_conv_out, mla_q_norm_g, w_uq, mla_kv_norm_g, w_ukv, mla_qn_nope_g, mla_qn_rope_g, mla_kn_nope_g, mla_kn_rope_g, w_mla_out, mem_norm_g, w_mem_kv, mem_qn_g, mem_kn_g, w_mem_out, w_o, loss_target, m_norm_g, m_w_in, m_conv_w, m_w_conv_out, m_mla_q_norm_g, m_w_uq, m_mla_kv_norm_g, m_w_ukv, m_mla_qn_nope_g, m_mla_qn_rope_g, m_mla_kn_nope_g, m_mla_kn_rope_g, m_w_mla_out, m_mem_norm_g, m_w_mem_kv, m_mem_qn_g, m_mem_kn_g, m_w_mem_out, m_w_o, v_norm_g, v_w_in, v_conv_w, v_w_conv_out, v_mla_q_norm_g, v_w_uq, v_mla_kv_norm_g, v_w_ukv, v_mla_qn_nope_g, v_mla_qn_rope_g, v_mla_kn_nope_g, v_mla_kn_rope_g, v_w_mla_out, v_mem_norm_g, v_w_mem_kv, v_mem_qn_g, v_mem_kn_g, v_w_mem_out, v_w_o):
    args = dict(locals())
    W = {n: args[n] for n in WEIGHTS}
    Mo = {n: args['m_' + n] for n in WEIGHTS}
    Vo = {n: args['v_' + n] for n in WEIGHTS}
    c = _config(x, conv_w, w_uq, w_ukv, mla_qn_nope_g, mla_qn_rope_g, mem, mem_qn_g, w_mem_out, w_mla_out)
    N, D, R, B, H = c.N, c.D, c.R, c.B, c.H
    assert x.shape[0] == 1
    xs = x[0]
    tgt = loss_target[0]
    memx = mem[0]
    me = 4 * lax.axis_index("x") + 2 * lax.axis_index("y") + lax.axis_index("c")
    nr = N // R

    gathered = _all_gather([W[n][0].astype(_BF) for n in BIG] + [conv_w[0]], "ag_weights")
    Wf, WfT = {}, {}
    for n, g in zip(BIG, gathered[:-1]):
        if n in COL_SHARDED:
            Wf[n] = jnp.transpose(g, (1, 0, 2)).reshape(g.shape[1], -1)
            WfT[n] = jnp.transpose(g, (0, 2, 1)).reshape(-1, g.shape[1])
        else:
            Wf[n] = g.reshape(-1, g.shape[2])
            WfT[n] = Wf[n].T
    convw = jnp.transpose(gathered[-1], (1, 0, 2)).reshape(3, c.CW)
    convw8 = jnp.pad(convw, ((0, 5), (0, 0)))

    win_p = _win_to_padded(Wf['w_in'], c)
    win_pT = win_p.T
    wuq = Wf['w_uq'].reshape(c.QL, H, c.NOPE + c.ROPE)
    wuq_p = jnp.pad(wuq, ((0, 0), (0, 0), (0, c.HW - c.NOPE - c.ROPE))).reshape(c.QL, H * c.HW)
    wuq_pT = wuq_p.T
    wukv, wukvT = Wf['w_ukv'], WfT['w_ukv']
    wco, wmo, wmkv, wmemo, wo = Wf['w_conv_out'], Wf['w_mla_out'], Wf['w_mem_kv'], Wf['w_mem_out'], Wf['w_o']
    wcoT, wmoT, wmkvT, wmemoT, woT = WfT['w_conv_out'], WfT['w_mla_out'], WfT['w_mem_kv'], WfT['w_mem_out'], WfT['w_o']

    def rowb(width, cidx):
        return pl.BlockSpec((R, width), lambda i, _c=cidx: (i, _c))

    def fullb(shape):
        nd = len(shape)
        return pl.BlockSpec(shape, lambda *_: (0,) * nd)

    def pad_lanes(g, w=LANES):
        return jnp.pad(g, ((0, 0), (0, w - g.shape[1])))

    def tabs_of(rows):
        return pl.BlockSpec((3, rows, LANES), lambda i, *_: (0, i, 0))

    half = c.ROPE // 2
    inv_freq = jnp.power(ROPE_THETA, -jnp.arange(half, dtype=F32) / half)
    invf = jnp.concatenate([inv_freq, inv_freq, jnp.zeros((LANES - c.ROPE,), F32)])[None, :]
    pos_col = positions[0].astype(F32).reshape(N, 1)

    def rope_tab_kern(pos_ref, invf_ref, o_ref):
        ang = pos_ref[...] * invf_ref[...]
        co, si = jnp.cos(ang), jnp.sin(ang)
        lane = lax.broadcasted_iota(jnp.int32, ang.shape, 1)
        o_ref[0] = jnp.where(lane < c.ROPE, co, 0.0)
        o_ref[1] = jnp.where(lane < half, -si, 0.0)
        o_ref[2] = jnp.where(jnp.logical_and(lane >= half, lane < c.ROPE), si, 0.0)

    tabs = pl.pallas_call(
        rope_tab_kern, name="rope_tab", grid=(nr,),
        in_specs=[pl.BlockSpec((R, 1), lambda i: (i, 0)), fullb((1, LANES))],
        out_specs=tabs_of(R),
        out_shape=jax.ShapeDtypeStruct((3, N, LANES), F32),
        compiler_params=_cparams(("parallel",)),
    )(pos_col, invf)

    def make_rms_kern():
        def rms_fwd_kern(x_ref, g_ref, o_ref):
            xh, _ = _rms(x_ref[...].astype(F32), x_ref.shape[-1])
            o_ref[...] = (xh * g_ref[...]).astype(o_ref.dtype)
        return rms_fwd_kern

    h = pl.pallas_call(
        make_rms_kern(), name="rms_x", grid=(nr,),
        in_specs=[rowb(D, 0), fullb((1, D))], out_specs=rowb(D, 0),
        out_shape=jax.ShapeDtypeStruct((N, D), _BF), compiler_params=_cparams(("parallel",)),
    )(xs, norm_g)

    proj = _mm(h, win_p, name="mm_proj", out_dtype=_BF, bn=1408)

    CW = c.CW
    conv_blk = c.o_conv // (4 * CW)
    HALO = 16
    rh = R // HALO

    def conv_parts(blk):
        blk = blk.astype(F32)
        return blk[:, 0:CW], blk[:, CW:2 * CW], blk[:, 2 * CW:3 * CW], blk[:, 3 * CW:4 * CW]

    def shifted(cu, prev, i):
        prev = jnp.where(i > 0, prev, 0.0)
        rid = lax.broadcasted_iota(jnp.int32, cu.shape, 0)
        last, last2 = prev[HALO - 1:HALO, :], prev[HALO - 2:HALO - 1, :]
        sh1 = jnp.where(rid == 0, last, pltpu.roll(cu, 1, 0))
        sh2 = jnp.where(rid == 0, last2, jnp.where(rid == 1, last, pltpu.roll(cu, 2, 0)))
        return sh1, sh2

    def conv_fwd_kern(p_ref, prev_ref, w_ref, o_ref):
        i = pl.program_id(0)
        cg, bg, u, z = conv_parts(p_ref[...])
        pc, _, pu, _ = conv_parts(prev_ref[...])
        cu = cg * u
        sh1, sh2 = shifted(cu, pc * pu, i)
        w = w_ref[...]
        conv = w[0:1, :] * sh2 + w[1:2, :] * sh1 + w[2:3, :] * cu
        o_ref[...] = (bg * conv * (z * _sig(z))).astype(o_ref.dtype)

    prev_spec = pl.BlockSpec((HALO, 4 * CW), lambda i: (jnp.maximum(i * rh - 1, 0), conv_blk))
    a_conv = pl.pallas_call(
        conv_fwd_kern, name="conv_fwd", grid=(nr,),
        in_specs=[rowb(4 * CW, conv_blk), prev_spec, fullb((8, CW))],
        out_specs=rowb(CW, 0), out_shape=jax.ShapeDtypeStruct((N, CW), _BF),
        compiler_params=_cparams(("parallel",)),
    )(proj, proj, convw8)
    o_conv = _mm(a_conv, wco, name="mm_oconv", out_dtype=_BF)

    QL, KVL, HW = c.QL, c.KVL, c.HW
    lora_blk = c.o_lora // (QL + KVL)

    def lora_fwd_kern(p_ref, gq_ref, gkv_ref, q_ref, kv_ref):
        blk = p_ref[...].astype(F32)
        qh, _ = _rms(blk[:, :QL], QL)
        kh, _ = _rms(blk[:, QL:], KVL)
        q_ref[...] = (qh * gq_ref[...]).astype(q_ref.dtype)
        kv_ref[...] = (kh * gkv_ref[...]).astype(kv_ref.dtype)

    cqn, ckvn = pl.pallas_call(
        lora_fwd_kern, name="lora_fwd", grid=(nr,),
        in_specs=[rowb(QL + KVL, lora_blk), fullb((1, QL)), fullb((1, KVL))],
        out_specs=[rowb(QL, 0), rowb(KVL, 0)],
        out_shape=[jax.ShapeDtypeStruct((N, QL), _BF), jax.ShapeDtypeStruct((N, KVL), _BF)],
        compiler_params=_cparams(("parallel",)),
    )(proj, mla_q_norm_g, mla_kv_norm_g)
    q_p = _mm(cqn, wuq_p, name="mm_q", out_dtype=_BF)
    kv = _mm(ckvn, wukv, name="mm_kv", out_dtype=_BF)

    g_qn, g_qr = mla_qn_nope_g, pad_lanes(mla_qn_rope_g)
    g_kn, g_kr = mla_kn_nope_g, pad_lanes(mla_kn_rope_g)
    kr_blk = c.o_kr // LANES

    def krope_fwd_kern(p_ref, t_ref, g_ref, o_ref):
        xh, _ = _rms(p_ref[...].astype(F32), c.ROPE)
        o_ref[...] = _rope(xh * g_ref[...], t_ref[0], t_ref[1], t_ref[2]).astype(o_ref.dtype)

    k_rope = pl.pallas_call(
        krope_fwd_kern, name="krope_fwd", grid=(nr,),
        in_specs=[rowb(LANES, kr_blk), tabs_of(R), fullb((1, LANES))],
        out_specs=rowb(LANES, 0), out_shape=jax.ShapeDtypeStruct((N, LANES), _BF),
        compiler_params=_cparams(("parallel",)),
    )(proj, tabs, g_kr)

    RP, HG = c.RP, c.HG
    nrp, nhg = N // RP, H // HG
    heads_in = pl.BlockSpec((RP, HG * HW), lambda i, hg: (i, hg))
    heads_out = pl.BlockSpec((HG, RP, HW), lambda i, hg: (hg, i, 0))

    def q_prep_kern(q_ref, t_ref, gn_ref, gr_ref, o_ref):
        for g in range(HG):
            blk = q_ref[:, g * HW:(g + 1) * HW].astype(F32)
            nh, _ = _rms(blk[:, :LANES], c.NOPE)
            rhat, _ = _rms(blk[:, LANES:], c.ROPE)
            rot = _rope(rhat * gr_ref[...], t_ref[0], t_ref[1], t_ref[2])
            o_ref[g] = (jnp.concatenate([nh * gn_ref[...], rot], axis=1) * (c.scale * LOG2E)).astype(o_ref.dtype)

    q_cat = pl.pallas_call(
        q_prep_kern, name="q_prep", grid=(nrp, nhg),
        in_specs=[heads_in, tabs_of(RP), fullb((1, LANES)), fullb((1, LANES))],
        out_specs=heads_out, out_shape=jax.ShapeDtypeStruct((H, N, HW), _BF),
        compiler_params=_cparams(("parallel", "parallel")),
    )(q_p, tabs, g_qn, g_qr)

    def k_prep_kern(kv_ref, kr_ref, gn_ref, o_ref):
        for g in range(HG):
            kn, _ = _rms(kv_ref[:, g * HW:g * HW + LANES].astype(F32), c.NOPE)
            o_ref[g] = jnp.concatenate([(kn * gn_ref[...]).astype(o_ref.dtype), kr_ref[...]], axis=1)

    k_cat = pl.pallas_call(
        k_prep_kern, name="k_prep", grid=(nrp, nhg),
        in_specs=[heads_in, pl.BlockSpec((RP, LANES), lambda i, hg: (i, 0)), fullb((1, LANES))],
        out_specs=heads_out, out_shape=jax.ShapeDtypeStruct((H, N, HW), _BF),
        compiler_params=_cparams(("parallel", "parallel")),
    )(kv, k_rope, g_kn)

    nb = N // B
    assert CHUNK & (CHUNK - 1) == 0 and B % CHUNK == 0

    def diag_mask(s, row0=0):
        row = lax.broadcasted_iota(jnp.int32, s.shape, 0) + row0
        col = lax.broadcasted_iota(jnp.int32, s.shape, 1)
        shift = CHUNK.bit_length() - 1
        allowed = jnp.right_shift(col, shift) <= jnp.right_shift(row, shift)
        return jnp.where(allowed, s, NEG)

    k_head = pl.BlockSpec((1, N, HW), lambda hh, i: (hh, 0, 0))
    v_head = pl.BlockSpec((N, LANES), lambda hh, i: (0, 2 * hh + 1))
    q_blk = pl.BlockSpec((1, B, HW), lambda hh, i: (hh, i, 0))
    o_blk = pl.BlockSpec((B, LANES), lambda hh, i: (i, hh))
    lse_blk = pl.BlockSpec((1, B, LANES), lambda hh, i: (hh, i, 0))

    def attn_fwd_kern(q_ref, k_ref, v_ref, o_ref, lse_ref, m_sc, acc_sc, s_sc):
        i = pl.program_id(1)
        m_sc[...] = jnp.full(m_sc.shape, NEG, F32)
        acc_sc[...] = jnp.zeros(acc_sc.shape, F32)

        def rows_of(t):
            return pl.ds(pl.multiple_of(t * B, B), B)

        def scores(t, slot):
            s_sc[slot] = _dot_nt(q_ref[0], k_ref[0, rows_of(t), :])

        def softmax_pv(t, slot, masked):
            s = s_sc[slot]
            if masked:
                s = diag_mask(s)
            mt = s[:, 0:LANES]
            for cb in range(1, B // LANES):
                mt = jnp.maximum(mt, s[:, cb * LANES:(cb + 1) * LANES])
            m_prev = m_sc[...]
            m_new = jnp.maximum(m_prev, jnp.max(mt, axis=1, keepdims=True))
            alpha = jnp.exp2(m_prev - m_new)
            p = jnp.concatenate([jnp.exp2(s[:, cb * LANES:(cb + 1) * LANES] - m_new).astype(_BF)
                                 for cb in range(B // LANES)], axis=1)
            v_ones = jnp.concatenate([v_ref[rows_of(t), :], jnp.ones((B, LANES), _BF)], axis=1)
            acc_sc[...] = jnp.concatenate([alpha, alpha], axis=1) * acc_sc[...] + _dot(p, v_ones)
            m_sc[...] = m_new

        scores(0, 0)

        def two_blocks(u, carry):
            t = 2 * u
            scores(t + 1, 1)
            softmax_pv(t, 0, False)
            scores(t + 2, 0)
            softmax_pv(t + 1, 1, False)
            return carry

        lax.fori_loop(0, i // 2, two_blocks, 0)

        @pl.when(i % 2 == 1)
        def _():
            scores(i, 1)
            softmax_pv(i - 1, 0, False)
            softmax_pv(i, 1, True)

        @pl.when(i % 2 == 0)
        def _():
            softmax_pv(i, 0, True)

        acc = acc_sc[...]
        o_ref[...] = (acc[:, :LANES] / acc[:, LANES:]).astype(o_ref.dtype)
        lse_ref[0] = m_sc[...] + jnp.log2(acc[:, LANES:])

    mla_y, lse = pl.pallas_call(
        attn_fwd_kern, name="attn_fwd", grid=(H, nb),
        in_specs=[q_blk, k_head, v_head], out_specs=[o_blk, lse_blk],
        out_shape=[jax.ShapeDtypeStruct((N, c.HV), _BF), jax.ShapeDtypeStruct((H, N, LANES), F32)],
        scratch_shapes=[pltpu.VMEM((B, LANES), F32), pltpu.VMEM((B, HW), F32), pltpu.VMEM((2, B, B), F32)],
        compiler_params=_cparams(("parallel", "arbitrary")),
    )(q_cat, k_cat, kv)

    HV = c.HV
    mz_blk = c.o_mz // HV

    def gate_fwd_kern(y_ref, z_ref, o_ref):
        z = z_ref[...].astype(F32)
        o_ref[...] = (y_ref[...].astype(F32) * (z * _sig(z))).astype(o_ref.dtype)

    a_mla = pl.pallas_call(
        gate_fwd_kern, name="gate_mla", grid=(nr,),
        in_specs=[rowb(HV, 0), rowb(HV, mz_blk)], out_specs=rowb(HV, 0),
        out_shape=jax.ShapeDtypeStruct((N, HV), _BF), compiler_params=_cparams(("parallel",)),
    )(mla_y, proj)
    o_mla = _mm(a_mla, wmo, name="mm_omla", out_dtype=_BF)

    M, MW, MH, MHD = c.M, c.MW, c.MH, c.MHD
    memn = pl.pallas_call(
        make_rms_kern(), name="rms_mem",
        grid=(1,), in_specs=[fullb((M, D)), fullb((1, D))], out_specs=fullb((M, D)),
        out_shape=jax.ShapeDtypeStruct((M, D), _BF), compiler_params=_cparams(("arbitrary",)),
    )(memx, mem_norm_g)
    kvm = _mm(memn, wmkv, name="mm_memkv", out_dtype=F32)

    def memk_fwd_kern(kv_ref, g_ref, k_ref, v_ref):
        for hh in range(MH):
            kh, _ = _rms(kv_ref[:, hh * MHD:(hh + 1) * MHD], MHD)
            k_ref[:, hh * MHD:(hh + 1) * MHD] = (kh * g_ref[...]).astype(k_ref.dtype)
        v_ref[...] = kv_ref[:, MW:].astype(v_ref.dtype)

    mem_k, mem_v = pl.pallas_call(
        memk_fwd_kern, name="memk_fwd", grid=(1,),
        in_specs=[fullb((M, 2 * MW)), fullb((1, MHD))], out_specs=[fullb((M, MW)), fullb((M, MW))],
        out_shape=[jax.ShapeDtypeStruct((M, MW), _BF)] * 2, compiler_params=_cparams(("arbitrary",)),
    )(kvm, mem_kn_g)

    mem_blk = c.o_mem // (2 * MW)

    def mem_head(qz_ref, k_ref, v_ref, g_ref, hh):
        sl = slice(hh * MHD, (hh + 1) * MHD)
        qh, r = _rms(qz_ref[:, sl].astype(F32), MHD)
        qn = (qh * g_ref[...]).astype(_BF)
        s = _dot_nt(qn, k_ref[:, sl]) * c.mscale
        e = jnp.exp(s - jnp.max(s, axis=1, keepdims=True))
        p = e / jnp.sum(e, axis=1, keepdims=True)
        y = _dot(p.astype(_BF), v_ref[:, sl])
        z = qz_ref[:, MW + hh * MHD:MW + (hh + 1) * MHD].astype(F32)
        return sl, qh, r, qn, p, y, z

    def mem_fwd_kern(qz_ref, k_ref, v_ref, g_ref, o_ref):
        for hh in range(MH):
            sl, _, _, _, _, y, z = mem_head(qz_ref, k_ref, v_ref, g_ref, hh)
            o_ref[:, sl] = (y * (z * _sig(z))).astype(o_ref.dtype)

    a_mem = pl.pallas_call(
        mem_fwd_kern, name="mem_fwd", grid=(nr,),
        in_specs=[rowb(2 * MW, mem_blk), fullb((M, MW)), fullb((M, MW)), fullb((1, MHD))],
        out_specs=rowb(MW, 0), out_shape=jax.ShapeDtypeStruct((N, MW), _BF),
        compiler_params=_cparams(("parallel",)),
    )(proj, mem_k, mem_v, mem_qn_g)
    o_mem = _mm(a_mem, wmemo, name="mm_omem", out_dtype=_BF)

    g_blk = c.o_g // (3 * D)

    def merge_fwd_kern(g_ref, oc_ref, om_ref, ome_ref, o_ref):
        g = g_ref[...].astype(F32)
        acc = _sig(g[:, :D]) * oc_ref[...].astype(F32)
        acc += _sig(g[:, D:2 * D]) * om_ref[...].astype(F32)
        acc += _sig(g[:, 2 * D:]) * ome_ref[...].astype(F32)
        o_ref[...] = acc.astype(o_ref.dtype)

    merged = pl.pallas_call(
        merge_fwd_kern, name="merge_fwd", grid=(nr,),
        in_specs=[rowb(3 * D, g_blk), rowb(D, 0), rowb(D, 0), rowb(D, 0)], out_specs=rowb(D, 0),
        out_shape=jax.ShapeDtypeStruct((N, D), _BF), compiler_params=_cparams(("parallel",)),
    )(proj, o_conv, o_mla, o_mem)
    y2 = _mm(merged, wo, name="mm_out", out_dtype=F32)

    def loss_kern(x_ref, y_ref, t_ref, dy_ref, dyb_ref, l_ref):
        e = x_ref[...] + y_ref[...] - t_ref[...]
        dy = e * (1.0 / D)
        dy_ref[...] = dy
        dyb_ref[...] = dy.astype(dyb_ref.dtype)

        @pl.when(pl.program_id(0) == 0)
        def _():
            l_ref[...] = jnp.zeros(l_ref.shape, F32)

        l_ref[...] += jnp.sum(e * e, axis=0, keepdims=True)

    dy, dyb, lpart = pl.pallas_call(
        loss_kern, name="loss", grid=(nr,),
        in_specs=[rowb(D, 0)] * 3, out_specs=[rowb(D, 0), rowb(D, 0), fullb((1, D))],
        out_shape=[jax.ShapeDtypeStruct((N, D), F32), jax.ShapeDtypeStruct((N, D), _BF),
                   jax.ShapeDtypeStruct((1, D), F32)],
        compiler_params=_cparams(("arbitrary",)),
    )(xs, y2, tgt)
    loss = lax.psum(jnp.sum(lpart) * (0.5 / D), AXES)

    G = {}
    d_merged = _mm(dyb, woT, name="mm_dmerged", out_dtype=_BF)
    G['w_o'] = _mm(merged, dyb, ta=True, name="mm_dwo", out_dtype=_BF)

    dproj0 = lax.empty((N, c.P), _BF)
    any_spec = pl.BlockSpec(memory_space=pl.ANY)

    def merge_bwd_kern(dp_any, g_ref, dm_ref, oc_ref, om_ref, ome_ref, dg_ref, doc_ref, dom_ref, dome_ref):
        g = g_ref[...].astype(F32)
        dm = dm_ref[...].astype(F32)
        for idx, (o_in, d_out) in enumerate(((oc_ref, doc_ref), (om_ref, dom_ref), (ome_ref, dome_ref))):
            sg = _sig(g[:, idx * D:(idx + 1) * D])
            d_out[...] = (sg * dm).astype(d_out.dtype)
            dg_ref[:, idx * D:(idx + 1) * D] = (dm * o_in[...].astype(F32) * sg * (1.0 - sg)).astype(dg_ref.dtype)

    dproj1, d_oconv, d_omla, d_omem = pl.pallas_call(
        merge_bwd_kern, name="merge_bwd", grid=(nr,),
        in_specs=[any_spec, rowb(3 * D, g_blk), rowb(D, 0), rowb(D, 0), rowb(D, 0), rowb(D, 0)],
        out_specs=[rowb(3 * D, g_blk), rowb(D, 0), rowb(D, 0), rowb(D, 0)],
        out_shape=[jax.ShapeDtypeStruct((N, c.P), _BF)] + [jax.ShapeDtypeStruct((N, D), _BF)] * 3,
        input_output_aliases={0: 0}, compiler_params=_cparams(("parallel",)),
    )(dproj0, proj, d_merged, o_conv, o_mla, o_mem)

    G['w_conv_out'] = _mm(a_conv, d_oconv, ta=True, name="mm_dwco", out_dtype=_BF)
    d_aconv = _mm(d_oconv, wcoT, name="mm_daconv", out_dtype=_BF)
    G['w_mla_out'] = _mm(a_mla, d_omla, ta=True, name="mm_dwmo", out_dtype=_BF)
    d_amla = _mm(d_omla, wmoT, name="mm_damla", out_dtype=_BF)
    G['w_mem_out'] = _mm(a_mem, d_omem, ta=True, name="mm_dwmemo", out_dtype=_BF)
    d_amem = _mm(d_omem, wmemoT, name="mm_damem", out_dtype=_BF)

    def conv_bwd_kern(dp_any, p_ref, prev_ref, next_ref, da_ref, dan_ref, w_ref, o_ref, dw_ref):
        i = pl.program_id(0)
        cg, bg, u, z = conv_parts(p_ref[...])
        pc, _, pu, _ = conv_parts(prev_ref[...])
        _, nbg, _, nz = conv_parts(next_ref[...])
        cu = cg * u
        sh1, sh2 = shifted(cu, pc * pu, i)
        w = w_ref[...]
        conv = w[0:1, :] * sh2 + w[1:2, :] * sh1 + w[2:3, :] * cu
        sg = _sig(z)
        sz = z * sg
        da = da_ref[...].astype(F32)
        dcy = da * sz
        d_z = da * (bg * conv) * (sg * (1.0 + z * (1.0 - sg)))
        d_b = dcy * conv
        dconv = dcy * bg
        dnext = dan_ref[...].astype(F32) * (nz * _sig(nz)) * nbg
        dnext = jnp.where(i < nr - 1, dnext, 0.0)
        rid = lax.broadcasted_iota(jnp.int32, cu.shape, 0)
        up1 = jnp.where(rid == R - 1, dnext[0:1, :], pltpu.roll(dconv, R - 1, 0))
        up2 = jnp.where(rid == R - 2, dnext[0:1, :], jnp.where(rid == R - 1, dnext[1:2, :], pltpu.roll(dconv, R - 2, 0)))
        dcu = w[2:3, :] * dconv + w[1:2, :] * up1 + w[0:1, :] * up2
        o_ref[:, 0:CW] = (dcu * u).astype(o_ref.dtype)
        o_ref[:, CW:2 * CW] = d_b.astype(o_ref.dtype)
        o_ref[:, 2 * CW:3 * CW] = (dcu * cg).astype(o_ref.dtype)
        o_ref[:, 3 * CW:4 * CW] = d_z.astype(o_ref.dtype)

        @pl.when(i == 0)
        def _():
            dw_ref[...] = jnp.zeros(dw_ref.shape, F32)

        dw_ref[0:1, :] += jnp.sum(dconv * sh2, axis=0, keepdims=True)
        dw_ref[1:2, :] += jnp.sum(dconv * sh1, axis=0, keepdims=True)
        dw_ref[2:3, :] += jnp.sum(dconv * cu, axis=0, keepdims=True)

    next_spec = pl.BlockSpec((HALO, 4 * CW), lambda i: (jnp.minimum((i + 1) * rh, N // HALO - 1), conv_blk))
    dan_spec = pl.BlockSpec((HALO, CW), lambda i: (jnp.minimum((i + 1) * rh, N // HALO - 1), 0))
    dproj2, g_convw = pl.pallas_call(
        conv_bwd_kern, name="conv_bwd", grid=(nr,),
        in_specs=[any_spec, rowb(4 * CW, conv_blk), prev_spec, next_spec, rowb(CW, 0), dan_spec, fullb((8, CW))],
        out_specs=[rowb(4 * CW, conv_blk), fullb((8, CW))],
        out_shape=[jax.ShapeDtypeStruct((N, c.P), _BF), jax.ShapeDtypeStruct((8, CW), F32)],
        input_output_aliases={0: 0}, compiler_params=_cparams(("arbitrary",)),
    )(dproj1, proj, proj, proj, d_aconv, d_aconv, convw8)

    def mem_bwd_kern(dp_any, qz_ref, da_ref, k_ref, v_ref, g_ref, o_ref, dk_ref, dv_ref, dg_ref):
        @pl.when(pl.program_id(0) == 0)
        def _():
            dk_ref[...] = jnp.zeros(dk_ref.shape, F32)
            dv_ref[...] = jnp.zeros(dv_ref.shape, F32)
            dg_ref[...] = jnp.zeros(dg_ref.shape, F32)

        for hh in range(MH):
            sl, qh, r, qn, p, y, z = mem_head(qz_ref, k_ref, v_ref, g_ref, hh)
            da = da_ref[:, sl].astype(F32)
            sg = _sig(z)
            dyh = da * (z * sg)
            o_ref[:, MW + hh * MHD:MW + (hh + 1) * MHD] = (da * y * (sg * (1.0 + z * (1.0 - sg)))).astype(o_ref.dtype)
            dyb_h = dyh.astype(_BF)
            dpm = _dot_nt(dyb_h, v_ref[:, sl])
            ds = (p * (dpm - jnp.sum(dpm * p, axis=1, keepdims=True)) * c.mscale).astype(_BF)
            dqn = _dot(ds, k_ref[:, sl])
            dk_ref[:, sl] += _dot_tn(ds, qn)
            dv_ref[:, sl] += _dot_tn(p.astype(_BF), dyb_h)
            dq, dgp = _rms_bwd(qh, r, g_ref[...], dqn, MHD)
            o_ref[:, sl] = dq.astype(o_ref.dtype)
            dg_ref[...] += jnp.sum(dgp, axis=0, keepdims=True)

    dproj3, d_memk, d_memv, g_mem_qn = pl.pallas_call(
        mem_bwd_kern, name="mem_bwd", grid=(nr,),
        in_specs=[any_spec, rowb(2 * MW, mem_blk), rowb(MW, 0), fullb((M, MW)), fullb((M, MW)), fullb((1, MHD))],
        out_specs=[rowb(2 * MW, mem_blk), fullb((M, MW)), fullb((M, MW)), fullb((1, MHD))],
        out_shape=[jax.ShapeDtypeStruct((N, c.P), _BF), jax.ShapeDtypeStruct((M, MW), F32),
                   jax.ShapeDtypeStruct((M, MW), F32), jax.ShapeDtypeStruct((1, MHD), F32)],
        input_output_aliases={0: 0}, compiler_params=_cparams(("arbitrary",)),
    )(dproj2, proj, d_amem, mem_k, mem_v, mem_qn_g)

    def memk_bwd_kern(kv_ref, dk_ref, dv_ref, g_ref, o_ref, dg_ref):
        dg = jnp.zeros((1, MHD), F32)
        for hh in range(MH):
            sl = slice(hh * MHD, (hh + 1) * MHD)
            kh, r = _rms(kv_ref[:, sl], MHD)
            dkr, dgp = _rms_bwd(kh, r, g_ref[...], dk_ref[:, sl], MHD)
            o_ref[:, sl] = dkr.astype(o_ref.dtype)
            dg += jnp.sum(dgp, axis=0, keepdims=True)
        o_ref[:, MW:] = dv_ref[...].astype(o_ref.dtype)
        dg_ref[...] = dg

    d_kvm, g_mem_kn = pl.pallas_call(
        memk_bwd_kern, name="memk_bwd", grid=(1,),
        in_specs=[fullb((M, 2 * MW)), fullb((M, MW)), fullb((M, MW)), fullb((1, MHD))],
        out_specs=[fullb((M, 2 * MW)), fullb((1, MHD))],
        out_shape=[jax.ShapeDtypeStruct((M, 2 * MW), _BF), jax.ShapeDtypeStruct((1, MHD), F32)],
        compiler_params=_cparams(("arbitrary",)),
    )(kvm, d_memk, d_memv, mem_kn_g)
    G['w_mem_kv'] = _mm(memn, d_kvm, ta=True, name="mm_dwmkv", out_dtype=_BF)
    d_memn = _mm(d_kvm, wmkvT, name="mm_dmemn", out_dtype=F32)

    def memnorm_bwd_kern(x_ref, d_ref, dg_ref):
        xh, _ = _rms(x_ref[...], D)
        dg_ref[...] = jnp.sum(d_ref[...] * xh, axis=0, keepdims=True)

    g_mem_norm = pl.pallas_call(
        memnorm_bwd_kern, name="memnorm_bwd", grid=(1,),
        in_specs=[fullb((M, D)), fullb((M, D))], out_specs=fullb((1, D)),
        out_shape=jax.ShapeDtypeStruct((1, D), F32), compiler_params=_cparams(("arbitrary",)),
    )(memx, d_memn)

    def gate_bwd_kern(dp_any, da_ref, y_ref, z_ref, dy_ref, dz_ref):
        z = z_ref[...].astype(F32)
        da = da_ref[...].astype(F32)
        sg = _sig(z)
        dy_ref[...] = (da * (z * sg)).astype(dy_ref.dtype)
        dz_ref[...] = (da * y_ref[...].astype(F32) * (sg * (1.0 + z * (1.0 - sg)))).astype(dz_ref.dtype)

    d_mlay, dproj4 = pl.pallas_call(
        gate_bwd_kern, name="gate_mla_bwd", grid=(nr,),
        in_specs=[any_spec, rowb(HV, 0), rowb(HV, 0), rowb(HV, mz_blk)],
        out_specs=[rowb(HV, 0), rowb(HV, mz_blk)],
        out_shape=[jax.ShapeDtypeStruct((N, HV), _BF), jax.ShapeDtypeStruct((N, c.P), _BF)],
        input_output_aliases={0: 1}, compiler_params=_cparams(("parallel",)),
    )(dproj3, d_amla, mla_y, proj)

    def attn_bwd_kern(q_ref, k_ref, v_ref, o_ref, do_ref, lse_ref, dq_ref, dk_ref, dv_ref, dq_sc, dl_sc):
        i = pl.program_id(1)
        q, do = q_ref[0], do_ref[...]
        delta = jnp.sum(do.astype(F32) * o_ref[...].astype(F32), axis=1, keepdims=True)
        dl_sc[...] = jnp.broadcast_to(delta, dl_sc.shape)
        dq_sc[...] = jnp.zeros(dq_sc.shape, F32)

        def step(off, masked):
            rows = pl.ds(off, B)
            k = k_ref[0, rows, :]
            s = _dot_nt(q, k)
            if masked:
                s = diag_mask(s)
            dpm = _dot_nt(do, v_ref[rows, :])
            lse_t, dl = lse_ref[0], dl_sc[...]
            ps, dss = [], []
            for cb in range(B // LANES):
                cols = slice(cb * LANES, (cb + 1) * LANES)
                p_cb = jnp.exp2(s[:, cols] - lse_t)
                ps.append(p_cb.astype(_BF))
                dss.append((p_cb * (dpm[:, cols] - dl)).astype(_BF))
            p, ds = jnp.concatenate(ps, axis=1), jnp.concatenate(dss, axis=1)
            dvp = _dot_tn(p, do)
            dkp = _dot_tn(ds, q)
            if masked:
                dk_ref[0, rows, :] = dkp
                dv_ref[0, rows, :] = dvp
            else:
                dk_ref[0, rows, :] += dkp
                dv_ref[0, rows, :] += dvp
            dq_sc[...] += _dot(ds, k)

        def pair(t, carry):
            step(pl.multiple_of(t * 2 * B, B), False)
            step(pl.multiple_of(t * 2 * B + B, B), False)
            return carry

        lax.fori_loop(0, i // 2, pair, 0)

        @pl.when(i % 2 == 1)
        def _():
            step(pl.multiple_of((i - 1) * B, B), False)

        step(pl.multiple_of(i * B, B), True)
        dq_ref[0] = dq_sc[...]

    d_qcat, d_kcat, d_v = pl.pallas_call(
        attn_bwd_kern, name="attn_bwd", grid=(H, nb),
        in_specs=[q_blk, k_head, v_head, o_blk, o_blk, lse_blk],
        out_specs=[pl.BlockSpec((1, B, HW), lambda hh, i: (hh, i, 0)),
                   pl.BlockSpec((1, N, HW), lambda hh, i: (hh, 0, 0)),
                   pl.BlockSpec((1, N, LANES), lambda hh, i: (hh, 0, 0))],
        out_shape=[jax.ShapeDtypeStruct((H, N, HW), F32), jax.ShapeDtypeStruct((H, N, HW), F32),
                   jax.ShapeDtypeStruct((H, N, LANES), F32)],
        scratch_shapes=[pltpu.VMEM((B, HW), F32), pltpu.VMEM((B, LANES), F32)],
        compiler_params=_cparams(("parallel", "arbitrary")),
    )(q_cat, k_cat, kv, mla_y, d_mlay, lse)

    def q_prep_bwd_kern(q_ref, dq_ref, t_ref, gn_ref, gr_ref, o_ref, dgn_ref, dgr_ref):
        @pl.when(jnp.logical_and(pl.program_id(0) == 0, pl.program_id(1) == 0))
        def _():
            dgn_ref[...] = jnp.zeros(dgn_ref.shape, F32)
            dgr_ref[...] = jnp.zeros(dgr_ref.shape, F32)

        for g in range(HG):
            blk = q_ref[:, g * HW:(g + 1) * HW].astype(F32)
            d = dq_ref[g] * c.scale
            nh, rn = _rms(blk[:, :LANES], c.NOPE)
            rhat, rr = _rms(blk[:, LANES:], c.ROPE)
            dn, dgn = _rms_bwd(nh, rn, gn_ref[...], d[:, :LANES], c.NOPE)
            drot = _rope_t(d[:, LANES:], t_ref[0], t_ref[1], t_ref[2])
            dr, dgr = _rms_bwd(rhat, rr, gr_ref[...], drot, c.ROPE)
            o_ref[:, g * HW:(g + 1) * HW] = jnp.concatenate([dn, dr], axis=1).astype(o_ref.dtype)
            dgn_ref[...] += jnp.sum(dgn, axis=0, keepdims=True)
            dgr_ref[...] += jnp.sum(dgr, axis=0, keepdims=True)

    d_qp, g_qn_nope, g_qn_rope = pl.pallas_call(
        q_prep_bwd_kern, name="q_prep_bwd", grid=(nrp, nhg),
        in_specs=[heads_in, heads_out, tabs_of(RP), fullb((1, LANES)), fullb((1, LANES))],
        out_specs=[heads_in, fullb((1, LANES)), fullb((1, LANES))],
        out_shape=[jax.ShapeDtypeStruct((N, H * HW), _BF), jax.ShapeDtypeStruct((1, LANES), F32),
                   jax.ShapeDtypeStruct((1, LANES), F32)],
        compiler_params=_cparams(("arbitrary", "arbitrary")),
    )(q_p, d_qcat, tabs, g_qn, g_qr)

    def k_prep_bwd_kern(kv_ref, dk_ref, dv_ref, gn_ref, o_ref, dkr_ref, dgn_ref):
        hg = pl.program_id(1)

        @pl.when(jnp.logical_and(pl.program_id(0) == 0, hg == 0))
        def _():
            dgn_ref[...] = jnp.zeros(dgn_ref.shape, F32)

        @pl.when(hg == 0)
        def _():
            dkr_ref[...] = jnp.zeros(dkr_ref.shape, F32)

        dkr = jnp.zeros((RP, LANES), F32)
        for g in range(HG):
            dk = dk_ref[g] * (1.0 / LOG2E)
            kn, r = _rms(kv_ref[:, g * HW:g * HW + LANES].astype(F32), c.NOPE)
            dkn, dgn = _rms_bwd(kn, r, gn_ref[...], dk[:, :LANES], c.NOPE)
            o_ref[:, g * HW:(g + 1) * HW] = jnp.concatenate([dkn, dv_ref[g]], axis=1).astype(o_ref.dtype)
            dgn_ref[...] += jnp.sum(dgn, axis=0, keepdims=True)
            dkr += dk[:, LANES:]
        dkr_ref[...] += dkr

    d_kv, d_krsum, g_kn_nope = pl.pallas_call(
        k_prep_bwd_kern, name="k_prep_bwd", grid=(nrp, nhg),
        in_specs=[heads_in, heads_out, pl.BlockSpec((HG, RP, LANES), lambda i, hg: (hg, i, 0)), fullb((1, LANES))],
        out_specs=[heads_in, pl.BlockSpec((RP, LANES), lambda i, hg: (i, 0)), fullb((1, LANES))],
        out_shape=[jax.ShapeDtypeStruct((N, H * HW), _BF), jax.ShapeDtypeStruct((N, LANES), F32),
                   jax.ShapeDtypeStruct((1, LANES), F32)],
        compiler_params=_cparams(("arbitrary", "arbitrary")),
    )(kv, d_kcat, d_v, g_kn)

    def krope_bwd_kern(dp_any, p_ref, d_ref, t_ref, g_ref, o_ref, dg_ref):
        @pl.when(pl.program_id(0) == 0)
        def _():
            dg_ref[...] = jnp.zeros(dg_ref.shape, F32)

        xh, r = _rms(p_ref[...].astype(F32), c.ROPE)
        drot = _rope_t(d_ref[...], t_ref[0], t_ref[1], t_ref[2])
        dx, dg = _rms_bwd(xh, r, g_ref[...], drot, c.ROPE)
        o_ref[...] = dx.astype(o_ref.dtype)
        dg_ref[...] += jnp.sum(dg, axis=0, keepdims=True)

    dproj5, g_kn_rope = pl.pallas_call(
        krope_bwd_kern, name="krope_bwd", grid=(nr,),
        in_specs=[any_spec, rowb(LANES, kr_blk), rowb(LANES, 0), tabs_of(R), fullb((1, LANES))],
        out_specs=[rowb(LANES, kr_blk), fullb((1, LANES))],
        out_shape=[jax.ShapeDtypeStruct((N, c.P), _BF), jax.ShapeDtypeStruct((1, LANES), F32)],
        input_output_aliases={0: 0}, compiler_params=_cparams(("arbitrary",)),
    )(dproj4, proj, d_krsum, tabs, g_kr)

    g_wuq_p = _mm(cqn, d_qp, ta=True, name="mm_dwuq", out_dtype=_BF)
    G['w_uq'] = g_wuq_p.reshape(QL, H, HW)[:, :, :c.NOPE + c.ROPE].reshape(QL, H * (c.NOPE + c.ROPE))
    d_cqn = _mm(d_qp, wuq_pT, name="mm_dcqn", out_dtype=F32)
    G['w_ukv'] = _mm(ckvn, d_kv, ta=True, name="mm_dwukv", out_dtype=_BF)
    d_ckvn = _mm(d_kv, wukvT, name="mm_dckvn", out_dtype=F32)

    def lora_bwd_kern(dp_any, p_ref, dq_ref, dkv_ref, gq_ref, gkv_ref, o_ref, dgq_ref, dgkv_ref):
        @pl.when(pl.program_id(0) == 0)
        def _():
            dgq_ref[...] = jnp.zeros(dgq_ref.shape, F32)
            dgkv_ref[...] = jnp.zeros(dgkv_ref.shape, F32)

        blk = p_ref[...].astype(F32)
        qh, rq = _rms(blk[:, :QL], QL)
        kh, rk = _rms(blk[:, QL:], KVL)
        dq, dgq = _rms_bwd(qh, rq, gq_ref[...], dq_ref[...], QL)
        dk, dgk = _rms_bwd(kh, rk, gkv_ref[...], dkv_ref[...], KVL)
        o_ref[:, :QL] = dq.astype(o_ref.dtype)
        o_ref[:, QL:] = dk.astype(o_ref.dtype)
        dgq_ref[...] += jnp.sum(dgq, axis=0, keepdims=True)
        dgkv_ref[...] += jnp.sum(dgk, axis=0, keepdims=True)

    dproj6, g_q_norm, g_kv_norm = pl.pallas_call(
        lora_bwd_kern, name="lora_bwd", grid=(nr,),
        in_specs=[any_spec, rowb(QL + KVL, lora_blk), rowb(QL, 0), rowb(KVL, 0), fullb((1, QL)), fullb((1, KVL))],
        out_specs=[rowb(QL + KVL, lora_blk), fullb((1, QL)), fullb((1, KVL))],
        out_shape=[jax.ShapeDtypeStruct((N, c.P), _BF), jax.ShapeDtypeStruct((1, QL), F32),
                   jax.ShapeDtypeStruct((1, KVL), F32)],
        input_output_aliases={0: 0}, compiler_params=_cparams(("arbitrary",)),
    )(dproj5, proj, d_cqn, d_ckvn, mla_q_norm_g, mla_kv_norm_g)

    g_win_p = _mm(h, dproj6, ta=True, name="mm_dwin", out_dtype=_BF, bn=1408)
    G['w_in'] = _win_from_padded(g_win_p, c)
    d_h = _mm(dproj6, win_pT, name="mm_dh", out_dtype=F32, bk=1408)

    def final_bwd_kern(x_ref, g_ref, dh_ref, dy_ref, gx_ref, dg_ref):
        @pl.when(pl.program_id(0) == 0)
        def _():
            dg_ref[...] = jnp.zeros(dg_ref.shape, F32)

        xh, r = _rms(x_ref[...], D)
        dx, dg = _rms_bwd(xh, r, g_ref[...], dh_ref[...], D)
        gx_ref[...] = dy_ref[...] + dx
        dg_ref[...] += jnp.sum(dg, axis=0, keepdims=True)

    grad_x, g_norm = pl.pallas_call(
        final_bwd_kern, name="final_bwd", grid=(nr,),
        in_specs=[rowb(D, 0), fullb((1, D)), rowb(D, 0), rowb(D, 0)],
        out_specs=[rowb(D, 0), fullb((1, D))],
        out_shape=[jax.ShapeDtypeStruct((N, D), F32), jax.ShapeDtypeStruct((1, D), F32)],
        compiler_params=_cparams(("arbitrary",)),
    )(xs, norm_g, d_h, dy)

    def to_blocks(n, g):
        if n in COL_SHARDED:
            return jnp.transpose(g.reshape(g.shape[0], NDEV, -1), (1, 0, 2))
        return g.reshape(NDEV, -1, g.shape[1])

    recv = _exchange([to_blocks(n, G[n]) for n in BIG], "exchange_grads")
    res = [{}, {}, {}, {}]
    for n, parts in zip(BIG, recv):
        outs = _adam(parts, W[n][0], Mo[n][0], Vo[n][0], "adam_" + n)
        for k in range(4):
            res[k][n] = outs[k][None]

    small_g = {'norm_g': g_norm, 'mla_q_norm_g': g_q_norm, 'mla_kv_norm_g': g_kv_norm,
               'mla_qn_nope_g': g_qn_nope, 'mla_qn_rope_g': g_qn_rope[:, :c.ROPE], 'mla_kn_nope_g': g_kn_nope,
               'mla_kn_rope_g': g_kn_rope[:, :c.ROPE], 'mem_norm_g': g_mem_norm, 'mem_qn_g': g_mem_qn,
               'mem_kn_g': g_mem_kn}
    small_part = _pack([small_g[n] for n in SMALL] + [g_convw[0:3, :]], 0)
    small_all = _all_gather([small_part], "ag_small_grads")[0]
    small_shapes = [W[n].shape for n in SMALL]
    pieces = _unpack(small_all, small_shapes + [(3, CW)])
    cw8 = CW // NDEV
    conv_mine = lax.dynamic_slice_in_dim(pieces[-1].reshape(NDEV, 3, NDEV, cw8), me, 1, axis=2)[:, :, 0, :]
    sm_parts = _pack(pieces[:-1] + [conv_mine], 1)
    sm_names = SMALL + ['conv_w']
    sm_shapes = small_shapes + [(3, cw8)]
    w_sm = _pack([W[n] for n in SMALL] + [conv_w[0]], 0)
    m_sm = _pack([Mo[n] for n in SMALL] + [m_conv_w[0]], 0)
    v_sm = _pack([Vo[n] for n in SMALL] + [v_conv_w[0]], 0)
    outs_sm = [_unpack(o, sm_shapes) for o in _adam(sm_parts, w_sm, m_sm, v_sm, "adam_small")]
    for k in range(4):
        for n, a in zip(sm_names, outs_sm[k]):
            res[k][n] = a[None] if n == 'conv_w' else a
    return (loss, grad_x[None], *[res[0][n] for n in WEIGHTS], *[res[1][n] for n in WEIGHTS],
            *[res[2][n] for n in WEIGHTS], *[res[3][n] for n in WEIGHTS])
```

```python
import math

import jax
import jax.numpy as jnp
from jax import lax
from jax.experimental import pallas as pl
from jax.experimental.pallas import tpu as pltpu

F32 = jnp.float32
_BF = jnp.bfloat16
EPS = 1e-6
CHUNK = 64
ROPE_THETA = 10000.0
ADAM_LR, ADAM_B1, ADAM_B2, ADAM_EPS, ADAM_WD, ADAM_STEP = 0.001, 0.9, 0.999, 1e-08, 0.01, 10
NDEV = 8
AXES = ("x", "y", "c")
MESH = pl.DeviceIdType.MESH
LANES = 128
NEG = -1e30
LOG2E = math.log2(math.e)
V7X_VMEM_LIMIT = 56 * 1024 * 1024
PACK_C = 1024
ATT_BLOCK = 512
ADAM_BLOCK_ELEMS = 256 * 1024

WEIGHTS = ['norm_g', 'w_in', 'conv_w', 'w_conv_out', 'mla_q_norm_g', 'w_uq', 'mla_kv_norm_g', 'w_ukv',
           'mla_qn_nope_g', 'mla_qn_rope_g', 'mla_kn_nope_g', 'mla_kn_rope_g', 'w_mla_out', 'mem_norm_g',
           'w_mem_kv', 'mem_qn_g', 'mem_kn_g', 'w_mem_out', 'w_o']
BIG = ['w_in', 'w_conv_out', 'w_uq', 'w_ukv', 'w_mla_out', 'w_mem_kv', 'w_mem_out', 'w_o']
COL_SHARDED = ('w_in', 'w_conv_out', 'w_uq', 'w_ukv', 'w_mem_out')
SMALL = ['norm_g', 'mla_q_norm_g', 'mla_kv_norm_g', 'mla_qn_nope_g', 'mla_qn_rope_g', 'mla_kn_nope_g',
         'mla_kn_rope_g', 'mem_norm_g', 'mem_qn_g', 'mem_kn_g']


def _tile(dim, target, align):
    if dim <= target:
        return dim
    t = target - target % align
    while t > 0:
        if dim % t == 0:
            return t
        t -= align
    raise ValueError(f"no tile for {dim} {target} {align}")


def _cparams(sem):
    return pltpu.CompilerParams(dimension_semantics=sem, vmem_limit_bytes=V7X_VMEM_LIMIT)


def _sig(x):
    return 1.0 / (1.0 + jnp.exp(-x))


def _rms(x, n):
    r = lax.rsqrt(jnp.sum(x * x, axis=-1, keepdims=True) * (1.0 / n) + EPS)
    return x * r, r


def _rms_bwd(xhat, r, g, dy, n):
    dxh = dy * g
    dx = r * (dxh - xhat * (jnp.sum(dxh * xhat, axis=-1, keepdims=True) * (1.0 / n)))
    return dx, dy * xhat


def _rope(x, cosp, sina, sinb):
    return x * cosp + pltpu.roll(x, 96, 1) * sina + pltpu.roll(x, 32, 1) * sinb


def _rope_t(d, cosp, sina, sinb):
    return d * cosp + pltpu.roll(d * sina, 32, 1) + pltpu.roll(d * sinb, 96, 1)


def _dot_nt(a, b):
    return lax.dot_general(a, b, (((1,), (1,)), ((), ())), preferred_element_type=F32)


def _dot_tn(a, b):
    return lax.dot_general(a, b, (((0,), (0,)), ((), ())), preferred_element_type=F32)


def _dot(a, b):
    return jnp.dot(a, b, preferred_element_type=F32)


def _all_gather(shards, name):
    na = len(shards)

    def body(*refs):
        x_refs, out_refs = refs[:na], refs[na:2 * na]
        send_sems, recv_sems, local_sems = refs[2 * na:]
        x, y, c = lax.axis_index("x"), lax.axis_index("y"), lax.axis_index("c")
        me, sibling = (x, y, c), (x, y, 1 - c)
        chips = [(1 - x, y), (x, 1 - y), (1 - x, 1 - y)]

        def rows(a, px, py, pc):
            return out_refs[a].at[4 * px + 2 * py + pc]

        def copy(a, k, block, to, src=None):
            return pltpu.make_async_remote_copy(
                src_ref=rows(a, *block) if src is None else src, dst_ref=rows(a, *block),
                send_sem=send_sems.at[7 * a + k], recv_sem=recv_sems.at[7 * a + k],
                device_id=to, device_id_type=MESH)

        mine = [pltpu.make_async_copy(x_refs[a], rows(a, *me), local_sems.at[a]) for a in range(na)]
        for cp in mine:
            cp.start()
        first = [copy(a, 0, me, sibling, src=x_refs[a]) for a in range(na)]
        first += [copy(a, 1 + j, me, (*chip, c), src=x_refs[a]) for j, chip in enumerate(chips) for a in range(na)]
        for cp in first:
            cp.start()
        passed = []
        for j, chip in enumerate(chips):
            for a in range(na):
                copy(a, 1 + j, (*chip, c), me).wait_recv()
                fwd = copy(a, 4 + j, (*chip, c), sibling)
                fwd.start()
                passed.append(fwd)
        for a in range(na):
            copy(a, 0, sibling, me).wait_recv()
        for j, chip in enumerate(chips):
            for a in range(na):
                copy(a, 4 + j, (*chip, 1 - c), me).wait_recv()
        for cp in first + passed:
            cp.wait_send()
        for cp in mine:
            cp.wait()

    any_spec = pl.BlockSpec(memory_space=pl.ANY)
    return pl.pallas_call(
        body, name=name,
        out_shape=[jax.ShapeDtypeStruct((NDEV,) + s.shape, s.dtype) for s in shards],
        in_specs=[any_spec] * na, out_specs=[any_spec] * na,
        scratch_shapes=[pltpu.SemaphoreType.DMA((7 * na,)), pltpu.SemaphoreType.DMA((7 * na,)),
                        pltpu.SemaphoreType.DMA((na,))],
    )(*shards)


def _exchange(blocks, name):
    na = len(blocks)

    def body(*refs):
        x_refs, out_refs = refs[:na], refs[na:2 * na]
        send_sems, recv_sems, local_sems = refs[2 * na:]
        x, y, c = lax.axis_index("x"), lax.axis_index("y"), lax.axis_index("c")
        me = 4 * x + 2 * y + c

        def copy(a, k):
            fx, fy, fc = (k + 1) >> 2 & 1, (k + 1) >> 1 & 1, (k + 1) & 1
            tx, ty, tc = x ^ fx, y ^ fy, c ^ fc
            return pltpu.make_async_remote_copy(
                src_ref=x_refs[a].at[4 * tx + 2 * ty + tc], dst_ref=out_refs[a].at[me],
                send_sem=send_sems.at[7 * a + k], recv_sem=recv_sems.at[7 * a + k],
                device_id=(tx, ty, tc), device_id_type=MESH)

        mine = [pltpu.make_async_copy(x_refs[a].at[me], out_refs[a].at[me], local_sems.at[a]) for a in range(na)]
        for cp in mine:
            cp.start()
        copies = [copy(a, k) for k in range(7) for a in range(na)]
        for cp in copies:
            cp.start()
        for cp in copies:
            cp.wait_recv()
        for cp in copies:
            cp.wait_send()
        for cp in mine:
            cp.wait()

    any_spec = pl.BlockSpec(memory_space=pl.ANY)
    return pl.pallas_call(
        body, name=name,
        out_shape=[jax.ShapeDtypeStruct(b.shape, b.dtype) for b in blocks],
        in_specs=[any_spec] * na, out_specs=[any_spec] * na,
        scratch_shapes=[pltpu.SemaphoreType.DMA((7 * na,)), pltpu.SemaphoreType.DMA((7 * na,)),
                        pltpu.SemaphoreType.DMA((na,))],
    )(*blocks)


_HBM = pl.BlockSpec(memory_space=pltpu.HBM)
_SEM = pl.BlockSpec(memory_space=pltpu.SEMAPHORE)
_EFFECT = pltpu.SideEffectType.DATAFLOW_SIDE_EFFECTING


def _peer(k):
    x, y, c = lax.axis_index("x"), lax.axis_index("y"), lax.axis_index("c")
    tx, ty, tc = x ^ ((k + 1) >> 2 & 1), y ^ ((k + 1) >> 1 & 1), c ^ ((k + 1) & 1)
    return (tx, ty, tc), 4 * tx + 2 * ty + tc


def _exchange_copy(a, k, x_refs, land_refs, send_sems, recv_sems, receive_side):
    x, y, c = lax.axis_index("x"), lax.axis_index("y"), lax.axis_index("c")
    me = 4 * x + 2 * y + c
    peer, peer_idx = _peer(k)
    return pltpu.make_async_remote_copy(
        src_ref=x_refs[a].at[peer_idx], dst_ref=land_refs[a].at[peer_idx if receive_side else me],
        send_sem=send_sems.at[7 * a + k], recv_sem=recv_sems.at[7 * a + k], device_id=peer, device_id_type=MESH)


def _exchange_start(blocks, lands, name):
    na = len(blocks)

    def body(*refs):
        x_refs, land_refs = refs[:na], refs[na:2 * na]
        send_sems, recv_sems = refs[2 * na], refs[2 * na + 1]
        token = refs[-1]
        for k in range(7):
            for a in range(na):
                _exchange_copy(a, k, x_refs, land_refs, send_sems, recv_sems, False).start()
        token[...] = jnp.zeros_like(token)

    hbm = [pltpu.HBM(b.shape, b.dtype) for b in blocks]
    outs = pl.pallas_call(
        body, name=name,
        out_shape=(pltpu.SemaphoreType.DMA((7 * na,)), pltpu.SemaphoreType.DMA((7 * na,)), *hbm, *hbm,
                   jax.ShapeDtypeStruct((8, LANES), F32)),
        in_specs=[_HBM] * (2 * na),
        out_specs=(_SEM, _SEM, *[_HBM] * (2 * na), pl.BlockSpec(memory_space=pltpu.VMEM)),
        input_output_aliases={j: 2 + j for j in range(2 * na)},
        compiler_params=pltpu.CompilerParams(has_side_effects=_EFFECT),
    )(*[pltpu.with_memory_space_constraint(b, pltpu.HBM) for b in blocks],
      *[pltpu.with_memory_space_constraint(l, pltpu.HBM) for l in lands])
    return outs[0], outs[1], outs[2:2 + na], outs[2 + na:2 + 2 * na], outs[-1]


def _exchange_wait(send_sems, recv_sems, blocks, lands, after, name):
    na = len(blocks)

    def body(*refs):
        x_refs, land_refs = refs[:na], refs[na:2 * na]
        send_s, recv_s = refs[2 * na], refs[2 * na + 1]
        for k in range(7):
            for a in range(na):
                cp = _exchange_copy(a, k, x_refs, land_refs, send_s, recv_s, True)
                cp.wait_send()
                cp.wait_recv()

    hbm = [pltpu.HBM(b.shape, b.dtype) for b in blocks]
    outs = pl.pallas_call(
        body, name=name, out_shape=(*hbm, *hbm),
        in_specs=[_HBM] * (2 * na) + [_SEM, _SEM, pl.BlockSpec(memory_space=pl.ANY)],
        out_specs=tuple([_HBM] * (2 * na)),
        input_output_aliases={j: j for j in range(2 * na)},
        compiler_params=pltpu.CompilerParams(has_side_effects=_EFFECT),
    )(*blocks, *lands, send_sems, recv_sems, after)
    return outs[na:]


def _seg_rows(size):
    rows = -(-size // PACK_C)
    return -(-rows // 16) * 16


def _pack(arrs, lead):
    parts = []
    for a in arrs:
        lshape = a.shape[:lead]
        f = a.reshape(lshape + (-1,)).astype(F32)
        rows = _seg_rows(f.shape[-1])
        f = jnp.pad(f, [(0, 0)] * lead + [(0, rows * PACK_C - f.shape[-1])])
        parts.append(f.reshape(lshape + (rows, PACK_C)))
    return jnp.concatenate(parts, axis=lead)


def _unpack(buf, shapes):
    lshape = buf.shape[:-2]
    out, r = [], 0
    for shp in shapes:
        size = math.prod(shp)
        rows = _seg_rows(size)
        seg = buf[..., r:r + rows, :].reshape(lshape + (rows * PACK_C,))[..., :size]
        out.append(seg.reshape(lshape + tuple(shp)))
        r += rows
    return out


def _mm(a, b, *, name, out_dtype, ta=False, bm=1024, bn=1024, bk=2048, after=None):
    if ta:
        kdim, m = a.shape
    else:
        m, kdim = a.shape
    k2, n = b.shape
    assert kdim == k2, (a.shape, b.shape)
    bm = _tile(m, bm, LANES if ta else 16)
    bn = _tile(n, bn, LANES)
    bk = _tile(kdim, bk, LANES)
    nk = kdim // bk

    def kern(a_ref, b_ref, *rest):
        o_ref, scratch = (rest[1], rest[2:]) if after is not None else (rest[0], rest[1:])
        part = _dot_tn(a_ref[...], b_ref[...]) if ta else _dot(a_ref[...], b_ref[...])
        if nk == 1:
            o_ref[...] = part.astype(o_ref.dtype)
        else:
            acc = scratch[0]
            k = pl.program_id(2)

            @pl.when(k == 0)
            def _():
                acc[...] = part

            @pl.when(jnp.logical_and(k > 0, k < nk - 1))
            def _():
                acc[...] += part

            @pl.when(k == nk - 1)
            def _():
                o_ref[...] = (acc[...] + part).astype(o_ref.dtype)

    a_spec = pl.BlockSpec((bk, bm), lambda i, j, k: (k, i)) if ta else pl.BlockSpec((bm, bk), lambda i, j, k: (i, k))
    extra_specs = [pl.BlockSpec(after.shape, lambda i, j, k: (0, 0))] if after is not None else []
    extra_args = [after] if after is not None else []
    return pl.pallas_call(
        kern, name=name, grid=(m // bm, n // bn, nk),
        in_specs=[a_spec, pl.BlockSpec((bk, bn), lambda i, j, k: (k, j))] + extra_specs,
        out_specs=pl.BlockSpec((bm, bn), lambda i, j, k: (i, j)),
        out_shape=jax.ShapeDtypeStruct((m, n), out_dtype),
        scratch_shapes=[pltpu.VMEM((bm, bn), F32)] if nk > 1 else [],
        compiler_params=_cparams(("parallel", "parallel", "arbitrary")),
    )(a, b, *extra_args)


def _adam(parts, w_a, m_a, v_a, name):
    rows, cols = w_a.shape
    rb = _tile(rows, max(8, ADAM_BLOCK_ELEMS // cols // 8 * 8), 8)
    bc1 = 1.0 - ADAM_B1 ** ADAM_STEP
    bc2 = 1.0 - ADAM_B2 ** ADAM_STEP

    def adam_kern(p_ref, w_ref, m_ref, v_ref, g_ref, d_ref, nm_ref, nv_ref):
        g = p_ref[0].astype(F32)
        for j in range(1, NDEV):
            g = g + p_ref[j].astype(F32)
        m_new = ADAM_B1 * m_ref[...] + (1.0 - ADAM_B1) * g
        v_new = ADAM_B2 * v_ref[...] + (1.0 - ADAM_B2) * (g * g)
        g_ref[...] = g
        nm_ref[...] = m_new
        nv_ref[...] = v_new
        d_ref[...] = -ADAM_LR * ((m_new / bc1) / (jnp.sqrt(v_new / bc2) + ADAM_EPS) + ADAM_WD * w_ref[...])

    blk = pl.BlockSpec((rb, cols), lambda i: (i, 0))
    return pl.pallas_call(
        adam_kern, name=name, grid=(rows // rb,),
        in_specs=[pl.BlockSpec((NDEV, rb, cols), lambda i: (0, i, 0)), blk, blk, blk],
        out_specs=[blk] * 4, out_shape=[jax.ShapeDtypeStruct((rows, cols), F32)] * 4,
        compiler_params=_cparams(("parallel",)),
    )(parts, w_a, m_a, v_a)


class _Cfg:
    pass


def _config(x, conv_w, w_uq, w_ukv, mla_qn_nope_g, mla_qn_rope_g, mem, mem_qn_g, w_mem_out, w_mla_out):
    c = _Cfg()
    c.N, c.D = x.shape[1], x.shape[2]
    c.CW = conv_w.shape[2] * NDEV
    c.QL, c.KVL = w_uq.shape[1], w_ukv.shape[1]
    c.NOPE, c.ROPE = mla_qn_nope_g.shape[1], mla_qn_rope_g.shape[1]
    c.H = w_uq.shape[2] * NDEV // (c.NOPE + c.ROPE)
    c.V = w_ukv.shape[2] * NDEV // c.H - c.NOPE
    assert c.NOPE == LANES and c.V == LANES and c.ROPE == LANES // 2
    c.HW = 2 * LANES
    c.HV = c.H * c.V
    assert w_mla_out.shape[1] * NDEV == c.HV
    c.M = mem.shape[1]
    c.MHD = mem_qn_g.shape[1]
    c.MW = w_mem_out.shape[1]
    c.MH = c.MW // c.MHD
    c.o_conv = 0
    c.o_mz = 4 * c.CW
    c.o_g = c.o_mz + c.HV
    c.o_mem = c.o_g + 3 * c.D
    c.o_lora = c.o_mem + 2 * c.MW
    c.o_kr = c.o_lora + c.QL + c.KVL
    c.P = c.o_kr + LANES
    assert c.o_mz % c.HV == 0 and c.o_g % (3 * c.D) == 0 and c.o_mem % (2 * c.MW) == 0
    assert c.o_lora % (c.QL + c.KVL) == 0 and c.QL % LANES == 0 and c.KVL % LANES == 0
    c.IN = 4 * c.CW + c.QL + c.KVL + c.ROPE + c.HV + 2 * c.MW + 3 * c.D
    c.R = _tile(c.N, 256, 16)
    c.RP = _tile(c.N, 512, 16)
    c.HG = _tile(c.H, 4, 1)
    c.B = _tile(c.N, ATT_BLOCK, CHUNK)
    c.scale = float((c.NOPE + c.ROPE) ** -0.5)
    c.mscale = float(c.MHD ** -0.5)
    return c


def _win_to_padded(w, c):
    o = 0
    parts = {}
    for nm, wd in (('conv', 4 * c.CW), ('lora', c.QL + c.KVL), ('kr', c.ROPE), ('mz', c.HV), ('mem', 2 * c.MW), ('g', 3 * c.D)):
        parts[nm] = w[:, o:o + wd]
        o += wd
    kr = jnp.pad(parts['kr'], ((0, 0), (0, LANES - c.ROPE)))
    return jnp.concatenate([parts['conv'], parts['mz'], parts['g'], parts['mem'], parts['lora'], kr], axis=1)


def _win_from_padded(g, c):
    conv = g[:, c.o_conv:c.o_mz]
    mz = g[:, c.o_mz:c.o_g]
    gates = g[:, c.o_g:c.o_mem]
    mem = g[:, c.o_mem:c.o_lora]
    lora = g[:, c.o_lora:c.o_kr]
    kr = g[:, c.o_kr:c.o_kr + c.ROPE]
    return jnp.concatenate([conv, lora, kr, mz, mem, gates], axis=1)


def kernel(x, positions, mem, norm_g, w_in, conv_w, w_conv_out, mla_q_norm_g, w_uq, mla_kv_norm_g, w_ukv, mla_qn_nope_g, mla_qn_rope_g, mla_kn_nope_g, mla_kn_rope_g, w_mla_out, mem_norm_g, w_mem_kv, mem_qn_g, mem_kn_g, w_mem_out, w_o, loss_target, m_norm_g, m_w_in, m_conv_w, m_w_conv_out, m_mla_q_norm_g, m_w_uq, m_mla_kv_norm_g, m_w_ukv, m_mla_qn_nope_g, m_mla_qn_rope_g, m_mla_kn_nope_g, m_mla_kn_rope_g, m_w_mla_out, m_mem_norm_g, m_w_mem_kv, m_mem_qn_g, m_mem_kn_g, m_w_mem_out, m_w_o, v_norm_g, v_w_in, v_conv_w, v_w_conv_out, v_mla_q_norm_g, v_w_uq, v_mla_kv_norm_g, v_w_ukv, v_mla_qn_nope_g, v_mla_qn_rope_g, v_mla_kn_nope_g, v_mla_kn_rope_g, v_w_mla_out, v_mem_norm_g, v_w_mem_kv, v_mem_qn_g, v_mem_kn_g, v_w_mem_out, v_w_o):
    args = dict(locals())
    W = {n: args[n] for n in WEIGHTS}
    Mo = {n: args['m_' + n] for n in WEIGHTS}
    Vo = {n: args['v_' + n] for n in WEIGHTS}
    c = _config(x, conv_w, w_uq, w_ukv, mla_qn_nope_g, mla_qn_rope_g, mem, mem_qn_g, w_mem_out, w_mla_out)
    N, D, R, B, H = c.N, c.D, c.R, c.B, c.H
    assert x.shape[0] == 1
    xs = x[0]
    tgt = loss_target[0]
    memx = mem[0]
    me = 4 * lax.axis_index("x") + 2 * lax.axis_index("y") + lax.axis_index("c")
    nr = N // R

    gathered = _all_gather([W[n][0].astype(_BF) for n in BIG] + [conv_w[0]], "ag_weights")
    Wf, WfT = {}, {}
    for n, g in zip(BIG, gathered[:-1]):
        if n in COL_SHARDED:
            Wf[n] = jnp.transpose(g, (1, 0, 2)).reshape(g.shape[1], -1)
            WfT[n] = jnp.transpose(g, (0, 2, 1)).reshape(-1, g.shape[1])
        else:
            Wf[n] = g.reshape(-1, g.shape[2])
            WfT[n] = Wf[n].T
    convw = jnp.transpose(gathered[-1], (1, 0, 2)).reshape(3, c.CW)
    convw8 = jnp.pad(convw, ((0, 5), (0, 0)))

    win_p = _win_to_padded(Wf['w_in'], c)
    win_pT = win_p.T
    wuq = Wf['w_uq'].reshape(c.QL, H, c.NOPE + c.ROPE)
    wuq_p = jnp.pad(wuq, ((0, 0), (0, 0), (0, c.HW - c.NOPE - c.ROPE))).reshape(c.QL, H * c.HW)
    wuq_pT = wuq_p.T
    wukv, wukvT = Wf['w_ukv'], WfT['w_ukv']
    wco, wmo, wmkv, wmemo, wo = Wf['w_conv_out'], Wf['w_mla_out'], Wf['w_mem_kv'], Wf['w_mem_out'], Wf['w_o']
    wcoT, wmoT, wmkvT, wmemoT, woT = WfT['w_conv_out'], WfT['w_mla_out'], WfT['w_mem_kv'], WfT['w_mem_out'], WfT['w_o']

    def rowb(width, cidx):
        return pl.BlockSpec((R, width), lambda i, _c=cidx: (i, _c))

    def fullb(shape):
        nd = len(shape)
        return pl.BlockSpec(shape, lambda *_: (0,) * nd)

    def pad_lanes(g, w=LANES):
        return jnp.pad(g, ((0, 0), (0, w - g.shape[1])))

    def tabs_of(rows):
        return pl.BlockSpec((3, rows, LANES), lambda i, *_: (0, i, 0))

    half = c.ROPE // 2
    inv_freq = jnp.power(ROPE_THETA, -jnp.arange(half, dtype=F32) / half)
    invf = jnp.concatenate([inv_freq, inv_freq, jnp.zeros((LANES - c.ROPE,), F32)])[None, :]
    pos_col = positions[0].astype(F32).reshape(N, 1)

    def rope_tab_kern(pos_ref, invf_ref, o_ref):
        ang = pos_ref[...] * invf_ref[...]
        co, si = jnp.cos(ang), jnp.sin(ang)
        lane = lax.broadcasted_iota(jnp.int32, ang.shape, 1)
        o_ref[0] = jnp.where(lane < c.ROPE, co, 0.0)
        o_ref[1] = jnp.where(lane < half, -si, 0.0)
        o_ref[2] = jnp.where(jnp.logical_and(lane >= half, lane < c.ROPE), si, 0.0)

    tabs = pl.pallas_call(
        rope_tab_kern, name="rope_tab", grid=(nr,),
        in_specs=[pl.BlockSpec((R, 1), lambda i: (i, 0)), fullb((1, LANES))],
        out_specs=tabs_of(R),
        out_shape=jax.ShapeDtypeStruct((3, N, LANES), F32),
        compiler_params=_cparams(("parallel",)),
    )(pos_col, invf)

    def make_rms_kern():
        def rms_fwd_kern(x_ref, g_ref, o_ref):
            xh, _ = _rms(x_ref[...].astype(F32), x_ref.shape[-1])
            o_ref[...] = (xh * g_ref[...]).astype(o_ref.dtype)
        return rms_fwd_kern

    h = pl.pallas_call(
        make_rms_kern(), name="rms_x", grid=(nr,),
        in_specs=[rowb(D, 0), fullb((1, D))], out_specs=rowb(D, 0),
        out_shape=jax.ShapeDtypeStruct((N, D), _BF), compiler_params=_cparams(("parallel",)),
    )(xs, norm_g)

    proj = _mm(h, win_p, name="mm_proj", out_dtype=_BF, bn=1408)

    CW = c.CW
    conv_blk = c.o_conv // (4 * CW)
    HALO = 16
    rh = R // HALO

    def conv_parts(blk):
        blk = blk.astype(F32)
        return blk[:, 0:CW], blk[:, CW:2 * CW], blk[:, 2 * CW:3 * CW], blk[:, 3 * CW:4 * CW]

    def shifted(cu, prev, i):
        prev = jnp.where(i > 0, prev, 0.0)
        rid = lax.broadcasted_iota(jnp.int32, cu.shape, 0)
        last, last2 = prev[HALO - 1:HALO, :], prev[HALO - 2:HALO - 1, :]
        sh1 = jnp.where(rid == 0, last, pltpu.roll(cu, 1, 0))
        sh2 = jnp.where(rid == 0, last2, jnp.where(rid == 1, last, pltpu.roll(cu, 2, 0)))
        return sh1, sh2

    def conv_fwd_kern(p_ref, prev_ref, w_ref, o_ref):
        i = pl.program_id(0)
        cg, bg, u, z = conv_parts(p_ref[...])
        pc, _, pu, _ = conv_parts(prev_ref[...])
        cu = cg * u
        sh1, sh2 = shifted(cu, pc * pu, i)
        w = w_ref[...]
        conv = w[0:1, :] * sh2 + w[1:2, :] * sh1 + w[2:3, :] * cu
        o_ref[...] = (bg * conv * (z * _sig(z))).astype(o_ref.dtype)

    prev_spec = pl.BlockSpec((HALO, 4 * CW), lambda i: (jnp.maximum(i * rh - 1, 0), conv_blk))
    a_conv = pl.pallas_call(
        conv_fwd_kern, name="conv_fwd", grid=(nr,),
        in_specs=[rowb(4 * CW, conv_blk), prev_spec, fullb((8, CW))],
        out_specs=rowb(CW, 0), out_shape=jax.ShapeDtypeStruct((N, CW), _BF),
        compiler_params=_cparams(("parallel",)),
    )(proj, proj, convw8)
    o_conv = _mm(a_conv, wco, name="mm_oconv", out_dtype=_BF)

    QL, KVL, HW = c.QL, c.KVL, c.HW
    lora_blk = c.o_lora // (QL + KVL)

    def lora_fwd_kern(p_ref, gq_ref, gkv_ref, q_ref, kv_ref):
        blk = p_ref[...].astype(F32)
        qh, _ = _rms(blk[:, :QL], QL)
        kh, _ = _rms(blk[:, QL:], KVL)
        q_ref[...] = (qh * gq_ref[...]).astype(q_ref.dtype)
        kv_ref[...] = (kh * gkv_ref[...]).astype(kv_ref.dtype)

    cqn, ckvn = pl.pallas_call(
        lora_fwd_kern, name="lora_fwd", grid=(nr,),
        in_specs=[rowb(QL + KVL, lora_blk), fullb((1, QL)), fullb((1, KVL))],
        out_specs=[rowb(QL, 0), rowb(KVL, 0)],
        out_shape=[jax.ShapeDtypeStruct((N, QL), _BF), jax.ShapeDtypeStruct((N, KVL), _BF)],
        compiler_params=_cparams(("parallel",)),
    )(proj, mla_q_norm_g, mla_kv_norm_g)
    q_p = _mm(cqn, wuq_p, name="mm_q", out_dtype=_BF)
    kv = _mm(ckvn, wukv, name="mm_kv", out_dtype=_BF)

    g_qn, g_qr = mla_qn_nope_g, pad_lanes(mla_qn_rope_g)
    g_kn, g_kr = mla_kn_nope_g, pad_lanes(mla_kn_rope_g)
    kr_blk = c.o_kr // LANES

    def krope_fwd_kern(p_ref, t_ref, g_ref, o_ref):
        xh, _ = _rms(p_ref[...].astype(F32), c.ROPE)
        o_ref[...] = _rope(xh * g_ref[...], t_ref[0], t_ref[1], t_ref[2]).astype(o_ref.dtype)

    k_rope = pl.pallas_call(
        krope_fwd_kern, name="krope_fwd", grid=(nr,),
        in_specs=[rowb(LANES, kr_blk), tabs_of(R), fullb((1, LANES))],
        out_specs=rowb(LANES, 0), out_shape=jax.ShapeDtypeStruct((N, LANES), _BF),
        compiler_params=_cparams(("parallel",)),
    )(proj, tabs, g_kr)

    RP, HG = c.RP, c.HG
    nrp, nhg = N // RP, H // HG
    heads_in = pl.BlockSpec((RP, HG * HW), lambda i, hg: (i, hg))
    heads_out = pl.BlockSpec((HG, RP, HW), lambda i, hg: (hg, i, 0))

    def q_prep_kern(q_ref, t_ref, gn_ref, gr_ref, o_ref):
        for g in range(HG):
            blk = q_ref[:, g * HW:(g + 1) * HW].astype(F32)
            nh, _ = _rms(blk[:, :LANES], c.NOPE)
            rhat, _ = _rms(blk[:, LANES:], c.ROPE)
            rot = _rope(rhat * gr_ref[...], t_ref[0], t_ref[1], t_ref[2])
            o_ref[g] = (jnp.concatenate([nh * gn_ref[...], rot], axis=1) * (c.scale * LOG2E)).astype(o_ref.dtype)

    q_cat = pl.pallas_call(
        q_prep_kern, name="q_prep", grid=(nrp, nhg),
        in_specs=[heads_in, tabs_of(RP), fullb((1, LANES)), fullb((1, LANES))],
        out_specs=heads_out, out_shape=jax.ShapeDtypeStruct((H, N, HW), _BF),
        compiler_params=_cparams(("parallel", "parallel")),
    )(q_p, tabs, g_qn, g_qr)

    def k_prep_kern(kv_ref, kr_ref, gn_ref, o_ref):
        for g in range(HG):
            kn, _ = _rms(kv_ref[:, g * HW:g * HW + LANES].astype(F32), c.NOPE)
            o_ref[g] = jnp.concatenate([(kn * gn_ref[...]).astype(o_ref.dtype), kr_ref[...]], axis=1)

    k_cat = pl.pallas_call(
        k_prep_kern, name="k_prep", grid=(nrp, nhg),
        in_specs=[heads_in, pl.BlockSpec((RP, LANES), lambda i, hg: (i, 0)), fullb((1, LANES))],
        out_specs=heads_out, out_shape=jax.ShapeDtypeStruct((H, N, HW), _BF),
        compiler_params=_cparams(("parallel", "parallel")),
    )(kv, k_rope, g_kn)

    nb = N // B
    assert CHUNK & (CHUNK - 1) == 0 and B % CHUNK == 0

    def diag_mask(s, row0=0):
        row = lax.broadcasted_iota(jnp.int32, s.shape, 0) + row0
        col = lax.broadcasted_iota(jnp.int32, s.shape, 1)
        shift = CHUNK.bit_length() - 1
        allowed = jnp.right_shift(col, shift) <= jnp.right_shift(row, shift)
        return jnp.where(allowed, s, NEG)

    k_head = pl.BlockSpec((1, N, HW), lambda hh, i: (hh, 0, 0))
    v_head = pl.BlockSpec((N, LANES), lambda hh, i: (0, 2 * hh + 1))
    q_blk = pl.BlockSpec((1, B, HW), lambda hh, i: (hh, i, 0))
    o_blk = pl.BlockSpec((B, LANES), lambda hh, i: (i, hh))
    lse_blk = pl.BlockSpec((1, B, LANES), lambda hh, i: (hh, i, 0))

    def attn_fwd_kern(q_ref, k_ref, v_ref, o_ref, lse_ref, m_sc, acc_sc, s_sc):
        i = pl.program_id(1)
        m_sc[...] = jnp.full(m_sc.shape, NEG, F32)
        acc_sc[...] = jnp.zeros(acc_sc.shape, F32)

        def rows_of(t):
            return pl.ds(pl.multiple_of(t * B, B), B)

        def scores(t, slot):
            s_sc[slot] = _dot_nt(q_ref[0], k_ref[0, rows_of(t), :])

        def softmax_pv(t, slot, masked):
            s = s_sc[slot]
            if masked:
                s = diag_mask(s)
            mt = s[:, 0:LANES]
            for cb in range(1, B // LANES):
                mt = jnp.maximum(mt, s[:, cb * LANES:(cb + 1) * LANES])
            m_prev = m_sc[...]
            m_new = jnp.maximum(m_prev, jnp.max(mt, axis=1, keepdims=True))
            alpha = jnp.exp2(m_prev - m_new)
            p = jnp.concatenate([jnp.exp2(s[:, cb * LANES:(cb + 1) * LANES] - m_new).astype(_BF)
                                 for cb in range(B // LANES)], axis=1)
            v_ones = jnp.concatenate([v_ref[rows_of(t), :], jnp.ones((B, LANES), _BF)], axis=1)
            acc_sc[...] = jnp.concatenate([alpha, alpha], axis=1) * acc_sc[...] + _dot(p, v_ones)
            m_sc[...] = m_new

        scores(0, 0)

        def two_blocks(u, carry):
            t = 2 * u
            scores(t + 1, 1)
            softmax_pv(t, 0, False)
            scores(t + 2, 0)
            softmax_pv(t + 1, 1, False)
            return carry

        lax.fori_loop(0, i // 2, two_blocks, 0)

        @pl.when(i % 2 == 1)
        def _():
            scores(i, 1)
            softmax_pv(i - 1, 0, False)
            softmax_pv(i, 1, True)

        @pl.when(i % 2 == 0)
        def _():
            softmax_pv(i, 0, True)

        acc = acc_sc[...]
        o_ref[...] = (acc[:, :LANES] / acc[:, LANES:]).astype(o_ref.dtype)
        lse_ref[0] = m_sc[...] + jnp.log2(acc[:, LANES:])

    mla_y, lse = pl.pallas_call(
        attn_fwd_kern, name="attn_fwd", grid=(H, nb),
        in_specs=[q_blk, k_head, v_head], out_specs=[o_blk, lse_blk],
        out_shape=[jax.ShapeDtypeStruct((N, c.HV), _BF), jax.ShapeDtypeStruct((H, N, LANES), F32)],
        scratch_shapes=[pltpu.VMEM((B, LANES), F32), pltpu.VMEM((B, HW), F32), pltpu.VMEM((2, B, B), F32)],
        compiler_params=_cparams(("parallel", "arbitrary")),
    )(q_cat, k_cat, kv)

    HV = c.HV
    mz_blk = c.o_mz // HV

    def gate_fwd_kern(y_ref, z_ref, o_ref):
        z = z_ref[...].astype(F32)
        o_ref[...] = (y_ref[...].astype(F32) * (z * _sig(z))).astype(o_ref.dtype)

    a_mla = pl.pallas_call(
        gate_fwd_kern, name="gate_mla", grid=(nr,),
        in_specs=[rowb(HV, 0), rowb(HV, mz_blk)], out_specs=rowb(HV, 0),
        out_shape=jax.ShapeDtypeStruct((N, HV), _BF), compiler_params=_cparams(("parallel",)),
    )(mla_y, proj)
    o_mla = _mm(a_mla, wmo, name="mm_omla", out_dtype=_BF)

    M, MW, MH, MHD = c.M, c.MW, c.MH, c.MHD
    memn = pl.pallas_call(
        make_rms_kern(), name="rms_mem",
        grid=(1,), in_specs=[fullb((M, D)), fullb((1, D))], out_specs=fullb((M, D)),
        out_shape=jax.ShapeDtypeStruct((M, D), _BF), compiler_params=_cparams(("arbitrary",)),
    )(memx, mem_norm_g)
    kvm = _mm(memn, wmkv, name="mm_memkv", out_dtype=F32)

    def memk_fwd_kern(kv_ref, g_ref, k_ref, v_ref):
        for hh in range(MH):
            kh, _ = _rms(kv_ref[:, hh * MHD:(hh + 1) * MHD], MHD)
            k_ref[:, hh * MHD:(hh + 1) * MHD] = (kh * g_ref[...]).astype(k_ref.dtype)
        v_ref[...] = kv_ref[:, MW:].astype(v_ref.dtype)

    mem_k, mem_v = pl.pallas_call(
        memk_fwd_kern, name="memk_fwd", grid=(1,),
        in_specs=[fullb((M, 2 * MW)), fullb((1, MHD))], out_specs=[fullb((M, MW)), fullb((M, MW))],
        out_shape=[jax.ShapeDtypeStruct((M, MW), _BF)] * 2, compiler_params=_cparams(("arbitrary",)),
    )(kvm, mem_kn_g)

    mem_blk = c.o_mem // (2 * MW)

    def mem_head(qz_ref, k_ref, v_ref, g_ref, hh):
        sl = slice(hh * MHD, (hh + 1) * MHD)
        qh, r = _rms(qz_ref[:, sl].astype(F32), MHD)
        qn = (qh * g_ref[...]).astype(_BF)
        s = _dot_nt(qn, k_ref[:, sl]) * c.mscale
        e = jnp.exp(s - jnp.max(s, axis=1, keepdims=True))
        p = e / jnp.sum(e, axis=1, keepdims=True)
        y = _dot(p.astype(_BF), v_ref[:, sl])
        z = qz_ref[:, MW + hh * MHD:MW + (hh + 1) * MHD].astype(F32)
        return sl, qh, r, qn, p, y, z

    def mem_fwd_kern(qz_ref, k_ref, v_ref, g_ref, o_ref):
        for hh in range(MH):
            sl, _, _, _, _, y, z = mem_head(qz_ref, k_ref, v_ref, g_ref, hh)
            o_ref[:, sl] = (y * (z * _sig(z))).astype(o_ref.dtype)

    a_mem = pl.pallas_call(
        mem_fwd_kern, name="mem_fwd", grid=(nr,),
        in_specs=[rowb(2 * MW, mem_blk), fullb((M, MW)), fullb((M, MW)), fullb((1, MHD))],
        out_specs=rowb(MW, 0), out_shape=jax.ShapeDtypeStruct((N, MW), _BF),
        compiler_params=_cparams(("parallel",)),
    )(proj, mem_k, mem_v, mem_qn_g)
    o_mem = _mm(a_mem, wmemo, name="mm_omem", out_dtype=_BF)

    g_blk = c.o_g // (3 * D)

    def merge_fwd_kern(g_ref, oc_ref, om_ref, ome_ref, o_ref):
        g = g_ref[...].astype(F32)
        acc = _sig(g[:, :D]) * oc_ref[...].astype(F32)
        acc += _sig(g[:, D:2 * D]) * om_ref[...].astype(F32)
        acc += _sig(g[:, 2 * D:]) * ome_ref[...].astype(F32)
        o_ref[...] = acc.astype(o_ref.dtype)

    merged = pl.pallas_call(
        merge_fwd_kern, name="merge_fwd", grid=(nr,),
        in_specs=[rowb(3 * D, g_blk), rowb(D, 0), rowb(D, 0), rowb(D, 0)], out_specs=rowb(D, 0),
        out_shape=jax.ShapeDtypeStruct((N, D), _BF), compiler_params=_cparams(("parallel",)),
    )(proj, o_conv, o_mla, o_mem)
    y2 = _mm(merged, wo, name="mm_out", out_dtype=F32)

    def loss_kern(x_ref, y_ref, t_ref, dy_ref, dyb_ref, l_ref):
        e = x_ref[...] + y_ref[...] - t_ref[...]
        dy = e * (1.0 / D)
        dy_ref[...] = dy
        dyb_ref[...] = dy.astype(dyb_ref.dtype)

        @pl.when(pl.program_id(0) == 0)
        def _():
            l_ref[...] = jnp.zeros(l_ref.shape, F32)

        l_ref[...] += jnp.sum(e * e, axis=0, keepdims=True)

    dy, dyb, lpart = pl.pallas_call(
        loss_kern, name="loss", grid=(nr,),
        in_specs=[rowb(D, 0)] * 3, out_specs=[rowb(D, 0), rowb(D, 0), fullb((1, D))],
        out_shape=[jax.ShapeDtypeStruct((N, D), F32), jax.ShapeDtypeStruct((N, D), _BF),
                   jax.ShapeDtypeStruct((1, D), F32)],
        compiler_params=_cparams(("arbitrary",)),
    )(xs, y2, tgt)
    loss = lax.psum(jnp.sum(lpart) * (0.5 / D), AXES)

    G = {}
    d_merged = _mm(dyb, woT, name="mm_dmerged", out_dtype=_BF)
    G['w_o'] = _mm(merged, dyb, ta=True, name="mm_dwo", out_dtype=_BF)

    dproj0 = lax.empty((N, c.P), _BF)
    any_spec = pl.BlockSpec(memory_space=pl.ANY)

    def merge_bwd_kern(dp_any, g_ref, dm_ref, oc_ref, om_ref, ome_ref, dg_ref, doc_ref, dom_ref, dome_ref):
        g = g_ref[...].astype(F32)
        dm = dm_ref[...].astype(F32)
        for idx, (o_in, d_out) in enumerate(((oc_ref, doc_ref), (om_ref, dom_ref), (ome_ref, dome_ref))):
            sg = _sig(g[:, idx * D:(idx + 1) * D])
            d_out[...] = (sg * dm).astype(d_out.dtype)
            dg_ref[:, idx * D:(idx + 1) * D] = (dm * o_in[...].astype(F32) * sg * (1.0 - sg)).astype(dg_ref.dtype)

    dproj1, d_oconv, d_omla, d_omem = pl.pallas_call(
        merge_bwd_kern, name="merge_bwd", grid=(nr,),
        in_specs=[any_spec, rowb(3 * D, g_blk), rowb(D, 0), rowb(D, 0), rowb(D, 0), rowb(D, 0)],
        out_specs=[rowb(3 * D, g_blk), rowb(D, 0), rowb(D, 0), rowb(D, 0)],
        out_shape=[jax.ShapeDtypeStruct((N, c.P), _BF)] + [jax.ShapeDtypeStruct((N, D), _BF)] * 3,
        input_output_aliases={0: 0}, compiler_params=_cparams(("parallel",)),
    )(dproj0, proj, d_merged, o_conv, o_mla, o_mem)

    G['w_conv_out'] = _mm(a_conv, d_oconv, ta=True, name="mm_dwco", out_dtype=_BF)
    d_aconv = _mm(d_oconv, wcoT, name="mm_daconv", out_dtype=_BF)
    G['w_mla_out'] = _mm(a_mla, d_omla, ta=True, name="mm_dwmo", out_dtype=_BF)
    d_amla = _mm(d_omla, wmoT, name="mm_damla", out_dtype=_BF)
    G['w_mem_out'] = _mm(a_mem, d_omem, ta=True, name="mm_dwmemo", out_dtype=_BF)
    d_amem = _mm(d_omem, wmemoT, name="mm_damem", out_dtype=_BF)

    def conv_bwd_kern(dp_any, p_ref, prev_ref, next_ref, da_ref, dan_ref, w_ref, o_ref, dw_ref):
        i = pl.program_id(0)
        cg, bg, u, z = conv_parts(p_ref[...])
        pc, _, pu, _ = conv_parts(prev_ref[...])
        _, nbg, _, nz = conv_parts(next_ref[...])
        cu = cg * u
        sh1, sh2 = shifted(cu, pc * pu, i)
        w = w_ref[...]
        conv = w[0:1, :] * sh2 + w[1:2, :] * sh1 + w[2:3, :] * cu
        sg = _sig(z)
        sz = z * sg
        da = da_ref[...].astype(F32)
        dcy = da * sz
        d_z = da * (bg * conv) * (sg * (1.0 + z * (1.0 - sg)))
        d_b = dcy * conv
        dconv = dcy * bg
        dnext = dan_ref[...].astype(F32) * (nz * _sig(nz)) * nbg
        dnext = jnp.where(i < nr - 1, dnext, 0.0)
        rid = lax.broadcasted_iota(jnp.int32, cu.shape, 0)
        up1 = jnp.where(rid == R - 1, dnext[0:1, :], pltpu.roll(dconv, R - 1, 0))
        up2 = jnp.where(rid == R - 2, dnext[0:1, :], jnp.where(rid == R - 1, dnext[1:2, :], pltpu.roll(dconv, R - 2, 0)))
        dcu = w[2:3, :] * dconv + w[1:2, :] * up1 + w[0:1, :] * up2
        o_ref[:, 0:CW] = (dcu * u).astype(o_ref.dtype)
        o_ref[:, CW:2 * CW] = d_b.astype(o_ref.dtype)
        o_ref[:, 2 * CW:3 * CW] = (dcu * cg).astype(o_ref.dtype)
        o_ref[:, 3 * CW:4 * CW] = d_z.astype(o_ref.dtype)

        @pl.when(i == 0)
        def _():
            dw_ref[...] = jnp.zeros(dw_ref.shape, F32)

        dw_ref[0:1, :] += jnp.sum(dconv * sh2, axis=0, keepdims=True)
        dw_ref[1:2, :] += jnp.sum(dconv * sh1, axis=0, keepdims=True)
        dw_ref[2:3, :] += jnp.sum(dconv * cu, axis=0, keepdims=True)

    next_spec = pl.BlockSpec((HALO, 4 * CW), lambda i: (jnp.minimum((i + 1) * rh, N // HALO - 1), conv_blk))
    dan_spec = pl.BlockSpec((HALO, CW), lambda i: (jnp.minimum((i + 1) * rh, N // HALO - 1), 0))
    dproj2, g_convw = pl.pallas_call(
        conv_bwd_kern, name="conv_bwd", grid=(nr,),
        in_specs=[any_spec, rowb(4 * CW, conv_blk), prev_spec, next_spec, rowb(CW, 0), dan_spec, fullb((8, CW))],
        out_specs=[rowb(4 * CW, conv_blk), fullb((8, CW))],
        out_shape=[jax.ShapeDtypeStruct((N, c.P), _BF), jax.ShapeDtypeStruct((8, CW), F32)],
        input_output_aliases={0: 0}, compiler_params=_cparams(("arbitrary",)),
    )(dproj1, proj, proj, proj, d_aconv, d_aconv, convw8)

    def mem_bwd_kern(dp_any, qz_ref, da_ref, k_ref, v_ref, g_ref, o_ref, dk_ref, dv_ref, dg_ref):
        @pl.when(pl.program_id(0) == 0)
        def _():
            dk_ref[...] = jnp.zeros(dk_ref.shape, F32)
            dv_ref[...] = jnp.zeros(dv_ref.shape, F32)
            dg_ref[...] = jnp.zeros(dg_ref.shape, F32)

        for hh in range(MH):
            sl, qh, r, qn, p, y, z = mem_head(qz_ref, k_ref, v_ref, g_ref, hh)
            da = da_ref[:, sl].astype(F32)
            sg = _sig(z)
            dyh = da * (z * sg)
            o_ref[:, MW + hh * MHD:MW + (hh + 1) * MHD] = (da * y * (sg * (1.0 + z * (1.0 - sg)))).astype(o_ref.dtype)
            dyb_h = dyh.astype(_BF)
            dpm = _dot_nt(dyb_h, v_ref[:, sl])
            ds = (p * (dpm - jnp.sum(dpm * p, axis=1, keepdims=True)) * c.mscale).astype(_BF)
            dqn = _dot(ds, k_ref[:, sl])
            dk_ref[:, sl] += _dot_tn(ds, qn)
            dv_ref[:, sl] += _dot_tn(p.astype(_BF), dyb_h)
            dq, dgp = _rms_bwd(qh, r, g_ref[...], dqn, MHD)
            o_ref[:, sl] = dq.astype(o_ref.dtype)
            dg_ref[...] += jnp.sum(dgp, axis=0, keepdims=True)

    dproj3, d_memk, d_memv, g_mem_qn = pl.pallas_call(
        mem_bwd_kern, name="mem_bwd", grid=(nr,),
        in_specs=[any_spec, rowb(2 * MW, mem_blk), rowb(MW, 0), fullb((M, MW)), fullb((M, MW)), fullb((1, MHD))],
        out_specs=[rowb(2 * MW, mem_blk), fullb((M, MW)), fullb((M, MW)), fullb((1, MHD))],
        out_shape=[jax.ShapeDtypeStruct((N, c.P), _BF), jax.ShapeDtypeStruct((M, MW), F32),
                   jax.ShapeDtypeStruct((M, MW), F32), jax.ShapeDtypeStruct((1, MHD), F32)],
        input_output_aliases={0: 0}, compiler_params=_cparams(("arbitrary",)),
    )(dproj2, proj, d_amem, mem_k, mem_v, mem_qn_g)

    def memk_bwd_kern(kv_ref, dk_ref, dv_ref, g_ref, o_ref, dg_ref):
        dg = jnp.zeros((1, MHD), F32)
        for hh in range(MH):
            sl = slice(hh * MHD, (hh + 1) * MHD)
            kh, r = _rms(kv_ref[:, sl], MHD)
            dkr, dgp = _rms_bwd(kh, r, g_ref[...], dk_ref[:, sl], MHD)
            o_ref[:, sl] = dkr.astype(o_ref.dtype)
            dg += jnp.sum(dgp, axis=0, keepdims=True)
        o_ref[:, MW:] = dv_ref[...].astype(o_ref.dtype)
        dg_ref[...] = dg

    d_kvm, g_mem_kn = pl.pallas_call(
        memk_bwd_kern, name="memk_bwd", grid=(1,),
        in_specs=[fullb((M, 2 * MW)), fullb((M, MW)), fullb((M, MW)), fullb((1, MHD))],
        out_specs=[fullb((M, 2 * MW)), fullb((1, MHD))],
        out_shape=[jax.ShapeDtypeStruct((M, 2 * MW), _BF), jax.ShapeDtypeStruct((1, MHD), F32)],
        compiler_params=_cparams(("arbitrary",)),
    )(kvm, d_memk, d_memv, mem_kn_g)
    G['w_mem_kv'] = _mm(memn, d_kvm, ta=True, name="mm_dwmkv", out_dtype=_BF)
    d_memn = _mm(d_kvm, wmkvT, name="mm_dmemn", out_dtype=F32)

    def memnorm_bwd_kern(x_ref, d_ref, dg_ref):
        xh, _ = _rms(x_ref[...], D)
        dg_ref[...] = jnp.sum(d_ref[...] * xh, axis=0, keepdims=True)

    g_mem_norm = pl.pallas_call(
        memnorm_bwd_kern, name="memnorm_bwd", grid=(1,),
        in_specs=[fullb((M, D)), fullb((M, D))], out_specs=fullb((1, D)),
        out_shape=jax.ShapeDtypeStruct((1, D), F32), compiler_params=_cparams(("arbitrary",)),
    )(memx, d_memn)

    def gate_bwd_kern(dp_any, da_ref, y_ref, z_ref, dy_ref, dz_ref):
        z = z_ref[...].astype(F32)
        da = da_ref[...].astype(F32)
        sg = _sig(z)
        dy_ref[...] = (da * (z * sg)).astype(dy_ref.dtype)
        dz_ref[...] = (da * y_ref[...].astype(F32) * (sg * (1.0 + z * (1.0 - sg)))).astype(dz_ref.dtype)

    d_mlay, dproj4 = pl.pallas_call(
        gate_bwd_kern, name="gate_mla_bwd", grid=(nr,),
        in_specs=[any_spec, rowb(HV, 0), rowb(HV, 0), rowb(HV, mz_blk)],
        out_specs=[rowb(HV, 0), rowb(HV, mz_blk)],
        out_shape=[jax.ShapeDtypeStruct((N, HV), _BF), jax.ShapeDtypeStruct((N, c.P), _BF)],
        input_output_aliases={0: 1}, compiler_params=_cparams(("parallel",)),
    )(dproj3, d_amla, mla_y, proj)

    def attn_bwd_kern(q_ref, k_ref, v_ref, o_ref, do_ref, lse_ref, dq_ref, dk_ref, dv_ref, dq_sc, dl_sc):
        i = pl.program_id(1)
        q, do = q_ref[0], do_ref[...]
        delta = jnp.sum(do.astype(F32) * o_ref[...].astype(F32), axis=1, keepdims=True)
        dl_sc[...] = jnp.broadcast_to(delta, dl_sc.shape)
        dq_sc[...] = jnp.zeros(dq_sc.shape, F32)

        def step(off, masked):
            rows = pl.ds(off, B)
            k = k_ref[0, rows, :]
            s = _dot_nt(q, k)
            if masked:
                s = diag_mask(s)
            dpm = _dot_nt(do, v_ref[rows, :])
            lse_t, dl = lse_ref[0], dl_sc[...]
            ps, dss = [], []
            for cb in range(B // LANES):
                cols = slice(cb * LANES, (cb + 1) * LANES)
                p_cb = jnp.exp2(s[:, cols] - lse_t)
                ps.append(p_cb.astype(_BF))
                dss.append((p_cb * (dpm[:, cols] - dl)).astype(_BF))
            p, ds = jnp.concatenate(ps, axis=1), jnp.concatenate(dss, axis=1)
            dvp = _dot_tn(p, do)
            dkp = _dot_tn(ds, q)
            if masked:
                dk_ref[0, rows, :] = dkp
                dv_ref[0, rows, :] = dvp
            else:
                dk_ref[0, rows, :] += dkp
                dv_ref[0, rows, :] += dvp
            dq_sc[...] += _dot(ds, k)

        def pair(t, carry):
            step(pl.multiple_of(t * 2 * B, B), False)
            step(pl.multiple_of(t * 2 * B + B, B), False)
            return carry

        lax.fori_loop(0, i // 2, pair, 0)

        @pl.when(i % 2 == 1)
        def _():
            step(pl.multiple_of((i - 1) * B, B), False)

        step(pl.multiple_of(i * B, B), True)
        dq_ref[0] = dq_sc[...]

    d_qcat, d_kcat, d_v = pl.pallas_call(
        attn_bwd_kern, name="attn_bwd", grid=(H, nb),
        in_specs=[q_blk, k_head, v_head, o_blk, o_blk, lse_blk],
        out_specs=[pl.BlockSpec((1, B, HW), lambda hh, i: (hh, i, 0)),
                   pl.BlockSpec((1, N, HW), lambda hh, i: (hh, 0, 0)),
                   pl.BlockSpec((1, N, LANES), lambda hh, i: (hh, 0, 0))],
        out_shape=[jax.ShapeDtypeStruct((H, N, HW), F32), jax.ShapeDtypeStruct((H, N, HW), F32),
                   jax.ShapeDtypeStruct((H, N, LANES), F32)],
        scratch_shapes=[pltpu.VMEM((B, HW), F32), pltpu.VMEM((B, LANES), F32)],
        compiler_params=_cparams(("parallel", "arbitrary")),
    )(q_cat, k_cat, kv, mla_y, d_mlay, lse)

    def q_prep_bwd_kern(q_ref, dq_ref, t_ref, gn_ref, gr_ref, o_ref, dgn_ref, dgr_ref):
        @pl.when(jnp.logical_and(pl.program_id(0) == 0, pl.program_id(1) == 0))
        def _():
            dgn_ref[...] = jnp.zeros(dgn_ref.shape, F32)
            dgr_ref[...] = jnp.zeros(dgr_ref.shape, F32)

        for g in range(HG):
            blk = q_ref[:, g * HW:(g + 1) * HW].astype(F32)
            d = dq_ref[g] * c.scale
            nh, rn = _rms(blk[:, :LANES], c.NOPE)
            rhat, rr = _rms(blk[:, LANES:], c.ROPE)
            dn, dgn = _rms_bwd(nh, rn, gn_ref[...], d[:, :LANES], c.NOPE)
            drot = _rope_t(d[:, LANES:], t_ref[0], t_ref[1], t_ref[2])
            dr, dgr = _rms_bwd(rhat, rr, gr_ref[...], drot, c.ROPE)
            o_ref[:, g * HW:(g + 1) * HW] = jnp.concatenate([dn, dr], axis=1).astype(o_ref.dtype)
            dgn_ref[...] += jnp.sum(dgn, axis=0, keepdims=True)
            dgr_ref[...] += jnp.sum(dgr, axis=0, keepdims=True)

    d_qp, g_qn_nope, g_qn_rope = pl.pallas_call(
        q_prep_bwd_kern, name="q_prep_bwd", grid=(nrp, nhg),
        in_specs=[heads_in, heads_out, tabs_of(RP), fullb((1, LANES)), fullb((1, LANES))],
        out_specs=[heads_in, fullb((1, LANES)), fullb((1, LANES))],
        out_shape=[jax.ShapeDtypeStruct((N, H * HW), _BF), jax.ShapeDtypeStruct((1, LANES), F32),
                   jax.ShapeDtypeStruct((1, LANES), F32)],
        compiler_params=_cparams(("arbitrary", "arbitrary")),
    )(q_p, d_qcat, tabs, g_qn, g_qr)

    def k_prep_bwd_kern(kv_ref, dk_ref, dv_ref, gn_ref, o_ref, dkr_ref, dgn_ref):
        hg = pl.program_id(1)

        @pl.when(jnp.logical_and(pl.program_id(0) == 0, hg == 0))
        def _():
            dgn_ref[...] = jnp.zeros(dgn_ref.shape, F32)

        @pl.when(hg == 0)
        def _():
            dkr_ref[...] = jnp.zeros(dkr_ref.shape, F32)

        dkr = jnp.zeros((RP, LANES), F32)
        for g in range(HG):
            dk = dk_ref[g] * (1.0 / LOG2E)
            kn, r = _rms(kv_ref[:, g * HW:g * HW + LANES].astype(F32), c.NOPE)
            dkn, dgn = _rms_bwd(kn, r, gn_ref[...], dk[:, :LANES], c.NOPE)
            o_ref[:, g * HW:(g + 1) * HW] = jnp.concatenate([dkn, dv_ref[g]], axis=1).astype(o_ref.dtype)
            dgn_ref[...] += jnp.sum(dgn, axis=0, keepdims=True)
            dkr += dk[:, LANES:]
        dkr_ref[...] += dkr

    d_kv, d_krsum, g_kn_nope = pl.pallas_call(
        k_prep_bwd_kern, name="k_prep_bwd", grid=(nrp, nhg),
        in_specs=[heads_in, heads_out, pl.BlockSpec((HG, RP, LANES), lambda i, hg: (hg, i, 0)), fullb((1, LANES))],
        out_specs=[heads_in, pl.BlockSpec((RP, LANES), lambda i, hg: (i, 0)), fullb((1, LANES))],
        out_shape=[jax.ShapeDtypeStruct((N, H * HW), _BF), jax.ShapeDtypeStruct((N, LANES), F32),
                   jax.ShapeDtypeStruct((1, LANES), F32)],
        compiler_params=_cparams(("arbitrary", "arbitrary")),
    )(kv, d_kcat, d_v, g_kn)

    def krope_bwd_kern(dp_any, p_ref, d_ref, t_ref, g_ref, o_ref, dg_ref):
        @pl.when(pl.program_id(0) == 0)
        def _():
            dg_ref[...] = jnp.zeros(dg_ref.shape, F32)

        xh, r = _rms(p_ref[...].astype(F32), c.ROPE)
        drot = _rope_t(d_ref[...], t_ref[0], t_ref[1], t_ref[2])
        dx, dg = _rms_bwd(xh, r, g_ref[...], drot, c.ROPE)
        o_ref[...] = dx.astype(o_ref.dtype)
        dg_ref[...] += jnp.sum(dg, axis=0, keepdims=True)

    dproj5, g_kn_rope = pl.pallas_call(
        krope_bwd_kern, name="krope_bwd", grid=(nr,),
        in_specs=[any_spec, rowb(LANES, kr_blk), rowb(LANES, 0), tabs_of(R), fullb((1, LANES))],
        out_specs=[rowb(LANES, kr_blk), fullb((1, LANES))],
        out_shape=[jax.ShapeDtypeStruct((N, c.P), _BF), jax.ShapeDtypeStruct((1, LANES), F32)],
        input_output_aliases={0: 0}, compiler_params=_cparams(("arbitrary",)),
    )(dproj4, proj, d_krsum, tabs, g_kr)

    g_wuq_p = _mm(cqn, d_qp, ta=True, name="mm_dwuq", out_dtype=_BF)
    G['w_uq'] = g_wuq_p.reshape(QL, H, HW)[:, :, :c.NOPE + c.ROPE].reshape(QL, H * (c.NOPE + c.ROPE))
    d_cqn = _mm(d_qp, wuq_pT, name="mm_dcqn", out_dtype=F32)
    G['w_ukv'] = _mm(ckvn, d_kv, ta=True, name="mm_dwukv", out_dtype=_BF)
    d_ckvn = _mm(d_kv, wukvT, name="mm_dckvn", out_dtype=F32)

    def to_blocks(n, g):
        if n in COL_SHARDED:
            return jnp.transpose(g.reshape(g.shape[0], NDEV, -1), (1, 0, 2))
        return g.reshape(NDEV, -1, g.shape[1])

    def landing(b):
        own = lax.dynamic_index_in_dim(b, me, 0, keepdims=True)
        return lax.dynamic_update_index_in_dim(lax.empty(b.shape, b.dtype), own, me, 0)

    early = [n for n in BIG if n != 'w_in']
    blocks_e = [to_blocks(n, G[n]) for n in early]
    xe = _exchange_start(blocks_e, [landing(b) for b in blocks_e], "xchg_early_start")
    gq_after = mla_q_norm_g + xe[4][0:1, 0:1]

    def lora_bwd_kern(dp_any, p_ref, dq_ref, dkv_ref, gq_ref, gkv_ref, o_ref, dgq_ref, dgkv_ref):
        @pl.when(pl.program_id(0) == 0)
        def _():
            dgq_ref[...] = jnp.zeros(dgq_ref.shape, F32)
            dgkv_ref[...] = jnp.zeros(dgkv_ref.shape, F32)

        blk = p_ref[...].astype(F32)
        qh, rq = _rms(blk[:, :QL], QL)
        kh, rk = _rms(blk[:, QL:], KVL)
        dq, dgq = _rms_bwd(qh, rq, gq_ref[...], dq_ref[...], QL)
        dk, dgk = _rms_bwd(kh, rk, gkv_ref[...], dkv_ref[...], KVL)
        o_ref[:, :QL] = dq.astype(o_ref.dtype)
        o_ref[:, QL:] = dk.astype(o_ref.dtype)
        dgq_ref[...] += jnp.sum(dgq, axis=0, keepdims=True)
        dgkv_ref[...] += jnp.sum(dgk, axis=0, keepdims=True)

    dproj6, g_q_norm, g_kv_norm = pl.pallas_call(
        lora_bwd_kern, name="lora_bwd", grid=(nr,),
        in_specs=[any_spec, rowb(QL + KVL, lora_blk), rowb(QL, 0), rowb(KVL, 0), fullb((1, QL)), fullb((1, KVL))],
        out_specs=[rowb(QL + KVL, lora_blk), fullb((1, QL)), fullb((1, KVL))],
        out_shape=[jax.ShapeDtypeStruct((N, c.P), _BF), jax.ShapeDtypeStruct((1, QL), F32),
                   jax.ShapeDtypeStruct((1, KVL), F32)],
        input_output_aliases={0: 0}, compiler_params=_cparams(("arbitrary",)),
    )(dproj5, proj, d_cqn, d_ckvn, gq_after, mla_kv_norm_g)

    g_win_p = _mm(h, dproj6, ta=True, name="mm_dwin", out_dtype=_BF, bn=1408)
    G['w_in'] = _win_from_padded(g_win_p, c)
    blocks_w = [to_blocks('w_in', G['w_in'])]
    xw = _exchange_start(blocks_w, [landing(b) for b in blocks_w], "xchg_win_start")
    d_h = _mm(dproj6, win_pT, name="mm_dh", out_dtype=F32, bk=1408, after=xw[4])

    def final_bwd_kern(x_ref, g_ref, dh_ref, dy_ref, gx_ref, dg_ref):
        @pl.when(pl.program_id(0) == 0)
        def _():
            dg_ref[...] = jnp.zeros(dg_ref.shape, F32)

        xh, r = _rms(x_ref[...], D)
        dx, dg = _rms_bwd(xh, r, g_ref[...], dh_ref[...], D)
        gx_ref[...] = dy_ref[...] + dx
        dg_ref[...] += jnp.sum(dg, axis=0, keepdims=True)

    grad_x, g_norm = pl.pallas_call(
        final_bwd_kern, name="final_bwd", grid=(nr,),
        in_specs=[rowb(D, 0), fullb((1, D)), rowb(D, 0), rowb(D, 0)],
        out_specs=[rowb(D, 0), fullb((1, D))],
        out_shape=[jax.ShapeDtypeStruct((N, D), F32), jax.ShapeDtypeStruct((1, D), F32)],
        compiler_params=_cparams(("arbitrary",)),
    )(xs, norm_g, d_h, dy)

    recv_e = _exchange_wait(xe[0], xe[1], xe[2], xe[3], grad_x, "xchg_early_wait")
    recv_w = _exchange_wait(xw[0], xw[1], xw[2], xw[3], grad_x, "xchg_win_wait")
    res = [{}, {}, {}, {}]
    for n, parts in zip(['w_in'] + early, list(recv_w) + list(recv_e)):
        outs = _adam(parts, W[n][0], Mo[n][0], Vo[n][0], "adam_" + n)
        for k in range(4):
            res[k][n] = outs[k][None]

    small_g = {'norm_g': g_norm, 'mla_q_norm_g': g_q_norm, 'mla_kv_norm_g': g_kv_norm,
               'mla_qn_nope_g': g_qn_nope, 'mla_qn_rope_g': g_qn_rope[:, :c.ROPE], 'mla_kn_nope_g': g_kn_nope,
               'mla_kn_rope_g': g_kn_rope[:, :c.ROPE], 'mem_norm_g': g_mem_norm, 'mem_qn_g': g_mem_qn,
               'mem_kn_g': g_mem_kn}
    small_part = _pack([small_g[n] for n in SMALL] + [g_convw[0:3, :]], 0)
    small_all = _all_gather([small_part], "ag_small_grads")[0]
    small_shapes = [W[n].shape for n in SMALL]
    pieces = _unpack(small_all, small_shapes + [(3, CW)])
    cw8 = CW // NDEV
    conv_mine = lax.dynamic_slice_in_dim(pieces[-1].reshape(NDEV, 3, NDEV, cw8), me, 1, axis=2)[:, :, 0, :]
    sm_parts = _pack(pieces[:-1] + [conv_mine], 1)
    sm_names = SMALL + ['conv_w']
    sm_shapes = small_shapes + [(3, cw8)]
    w_sm = _pack([W[n] for n in SMALL] + [conv_w[0]], 0)
    m_sm = _pack([Mo[n] for n in SMALL] + [m_conv_w[0]], 0)
    v_sm = _pack([Vo[n] for n in SMALL] + [v_conv_w[0]], 0)
    outs_sm = [_unpack(o, sm_shapes) for o in _adam(sm_parts, w_sm, m_sm, v_sm, "adam_small")]
    for k in range(4):
        for n, a in zip(sm_names, outs_sm[k]):
            res[k][n] = a[None] if n == 'conv_w' else a
    return (loss, grad_x[None], *[res[0][n] for n in WEIGHTS], *[res[1][n] for n in WEIGHTS],
            *[res[2][n] for n in WEIGHTS], *[res[3][n] for n in WEIGHTS])
```

```python
import math

import jax
import jax.numpy as jnp
from jax import lax
from jax.experimental import pallas as pl
from jax.experimental.pallas import tpu as pltpu

F32 = jnp.float32
_BF = jnp.bfloat16
EPS = 1e-6
CHUNK = 64
ROPE_THETA = 10000.0
ADAM_LR, ADAM_B1, ADAM_B2, ADAM_EPS, ADAM_WD, ADAM_STEP = 0.001, 0.9, 0.999, 1e-08, 0.01, 10
NDEV = 8
AXES = ("x", "y", "c")
MESH = pl.DeviceIdType.MESH
LANES = 128
NEG = -1e30
LOG2E = math.log2(math.e)
V7X_VMEM_LIMIT = 56 * 1024 * 1024
PACK_C = 1024
ATT_BLOCK = 512
ATT_UNROLL = 4
ADAM_BLOCK_ELEMS = 256 * 1024

WEIGHTS = ['norm_g', 'w_in', 'conv_w', 'w_conv_out', 'mla_q_norm_g', 'w_uq', 'mla_kv_norm_g', 'w_ukv',
           'mla_qn_nope_g', 'mla_qn_rope_g', 'mla_kn_nope_g', 'mla_kn_rope_g', 'w_mla_out', 'mem_norm_g',
           'w_mem_kv', 'mem_qn_g', 'mem_kn_g', 'w_mem_out', 'w_o']
BIG = ['w_in', 'w_conv_out', 'w_uq', 'w_ukv', 'w_mla_out', 'w_mem_kv', 'w_mem_out', 'w_o']
COL_SHARDED = ('w_in', 'w_conv_out', 'w_uq', 'w_ukv', 'w_mem_out')
SMALL = ['norm_g', 'mla_q_norm_g', 'mla_kv_norm_g', 'mla_qn_nope_g', 'mla_qn_rope_g', 'mla_kn_nope_g',
         'mla_kn_rope_g', 'mem_norm_g', 'mem_qn_g', 'mem_kn_g']


def _tile(dim, target, align):
    if dim <= target:
        return dim
    t = target - target % align
    while t > 0:
        if dim % t == 0:
            return t
        t -= align
    raise ValueError(f"no tile for {dim} {target} {align}")


def _cparams(sem):
    return pltpu.CompilerParams(dimension_semantics=sem, vmem_limit_bytes=V7X_VMEM_LIMIT)


def _sig(x):
    return 1.0 / (1.0 + jnp.exp(-x))


def _rms(x, n):
    r = lax.rsqrt(jnp.sum(x * x, axis=-1, keepdims=True) * (1.0 / n) + EPS)
    return x * r, r


def _rms_bwd(xhat, r, g, dy, n):
    dxh = dy * g
    dx = r * (dxh - xhat * (jnp.sum(dxh * xhat, axis=-1, keepdims=True) * (1.0 / n)))
    return dx, dy * xhat


def _rope(x, cosp, sina, sinb):
    return x * cosp + pltpu.roll(x, 96, 1) * sina + pltpu.roll(x, 32, 1) * sinb


def _rope_t(d, cosp, sina, sinb):
    return d * cosp + pltpu.roll(d * sina, 32, 1) + pltpu.roll(d * sinb, 96, 1)


def _dot_nt(a, b):
    return lax.dot_general(a, b, (((1,), (1,)), ((), ())), preferred_element_type=F32)


def _dot_tn(a, b):
    return lax.dot_general(a, b, (((0,), (0,)), ((), ())), preferred_element_type=F32)


def _dot(a, b):
    return jnp.dot(a, b, preferred_element_type=F32)


def _all_gather(shards, name):
    na = len(shards)

    def body(*refs):
        x_refs, out_refs = refs[:na], refs[na:2 * na]
        send_sems, recv_sems, local_sems = refs[2 * na:]
        x, y, c = lax.axis_index("x"), lax.axis_index("y"), lax.axis_index("c")
        me, sibling = (x, y, c), (x, y, 1 - c)
        chips = [(1 - x, y), (x, 1 - y), (1 - x, 1 - y)]

        def rows(a, px, py, pc):
            return out_refs[a].at[4 * px + 2 * py + pc]

        def copy(a, k, block, to, src=None):
            return pltpu.make_async_remote_copy(
                src_ref=rows(a, *block) if src is None else src, dst_ref=rows(a, *block),
                send_sem=send_sems.at[7 * a + k], recv_sem=recv_sems.at[7 * a + k],
                device_id=to, device_id_type=MESH)

        mine = [pltpu.make_async_copy(x_refs[a], rows(a, *me), local_sems.at[a]) for a in range(na)]
        for cp in mine:
            cp.start()
        first = [copy(a, 0, me, sibling, src=x_refs[a]) for a in range(na)]
        first += [copy(a, 1 + j, me, (*chip, c), src=x_refs[a]) for j, chip in enumerate(chips) for a in range(na)]
        for cp in first:
            cp.start()
        passed = []
        for j, chip in enumerate(chips):
            for a in range(na):
                copy(a, 1 + j, (*chip, c), me).wait_recv()
                fwd = copy(a, 4 + j, (*chip, c), sibling)
                fwd.start()
                passed.append(fwd)
        for a in range(na):
            copy(a, 0, sibling, me).wait_recv()
        for j, chip in enumerate(chips):
            for a in range(na):
                copy(a, 4 + j, (*chip, 1 - c), me).wait_recv()
        for cp in first + passed:
            cp.wait_send()
        for cp in mine:
            cp.wait()

    any_spec = pl.BlockSpec(memory_space=pl.ANY)
    return pl.pallas_call(
        body, name=name,
        out_shape=[jax.ShapeDtypeStruct((NDEV,) + s.shape, s.dtype) for s in shards],
        in_specs=[any_spec] * na, out_specs=[any_spec] * na,
        scratch_shapes=[pltpu.SemaphoreType.DMA((7 * na,)), pltpu.SemaphoreType.DMA((7 * na,)),
                        pltpu.SemaphoreType.DMA((na,))],
    )(*shards)


_HBM = pl.BlockSpec(memory_space=pltpu.HBM)
_SEM = pl.BlockSpec(memory_space=pltpu.SEMAPHORE)
_EFFECT = pltpu.SideEffectType.DATAFLOW_SIDE_EFFECTING


def _split_copy(a, k, src_refs, land_refs, send_sems, recv_sems, gather, receive_side):
    x, y, c = lax.axis_index("x"), lax.axis_index("y"), lax.axis_index("c")
    me = 4 * x + 2 * y + c
    tx, ty, tc = x ^ ((k + 1) >> 2 & 1), y ^ ((k + 1) >> 1 & 1), c ^ ((k + 1) & 1)
    peer = 4 * tx + 2 * ty + tc
    return pltpu.make_async_remote_copy(
        src_ref=src_refs[a] if gather else src_refs[a].at[peer],
        dst_ref=land_refs[a].at[peer if receive_side else me],
        send_sem=send_sems.at[7 * a + k], recv_sem=recv_sems.at[7 * a + k],
        device_id=(tx, ty, tc), device_id_type=MESH)


def _split_start(srcs, lands, gather, name, after=None):
    na = len(srcs)
    extra = [] if after is None else [after]

    def body(*refs):
        src_refs, land_refs = refs[:na], refs[na:2 * na]
        send_sems, recv_sems = refs[2 * na + len(extra)], refs[2 * na + len(extra) + 1]
        token = refs[-1]
        for k in range(7):
            for a in range(na):
                _split_copy(a, k, src_refs, land_refs, send_sems, recv_sems, gather, False).start()
        token[...] = jnp.zeros_like(token)

    hbm = [pltpu.HBM(b.shape, b.dtype) for b in list(srcs) + list(lands)]
    outs = pl.pallas_call(
        body, name=name,
        out_shape=(pltpu.SemaphoreType.DMA((7 * na,)), pltpu.SemaphoreType.DMA((7 * na,)), *hbm,
                   jax.ShapeDtypeStruct((8, LANES), F32)),
        in_specs=[_HBM] * (2 * na) + [pl.BlockSpec(memory_space=pl.ANY)] * len(extra),
        out_specs=(_SEM, _SEM, *[_HBM] * (2 * na), pl.BlockSpec(memory_space=pltpu.VMEM)),
        input_output_aliases={j: 2 + j for j in range(2 * na)},
        compiler_params=pltpu.CompilerParams(has_side_effects=_EFFECT),
    )(*[pltpu.with_memory_space_constraint(b, pltpu.HBM) for b in srcs],
      *[pltpu.with_memory_space_constraint(l, pltpu.HBM) for l in lands], *extra)
    return outs[0], outs[1], outs[2:2 + na], outs[2 + na:2 + 2 * na], outs[-1]


def _split_wait(started, gather, after, name):
    send_sems, recv_sems, srcs, lands, _ = started
    na = len(srcs)

    def body(*refs):
        src_refs, land_refs = refs[:na], refs[na:2 * na]
        send_s, recv_s = refs[2 * na], refs[2 * na + 1]
        for k in range(7):
            for a in range(na):
                cp = _split_copy(a, k, src_refs, land_refs, send_s, recv_s, gather, True)
                cp.wait_send()
                cp.wait_recv()

    hbm = [pltpu.HBM(b.shape, b.dtype) for b in list(srcs) + list(lands)]
    outs = pl.pallas_call(
        body, name=name, out_shape=tuple(hbm),
        in_specs=[_HBM] * (2 * na) + [_SEM, _SEM, pl.BlockSpec(memory_space=pl.ANY)],
        out_specs=tuple([_HBM] * (2 * na)),
        input_output_aliases={j: j for j in range(2 * na)},
        compiler_params=pltpu.CompilerParams(has_side_effects=_EFFECT),
    )(*srcs, *lands, send_sems, recv_sems, after)
    return outs[na:]


def _seg_rows(size):
    rows = -(-size // PACK_C)
    return -(-rows // 16) * 16


def _pack(arrs, lead):
    parts = []
    for a in arrs:
        lshape = a.shape[:lead]
        f = a.reshape(lshape + (-1,)).astype(F32)
        rows = _seg_rows(f.shape[-1])
        f = jnp.pad(f, [(0, 0)] * lead + [(0, rows * PACK_C - f.shape[-1])])
        parts.append(f.reshape(lshape + (rows, PACK_C)))
    return jnp.concatenate(parts, axis=lead)


def _unpack(buf, shapes):
    lshape = buf.shape[:-2]
    out, r = [], 0
    for shp in shapes:
        size = math.prod(shp)
        rows = _seg_rows(size)
        seg = buf[..., r:r + rows, :].reshape(lshape + (rows * PACK_C,))[..., :size]
        out.append(seg.reshape(lshape + tuple(shp)))
        r += rows
    return out


def _mm(a, b, *, name, out_dtype, ta=False, bm=1024, bn=1024, bk=2048, after=None):
    if ta:
        kdim, m = a.shape
    else:
        m, kdim = a.shape
    k2, n = b.shape
    assert kdim == k2, (a.shape, b.shape)
    bm = _tile(m, bm, LANES if ta else 16)
    bn = _tile(n, bn, LANES)
    bk = _tile(kdim, bk, LANES)
    nk = kdim // bk

    def kern(a_ref, b_ref, *rest):
        o_ref, scratch = (rest[1], rest[2:]) if after is not None else (rest[0], rest[1:])
        part = _dot_tn(a_ref[...], b_ref[...]) if ta else _dot(a_ref[...], b_ref[...])
        if nk == 1:
            o_ref[...] = part.astype(o_ref.dtype)
        else:
            acc = scratch[0]
            k = pl.program_id(2)

            @pl.when(k == 0)
            def _():
                acc[...] = part

            @pl.when(jnp.logical_and(k > 0, k < nk - 1))
            def _():
                acc[...] += part

            @pl.when(k == nk - 1)
            def _():
                o_ref[...] = (acc[...] + part).astype(o_ref.dtype)

    a_spec = pl.BlockSpec((bk, bm), lambda i, j, k: (k, i)) if ta else pl.BlockSpec((bm, bk), lambda i, j, k: (i, k))
    extra_specs = [pl.BlockSpec(after.shape, lambda i, j, k: (0, 0))] if after is not None else []
    extra_args = [after] if after is not None else []
    return pl.pallas_call(
        kern, name=name, grid=(m // bm, n // bn, nk),
        in_specs=[a_spec, pl.BlockSpec((bk, bn), lambda i, j, k: (k, j))] + extra_specs,
        out_specs=pl.BlockSpec((bm, bn), lambda i, j, k: (i, j)),
        out_shape=jax.ShapeDtypeStruct((m, n), out_dtype),
        scratch_shapes=[pltpu.VMEM((bm, bn), F32)] if nk > 1 else [],
        compiler_params=_cparams(("parallel", "parallel", "arbitrary")),
    )(a, b, *extra_args)


def _adam(parts, w_a, m_a, v_a, name):
    rows, cols = w_a.shape
    rb = _tile(rows, max(8, ADAM_BLOCK_ELEMS // cols // 8 * 8), 8)
    bc1 = 1.0 - ADAM_B1 ** ADAM_STEP
    bc2 = 1.0 - ADAM_B2 ** ADAM_STEP

    def adam_kern(p_ref, w_ref, m_ref, v_ref, g_ref, d_ref, nm_ref, nv_ref):
        g = p_ref[0].astype(F32)
        for j in range(1, NDEV):
            g = g + p_ref[j].astype(F32)
        m_new = ADAM_B1 * m_ref[...] + (1.0 - ADAM_B1) * g
        v_new = ADAM_B2 * v_ref[...] + (1.0 - ADAM_B2) * (g * g)
        g_ref[...] = g
        nm_ref[...] = m_new
        nv_ref[...] = v_new
        d_ref[...] = -ADAM_LR * ((m_new / bc1) / (jnp.sqrt(v_new / bc2) + ADAM_EPS) + ADAM_WD * w_ref[...])

    blk = pl.BlockSpec((rb, cols), lambda i: (i, 0))
    return pl.pallas_call(
        adam_kern, name=name, grid=(rows // rb,),
        in_specs=[pl.BlockSpec((NDEV, rb, cols), lambda i: (0, i, 0)), blk, blk, blk],
        out_specs=[blk] * 4, out_shape=[jax.ShapeDtypeStruct((rows, cols), F32)] * 4,
        compiler_params=_cparams(("parallel",)),
    )(parts, w_a, m_a, v_a)


class _Cfg:
    pass


def _config(x, conv_w, w_uq, w_ukv, mla_qn_nope_g, mla_qn_rope_g, mem, mem_qn_g, w_mem_out, w_mla_out):
    c = _Cfg()
    c.N, c.D = x.shape[1], x.shape[2]
    c.CW = conv_w.shape[2] * NDEV
    c.QL, c.KVL = w_uq.shape[1], w_ukv.shape[1]
    c.NOPE, c.ROPE = mla_qn_nope_g.shape[1], mla_qn_rope_g.shape[1]
    c.H = w_uq.shape[2] * NDEV // (c.NOPE + c.ROPE)
    c.V = w_ukv.shape[2] * NDEV // c.H - c.NOPE
    assert c.NOPE == LANES and c.V == LANES and c.ROPE == LANES // 2
    c.HW = 2 * LANES
    c.HV = c.H * c.V
    assert w_mla_out.shape[1] * NDEV == c.HV
    c.M = mem.shape[1]
    c.MHD = mem_qn_g.shape[1]
    c.MW = w_mem_out.shape[1]
    c.MH = c.MW // c.MHD
    c.o_conv = 0
    c.o_mz = 4 * c.CW
    c.o_g = c.o_mz + c.HV
    c.o_mem = c.o_g + 3 * c.D
    c.o_lora = c.o_mem + 2 * c.MW
    c.o_kr = c.o_lora + c.QL + c.KVL
    c.P = c.o_kr + LANES
    assert c.o_mz % c.HV == 0 and c.o_g % (3 * c.D) == 0 and c.o_mem % (2 * c.MW) == 0
    assert c.o_lora % (c.QL + c.KVL) == 0 and c.QL % LANES == 0 and c.KVL % LANES == 0
    c.IN = 4 * c.CW + c.QL + c.KVL + c.ROPE + c.HV + 2 * c.MW + 3 * c.D
    c.R = _tile(c.N, 256, 16)
    c.RP = _tile(c.N, 512, 16)
    c.HG = _tile(c.H, 4, 1)
    c.B = _tile(c.N, ATT_BLOCK, CHUNK)
    c.scale = float((c.NOPE + c.ROPE) ** -0.5)
    c.mscale = float(c.MHD ** -0.5)
    return c


def _win_to_padded(w, c):
    o = 0
    parts = {}
    for nm, wd in (('conv', 4 * c.CW), ('lora', c.QL + c.KVL), ('kr', c.ROPE), ('mz', c.HV), ('mem', 2 * c.MW), ('g', 3 * c.D)):
        parts[nm] = w[:, o:o + wd]
        o += wd
    kr = jnp.pad(parts['kr'], ((0, 0), (0, LANES - c.ROPE)))
    return jnp.concatenate([parts['conv'], parts['mz'], parts['g'], parts['mem'], parts['lora'], kr], axis=1)


def _win_from_padded(g, c):
    conv = g[:, c.o_conv:c.o_mz]
    mz = g[:, c.o_mz:c.o_g]
    gates = g[:, c.o_g:c.o_mem]
    mem = g[:, c.o_mem:c.o_lora]
    lora = g[:, c.o_lora:c.o_kr]
    kr = g[:, c.o_kr:c.o_kr + c.ROPE]
    return jnp.concatenate([conv, lora, kr, mz, mem, gates], axis=1)


def kernel(x, positions, mem, norm_g, w_in, conv_w, w_conv_out, mla_q_norm_g, w_uq, mla_kv_norm_g, w_ukv, mla_qn_nope_g, mla_qn_rope_g, mla_kn_nope_g, mla_kn_rope_g, w_mla_out, mem_norm_g, w_mem_kv, mem_qn_g, mem_kn_g, w_mem_out, w_o, loss_target, m_norm_g, m_w_in, m_conv_w, m_w_conv_out, m_mla_q_norm_g, m_w_uq, m_mla_kv_norm_g, m_w_ukv, m_mla_qn_nope_g, m_mla_qn_rope_g, m_mla_kn_nope_g, m_mla_kn_rope_g, m_w_mla_out, m_mem_norm_g, m_w_mem_kv, m_mem_qn_g, m_mem_kn_g, m_w_mem_out, m_w_o, v_norm_g, v_w_in, v_conv_w, v_w_conv_out, v_mla_q_norm_g, v_w_uq, v_mla_kv_norm_g, v_w_ukv, v_mla_qn_nope_g, v_mla_qn_rope_g, v_mla_kn_nope_g, v_mla_kn_rope_g, v_w_mla_out, v_mem_norm_g, v_w_mem_kv, v_mem_qn_g, v_mem_kn_g, v_w_mem_out, v_w_o):
    args = dict(locals())
    W = {n: args[n] for n in WEIGHTS}
    Mo = {n: args['m_' + n] for n in WEIGHTS}
    Vo = {n: args['v_' + n] for n in WEIGHTS}
    c = _config(x, conv_w, w_uq, w_ukv, mla_qn_nope_g, mla_qn_rope_g, mem, mem_qn_g, w_mem_out, w_mla_out)
    N, D, R, B, H = c.N, c.D, c.R, c.B, c.H
    assert x.shape[0] == 1
    xs = x[0]
    tgt = loss_target[0]
    memx = mem[0]
    me = 4 * lax.axis_index("x") + 2 * lax.axis_index("y") + lax.axis_index("c")
    nr = N // R

    g_win, g_taps = _all_gather([W['w_in'][0].astype(_BF), conv_w[0]], "ag_w_in")
    rest = [n for n in BIG if n != 'w_in']
    shards_r = [W[n][0].astype(_BF) for n in rest]
    lands_r = [lax.dynamic_update_index_in_dim(lax.empty((NDEV,) + s.shape, s.dtype), s[None], me, 0)
               for s in shards_r]
    ag_rest = _split_start(shards_r, lands_r, True, "ag_rest_start", after=g_win)
    win_p = _win_to_padded(jnp.transpose(g_win, (1, 0, 2)).reshape(D, -1), c)
    win_pT = win_p.T
    convw = jnp.transpose(g_taps, (1, 0, 2)).reshape(3, c.CW)
    convw8 = jnp.pad(convw, ((0, 5), (0, 0)))

    def rowb(width, cidx):
        return pl.BlockSpec((R, width), lambda i, _c=cidx: (i, _c))

    def fullb(shape):
        nd = len(shape)
        return pl.BlockSpec(shape, lambda *_: (0,) * nd)

    def pad_lanes(g, w=LANES):
        return jnp.pad(g, ((0, 0), (0, w - g.shape[1])))

    def tabs_of(rows):
        return pl.BlockSpec((3, rows, LANES), lambda i, *_: (0, i, 0))

    half = c.ROPE // 2
    inv_freq = jnp.power(ROPE_THETA, -jnp.arange(half, dtype=F32) / half)
    invf = jnp.concatenate([inv_freq, inv_freq, jnp.zeros((LANES - c.ROPE,), F32)])[None, :]
    pos_col = positions[0].astype(F32).reshape(N, 1)

    def rope_tab_kern(pos_ref, invf_ref, o_ref):
        ang = pos_ref[...] * invf_ref[...]
        co, si = jnp.cos(ang), jnp.sin(ang)
        lane = lax.broadcasted_iota(jnp.int32, ang.shape, 1)
        o_ref[0] = jnp.where(lane < c.ROPE, co, 0.0)
        o_ref[1] = jnp.where(lane < half, -si, 0.0)
        o_ref[2] = jnp.where(jnp.logical_and(lane >= half, lane < c.ROPE), si, 0.0)

    tabs = pl.pallas_call(
        rope_tab_kern, name="rope_tab", grid=(nr,),
        in_specs=[pl.BlockSpec((R, 1), lambda i: (i, 0)), fullb((1, LANES))],
        out_specs=tabs_of(R),
        out_shape=jax.ShapeDtypeStruct((3, N, LANES), F32),
        compiler_params=_cparams(("parallel",)),
    )(pos_col, invf)

    def make_rms_kern():
        def rms_fwd_kern(x_ref, g_ref, o_ref):
            xh, _ = _rms(x_ref[...].astype(F32), x_ref.shape[-1])
            o_ref[...] = (xh * g_ref[...]).astype(o_ref.dtype)
        return rms_fwd_kern

    h = pl.pallas_call(
        make_rms_kern(), name="rms_x", grid=(nr,),
        in_specs=[rowb(D, 0), fullb((1, D))], out_specs=rowb(D, 0),
        out_shape=jax.ShapeDtypeStruct((N, D), _BF), compiler_params=_cparams(("parallel",)),
    )(xs, norm_g)

    proj = _mm(h, win_p, name="mm_proj", out_dtype=_BF, bn=1408, after=ag_rest[4])

    Wf, WfT = {}, {}
    for n, g in zip(rest, _split_wait(ag_rest, True, proj, "ag_rest_wait")):
        if n in COL_SHARDED:
            Wf[n] = jnp.transpose(g, (1, 0, 2)).reshape(g.shape[1], -1)
            WfT[n] = jnp.transpose(g, (0, 2, 1)).reshape(-1, g.shape[1])
        else:
            Wf[n] = g.reshape(-1, g.shape[2])
            WfT[n] = Wf[n].T
    wuq = Wf['w_uq'].reshape(c.QL, H, c.NOPE + c.ROPE)
    wuq_p = jnp.pad(wuq, ((0, 0), (0, 0), (0, c.HW - c.NOPE - c.ROPE))).reshape(c.QL, H * c.HW)
    wuq_pT = wuq_p.T
    wukv, wukvT = Wf['w_ukv'], WfT['w_ukv']
    wco, wmo, wmkv, wmemo, wo = Wf['w_conv_out'], Wf['w_mla_out'], Wf['w_mem_kv'], Wf['w_mem_out'], Wf['w_o']
    wcoT, wmoT, wmkvT, wmemoT, woT = WfT['w_conv_out'], WfT['w_mla_out'], WfT['w_mem_kv'], WfT['w_mem_out'], WfT['w_o']

    CW = c.CW
    conv_blk = c.o_conv // (4 * CW)
    HALO = 16
    rh = R // HALO

    def conv_parts(blk):
        blk = blk.astype(F32)
        return blk[:, 0:CW], blk[:, CW:2 * CW], blk[:, 2 * CW:3 * CW], blk[:, 3 * CW:4 * CW]

    def shifted(cu, prev, i):
        prev = jnp.where(i > 0, prev, 0.0)
        rid = lax.broadcasted_iota(jnp.int32, cu.shape, 0)
        last, last2 = prev[HALO - 1:HALO, :], prev[HALO - 2:HALO - 1, :]
        sh1 = jnp.where(rid == 0, last, pltpu.roll(cu, 1, 0))
        sh2 = jnp.where(rid == 0, last2, jnp.where(rid == 1, last, pltpu.roll(cu, 2, 0)))
        return sh1, sh2

    def conv_fwd_kern(p_ref, prev_ref, w_ref, o_ref):
        i = pl.program_id(0)
        cg, bg, u, z = conv_parts(p_ref[...])
        pc, _, pu, _ = conv_parts(prev_ref[...])
        cu = cg * u
        sh1, sh2 = shifted(cu, pc * pu, i)
        w = w_ref[...]
        conv = w[0:1, :] * sh2 + w[1:2, :] * sh1 + w[2:3, :] * cu
        o_ref[...] = (bg * conv * (z * _sig(z))).astype(o_ref.dtype)

    prev_spec = pl.BlockSpec((HALO, 4 * CW), lambda i: (jnp.maximum(i * rh - 1, 0), conv_blk))
    a_conv = pl.pallas_call(
        conv_fwd_kern, name="conv_fwd", grid=(nr,),
        in_specs=[rowb(4 * CW, conv_blk), prev_spec, fullb((8, CW))],
        out_specs=rowb(CW, 0), out_shape=jax.ShapeDtypeStruct((N, CW), _BF),
        compiler_params=_cparams(("parallel",)),
    )(proj, proj, convw8)
    o_conv = _mm(a_conv, wco, name="mm_oconv", out_dtype=_BF)

    QL, KVL, HW = c.QL, c.KVL, c.HW
    lora_blk = c.o_lora // (QL + KVL)

    def lora_fwd_kern(p_ref, gq_ref, gkv_ref, q_ref, kv_ref):
        blk = p_ref[...].astype(F32)
        qh, _ = _rms(blk[:, :QL], QL)
        kh, _ = _rms(blk[:, QL:], KVL)
        q_ref[...] = (qh * gq_ref[...]).astype(q_ref.dtype)
        kv_ref[...] = (kh * gkv_ref[...]).astype(kv_ref.dtype)

    cqn, ckvn = pl.pallas_call(
        lora_fwd_kern, name="lora_fwd", grid=(nr,),
        in_specs=[rowb(QL + KVL, lora_blk), fullb((1, QL)), fullb((1, KVL))],
        out_specs=[rowb(QL, 0), rowb(KVL, 0)],
        out_shape=[jax.ShapeDtypeStruct((N, QL), _BF), jax.ShapeDtypeStruct((N, KVL), _BF)],
        compiler_params=_cparams(("parallel",)),
    )(proj, mla_q_norm_g, mla_kv_norm_g)
    q_p = _mm(cqn, wuq_p, name="mm_q", out_dtype=_BF)
    kv = _mm(ckvn, wukv, name="mm_kv", out_dtype=_BF)

    g_qn, g_qr = mla_qn_nope_g, pad_lanes(mla_qn_rope_g)
    g_kn, g_kr = mla_kn_nope_g, pad_lanes(mla_kn_rope_g)
    kr_blk = c.o_kr // LANES

    def krope_fwd_kern(p_ref, t_ref, g_ref, o_ref):
        xh, _ = _rms(p_ref[...].astype(F32), c.ROPE)
        o_ref[...] = _rope(xh * g_ref[...], t_ref[0], t_ref[1], t_ref[2]).astype(o_ref.dtype)

    k_rope = pl.pallas_call(
        krope_fwd_kern, name="krope_fwd", grid=(nr,),
        in_specs=[rowb(LANES, kr_blk), tabs_of(R), fullb((1, LANES))],
        out_specs=rowb(LANES, 0), out_shape=jax.ShapeDtypeStruct((N, LANES), _BF),
        compiler_params=_cparams(("parallel",)),
    )(proj, tabs, g_kr)

    RP, HG = c.RP, c.HG
    nrp, nhg = N // RP, H // HG
    heads_in = pl.BlockSpec((RP, HG * HW), lambda i, hg: (i, hg))
    heads_out = pl.BlockSpec((HG, RP, HW), lambda i, hg: (hg, i, 0))

    def q_prep_kern(q_ref, t_ref, gn_ref, gr_ref, o_ref):
        for g in range(HG):
            blk = q_ref[:, g * HW:(g + 1) * HW].astype(F32)
            nh, _ = _rms(blk[:, :LANES], c.NOPE)
            rhat, _ = _rms(blk[:, LANES:], c.ROPE)
            rot = _rope(rhat * gr_ref[...], t_ref[0], t_ref[1], t_ref[2])
            o_ref[g] = (jnp.concatenate([nh * gn_ref[...], rot], axis=1) * (c.scale * LOG2E)).astype(o_ref.dtype)

    q_cat = pl.pallas_call(
        q_prep_kern, name="q_prep", grid=(nrp, nhg),
        in_specs=[heads_in, tabs_of(RP), fullb((1, LANES)), fullb((1, LANES))],
        out_specs=heads_out, out_shape=jax.ShapeDtypeStruct((H, N, HW), _BF),
        compiler_params=_cparams(("parallel", "parallel")),
    )(q_p, tabs, g_qn, g_qr)

    def k_prep_kern(kv_ref, kr_ref, gn_ref, o_ref):
        for g in range(HG):
            kn, _ = _rms(kv_ref[:, g * HW:g * HW + LANES].astype(F32), c.NOPE)
            o_ref[g] = jnp.concatenate([(kn * gn_ref[...]).astype(o_ref.dtype), kr_ref[...]], axis=1)

    k_cat = pl.pallas_call(
        k_prep_kern, name="k_prep", grid=(nrp, nhg),
        in_specs=[heads_in, pl.BlockSpec((RP, LANES), lambda i, hg: (i, 0)), fullb((1, LANES))],
        out_specs=heads_out, out_shape=jax.ShapeDtypeStruct((H, N, HW), _BF),
        compiler_params=_cparams(("parallel", "parallel")),
    )(kv, k_rope, g_kn)

    nb = N // B
    assert CHUNK & (CHUNK - 1) == 0 and B % CHUNK == 0

    def diag_mask(s, row0=0):
        row = lax.broadcasted_iota(jnp.int32, s.shape, 0) + row0
        col = lax.broadcasted_iota(jnp.int32, s.shape, 1)
        shift = CHUNK.bit_length() - 1
        allowed = jnp.right_shift(col, shift) <= jnp.right_shift(row, shift)
        return jnp.where(allowed, s, NEG)

    k_head = pl.BlockSpec((1, N, HW), lambda hh, i: (hh, 0, 0))
    v_head = pl.BlockSpec((N, LANES), lambda hh, i: (0, 2 * hh + 1))
    q_blk = pl.BlockSpec((1, B, HW), lambda hh, i: (hh, i, 0))
    o_blk = pl.BlockSpec((B, LANES), lambda hh, i: (i, hh))
    lse_blk = pl.BlockSpec((1, B, LANES), lambda hh, i: (hh, i, 0))

    def attn_fwd_kern(q_ref, k_ref, v_ref, o_ref, lse_ref, m_sc, acc_sc, s_sc):
        i = pl.program_id(1)
        m_sc[...] = jnp.full(m_sc.shape, NEG, F32)
        acc_sc[...] = jnp.zeros(acc_sc.shape, F32)

        def rows_of(t):
            return pl.ds(pl.multiple_of(t * B, B), B)

        def scores(t, slot):
            s_sc[slot] = _dot_nt(q_ref[0], k_ref[0, rows_of(t), :])

        def softmax_pv(t, slot, masked):
            s = s_sc[slot]
            if masked:
                s = diag_mask(s)
            mt = s[:, 0:LANES]
            for cb in range(1, B // LANES):
                mt = jnp.maximum(mt, s[:, cb * LANES:(cb + 1) * LANES])
            m_prev = m_sc[...]
            m_new = jnp.maximum(m_prev, jnp.max(mt, axis=1, keepdims=True))
            alpha = jnp.exp2(m_prev - m_new)
            p = jnp.concatenate([jnp.exp2(s[:, cb * LANES:(cb + 1) * LANES] - m_new).astype(_BF)
                                 for cb in range(B // LANES)], axis=1)
            v_ones = jnp.concatenate([v_ref[rows_of(t), :], jnp.ones((B, LANES), _BF)], axis=1)
            acc_sc[...] = jnp.concatenate([alpha, alpha], axis=1) * acc_sc[...] + _dot(p, v_ones)
            m_sc[...] = m_new

        scores(0, 0)

        def run(first, count, last_masked):
            for u in range(count):
                if u + 1 < count or not last_masked:
                    scores(first + u + 1, (u + 1) % 2)
                softmax_pv(first + u, u % 2, last_masked and u == count - 1)

        def unrolled(u, carry):
            run(ATT_UNROLL * u, ATT_UNROLL, False)
            return carry

        lax.fori_loop(0, i // ATT_UNROLL, unrolled, 0)
        for rem in range(ATT_UNROLL):
            @pl.when(i % ATT_UNROLL == rem)
            def _(rem=rem):
                run(i - rem, rem + 1, True)

        acc = acc_sc[...]
        o_ref[...] = (acc[:, :LANES] / acc[:, LANES:]).astype(o_ref.dtype)
        lse_ref[0] = m_sc[...] + jnp.log2(acc[:, LANES:])

    mla_y, lse = pl.pallas_call(
        attn_fwd_kern, name="attn_fwd", grid=(H, nb),
        in_specs=[q_blk, k_head, v_head], out_specs=[o_blk, lse_blk],
        out_shape=[jax.ShapeDtypeStruct((N, c.HV), _BF), jax.ShapeDtypeStruct((H, N, LANES), F32)],
        scratch_shapes=[pltpu.VMEM((B, LANES), F32), pltpu.VMEM((B, HW), F32), pltpu.VMEM((2, B, B), F32)],
        compiler_params=_cparams(("parallel", "arbitrary")),
    )(q_cat, k_cat, kv)

    HV = c.HV
    mz_blk = c.o_mz // HV

    def gate_fwd_kern(y_ref, z_ref, o_ref):
        z = z_ref[...].astype(F32)
        o_ref[...] = (y_ref[...].astype(F32) * (z * _sig(z))).astype(o_ref.dtype)

    a_mla = pl.pallas_call(
        gate_fwd_kern, name="gate_mla", grid=(nr,),
        in_specs=[rowb(HV, 0), rowb(HV, mz_blk)], out_specs=rowb(HV, 0),
        out_shape=jax.ShapeDtypeStruct((N, HV), _BF), compiler_params=_cparams(("parallel",)),
    )(mla_y, proj)
    o_mla = _mm(a_mla, wmo, name="mm_omla", out_dtype=_BF)

    M, MW, MH, MHD = c.M, c.MW, c.MH, c.MHD
    memn = pl.pallas_call(
        make_rms_kern(), name="rms_mem",
        grid=(1,), in_specs=[fullb((M, D)), fullb((1, D))], out_specs=fullb((M, D)),
        out_shape=jax.ShapeDtypeStruct((M, D), _BF), compiler_params=_cparams(("arbitrary",)),
    )(memx, mem_norm_g)
    kvm = _mm(memn, wmkv, name="mm_memkv", out_dtype=F32)

    def memk_fwd_kern(kv_ref, g_ref, k_ref, v_ref):
        for hh in range(MH):
            kh, _ = _rms(kv_ref[:, hh * MHD:(hh + 1) * MHD], MHD)
            k_ref[:, hh * MHD:(hh + 1) * MHD] = (kh * g_ref[...]).astype(k_ref.dtype)
        v_ref[...] = kv_ref[:, MW:].astype(v_ref.dtype)

    mem_k, mem_v = pl.pallas_call(
        memk_fwd_kern, name="memk_fwd", grid=(1,),
        in_specs=[fullb((M, 2 * MW)), fullb((1, MHD))], out_specs=[fullb((M, MW)), fullb((M, MW))],
        out_shape=[jax.ShapeDtypeStruct((M, MW), _BF)] * 2, compiler_params=_cparams(("arbitrary",)),
    )(kvm, mem_kn_g)

    mem_blk = c.o_mem // (2 * MW)

    def mem_head(qz_ref, k_ref, v_ref, g_ref, hh):
        sl = slice(hh * MHD, (hh + 1) * MHD)
        qh, r = _rms(qz_ref[:, sl].astype(F32), MHD)
        qn = (qh * g_ref[...]).astype(_BF)
        s = _dot_nt(qn, k_ref[:, sl]) * c.mscale
        e = jnp.exp(s - jnp.max(s, axis=1, keepdims=True))
        p = e / jnp.sum(e, axis=1, keepdims=True)
        y = _dot(p.astype(_BF), v_ref[:, sl])
        z = qz_ref[:, MW + hh * MHD:MW + (hh + 1) * MHD].astype(F32)
        return sl, qh, r, qn, p, y, z

    def mem_fwd_kern(qz_ref, k_ref, v_ref, g_ref, o_ref):
        for hh in range(MH):
            sl, _, _, _, _, y, z = mem_head(qz_ref, k_ref, v_ref, g_ref, hh)
            o_ref[:, sl] = (y * (z * _sig(z))).astype(o_ref.dtype)

    a_mem = pl.pallas_call(
        mem_fwd_kern, name="mem_fwd", grid=(nr,),
        in_specs=[rowb(2 * MW, mem_blk), fullb((M, MW)), fullb((M, MW)), fullb((1, MHD))],
        out_specs=rowb(MW, 0), out_shape=jax.ShapeDtypeStruct((N, MW), _BF),
        compiler_params=_cparams(("parallel",)),
    )(proj, mem_k, mem_v, mem_qn_g)
    o_mem = _mm(a_mem, wmemo, name="mm_omem", out_dtype=_BF)

    g_blk = c.o_g // (3 * D)

    def merge_fwd_kern(g_ref, oc_ref, om_ref, ome_ref, o_ref):
        g = g_ref[...].astype(F32)
        acc = _sig(g[:, :D]) * oc_ref[...].astype(F32)
        acc += _sig(g[:, D:2 * D]) * om_ref[...].astype(F32)
        acc += _sig(g[:, 2 * D:]) * ome_ref[...].astype(F32)
        o_ref[...] = acc.astype(o_ref.dtype)

    merged = pl.pallas_call(
        merge_fwd_kern, name="merge_fwd", grid=(nr,),
        in_specs=[rowb(3 * D, g_blk), rowb(D, 0), rowb(D, 0), rowb(D, 0)], out_specs=rowb(D, 0),
        out_shape=jax.ShapeDtypeStruct((N, D), _BF), compiler_params=_cparams(("parallel",)),
    )(proj, o_conv, o_mla, o_mem)
    y2 = _mm(merged, wo, name="mm_out", out_dtype=F32)

    def loss_kern(x_ref, y_ref, t_ref, dy_ref, dyb_ref, l_ref):
        e = x_ref[...] + y_ref[...] - t_ref[...]
        dy = e * (1.0 / D)
        dy_ref[...] = dy
        dyb_ref[...] = dy.astype(dyb_ref.dtype)

        @pl.when(pl.program_id(0) == 0)
        def _():
            l_ref[...] = jnp.zeros(l_ref.shape, F32)

        l_ref[...] += jnp.sum(e * e, axis=0, keepdims=True)

    dy, dyb, lpart = pl.pallas_call(
        loss_kern, name="loss", grid=(nr,),
        in_specs=[rowb(D, 0)] * 3, out_specs=[rowb(D, 0), rowb(D, 0), fullb((1, D))],
        out_shape=[jax.ShapeDtypeStruct((N, D), F32), jax.ShapeDtypeStruct((N, D), _BF),
                   jax.ShapeDtypeStruct((1, D), F32)],
        compiler_params=_cparams(("arbitrary",)),
    )(xs, y2, tgt)
    loss = lax.psum(jnp.sum(lpart) * (0.5 / D), AXES)

    G = {}
    d_merged = _mm(dyb, woT, name="mm_dmerged", out_dtype=_BF)
    G['w_o'] = _mm(merged, dyb, ta=True, name="mm_dwo", out_dtype=_BF)

    dproj0 = lax.empty((N, c.P), _BF)
    any_spec = pl.BlockSpec(memory_space=pl.ANY)

    def merge_bwd_kern(dp_any, g_ref, dm_ref, oc_ref, om_ref, ome_ref, dg_ref, doc_ref, dom_ref, dome_ref):
        g = g_ref[...].astype(F32)
        dm = dm_ref[...].astype(F32)
        for idx, (o_in, d_out) in enumerate(((oc_ref, doc_ref), (om_ref, dom_ref), (ome_ref, dome_ref))):
            sg = _sig(g[:, idx * D:(idx + 1) * D])
            d_out[...] = (sg * dm).astype(d_out.dtype)
            dg_ref[:, idx * D:(idx + 1) * D] = (dm * o_in[...].astype(F32) * sg * (1.0 - sg)).astype(dg_ref.dtype)

    dproj1, d_oconv, d_omla, d_omem = pl.pallas_call(
        merge_bwd_kern, name="merge_bwd", grid=(nr,),
        in_specs=[any_spec, rowb(3 * D, g_blk), rowb(D, 0), rowb(D, 0), rowb(D, 0), rowb(D, 0)],
        out_specs=[rowb(3 * D, g_blk), rowb(D, 0), rowb(D, 0), rowb(D, 0)],
        out_shape=[jax.ShapeDtypeStruct((N, c.P), _BF)] + [jax.ShapeDtypeStruct((N, D), _BF)] * 3,
        input_output_aliases={0: 0}, compiler_params=_cparams(("parallel",)),
    )(dproj0, proj, d_merged, o_conv, o_mla, o_mem)

    G['w_conv_out'] = _mm(a_conv, d_oconv, ta=True, name="mm_dwco", out_dtype=_BF)
    d_aconv = _mm(d_oconv, wcoT, name="mm_daconv", out_dtype=_BF)
    G['w_mla_out'] = _mm(a_mla, d_omla, ta=True, name="mm_dwmo", out_dtype=_BF)
    d_amla = _mm(d_omla, wmoT, name="mm_damla", out_dtype=_BF)
    G['w_mem_out'] = _mm(a_mem, d_omem, ta=True, name="mm_dwmemo", out_dtype=_BF)
    d_amem = _mm(d_omem, wmemoT, name="mm_damem", out_dtype=_BF)

    def conv_bwd_kern(dp_any, p_ref, prev_ref, next_ref, da_ref, dan_ref, w_ref, o_ref, dw_ref):
        i = pl.program_id(0)
        cg, bg, u, z = conv_parts(p_ref[...])
        pc, _, pu, _ = conv_parts(prev_ref[...])
        _, nbg, _, nz = conv_parts(next_ref[...])
        cu = cg * u
        sh1, sh2 = shifted(cu, pc * pu, i)
        w = w_ref[...]
        conv = w[0:1, :] * sh2 + w[1:2, :] * sh1 + w[2:3, :] * cu
        sg = _sig(z)
        sz = z * sg
        da = da_ref[...].astype(F32)
        dcy = da * sz
        d_z = da * (bg * conv) * (sg * (1.0 + z * (1.0 - sg)))
        d_b = dcy * conv
        dconv = dcy * bg
        dnext = dan_ref[...].astype(F32) * (nz * _sig(nz)) * nbg
        dnext = jnp.where(i < nr - 1, dnext, 0.0)
        rid = lax.broadcasted_iota(jnp.int32, cu.shape, 0)
        up1 = jnp.where(rid == R - 1, dnext[0:1, :], pltpu.roll(dconv, R - 1, 0))
        up2 = jnp.where(rid == R - 2, dnext[0:1, :], jnp.where(rid == R - 1, dnext[1:2, :], pltpu.roll(dconv, R - 2, 0)))
        dcu = w[2:3, :] * dconv + w[1:2, :] * up1 + w[0:1, :] * up2
        o_ref[:, 0:CW] = (dcu * u).astype(o_ref.dtype)
        o_ref[:, CW:2 * CW] = d_b.astype(o_ref.dtype)
        o_ref[:, 2 * CW:3 * CW] = (dcu * cg).astype(o_ref.dtype)
        o_ref[:, 3 * CW:4 * CW] = d_z.astype(o_ref.dtype)

        @pl.when(i == 0)
        def _():
            dw_ref[...] = jnp.zeros(dw_ref.shape, F32)

        dw_ref[0:1, :] += jnp.sum(dconv * sh2, axis=0, keepdims=True)
        dw_ref[1:2, :] += jnp.sum(dconv * sh1, axis=0, keepdims=True)
        dw_ref[2:3, :] += jnp.sum(dconv * cu, axis=0, keepdims=True)

    next_spec = pl.BlockSpec((HALO, 4 * CW), lambda i: (jnp.minimum((i + 1) * rh, N // HALO - 1), conv_blk))
    dan_spec = pl.BlockSpec((HALO, CW), lambda i: (jnp.minimum((i + 1) * rh, N // HALO - 1), 0))
    dproj2, g_convw = pl.pallas_call(
        conv_bwd_kern, name="conv_bwd", grid=(nr,),
        in_specs=[any_spec, rowb(4 * CW, conv_blk), prev_spec, next_spec, rowb(CW, 0), dan_spec, fullb((8, CW))],
        out_specs=[rowb(4 * CW, conv_blk), fullb((8, CW))],
        out_shape=[jax.ShapeDtypeStruct((N, c.P), _BF), jax.ShapeDtypeStruct((8, CW), F32)],
        input_output_aliases={0: 0}, compiler_params=_cparams(("arbitrary",)),
    )(dproj1, proj, proj, proj, d_aconv, d_aconv, convw8)

    def mem_bwd_kern(dp_any, qz_ref, da_ref, k_ref, v_ref, g_ref, o_ref, dk_ref, dv_ref, dg_ref):
        @pl.when(pl.program_id(0) == 0)
        def _():
            dk_ref[...] = jnp.zeros(dk_ref.shape, F32)
            dv_ref[...] = jnp.zeros(dv_ref.shape, F32)
            dg_ref[...] = jnp.zeros(dg_ref.shape, F32)

        for hh in range(MH):
            sl, qh, r, qn, p, y, z = mem_head(qz_ref, k_ref, v_ref, g_ref, hh)
            da = da_ref[:, sl].astype(F32)
            sg = _sig(z)
            dyh = da * (z * sg)
            o_ref[:, MW + hh * MHD:MW + (hh + 1) * MHD] = (da * y * (sg * (1.0 + z * (1.0 - sg)))).astype(o_ref.dtype)
            dyb_h = dyh.astype(_BF)
            dpm = _dot_nt(dyb_h, v_ref[:, sl])
            ds = (p * (dpm - jnp.sum(dpm * p, axis=1, keepdims=True)) * c.mscale).astype(_BF)
            dqn = _dot(ds, k_ref[:, sl])
            dk_ref[:, sl] += _dot_tn(ds, qn)
            dv_ref[:, sl] += _dot_tn(p.astype(_BF), dyb_h)
            dq, dgp = _rms_bwd(qh, r, g_ref[...], dqn, MHD)
            o_ref[:, sl] = dq.astype(o_ref.dtype)
            dg_ref[...] += jnp.sum(dgp, axis=0, keepdims=True)

    dproj3, d_memk, d_memv, g_mem_qn = pl.pallas_call(
        mem_bwd_kern, name="mem_bwd", grid=(nr,),
        in_specs=[any_spec, rowb(2 * MW, mem_blk), rowb(MW, 0), fullb((M, MW)), fullb((M, MW)), fullb((1, MHD))],
        out_specs=[rowb(2 * MW, mem_blk), fullb((M, MW)), fullb((M, MW)), fullb((1, MHD))],
        out_shape=[jax.ShapeDtypeStruct((N, c.P), _BF), jax.ShapeDtypeStruct((M, MW), F32),
                   jax.ShapeDtypeStruct((M, MW), F32), jax.ShapeDtypeStruct((1, MHD), F32)],
        input_output_aliases={0: 0}, compiler_params=_cparams(("arbitrary",)),
    )(dproj2, proj, d_amem, mem_k, mem_v, mem_qn_g)

    def memk_bwd_kern(kv_ref, dk_ref, dv_ref, g_ref, o_ref, dg_ref):
        dg = jnp.zeros((1, MHD), F32)
        for hh in range(MH):
            sl = slice(hh * MHD, (hh + 1) * MHD)
            kh, r = _rms(kv_ref[:, sl], MHD)
            dkr, dgp = _rms_bwd(kh, r, g_ref[...], dk_ref[:, sl], MHD)
            o_ref[:, sl] = dkr.astype(o_ref.dtype)
            dg += jnp.sum(dgp, axis=0, keepdims=True)
        o_ref[:, MW:] = dv_ref[...].astype(o_ref.dtype)
        dg_ref[...] = dg

    d_kvm, g_mem_kn = pl.pallas_call(
        memk_bwd_kern, name="memk_bwd", grid=(1,),
        in_specs=[fullb((M, 2 * MW)), fullb((M, MW)), fullb((M, MW)), fullb((1, MHD))],
        out_specs=[fullb((M, 2 * MW)), fullb((1, MHD))],
        out_shape=[jax.ShapeDtypeStruct((M, 2 * MW), _BF), jax.ShapeDtypeStruct((1, MHD), F32)],
        compiler_params=_cparams(("arbitrary",)),
    )(kvm, d_memk, d_memv, mem_kn_g)
    G['w_mem_kv'] = _mm(memn, d_kvm, ta=True, name="mm_dwmkv", out_dtype=_BF)
    d_memn = _mm(d_kvm, wmkvT, name="mm_dmemn", out_dtype=F32)

    def memnorm_bwd_kern(x_ref, d_ref, dg_ref):
        xh, _ = _rms(x_ref[...], D)
        dg_ref[...] = jnp.sum(d_ref[...] * xh, axis=0, keepdims=True)

    g_mem_norm = pl.pallas_call(
        memnorm_bwd_kern, name="memnorm_bwd", grid=(1,),
        in_specs=[fullb((M, D)), fullb((M, D))], out_specs=fullb((1, D)),
        out_shape=jax.ShapeDtypeStruct((1, D), F32), compiler_params=_cparams(("arbitrary",)),
    )(memx, d_memn)

    def gate_bwd_kern(dp_any, da_ref, y_ref, z_ref, dy_ref, dz_ref):
        z = z_ref[...].astype(F32)
        da = da_ref[...].astype(F32)
        sg = _sig(z)
        dy_ref[...] = (da * (z * sg)).astype(dy_ref.dtype)
        dz_ref[...] = (da * y_ref[...].astype(F32) * (sg * (1.0 + z * (1.0 - sg)))).astype(dz_ref.dtype)

    d_mlay, dproj4 = pl.pallas_call(
        gate_bwd_kern, name="gate_mla_bwd", grid=(nr,),
        in_specs=[any_spec, rowb(HV, 0), rowb(HV, 0), rowb(HV, mz_blk)],
        out_specs=[rowb(HV, 0), rowb(HV, mz_blk)],
        out_shape=[jax.ShapeDtypeStruct((N, HV), _BF), jax.ShapeDtypeStruct((N, c.P), _BF)],
        input_output_aliases={0: 1}, compiler_params=_cparams(("parallel",)),
    )(dproj3, d_amla, mla_y, proj)

    def attn_bwd_kern(q_ref, k_ref, v_ref, o_ref, do_ref, lse_ref, dq_ref, dk_ref, dv_ref, dq_sc, dl_sc):
        i = pl.program_id(1)
        q, do = q_ref[0], do_ref[...]
        delta = jnp.sum(do.astype(F32) * o_ref[...].astype(F32), axis=1, keepdims=True)
        dl_sc[...] = jnp.broadcast_to(delta, dl_sc.shape)
        dq_sc[...] = jnp.zeros(dq_sc.shape, F32)

        def step(off, masked):
            rows = pl.ds(off, B)
            k = k_ref[0, rows, :]
            s = _dot_nt(q, k)
            if masked:
                s = diag_mask(s)
            dpm = _dot_nt(do, v_ref[rows, :])
            lse_t, dl = lse_ref[0], dl_sc[...]
            ps, dss = [], []
            for cb in range(B // LANES):
                cols = slice(cb * LANES, (cb + 1) * LANES)
                p_cb = jnp.exp2(s[:, cols] - lse_t)
                ps.append(p_cb.astype(_BF))
                dss.append((p_cb * (dpm[:, cols] - dl)).astype(_BF))
            p, ds = jnp.concatenate(ps, axis=1), jnp.concatenate(dss, axis=1)
            dvp = _dot_tn(p, do)
            dkp = _dot_tn(ds, q)
            if masked:
                dk_ref[0, rows, :] = dkp
                dv_ref[0, rows, :] = dvp
            else:
                dk_ref[0, rows, :] += dkp
                dv_ref[0, rows, :] += dvp
            dq_sc[...] += _dot(ds, k)

        def unrolled(t, carry):
            for u in range(ATT_UNROLL):
                step(pl.multiple_of((t * ATT_UNROLL + u) * B, B), False)
            return carry

        lax.fori_loop(0, i // ATT_UNROLL, unrolled, 0)
        for u in range(ATT_UNROLL - 1):
            @pl.when(i % ATT_UNROLL > u)
            def _(u=u):
                step(pl.multiple_of((i // ATT_UNROLL * ATT_UNROLL + u) * B, B), False)

        step(pl.multiple_of(i * B, B), True)
        dq_ref[0] = dq_sc[...].astype(dq_ref.dtype)

    d_qcat, d_kcat, d_v = pl.pallas_call(
        attn_bwd_kern, name="attn_bwd", grid=(H, nb),
        in_specs=[q_blk, k_head, v_head, o_blk, o_blk, lse_blk],
        out_specs=[pl.BlockSpec((1, B, HW), lambda hh, i: (hh, i, 0)),
                   pl.BlockSpec((1, N, HW), lambda hh, i: (hh, 0, 0)),
                   pl.BlockSpec((1, N, LANES), lambda hh, i: (hh, 0, 0))],
        out_shape=[jax.ShapeDtypeStruct((H, N, HW), _BF), jax.ShapeDtypeStruct((H, N, HW), F32),
                   jax.ShapeDtypeStruct((H, N, LANES), F32)],
        scratch_shapes=[pltpu.VMEM((B, HW), F32), pltpu.VMEM((B, LANES), F32)],
        compiler_params=_cparams(("parallel", "arbitrary")),
    )(q_cat, k_cat, kv, mla_y, d_mlay, lse)

    def q_prep_bwd_kern(q_ref, dq_ref, t_ref, gn_ref, gr_ref, o_ref, dgn_ref, dgr_ref):
        @pl.when(jnp.logical_and(pl.program_id(0) == 0, pl.program_id(1) == 0))
        def _():
            dgn_ref[...] = jnp.zeros(dgn_ref.shape, F32)
            dgr_ref[...] = jnp.zeros(dgr_ref.shape, F32)

        for g in range(HG):
            blk = q_ref[:, g * HW:(g + 1) * HW].astype(F32)
            d = dq_ref[g].astype(F32) * c.scale
            nh, rn = _rms(blk[:, :LANES], c.NOPE)
            rhat, rr = _rms(blk[:, LANES:], c.ROPE)
            dn, dgn = _rms_bwd(nh, rn, gn_ref[...], d[:, :LANES], c.NOPE)
            drot = _rope_t(d[:, LANES:], t_ref[0], t_ref[1], t_ref[2])
            dr, dgr = _rms_bwd(rhat, rr, gr_ref[...], drot, c.ROPE)
            o_ref[:, g * HW:(g + 1) * HW] = jnp.concatenate([dn, dr], axis=1).astype(o_ref.dtype)
            dgn_ref[...] += jnp.sum(dgn, axis=0, keepdims=True)
            dgr_ref[...] += jnp.sum(dgr, axis=0, keepdims=True)

    d_qp, g_qn_nope, g_qn_rope = pl.pallas_call(
        q_prep_bwd_kern, name="q_prep_bwd", grid=(nrp, nhg),
        in_specs=[heads_in, heads_out, tabs_of(RP), fullb((1, LANES)), fullb((1, LANES))],
        out_specs=[heads_in, fullb((1, LANES)), fullb((1, LANES))],
        out_shape=[jax.ShapeDtypeStruct((N, H * HW), _BF), jax.ShapeDtypeStruct((1, LANES), F32),
                   jax.ShapeDtypeStruct((1, LANES), F32)],
        compiler_params=_cparams(("arbitrary", "arbitrary")),
    )(q_p, d_qcat, tabs, g_qn, g_qr)

    def k_prep_bwd_kern(kv_ref, dk_ref, dv_ref, gn_ref, o_ref, dkr_ref, dgn_ref):
        hg = pl.program_id(1)

        @pl.when(jnp.logical_and(pl.program_id(0) == 0, hg == 0))
        def _():
            dgn_ref[...] = jnp.zeros(dgn_ref.shape, F32)

        @pl.when(hg == 0)
        def _():
            dkr_ref[...] = jnp.zeros(dkr_ref.shape, F32)

        dkr = jnp.zeros((RP, LANES), F32)
        for g in range(HG):
            dk = dk_ref[g] * (1.0 / LOG2E)
            kn, r = _rms(kv_ref[:, g * HW:g * HW + LANES].astype(F32), c.NOPE)
            dkn, dgn = _rms_bwd(kn, r, gn_ref[...], dk[:, :LANES], c.NOPE)
            o_ref[:, g * HW:(g + 1) * HW] = jnp.concatenate([dkn, dv_ref[g]], axis=1).astype(o_ref.dtype)
            dgn_ref[...] += jnp.sum(dgn, axis=0, keepdims=True)
            dkr += dk[:, LANES:]
        dkr_ref[...] += dkr

    d_kv, d_krsum, g_kn_nope = pl.pallas_call(
        k_prep_bwd_kern, name="k_prep_bwd", grid=(nrp, nhg),
        in_specs=[heads_in, heads_out, pl.BlockSpec((HG, RP, LANES), lambda i, hg: (hg, i, 0)), fullb((1, LANES))],
        out_specs=[heads_in, pl.BlockSpec((RP, LANES), lambda i, hg: (i, 0)), fullb((1, LANES))],
        out_shape=[jax.ShapeDtypeStruct((N, H * HW), _BF), jax.ShapeDtypeStruct((N, LANES), F32),
                   jax.ShapeDtypeStruct((1, LANES), F32)],
        compiler_params=_cparams(("arbitrary", "arbitrary")),
    )(kv, d_kcat, d_v, g_kn)

    def krope_bwd_kern(dp_any, p_ref, d_ref, t_ref, g_ref, o_ref, dg_ref):
        @pl.when(pl.program_id(0) == 0)
        def _():
            dg_ref[...] = jnp.zeros(dg_ref.shape, F32)

        xh, r = _rms(p_ref[...].astype(F32), c.ROPE)
        drot = _rope_t(d_ref[...], t_ref[0], t_ref[1], t_ref[2])
        dx, dg = _rms_bwd(xh, r, g_ref[...], drot, c.ROPE)
        o_ref[...] = dx.astype(o_ref.dtype)
        dg_ref[...] += jnp.sum(dg, axis=0, keepdims=True)

    dproj5, g_kn_rope = pl.pallas_call(
        krope_bwd_kern, name="krope_bwd", grid=(nr,),
        in_specs=[any_spec, rowb(LANES, kr_blk), rowb(LANES, 0), tabs_of(R), fullb((1, LANES))],
        out_specs=[rowb(LANES, kr_blk), fullb((1, LANES))],
        out_shape=[jax.ShapeDtypeStruct((N, c.P), _BF), jax.ShapeDtypeStruct((1, LANES), F32)],
        input_output_aliases={0: 0}, compiler_params=_cparams(("arbitrary",)),
    )(dproj4, proj, d_krsum, tabs, g_kr)

    g_wuq_p = _mm(cqn, d_qp, ta=True, name="mm_dwuq", out_dtype=_BF)
    G['w_uq'] = g_wuq_p.reshape(QL, H, HW)[:, :, :c.NOPE + c.ROPE].reshape(QL, H * (c.NOPE + c.ROPE))
    d_cqn = _mm(d_qp, wuq_pT, name="mm_dcqn", out_dtype=F32)
    G['w_ukv'] = _mm(ckvn, d_kv, ta=True, name="mm_dwukv", out_dtype=_BF)
    d_ckvn = _mm(d_kv, wukvT, name="mm_dckvn", out_dtype=F32)

    def to_blocks(n, g):
        if n in COL_SHARDED:
            return jnp.transpose(g.reshape(g.shape[0], NDEV, -1), (1, 0, 2))
        return g.reshape(NDEV, -1, g.shape[1])

    def landing(b):
        own = lax.dynamic_index_in_dim(b, me, 0, keepdims=True)
        return lax.dynamic_update_index_in_dim(lax.empty(b.shape, b.dtype), own, me, 0)

    early = [n for n in BIG if n != 'w_in']
    blocks_e = [to_blocks(n, G[n]) for n in early]
    xe = _split_start(blocks_e, [landing(b) for b in blocks_e], False, "xchg_early_start")
    gq_after = mla_q_norm_g + xe[4][0:1, 0:1]

    def lora_bwd_kern(dp_any, p_ref, dq_ref, dkv_ref, gq_ref, gkv_ref, o_ref, dgq_ref, dgkv_ref):
        @pl.when(pl.program_id(0) == 0)
        def _():
            dgq_ref[...] = jnp.zeros(dgq_ref.shape, F32)
            dgkv_ref[...] = jnp.zeros(dgkv_ref.shape, F32)

        blk = p_ref[...].astype(F32)
        qh, rq = _rms(blk[:, :QL], QL)
        kh, rk = _rms(blk[:, QL:], KVL)
        dq, dgq = _rms_bwd(qh, rq, gq_ref[...], dq_ref[...], QL)
        dk, dgk = _rms_bwd(kh, rk, gkv_ref[...], dkv_ref[...], KVL)
        o_ref[:, :QL] = dq.astype(o_ref.dtype)
        o_ref[:, QL:] = dk.astype(o_ref.dtype)
        dgq_ref[...] += jnp.sum(dgq, axis=0, keepdims=True)
        dgkv_ref[...] += jnp.sum(dgk, axis=0, keepdims=True)

    dproj6, g_q_norm, g_kv_norm = pl.pallas_call(
        lora_bwd_kern, name="lora_bwd", grid=(nr,),
        in_specs=[any_spec, rowb(QL + KVL, lora_blk), rowb(QL, 0), rowb(KVL, 0), fullb((1, QL)), fullb((1, KVL))],
        out_specs=[rowb(QL + KVL, lora_blk), fullb((1, QL)), fullb((1, KVL))],
        out_shape=[jax.ShapeDtypeStruct((N, c.P), _BF), jax.ShapeDtypeStruct((1, QL), F32),
                   jax.ShapeDtypeStruct((1, KVL), F32)],
        input_output_aliases={0: 0}, compiler_params=_cparams(("arbitrary",)),
    )(dproj5, proj, d_cqn, d_ckvn, gq_after, mla_kv_norm_g)

    g_win_p = _mm(h, dproj6, ta=True, name="mm_dwin", out_dtype=_BF, bn=1408)
    G['w_in'] = _win_from_padded(g_win_p, c)
    blocks_w = [to_blocks('w_in', G['w_in'])]
    xw = _split_start(blocks_w, [landing(b) for b in blocks_w], False, "xchg_win_start")
    d_h = _mm(dproj6, win_pT, name="mm_dh", out_dtype=F32, bk=1408, after=xw[4])

    def final_bwd_kern(x_ref, g_ref, dh_ref, dy_ref, gx_ref, dg_ref):
        @pl.when(pl.program_id(0) == 0)
        def _():
            dg_ref[...] = jnp.zeros(dg_ref.shape, F32)

        xh, r = _rms(x_ref[...], D)
        dx, dg = _rms_bwd(xh, r, g_ref[...], dh_ref[...], D)
        gx_ref[...] = dy_ref[...] + dx
        dg_ref[...] += jnp.sum(dg, axis=0, keepdims=True)

    grad_x, g_norm = pl.pallas_call(
        final_bwd_kern, name="final_bwd", grid=(nr,),
        in_specs=[rowb(D, 0), fullb((1, D)), rowb(D, 0), rowb(D, 0)],
        out_specs=[rowb(D, 0), fullb((1, D))],
        out_shape=[jax.ShapeDtypeStruct((N, D), F32), jax.ShapeDtypeStruct((1, D), F32)],
        compiler_params=_cparams(("arbitrary",)),
    )(xs, norm_g, d_h, dy)

    recv_e = _split_wait(xe, False, grad_x, "xchg_early_wait")
    recv_w = _split_wait(xw, False, grad_x, "xchg_win_wait")
    res = [{}, {}, {}, {}]
    for n, parts in zip(['w_in'] + early, list(recv_w) + list(recv_e)):
        outs = _adam(parts, W[n][0], Mo[n][0], Vo[n][0], "adam_" + n)
        for k in range(4):
            res[k][n] = outs[k][None]

    small_g = {'norm_g': g_norm, 'mla_q_norm_g': g_q_norm, 'mla_kv_norm_g': g_kv_norm,
               'mla_qn_nope_g': g_qn_nope, 'mla_qn_rope_g': g_qn_rope[:, :c.ROPE], 'mla_kn_nope_g': g_kn_nope,
               'mla_kn_rope_g': g_kn_rope[:, :c.ROPE], 'mem_norm_g': g_mem_norm, 'mem_qn_g': g_mem_qn,
               'mem_kn_g': g_mem_kn}
    small_part = _pack([small_g[n] for n in SMALL] + [g_convw[0:3, :]], 0)
    small_all = _all_gather([small_part], "ag_small_grads")[0]
    small_shapes = [W[n].shape for n in SMALL]
    pieces = _unpack(small_all, small_shapes + [(3, CW)])
    cw8 = CW // NDEV
    conv_mine = lax.dynamic_slice_in_dim(pieces[-1].reshape(NDEV, 3, NDEV, cw8), me, 1, axis=2)[:, :, 0, :]
    sm_parts = _pack(pieces[:-1] + [conv_mine], 1)
    sm_names = SMALL + ['conv_w']
    sm_shapes = small_shapes + [(3, cw8)]
    w_sm = _pack([W[n] for n in SMALL] + [conv_w[0]], 0)
    m_sm = _pack([Mo[n] for n in SMALL] + [m_conv_w[0]], 0)
    v_sm = _pack([Vo[n] for n in SMALL] + [v_conv_w[0]], 0)
    outs_sm = [_unpack(o, sm_shapes) for o in _adam(sm_parts, w_sm, m_sm, v_sm, "adam_small")]
    for k in range(4):
        for n, a in zip(sm_names, outs_sm[k]):
            res[k][n] = a[None] if n == 'conv_w' else a
    return (loss, grad_x[None], *[res[0][n] for n in WEIGHTS], *[res[1][n] for n in WEIGHTS],
            *[res[2][n] for n in WEIGHTS], *[res[3][n] for n in WEIGHTS])
```

```python
import math

import jax
import jax.numpy as jnp
from jax import lax
from jax.experimental import pallas as pl
from jax.experimental.pallas import tpu as pltpu

F32 = jnp.float32
_BF = jnp.bfloat16
EPS = 1e-6
CHUNK = 64
ROPE_THETA = 10000.0
ADAM_LR, ADAM_B1, ADAM_B2, ADAM_EPS, ADAM_WD, ADAM_STEP = 0.001, 0.9, 0.999, 1e-08, 0.01, 10
NDEV = 8
AXES = ("x", "y", "c")
MESH = pl.DeviceIdType.MESH
LANES = 128
NEG = -1e30
LOG2E = math.log2(math.e)
V7X_VMEM_LIMIT = 56 * 1024 * 1024
PACK_C = 1024
ATT_BLOCK = 512
ATT_UNROLL = 4
ADAM_BLOCK_ELEMS = 256 * 1024

WEIGHTS = ['norm_g', 'w_in', 'conv_w', 'w_conv_out', 'mla_q_norm_g', 'w_uq', 'mla_kv_norm_g', 'w_ukv',
           'mla_qn_nope_g', 'mla_qn_rope_g', 'mla_kn_nope_g', 'mla_kn_rope_g', 'w_mla_out', 'mem_norm_g',
           'w_mem_kv', 'mem_qn_g', 'mem_kn_g', 'w_mem_out', 'w_o']
BIG = ['w_in', 'w_conv_out', 'w_uq', 'w_ukv', 'w_mla_out', 'w_mem_kv', 'w_mem_out', 'w_o']
COL_SHARDED = ('w_in', 'w_conv_out', 'w_uq', 'w_ukv', 'w_mem_out')
SMALL = ['norm_g', 'mla_q_norm_g', 'mla_kv_norm_g', 'mla_qn_nope_g', 'mla_qn_rope_g', 'mla_kn_nope_g',
         'mla_kn_rope_g', 'mem_norm_g', 'mem_qn_g', 'mem_kn_g']


def _tile(dim, target, align):
    if dim <= target:
        return dim
    t = target - target % align
    while t > 0:
        if dim % t == 0:
            return t
        t -= align
    raise ValueError(f"no tile for {dim} {target} {align}")


def _cparams(sem):
    return pltpu.CompilerParams(dimension_semantics=sem, vmem_limit_bytes=V7X_VMEM_LIMIT)


def _sig(x):
    return 1.0 / (1.0 + jnp.exp(-x))


def _rms(x, n):
    r = lax.rsqrt(jnp.sum(x * x, axis=-1, keepdims=True) * (1.0 / n) + EPS)
    return x * r, r


def _rms_bwd(xhat, r, g, dy, n):
    dxh = dy * g
    dx = r * (dxh - xhat * (jnp.sum(dxh * xhat, axis=-1, keepdims=True) * (1.0 / n)))
    return dx, dy * xhat


def _rope(x, cosp, sina, sinb):
    return x * cosp + pltpu.roll(x, 96, 1) * sina + pltpu.roll(x, 32, 1) * sinb


def _rope_t(d, cosp, sina, sinb):
    return d * cosp + pltpu.roll(d * sina, 32, 1) + pltpu.roll(d * sinb, 96, 1)


def _dot_nt(a, b):
    return lax.dot_general(a, b, (((1,), (1,)), ((), ())), preferred_element_type=F32)


def _dot_tn(a, b):
    return lax.dot_general(a, b, (((0,), (0,)), ((), ())), preferred_element_type=F32)


def _dot(a, b):
    return jnp.dot(a, b, preferred_element_type=F32)


def _all_gather(shards, name):
    na = len(shards)

    def body(*refs):
        x_refs, out_refs = refs[:na], refs[na:2 * na]
        send_sems, recv_sems, local_sems = refs[2 * na:]
        x, y, c = lax.axis_index("x"), lax.axis_index("y"), lax.axis_index("c")
        me, sibling = (x, y, c), (x, y, 1 - c)
        chips = [(1 - x, y), (x, 1 - y), (1 - x, 1 - y)]

        def rows(a, px, py, pc):
            return out_refs[a].at[4 * px + 2 * py + pc]

        def copy(a, k, block, to, src=None):
            return pltpu.make_async_remote_copy(
                src_ref=rows(a, *block) if src is None else src, dst_ref=rows(a, *block),
                send_sem=send_sems.at[7 * a + k], recv_sem=recv_sems.at[7 * a + k],
                device_id=to, device_id_type=MESH)

        mine = [pltpu.make_async_copy(x_refs[a], rows(a, *me), local_sems.at[a]) for a in range(na)]
        for cp in mine:
            cp.start()
        first = [copy(a, 0, me, sibling, src=x_refs[a]) for a in range(na)]
        first += [copy(a, 1 + j, me, (*chip, c), src=x_refs[a]) for j, chip in enumerate(chips) for a in range(na)]
        for cp in first:
            cp.start()
        passed = []
        for j, chip in enumerate(chips):
            for a in range(na):
                copy(a, 1 + j, (*chip, c), me).wait_recv()
                fwd = copy(a, 4 + j, (*chip, c), sibling)
                fwd.start()
                passed.append(fwd)
        for a in range(na):
            copy(a, 0, sibling, me).wait_recv()
        for j, chip in enumerate(chips):
            for a in range(na):
                copy(a, 4 + j, (*chip, 1 - c), me).wait_recv()
        for cp in first + passed:
            cp.wait_send()
        for cp in mine:
            cp.wait()

    any_spec = pl.BlockSpec(memory_space=pl.ANY)
    return pl.pallas_call(
        body, name=name,
        out_shape=[jax.ShapeDtypeStruct((NDEV,) + s.shape, s.dtype) for s in shards],
        in_specs=[any_spec] * na, out_specs=[any_spec] * na,
        scratch_shapes=[pltpu.SemaphoreType.DMA((7 * na,)), pltpu.SemaphoreType.DMA((7 * na,)),
                        pltpu.SemaphoreType.DMA((na,))],
    )(*shards)


_HBM = pl.BlockSpec(memory_space=pltpu.HBM)
_SEM = pl.BlockSpec(memory_space=pltpu.SEMAPHORE)
_EFFECT = pltpu.SideEffectType.DATAFLOW_SIDE_EFFECTING


def _split_copy(a, k, src_refs, land_refs, send_sems, recv_sems, gather, receive_side):
    x, y, c = lax.axis_index("x"), lax.axis_index("y"), lax.axis_index("c")
    me = 4 * x + 2 * y + c
    tx, ty, tc = x ^ ((k + 1) >> 2 & 1), y ^ ((k + 1) >> 1 & 1), c ^ ((k + 1) & 1)
    peer = 4 * tx + 2 * ty + tc
    return pltpu.make_async_remote_copy(
        src_ref=src_refs[a] if gather else src_refs[a].at[peer],
        dst_ref=land_refs[a].at[peer if receive_side else me],
        send_sem=send_sems.at[7 * a + k], recv_sem=recv_sems.at[7 * a + k],
        device_id=(tx, ty, tc), device_id_type=MESH)


def _split_start(srcs, lands, gather, name, after=None):
    na = len(srcs)
    extra = [] if after is None else [after]

    def body(*refs):
        src_refs, land_refs = refs[:na], refs[na:2 * na]
        send_sems, recv_sems = refs[2 * na + len(extra)], refs[2 * na + len(extra) + 1]
        token = refs[-1]
        for k in range(7):
            for a in range(na):
                _split_copy(a, k, src_refs, land_refs, send_sems, recv_sems, gather, False).start()
        token[...] = jnp.zeros_like(token)

    hbm = [pltpu.HBM(b.shape, b.dtype) for b in list(srcs) + list(lands)]
    outs = pl.pallas_call(
        body, name=name,
        out_shape=(pltpu.SemaphoreType.DMA((7 * na,)), pltpu.SemaphoreType.DMA((7 * na,)), *hbm,
                   jax.ShapeDtypeStruct((8, LANES), F32)),
        in_specs=[_HBM] * (2 * na) + [pl.BlockSpec(memory_space=pl.ANY)] * len(extra),
        out_specs=(_SEM, _SEM, *[_HBM] * (2 * na), pl.BlockSpec(memory_space=pltpu.VMEM)),
        input_output_aliases={j: 2 + j for j in range(2 * na)},
        compiler_params=pltpu.CompilerParams(has_side_effects=_EFFECT),
    )(*[pltpu.with_memory_space_constraint(b, pltpu.HBM) for b in srcs],
      *[pltpu.with_memory_space_constraint(l, pltpu.HBM) for l in lands], *extra)
    return outs[0], outs[1], outs[2:2 + na], outs[2 + na:2 + 2 * na], outs[-1]


def _split_wait(started, gather, after, name):
    send_sems, recv_sems, srcs, lands, _ = started
    na = len(srcs)

    def body(*refs):
        src_refs, land_refs = refs[:na], refs[na:2 * na]
        send_s, recv_s = refs[2 * na], refs[2 * na + 1]
        for k in range(7):
            for a in range(na):
                cp = _split_copy(a, k, src_refs, land_refs, send_s, recv_s, gather, True)
                cp.wait_send()
                cp.wait_recv()

    hbm = [pltpu.HBM(b.shape, b.dtype) for b in list(srcs) + list(lands)]
    outs = pl.pallas_call(
        body, name=name, out_shape=tuple(hbm),
        in_specs=[_HBM] * (2 * na) + [_SEM, _SEM, pl.BlockSpec(memory_space=pl.ANY)],
        out_specs=tuple([_HBM] * (2 * na)),
        input_output_aliases={j: j for j in range(2 * na)},
        compiler_params=pltpu.CompilerParams(has_side_effects=_EFFECT),
    )(*srcs, *lands, send_sems, recv_sems, after)
    return outs[na:]


def _seg_rows(size):
    rows = -(-size // PACK_C)
    return -(-rows // 16) * 16


def _pack(arrs, lead):
    parts = []
    for a in arrs:
        lshape = a.shape[:lead]
        f = a.reshape(lshape + (-1,)).astype(F32)
        rows = _seg_rows(f.shape[-1])
        f = jnp.pad(f, [(0, 0)] * lead + [(0, rows * PACK_C - f.shape[-1])])
        parts.append(f.reshape(lshape + (rows, PACK_C)))
    return jnp.concatenate(parts, axis=lead)


def _unpack(buf, shapes):
    lshape = buf.shape[:-2]
    out, r = [], 0
    for shp in shapes:
        size = math.prod(shp)
        rows = _seg_rows(size)
        seg = buf[..., r:r + rows, :].reshape(lshape + (rows * PACK_C,))[..., :size]
        out.append(seg.reshape(lshape + tuple(shp)))
        r += rows
    return out


def _mm(a, b, *, name, out_dtype, ta=False, bm=1024, bn=1024, bk=2048, after=None, plus=None):
    if ta:
        kdim, m = a.shape
    else:
        m, kdim = a.shape
    k2, n = b.shape
    assert kdim == k2, (a.shape, b.shape)
    bm = _tile(m, bm, LANES if ta else 16)
    bn = _tile(n, bn, LANES)
    bk = _tile(kdim, bk, LANES)
    nk = kdim // bk
    n_after = 0 if after is None else 1
    n_plus = 0 if plus is None else 2

    def kern(a_ref, b_ref, *rest):
        plus_refs = rest[n_after:n_after + n_plus]
        o_ref, scratch = rest[n_after + n_plus], rest[n_after + n_plus + 1:]
        part = _dot_tn(a_ref[...], b_ref[...]) if ta else _dot(a_ref[...], b_ref[...])

        def first(p):
            return p + _dot(plus_refs[0][...], plus_refs[1][...]) if plus is not None else p

        if nk == 1:
            o_ref[...] = first(part).astype(o_ref.dtype)
        else:
            acc = scratch[0] if scratch else o_ref
            k = pl.program_id(2)

            @pl.when(k == 0)
            def _():
                acc[...] = first(jnp.zeros(acc.shape, F32))

            acc[...] += part
            if scratch:
                @pl.when(k == nk - 1)
                def _():
                    o_ref[...] = acc[...].astype(o_ref.dtype)

    a_spec = pl.BlockSpec((bk, bm), lambda i, j, k: (k, i)) if ta else pl.BlockSpec((bm, bk), lambda i, j, k: (i, k))
    extra_specs, extra_args = [], []
    if after is not None:
        extra_specs.append(pl.BlockSpec(after.shape, lambda i, j, k: (0, 0)))
        extra_args.append(after)
    if plus is not None:
        kk = plus[0].shape[1]
        extra_specs += [pl.BlockSpec((bm, kk), lambda i, j, k: (i, 0)), pl.BlockSpec((kk, bn), lambda i, j, k: (0, j))]
        extra_args += list(plus)
    return pl.pallas_call(
        kern, name=name, grid=(m // bm, n // bn, nk),
        in_specs=[a_spec, pl.BlockSpec((bk, bn), lambda i, j, k: (k, j))] + extra_specs,
        out_specs=pl.BlockSpec((bm, bn), lambda i, j, k: (i, j)),
        out_shape=jax.ShapeDtypeStruct((m, n), out_dtype),
        scratch_shapes=[pltpu.VMEM((bm, bn), F32)] if nk > 1 and out_dtype != F32 else [],
        compiler_params=_cparams(("parallel", "parallel", "arbitrary")),
    )(a, b, *extra_args)


def _adam(parts, w_a, m_a, v_a, name):
    rows, cols = w_a.shape
    rb = _tile(rows, max(8, ADAM_BLOCK_ELEMS // cols // 8 * 8), 8)
    bc1 = 1.0 - ADAM_B1 ** ADAM_STEP
    bc2 = 1.0 - ADAM_B2 ** ADAM_STEP

    def adam_kern(p_ref, w_ref, m_ref, v_ref, g_ref, d_ref, nm_ref, nv_ref):
        g = p_ref[0].astype(F32)
        for j in range(1, NDEV):
            g = g + p_ref[j].astype(F32)
        m_new = ADAM_B1 * m_ref[...] + (1.0 - ADAM_B1) * g
        v_new = ADAM_B2 * v_ref[...] + (1.0 - ADAM_B2) * (g * g)
        g_ref[...] = g
        nm_ref[...] = m_new
        nv_ref[...] = v_new
        d_ref[...] = -ADAM_LR * ((m_new / bc1) / (jnp.sqrt(v_new / bc2) + ADAM_EPS) + ADAM_WD * w_ref[...])

    blk = pl.BlockSpec((rb, cols), lambda i: (i, 0))
    return pl.pallas_call(
        adam_kern, name=name, grid=(rows // rb,),
        in_specs=[pl.BlockSpec((NDEV, rb, cols), lambda i: (0, i, 0)), blk, blk, blk],
        out_specs=[blk] * 4, out_shape=[jax.ShapeDtypeStruct((rows, cols), F32)] * 4,
        compiler_params=_cparams(("parallel",)),
    )(parts, w_a, m_a, v_a)


class _Cfg:
    pass


def _config(x, conv_w, w_uq, w_ukv, mla_qn_nope_g, mla_qn_rope_g, mem, mem_qn_g, w_mem_out, w_mla_out):
    c = _Cfg()
    c.N, c.D = x.shape[1], x.shape[2]
    c.CW = conv_w.shape[2] * NDEV
    c.QL, c.KVL = w_uq.shape[1], w_ukv.shape[1]
    c.NOPE, c.ROPE = mla_qn_nope_g.shape[1], mla_qn_rope_g.shape[1]
    c.H = w_uq.shape[2] * NDEV // (c.NOPE + c.ROPE)
    c.V = w_ukv.shape[2] * NDEV // c.H - c.NOPE
    assert c.NOPE == LANES and c.V == LANES and c.ROPE == LANES // 2
    c.HW = 2 * LANES
    c.HV = c.H * c.V
    assert w_mla_out.shape[1] * NDEV == c.HV
    c.M = mem.shape[1]
    c.MHD = mem_qn_g.shape[1]
    c.MW = w_mem_out.shape[1]
    c.MH = c.MW // c.MHD
    c.o_conv = 0
    c.o_mz = 4 * c.CW
    c.o_g = c.o_mz + c.HV
    c.o_mem = c.o_g + 3 * c.D
    c.o_lora = c.o_mem + 2 * c.MW
    c.P = c.o_lora + c.QL + c.KVL
    assert c.o_mz % c.HV == 0 and c.o_g % (3 * c.D) == 0 and c.o_mem % (2 * c.MW) == 0
    assert c.o_lora % (c.QL + c.KVL) == 0 and c.QL % LANES == 0 and c.KVL % LANES == 0
    c.IN = 4 * c.CW + c.QL + c.KVL + c.ROPE + c.HV + 2 * c.MW + 3 * c.D
    c.R = _tile(c.N, 256, 16)
    c.RP = _tile(c.N, 512, 16)
    c.HG = _tile(c.H, 4, 1)
    c.B = _tile(c.N, ATT_BLOCK, CHUNK)
    c.scale = float((c.NOPE + c.ROPE) ** -0.5)
    c.mscale = float(c.MHD ** -0.5)
    return c


def _win_split(w, c):
    o = 0
    parts = {}
    for nm, wd in (('conv', 4 * c.CW), ('lora', c.QL + c.KVL), ('kr', c.ROPE), ('mz', c.HV), ('mem', 2 * c.MW), ('g', 3 * c.D)):
        parts[nm] = w[:, o:o + wd]
        o += wd
    kr = jnp.pad(parts['kr'], ((0, 0), (0, LANES - c.ROPE)))
    return jnp.concatenate([parts['conv'], parts['mz'], parts['g'], parts['mem'], parts['lora']], axis=1), kr


def _win_join(g, g_kr, c):
    conv = g[:, c.o_conv:c.o_mz]
    mz = g[:, c.o_mz:c.o_g]
    gates = g[:, c.o_g:c.o_mem]
    mem = g[:, c.o_mem:c.o_lora]
    lora = g[:, c.o_lora:c.P]
    return jnp.concatenate([conv, lora, g_kr[:, :c.ROPE], mz, mem, gates], axis=1)


def kernel(x, positions, mem, norm_g, w_in, conv_w, w_conv_out, mla_q_norm_g, w_uq, mla_kv_norm_g, w_ukv, mla_qn_nope_g, mla_qn_rope_g, mla_kn_nope_g, mla_kn_rope_g, w_mla_out, mem_norm_g, w_mem_kv, mem_qn_g, mem_kn_g, w_mem_out, w_o, loss_target, m_norm_g, m_w_in, m_conv_w, m_w_conv_out, m_mla_q_norm_g, m_w_uq, m_mla_kv_norm_g, m_w_ukv, m_mla_qn_nope_g, m_mla_qn_rope_g, m_mla_kn_nope_g, m_mla_kn_rope_g, m_w_mla_out, m_mem_norm_g, m_w_mem_kv, m_mem_qn_g, m_mem_kn_g, m_w_mem_out, m_w_o, v_norm_g, v_w_in, v_conv_w, v_w_conv_out, v_mla_q_norm_g, v_w_uq, v_mla_kv_norm_g, v_w_ukv, v_mla_qn_nope_g, v_mla_qn_rope_g, v_mla_kn_nope_g, v_mla_kn_rope_g, v_w_mla_out, v_mem_norm_g, v_w_mem_kv, v_mem_qn_g, v_mem_kn_g, v_w_mem_out, v_w_o):
    args = dict(locals())
    W = {n: args[n] for n in WEIGHTS}
    Mo = {n: args['m_' + n] for n in WEIGHTS}
    Vo = {n: args['v_' + n] for n in WEIGHTS}
    c = _config(x, conv_w, w_uq, w_ukv, mla_qn_nope_g, mla_qn_rope_g, mem, mem_qn_g, w_mem_out, w_mla_out)
    N, D, R, B, H = c.N, c.D, c.R, c.B, c.H
    assert x.shape[0] == 1
    xs = x[0]
    tgt = loss_target[0]
    memx = mem[0]
    me = 4 * lax.axis_index("x") + 2 * lax.axis_index("y") + lax.axis_index("c")
    nr = N // R

    g_win, g_taps = _all_gather([W['w_in'][0].astype(_BF), conv_w[0]], "ag_w_in")
    rest = [n for n in BIG if n != 'w_in']
    shards_r = [W[n][0].astype(_BF) for n in rest]
    lands_r = [lax.dynamic_update_index_in_dim(lax.empty((NDEV,) + s.shape, s.dtype), s[None], me, 0)
               for s in shards_r]
    ag_rest = _split_start(shards_r, lands_r, True, "ag_rest_start", after=g_win)
    win_p, w_kr = _win_split(jnp.transpose(g_win, (1, 0, 2)).reshape(D, -1), c)
    win_pT, w_krT = win_p.T, w_kr.T
    convw = jnp.transpose(g_taps, (1, 0, 2)).reshape(3, c.CW)
    convw8 = jnp.pad(convw, ((0, 5), (0, 0)))

    def rowb(width, cidx):
        return pl.BlockSpec((R, width), lambda i, _c=cidx: (i, _c))

    def fullb(shape):
        nd = len(shape)
        return pl.BlockSpec(shape, lambda *_: (0,) * nd)

    def pad_lanes(g, w=LANES):
        return jnp.pad(g, ((0, 0), (0, w - g.shape[1])))

    def tabs_of(rows):
        return pl.BlockSpec((3, rows, LANES), lambda i, *_: (0, i, 0))

    half = c.ROPE // 2
    inv_freq = jnp.power(ROPE_THETA, -jnp.arange(half, dtype=F32) / half)
    invf = jnp.concatenate([inv_freq, inv_freq, jnp.zeros((LANES - c.ROPE,), F32)])[None, :]
    pos_col = positions[0].astype(F32).reshape(N, 1)

    def rope_tab_kern(pos_ref, invf_ref, o_ref):
        ang = pos_ref[...] * invf_ref[...]
        co, si = jnp.cos(ang), jnp.sin(ang)
        lane = lax.broadcasted_iota(jnp.int32, ang.shape, 1)
        o_ref[0] = jnp.where(lane < c.ROPE, co, 0.0)
        o_ref[1] = jnp.where(lane < half, -si, 0.0)
        o_ref[2] = jnp.where(jnp.logical_and(lane >= half, lane < c.ROPE), si, 0.0)

    tabs = pl.pallas_call(
        rope_tab_kern, name="rope_tab", grid=(nr,),
        in_specs=[pl.BlockSpec((R, 1), lambda i: (i, 0)), fullb((1, LANES))],
        out_specs=tabs_of(R),
        out_shape=jax.ShapeDtypeStruct((3, N, LANES), F32),
        compiler_params=_cparams(("parallel",)),
    )(pos_col, invf)

    def make_rms_kern():
        def rms_fwd_kern(x_ref, g_ref, o_ref):
            xh, _ = _rms(x_ref[...].astype(F32), x_ref.shape[-1])
            o_ref[...] = (xh * g_ref[...]).astype(o_ref.dtype)
        return rms_fwd_kern

    h = pl.pallas_call(
        make_rms_kern(), name="rms_x", grid=(nr,),
        in_specs=[rowb(D, 0), fullb((1, D))], out_specs=rowb(D, 0),
        out_shape=jax.ShapeDtypeStruct((N, D), _BF), compiler_params=_cparams(("parallel",)),
    )(xs, norm_g)

    proj = _mm(h, win_p, name="mm_proj", out_dtype=_BF, after=ag_rest[4])
    kr_raw = _mm(h, w_kr, name="mm_kr", out_dtype=_BF)

    Wf, WfT = {}, {}
    for n, g in zip(rest, _split_wait(ag_rest, True, proj, "ag_rest_wait")):
        if n in COL_SHARDED:
            Wf[n] = jnp.transpose(g, (1, 0, 2)).reshape(g.shape[1], -1)
            WfT[n] = jnp.transpose(g, (0, 2, 1)).reshape(-1, g.shape[1])
        else:
            Wf[n] = g.reshape(-1, g.shape[2])
            WfT[n] = Wf[n].T
    wuq = Wf['w_uq'].reshape(c.QL, H, c.NOPE + c.ROPE)
    wuq_p = jnp.pad(wuq, ((0, 0), (0, 0), (0, c.HW - c.NOPE - c.ROPE))).reshape(c.QL, H * c.HW)
    wuq_pT = wuq_p.T
    wukv, wukvT = Wf['w_ukv'], WfT['w_ukv']
    wco, wmo, wmkv, wmemo, wo = Wf['w_conv_out'], Wf['w_mla_out'], Wf['w_mem_kv'], Wf['w_mem_out'], Wf['w_o']
    wcoT, wmoT, wmkvT, wmemoT, woT = WfT['w_conv_out'], WfT['w_mla_out'], WfT['w_mem_kv'], WfT['w_mem_out'], WfT['w_o']

    CW = c.CW
    conv_blk = c.o_conv // (4 * CW)
    HALO = 16
    rh = R // HALO

    def conv_parts(blk):
        blk = blk.astype(F32)
        return blk[:, 0:CW], blk[:, CW:2 * CW], blk[:, 2 * CW:3 * CW], blk[:, 3 * CW:4 * CW]

    def shifted(cu, prev, i):
        prev = jnp.where(i > 0, prev, 0.0)
        rid = lax.broadcasted_iota(jnp.int32, cu.shape, 0)
        last, last2 = prev[HALO - 1:HALO, :], prev[HALO - 2:HALO - 1, :]
        sh1 = jnp.where(rid == 0, last, pltpu.roll(cu, 1, 0))
        sh2 = jnp.where(rid == 0, last2, jnp.where(rid == 1, last, pltpu.roll(cu, 2, 0)))
        return sh1, sh2

    def conv_fwd_kern(p_ref, prev_ref, w_ref, o_ref):
        i = pl.program_id(0)
        cg, bg, u, z = conv_parts(p_ref[...])
        pc, _, pu, _ = conv_parts(prev_ref[...])
        cu = cg * u
        sh1, sh2 = shifted(cu, pc * pu, i)
        w = w_ref[...]
        conv = w[0:1, :] * sh2 + w[1:2, :] * sh1 + w[2:3, :] * cu
        o_ref[...] = (bg * conv * (z * _sig(z))).astype(o_ref.dtype)

    prev_spec = pl.BlockSpec((HALO, 4 * CW), lambda i: (jnp.maximum(i * rh - 1, 0), conv_blk))
    a_conv = pl.pallas_call(
        conv_fwd_kern, name="conv_fwd", grid=(nr,),
        in_specs=[rowb(4 * CW, conv_blk), prev_spec, fullb((8, CW))],
        out_specs=rowb(CW, 0), out_shape=jax.ShapeDtypeStruct((N, CW), _BF),
        compiler_params=_cparams(("parallel",)),
    )(proj, proj, convw8)
    o_conv = _mm(a_conv, wco, name="mm_oconv", out_dtype=_BF)

    QL, KVL, HW = c.QL, c.KVL, c.HW
    lora_blk = c.o_lora // (QL + KVL)

    def lora_fwd_kern(p_ref, gq_ref, gkv_ref, q_ref, kv_ref):
        blk = p_ref[...].astype(F32)
        qh, _ = _rms(blk[:, :QL], QL)
        kh, _ = _rms(blk[:, QL:], KVL)
        q_ref[...] = (qh * gq_ref[...]).astype(q_ref.dtype)
        kv_ref[...] = (kh * gkv_ref[...]).astype(kv_ref.dtype)

    cqn, ckvn = pl.pallas_call(
        lora_fwd_kern, name="lora_fwd", grid=(nr,),
        in_specs=[rowb(QL + KVL, lora_blk), fullb((1, QL)), fullb((1, KVL))],
        out_specs=[rowb(QL, 0), rowb(KVL, 0)],
        out_shape=[jax.ShapeDtypeStruct((N, QL), _BF), jax.ShapeDtypeStruct((N, KVL), _BF)],
        compiler_params=_cparams(("parallel",)),
    )(proj, mla_q_norm_g, mla_kv_norm_g)
    q_p = _mm(cqn, wuq_p, name="mm_q", out_dtype=_BF)
    kv = _mm(ckvn, wukv, name="mm_kv", out_dtype=_BF)

    g_qn, g_qr = mla_qn_nope_g, pad_lanes(mla_qn_rope_g)
    g_kn, g_kr = mla_kn_nope_g, pad_lanes(mla_kn_rope_g)

    def krope_fwd_kern(p_ref, t_ref, g_ref, o_ref):
        xh, _ = _rms(p_ref[...].astype(F32), c.ROPE)
        o_ref[...] = _rope(xh * g_ref[...], t_ref[0], t_ref[1], t_ref[2]).astype(o_ref.dtype)

    k_rope = pl.pallas_call(
        krope_fwd_kern, name="krope_fwd", grid=(nr,),
        in_specs=[rowb(LANES, 0), tabs_of(R), fullb((1, LANES))],
        out_specs=rowb(LANES, 0), out_shape=jax.ShapeDtypeStruct((N, LANES), _BF),
        compiler_params=_cparams(("parallel",)),
    )(kr_raw, tabs, g_kr)

    RP, HG = c.RP, c.HG
    nrp, nhg = N // RP, H // HG
    heads_in = pl.BlockSpec((RP, HG * HW), lambda i, hg: (i, hg))
    heads_out = pl.BlockSpec((HG, RP, HW), lambda i, hg: (hg, i, 0))

    def q_prep_kern(q_ref, t_ref, gn_ref, gr_ref, o_ref):
        for g in range(HG):
            blk = q_ref[:, g * HW:(g + 1) * HW].astype(F32)
            nh, _ = _rms(blk[:, :LANES], c.NOPE)
            rhat, _ = _rms(blk[:, LANES:], c.ROPE)
            rot = _rope(rhat * gr_ref[...], t_ref[0], t_ref[1], t_ref[2])
            o_ref[g] = (jnp.concatenate([nh * gn_ref[...], rot], axis=1) * (c.scale * LOG2E)).astype(o_ref.dtype)

    q_cat = pl.pallas_call(
        q_prep_kern, name="q_prep", grid=(nrp, nhg),
        in_specs=[heads_in, tabs_of(RP), fullb((1, LANES)), fullb((1, LANES))],
        out_specs=heads_out, out_shape=jax.ShapeDtypeStruct((H, N, HW), _BF),
        compiler_params=_cparams(("parallel", "parallel")),
    )(q_p, tabs, g_qn, g_qr)

    def k_prep_kern(kv_ref, kr_ref, gn_ref, o_ref):
        for g in range(HG):
            kn, _ = _rms(kv_ref[:, g * HW:g * HW + LANES].astype(F32), c.NOPE)
            o_ref[g] = jnp.concatenate([(kn * gn_ref[...]).astype(o_ref.dtype), kr_ref[...]], axis=1)

    k_cat = pl.pallas_call(
        k_prep_kern, name="k_prep", grid=(nrp, nhg),
        in_specs=[heads_in, pl.BlockSpec((RP, LANES), lambda i, hg: (i, 0)), fullb((1, LANES))],
        out_specs=heads_out, out_shape=jax.ShapeDtypeStruct((H, N, HW), _BF),
        compiler_params=_cparams(("parallel", "parallel")),
    )(kv, k_rope, g_kn)

    nb = N // B
    assert CHUNK & (CHUNK - 1) == 0 and B % CHUNK == 0

    def diag_mask(s, row0=0):
        row = lax.broadcasted_iota(jnp.int32, s.shape, 0) + row0
        col = lax.broadcasted_iota(jnp.int32, s.shape, 1)
        shift = CHUNK.bit_length() - 1
        allowed = jnp.right_shift(col, shift) <= jnp.right_shift(row, shift)
        return jnp.where(allowed, s, NEG)

    k_head = pl.BlockSpec((1, N, HW), lambda hh, i: (hh, 0, 0))
    v_head = pl.BlockSpec((N, LANES), lambda hh, i: (0, 2 * hh + 1))
    q_blk = pl.BlockSpec((1, B, HW), lambda hh, i: (hh, i, 0))
    o_blk = pl.BlockSpec((B, LANES), lambda hh, i: (i, hh))
    lse_blk = pl.BlockSpec((1, B, LANES), lambda hh, i: (hh, i, 0))

    def attn_fwd_kern(q_ref, k_ref, v_ref, o_ref, lse_ref, m_sc, acc_sc, s_sc):
        i = pl.program_id(1)
        m_sc[...] = jnp.full(m_sc.shape, NEG, F32)
        acc_sc[...] = jnp.zeros(acc_sc.shape, F32)

        def rows_of(t):
            return pl.ds(pl.multiple_of(t * B, B), B)

        def scores(t, slot):
            s_sc[slot] = _dot_nt(q_ref[0], k_ref[0, rows_of(t), :])

        def softmax_pv(t, slot, masked):
            s = s_sc[slot]
            if masked:
                s = diag_mask(s)
            mt = s[:, 0:LANES]
            for cb in range(1, B // LANES):
                mt = jnp.maximum(mt, s[:, cb * LANES:(cb + 1) * LANES])
            m_prev = m_sc[...]
            m_new = jnp.maximum(m_prev, jnp.max(mt, axis=1, keepdims=True))
            alpha = jnp.exp2(m_prev - m_new)
            p = jnp.concatenate([jnp.exp2(s[:, cb * LANES:(cb + 1) * LANES] - m_new).astype(_BF)
                                 for cb in range(B // LANES)], axis=1)
            v_ones = jnp.concatenate([v_ref[rows_of(t), :], jnp.ones((B, LANES), _BF)], axis=1)
            acc_sc[...] = jnp.concatenate([alpha, alpha], axis=1) * acc_sc[...] + _dot(p, v_ones)
            m_sc[...] = m_new

        scores(0, 0)

        def run(first, count, last_masked):
            for u in range(count):
                if u + 1 < count or not last_masked:
                    scores(first + u + 1, (u + 1) % 2)
                softmax_pv(first + u, u % 2, last_masked and u == count - 1)

        def unrolled(u, carry):
            run(ATT_UNROLL * u, ATT_UNROLL, False)
            return carry

        lax.fori_loop(0, i // ATT_UNROLL, unrolled, 0)
        for rem in range(ATT_UNROLL):
            @pl.when(i % ATT_UNROLL == rem)
            def _(rem=rem):
                run(i - rem, rem + 1, True)

        acc = acc_sc[...]
        o_ref[...] = (acc[:, :LANES] / acc[:, LANES:]).astype(o_ref.dtype)
        lse_ref[0] = m_sc[...] + jnp.log2(acc[:, LANES:])

    mla_y, lse = pl.pallas_call(
        attn_fwd_kern, name="attn_fwd", grid=(H, nb),
        in_specs=[q_blk, k_head, v_head], out_specs=[o_blk, lse_blk],
        out_shape=[jax.ShapeDtypeStruct((N, c.HV), _BF), jax.ShapeDtypeStruct((H, N, LANES), F32)],
        scratch_shapes=[pltpu.VMEM((B, LANES), F32), pltpu.VMEM((B, HW), F32), pltpu.VMEM((2, B, B), F32)],
        compiler_params=_cparams(("parallel", "arbitrary")),
    )(q_cat, k_cat, kv)

    HV = c.HV
    mz_blk = c.o_mz // HV

    def gate_fwd_kern(y_ref, z_ref, o_ref):
        z = z_ref[...].astype(F32)
        o_ref[...] = (y_ref[...].astype(F32) * (z * _sig(z))).astype(o_ref.dtype)

    a_mla = pl.pallas_call(
        gate_fwd_kern, name="gate_mla", grid=(nr,),
        in_specs=[rowb(HV, 0), rowb(HV, mz_blk)], out_specs=rowb(HV, 0),
        out_shape=jax.ShapeDtypeStruct((N, HV), _BF), compiler_params=_cparams(("parallel",)),
    )(mla_y, proj)
    o_mla = _mm(a_mla, wmo, name="mm_omla", out_dtype=_BF)

    M, MW, MH, MHD = c.M, c.MW, c.MH, c.MHD
    memn = pl.pallas_call(
        make_rms_kern(), name="rms_mem",
        grid=(1,), in_specs=[fullb((M, D)), fullb((1, D))], out_specs=fullb((M, D)),
        out_shape=jax.ShapeDtypeStruct((M, D), _BF), compiler_params=_cparams(("arbitrary",)),
    )(memx, mem_norm_g)
    kvm = _mm(memn, wmkv, name="mm_memkv", out_dtype=F32)

    def memk_fwd_kern(kv_ref, g_ref, k_ref, v_ref):
        for hh in range(MH):
            kh, _ = _rms(kv_ref[:, hh * MHD:(hh + 1) * MHD], MHD)
            k_ref[:, hh * MHD:(hh + 1) * MHD] = (kh * g_ref[...]).astype(k_ref.dtype)
        v_ref[...] = kv_ref[:, MW:].astype(v_ref.dtype)

    mem_k, mem_v = pl.pallas_call(
        memk_fwd_kern, name="memk_fwd", grid=(1,),
        in_specs=[fullb((M, 2 * MW)), fullb((1, MHD))], out_specs=[fullb((M, MW)), fullb((M, MW))],
        out_shape=[jax.ShapeDtypeStruct((M, MW), _BF)] * 2, compiler_params=_cparams(("arbitrary",)),
    )(kvm, mem_kn_g)

    mem_blk = c.o_mem // (2 * MW)

    def mem_head(qz_ref, k_ref, v_ref, g_ref, hh):
        sl = slice(hh * MHD, (hh + 1) * MHD)
        qh, r = _rms(qz_ref[:, sl].astype(F32), MHD)
        qn = (qh * g_ref[...]).astype(_BF)
        s = _dot_nt(qn, k_ref[:, sl]) * c.mscale
        e = jnp.exp(s - jnp.max(s, axis=1, keepdims=True))
        p = e / jnp.sum(e, axis=1, keepdims=True)
        y = _dot(p.astype(_BF), v_ref[:, sl])
        z = qz_ref[:, MW + hh * MHD:MW + (hh + 1) * MHD].astype(F32)
        return sl, qh, r, qn, p, y, z

    def mem_fwd_kern(qz_ref, k_ref, v_ref, g_ref, o_ref):
        for hh in range(MH):
            sl, _, _, _, _, y, z = mem_head(qz_ref, k_ref, v_ref, g_ref, hh)
            o_ref[:, sl] = (y * (z * _sig(z))).astype(o_ref.dtype)

    a_mem = pl.pallas_call(
        mem_fwd_kern, name="mem_fwd", grid=(nr,),
        in_specs=[rowb(2 * MW, mem_blk), fullb((M, MW)), fullb((M, MW)), fullb((1, MHD))],
        out_specs=rowb(MW, 0), out_shape=jax.ShapeDtypeStruct((N, MW), _BF),
        compiler_params=_cparams(("parallel",)),
    )(proj, mem_k, mem_v, mem_qn_g)
    o_mem = _mm(a_mem, wmemo, name="mm_omem", out_dtype=_BF)

    g_blk = c.o_g // (3 * D)

    def merge_fwd_kern(g_ref, oc_ref, om_ref, ome_ref, o_ref):
        g = g_ref[...].astype(F32)
        acc = _sig(g[:, :D]) * oc_ref[...].astype(F32)
        acc += _sig(g[:, D:2 * D]) * om_ref[...].astype(F32)
        acc += _sig(g[:, 2 * D:]) * ome_ref[...].astype(F32)
        o_ref[...] = acc.astype(o_ref.dtype)

    merged = pl.pallas_call(
        merge_fwd_kern, name="merge_fwd", grid=(nr,),
        in_specs=[rowb(3 * D, g_blk), rowb(D, 0), rowb(D, 0), rowb(D, 0)], out_specs=rowb(D, 0),
        out_shape=jax.ShapeDtypeStruct((N, D), _BF), compiler_params=_cparams(("parallel",)),
    )(proj, o_conv, o_mla, o_mem)
    y2 = _mm(merged, wo, name="mm_out", out_dtype=F32)

    def loss_kern(x_ref, y_ref, t_ref, dy_ref, dyb_ref, l_ref):
        e = x_ref[...] + y_ref[...] - t_ref[...]
        dy = e * (1.0 / D)
        dy_ref[...] = dy
        dyb_ref[...] = dy.astype(dyb_ref.dtype)

        @pl.when(pl.program_id(0) == 0)
        def _():
            l_ref[...] = jnp.zeros(l_ref.shape, F32)

        l_ref[...] += jnp.sum(e * e, axis=0, keepdims=True)

    dy, dyb, lpart = pl.pallas_call(
        loss_kern, name="loss", grid=(nr,),
        in_specs=[rowb(D, 0)] * 3, out_specs=[rowb(D, 0), rowb(D, 0), fullb((1, D))],
        out_shape=[jax.ShapeDtypeStruct((N, D), F32), jax.ShapeDtypeStruct((N, D), _BF),
                   jax.ShapeDtypeStruct((1, D), F32)],
        compiler_params=_cparams(("arbitrary",)),
    )(xs, y2, tgt)
    loss = lax.psum(jnp.sum(lpart) * (0.5 / D), AXES)

    G = {}
    d_merged = _mm(dyb, woT, name="mm_dmerged", out_dtype=_BF)
    G['w_o'] = _mm(merged, dyb, ta=True, name="mm_dwo", out_dtype=_BF)

    dproj0 = lax.empty((N, c.P), _BF)
    any_spec = pl.BlockSpec(memory_space=pl.ANY)

    def merge_bwd_kern(dp_any, g_ref, dm_ref, oc_ref, om_ref, ome_ref, dg_ref, doc_ref, dom_ref, dome_ref):
        g = g_ref[...].astype(F32)
        dm = dm_ref[...].astype(F32)
        for idx, (o_in, d_out) in enumerate(((oc_ref, doc_ref), (om_ref, dom_ref), (ome_ref, dome_ref))):
            sg = _sig(g[:, idx * D:(idx + 1) * D])
            d_out[...] = (sg * dm).astype(d_out.dtype)
            dg_ref[:, idx * D:(idx + 1) * D] = (dm * o_in[...].astype(F32) * sg * (1.0 - sg)).astype(dg_ref.dtype)

    dproj1, d_oconv, d_omla, d_omem = pl.pallas_call(
        merge_bwd_kern, name="merge_bwd", grid=(nr,),
        in_specs=[any_spec, rowb(3 * D, g_blk), rowb(D, 0), rowb(D, 0), rowb(D, 0), rowb(D, 0)],
        out_specs=[rowb(3 * D, g_blk), rowb(D, 0), rowb(D, 0), rowb(D, 0)],
        out_shape=[jax.ShapeDtypeStruct((N, c.P), _BF)] + [jax.ShapeDtypeStruct((N, D), _BF)] * 3,
        input_output_aliases={0: 0}, compiler_params=_cparams(("parallel",)),
    )(dproj0, proj, d_merged, o_conv, o_mla, o_mem)

    G['w_conv_out'] = _mm(a_conv, d_oconv, ta=True, name="mm_dwco", out_dtype=_BF)
    d_aconv = _mm(d_oconv, wcoT, name="mm_daconv", out_dtype=_BF)
    G['w_mla_out'] = _mm(a_mla, d_omla, ta=True, name="mm_dwmo", out_dtype=_BF)
    d_amla = _mm(d_omla, wmoT, name="mm_damla", out_dtype=_BF)
    G['w_mem_out'] = _mm(a_mem, d_omem, ta=True, name="mm_dwmemo", out_dtype=_BF)
    d_amem = _mm(d_omem, wmemoT, name="mm_damem", out_dtype=_BF)

    def conv_bwd_kern(dp_any, p_ref, prev_ref, next_ref, da_ref, dan_ref, w_ref, o_ref, dw_ref):
        i = pl.program_id(0)
        cg, bg, u, z = conv_parts(p_ref[...])
        pc, _, pu, _ = conv_parts(prev_ref[...])
        _, nbg, _, nz = conv_parts(next_ref[...])
        cu = cg * u
        sh1, sh2 = shifted(cu, pc * pu, i)
        w = w_ref[...]
        conv = w[0:1, :] * sh2 + w[1:2, :] * sh1 + w[2:3, :] * cu
        sg = _sig(z)
        sz = z * sg
        da = da_ref[...].astype(F32)
        dcy = da * sz
        d_z = da * (bg * conv) * (sg * (1.0 + z * (1.0 - sg)))
        d_b = dcy * conv
        dconv = dcy * bg
        dnext = dan_ref[...].astype(F32) * (nz * _sig(nz)) * nbg
        dnext = jnp.where(i < nr - 1, dnext, 0.0)
        rid = lax.broadcasted_iota(jnp.int32, cu.shape, 0)
        up1 = jnp.where(rid == R - 1, dnext[0:1, :], pltpu.roll(dconv, R - 1, 0))
        up2 = jnp.where(rid == R - 2, dnext[0:1, :], jnp.where(rid == R - 1, dnext[1:2, :], pltpu.roll(dconv, R - 2, 0)))
        dcu = w[2:3, :] * dconv + w[1:2, :] * up1 + w[0:1, :] * up2
        o_ref[:, 0:CW] = (dcu * u).astype(o_ref.dtype)
        o_ref[:, CW:2 * CW] = d_b.astype(o_ref.dtype)
        o_ref[:, 2 * CW:3 * CW] = (dcu * cg).astype(o_ref.dtype)
        o_ref[:, 3 * CW:4 * CW] = d_z.astype(o_ref.dtype)

        @pl.when(i == 0)
        def _():
            dw_ref[...] = jnp.zeros(dw_ref.shape, F32)

        dw_ref[0:1, :] += jnp.sum(dconv * sh2, axis=0, keepdims=True)
        dw_ref[1:2, :] += jnp.sum(dconv * sh1, axis=0, keepdims=True)
        dw_ref[2:3, :] += jnp.sum(dconv * cu, axis=0, keepdims=True)

    next_spec = pl.BlockSpec((HALO, 4 * CW), lambda i: (jnp.minimum((i + 1) * rh, N // HALO - 1), conv_blk))
    dan_spec = pl.BlockSpec((HALO, CW), lambda i: (jnp.minimum((i + 1) * rh, N // HALO - 1), 0))
    dproj2, g_convw = pl.pallas_call(
        conv_bwd_kern, name="conv_bwd", grid=(nr,),
        in_specs=[any_spec, rowb(4 * CW, conv_blk), prev_spec, next_spec, rowb(CW, 0), dan_spec, fullb((8, CW))],
        out_specs=[rowb(4 * CW, conv_blk), fullb((8, CW))],
        out_shape=[jax.ShapeDtypeStruct((N, c.P), _BF), jax.ShapeDtypeStruct((8, CW), F32)],
        input_output_aliases={0: 0}, compiler_params=_cparams(("arbitrary",)),
    )(dproj1, proj, proj, proj, d_aconv, d_aconv, convw8)

    def mem_bwd_kern(dp_any, qz_ref, da_ref, k_ref, v_ref, g_ref, o_ref, dk_ref, dv_ref, dg_ref):
        @pl.when(pl.program_id(0) == 0)
        def _():
            dk_ref[...] = jnp.zeros(dk_ref.shape, F32)
            dv_ref[...] = jnp.zeros(dv_ref.shape, F32)
            dg_ref[...] = jnp.zeros(dg_ref.shape, F32)

        for hh in range(MH):
            sl, qh, r, qn, p, y, z = mem_head(qz_ref, k_ref, v_ref, g_ref, hh)
            da = da_ref[:, sl].astype(F32)
            sg = _sig(z)
            dyh = da * (z * sg)
            o_ref[:, MW + hh * MHD:MW + (hh + 1) * MHD] = (da * y * (sg * (1.0 + z * (1.0 - sg)))).astype(o_ref.dtype)
            dyb_h = dyh.astype(_BF)
            dpm = _dot_nt(dyb_h, v_ref[:, sl])
            ds = (p * (dpm - jnp.sum(dpm * p, axis=1, keepdims=True)) * c.mscale).astype(_BF)
            dqn = _dot(ds, k_ref[:, sl])
            dk_ref[:, sl] += _dot_tn(ds, qn)
            dv_ref[:, sl] += _dot_tn(p.astype(_BF), dyb_h)
            dq, dgp = _rms_bwd(qh, r, g_ref[...], dqn, MHD)
            o_ref[:, sl] = dq.astype(o_ref.dtype)
            dg_ref[...] += jnp.sum(dgp, axis=0, keepdims=True)

    dproj3, d_memk, d_memv, g_mem_qn = pl.pallas_call(
        mem_bwd_kern, name="mem_bwd", grid=(nr,),
        in_specs=[any_spec, rowb(2 * MW, mem_blk), rowb(MW, 0), fullb((M, MW)), fullb((M, MW)), fullb((1, MHD))],
        out_specs=[rowb(2 * MW, mem_blk), fullb((M, MW)), fullb((M, MW)), fullb((1, MHD))],
        out_shape=[jax.ShapeDtypeStruct((N, c.P), _BF), jax.ShapeDtypeStruct((M, MW), F32),
                   jax.ShapeDtypeStruct((M, MW), F32), jax.ShapeDtypeStruct((1, MHD), F32)],
        input_output_aliases={0: 0}, compiler_params=_cparams(("arbitrary",)),
    )(dproj2, proj, d_amem, mem_k, mem_v, mem_qn_g)

    def memk_bwd_kern(kv_ref, dk_ref, dv_ref, g_ref, o_ref, dg_ref):
        dg = jnp.zeros((1, MHD), F32)
        for hh in range(MH):
            sl = slice(hh * MHD, (hh + 1) * MHD)
            kh, r = _rms(kv_ref[:, sl], MHD)
            dkr, dgp = _rms_bwd(kh, r, g_ref[...], dk_ref[:, sl], MHD)
            o_ref[:, sl] = dkr.astype(o_ref.dtype)
            dg += jnp.sum(dgp, axis=0, keepdims=True)
        o_ref[:, MW:] = dv_ref[...].astype(o_ref.dtype)
        dg_ref[...] = dg

    d_kvm, g_mem_kn = pl.pallas_call(
        memk_bwd_kern, name="memk_bwd", grid=(1,),
        in_specs=[fullb((M, 2 * MW)), fullb((M, MW)), fullb((M, MW)), fullb((1, MHD))],
        out_specs=[fullb((M, 2 * MW)), fullb((1, MHD))],
        out_shape=[jax.ShapeDtypeStruct((M, 2 * MW), _BF), jax.ShapeDtypeStruct((1, MHD), F32)],
        compiler_params=_cparams(("arbitrary",)),
    )(kvm, d_memk, d_memv, mem_kn_g)
    G['w_mem_kv'] = _mm(memn, d_kvm, ta=True, name="mm_dwmkv", out_dtype=_BF)
    d_memn = _mm(d_kvm, wmkvT, name="mm_dmemn", out_dtype=F32)

    def memnorm_bwd_kern(x_ref, d_ref, dg_ref):
        xh, _ = _rms(x_ref[...], D)
        dg_ref[...] = jnp.sum(d_ref[...] * xh, axis=0, keepdims=True)

    g_mem_norm = pl.pallas_call(
        memnorm_bwd_kern, name="memnorm_bwd", grid=(1,),
        in_specs=[fullb((M, D)), fullb((M, D))], out_specs=fullb((1, D)),
        out_shape=jax.ShapeDtypeStruct((1, D), F32), compiler_params=_cparams(("arbitrary",)),
    )(memx, d_memn)

    def gate_bwd_kern(dp_any, da_ref, y_ref, z_ref, dy_ref, dz_ref):
        z = z_ref[...].astype(F32)
        da = da_ref[...].astype(F32)
        sg = _sig(z)
        dy_ref[...] = (da * (z * sg)).astype(dy_ref.dtype)
        dz_ref[...] = (da * y_ref[...].astype(F32) * (sg * (1.0 + z * (1.0 - sg)))).astype(dz_ref.dtype)

    d_mlay, dproj4 = pl.pallas_call(
        gate_bwd_kern, name="gate_mla_bwd", grid=(nr,),
        in_specs=[any_spec, rowb(HV, 0), rowb(HV, 0), rowb(HV, mz_blk)],
        out_specs=[rowb(HV, 0), rowb(HV, mz_blk)],
        out_shape=[jax.ShapeDtypeStruct((N, HV), _BF), jax.ShapeDtypeStruct((N, c.P), _BF)],
        input_output_aliases={0: 1}, compiler_params=_cparams(("parallel",)),
    )(dproj3, d_amla, mla_y, proj)

    def attn_bwd_kern(q_ref, k_ref, v_ref, o_ref, do_ref, lse_ref, dq_ref, dk_ref, dv_ref, dq_sc, dl_sc):
        i = pl.program_id(1)
        q, do = q_ref[0], do_ref[...]
        delta = jnp.sum(do.astype(F32) * o_ref[...].astype(F32), axis=1, keepdims=True)
        dl_sc[...] = jnp.broadcast_to(delta, dl_sc.shape)
        dq_sc[...] = jnp.zeros(dq_sc.shape, F32)

        def step(off, masked):
            rows = pl.ds(off, B)
            k = k_ref[0, rows, :]
            s = _dot_nt(q, k)
            if masked:
                s = diag_mask(s)
            dpm = _dot_nt(do, v_ref[rows, :])
            lse_t, dl = lse_ref[0], dl_sc[...]
            ps, dss = [], []
            for cb in range(B // LANES):
                cols = slice(cb * LANES, (cb + 1) * LANES)
                p_cb = jnp.exp2(s[:, cols] - lse_t)
                ps.append(p_cb.astype(_BF))
                dss.append((p_cb * (dpm[:, cols] - dl)).astype(_BF))
            p, ds = jnp.concatenate(ps, axis=1), jnp.concatenate(dss, axis=1)
            dvp = _dot_tn(p, do)
            dkp = _dot_tn(ds, q)
            if masked:
                dk_ref[0, rows, :] = dkp
                dv_ref[0, rows, :] = dvp
            else:
                dk_ref[0, rows, :] += dkp
                dv_ref[0, rows, :] += dvp
            dq_sc[...] += _dot(ds, k)

        def unrolled(t, carry):
            for u in range(ATT_UNROLL):
                step(pl.multiple_of((t * ATT_UNROLL + u) * B, B), False)
            return carry

        lax.fori_loop(0, i // ATT_UNROLL, unrolled, 0)
        for u in range(ATT_UNROLL - 1):
            @pl.when(i % ATT_UNROLL > u)
            def _(u=u):
                step(pl.multiple_of((i // ATT_UNROLL * ATT_UNROLL + u) * B, B), False)

        step(pl.multiple_of(i * B, B), True)
        dq_ref[0] = dq_sc[...].astype(dq_ref.dtype)

    d_qcat, d_kcat, d_v = pl.pallas_call(
        attn_bwd_kern, name="attn_bwd", grid=(H, nb),
        in_specs=[q_blk, k_head, v_head, o_blk, o_blk, lse_blk],
        out_specs=[pl.BlockSpec((1, B, HW), lambda hh, i: (hh, i, 0)),
                   pl.BlockSpec((1, N, HW), lambda hh, i: (hh, 0, 0)),
                   pl.BlockSpec((1, N, LANES), lambda hh, i: (hh, 0, 0))],
        out_shape=[jax.ShapeDtypeStruct((H, N, HW), _BF), jax.ShapeDtypeStruct((H, N, HW), F32),
                   jax.ShapeDtypeStruct((H, N, LANES), F32)],
        scratch_shapes=[pltpu.VMEM((B, HW), F32), pltpu.VMEM((B, LANES), F32)],
        compiler_params=_cparams(("parallel", "arbitrary")),
    )(q_cat, k_cat, kv, mla_y, d_mlay, lse)

    def q_prep_bwd_kern(q_ref, dq_ref, t_ref, gn_ref, gr_ref, o_ref, dgn_ref, dgr_ref):
        @pl.when(jnp.logical_and(pl.program_id(0) == 0, pl.program_id(1) == 0))
        def _():
            dgn_ref[...] = jnp.zeros(dgn_ref.shape, F32)
            dgr_ref[...] = jnp.zeros(dgr_ref.shape, F32)

        for g in range(HG):
            blk = q_ref[:, g * HW:(g + 1) * HW].astype(F32)
            d = dq_ref[g].astype(F32) * c.scale
            nh, rn = _rms(blk[:, :LANES], c.NOPE)
            rhat, rr = _rms(blk[:, LANES:], c.ROPE)
            dn, dgn = _rms_bwd(nh, rn, gn_ref[...], d[:, :LANES], c.NOPE)
            drot = _rope_t(d[:, LANES:], t_ref[0], t_ref[1], t_ref[2])
            dr, dgr = _rms_bwd(rhat, rr, gr_ref[...], drot, c.ROPE)
            o_ref[:, g * HW:(g + 1) * HW] = jnp.concatenate([dn, dr], axis=1).astype(o_ref.dtype)
            dgn_ref[...] += jnp.sum(dgn, axis=0, keepdims=True)
            dgr_ref[...] += jnp.sum(dgr, axis=0, keepdims=True)

    d_qp, g_qn_nope, g_qn_rope = pl.pallas_call(
        q_prep_bwd_kern, name="q_prep_bwd", grid=(nrp, nhg),
        in_specs=[heads_in, heads_out, tabs_of(RP), fullb((1, LANES)), fullb((1, LANES))],
        out_specs=[heads_in, fullb((1, LANES)), fullb((1, LANES))],
        out_shape=[jax.ShapeDtypeStruct((N, H * HW), _BF), jax.ShapeDtypeStruct((1, LANES), F32),
                   jax.ShapeDtypeStruct((1, LANES), F32)],
        compiler_params=_cparams(("arbitrary", "arbitrary")),
    )(q_p, d_qcat, tabs, g_qn, g_qr)

    def k_prep_bwd_kern(kv_ref, dk_ref, dv_ref, gn_ref, o_ref, dkr_ref, dgn_ref):
        hg = pl.program_id(1)

        @pl.when(jnp.logical_and(pl.program_id(0) == 0, hg == 0))
        def _():
            dgn_ref[...] = jnp.zeros(dgn_ref.shape, F32)

        @pl.when(hg == 0)
        def _():
            dkr_ref[...] = jnp.zeros(dkr_ref.shape, F32)

        dkr = jnp.zeros((RP, LANES), F32)
        for g in range(HG):
            dk = dk_ref[g] * (1.0 / LOG2E)
            kn, r = _rms(kv_ref[:, g * HW:g * HW + LANES].astype(F32), c.NOPE)
            dkn, dgn = _rms_bwd(kn, r, gn_ref[...], dk[:, :LANES], c.NOPE)
            o_ref[:, g * HW:(g + 1) * HW] = jnp.concatenate([dkn, dv_ref[g]], axis=1).astype(o_ref.dtype)
            dgn_ref[...] += jnp.sum(dgn, axis=0, keepdims=True)
            dkr += dk[:, LANES:]
        dkr_ref[...] += dkr

    d_kv, d_krsum, g_kn_nope = pl.pallas_call(
        k_prep_bwd_kern, name="k_prep_bwd", grid=(nrp, nhg),
        in_specs=[heads_in, heads_out, pl.BlockSpec((HG, RP, LANES), lambda i, hg: (hg, i, 0)), fullb((1, LANES))],
        out_specs=[heads_in, pl.BlockSpec((RP, LANES), lambda i, hg: (i, 0)), fullb((1, LANES))],
        out_shape=[jax.ShapeDtypeStruct((N, H * HW), _BF), jax.ShapeDtypeStruct((N, LANES), F32),
                   jax.ShapeDtypeStruct((1, LANES), F32)],
        compiler_params=_cparams(("arbitrary", "arbitrary")),
    )(kv, d_kcat, d_v, g_kn)

    def krope_bwd_kern(p_ref, d_ref, t_ref, g_ref, o_ref, dg_ref):
        @pl.when(pl.program_id(0) == 0)
        def _():
            dg_ref[...] = jnp.zeros(dg_ref.shape, F32)

        xh, r = _rms(p_ref[...].astype(F32), c.ROPE)
        drot = _rope_t(d_ref[...], t_ref[0], t_ref[1], t_ref[2])
        dx, dg = _rms_bwd(xh, r, g_ref[...], drot, c.ROPE)
        o_ref[...] = dx.astype(o_ref.dtype)
        dg_ref[...] += jnp.sum(dg, axis=0, keepdims=True)

    d_kr, g_kn_rope = pl.pallas_call(
        krope_bwd_kern, name="krope_bwd", grid=(nr,),
        in_specs=[rowb(LANES, 0), rowb(LANES, 0), tabs_of(R), fullb((1, LANES))],
        out_specs=[rowb(LANES, 0), fullb((1, LANES))],
        out_shape=[jax.ShapeDtypeStruct((N, LANES), _BF), jax.ShapeDtypeStruct((1, LANES), F32)],
        compiler_params=_cparams(("arbitrary",)),
    )(kr_raw, d_krsum, tabs, g_kr)
    dproj5 = dproj4

    g_wuq_p = _mm(cqn, d_qp, ta=True, name="mm_dwuq", out_dtype=_BF)
    G['w_uq'] = g_wuq_p.reshape(QL, H, HW)[:, :, :c.NOPE + c.ROPE].reshape(QL, H * (c.NOPE + c.ROPE))
    d_cqn = _mm(d_qp, wuq_pT, name="mm_dcqn", out_dtype=F32)
    G['w_ukv'] = _mm(ckvn, d_kv, ta=True, name="mm_dwukv", out_dtype=_BF)
    d_ckvn = _mm(d_kv, wukvT, name="mm_dckvn", out_dtype=F32)

    def to_blocks(n, g):
        if n in COL_SHARDED:
            return jnp.transpose(g.reshape(g.shape[0], NDEV, -1), (1, 0, 2))
        return g.reshape(NDEV, -1, g.shape[1])

    def landing(b):
        own = lax.dynamic_index_in_dim(b, me, 0, keepdims=True)
        return lax.dynamic_update_index_in_dim(lax.empty(b.shape, b.dtype), own, me, 0)

    early = [n for n in BIG if n != 'w_in']
    blocks_e = [to_blocks(n, G[n]) for n in early]
    xe = _split_start(blocks_e, [landing(b) for b in blocks_e], False, "xchg_early_start")
    gq_after = mla_q_norm_g + xe[4][0:1, 0:1]

    def lora_bwd_kern(dp_any, p_ref, dq_ref, dkv_ref, gq_ref, gkv_ref, o_ref, dgq_ref, dgkv_ref):
        @pl.when(pl.program_id(0) == 0)
        def _():
            dgq_ref[...] = jnp.zeros(dgq_ref.shape, F32)
            dgkv_ref[...] = jnp.zeros(dgkv_ref.shape, F32)

        blk = p_ref[...].astype(F32)
        qh, rq = _rms(blk[:, :QL], QL)
        kh, rk = _rms(blk[:, QL:], KVL)
        dq, dgq = _rms_bwd(qh, rq, gq_ref[...], dq_ref[...], QL)
        dk, dgk = _rms_bwd(kh, rk, gkv_ref[...], dkv_ref[...], KVL)
        o_ref[:, :QL] = dq.astype(o_ref.dtype)
        o_ref[:, QL:] = dk.astype(o_ref.dtype)
        dgq_ref[...] += jnp.sum(dgq, axis=0, keepdims=True)
        dgkv_ref[...] += jnp.sum(dgk, axis=0, keepdims=True)

    dproj6, g_q_norm, g_kv_norm = pl.pallas_call(
        lora_bwd_kern, name="lora_bwd", grid=(nr,),
        in_specs=[any_spec, rowb(QL + KVL, lora_blk), rowb(QL, 0), rowb(KVL, 0), fullb((1, QL)), fullb((1, KVL))],
        out_specs=[rowb(QL + KVL, lora_blk), fullb((1, QL)), fullb((1, KVL))],
        out_shape=[jax.ShapeDtypeStruct((N, c.P), _BF), jax.ShapeDtypeStruct((1, QL), F32),
                   jax.ShapeDtypeStruct((1, KVL), F32)],
        input_output_aliases={0: 0}, compiler_params=_cparams(("arbitrary",)),
    )(dproj5, proj, d_cqn, d_ckvn, gq_after, mla_kv_norm_g)

    g_win_p = _mm(h, dproj6, ta=True, name="mm_dwin", out_dtype=_BF, bk=4096)
    g_wkr = _mm(h, d_kr, ta=True, name="mm_dwkr", out_dtype=_BF, bk=4096)
    G['w_in'] = _win_join(g_win_p, g_wkr, c)
    blocks_w = [to_blocks('w_in', G['w_in'])]
    xw = _split_start(blocks_w, [landing(b) for b in blocks_w], False, "xchg_win_start")
    d_h = _mm(dproj6, win_pT, name="mm_dh", out_dtype=F32, bk=3072, after=xw[4], plus=(d_kr, w_krT))

    def final_bwd_kern(x_ref, g_ref, dh_ref, dy_ref, gx_ref, dg_ref):
        @pl.when(pl.program_id(0) == 0)
        def _():
            dg_ref[...] = jnp.zeros(dg_ref.shape, F32)

        xh, r = _rms(x_ref[...], D)
        dx, dg = _rms_bwd(xh, r, g_ref[...], dh_ref[...], D)
        gx_ref[...] = dy_ref[...] + dx
        dg_ref[...] += jnp.sum(dg, axis=0, keepdims=True)

    grad_x, g_norm = pl.pallas_call(
        final_bwd_kern, name="final_bwd", grid=(nr,),
        in_specs=[rowb(D, 0), fullb((1, D)), rowb(D, 0), rowb(D, 0)],
        out_specs=[rowb(D, 0), fullb((1, D))],
        out_shape=[jax.ShapeDtypeStruct((N, D), F32), jax.ShapeDtypeStruct((1, D), F32)],
        compiler_params=_cparams(("arbitrary",)),
    )(xs, norm_g, d_h, dy)

    recv_e = _split_wait(xe, False, grad_x, "xchg_early_wait")
    recv_w = _split_wait(xw, False, grad_x, "xchg_win_wait")
    res = [{}, {}, {}, {}]
    for n, parts in zip(['w_in'] + early, list(recv_w) + list(recv_e)):
        outs = _adam(parts, W[n][0], Mo[n][0], Vo[n][0], "adam_" + n)
        for k in range(4):
            res[k][n] = outs[k][None]

    small_g = {'norm_g': g_norm, 'mla_q_norm_g': g_q_norm, 'mla_kv_norm_g': g_kv_norm,
               'mla_qn_nope_g': g_qn_nope, 'mla_qn_rope_g': g_qn_rope[:, :c.ROPE], 'mla_kn_nope_g': g_kn_nope,
               'mla_kn_rope_g': g_kn_rope[:, :c.ROPE], 'mem_norm_g': g_mem_norm, 'mem_qn_g': g_mem_qn,
               'mem_kn_g': g_mem_kn}
    small_part = _pack([small_g[n] for n in SMALL] + [g_convw[0:3, :]], 0)
    small_all = _all_gather([small_part], "ag_small_grads")[0]
    small_shapes = [W[n].shape for n in SMALL]
    pieces = _unpack(small_all, small_shapes + [(3, CW)])
    cw8 = CW // NDEV
    conv_mine = lax.dynamic_slice_in_dim(pieces[-1].reshape(NDEV, 3, NDEV, cw8), me, 1, axis=2)[:, :, 0, :]
    sm_parts = _pack(pieces[:-1] + [conv_mine], 1)
    sm_names = SMALL + ['conv_w']
    sm_shapes = small_shapes + [(3, cw8)]
    w_sm = _pack([W[n] for n in SMALL] + [conv_w[0]], 0)
    m_sm = _pack([Mo[n] for n in SMALL] + [m_conv_w[0]], 0)
    v_sm = _pack([Vo[n] for n in SMALL] + [v_conv_w[0]], 0)
    outs_sm = [_unpack(o, sm_shapes) for o in _adam(sm_parts, w_sm, m_sm, v_sm, "adam_small")]
    for k in range(4):
        for n, a in zip(sm_names, outs_sm[k]):
            res[k][n] = a[None] if n == 'conv_w' else a
    return (loss, grad_x[None], *[res[0][n] for n in WEIGHTS], *[res[1][n] for n in WEIGHTS],
            *[res[2][n] for n in WEIGHTS], *[res[3][n] for n in WEIGHTS])
```

```python
import math

import jax
import jax.numpy as jnp
from jax import lax
from jax.experimental import pallas as pl
from jax.experimental.pallas import tpu as pltpu

F32 = jnp.float32
_BF = jnp.bfloat16
EPS = 1e-6
CHUNK = 64
ROPE_THETA = 10000.0
ADAM_LR, ADAM_B1, ADAM_B2, ADAM_EPS, ADAM_WD, ADAM_STEP = 0.001, 0.9, 0.999, 1e-08, 0.01, 10
NDEV = 8
AXES = ("x", "y", "c")
MESH = pl.DeviceIdType.MESH
LANES = 128
NEG = -1e30
LOG2E = math.log2(math.e)
V7X_VMEM_LIMIT = 56 * 1024 * 1024
PACK_C = 1024
ATT_BLOCK = 512
ATT_UNROLL = 4
ADAM_BLOCK_ELEMS = 256 * 1024

WEIGHTS = ['norm_g', 'w_in', 'conv_w', 'w_conv_out', 'mla_q_norm_g', 'w_uq', 'mla_kv_norm_g', 'w_ukv',
           'mla_qn_nope_g', 'mla_qn_rope_g', 'mla_kn_nope_g', 'mla_kn_rope_g', 'w_mla_out', 'mem_norm_g',
           'w_mem_kv', 'mem_qn_g', 'mem_kn_g', 'w_mem_out', 'w_o']
BIG = ['w_in', 'w_conv_out', 'w_uq', 'w_ukv', 'w_mla_out', 'w_mem_kv', 'w_mem_out', 'w_o']
COL_SHARDED = ('w_in', 'w_conv_out', 'w_uq', 'w_ukv', 'w_mem_out')
SMALL = ['norm_g', 'mla_q_norm_g', 'mla_kv_norm_g', 'mla_qn_nope_g', 'mla_qn_rope_g', 'mla_kn_nope_g',
         'mla_kn_rope_g', 'mem_norm_g', 'mem_qn_g', 'mem_kn_g']


def _tile(dim, target, align):
    if dim <= target:
        return dim
    t = target - target % align
    while t > 0:
        if dim % t == 0:
            return t
        t -= align
    raise ValueError(f"no tile for {dim} {target} {align}")


def _cparams(sem):
    return pltpu.CompilerParams(dimension_semantics=sem, vmem_limit_bytes=V7X_VMEM_LIMIT)


def _sig(x):
    return 1.0 / (1.0 + jnp.exp(-x))


def _rms(x, n):
    r = lax.rsqrt(jnp.sum(x * x, axis=-1, keepdims=True) * (1.0 / n) + EPS)
    return x * r, r


def _rms_bwd(xhat, r, g, dy, n):
    dxh = dy * g
    dx = r * (dxh - xhat * (jnp.sum(dxh * xhat, axis=-1, keepdims=True) * (1.0 / n)))
    return dx, dy * xhat


def _rope(x, cosp, sina, sinb):
    return x * cosp + pltpu.roll(x, 96, 1) * sina + pltpu.roll(x, 32, 1) * sinb


def _rope_t(d, cosp, sina, sinb):
    return d * cosp + pltpu.roll(d * sina, 32, 1) + pltpu.roll(d * sinb, 96, 1)


def _dot_nt(a, b):
    return lax.dot_general(a, b, (((1,), (1,)), ((), ())), preferred_element_type=F32)


def _dot_tn(a, b):
    return lax.dot_general(a, b, (((0,), (0,)), ((), ())), preferred_element_type=F32)


def _dot(a, b):
    return jnp.dot(a, b, preferred_element_type=F32)


def _all_gather(shards, name):
    na = len(shards)

    def body(*refs):
        x_refs, out_refs = refs[:na], refs[na:2 * na]
        send_sems, recv_sems, local_sems = refs[2 * na:]
        x, y, c = lax.axis_index("x"), lax.axis_index("y"), lax.axis_index("c")
        me, sibling = (x, y, c), (x, y, 1 - c)
        chips = [(1 - x, y), (x, 1 - y), (1 - x, 1 - y)]

        def rows(a, px, py, pc):
            return out_refs[a].at[4 * px + 2 * py + pc]

        def copy(a, k, block, to, src=None):
            return pltpu.make_async_remote_copy(
                src_ref=rows(a, *block) if src is None else src, dst_ref=rows(a, *block),
                send_sem=send_sems.at[7 * a + k], recv_sem=recv_sems.at[7 * a + k],
                device_id=to, device_id_type=MESH)

        mine = [pltpu.make_async_copy(x_refs[a], rows(a, *me), local_sems.at[a]) for a in range(na)]
        for cp in mine:
            cp.start()
        first = [copy(a, 0, me, sibling, src=x_refs[a]) for a in range(na)]
        first += [copy(a, 1 + j, me, (*chip, c), src=x_refs[a]) for j, chip in enumerate(chips) for a in range(na)]
        for cp in first:
            cp.start()
        passed = []
        for j, chip in enumerate(chips):
            for a in range(na):
                copy(a, 1 + j, (*chip, c), me).wait_recv()
                fwd = copy(a, 4 + j, (*chip, c), sibling)
                fwd.start()
                passed.append(fwd)
        for a in range(na):
            copy(a, 0, sibling, me).wait_recv()
        for j, chip in enumerate(chips):
            for a in range(na):
                copy(a, 4 + j, (*chip, 1 - c), me).wait_recv()
        for cp in first + passed:
            cp.wait_send()
        for cp in mine:
            cp.wait()

    any_spec = pl.BlockSpec(memory_space=pl.ANY)
    return pl.pallas_call(
        body, name=name,
        out_shape=[jax.ShapeDtypeStruct((NDEV,) + s.shape, s.dtype) for s in shards],
        in_specs=[any_spec] * na, out_specs=[any_spec] * na,
        scratch_shapes=[pltpu.SemaphoreType.DMA((7 * na,)), pltpu.SemaphoreType.DMA((7 * na,)),
                        pltpu.SemaphoreType.DMA((na,))],
    )(*shards)


_HBM = pl.BlockSpec(memory_space=pltpu.HBM)
_SEM = pl.BlockSpec(memory_space=pltpu.SEMAPHORE)
_EFFECT = pltpu.SideEffectType.DATAFLOW_SIDE_EFFECTING


def _split_copy(a, k, src_refs, land_refs, send_sems, recv_sems, gather, receive_side):
    x, y, c = lax.axis_index("x"), lax.axis_index("y"), lax.axis_index("c")
    me = 4 * x + 2 * y + c
    tx, ty, tc = x ^ ((k + 1) >> 2 & 1), y ^ ((k + 1) >> 1 & 1), c ^ ((k + 1) & 1)
    peer = 4 * tx + 2 * ty + tc
    return pltpu.make_async_remote_copy(
        src_ref=src_refs[a] if gather else src_refs[a].at[peer],
        dst_ref=land_refs[a].at[peer if receive_side else me],
        send_sem=send_sems.at[7 * a + k], recv_sem=recv_sems.at[7 * a + k],
        device_id=(tx, ty, tc), device_id_type=MESH)


def _split_start(srcs, lands, gather, name, after=None):
    na = len(srcs)
    extra = [] if after is None else [after]

    def body(*refs):
        src_refs, land_refs = refs[:na], refs[na:2 * na]
        send_sems, recv_sems = refs[2 * na + len(extra)], refs[2 * na + len(extra) + 1]
        token = refs[-1]
        for k in range(7):
            for a in range(na):
                _split_copy(a, k, src_refs, land_refs, send_sems, recv_sems, gather, False).start()
        token[...] = jnp.zeros_like(token)

    hbm = [pltpu.HBM(b.shape, b.dtype) for b in list(srcs) + list(lands)]
    outs = pl.pallas_call(
        body, name=name,
        out_shape=(pltpu.SemaphoreType.DMA((7 * na,)), pltpu.SemaphoreType.DMA((7 * na,)), *hbm,
                   jax.ShapeDtypeStruct((8, LANES), F32)),
        in_specs=[_HBM] * (2 * na) + [pl.BlockSpec(memory_space=pl.ANY)] * len(extra),
        out_specs=(_SEM, _SEM, *[_HBM] * (2 * na), pl.BlockSpec(memory_space=pltpu.VMEM)),
        input_output_aliases={j: 2 + j for j in range(2 * na)},
        compiler_params=pltpu.CompilerParams(has_side_effects=_EFFECT),
    )(*[pltpu.with_memory_space_constraint(b, pltpu.HBM) for b in srcs],
      *[pltpu.with_memory_space_constraint(l, pltpu.HBM) for l in lands], *extra)
    return outs[0], outs[1], outs[2:2 + na], outs[2 + na:2 + 2 * na], outs[-1]


def _split_wait(started, gather, after, name):
    send_sems, recv_sems, srcs, lands, _ = started
    na = len(srcs)

    def body(*refs):
        src_refs, land_refs = refs[:na], refs[na:2 * na]
        send_s, recv_s = refs[2 * na], refs[2 * na + 1]
        for k in range(7):
            for a in range(na):
                cp = _split_copy(a, k, src_refs, land_refs, send_s, recv_s, gather, True)
                cp.wait_send()
                cp.wait_recv()

    hbm = [pltpu.HBM(b.shape, b.dtype) for b in list(srcs) + list(lands)]
    outs = pl.pallas_call(
        body, name=name, out_shape=tuple(hbm),
        in_specs=[_HBM] * (2 * na) + [_SEM, _SEM, pl.BlockSpec(memory_space=pl.ANY)],
        out_specs=tuple([_HBM] * (2 * na)),
        input_output_aliases={j: j for j in range(2 * na)},
        compiler_params=pltpu.CompilerParams(has_side_effects=_EFFECT),
    )(*srcs, *lands, send_sems, recv_sems, after)
    return outs[na:]


def _seg_rows(size):
    rows = -(-size // PACK_C)
    return -(-rows // 16) * 16


def _pack(arrs, lead):
    parts = []
    for a in arrs:
        lshape = a.shape[:lead]
        f = a.reshape(lshape + (-1,)).astype(F32)
        rows = _seg_rows(f.shape[-1])
        f = jnp.pad(f, [(0, 0)] * lead + [(0, rows * PACK_C - f.shape[-1])])
        parts.append(f.reshape(lshape + (rows, PACK_C)))
    return jnp.concatenate(parts, axis=lead)


def _unpack(buf, shapes):
    lshape = buf.shape[:-2]
    out, r = [], 0
    for shp in shapes:
        size = math.prod(shp)
        rows = _seg_rows(size)
        seg = buf[..., r:r + rows, :].reshape(lshape + (rows * PACK_C,))[..., :size]
        out.append(seg.reshape(lshape + tuple(shp)))
        r += rows
    return out


def _mm(a, b, *, name, out_dtype, ta=False, bm=1024, bn=1024, bk=2048, after=None, plus=None):
    if ta:
        kdim, m = a.shape
    else:
        m, kdim = a.shape
    k2, n = b.shape
    assert kdim == k2, (a.shape, b.shape)
    bm = _tile(m, bm, LANES if ta else 16)
    bn = _tile(n, bn, LANES)
    bk = _tile(kdim, bk, LANES)
    nk = kdim // bk
    n_after = 0 if after is None else 1
    n_plus = 0 if plus is None else 2

    def kern(a_ref, b_ref, *rest):
        plus_refs = rest[n_after:n_after + n_plus]
        o_ref, scratch = rest[n_after + n_plus], rest[n_after + n_plus + 1:]
        part = _dot_tn(a_ref[...], b_ref[...]) if ta else _dot(a_ref[...], b_ref[...])

        def first(p):
            return p + _dot(plus_refs[0][...], plus_refs[1][...]) if plus is not None else p

        if nk == 1:
            o_ref[...] = first(part).astype(o_ref.dtype)
        else:
            acc = scratch[0] if scratch else o_ref
            k = pl.program_id(2)

            @pl.when(k == 0)
            def _():
                acc[...] = first(jnp.zeros(acc.shape, F32))

            acc[...] += part
            if scratch:
                @pl.when(k == nk - 1)
                def _():
                    o_ref[...] = acc[...].astype(o_ref.dtype)

    a_spec = pl.BlockSpec((bk, bm), lambda i, j, k: (k, i)) if ta else pl.BlockSpec((bm, bk), lambda i, j, k: (i, k))
    extra_specs, extra_args = [], []
    if after is not None:
        extra_specs.append(pl.BlockSpec(after.shape, lambda i, j, k: (0, 0)))
        extra_args.append(after)
    if plus is not None:
        kk = plus[0].shape[1]
        extra_specs += [pl.BlockSpec((bm, kk), lambda i, j, k: (i, 0)), pl.BlockSpec((kk, bn), lambda i, j, k: (0, j))]
        extra_args += list(plus)
    return pl.pallas_call(
        kern, name=name, grid=(m // bm, n // bn, nk),
        in_specs=[a_spec, pl.BlockSpec((bk, bn), lambda i, j, k: (k, j))] + extra_specs,
        out_specs=pl.BlockSpec((bm, bn), lambda i, j, k: (i, j)),
        out_shape=jax.ShapeDtypeStruct((m, n), out_dtype),
        scratch_shapes=[pltpu.VMEM((bm, bn), F32)] if nk > 1 and out_dtype != F32 else [],
        compiler_params=_cparams(("parallel", "parallel", "arbitrary")),
    )(a, b, *extra_args)


def _adam(parts, w_a, m_a, v_a, name):
    rows, cols = w_a.shape
    rb = _tile(rows, max(8, ADAM_BLOCK_ELEMS // cols // 8 * 8), 8)
    bc1 = 1.0 - ADAM_B1 ** ADAM_STEP
    bc2 = 1.0 - ADAM_B2 ** ADAM_STEP

    def adam_kern(p_ref, w_ref, m_ref, v_ref, g_ref, d_ref, nm_ref, nv_ref):
        g = p_ref[0].astype(F32)
        for j in range(1, NDEV):
            g = g + p_ref[j].astype(F32)
        m_new = ADAM_B1 * m_ref[...] + (1.0 - ADAM_B1) * g
        v_new = ADAM_B2 * v_ref[...] + (1.0 - ADAM_B2) * (g * g)
        g_ref[...] = g
        nm_ref[...] = m_new
        nv_ref[...] = v_new
        d_ref[...] = -ADAM_LR * ((m_new / bc1) / (jnp.sqrt(v_new / bc2) + ADAM_EPS) + ADAM_WD * w_ref[...])

    blk = pl.BlockSpec((rb, cols), lambda i: (i, 0))
    return pl.pallas_call(
        adam_kern, name=name, grid=(rows // rb,),
        in_specs=[pl.BlockSpec((NDEV, rb, cols), lambda i: (0, i, 0)), blk, blk, blk],
        out_specs=[blk] * 4, out_shape=[jax.ShapeDtypeStruct((rows, cols), F32)] * 4,
        compiler_params=_cparams(("parallel",)),
    )(parts, w_a, m_a, v_a)


class _Cfg:
    pass


def _config(x, conv_w, w_uq, w_ukv, mla_qn_nope_g, mla_qn_rope_g, mem, mem_qn_g, w_mem_out, w_mla_out):
    c = _Cfg()
    c.N, c.D = x.shape[1], x.shape[2]
    c.CW = conv_w.shape[2] * NDEV
    c.QL, c.KVL = w_uq.shape[1], w_ukv.shape[1]
    c.NOPE, c.ROPE = mla_qn_nope_g.shape[1], mla_qn_rope_g.shape[1]
    c.H = w_uq.shape[2] * NDEV // (c.NOPE + c.ROPE)
    c.V = w_ukv.shape[2] * NDEV // c.H - c.NOPE
    assert c.NOPE == LANES and c.V == LANES and c.ROPE == LANES // 2
    c.HW = 2 * LANES
    c.HV = c.H * c.V
    assert w_mla_out.shape[1] * NDEV == c.HV
    c.M = mem.shape[1]
    c.MHD = mem_qn_g.shape[1]
    c.MW = w_mem_out.shape[1]
    c.MH = c.MW // c.MHD
    c.o_conv = 0
    c.o_mz = 4 * c.CW
    c.o_g = c.o_mz + c.HV
    c.o_mem = c.o_g + 3 * c.D
    c.o_lora = c.o_mem + 2 * c.MW
    c.P = c.o_lora + c.QL + c.KVL
    assert c.o_mz % c.HV == 0 and c.o_g % (3 * c.D) == 0 and c.o_mem % (2 * c.MW) == 0
    assert c.o_lora % (c.QL + c.KVL) == 0 and c.QL % LANES == 0 and c.KVL % LANES == 0
    c.IN = 4 * c.CW + c.QL + c.KVL + c.ROPE + c.HV + 2 * c.MW + 3 * c.D
    c.R = _tile(c.N, 256, 16)
    c.RP = _tile(c.N, 512, 16)
    c.HG = _tile(c.H, 4, 1)
    c.B = _tile(c.N, ATT_BLOCK, CHUNK)
    c.scale = float((c.NOPE + c.ROPE) ** -0.5)
    c.mscale = float(c.MHD ** -0.5)
    return c


def _win_split(w, c):
    o = 0
    parts = {}
    for nm, wd in (('conv', 4 * c.CW), ('lora', c.QL + c.KVL), ('kr', c.ROPE), ('mz', c.HV), ('mem', 2 * c.MW), ('g', 3 * c.D)):
        parts[nm] = w[:, o:o + wd]
        o += wd
    kr = jnp.pad(parts['kr'], ((0, 0), (0, LANES - c.ROPE)))
    return jnp.concatenate([parts['conv'], parts['mz'], parts['g'], parts['mem'], parts['lora']], axis=1), kr


def _win_join(g, g_kr, c):
    conv = g[:, c.o_conv:c.o_mz]
    mz = g[:, c.o_mz:c.o_g]
    gates = g[:, c.o_g:c.o_mem]
    mem = g[:, c.o_mem:c.o_lora]
    lora = g[:, c.o_lora:c.P]
    return jnp.concatenate([conv, lora, g_kr[:, :c.ROPE], mz, mem, gates], axis=1)


def kernel(x, positions, mem, norm_g, w_in, conv_w, w_conv_out, mla_q_norm_g, w_uq, mla_kv_norm_g, w_ukv, mla_qn_nope_g, mla_qn_rope_g, mla_kn_nope_g, mla_kn_rope_g, w_mla_out, mem_norm_g, w_mem_kv, mem_qn_g, mem_kn_g, w_mem_out, w_o, loss_target, m_norm_g, m_w_in, m_conv_w, m_w_conv_out, m_mla_q_norm_g, m_w_uq, m_mla_kv_norm_g, m_w_ukv, m_mla_qn_nope_g, m_mla_qn_rope_g, m_mla_kn_nope_g, m_mla_kn_rope_g, m_w_mla_out, m_mem_norm_g, m_w_mem_kv, m_mem_qn_g, m_mem_kn_g, m_w_mem_out, m_w_o, v_norm_g, v_w_in, v_conv_w, v_w_conv_out, v_mla_q_norm_g, v_w_uq, v_mla_kv_norm_g, v_w_ukv, v_mla_qn_nope_g, v_mla_qn_rope_g, v_mla_kn_nope_g, v_mla_kn_rope_g, v_w_mla_out, v_mem_norm_g, v_w_mem_kv, v_mem_qn_g, v_mem_kn_g, v_w_mem_out, v_w_o):
    args = dict(locals())
    W = {n: args[n] for n in WEIGHTS}
    Mo = {n: args['m_' + n] for n in WEIGHTS}
    Vo = {n: args['v_' + n] for n in WEIGHTS}
    c = _config(x, conv_w, w_uq, w_ukv, mla_qn_nope_g, mla_qn_rope_g, mem, mem_qn_g, w_mem_out, w_mla_out)
    N, D, R, B, H = c.N, c.D, c.R, c.B, c.H
    assert x.shape[0] == 1
    xs = x[0]
    tgt = loss_target[0]
    memx = mem[0]
    me = 4 * lax.axis_index("x") + 2 * lax.axis_index("y") + lax.axis_index("c")
    nr = N // R

    g_win, g_taps = _all_gather([W['w_in'][0].astype(_BF), conv_w[0]], "ag_w_in")
    rest = [n for n in BIG if n != 'w_in']
    shards_r = [W[n][0].astype(_BF) for n in rest]
    lands_r = [lax.dynamic_update_index_in_dim(lax.empty((NDEV,) + s.shape, s.dtype), s[None], me, 0)
               for s in shards_r]
    ag_rest = _split_start(shards_r, lands_r, True, "ag_rest_start", after=g_win)
    win_p, w_kr = _win_split(jnp.transpose(g_win, (1, 0, 2)).reshape(D, -1), c)
    win_pT, w_krT = win_p.T, w_kr.T
    convw = jnp.transpose(g_taps, (1, 0, 2)).reshape(3, c.CW)
    convw8 = jnp.pad(convw, ((0, 5), (0, 0)))

    def rowb(width, cidx):
        return pl.BlockSpec((R, width), lambda i, _c=cidx: (i, _c))

    def fullb(shape):
        nd = len(shape)
        return pl.BlockSpec(shape, lambda *_: (0,) * nd)

    def pad_lanes(g, w=LANES):
        return jnp.pad(g, ((0, 0), (0, w - g.shape[1])))

    def tabs_of(rows):
        return pl.BlockSpec((3, rows, LANES), lambda i, *_: (0, i, 0))

    half = c.ROPE // 2
    inv_freq = jnp.power(ROPE_THETA, -jnp.arange(half, dtype=F32) / half)
    invf = jnp.concatenate([inv_freq, inv_freq, jnp.zeros((LANES - c.ROPE,), F32)])[None, :]
    pos_col = positions[0].astype(F32).reshape(N, 1)

    def rope_tab_kern(pos_ref, invf_ref, o_ref):
        ang = pos_ref[...] * invf_ref[...]
        co, si = jnp.cos(ang), jnp.sin(ang)
        lane = lax.broadcasted_iota(jnp.int32, ang.shape, 1)
        o_ref[0] = jnp.where(lane < c.ROPE, co, 0.0)
        o_ref[1] = jnp.where(lane < half, -si, 0.0)
        o_ref[2] = jnp.where(jnp.logical_and(lane >= half, lane < c.ROPE), si, 0.0)

    tabs = pl.pallas_call(
        rope_tab_kern, name="rope_tab", grid=(nr,),
        in_specs=[pl.BlockSpec((R, 1), lambda i: (i, 0)), fullb((1, LANES))],
        out_specs=tabs_of(R),
        out_shape=jax.ShapeDtypeStruct((3, N, LANES), F32),
        compiler_params=_cparams(("parallel",)),
    )(pos_col, invf)

    def make_rms_kern():
        def rms_fwd_kern(x_ref, g_ref, o_ref):
            xh, _ = _rms(x_ref[...].astype(F32), x_ref.shape[-1])
            o_ref[...] = (xh * g_ref[...]).astype(o_ref.dtype)
        return rms_fwd_kern

    h = pl.pallas_call(
        make_rms_kern(), name="rms_x", grid=(nr,),
        in_specs=[rowb(D, 0), fullb((1, D))], out_specs=rowb(D, 0),
        out_shape=jax.ShapeDtypeStruct((N, D), _BF), compiler_params=_cparams(("parallel",)),
    )(xs, norm_g)

    proj = _mm(h, win_p, name="mm_proj", out_dtype=_BF, after=ag_rest[4])
    kr_raw = _mm(h, w_kr, name="mm_kr", out_dtype=_BF)

    Wf, WfT = {}, {}
    for n, g in zip(rest, _split_wait(ag_rest, True, proj, "ag_rest_wait")):
        if n in COL_SHARDED:
            Wf[n] = jnp.transpose(g, (1, 0, 2)).reshape(g.shape[1], -1)
            WfT[n] = jnp.transpose(g, (0, 2, 1)).reshape(-1, g.shape[1])
        else:
            Wf[n] = g.reshape(-1, g.shape[2])
            WfT[n] = Wf[n].T
    wuq = Wf['w_uq'].reshape(c.QL, H, c.NOPE + c.ROPE)
    wuq_p = jnp.pad(wuq, ((0, 0), (0, 0), (0, c.HW - c.NOPE - c.ROPE))).reshape(c.QL, H * c.HW)
    wuq_pT = wuq_p.T
    wukv, wukvT = Wf['w_ukv'], WfT['w_ukv']
    wco, wmo, wmkv, wmemo, wo = Wf['w_conv_out'], Wf['w_mla_out'], Wf['w_mem_kv'], Wf['w_mem_out'], Wf['w_o']
    wcoT, wmoT, wmkvT, wmemoT, woT = WfT['w_conv_out'], WfT['w_mla_out'], WfT['w_mem_kv'], WfT['w_mem_out'], WfT['w_o']

    CW = c.CW
    conv_blk = c.o_conv // (4 * CW)
    HALO = 16
    rh = R // HALO

    def conv_parts(blk):
        blk = blk.astype(F32)
        return blk[:, 0:CW], blk[:, CW:2 * CW], blk[:, 2 * CW:3 * CW], blk[:, 3 * CW:4 * CW]

    def shifted(cu, prev, i):
        prev = jnp.where(i > 0, prev, 0.0)
        rid = lax.broadcasted_iota(jnp.int32, cu.shape, 0)
        last, last2 = prev[HALO - 1:HALO, :], prev[HALO - 2:HALO - 1, :]
        sh1 = jnp.where(rid == 0, last, pltpu.roll(cu, 1, 0))
        sh2 = jnp.where(rid == 0, last2, jnp.where(rid == 1, last, pltpu.roll(cu, 2, 0)))
        return sh1, sh2

    def conv_fwd_kern(p_ref, prev_ref, w_ref, o_ref):
        i = pl.program_id(0)
        cg, bg, u, z = conv_parts(p_ref[...])
        pc, _, pu, _ = conv_parts(prev_ref[...])
        cu = cg * u
        sh1, sh2 = shifted(cu, pc * pu, i)
        w = w_ref[...]
        conv = w[0:1, :] * sh2 + w[1:2, :] * sh1 + w[2:3, :] * cu
        o_ref[...] = (bg * conv * (z * _sig(z))).astype(o_ref.dtype)

    prev_spec = pl.BlockSpec((HALO, 4 * CW), lambda i: (jnp.maximum(i * rh - 1, 0), conv_blk))
    a_conv = pl.pallas_call(
        conv_fwd_kern, name="conv_fwd", grid=(nr,),
        in_specs=[rowb(4 * CW, conv_blk), prev_spec, fullb((8, CW))],
        out_specs=rowb(CW, 0), out_shape=jax.ShapeDtypeStruct((N, CW), _BF),
        compiler_params=_cparams(("parallel",)),
    )(proj, proj, convw8)
    o_conv = _mm(a_conv, wco, name="mm_oconv", out_dtype=_BF)

    QL, KVL, HW = c.QL, c.KVL, c.HW
    lora_blk = c.o_lora // (QL + KVL)

    def lora_fwd_kern(p_ref, gq_ref, gkv_ref, q_ref, kv_ref):
        blk = p_ref[...].astype(F32)
        qh, _ = _rms(blk[:, :QL], QL)
        kh, _ = _rms(blk[:, QL:], KVL)
        q_ref[...] = (qh * gq_ref[...]).astype(q_ref.dtype)
        kv_ref[...] = (kh * gkv_ref[...]).astype(kv_ref.dtype)

    cqn, ckvn = pl.pallas_call(
        lora_fwd_kern, name="lora_fwd", grid=(nr,),
        in_specs=[rowb(QL + KVL, lora_blk), fullb((1, QL)), fullb((1, KVL))],
        out_specs=[rowb(QL, 0), rowb(KVL, 0)],
        out_shape=[jax.ShapeDtypeStruct((N, QL), _BF), jax.ShapeDtypeStruct((N, KVL), _BF)],
        compiler_params=_cparams(("parallel",)),
    )(proj, mla_q_norm_g, mla_kv_norm_g)
    q_p = _mm(cqn, wuq_p, name="mm_q", out_dtype=_BF)
    kv = _mm(ckvn, wukv, name="mm_kv", out_dtype=_BF)

    g_qn, g_qr = mla_qn_nope_g, pad_lanes(mla_qn_rope_g)
    g_kn, g_kr = mla_kn_nope_g, pad_lanes(mla_kn_rope_g)

    def krope_fwd_kern(p_ref, t_ref, g_ref, o_ref):
        xh, _ = _rms(p_ref[...].astype(F32), c.ROPE)
        o_ref[...] = _rope(xh * g_ref[...], t_ref[0], t_ref[1], t_ref[2]).astype(o_ref.dtype)

    k_rope = pl.pallas_call(
        krope_fwd_kern, name="krope_fwd", grid=(nr,),
        in_specs=[rowb(LANES, 0), tabs_of(R), fullb((1, LANES))],
        out_specs=rowb(LANES, 0), out_shape=jax.ShapeDtypeStruct((N, LANES), _BF),
        compiler_params=_cparams(("parallel",)),
    )(kr_raw, tabs, g_kr)

    RP, HG = c.RP, c.HG
    nrp, nhg = N // RP, H // HG
    heads_in = pl.BlockSpec((RP, HG * HW), lambda i, hg: (i, hg))
    heads_out = pl.BlockSpec((HG, RP, HW), lambda i, hg: (hg, i, 0))

    def q_head(q_ref, t_ref, gn_ref, gr_ref):
        blk = q_ref[...].astype(F32)
        nh, rn = _rms(blk[:, :LANES], c.NOPE)
        rhat, rr = _rms(blk[:, LANES:], c.ROPE)
        rot = _rope(rhat * gr_ref[...], t_ref[0], t_ref[1], t_ref[2])
        qc = (jnp.concatenate([nh * gn_ref[...], rot], axis=1) * (c.scale * LOG2E)).astype(_BF)
        return qc, (nh, rn, rhat, rr)

    def k_prep_kern(kv_ref, kr_ref, gn_ref, o_ref):
        for g in range(HG):
            kn, _ = _rms(kv_ref[:, g * HW:g * HW + LANES].astype(F32), c.NOPE)
            o_ref[g] = jnp.concatenate([(kn * gn_ref[...]).astype(o_ref.dtype), kr_ref[...]], axis=1)

    k_cat = pl.pallas_call(
        k_prep_kern, name="k_prep", grid=(nrp, nhg),
        in_specs=[heads_in, pl.BlockSpec((RP, LANES), lambda i, hg: (i, 0)), fullb((1, LANES))],
        out_specs=heads_out, out_shape=jax.ShapeDtypeStruct((H, N, HW), _BF),
        compiler_params=_cparams(("parallel", "parallel")),
    )(kv, k_rope, g_kn)

    nb = N // B
    assert CHUNK & (CHUNK - 1) == 0 and B % CHUNK == 0

    def diag_mask(s, row0=0):
        row = lax.broadcasted_iota(jnp.int32, s.shape, 0) + row0
        col = lax.broadcasted_iota(jnp.int32, s.shape, 1)
        shift = CHUNK.bit_length() - 1
        allowed = jnp.right_shift(col, shift) <= jnp.right_shift(row, shift)
        return jnp.where(allowed, s, NEG)

    k_head = pl.BlockSpec((1, N, HW), lambda hh, i: (hh, 0, 0))
    v_head = pl.BlockSpec((N, LANES), lambda hh, i: (0, 2 * hh + 1))
    o_blk = pl.BlockSpec((B, LANES), lambda hh, i: (i, hh))
    lse_blk = pl.BlockSpec((1, B, LANES), lambda hh, i: (hh, i, 0))
    tab_blk = pl.BlockSpec((3, B, LANES), lambda hh, i: (0, i, 0))
    qraw_blk = pl.BlockSpec((B, HW), lambda hh, i: (i, hh))
    HV = c.HV
    z_blk = pl.BlockSpec((B, LANES), lambda hh, i: (i, c.o_mz // LANES + hh))

    def attn_fwd_kern(q_ref, t_ref, gn_ref, gr_ref, k_ref, v_ref, z_ref, o_ref, a_ref, lse_ref,
                      m_sc, acc_sc, s_sc, q_sc):
        i = pl.program_id(1)
        m_sc[...] = jnp.full(m_sc.shape, NEG, F32)
        acc_sc[...] = jnp.zeros(acc_sc.shape, F32)
        q_sc[...] = q_head(q_ref, t_ref, gn_ref, gr_ref)[0]

        def rows_of(t):
            return pl.ds(pl.multiple_of(t * B, B), B)

        def scores(t, slot):
            s_sc[slot] = _dot_nt(q_sc[...], k_ref[0, rows_of(t), :])

        def softmax_pv(t, slot, masked):
            s = s_sc[slot]
            if masked:
                s = diag_mask(s)
            mt = s[:, 0:LANES]
            for cb in range(1, B // LANES):
                mt = jnp.maximum(mt, s[:, cb * LANES:(cb + 1) * LANES])
            m_prev = m_sc[...]
            m_new = jnp.maximum(m_prev, jnp.max(mt, axis=1, keepdims=True))
            alpha = jnp.exp2(m_prev - m_new)
            p = jnp.concatenate([jnp.exp2(s[:, cb * LANES:(cb + 1) * LANES] - m_new).astype(_BF)
                                 for cb in range(B // LANES)], axis=1)
            v_ones = jnp.concatenate([v_ref[rows_of(t), :], jnp.ones((B, LANES), _BF)], axis=1)
            acc_sc[...] = jnp.concatenate([alpha, alpha], axis=1) * acc_sc[...] + _dot(p, v_ones)
            m_sc[...] = m_new

        scores(0, 0)

        def run(first, count, last_masked):
            for u in range(count):
                if u + 1 < count or not last_masked:
                    scores(first + u + 1, (u + 1) % 2)
                softmax_pv(first + u, u % 2, last_masked and u == count - 1)

        def unrolled(u, carry):
            run(ATT_UNROLL * u, ATT_UNROLL, False)
            return carry

        lax.fori_loop(0, i // ATT_UNROLL, unrolled, 0)
        for rem in range(ATT_UNROLL):
            @pl.when(i % ATT_UNROLL == rem)
            def _(rem=rem):
                run(i - rem, rem + 1, True)

        acc = acc_sc[...]
        y = acc[:, :LANES] / acc[:, LANES:]
        z = z_ref[...].astype(F32)
        o_ref[...] = y.astype(o_ref.dtype)
        a_ref[...] = (y * (z * _sig(z))).astype(a_ref.dtype)
        lse_ref[0] = m_sc[...] + jnp.log2(acc[:, LANES:])

    mla_y, a_mla, lse = pl.pallas_call(
        attn_fwd_kern, name="attn_fwd", grid=(H, nb),
        in_specs=[qraw_blk, tab_blk, fullb((1, LANES)), fullb((1, LANES)), k_head, v_head, z_blk],
        out_specs=[o_blk, o_blk, lse_blk],
        out_shape=[jax.ShapeDtypeStruct((N, HV), _BF), jax.ShapeDtypeStruct((N, HV), _BF),
                   jax.ShapeDtypeStruct((H, N, LANES), F32)],
        scratch_shapes=[pltpu.VMEM((B, LANES), F32), pltpu.VMEM((B, HW), F32), pltpu.VMEM((2, B, B), F32),
                        pltpu.VMEM((B, HW), _BF)],
        compiler_params=_cparams(("parallel", "arbitrary")),
    )(q_p, tabs, g_qn, g_qr, k_cat, kv, proj)
    o_mla = _mm(a_mla, wmo, name="mm_omla", out_dtype=_BF)

    M, MW, MH, MHD = c.M, c.MW, c.MH, c.MHD
    memn = pl.pallas_call(
        make_rms_kern(), name="rms_mem",
        grid=(1,), in_specs=[fullb((M, D)), fullb((1, D))], out_specs=fullb((M, D)),
        out_shape=jax.ShapeDtypeStruct((M, D), _BF), compiler_params=_cparams(("arbitrary",)),
    )(memx, mem_norm_g)
    kvm = _mm(memn, wmkv, name="mm_memkv", out_dtype=F32)

    def memk_fwd_kern(kv_ref, g_ref, k_ref, v_ref):
        for hh in range(MH):
            kh, _ = _rms(kv_ref[:, hh * MHD:(hh + 1) * MHD], MHD)
            k_ref[:, hh * MHD:(hh + 1) * MHD] = (kh * g_ref[...]).astype(k_ref.dtype)
        v_ref[...] = kv_ref[:, MW:].astype(v_ref.dtype)

    mem_k, mem_v = pl.pallas_call(
        memk_fwd_kern, name="memk_fwd", grid=(1,),
        in_specs=[fullb((M, 2 * MW)), fullb((1, MHD))], out_specs=[fullb((M, MW)), fullb((M, MW))],
        out_shape=[jax.ShapeDtypeStruct((M, MW), _BF)] * 2, compiler_params=_cparams(("arbitrary",)),
    )(kvm, mem_kn_g)

    mem_blk = c.o_mem // (2 * MW)

    def mem_head(qz_ref, k_ref, v_ref, g_ref, hh):
        sl = slice(hh * MHD, (hh + 1) * MHD)
        qh, r = _rms(qz_ref[:, sl].astype(F32), MHD)
        qn = (qh * g_ref[...]).astype(_BF)
        s = _dot_nt(qn, k_ref[:, sl]) * c.mscale
        e = jnp.exp(s - jnp.max(s, axis=1, keepdims=True))
        p = e / jnp.sum(e, axis=1, keepdims=True)
        y = _dot(p.astype(_BF), v_ref[:, sl])
        z = qz_ref[:, MW + hh * MHD:MW + (hh + 1) * MHD].astype(F32)
        return sl, qh, r, qn, p, y, z

    def mem_fwd_kern(qz_ref, k_ref, v_ref, g_ref, o_ref):
        for hh in range(MH):
            sl, _, _, _, _, y, z = mem_head(qz_ref, k_ref, v_ref, g_ref, hh)
            o_ref[:, sl] = (y * (z * _sig(z))).astype(o_ref.dtype)

    a_mem = pl.pallas_call(
        mem_fwd_kern, name="mem_fwd", grid=(nr,),
        in_specs=[rowb(2 * MW, mem_blk), fullb((M, MW)), fullb((M, MW)), fullb((1, MHD))],
        out_specs=rowb(MW, 0), out_shape=jax.ShapeDtypeStruct((N, MW), _BF),
        compiler_params=_cparams(("parallel",)),
    )(proj, mem_k, mem_v, mem_qn_g)
    o_mem = _mm(a_mem, wmemo, name="mm_omem", out_dtype=_BF)

    g_blk = c.o_g // (3 * D)

    def merge_fwd_kern(g_ref, oc_ref, om_ref, ome_ref, o_ref):
        g = g_ref[...].astype(F32)
        acc = _sig(g[:, :D]) * oc_ref[...].astype(F32)
        acc += _sig(g[:, D:2 * D]) * om_ref[...].astype(F32)
        acc += _sig(g[:, 2 * D:]) * ome_ref[...].astype(F32)
        o_ref[...] = acc.astype(o_ref.dtype)

    merged = pl.pallas_call(
        merge_fwd_kern, name="merge_fwd", grid=(nr,),
        in_specs=[rowb(3 * D, g_blk), rowb(D, 0), rowb(D, 0), rowb(D, 0)], out_specs=rowb(D, 0),
        out_shape=jax.ShapeDtypeStruct((N, D), _BF), compiler_params=_cparams(("parallel",)),
    )(proj, o_conv, o_mla, o_mem)
    y2 = _mm(merged, wo, name="mm_out", out_dtype=F32)

    def loss_kern(x_ref, y_ref, t_ref, dy_ref, dyb_ref, l_ref):
        e = x_ref[...] + y_ref[...] - t_ref[...]
        dy = e * (1.0 / D)
        dy_ref[...] = dy
        dyb_ref[...] = dy.astype(dyb_ref.dtype)

        @pl.when(pl.program_id(0) == 0)
        def _():
            l_ref[...] = jnp.zeros(l_ref.shape, F32)

        l_ref[...] += jnp.sum(e * e, axis=0, keepdims=True)

    dy, dyb, lpart = pl.pallas_call(
        loss_kern, name="loss", grid=(nr,),
        in_specs=[rowb(D, 0)] * 3, out_specs=[rowb(D, 0), rowb(D, 0), fullb((1, D))],
        out_shape=[jax.ShapeDtypeStruct((N, D), F32), jax.ShapeDtypeStruct((N, D), _BF),
                   jax.ShapeDtypeStruct((1, D), F32)],
        compiler_params=_cparams(("arbitrary",)),
    )(xs, y2, tgt)
    loss = lax.psum(jnp.sum(lpart) * (0.5 / D), AXES)

    G = {}
    d_merged = _mm(dyb, woT, name="mm_dmerged", out_dtype=_BF)
    G['w_o'] = _mm(merged, dyb, ta=True, name="mm_dwo", out_dtype=_BF)

    dproj0 = lax.empty((N, c.P), _BF)
    any_spec = pl.BlockSpec(memory_space=pl.ANY)

    def merge_bwd_kern(dp_any, g_ref, dm_ref, oc_ref, om_ref, ome_ref, dg_ref, doc_ref, dom_ref, dome_ref):
        g = g_ref[...].astype(F32)
        dm = dm_ref[...].astype(F32)
        for idx, (o_in, d_out) in enumerate(((oc_ref, doc_ref), (om_ref, dom_ref), (ome_ref, dome_ref))):
            sg = _sig(g[:, idx * D:(idx + 1) * D])
            d_out[...] = (sg * dm).astype(d_out.dtype)
            dg_ref[:, idx * D:(idx + 1) * D] = (dm * o_in[...].astype(F32) * sg * (1.0 - sg)).astype(dg_ref.dtype)

    dproj1, d_oconv, d_omla, d_omem = pl.pallas_call(
        merge_bwd_kern, name="merge_bwd", grid=(nr,),
        in_specs=[any_spec, rowb(3 * D, g_blk), rowb(D, 0), rowb(D, 0), rowb(D, 0), rowb(D, 0)],
        out_specs=[rowb(3 * D, g_blk), rowb(D, 0), rowb(D, 0), rowb(D, 0)],
        out_shape=[jax.ShapeDtypeStruct((N, c.P), _BF)] + [jax.ShapeDtypeStruct((N, D), _BF)] * 3,
        input_output_aliases={0: 0}, compiler_params=_cparams(("parallel",)),
    )(dproj0, proj, d_merged, o_conv, o_mla, o_mem)

    G['w_conv_out'] = _mm(a_conv, d_oconv, ta=True, name="mm_dwco", out_dtype=_BF)
    d_aconv = _mm(d_oconv, wcoT, name="mm_daconv", out_dtype=_BF)
    G['w_mla_out'] = _mm(a_mla, d_omla, ta=True, name="mm_dwmo", out_dtype=_BF)
    d_amla = _mm(d_omla, wmoT, name="mm_damla", out_dtype=_BF)
    G['w_mem_out'] = _mm(a_mem, d_omem, ta=True, name="mm_dwmemo", out_dtype=_BF)
    d_amem = _mm(d_omem, wmemoT, name="mm_damem", out_dtype=_BF)

    def conv_bwd_kern(dp_any, p_ref, prev_ref, next_ref, da_ref, dan_ref, w_ref, o_ref, dw_ref):
        i = pl.program_id(0)
        cg, bg, u, z = conv_parts(p_ref[...])
        pc, _, pu, _ = conv_parts(prev_ref[...])
        _, nbg, _, nz = conv_parts(next_ref[...])
        cu = cg * u
        sh1, sh2 = shifted(cu, pc * pu, i)
        w = w_ref[...]
        conv = w[0:1, :] * sh2 + w[1:2, :] * sh1 + w[2:3, :] * cu
        sg = _sig(z)
        sz = z * sg
        da = da_ref[...].astype(F32)
        dcy = da * sz
        d_z = da * (bg * conv) * (sg * (1.0 + z * (1.0 - sg)))
        d_b = dcy * conv
        dconv = dcy * bg
        dnext = dan_ref[...].astype(F32) * (nz * _sig(nz)) * nbg
        dnext = jnp.where(i < nr - 1, dnext, 0.0)
        rid = lax.broadcasted_iota(jnp.int32, cu.shape, 0)
        up1 = jnp.where(rid == R - 1, dnext[0:1, :], pltpu.roll(dconv, R - 1, 0))
        up2 = jnp.where(rid == R - 2, dnext[0:1, :], jnp.where(rid == R - 1, dnext[1:2, :], pltpu.roll(dconv, R - 2, 0)))
        dcu = w[2:3, :] * dconv + w[1:2, :] * up1 + w[0:1, :] * up2
        o_ref[:, 0:CW] = (dcu * u).astype(o_ref.dtype)
        o_ref[:, CW:2 * CW] = d_b.astype(o_ref.dtype)
        o_ref[:, 2 * CW:3 * CW] = (dcu * cg).astype(o_ref.dtype)
        o_ref[:, 3 * CW:4 * CW] = d_z.astype(o_ref.dtype)

        @pl.when(i == 0)
        def _():
            dw_ref[...] = jnp.zeros(dw_ref.shape, F32)

        dw_ref[0:1, :] += jnp.sum(dconv * sh2, axis=0, keepdims=True)
        dw_ref[1:2, :] += jnp.sum(dconv * sh1, axis=0, keepdims=True)
        dw_ref[2:3, :] += jnp.sum(dconv * cu, axis=0, keepdims=True)

    next_spec = pl.BlockSpec((HALO, 4 * CW), lambda i: (jnp.minimum((i + 1) * rh, N // HALO - 1), conv_blk))
    dan_spec = pl.BlockSpec((HALO, CW), lambda i: (jnp.minimum((i + 1) * rh, N // HALO - 1), 0))
    dproj2, g_convw = pl.pallas_call(
        conv_bwd_kern, name="conv_bwd", grid=(nr,),
        in_specs=[any_spec, rowb(4 * CW, conv_blk), prev_spec, next_spec, rowb(CW, 0), dan_spec, fullb((8, CW))],
        out_specs=[rowb(4 * CW, conv_blk), fullb((8, CW))],
        out_shape=[jax.ShapeDtypeStruct((N, c.P), _BF), jax.ShapeDtypeStruct((8, CW), F32)],
        input_output_aliases={0: 0}, compiler_params=_cparams(("arbitrary",)),
    )(dproj1, proj, proj, proj, d_aconv, d_aconv, convw8)

    def mem_bwd_kern(dp_any, qz_ref, da_ref, k_ref, v_ref, g_ref, o_ref, dk_ref, dv_ref, dg_ref):
        @pl.when(pl.program_id(0) == 0)
        def _():
            dk_ref[...] = jnp.zeros(dk_ref.shape, F32)
            dv_ref[...] = jnp.zeros(dv_ref.shape, F32)
            dg_ref[...] = jnp.zeros(dg_ref.shape, F32)

        for hh in range(MH):
            sl, qh, r, qn, p, y, z = mem_head(qz_ref, k_ref, v_ref, g_ref, hh)
            da = da_ref[:, sl].astype(F32)
            sg = _sig(z)
            dyh = da * (z * sg)
            o_ref[:, MW + hh * MHD:MW + (hh + 1) * MHD] = (da * y * (sg * (1.0 + z * (1.0 - sg)))).astype(o_ref.dtype)
            dyb_h = dyh.astype(_BF)
            dpm = _dot_nt(dyb_h, v_ref[:, sl])
            ds = (p * (dpm - jnp.sum(dpm * p, axis=1, keepdims=True)) * c.mscale).astype(_BF)
            dqn = _dot(ds, k_ref[:, sl])
            dk_ref[:, sl] += _dot_tn(ds, qn)
            dv_ref[:, sl] += _dot_tn(p.astype(_BF), dyb_h)
            dq, dgp = _rms_bwd(qh, r, g_ref[...], dqn, MHD)
            o_ref[:, sl] = dq.astype(o_ref.dtype)
            dg_ref[...] += jnp.sum(dgp, axis=0, keepdims=True)

    dproj3, d_memk, d_memv, g_mem_qn = pl.pallas_call(
        mem_bwd_kern, name="mem_bwd", grid=(nr,),
        in_specs=[any_spec, rowb(2 * MW, mem_blk), rowb(MW, 0), fullb((M, MW)), fullb((M, MW)), fullb((1, MHD))],
        out_specs=[rowb(2 * MW, mem_blk), fullb((M, MW)), fullb((M, MW)), fullb((1, MHD))],
        out_shape=[jax.ShapeDtypeStruct((N, c.P), _BF), jax.ShapeDtypeStruct((M, MW), F32),
                   jax.ShapeDtypeStruct((M, MW), F32), jax.ShapeDtypeStruct((1, MHD), F32)],
        input_output_aliases={0: 0}, compiler_params=_cparams(("arbitrary",)),
    )(dproj2, proj, d_amem, mem_k, mem_v, mem_qn_g)

    def memk_bwd_kern(kv_ref, dk_ref, dv_ref, g_ref, o_ref, dg_ref):
        dg = jnp.zeros((1, MHD), F32)
        for hh in range(MH):
            sl = slice(hh * MHD, (hh + 1) * MHD)
            kh, r = _rms(kv_ref[:, sl], MHD)
            dkr, dgp = _rms_bwd(kh, r, g_ref[...], dk_ref[:, sl], MHD)
            o_ref[:, sl] = dkr.astype(o_ref.dtype)
            dg += jnp.sum(dgp, axis=0, keepdims=True)
        o_ref[:, MW:] = dv_ref[...].astype(o_ref.dtype)
        dg_ref[...] = dg

    d_kvm, g_mem_kn = pl.pallas_call(
        memk_bwd_kern, name="memk_bwd", grid=(1,),
        in_specs=[fullb((M, 2 * MW)), fullb((M, MW)), fullb((M, MW)), fullb((1, MHD))],
        out_specs=[fullb((M, 2 * MW)), fullb((1, MHD))],
        out_shape=[jax.ShapeDtypeStruct((M, 2 * MW), _BF), jax.ShapeDtypeStruct((1, MHD), F32)],
        compiler_params=_cparams(("arbitrary",)),
    )(kvm, d_memk, d_memv, mem_kn_g)
    G['w_mem_kv'] = _mm(memn, d_kvm, ta=True, name="mm_dwmkv", out_dtype=_BF)
    d_memn = _mm(d_kvm, wmkvT, name="mm_dmemn", out_dtype=F32)

    def memnorm_bwd_kern(x_ref, d_ref, dg_ref):
        xh, _ = _rms(x_ref[...], D)
        dg_ref[...] = jnp.sum(d_ref[...] * xh, axis=0, keepdims=True)

    g_mem_norm = pl.pallas_call(
        memnorm_bwd_kern, name="memnorm_bwd", grid=(1,),
        in_specs=[fullb((M, D)), fullb((M, D))], out_specs=fullb((1, D)),
        out_shape=jax.ShapeDtypeStruct((1, D), F32), compiler_params=_cparams(("arbitrary",)),
    )(memx, d_memn)

    def attn_bwd_kern(dp_any, q_ref, t_ref, gn_ref, gr_ref, k_ref, v_ref, o_ref, da_ref, z_ref, lse_ref,
                      dz_ref, dqp_ref, dk_ref, dv_ref, dgn_ref, dgr_ref, dq_sc, dl_sc, q_sc, do_sc):
        hh, i = pl.program_id(0), pl.program_id(1)

        @pl.when(jnp.logical_and(hh == 0, i == 0))
        def _():
            dgn_ref[...] = jnp.zeros(dgn_ref.shape, F32)
            dgr_ref[...] = jnp.zeros(dgr_ref.shape, F32)

        q_sc[...] = q_head(q_ref, t_ref, gn_ref, gr_ref)[0]
        z = z_ref[...].astype(F32)
        da = da_ref[...].astype(F32)
        y = o_ref[...].astype(F32)
        sg = _sig(z)
        dz_ref[...] = (da * y * (sg * (1.0 + z * (1.0 - sg)))).astype(dz_ref.dtype)
        do_sc[...] = (da * (z * sg)).astype(_BF)
        delta = jnp.sum(do_sc[...].astype(F32) * y, axis=1, keepdims=True)
        dl_sc[...] = jnp.broadcast_to(delta, dl_sc.shape)
        dq_sc[...] = jnp.zeros(dq_sc.shape, F32)

        def step(t, masked):
            rows = pl.ds(pl.multiple_of(t * B, B), B)
            q, do = q_sc[...], do_sc[...]
            k = k_ref[0, rows, :]
            s = _dot_nt(q, k)
            if masked:
                s = diag_mask(s)
            dpm = _dot_nt(do, v_ref[rows, :])
            lse_t, dl = lse_ref[0], dl_sc[...]
            ps, dss = [], []
            for cb in range(B // LANES):
                cols = slice(cb * LANES, (cb + 1) * LANES)
                p_cb = jnp.exp2(s[:, cols] - lse_t)
                ps.append(p_cb.astype(_BF))
                dss.append((p_cb * (dpm[:, cols] - dl)).astype(_BF))
            p, ds = jnp.concatenate(ps, axis=1), jnp.concatenate(dss, axis=1)
            dvp = _dot_tn(p, do)
            dkp = _dot_tn(ds, q)
            if masked:
                dk_ref[0, rows, :] = dkp
                dv_ref[0, rows, :] = dvp
            else:
                dk_ref[0, rows, :] += dkp
                dv_ref[0, rows, :] += dvp
            dq_sc[...] += _dot(ds, k)

        def unrolled(t, carry):
            for u in range(ATT_UNROLL):
                step(t * ATT_UNROLL + u, False)
            return carry

        lax.fori_loop(0, i // ATT_UNROLL, unrolled, 0)
        for rem in range(ATT_UNROLL):
            @pl.when(i % ATT_UNROLL == rem)
            def _(rem=rem):
                for u in range(rem):
                    step(i - rem + u, False)
                step(i, True)

        _, (nh, rn, rhat, rr) = q_head(q_ref, t_ref, gn_ref, gr_ref)
        d = dq_sc[...] * c.scale
        dn, dgn = _rms_bwd(nh, rn, gn_ref[...], d[:, :LANES], c.NOPE)
        drot = _rope_t(d[:, LANES:], t_ref[0], t_ref[1], t_ref[2])
        dr, dgr = _rms_bwd(rhat, rr, gr_ref[...], drot, c.ROPE)
        dqp_ref[...] = jnp.concatenate([dn, dr], axis=1).astype(dqp_ref.dtype)
        dgn_ref[...] += jnp.sum(dgn, axis=0, keepdims=True)
        dgr_ref[...] += jnp.sum(dgr, axis=0, keepdims=True)

    dproj4, d_qp, d_kcat, d_v, g_qn_nope, g_qn_rope = pl.pallas_call(
        attn_bwd_kern, name="attn_bwd", grid=(H, nb),
        in_specs=[any_spec, qraw_blk, tab_blk, fullb((1, LANES)), fullb((1, LANES)), k_head, v_head, o_blk, o_blk,
                  z_blk, lse_blk],
        out_specs=[z_blk, qraw_blk,
                   pl.BlockSpec((1, N, HW), lambda hh, i: (hh, 0, 0)),
                   pl.BlockSpec((1, N, LANES), lambda hh, i: (hh, 0, 0)),
                   fullb((1, LANES)), fullb((1, LANES))],
        out_shape=[jax.ShapeDtypeStruct((N, c.P), _BF), jax.ShapeDtypeStruct((N, H * HW), _BF),
                   jax.ShapeDtypeStruct((H, N, HW), F32), jax.ShapeDtypeStruct((H, N, LANES), F32),
                   jax.ShapeDtypeStruct((1, LANES), F32), jax.ShapeDtypeStruct((1, LANES), F32)],
        scratch_shapes=[pltpu.VMEM((B, HW), F32), pltpu.VMEM((B, LANES), F32), pltpu.VMEM((B, HW), _BF),
                        pltpu.VMEM((B, LANES), _BF)],
        input_output_aliases={0: 0}, compiler_params=_cparams(("arbitrary", "arbitrary")),
    )(dproj3, q_p, tabs, g_qn, g_qr, k_cat, kv, mla_y, d_amla, proj, lse)

    def k_prep_bwd_kern(kv_ref, dk_ref, dv_ref, gn_ref, o_ref, dkr_ref, dgn_ref):
        hg = pl.program_id(1)

        @pl.when(jnp.logical_and(pl.program_id(0) == 0, hg == 0))
        def _():
            dgn_ref[...] = jnp.zeros(dgn_ref.shape, F32)

        @pl.when(hg == 0)
        def _():
            dkr_ref[...] = jnp.zeros(dkr_ref.shape, F32)

        dkr = jnp.zeros((RP, LANES), F32)
        for g in range(HG):
            dk = dk_ref[g] * (1.0 / LOG2E)
            kn, r = _rms(kv_ref[:, g * HW:g * HW + LANES].astype(F32), c.NOPE)
            dkn, dgn = _rms_bwd(kn, r, gn_ref[...], dk[:, :LANES], c.NOPE)
            o_ref[:, g * HW:(g + 1) * HW] = jnp.concatenate([dkn, dv_ref[g]], axis=1).astype(o_ref.dtype)
            dgn_ref[...] += jnp.sum(dgn, axis=0, keepdims=True)
            dkr += dk[:, LANES:]
        dkr_ref[...] += dkr

    d_kv, d_krsum, g_kn_nope = pl.pallas_call(
        k_prep_bwd_kern, name="k_prep_bwd", grid=(nrp, nhg),
        in_specs=[heads_in, heads_out, pl.BlockSpec((HG, RP, LANES), lambda i, hg: (hg, i, 0)), fullb((1, LANES))],
        out_specs=[heads_in, pl.BlockSpec((RP, LANES), lambda i, hg: (i, 0)), fullb((1, LANES))],
        out_shape=[jax.ShapeDtypeStruct((N, H * HW), _BF), jax.ShapeDtypeStruct((N, LANES), F32),
                   jax.ShapeDtypeStruct((1, LANES), F32)],
        compiler_params=_cparams(("arbitrary", "arbitrary")),
    )(kv, d_kcat, d_v, g_kn)

    def krope_bwd_kern(p_ref, d_ref, t_ref, g_ref, o_ref, dg_ref):
        @pl.when(pl.program_id(0) == 0)
        def _():
            dg_ref[...] = jnp.zeros(dg_ref.shape, F32)

        xh, r = _rms(p_ref[...].astype(F32), c.ROPE)
        drot = _rope_t(d_ref[...], t_ref[0], t_ref[1], t_ref[2])
        dx, dg = _rms_bwd(xh, r, g_ref[...], drot, c.ROPE)
        o_ref[...] = dx.astype(o_ref.dtype)
        dg_ref[...] += jnp.sum(dg, axis=0, keepdims=True)

    d_kr, g_kn_rope = pl.pallas_call(
        krope_bwd_kern, name="krope_bwd", grid=(nr,),
        in_specs=[rowb(LANES, 0), rowb(LANES, 0), tabs_of(R), fullb((1, LANES))],
        out_specs=[rowb(LANES, 0), fullb((1, LANES))],
        out_shape=[jax.ShapeDtypeStruct((N, LANES), _BF), jax.ShapeDtypeStruct((1, LANES), F32)],
        compiler_params=_cparams(("arbitrary",)),
    )(kr_raw, d_krsum, tabs, g_kr)
    dproj5 = dproj4

    g_wuq_p = _mm(cqn, d_qp, ta=True, name="mm_dwuq", out_dtype=_BF)
    G['w_uq'] = g_wuq_p.reshape(QL, H, HW)[:, :, :c.NOPE + c.ROPE].reshape(QL, H * (c.NOPE + c.ROPE))
    d_cqn = _mm(d_qp, wuq_pT, name="mm_dcqn", out_dtype=F32)
    G['w_ukv'] = _mm(ckvn, d_kv, ta=True, name="mm_dwukv", out_dtype=_BF)
    d_ckvn = _mm(d_kv, wukvT, name="mm_dckvn", out_dtype=F32)

    def to_blocks(n, g):
        if n in COL_SHARDED:
            return jnp.transpose(g.reshape(g.shape[0], NDEV, -1), (1, 0, 2))
        return g.reshape(NDEV, -1, g.shape[1])

    def landing(b):
        own = lax.dynamic_index_in_dim(b, me, 0, keepdims=True)
        return lax.dynamic_update_index_in_dim(lax.empty(b.shape, b.dtype), own, me, 0)

    early = [n for n in BIG if n != 'w_in']
    blocks_e = [to_blocks(n, G[n]) for n in early]
    xe = _split_start(blocks_e, [landing(b) for b in blocks_e], False, "xchg_early_start")
    gq_after = mla_q_norm_g + xe[4][0:1, 0:1]

    def lora_bwd_kern(dp_any, p_ref, dq_ref, dkv_ref, gq_ref, gkv_ref, o_ref, dgq_ref, dgkv_ref):
        @pl.when(pl.program_id(0) == 0)
        def _():
            dgq_ref[...] = jnp.zeros(dgq_ref.shape, F32)
            dgkv_ref[...] = jnp.zeros(dgkv_ref.shape, F32)

        blk = p_ref[...].astype(F32)
        qh, rq = _rms(blk[:, :QL], QL)
        kh, rk = _rms(blk[:, QL:], KVL)
        dq, dgq = _rms_bwd(qh, rq, gq_ref[...], dq_ref[...], QL)
        dk, dgk = _rms_bwd(kh, rk, gkv_ref[...], dkv_ref[...], KVL)
        o_ref[:, :QL] = dq.astype(o_ref.dtype)
        o_ref[:, QL:] = dk.astype(o_ref.dtype)
        dgq_ref[...] += jnp.sum(dgq, axis=0, keepdims=True)
        dgkv_ref[...] += jnp.sum(dgk, axis=0, keepdims=True)

    dproj6, g_q_norm, g_kv_norm = pl.pallas_call(
        lora_bwd_kern, name="lora_bwd", grid=(nr,),
        in_specs=[any_spec, rowb(QL + KVL, lora_blk), rowb(QL, 0), rowb(KVL, 0), fullb((1, QL)), fullb((1, KVL))],
        out_specs=[rowb(QL + KVL, lora_blk), fullb((1, QL)), fullb((1, KVL))],
        out_shape=[jax.ShapeDtypeStruct((N, c.P), _BF), jax.ShapeDtypeStruct((1, QL), F32),
                   jax.ShapeDtypeStruct((1, KVL), F32)],
        input_output_aliases={0: 0}, compiler_params=_cparams(("arbitrary",)),
    )(dproj5, proj, d_cqn, d_ckvn, gq_after, mla_kv_norm_g)

    g_win_p = _mm(h, dproj6, ta=True, name="mm_dwin", out_dtype=_BF, bk=4096)
    g_wkr = _mm(h, d_kr, ta=True, name="mm_dwkr", out_dtype=_BF, bk=4096)
    G['w_in'] = _win_join(g_win_p, g_wkr, c)
    blocks_w = [to_blocks('w_in', G['w_in'])]
    xw = _split_start(blocks_w, [landing(b) for b in blocks_w], False, "xchg_win_start")
    d_h = _mm(dproj6, win_pT, name="mm_dh", out_dtype=F32, bk=3072, after=xw[4], plus=(d_kr, w_krT))

    def final_bwd_kern(x_ref, g_ref, dh_ref, dy_ref, gx_ref, dg_ref):
        @pl.when(pl.program_id(0) == 0)
        def _():
            dg_ref[...] = jnp.zeros(dg_ref.shape, F32)

        xh, r = _rms(x_ref[...], D)
        dx, dg = _rms_bwd(xh, r, g_ref[...], dh_ref[...], D)
        gx_ref[...] = dy_ref[...] + dx
        dg_ref[...] += jnp.sum(dg, axis=0, keepdims=True)

    grad_x, g_norm = pl.pallas_call(
        final_bwd_kern, name="final_bwd", grid=(nr,),
        in_specs=[rowb(D, 0), fullb((1, D)), rowb(D, 0), rowb(D, 0)],
        out_specs=[rowb(D, 0), fullb((1, D))],
        out_shape=[jax.ShapeDtypeStruct((N, D), F32), jax.ShapeDtypeStruct((1, D), F32)],
        compiler_params=_cparams(("arbitrary",)),
    )(xs, norm_g, d_h, dy)

    recv_e = _split_wait(xe, False, grad_x, "xchg_early_wait")
    recv_w = _split_wait(xw, False, grad_x, "xchg_win_wait")
    res = [{}, {}, {}, {}]
    for n, parts in zip(['w_in'] + early, list(recv_w) + list(recv_e)):
        outs = _adam(parts, W[n][0], Mo[n][0], Vo[n][0], "adam_" + n)
        for k in range(4):
            res[k][n] = outs[k][None]

    small_g = {'norm_g': g_norm, 'mla_q_norm_g': g_q_norm, 'mla_kv_norm_g': g_kv_norm,
               'mla_qn_nope_g': g_qn_nope, 'mla_qn_rope_g': g_qn_rope[:, :c.ROPE], 'mla_kn_nope_g': g_kn_nope,
               'mla_kn_rope_g': g_kn_rope[:, :c.ROPE], 'mem_norm_g': g_mem_norm, 'mem_qn_g': g_mem_qn,
               'mem_kn_g': g_mem_kn}
    small_part = _pack([small_g[n] for n in SMALL] + [g_convw[0:3, :]], 0)
    small_all = _all_gather([small_part], "ag_small_grads")[0]
    small_shapes = [W[n].shape for n in SMALL]
    pieces = _unpack(small_all, small_shapes + [(3, CW)])
    cw8 = CW // NDEV
    conv_mine = lax.dynamic_slice_in_dim(pieces[-1].reshape(NDEV, 3, NDEV, cw8), me, 1, axis=2)[:, :, 0, :]
    sm_parts = _pack(pieces[:-1] + [conv_mine], 1)
    sm_names = SMALL + ['conv_w']
    sm_shapes = small_shapes + [(3, cw8)]
    w_sm = _pack([W[n] for n in SMALL] + [conv_w[0]], 0)
    m_sm = _pack([Mo[n] for n in SMALL] + [m_conv_w[0]], 0)
    v_sm = _pack([Vo[n] for n in SMALL] + [v_conv_w[0]], 0)
    outs_sm = [_unpack(o, sm_shapes) for o in _adam(sm_parts, w_sm, m_sm, v_sm, "adam_small")]
    for k in range(4):
        for n, a in zip(sm_names, outs_sm[k]):
            res[k][n] = a[None] if n == 'conv_w' else a
    return (loss, grad_x[None], *[res[0][n] for n in WEIGHTS], *[res[1][n] for n in WEIGHTS],
            *[res[2][n] for n in WEIGHTS], *[res[3][n] for n in WEIGHTS])
```

```python
import math

import jax
import jax.numpy as jnp
from jax import lax
from jax.experimental import pallas as pl
from jax.experimental.pallas import tpu as pltpu

F32 = jnp.float32
_BF = jnp.bfloat16
EPS = 1e-6
CHUNK = 64
ROPE_THETA = 10000.0
ADAM_LR, ADAM_B1, ADAM_B2, ADAM_EPS, ADAM_WD, ADAM_STEP = 0.001, 0.9, 0.999, 1e-08, 0.01, 10
NDEV = 8
AXES = ("x", "y", "c")
MESH = pl.DeviceIdType.MESH
LANES = 128
NEG = -1e30
LOG2E = math.log2(math.e)
V7X_VMEM_LIMIT = 56 * 1024 * 1024
PACK_C = 1024
ATT_BLOCK = 512
ATT_UNROLL = 4
ADAM_BLOCK_ELEMS = 256 * 1024

WEIGHTS = ['norm_g', 'w_in', 'conv_w', 'w_conv_out', 'mla_q_norm_g', 'w_uq', 'mla_kv_norm_g', 'w_ukv',
           'mla_qn_nope_g', 'mla_qn_rope_g', 'mla_kn_nope_g', 'mla_kn_rope_g', 'w_mla_out', 'mem_norm_g',
           'w_mem_kv', 'mem_qn_g', 'mem_kn_g', 'w_mem_out', 'w_o']
BIG = ['w_in', 'w_conv_out', 'w_uq', 'w_ukv', 'w_mla_out', 'w_mem_kv', 'w_mem_out', 'w_o']
COL_SHARDED = ('w_in', 'w_conv_out', 'w_uq', 'w_ukv', 'w_mem_out')
SMALL = ['norm_g', 'mla_q_norm_g', 'mla_kv_norm_g', 'mla_qn_nope_g', 'mla_qn_rope_g', 'mla_kn_nope_g',
         'mla_kn_rope_g', 'mem_norm_g', 'mem_qn_g', 'mem_kn_g']


def _tile(dim, target, align):
    if dim <= target:
        return dim
    t = target - target % align
    while t > 0:
        if dim % t == 0:
            return t
        t -= align
    raise ValueError(f"no tile for {dim} {target} {align}")


def _cparams(sem):
    return pltpu.CompilerParams(dimension_semantics=sem, vmem_limit_bytes=V7X_VMEM_LIMIT)


def _sig(x):
    return 1.0 / (1.0 + jnp.exp(-x))


def _rms(x, n):
    r = lax.rsqrt(jnp.sum(x * x, axis=-1, keepdims=True) * (1.0 / n) + EPS)
    return x * r, r


def _rms_bwd(xhat, r, g, dy, n):
    dxh = dy * g
    dx = r * (dxh - xhat * (jnp.sum(dxh * xhat, axis=-1, keepdims=True) * (1.0 / n)))
    return dx, dy * xhat


def _rope(x, cosp, sina, sinb):
    return x * cosp + pltpu.roll(x, 96, 1) * sina + pltpu.roll(x, 32, 1) * sinb


def _rope_t(d, cosp, sina, sinb):
    return d * cosp + pltpu.roll(d * sina, 32, 1) + pltpu.roll(d * sinb, 96, 1)


def _dot_nt(a, b):
    return lax.dot_general(a, b, (((1,), (1,)), ((), ())), preferred_element_type=F32)


def _dot_tn(a, b):
    return lax.dot_general(a, b, (((0,), (0,)), ((), ())), preferred_element_type=F32)


def _dot(a, b):
    return jnp.dot(a, b, preferred_element_type=F32)


def _all_gather(shards, name):
    na = len(shards)

    def body(*refs):
        x_refs, out_refs = refs[:na], refs[na:2 * na]
        send_sems, recv_sems, local_sems = refs[2 * na:]
        x, y, c = lax.axis_index("x"), lax.axis_index("y"), lax.axis_index("c")
        me, sibling = (x, y, c), (x, y, 1 - c)
        chips = [(1 - x, y), (x, 1 - y), (1 - x, 1 - y)]

        def rows(a, px, py, pc):
            return out_refs[a].at[4 * px + 2 * py + pc]

        def copy(a, k, block, to, src=None):
            return pltpu.make_async_remote_copy(
                src_ref=rows(a, *block) if src is None else src, dst_ref=rows(a, *block),
                send_sem=send_sems.at[7 * a + k], recv_sem=recv_sems.at[7 * a + k],
                device_id=to, device_id_type=MESH)

        mine = [pltpu.make_async_copy(x_refs[a], rows(a, *me), local_sems.at[a]) for a in range(na)]
        for cp in mine:
            cp.start()
        first = [copy(a, 0, me, sibling, src=x_refs[a]) for a in range(na)]
        first += [copy(a, 1 + j, me, (*chip, c), src=x_refs[a]) for j, chip in enumerate(chips) for a in range(na)]
        for cp in first:
            cp.start()
        passed = []
        for j, chip in enumerate(chips):
            for a in range(na):
                copy(a, 1 + j, (*chip, c), me).wait_recv()
                fwd = copy(a, 4 + j, (*chip, c), sibling)
                fwd.start()
                passed.append(fwd)
        for a in range(na):
            copy(a, 0, sibling, me).wait_recv()
        for j, chip in enumerate(chips):
            for a in range(na):
                copy(a, 4 + j, (*chip, 1 - c), me).wait_recv()
        for cp in first + passed:
            cp.wait_send()
        for cp in mine:
            cp.wait()

    any_spec = pl.BlockSpec(memory_space=pl.ANY)
    return pl.pallas_call(
        body, name=name,
        out_shape=[jax.ShapeDtypeStruct((NDEV,) + s.shape, s.dtype) for s in shards],
        in_specs=[any_spec] * na, out_specs=[any_spec] * na,
        scratch_shapes=[pltpu.SemaphoreType.DMA((7 * na,)), pltpu.SemaphoreType.DMA((7 * na,)),
                        pltpu.SemaphoreType.DMA((na,))],
    )(*shards)


_HBM = pl.BlockSpec(memory_space=pltpu.HBM)
_SEM = pl.BlockSpec(memory_space=pltpu.SEMAPHORE)
_EFFECT = pltpu.SideEffectType.DATAFLOW_SIDE_EFFECTING


def _split_copy(a, k, src_refs, land_refs, send_sems, recv_sems, gather, receive_side):
    x, y, c = lax.axis_index("x"), lax.axis_index("y"), lax.axis_index("c")
    me = 4 * x + 2 * y + c
    tx, ty, tc = x ^ ((k + 1) >> 2 & 1), y ^ ((k + 1) >> 1 & 1), c ^ ((k + 1) & 1)
    peer = 4 * tx + 2 * ty + tc
    return pltpu.make_async_remote_copy(
        src_ref=src_refs[a] if gather else src_refs[a].at[peer],
        dst_ref=land_refs[a].at[peer if receive_side else me],
        send_sem=send_sems.at[7 * a + k], recv_sem=recv_sems.at[7 * a + k],
        device_id=(tx, ty, tc), device_id_type=MESH)


def _split_start(srcs, lands, gather, name, after=None):
    na = len(srcs)
    extra = [] if after is None else [after]

    def body(*refs):
        src_refs, land_refs = refs[:na], refs[na:2 * na]
        send_sems, recv_sems = refs[2 * na + len(extra)], refs[2 * na + len(extra) + 1]
        token = refs[-1]
        for k in range(7):
            for a in range(na):
                _split_copy(a, k, src_refs, land_refs, send_sems, recv_sems, gather, False).start()
        token[...] = jnp.zeros_like(token)

    hbm = [pltpu.HBM(b.shape, b.dtype) for b in list(srcs) + list(lands)]
    outs = pl.pallas_call(
        body, name=name,
        out_shape=(pltpu.SemaphoreType.DMA((7 * na,)), pltpu.SemaphoreType.DMA((7 * na,)), *hbm,
                   jax.ShapeDtypeStruct((8, LANES), F32)),
        in_specs=[_HBM] * (2 * na) + [pl.BlockSpec(memory_space=pl.ANY)] * len(extra),
        out_specs=(_SEM, _SEM, *[_HBM] * (2 * na), pl.BlockSpec(memory_space=pltpu.VMEM)),
        input_output_aliases={j: 2 + j for j in range(2 * na)},
        compiler_params=pltpu.CompilerParams(has_side_effects=_EFFECT),
    )(*[pltpu.with_memory_space_constraint(b, pltpu.HBM) for b in srcs],
      *[pltpu.with_memory_space_constraint(l, pltpu.HBM) for l in lands], *extra)
    return outs[0], outs[1], outs[2:2 + na], outs[2 + na:2 + 2 * na], outs[-1]


def _split_wait(started, gather, after, name):
    send_sems, recv_sems, srcs, lands, _ = started
    na = len(srcs)

    def body(*refs):
        src_refs, land_refs = refs[:na], refs[na:2 * na]
        send_s, recv_s = refs[2 * na], refs[2 * na + 1]
        for k in range(7):
            for a in range(na):
                cp = _split_copy(a, k, src_refs, land_refs, send_s, recv_s, gather, True)
                cp.wait_send()
                cp.wait_recv()

    hbm = [pltpu.HBM(b.shape, b.dtype) for b in list(srcs) + list(lands)]
    outs = pl.pallas_call(
        body, name=name, out_shape=tuple(hbm),
        in_specs=[_HBM] * (2 * na) + [_SEM, _SEM, pl.BlockSpec(memory_space=pl.ANY)],
        out_specs=tuple([_HBM] * (2 * na)),
        input_output_aliases={j: j for j in range(2 * na)},
        compiler_params=pltpu.CompilerParams(has_side_effects=_EFFECT),
    )(*srcs, *lands, send_sems, recv_sems, after)
    return outs[na:]


def _seg_rows(size):
    rows = -(-size // PACK_C)
    return -(-rows // 16) * 16


def _pack(arrs, lead):
    parts = []
    for a in arrs:
        lshape = a.shape[:lead]
        f = a.reshape(lshape + (-1,)).astype(F32)
        rows = _seg_rows(f.shape[-1])
        f = jnp.pad(f, [(0, 0)] * lead + [(0, rows * PACK_C - f.shape[-1])])
        parts.append(f.reshape(lshape + (rows, PACK_C)))
    return jnp.concatenate(parts, axis=lead)


def _unpack(buf, shapes):
    lshape = buf.shape[:-2]
    out, r = [], 0
    for shp in shapes:
        size = math.prod(shp)
        rows = _seg_rows(size)
        seg = buf[..., r:r + rows, :].reshape(lshape + (rows * PACK_C,))[..., :size]
        out.append(seg.reshape(lshape + tuple(shp)))
        r += rows
    return out


def _mm(a, b, *, name, out_dtype, ta=False, tb=False, bm=1024, bn=1024, bk=2048, after=None, plus=None):
    if ta:
        kdim, m = a.shape
    else:
        m, kdim = a.shape
    n, k2 = b.shape if tb else b.shape[::-1]
    assert kdim == k2 and not (ta and tb), (a.shape, b.shape)
    bm = _tile(m, bm, LANES if ta else 16)
    bn = _tile(n, bn, LANES)
    bk = _tile(kdim, bk, LANES)
    nk = kdim // bk
    n_after = 0 if after is None else 1
    n_plus = 0 if plus is None else 2

    def kern(a_ref, b_ref, *rest):
        plus_refs = rest[n_after:n_after + n_plus]
        o_ref, scratch = rest[n_after + n_plus], rest[n_after + n_plus + 1:]
        part = (_dot_tn if ta else _dot_nt if tb else _dot)(a_ref[...], b_ref[...])

        def first(p):
            return p + _dot(plus_refs[0][...], plus_refs[1][...]) if plus is not None else p

        if nk == 1:
            o_ref[...] = first(part).astype(o_ref.dtype)
        else:
            acc = scratch[0] if scratch else o_ref
            k = pl.program_id(2)

            @pl.when(k == 0)
            def _():
                acc[...] = first(jnp.zeros(acc.shape, F32))

            acc[...] += part
            if scratch:
                @pl.when(k == nk - 1)
                def _():
                    o_ref[...] = acc[...].astype(o_ref.dtype)

    a_spec = pl.BlockSpec((bk, bm), lambda i, j, k: (k, i)) if ta else pl.BlockSpec((bm, bk), lambda i, j, k: (i, k))
    b_spec = pl.BlockSpec((bn, bk), lambda i, j, k: (j, k)) if tb else pl.BlockSpec((bk, bn), lambda i, j, k: (k, j))
    extra_specs, extra_args = [], []
    if after is not None:
        extra_specs.append(pl.BlockSpec(after.shape, lambda i, j, k: (0, 0)))
        extra_args.append(after)
    if plus is not None:
        kk = plus[0].shape[1]
        extra_specs += [pl.BlockSpec((bm, kk), lambda i, j, k: (i, 0)), pl.BlockSpec((kk, bn), lambda i, j, k: (0, j))]
        extra_args += list(plus)
    return pl.pallas_call(
        kern, name=name, grid=(m // bm, n // bn, nk),
        in_specs=[a_spec, b_spec] + extra_specs,
        out_specs=pl.BlockSpec((bm, bn), lambda i, j, k: (i, j)),
        out_shape=jax.ShapeDtypeStruct((m, n), out_dtype),
        scratch_shapes=[pltpu.VMEM((bm, bn), F32)] if nk > 1 and out_dtype != F32 else [],
        compiler_params=_cparams(("parallel", "parallel", "arbitrary")),
    )(a, b, *extra_args)


def _adam(parts, w_a, m_a, v_a, name):
    rows, cols = w_a.shape
    rb = _tile(rows, max(8, ADAM_BLOCK_ELEMS // cols // 8 * 8), 8)
    bc1 = 1.0 - ADAM_B1 ** ADAM_STEP
    bc2 = 1.0 - ADAM_B2 ** ADAM_STEP

    def adam_kern(p_ref, w_ref, m_ref, v_ref, g_ref, d_ref, nm_ref, nv_ref):
        g = p_ref[0].astype(F32)
        for j in range(1, NDEV):
            g = g + p_ref[j].astype(F32)
        m_new = ADAM_B1 * m_ref[...] + (1.0 - ADAM_B1) * g
        v_new = ADAM_B2 * v_ref[...] + (1.0 - ADAM_B2) * (g * g)
        g_ref[...] = g
        nm_ref[...] = m_new
        nv_ref[...] = v_new
        d_ref[...] = -ADAM_LR * ((m_new / bc1) / (jnp.sqrt(v_new / bc2) + ADAM_EPS) + ADAM_WD * w_ref[...])

    blk = pl.BlockSpec((rb, cols), lambda i: (i, 0))
    return pl.pallas_call(
        adam_kern, name=name, grid=(rows // rb,),
        in_specs=[pl.BlockSpec((NDEV, rb, cols), lambda i: (0, i, 0)), blk, blk, blk],
        out_specs=[blk] * 4, out_shape=[jax.ShapeDtypeStruct((rows, cols), F32)] * 4,
        compiler_params=_cparams(("parallel",)),
    )(parts, w_a, m_a, v_a)


class _Cfg:
    pass


def _config(x, conv_w, w_uq, w_ukv, mla_qn_nope_g, mla_qn_rope_g, mem, mem_qn_g, w_mem_out, w_mla_out):
    c = _Cfg()
    c.N, c.D = x.shape[1], x.shape[2]
    c.CW = conv_w.shape[2] * NDEV
    c.QL, c.KVL = w_uq.shape[1], w_ukv.shape[1]
    c.NOPE, c.ROPE = mla_qn_nope_g.shape[1], mla_qn_rope_g.shape[1]
    c.H = w_uq.shape[2] * NDEV // (c.NOPE + c.ROPE)
    c.V = w_ukv.shape[2] * NDEV // c.H - c.NOPE
    assert c.NOPE == LANES and c.V == LANES and c.ROPE == LANES // 2
    c.HW = 2 * LANES
    c.HV = c.H * c.V
    assert w_mla_out.shape[1] * NDEV == c.HV
    c.M = mem.shape[1]
    c.MHD = mem_qn_g.shape[1]
    c.MW = w_mem_out.shape[1]
    c.MH = c.MW // c.MHD
    c.o_conv = 0
    c.o_mz = 4 * c.CW
    c.o_g = c.o_mz + c.HV
    c.o_mem = c.o_g + 3 * c.D
    c.o_lora = c.o_mem + 2 * c.MW
    c.P = c.o_lora + c.QL + c.KVL
    assert c.o_mz % c.HV == 0 and c.o_g % (3 * c.D) == 0 and c.o_mem % (2 * c.MW) == 0
    assert c.o_lora % (c.QL + c.KVL) == 0 and c.QL % LANES == 0 and c.KVL % LANES == 0
    c.IN = 4 * c.CW + c.QL + c.KVL + c.ROPE + c.HV + 2 * c.MW + 3 * c.D
    c.R = _tile(c.N, 256, 16)
    c.RP = _tile(c.N, 512, 16)
    c.HG = _tile(c.H, 4, 1)
    c.B = _tile(c.N, ATT_BLOCK, CHUNK)
    c.scale = float((c.NOPE + c.ROPE) ** -0.5)
    c.mscale = float(c.MHD ** -0.5)
    return c


def _win_split(w, c):
    o = 0
    parts = {}
    for nm, wd in (('conv', 4 * c.CW), ('lora', c.QL + c.KVL), ('kr', c.ROPE), ('mz', c.HV), ('mem', 2 * c.MW), ('g', 3 * c.D)):
        parts[nm] = w[:, o:o + wd]
        o += wd
    kr = jnp.pad(parts['kr'], ((0, 0), (0, LANES - c.ROPE)))
    return jnp.concatenate([parts['conv'], parts['mz'], parts['g'], parts['mem'], parts['lora']], axis=1), kr


def _win_join(g, g_kr, c):
    conv = g[:, c.o_conv:c.o_mz]
    mz = g[:, c.o_mz:c.o_g]
    gates = g[:, c.o_g:c.o_mem]
    mem = g[:, c.o_mem:c.o_lora]
    lora = g[:, c.o_lora:c.P]
    return jnp.concatenate([conv, lora, g_kr[:, :c.ROPE], mz, mem, gates], axis=1)


def kernel(x, positions, mem, norm_g, w_in, conv_w, w_conv_out, mla_q_norm_g, w_uq, mla_kv_norm_g, w_ukv, mla_qn_nope_g, mla_qn_rope_g, mla_kn_nope_g, mla_kn_rope_g, w_mla_out, mem_norm_g, w_mem_kv, mem_qn_g, mem_kn_g, w_mem_out, w_o, loss_target, m_norm_g, m_w_in, m_conv_w, m_w_conv_out, m_mla_q_norm_g, m_w_uq, m_mla_kv_norm_g, m_w_ukv, m_mla_qn_nope_g, m_mla_qn_rope_g, m_mla_kn_nope_g, m_mla_kn_rope_g, m_w_mla_out, m_mem_norm_g, m_w_mem_kv, m_mem_qn_g, m_mem_kn_g, m_w_mem_out, m_w_o, v_norm_g, v_w_in, v_conv_w, v_w_conv_out, v_mla_q_norm_g, v_w_uq, v_mla_kv_norm_g, v_w_ukv, v_mla_qn_nope_g, v_mla_qn_rope_g, v_mla_kn_nope_g, v_mla_kn_rope_g, v_w_mla_out, v_mem_norm_g, v_w_mem_kv, v_mem_qn_g, v_mem_kn_g, v_w_mem_out, v_w_o):
    args = dict(locals())
    W = {n: args[n] for n in WEIGHTS}
    Mo = {n: args['m_' + n] for n in WEIGHTS}
    Vo = {n: args['v_' + n] for n in WEIGHTS}
    c = _config(x, conv_w, w_uq, w_ukv, mla_qn_nope_g, mla_qn_rope_g, mem, mem_qn_g, w_mem_out, w_mla_out)
    N, D, R, B, H = c.N, c.D, c.R, c.B, c.H
    assert x.shape[0] == 1
    xs = x[0]
    tgt = loss_target[0]
    memx = mem[0]
    me = 4 * lax.axis_index("x") + 2 * lax.axis_index("y") + lax.axis_index("c")
    nr = N // R

    g_win, g_taps = _all_gather([W['w_in'][0].astype(_BF), conv_w[0]], "ag_w_in")
    rest = [n for n in BIG if n != 'w_in']
    shards_r = [W[n][0].astype(_BF) for n in rest]
    lands_r = [lax.dynamic_update_index_in_dim(lax.empty((NDEV,) + s.shape, s.dtype), s[None], me, 0)
               for s in shards_r]
    ag_rest = _split_start(shards_r, lands_r, True, "ag_rest_start", after=g_win)
    win_p, w_kr = _win_split(jnp.transpose(g_win, (1, 0, 2)).reshape(D, -1), c)
    w_krT = w_kr.T
    convw = jnp.transpose(g_taps, (1, 0, 2)).reshape(3, c.CW)
    convw8 = jnp.pad(convw, ((0, 5), (0, 0)))

    def rowb(width, cidx):
        return pl.BlockSpec((R, width), lambda i, _c=cidx: (i, _c))

    def fullb(shape):
        nd = len(shape)
        return pl.BlockSpec(shape, lambda *_: (0,) * nd)

    def pad_lanes(g, w=LANES):
        return jnp.pad(g, ((0, 0), (0, w - g.shape[1])))

    def tabs_of(rows):
        return pl.BlockSpec((3, rows, LANES), lambda i, *_: (0, i, 0))

    half = c.ROPE // 2
    inv_freq = jnp.power(ROPE_THETA, -jnp.arange(half, dtype=F32) / half)
    invf = jnp.concatenate([inv_freq, inv_freq, jnp.zeros((LANES - c.ROPE,), F32)])[None, :]
    pos_col = positions[0].astype(F32).reshape(N, 1)

    def rope_tab_kern(pos_ref, invf_ref, o_ref):
        ang = pos_ref[...] * invf_ref[...]
        co, si = jnp.cos(ang), jnp.sin(ang)
        lane = lax.broadcasted_iota(jnp.int32, ang.shape, 1)
        o_ref[0] = jnp.where(lane < c.ROPE, co, 0.0)
        o_ref[1] = jnp.where(lane < half, -si, 0.0)
        o_ref[2] = jnp.where(jnp.logical_and(lane >= half, lane < c.ROPE), si, 0.0)

    tabs = pl.pallas_call(
        rope_tab_kern, name="rope_tab", grid=(nr,),
        in_specs=[pl.BlockSpec((R, 1), lambda i: (i, 0)), fullb((1, LANES))],
        out_specs=tabs_of(R),
        out_shape=jax.ShapeDtypeStruct((3, N, LANES), F32),
        compiler_params=_cparams(("parallel",)),
    )(pos_col, invf)

    def make_rms_kern():
        def rms_fwd_kern(x_ref, g_ref, o_ref):
            xh, _ = _rms(x_ref[...].astype(F32), x_ref.shape[-1])
            o_ref[...] = (xh * g_ref[...]).astype(o_ref.dtype)
        return rms_fwd_kern

    h = pl.pallas_call(
        make_rms_kern(), name="rms_x", grid=(nr,),
        in_specs=[rowb(D, 0), fullb((1, D))], out_specs=rowb(D, 0),
        out_shape=jax.ShapeDtypeStruct((N, D), _BF), compiler_params=_cparams(("parallel",)),
    )(xs, norm_g)

    proj = _mm(h, win_p, name="mm_proj", out_dtype=_BF, after=ag_rest[4])
    kr_raw = _mm(h, w_kr, name="mm_kr", out_dtype=_BF)

    Wf = {}
    for n, g in zip(rest, _split_wait(ag_rest, True, proj, "ag_rest_wait")):
        if n in COL_SHARDED:
            Wf[n] = jnp.transpose(g, (1, 0, 2)).reshape(g.shape[1], -1)
        else:
            Wf[n] = g.reshape(-1, g.shape[2])
    wuq = Wf['w_uq'].reshape(c.QL, H, c.NOPE + c.ROPE)
    wuq_p = jnp.pad(wuq, ((0, 0), (0, 0), (0, c.HW - c.NOPE - c.ROPE))).reshape(c.QL, H * c.HW)
    wukv = Wf['w_ukv']
    wco, wmo, wmkv, wmemo, wo = Wf['w_conv_out'], Wf['w_mla_out'], Wf['w_mem_kv'], Wf['w_mem_out'], Wf['w_o']

    CW = c.CW
    conv_blk = c.o_conv // (4 * CW)
    HALO = 16
    rh = R // HALO

    def conv_parts(blk):
        blk = blk.astype(F32)
        return blk[:, 0:CW], blk[:, CW:2 * CW], blk[:, 2 * CW:3 * CW], blk[:, 3 * CW:4 * CW]

    def shifted(cu, prev, i):
        prev = jnp.where(i > 0, prev, 0.0)
        rid = lax.broadcasted_iota(jnp.int32, cu.shape, 0)
        last, last2 = prev[HALO - 1:HALO, :], prev[HALO - 2:HALO - 1, :]
        sh1 = jnp.where(rid == 0, last, pltpu.roll(cu, 1, 0))
        sh2 = jnp.where(rid == 0, last2, jnp.where(rid == 1, last, pltpu.roll(cu, 2, 0)))
        return sh1, sh2

    def conv_fwd_kern(p_ref, prev_ref, w_ref, o_ref):
        i = pl.program_id(0)
        cg, bg, u, z = conv_parts(p_ref[...])
        pc, _, pu, _ = conv_parts(prev_ref[...])
        cu = cg * u
        sh1, sh2 = shifted(cu, pc * pu, i)
        w = w_ref[...]
        conv = w[0:1, :] * sh2 + w[1:2, :] * sh1 + w[2:3, :] * cu
        o_ref[...] = (bg * conv * (z * _sig(z))).astype(o_ref.dtype)

    prev_spec = pl.BlockSpec((HALO, 4 * CW), lambda i: (jnp.maximum(i * rh - 1, 0), conv_blk))
    a_conv = pl.pallas_call(
        conv_fwd_kern, name="conv_fwd", grid=(nr,),
        in_specs=[rowb(4 * CW, conv_blk), prev_spec, fullb((8, CW))],
        out_specs=rowb(CW, 0), out_shape=jax.ShapeDtypeStruct((N, CW), _BF),
        compiler_params=_cparams(("parallel",)),
    )(proj, proj, convw8)
    o_conv = _mm(a_conv, wco, name="mm_oconv", out_dtype=_BF)

    QL, KVL, HW = c.QL, c.KVL, c.HW
    lora_blk = c.o_lora // (QL + KVL)

    def lora_fwd_kern(p_ref, gq_ref, gkv_ref, q_ref, kv_ref):
        blk = p_ref[...].astype(F32)
        qh, _ = _rms(blk[:, :QL], QL)
        kh, _ = _rms(blk[:, QL:], KVL)
        q_ref[...] = (qh * gq_ref[...]).astype(q_ref.dtype)
        kv_ref[...] = (kh * gkv_ref[...]).astype(kv_ref.dtype)

    cqn, ckvn = pl.pallas_call(
        lora_fwd_kern, name="lora_fwd", grid=(nr,),
        in_specs=[rowb(QL + KVL, lora_blk), fullb((1, QL)), fullb((1, KVL))],
        out_specs=[rowb(QL, 0), rowb(KVL, 0)],
        out_shape=[jax.ShapeDtypeStruct((N, QL), _BF), jax.ShapeDtypeStruct((N, KVL), _BF)],
        compiler_params=_cparams(("parallel",)),
    )(proj, mla_q_norm_g, mla_kv_norm_g)
    q_p = _mm(cqn, wuq_p, name="mm_q", out_dtype=_BF)
    kv = _mm(ckvn, wukv, name="mm_kv", out_dtype=_BF)

    g_qn, g_qr = mla_qn_nope_g, pad_lanes(mla_qn_rope_g)
    g_kn, g_kr = mla_kn_nope_g, pad_lanes(mla_kn_rope_g)

    def krope_fwd_kern(p_ref, t_ref, g_ref, o_ref):
        xh, _ = _rms(p_ref[...].astype(F32), c.ROPE)
        o_ref[...] = _rope(xh * g_ref[...], t_ref[0], t_ref[1], t_ref[2]).astype(o_ref.dtype)

    k_rope = pl.pallas_call(
        krope_fwd_kern, name="krope_fwd", grid=(nr,),
        in_specs=[rowb(LANES, 0), tabs_of(R), fullb((1, LANES))],
        out_specs=rowb(LANES, 0), out_shape=jax.ShapeDtypeStruct((N, LANES), _BF),
        compiler_params=_cparams(("parallel",)),
    )(kr_raw, tabs, g_kr)

    RP, HG = c.RP, c.HG
    nrp, nhg = N // RP, H // HG
    heads_in = pl.BlockSpec((RP, HG * HW), lambda i, hg: (i, hg))
    heads_out = pl.BlockSpec((HG, RP, HW), lambda i, hg: (hg, i, 0))

    def q_prep_kern(q_ref, t_ref, gn_ref, gr_ref, o_ref):
        for g in range(HG):
            blk = q_ref[:, g * HW:(g + 1) * HW].astype(F32)
            nh, _ = _rms(blk[:, :LANES], c.NOPE)
            rhat, _ = _rms(blk[:, LANES:], c.ROPE)
            rot = _rope(rhat * gr_ref[...], t_ref[0], t_ref[1], t_ref[2])
            o_ref[g] = (jnp.concatenate([nh * gn_ref[...], rot], axis=1) * (c.scale * LOG2E)).astype(o_ref.dtype)

    q_cat = pl.pallas_call(
        q_prep_kern, name="q_prep", grid=(nrp, nhg),
        in_specs=[heads_in, tabs_of(RP), fullb((1, LANES)), fullb((1, LANES))],
        out_specs=heads_out, out_shape=jax.ShapeDtypeStruct((H, N, HW), _BF),
        compiler_params=_cparams(("parallel", "parallel")),
    )(q_p, tabs, g_qn, g_qr)

    def k_prep_kern(kv_ref, kr_ref, gn_ref, o_ref):
        for g in range(HG):
            kn, _ = _rms(kv_ref[:, g * HW:g * HW + LANES].astype(F32), c.NOPE)
            o_ref[g] = jnp.concatenate([(kn * gn_ref[...]).astype(o_ref.dtype), kr_ref[...]], axis=1)

    k_cat = pl.pallas_call(
        k_prep_kern, name="k_prep", grid=(nrp, nhg),
        in_specs=[heads_in, pl.BlockSpec((RP, LANES), lambda i, hg: (i, 0)), fullb((1, LANES))],
        out_specs=heads_out, out_shape=jax.ShapeDtypeStruct((H, N, HW), _BF),
        compiler_params=_cparams(("parallel", "parallel")),
    )(kv, k_rope, g_kn)

    nb = N // B
    assert CHUNK & (CHUNK - 1) == 0 and B % CHUNK == 0

    def diag_mask(s, row0=0):
        row = lax.broadcasted_iota(jnp.int32, s.shape, 0) + row0
        col = lax.broadcasted_iota(jnp.int32, s.shape, 1)
        shift = CHUNK.bit_length() - 1
        allowed = jnp.right_shift(col, shift) <= jnp.right_shift(row, shift)
        return jnp.where(allowed, s, NEG)

    k_head = pl.BlockSpec((1, N, HW), lambda hh, i: (hh, 0, 0))
    v_head = pl.BlockSpec((N, LANES), lambda hh, i: (0, 2 * hh + 1))
    q_blk = pl.BlockSpec((1, B, HW), lambda hh, i: (hh, i, 0))
    o_blk = pl.BlockSpec((B, LANES), lambda hh, i: (i, hh))
    lse_blk = pl.BlockSpec((1, B, LANES), lambda hh, i: (hh, i, 0))

    def attn_fwd_kern(q_ref, k_ref, v_ref, o_ref, lse_ref, m_sc, acc_sc, s_sc):
        i = pl.program_id(1)
        m_sc[...] = jnp.full(m_sc.shape, NEG, F32)
        acc_sc[...] = jnp.zeros(acc_sc.shape, F32)

        def rows_of(t):
            return pl.ds(pl.multiple_of(t * B, B), B)

        def scores(t, slot):
            s_sc[slot] = _dot_nt(q_ref[0], k_ref[0, rows_of(t), :])

        def softmax_pv(t, slot, masked):
            s = s_sc[slot]
            if masked:
                s = diag_mask(s)
            mt = s[:, 0:LANES]
            for cb in range(1, B // LANES):
                mt = jnp.maximum(mt, s[:, cb * LANES:(cb + 1) * LANES])
            m_prev = m_sc[...]
            m_new = jnp.maximum(m_prev, jnp.max(mt, axis=1, keepdims=True))
            alpha = jnp.exp2(m_prev - m_new)
            p = jnp.concatenate([jnp.exp2(s[:, cb * LANES:(cb + 1) * LANES] - m_new).astype(_BF)
                                 for cb in range(B // LANES)], axis=1)
            v_ones = jnp.concatenate([v_ref[rows_of(t), :], jnp.ones((B, LANES), _BF)], axis=1)
            acc_sc[...] = jnp.concatenate([alpha, alpha], axis=1) * acc_sc[...] + _dot(p, v_ones)
            m_sc[...] = m_new

        scores(0, 0)

        def run(first, count, last_masked):
            for u in range(count):
                if u + 1 < count or not last_masked:
                    scores(first + u + 1, (u + 1) % 2)
                softmax_pv(first + u, u % 2, last_masked and u == count - 1)

        def unrolled(u, carry):
            run(ATT_UNROLL * u, ATT_UNROLL, False)
            return carry

        lax.fori_loop(0, i // ATT_UNROLL, unrolled, 0)
        for rem in range(ATT_UNROLL):
            @pl.when(i % ATT_UNROLL == rem)
            def _(rem=rem):
                run(i - rem, rem + 1, True)

        acc = acc_sc[...]
        o_ref[...] = (acc[:, :LANES] / acc[:, LANES:]).astype(o_ref.dtype)
        lse_ref[0] = m_sc[...] + jnp.log2(acc[:, LANES:])

    mla_y, lse = pl.pallas_call(
        attn_fwd_kern, name="attn_fwd", grid=(H, nb),
        in_specs=[q_blk, k_head, v_head], out_specs=[o_blk, lse_blk],
        out_shape=[jax.ShapeDtypeStruct((N, c.HV), _BF), jax.ShapeDtypeStruct((H, N, LANES), F32)],
        scratch_shapes=[pltpu.VMEM((B, LANES), F32), pltpu.VMEM((B, HW), F32), pltpu.VMEM((2, B, B), F32)],
        compiler_params=_cparams(("parallel", "arbitrary")),
    )(q_cat, k_cat, kv)

    HV = c.HV
    mz_blk = c.o_mz // HV

    def gate_fwd_kern(y_ref, z_ref, o_ref):
        z = z_ref[...].astype(F32)
        o_ref[...] = (y_ref[...].astype(F32) * (z * _sig(z))).astype(o_ref.dtype)

    a_mla = pl.pallas_call(
        gate_fwd_kern, name="gate_mla", grid=(nr,),
        in_specs=[rowb(HV, 0), rowb(HV, mz_blk)], out_specs=rowb(HV, 0),
        out_shape=jax.ShapeDtypeStruct((N, HV), _BF), compiler_params=_cparams(("parallel",)),
    )(mla_y, proj)
    o_mla = _mm(a_mla, wmo, name="mm_omla", out_dtype=_BF)

    M, MW, MH, MHD = c.M, c.MW, c.MH, c.MHD
    memn = pl.pallas_call(
        make_rms_kern(), name="rms_mem",
        grid=(1,), in_specs=[fullb((M, D)), fullb((1, D))], out_specs=fullb((M, D)),
        out_shape=jax.ShapeDtypeStruct((M, D), _BF), compiler_params=_cparams(("arbitrary",)),
    )(memx, mem_norm_g)
    kvm = _mm(memn, wmkv, name="mm_memkv", out_dtype=F32)

    def memk_fwd_kern(kv_ref, g_ref, k_ref, v_ref):
        for hh in range(MH):
            kh, _ = _rms(kv_ref[:, hh * MHD:(hh + 1) * MHD], MHD)
            k_ref[:, hh * MHD:(hh + 1) * MHD] = (kh * g_ref[...]).astype(k_ref.dtype)
        v_ref[...] = kv_ref[:, MW:].astype(v_ref.dtype)

    mem_k, mem_v = pl.pallas_call(
        memk_fwd_kern, name="memk_fwd", grid=(1,),
        in_specs=[fullb((M, 2 * MW)), fullb((1, MHD))], out_specs=[fullb((M, MW)), fullb((M, MW))],
        out_shape=[jax.ShapeDtypeStruct((M, MW), _BF)] * 2, compiler_params=_cparams(("arbitrary",)),
    )(kvm, mem_kn_g)

    mem_blk = c.o_mem // (2 * MW)

    def mem_head(qz_ref, k_ref, v_ref, g_ref, hh):
        sl = slice(hh * MHD, (hh + 1) * MHD)
        qh, r = _rms(qz_ref[:, sl].astype(F32), MHD)
        qn = (qh * g_ref[...]).astype(_BF)
        s = _dot_nt(qn, k_ref[:, sl]) * c.mscale
        e = jnp.exp(s - jnp.max(s, axis=1, keepdims=True))
        p = e / jnp.sum(e, axis=1, keepdims=True)
        y = _dot(p.astype(_BF), v_ref[:, sl])
        z = qz_ref[:, MW + hh * MHD:MW + (hh + 1) * MHD].astype(F32)
        return sl, qh, r, qn, p, y, z

    def mem_fwd_kern(qz_ref, k_ref, v_ref, g_ref, o_ref):
        for hh in range(MH):
            sl, _, _, _, _, y, z = mem_head(qz_ref, k_ref, v_ref, g_ref, hh)
            o_ref[:, sl] = (y * (z * _sig(z))).astype(o_ref.dtype)

    a_mem = pl.pallas_call(
        mem_fwd_kern, name="mem_fwd", grid=(nr,),
        in_specs=[rowb(2 * MW, mem_blk), fullb((M, MW)), fullb((M, MW)), fullb((1, MHD))],
        out_specs=rowb(MW, 0), out_shape=jax.ShapeDtypeStruct((N, MW), _BF),
        compiler_params=_cparams(("parallel",)),
    )(proj, mem_k, mem_v, mem_qn_g)
    o_mem = _mm(a_mem, wmemo, name="mm_omem", out_dtype=_BF)

    g_blk = c.o_g // (3 * D)

    def merge_fwd_kern(g_ref, oc_ref, om_ref, ome_ref, o_ref):
        g = g_ref[...].astype(F32)
        acc = _sig(g[:, :D]) * oc_ref[...].astype(F32)
        acc += _sig(g[:, D:2 * D]) * om_ref[...].astype(F32)
        acc += _sig(g[:, 2 * D:]) * ome_ref[...].astype(F32)
        o_ref[...] = acc.astype(o_ref.dtype)

    merged = pl.pallas_call(
        merge_fwd_kern, name="merge_fwd", grid=(nr,),
        in_specs=[rowb(3 * D, g_blk), rowb(D, 0), rowb(D, 0), rowb(D, 0)], out_specs=rowb(D, 0),
        out_shape=jax.ShapeDtypeStruct((N, D), _BF), compiler_params=_cparams(("parallel",)),
    )(proj, o_conv, o_mla, o_mem)
    y2 = _mm(merged, wo, name="mm_out", out_dtype=F32)

    def loss_kern(x_ref, y_ref, t_ref, dy_ref, dyb_ref, l_ref):
        e = x_ref[...] + y_ref[...] - t_ref[...]
        dy = e * (1.0 / D)
        dy_ref[...] = dy
        dyb_ref[...] = dy.astype(dyb_ref.dtype)

        @pl.when(pl.program_id(0) == 0)
        def _():
            l_ref[...] = jnp.zeros(l_ref.shape, F32)

        l_ref[...] += jnp.sum(e * e, axis=0, keepdims=True)

    dy, dyb, lpart = pl.pallas_call(
        loss_kern, name="loss", grid=(nr,),
        in_specs=[rowb(D, 0)] * 3, out_specs=[rowb(D, 0), rowb(D, 0), fullb((1, D))],
        out_shape=[jax.ShapeDtypeStruct((N, D), F32), jax.ShapeDtypeStruct((N, D), _BF),
                   jax.ShapeDtypeStruct((1, D), F32)],
        compiler_params=_cparams(("arbitrary",)),
    )(xs, y2, tgt)
    loss = lax.psum(jnp.sum(lpart) * (0.5 / D), AXES)

    G = {}
    d_merged = _mm(dyb, wo, tb=True, name="mm_dmerged", out_dtype=_BF)
    G['w_o'] = _mm(merged, dyb, ta=True, name="mm_dwo", out_dtype=_BF)

    dproj0 = lax.empty((N, c.P), _BF)
    any_spec = pl.BlockSpec(memory_space=pl.ANY)

    def merge_bwd_kern(dp_any, g_ref, dm_ref, oc_ref, om_ref, ome_ref, dg_ref, doc_ref, dom_ref, dome_ref):
        g = g_ref[...].astype(F32)
        dm = dm_ref[...].astype(F32)
        for idx, (o_in, d_out) in enumerate(((oc_ref, doc_ref), (om_ref, dom_ref), (ome_ref, dome_ref))):
            sg = _sig(g[:, idx * D:(idx + 1) * D])
            d_out[...] = (sg * dm).astype(d_out.dtype)
            dg_ref[:, idx * D:(idx + 1) * D] = (dm * o_in[...].astype(F32) * sg * (1.0 - sg)).astype(dg_ref.dtype)

    dproj1, d_oconv, d_omla, d_omem = pl.pallas_call(
        merge_bwd_kern, name="merge_bwd", grid=(nr,),
        in_specs=[any_spec, rowb(3 * D, g_blk), rowb(D, 0), rowb(D, 0), rowb(D, 0), rowb(D, 0)],
        out_specs=[rowb(3 * D, g_blk), rowb(D, 0), rowb(D, 0), rowb(D, 0)],
        out_shape=[jax.ShapeDtypeStruct((N, c.P), _BF)] + [jax.ShapeDtypeStruct((N, D), _BF)] * 3,
        input_output_aliases={0: 0}, compiler_params=_cparams(("parallel",)),
    )(dproj0, proj, d_merged, o_conv, o_mla, o_mem)

    G['w_conv_out'] = _mm(a_conv, d_oconv, ta=True, name="mm_dwco", out_dtype=_BF)
    d_aconv = _mm(d_oconv, wco, tb=True, name="mm_daconv", out_dtype=_BF)
    G['w_mla_out'] = _mm(a_mla, d_omla, ta=True, name="mm_dwmo", out_dtype=_BF)
    d_amla = _mm(d_omla, wmo, tb=True, name="mm_damla", out_dtype=_BF)
    G['w_mem_out'] = _mm(a_mem, d_omem, ta=True, name="mm_dwmemo", out_dtype=_BF)
    d_amem = _mm(d_omem, wmemo, tb=True, name="mm_damem", out_dtype=_BF)

    def conv_bwd_kern(dp_any, p_ref, prev_ref, next_ref, da_ref, dan_ref, w_ref, o_ref, dw_ref):
        i = pl.program_id(0)
        cg, bg, u, z = conv_parts(p_ref[...])
        pc, _, pu, _ = conv_parts(prev_ref[...])
        _, nbg, _, nz = conv_parts(next_ref[...])
        cu = cg * u
        sh1, sh2 = shifted(cu, pc * pu, i)
        w = w_ref[...]
        conv = w[0:1, :] * sh2 + w[1:2, :] * sh1 + w[2:3, :] * cu
        sg = _sig(z)
        sz = z * sg
        da = da_ref[...].astype(F32)
        dcy = da * sz
        d_z = da * (bg * conv) * (sg * (1.0 + z * (1.0 - sg)))
        d_b = dcy * conv
        dconv = dcy * bg
        dnext = dan_ref[...].astype(F32) * (nz * _sig(nz)) * nbg
        dnext = jnp.where(i < nr - 1, dnext, 0.0)
        rid = lax.broadcasted_iota(jnp.int32, cu.shape, 0)
        up1 = jnp.where(rid == R - 1, dnext[0:1, :], pltpu.roll(dconv, R - 1, 0))
        up2 = jnp.where(rid == R - 2, dnext[0:1, :], jnp.where(rid == R - 1, dnext[1:2, :], pltpu.roll(dconv, R - 2, 0)))
        dcu = w[2:3, :] * dconv + w[1:2, :] * up1 + w[0:1, :] * up2
        o_ref[:, 0:CW] = (dcu * u).astype(o_ref.dtype)
        o_ref[:, CW:2 * CW] = d_b.astype(o_ref.dtype)
        o_ref[:, 2 * CW:3 * CW] = (dcu * cg).astype(o_ref.dtype)
        o_ref[:, 3 * CW:4 * CW] = d_z.astype(o_ref.dtype)

        @pl.when(i == 0)
        def _():
            dw_ref[...] = jnp.zeros(dw_ref.shape, F32)

        dw_ref[0:1, :] += jnp.sum(dconv * sh2, axis=0, keepdims=True)
        dw_ref[1:2, :] += jnp.sum(dconv * sh1, axis=0, keepdims=True)
        dw_ref[2:3, :] += jnp.sum(dconv * cu, axis=0, keepdims=True)

    next_spec = pl.BlockSpec((HALO, 4 * CW), lambda i: (jnp.minimum((i + 1) * rh, N // HALO - 1), conv_blk))
    dan_spec = pl.BlockSpec((HALO, CW), lambda i: (jnp.minimum((i + 1) * rh, N // HALO - 1), 0))
    dproj2, g_convw = pl.pallas_call(
        conv_bwd_kern, name="conv_bwd", grid=(nr,),
        in_specs=[any_spec, rowb(4 * CW, conv_blk), prev_spec, next_spec, rowb(CW, 0), dan_spec, fullb((8, CW))],
        out_specs=[rowb(4 * CW, conv_blk), fullb((8, CW))],
        out_shape=[jax.ShapeDtypeStruct((N, c.P), _BF), jax.ShapeDtypeStruct((8, CW), F32)],
        input_output_aliases={0: 0}, compiler_params=_cparams(("arbitrary",)),
    )(dproj1, proj, proj, proj, d_aconv, d_aconv, convw8)

    def mem_bwd_kern(dp_any, qz_ref, da_ref, k_ref, v_ref, g_ref, o_ref, dk_ref, dv_ref, dg_ref):
        @pl.when(pl.program_id(0) == 0)
        def _():
            dk_ref[...] = jnp.zeros(dk_ref.shape, F32)
            dv_ref[...] = jnp.zeros(dv_ref.shape, F32)
            dg_ref[...] = jnp.zeros(dg_ref.shape, F32)

        for hh in range(MH):
            sl, qh, r, qn, p, y, z = mem_head(qz_ref, k_ref, v_ref, g_ref, hh)
            da = da_ref[:, sl].astype(F32)
            sg = _sig(z)
            dyh = da * (z * sg)
            o_ref[:, MW + hh * MHD:MW + (hh + 1) * MHD] = (da * y * (sg * (1.0 + z * (1.0 - sg)))).astype(o_ref.dtype)
            dyb_h = dyh.astype(_BF)
            dpm = _dot_nt(dyb_h, v_ref[:, sl])
            ds = (p * (dpm - jnp.sum(dpm * p, axis=1, keepdims=True)) * c.mscale).astype(_BF)
            dqn = _dot(ds, k_ref[:, sl])
            dk_ref[:, sl] += _dot_tn(ds, qn)
            dv_ref[:, sl] += _dot_tn(p.astype(_BF), dyb_h)
            dq, dgp = _rms_bwd(qh, r, g_ref[...], dqn, MHD)
            o_ref[:, sl] = dq.astype(o_ref.dtype)
            dg_ref[...] += jnp.sum(dgp, axis=0, keepdims=True)

    dproj3, d_memk, d_memv, g_mem_qn = pl.pallas_call(
        mem_bwd_kern, name="mem_bwd", grid=(nr,),
        in_specs=[any_spec, rowb(2 * MW, mem_blk), rowb(MW, 0), fullb((M, MW)), fullb((M, MW)), fullb((1, MHD))],
        out_specs=[rowb(2 * MW, mem_blk), fullb((M, MW)), fullb((M, MW)), fullb((1, MHD))],
        out_shape=[jax.ShapeDtypeStruct((N, c.P), _BF), jax.ShapeDtypeStruct((M, MW), F32),
                   jax.ShapeDtypeStruct((M, MW), F32), jax.ShapeDtypeStruct((1, MHD), F32)],
        input_output_aliases={0: 0}, compiler_params=_cparams(("arbitrary",)),
    )(dproj2, proj, d_amem, mem_k, mem_v, mem_qn_g)

    def memk_bwd_kern(kv_ref, dk_ref, dv_ref, g_ref, o_ref, dg_ref):
        dg = jnp.zeros((1, MHD), F32)
        for hh in range(MH):
            sl = slice(hh * MHD, (hh + 1) * MHD)
            kh, r = _rms(kv_ref[:, sl], MHD)
            dkr, dgp = _rms_bwd(kh, r, g_ref[...], dk_ref[:, sl], MHD)
            o_ref[:, sl] = dkr.astype(o_ref.dtype)
            dg += jnp.sum(dgp, axis=0, keepdims=True)
        o_ref[:, MW:] = dv_ref[...].astype(o_ref.dtype)
        dg_ref[...] = dg

    d_kvm, g_mem_kn = pl.pallas_call(
        memk_bwd_kern, name="memk_bwd", grid=(1,),
        in_specs=[fullb((M, 2 * MW)), fullb((M, MW)), fullb((M, MW)), fullb((1, MHD))],
        out_specs=[fullb((M, 2 * MW)), fullb((1, MHD))],
        out_shape=[jax.ShapeDtypeStruct((M, 2 * MW), _BF), jax.ShapeDtypeStruct((1, MHD), F32)],
        compiler_params=_cparams(("arbitrary",)),
    )(kvm, d_memk, d_memv, mem_kn_g)
    G['w_mem_kv'] = _mm(memn, d_kvm, ta=True, name="mm_dwmkv", out_dtype=_BF)
    d_memn = _mm(d_kvm, wmkv, tb=True, name="mm_dmemn", out_dtype=F32)

    def memnorm_bwd_kern(x_ref, d_ref, dg_ref):
        xh, _ = _rms(x_ref[...], D)
        dg_ref[...] = jnp.sum(d_ref[...] * xh, axis=0, keepdims=True)

    g_mem_norm = pl.pallas_call(
        memnorm_bwd_kern, name="memnorm_bwd", grid=(1,),
        in_specs=[fullb((M, D)), fullb((M, D))], out_specs=fullb((1, D)),
        out_shape=jax.ShapeDtypeStruct((1, D), F32), compiler_params=_cparams(("arbitrary",)),
    )(memx, d_memn)

    def gate_bwd_kern(dp_any, da_ref, y_ref, z_ref, dy_ref, dz_ref):
        z = z_ref[...].astype(F32)
        da = da_ref[...].astype(F32)
        sg = _sig(z)
        dy_ref[...] = (da * (z * sg)).astype(dy_ref.dtype)
        dz_ref[...] = (da * y_ref[...].astype(F32) * (sg * (1.0 + z * (1.0 - sg)))).astype(dz_ref.dtype)

    d_mlay, dproj4 = pl.pallas_call(
        gate_bwd_kern, name="gate_mla_bwd", grid=(nr,),
        in_specs=[any_spec, rowb(HV, 0), rowb(HV, 0), rowb(HV, mz_blk)],
        out_specs=[rowb(HV, 0), rowb(HV, mz_blk)],
        out_shape=[jax.ShapeDtypeStruct((N, HV), _BF), jax.ShapeDtypeStruct((N, c.P), _BF)],
        input_output_aliases={0: 1}, compiler_params=_cparams(("parallel",)),
    )(dproj3, d_amla, mla_y, proj)

    def attn_bwd_kern(q_ref, k_ref, v_ref, o_ref, do_ref, lse_ref, dq_ref, dk_ref, dv_ref, dq_sc, dl_sc, dk_sc, dv_sc):
        i = pl.program_id(1)
        q, do = q_ref[0], do_ref[...]
        delta = jnp.sum(do.astype(F32) * o_ref[...].astype(F32), axis=1, keepdims=True)
        dl_sc[...] = jnp.broadcast_to(delta, dl_sc.shape)
        dq_sc[...] = jnp.zeros(dq_sc.shape, F32)

        def step(t, masked):
            rows = pl.ds(pl.multiple_of(t * B, B), B)
            k = k_ref[0, rows, :]
            s = _dot_nt(q, k)
            if masked:
                s = diag_mask(s)
            dpm = _dot_nt(do, v_ref[rows, :])
            lse_t, dl = lse_ref[0], dl_sc[...]
            ps, dss = [], []
            for cb in range(B // LANES):
                cols = slice(cb * LANES, (cb + 1) * LANES)
                p_cb = jnp.exp2(s[:, cols] - lse_t)
                ps.append(p_cb.astype(_BF))
                dss.append((p_cb * (dpm[:, cols] - dl)).astype(_BF))
            p, ds = jnp.concatenate(ps, axis=1), jnp.concatenate(dss, axis=1)
            dvp = _dot_tn(p, do)
            dkp = _dot_tn(ds, q)
            if masked:
                dk_sc[rows, :] = dkp
                dv_sc[rows, :] = dvp
            else:
                dk_sc[rows, :] += dkp
                dv_sc[rows, :] += dvp
            dq_sc[...] += _dot(ds, k)

        def unrolled(t, carry):
            for u in range(ATT_UNROLL):
                step(t * ATT_UNROLL + u, False)
            return carry

        lax.fori_loop(0, i // ATT_UNROLL, unrolled, 0)
        for rem in range(ATT_UNROLL):
            @pl.when(i % ATT_UNROLL == rem)
            def _(rem=rem):
                for u in range(rem):
                    step(i - rem + u, False)
                step(i, True)

        dq_ref[0] = dq_sc[...].astype(dq_ref.dtype)

        @pl.when(i == nb - 1)
        def _():
            dk_ref[0] = dk_sc[...].astype(dk_ref.dtype)
            dv_ref[0] = dv_sc[...].astype(dv_ref.dtype)

    d_qcat, d_kcat, d_v = pl.pallas_call(
        attn_bwd_kern, name="attn_bwd", grid=(H, nb),
        in_specs=[q_blk, k_head, v_head, o_blk, o_blk, lse_blk],
        out_specs=[pl.BlockSpec((1, B, HW), lambda hh, i: (hh, i, 0)),
                   pl.BlockSpec((1, N, HW), lambda hh, i: (hh, 0, 0)),
                   pl.BlockSpec((1, N, LANES), lambda hh, i: (hh, 0, 0))],
        out_shape=[jax.ShapeDtypeStruct((H, N, HW), _BF), jax.ShapeDtypeStruct((H, N, HW), _BF),
                   jax.ShapeDtypeStruct((H, N, LANES), _BF)],
        scratch_shapes=[pltpu.VMEM((B, HW), F32), pltpu.VMEM((B, LANES), F32), pltpu.VMEM((N, HW), F32),
                        pltpu.VMEM((N, LANES), F32)],
        compiler_params=_cparams(("parallel", "arbitrary")),
    )(q_cat, k_cat, kv, mla_y, d_mlay, lse)

    def q_prep_bwd_kern(q_ref, dq_ref, t_ref, gn_ref, gr_ref, o_ref, dgn_ref, dgr_ref):
        @pl.when(jnp.logical_and(pl.program_id(0) == 0, pl.program_id(1) == 0))
        def _():
            dgn_ref[...] = jnp.zeros(dgn_ref.shape, F32)
            dgr_ref[...] = jnp.zeros(dgr_ref.shape, F32)

        for g in range(HG):
            blk = q_ref[:, g * HW:(g + 1) * HW].astype(F32)
            d = dq_ref[g].astype(F32) * c.scale
            nh, rn = _rms(blk[:, :LANES], c.NOPE)
            rhat, rr = _rms(blk[:, LANES:], c.ROPE)
            dn, dgn = _rms_bwd(nh, rn, gn_ref[...], d[:, :LANES], c.NOPE)
            drot = _rope_t(d[:, LANES:], t_ref[0], t_ref[1], t_ref[2])
            dr, dgr = _rms_bwd(rhat, rr, gr_ref[...], drot, c.ROPE)
            o_ref[:, g * HW:(g + 1) * HW] = jnp.concatenate([dn, dr], axis=1).astype(o_ref.dtype)
            dgn_ref[...] += jnp.sum(dgn, axis=0, keepdims=True)
            dgr_ref[...] += jnp.sum(dgr, axis=0, keepdims=True)

    d_qp, g_qn_nope, g_qn_rope = pl.pallas_call(
        q_prep_bwd_kern, name="q_prep_bwd", grid=(nrp, nhg),
        in_specs=[heads_in, heads_out, tabs_of(RP), fullb((1, LANES)), fullb((1, LANES))],
        out_specs=[heads_in, fullb((1, LANES)), fullb((1, LANES))],
        out_shape=[jax.ShapeDtypeStruct((N, H * HW), _BF), jax.ShapeDtypeStruct((1, LANES), F32),
                   jax.ShapeDtypeStruct((1, LANES), F32)],
        compiler_params=_cparams(("arbitrary", "arbitrary")),
    )(q_p, d_qcat, tabs, g_qn, g_qr)

    def k_prep_bwd_kern(kv_ref, dk_ref, dv_ref, gn_ref, o_ref, dkr_ref, dgn_ref):
        hg = pl.program_id(1)

        @pl.when(jnp.logical_and(pl.program_id(0) == 0, hg == 0))
        def _():
            dgn_ref[...] = jnp.zeros(dgn_ref.shape, F32)

        @pl.when(hg == 0)
        def _():
            dkr_ref[...] = jnp.zeros(dkr_ref.shape, F32)

        dkr = jnp.zeros((RP, LANES), F32)
        for g in range(HG):
            dk = dk_ref[g].astype(F32) * (1.0 / LOG2E)
            kn, r = _rms(kv_ref[:, g * HW:g * HW + LANES].astype(F32), c.NOPE)
            dkn, dgn = _rms_bwd(kn, r, gn_ref[...], dk[:, :LANES], c.NOPE)
            o_ref[:, g * HW:(g + 1) * HW] = jnp.concatenate([dkn.astype(o_ref.dtype), dv_ref[g]], axis=1)
            dgn_ref[...] += jnp.sum(dgn, axis=0, keepdims=True)
            dkr += dk[:, LANES:]
        dkr_ref[...] += dkr

    d_kv, d_krsum, g_kn_nope = pl.pallas_call(
        k_prep_bwd_kern, name="k_prep_bwd", grid=(nrp, nhg),
        in_specs=[heads_in, heads_out, pl.BlockSpec((HG, RP, LANES), lambda i, hg: (hg, i, 0)), fullb((1, LANES))],
        out_specs=[heads_in, pl.BlockSpec((RP, LANES), lambda i, hg: (i, 0)), fullb((1, LANES))],
        out_shape=[jax.ShapeDtypeStruct((N, H * HW), _BF), jax.ShapeDtypeStruct((N, LANES), F32),
                   jax.ShapeDtypeStruct((1, LANES), F32)],
        compiler_params=_cparams(("arbitrary", "arbitrary")),
    )(kv, d_kcat, d_v, g_kn)

    def krope_bwd_kern(p_ref, d_ref, t_ref, g_ref, o_ref, dg_ref):
        @pl.when(pl.program_id(0) == 0)
        def _():
            dg_ref[...] = jnp.zeros(dg_ref.shape, F32)

        xh, r = _rms(p_ref[...].astype(F32), c.ROPE)
        drot = _rope_t(d_ref[...], t_ref[0], t_ref[1], t_ref[2])
        dx, dg = _rms_bwd(xh, r, g_ref[...], drot, c.ROPE)
        o_ref[...] = dx.astype(o_ref.dtype)
        dg_ref[...] += jnp.sum(dg, axis=0, keepdims=True)

    d_kr, g_kn_rope = pl.pallas_call(
        krope_bwd_kern, name="krope_bwd", grid=(nr,),
        in_specs=[rowb(LANES, 0), rowb(LANES, 0), tabs_of(R), fullb((1, LANES))],
        out_specs=[rowb(LANES, 0), fullb((1, LANES))],
        out_shape=[jax.ShapeDtypeStruct((N, LANES), _BF), jax.ShapeDtypeStruct((1, LANES), F32)],
        compiler_params=_cparams(("arbitrary",)),
    )(kr_raw, d_krsum, tabs, g_kr)
    dproj5 = dproj4

    g_wuq_p = _mm(cqn, d_qp, ta=True, name="mm_dwuq", out_dtype=_BF)
    G['w_uq'] = g_wuq_p.reshape(QL, H, HW)[:, :, :c.NOPE + c.ROPE].reshape(QL, H * (c.NOPE + c.ROPE))
    d_cqn = _mm(d_qp, wuq_p, tb=True, name="mm_dcqn", out_dtype=F32)
    G['w_ukv'] = _mm(ckvn, d_kv, ta=True, name="mm_dwukv", out_dtype=_BF)
    d_ckvn = _mm(d_kv, wukv, tb=True, name="mm_dckvn", out_dtype=F32)

    def to_blocks(n, g):
        if n in COL_SHARDED:
            return jnp.transpose(g.reshape(g.shape[0], NDEV, -1), (1, 0, 2))
        return g.reshape(NDEV, -1, g.shape[1])

    def landing(b):
        own = lax.dynamic_index_in_dim(b, me, 0, keepdims=True)
        return lax.dynamic_update_index_in_dim(lax.empty(b.shape, b.dtype), own, me, 0)

    early = [n for n in BIG if n != 'w_in']
    blocks_e = [to_blocks(n, G[n]) for n in early]
    xe = _split_start(blocks_e, [landing(b) for b in blocks_e], False, "xchg_early_start")
    gq_after = mla_q_norm_g + xe[4][0:1, 0:1]

    def lora_bwd_kern(dp_any, p_ref, dq_ref, dkv_ref, gq_ref, gkv_ref, o_ref, dgq_ref, dgkv_ref):
        @pl.when(pl.program_id(0) == 0)
        def _():
            dgq_ref[...] = jnp.zeros(dgq_ref.shape, F32)
            dgkv_ref[...] = jnp.zeros(dgkv_ref.shape, F32)

        blk = p_ref[...].astype(F32)
        qh, rq = _rms(blk[:, :QL], QL)
        kh, rk = _rms(blk[:, QL:], KVL)
        dq, dgq = _rms_bwd(qh, rq, gq_ref[...], dq_ref[...], QL)
        dk, dgk = _rms_bwd(kh, rk, gkv_ref[...], dkv_ref[...], KVL)
        o_ref[:, :QL] = dq.astype(o_ref.dtype)
        o_ref[:, QL:] = dk.astype(o_ref.dtype)
        dgq_ref[...] += jnp.sum(dgq, axis=0, keepdims=True)
        dgkv_ref[...] += jnp.sum(dgk, axis=0, keepdims=True)

    dproj6, g_q_norm, g_kv_norm = pl.pallas_call(
        lora_bwd_kern, name="lora_bwd", grid=(nr,),
        in_specs=[any_spec, rowb(QL + KVL, lora_blk), rowb(QL, 0), rowb(KVL, 0), fullb((1, QL)), fullb((1, KVL))],
        out_specs=[rowb(QL + KVL, lora_blk), fullb((1, QL)), fullb((1, KVL))],
        out_shape=[jax.ShapeDtypeStruct((N, c.P), _BF), jax.ShapeDtypeStruct((1, QL), F32),
                   jax.ShapeDtypeStruct((1, KVL), F32)],
        input_output_aliases={0: 0}, compiler_params=_cparams(("arbitrary",)),
    )(dproj5, proj, d_cqn, d_ckvn, gq_after, mla_kv_norm_g)

    g_win_p = _mm(h, dproj6, ta=True, name="mm_dwin", out_dtype=_BF, bk=4096)
    g_wkr = _mm(h, d_kr, ta=True, name="mm_dwkr", out_dtype=_BF, bk=4096)
    G['w_in'] = _win_join(g_win_p, g_wkr, c)
    blocks_w = [to_blocks('w_in', G['w_in'])]
    xw = _split_start(blocks_w, [landing(b) for b in blocks_w], False, "xchg_win_start")
    d_h = _mm(dproj6, win_p, tb=True, name="mm_dh", out_dtype=F32, bk=3072, after=xw[4], plus=(d_kr, w_krT))

    def final_bwd_kern(x_ref, g_ref, dh_ref, dy_ref, gx_ref, dg_ref):
        @pl.when(pl.program_id(0) == 0)
        def _():
            dg_ref[...] = jnp.zeros(dg_ref.shape, F32)

        xh, r = _rms(x_ref[...], D)
        dx, dg = _rms_bwd(xh, r, g_ref[...], dh_ref[...], D)
        gx_ref[...] = dy_ref[...] + dx
        dg_ref[...] += jnp.sum(dg, axis=0, keepdims=True)

    grad_x, g_norm = pl.pallas_call(
        final_bwd_kern, name="final_bwd", grid=(nr,),
        in_specs=[rowb(D, 0), fullb((1, D)), rowb(D, 0), rowb(D, 0)],
        out_specs=[rowb(D, 0), fullb((1, D))],
        out_shape=[jax.ShapeDtypeStruct((N, D), F32), jax.ShapeDtypeStruct((1, D), F32)],
        compiler_params=_cparams(("arbitrary",)),
    )(xs, norm_g, d_h, dy)

    recv_e = _split_wait(xe, False, grad_x, "xchg_early_wait")
    recv_w = _split_wait(xw, False, grad_x, "xchg_win_wait")
    res = [{}, {}, {}, {}]
    for n, parts in zip(['w_in'] + early, list(recv_w) + list(recv_e)):
        outs = _adam(parts, W[n][0], Mo[n][0], Vo[n][0], "adam_" + n)
        for k in range(4):
            res[k][n] = outs[k][None]

    small_g = {'norm_g': g_norm, 'mla_q_norm_g': g_q_norm, 'mla_kv_norm_g': g_kv_norm,
               'mla_qn_nope_g': g_qn_nope, 'mla_qn_rope_g': g_qn_rope[:, :c.ROPE], 'mla_kn_nope_g': g_kn_nope,
               'mla_kn_rope_g': g_kn_rope[:, :c.ROPE], 'mem_norm_g': g_mem_norm, 'mem_qn_g': g_mem_qn,
               'mem_kn_g': g_mem_kn}
    small_part = _pack([small_g[n] for n in SMALL] + [g_convw[0:3, :]], 0)
    small_all = _all_gather([small_part], "ag_small_grads")[0]
    small_shapes = [W[n].shape for n in SMALL]
    pieces = _unpack(small_all, small_shapes + [(3, CW)])
    cw8 = CW // NDEV
    conv_mine = lax.dynamic_slice_in_dim(pieces[-1].reshape(NDEV, 3, NDEV, cw8), me, 1, axis=2)[:, :, 0, :]
    sm_parts = _pack(pieces[:-1] + [conv_mine], 1)
    sm_names = SMALL + ['conv_w']
    sm_shapes = small_shapes + [(3, cw8)]
    w_sm = _pack([W[n] for n in SMALL] + [conv_w[0]], 0)
    m_sm = _pack([Mo[n] for n in SMALL] + [m_conv_w[0]], 0)
    v_sm = _pack([Vo[n] for n in SMALL] + [v_conv_w[0]], 0)
    outs_sm = [_unpack(o, sm_shapes) for o in _adam(sm_parts, w_sm, m_sm, v_sm, "adam_small")]
    for k in range(4):
        for n, a in zip(sm_names, outs_sm[k]):
            res[k][n] = a[None] if n == 'conv_w' else a
    return (loss, grad_x[None], *[res[0][n] for n in WEIGHTS], *[res[1][n] for n in WEIGHTS],
            *[res[2][n] for n in WEIGHTS], *[res[3][n] for n in WEIGHTS])
```

```python
import math

import jax
import jax.numpy as jnp
from jax import lax
from jax.experimental import pallas as pl
from jax.experimental.pallas import tpu as pltpu

F32 = jnp.float32
_BF = jnp.bfloat16
EPS = 1e-6
CHUNK = 64
ROPE_THETA = 10000.0
ADAM_LR, ADAM_B1, ADAM_B2, ADAM_EPS, ADAM_WD, ADAM_STEP = 0.001, 0.9, 0.999, 1e-08, 0.01, 10
NDEV = 8
AXES = ("x", "y", "c")
MESH = pl.DeviceIdType.MESH
LANES = 128
NEG = -1e30
LOG2E = math.log2(math.e)
V7X_VMEM_LIMIT = 56 * 1024 * 1024
PACK_C = 1024
ATT_BLOCK = 512
ATT_UNROLL = 4
ADAM_BLOCK_ELEMS = 256 * 1024

WEIGHTS = ['norm_g', 'w_in', 'conv_w', 'w_conv_out', 'mla_q_norm_g', 'w_uq', 'mla_kv_norm_g', 'w_ukv',
           'mla_qn_nope_g', 'mla_qn_rope_g', 'mla_kn_nope_g', 'mla_kn_rope_g', 'w_mla_out', 'mem_norm_g',
           'w_mem_kv', 'mem_qn_g', 'mem_kn_g', 'w_mem_out', 'w_o']
BIG = ['w_in', 'w_conv_out', 'w_uq', 'w_ukv', 'w_mla_out', 'w_mem_kv', 'w_mem_out', 'w_o']
COL_SHARDED = ('w_in', 'w_conv_out', 'w_uq', 'w_ukv', 'w_mem_out')
SMALL = ['norm_g', 'mla_q_norm_g', 'mla_kv_norm_g', 'mla_qn_nope_g', 'mla_qn_rope_g', 'mla_kn_nope_g',
         'mla_kn_rope_g', 'mem_norm_g', 'mem_qn_g', 'mem_kn_g']


def _tile(dim, target, align):
    if dim <= target:
        return dim
    t = target - target % align
    while t > 0:
        if dim % t == 0:
            return t
        t -= align
    raise ValueError(f"no tile for {dim} {target} {align}")


def _cparams(sem):
    return pltpu.CompilerParams(dimension_semantics=sem, vmem_limit_bytes=V7X_VMEM_LIMIT)


def _sig(x):
    return 1.0 / (1.0 + jnp.exp(-x))


def _rms(x, n):
    r = lax.rsqrt(jnp.sum(x * x, axis=-1, keepdims=True) * (1.0 / n) + EPS)
    return x * r, r


def _rms_bwd(xhat, r, g, dy, n):
    dxh = dy * g
    dx = r * (dxh - xhat * (jnp.sum(dxh * xhat, axis=-1, keepdims=True) * (1.0 / n)))
    return dx, dy * xhat


def _rope(x, cosp, sina, sinb):
    return x * cosp + pltpu.roll(x, 96, 1) * sina + pltpu.roll(x, 32, 1) * sinb


def _rope_t(d, cosp, sina, sinb):
    return d * cosp + pltpu.roll(d * sina, 32, 1) + pltpu.roll(d * sinb, 96, 1)


def _dot_nt(a, b):
    return lax.dot_general(a, b, (((1,), (1,)), ((), ())), preferred_element_type=F32)


def _dot_tn(a, b):
    return lax.dot_general(a, b, (((0,), (0,)), ((), ())), preferred_element_type=F32)


def _dot(a, b):
    return jnp.dot(a, b, preferred_element_type=F32)


def _all_gather(shards, name):
    na = len(shards)

    def body(*refs):
        x_refs, out_refs = refs[:na], refs[na:2 * na]
        send_sems, recv_sems, local_sems = refs[2 * na:]
        x, y, c = lax.axis_index("x"), lax.axis_index("y"), lax.axis_index("c")
        me, sibling = (x, y, c), (x, y, 1 - c)
        chips = [(1 - x, y), (x, 1 - y), (1 - x, 1 - y)]

        def rows(a, px, py, pc):
            return out_refs[a].at[4 * px + 2 * py + pc]

        def copy(a, k, block, to, src=None):
            return pltpu.make_async_remote_copy(
                src_ref=rows(a, *block) if src is None else src, dst_ref=rows(a, *block),
                send_sem=send_sems.at[7 * a + k], recv_sem=recv_sems.at[7 * a + k],
                device_id=to, device_id_type=MESH)

        mine = [pltpu.make_async_copy(x_refs[a], rows(a, *me), local_sems.at[a]) for a in range(na)]
        for cp in mine:
            cp.start()
        first = [copy(a, 0, me, sibling, src=x_refs[a]) for a in range(na)]
        first += [copy(a, 1 + j, me, (*chip, c), src=x_refs[a]) for j, chip in enumerate(chips) for a in range(na)]
        for cp in first:
            cp.start()
        passed = []
        for j, chip in enumerate(chips):
            for a in range(na):
                copy(a, 1 + j, (*chip, c), me).wait_recv()
                fwd = copy(a, 4 + j, (*chip, c), sibling)
                fwd.start()
                passed.append(fwd)
        for a in range(na):
            copy(a, 0, sibling, me).wait_recv()
        for j, chip in enumerate(chips):
            for a in range(na):
                copy(a, 4 + j, (*chip, 1 - c), me).wait_recv()
        for cp in first + passed:
            cp.wait_send()
        for cp in mine:
            cp.wait()

    any_spec = pl.BlockSpec(memory_space=pl.ANY)
    return pl.pallas_call(
        body, name=name,
        out_shape=[jax.ShapeDtypeStruct((NDEV,) + s.shape, s.dtype) for s in shards],
        in_specs=[any_spec] * na, out_specs=[any_spec] * na,
        scratch_shapes=[pltpu.SemaphoreType.DMA((7 * na,)), pltpu.SemaphoreType.DMA((7 * na,)),
                        pltpu.SemaphoreType.DMA((na,))],
    )(*shards)


_HBM = pl.BlockSpec(memory_space=pltpu.HBM)
_SEM = pl.BlockSpec(memory_space=pltpu.SEMAPHORE)
_EFFECT = pltpu.SideEffectType.DATAFLOW_SIDE_EFFECTING


def _split_copy(a, k, src_refs, land_refs, send_sems, recv_sems, gather, receive_side):
    x, y, c = lax.axis_index("x"), lax.axis_index("y"), lax.axis_index("c")
    me = 4 * x + 2 * y + c
    tx, ty, tc = x ^ ((k + 1) >> 2 & 1), y ^ ((k + 1) >> 1 & 1), c ^ ((k + 1) & 1)
    peer = 4 * tx + 2 * ty + tc
    return pltpu.make_async_remote_copy(
        src_ref=src_refs[a] if gather else src_refs[a].at[peer],
        dst_ref=land_refs[a].at[peer if receive_side else me],
        send_sem=send_sems.at[7 * a + k], recv_sem=recv_sems.at[7 * a + k],
        device_id=(tx, ty, tc), device_id_type=MESH)


def _split_start(srcs, lands, gather, name, after=None):
    na = len(srcs)
    extra = [] if after is None else [after]

    def body(*refs):
        src_refs, land_refs = refs[:na], refs[na:2 * na]
        send_sems, recv_sems = refs[2 * na + len(extra)], refs[2 * na + len(extra) + 1]
        token = refs[-1]
        for k in range(7):
            for a in range(na):
                _split_copy(a, k, src_refs, land_refs, send_sems, recv_sems, gather, False).start()
        token[...] = jnp.zeros_like(token)

    hbm = [pltpu.HBM(b.shape, b.dtype) for b in list(srcs) + list(lands)]
    outs = pl.pallas_call(
        body, name=name,
        out_shape=(pltpu.SemaphoreType.DMA((7 * na,)), pltpu.SemaphoreType.DMA((7 * na,)), *hbm,
                   jax.ShapeDtypeStruct((8, LANES), F32)),
        in_specs=[_HBM] * (2 * na) + [pl.BlockSpec(memory_space=pl.ANY)] * len(extra),
        out_specs=(_SEM, _SEM, *[_HBM] * (2 * na), pl.BlockSpec(memory_space=pltpu.VMEM)),
        input_output_aliases={j: 2 + j for j in range(2 * na)},
        compiler_params=pltpu.CompilerParams(has_side_effects=_EFFECT),
    )(*[pltpu.with_memory_space_constraint(b, pltpu.HBM) for b in srcs],
      *[pltpu.with_memory_space_constraint(l, pltpu.HBM) for l in lands], *extra)
    return outs[0], outs[1], outs[2:2 + na], outs[2 + na:2 + 2 * na], outs[-1]


def _split_wait(started, gather, after, name):
    send_sems, recv_sems, srcs, lands, _ = started
    na = len(srcs)

    def body(*refs):
        src_refs, land_refs = refs[:na], refs[na:2 * na]
        send_s, recv_s = refs[2 * na], refs[2 * na + 1]
        for k in range(7):
            for a in range(na):
                cp = _split_copy(a, k, src_refs, land_refs, send_s, recv_s, gather, True)
                cp.wait_send()
                cp.wait_recv()

    hbm = [pltpu.HBM(b.shape, b.dtype) for b in list(srcs) + list(lands)]
    outs = pl.pallas_call(
        body, name=name, out_shape=tuple(hbm),
        in_specs=[_HBM] * (2 * na) + [_SEM, _SEM, pl.BlockSpec(memory_space=pl.ANY)],
        out_specs=tuple([_HBM] * (2 * na)),
        input_output_aliases={j: j for j in range(2 * na)},
        compiler_params=pltpu.CompilerParams(has_side_effects=_EFFECT),
    )(*srcs, *lands, send_sems, recv_sems, after)
    return outs[na:]


def _seg_rows(size):
    rows = -(-size // PACK_C)
    return -(-rows // 16) * 16


def _pack(arrs, lead):
    parts = []
    for a in arrs:
        lshape = a.shape[:lead]
        f = a.reshape(lshape + (-1,)).astype(F32)
        rows = _seg_rows(f.shape[-1])
        f = jnp.pad(f, [(0, 0)] * lead + [(0, rows * PACK_C - f.shape[-1])])
        parts.append(f.reshape(lshape + (rows, PACK_C)))
    return jnp.concatenate(parts, axis=lead)


def _unpack(buf, shapes):
    lshape = buf.shape[:-2]
    out, r = [], 0
    for shp in shapes:
        size = math.prod(shp)
        rows = _seg_rows(size)
        seg = buf[..., r:r + rows, :].reshape(lshape + (rows * PACK_C,))[..., :size]
        out.append(seg.reshape(lshape + tuple(shp)))
        r += rows
    return out


def _mm(a, b, *, name, out_dtype, ta=False, tb=False, bm=1024, bn=1024, bk=2048, after=None, plus=None):
    if ta:
        kdim, m = a.shape
    else:
        m, kdim = a.shape
    n, k2 = b.shape if tb else b.shape[::-1]
    assert kdim == k2 and not (ta and tb), (a.shape, b.shape)
    bm = _tile(m, bm, LANES if ta else 16)
    bn = _tile(n, bn, LANES)
    bk = _tile(kdim, bk, LANES)
    nk = kdim // bk
    n_after = 0 if after is None else 1
    n_plus = 0 if plus is None else 2

    def kern(a_ref, b_ref, *rest):
        plus_refs = rest[n_after:n_after + n_plus]
        o_ref, scratch = rest[n_after + n_plus], rest[n_after + n_plus + 1:]
        part = (_dot_tn if ta else _dot_nt if tb else _dot)(a_ref[...], b_ref[...])

        def first(p):
            return p + _dot(plus_refs[0][...], plus_refs[1][...]) if plus is not None else p

        if nk == 1:
            o_ref[...] = first(part).astype(o_ref.dtype)
        else:
            acc = scratch[0] if scratch else o_ref
            k = pl.program_id(2)

            @pl.when(k == 0)
            def _():
                acc[...] = first(jnp.zeros(acc.shape, F32))

            acc[...] += part
            if scratch:
                @pl.when(k == nk - 1)
                def _():
                    o_ref[...] = acc[...].astype(o_ref.dtype)

    a_spec = pl.BlockSpec((bk, bm), lambda i, j, k: (k, i)) if ta else pl.BlockSpec((bm, bk), lambda i, j, k: (i, k))
    b_spec = pl.BlockSpec((bn, bk), lambda i, j, k: (j, k)) if tb else pl.BlockSpec((bk, bn), lambda i, j, k: (k, j))
    extra_specs, extra_args = [], []
    if after is not None:
        extra_specs.append(pl.BlockSpec(after.shape, lambda i, j, k: (0, 0)))
        extra_args.append(after)
    if plus is not None:
        kk = plus[0].shape[1]
        extra_specs += [pl.BlockSpec((bm, kk), lambda i, j, k: (i, 0)), pl.BlockSpec((kk, bn), lambda i, j, k: (0, j))]
        extra_args += list(plus)
    return pl.pallas_call(
        kern, name=name, grid=(m // bm, n // bn, nk),
        in_specs=[a_spec, b_spec] + extra_specs,
        out_specs=pl.BlockSpec((bm, bn), lambda i, j, k: (i, j)),
        out_shape=jax.ShapeDtypeStruct((m, n), out_dtype),
        scratch_shapes=[pltpu.VMEM((bm, bn), F32)] if nk > 1 and out_dtype != F32 else [],
        compiler_params=_cparams(("parallel", "parallel", "arbitrary")),
    )(a, b, *extra_args)


def _adam(parts, w_a, m_a, v_a, name):
    rows, cols = w_a.shape
    rb = _tile(rows, max(8, ADAM_BLOCK_ELEMS // cols // 8 * 8), 8)
    bc1 = 1.0 - ADAM_B1 ** ADAM_STEP
    bc2 = 1.0 - ADAM_B2 ** ADAM_STEP

    def adam_kern(p_ref, w_ref, m_ref, v_ref, g_ref, d_ref, nm_ref, nv_ref):
        g = p_ref[0].astype(F32)
        for j in range(1, NDEV):
            g = g + p_ref[j].astype(F32)
        m_new = ADAM_B1 * m_ref[...] + (1.0 - ADAM_B1) * g
        v_new = ADAM_B2 * v_ref[...] + (1.0 - ADAM_B2) * (g * g)
        g_ref[...] = g
        nm_ref[...] = m_new
        nv_ref[...] = v_new
        d_ref[...] = -ADAM_LR * ((m_new / bc1) / (jnp.sqrt(v_new / bc2) + ADAM_EPS) + ADAM_WD * w_ref[...])

    blk = pl.BlockSpec((rb, cols), lambda i: (i, 0))
    return pl.pallas_call(
        adam_kern, name=name, grid=(rows // rb,),
        in_specs=[pl.BlockSpec((NDEV, rb, cols), lambda i: (0, i, 0)), blk, blk, blk],
        out_specs=[blk] * 4, out_shape=[jax.ShapeDtypeStruct((rows, cols), F32)] * 4,
        compiler_params=_cparams(("parallel",)),
    )(parts, w_a, m_a, v_a)


class _Cfg:
    pass


def _config(x, conv_w, w_uq, w_ukv, mla_qn_nope_g, mla_qn_rope_g, mem, mem_qn_g, w_mem_out, w_mla_out):
    c = _Cfg()
    c.N, c.D = x.shape[1], x.shape[2]
    c.CW = conv_w.shape[2] * NDEV
    c.QL, c.KVL = w_uq.shape[1], w_ukv.shape[1]
    c.NOPE, c.ROPE = mla_qn_nope_g.shape[1], mla_qn_rope_g.shape[1]
    c.H = w_uq.shape[2] * NDEV // (c.NOPE + c.ROPE)
    c.V = w_ukv.shape[2] * NDEV // c.H - c.NOPE
    assert c.NOPE == LANES and c.V == LANES and c.ROPE == LANES // 2
    c.HW = 2 * LANES
    c.HV = c.H * c.V
    assert w_mla_out.shape[1] * NDEV == c.HV
    c.M = mem.shape[1]
    c.MHD = mem_qn_g.shape[1]
    c.MW = w_mem_out.shape[1]
    c.MH = c.MW // c.MHD
    c.o_conv = 0
    c.o_mz = 4 * c.CW
    c.o_g = c.o_mz + c.HV
    c.o_mem = c.o_g + 3 * c.D
    c.o_lora = c.o_mem + 2 * c.MW
    c.P = c.o_lora + c.QL + c.KVL
    assert c.o_mz % c.HV == 0 and c.o_g % (3 * c.D) == 0 and c.o_mem % (2 * c.MW) == 0
    assert c.o_lora % (c.QL + c.KVL) == 0 and c.QL % LANES == 0 and c.KVL % LANES == 0
    c.IN = 4 * c.CW + c.QL + c.KVL + c.ROPE + c.HV + 2 * c.MW + 3 * c.D
    c.R = _tile(c.N, 256, 16)
    c.RP = _tile(c.N, 512, 16)
    c.HG = _tile(c.H, 4, 1)
    c.B = _tile(c.N, ATT_BLOCK, CHUNK)
    c.scale = float((c.NOPE + c.ROPE) ** -0.5)
    c.mscale = float(c.MHD ** -0.5)
    return c


def _win_segments(c):
    ref_order = (('conv', 4 * c.CW), ('lora', c.QL + c.KVL), ('kr', c.ROPE), ('mz', c.HV), ('mem', 2 * c.MW), ('g', 3 * c.D))
    mine = {'conv': c.o_conv, 'mz': c.o_mz, 'g': c.o_g, 'mem': c.o_mem, 'lora': c.o_lora, 'kr': 0}
    segs, o = [], 0
    for nm, wd in ref_order:
        segs.append((nm, o, wd, mine[nm]))
        o += wd
    return segs


def _win_split(g_win, c):
    n8 = g_win.shape[2]

    def columns(a, wd):
        return [g_win[j][:, max(a, j * n8) - j * n8:min(a + wd, (j + 1) * n8) - j * n8]
                for j in range(a // n8, (a + wd - 1) // n8 + 1)]

    segs = {nm: (a, wd) for nm, a, wd, _ in _win_segments(c)}
    main = [p for nm in ('conv', 'mz', 'g', 'mem', 'lora') for p in columns(*segs[nm])]
    kr = jnp.concatenate(columns(*segs['kr']) + [jnp.zeros((g_win.shape[1], LANES - c.ROPE), g_win.dtype)], axis=1)
    return jnp.concatenate(main, axis=1), kr


def _win_blocks(g, g_kr, c):
    n8 = c.IN // NDEV
    blocks = []
    for j in range(NDEV):
        lo, hi = j * n8, (j + 1) * n8
        parts = []
        for nm, a, wd, mine in _win_segments(c):
            s, e = max(a, lo), min(a + wd, hi)
            if s < e:
                parts.append((g_kr if nm == 'kr' else g)[:, mine + s - a:mine + e - a])
        blocks.append(jnp.concatenate(parts, axis=1))
    return jnp.stack(blocks, axis=0)


def kernel(x, positions, mem, norm_g, w_in, conv_w, w_conv_out, mla_q_norm_g, w_uq, mla_kv_norm_g, w_ukv, mla_qn_nope_g, mla_qn_rope_g, mla_kn_nope_g, mla_kn_rope_g, w_mla_out, mem_norm_g, w_mem_kv, mem_qn_g, mem_kn_g, w_mem_out, w_o, loss_target, m_norm_g, m_w_in, m_conv_w, m_w_conv_out, m_mla_q_norm_g, m_w_uq, m_mla_kv_norm_g, m_w_ukv, m_mla_qn_nope_g, m_mla_qn_rope_g, m_mla_kn_nope_g, m_mla_kn_rope_g, m_w_mla_out, m_mem_norm_g, m_w_mem_kv, m_mem_qn_g, m_mem_kn_g, m_w_mem_out, m_w_o, v_norm_g, v_w_in, v_conv_w, v_w_conv_out, v_mla_q_norm_g, v_w_uq, v_mla_kv_norm_g, v_w_ukv, v_mla_qn_nope_g, v_mla_qn_rope_g, v_mla_kn_nope_g, v_mla_kn_rope_g, v_w_mla_out, v_mem_norm_g, v_w_mem_kv, v_mem_qn_g, v_mem_kn_g, v_w_mem_out, v_w_o):
    args = dict(locals())
    W = {n: args[n] for n in WEIGHTS}
    Mo = {n: args['m_' + n] for n in WEIGHTS}
    Vo = {n: args['v_' + n] for n in WEIGHTS}
    c = _config(x, conv_w, w_uq, w_ukv, mla_qn_nope_g, mla_qn_rope_g, mem, mem_qn_g, w_mem_out, w_mla_out)
    N, D, R, B, H = c.N, c.D, c.R, c.B, c.H
    assert x.shape[0] == 1
    xs = x[0]
    tgt = loss_target[0]
    memx = mem[0]
    me = 4 * lax.axis_index("x") + 2 * lax.axis_index("y") + lax.axis_index("c")
    nr = N // R

    g_win, g_taps = _all_gather([W['w_in'][0].astype(_BF), conv_w[0]], "ag_w_in")
    rest = [n for n in BIG if n != 'w_in']
    shards_r = [W[n][0].astype(_BF) for n in rest]
    lands_r = [lax.dynamic_update_index_in_dim(lax.empty((NDEV,) + s.shape, s.dtype), s[None], me, 0)
               for s in shards_r]
    ag_rest = _split_start(shards_r, lands_r, True, "ag_rest_start", after=g_win)
    win_p, w_kr = _win_split(g_win, c)
    w_krT = w_kr.T
    convw = jnp.transpose(g_taps, (1, 0, 2)).reshape(3, c.CW)
    convw8 = jnp.pad(convw, ((0, 5), (0, 0)))

    def rowb(width, cidx):
        return pl.BlockSpec((R, width), lambda i, _c=cidx: (i, _c))

    def fullb(shape):
        nd = len(shape)
        return pl.BlockSpec(shape, lambda *_: (0,) * nd)

    def pad_lanes(g, w=LANES):
        return jnp.pad(g, ((0, 0), (0, w - g.shape[1])))

    def tabs_of(rows):
        return pl.BlockSpec((3, rows, LANES), lambda i, *_: (0, i, 0))

    half = c.ROPE // 2
    inv_freq = jnp.power(ROPE_THETA, -jnp.arange(half, dtype=F32) / half)
    invf = jnp.concatenate([inv_freq, inv_freq, jnp.zeros((LANES - c.ROPE,), F32)])[None, :]
    pos_col = positions[0].astype(F32).reshape(N, 1)

    def rope_tab_kern(pos_ref, invf_ref, o_ref):
        ang = pos_ref[...] * invf_ref[...]
        co, si = jnp.cos(ang), jnp.sin(ang)
        lane = lax.broadcasted_iota(jnp.int32, ang.shape, 1)
        o_ref[0] = jnp.where(lane < c.ROPE, co, 0.0)
        o_ref[1] = jnp.where(lane < half, -si, 0.0)
        o_ref[2] = jnp.where(jnp.logical_and(lane >= half, lane < c.ROPE), si, 0.0)

    tabs = pl.pallas_call(
        rope_tab_kern, name="rope_tab", grid=(nr,),
        in_specs=[pl.BlockSpec((R, 1), lambda i: (i, 0)), fullb((1, LANES))],
        out_specs=tabs_of(R),
        out_shape=jax.ShapeDtypeStruct((3, N, LANES), F32),
        compiler_params=_cparams(("parallel",)),
    )(pos_col, invf)

    def make_rms_kern():
        def rms_fwd_kern(x_ref, g_ref, o_ref):
            xh, _ = _rms(x_ref[...].astype(F32), x_ref.shape[-1])
            o_ref[...] = (xh * g_ref[...]).astype(o_ref.dtype)
        return rms_fwd_kern

    h = pl.pallas_call(
        make_rms_kern(), name="rms_x", grid=(nr,),
        in_specs=[rowb(D, 0), fullb((1, D))], out_specs=rowb(D, 0),
        out_shape=jax.ShapeDtypeStruct((N, D), _BF), compiler_params=_cparams(("parallel",)),
    )(xs, norm_g)

    proj = _mm(h, win_p, name="mm_proj", out_dtype=_BF, after=ag_rest[4])
    kr_raw = _mm(h, w_kr, name="mm_kr", out_dtype=_BF)

    Wf = {}
    for n, g in zip(rest, _split_wait(ag_rest, True, proj, "ag_rest_wait")):
        if n in COL_SHARDED:
            Wf[n] = jnp.transpose(g, (1, 0, 2)).reshape(g.shape[1], -1)
        else:
            Wf[n] = g.reshape(-1, g.shape[2])
    wuq = Wf['w_uq'].reshape(c.QL, H, c.NOPE + c.ROPE)
    wuq_p = jnp.pad(wuq, ((0, 0), (0, 0), (0, c.HW - c.NOPE - c.ROPE))).reshape(c.QL, H * c.HW)
    wukv = Wf['w_ukv']
    wco, wmo, wmkv, wmemo, wo = Wf['w_conv_out'], Wf['w_mla_out'], Wf['w_mem_kv'], Wf['w_mem_out'], Wf['w_o']

    CW = c.CW
    conv_blk = c.o_conv // (4 * CW)
    HALO = 16
    rh = R // HALO

    def conv_parts(blk):
        blk = blk.astype(F32)
        return blk[:, 0:CW], blk[:, CW:2 * CW], blk[:, 2 * CW:3 * CW], blk[:, 3 * CW:4 * CW]

    def shifted(cu, prev, i):
        prev = jnp.where(i > 0, prev, 0.0)
        rid = lax.broadcasted_iota(jnp.int32, cu.shape, 0)
        last, last2 = prev[HALO - 1:HALO, :], prev[HALO - 2:HALO - 1, :]
        sh1 = jnp.where(rid == 0, last, pltpu.roll(cu, 1, 0))
        sh2 = jnp.where(rid == 0, last2, jnp.where(rid == 1, last, pltpu.roll(cu, 2, 0)))
        return sh1, sh2

    def conv_fwd_kern(p_ref, prev_ref, w_ref, o_ref):
        i = pl.program_id(0)
        cg, bg, u, z = conv_parts(p_ref[...])
        pc, _, pu, _ = conv_parts(prev_ref[...])
        cu = cg * u
        sh1, sh2 = shifted(cu, pc * pu, i)
        w = w_ref[...]
        conv = w[0:1, :] * sh2 + w[1:2, :] * sh1 + w[2:3, :] * cu
        o_ref[...] = (bg * conv * (z * _sig(z))).astype(o_ref.dtype)

    prev_spec = pl.BlockSpec((HALO, 4 * CW), lambda i: (jnp.maximum(i * rh - 1, 0), conv_blk))
    a_conv = pl.pallas_call(
        conv_fwd_kern, name="conv_fwd", grid=(nr,),
        in_specs=[rowb(4 * CW, conv_blk), prev_spec, fullb((8, CW))],
        out_specs=rowb(CW, 0), out_shape=jax.ShapeDtypeStruct((N, CW), _BF),
        compiler_params=_cparams(("parallel",)),
    )(proj, proj, convw8)
    o_conv = _mm(a_conv, wco, name="mm_oconv", out_dtype=_BF)

    QL, KVL, HW = c.QL, c.KVL, c.HW
    lora_blk = c.o_lora // (QL + KVL)

    def lora_fwd_kern(p_ref, gq_ref, gkv_ref, q_ref, kv_ref):
        blk = p_ref[...].astype(F32)
        qh, _ = _rms(blk[:, :QL], QL)
        kh, _ = _rms(blk[:, QL:], KVL)
        q_ref[...] = (qh * gq_ref[...]).astype(q_ref.dtype)
        kv_ref[...] = (kh * gkv_ref[...]).astype(kv_ref.dtype)

    cqn, ckvn = pl.pallas_call(
        lora_fwd_kern, name="lora_fwd", grid=(nr,),
        in_specs=[rowb(QL + KVL, lora_blk), fullb((1, QL)), fullb((1, KVL))],
        out_specs=[rowb(QL, 0), rowb(KVL, 0)],
        out_shape=[jax.ShapeDtypeStruct((N, QL), _BF), jax.ShapeDtypeStruct((N, KVL), _BF)],
        compiler_params=_cparams(("parallel",)),
    )(proj, mla_q_norm_g, mla_kv_norm_g)
    q_p = _mm(cqn, wuq_p, name="mm_q", out_dtype=_BF)
    kv = _mm(ckvn, wukv, name="mm_kv", out_dtype=_BF)

    g_qn, g_qr = mla_qn_nope_g, pad_lanes(mla_qn_rope_g)
    g_kn, g_kr = mla_kn_nope_g, pad_lanes(mla_kn_rope_g)

    def krope_fwd_kern(p_ref, t_ref, g_ref, o_ref):
        xh, _ = _rms(p_ref[...].astype(F32), c.ROPE)
        o_ref[...] = _rope(xh * g_ref[...], t_ref[0], t_ref[1], t_ref[2]).astype(o_ref.dtype)

    k_rope = pl.pallas_call(
        krope_fwd_kern, name="krope_fwd", grid=(nr,),
        in_specs=[rowb(LANES, 0), tabs_of(R), fullb((1, LANES))],
        out_specs=rowb(LANES, 0), out_shape=jax.ShapeDtypeStruct((N, LANES), _BF),
        compiler_params=_cparams(("parallel",)),
    )(kr_raw, tabs, g_kr)

    RP, HG = c.RP, c.HG
    nrp, nhg = N // RP, H // HG
    heads_in = pl.BlockSpec((RP, HG * HW), lambda i, hg: (i, hg))
    heads_out = pl.BlockSpec((HG, RP, HW), lambda i, hg: (hg, i, 0))

    def q_prep_kern(q_ref, t_ref, gn_ref, gr_ref, o_ref):
        for g in range(HG):
            blk = q_ref[:, g * HW:(g + 1) * HW].astype(F32)
            nh, _ = _rms(blk[:, :LANES], c.NOPE)
            rhat, _ = _rms(blk[:, LANES:], c.ROPE)
            rot = _rope(rhat * gr_ref[...], t_ref[0], t_ref[1], t_ref[2])
            o_ref[g] = (jnp.concatenate([nh * gn_ref[...], rot], axis=1) * (c.scale * LOG2E)).astype(o_ref.dtype)

    q_cat = pl.pallas_call(
        q_prep_kern, name="q_prep", grid=(nrp, nhg),
        in_specs=[heads_in, tabs_of(RP), fullb((1, LANES)), fullb((1, LANES))],
        out_specs=heads_out, out_shape=jax.ShapeDtypeStruct((H, N, HW), _BF),
        compiler_params=_cparams(("parallel", "parallel")),
    )(q_p, tabs, g_qn, g_qr)

    def k_prep_kern(kv_ref, kr_ref, gn_ref, o_ref):
        for g in range(HG):
            kn, _ = _rms(kv_ref[:, g * HW:g * HW + LANES].astype(F32), c.NOPE)
            o_ref[g] = jnp.concatenate([(kn * gn_ref[...]).astype(o_ref.dtype), kr_ref[...]], axis=1)

    k_cat = pl.pallas_call(
        k_prep_kern, name="k_prep", grid=(nrp, nhg),
        in_specs=[heads_in, pl.BlockSpec((RP, LANES), lambda i, hg: (i, 0)), fullb((1, LANES))],
        out_specs=heads_out, out_shape=jax.ShapeDtypeStruct((H, N, HW), _BF),
        compiler_params=_cparams(("parallel", "parallel")),
    )(kv, k_rope, g_kn)

    nb = N // B
    assert CHUNK & (CHUNK - 1) == 0 and B % CHUNK == 0

    def diag_mask(s, row0=0):
        row = lax.broadcasted_iota(jnp.int32, s.shape, 0) + row0
        col = lax.broadcasted_iota(jnp.int32, s.shape, 1)
        shift = CHUNK.bit_length() - 1
        allowed = jnp.right_shift(col, shift) <= jnp.right_shift(row, shift)
        return jnp.where(allowed, s, NEG)

    k_head = pl.BlockSpec((1, N, HW), lambda hh, i: (hh, 0, 0))
    v_head = pl.BlockSpec((N, LANES), lambda hh, i: (0, 2 * hh + 1))
    q_blk = pl.BlockSpec((1, B, HW), lambda hh, i: (hh, i, 0))
    o_blk = pl.BlockSpec((B, LANES), lambda hh, i: (i, hh))
    lse_blk = pl.BlockSpec((1, B, LANES), lambda hh, i: (hh, i, 0))

    def attn_fwd_kern(q_ref, k_ref, v_ref, o_ref, lse_ref, m_sc, acc_sc, s_sc):
        i = pl.program_id(1)
        m_sc[...] = jnp.full(m_sc.shape, NEG, F32)
        acc_sc[...] = jnp.zeros(acc_sc.shape, F32)

        def rows_of(t):
            return pl.ds(pl.multiple_of(t * B, B), B)

        def scores(t, slot):
            s_sc[slot] = _dot_nt(q_ref[0], k_ref[0, rows_of(t), :])

        def softmax_pv(t, slot, masked):
            s = s_sc[slot]
            if masked:
                s = diag_mask(s)
            mt = s[:, 0:LANES]
            for cb in range(1, B // LANES):
                mt = jnp.maximum(mt, s[:, cb * LANES:(cb + 1) * LANES])
            m_prev = m_sc[...]
            m_new = jnp.maximum(m_prev, jnp.max(mt, axis=1, keepdims=True))
            alpha = jnp.exp2(m_prev - m_new)
            p = jnp.concatenate([jnp.exp2(s[:, cb * LANES:(cb + 1) * LANES] - m_new).astype(_BF)
                                 for cb in range(B // LANES)], axis=1)
            v_ones = jnp.concatenate([v_ref[rows_of(t), :], jnp.ones((B, LANES), _BF)], axis=1)
            acc_sc[...] = jnp.concatenate([alpha, alpha], axis=1) * acc_sc[...] + _dot(p, v_ones)
            m_sc[...] = m_new

        scores(0, 0)

        def run(first, count, last_masked):
            for u in range(count):
                if u + 1 < count or not last_masked:
                    scores(first + u + 1, (u + 1) % 2)
                softmax_pv(first + u, u % 2, last_masked and u == count - 1)

        def unrolled(u, carry):
            run(ATT_UNROLL * u, ATT_UNROLL, False)
            return carry

        lax.fori_loop(0, i // ATT_UNROLL, unrolled, 0)
        for rem in range(ATT_UNROLL):
            @pl.when(i % ATT_UNROLL == rem)
            def _(rem=rem):
                run(i - rem, rem + 1, True)

        acc = acc_sc[...]
        o_ref[...] = (acc[:, :LANES] / acc[:, LANES:]).astype(o_ref.dtype)
        lse_ref[0] = m_sc[...] + jnp.log2(acc[:, LANES:])

    mla_y, lse = pl.pallas_call(
        attn_fwd_kern, name="attn_fwd", grid=(H, nb),
        in_specs=[q_blk, k_head, v_head], out_specs=[o_blk, lse_blk],
        out_shape=[jax.ShapeDtypeStruct((N, c.HV), _BF), jax.ShapeDtypeStruct((H, N, LANES), F32)],
        scratch_shapes=[pltpu.VMEM((B, LANES), F32), pltpu.VMEM((B, HW), F32), pltpu.VMEM((2, B, B), F32)],
        compiler_params=_cparams(("parallel", "arbitrary")),
    )(q_cat, k_cat, kv)

    HV = c.HV
    mz_blk = c.o_mz // HV

    def gate_fwd_kern(y_ref, z_ref, o_ref):
        z = z_ref[...].astype(F32)
        o_ref[...] = (y_ref[...].astype(F32) * (z * _sig(z))).astype(o_ref.dtype)

    a_mla = pl.pallas_call(
        gate_fwd_kern, name="gate_mla", grid=(nr,),
        in_specs=[rowb(HV, 0), rowb(HV, mz_blk)], out_specs=rowb(HV, 0),
        out_shape=jax.ShapeDtypeStruct((N, HV), _BF), compiler_params=_cparams(("parallel",)),
    )(mla_y, proj)
    o_mla = _mm(a_mla, wmo, name="mm_omla", out_dtype=_BF)

    M, MW, MH, MHD = c.M, c.MW, c.MH, c.MHD
    memn = pl.pallas_call(
        make_rms_kern(), name="rms_mem",
        grid=(1,), in_specs=[fullb((M, D)), fullb((1, D))], out_specs=fullb((M, D)),
        out_shape=jax.ShapeDtypeStruct((M, D), _BF), compiler_params=_cparams(("arbitrary",)),
    )(memx, mem_norm_g)
    kvm = _mm(memn, wmkv, name="mm_memkv", out_dtype=F32)

    def memk_fwd_kern(kv_ref, g_ref, k_ref, v_ref):
        for hh in range(MH):
            kh, _ = _rms(kv_ref[:, hh * MHD:(hh + 1) * MHD], MHD)
            k_ref[:, hh * MHD:(hh + 1) * MHD] = (kh * g_ref[...]).astype(k_ref.dtype)
        v_ref[...] = kv_ref[:, MW:].astype(v_ref.dtype)

    mem_k, mem_v = pl.pallas_call(
        memk_fwd_kern, name="memk_fwd", grid=(1,),
        in_specs=[fullb((M, 2 * MW)), fullb((1, MHD))], out_specs=[fullb((M, MW)), fullb((M, MW))],
        out_shape=[jax.ShapeDtypeStruct((M, MW), _BF)] * 2, compiler_params=_cparams(("arbitrary",)),
    )(kvm, mem_kn_g)

    mem_blk = c.o_mem // (2 * MW)

    def mem_head(qz_ref, k_ref, v_ref, g_ref, hh):
        sl = slice(hh * MHD, (hh + 1) * MHD)
        qh, r = _rms(qz_ref[:, sl].astype(F32), MHD)
        qn = (qh * g_ref[...]).astype(_BF)
        s = _dot_nt(qn, k_ref[:, sl]) * c.mscale
        e = jnp.exp(s - jnp.max(s, axis=1, keepdims=True))
        p = e / jnp.sum(e, axis=1, keepdims=True)
        y = _dot(p.astype(_BF), v_ref[:, sl])
        z = qz_ref[:, MW + hh * MHD:MW + (hh + 1) * MHD].astype(F32)
        return sl, qh, r, qn, p, y, z

    def mem_fwd_kern(qz_ref, k_ref, v_ref, g_ref, o_ref):
        for hh in range(MH):
            sl, _, _, _, _, y, z = mem_head(qz_ref, k_ref, v_ref, g_ref, hh)
            o_ref[:, sl] = (y * (z * _sig(z))).astype(o_ref.dtype)

    a_mem = pl.pallas_call(
        mem_fwd_kern, name="mem_fwd", grid=(nr,),
        in_specs=[rowb(2 * MW, mem_blk), fullb((M, MW)), fullb((M, MW)), fullb((1, MHD))],
        out_specs=rowb(MW, 0), out_shape=jax.ShapeDtypeStruct((N, MW), _BF),
        compiler_params=_cparams(("parallel",)),
    )(proj, mem_k, mem_v, mem_qn_g)
    o_mem = _mm(a_mem, wmemo, name="mm_omem", out_dtype=_BF)

    g_blk = c.o_g // (3 * D)

    def merge_fwd_kern(g_ref, oc_ref, om_ref, ome_ref, o_ref):
        g = g_ref[...].astype(F32)
        acc = _sig(g[:, :D]) * oc_ref[...].astype(F32)
        acc += _sig(g[:, D:2 * D]) * om_ref[...].astype(F32)
        acc += _sig(g[:, 2 * D:]) * ome_ref[...].astype(F32)
        o_ref[...] = acc.astype(o_ref.dtype)

    merged = pl.pallas_call(
        merge_fwd_kern, name="merge_fwd", grid=(nr,),
        in_specs=[rowb(3 * D, g_blk), rowb(D, 0), rowb(D, 0), rowb(D, 0)], out_specs=rowb(D, 0),
        out_shape=jax.ShapeDtypeStruct((N, D), _BF), compiler_params=_cparams(("parallel",)),
    )(proj, o_conv, o_mla, o_mem)
    y2 = _mm(merged, wo, name="mm_out", out_dtype=F32)

    def loss_kern(x_ref, y_ref, t_ref, dy_ref, dyb_ref, l_ref):
        e = x_ref[...] + y_ref[...] - t_ref[...]
        dy = e * (1.0 / D)
        dy_ref[...] = dy
        dyb_ref[...] = dy.astype(dyb_ref.dtype)

        @pl.when(pl.program_id(0) == 0)
        def _():
            l_ref[...] = jnp.zeros(l_ref.shape, F32)

        l_ref[...] += jnp.sum(e * e, axis=0, keepdims=True)

    dy, dyb, lpart = pl.pallas_call(
        loss_kern, name="loss", grid=(nr,),
        in_specs=[rowb(D, 0)] * 3, out_specs=[rowb(D, 0), rowb(D, 0), fullb((1, D))],
        out_shape=[jax.ShapeDtypeStruct((N, D), F32), jax.ShapeDtypeStruct((N, D), _BF),
                   jax.ShapeDtypeStruct((1, D), F32)],
        compiler_params=_cparams(("arbitrary",)),
    )(xs, y2, tgt)
    loss = lax.psum(jnp.sum(lpart) * (0.5 / D), AXES)

    G = {}
    d_merged = _mm(dyb, wo, tb=True, name="mm_dmerged", out_dtype=_BF)
    G['w_o'] = _mm(merged, dyb, ta=True, name="mm_dwo", out_dtype=_BF)

    dproj0 = lax.empty((N, c.P), _BF)
    any_spec = pl.BlockSpec(memory_space=pl.ANY)

    def merge_bwd_kern(dp_any, g_ref, dm_ref, oc_ref, om_ref, ome_ref, dg_ref, doc_ref, dom_ref, dome_ref):
        g = g_ref[...].astype(F32)
        dm = dm_ref[...].astype(F32)
        for idx, (o_in, d_out) in enumerate(((oc_ref, doc_ref), (om_ref, dom_ref), (ome_ref, dome_ref))):
            sg = _sig(g[:, idx * D:(idx + 1) * D])
            d_out[...] = (sg * dm).astype(d_out.dtype)
            dg_ref[:, idx * D:(idx + 1) * D] = (dm * o_in[...].astype(F32) * sg * (1.0 - sg)).astype(dg_ref.dtype)

    dproj1, d_oconv, d_omla, d_omem = pl.pallas_call(
        merge_bwd_kern, name="merge_bwd", grid=(nr,),
        in_specs=[any_spec, rowb(3 * D, g_blk), rowb(D, 0), rowb(D, 0), rowb(D, 0), rowb(D, 0)],
        out_specs=[rowb(3 * D, g_blk), rowb(D, 0), rowb(D, 0), rowb(D, 0)],
        out_shape=[jax.ShapeDtypeStruct((N, c.P), _BF)] + [jax.ShapeDtypeStruct((N, D), _BF)] * 3,
        input_output_aliases={0: 0}, compiler_params=_cparams(("parallel",)),
    )(dproj0, proj, d_merged, o_conv, o_mla, o_mem)

    G['w_conv_out'] = _mm(a_conv, d_oconv, ta=True, name="mm_dwco", out_dtype=_BF)
    d_aconv = _mm(d_oconv, wco, tb=True, name="mm_daconv", out_dtype=_BF)
    G['w_mla_out'] = _mm(a_mla, d_omla, ta=True, name="mm_dwmo", out_dtype=_BF)
    d_amla = _mm(d_omla, wmo, tb=True, name="mm_damla", out_dtype=_BF)
    G['w_mem_out'] = _mm(a_mem, d_omem, ta=True, name="mm_dwmemo", out_dtype=_BF)
    d_amem = _mm(d_omem, wmemo, tb=True, name="mm_damem", out_dtype=_BF)

    def conv_bwd_kern(dp_any, p_ref, prev_ref, next_ref, da_ref, dan_ref, w_ref, o_ref, dw_ref):
        i = pl.program_id(0)
        cg, bg, u, z = conv_parts(p_ref[...])
        pc, _, pu, _ = conv_parts(prev_ref[...])
        _, nbg, _, nz = conv_parts(next_ref[...])
        cu = cg * u
        sh1, sh2 = shifted(cu, pc * pu, i)
        w = w_ref[...]
        conv = w[0:1, :] * sh2 + w[1:2, :] * sh1 + w[2:3, :] * cu
        sg = _sig(z)
        sz = z * sg
        da = da_ref[...].astype(F32)
        dcy = da * sz
        d_z = da * (bg * conv) * (sg * (1.0 + z * (1.0 - sg)))
        d_b = dcy * conv
        dconv = dcy * bg
        dnext = dan_ref[...].astype(F32) * (nz * _sig(nz)) * nbg
        dnext = jnp.where(i < nr - 1, dnext, 0.0)
        rid = lax.broadcasted_iota(jnp.int32, cu.shape, 0)
        up1 = jnp.where(rid == R - 1, dnext[0:1, :], pltpu.roll(dconv, R - 1, 0))
        up2 = jnp.where(rid == R - 2, dnext[0:1, :], jnp.where(rid == R - 1, dnext[1:2, :], pltpu.roll(dconv, R - 2, 0)))
        dcu = w[2:3, :] * dconv + w[1:2, :] * up1 + w[0:1, :] * up2
        o_ref[:, 0:CW] = (dcu * u).astype(o_ref.dtype)
        o_ref[:, CW:2 * CW] = d_b.astype(o_ref.dtype)
        o_ref[:, 2 * CW:3 * CW] = (dcu * cg).astype(o_ref.dtype)
        o_ref[:, 3 * CW:4 * CW] = d_z.astype(o_ref.dtype)

        @pl.when(i == 0)
        def _():
            dw_ref[...] = jnp.zeros(dw_ref.shape, F32)

        dw_ref[0:1, :] += jnp.sum(dconv * sh2, axis=0, keepdims=True)
        dw_ref[1:2, :] += jnp.sum(dconv * sh1, axis=0, keepdims=True)
        dw_ref[2:3, :] += jnp.sum(dconv * cu, axis=0, keepdims=True)

    next_spec = pl.BlockSpec((HALO, 4 * CW), lambda i: (jnp.minimum((i + 1) * rh, N // HALO - 1), conv_blk))
    dan_spec = pl.BlockSpec((HALO, CW), lambda i: (jnp.minimum((i + 1) * rh, N // HALO - 1), 0))
    dproj2, g_convw = pl.pallas_call(
        conv_bwd_kern, name="conv_bwd", grid=(nr,),
        in_specs=[any_spec, rowb(4 * CW, conv_blk), prev_spec, next_spec, rowb(CW, 0), dan_spec, fullb((8, CW))],
        out_specs=[rowb(4 * CW, conv_blk), fullb((8, CW))],
        out_shape=[jax.ShapeDtypeStruct((N, c.P), _BF), jax.ShapeDtypeStruct((8, CW), F32)],
        input_output_aliases={0: 0}, compiler_params=_cparams(("arbitrary",)),
    )(dproj1, proj, proj, proj, d_aconv, d_aconv, convw8)

    def mem_bwd_kern(dp_any, qz_ref, da_ref, k_ref, v_ref, g_ref, o_ref, dk_ref, dv_ref, dg_ref):
        @pl.when(pl.program_id(0) == 0)
        def _():
            dk_ref[...] = jnp.zeros(dk_ref.shape, F32)
            dv_ref[...] = jnp.zeros(dv_ref.shape, F32)
            dg_ref[...] = jnp.zeros(dg_ref.shape, F32)

        for hh in range(MH):
            sl, qh, r, qn, p, y, z = mem_head(qz_ref, k_ref, v_ref, g_ref, hh)
            da = da_ref[:, sl].astype(F32)
            sg = _sig(z)
            dyh = da * (z * sg)
            o_ref[:, MW + hh * MHD:MW + (hh + 1) * MHD] = (da * y * (sg * (1.0 + z * (1.0 - sg)))).astype(o_ref.dtype)
            dyb_h = dyh.astype(_BF)
            dpm = _dot_nt(dyb_h, v_ref[:, sl])
            ds = (p * (dpm - jnp.sum(dpm * p, axis=1, keepdims=True)) * c.mscale).astype(_BF)
            dqn = _dot(ds, k_ref[:, sl])
            dk_ref[:, sl] += _dot_tn(ds, qn)
            dv_ref[:, sl] += _dot_tn(p.astype(_BF), dyb_h)
            dq, dgp = _rms_bwd(qh, r, g_ref[...], dqn, MHD)
            o_ref[:, sl] = dq.astype(o_ref.dtype)
            dg_ref[...] += jnp.sum(dgp, axis=0, keepdims=True)

    dproj3, d_memk, d_memv, g_mem_qn = pl.pallas_call(
        mem_bwd_kern, name="mem_bwd", grid=(nr,),
        in_specs=[any_spec, rowb(2 * MW, mem_blk), rowb(MW, 0), fullb((M, MW)), fullb((M, MW)), fullb((1, MHD))],
        out_specs=[rowb(2 * MW, mem_blk), fullb((M, MW)), fullb((M, MW)), fullb((1, MHD))],
        out_shape=[jax.ShapeDtypeStruct((N, c.P), _BF), jax.ShapeDtypeStruct((M, MW), F32),
                   jax.ShapeDtypeStruct((M, MW), F32), jax.ShapeDtypeStruct((1, MHD), F32)],
        input_output_aliases={0: 0}, compiler_params=_cparams(("arbitrary",)),
    )(dproj2, proj, d_amem, mem_k, mem_v, mem_qn_g)

    def memk_bwd_kern(kv_ref, dk_ref, dv_ref, g_ref, o_ref, dg_ref):
        dg = jnp.zeros((1, MHD), F32)
        for hh in range(MH):
            sl = slice(hh * MHD, (hh + 1) * MHD)
            kh, r = _rms(kv_ref[:, sl], MHD)
            dkr, dgp = _rms_bwd(kh, r, g_ref[...], dk_ref[:, sl], MHD)
            o_ref[:, sl] = dkr.astype(o_ref.dtype)
            dg += jnp.sum(dgp, axis=0, keepdims=True)
        o_ref[:, MW:] = dv_ref[...].astype(o_ref.dtype)
        dg_ref[...] = dg

    d_kvm, g_mem_kn = pl.pallas_call(
        memk_bwd_kern, name="memk_bwd", grid=(1,),
        in_specs=[fullb((M, 2 * MW)), fullb((M, MW)), fullb((M, MW)), fullb((1, MHD))],
        out_specs=[fullb((M, 2 * MW)), fullb((1, MHD))],
        out_shape=[jax.ShapeDtypeStruct((M, 2 * MW), _BF), jax.ShapeDtypeStruct((1, MHD), F32)],
        compiler_params=_cparams(("arbitrary",)),
    )(kvm, d_memk, d_memv, mem_kn_g)
    G['w_mem_kv'] = _mm(memn, d_kvm, ta=True, name="mm_dwmkv", out_dtype=_BF)
    d_memn = _mm(d_kvm, wmkv, tb=True, name="mm_dmemn", out_dtype=F32)

    def memnorm_bwd_kern(x_ref, d_ref, dg_ref):
        xh, _ = _rms(x_ref[...], D)
        dg_ref[...] = jnp.sum(d_ref[...] * xh, axis=0, keepdims=True)

    g_mem_norm = pl.pallas_call(
        memnorm_bwd_kern, name="memnorm_bwd", grid=(1,),
        in_specs=[fullb((M, D)), fullb((M, D))], out_specs=fullb((1, D)),
        out_shape=jax.ShapeDtypeStruct((1, D), F32), compiler_params=_cparams(("arbitrary",)),
    )(memx, d_memn)

    def gate_bwd_kern(dp_any, da_ref, y_ref, z_ref, dy_ref, dz_ref):
        z = z_ref[...].astype(F32)
        da = da_ref[...].astype(F32)
        sg = _sig(z)
        dy_ref[...] = (da * (z * sg)).astype(dy_ref.dtype)
        dz_ref[...] = (da * y_ref[...].astype(F32) * (sg * (1.0 + z * (1.0 - sg)))).astype(dz_ref.dtype)

    d_mlay, dproj4 = pl.pallas_call(
        gate_bwd_kern, name="gate_mla_bwd", grid=(nr,),
        in_specs=[any_spec, rowb(HV, 0), rowb(HV, 0), rowb(HV, mz_blk)],
        out_specs=[rowb(HV, 0), rowb(HV, mz_blk)],
        out_shape=[jax.ShapeDtypeStruct((N, HV), _BF), jax.ShapeDtypeStruct((N, c.P), _BF)],
        input_output_aliases={0: 1}, compiler_params=_cparams(("parallel",)),
    )(dproj3, d_amla, mla_y, proj)

    def attn_bwd_kern(q_ref, k_ref, v_ref, o_ref, do_ref, lse_ref, dq_ref, dk_ref, dv_ref, dq_sc, dl_sc, dk_sc, dv_sc):
        i = pl.program_id(1)
        q, do = q_ref[0], do_ref[...]
        delta = jnp.sum(do.astype(F32) * o_ref[...].astype(F32), axis=1, keepdims=True)
        dl_sc[...] = jnp.broadcast_to(delta, dl_sc.shape)
        dq_sc[...] = jnp.zeros(dq_sc.shape, F32)

        def step(t, masked):
            rows = pl.ds(pl.multiple_of(t * B, B), B)
            k = k_ref[0, rows, :]
            s = _dot_nt(q, k)
            if masked:
                s = diag_mask(s)
            dpm = _dot_nt(do, v_ref[rows, :])
            lse_t, dl = lse_ref[0], dl_sc[...]
            ps, dss = [], []
            for cb in range(B // LANES):
                cols = slice(cb * LANES, (cb + 1) * LANES)
                p_cb = jnp.exp2(s[:, cols] - lse_t)
                ps.append(p_cb.astype(_BF))
                dss.append((p_cb * (dpm[:, cols] - dl)).astype(_BF))
            p, ds = jnp.concatenate(ps, axis=1), jnp.concatenate(dss, axis=1)
            dvp = _dot_tn(p, do)
            dkp = _dot_tn(ds, q)
            if masked:
                dk_sc[rows, :] = dkp
                dv_sc[rows, :] = dvp
            else:
                dk_sc[rows, :] += dkp
                dv_sc[rows, :] += dvp
            dq_sc[...] += _dot(ds, k)

        def unrolled(t, carry):
            for u in range(ATT_UNROLL):
                step(t * ATT_UNROLL + u, False)
            return carry

        lax.fori_loop(0, i // ATT_UNROLL, unrolled, 0)
        for rem in range(ATT_UNROLL):
            @pl.when(i % ATT_UNROLL == rem)
            def _(rem=rem):
                for u in range(rem):
                    step(i - rem + u, False)
                step(i, True)

        dq_ref[0] = dq_sc[...].astype(dq_ref.dtype)

        @pl.when(i == nb - 1)
        def _():
            dk_ref[0] = dk_sc[...].astype(dk_ref.dtype)
            dv_ref[0] = dv_sc[...].astype(dv_ref.dtype)

    d_qcat, d_kcat, d_v = pl.pallas_call(
        attn_bwd_kern, name="attn_bwd", grid=(H, nb),
        in_specs=[q_blk, k_head, v_head, o_blk, o_blk, lse_blk],
        out_specs=[pl.BlockSpec((1, B, HW), lambda hh, i: (hh, i, 0)),
                   pl.BlockSpec((1, N, HW), lambda hh, i: (hh, 0, 0)),
                   pl.BlockSpec((1, N, LANES), lambda hh, i: (hh, 0, 0))],
        out_shape=[jax.ShapeDtypeStruct((H, N, HW), _BF), jax.ShapeDtypeStruct((H, N, HW), _BF),
                   jax.ShapeDtypeStruct((H, N, LANES), _BF)],
        scratch_shapes=[pltpu.VMEM((B, HW), F32), pltpu.VMEM((B, LANES), F32), pltpu.VMEM((N, HW), F32),
                        pltpu.VMEM((N, LANES), F32)],
        compiler_params=_cparams(("parallel", "arbitrary")),
    )(q_cat, k_cat, kv, mla_y, d_mlay, lse)

    def q_prep_bwd_kern(q_ref, dq_ref, t_ref, gn_ref, gr_ref, o_ref, dgn_ref, dgr_ref):
        @pl.when(jnp.logical_and(pl.program_id(0) == 0, pl.program_id(1) == 0))
        def _():
            dgn_ref[...] = jnp.zeros(dgn_ref.shape, F32)
            dgr_ref[...] = jnp.zeros(dgr_ref.shape, F32)

        for g in range(HG):
            blk = q_ref[:, g * HW:(g + 1) * HW].astype(F32)
            d = dq_ref[g].astype(F32) * c.scale
            nh, rn = _rms(blk[:, :LANES], c.NOPE)
            rhat, rr = _rms(blk[:, LANES:], c.ROPE)
            dn, dgn = _rms_bwd(nh, rn, gn_ref[...], d[:, :LANES], c.NOPE)
            drot = _rope_t(d[:, LANES:], t_ref[0], t_ref[1], t_ref[2])
            dr, dgr = _rms_bwd(rhat, rr, gr_ref[...], drot, c.ROPE)
            o_ref[:, g * HW:(g + 1) * HW] = jnp.concatenate([dn, dr], axis=1).astype(o_ref.dtype)
            dgn_ref[...] += jnp.sum(dgn, axis=0, keepdims=True)
            dgr_ref[...] += jnp.sum(dgr, axis=0, keepdims=True)

    d_qp, g_qn_nope, g_qn_rope = pl.pallas_call(
        q_prep_bwd_kern, name="q_prep_bwd", grid=(nrp, nhg),
        in_specs=[heads_in, heads_out, tabs_of(RP), fullb((1, LANES)), fullb((1, LANES))],
        out_specs=[heads_in, fullb((1, LANES)), fullb((1, LANES))],
        out_shape=[jax.ShapeDtypeStruct((N, H * HW), _BF), jax.ShapeDtypeStruct((1, LANES), F32),
                   jax.ShapeDtypeStruct((1, LANES), F32)],
        compiler_params=_cparams(("arbitrary", "arbitrary")),
    )(q_p, d_qcat, tabs, g_qn, g_qr)

    def k_prep_bwd_kern(kv_ref, dk_ref, dv_ref, gn_ref, o_ref, dkr_ref, dgn_ref):
        hg = pl.program_id(1)

        @pl.when(jnp.logical_and(pl.program_id(0) == 0, hg == 0))
        def _():
            dgn_ref[...] = jnp.zeros(dgn_ref.shape, F32)

        @pl.when(hg == 0)
        def _():
            dkr_ref[...] = jnp.zeros(dkr_ref.shape, F32)

        dkr = jnp.zeros((RP, LANES), F32)
        for g in range(HG):
            dk = dk_ref[g].astype(F32) * (1.0 / LOG2E)
            kn, r = _rms(kv_ref[:, g * HW:g * HW + LANES].astype(F32), c.NOPE)
            dkn, dgn = _rms_bwd(kn, r, gn_ref[...], dk[:, :LANES], c.NOPE)
            o_ref[:, g * HW:(g + 1) * HW] = jnp.concatenate([dkn.astype(o_ref.dtype), dv_ref[g]], axis=1)
            dgn_ref[...] += jnp.sum(dgn, axis=0, keepdims=True)
            dkr += dk[:, LANES:]
        dkr_ref[...] += dkr

    d_kv, d_krsum, g_kn_nope = pl.pallas_call(
        k_prep_bwd_kern, name="k_prep_bwd", grid=(nrp, nhg),
        in_specs=[heads_in, heads_out, pl.BlockSpec((HG, RP, LANES), lambda i, hg: (hg, i, 0)), fullb((1, LANES))],
        out_specs=[heads_in, pl.BlockSpec((RP, LANES), lambda i, hg: (i, 0)), fullb((1, LANES))],
        out_shape=[jax.ShapeDtypeStruct((N, H * HW), _BF), jax.ShapeDtypeStruct((N, LANES), F32),
                   jax.ShapeDtypeStruct((1, LANES), F32)],
        compiler_params=_cparams(("arbitrary", "arbitrary")),
    )(kv, d_kcat, d_v, g_kn)

    def krope_bwd_kern(p_ref, d_ref, t_ref, g_ref, o_ref, dg_ref):
        @pl.when(pl.program_id(0) == 0)
        def _():
            dg_ref[...] = jnp.zeros(dg_ref.shape, F32)

        xh, r = _rms(p_ref[...].astype(F32), c.ROPE)
        drot = _rope_t(d_ref[...], t_ref[0], t_ref[1], t_ref[2])
        dx, dg = _rms_bwd(xh, r, g_ref[...], drot, c.ROPE)
        o_ref[...] = dx.astype(o_ref.dtype)
        dg_ref[...] += jnp.sum(dg, axis=0, keepdims=True)

    d_kr, g_kn_rope = pl.pallas_call(
        krope_bwd_kern, name="krope_bwd", grid=(nr,),
        in_specs=[rowb(LANES, 0), rowb(LANES, 0), tabs_of(R), fullb((1, LANES))],
        out_specs=[rowb(LANES, 0), fullb((1, LANES))],
        out_shape=[jax.ShapeDtypeStruct((N, LANES), _BF), jax.ShapeDtypeStruct((1, LANES), F32)],
        compiler_params=_cparams(("arbitrary",)),
    )(kr_raw, d_krsum, tabs, g_kr)
    dproj5 = dproj4

    g_wuq_p = _mm(cqn, d_qp, ta=True, name="mm_dwuq", out_dtype=_BF)
    G['w_uq'] = g_wuq_p.reshape(QL, H, HW)[:, :, :c.NOPE + c.ROPE].reshape(QL, H * (c.NOPE + c.ROPE))
    d_cqn = _mm(d_qp, wuq_p, tb=True, name="mm_dcqn", out_dtype=F32)
    G['w_ukv'] = _mm(ckvn, d_kv, ta=True, name="mm_dwukv", out_dtype=_BF)
    d_ckvn = _mm(d_kv, wukv, tb=True, name="mm_dckvn", out_dtype=F32)

    def to_blocks(n, g):
        if n in COL_SHARDED:
            return jnp.transpose(g.reshape(g.shape[0], NDEV, -1), (1, 0, 2))
        return g.reshape(NDEV, -1, g.shape[1])

    def landing(b):
        own = lax.dynamic_index_in_dim(b, me, 0, keepdims=True)
        return lax.dynamic_update_index_in_dim(lax.empty(b.shape, b.dtype), own, me, 0)

    early = [n for n in BIG if n != 'w_in']
    blocks_e = [to_blocks(n, G[n]) for n in early]
    xe = _split_start(blocks_e, [landing(b) for b in blocks_e], False, "xchg_early_start")
    gq_after = mla_q_norm_g + xe[4][0:1, 0:1]

    def lora_bwd_kern(dp_any, p_ref, dq_ref, dkv_ref, gq_ref, gkv_ref, o_ref, dgq_ref, dgkv_ref):
        @pl.when(pl.program_id(0) == 0)
        def _():
            dgq_ref[...] = jnp.zeros(dgq_ref.shape, F32)
            dgkv_ref[...] = jnp.zeros(dgkv_ref.shape, F32)

        blk = p_ref[...].astype(F32)
        qh, rq = _rms(blk[:, :QL], QL)
        kh, rk = _rms(blk[:, QL:], KVL)
        dq, dgq = _rms_bwd(qh, rq, gq_ref[...], dq_ref[...], QL)
        dk, dgk = _rms_bwd(kh, rk, gkv_ref[...], dkv_ref[...], KVL)
        o_ref[:, :QL] = dq.astype(o_ref.dtype)
        o_ref[:, QL:] = dk.astype(o_ref.dtype)
        dgq_ref[...] += jnp.sum(dgq, axis=0, keepdims=True)
        dgkv_ref[...] += jnp.sum(dgk, axis=0, keepdims=True)

    dproj6, g_q_norm, g_kv_norm = pl.pallas_call(
        lora_bwd_kern, name="lora_bwd", grid=(nr,),
        in_specs=[any_spec, rowb(QL + KVL, lora_blk), rowb(QL, 0), rowb(KVL, 0), fullb((1, QL)), fullb((1, KVL))],
        out_specs=[rowb(QL + KVL, lora_blk), fullb((1, QL)), fullb((1, KVL))],
        out_shape=[jax.ShapeDtypeStruct((N, c.P), _BF), jax.ShapeDtypeStruct((1, QL), F32),
                   jax.ShapeDtypeStruct((1, KVL), F32)],
        input_output_aliases={0: 0}, compiler_params=_cparams(("arbitrary",)),
    )(dproj5, proj, d_cqn, d_ckvn, gq_after, mla_kv_norm_g)

    g_win_p = _mm(h, dproj6, ta=True, name="mm_dwin", out_dtype=_BF, bk=4096)
    g_wkr = _mm(h, d_kr, ta=True, name="mm_dwkr", out_dtype=_BF, bk=4096)
    blocks_w = [_win_blocks(g_win_p, g_wkr, c)]
    xw = _split_start(blocks_w, [landing(b) for b in blocks_w], False, "xchg_win_start")
    d_h = _mm(dproj6, win_p, tb=True, name="mm_dh", out_dtype=F32, bk=3072, after=xw[4], plus=(d_kr, w_krT))

    def final_bwd_kern(x_ref, g_ref, dh_ref, dy_ref, gx_ref, dg_ref):
        @pl.when(pl.program_id(0) == 0)
        def _():
            dg_ref[...] = jnp.zeros(dg_ref.shape, F32)

        xh, r = _rms(x_ref[...], D)
        dx, dg = _rms_bwd(xh, r, g_ref[...], dh_ref[...], D)
        gx_ref[...] = dy_ref[...] + dx
        dg_ref[...] += jnp.sum(dg, axis=0, keepdims=True)

    grad_x, g_norm = pl.pallas_call(
        final_bwd_kern, name="final_bwd", grid=(nr,),
        in_specs=[rowb(D, 0), fullb((1, D)), rowb(D, 0), rowb(D, 0)],
        out_specs=[rowb(D, 0), fullb((1, D))],
        out_shape=[jax.ShapeDtypeStruct((N, D), F32), jax.ShapeDtypeStruct((1, D), F32)],
        compiler_params=_cparams(("arbitrary",)),
    )(xs, norm_g, d_h, dy)

    recv_e = _split_wait(xe, False, grad_x, "xchg_early_wait")
    recv_w = _split_wait(xw, False, grad_x, "xchg_win_wait")
    res = [{}, {}, {}, {}]
    for n, parts in zip(['w_in'] + early, list(recv_w) + list(recv_e)):
        outs = _adam(parts, W[n][0], Mo[n][0], Vo[n][0], "adam_" + n)
        for k in range(4):
            res[k][n] = outs[k][None]

    small_g = {'norm_g': g_norm, 'mla_q_norm_g': g_q_norm, 'mla_kv_norm_g': g_kv_norm,
               'mla_qn_nope_g': g_qn_nope, 'mla_qn_rope_g': g_qn_rope[:, :c.ROPE], 'mla_kn_nope_g': g_kn_nope,
               'mla_kn_rope_g': g_kn_rope[:, :c.ROPE], 'mem_norm_g': g_mem_norm, 'mem_qn_g': g_mem_qn,
               'mem_kn_g': g_mem_kn}
    small_part = _pack([small_g[n] for n in SMALL] + [g_convw[0:3, :]], 0)
    small_all = _all_gather([small_part], "ag_small_grads")[0]
    small_shapes = [W[n].shape for n in SMALL]
    pieces = _unpack(small_all, small_shapes + [(3, CW)])
    cw8 = CW // NDEV
    conv_mine = lax.dynamic_slice_in_dim(pieces[-1].reshape(NDEV, 3, NDEV, cw8), me, 1, axis=2)[:, :, 0, :]
    sm_parts = _pack(pieces[:-1] + [conv_mine], 1)
    sm_names = SMALL + ['conv_w']
    sm_shapes = small_shapes + [(3, cw8)]
    w_sm = _pack([W[n] for n in SMALL] + [conv_w[0]], 0)
    m_sm = _pack([Mo[n] for n in SMALL] + [m_conv_w[0]], 0)
    v_sm = _pack([Vo[n] for n in SMALL] + [v_conv_w[0]], 0)
    outs_sm = [_unpack(o, sm_shapes) for o in _adam(sm_parts, w_sm, m_sm, v_sm, "adam_small")]
    for k in range(4):
        for n, a in zip(sm_names, outs_sm[k]):
            res[k][n] = a[None] if n == 'conv_w' else a
    return (loss, grad_x[None], *[res[0][n] for n in WEIGHTS], *[res[1][n] for n in WEIGHTS],
            *[res[2][n] for n in WEIGHTS], *[res[3][n] for n in WEIGHTS])
```

```python
import math

import jax
import jax.numpy as jnp
from jax import lax
from jax.experimental import pallas as pl
from jax.experimental.pallas import tpu as pltpu

F32 = jnp.float32
_BF = jnp.bfloat16
EPS = 1e-6
CHUNK = 64
ROPE_THETA = 10000.0
ADAM_LR, ADAM_B1, ADAM_B2, ADAM_EPS, ADAM_WD, ADAM_STEP = 0.001, 0.9, 0.999, 1e-08, 0.01, 10
NDEV = 8
AXES = ("x", "y", "c")
MESH = pl.DeviceIdType.MESH
LANES = 128
NEG = -1e30
LOG2E = math.log2(math.e)
V7X_VMEM_LIMIT = 56 * 1024 * 1024
PACK_C = 1024
ATT_BLOCK = 512
ATT_UNROLL = 4
ATT_QBLOCKS = 2
ADAM_BLOCK_ELEMS = 256 * 1024

WEIGHTS = ['norm_g', 'w_in', 'conv_w', 'w_conv_out', 'mla_q_norm_g', 'w_uq', 'mla_kv_norm_g', 'w_ukv',
           'mla_qn_nope_g', 'mla_qn_rope_g', 'mla_kn_nope_g', 'mla_kn_rope_g', 'w_mla_out', 'mem_norm_g',
           'w_mem_kv', 'mem_qn_g', 'mem_kn_g', 'w_mem_out', 'w_o']
BIG = ['w_in', 'w_conv_out', 'w_uq', 'w_ukv', 'w_mla_out', 'w_mem_kv', 'w_mem_out', 'w_o']
COL_SHARDED = ('w_in', 'w_conv_out', 'w_uq', 'w_ukv', 'w_mem_out')
SMALL = ['norm_g', 'mla_q_norm_g', 'mla_kv_norm_g', 'mla_qn_nope_g', 'mla_qn_rope_g', 'mla_kn_nope_g',
         'mla_kn_rope_g', 'mem_norm_g', 'mem_qn_g', 'mem_kn_g']


def _tile(dim, target, align):
    if dim <= target:
        return dim
    t = target - target % align
    while t > 0:
        if dim % t == 0:
            return t
        t -= align
    raise ValueError(f"no tile for {dim} {target} {align}")


def _cparams(sem):
    return pltpu.CompilerParams(dimension_semantics=sem, vmem_limit_bytes=V7X_VMEM_LIMIT)


def _sig(x):
    return 1.0 / (1.0 + jnp.exp(-x))


def _rms(x, n):
    r = lax.rsqrt(jnp.sum(x * x, axis=-1, keepdims=True) * (1.0 / n) + EPS)
    return x * r, r


def _rms_bwd(xhat, r, g, dy, n):
    dxh = dy * g
    dx = r * (dxh - xhat * (jnp.sum(dxh * xhat, axis=-1, keepdims=True) * (1.0 / n)))
    return dx, dy * xhat


def _rope(x, cosp, sina, sinb):
    return x * cosp + pltpu.roll(x, 96, 1) * sina + pltpu.roll(x, 32, 1) * sinb


def _rope_t(d, cosp, sina, sinb):
    return d * cosp + pltpu.roll(d * sina, 32, 1) + pltpu.roll(d * sinb, 96, 1)


def _dot_nt(a, b):
    return lax.dot_general(a, b, (((1,), (1,)), ((), ())), preferred_element_type=F32)


def _dot_tn(a, b):
    return lax.dot_general(a, b, (((0,), (0,)), ((), ())), preferred_element_type=F32)


def _dot(a, b):
    return jnp.dot(a, b, preferred_element_type=F32)


def _all_gather(shards, name):
    na = len(shards)

    def body(*refs):
        x_refs, out_refs = refs[:na], refs[na:2 * na]
        send_sems, recv_sems, local_sems = refs[2 * na:]
        x, y, c = lax.axis_index("x"), lax.axis_index("y"), lax.axis_index("c")
        me, sibling = (x, y, c), (x, y, 1 - c)
        chips = [(1 - x, y), (x, 1 - y), (1 - x, 1 - y)]

        def rows(a, px, py, pc):
            return out_refs[a].at[4 * px + 2 * py + pc]

        def copy(a, k, block, to, src=None):
            return pltpu.make_async_remote_copy(
                src_ref=rows(a, *block) if src is None else src, dst_ref=rows(a, *block),
                send_sem=send_sems.at[7 * a + k], recv_sem=recv_sems.at[7 * a + k],
                device_id=to, device_id_type=MESH)

        mine = [pltpu.make_async_copy(x_refs[a], rows(a, *me), local_sems.at[a]) for a in range(na)]
        for cp in mine:
            cp.start()
        first = [copy(a, 0, me, sibling, src=x_refs[a]) for a in range(na)]
        first += [copy(a, 1 + j, me, (*chip, c), src=x_refs[a]) for j, chip in enumerate(chips) for a in range(na)]
        for cp in first:
            cp.start()
        passed = []
        for j, chip in enumerate(chips):
            for a in range(na):
                copy(a, 1 + j, (*chip, c), me).wait_recv()
                fwd = copy(a, 4 + j, (*chip, c), sibling)
                fwd.start()
                passed.append(fwd)
        for a in range(na):
            copy(a, 0, sibling, me).wait_recv()
        for j, chip in enumerate(chips):
            for a in range(na):
                copy(a, 4 + j, (*chip, 1 - c), me).wait_recv()
        for cp in first + passed:
            cp.wait_send()
        for cp in mine:
            cp.wait()

    any_spec = pl.BlockSpec(memory_space=pl.ANY)
    return pl.pallas_call(
        body, name=name,
        out_shape=[jax.ShapeDtypeStruct((NDEV,) + s.shape, s.dtype) for s in shards],
        in_specs=[any_spec] * na, out_specs=[any_spec] * na,
        scratch_shapes=[pltpu.SemaphoreType.DMA((7 * na,)), pltpu.SemaphoreType.DMA((7 * na,)),
                        pltpu.SemaphoreType.DMA((na,))],
    )(*shards)


_HBM = pl.BlockSpec(memory_space=pltpu.HBM)
_SEM = pl.BlockSpec(memory_space=pltpu.SEMAPHORE)
_EFFECT = pltpu.SideEffectType.DATAFLOW_SIDE_EFFECTING


def _split_copy(a, k, src_refs, land_refs, send_sems, recv_sems, gather, receive_side):
    x, y, c = lax.axis_index("x"), lax.axis_index("y"), lax.axis_index("c")
    me = 4 * x + 2 * y + c
    tx, ty, tc = x ^ ((k + 1) >> 2 & 1), y ^ ((k + 1) >> 1 & 1), c ^ ((k + 1) & 1)
    peer = 4 * tx + 2 * ty + tc
    return pltpu.make_async_remote_copy(
        src_ref=src_refs[a] if gather else src_refs[a].at[peer],
        dst_ref=land_refs[a].at[peer if receive_side else me],
        send_sem=send_sems.at[7 * a + k], recv_sem=recv_sems.at[7 * a + k],
        device_id=(tx, ty, tc), device_id_type=MESH)


def _split_start(srcs, lands, gather, name, after=None):
    na = len(srcs)
    extra = [] if after is None else [after]

    def body(*refs):
        src_refs, land_refs = refs[:na], refs[na:2 * na]
        send_sems, recv_sems = refs[2 * na + len(extra)], refs[2 * na + len(extra) + 1]
        token = refs[-1]
        for k in range(7):
            for a in range(na):
                _split_copy(a, k, src_refs, land_refs, send_sems, recv_sems, gather, False).start()
        token[...] = jnp.zeros_like(token)

    hbm = [pltpu.HBM(b.shape, b.dtype) for b in list(srcs) + list(lands)]
    outs = pl.pallas_call(
        body, name=name,
        out_shape=(pltpu.SemaphoreType.DMA((7 * na,)), pltpu.SemaphoreType.DMA((7 * na,)), *hbm,
                   jax.ShapeDtypeStruct((8, LANES), F32)),
        in_specs=[_HBM] * (2 * na) + [pl.BlockSpec(memory_space=pl.ANY)] * len(extra),
        out_specs=(_SEM, _SEM, *[_HBM] * (2 * na), pl.BlockSpec(memory_space=pltpu.VMEM)),
        input_output_aliases={j: 2 + j for j in range(2 * na)},
        compiler_params=pltpu.CompilerParams(has_side_effects=_EFFECT),
    )(*[pltpu.with_memory_space_constraint(b, pltpu.HBM) for b in srcs],
      *[pltpu.with_memory_space_constraint(l, pltpu.HBM) for l in lands], *extra)
    return outs[0], outs[1], outs[2:2 + na], outs[2 + na:2 + 2 * na], outs[-1]


def _split_wait(started, gather, after, name):
    send_sems, recv_sems, srcs, lands, _ = started
    na = len(srcs)

    def body(*refs):
        src_refs, land_refs = refs[:na], refs[na:2 * na]
        send_s, recv_s = refs[2 * na], refs[2 * na + 1]
        for k in range(7):
            for a in range(na):
                cp = _split_copy(a, k, src_refs, land_refs, send_s, recv_s, gather, True)
                cp.wait_send()
                cp.wait_recv()

    hbm = [pltpu.HBM(b.shape, b.dtype) for b in list(srcs) + list(lands)]
    outs = pl.pallas_call(
        body, name=name, out_shape=tuple(hbm),
        in_specs=[_HBM] * (2 * na) + [_SEM, _SEM, pl.BlockSpec(memory_space=pl.ANY)],
        out_specs=tuple([_HBM] * (2 * na)),
        input_output_aliases={j: j for j in range(2 * na)},
        compiler_params=pltpu.CompilerParams(has_side_effects=_EFFECT),
    )(*srcs, *lands, send_sems, recv_sems, after)
    return outs[na:]


def _seg_rows(size):
    rows = -(-size // PACK_C)
    return -(-rows // 16) * 16


def _pack(arrs, lead):
    parts = []
    for a in arrs:
        lshape = a.shape[:lead]
        f = a.reshape(lshape + (-1,)).astype(F32)
        rows = _seg_rows(f.shape[-1])
        f = jnp.pad(f, [(0, 0)] * lead + [(0, rows * PACK_C - f.shape[-1])])
        parts.append(f.reshape(lshape + (rows, PACK_C)))
    return jnp.concatenate(parts, axis=lead)


def _unpack(buf, shapes):
    lshape = buf.shape[:-2]
    out, r = [], 0
    for shp in shapes:
        size = math.prod(shp)
        rows = _seg_rows(size)
        seg = buf[..., r:r + rows, :].reshape(lshape + (rows * PACK_C,))[..., :size]
        out.append(seg.reshape(lshape + tuple(shp)))
        r += rows
    return out


def _mm(a, b, *, name, out_dtype, ta=False, tb=False, bm=1024, bn=1024, bk=2048, after=None, plus=None):
    if ta:
        kdim, m = a.shape
    else:
        m, kdim = a.shape
    n, k2 = b.shape if tb else b.shape[::-1]
    assert kdim == k2 and not (ta and tb), (a.shape, b.shape)
    bm = _tile(m, bm, LANES if ta else 16)
    bn = _tile(n, bn, LANES)
    bk = _tile(kdim, bk, LANES)
    nk = kdim // bk
    n_after = 0 if after is None else 1
    n_plus = 0 if plus is None else 2

    def kern(a_ref, b_ref, *rest):
        plus_refs = rest[n_after:n_after + n_plus]
        o_ref, scratch = rest[n_after + n_plus], rest[n_after + n_plus + 1:]
        part = (_dot_tn if ta else _dot_nt if tb else _dot)(a_ref[...], b_ref[...])

        def first(p):
            return p + _dot(plus_refs[0][...], plus_refs[1][...]) if plus is not None else p

        if nk == 1:
            o_ref[...] = first(part).astype(o_ref.dtype)
        else:
            acc = scratch[0] if scratch else o_ref
            k = pl.program_id(2)

            @pl.when(k == 0)
            def _():
                acc[...] = first(jnp.zeros(acc.shape, F32))

            acc[...] += part
            if scratch:
                @pl.when(k == nk - 1)
                def _():
                    o_ref[...] = acc[...].astype(o_ref.dtype)

    a_spec = pl.BlockSpec((bk, bm), lambda i, j, k: (k, i)) if ta else pl.BlockSpec((bm, bk), lambda i, j, k: (i, k))
    b_spec = pl.BlockSpec((bn, bk), lambda i, j, k: (j, k)) if tb else pl.BlockSpec((bk, bn), lambda i, j, k: (k, j))
    extra_specs, extra_args = [], []
    if after is not None:
        extra_specs.append(pl.BlockSpec(after.shape, lambda i, j, k: (0, 0)))
        extra_args.append(after)
    if plus is not None:
        kk = plus[0].shape[1]
        extra_specs += [pl.BlockSpec((bm, kk), lambda i, j, k: (i, 0)), pl.BlockSpec((kk, bn), lambda i, j, k: (0, j))]
        extra_args += list(plus)
    return pl.pallas_call(
        kern, name=name, grid=(m // bm, n // bn, nk),
        in_specs=[a_spec, b_spec] + extra_specs,
        out_specs=pl.BlockSpec((bm, bn), lambda i, j, k: (i, j)),
        out_shape=jax.ShapeDtypeStruct((m, n), out_dtype),
        scratch_shapes=[pltpu.VMEM((bm, bn), F32)] if nk > 1 and out_dtype != F32 else [],
        compiler_params=_cparams(("parallel", "parallel", "arbitrary")),
    )(a, b, *extra_args)


def _adam(parts, w_a, m_a, v_a, name):
    rows, cols = w_a.shape
    rb = _tile(rows, max(8, ADAM_BLOCK_ELEMS // cols // 8 * 8), 8)
    bc1 = 1.0 - ADAM_B1 ** ADAM_STEP
    bc2 = 1.0 - ADAM_B2 ** ADAM_STEP

    def adam_kern(p_ref, w_ref, m_ref, v_ref, g_ref, d_ref, nm_ref, nv_ref):
        g = p_ref[0].astype(F32)
        for j in range(1, NDEV):
            g = g + p_ref[j].astype(F32)
        m_new = ADAM_B1 * m_ref[...] + (1.0 - ADAM_B1) * g
        v_new = ADAM_B2 * v_ref[...] + (1.0 - ADAM_B2) * (g * g)
        g_ref[...] = g
        nm_ref[...] = m_new
        nv_ref[...] = v_new
        d_ref[...] = -ADAM_LR * ((m_new / bc1) / (jnp.sqrt(v_new / bc2) + ADAM_EPS) + ADAM_WD * w_ref[...])

    blk = pl.BlockSpec((rb, cols), lambda i: (i, 0))
    return pl.pallas_call(
        adam_kern, name=name, grid=(rows // rb,),
        in_specs=[pl.BlockSpec((NDEV, rb, cols), lambda i: (0, i, 0)), blk, blk, blk],
        out_specs=[blk] * 4, out_shape=[jax.ShapeDtypeStruct((rows, cols), F32)] * 4,
        compiler_params=_cparams(("parallel",)),
    )(parts, w_a, m_a, v_a)


class _Cfg:
    pass


def _config(x, conv_w, w_uq, w_ukv, mla_qn_nope_g, mla_qn_rope_g, mem, mem_qn_g, w_mem_out, w_mla_out):
    c = _Cfg()
    c.N, c.D = x.shape[1], x.shape[2]
    c.CW = conv_w.shape[2] * NDEV
    c.QL, c.KVL = w_uq.shape[1], w_ukv.shape[1]
    c.NOPE, c.ROPE = mla_qn_nope_g.shape[1], mla_qn_rope_g.shape[1]
    c.H = w_uq.shape[2] * NDEV // (c.NOPE + c.ROPE)
    c.V = w_ukv.shape[2] * NDEV // c.H - c.NOPE
    assert c.NOPE == LANES and c.V == LANES and c.ROPE == LANES // 2
    c.HW = 2 * LANES
    c.HV = c.H * c.V
    assert w_mla_out.shape[1] * NDEV == c.HV
    c.M = mem.shape[1]
    c.MHD = mem_qn_g.shape[1]
    c.MW = w_mem_out.shape[1]
    c.MH = c.MW // c.MHD
    c.o_conv = 0
    c.o_mz = 4 * c.CW
    c.o_g = c.o_mz + c.HV
    c.o_mem = c.o_g + 3 * c.D
    c.o_lora = c.o_mem + 2 * c.MW
    c.P = c.o_lora + c.QL + c.KVL
    assert c.o_mz % c.HV == 0 and c.o_g % (3 * c.D) == 0 and c.o_mem % (2 * c.MW) == 0
    assert c.o_lora % (c.QL + c.KVL) == 0 and c.QL % LANES == 0 and c.KVL % LANES == 0
    c.IN = 4 * c.CW + c.QL + c.KVL + c.ROPE + c.HV + 2 * c.MW + 3 * c.D
    c.R = _tile(c.N, 256, 16)
    c.RP = _tile(c.N, 512, 16)
    c.HG = _tile(c.H, 4, 1)
    c.B = _tile(c.N, ATT_BLOCK, CHUNK)
    c.scale = float((c.NOPE + c.ROPE) ** -0.5)
    c.mscale = float(c.MHD ** -0.5)
    return c


def _win_segments(c):
    ref_order = (('conv', 4 * c.CW), ('lora', c.QL + c.KVL), ('kr', c.ROPE), ('mz', c.HV), ('mem', 2 * c.MW), ('g', 3 * c.D))
    mine = {'conv': c.o_conv, 'mz': c.o_mz, 'g': c.o_g, 'mem': c.o_mem, 'lora': c.o_lora, 'kr': 0}
    segs, o = [], 0
    for nm, wd in ref_order:
        segs.append((nm, o, wd, mine[nm]))
        o += wd
    return segs


def _win_split(g_win, c):
    n8 = g_win.shape[2]

    def columns(a, wd):
        return [g_win[j][:, max(a, j * n8) - j * n8:min(a + wd, (j + 1) * n8) - j * n8]
                for j in range(a // n8, (a + wd - 1) // n8 + 1)]

    segs = {nm: (a, wd) for nm, a, wd, _ in _win_segments(c)}
    main = [p for nm in ('conv', 'mz', 'g', 'mem', 'lora') for p in columns(*segs[nm])]
    kr = jnp.concatenate(columns(*segs['kr']) + [jnp.zeros((g_win.shape[1], LANES - c.ROPE), g_win.dtype)], axis=1)
    return jnp.concatenate(main, axis=1), kr


def _win_blocks(g, g_kr, c):
    n8 = c.IN // NDEV
    blocks = []
    for j in range(NDEV):
        lo, hi = j * n8, (j + 1) * n8
        parts = []
        for nm, a, wd, mine in _win_segments(c):
            s, e = max(a, lo), min(a + wd, hi)
            if s < e:
                parts.append((g_kr if nm == 'kr' else g)[:, mine + s - a:mine + e - a])
        blocks.append(jnp.concatenate(parts, axis=1))
    return jnp.stack(blocks, axis=0)


def kernel(x, positions, mem, norm_g, w_in, conv_w, w_conv_out, mla_q_norm_g, w_uq, mla_kv_norm_g, w_ukv, mla_qn_nope_g, mla_qn_rope_g, mla_kn_nope_g, mla_kn_rope_g, w_mla_out, mem_norm_g, w_mem_kv, mem_qn_g, mem_kn_g, w_mem_out, w_o, loss_target, m_norm_g, m_w_in, m_conv_w, m_w_conv_out, m_mla_q_norm_g, m_w_uq, m_mla_kv_norm_g, m_w_ukv, m_mla_qn_nope_g, m_mla_qn_rope_g, m_mla_kn_nope_g, m_mla_kn_rope_g, m_w_mla_out, m_mem_norm_g, m_w_mem_kv, m_mem_qn_g, m_mem_kn_g, m_w_mem_out, m_w_o, v_norm_g, v_w_in, v_conv_w, v_w_conv_out, v_mla_q_norm_g, v_w_uq, v_mla_kv_norm_g, v_w_ukv, v_mla_qn_nope_g, v_mla_qn_rope_g, v_mla_kn_nope_g, v_mla_kn_rope_g, v_w_mla_out, v_mem_norm_g, v_w_mem_kv, v_mem_qn_g, v_mem_kn_g, v_w_mem_out, v_w_o):
    args = dict(locals())
    W = {n: args[n] for n in WEIGHTS}
    Mo = {n: args['m_' + n] for n in WEIGHTS}
    Vo = {n: args['v_' + n] for n in WEIGHTS}
    c = _config(x, conv_w, w_uq, w_ukv, mla_qn_nope_g, mla_qn_rope_g, mem, mem_qn_g, w_mem_out, w_mla_out)
    N, D, R, B, H = c.N, c.D, c.R, c.B, c.H
    assert x.shape[0] == 1
    xs = x[0]
    tgt = loss_target[0]
    memx = mem[0]
    me = 4 * lax.axis_index("x") + 2 * lax.axis_index("y") + lax.axis_index("c")
    nr = N // R

    g_win, g_taps = _all_gather([W['w_in'][0].astype(_BF), conv_w[0]], "ag_w_in")
    rest = [n for n in BIG if n != 'w_in']
    shards_r = [W[n][0].astype(_BF) for n in rest]
    lands_r = [lax.dynamic_update_index_in_dim(lax.empty((NDEV,) + s.shape, s.dtype), s[None], me, 0)
               for s in shards_r]
    ag_rest = _split_start(shards_r, lands_r, True, "ag_rest_start", after=g_win)
    win_p, w_kr = _win_split(g_win, c)
    w_krT = w_kr.T
    convw = jnp.transpose(g_taps, (1, 0, 2)).reshape(3, c.CW)
    convw8 = jnp.pad(convw, ((0, 5), (0, 0)))

    def rowb(width, cidx):
        return pl.BlockSpec((R, width), lambda i, _c=cidx: (i, _c))

    def fullb(shape):
        nd = len(shape)
        return pl.BlockSpec(shape, lambda *_: (0,) * nd)

    def pad_lanes(g, w=LANES):
        return jnp.pad(g, ((0, 0), (0, w - g.shape[1])))

    def tabs_of(rows):
        return pl.BlockSpec((3, rows, LANES), lambda i, *_: (0, i, 0))

    half = c.ROPE // 2
    inv_freq = jnp.power(ROPE_THETA, -jnp.arange(half, dtype=F32) / half)
    invf = jnp.concatenate([inv_freq, inv_freq, jnp.zeros((LANES - c.ROPE,), F32)])[None, :]
    pos_col = positions[0].astype(F32).reshape(N, 1)

    def rope_tab_kern(pos_ref, invf_ref, o_ref):
        ang = pos_ref[...] * invf_ref[...]
        co, si = jnp.cos(ang), jnp.sin(ang)
        lane = lax.broadcasted_iota(jnp.int32, ang.shape, 1)
        o_ref[0] = jnp.where(lane < c.ROPE, co, 0.0)
        o_ref[1] = jnp.where(lane < half, -si, 0.0)
        o_ref[2] = jnp.where(jnp.logical_and(lane >= half, lane < c.ROPE), si, 0.0)

    tabs = pl.pallas_call(
        rope_tab_kern, name="rope_tab", grid=(nr,),
        in_specs=[pl.BlockSpec((R, 1), lambda i: (i, 0)), fullb((1, LANES))],
        out_specs=tabs_of(R),
        out_shape=jax.ShapeDtypeStruct((3, N, LANES), F32),
        compiler_params=_cparams(("parallel",)),
    )(pos_col, invf)

    def make_rms_kern():
        def rms_fwd_kern(x_ref, g_ref, o_ref):
            xh, _ = _rms(x_ref[...].astype(F32), x_ref.shape[-1])
            o_ref[...] = (xh * g_ref[...]).astype(o_ref.dtype)
        return rms_fwd_kern

    h = pl.pallas_call(
        make_rms_kern(), name="rms_x", grid=(nr,),
        in_specs=[rowb(D, 0), fullb((1, D))], out_specs=rowb(D, 0),
        out_shape=jax.ShapeDtypeStruct((N, D), _BF), compiler_params=_cparams(("parallel",)),
    )(xs, norm_g)

    proj = _mm(h, win_p, name="mm_proj", out_dtype=_BF, after=ag_rest[4])
    kr_raw = _mm(h, w_kr, name="mm_kr", out_dtype=_BF)

    Wf = {}
    for n, g in zip(rest, _split_wait(ag_rest, True, proj, "ag_rest_wait")):
        if n in COL_SHARDED:
            Wf[n] = jnp.transpose(g, (1, 0, 2)).reshape(g.shape[1], -1)
        else:
            Wf[n] = g.reshape(-1, g.shape[2])
    wuq = Wf['w_uq'].reshape(c.QL, H, c.NOPE + c.ROPE)
    wuq_p = jnp.pad(wuq, ((0, 0), (0, 0), (0, c.HW - c.NOPE - c.ROPE))).reshape(c.QL, H * c.HW)
    wukv = Wf['w_ukv']
    wco, wmo, wmkv, wmemo, wo = Wf['w_conv_out'], Wf['w_mla_out'], Wf['w_mem_kv'], Wf['w_mem_out'], Wf['w_o']

    CW = c.CW
    conv_blk = c.o_conv // (4 * CW)
    HALO = 16
    rh = R // HALO

    def conv_parts(blk):
        blk = blk.astype(F32)
        return blk[:, 0:CW], blk[:, CW:2 * CW], blk[:, 2 * CW:3 * CW], blk[:, 3 * CW:4 * CW]

    def shifted(cu, prev, i):
        prev = jnp.where(i > 0, prev, 0.0)
        rid = lax.broadcasted_iota(jnp.int32, cu.shape, 0)
        last, last2 = prev[HALO - 1:HALO, :], prev[HALO - 2:HALO - 1, :]
        sh1 = jnp.where(rid == 0, last, pltpu.roll(cu, 1, 0))
        sh2 = jnp.where(rid == 0, last2, jnp.where(rid == 1, last, pltpu.roll(cu, 2, 0)))
        return sh1, sh2

    def conv_fwd_kern(p_ref, prev_ref, w_ref, o_ref):
        i = pl.program_id(0)
        cg, bg, u, z = conv_parts(p_ref[...])
        pc, _, pu, _ = conv_parts(prev_ref[...])
        cu = cg * u
        sh1, sh2 = shifted(cu, pc * pu, i)
        w = w_ref[...]
        conv = w[0:1, :] * sh2 + w[1:2, :] * sh1 + w[2:3, :] * cu
        o_ref[...] = (bg * conv * (z * _sig(z))).astype(o_ref.dtype)

    prev_spec = pl.BlockSpec((HALO, 4 * CW), lambda i: (jnp.maximum(i * rh - 1, 0), conv_blk))
    a_conv = pl.pallas_call(
        conv_fwd_kern, name="conv_fwd", grid=(nr,),
        in_specs=[rowb(4 * CW, conv_blk), prev_spec, fullb((8, CW))],
        out_specs=rowb(CW, 0), out_shape=jax.ShapeDtypeStruct((N, CW), _BF),
        compiler_params=_cparams(("parallel",)),
    )(proj, proj, convw8)
    o_conv = _mm(a_conv, wco, name="mm_oconv", out_dtype=_BF)

    QL, KVL, HW = c.QL, c.KVL, c.HW
    lora_blk = c.o_lora // (QL + KVL)

    def lora_fwd_kern(p_ref, gq_ref, gkv_ref, q_ref, kv_ref):
        blk = p_ref[...].astype(F32)
        qh, _ = _rms(blk[:, :QL], QL)
        kh, _ = _rms(blk[:, QL:], KVL)
        q_ref[...] = (qh * gq_ref[...]).astype(q_ref.dtype)
        kv_ref[...] = (kh * gkv_ref[...]).astype(kv_ref.dtype)

    cqn, ckvn = pl.pallas_call(
        lora_fwd_kern, name="lora_fwd", grid=(nr,),
        in_specs=[rowb(QL + KVL, lora_blk), fullb((1, QL)), fullb((1, KVL))],
        out_specs=[rowb(QL, 0), rowb(KVL, 0)],
        out_shape=[jax.ShapeDtypeStruct((N, QL), _BF), jax.ShapeDtypeStruct((N, KVL), _BF)],
        compiler_params=_cparams(("parallel",)),
    )(proj, mla_q_norm_g, mla_kv_norm_g)
    q_p = _mm(cqn, wuq_p, name="mm_q", out_dtype=_BF)
    kv = _mm(ckvn, wukv, name="mm_kv", out_dtype=_BF)

    g_qn, g_qr = mla_qn_nope_g, pad_lanes(mla_qn_rope_g)
    g_kn, g_kr = mla_kn_nope_g, pad_lanes(mla_kn_rope_g)

    def krope_fwd_kern(p_ref, t_ref, g_ref, o_ref):
        xh, _ = _rms(p_ref[...].astype(F32), c.ROPE)
        o_ref[...] = _rope(xh * g_ref[...], t_ref[0], t_ref[1], t_ref[2]).astype(o_ref.dtype)

    k_rope = pl.pallas_call(
        krope_fwd_kern, name="krope_fwd", grid=(nr,),
        in_specs=[rowb(LANES, 0), tabs_of(R), fullb((1, LANES))],
        out_specs=rowb(LANES, 0), out_shape=jax.ShapeDtypeStruct((N, LANES), _BF),
        compiler_params=_cparams(("parallel",)),
    )(kr_raw, tabs, g_kr)

    RP, HG = c.RP, c.HG
    nrp, nhg = N // RP, H // HG
    heads_in = pl.BlockSpec((RP, HG * HW), lambda i, hg: (i, hg))
    heads_out = pl.BlockSpec((HG, RP, HW), lambda i, hg: (hg, i, 0))

    def q_prep_kern(q_ref, t_ref, gn_ref, gr_ref, o_ref):
        for g in range(HG):
            blk = q_ref[:, g * HW:(g + 1) * HW].astype(F32)
            nh, _ = _rms(blk[:, :LANES], c.NOPE)
            rhat, _ = _rms(blk[:, LANES:], c.ROPE)
            rot = _rope(rhat * gr_ref[...], t_ref[0], t_ref[1], t_ref[2])
            o_ref[g] = (jnp.concatenate([nh * gn_ref[...], rot], axis=1) * (c.scale * LOG2E)).astype(o_ref.dtype)

    q_cat = pl.pallas_call(
        q_prep_kern, name="q_prep", grid=(nrp, nhg),
        in_specs=[heads_in, tabs_of(RP), fullb((1, LANES)), fullb((1, LANES))],
        out_specs=heads_out, out_shape=jax.ShapeDtypeStruct((H, N, HW), _BF),
        compiler_params=_cparams(("parallel", "parallel")),
    )(q_p, tabs, g_qn, g_qr)

    def k_prep_kern(kv_ref, kr_ref, gn_ref, o_ref):
        for g in range(HG):
            kn, _ = _rms(kv_ref[:, g * HW:g * HW + LANES].astype(F32), c.NOPE)
            o_ref[g] = jnp.concatenate([(kn * gn_ref[...]).astype(o_ref.dtype), kr_ref[...]], axis=1)

    k_cat = pl.pallas_call(
        k_prep_kern, name="k_prep", grid=(nrp, nhg),
        in_specs=[heads_in, pl.BlockSpec((RP, LANES), lambda i, hg: (i, 0)), fullb((1, LANES))],
        out_specs=heads_out, out_shape=jax.ShapeDtypeStruct((H, N, HW), _BF),
        compiler_params=_cparams(("parallel", "parallel")),
    )(kv, k_rope, g_kn)

    QB = ATT_QBLOCKS if N % (ATT_QBLOCKS * B) == 0 else 1
    BQ = QB * B
    nq = N // BQ
    assert CHUNK & (CHUNK - 1) == 0 and B % CHUNK == 0 and ATT_UNROLL % QB == 0

    def diag_mask(s, d):
        row = lax.broadcasted_iota(jnp.int32, s.shape, 0)
        col = lax.broadcasted_iota(jnp.int32, s.shape, 1) + d * B
        shift = CHUNK.bit_length() - 1
        allowed = jnp.right_shift(col, shift) <= jnp.right_shift(row, shift)
        return jnp.where(allowed, s, NEG)

    k_head = pl.BlockSpec((1, N, HW), lambda hh, i: (hh, 0, 0))
    v_head = pl.BlockSpec((N, LANES), lambda hh, i: (0, 2 * hh + 1))
    q_blk = pl.BlockSpec((1, BQ, HW), lambda hh, i: (hh, i, 0))
    o_blk = pl.BlockSpec((BQ, LANES), lambda hh, i: (i, hh))
    lse_blk = pl.BlockSpec((1, BQ, LANES), lambda hh, i: (hh, i, 0))

    def key_block_plan(i, run):
        def unrolled(u, carry):
            run(ATT_UNROLL * u, ATT_UNROLL, 0)
            return carry

        n_full = QB * i
        lax.fori_loop(0, n_full // ATT_UNROLL, unrolled, 0)
        for rem in range(0, ATT_UNROLL, QB):
            @pl.when(n_full % ATT_UNROLL == rem)
            def _(rem=rem):
                run(n_full - rem, rem + QB, QB)

    def attn_fwd_kern(q_ref, k_ref, v_ref, o_ref, lse_ref, m_sc, acc_sc, s_sc):
        i = pl.program_id(1)
        m_sc[...] = jnp.full(m_sc.shape, NEG, F32)
        acc_sc[...] = jnp.zeros(acc_sc.shape, F32)

        def rows_of(t):
            return pl.ds(pl.multiple_of(t * B, B), B)

        def scores(t, slot):
            s_sc[slot] = _dot_nt(q_ref[0], k_ref[0, rows_of(t), :])

        def softmax_pv(t, slot, diag):
            s = s_sc[slot]
            if diag is not None:
                s = diag_mask(s, diag)
            mt = s[:, 0:LANES]
            for cb in range(1, B // LANES):
                mt = jnp.maximum(mt, s[:, cb * LANES:(cb + 1) * LANES])
            m_prev = m_sc[...]
            m_new = jnp.maximum(m_prev, jnp.max(mt, axis=1, keepdims=True))
            alpha = jnp.exp2(m_prev - m_new)
            p = jnp.concatenate([jnp.exp2(s[:, cb * LANES:(cb + 1) * LANES] - m_new).astype(_BF)
                                 for cb in range(B // LANES)], axis=1)
            v_ones = jnp.concatenate([v_ref[rows_of(t), :], jnp.ones((B, LANES), _BF)], axis=1)
            acc_sc[...] = jnp.concatenate([alpha, alpha], axis=1) * acc_sc[...] + _dot(p, v_ones)
            m_sc[...] = m_new

        scores(0, 0)

        def run(first, count, n_diag):
            for u in range(count):
                if u + 1 < count or n_diag == 0:
                    scores(first + u + 1, (u + 1) % 2)
                softmax_pv(first + u, u % 2, u - (count - n_diag) if u >= count - n_diag else None)

        key_block_plan(i, run)
        acc = acc_sc[...]
        o_ref[...] = (acc[:, :LANES] / acc[:, LANES:]).astype(o_ref.dtype)
        lse_ref[0] = m_sc[...] + jnp.log2(acc[:, LANES:])

    mla_y, lse = pl.pallas_call(
        attn_fwd_kern, name="attn_fwd", grid=(H, nq),
        in_specs=[q_blk, k_head, v_head], out_specs=[o_blk, lse_blk],
        out_shape=[jax.ShapeDtypeStruct((N, c.HV), _BF), jax.ShapeDtypeStruct((H, N, LANES), F32)],
        scratch_shapes=[pltpu.VMEM((BQ, LANES), F32), pltpu.VMEM((BQ, HW), F32), pltpu.VMEM((2, BQ, B), F32)],
        compiler_params=_cparams(("parallel", "arbitrary")),
    )(q_cat, k_cat, kv)

    HV = c.HV
    mz_blk = c.o_mz // HV

    def gate_fwd_kern(y_ref, z_ref, o_ref):
        z = z_ref[...].astype(F32)
        o_ref[...] = (y_ref[...].astype(F32) * (z * _sig(z))).astype(o_ref.dtype)

    a_mla = pl.pallas_call(
        gate_fwd_kern, name="gate_mla", grid=(nr,),
        in_specs=[rowb(HV, 0), rowb(HV, mz_blk)], out_specs=rowb(HV, 0),
        out_shape=jax.ShapeDtypeStruct((N, HV), _BF), compiler_params=_cparams(("parallel",)),
    )(mla_y, proj)
    o_mla = _mm(a_mla, wmo, name="mm_omla", out_dtype=_BF)

    M, MW, MH, MHD = c.M, c.MW, c.MH, c.MHD
    memn = pl.pallas_call(
        make_rms_kern(), name="rms_mem",
        grid=(1,), in_specs=[fullb((M, D)), fullb((1, D))], out_specs=fullb((M, D)),
        out_shape=jax.ShapeDtypeStruct((M, D), _BF), compiler_params=_cparams(("arbitrary",)),
    )(memx, mem_norm_g)
    kvm = _mm(memn, wmkv, name="mm_memkv", out_dtype=F32)

    def memk_fwd_kern(kv_ref, g_ref, k_ref, v_ref):
        for hh in range(MH):
            kh, _ = _rms(kv_ref[:, hh * MHD:(hh + 1) * MHD], MHD)
            k_ref[:, hh * MHD:(hh + 1) * MHD] = (kh * g_ref[...]).astype(k_ref.dtype)
        v_ref[...] = kv_ref[:, MW:].astype(v_ref.dtype)

    mem_k, mem_v = pl.pallas_call(
        memk_fwd_kern, name="memk_fwd", grid=(1,),
        in_specs=[fullb((M, 2 * MW)), fullb((1, MHD))], out_specs=[fullb((M, MW)), fullb((M, MW))],
        out_shape=[jax.ShapeDtypeStruct((M, MW), _BF)] * 2, compiler_params=_cparams(("arbitrary",)),
    )(kvm, mem_kn_g)

    mem_blk = c.o_mem // (2 * MW)

    def mem_head(qz_ref, k_ref, v_ref, g_ref, hh):
        sl = slice(hh * MHD, (hh + 1) * MHD)
        qh, r = _rms(qz_ref[:, sl].astype(F32), MHD)
        qn = (qh * g_ref[...]).astype(_BF)
        s = _dot_nt(qn, k_ref[:, sl]) * c.mscale
        e = jnp.exp(s - jnp.max(s, axis=1, keepdims=True))
        p = e / jnp.sum(e, axis=1, keepdims=True)
        y = _dot(p.astype(_BF), v_ref[:, sl])
        z = qz_ref[:, MW + hh * MHD:MW + (hh + 1) * MHD].astype(F32)
        return sl, qh, r, qn, p, y, z

    def mem_fwd_kern(qz_ref, k_ref, v_ref, g_ref, o_ref):
        for hh in range(MH):
            sl, _, _, _, _, y, z = mem_head(qz_ref, k_ref, v_ref, g_ref, hh)
            o_ref[:, sl] = (y * (z * _sig(z))).astype(o_ref.dtype)

    a_mem = pl.pallas_call(
        mem_fwd_kern, name="mem_fwd", grid=(nr,),
        in_specs=[rowb(2 * MW, mem_blk), fullb((M, MW)), fullb((M, MW)), fullb((1, MHD))],
        out_specs=rowb(MW, 0), out_shape=jax.ShapeDtypeStruct((N, MW), _BF),
        compiler_params=_cparams(("parallel",)),
    )(proj, mem_k, mem_v, mem_qn_g)
    o_mem = _mm(a_mem, wmemo, name="mm_omem", out_dtype=_BF)

    g_blk = c.o_g // (3 * D)

    def merge_fwd_kern(g_ref, oc_ref, om_ref, ome_ref, o_ref):
        g = g_ref[...].astype(F32)
        acc = _sig(g[:, :D]) * oc_ref[...].astype(F32)
        acc += _sig(g[:, D:2 * D]) * om_ref[...].astype(F32)
        acc += _sig(g[:, 2 * D:]) * ome_ref[...].astype(F32)
        o_ref[...] = acc.astype(o_ref.dtype)

    merged = pl.pallas_call(
        merge_fwd_kern, name="merge_fwd", grid=(nr,),
        in_specs=[rowb(3 * D, g_blk), rowb(D, 0), rowb(D, 0), rowb(D, 0)], out_specs=rowb(D, 0),
        out_shape=jax.ShapeDtypeStruct((N, D), _BF), compiler_params=_cparams(("parallel",)),
    )(proj, o_conv, o_mla, o_mem)
    y2 = _mm(merged, wo, name="mm_out", out_dtype=F32)

    def loss_kern(x_ref, y_ref, t_ref, dy_ref, dyb_ref, l_ref):
        e = x_ref[...] + y_ref[...] - t_ref[...]
        dy = e * (1.0 / D)
        dy_ref[...] = dy
        dyb_ref[...] = dy.astype(dyb_ref.dtype)

        @pl.when(pl.program_id(0) == 0)
        def _():
            l_ref[...] = jnp.zeros(l_ref.shape, F32)

        l_ref[...] += jnp.sum(e * e, axis=0, keepdims=True)

    dy, dyb, lpart = pl.pallas_call(
        loss_kern, name="loss", grid=(nr,),
        in_specs=[rowb(D, 0)] * 3, out_specs=[rowb(D, 0), rowb(D, 0), fullb((1, D))],
        out_shape=[jax.ShapeDtypeStruct((N, D), F32), jax.ShapeDtypeStruct((N, D), _BF),
                   jax.ShapeDtypeStruct((1, D), F32)],
        compiler_params=_cparams(("arbitrary",)),
    )(xs, y2, tgt)
    loss = lax.psum(jnp.sum(lpart) * (0.5 / D), AXES)

    G = {}
    d_merged = _mm(dyb, wo, tb=True, name="mm_dmerged", out_dtype=_BF)
    G['w_o'] = _mm(merged, dyb, ta=True, name="mm_dwo", out_dtype=_BF)

    dproj0 = lax.empty((N, c.P), _BF)
    any_spec = pl.BlockSpec(memory_space=pl.ANY)

    def merge_bwd_kern(dp_any, g_ref, dm_ref, oc_ref, om_ref, ome_ref, dg_ref, doc_ref, dom_ref, dome_ref):
        g = g_ref[...].astype(F32)
        dm = dm_ref[...].astype(F32)
        for idx, (o_in, d_out) in enumerate(((oc_ref, doc_ref), (om_ref, dom_ref), (ome_ref, dome_ref))):
            sg = _sig(g[:, idx * D:(idx + 1) * D])
            d_out[...] = (sg * dm).astype(d_out.dtype)
            dg_ref[:, idx * D:(idx + 1) * D] = (dm * o_in[...].astype(F32) * sg * (1.0 - sg)).astype(dg_ref.dtype)

    dproj1, d_oconv, d_omla, d_omem = pl.pallas_call(
        merge_bwd_kern, name="merge_bwd", grid=(nr,),
        in_specs=[any_spec, rowb(3 * D, g_blk), rowb(D, 0), rowb(D, 0), rowb(D, 0), rowb(D, 0)],
        out_specs=[rowb(3 * D, g_blk), rowb(D, 0), rowb(D, 0), rowb(D, 0)],
        out_shape=[jax.ShapeDtypeStruct((N, c.P), _BF)] + [jax.ShapeDtypeStruct((N, D), _BF)] * 3,
        input_output_aliases={0: 0}, compiler_params=_cparams(("parallel",)),
    )(dproj0, proj, d_merged, o_conv, o_mla, o_mem)

    G['w_conv_out'] = _mm(a_conv, d_oconv, ta=True, name="mm_dwco", out_dtype=_BF)
    d_aconv = _mm(d_oconv, wco, tb=True, name="mm_daconv", out_dtype=_BF)
    G['w_mla_out'] = _mm(a_mla, d_omla, ta=True, name="mm_dwmo", out_dtype=_BF)
    d_amla = _mm(d_omla, wmo, tb=True, name="mm_damla", out_dtype=_BF)
    G['w_mem_out'] = _mm(a_mem, d_omem, ta=True, name="mm_dwmemo", out_dtype=_BF)
    d_amem = _mm(d_omem, wmemo, tb=True, name="mm_damem", out_dtype=_BF)

    def conv_bwd_kern(dp_any, p_ref, prev_ref, next_ref, da_ref, dan_ref, w_ref, o_ref, dw_ref):
        i = pl.program_id(0)
        cg, bg, u, z = conv_parts(p_ref[...])
        pc, _, pu, _ = conv_parts(prev_ref[...])
        _, nbg, _, nz = conv_parts(next_ref[...])
        cu = cg * u
        sh1, sh2 = shifted(cu, pc * pu, i)
        w = w_ref[...]
        conv = w[0:1, :] * sh2 + w[1:2, :] * sh1 + w[2:3, :] * cu
        sg = _sig(z)
        sz = z * sg
        da = da_ref[...].astype(F32)
        dcy = da * sz
        d_z = da * (bg * conv) * (sg * (1.0 + z * (1.0 - sg)))
        d_b = dcy * conv
        dconv = dcy * bg
        dnext = dan_ref[...].astype(F32) * (nz * _sig(nz)) * nbg
        dnext = jnp.where(i < nr - 1, dnext, 0.0)
        rid = lax.broadcasted_iota(jnp.int32, cu.shape, 0)
        up1 = jnp.where(rid == R - 1, dnext[0:1, :], pltpu.roll(dconv, R - 1, 0))
        up2 = jnp.where(rid == R - 2, dnext[0:1, :], jnp.where(rid == R - 1, dnext[1:2, :], pltpu.roll(dconv, R - 2, 0)))
        dcu = w[2:3, :] * dconv + w[1:2, :] * up1 + w[0:1, :] * up2
        o_ref[:, 0:CW] = (dcu * u).astype(o_ref.dtype)
        o_ref[:, CW:2 * CW] = d_b.astype(o_ref.dtype)
        o_ref[:, 2 * CW:3 * CW] = (dcu * cg).astype(o_ref.dtype)
        o_ref[:, 3 * CW:4 * CW] = d_z.astype(o_ref.dtype)

        @pl.when(i == 0)
        def _():
            dw_ref[...] = jnp.zeros(dw_ref.shape, F32)

        dw_ref[0:1, :] += jnp.sum(dconv * sh2, axis=0, keepdims=True)
        dw_ref[1:2, :] += jnp.sum(dconv * sh1, axis=0, keepdims=True)
        dw_ref[2:3, :] += jnp.sum(dconv * cu, axis=0, keepdims=True)

    next_spec = pl.BlockSpec((HALO, 4 * CW), lambda i: (jnp.minimum((i + 1) * rh, N // HALO - 1), conv_blk))
    dan_spec = pl.BlockSpec((HALO, CW), lambda i: (jnp.minimum((i + 1) * rh, N // HALO - 1), 0))
    dproj2, g_convw = pl.pallas_call(
        conv_bwd_kern, name="conv_bwd", grid=(nr,),
        in_specs=[any_spec, rowb(4 * CW, conv_blk), prev_spec, next_spec, rowb(CW, 0), dan_spec, fullb((8, CW))],
        out_specs=[rowb(4 * CW, conv_blk), fullb((8, CW))],
        out_shape=[jax.ShapeDtypeStruct((N, c.P), _BF), jax.ShapeDtypeStruct((8, CW), F32)],
        input_output_aliases={0: 0}, compiler_params=_cparams(("arbitrary",)),
    )(dproj1, proj, proj, proj, d_aconv, d_aconv, convw8)

    def mem_bwd_kern(dp_any, qz_ref, da_ref, k_ref, v_ref, g_ref, o_ref, dk_ref, dv_ref, dg_ref):
        @pl.when(pl.program_id(0) == 0)
        def _():
            dk_ref[...] = jnp.zeros(dk_ref.shape, F32)
            dv_ref[...] = jnp.zeros(dv_ref.shape, F32)
            dg_ref[...] = jnp.zeros(dg_ref.shape, F32)

        for hh in range(MH):
            sl, qh, r, qn, p, y, z = mem_head(qz_ref, k_ref, v_ref, g_ref, hh)
            da = da_ref[:, sl].astype(F32)
            sg = _sig(z)
            dyh = da * (z * sg)
            o_ref[:, MW + hh * MHD:MW + (hh + 1) * MHD] = (da * y * (sg * (1.0 + z * (1.0 - sg)))).astype(o_ref.dtype)
            dyb_h = dyh.astype(_BF)
            dpm = _dot_nt(dyb_h, v_ref[:, sl])
            ds = (p * (dpm - jnp.sum(dpm * p, axis=1, keepdims=True)) * c.mscale).astype(_BF)
            dqn = _dot(ds, k_ref[:, sl])
            dk_ref[:, sl] += _dot_tn(ds, qn)
            dv_ref[:, sl] += _dot_tn(p.astype(_BF), dyb_h)
            dq, dgp = _rms_bwd(qh, r, g_ref[...], dqn, MHD)
            o_ref[:, sl] = dq.astype(o_ref.dtype)
            dg_ref[...] += jnp.sum(dgp, axis=0, keepdims=True)

    dproj3, d_memk, d_memv, g_mem_qn = pl.pallas_call(
        mem_bwd_kern, name="mem_bwd", grid=(nr,),
        in_specs=[any_spec, rowb(2 * MW, mem_blk), rowb(MW, 0), fullb((M, MW)), fullb((M, MW)), fullb((1, MHD))],
        out_specs=[rowb(2 * MW, mem_blk), fullb((M, MW)), fullb((M, MW)), fullb((1, MHD))],
        out_shape=[jax.ShapeDtypeStruct((N, c.P), _BF), jax.ShapeDtypeStruct((M, MW), F32),
                   jax.ShapeDtypeStruct((M, MW), F32), jax.ShapeDtypeStruct((1, MHD), F32)],
        input_output_aliases={0: 0}, compiler_params=_cparams(("arbitrary",)),
    )(dproj2, proj, d_amem, mem_k, mem_v, mem_qn_g)

    def memk_bwd_kern(kv_ref, dk_ref, dv_ref, g_ref, o_ref, dg_ref):
        dg = jnp.zeros((1, MHD), F32)
        for hh in range(MH):
            sl = slice(hh * MHD, (hh + 1) * MHD)
            kh, r = _rms(kv_ref[:, sl], MHD)
            dkr, dgp = _rms_bwd(kh, r, g_ref[...], dk_ref[:, sl], MHD)
            o_ref[:, sl] = dkr.astype(o_ref.dtype)
            dg += jnp.sum(dgp, axis=0, keepdims=True)
        o_ref[:, MW:] = dv_ref[...].astype(o_ref.dtype)
        dg_ref[...] = dg

    d_kvm, g_mem_kn = pl.pallas_call(
        memk_bwd_kern, name="memk_bwd", grid=(1,),
        in_specs=[fullb((M, 2 * MW)), fullb((M, MW)), fullb((M, MW)), fullb((1, MHD))],
        out_specs=[fullb((M, 2 * MW)), fullb((1, MHD))],
        out_shape=[jax.ShapeDtypeStruct((M, 2 * MW), _BF), jax.ShapeDtypeStruct((1, MHD), F32)],
        compiler_params=_cparams(("arbitrary",)),
    )(kvm, d_memk, d_memv, mem_kn_g)
    G['w_mem_kv'] = _mm(memn, d_kvm, ta=True, name="mm_dwmkv", out_dtype=_BF)
    d_memn = _mm(d_kvm, wmkv, tb=True, name="mm_dmemn", out_dtype=F32)

    def memnorm_bwd_kern(x_ref, d_ref, dg_ref):
        xh, _ = _rms(x_ref[...], D)
        dg_ref[...] = jnp.sum(d_ref[...] * xh, axis=0, keepdims=True)

    g_mem_norm = pl.pallas_call(
        memnorm_bwd_kern, name="memnorm_bwd", grid=(1,),
        in_specs=[fullb((M, D)), fullb((M, D))], out_specs=fullb((1, D)),
        out_shape=jax.ShapeDtypeStruct((1, D), F32), compiler_params=_cparams(("arbitrary",)),
    )(memx, d_memn)

    def gate_bwd_kern(dp_any, da_ref, y_ref, z_ref, dy_ref, dz_ref):
        z = z_ref[...].astype(F32)
        da = da_ref[...].astype(F32)
        sg = _sig(z)
        dy_ref[...] = (da * (z * sg)).astype(dy_ref.dtype)
        dz_ref[...] = (da * y_ref[...].astype(F32) * (sg * (1.0 + z * (1.0 - sg)))).astype(dz_ref.dtype)

    d_mlay, dproj4 = pl.pallas_call(
        gate_bwd_kern, name="gate_mla_bwd", grid=(nr,),
        in_specs=[any_spec, rowb(HV, 0), rowb(HV, 0), rowb(HV, mz_blk)],
        out_specs=[rowb(HV, 0), rowb(HV, mz_blk)],
        out_shape=[jax.ShapeDtypeStruct((N, HV), _BF), jax.ShapeDtypeStruct((N, c.P), _BF)],
        input_output_aliases={0: 1}, compiler_params=_cparams(("parallel",)),
    )(dproj3, d_amla, mla_y, proj)

    def attn_bwd_kern(q_ref, k_ref, v_ref, o_ref, do_ref, lse_ref, dq_ref, dk_ref, dv_ref, dq_sc, dl_sc, dk_sc, dv_sc):
        i = pl.program_id(1)
        q, do = q_ref[0], do_ref[...]
        delta = jnp.sum(do.astype(F32) * o_ref[...].astype(F32), axis=1, keepdims=True)
        dl_sc[...] = jnp.broadcast_to(delta, dl_sc.shape)
        dq_sc[...] = jnp.zeros(dq_sc.shape, F32)

        def step(t, diag):
            rows = pl.ds(pl.multiple_of(t * B, B), B)
            k = k_ref[0, rows, :]
            s = _dot_nt(q, k)
            if diag is not None:
                s = diag_mask(s, diag)
            dpm = _dot_nt(do, v_ref[rows, :])
            lse_t, dl = lse_ref[0], dl_sc[...]
            ps, dss = [], []
            for cb in range(B // LANES):
                cols = slice(cb * LANES, (cb + 1) * LANES)
                p_cb = jnp.exp2(s[:, cols] - lse_t)
                ps.append(p_cb.astype(_BF))
                dss.append((p_cb * (dpm[:, cols] - dl)).astype(_BF))
            p, ds = jnp.concatenate(ps, axis=1), jnp.concatenate(dss, axis=1)
            dvp = _dot_tn(p, do)
            dkp = _dot_tn(ds, q)
            if diag is not None:
                dk_sc[rows, :] = dkp
                dv_sc[rows, :] = dvp
            else:
                dk_sc[rows, :] += dkp
                dv_sc[rows, :] += dvp
            dq_sc[...] += _dot(ds, k)

        def run(first, count, n_diag):
            for u in range(count):
                step(first + u, u - (count - n_diag) if u >= count - n_diag else None)

        key_block_plan(i, run)
        dq_ref[0] = dq_sc[...].astype(dq_ref.dtype)

        @pl.when(i == nq - 1)
        def _():
            dk_ref[0] = dk_sc[...].astype(dk_ref.dtype)
            dv_ref[0] = dv_sc[...].astype(dv_ref.dtype)

    d_qcat, d_kcat, d_v = pl.pallas_call(
        attn_bwd_kern, name="attn_bwd", grid=(H, nq),
        in_specs=[q_blk, k_head, v_head, o_blk, o_blk, lse_blk],
        out_specs=[pl.BlockSpec((1, BQ, HW), lambda hh, i: (hh, i, 0)),
                   pl.BlockSpec((1, N, HW), lambda hh, i: (hh, 0, 0)),
                   pl.BlockSpec((1, N, LANES), lambda hh, i: (hh, 0, 0))],
        out_shape=[jax.ShapeDtypeStruct((H, N, HW), _BF), jax.ShapeDtypeStruct((H, N, HW), _BF),
                   jax.ShapeDtypeStruct((H, N, LANES), _BF)],
        scratch_shapes=[pltpu.VMEM((BQ, HW), F32), pltpu.VMEM((BQ, LANES), F32), pltpu.VMEM((N, HW), F32),
                        pltpu.VMEM((N, LANES), F32)],
        compiler_params=_cparams(("parallel", "arbitrary")),
    )(q_cat, k_cat, kv, mla_y, d_mlay, lse)

    def q_prep_bwd_kern(q_ref, dq_ref, t_ref, gn_ref, gr_ref, o_ref, dgn_ref, dgr_ref):
        @pl.when(jnp.logical_and(pl.program_id(0) == 0, pl.program_id(1) == 0))
        def _():
            dgn_ref[...] = jnp.zeros(dgn_ref.shape, F32)
            dgr_ref[...] = jnp.zeros(dgr_ref.shape, F32)

        for g in range(HG):
            blk = q_ref[:, g * HW:(g + 1) * HW].astype(F32)
            d = dq_ref[g].astype(F32) * c.scale
            nh, rn = _rms(blk[:, :LANES], c.NOPE)
            rhat, rr = _rms(blk[:, LANES:], c.ROPE)
            dn, dgn = _rms_bwd(nh, rn, gn_ref[...], d[:, :LANES], c.NOPE)
            drot = _rope_t(d[:, LANES:], t_ref[0], t_ref[1], t_ref[2])
            dr, dgr = _rms_bwd(rhat, rr, gr_ref[...], drot, c.ROPE)
            o_ref[:, g * HW:(g + 1) * HW] = jnp.concatenate([dn, dr], axis=1).astype(o_ref.dtype)
            dgn_ref[...] += jnp.sum(dgn, axis=0, keepdims=True)
            dgr_ref[...] += jnp.sum(dgr, axis=0, keepdims=True)

    d_qp, g_qn_nope, g_qn_rope = pl.pallas_call(
        q_prep_bwd_kern, name="q_prep_bwd", grid=(nrp, nhg),
        in_specs=[heads_in, heads_out, tabs_of(RP), fullb((1, LANES)), fullb((1, LANES))],
        out_specs=[heads_in, fullb((1, LANES)), fullb((1, LANES))],
        out_shape=[jax.ShapeDtypeStruct((N, H * HW), _BF), jax.ShapeDtypeStruct((1, LANES), F32),
                   jax.ShapeDtypeStruct((1, LANES), F32)],
        compiler_params=_cparams(("arbitrary", "arbitrary")),
    )(q_p, d_qcat, tabs, g_qn, g_qr)

    def k_prep_bwd_kern(kv_ref, dk_ref, dv_ref, gn_ref, o_ref, dkr_ref, dgn_ref):
        hg = pl.program_id(1)

        @pl.when(jnp.logical_and(pl.program_id(0) == 0, hg == 0))
        def _():
            dgn_ref[...] = jnp.zeros(dgn_ref.shape, F32)

        @pl.when(hg == 0)
        def _():
            dkr_ref[...] = jnp.zeros(dkr_ref.shape, F32)

        dkr = jnp.zeros((RP, LANES), F32)
        for g in range(HG):
            dk = dk_ref[g].astype(F32) * (1.0 / LOG2E)
            kn, r = _rms(kv_ref[:, g * HW:g * HW + LANES].astype(F32), c.NOPE)
            dkn, dgn = _rms_bwd(kn, r, gn_ref[...], dk[:, :LANES], c.NOPE)
            o_ref[:, g * HW:(g + 1) * HW] = jnp.concatenate([dkn.astype(o_ref.dtype), dv_ref[g]], axis=1)
            dgn_ref[...] += jnp.sum(dgn, axis=0, keepdims=True)
            dkr += dk[:, LANES:]
        dkr_ref[...] += dkr

    d_kv, d_krsum, g_kn_nope = pl.pallas_call(
        k_prep_bwd_kern, name="k_prep_bwd", grid=(nrp, nhg),
        in_specs=[heads_in, heads_out, pl.BlockSpec((HG, RP, LANES), lambda i, hg: (hg, i, 0)), fullb((1, LANES))],
        out_specs=[heads_in, pl.BlockSpec((RP, LANES), lambda i, hg: (i, 0)), fullb((1, LANES))],
        out_shape=[jax.ShapeDtypeStruct((N, H * HW), _BF), jax.ShapeDtypeStruct((N, LANES), F32),
                   jax.ShapeDtypeStruct((1, LANES), F32)],
        compiler_params=_cparams(("arbitrary", "arbitrary")),
    )(kv, d_kcat, d_v, g_kn)

    def krope_bwd_kern(p_ref, d_ref, t_ref, g_ref, o_ref, dg_ref):
        @pl.when(pl.program_id(0) == 0)
        def _():
            dg_ref[...] = jnp.zeros(dg_ref.shape, F32)

        xh, r = _rms(p_ref[...].astype(F32), c.ROPE)
        drot = _rope_t(d_ref[...], t_ref[0], t_ref[1], t_ref[2])
        dx, dg = _rms_bwd(xh, r, g_ref[...], drot, c.ROPE)
        o_ref[...] = dx.astype(o_ref.dtype)
        dg_ref[...] += jnp.sum(dg, axis=0, keepdims=True)

    d_kr, g_kn_rope = pl.pallas_call(
        krope_bwd_kern, name="krope_bwd", grid=(nr,),
        in_specs=[rowb(LANES, 0), rowb(LANES, 0), tabs_of(R), fullb((1, LANES))],
        out_specs=[rowb(LANES, 0), fullb((1, LANES))],
        out_shape=[jax.ShapeDtypeStruct((N, LANES), _BF), jax.ShapeDtypeStruct((1, LANES), F32)],
        compiler_params=_cparams(("arbitrary",)),
    )(kr_raw, d_krsum, tabs, g_kr)
    dproj5 = dproj4

    g_wuq_p = _mm(cqn, d_qp, ta=True, name="mm_dwuq", out_dtype=_BF)
    G['w_uq'] = g_wuq_p.reshape(QL, H, HW)[:, :, :c.NOPE + c.ROPE].reshape(QL, H * (c.NOPE + c.ROPE))
    d_cqn = _mm(d_qp, wuq_p, tb=True, name="mm_dcqn", out_dtype=F32)
    G['w_ukv'] = _mm(ckvn, d_kv, ta=True, name="mm_dwukv", out_dtype=_BF)
    d_ckvn = _mm(d_kv, wukv, tb=True, name="mm_dckvn", out_dtype=F32)

    def to_blocks(n, g):
        if n in COL_SHARDED:
            return jnp.transpose(g.reshape(g.shape[0], NDEV, -1), (1, 0, 2))
        return g.reshape(NDEV, -1, g.shape[1])

    def landing(b):
        own = lax.dynamic_index_in_dim(b, me, 0, keepdims=True)
        return lax.dynamic_update_index_in_dim(lax.empty(b.shape, b.dtype), own, me, 0)

    early = [n for n in BIG if n != 'w_in']
    blocks_e = [to_blocks(n, G[n]) for n in early]
    xe = _split_start(blocks_e, [landing(b) for b in blocks_e], False, "xchg_early_start")
    gq_after = mla_q_norm_g + xe[4][0:1, 0:1]

    def lora_bwd_kern(dp_any, p_ref, dq_ref, dkv_ref, gq_ref, gkv_ref, o_ref, dgq_ref, dgkv_ref):
        @pl.when(pl.program_id(0) == 0)
        def _():
            dgq_ref[...] = jnp.zeros(dgq_ref.shape, F32)
            dgkv_ref[...] = jnp.zeros(dgkv_ref.shape, F32)

        blk = p_ref[...].astype(F32)
        qh, rq = _rms(blk[:, :QL], QL)
        kh, rk = _rms(blk[:, QL:], KVL)
        dq, dgq = _rms_bwd(qh, rq, gq_ref[...], dq_ref[...], QL)
        dk, dgk = _rms_bwd(kh, rk, gkv_ref[...], dkv_ref[...], KVL)
        o_ref[:, :QL] = dq.astype(o_ref.dtype)
        o_ref[:, QL:] = dk.astype(o_ref.dtype)
        dgq_ref[...] += jnp.sum(dgq, axis=0, keepdims=True)
        dgkv_ref[...] += jnp.sum(dgk, axis=0, keepdims=True)

    dproj6, g_q_norm, g_kv_norm = pl.pallas_call(
        lora_bwd_kern, name="lora_bwd", grid=(nr,),
        in_specs=[any_spec, rowb(QL + KVL, lora_blk), rowb(QL, 0), rowb(KVL, 0), fullb((1, QL)), fullb((1, KVL))],
        out_specs=[rowb(QL + KVL, lora_blk), fullb((1, QL)), fullb((1, KVL))],
        out_shape=[jax.ShapeDtypeStruct((N, c.P), _BF), jax.ShapeDtypeStruct((1, QL), F32),
                   jax.ShapeDtypeStruct((1, KVL), F32)],
        input_output_aliases={0: 0}, compiler_params=_cparams(("arbitrary",)),
    )(dproj5, proj, d_cqn, d_ckvn, gq_after, mla_kv_norm_g)

    g_win_p = _mm(h, dproj6, ta=True, name="mm_dwin", out_dtype=_BF, bk=4096)
    g_wkr = _mm(h, d_kr, ta=True, name="mm_dwkr", out_dtype=_BF, bk=4096)
    blocks_w = [_win_blocks(g_win_p, g_wkr, c)]
    xw = _split_start(blocks_w, [landing(b) for b in blocks_w], False, "xchg_win_start")
    d_h = _mm(dproj6, win_p, tb=True, name="mm_dh", out_dtype=F32, bk=3072, after=xw[4], plus=(d_kr, w_krT))

    def final_bwd_kern(x_ref, g_ref, dh_ref, dy_ref, gx_ref, dg_ref):
        @pl.when(pl.program_id(0) == 0)
        def _():
            dg_ref[...] = jnp.zeros(dg_ref.shape, F32)

        xh, r = _rms(x_ref[...], D)
        dx, dg = _rms_bwd(xh, r, g_ref[...], dh_ref[...], D)
        gx_ref[...] = dy_ref[...] + dx
        dg_ref[...] += jnp.sum(dg, axis=0, keepdims=True)

    grad_x, g_norm = pl.pallas_call(
        final_bwd_kern, name="final_bwd", grid=(nr,),
        in_specs=[rowb(D, 0), fullb((1, D)), rowb(D, 0), rowb(D, 0)],
        out_specs=[rowb(D, 0), fullb((1, D))],
        out_shape=[jax.ShapeDtypeStruct((N, D), F32), jax.ShapeDtypeStruct((1, D), F32)],
        compiler_params=_cparams(("arbitrary",)),
    )(xs, norm_g, d_h, dy)

    recv_e = _split_wait(xe, False, grad_x, "xchg_early_wait")
    recv_w = _split_wait(xw, False, grad_x, "xchg_win_wait")
    res = [{}, {}, {}, {}]
    for n, parts in zip(['w_in'] + early, list(recv_w) + list(recv_e)):
        outs = _adam(parts, W[n][0], Mo[n][0], Vo[n][0], "adam_" + n)
        for k in range(4):
            res[k][n] = outs[k][None]

    small_g = {'norm_g': g_norm, 'mla_q_norm_g': g_q_norm, 'mla_kv_norm_g': g_kv_norm,
               'mla_qn_nope_g': g_qn_nope, 'mla_qn_rope_g': g_qn_rope[:, :c.ROPE], 'mla_kn_nope_g': g_kn_nope,
               'mla_kn_rope_g': g_kn_rope[:, :c.ROPE], 'mem_norm_g': g_mem_norm, 'mem_qn_g': g_mem_qn,
               'mem_kn_g': g_mem_kn}
    small_part = _pack([small_g[n] for n in SMALL] + [g_convw[0:3, :]], 0)
    small_all = _all_gather([small_part], "ag_small_grads")[0]
    small_shapes = [W[n].shape for n in SMALL]
    pieces = _unpack(small_all, small_shapes + [(3, CW)])
    cw8 = CW // NDEV
    conv_mine = lax.dynamic_slice_in_dim(pieces[-1].reshape(NDEV, 3, NDEV, cw8), me, 1, axis=2)[:, :, 0, :]
    sm_parts = _pack(pieces[:-1] + [conv_mine], 1)
    sm_names = SMALL + ['conv_w']
    sm_shapes = small_shapes + [(3, cw8)]
    w_sm = _pack([W[n] for n in SMALL] + [conv_w[0]], 0)
    m_sm = _pack([Mo[n] for n in SMALL] + [m_conv_w[0]], 0)
    v_sm = _pack([Vo[n] for n in SMALL] + [v_conv_w[0]], 0)
    outs_sm = [_unpack(o, sm_shapes) for o in _adam(sm_parts, w_sm, m_sm, v_sm, "adam_small")]
    for k in range(4):
        for n, a in zip(sm_names, outs_sm[k]):
            res[k][n] = a[None] if n == 'conv_w' else a
    return (loss, grad_x[None], *[res[0][n] for n in WEIGHTS], *[res[1][n] for n in WEIGHTS],
            *[res[2][n] for n in WEIGHTS], *[res[3][n] for n in WEIGHTS])
```

```python
import math

import jax
import jax.numpy as jnp
from jax import lax
from jax.experimental import pallas as pl
from jax.experimental.pallas import tpu as pltpu

F32 = jnp.float32
_BF = jnp.bfloat16
EPS = 1e-6
CHUNK = 64
ROPE_THETA = 10000.0
ADAM_LR, ADAM_B1, ADAM_B2, ADAM_EPS, ADAM_WD, ADAM_STEP = 0.001, 0.9, 0.999, 1e-08, 0.01, 10
NDEV = 8
AXES = ("x", "y", "c")
MESH = pl.DeviceIdType.MESH
LANES = 128
NEG = -1e30
LOG2E = math.log2(math.e)
V7X_VMEM_LIMIT = 56 * 1024 * 1024
PACK_C = 1024
ATT_BLOCK = 512
ATT_UNROLL = 4
ATT_QBLOCKS = 2
ADAM_BLOCK_ELEMS = 256 * 1024

WEIGHTS = ['norm_g', 'w_in', 'conv_w', 'w_conv_out', 'mla_q_norm_g', 'w_uq', 'mla_kv_norm_g', 'w_ukv',
           'mla_qn_nope_g', 'mla_qn_rope_g', 'mla_kn_nope_g', 'mla_kn_rope_g', 'w_mla_out', 'mem_norm_g',
           'w_mem_kv', 'mem_qn_g', 'mem_kn_g', 'w_mem_out', 'w_o']
BIG = ['w_in', 'w_conv_out', 'w_uq', 'w_ukv', 'w_mla_out', 'w_mem_kv', 'w_mem_out', 'w_o']
COL_SHARDED = ('w_in', 'w_conv_out', 'w_uq', 'w_ukv', 'w_mem_out')
SMALL = ['norm_g', 'mla_q_norm_g', 'mla_kv_norm_g', 'mla_qn_nope_g', 'mla_qn_rope_g', 'mla_kn_nope_g',
         'mla_kn_rope_g', 'mem_norm_g', 'mem_qn_g', 'mem_kn_g']


def _tile(dim, target, align):
    if dim <= target:
        return dim
    t = target - target % align
    while t > 0:
        if dim % t == 0:
            return t
        t -= align
    raise ValueError(f"no tile for {dim} {target} {align}")


def _cparams(sem):
    return pltpu.CompilerParams(dimension_semantics=sem, vmem_limit_bytes=V7X_VMEM_LIMIT)


def _sig(x):
    return 1.0 / (1.0 + jnp.exp(-x))


def _rms(x, n):
    r = lax.rsqrt(jnp.sum(x * x, axis=-1, keepdims=True) * (1.0 / n) + EPS)
    return x * r, r


def _rms_bwd(xhat, r, g, dy, n):
    dxh = dy * g
    dx = r * (dxh - xhat * (jnp.sum(dxh * xhat, axis=-1, keepdims=True) * (1.0 / n)))
    return dx, dy * xhat


def _rope(x, cosp, sina, sinb):
    return x * cosp + pltpu.roll(x, 96, 1) * sina + pltpu.roll(x, 32, 1) * sinb


def _rope_t(d, cosp, sina, sinb):
    return d * cosp + pltpu.roll(d * sina, 32, 1) + pltpu.roll(d * sinb, 96, 1)


def _dot_nt(a, b):
    return lax.dot_general(a, b, (((1,), (1,)), ((), ())), preferred_element_type=F32)


def _dot_tn(a, b):
    return lax.dot_general(a, b, (((0,), (0,)), ((), ())), preferred_element_type=F32)


def _dot(a, b):
    return jnp.dot(a, b, preferred_element_type=F32)


def _all_gather(shards, name):
    na = len(shards)
    nc = 9
    halves = [s.shape[0] // 2 if s.shape[0] % 32 == 0 else None for s in shards]

    def body(*refs):
        x_refs, out_refs = refs[:na], refs[na:2 * na]
        send_sems, recv_sems, local_sems = refs[2 * na:]
        x, y, c = lax.axis_index("x"), lax.axis_index("y"), lax.axis_index("c")
        me, sib = (x, y, c), (x, y, 1 - c)
        px, py, pd = (1 - x, y, c), (x, 1 - y, c), (1 - x, 1 - y, c)

        def other_core(p):
            return (p[0], p[1], 1 - p[2])

        def rows(a, blk, part=None):
            r = out_refs[a].at[4 * blk[0] + 2 * blk[1] + blk[2]]
            if part is None or halves[a] is None:
                return r
            return r.at[pl.ds(0, halves[a])] if part == 0 else r.at[pl.ds(halves[a], halves[a])]

        def copy(a, k, blk, to, part=None, src=None):
            dst = rows(a, blk, part)
            return pltpu.make_async_remote_copy(
                src_ref=dst if src is None else src, dst_ref=dst,
                send_sem=send_sems.at[nc * a + k], recv_sem=recv_sems.at[nc * a + k],
                device_id=to, device_id_type=MESH)

        mine = [pltpu.make_async_copy(x_refs[a], rows(a, me), local_sems.at[a]) for a in range(na)]
        for cp in mine:
            cp.start()
        started = []
        for a in range(na):
            started += [copy(a, 0, me, px, src=x_refs[a]), copy(a, 1, me, py, src=x_refs[a]),
                        copy(a, 2, me, sib, src=x_refs[a])]
        for cp in started:
            cp.start()

        def forward(cp):
            cp.start()
            started.append(cp)

        for a in range(na):
            copy(a, 0, px, me).wait_recv()
            forward(copy(a, 3, px, py, part=0))
            forward(copy(a, 4, px, sib))
        for a in range(na):
            copy(a, 1, py, me).wait_recv()
            if halves[a] is not None:
                forward(copy(a, 5, py, px, part=1))
            forward(copy(a, 6, py, sib))
        for a in range(na):
            copy(a, 3, pd, me, part=0).wait_recv()
            forward(copy(a, 7, pd, sib, part=0))
            if halves[a] is not None:
                copy(a, 5, pd, me, part=1).wait_recv()
                forward(copy(a, 8, pd, sib, part=1))
        for a in range(na):
            copy(a, 2, sib, me).wait_recv()
            copy(a, 4, other_core(px), me).wait_recv()
            copy(a, 6, other_core(py), me).wait_recv()
            copy(a, 7, other_core(pd), me, part=0).wait_recv()
            if halves[a] is not None:
                copy(a, 8, other_core(pd), me, part=1).wait_recv()
        for cp in started:
            cp.wait_send()
        for cp in mine:
            cp.wait()

    any_spec = pl.BlockSpec(memory_space=pl.ANY)
    return pl.pallas_call(
        body, name=name,
        out_shape=[jax.ShapeDtypeStruct((NDEV,) + s.shape, s.dtype) for s in shards],
        in_specs=[any_spec] * na, out_specs=[any_spec] * na,
        scratch_shapes=[pltpu.SemaphoreType.DMA((nc * na,)), pltpu.SemaphoreType.DMA((nc * na,)),
                        pltpu.SemaphoreType.DMA((na,))],
    )(*shards)


_HBM = pl.BlockSpec(memory_space=pltpu.HBM)
_SEM = pl.BlockSpec(memory_space=pltpu.SEMAPHORE)
_EFFECT = pltpu.SideEffectType.DATAFLOW_SIDE_EFFECTING


def _split_copy(a, k, src_refs, land_refs, send_sems, recv_sems, gather, receive_side):
    x, y, c = lax.axis_index("x"), lax.axis_index("y"), lax.axis_index("c")
    me = 4 * x + 2 * y + c
    tx, ty, tc = x ^ ((k + 1) >> 2 & 1), y ^ ((k + 1) >> 1 & 1), c ^ ((k + 1) & 1)
    peer = 4 * tx + 2 * ty + tc
    return pltpu.make_async_remote_copy(
        src_ref=src_refs[a] if gather else src_refs[a].at[peer],
        dst_ref=land_refs[a].at[peer if receive_side else me],
        send_sem=send_sems.at[7 * a + k], recv_sem=recv_sems.at[7 * a + k],
        device_id=(tx, ty, tc), device_id_type=MESH)


def _split_start(srcs, lands, gather, name, after=None):
    na = len(srcs)
    extra = [] if after is None else [after]

    def body(*refs):
        src_refs, land_refs = refs[:na], refs[na:2 * na]
        send_sems, recv_sems = refs[2 * na + len(extra)], refs[2 * na + len(extra) + 1]
        token = refs[-1]
        for k in range(7):
            for a in range(na):
                _split_copy(a, k, src_refs, land_refs, send_sems, recv_sems, gather, False).start()
        token[...] = jnp.zeros_like(token)

    hbm = [pltpu.HBM(b.shape, b.dtype) for b in list(srcs) + list(lands)]
    outs = pl.pallas_call(
        body, name=name,
        out_shape=(pltpu.SemaphoreType.DMA((7 * na,)), pltpu.SemaphoreType.DMA((7 * na,)), *hbm,
                   jax.ShapeDtypeStruct((8, LANES), F32)),
        in_specs=[_HBM] * (2 * na) + [pl.BlockSpec(memory_space=pl.ANY)] * len(extra),
        out_specs=(_SEM, _SEM, *[_HBM] * (2 * na), pl.BlockSpec(memory_space=pltpu.VMEM)),
        input_output_aliases={j: 2 + j for j in range(2 * na)},
        compiler_params=pltpu.CompilerParams(has_side_effects=_EFFECT),
    )(*[pltpu.with_memory_space_constraint(b, pltpu.HBM) for b in srcs],
      *[pltpu.with_memory_space_constraint(l, pltpu.HBM) for l in lands], *extra)
    return outs[0], outs[1], outs[2:2 + na], outs[2 + na:2 + 2 * na], outs[-1]


def _split_wait(started, gather, after, name):
    send_sems, recv_sems, srcs, lands, _ = started
    na = len(srcs)

    def body(*refs):
        src_refs, land_refs = refs[:na], refs[na:2 * na]
        send_s, recv_s = refs[2 * na], refs[2 * na + 1]
        for k in range(7):
            for a in range(na):
                cp = _split_copy(a, k, src_refs, land_refs, send_s, recv_s, gather, True)
                cp.wait_send()
                cp.wait_recv()

    hbm = [pltpu.HBM(b.shape, b.dtype) for b in list(srcs) + list(lands)]
    outs = pl.pallas_call(
        body, name=name, out_shape=tuple(hbm),
        in_specs=[_HBM] * (2 * na) + [_SEM, _SEM, pl.BlockSpec(memory_space=pl.ANY)],
        out_specs=tuple([_HBM] * (2 * na)),
        input_output_aliases={j: j for j in range(2 * na)},
        compiler_params=pltpu.CompilerParams(has_side_effects=_EFFECT),
    )(*srcs, *lands, send_sems, recv_sems, after)
    return outs[na:]


def _seg_rows(size):
    rows = -(-size // PACK_C)
    return -(-rows // 16) * 16


def _pack(arrs, lead):
    parts = []
    for a in arrs:
        lshape = a.shape[:lead]
        f = a.reshape(lshape + (-1,)).astype(F32)
        rows = _seg_rows(f.shape[-1])
        f = jnp.pad(f, [(0, 0)] * lead + [(0, rows * PACK_C - f.shape[-1])])
        parts.append(f.reshape(lshape + (rows, PACK_C)))
    return jnp.concatenate(parts, axis=lead)


def _unpack(buf, shapes):
    lshape = buf.shape[:-2]
    out, r = [], 0
    for shp in shapes:
        size = math.prod(shp)
        rows = _seg_rows(size)
        seg = buf[..., r:r + rows, :].reshape(lshape + (rows * PACK_C,))[..., :size]
        out.append(seg.reshape(lshape + tuple(shp)))
        r += rows
    return out


def _mm(a, b, *, name, out_dtype, ta=False, tb=False, bm=1024, bn=1024, bk=2048, after=None, plus=None):
    if ta:
        kdim, m = a.shape
    else:
        m, kdim = a.shape
    n, k2 = b.shape if tb else b.shape[::-1]
    assert kdim == k2 and not (ta and tb), (a.shape, b.shape)
    bm = _tile(m, bm, LANES if ta else 16)
    bn = _tile(n, bn, LANES)
    bk = _tile(kdim, bk, LANES)
    nk = kdim // bk
    n_after = 0 if after is None else 1
    n_plus = 0 if plus is None else 2

    def kern(a_ref, b_ref, *rest):
        plus_refs = rest[n_after:n_after + n_plus]
        o_ref, scratch = rest[n_after + n_plus], rest[n_after + n_plus + 1:]
        part = (_dot_tn if ta else _dot_nt if tb else _dot)(a_ref[...], b_ref[...])

        def first(p):
            return p + _dot(plus_refs[0][...], plus_refs[1][...]) if plus is not None else p

        if nk == 1:
            o_ref[...] = first(part).astype(o_ref.dtype)
        else:
            acc = scratch[0] if scratch else o_ref
            k = pl.program_id(2)

            @pl.when(k == 0)
            def _():
                acc[...] = first(jnp.zeros(acc.shape, F32))

            acc[...] += part
            if scratch:
                @pl.when(k == nk - 1)
                def _():
                    o_ref[...] = acc[...].astype(o_ref.dtype)

    a_spec = pl.BlockSpec((bk, bm), lambda i, j, k: (k, i)) if ta else pl.BlockSpec((bm, bk), lambda i, j, k: (i, k))
    b_spec = pl.BlockSpec((bn, bk), lambda i, j, k: (j, k)) if tb else pl.BlockSpec((bk, bn), lambda i, j, k: (k, j))
    extra_specs, extra_args = [], []
    if after is not None:
        extra_specs.append(pl.BlockSpec(after.shape, lambda i, j, k: (0, 0)))
        extra_args.append(after)
    if plus is not None:
        kk = plus[0].shape[1]
        extra_specs += [pl.BlockSpec((bm, kk), lambda i, j, k: (i, 0)), pl.BlockSpec((kk, bn), lambda i, j, k: (0, j))]
        extra_args += list(plus)
    return pl.pallas_call(
        kern, name=name, grid=(m // bm, n // bn, nk),
        in_specs=[a_spec, b_spec] + extra_specs,
        out_specs=pl.BlockSpec((bm, bn), lambda i, j, k: (i, j)),
        out_shape=jax.ShapeDtypeStruct((m, n), out_dtype),
        scratch_shapes=[pltpu.VMEM((bm, bn), F32)] if nk > 1 and out_dtype != F32 else [],
        compiler_params=_cparams(("parallel", "parallel", "arbitrary")),
    )(a, b, *extra_args)


def _adam(parts, w_a, m_a, v_a, name):
    rows, cols = w_a.shape
    rb = _tile(rows, max(8, ADAM_BLOCK_ELEMS // cols // 8 * 8), 8)
    bc1 = 1.0 - ADAM_B1 ** ADAM_STEP
    bc2 = 1.0 - ADAM_B2 ** ADAM_STEP

    def adam_kern(p_ref, w_ref, m_ref, v_ref, g_ref, d_ref, nm_ref, nv_ref):
        g = p_ref[0].astype(F32)
        for j in range(1, NDEV):
            g = g + p_ref[j].astype(F32)
        m_new = ADAM_B1 * m_ref[...] + (1.0 - ADAM_B1) * g
        v_new = ADAM_B2 * v_ref[...] + (1.0 - ADAM_B2) * (g * g)
        g_ref[...] = g
        nm_ref[...] = m_new
        nv_ref[...] = v_new
        d_ref[...] = -ADAM_LR * ((m_new / bc1) / (jnp.sqrt(v_new / bc2) + ADAM_EPS) + ADAM_WD * w_ref[...])

    blk = pl.BlockSpec((rb, cols), lambda i: (i, 0))
    return pl.pallas_call(
        adam_kern, name=name, grid=(rows // rb,),
        in_specs=[pl.BlockSpec((NDEV, rb, cols), lambda i: (0, i, 0)), blk, blk, blk],
        out_specs=[blk] * 4, out_shape=[jax.ShapeDtypeStruct((rows, cols), F32)] * 4,
        compiler_params=_cparams(("parallel",)),
    )(parts, w_a, m_a, v_a)


class _Cfg:
    pass


def _config(x, conv_w, w_uq, w_ukv, mla_qn_nope_g, mla_qn_rope_g, mem, mem_qn_g, w_mem_out, w_mla_out):
    c = _Cfg()
    c.N, c.D = x.shape[1], x.shape[2]
    c.CW = conv_w.shape[2] * NDEV
    c.QL, c.KVL = w_uq.shape[1], w_ukv.shape[1]
    c.NOPE, c.ROPE = mla_qn_nope_g.shape[1], mla_qn_rope_g.shape[1]
    c.H = w_uq.shape[2] * NDEV // (c.NOPE + c.ROPE)
    c.V = w_ukv.shape[2] * NDEV // c.H - c.NOPE
    assert c.NOPE == LANES and c.V == LANES and c.ROPE == LANES // 2
    c.HW = 2 * LANES
    c.HV = c.H * c.V
    assert w_mla_out.shape[1] * NDEV == c.HV
    c.M = mem.shape[1]
    c.MHD = mem_qn_g.shape[1]
    c.MW = w_mem_out.shape[1]
    c.MH = c.MW // c.MHD
    c.o_conv = 0
    c.o_mz = 4 * c.CW
    c.o_g = c.o_mz + c.HV
    c.o_mem = c.o_g + 3 * c.D
    c.o_lora = c.o_mem + 2 * c.MW
    c.P = c.o_lora + c.QL + c.KVL
    assert c.o_mz % c.HV == 0 and c.o_g % (3 * c.D) == 0 and c.o_mem % (2 * c.MW) == 0
    assert c.o_lora % (c.QL + c.KVL) == 0 and c.QL % LANES == 0 and c.KVL % LANES == 0
    c.IN = 4 * c.CW + c.QL + c.KVL + c.ROPE + c.HV + 2 * c.MW + 3 * c.D
    c.R = _tile(c.N, 256, 16)
    c.RP = _tile(c.N, 512, 16)
    c.HG = _tile(c.H, 4, 1)
    c.B = _tile(c.N, ATT_BLOCK, CHUNK)
    c.scale = float((c.NOPE + c.ROPE) ** -0.5)
    c.mscale = float(c.MHD ** -0.5)
    return c


def _win_segments(c):
    ref_order = (('conv', 4 * c.CW), ('lora', c.QL + c.KVL), ('kr', c.ROPE), ('mz', c.HV), ('mem', 2 * c.MW), ('g', 3 * c.D))
    mine = {'conv': c.o_conv, 'mz': c.o_mz, 'g': c.o_g, 'mem': c.o_mem, 'lora': c.o_lora, 'kr': 0}
    segs, o = [], 0
    for nm, wd in ref_order:
        segs.append((nm, o, wd, mine[nm]))
        o += wd
    return segs


def _win_split(g_win, c):
    n8 = g_win.shape[2]

    def columns(a, wd):
        return [g_win[j][:, max(a, j * n8) - j * n8:min(a + wd, (j + 1) * n8) - j * n8]
                for j in range(a // n8, (a + wd - 1) // n8 + 1)]

    segs = {nm: (a, wd) for nm, a, wd, _ in _win_segments(c)}
    main = [p for nm in ('conv', 'mz', 'g', 'mem', 'lora') for p in columns(*segs[nm])]
    kr = jnp.concatenate(columns(*segs['kr']) + [jnp.zeros((g_win.shape[1], LANES - c.ROPE), g_win.dtype)], axis=1)
    return jnp.concatenate(main, axis=1), kr


def _win_blocks(g, g_kr, c):
    n8 = c.IN // NDEV
    blocks = []
    for j in range(NDEV):
        lo, hi = j * n8, (j + 1) * n8
        parts = []
        for nm, a, wd, mine in _win_segments(c):
            s, e = max(a, lo), min(a + wd, hi)
            if s < e:
                parts.append((g_kr if nm == 'kr' else g)[:, mine + s - a:mine + e - a])
        blocks.append(jnp.concatenate(parts, axis=1))
    return jnp.stack(blocks, axis=0)


def kernel(x, positions, mem, norm_g, w_in, conv_w, w_conv_out, mla_q_norm_g, w_uq, mla_kv_norm_g, w_ukv, mla_qn_nope_g, mla_qn_rope_g, mla_kn_nope_g, mla_kn_rope_g, w_mla_out, mem_norm_g, w_mem_kv, mem_qn_g, mem_kn_g, w_mem_out, w_o, loss_target, m_norm_g, m_w_in, m_conv_w, m_w_conv_out, m_mla_q_norm_g, m_w_uq, m_mla_kv_norm_g, m_w_ukv, m_mla_qn_nope_g, m_mla_qn_rope_g, m_mla_kn_nope_g, m_mla_kn_rope_g, m_w_mla_out, m_mem_norm_g, m_w_mem_kv, m_mem_qn_g, m_mem_kn_g, m_w_mem_out, m_w_o, v_norm_g, v_w_in, v_conv_w, v_w_conv_out, v_mla_q_norm_g, v_w_uq, v_mla_kv_norm_g, v_w_ukv, v_mla_qn_nope_g, v_mla_qn_rope_g, v_mla_kn_nope_g, v_mla_kn_rope_g, v_w_mla_out, v_mem_norm_g, v_w_mem_kv, v_mem_qn_g, v_mem_kn_g, v_w_mem_out, v_w_o):
    args = dict(locals())
    W = {n: args[n] for n in WEIGHTS}
    Mo = {n: args['m_' + n] for n in WEIGHTS}
    Vo = {n: args['v_' + n] for n in WEIGHTS}
    c = _config(x, conv_w, w_uq, w_ukv, mla_qn_nope_g, mla_qn_rope_g, mem, mem_qn_g, w_mem_out, w_mla_out)
    N, D, R, B, H = c.N, c.D, c.R, c.B, c.H
    assert x.shape[0] == 1
    xs = x[0]
    tgt = loss_target[0]
    memx = mem[0]
    me = 4 * lax.axis_index("x") + 2 * lax.axis_index("y") + lax.axis_index("c")
    nr = N // R

    g_win, g_taps = _all_gather([W['w_in'][0].astype(_BF), conv_w[0]], "ag_w_in")
    rest = [n for n in BIG if n != 'w_in']
    shards_r = [W[n][0].astype(_BF) for n in rest]
    lands_r = [lax.dynamic_update_index_in_dim(lax.empty((NDEV,) + s.shape, s.dtype), s[None], me, 0)
               for s in shards_r]
    ag_rest = _split_start(shards_r, lands_r, True, "ag_rest_start", after=g_win)
    win_p, w_kr = _win_split(g_win, c)
    w_krT = w_kr.T
    convw = jnp.transpose(g_taps, (1, 0, 2)).reshape(3, c.CW)
    convw8 = jnp.pad(convw, ((0, 5), (0, 0)))

    def rowb(width, cidx, rows=R):
        return pl.BlockSpec((rows, width), lambda i, _c=cidx: (i, _c))

    R2 = c.RP
    nr2 = N // R2

    def rowb2(width, cidx):
        return rowb(width, cidx, R2)

    def fullb(shape):
        nd = len(shape)
        return pl.BlockSpec(shape, lambda *_: (0,) * nd)

    def pad_lanes(g, w=LANES):
        return jnp.pad(g, ((0, 0), (0, w - g.shape[1])))

    def tabs_of(rows):
        return pl.BlockSpec((3, rows, LANES), lambda i, *_: (0, i, 0))

    half = c.ROPE // 2
    inv_freq = jnp.power(ROPE_THETA, -jnp.arange(half, dtype=F32) / half)
    invf = jnp.concatenate([inv_freq, inv_freq, jnp.zeros((LANES - c.ROPE,), F32)])[None, :]
    pos_col = positions[0].astype(F32).reshape(N, 1)

    def rope_tab_kern(pos_ref, invf_ref, o_ref):
        ang = pos_ref[...] * invf_ref[...]
        co, si = jnp.cos(ang), jnp.sin(ang)
        lane = lax.broadcasted_iota(jnp.int32, ang.shape, 1)
        o_ref[0] = jnp.where(lane < c.ROPE, co, 0.0)
        o_ref[1] = jnp.where(lane < half, -si, 0.0)
        o_ref[2] = jnp.where(jnp.logical_and(lane >= half, lane < c.ROPE), si, 0.0)

    tabs = pl.pallas_call(
        rope_tab_kern, name="rope_tab", grid=(nr,),
        in_specs=[pl.BlockSpec((R, 1), lambda i: (i, 0)), fullb((1, LANES))],
        out_specs=tabs_of(R),
        out_shape=jax.ShapeDtypeStruct((3, N, LANES), F32),
        compiler_params=_cparams(("parallel",)),
    )(pos_col, invf)

    def make_rms_kern():
        def rms_fwd_kern(x_ref, g_ref, o_ref):
            xh, _ = _rms(x_ref[...].astype(F32), x_ref.shape[-1])
            o_ref[...] = (xh * g_ref[...]).astype(o_ref.dtype)
        return rms_fwd_kern

    h = pl.pallas_call(
        make_rms_kern(), name="rms_x", grid=(nr2,),
        in_specs=[rowb2(D, 0), fullb((1, D))], out_specs=rowb2(D, 0),
        out_shape=jax.ShapeDtypeStruct((N, D), _BF), compiler_params=_cparams(("parallel",)),
    )(xs, norm_g)

    proj = _mm(h, win_p, name="mm_proj", out_dtype=_BF, after=ag_rest[4])
    kr_raw = _mm(h, w_kr, name="mm_kr", out_dtype=_BF)

    Wf = {}
    for n, g in zip(rest, _split_wait(ag_rest, True, proj, "ag_rest_wait")):
        if n in COL_SHARDED:
            Wf[n] = jnp.transpose(g, (1, 0, 2)).reshape(g.shape[1], -1)
        else:
            Wf[n] = g.reshape(-1, g.shape[2])
    wuq = Wf['w_uq'].reshape(c.QL, H, c.NOPE + c.ROPE)
    wuq_p = jnp.pad(wuq, ((0, 0), (0, 0), (0, c.HW - c.NOPE - c.ROPE))).reshape(c.QL, H * c.HW)
    wukv = Wf['w_ukv']
    wco, wmo, wmkv, wmemo, wo = Wf['w_conv_out'], Wf['w_mla_out'], Wf['w_mem_kv'], Wf['w_mem_out'], Wf['w_o']

    CW = c.CW
    conv_blk = c.o_conv // (4 * CW)
    HALO = 16
    rh = R // HALO

    def conv_parts(blk):
        blk = blk.astype(F32)
        return blk[:, 0:CW], blk[:, CW:2 * CW], blk[:, 2 * CW:3 * CW], blk[:, 3 * CW:4 * CW]

    def shifted(cu, prev, i):
        prev = jnp.where(i > 0, prev, 0.0)
        rid = lax.broadcasted_iota(jnp.int32, cu.shape, 0)
        last, last2 = prev[HALO - 1:HALO, :], prev[HALO - 2:HALO - 1, :]
        sh1 = jnp.where(rid == 0, last, pltpu.roll(cu, 1, 0))
        sh2 = jnp.where(rid == 0, last2, jnp.where(rid == 1, last, pltpu.roll(cu, 2, 0)))
        return sh1, sh2

    def conv_fwd_kern(p_ref, prev_ref, w_ref, o_ref):
        i = pl.program_id(0)
        cg, bg, u, z = conv_parts(p_ref[...])
        pc, _, pu, _ = conv_parts(prev_ref[...])
        cu = cg * u
        sh1, sh2 = shifted(cu, pc * pu, i)
        w = w_ref[...]
        conv = w[0:1, :] * sh2 + w[1:2, :] * sh1 + w[2:3, :] * cu
        o_ref[...] = (bg * conv * (z * _sig(z))).astype(o_ref.dtype)

    prev_spec = pl.BlockSpec((HALO, 4 * CW), lambda i: (jnp.maximum(i * rh - 1, 0), conv_blk))
    a_conv = pl.pallas_call(
        conv_fwd_kern, name="conv_fwd", grid=(nr,),
        in_specs=[rowb(4 * CW, conv_blk), prev_spec, fullb((8, CW))],
        out_specs=rowb(CW, 0), out_shape=jax.ShapeDtypeStruct((N, CW), _BF),
        compiler_params=_cparams(("parallel",)),
    )(proj, proj, convw8)
    o_conv = _mm(a_conv, wco, name="mm_oconv", out_dtype=_BF)

    QL, KVL, HW = c.QL, c.KVL, c.HW
    lora_blk = c.o_lora // (QL + KVL)

    def lora_fwd_kern(p_ref, gq_ref, gkv_ref, q_ref, kv_ref):
        blk = p_ref[...].astype(F32)
        qh, _ = _rms(blk[:, :QL], QL)
        kh, _ = _rms(blk[:, QL:], KVL)
        q_ref[...] = (qh * gq_ref[...]).astype(q_ref.dtype)
        kv_ref[...] = (kh * gkv_ref[...]).astype(kv_ref.dtype)

    cqn, ckvn = pl.pallas_call(
        lora_fwd_kern, name="lora_fwd", grid=(nr,),
        in_specs=[rowb(QL + KVL, lora_blk), fullb((1, QL)), fullb((1, KVL))],
        out_specs=[rowb(QL, 0), rowb(KVL, 0)],
        out_shape=[jax.ShapeDtypeStruct((N, QL), _BF), jax.ShapeDtypeStruct((N, KVL), _BF)],
        compiler_params=_cparams(("parallel",)),
    )(proj, mla_q_norm_g, mla_kv_norm_g)
    q_p = _mm(cqn, wuq_p, name="mm_q", out_dtype=_BF, bn=2048)
    kv = _mm(ckvn, wukv, name="mm_kv", out_dtype=_BF, bn=2048)

    g_qn, g_qr = mla_qn_nope_g, pad_lanes(mla_qn_rope_g)
    g_kn, g_kr = mla_kn_nope_g, pad_lanes(mla_kn_rope_g)

    def krope_fwd_kern(p_ref, t_ref, g_ref, o_ref):
        xh, _ = _rms(p_ref[...].astype(F32), c.ROPE)
        o_ref[...] = _rope(xh * g_ref[...], t_ref[0], t_ref[1], t_ref[2]).astype(o_ref.dtype)

    k_rope = pl.pallas_call(
        krope_fwd_kern, name="krope_fwd", grid=(nr,),
        in_specs=[rowb(LANES, 0), tabs_of(R), fullb((1, LANES))],
        out_specs=rowb(LANES, 0), out_shape=jax.ShapeDtypeStruct((N, LANES), _BF),
        compiler_params=_cparams(("parallel",)),
    )(kr_raw, tabs, g_kr)

    RP, HG = c.RP, c.HG
    nrp, nhg = N // RP, H // HG
    heads_in = pl.BlockSpec((RP, HG * HW), lambda i, hg: (i, hg))
    heads_out = pl.BlockSpec((HG, RP, HW), lambda i, hg: (hg, i, 0))

    def q_prep_kern(q_ref, t_ref, gn_ref, gr_ref, o_ref):
        for g in range(HG):
            blk = q_ref[:, g * HW:(g + 1) * HW].astype(F32)
            nh, _ = _rms(blk[:, :LANES], c.NOPE)
            rhat, _ = _rms(blk[:, LANES:], c.ROPE)
            rot = _rope(rhat * gr_ref[...], t_ref[0], t_ref[1], t_ref[2])
            o_ref[g] = (jnp.concatenate([nh * gn_ref[...], rot], axis=1) * (c.scale * LOG2E)).astype(o_ref.dtype)

    q_cat = pl.pallas_call(
        q_prep_kern, name="q_prep", grid=(nrp, nhg),
        in_specs=[heads_in, tabs_of(RP), fullb((1, LANES)), fullb((1, LANES))],
        out_specs=heads_out, out_shape=jax.ShapeDtypeStruct((H, N, HW), _BF),
        compiler_params=_cparams(("parallel", "parallel")),
    )(q_p, tabs, g_qn, g_qr)

    def k_prep_kern(kv_ref, kr_ref, gn_ref, o_ref):
        for g in range(HG):
            kn, _ = _rms(kv_ref[:, g * HW:g * HW + LANES].astype(F32), c.NOPE)
            o_ref[g] = jnp.concatenate([(kn * gn_ref[...]).astype(o_ref.dtype), kr_ref[...]], axis=1)

    k_cat = pl.pallas_call(
        k_prep_kern, name="k_prep", grid=(nrp, nhg),
        in_specs=[heads_in, pl.BlockSpec((RP, LANES), lambda i, hg: (i, 0)), fullb((1, LANES))],
        out_specs=heads_out, out_shape=jax.ShapeDtypeStruct((H, N, HW), _BF),
        compiler_params=_cparams(("parallel", "parallel")),
    )(kv, k_rope, g_kn)

    QB = ATT_QBLOCKS if N % (ATT_QBLOCKS * B) == 0 else 1
    BQ = QB * B
    nq = N // BQ
    assert CHUNK & (CHUNK - 1) == 0 and B % CHUNK == 0 and ATT_UNROLL % QB == 0

    def diag_mask(s, d):
        row = lax.broadcasted_iota(jnp.int32, s.shape, 0)
        col = lax.broadcasted_iota(jnp.int32, s.shape, 1) + d * B
        shift = CHUNK.bit_length() - 1
        allowed = jnp.right_shift(col, shift) <= jnp.right_shift(row, shift)
        return jnp.where(allowed, s, NEG)

    k_head = pl.BlockSpec((1, N, HW), lambda hh, i: (hh, 0, 0))
    v_head = pl.BlockSpec((N, LANES), lambda hh, i: (0, 2 * hh + 1))
    q_blk = pl.BlockSpec((1, BQ, HW), lambda hh, i: (hh, i, 0))
    o_blk = pl.BlockSpec((BQ, LANES), lambda hh, i: (i, hh))
    lse_blk = pl.BlockSpec((1, BQ, LANES), lambda hh, i: (hh, i, 0))

    def key_block_plan(i, run):
        def unrolled(u, carry):
            run(ATT_UNROLL * u, ATT_UNROLL, 0)
            return carry

        n_full = QB * i
        lax.fori_loop(0, n_full // ATT_UNROLL, unrolled, 0)
        for rem in range(0, ATT_UNROLL, QB):
            @pl.when(n_full % ATT_UNROLL == rem)
            def _(rem=rem):
                run(n_full - rem, rem + QB, QB)

    def attn_fwd_kern(q_ref, k_ref, v_ref, o_ref, lse_ref, m_sc, acc_sc, s_sc):
        i = pl.program_id(1)
        m_sc[...] = jnp.full(m_sc.shape, NEG, F32)
        acc_sc[...] = jnp.zeros(acc_sc.shape, F32)

        def rows_of(t):
            return pl.ds(pl.multiple_of(t * B, B), B)

        def scores(t, slot):
            s_sc[slot] = _dot_nt(q_ref[0], k_ref[0, rows_of(t), :])

        def softmax_pv(t, slot, diag):
            s = s_sc[slot]
            if diag is not None:
                s = diag_mask(s, diag)
            mt = s[:, 0:LANES]
            for cb in range(1, B // LANES):
                mt = jnp.maximum(mt, s[:, cb * LANES:(cb + 1) * LANES])
            m_prev = m_sc[...]
            m_new = jnp.maximum(m_prev, jnp.max(mt, axis=1, keepdims=True))
            alpha = jnp.exp2(m_prev - m_new)
            p = jnp.concatenate([jnp.exp2(s[:, cb * LANES:(cb + 1) * LANES] - m_new).astype(_BF)
                                 for cb in range(B // LANES)], axis=1)
            v_ones = jnp.concatenate([v_ref[rows_of(t), :], jnp.ones((B, LANES), _BF)], axis=1)
            acc_sc[...] = jnp.concatenate([alpha, alpha], axis=1) * acc_sc[...] + _dot(p, v_ones)
            m_sc[...] = m_new

        scores(0, 0)

        def run(first, count, n_diag):
            for u in range(count):
                if u + 1 < count or n_diag == 0:
                    scores(first + u + 1, (u + 1) % 2)
                softmax_pv(first + u, u % 2, u - (count - n_diag) if u >= count - n_diag else None)

        key_block_plan(i, run)
        acc = acc_sc[...]
        o_ref[...] = (acc[:, :LANES] / acc[:, LANES:]).astype(o_ref.dtype)
        lse_ref[0] = m_sc[...] + jnp.log2(acc[:, LANES:])

    mla_y, lse = pl.pallas_call(
        attn_fwd_kern, name="attn_fwd", grid=(H, nq),
        in_specs=[q_blk, k_head, v_head], out_specs=[o_blk, lse_blk],
        out_shape=[jax.ShapeDtypeStruct((N, c.HV), _BF), jax.ShapeDtypeStruct((H, N, LANES), F32)],
        scratch_shapes=[pltpu.VMEM((BQ, LANES), F32), pltpu.VMEM((BQ, HW), F32), pltpu.VMEM((2, BQ, B), F32)],
        compiler_params=_cparams(("parallel", "arbitrary")),
    )(q_cat, k_cat, kv)

    HV = c.HV
    mz_blk = c.o_mz // HV

    def gate_fwd_kern(y_ref, z_ref, o_ref):
        z = z_ref[...].astype(F32)
        o_ref[...] = (y_ref[...].astype(F32) * (z * _sig(z))).astype(o_ref.dtype)

    a_mla = pl.pallas_call(
        gate_fwd_kern, name="gate_mla", grid=(nr2,),
        in_specs=[rowb2(HV, 0), rowb2(HV, mz_blk)], out_specs=rowb2(HV, 0),
        out_shape=jax.ShapeDtypeStruct((N, HV), _BF), compiler_params=_cparams(("parallel",)),
    )(mla_y, proj)
    o_mla = _mm(a_mla, wmo, name="mm_omla", out_dtype=_BF)

    M, MW, MH, MHD = c.M, c.MW, c.MH, c.MHD
    memn = pl.pallas_call(
        make_rms_kern(), name="rms_mem",
        grid=(1,), in_specs=[fullb((M, D)), fullb((1, D))], out_specs=fullb((M, D)),
        out_shape=jax.ShapeDtypeStruct((M, D), _BF), compiler_params=_cparams(("arbitrary",)),
    )(memx, mem_norm_g)
    kvm = _mm(memn, wmkv, name="mm_memkv", out_dtype=F32)

    def memk_fwd_kern(kv_ref, g_ref, k_ref, v_ref):
        for hh in range(MH):
            kh, _ = _rms(kv_ref[:, hh * MHD:(hh + 1) * MHD], MHD)
            k_ref[:, hh * MHD:(hh + 1) * MHD] = (kh * g_ref[...]).astype(k_ref.dtype)
        v_ref[...] = kv_ref[:, MW:].astype(v_ref.dtype)

    mem_k, mem_v = pl.pallas_call(
        memk_fwd_kern, name="memk_fwd", grid=(1,),
        in_specs=[fullb((M, 2 * MW)), fullb((1, MHD))], out_specs=[fullb((M, MW)), fullb((M, MW))],
        out_shape=[jax.ShapeDtypeStruct((M, MW), _BF)] * 2, compiler_params=_cparams(("arbitrary",)),
    )(kvm, mem_kn_g)

    mem_blk = c.o_mem // (2 * MW)

    def mem_head(qz_ref, k_ref, v_ref, g_ref, hh):
        sl = slice(hh * MHD, (hh + 1) * MHD)
        qh, r = _rms(qz_ref[:, sl].astype(F32), MHD)
        qn = (qh * g_ref[...]).astype(_BF)
        s = _dot_nt(qn, k_ref[:, sl]) * c.mscale
        e = jnp.exp(s - jnp.max(s, axis=1, keepdims=True))
        p = e / jnp.sum(e, axis=1, keepdims=True)
        y = _dot(p.astype(_BF), v_ref[:, sl])
        z = qz_ref[:, MW + hh * MHD:MW + (hh + 1) * MHD].astype(F32)
        return sl, qh, r, qn, p, y, z

    def mem_fwd_kern(qz_ref, k_ref, v_ref, g_ref, o_ref):
        for hh in range(MH):
            sl, _, _, _, _, y, z = mem_head(qz_ref, k_ref, v_ref, g_ref, hh)
            o_ref[:, sl] = (y * (z * _sig(z))).astype(o_ref.dtype)

    a_mem = pl.pallas_call(
        mem_fwd_kern, name="mem_fwd", grid=(nr,),
        in_specs=[rowb(2 * MW, mem_blk), fullb((M, MW)), fullb((M, MW)), fullb((1, MHD))],
        out_specs=rowb(MW, 0), out_shape=jax.ShapeDtypeStruct((N, MW), _BF),
        compiler_params=_cparams(("parallel",)),
    )(proj, mem_k, mem_v, mem_qn_g)
    o_mem = _mm(a_mem, wmemo, name="mm_omem", out_dtype=_BF)

    g_blk = c.o_g // (3 * D)

    def merge_fwd_kern(g_ref, oc_ref, om_ref, ome_ref, o_ref):
        g = g_ref[...].astype(F32)
        acc = _sig(g[:, :D]) * oc_ref[...].astype(F32)
        acc += _sig(g[:, D:2 * D]) * om_ref[...].astype(F32)
        acc += _sig(g[:, 2 * D:]) * ome_ref[...].astype(F32)
        o_ref[...] = acc.astype(o_ref.dtype)

    merged = pl.pallas_call(
        merge_fwd_kern, name="merge_fwd", grid=(nr,),
        in_specs=[rowb(3 * D, g_blk), rowb(D, 0), rowb(D, 0), rowb(D, 0)], out_specs=rowb(D, 0),
        out_shape=jax.ShapeDtypeStruct((N, D), _BF), compiler_params=_cparams(("parallel",)),
    )(proj, o_conv, o_mla, o_mem)
    y2 = _mm(merged, wo, name="mm_out", out_dtype=F32)

    def loss_kern(x_ref, y_ref, t_ref, dy_ref, dyb_ref, l_ref):
        e = x_ref[...] + y_ref[...] - t_ref[...]
        dy = e * (1.0 / D)
        dy_ref[...] = dy
        dyb_ref[...] = dy.astype(dyb_ref.dtype)

        @pl.when(pl.program_id(0) == 0)
        def _():
            l_ref[...] = jnp.zeros(l_ref.shape, F32)

        l_ref[...] += jnp.sum(e * e, axis=0, keepdims=True)

    dy, dyb, lpart = pl.pallas_call(
        loss_kern, name="loss", grid=(nr2,),
        in_specs=[rowb2(D, 0)] * 3, out_specs=[rowb2(D, 0), rowb2(D, 0), fullb((1, D))],
        out_shape=[jax.ShapeDtypeStruct((N, D), F32), jax.ShapeDtypeStruct((N, D), _BF),
                   jax.ShapeDtypeStruct((1, D), F32)],
        compiler_params=_cparams(("arbitrary",)),
    )(xs, y2, tgt)
    loss = lax.psum(jnp.sum(lpart) * (0.5 / D), AXES)

    G = {}
    d_merged = _mm(dyb, wo, tb=True, name="mm_dmerged", out_dtype=_BF)
    G['w_o'] = _mm(merged, dyb, ta=True, name="mm_dwo", out_dtype=_BF)

    dproj0 = lax.empty((N, c.P), _BF)
    any_spec = pl.BlockSpec(memory_space=pl.ANY)

    def merge_bwd_kern(dp_any, g_ref, dm_ref, oc_ref, om_ref, ome_ref, dg_ref, doc_ref, dom_ref, dome_ref):
        g = g_ref[...].astype(F32)
        dm = dm_ref[...].astype(F32)
        for idx, (o_in, d_out) in enumerate(((oc_ref, doc_ref), (om_ref, dom_ref), (ome_ref, dome_ref))):
            sg = _sig(g[:, idx * D:(idx + 1) * D])
            d_out[...] = (sg * dm).astype(d_out.dtype)
            dg_ref[:, idx * D:(idx + 1) * D] = (dm * o_in[...].astype(F32) * sg * (1.0 - sg)).astype(dg_ref.dtype)

    dproj1, d_oconv, d_omla, d_omem = pl.pallas_call(
        merge_bwd_kern, name="merge_bwd", grid=(nr,),
        in_specs=[any_spec, rowb(3 * D, g_blk), rowb(D, 0), rowb(D, 0), rowb(D, 0), rowb(D, 0)],
        out_specs=[rowb(3 * D, g_blk), rowb(D, 0), rowb(D, 0), rowb(D, 0)],
        out_shape=[jax.ShapeDtypeStruct((N, c.P), _BF)] + [jax.ShapeDtypeStruct((N, D), _BF)] * 3,
        input_output_aliases={0: 0}, compiler_params=_cparams(("parallel",)),
    )(dproj0, proj, d_merged, o_conv, o_mla, o_mem)

    G['w_conv_out'] = _mm(a_conv, d_oconv, ta=True, name="mm_dwco", out_dtype=_BF)
    d_aconv = _mm(d_oconv, wco, tb=True, name="mm_daconv", out_dtype=_BF)
    G['w_mla_out'] = _mm(a_mla, d_omla, ta=True, name="mm_dwmo", out_dtype=_BF)
    d_amla = _mm(d_omla, wmo, tb=True, name="mm_damla", out_dtype=_BF)
    G['w_mem_out'] = _mm(a_mem, d_omem, ta=True, name="mm_dwmemo", out_dtype=_BF)
    d_amem = _mm(d_omem, wmemo, tb=True, name="mm_damem", out_dtype=_BF)

    def conv_bwd_kern(dp_any, p_ref, prev_ref, next_ref, da_ref, dan_ref, w_ref, o_ref, dw_ref):
        i = pl.program_id(0)
        cg, bg, u, z = conv_parts(p_ref[...])
        pc, _, pu, _ = conv_parts(prev_ref[...])
        _, nbg, _, nz = conv_parts(next_ref[...])
        cu = cg * u
        sh1, sh2 = shifted(cu, pc * pu, i)
        w = w_ref[...]
        conv = w[0:1, :] * sh2 + w[1:2, :] * sh1 + w[2:3, :] * cu
        sg = _sig(z)
        sz = z * sg
        da = da_ref[...].astype(F32)
        dcy = da * sz
        d_z = da * (bg * conv) * (sg * (1.0 + z * (1.0 - sg)))
        d_b = dcy * conv
        dconv = dcy * bg
        dnext = dan_ref[...].astype(F32) * (nz * _sig(nz)) * nbg
        dnext = jnp.where(i < nr - 1, dnext, 0.0)
        rid = lax.broadcasted_iota(jnp.int32, cu.shape, 0)
        up1 = jnp.where(rid == R - 1, dnext[0:1, :], pltpu.roll(dconv, R - 1, 0))
        up2 = jnp.where(rid == R - 2, dnext[0:1, :], jnp.where(rid == R - 1, dnext[1:2, :], pltpu.roll(dconv, R - 2, 0)))
        dcu = w[2:3, :] * dconv + w[1:2, :] * up1 + w[0:1, :] * up2
        o_ref[:, 0:CW] = (dcu * u).astype(o_ref.dtype)
        o_ref[:, CW:2 * CW] = d_b.astype(o_ref.dtype)
        o_ref[:, 2 * CW:3 * CW] = (dcu * cg).astype(o_ref.dtype)
        o_ref[:, 3 * CW:4 * CW] = d_z.astype(o_ref.dtype)

        @pl.when(i == 0)
        def _():
            dw_ref[...] = jnp.zeros(dw_ref.shape, F32)

        dw_ref[0:1, :] += jnp.sum(dconv * sh2, axis=0, keepdims=True)
        dw_ref[1:2, :] += jnp.sum(dconv * sh1, axis=0, keepdims=True)
        dw_ref[2:3, :] += jnp.sum(dconv * cu, axis=0, keepdims=True)

    next_spec = pl.BlockSpec((HALO, 4 * CW), lambda i: (jnp.minimum((i + 1) * rh, N // HALO - 1), conv_blk))
    dan_spec = pl.BlockSpec((HALO, CW), lambda i: (jnp.minimum((i + 1) * rh, N // HALO - 1), 0))
    dproj2, g_convw = pl.pallas_call(
        conv_bwd_kern, name="conv_bwd", grid=(nr,),
        in_specs=[any_spec, rowb(4 * CW, conv_blk), prev_spec, next_spec, rowb(CW, 0), dan_spec, fullb((8, CW))],
        out_specs=[rowb(4 * CW, conv_blk), fullb((8, CW))],
        out_shape=[jax.ShapeDtypeStruct((N, c.P), _BF), jax.ShapeDtypeStruct((8, CW), F32)],
        input_output_aliases={0: 0}, compiler_params=_cparams(("arbitrary",)),
    )(dproj1, proj, proj, proj, d_aconv, d_aconv, convw8)

    def mem_bwd_kern(dp_any, qz_ref, da_ref, k_ref, v_ref, g_ref, o_ref, dk_ref, dv_ref, dg_ref):
        @pl.when(pl.program_id(0) == 0)
        def _():
            dk_ref[...] = jnp.zeros(dk_ref.shape, F32)
            dv_ref[...] = jnp.zeros(dv_ref.shape, F32)
            dg_ref[...] = jnp.zeros(dg_ref.shape, F32)

        for hh in range(MH):
            sl, qh, r, qn, p, y, z = mem_head(qz_ref, k_ref, v_ref, g_ref, hh)
            da = da_ref[:, sl].astype(F32)
            sg = _sig(z)
            dyh = da * (z * sg)
            o_ref[:, MW + hh * MHD:MW + (hh + 1) * MHD] = (da * y * (sg * (1.0 + z * (1.0 - sg)))).astype(o_ref.dtype)
            dyb_h = dyh.astype(_BF)
            dpm = _dot_nt(dyb_h, v_ref[:, sl])
            ds = (p * (dpm - jnp.sum(dpm * p, axis=1, keepdims=True)) * c.mscale).astype(_BF)
            dqn = _dot(ds, k_ref[:, sl])
            dk_ref[:, sl] += _dot_tn(ds, qn)
            dv_ref[:, sl] += _dot_tn(p.astype(_BF), dyb_h)
            dq, dgp = _rms_bwd(qh, r, g_ref[...], dqn, MHD)
            o_ref[:, sl] = dq.astype(o_ref.dtype)
            dg_ref[...] += jnp.sum(dgp, axis=0, keepdims=True)

    dproj3, d_memk, d_memv, g_mem_qn = pl.pallas_call(
        mem_bwd_kern, name="mem_bwd", grid=(nr,),
        in_specs=[any_spec, rowb(2 * MW, mem_blk), rowb(MW, 0), fullb((M, MW)), fullb((M, MW)), fullb((1, MHD))],
        out_specs=[rowb(2 * MW, mem_blk), fullb((M, MW)), fullb((M, MW)), fullb((1, MHD))],
        out_shape=[jax.ShapeDtypeStruct((N, c.P), _BF), jax.ShapeDtypeStruct((M, MW), F32),
                   jax.ShapeDtypeStruct((M, MW), F32), jax.ShapeDtypeStruct((1, MHD), F32)],
        input_output_aliases={0: 0}, compiler_params=_cparams(("arbitrary",)),
    )(dproj2, proj, d_amem, mem_k, mem_v, mem_qn_g)

    def memk_bwd_kern(kv_ref, dk_ref, dv_ref, g_ref, o_ref, dg_ref):
        dg = jnp.zeros((1, MHD), F32)
        for hh in range(MH):
            sl = slice(hh * MHD, (hh + 1) * MHD)
            kh, r = _rms(kv_ref[:, sl], MHD)
            dkr, dgp = _rms_bwd(kh, r, g_ref[...], dk_ref[:, sl], MHD)
            o_ref[:, sl] = dkr.astype(o_ref.dtype)
            dg += jnp.sum(dgp, axis=0, keepdims=True)
        o_ref[:, MW:] = dv_ref[...].astype(o_ref.dtype)
        dg_ref[...] = dg

    d_kvm, g_mem_kn = pl.pallas_call(
        memk_bwd_kern, name="memk_bwd", grid=(1,),
        in_specs=[fullb((M, 2 * MW)), fullb((M, MW)), fullb((M, MW)), fullb((1, MHD))],
        out_specs=[fullb((M, 2 * MW)), fullb((1, MHD))],
        out_shape=[jax.ShapeDtypeStruct((M, 2 * MW), _BF), jax.ShapeDtypeStruct((1, MHD), F32)],
        compiler_params=_cparams(("arbitrary",)),
    )(kvm, d_memk, d_memv, mem_kn_g)
    G['w_mem_kv'] = _mm(memn, d_kvm, ta=True, name="mm_dwmkv", out_dtype=_BF)
    d_memn = _mm(d_kvm, wmkv, tb=True, name="mm_dmemn", out_dtype=F32)

    def memnorm_bwd_kern(x_ref, d_ref, dg_ref):
        xh, _ = _rms(x_ref[...], D)
        dg_ref[...] = jnp.sum(d_ref[...] * xh, axis=0, keepdims=True)

    g_mem_norm = pl.pallas_call(
        memnorm_bwd_kern, name="memnorm_bwd", grid=(1,),
        in_specs=[fullb((M, D)), fullb((M, D))], out_specs=fullb((1, D)),
        out_shape=jax.ShapeDtypeStruct((1, D), F32), compiler_params=_cparams(("arbitrary",)),
    )(memx, d_memn)

    def gate_bwd_kern(dp_any, da_ref, y_ref, z_ref, dy_ref, dz_ref):
        z = z_ref[...].astype(F32)
        da = da_ref[...].astype(F32)
        sg = _sig(z)
        dy_ref[...] = (da * (z * sg)).astype(dy_ref.dtype)
        dz_ref[...] = (da * y_ref[...].astype(F32) * (sg * (1.0 + z * (1.0 - sg)))).astype(dz_ref.dtype)

    d_mlay, dproj4 = pl.pallas_call(
        gate_bwd_kern, name="gate_mla_bwd", grid=(nr2,),
        in_specs=[any_spec, rowb2(HV, 0), rowb2(HV, 0), rowb2(HV, mz_blk)],
        out_specs=[rowb2(HV, 0), rowb2(HV, mz_blk)],
        out_shape=[jax.ShapeDtypeStruct((N, HV), _BF), jax.ShapeDtypeStruct((N, c.P), _BF)],
        input_output_aliases={0: 1}, compiler_params=_cparams(("parallel",)),
    )(dproj3, d_amla, mla_y, proj)

    def attn_bwd_kern(q_ref, k_ref, v_ref, o_ref, do_ref, lse_ref, dq_ref, dk_ref, dv_ref, dq_sc, dl_sc, dk_sc, dv_sc):
        i = pl.program_id(1)
        q, do = q_ref[0], do_ref[...]
        delta = jnp.sum(do.astype(F32) * o_ref[...].astype(F32), axis=1, keepdims=True)
        dl_sc[...] = jnp.broadcast_to(delta, dl_sc.shape)
        dq_sc[...] = jnp.zeros(dq_sc.shape, F32)

        def step(t, diag):
            rows = pl.ds(pl.multiple_of(t * B, B), B)
            k = k_ref[0, rows, :]
            s = _dot_nt(q, k)
            if diag is not None:
                s = diag_mask(s, diag)
            dpm = _dot_nt(do, v_ref[rows, :])
            lse_t, dl = lse_ref[0], dl_sc[...]
            ps, dss = [], []
            for cb in range(B // LANES):
                cols = slice(cb * LANES, (cb + 1) * LANES)
                p_cb = jnp.exp2(s[:, cols] - lse_t)
                ps.append(p_cb.astype(_BF))
                dss.append((p_cb * (dpm[:, cols] - dl)).astype(_BF))
            p, ds = jnp.concatenate(ps, axis=1), jnp.concatenate(dss, axis=1)
            dvp = _dot_tn(p, do)
            dkp = _dot_tn(ds, q)
            if diag is not None:
                dk_sc[rows, :] = dkp
                dv_sc[rows, :] = dvp
            else:
                dk_sc[rows, :] += dkp
                dv_sc[rows, :] += dvp
            dq_sc[...] += _dot(ds, k)

        def run(first, count, n_diag):
            for u in range(count):
                step(first + u, u - (count - n_diag) if u >= count - n_diag else None)

        key_block_plan(i, run)
        dq_ref[0] = dq_sc[...].astype(dq_ref.dtype)

        @pl.when(i == nq - 1)
        def _():
            dk_ref[0] = dk_sc[...].astype(dk_ref.dtype)
            dv_ref[0] = dv_sc[...].astype(dv_ref.dtype)

    d_qcat, d_kcat, d_v = pl.pallas_call(
        attn_bwd_kern, name="attn_bwd", grid=(H, nq),
        in_specs=[q_blk, k_head, v_head, o_blk, o_blk, lse_blk],
        out_specs=[pl.BlockSpec((1, BQ, HW), lambda hh, i: (hh, i, 0)),
                   pl.BlockSpec((1, N, HW), lambda hh, i: (hh, 0, 0)),
                   pl.BlockSpec((1, N, LANES), lambda hh, i: (hh, 0, 0))],
        out_shape=[jax.ShapeDtypeStruct((H, N, HW), _BF), jax.ShapeDtypeStruct((H, N, HW), _BF),
                   jax.ShapeDtypeStruct((H, N, LANES), _BF)],
        scratch_shapes=[pltpu.VMEM((BQ, HW), F32), pltpu.VMEM((BQ, LANES), F32), pltpu.VMEM((N, HW), F32),
                        pltpu.VMEM((N, LANES), F32)],
        compiler_params=_cparams(("parallel", "arbitrary")),
    )(q_cat, k_cat, kv, mla_y, d_mlay, lse)

    def q_prep_bwd_kern(q_ref, dq_ref, t_ref, gn_ref, gr_ref, o_ref, dgn_ref, dgr_ref):
        @pl.when(jnp.logical_and(pl.program_id(0) == 0, pl.program_id(1) == 0))
        def _():
            dgn_ref[...] = jnp.zeros(dgn_ref.shape, F32)
            dgr_ref[...] = jnp.zeros(dgr_ref.shape, F32)

        for g in range(HG):
            blk = q_ref[:, g * HW:(g + 1) * HW].astype(F32)
            d = dq_ref[g].astype(F32) * c.scale
            nh, rn = _rms(blk[:, :LANES], c.NOPE)
            rhat, rr = _rms(blk[:, LANES:], c.ROPE)
            dn, dgn = _rms_bwd(nh, rn, gn_ref[...], d[:, :LANES], c.NOPE)
            drot = _rope_t(d[:, LANES:], t_ref[0], t_ref[1], t_ref[2])
            dr, dgr = _rms_bwd(rhat, rr, gr_ref[...], drot, c.ROPE)
            o_ref[:, g * HW:(g + 1) * HW] = jnp.concatenate([dn, dr], axis=1).astype(o_ref.dtype)
            dgn_ref[...] += jnp.sum(dgn, axis=0, keepdims=True)
            dgr_ref[...] += jnp.sum(dgr, axis=0, keepdims=True)

    d_qp, g_qn_nope, g_qn_rope = pl.pallas_call(
        q_prep_bwd_kern, name="q_prep_bwd", grid=(nrp, nhg),
        in_specs=[heads_in, heads_out, tabs_of(RP), fullb((1, LANES)), fullb((1, LANES))],
        out_specs=[heads_in, fullb((1, LANES)), fullb((1, LANES))],
        out_shape=[jax.ShapeDtypeStruct((N, H * HW), _BF), jax.ShapeDtypeStruct((1, LANES), F32),
                   jax.ShapeDtypeStruct((1, LANES), F32)],
        compiler_params=_cparams(("arbitrary", "arbitrary")),
    )(q_p, d_qcat, tabs, g_qn, g_qr)

    def k_prep_bwd_kern(kv_ref, dk_ref, dv_ref, gn_ref, o_ref, dkr_ref, dgn_ref):
        hg = pl.program_id(1)

        @pl.when(jnp.logical_and(pl.program_id(0) == 0, hg == 0))
        def _():
            dgn_ref[...] = jnp.zeros(dgn_ref.shape, F32)

        @pl.when(hg == 0)
        def _():
            dkr_ref[...] = jnp.zeros(dkr_ref.shape, F32)

        dkr = jnp.zeros((RP, LANES), F32)
        for g in range(HG):
            dk = dk_ref[g].astype(F32) * (1.0 / LOG2E)
            kn, r = _rms(kv_ref[:, g * HW:g * HW + LANES].astype(F32), c.NOPE)
            dkn, dgn = _rms_bwd(kn, r, gn_ref[...], dk[:, :LANES], c.NOPE)
            o_ref[:, g * HW:(g + 1) * HW] = jnp.concatenate([dkn.astype(o_ref.dtype), dv_ref[g]], axis=1)
            dgn_ref[...] += jnp.sum(dgn, axis=0, keepdims=True)
            dkr += dk[:, LANES:]
        dkr_ref[...] += dkr

    d_kv, d_krsum, g_kn_nope = pl.pallas_call(
        k_prep_bwd_kern, name="k_prep_bwd", grid=(nrp, nhg),
        in_specs=[heads_in, heads_out, pl.BlockSpec((HG, RP, LANES), lambda i, hg: (hg, i, 0)), fullb((1, LANES))],
        out_specs=[heads_in, pl.BlockSpec((RP, LANES), lambda i, hg: (i, 0)), fullb((1, LANES))],
        out_shape=[jax.ShapeDtypeStruct((N, H * HW), _BF), jax.ShapeDtypeStruct((N, LANES), F32),
                   jax.ShapeDtypeStruct((1, LANES), F32)],
        compiler_params=_cparams(("arbitrary", "arbitrary")),
    )(kv, d_kcat, d_v, g_kn)

    def krope_bwd_kern(p_ref, d_ref, t_ref, g_ref, o_ref, dg_ref):
        @pl.when(pl.program_id(0) == 0)
        def _():
            dg_ref[...] = jnp.zeros(dg_ref.shape, F32)

        xh, r = _rms(p_ref[...].astype(F32), c.ROPE)
        drot = _rope_t(d_ref[...], t_ref[0], t_ref[1], t_ref[2])
        dx, dg = _rms_bwd(xh, r, g_ref[...], drot, c.ROPE)
        o_ref[...] = dx.astype(o_ref.dtype)
        dg_ref[...] += jnp.sum(dg, axis=0, keepdims=True)

    d_kr, g_kn_rope = pl.pallas_call(
        krope_bwd_kern, name="krope_bwd", grid=(nr,),
        in_specs=[rowb(LANES, 0), rowb(LANES, 0), tabs_of(R), fullb((1, LANES))],
        out_specs=[rowb(LANES, 0), fullb((1, LANES))],
        out_shape=[jax.ShapeDtypeStruct((N, LANES), _BF), jax.ShapeDtypeStruct((1, LANES), F32)],
        compiler_params=_cparams(("arbitrary",)),
    )(kr_raw, d_krsum, tabs, g_kr)
    dproj5 = dproj4

    g_wuq_p = _mm(cqn, d_qp, ta=True, name="mm_dwuq", out_dtype=_BF, bk=4096)
    G['w_uq'] = g_wuq_p.reshape(QL, H, HW)[:, :, :c.NOPE + c.ROPE].reshape(QL, H * (c.NOPE + c.ROPE))
    d_cqn = _mm(d_qp, wuq_p, tb=True, name="mm_dcqn", out_dtype=F32, bk=4096)
    G['w_ukv'] = _mm(ckvn, d_kv, ta=True, name="mm_dwukv", out_dtype=_BF, bk=4096)
    d_ckvn = _mm(d_kv, wukv, tb=True, name="mm_dckvn", out_dtype=F32, bk=4096)

    def to_blocks(n, g):
        if n in COL_SHARDED:
            return jnp.transpose(g.reshape(g.shape[0], NDEV, -1), (1, 0, 2))
        return g.reshape(NDEV, -1, g.shape[1])

    def landing(b):
        own = lax.dynamic_index_in_dim(b, me, 0, keepdims=True)
        return lax.dynamic_update_index_in_dim(lax.empty(b.shape, b.dtype), own, me, 0)

    early = [n for n in BIG if n != 'w_in']
    blocks_e = [to_blocks(n, G[n]) for n in early]
    xe = _split_start(blocks_e, [landing(b) for b in blocks_e], False, "xchg_early_start")
    gq_after = mla_q_norm_g + xe[4][0:1, 0:1]

    def lora_bwd_kern(dp_any, p_ref, dq_ref, dkv_ref, gq_ref, gkv_ref, o_ref, dgq_ref, dgkv_ref):
        @pl.when(pl.program_id(0) == 0)
        def _():
            dgq_ref[...] = jnp.zeros(dgq_ref.shape, F32)
            dgkv_ref[...] = jnp.zeros(dgkv_ref.shape, F32)

        blk = p_ref[...].astype(F32)
        qh, rq = _rms(blk[:, :QL], QL)
        kh, rk = _rms(blk[:, QL:], KVL)
        dq, dgq = _rms_bwd(qh, rq, gq_ref[...], dq_ref[...], QL)
        dk, dgk = _rms_bwd(kh, rk, gkv_ref[...], dkv_ref[...], KVL)
        o_ref[:, :QL] = dq.astype(o_ref.dtype)
        o_ref[:, QL:] = dk.astype(o_ref.dtype)
        dgq_ref[...] += jnp.sum(dgq, axis=0, keepdims=True)
        dgkv_ref[...] += jnp.sum(dgk, axis=0, keepdims=True)

    dproj6, g_q_norm, g_kv_norm = pl.pallas_call(
        lora_bwd_kern, name="lora_bwd", grid=(nr,),
        in_specs=[any_spec, rowb(QL + KVL, lora_blk), rowb(QL, 0), rowb(KVL, 0), fullb((1, QL)), fullb((1, KVL))],
        out_specs=[rowb(QL + KVL, lora_blk), fullb((1, QL)), fullb((1, KVL))],
        out_shape=[jax.ShapeDtypeStruct((N, c.P), _BF), jax.ShapeDtypeStruct((1, QL), F32),
                   jax.ShapeDtypeStruct((1, KVL), F32)],
        input_output_aliases={0: 0}, compiler_params=_cparams(("arbitrary",)),
    )(dproj5, proj, d_cqn, d_ckvn, gq_after, mla_kv_norm_g)

    g_win_p = _mm(h, dproj6, ta=True, name="mm_dwin", out_dtype=_BF, bk=4096)
    g_wkr = _mm(h, d_kr, ta=True, name="mm_dwkr", out_dtype=_BF, bk=4096)
    blocks_w = [_win_blocks(g_win_p, g_wkr, c)]
    xw = _split_start(blocks_w, [landing(b) for b in blocks_w], False, "xchg_win_start")
    d_h = _mm(dproj6, win_p, tb=True, name="mm_dh", out_dtype=F32, bk=3072, after=xw[4], plus=(d_kr, w_krT))

    def final_bwd_kern(x_ref, g_ref, dh_ref, dy_ref, gx_ref, dg_ref):
        @pl.when(pl.program_id(0) == 0)
        def _():
            dg_ref[...] = jnp.zeros(dg_ref.shape, F32)

        xh, r = _rms(x_ref[...], D)
        dx, dg = _rms_bwd(xh, r, g_ref[...], dh_ref[...], D)
        gx_ref[...] = dy_ref[...] + dx
        dg_ref[...] += jnp.sum(dg, axis=0, keepdims=True)

    grad_x, g_norm = pl.pallas_call(
        final_bwd_kern, name="final_bwd", grid=(nr2,),
        in_specs=[rowb2(D, 0), fullb((1, D)), rowb2(D, 0), rowb2(D, 0)],
        out_specs=[rowb2(D, 0), fullb((1, D))],
        out_shape=[jax.ShapeDtypeStruct((N, D), F32), jax.ShapeDtypeStruct((1, D), F32)],
        compiler_params=_cparams(("arbitrary",)),
    )(xs, norm_g, d_h, dy)

    recv_e = _split_wait(xe, False, grad_x, "xchg_early_wait")
    recv_w = _split_wait(xw, False, grad_x, "xchg_win_wait")
    res = [{}, {}, {}, {}]
    for n, parts in zip(['w_in'] + early, list(recv_w) + list(recv_e)):
        outs = _adam(parts, W[n][0], Mo[n][0], Vo[n][0], "adam_" + n)
        for k in range(4):
            res[k][n] = outs[k][None]

    small_g = {'norm_g': g_norm, 'mla_q_norm_g': g_q_norm, 'mla_kv_norm_g': g_kv_norm,
               'mla_qn_nope_g': g_qn_nope, 'mla_qn_rope_g': g_qn_rope[:, :c.ROPE], 'mla_kn_nope_g': g_kn_nope,
               'mla_kn_rope_g': g_kn_rope[:, :c.ROPE], 'mem_norm_g': g_mem_norm, 'mem_qn_g': g_mem_qn,
               'mem_kn_g': g_mem_kn}
    small_part = _pack([small_g[n] for n in SMALL] + [g_convw[0:3, :]], 0)
    small_all = _all_gather([small_part], "ag_small_grads")[0]
    small_shapes = [W[n].shape for n in SMALL]
    pieces = _unpack(small_all, small_shapes + [(3, CW)])
    cw8 = CW // NDEV
    conv_mine = lax.dynamic_slice_in_dim(pieces[-1].reshape(NDEV, 3, NDEV, cw8), me, 1, axis=2)[:, :, 0, :]
    sm_parts = _pack(pieces[:-1] + [conv_mine], 1)
    sm_names = SMALL + ['conv_w']
    sm_shapes = small_shapes + [(3, cw8)]
    w_sm = _pack([W[n] for n in SMALL] + [conv_w[0]], 0)
    m_sm = _pack([Mo[n] for n in SMALL] + [m_conv_w[0]], 0)
    v_sm = _pack([Vo[n] for n in SMALL] + [v_conv_w[0]], 0)
    outs_sm = [_unpack(o, sm_shapes) for o in _adam(sm_parts, w_sm, m_sm, v_sm, "adam_small")]
    for k in range(4):
        for n, a in zip(sm_names, outs_sm[k]):
            res[k][n] = a[None] if n == 'conv_w' else a
    return (loss, grad_x[None], *[res[0][n] for n in WEIGHTS], *[res[1][n] for n in WEIGHTS],
            *[res[2][n] for n in WEIGHTS], *[res[3][n] for n in WEIGHTS])
```

```python
import math

import jax
import jax.numpy as jnp
from jax import lax
from jax.experimental import pallas as pl
from jax.experimental.pallas import tpu as pltpu

F32 = jnp.float32
_BF = jnp.bfloat16
EPS = 1e-6
CHUNK = 64
ROPE_THETA = 10000.0
ADAM_LR, ADAM_B1, ADAM_B2, ADAM_EPS, ADAM_WD, ADAM_STEP = 0.001, 0.9, 0.999, 1e-08, 0.01, 10
NDEV = 8
AXES = ("x", "y", "c")
MESH = pl.DeviceIdType.MESH
LANES = 128
NEG = -1e30
LOG2E = math.log2(math.e)
V7X_VMEM_LIMIT = 56 * 1024 * 1024
PACK_C = 1024
ATT_BLOCK = 512
ATT_UNROLL = 4
ATT_QBLOCKS = 2
ADAM_BLOCK_ELEMS = 256 * 1024

WEIGHTS = ['norm_g', 'w_in', 'conv_w', 'w_conv_out', 'mla_q_norm_g', 'w_uq', 'mla_kv_norm_g', 'w_ukv',
           'mla_qn_nope_g', 'mla_qn_rope_g', 'mla_kn_nope_g', 'mla_kn_rope_g', 'w_mla_out', 'mem_norm_g',
           'w_mem_kv', 'mem_qn_g', 'mem_kn_g', 'w_mem_out', 'w_o']
BIG = ['w_in', 'w_conv_out', 'w_uq', 'w_ukv', 'w_mla_out', 'w_mem_kv', 'w_mem_out', 'w_o']
COL_SHARDED = ('w_in', 'w_conv_out', 'w_uq', 'w_ukv', 'w_mem_out')
SMALL = ['norm_g', 'mla_q_norm_g', 'mla_kv_norm_g', 'mla_qn_nope_g', 'mla_qn_rope_g', 'mla_kn_nope_g',
         'mla_kn_rope_g', 'mem_norm_g', 'mem_qn_g', 'mem_kn_g']


def _tile(dim, target, align):
    if dim <= target:
        return dim
    t = target - target % align
    while t > 0:
        if dim % t == 0:
            return t
        t -= align
    raise ValueError(f"no tile for {dim} {target} {align}")


def _cparams(sem):
    return pltpu.CompilerParams(dimension_semantics=sem, vmem_limit_bytes=V7X_VMEM_LIMIT)


def _sig(x):
    return 1.0 / (1.0 + jnp.exp(-x))


def _rms(x, n):
    r = lax.rsqrt(jnp.sum(x * x, axis=-1, keepdims=True) * (1.0 / n) + EPS)
    return x * r, r


def _rms_bwd(xhat, r, g, dy, n):
    dxh = dy * g
    dx = r * (dxh - xhat * (jnp.sum(dxh * xhat, axis=-1, keepdims=True) * (1.0 / n)))
    return dx, dy * xhat


def _rowsum128(x):
    return jnp.dot(x.astype(_BF), jnp.ones((LANES, LANES), _BF), preferred_element_type=F32)


def _rms128(x, n):
    r = lax.rsqrt(_rowsum128(x * x) * (1.0 / n) + EPS)
    return x * r, r


def _rms128_bwd(xhat, r, g, dy, n):
    dxh = dy * g
    dx = r * (dxh - xhat * (_rowsum128(dxh * xhat) * (1.0 / n)))
    return dx, dy * xhat


def _rope(x, cosp, sina, sinb):
    return x * cosp + pltpu.roll(x, 96, 1) * sina + pltpu.roll(x, 32, 1) * sinb


def _rope_t(d, cosp, sina, sinb):
    return d * cosp + pltpu.roll(d * sina, 32, 1) + pltpu.roll(d * sinb, 96, 1)


def _dot_nt(a, b):
    return lax.dot_general(a, b, (((1,), (1,)), ((), ())), preferred_element_type=F32)


def _dot_tn(a, b):
    return lax.dot_general(a, b, (((0,), (0,)), ((), ())), preferred_element_type=F32)


def _dot(a, b):
    return jnp.dot(a, b, preferred_element_type=F32)


def _all_gather(shards, name):
    na = len(shards)
    nc = 9
    halves = [s.shape[0] // 2 if s.shape[0] % 32 == 0 else None for s in shards]

    def body(*refs):
        x_refs, out_refs = refs[:na], refs[na:2 * na]
        send_sems, recv_sems, local_sems = refs[2 * na:]
        x, y, c = lax.axis_index("x"), lax.axis_index("y"), lax.axis_index("c")
        me, sib = (x, y, c), (x, y, 1 - c)
        px, py, pd = (1 - x, y, c), (x, 1 - y, c), (1 - x, 1 - y, c)

        def other_core(p):
            return (p[0], p[1], 1 - p[2])

        def rows(a, blk, part=None):
            r = out_refs[a].at[4 * blk[0] + 2 * blk[1] + blk[2]]
            if part is None or halves[a] is None:
                return r
            return r.at[pl.ds(0, halves[a])] if part == 0 else r.at[pl.ds(halves[a], halves[a])]

        def copy(a, k, blk, to, part=None, src=None):
            dst = rows(a, blk, part)
            return pltpu.make_async_remote_copy(
                src_ref=dst if src is None else src, dst_ref=dst,
                send_sem=send_sems.at[nc * a + k], recv_sem=recv_sems.at[nc * a + k],
                device_id=to, device_id_type=MESH)

        mine = [pltpu.make_async_copy(x_refs[a], rows(a, me), local_sems.at[a]) for a in range(na)]
        for cp in mine:
            cp.start()
        started = []
        for a in range(na):
            started += [copy(a, 0, me, px, src=x_refs[a]), copy(a, 1, me, py, src=x_refs[a]),
                        copy(a, 2, me, sib, src=x_refs[a])]
        for cp in started:
            cp.start()

        def forward(cp):
            cp.start()
            started.append(cp)

        for a in range(na):
            copy(a, 0, px, me).wait_recv()
            forward(copy(a, 3, px, py, part=0))
            forward(copy(a, 4, px, sib))
        for a in range(na):
            copy(a, 1, py, me).wait_recv()
            if halves[a] is not None:
                forward(copy(a, 5, py, px, part=1))
            forward(copy(a, 6, py, sib))
        for a in range(na):
            copy(a, 3, pd, me, part=0).wait_recv()
            forward(copy(a, 7, pd, sib, part=0))
            if halves[a] is not None:
                copy(a, 5, pd, me, part=1).wait_recv()
                forward(copy(a, 8, pd, sib, part=1))
        for a in range(na):
            copy(a, 2, sib, me).wait_recv()
            copy(a, 4, other_core(px), me).wait_recv()
            copy(a, 6, other_core(py), me).wait_recv()
            copy(a, 7, other_core(pd), me, part=0).wait_recv()
            if halves[a] is not None:
                copy(a, 8, other_core(pd), me, part=1).wait_recv()
        for cp in started:
            cp.wait_send()
        for cp in mine:
            cp.wait()

    any_spec = pl.BlockSpec(memory_space=pl.ANY)
    return pl.pallas_call(
        body, name=name,
        out_shape=[jax.ShapeDtypeStruct((NDEV,) + s.shape, s.dtype) for s in shards],
        in_specs=[any_spec] * na, out_specs=[any_spec] * na,
        scratch_shapes=[pltpu.SemaphoreType.DMA((nc * na,)), pltpu.SemaphoreType.DMA((nc * na,)),
                        pltpu.SemaphoreType.DMA((na,))],
    )(*shards)


_HBM = pl.BlockSpec(memory_space=pltpu.HBM)
_SEM = pl.BlockSpec(memory_space=pltpu.SEMAPHORE)
_EFFECT = pltpu.SideEffectType.DATAFLOW_SIDE_EFFECTING


def _split_copy(a, k, src_refs, land_refs, send_sems, recv_sems, gather, receive_side):
    x, y, c = lax.axis_index("x"), lax.axis_index("y"), lax.axis_index("c")
    me = 4 * x + 2 * y + c
    tx, ty, tc = x ^ ((k + 1) >> 2 & 1), y ^ ((k + 1) >> 1 & 1), c ^ ((k + 1) & 1)
    peer = 4 * tx + 2 * ty + tc
    return pltpu.make_async_remote_copy(
        src_ref=src_refs[a] if gather else src_refs[a].at[peer],
        dst_ref=land_refs[a].at[peer if receive_side else me],
        send_sem=send_sems.at[7 * a + k], recv_sem=recv_sems.at[7 * a + k],
        device_id=(tx, ty, tc), device_id_type=MESH)


def _split_start(srcs, lands, gather, name, after=None):
    na = len(srcs)
    extra = [] if after is None else [after]

    def body(*refs):
        src_refs, land_refs = refs[:na], refs[na:2 * na]
        send_sems, recv_sems = refs[2 * na + len(extra)], refs[2 * na + len(extra) + 1]
        token = refs[-1]
        for k in range(7):
            for a in range(na):
                _split_copy(a, k, src_refs, land_refs, send_sems, recv_sems, gather, False).start()
        token[...] = jnp.zeros_like(token)

    hbm = [pltpu.HBM(b.shape, b.dtype) for b in list(srcs) + list(lands)]
    outs = pl.pallas_call(
        body, name=name,
        out_shape=(pltpu.SemaphoreType.DMA((7 * na,)), pltpu.SemaphoreType.DMA((7 * na,)), *hbm,
                   jax.ShapeDtypeStruct((8, LANES), F32)),
        in_specs=[_HBM] * (2 * na) + [pl.BlockSpec(memory_space=pl.ANY)] * len(extra),
        out_specs=(_SEM, _SEM, *[_HBM] * (2 * na), pl.BlockSpec(memory_space=pltpu.VMEM)),
        input_output_aliases={j: 2 + j for j in range(2 * na)},
        compiler_params=pltpu.CompilerParams(has_side_effects=_EFFECT),
    )(*[pltpu.with_memory_space_constraint(b, pltpu.HBM) for b in srcs],
      *[pltpu.with_memory_space_constraint(l, pltpu.HBM) for l in lands], *extra)
    return outs[0], outs[1], outs[2:2 + na], outs[2 + na:2 + 2 * na], outs[-1]


def _split_wait(started, gather, after, name):
    send_sems, recv_sems, srcs, lands, _ = started
    na = len(srcs)

    def body(*refs):
        src_refs, land_refs = refs[:na], refs[na:2 * na]
        send_s, recv_s = refs[2 * na], refs[2 * na + 1]
        for k in range(7):
            for a in range(na):
                cp = _split_copy(a, k, src_refs, land_refs, send_s, recv_s, gather, True)
                cp.wait_send()
                cp.wait_recv()

    hbm = [pltpu.HBM(b.shape, b.dtype) for b in list(srcs) + list(lands)]
    outs = pl.pallas_call(
        body, name=name, out_shape=tuple(hbm),
        in_specs=[_HBM] * (2 * na) + [_SEM, _SEM, pl.BlockSpec(memory_space=pl.ANY)],
        out_specs=tuple([_HBM] * (2 * na)),
        input_output_aliases={j: j for j in range(2 * na)},
        compiler_params=pltpu.CompilerParams(has_side_effects=_EFFECT),
    )(*srcs, *lands, send_sems, recv_sems, after)
    return outs[na:]


def _seg_rows(size):
    rows = -(-size // PACK_C)
    return -(-rows // 16) * 16


def _pack(arrs, lead):
    parts = []
    for a in arrs:
        lshape = a.shape[:lead]
        f = a.reshape(lshape + (-1,)).astype(F32)
        rows = _seg_rows(f.shape[-1])
        f = jnp.pad(f, [(0, 0)] * lead + [(0, rows * PACK_C - f.shape[-1])])
        parts.append(f.reshape(lshape + (rows, PACK_C)))
    return jnp.concatenate(parts, axis=lead)


def _unpack(buf, shapes):
    lshape = buf.shape[:-2]
    out, r = [], 0
    for shp in shapes:
        size = math.prod(shp)
        rows = _seg_rows(size)
        seg = buf[..., r:r + rows, :].reshape(lshape + (rows * PACK_C,))[..., :size]
        out.append(seg.reshape(lshape + tuple(shp)))
        r += rows
    return out


def _mm(a, b, *, name, out_dtype, ta=False, tb=False, bm=1024, bn=1024, bk=2048, after=None, plus=None):
    if ta:
        kdim, m = a.shape
    else:
        m, kdim = a.shape
    n, k2 = b.shape if tb else b.shape[::-1]
    assert kdim == k2 and not (ta and tb), (a.shape, b.shape)
    bm = _tile(m, bm, LANES if ta else 16)
    bn = _tile(n, bn, LANES)
    bk = _tile(kdim, bk, LANES)
    nk = kdim // bk
    n_after = 0 if after is None else 1
    n_plus = 0 if plus is None else 2

    def kern(a_ref, b_ref, *rest):
        plus_refs = rest[n_after:n_after + n_plus]
        o_ref, scratch = rest[n_after + n_plus], rest[n_after + n_plus + 1:]
        part = (_dot_tn if ta else _dot_nt if tb else _dot)(a_ref[...], b_ref[...])

        def first(p):
            return p + _dot(plus_refs[0][...], plus_refs[1][...]) if plus is not None else p

        if nk == 1:
            o_ref[...] = first(part).astype(o_ref.dtype)
        else:
            acc = scratch[0] if scratch else o_ref
            k = pl.program_id(2)

            @pl.when(k == 0)
            def _():
                acc[...] = first(jnp.zeros(acc.shape, F32))

            acc[...] += part
            if scratch:
                @pl.when(k == nk - 1)
                def _():
                    o_ref[...] = acc[...].astype(o_ref.dtype)

    a_spec = pl.BlockSpec((bk, bm), lambda i, j, k: (k, i)) if ta else pl.BlockSpec((bm, bk), lambda i, j, k: (i, k))
    b_spec = pl.BlockSpec((bn, bk), lambda i, j, k: (j, k)) if tb else pl.BlockSpec((bk, bn), lambda i, j, k: (k, j))
    extra_specs, extra_args = [], []
    if after is not None:
        extra_specs.append(pl.BlockSpec(after.shape, lambda i, j, k: (0, 0)))
        extra_args.append(after)
    if plus is not None:
        kk = plus[0].shape[1]
        extra_specs += [pl.BlockSpec((bm, kk), lambda i, j, k: (i, 0)), pl.BlockSpec((kk, bn), lambda i, j, k: (0, j))]
        extra_args += list(plus)
    return pl.pallas_call(
        kern, name=name, grid=(m // bm, n // bn, nk),
        in_specs=[a_spec, b_spec] + extra_specs,
        out_specs=pl.BlockSpec((bm, bn), lambda i, j, k: (i, j)),
        out_shape=jax.ShapeDtypeStruct((m, n), out_dtype),
        scratch_shapes=[pltpu.VMEM((bm, bn), F32)] if nk > 1 and out_dtype != F32 else [],
        compiler_params=_cparams(("parallel", "parallel", "arbitrary")),
    )(a, b, *extra_args)


def _adam(parts, w_a, m_a, v_a, name):
    rows, cols = w_a.shape
    rb = _tile(rows, max(8, ADAM_BLOCK_ELEMS // cols // 8 * 8), 8)
    bc1 = 1.0 - ADAM_B1 ** ADAM_STEP
    bc2 = 1.0 - ADAM_B2 ** ADAM_STEP

    def adam_kern(p_ref, w_ref, m_ref, v_ref, g_ref, d_ref, nm_ref, nv_ref):
        g = p_ref[0].astype(F32)
        for j in range(1, NDEV):
            g = g + p_ref[j].astype(F32)
        m_new = ADAM_B1 * m_ref[...] + (1.0 - ADAM_B1) * g
        v_new = ADAM_B2 * v_ref[...] + (1.0 - ADAM_B2) * (g * g)
        g_ref[...] = g
        nm_ref[...] = m_new
        nv_ref[...] = v_new
        d_ref[...] = -ADAM_LR * ((m_new / bc1) / (jnp.sqrt(v_new / bc2) + ADAM_EPS) + ADAM_WD * w_ref[...])

    blk = pl.BlockSpec((rb, cols), lambda i: (i, 0))
    return pl.pallas_call(
        adam_kern, name=name, grid=(rows // rb,),
        in_specs=[pl.BlockSpec((NDEV, rb, cols), lambda i: (0, i, 0)), blk, blk, blk],
        out_specs=[blk] * 4, out_shape=[jax.ShapeDtypeStruct((rows, cols), F32)] * 4,
        compiler_params=_cparams(("parallel",)),
    )(parts, w_a, m_a, v_a)


class _Cfg:
    pass


def _config(x, conv_w, w_uq, w_ukv, mla_qn_nope_g, mla_qn_rope_g, mem, mem_qn_g, w_mem_out, w_mla_out):
    c = _Cfg()
    c.N, c.D = x.shape[1], x.shape[2]
    c.CW = conv_w.shape[2] * NDEV
    c.QL, c.KVL = w_uq.shape[1], w_ukv.shape[1]
    c.NOPE, c.ROPE = mla_qn_nope_g.shape[1], mla_qn_rope_g.shape[1]
    c.H = w_uq.shape[2] * NDEV // (c.NOPE + c.ROPE)
    c.V = w_ukv.shape[2] * NDEV // c.H - c.NOPE
    assert c.NOPE == LANES and c.V == LANES and c.ROPE == LANES // 2
    c.HW = 2 * LANES
    c.HV = c.H * c.V
    assert w_mla_out.shape[1] * NDEV == c.HV
    c.M = mem.shape[1]
    c.MHD = mem_qn_g.shape[1]
    c.MW = w_mem_out.shape[1]
    c.MH = c.MW // c.MHD
    c.o_conv = 0
    c.o_mz = 4 * c.CW
    c.o_g = c.o_mz + c.HV
    c.o_mem = c.o_g + 3 * c.D
    c.o_lora = c.o_mem + 2 * c.MW
    c.P = c.o_lora + c.QL + c.KVL
    assert c.o_mz % c.HV == 0 and c.o_g % (3 * c.D) == 0 and c.o_mem % (2 * c.MW) == 0
    assert c.o_lora % (c.QL + c.KVL) == 0 and c.QL % LANES == 0 and c.KVL % LANES == 0
    c.IN = 4 * c.CW + c.QL + c.KVL + c.ROPE + c.HV + 2 * c.MW + 3 * c.D
    c.R = _tile(c.N, 256, 16)
    c.RP = _tile(c.N, 512, 16)
    c.HG = _tile(c.H, 4, 1)
    c.B = _tile(c.N, ATT_BLOCK, CHUNK)
    c.scale = float((c.NOPE + c.ROPE) ** -0.5)
    c.mscale = float(c.MHD ** -0.5)
    return c


def _win_segments(c):
    ref_order = (('conv', 4 * c.CW), ('lora', c.QL + c.KVL), ('kr', c.ROPE), ('mz', c.HV), ('mem', 2 * c.MW), ('g', 3 * c.D))
    mine = {'conv': c.o_conv, 'mz': c.o_mz, 'g': c.o_g, 'mem': c.o_mem, 'lora': c.o_lora, 'kr': 0}
    segs, o = [], 0
    for nm, wd in ref_order:
        segs.append((nm, o, wd, mine[nm]))
        o += wd
    return segs


def _win_split(g_win, c):
    n8 = g_win.shape[2]

    def columns(a, wd):
        return [g_win[j][:, max(a, j * n8) - j * n8:min(a + wd, (j + 1) * n8) - j * n8]
                for j in range(a // n8, (a + wd - 1) // n8 + 1)]

    segs = {nm: (a, wd) for nm, a, wd, _ in _win_segments(c)}
    main = [p for nm in ('conv', 'mz', 'g', 'mem', 'lora') for p in columns(*segs[nm])]
    kr = jnp.concatenate(columns(*segs['kr']) + [jnp.zeros((g_win.shape[1], LANES - c.ROPE), g_win.dtype)], axis=1)
    return jnp.concatenate(main, axis=1), kr


def _win_blocks(g, g_kr, c):
    n8 = c.IN // NDEV
    blocks = []
    for j in range(NDEV):
        lo, hi = j * n8, (j + 1) * n8
        parts = []
        for nm, a, wd, mine in _win_segments(c):
            s, e = max(a, lo), min(a + wd, hi)
            if s < e:
                parts.append((g_kr if nm == 'kr' else g)[:, mine + s - a:mine + e - a])
        blocks.append(jnp.concatenate(parts, axis=1))
    return jnp.stack(blocks, axis=0)


def kernel(x, positions, mem, norm_g, w_in, conv_w, w_conv_out, mla_q_norm_g, w_uq, mla_kv_norm_g, w_ukv, mla_qn_nope_g, mla_qn_rope_g, mla_kn_nope_g, mla_kn_rope_g, w_mla_out, mem_norm_g, w_mem_kv, mem_qn_g, mem_kn_g, w_mem_out, w_o, loss_target, m_norm_g, m_w_in, m_conv_w, m_w_conv_out, m_mla_q_norm_g, m_w_uq, m_mla_kv_norm_g, m_w_ukv, m_mla_qn_nope_g, m_mla_qn_rope_g, m_mla_kn_nope_g, m_mla_kn_rope_g, m_w_mla_out, m_mem_norm_g, m_w_mem_kv, m_mem_qn_g, m_mem_kn_g, m_w_mem_out, m_w_o, v_norm_g, v_w_in, v_conv_w, v_w_conv_out, v_mla_q_norm_g, v_w_uq, v_mla_kv_norm_g, v_w_ukv, v_mla_qn_nope_g, v_mla_qn_rope_g, v_mla_kn_nope_g, v_mla_kn_rope_g, v_w_mla_out, v_mem_norm_g, v_w_mem_kv, v_mem_qn_g, v_mem_kn_g, v_w_mem_out, v_w_o):
    args = dict(locals())
    W = {n: args[n] for n in WEIGHTS}
    Mo = {n: args['m_' + n] for n in WEIGHTS}
    Vo = {n: args['v_' + n] for n in WEIGHTS}
    c = _config(x, conv_w, w_uq, w_ukv, mla_qn_nope_g, mla_qn_rope_g, mem, mem_qn_g, w_mem_out, w_mla_out)
    N, D, R, B, H = c.N, c.D, c.R, c.B, c.H
    assert x.shape[0] == 1
    xs = x[0]
    tgt = loss_target[0]
    memx = mem[0]
    me = 4 * lax.axis_index("x") + 2 * lax.axis_index("y") + lax.axis_index("c")
    nr = N // R

    g_win, g_taps = _all_gather([W['w_in'][0].astype(_BF), conv_w[0]], "ag_w_in")
    rest = [n for n in BIG if n != 'w_in']
    shards_r = [W[n][0].astype(_BF) for n in rest]
    lands_r = [lax.dynamic_update_index_in_dim(lax.empty((NDEV,) + s.shape, s.dtype), s[None], me, 0)
               for s in shards_r]
    ag_rest = _split_start(shards_r, lands_r, True, "ag_rest_start", after=g_win)
    win_p, w_kr = _win_split(g_win, c)
    w_krT = w_kr.T
    convw = jnp.transpose(g_taps, (1, 0, 2)).reshape(3, c.CW)
    convw8 = jnp.pad(convw, ((0, 5), (0, 0)))

    def rowb(width, cidx, rows=R):
        return pl.BlockSpec((rows, width), lambda i, _c=cidx: (i, _c))

    R2 = c.RP
    nr2 = N // R2
    RM = _tile(N, 1024, 16)

    def rowb2(width, cidx):
        return rowb(width, cidx, R2)

    def fullb(shape):
        nd = len(shape)
        return pl.BlockSpec(shape, lambda *_: (0,) * nd)

    def pad_lanes(g, w=LANES):
        return jnp.pad(g, ((0, 0), (0, w - g.shape[1])))

    def tabs_of(rows):
        return pl.BlockSpec((3, rows, LANES), lambda i, *_: (0, i, 0))

    half = c.ROPE // 2
    inv_freq = jnp.power(ROPE_THETA, -jnp.arange(half, dtype=F32) / half)
    invf = jnp.concatenate([inv_freq, inv_freq, jnp.zeros((LANES - c.ROPE,), F32)])[None, :]
    pos_col = positions[0].astype(F32).reshape(N, 1)

    def rope_tab_kern(pos_ref, invf_ref, o_ref):
        ang = pos_ref[...] * invf_ref[...]
        co, si = jnp.cos(ang), jnp.sin(ang)
        lane = lax.broadcasted_iota(jnp.int32, ang.shape, 1)
        o_ref[0] = jnp.where(lane < c.ROPE, co, 0.0)
        o_ref[1] = jnp.where(lane < half, -si, 0.0)
        o_ref[2] = jnp.where(jnp.logical_and(lane >= half, lane < c.ROPE), si, 0.0)

    tabs = pl.pallas_call(
        rope_tab_kern, name="rope_tab", grid=(nr,),
        in_specs=[pl.BlockSpec((R, 1), lambda i: (i, 0)), fullb((1, LANES))],
        out_specs=tabs_of(R),
        out_shape=jax.ShapeDtypeStruct((3, N, LANES), F32),
        compiler_params=_cparams(("parallel",)),
    )(pos_col, invf)

    def make_rms_kern():
        def rms_fwd_kern(x_ref, g_ref, o_ref):
            xh, _ = _rms(x_ref[...].astype(F32), x_ref.shape[-1])
            o_ref[...] = (xh * g_ref[...]).astype(o_ref.dtype)
        return rms_fwd_kern

    h = pl.pallas_call(
        make_rms_kern(), name="rms_x", grid=(nr2,),
        in_specs=[rowb2(D, 0), fullb((1, D))], out_specs=rowb2(D, 0),
        out_shape=jax.ShapeDtypeStruct((N, D), _BF), compiler_params=_cparams(("parallel",)),
    )(xs, norm_g)

    proj = _mm(h, win_p, name="mm_proj", out_dtype=_BF, after=ag_rest[4])
    kr_raw = _mm(h, w_kr, name="mm_kr", out_dtype=_BF)

    Wf = {}
    for n, g in zip(rest, _split_wait(ag_rest, True, proj, "ag_rest_wait")):
        if n in COL_SHARDED:
            Wf[n] = jnp.transpose(g, (1, 0, 2)).reshape(g.shape[1], -1)
        else:
            Wf[n] = g.reshape(-1, g.shape[2])
    wuq = Wf['w_uq'].reshape(c.QL, H, c.NOPE + c.ROPE)
    wuq_p = jnp.pad(wuq, ((0, 0), (0, 0), (0, c.HW - c.NOPE - c.ROPE))).reshape(c.QL, H * c.HW)
    wukv = Wf['w_ukv']
    wco, wmo, wmkv, wmemo, wo = Wf['w_conv_out'], Wf['w_mla_out'], Wf['w_mem_kv'], Wf['w_mem_out'], Wf['w_o']

    CW = c.CW
    conv_blk = c.o_conv // (4 * CW)
    HALO = 16
    rh = R // HALO

    def conv_parts(blk):
        blk = blk.astype(F32)
        return blk[:, 0:CW], blk[:, CW:2 * CW], blk[:, 2 * CW:3 * CW], blk[:, 3 * CW:4 * CW]

    def shifted(cu, prev, i):
        prev = jnp.where(i > 0, prev, 0.0)
        rid = lax.broadcasted_iota(jnp.int32, cu.shape, 0)
        last, last2 = prev[HALO - 1:HALO, :], prev[HALO - 2:HALO - 1, :]
        sh1 = jnp.where(rid == 0, last, pltpu.roll(cu, 1, 0))
        sh2 = jnp.where(rid == 0, last2, jnp.where(rid == 1, last, pltpu.roll(cu, 2, 0)))
        return sh1, sh2

    def conv_fwd_kern(p_ref, prev_ref, w_ref, o_ref):
        i = pl.program_id(0)
        cg, bg, u, z = conv_parts(p_ref[...])
        pc, _, pu, _ = conv_parts(prev_ref[...])
        cu = cg * u
        sh1, sh2 = shifted(cu, pc * pu, i)
        w = w_ref[...]
        conv = w[0:1, :] * sh2 + w[1:2, :] * sh1 + w[2:3, :] * cu
        o_ref[...] = (bg * conv * (z * _sig(z))).astype(o_ref.dtype)

    prev_spec = pl.BlockSpec((HALO, 4 * CW), lambda i: (jnp.maximum(i * rh - 1, 0), conv_blk))
    a_conv = pl.pallas_call(
        conv_fwd_kern, name="conv_fwd", grid=(nr,),
        in_specs=[rowb(4 * CW, conv_blk), prev_spec, fullb((8, CW))],
        out_specs=rowb(CW, 0), out_shape=jax.ShapeDtypeStruct((N, CW), _BF),
        compiler_params=_cparams(("parallel",)),
    )(proj, proj, convw8)
    o_conv = _mm(a_conv, wco, name="mm_oconv", out_dtype=_BF)

    QL, KVL, HW = c.QL, c.KVL, c.HW
    lora_blk = c.o_lora // (QL + KVL)

    def lora_fwd_kern(p_ref, gq_ref, gkv_ref, q_ref, kv_ref):
        blk = p_ref[...].astype(F32)
        qh, _ = _rms(blk[:, :QL], QL)
        kh, _ = _rms(blk[:, QL:], KVL)
        q_ref[...] = (qh * gq_ref[...]).astype(q_ref.dtype)
        kv_ref[...] = (kh * gkv_ref[...]).astype(kv_ref.dtype)

    cqn, ckvn = pl.pallas_call(
        lora_fwd_kern, name="lora_fwd", grid=(nr,),
        in_specs=[rowb(QL + KVL, lora_blk), fullb((1, QL)), fullb((1, KVL))],
        out_specs=[rowb(QL, 0), rowb(KVL, 0)],
        out_shape=[jax.ShapeDtypeStruct((N, QL), _BF), jax.ShapeDtypeStruct((N, KVL), _BF)],
        compiler_params=_cparams(("parallel",)),
    )(proj, mla_q_norm_g, mla_kv_norm_g)
    q_p = _mm(cqn, wuq_p, name="mm_q", out_dtype=_BF, bn=2048)
    kv = _mm(ckvn, wukv, name="mm_kv", out_dtype=_BF, bn=2048)

    g_qn, g_qr = mla_qn_nope_g, pad_lanes(mla_qn_rope_g)
    g_kn, g_kr = mla_kn_nope_g, pad_lanes(mla_kn_rope_g)

    def krope_fwd_kern(p_ref, t_ref, g_ref, o_ref):
        xh, _ = _rms128(p_ref[...].astype(F32), c.ROPE)
        o_ref[...] = _rope(xh * g_ref[...], t_ref[0], t_ref[1], t_ref[2]).astype(o_ref.dtype)

    k_rope = pl.pallas_call(
        krope_fwd_kern, name="krope_fwd", grid=(nr,),
        in_specs=[rowb(LANES, 0), tabs_of(R), fullb((1, LANES))],
        out_specs=rowb(LANES, 0), out_shape=jax.ShapeDtypeStruct((N, LANES), _BF),
        compiler_params=_cparams(("parallel",)),
    )(kr_raw, tabs, g_kr)

    RP, HG = c.RP, c.HG
    nrp, nhg = N // RP, H // HG
    heads_in = pl.BlockSpec((RP, HG * HW), lambda i, hg: (i, hg))
    heads_out = pl.BlockSpec((HG, RP, HW), lambda i, hg: (hg, i, 0))

    def q_prep_kern(q_ref, t_ref, gn_ref, gr_ref, o_ref):
        for g in range(HG):
            blk = q_ref[:, g * HW:(g + 1) * HW].astype(F32)
            nh, _ = _rms128(blk[:, :LANES], c.NOPE)
            rhat, _ = _rms128(blk[:, LANES:], c.ROPE)
            rot = _rope(rhat * gr_ref[...], t_ref[0], t_ref[1], t_ref[2])
            o_ref[g] = (jnp.concatenate([nh * gn_ref[...], rot], axis=1) * (c.scale * LOG2E)).astype(o_ref.dtype)

    q_cat = pl.pallas_call(
        q_prep_kern, name="q_prep", grid=(nrp, nhg),
        in_specs=[heads_in, tabs_of(RP), fullb((1, LANES)), fullb((1, LANES))],
        out_specs=heads_out, out_shape=jax.ShapeDtypeStruct((H, N, HW), _BF),
        compiler_params=_cparams(("parallel", "parallel")),
    )(q_p, tabs, g_qn, g_qr)

    def k_prep_kern(kv_ref, kr_ref, gn_ref, o_ref):
        for g in range(HG):
            kn, _ = _rms128(kv_ref[:, g * HW:g * HW + LANES].astype(F32), c.NOPE)
            o_ref[g] = jnp.concatenate([(kn * gn_ref[...]).astype(o_ref.dtype), kr_ref[...]], axis=1)

    k_cat = pl.pallas_call(
        k_prep_kern, name="k_prep", grid=(nrp, nhg),
        in_specs=[heads_in, pl.BlockSpec((RP, LANES), lambda i, hg: (i, 0)), fullb((1, LANES))],
        out_specs=heads_out, out_shape=jax.ShapeDtypeStruct((H, N, HW), _BF),
        compiler_params=_cparams(("parallel", "parallel")),
    )(kv, k_rope, g_kn)

    QB = ATT_QBLOCKS if N % (ATT_QBLOCKS * B) == 0 else 1
    BQ = QB * B
    nq = N // BQ
    assert CHUNK & (CHUNK - 1) == 0 and B % CHUNK == 0 and ATT_UNROLL % QB == 0

    def diag_mask(s, d):
        row = lax.broadcasted_iota(jnp.int32, s.shape, 0)
        col = lax.broadcasted_iota(jnp.int32, s.shape, 1) + d * B
        shift = CHUNK.bit_length() - 1
        allowed = jnp.right_shift(col, shift) <= jnp.right_shift(row, shift)
        return jnp.where(allowed, s, NEG)

    k_head = pl.BlockSpec((1, N, HW), lambda hh, i: (hh, 0, 0))
    v_head = pl.BlockSpec((N, LANES), lambda hh, i: (0, 2 * hh + 1))
    q_blk = pl.BlockSpec((1, BQ, HW), lambda hh, i: (hh, i, 0))
    o_blk = pl.BlockSpec((BQ, LANES), lambda hh, i: (i, hh))
    lse_blk = pl.BlockSpec((1, BQ, LANES), lambda hh, i: (hh, i, 0))

    def key_block_plan(i, run):
        def unrolled(u, carry):
            run(ATT_UNROLL * u, ATT_UNROLL, 0)
            return carry

        n_full = QB * i
        lax.fori_loop(0, n_full // ATT_UNROLL, unrolled, 0)
        for rem in range(0, ATT_UNROLL, QB):
            @pl.when(n_full % ATT_UNROLL == rem)
            def _(rem=rem):
                run(n_full - rem, rem + QB, QB)

    def attn_fwd_kern(q_ref, k_ref, v_ref, o_ref, lse_ref, m_sc, acc_sc, s_sc):
        i = pl.program_id(1)
        m_sc[...] = jnp.full(m_sc.shape, NEG, F32)
        acc_sc[...] = jnp.zeros(acc_sc.shape, F32)

        def rows_of(t):
            return pl.ds(pl.multiple_of(t * B, B), B)

        def scores(t, slot):
            s_sc[slot] = _dot_nt(q_ref[0], k_ref[0, rows_of(t), :])

        def softmax_pv(t, slot, diag):
            s = s_sc[slot]
            if diag is not None:
                s = diag_mask(s, diag)
            mt = s[:, 0:LANES]
            for cb in range(1, B // LANES):
                mt = jnp.maximum(mt, s[:, cb * LANES:(cb + 1) * LANES])
            m_prev = m_sc[...]
            m_new = jnp.maximum(m_prev, jnp.max(mt, axis=1, keepdims=True))
            alpha = jnp.exp2(m_prev - m_new)
            p = jnp.concatenate([jnp.exp2(s[:, cb * LANES:(cb + 1) * LANES] - m_new).astype(_BF)
                                 for cb in range(B // LANES)], axis=1)
            v_ones = jnp.concatenate([v_ref[rows_of(t), :], jnp.ones((B, LANES), _BF)], axis=1)
            acc_sc[...] = jnp.concatenate([alpha, alpha], axis=1) * acc_sc[...] + _dot(p, v_ones)
            m_sc[...] = m_new

        scores(0, 0)

        def run(first, count, n_diag):
            for u in range(count):
                if u + 1 < count or n_diag == 0:
                    scores(first + u + 1, (u + 1) % 2)
                softmax_pv(first + u, u % 2, u - (count - n_diag) if u >= count - n_diag else None)

        key_block_plan(i, run)
        acc = acc_sc[...]
        o_ref[...] = (acc[:, :LANES] / acc[:, LANES:]).astype(o_ref.dtype)
        lse_ref[0] = m_sc[...] + jnp.log2(acc[:, LANES:])

    mla_y, lse = pl.pallas_call(
        attn_fwd_kern, name="attn_fwd", grid=(H, nq),
        in_specs=[q_blk, k_head, v_head], out_specs=[o_blk, lse_blk],
        out_shape=[jax.ShapeDtypeStruct((N, c.HV), _BF), jax.ShapeDtypeStruct((H, N, LANES), F32)],
        scratch_shapes=[pltpu.VMEM((BQ, LANES), F32), pltpu.VMEM((BQ, HW), F32), pltpu.VMEM((2, BQ, B), F32)],
        compiler_params=_cparams(("parallel", "arbitrary")),
    )(q_cat, k_cat, kv)

    HV = c.HV
    mz_blk = c.o_mz // HV

    def gate_fwd_kern(y_ref, z_ref, o_ref):
        z = z_ref[...].astype(F32)
        o_ref[...] = (y_ref[...].astype(F32) * (z * _sig(z))).astype(o_ref.dtype)

    a_mla = pl.pallas_call(
        gate_fwd_kern, name="gate_mla", grid=(nr2,),
        in_specs=[rowb2(HV, 0), rowb2(HV, mz_blk)], out_specs=rowb2(HV, 0),
        out_shape=jax.ShapeDtypeStruct((N, HV), _BF), compiler_params=_cparams(("parallel",)),
    )(mla_y, proj)
    o_mla = _mm(a_mla, wmo, name="mm_omla", out_dtype=_BF)

    M, MW, MH, MHD = c.M, c.MW, c.MH, c.MHD
    memn = pl.pallas_call(
        make_rms_kern(), name="rms_mem",
        grid=(1,), in_specs=[fullb((M, D)), fullb((1, D))], out_specs=fullb((M, D)),
        out_shape=jax.ShapeDtypeStruct((M, D), _BF), compiler_params=_cparams(("arbitrary",)),
    )(memx, mem_norm_g)
    kvm = _mm(memn, wmkv, name="mm_memkv", out_dtype=F32)

    def memk_fwd_kern(kv_ref, g_ref, k_ref, v_ref):
        for hh in range(MH):
            kh, _ = _rms(kv_ref[:, hh * MHD:(hh + 1) * MHD], MHD)
            k_ref[:, hh * MHD:(hh + 1) * MHD] = (kh * g_ref[...]).astype(k_ref.dtype)
        v_ref[...] = kv_ref[:, MW:].astype(v_ref.dtype)

    mem_k, mem_v = pl.pallas_call(
        memk_fwd_kern, name="memk_fwd", grid=(1,),
        in_specs=[fullb((M, 2 * MW)), fullb((1, MHD))], out_specs=[fullb((M, MW)), fullb((M, MW))],
        out_shape=[jax.ShapeDtypeStruct((M, MW), _BF)] * 2, compiler_params=_cparams(("arbitrary",)),
    )(kvm, mem_kn_g)

    mem_blk = c.o_mem // (2 * MW)

    def mem_head(qz_ref, k_ref, v_ref, g_ref, hh):
        sl = slice(hh * MHD, (hh + 1) * MHD)
        qh, r = _rms(qz_ref[:, sl].astype(F32), MHD)
        qn = (qh * g_ref[...]).astype(_BF)
        s = _dot_nt(qn, k_ref[:, sl]) * c.mscale
        e = jnp.exp(s - jnp.max(s, axis=1, keepdims=True))
        p = e / jnp.sum(e, axis=1, keepdims=True)
        y = _dot(p.astype(_BF), v_ref[:, sl])
        z = qz_ref[:, MW + hh * MHD:MW + (hh + 1) * MHD].astype(F32)
        return sl, qh, r, qn, p, y, z

    def mem_fwd_kern(qz_ref, k_ref, v_ref, g_ref, o_ref):
        for hh in range(MH):
            sl, _, _, _, _, y, z = mem_head(qz_ref, k_ref, v_ref, g_ref, hh)
            o_ref[:, sl] = (y * (z * _sig(z))).astype(o_ref.dtype)

    a_mem = pl.pallas_call(
        mem_fwd_kern, name="mem_fwd", grid=(N // RM,),
        in_specs=[rowb(2 * MW, mem_blk, RM), fullb((M, MW)), fullb((M, MW)), fullb((1, MHD))],
        out_specs=rowb(MW, 0, RM), out_shape=jax.ShapeDtypeStruct((N, MW), _BF),
        compiler_params=_cparams(("parallel",)),
    )(proj, mem_k, mem_v, mem_qn_g)
    o_mem = _mm(a_mem, wmemo, name="mm_omem", out_dtype=_BF)

    g_blk = c.o_g // (3 * D)

    def merge_fwd_kern(g_ref, oc_ref, om_ref, ome_ref, o_ref):
        g = g_ref[...].astype(F32)
        acc = _sig(g[:, :D]) * oc_ref[...].astype(F32)
        acc += _sig(g[:, D:2 * D]) * om_ref[...].astype(F32)
        acc += _sig(g[:, 2 * D:]) * ome_ref[...].astype(F32)
        o_ref[...] = acc.astype(o_ref.dtype)

    merged = pl.pallas_call(
        merge_fwd_kern, name="merge_fwd", grid=(nr,),
        in_specs=[rowb(3 * D, g_blk), rowb(D, 0), rowb(D, 0), rowb(D, 0)], out_specs=rowb(D, 0),
        out_shape=jax.ShapeDtypeStruct((N, D), _BF), compiler_params=_cparams(("parallel",)),
    )(proj, o_conv, o_mla, o_mem)
    y2 = _mm(merged, wo, name="mm_out", out_dtype=F32)

    def loss_kern(x_ref, y_ref, t_ref, dy_ref, dyb_ref, l_ref):
        e = x_ref[...] + y_ref[...] - t_ref[...]
        dy = e * (1.0 / D)
        dy_ref[...] = dy
        dyb_ref[...] = dy.astype(dyb_ref.dtype)

        @pl.when(pl.program_id(0) == 0)
        def _():
            l_ref[...] = jnp.zeros(l_ref.shape, F32)

        l_ref[...] += jnp.sum(e * e, axis=0, keepdims=True)

    dy, dyb, lpart = pl.pallas_call(
        loss_kern, name="loss", grid=(nr2,),
        in_specs=[rowb2(D, 0)] * 3, out_specs=[rowb2(D, 0), rowb2(D, 0), fullb((1, D))],
        out_shape=[jax.ShapeDtypeStruct((N, D), F32), jax.ShapeDtypeStruct((N, D), _BF),
                   jax.ShapeDtypeStruct((1, D), F32)],
        compiler_params=_cparams(("arbitrary",)),
    )(xs, y2, tgt)
    loss = lax.psum(jnp.sum(lpart) * (0.5 / D), AXES)

    G = {}
    d_merged = _mm(dyb, wo, tb=True, name="mm_dmerged", out_dtype=_BF)
    G['w_o'] = _mm(merged, dyb, ta=True, name="mm_dwo", out_dtype=_BF)

    dproj0 = lax.empty((N, c.P), _BF)
    any_spec = pl.BlockSpec(memory_space=pl.ANY)

    def merge_bwd_kern(dp_any, g_ref, dm_ref, oc_ref, om_ref, ome_ref, dg_ref, doc_ref, dom_ref, dome_ref):
        g = g_ref[...].astype(F32)
        dm = dm_ref[...].astype(F32)
        for idx, (o_in, d_out) in enumerate(((oc_ref, doc_ref), (om_ref, dom_ref), (ome_ref, dome_ref))):
            sg = _sig(g[:, idx * D:(idx + 1) * D])
            d_out[...] = (sg * dm).astype(d_out.dtype)
            dg_ref[:, idx * D:(idx + 1) * D] = (dm * o_in[...].astype(F32) * sg * (1.0 - sg)).astype(dg_ref.dtype)

    dproj1, d_oconv, d_omla, d_omem = pl.pallas_call(
        merge_bwd_kern, name="merge_bwd", grid=(nr,),
        in_specs=[any_spec, rowb(3 * D, g_blk), rowb(D, 0), rowb(D, 0), rowb(D, 0), rowb(D, 0)],
        out_specs=[rowb(3 * D, g_blk), rowb(D, 0), rowb(D, 0), rowb(D, 0)],
        out_shape=[jax.ShapeDtypeStruct((N, c.P), _BF)] + [jax.ShapeDtypeStruct((N, D), _BF)] * 3,
        input_output_aliases={0: 0}, compiler_params=_cparams(("parallel",)),
    )(dproj0, proj, d_merged, o_conv, o_mla, o_mem)

    G['w_conv_out'] = _mm(a_conv, d_oconv, ta=True, name="mm_dwco", out_dtype=_BF)
    d_aconv = _mm(d_oconv, wco, tb=True, name="mm_daconv", out_dtype=_BF)
    G['w_mla_out'] = _mm(a_mla, d_omla, ta=True, name="mm_dwmo", out_dtype=_BF)
    d_amla = _mm(d_omla, wmo, tb=True, name="mm_damla", out_dtype=_BF)
    G['w_mem_out'] = _mm(a_mem, d_omem, ta=True, name="mm_dwmemo", out_dtype=_BF)
    d_amem = _mm(d_omem, wmemo, tb=True, name="mm_damem", out_dtype=_BF)

    def conv_bwd_kern(dp_any, p_ref, prev_ref, next_ref, da_ref, dan_ref, w_ref, o_ref, dw_ref):
        i = pl.program_id(0)
        cg, bg, u, z = conv_parts(p_ref[...])
        pc, _, pu, _ = conv_parts(prev_ref[...])
        _, nbg, _, nz = conv_parts(next_ref[...])
        cu = cg * u
        sh1, sh2 = shifted(cu, pc * pu, i)
        w = w_ref[...]
        conv = w[0:1, :] * sh2 + w[1:2, :] * sh1 + w[2:3, :] * cu
        sg = _sig(z)
        sz = z * sg
        da = da_ref[...].astype(F32)
        dcy = da * sz
        d_z = da * (bg * conv) * (sg * (1.0 + z * (1.0 - sg)))
        d_b = dcy * conv
        dconv = dcy * bg
        dnext = dan_ref[...].astype(F32) * (nz * _sig(nz)) * nbg
        dnext = jnp.where(i < nr - 1, dnext, 0.0)
        rid = lax.broadcasted_iota(jnp.int32, cu.shape, 0)
        up1 = jnp.where(rid == R - 1, dnext[0:1, :], pltpu.roll(dconv, R - 1, 0))
        up2 = jnp.where(rid == R - 2, dnext[0:1, :], jnp.where(rid == R - 1, dnext[1:2, :], pltpu.roll(dconv, R - 2, 0)))
        dcu = w[2:3, :] * dconv + w[1:2, :] * up1 + w[0:1, :] * up2
        o_ref[:, 0:CW] = (dcu * u).astype(o_ref.dtype)
        o_ref[:, CW:2 * CW] = d_b.astype(o_ref.dtype)
        o_ref[:, 2 * CW:3 * CW] = (dcu * cg).astype(o_ref.dtype)
        o_ref[:, 3 * CW:4 * CW] = d_z.astype(o_ref.dtype)

        @pl.when(i == 0)
        def _():
            dw_ref[...] = jnp.zeros(dw_ref.shape, F32)

        dw_ref[0:1, :] += jnp.sum(dconv * sh2, axis=0, keepdims=True)
        dw_ref[1:2, :] += jnp.sum(dconv * sh1, axis=0, keepdims=True)
        dw_ref[2:3, :] += jnp.sum(dconv * cu, axis=0, keepdims=True)

    next_spec = pl.BlockSpec((HALO, 4 * CW), lambda i: (jnp.minimum((i + 1) * rh, N // HALO - 1), conv_blk))
    dan_spec = pl.BlockSpec((HALO, CW), lambda i: (jnp.minimum((i + 1) * rh, N // HALO - 1), 0))
    dproj2, g_convw = pl.pallas_call(
        conv_bwd_kern, name="conv_bwd", grid=(nr,),
        in_specs=[any_spec, rowb(4 * CW, conv_blk), prev_spec, next_spec, rowb(CW, 0), dan_spec, fullb((8, CW))],
        out_specs=[rowb(4 * CW, conv_blk), fullb((8, CW))],
        out_shape=[jax.ShapeDtypeStruct((N, c.P), _BF), jax.ShapeDtypeStruct((8, CW), F32)],
        input_output_aliases={0: 0}, compiler_params=_cparams(("arbitrary",)),
    )(dproj1, proj, proj, proj, d_aconv, d_aconv, convw8)

    def mem_bwd_kern(dp_any, qz_ref, da_ref, k_ref, v_ref, g_ref, o_ref, dk_ref, dv_ref, dg_ref):
        @pl.when(pl.program_id(0) == 0)
        def _():
            dk_ref[...] = jnp.zeros(dk_ref.shape, F32)
            dv_ref[...] = jnp.zeros(dv_ref.shape, F32)
            dg_ref[...] = jnp.zeros(dg_ref.shape, F32)

        for hh in range(MH):
            sl, qh, r, qn, p, y, z = mem_head(qz_ref, k_ref, v_ref, g_ref, hh)
            da = da_ref[:, sl].astype(F32)
            sg = _sig(z)
            dyh = da * (z * sg)
            o_ref[:, MW + hh * MHD:MW + (hh + 1) * MHD] = (da * y * (sg * (1.0 + z * (1.0 - sg)))).astype(o_ref.dtype)
            dyb_h = dyh.astype(_BF)
            dpm = _dot_nt(dyb_h, v_ref[:, sl])
            ds = (p * (dpm - jnp.sum(dpm * p, axis=1, keepdims=True)) * c.mscale).astype(_BF)
            dqn = _dot(ds, k_ref[:, sl])
            dk_ref[:, sl] += _dot_tn(ds, qn)
            dv_ref[:, sl] += _dot_tn(p.astype(_BF), dyb_h)
            dq, dgp = _rms_bwd(qh, r, g_ref[...], dqn, MHD)
            o_ref[:, sl] = dq.astype(o_ref.dtype)
            dg_ref[...] += jnp.sum(dgp, axis=0, keepdims=True)

    dproj3, d_memk, d_memv, g_mem_qn = pl.pallas_call(
        mem_bwd_kern, name="mem_bwd", grid=(N // RM,),
        in_specs=[any_spec, rowb(2 * MW, mem_blk, RM), rowb(MW, 0, RM), fullb((M, MW)), fullb((M, MW)), fullb((1, MHD))],
        out_specs=[rowb(2 * MW, mem_blk, RM), fullb((M, MW)), fullb((M, MW)), fullb((1, MHD))],
        out_shape=[jax.ShapeDtypeStruct((N, c.P), _BF), jax.ShapeDtypeStruct((M, MW), F32),
                   jax.ShapeDtypeStruct((M, MW), F32), jax.ShapeDtypeStruct((1, MHD), F32)],
        input_output_aliases={0: 0}, compiler_params=_cparams(("arbitrary",)),
    )(dproj2, proj, d_amem, mem_k, mem_v, mem_qn_g)

    def memk_bwd_kern(kv_ref, dk_ref, dv_ref, g_ref, o_ref, dg_ref):
        dg = jnp.zeros((1, MHD), F32)
        for hh in range(MH):
            sl = slice(hh * MHD, (hh + 1) * MHD)
            kh, r = _rms(kv_ref[:, sl], MHD)
            dkr, dgp = _rms_bwd(kh, r, g_ref[...], dk_ref[:, sl], MHD)
            o_ref[:, sl] = dkr.astype(o_ref.dtype)
            dg += jnp.sum(dgp, axis=0, keepdims=True)
        o_ref[:, MW:] = dv_ref[...].astype(o_ref.dtype)
        dg_ref[...] = dg

    d_kvm, g_mem_kn = pl.pallas_call(
        memk_bwd_kern, name="memk_bwd", grid=(1,),
        in_specs=[fullb((M, 2 * MW)), fullb((M, MW)), fullb((M, MW)), fullb((1, MHD))],
        out_specs=[fullb((M, 2 * MW)), fullb((1, MHD))],
        out_shape=[jax.ShapeDtypeStruct((M, 2 * MW), _BF), jax.ShapeDtypeStruct((1, MHD), F32)],
        compiler_params=_cparams(("arbitrary",)),
    )(kvm, d_memk, d_memv, mem_kn_g)
    G['w_mem_kv'] = _mm(memn, d_kvm, ta=True, name="mm_dwmkv", out_dtype=_BF)
    d_memn = _mm(d_kvm, wmkv, tb=True, name="mm_dmemn", out_dtype=F32)

    def memnorm_bwd_kern(x_ref, d_ref, dg_ref):
        xh, _ = _rms(x_ref[...], D)
        dg_ref[...] = jnp.sum(d_ref[...] * xh, axis=0, keepdims=True)

    g_mem_norm = pl.pallas_call(
        memnorm_bwd_kern, name="memnorm_bwd", grid=(1,),
        in_specs=[fullb((M, D)), fullb((M, D))], out_specs=fullb((1, D)),
        out_shape=jax.ShapeDtypeStruct((1, D), F32), compiler_params=_cparams(("arbitrary",)),
    )(memx, d_memn)

    def gate_bwd_kern(dp_any, da_ref, y_ref, z_ref, dy_ref, dz_ref):
        z = z_ref[...].astype(F32)
        da = da_ref[...].astype(F32)
        sg = _sig(z)
        dy_ref[...] = (da * (z * sg)).astype(dy_ref.dtype)
        dz_ref[...] = (da * y_ref[...].astype(F32) * (sg * (1.0 + z * (1.0 - sg)))).astype(dz_ref.dtype)

    d_mlay, dproj4 = pl.pallas_call(
        gate_bwd_kern, name="gate_mla_bwd", grid=(nr2,),
        in_specs=[any_spec, rowb2(HV, 0), rowb2(HV, 0), rowb2(HV, mz_blk)],
        out_specs=[rowb2(HV, 0), rowb2(HV, mz_blk)],
        out_shape=[jax.ShapeDtypeStruct((N, HV), _BF), jax.ShapeDtypeStruct((N, c.P), _BF)],
        input_output_aliases={0: 1}, compiler_params=_cparams(("parallel",)),
    )(dproj3, d_amla, mla_y, proj)

    def attn_bwd_kern(q_ref, k_ref, v_ref, o_ref, do_ref, lse_ref, dq_ref, dk_ref, dv_ref, dq_sc, dl_sc, dk_sc, dv_sc):
        i = pl.program_id(1)
        q, do = q_ref[0], do_ref[...]
        delta = jnp.sum(do.astype(F32) * o_ref[...].astype(F32), axis=1, keepdims=True)
        dl_sc[...] = jnp.broadcast_to(delta, dl_sc.shape)
        dq_sc[...] = jnp.zeros(dq_sc.shape, F32)

        def step(t, diag):
            rows = pl.ds(pl.multiple_of(t * B, B), B)
            k = k_ref[0, rows, :]
            s = _dot_nt(q, k)
            if diag is not None:
                s = diag_mask(s, diag)
            dpm = _dot_nt(do, v_ref[rows, :])
            lse_t, dl = lse_ref[0], dl_sc[...]
            ps, dss = [], []
            for cb in range(B // LANES):
                cols = slice(cb * LANES, (cb + 1) * LANES)
                p_cb = jnp.exp2(s[:, cols] - lse_t)
                ps.append(p_cb.astype(_BF))
                dss.append((p_cb * (dpm[:, cols] - dl)).astype(_BF))
            p, ds = jnp.concatenate(ps, axis=1), jnp.concatenate(dss, axis=1)
            dvp = _dot_tn(p, do)
            dkp = _dot_tn(ds, q)
            if diag is not None:
                dk_sc[rows, :] = dkp
                dv_sc[rows, :] = dvp
            else:
                dk_sc[rows, :] += dkp
                dv_sc[rows, :] += dvp
            dq_sc[...] += _dot(ds, k)

        def run(first, count, n_diag):
            for u in range(count):
                step(first + u, u - (count - n_diag) if u >= count - n_diag else None)

        key_block_plan(i, run)
        dq_ref[0] = dq_sc[...].astype(dq_ref.dtype)

        @pl.when(i == nq - 1)
        def _():
            dk_ref[0] = dk_sc[...].astype(dk_ref.dtype)
            dv_ref[0] = dv_sc[...].astype(dv_ref.dtype)

    d_qcat, d_kcat, d_v = pl.pallas_call(
        attn_bwd_kern, name="attn_bwd", grid=(H, nq),
        in_specs=[q_blk, k_head, v_head, o_blk, o_blk, lse_blk],
        out_specs=[pl.BlockSpec((1, BQ, HW), lambda hh, i: (hh, i, 0)),
                   pl.BlockSpec((1, N, HW), lambda hh, i: (hh, 0, 0)),
                   pl.BlockSpec((1, N, LANES), lambda hh, i: (hh, 0, 0))],
        out_shape=[jax.ShapeDtypeStruct((H, N, HW), _BF), jax.ShapeDtypeStruct((H, N, HW), _BF),
                   jax.ShapeDtypeStruct((H, N, LANES), _BF)],
        scratch_shapes=[pltpu.VMEM((BQ, HW), F32), pltpu.VMEM((BQ, LANES), F32), pltpu.VMEM((N, HW), F32),
                        pltpu.VMEM((N, LANES), F32)],
        compiler_params=_cparams(("parallel", "arbitrary")),
    )(q_cat, k_cat, kv, mla_y, d_mlay, lse)

    def q_prep_bwd_kern(q_ref, dq_ref, t_ref, gn_ref, gr_ref, o_ref, dgn_ref, dgr_ref):
        @pl.when(jnp.logical_and(pl.program_id(0) == 0, pl.program_id(1) == 0))
        def _():
            dgn_ref[...] = jnp.zeros(dgn_ref.shape, F32)
            dgr_ref[...] = jnp.zeros(dgr_ref.shape, F32)

        for g in range(HG):
            blk = q_ref[:, g * HW:(g + 1) * HW].astype(F32)
            d = dq_ref[g].astype(F32) * c.scale
            nh, rn = _rms128(blk[:, :LANES], c.NOPE)
            rhat, rr = _rms128(blk[:, LANES:], c.ROPE)
            dn, dgn = _rms128_bwd(nh, rn, gn_ref[...], d[:, :LANES], c.NOPE)
            drot = _rope_t(d[:, LANES:], t_ref[0], t_ref[1], t_ref[2])
            dr, dgr = _rms128_bwd(rhat, rr, gr_ref[...], drot, c.ROPE)
            o_ref[:, g * HW:(g + 1) * HW] = jnp.concatenate([dn, dr], axis=1).astype(o_ref.dtype)
            dgn_ref[...] += jnp.sum(dgn, axis=0, keepdims=True)
            dgr_ref[...] += jnp.sum(dgr, axis=0, keepdims=True)

    d_qp, g_qn_nope, g_qn_rope = pl.pallas_call(
        q_prep_bwd_kern, name="q_prep_bwd", grid=(nrp, nhg),
        in_specs=[heads_in, heads_out, tabs_of(RP), fullb((1, LANES)), fullb((1, LANES))],
        out_specs=[heads_in, fullb((1, LANES)), fullb((1, LANES))],
        out_shape=[jax.ShapeDtypeStruct((N, H * HW), _BF), jax.ShapeDtypeStruct((1, LANES), F32),
                   jax.ShapeDtypeStruct((1, LANES), F32)],
        compiler_params=_cparams(("arbitrary", "arbitrary")),
    )(q_p, d_qcat, tabs, g_qn, g_qr)

    def k_prep_bwd_kern(kv_ref, dk_ref, dv_ref, gn_ref, o_ref, dkr_ref, dgn_ref):
        hg = pl.program_id(1)

        @pl.when(jnp.logical_and(pl.program_id(0) == 0, hg == 0))
        def _():
            dgn_ref[...] = jnp.zeros(dgn_ref.shape, F32)

        @pl.when(hg == 0)
        def _():
            dkr_ref[...] = jnp.zeros(dkr_ref.shape, F32)

        dkr = jnp.zeros((RP, LANES), F32)
        for g in range(HG):
            dk = dk_ref[g].astype(F32) * (1.0 / LOG2E)
            kn, r = _rms128(kv_ref[:, g * HW:g * HW + LANES].astype(F32), c.NOPE)
            dkn, dgn = _rms128_bwd(kn, r, gn_ref[...], dk[:, :LANES], c.NOPE)
            o_ref[:, g * HW:(g + 1) * HW] = jnp.concatenate([dkn.astype(o_ref.dtype), dv_ref[g]], axis=1)
            dgn_ref[...] += jnp.sum(dgn, axis=0, keepdims=True)
            dkr += dk[:, LANES:]
        dkr_ref[...] += dkr

    d_kv, d_krsum, g_kn_nope = pl.pallas_call(
        k_prep_bwd_kern, name="k_prep_bwd", grid=(nrp, nhg),
        in_specs=[heads_in, heads_out, pl.BlockSpec((HG, RP, LANES), lambda i, hg: (hg, i, 0)), fullb((1, LANES))],
        out_specs=[heads_in, pl.BlockSpec((RP, LANES), lambda i, hg: (i, 0)), fullb((1, LANES))],
        out_shape=[jax.ShapeDtypeStruct((N, H * HW), _BF), jax.ShapeDtypeStruct((N, LANES), F32),
                   jax.ShapeDtypeStruct((1, LANES), F32)],
        compiler_params=_cparams(("arbitrary", "arbitrary")),
    )(kv, d_kcat, d_v, g_kn)

    def krope_bwd_kern(p_ref, d_ref, t_ref, g_ref, o_ref, dg_ref):
        @pl.when(pl.program_id(0) == 0)
        def _():
            dg_ref[...] = jnp.zeros(dg_ref.shape, F32)

        xh, r = _rms128(p_ref[...].astype(F32), c.ROPE)
        drot = _rope_t(d_ref[...], t_ref[0], t_ref[1], t_ref[2])
        dx, dg = _rms128_bwd(xh, r, g_ref[...], drot, c.ROPE)
        o_ref[...] = dx.astype(o_ref.dtype)
        dg_ref[...] += jnp.sum(dg, axis=0, keepdims=True)

    d_kr, g_kn_rope = pl.pallas_call(
        krope_bwd_kern, name="krope_bwd", grid=(nr,),
        in_specs=[rowb(LANES, 0), rowb(LANES, 0), tabs_of(R), fullb((1, LANES))],
        out_specs=[rowb(LANES, 0), fullb((1, LANES))],
        out_shape=[jax.ShapeDtypeStruct((N, LANES), _BF), jax.ShapeDtypeStruct((1, LANES), F32)],
        compiler_params=_cparams(("arbitrary",)),
    )(kr_raw, d_krsum, tabs, g_kr)
    dproj5 = dproj4

    g_wuq_p = _mm(cqn, d_qp, ta=True, name="mm_dwuq", out_dtype=_BF, bk=4096)
    G['w_uq'] = g_wuq_p.reshape(QL, H, HW)[:, :, :c.NOPE + c.ROPE].reshape(QL, H * (c.NOPE + c.ROPE))
    d_cqn = _mm(d_qp, wuq_p, tb=True, name="mm_dcqn", out_dtype=F32, bk=4096)
    G['w_ukv'] = _mm(ckvn, d_kv, ta=True, name="mm_dwukv", out_dtype=_BF, bk=4096)
    d_ckvn = _mm(d_kv, wukv, tb=True, name="mm_dckvn", out_dtype=F32, bk=4096)

    def to_blocks(n, g):
        if n in COL_SHARDED:
            return jnp.transpose(g.reshape(g.shape[0], NDEV, -1), (1, 0, 2))
        return g.reshape(NDEV, -1, g.shape[1])

    def landing(b):
        own = lax.dynamic_index_in_dim(b, me, 0, keepdims=True)
        return lax.dynamic_update_index_in_dim(lax.empty(b.shape, b.dtype), own, me, 0)

    early = [n for n in BIG if n != 'w_in']
    blocks_e = [to_blocks(n, G[n]) for n in early]
    xe = _split_start(blocks_e, [landing(b) for b in blocks_e], False, "xchg_early_start")
    gq_after = mla_q_norm_g + xe[4][0:1, 0:1]

    def lora_bwd_kern(dp_any, p_ref, dq_ref, dkv_ref, gq_ref, gkv_ref, o_ref, dgq_ref, dgkv_ref):
        @pl.when(pl.program_id(0) == 0)
        def _():
            dgq_ref[...] = jnp.zeros(dgq_ref.shape, F32)
            dgkv_ref[...] = jnp.zeros(dgkv_ref.shape, F32)

        blk = p_ref[...].astype(F32)
        qh, rq = _rms(blk[:, :QL], QL)
        kh, rk = _rms(blk[:, QL:], KVL)
        dq, dgq = _rms_bwd(qh, rq, gq_ref[...], dq_ref[...], QL)
        dk, dgk = _rms_bwd(kh, rk, gkv_ref[...], dkv_ref[...], KVL)
        o_ref[:, :QL] = dq.astype(o_ref.dtype)
        o_ref[:, QL:] = dk.astype(o_ref.dtype)
        dgq_ref[...] += jnp.sum(dgq, axis=0, keepdims=True)
        dgkv_ref[...] += jnp.sum(dgk, axis=0, keepdims=True)

    dproj6, g_q_norm, g_kv_norm = pl.pallas_call(
        lora_bwd_kern, name="lora_bwd", grid=(nr,),
        in_specs=[any_spec, rowb(QL + KVL, lora_blk), rowb(QL, 0), rowb(KVL, 0), fullb((1, QL)), fullb((1, KVL))],
        out_specs=[rowb(QL + KVL, lora_blk), fullb((1, QL)), fullb((1, KVL))],
        out_shape=[jax.ShapeDtypeStruct((N, c.P), _BF), jax.ShapeDtypeStruct((1, QL), F32),
                   jax.ShapeDtypeStruct((1, KVL), F32)],
        input_output_aliases={0: 0}, compiler_params=_cparams(("arbitrary",)),
    )(dproj5, proj, d_cqn, d_ckvn, gq_after, mla_kv_norm_g)

    g_win_p = _mm(h, dproj6, ta=True, name="mm_dwin", out_dtype=_BF, bk=4096)
    g_wkr = _mm(h, d_kr, ta=True, name="mm_dwkr", out_dtype=_BF, bk=4096)
    blocks_w = [_win_blocks(g_win_p, g_wkr, c)]
    xw = _split_start(blocks_w, [landing(b) for b in blocks_w], False, "xchg_win_start")
    d_h = _mm(dproj6, win_p, tb=True, name="mm_dh", out_dtype=F32, bk=3072, after=xw[4], plus=(d_kr, w_krT))

    def final_bwd_kern(x_ref, g_ref, dh_ref, dy_ref, gx_ref, dg_ref):
        @pl.when(pl.program_id(0) == 0)
        def _():
            dg_ref[...] = jnp.zeros(dg_ref.shape, F32)

        xh, r = _rms(x_ref[...], D)
        dx, dg = _rms_bwd(xh, r, g_ref[...], dh_ref[...], D)
        gx_ref[...] = dy_ref[...] + dx
        dg_ref[...] += jnp.sum(dg, axis=0, keepdims=True)

    grad_x, g_norm = pl.pallas_call(
        final_bwd_kern, name="final_bwd", grid=(nr2,),
        in_specs=[rowb2(D, 0), fullb((1, D)), rowb2(D, 0), rowb2(D, 0)],
        out_specs=[rowb2(D, 0), fullb((1, D))],
        out_shape=[jax.ShapeDtypeStruct((N, D), F32), jax.ShapeDtypeStruct((1, D), F32)],
        compiler_params=_cparams(("arbitrary",)),
    )(xs, norm_g, d_h, dy)

    recv_e = _split_wait(xe, False, grad_x, "xchg_early_wait")
    recv_w = _split_wait(xw, False, grad_x, "xchg_win_wait")
    res = [{}, {}, {}, {}]
    for n, parts in zip(['w_in'] + early, list(recv_w) + list(recv_e)):
        outs = _adam(parts, W[n][0], Mo[n][0], Vo[n][0], "adam_" + n)
        for k in range(4):
            res[k][n] = outs[k][None]

    small_g = {'norm_g': g_norm, 'mla_q_norm_g': g_q_norm, 'mla_kv_norm_g': g_kv_norm,
               'mla_qn_nope_g': g_qn_nope, 'mla_qn_rope_g': g_qn_rope[:, :c.ROPE], 'mla_kn_nope_g': g_kn_nope,
               'mla_kn_rope_g': g_kn_rope[:, :c.ROPE], 'mem_norm_g': g_mem_norm, 'mem_qn_g': g_mem_qn,
               'mem_kn_g': g_mem_kn}
    small_part = _pack([small_g[n] for n in SMALL] + [g_convw[0:3, :]], 0)
    small_all = _all_gather([small_part], "ag_small_grads")[0]
    small_shapes = [W[n].shape for n in SMALL]
    pieces = _unpack(small_all, small_shapes + [(3, CW)])
    cw8 = CW // NDEV
    conv_mine = lax.dynamic_slice_in_dim(pieces[-1].reshape(NDEV, 3, NDEV, cw8), me, 1, axis=2)[:, :, 0, :]
    sm_parts = _pack(pieces[:-1] + [conv_mine], 1)
    sm_names = SMALL + ['conv_w']
    sm_shapes = small_shapes + [(3, cw8)]
    w_sm = _pack([W[n] for n in SMALL] + [conv_w[0]], 0)
    m_sm = _pack([Mo[n] for n in SMALL] + [m_conv_w[0]], 0)
    v_sm = _pack([Vo[n] for n in SMALL] + [v_conv_w[0]], 0)
    outs_sm = [_unpack(o, sm_shapes) for o in _adam(sm_parts, w_sm, m_sm, v_sm, "adam_small")]
    for k in range(4):
        for n, a in zip(sm_names, outs_sm[k]):
            res[k][n] = a[None] if n == 'conv_w' else a
    return (loss, grad_x[None], *[res[0][n] for n in WEIGHTS], *[res[1][n] for n in WEIGHTS],
            *[res[2][n] for n in WEIGHTS], *[res[3][n] for n in WEIGHTS])
```

```python
import math

import jax
import jax.numpy as jnp
from jax import lax
from jax.experimental import pallas as pl
from jax.experimental.pallas import tpu as pltpu

F32 = jnp.float32
_BF = jnp.bfloat16
EPS = 1e-6
CHUNK = 64
ROPE_THETA = 10000.0
ADAM_LR, ADAM_B1, ADAM_B2, ADAM_EPS, ADAM_WD, ADAM_STEP = 0.001, 0.9, 0.999, 1e-08, 0.01, 10
NDEV = 8
AXES = ("x", "y", "c")
MESH = pl.DeviceIdType.MESH
LANES = 128
NEG = -1e30
LOG2E = math.log2(math.e)
V7X_VMEM_LIMIT = 56 * 1024 * 1024
PACK_C = 1024
ATT_BLOCK = 512
ATT_UNROLL = 4
ATT_QBLOCKS = 2
ADAM_BLOCK_ELEMS = 256 * 1024

WEIGHTS = ['norm_g', 'w_in', 'conv_w', 'w_conv_out', 'mla_q_norm_g', 'w_uq', 'mla_kv_norm_g', 'w_ukv',
           'mla_qn_nope_g', 'mla_qn_rope_g', 'mla_kn_nope_g', 'mla_kn_rope_g', 'w_mla_out', 'mem_norm_g',
           'w_mem_kv', 'mem_qn_g', 'mem_kn_g', 'w_mem_out', 'w_o']
BIG = ['w_in', 'w_conv_out', 'w_uq', 'w_ukv', 'w_mla_out', 'w_mem_kv', 'w_mem_out', 'w_o']
COL_SHARDED = ('w_in', 'w_conv_out', 'w_uq', 'w_ukv', 'w_mem_out')
SMALL = ['norm_g', 'mla_q_norm_g', 'mla_kv_norm_g', 'mla_qn_nope_g', 'mla_qn_rope_g', 'mla_kn_nope_g',
         'mla_kn_rope_g', 'mem_norm_g', 'mem_qn_g', 'mem_kn_g']


def _tile(dim, target, align):
    if dim <= target:
        return dim
    t = target - target % align
    while t > 0:
        if dim % t == 0:
            return t
        t -= align
    raise ValueError(f"no tile for {dim} {target} {align}")


def _cparams(sem):
    return pltpu.CompilerParams(dimension_semantics=sem, vmem_limit_bytes=V7X_VMEM_LIMIT)


def _sig(x):
    return 1.0 / (1.0 + jnp.exp(-x))


def _rms(x, n):
    r = lax.rsqrt(jnp.sum(x * x, axis=-1, keepdims=True) * (1.0 / n) + EPS)
    return x * r, r


def _rms_bwd(xhat, r, g, dy, n):
    dxh = dy * g
    dx = r * (dxh - xhat * (jnp.sum(dxh * xhat, axis=-1, keepdims=True) * (1.0 / n)))
    return dx, dy * xhat


def _rowsum128(x):
    return jnp.dot(x.astype(_BF), jnp.ones((LANES, LANES), _BF), preferred_element_type=F32)


def _rms128(x, n):
    r = lax.rsqrt(_rowsum128(x * x) * (1.0 / n) + EPS)
    return x * r, r


def _rms128_bwd(xhat, r, g, dy, n):
    dxh = dy * g
    dx = r * (dxh - xhat * (_rowsum128(dxh * xhat) * (1.0 / n)))
    return dx, dy * xhat


def _rope(x, cosp, sina, sinb):
    return x * cosp + pltpu.roll(x, 96, 1) * sina + pltpu.roll(x, 32, 1) * sinb


def _rope_t(d, cosp, sina, sinb):
    return d * cosp + pltpu.roll(d * sina, 32, 1) + pltpu.roll(d * sinb, 96, 1)


def _dot_nt(a, b):
    return lax.dot_general(a, b, (((1,), (1,)), ((), ())), preferred_element_type=F32)


def _dot_tn(a, b):
    return lax.dot_general(a, b, (((0,), (0,)), ((), ())), preferred_element_type=F32)


def _dot(a, b):
    return jnp.dot(a, b, preferred_element_type=F32)


def _all_gather(shards, name):
    na = len(shards)
    nc = 9
    halves = [s.shape[0] // 2 if s.shape[0] % 32 == 0 else None for s in shards]

    def body(*refs):
        x_refs, out_refs = refs[:na], refs[na:2 * na]
        send_sems, recv_sems, local_sems = refs[2 * na:]
        x, y, c = lax.axis_index("x"), lax.axis_index("y"), lax.axis_index("c")
        me, sib = (x, y, c), (x, y, 1 - c)
        px, py, pd = (1 - x, y, c), (x, 1 - y, c), (1 - x, 1 - y, c)

        def other_core(p):
            return (p[0], p[1], 1 - p[2])

        def rows(a, blk, part=None):
            r = out_refs[a].at[4 * blk[0] + 2 * blk[1] + blk[2]]
            if part is None or halves[a] is None:
                return r
            return r.at[pl.ds(0, halves[a])] if part == 0 else r.at[pl.ds(halves[a], halves[a])]

        def copy(a, k, blk, to, part=None, src=None):
            dst = rows(a, blk, part)
            return pltpu.make_async_remote_copy(
                src_ref=dst if src is None else src, dst_ref=dst,
                send_sem=send_sems.at[nc * a + k], recv_sem=recv_sems.at[nc * a + k],
                device_id=to, device_id_type=MESH)

        mine = [pltpu.make_async_copy(x_refs[a], rows(a, me), local_sems.at[a]) for a in range(na)]
        for cp in mine:
            cp.start()
        started = []
        for a in range(na):
            started += [copy(a, 0, me, px, src=x_refs[a]), copy(a, 1, me, py, src=x_refs[a]),
                        copy(a, 2, me, sib, src=x_refs[a])]
        for cp in started:
            cp.start()

        def forward(cp):
            cp.start()
            started.append(cp)

        for a in range(na):
            copy(a, 0, px, me).wait_recv()
            forward(copy(a, 3, px, py, part=0))
            forward(copy(a, 4, px, sib))
        for a in range(na):
            copy(a, 1, py, me).wait_recv()
            if halves[a] is not None:
                forward(copy(a, 5, py, px, part=1))
            forward(copy(a, 6, py, sib))
        for a in range(na):
            copy(a, 3, pd, me, part=0).wait_recv()
            forward(copy(a, 7, pd, sib, part=0))
            if halves[a] is not None:
                copy(a, 5, pd, me, part=1).wait_recv()
                forward(copy(a, 8, pd, sib, part=1))
        for a in range(na):
            copy(a, 2, sib, me).wait_recv()
            copy(a, 4, other_core(px), me).wait_recv()
            copy(a, 6, other_core(py), me).wait_recv()
            copy(a, 7, other_core(pd), me, part=0).wait_recv()
            if halves[a] is not None:
                copy(a, 8, other_core(pd), me, part=1).wait_recv()
        for cp in started:
            cp.wait_send()
        for cp in mine:
            cp.wait()

    any_spec = pl.BlockSpec(memory_space=pl.ANY)
    return pl.pallas_call(
        body, name=name,
        out_shape=[jax.ShapeDtypeStruct((NDEV,) + s.shape, s.dtype) for s in shards],
        in_specs=[any_spec] * na, out_specs=[any_spec] * na,
        scratch_shapes=[pltpu.SemaphoreType.DMA((nc * na,)), pltpu.SemaphoreType.DMA((nc * na,)),
                        pltpu.SemaphoreType.DMA((na,))],
    )(*shards)


_HBM = pl.BlockSpec(memory_space=pltpu.HBM)
_SEM = pl.BlockSpec(memory_space=pltpu.SEMAPHORE)
_EFFECT = pltpu.SideEffectType.DATAFLOW_SIDE_EFFECTING


def _split_copy(a, k, src_refs, land_refs, send_sems, recv_sems, gather, receive_side):
    x, y, c = lax.axis_index("x"), lax.axis_index("y"), lax.axis_index("c")
    me = 4 * x + 2 * y + c
    tx, ty, tc = x ^ ((k + 1) >> 2 & 1), y ^ ((k + 1) >> 1 & 1), c ^ ((k + 1) & 1)
    peer = 4 * tx + 2 * ty + tc
    return pltpu.make_async_remote_copy(
        src_ref=src_refs[a] if gather else src_refs[a].at[peer],
        dst_ref=land_refs[a].at[peer if receive_side else me],
        send_sem=send_sems.at[7 * a + k], recv_sem=recv_sems.at[7 * a + k],
        device_id=(tx, ty, tc), device_id_type=MESH)


def _split_start(srcs, lands, gather, name, after=None):
    na = len(srcs)
    extra = [] if after is None else [after]

    def body(*refs):
        src_refs, land_refs = refs[:na], refs[na:2 * na]
        send_sems, recv_sems = refs[2 * na + len(extra)], refs[2 * na + len(extra) + 1]
        token = refs[-1]
        for k in range(7):
            for a in range(na):
                _split_copy(a, k, src_refs, land_refs, send_sems, recv_sems, gather, False).start()
        token[...] = jnp.zeros_like(token)

    hbm = [pltpu.HBM(b.shape, b.dtype) for b in list(srcs) + list(lands)]
    outs = pl.pallas_call(
        body, name=name,
        out_shape=(pltpu.SemaphoreType.DMA((7 * na,)), pltpu.SemaphoreType.DMA((7 * na,)), *hbm,
                   jax.ShapeDtypeStruct((8, LANES), F32)),
        in_specs=[_HBM] * (2 * na) + [pl.BlockSpec(memory_space=pl.ANY)] * len(extra),
        out_specs=(_SEM, _SEM, *[_HBM] * (2 * na), pl.BlockSpec(memory_space=pltpu.VMEM)),
        input_output_aliases={j: 2 + j for j in range(2 * na)},
        compiler_params=pltpu.CompilerParams(has_side_effects=_EFFECT),
    )(*[pltpu.with_memory_space_constraint(b, pltpu.HBM) for b in srcs],
      *[pltpu.with_memory_space_constraint(l, pltpu.HBM) for l in lands], *extra)
    return outs[0], outs[1], outs[2:2 + na], outs[2 + na:2 + 2 * na], outs[-1]


def _split_wait(started, gather, after, name):
    send_sems, recv_sems, srcs, lands, _ = started
    na = len(srcs)

    def body(*refs):
        src_refs, land_refs = refs[:na], refs[na:2 * na]
        send_s, recv_s = refs[2 * na], refs[2 * na + 1]
        for k in range(7):
            for a in range(na):
                cp = _split_copy(a, k, src_refs, land_refs, send_s, recv_s, gather, True)
                cp.wait_send()
                cp.wait_recv()

    hbm = [pltpu.HBM(b.shape, b.dtype) for b in list(srcs) + list(lands)]
    outs = pl.pallas_call(
        body, name=name, out_shape=tuple(hbm),
        in_specs=[_HBM] * (2 * na) + [_SEM, _SEM, pl.BlockSpec(memory_space=pl.ANY)],
        out_specs=tuple([_HBM] * (2 * na)),
        input_output_aliases={j: j for j in range(2 * na)},
        compiler_params=pltpu.CompilerParams(has_side_effects=_EFFECT),
    )(*srcs, *lands, send_sems, recv_sems, after)
    return outs[na:]


def _seg_rows(size):
    rows = -(-size // PACK_C)
    return -(-rows // 16) * 16


def _pack(arrs, lead):
    parts = []
    for a in arrs:
        lshape = a.shape[:lead]
        f = a.reshape(lshape + (-1,)).astype(F32)
        rows = _seg_rows(f.shape[-1])
        f = jnp.pad(f, [(0, 0)] * lead + [(0, rows * PACK_C - f.shape[-1])])
        parts.append(f.reshape(lshape + (rows, PACK_C)))
    return jnp.concatenate(parts, axis=lead)


def _unpack(buf, shapes):
    lshape = buf.shape[:-2]
    out, r = [], 0
    for shp in shapes:
        size = math.prod(shp)
        rows = _seg_rows(size)
        seg = buf[..., r:r + rows, :].reshape(lshape + (rows * PACK_C,))[..., :size]
        out.append(seg.reshape(lshape + tuple(shp)))
        r += rows
    return out


def _mm(a, b, *, name, out_dtype, ta=False, tb=False, bm=1024, bn=1024, bk=2048, after=None, plus=None):
    if ta:
        kdim, m = a.shape
    else:
        m, kdim = a.shape
    n, k2 = b.shape if tb else b.shape[::-1]
    assert kdim == k2 and not (ta and tb), (a.shape, b.shape)
    bm = _tile(m, bm, LANES if ta else 16)
    bn = _tile(n, bn, LANES)
    bk = _tile(kdim, bk, LANES)
    nk = kdim // bk
    n_after = 0 if after is None else 1
    n_plus = 0 if plus is None else 2

    def kern(a_ref, b_ref, *rest):
        plus_refs = rest[n_after:n_after + n_plus]
        o_ref, scratch = rest[n_after + n_plus], rest[n_after + n_plus + 1:]
        part = (_dot_tn if ta else _dot_nt if tb else _dot)(a_ref[...], b_ref[...])

        def first(p):
            return p + _dot(plus_refs[0][...], plus_refs[1][...]) if plus is not None else p

        if nk == 1:
            o_ref[...] = first(part).astype(o_ref.dtype)
        else:
            acc = scratch[0] if scratch else o_ref
            k = pl.program_id(2)

            @pl.when(k == 0)
            def _():
                acc[...] = first(jnp.zeros(acc.shape, F32))

            acc[...] += part
            if scratch:
                @pl.when(k == nk - 1)
                def _():
                    o_ref[...] = acc[...].astype(o_ref.dtype)

    a_spec = pl.BlockSpec((bk, bm), lambda i, j, k: (k, i)) if ta else pl.BlockSpec((bm, bk), lambda i, j, k: (i, k))
    b_spec = pl.BlockSpec((bn, bk), lambda i, j, k: (j, k)) if tb else pl.BlockSpec((bk, bn), lambda i, j, k: (k, j))
    extra_specs, extra_args = [], []
    if after is not None:
        extra_specs.append(pl.BlockSpec(after.shape, lambda i, j, k: (0, 0)))
        extra_args.append(after)
    if plus is not None:
        kk = plus[0].shape[1]
        extra_specs += [pl.BlockSpec((bm, kk), lambda i, j, k: (i, 0)), pl.BlockSpec((kk, bn), lambda i, j, k: (0, j))]
        extra_args += list(plus)
    return pl.pallas_call(
        kern, name=name, grid=(m // bm, n // bn, nk),
        in_specs=[a_spec, b_spec] + extra_specs,
        out_specs=pl.BlockSpec((bm, bn), lambda i, j, k: (i, j)),
        out_shape=jax.ShapeDtypeStruct((m, n), out_dtype),
        scratch_shapes=[pltpu.VMEM((bm, bn), F32)] if nk > 1 and out_dtype != F32 else [],
        compiler_params=_cparams(("parallel", "parallel", "arbitrary")),
    )(a, b, *extra_args)


def _adam(parts, w_a, m_a, v_a, name):
    rows, cols = w_a.shape
    rb = _tile(rows, max(8, ADAM_BLOCK_ELEMS // cols // 8 * 8), 8)
    bc1 = 1.0 - ADAM_B1 ** ADAM_STEP
    bc2 = 1.0 - ADAM_B2 ** ADAM_STEP

    def adam_kern(p_ref, w_ref, m_ref, v_ref, g_ref, d_ref, nm_ref, nv_ref):
        g = p_ref[0].astype(F32)
        for j in range(1, NDEV):
            g = g + p_ref[j].astype(F32)
        m_new = ADAM_B1 * m_ref[...] + (1.0 - ADAM_B1) * g
        v_new = ADAM_B2 * v_ref[...] + (1.0 - ADAM_B2) * (g * g)
        g_ref[...] = g
        nm_ref[...] = m_new
        nv_ref[...] = v_new
        d_ref[...] = -ADAM_LR * ((m_new / bc1) / (jnp.sqrt(v_new / bc2) + ADAM_EPS) + ADAM_WD * w_ref[...])

    blk = pl.BlockSpec((rb, cols), lambda i: (i, 0))
    return pl.pallas_call(
        adam_kern, name=name, grid=(rows // rb,),
        in_specs=[pl.BlockSpec((NDEV, rb, cols), lambda i: (0, i, 0)), blk, blk, blk],
        out_specs=[blk] * 4, out_shape=[jax.ShapeDtypeStruct((rows, cols), F32)] * 4,
        compiler_params=_cparams(("parallel",)),
    )(parts, w_a, m_a, v_a)


class _Cfg:
    pass


def _config(x, conv_w, w_uq, w_ukv, mla_qn_nope_g, mla_qn_rope_g, mem, mem_qn_g, w_mem_out, w_mla_out):
    c = _Cfg()
    c.N, c.D = x.shape[1], x.shape[2]
    c.CW = conv_w.shape[2] * NDEV
    c.QL, c.KVL = w_uq.shape[1], w_ukv.shape[1]
    c.NOPE, c.ROPE = mla_qn_nope_g.shape[1], mla_qn_rope_g.shape[1]
    c.H = w_uq.shape[2] * NDEV // (c.NOPE + c.ROPE)
    c.V = w_ukv.shape[2] * NDEV // c.H - c.NOPE
    assert c.NOPE == LANES and c.V == LANES and c.ROPE == LANES // 2
    c.HW = 2 * LANES
    c.HV = c.H * c.V
    assert w_mla_out.shape[1] * NDEV == c.HV
    c.M = mem.shape[1]
    c.MHD = mem_qn_g.shape[1]
    c.MW = w_mem_out.shape[1]
    c.MH = c.MW // c.MHD
    c.o_conv = 0
    c.o_mz = 4 * c.CW
    c.o_g = c.o_mz + c.HV
    c.o_mem = c.o_g + 3 * c.D
    c.o_lora = c.o_mem + 2 * c.MW
    c.P = c.o_lora + c.QL + c.KVL
    assert c.o_mz % c.HV == 0 and c.o_g % (3 * c.D) == 0 and c.o_mem % (2 * c.MW) == 0
    assert c.o_lora % (c.QL + c.KVL) == 0 and c.QL % LANES == 0 and c.KVL % LANES == 0
    c.IN = 4 * c.CW + c.QL + c.KVL + c.ROPE + c.HV + 2 * c.MW + 3 * c.D
    c.R = _tile(c.N, 256, 16)
    c.RP = _tile(c.N, 512, 16)
    c.HG = _tile(c.H, 4, 1)
    c.B = _tile(c.N, ATT_BLOCK, CHUNK)
    c.scale = float((c.NOPE + c.ROPE) ** -0.5)
    c.mscale = float(c.MHD ** -0.5)
    return c


def _win_segments(c):
    ref_order = (('conv', 4 * c.CW), ('lora', c.QL + c.KVL), ('kr', c.ROPE), ('mz', c.HV), ('mem', 2 * c.MW), ('g', 3 * c.D))
    mine = {'conv': c.o_conv, 'mz': c.o_mz, 'g': c.o_g, 'mem': c.o_mem, 'lora': c.o_lora, 'kr': 0}
    segs, o = [], 0
    for nm, wd in ref_order:
        segs.append((nm, o, wd, mine[nm]))
        o += wd
    return segs


def _win_split(g_win, c):
    n8 = g_win.shape[2]

    def columns(a, wd):
        return [g_win[j][:, max(a, j * n8) - j * n8:min(a + wd, (j + 1) * n8) - j * n8]
                for j in range(a // n8, (a + wd - 1) // n8 + 1)]

    segs = {nm: (a, wd) for nm, a, wd, _ in _win_segments(c)}
    main = [p for nm in ('conv', 'mz', 'g', 'mem', 'lora') for p in columns(*segs[nm])]
    kr = jnp.concatenate(columns(*segs['kr']) + [jnp.zeros((g_win.shape[1], LANES - c.ROPE), g_win.dtype)], axis=1)
    return jnp.concatenate(main, axis=1), kr


def _win_blocks(g, g_kr, c):
    n8 = c.IN // NDEV
    blocks = []
    for j in range(NDEV):
        lo, hi = j * n8, (j + 1) * n8
        parts = []
        for nm, a, wd, mine in _win_segments(c):
            s, e = max(a, lo), min(a + wd, hi)
            if s < e:
                parts.append((g_kr if nm == 'kr' else g)[:, mine + s - a:mine + e - a])
        blocks.append(jnp.concatenate(parts, axis=1))
    return jnp.stack(blocks, axis=0)


def kernel(x, positions, mem, norm_g, w_in, conv_w, w_conv_out, mla_q_norm_g, w_uq, mla_kv_norm_g, w_ukv, mla_qn_nope_g, mla_qn_rope_g, mla_kn_nope_g, mla_kn_rope_g, w_mla_out, mem_norm_g, w_mem_kv, mem_qn_g, mem_kn_g, w_mem_out, w_o, loss_target, m_norm_g, m_w_in, m_conv_w, m_w_conv_out, m_mla_q_norm_g, m_w_uq, m_mla_kv_norm_g, m_w_ukv, m_mla_qn_nope_g, m_mla_qn_rope_g, m_mla_kn_nope_g, m_mla_kn_rope_g, m_w_mla_out, m_mem_norm_g, m_w_mem_kv, m_mem_qn_g, m_mem_kn_g, m_w_mem_out, m_w_o, v_norm_g, v_w_in, v_conv_w, v_w_conv_out, v_mla_q_norm_g, v_w_uq, v_mla_kv_norm_g, v_w_ukv, v_mla_qn_nope_g, v_mla_qn_rope_g, v_mla_kn_nope_g, v_mla_kn_rope_g, v_w_mla_out, v_mem_norm_g, v_w_mem_kv, v_mem_qn_g, v_mem_kn_g, v_w_mem_out, v_w_o):
    args = dict(locals())
    W = {n: args[n] for n in WEIGHTS}
    Mo = {n: args['m_' + n] for n in WEIGHTS}
    Vo = {n: args['v_' + n] for n in WEIGHTS}
    c = _config(x, conv_w, w_uq, w_ukv, mla_qn_nope_g, mla_qn_rope_g, mem, mem_qn_g, w_mem_out, w_mla_out)
    N, D, R, B, H = c.N, c.D, c.R, c.B, c.H
    assert x.shape[0] == 1
    xs = x[0]
    tgt = loss_target[0]
    memx = mem[0]
    me = 4 * lax.axis_index("x") + 2 * lax.axis_index("y") + lax.axis_index("c")
    nr = N // R

    g_win, g_taps = _all_gather([W['w_in'][0].astype(_BF), conv_w[0]], "ag_w_in")
    rest = [n for n in BIG if n != 'w_in']
    shards_r = [W[n][0].astype(_BF) for n in rest]
    lands_r = [lax.dynamic_update_index_in_dim(lax.empty((NDEV,) + s.shape, s.dtype), s[None], me, 0)
               for s in shards_r]
    ag_rest = _split_start(shards_r, lands_r, True, "ag_rest_start", after=g_win)
    win_p, w_kr = _win_split(g_win, c)
    w_krT = w_kr.T
    convw = jnp.transpose(g_taps, (1, 0, 2)).reshape(3, c.CW)
    convw8 = jnp.pad(convw, ((0, 5), (0, 0)))

    def rowb(width, cidx, rows=R):
        return pl.BlockSpec((rows, width), lambda i, _c=cidx: (i, _c))

    R2 = c.RP
    nr2 = N // R2
    RM = _tile(N, 1024, 16)

    def rowb2(width, cidx):
        return rowb(width, cidx, R2)

    def fullb(shape):
        nd = len(shape)
        return pl.BlockSpec(shape, lambda *_: (0,) * nd)

    def pad_lanes(g, w=LANES):
        return jnp.pad(g, ((0, 0), (0, w - g.shape[1])))

    def tabs_of(rows):
        return pl.BlockSpec((3, rows, LANES), lambda i, *_: (0, i, 0))

    half = c.ROPE // 2
    inv_freq = jnp.power(ROPE_THETA, -jnp.arange(half, dtype=F32) / half)
    invf = jnp.concatenate([inv_freq, inv_freq, jnp.zeros((LANES - c.ROPE,), F32)])[None, :]
    pos_col = positions[0].astype(F32).reshape(N, 1)

    def rope_tab_kern(pos_ref, invf_ref, o_ref):
        ang = pos_ref[...] * invf_ref[...]
        co, si = jnp.cos(ang), jnp.sin(ang)
        lane = lax.broadcasted_iota(jnp.int32, ang.shape, 1)
        o_ref[0] = jnp.where(lane < c.ROPE, co, 0.0)
        o_ref[1] = jnp.where(lane < half, -si, 0.0)
        o_ref[2] = jnp.where(jnp.logical_and(lane >= half, lane < c.ROPE), si, 0.0)

    tabs = pl.pallas_call(
        rope_tab_kern, name="rope_tab", grid=(nr,),
        in_specs=[pl.BlockSpec((R, 1), lambda i: (i, 0)), fullb((1, LANES))],
        out_specs=tabs_of(R),
        out_shape=jax.ShapeDtypeStruct((3, N, LANES), F32),
        compiler_params=_cparams(("parallel",)),
    )(pos_col, invf)

    def make_rms_kern():
        def rms_fwd_kern(x_ref, g_ref, o_ref):
            xh, _ = _rms(x_ref[...].astype(F32), x_ref.shape[-1])
            o_ref[...] = (xh * g_ref[...]).astype(o_ref.dtype)
        return rms_fwd_kern

    h = pl.pallas_call(
        make_rms_kern(), name="rms_x", grid=(nr2,),
        in_specs=[rowb2(D, 0), fullb((1, D))], out_specs=rowb2(D, 0),
        out_shape=jax.ShapeDtypeStruct((N, D), _BF), compiler_params=_cparams(("parallel",)),
    )(xs, norm_g)

    proj = _mm(h, win_p, name="mm_proj", out_dtype=_BF, after=ag_rest[4])
    kr_raw = _mm(h, w_kr, name="mm_kr", out_dtype=_BF)

    Wf = {}
    for n, g in zip(rest, _split_wait(ag_rest, True, proj, "ag_rest_wait")):
        if n in COL_SHARDED:
            Wf[n] = jnp.transpose(g, (1, 0, 2)).reshape(g.shape[1], -1)
        else:
            Wf[n] = g.reshape(-1, g.shape[2])
    wuq = Wf['w_uq'].reshape(c.QL, H, c.NOPE + c.ROPE)
    wuq_p = jnp.pad(wuq, ((0, 0), (0, 0), (0, c.HW - c.NOPE - c.ROPE))).reshape(c.QL, H * c.HW)
    wukv = Wf['w_ukv']
    wco, wmo, wmkv, wmemo, wo = Wf['w_conv_out'], Wf['w_mla_out'], Wf['w_mem_kv'], Wf['w_mem_out'], Wf['w_o']

    CW = c.CW
    conv_blk = c.o_conv // (4 * CW)
    HALO = 16
    rh = R // HALO

    def conv_parts(blk):
        blk = blk.astype(F32)
        return blk[:, 0:CW], blk[:, CW:2 * CW], blk[:, 2 * CW:3 * CW], blk[:, 3 * CW:4 * CW]

    def shifted(cu, prev, i):
        prev = jnp.where(i > 0, prev, 0.0)
        rid = lax.broadcasted_iota(jnp.int32, cu.shape, 0)
        last, last2 = prev[HALO - 1:HALO, :], prev[HALO - 2:HALO - 1, :]
        sh1 = jnp.where(rid == 0, last, pltpu.roll(cu, 1, 0))
        sh2 = jnp.where(rid == 0, last2, jnp.where(rid == 1, last, pltpu.roll(cu, 2, 0)))
        return sh1, sh2

    def conv_fwd_kern(p_ref, prev_ref, w_ref, o_ref):
        i = pl.program_id(0)
        cg, bg, u, z = conv_parts(p_ref[...])
        pc, _, pu, _ = conv_parts(prev_ref[...])
        cu = cg * u
        sh1, sh2 = shifted(cu, pc * pu, i)
        w = w_ref[...]
        conv = w[0:1, :] * sh2 + w[1:2, :] * sh1 + w[2:3, :] * cu
        o_ref[...] = (bg * conv * (z * _sig(z))).astype(o_ref.dtype)

    prev_spec = pl.BlockSpec((HALO, 4 * CW), lambda i: (jnp.maximum(i * rh - 1, 0), conv_blk))
    a_conv = pl.pallas_call(
        conv_fwd_kern, name="conv_fwd", grid=(nr,),
        in_specs=[rowb(4 * CW, conv_blk), prev_spec, fullb((8, CW))],
        out_specs=rowb(CW, 0), out_shape=jax.ShapeDtypeStruct((N, CW), _BF),
        compiler_params=_cparams(("parallel",)),
    )(proj, proj, convw8)
    o_conv = _mm(a_conv, wco, name="mm_oconv", out_dtype=_BF)

    QL, KVL, HW = c.QL, c.KVL, c.HW
    lora_blk = c.o_lora // (QL + KVL)

    def lora_fwd_kern(p_ref, gq_ref, gkv_ref, q_ref, kv_ref):
        blk = p_ref[...].astype(F32)
        qh, _ = _rms(blk[:, :QL], QL)
        kh, _ = _rms(blk[:, QL:], KVL)
        q_ref[...] = (qh * gq_ref[...]).astype(q_ref.dtype)
        kv_ref[...] = (kh * gkv_ref[...]).astype(kv_ref.dtype)

    cqn, ckvn = pl.pallas_call(
        lora_fwd_kern, name="lora_fwd", grid=(nr,),
        in_specs=[rowb(QL + KVL, lora_blk), fullb((1, QL)), fullb((1, KVL))],
        out_specs=[rowb(QL, 0), rowb(KVL, 0)],
        out_shape=[jax.ShapeDtypeStruct((N, QL), _BF), jax.ShapeDtypeStruct((N, KVL), _BF)],
        compiler_params=_cparams(("parallel",)),
    )(proj, mla_q_norm_g, mla_kv_norm_g)
    q_p = _mm(cqn, wuq_p, name="mm_q", out_dtype=_BF, bn=2048)
    kv = _mm(ckvn, wukv, name="mm_kv", out_dtype=_BF, bn=2048)

    g_qn, g_qr = mla_qn_nope_g, pad_lanes(mla_qn_rope_g)
    g_kn, g_kr = mla_kn_nope_g, pad_lanes(mla_kn_rope_g)

    def krope_fwd_kern(p_ref, t_ref, g_ref, o_ref):
        xh, _ = _rms128(p_ref[...].astype(F32), c.ROPE)
        o_ref[...] = _rope(xh * g_ref[...], t_ref[0], t_ref[1], t_ref[2]).astype(o_ref.dtype)

    k_rope = pl.pallas_call(
        krope_fwd_kern, name="krope_fwd", grid=(nr,),
        in_specs=[rowb(LANES, 0), tabs_of(R), fullb((1, LANES))],
        out_specs=rowb(LANES, 0), out_shape=jax.ShapeDtypeStruct((N, LANES), _BF),
        compiler_params=_cparams(("parallel",)),
    )(kr_raw, tabs, g_kr)

    RP, HG = c.RP, c.HG
    nrp, nhg = N // RP, H // HG
    heads_in = pl.BlockSpec((RP, HG * HW), lambda i, hg: (i, hg))
    heads_out = pl.BlockSpec((HG, RP, HW), lambda i, hg: (hg, i, 0))

    def q_prep_kern(q_ref, t_ref, gn_ref, gr_ref, o_ref):
        for g in range(HG):
            blk = q_ref[:, g * HW:(g + 1) * HW].astype(F32)
            nh, _ = _rms128(blk[:, :LANES], c.NOPE)
            rhat, _ = _rms128(blk[:, LANES:], c.ROPE)
            rot = _rope(rhat * gr_ref[...], t_ref[0], t_ref[1], t_ref[2])
            o_ref[g] = (jnp.concatenate([nh * gn_ref[...], rot], axis=1) * (c.scale * LOG2E)).astype(o_ref.dtype)

    q_cat = pl.pallas_call(
        q_prep_kern, name="q_prep", grid=(nrp, nhg),
        in_specs=[heads_in, tabs_of(RP), fullb((1, LANES)), fullb((1, LANES))],
        out_specs=heads_out, out_shape=jax.ShapeDtypeStruct((H, N, HW), _BF),
        compiler_params=_cparams(("parallel", "parallel")),
    )(q_p, tabs, g_qn, g_qr)

    def k_prep_kern(kv_ref, kr_ref, gn_ref, o_ref):
        for g in range(HG):
            kn, _ = _rms128(kv_ref[:, g * HW:g * HW + LANES].astype(F32), c.NOPE)
            o_ref[g] = jnp.concatenate([(kn * gn_ref[...]).astype(o_ref.dtype), kr_ref[...]], axis=1)

    k_cat = pl.pallas_call(
        k_prep_kern, name="k_prep", grid=(nrp, nhg),
        in_specs=[heads_in, pl.BlockSpec((RP, LANES), lambda i, hg: (i, 0)), fullb((1, LANES))],
        out_specs=heads_out, out_shape=jax.ShapeDtypeStruct((H, N, HW), _BF),
        compiler_params=_cparams(("parallel", "parallel")),
    )(kv, k_rope, g_kn)

    QB = ATT_QBLOCKS if N % (ATT_QBLOCKS * B) == 0 else 1
    BQ = QB * B
    nq = N // BQ
    assert CHUNK & (CHUNK - 1) == 0 and B % CHUNK == 0 and ATT_UNROLL % QB == 0

    def diag_mask(s, d):
        row = lax.broadcasted_iota(jnp.int32, s.shape, 0)
        col = lax.broadcasted_iota(jnp.int32, s.shape, 1) + d * B
        shift = CHUNK.bit_length() - 1
        allowed = jnp.right_shift(col, shift) <= jnp.right_shift(row, shift)
        return jnp.where(allowed, s, NEG)

    k_head = pl.BlockSpec((1, N, HW), lambda hh, i: (hh, 0, 0))
    v_head = pl.BlockSpec((N, LANES), lambda hh, i: (0, 2 * hh + 1))
    q_blk = pl.BlockSpec((1, BQ, HW), lambda hh, i: (hh, i, 0))
    o_blk = pl.BlockSpec((BQ, LANES), lambda hh, i: (i, hh))
    lse_blk = pl.BlockSpec((1, BQ, LANES), lambda hh, i: (hh, i, 0))

    def key_block_plan(i, run):
        def unrolled(u, carry):
            run(ATT_UNROLL * u, ATT_UNROLL, 0)
            return carry

        n_full = QB * i
        lax.fori_loop(0, n_full // ATT_UNROLL, unrolled, 0)
        for rem in range(0, ATT_UNROLL, QB):
            @pl.when(n_full % ATT_UNROLL == rem)
            def _(rem=rem):
                run(n_full - rem, rem + QB, QB)

    def attn_fwd_kern(q_ref, k_ref, v_ref, o_ref, lse_ref, m_sc, acc_sc, s_sc):
        i = pl.program_id(1)
        m_sc[...] = jnp.full(m_sc.shape, NEG, F32)
        acc_sc[...] = jnp.zeros(acc_sc.shape, F32)

        def rows_of(t):
            return pl.ds(pl.multiple_of(t * B, B), B)

        def scores(t, slot):
            s_sc[slot] = _dot_nt(q_ref[0], k_ref[0, rows_of(t), :])

        def softmax_pv(t, slot, diag):
            s = s_sc[slot]
            if diag is not None:
                s = diag_mask(s, diag)
            mt = s[:, 0:LANES]
            for cb in range(1, B // LANES):
                mt = jnp.maximum(mt, s[:, cb * LANES:(cb + 1) * LANES])
            m_prev = m_sc[...]
            m_new = jnp.maximum(m_prev, jnp.max(mt, axis=1, keepdims=True))
            alpha = jnp.exp2(m_prev - m_new)
            p = jnp.concatenate([jnp.exp2(s[:, cb * LANES:(cb + 1) * LANES] - m_new).astype(_BF)
                                 for cb in range(B // LANES)], axis=1)
            v_ones = jnp.concatenate([v_ref[rows_of(t), :], jnp.ones((B, LANES), _BF)], axis=1)
            acc_sc[...] = jnp.concatenate([alpha, alpha], axis=1) * acc_sc[...] + _dot(p, v_ones)
            m_sc[...] = m_new

        scores(0, 0)

        def run(first, count, n_diag):
            for u in range(count):
                if u + 1 < count or n_diag == 0:
                    scores(first + u + 1, (u + 1) % 2)
                softmax_pv(first + u, u % 2, u - (count - n_diag) if u >= count - n_diag else None)

        key_block_plan(i, run)
        acc = acc_sc[...]
        o_ref[...] = (acc[:, :LANES] / acc[:, LANES:]).astype(o_ref.dtype)
        lse_ref[0] = m_sc[...] + jnp.log2(acc[:, LANES:])

    mla_y, lse = pl.pallas_call(
        attn_fwd_kern, name="attn_fwd", grid=(H, nq),
        in_specs=[q_blk, k_head, v_head], out_specs=[o_blk, lse_blk],
        out_shape=[jax.ShapeDtypeStruct((N, c.HV), _BF), jax.ShapeDtypeStruct((H, N, LANES), F32)],
        scratch_shapes=[pltpu.VMEM((BQ, LANES), F32), pltpu.VMEM((BQ, HW), F32), pltpu.VMEM((2, BQ, B), F32)],
        compiler_params=_cparams(("parallel", "arbitrary")),
    )(q_cat, k_cat, kv)

    HV = c.HV
    mz_blk = c.o_mz // HV

    def gate_fwd_kern(y_ref, z_ref, o_ref):
        z = z_ref[...].astype(F32)
        o_ref[...] = (y_ref[...].astype(F32) * (z * _sig(z))).astype(o_ref.dtype)

    a_mla = pl.pallas_call(
        gate_fwd_kern, name="gate_mla", grid=(nr2,),
        in_specs=[rowb2(HV, 0), rowb2(HV, mz_blk)], out_specs=rowb2(HV, 0),
        out_shape=jax.ShapeDtypeStruct((N, HV), _BF), compiler_params=_cparams(("parallel",)),
    )(mla_y, proj)
    o_mla = _mm(a_mla, wmo, name="mm_omla", out_dtype=_BF)

    M, MW, MH, MHD = c.M, c.MW, c.MH, c.MHD
    memn = pl.pallas_call(
        make_rms_kern(), name="rms_mem",
        grid=(1,), in_specs=[fullb((M, D)), fullb((1, D))], out_specs=fullb((M, D)),
        out_shape=jax.ShapeDtypeStruct((M, D), _BF), compiler_params=_cparams(("arbitrary",)),
    )(memx, mem_norm_g)
    kvm = _mm(memn, wmkv, name="mm_memkv", out_dtype=F32)

    def memk_fwd_kern(kv_ref, g_ref, k_ref, v_ref):
        for hh in range(MH):
            kh, _ = _rms(kv_ref[:, hh * MHD:(hh + 1) * MHD], MHD)
            k_ref[:, hh * MHD:(hh + 1) * MHD] = (kh * g_ref[...]).astype(k_ref.dtype)
        v_ref[...] = kv_ref[:, MW:].astype(v_ref.dtype)

    mem_k, mem_v = pl.pallas_call(
        memk_fwd_kern, name="memk_fwd", grid=(1,),
        in_specs=[fullb((M, 2 * MW)), fullb((1, MHD))], out_specs=[fullb((M, MW)), fullb((M, MW))],
        out_shape=[jax.ShapeDtypeStruct((M, MW), _BF)] * 2, compiler_params=_cparams(("arbitrary",)),
    )(kvm, mem_kn_g)

    mem_blk = c.o_mem // (2 * MW)

    def mem_head(qz_ref, k_ref, v_ref, g_ref, hh):
        sl = slice(hh * MHD, (hh + 1) * MHD)
        qh, r = _rms(qz_ref[:, sl].astype(F32), MHD)
        qn = (qh * g_ref[...]).astype(_BF)
        s = _dot_nt(qn, k_ref[:, sl]) * c.mscale
        e = jnp.exp(s - jnp.max(s, axis=1, keepdims=True))
        p = e / jnp.sum(e, axis=1, keepdims=True)
        y = _dot(p.astype(_BF), v_ref[:, sl])
        z = qz_ref[:, MW + hh * MHD:MW + (hh + 1) * MHD].astype(F32)
        return sl, qh, r, qn, p, y, z

    def mem_fwd_kern(qz_ref, k_ref, v_ref, g_ref, o_ref):
        for hh in range(MH):
            sl, _, _, _, _, y, z = mem_head(qz_ref, k_ref, v_ref, g_ref, hh)
            o_ref[:, sl] = (y * (z * _sig(z))).astype(o_ref.dtype)

    a_mem = pl.pallas_call(
        mem_fwd_kern, name="mem_fwd", grid=(N // RM,),
        in_specs=[rowb(2 * MW, mem_blk, RM), fullb((M, MW)), fullb((M, MW)), fullb((1, MHD))],
        out_specs=rowb(MW, 0, RM), out_shape=jax.ShapeDtypeStruct((N, MW), _BF),
        compiler_params=_cparams(("parallel",)),
    )(proj, mem_k, mem_v, mem_qn_g)
    o_mem = _mm(a_mem, wmemo, name="mm_omem", out_dtype=_BF)

    g_blk = c.o_g // (3 * D)

    def merge_fwd_kern(g_ref, oc_ref, om_ref, ome_ref, o_ref):
        g = g_ref[...].astype(F32)
        acc = _sig(g[:, :D]) * oc_ref[...].astype(F32)
        acc += _sig(g[:, D:2 * D]) * om_ref[...].astype(F32)
        acc += _sig(g[:, 2 * D:]) * ome_ref[...].astype(F32)
        o_ref[...] = acc.astype(o_ref.dtype)

    merged = pl.pallas_call(
        merge_fwd_kern, name="merge_fwd", grid=(nr,),
        in_specs=[rowb(3 * D, g_blk), rowb(D, 0), rowb(D, 0), rowb(D, 0)], out_specs=rowb(D, 0),
        out_shape=jax.ShapeDtypeStruct((N, D), _BF), compiler_params=_cparams(("parallel",)),
    )(proj, o_conv, o_mla, o_mem)
    obm, obn = _tile(N, 512, 16), _tile(D, 1024, LANES)

    def out_loss_kern(a_ref, w_ref, x_ref, t_ref, dy_ref, dyb_ref, l_ref):
        e = x_ref[...] + _dot(a_ref[...], w_ref[...]) - t_ref[...]
        dy = e * (1.0 / D)
        dy_ref[...] = dy
        dyb_ref[...] = dy.astype(dyb_ref.dtype)
        row = lax.broadcasted_iota(jnp.int32, l_ref.shape, 0)
        l_ref[...] = jnp.where(row == 0, jnp.sum(e * e, axis=0, keepdims=True), 0.0)

    oblk = pl.BlockSpec((obm, obn), lambda i, j: (i, j))
    dy, dyb, lpart = pl.pallas_call(
        out_loss_kern, name="mm_out_loss", grid=(N // obm, D // obn),
        in_specs=[pl.BlockSpec((obm, D), lambda i, j: (i, 0)), pl.BlockSpec((D, obn), lambda i, j: (0, j)), oblk, oblk],
        out_specs=[oblk, oblk, pl.BlockSpec((8, obn), lambda i, j: (i, j))],
        out_shape=[jax.ShapeDtypeStruct((N, D), F32), jax.ShapeDtypeStruct((N, D), _BF),
                   jax.ShapeDtypeStruct((8 * (N // obm), D), F32)],
        compiler_params=_cparams(("parallel", "parallel")),
    )(merged, wo, xs, tgt)
    loss = lax.psum(jnp.sum(lpart) * (0.5 / D), AXES)

    G = {}
    d_merged = _mm(dyb, wo, tb=True, name="mm_dmerged", out_dtype=_BF)
    G['w_o'] = _mm(merged, dyb, ta=True, name="mm_dwo", out_dtype=_BF, bk=4096)

    dproj0 = lax.empty((N, c.P), _BF)
    any_spec = pl.BlockSpec(memory_space=pl.ANY)

    def merge_bwd_kern(dp_any, g_ref, dm_ref, oc_ref, om_ref, ome_ref, dg_ref, doc_ref, dom_ref, dome_ref):
        g = g_ref[...].astype(F32)
        dm = dm_ref[...].astype(F32)
        for idx, (o_in, d_out) in enumerate(((oc_ref, doc_ref), (om_ref, dom_ref), (ome_ref, dome_ref))):
            sg = _sig(g[:, idx * D:(idx + 1) * D])
            d_out[...] = (sg * dm).astype(d_out.dtype)
            dg_ref[:, idx * D:(idx + 1) * D] = (dm * o_in[...].astype(F32) * sg * (1.0 - sg)).astype(dg_ref.dtype)

    dproj1, d_oconv, d_omla, d_omem = pl.pallas_call(
        merge_bwd_kern, name="merge_bwd", grid=(nr,),
        in_specs=[any_spec, rowb(3 * D, g_blk), rowb(D, 0), rowb(D, 0), rowb(D, 0), rowb(D, 0)],
        out_specs=[rowb(3 * D, g_blk), rowb(D, 0), rowb(D, 0), rowb(D, 0)],
        out_shape=[jax.ShapeDtypeStruct((N, c.P), _BF)] + [jax.ShapeDtypeStruct((N, D), _BF)] * 3,
        input_output_aliases={0: 0}, compiler_params=_cparams(("parallel",)),
    )(dproj0, proj, d_merged, o_conv, o_mla, o_mem)

    G['w_conv_out'] = _mm(a_conv, d_oconv, ta=True, name="mm_dwco", out_dtype=_BF, bk=4096)
    d_aconv = _mm(d_oconv, wco, tb=True, name="mm_daconv", out_dtype=_BF)
    G['w_mla_out'] = _mm(a_mla, d_omla, ta=True, name="mm_dwmo", out_dtype=_BF, bk=4096)
    d_amla = _mm(d_omla, wmo, tb=True, name="mm_damla", out_dtype=_BF)
    G['w_mem_out'] = _mm(a_mem, d_omem, ta=True, name="mm_dwmemo", out_dtype=_BF, bk=4096)
    d_amem = _mm(d_omem, wmemo, tb=True, name="mm_damem", out_dtype=_BF)

    def conv_bwd_kern(dp_any, p_ref, prev_ref, next_ref, da_ref, dan_ref, w_ref, o_ref, dw_ref):
        i = pl.program_id(0)
        cg, bg, u, z = conv_parts(p_ref[...])
        pc, _, pu, _ = conv_parts(prev_ref[...])
        _, nbg, _, nz = conv_parts(next_ref[...])
        cu = cg * u
        sh1, sh2 = shifted(cu, pc * pu, i)
        w = w_ref[...]
        conv = w[0:1, :] * sh2 + w[1:2, :] * sh1 + w[2:3, :] * cu
        sg = _sig(z)
        sz = z * sg
        da = da_ref[...].astype(F32)
        dcy = da * sz
        d_z = da * (bg * conv) * (sg * (1.0 + z * (1.0 - sg)))
        d_b = dcy * conv
        dconv = dcy * bg
        dnext = dan_ref[...].astype(F32) * (nz * _sig(nz)) * nbg
        dnext = jnp.where(i < nr - 1, dnext, 0.0)
        rid = lax.broadcasted_iota(jnp.int32, cu.shape, 0)
        up1 = jnp.where(rid == R - 1, dnext[0:1, :], pltpu.roll(dconv, R - 1, 0))
        up2 = jnp.where(rid == R - 2, dnext[0:1, :], jnp.where(rid == R - 1, dnext[1:2, :], pltpu.roll(dconv, R - 2, 0)))
        dcu = w[2:3, :] * dconv + w[1:2, :] * up1 + w[0:1, :] * up2
        o_ref[:, 0:CW] = (dcu * u).astype(o_ref.dtype)
        o_ref[:, CW:2 * CW] = d_b.astype(o_ref.dtype)
        o_ref[:, 2 * CW:3 * CW] = (dcu * cg).astype(o_ref.dtype)
        o_ref[:, 3 * CW:4 * CW] = d_z.astype(o_ref.dtype)

        @pl.when(i == 0)
        def _():
            dw_ref[...] = jnp.zeros(dw_ref.shape, F32)

        dw_ref[0:1, :] += jnp.sum(dconv * sh2, axis=0, keepdims=True)
        dw_ref[1:2, :] += jnp.sum(dconv * sh1, axis=0, keepdims=True)
        dw_ref[2:3, :] += jnp.sum(dconv * cu, axis=0, keepdims=True)

    next_spec = pl.BlockSpec((HALO, 4 * CW), lambda i: (jnp.minimum((i + 1) * rh, N // HALO - 1), conv_blk))
    dan_spec = pl.BlockSpec((HALO, CW), lambda i: (jnp.minimum((i + 1) * rh, N // HALO - 1), 0))
    dproj2, g_convw = pl.pallas_call(
        conv_bwd_kern, name="conv_bwd", grid=(nr,),
        in_specs=[any_spec, rowb(4 * CW, conv_blk), prev_spec, next_spec, rowb(CW, 0), dan_spec, fullb((8, CW))],
        out_specs=[rowb(4 * CW, conv_blk), fullb((8, CW))],
        out_shape=[jax.ShapeDtypeStruct((N, c.P), _BF), jax.ShapeDtypeStruct((8, CW), F32)],
        input_output_aliases={0: 0}, compiler_params=_cparams(("arbitrary",)),
    )(dproj1, proj, proj, proj, d_aconv, d_aconv, convw8)

    def mem_bwd_kern(dp_any, qz_ref, da_ref, k_ref, v_ref, g_ref, o_ref, dk_ref, dv_ref, dg_ref):
        @pl.when(pl.program_id(0) == 0)
        def _():
            dk_ref[...] = jnp.zeros(dk_ref.shape, F32)
            dv_ref[...] = jnp.zeros(dv_ref.shape, F32)
            dg_ref[...] = jnp.zeros(dg_ref.shape, F32)

        for hh in range(MH):
            sl, qh, r, qn, p, y, z = mem_head(qz_ref, k_ref, v_ref, g_ref, hh)
            da = da_ref[:, sl].astype(F32)
            sg = _sig(z)
            dyh = da * (z * sg)
            o_ref[:, MW + hh * MHD:MW + (hh + 1) * MHD] = (da * y * (sg * (1.0 + z * (1.0 - sg)))).astype(o_ref.dtype)
            dyb_h = dyh.astype(_BF)
            dpm = _dot_nt(dyb_h, v_ref[:, sl])
            ds = (p * (dpm - jnp.sum(dpm * p, axis=1, keepdims=True)) * c.mscale).astype(_BF)
            dqn = _dot(ds, k_ref[:, sl])
            dk_ref[:, sl] += _dot_tn(ds, qn)
            dv_ref[:, sl] += _dot_tn(p.astype(_BF), dyb_h)
            dq, dgp = _rms_bwd(qh, r, g_ref[...], dqn, MHD)
            o_ref[:, sl] = dq.astype(o_ref.dtype)
            dg_ref[...] += jnp.sum(dgp, axis=0, keepdims=True)

    dproj3, d_memk, d_memv, g_mem_qn = pl.pallas_call(
        mem_bwd_kern, name="mem_bwd", grid=(N // RM,),
        in_specs=[any_spec, rowb(2 * MW, mem_blk, RM), rowb(MW, 0, RM), fullb((M, MW)), fullb((M, MW)), fullb((1, MHD))],
        out_specs=[rowb(2 * MW, mem_blk, RM), fullb((M, MW)), fullb((M, MW)), fullb((1, MHD))],
        out_shape=[jax.ShapeDtypeStruct((N, c.P), _BF), jax.ShapeDtypeStruct((M, MW), F32),
                   jax.ShapeDtypeStruct((M, MW), F32), jax.ShapeDtypeStruct((1, MHD), F32)],
        input_output_aliases={0: 0}, compiler_params=_cparams(("arbitrary",)),
    )(dproj2, proj, d_amem, mem_k, mem_v, mem_qn_g)

    def memk_bwd_kern(kv_ref, dk_ref, dv_ref, g_ref, o_ref, dg_ref):
        dg = jnp.zeros((1, MHD), F32)
        for hh in range(MH):
            sl = slice(hh * MHD, (hh + 1) * MHD)
            kh, r = _rms(kv_ref[:, sl], MHD)
            dkr, dgp = _rms_bwd(kh, r, g_ref[...], dk_ref[:, sl], MHD)
            o_ref[:, sl] = dkr.astype(o_ref.dtype)
            dg += jnp.sum(dgp, axis=0, keepdims=True)
        o_ref[:, MW:] = dv_ref[...].astype(o_ref.dtype)
        dg_ref[...] = dg

    d_kvm, g_mem_kn = pl.pallas_call(
        memk_bwd_kern, name="memk_bwd", grid=(1,),
        in_specs=[fullb((M, 2 * MW)), fullb((M, MW)), fullb((M, MW)), fullb((1, MHD))],
        out_specs=[fullb((M, 2 * MW)), fullb((1, MHD))],
        out_shape=[jax.ShapeDtypeStruct((M, 2 * MW), _BF), jax.ShapeDtypeStruct((1, MHD), F32)],
        compiler_params=_cparams(("arbitrary",)),
    )(kvm, d_memk, d_memv, mem_kn_g)
    G['w_mem_kv'] = _mm(memn, d_kvm, ta=True, name="mm_dwmkv", out_dtype=_BF)
    d_memn = _mm(d_kvm, wmkv, tb=True, name="mm_dmemn", out_dtype=F32)

    def memnorm_bwd_kern(x_ref, d_ref, dg_ref):
        xh, _ = _rms(x_ref[...], D)
        dg_ref[...] = jnp.sum(d_ref[...] * xh, axis=0, keepdims=True)

    g_mem_norm = pl.pallas_call(
        memnorm_bwd_kern, name="memnorm_bwd", grid=(1,),
        in_specs=[fullb((M, D)), fullb((M, D))], out_specs=fullb((1, D)),
        out_shape=jax.ShapeDtypeStruct((1, D), F32), compiler_params=_cparams(("arbitrary",)),
    )(memx, d_memn)

    def gate_bwd_kern(dp_any, da_ref, y_ref, z_ref, dy_ref, dz_ref):
        z = z_ref[...].astype(F32)
        da = da_ref[...].astype(F32)
        sg = _sig(z)
        dy_ref[...] = (da * (z * sg)).astype(dy_ref.dtype)
        dz_ref[...] = (da * y_ref[...].astype(F32) * (sg * (1.0 + z * (1.0 - sg)))).astype(dz_ref.dtype)

    d_mlay, dproj4 = pl.pallas_call(
        gate_bwd_kern, name="gate_mla_bwd", grid=(nr2,),
        in_specs=[any_spec, rowb2(HV, 0), rowb2(HV, 0), rowb2(HV, mz_blk)],
        out_specs=[rowb2(HV, 0), rowb2(HV, mz_blk)],
        out_shape=[jax.ShapeDtypeStruct((N, HV), _BF), jax.ShapeDtypeStruct((N, c.P), _BF)],
        input_output_aliases={0: 1}, compiler_params=_cparams(("parallel",)),
    )(dproj3, d_amla, mla_y, proj)

    def attn_bwd_kern(q_ref, k_ref, v_ref, o_ref, do_ref, lse_ref, dq_ref, dk_ref, dv_ref, dq_sc, dl_sc, dk_sc, dv_sc):
        i = pl.program_id(1)
        q, do = q_ref[0], do_ref[...]
        delta = jnp.sum(do.astype(F32) * o_ref[...].astype(F32), axis=1, keepdims=True)
        dl_sc[...] = jnp.broadcast_to(delta, dl_sc.shape)
        dq_sc[...] = jnp.zeros(dq_sc.shape, F32)

        def step(t, diag):
            rows = pl.ds(pl.multiple_of(t * B, B), B)
            k = k_ref[0, rows, :]
            s = _dot_nt(q, k)
            if diag is not None:
                s = diag_mask(s, diag)
            dpm = _dot_nt(do, v_ref[rows, :])
            lse_t, dl = lse_ref[0], dl_sc[...]
            ps, dss = [], []
            for cb in range(B // LANES):
                cols = slice(cb * LANES, (cb + 1) * LANES)
                p_cb = jnp.exp2(s[:, cols] - lse_t)
                ps.append(p_cb.astype(_BF))
                dss.append((p_cb * (dpm[:, cols] - dl)).astype(_BF))
            p, ds = jnp.concatenate(ps, axis=1), jnp.concatenate(dss, axis=1)
            dvp = _dot_tn(p, do)
            dkp = _dot_tn(ds, q)
            if diag is not None:
                dk_sc[rows, :] = dkp
                dv_sc[rows, :] = dvp
            else:
                dk_sc[rows, :] += dkp
                dv_sc[rows, :] += dvp
            dq_sc[...] += _dot(ds, k)

        def run(first, count, n_diag):
            for u in range(count):
                step(first + u, u - (count - n_diag) if u >= count - n_diag else None)

        key_block_plan(i, run)
        dq_ref[0] = dq_sc[...].astype(dq_ref.dtype)

        @pl.when(i == nq - 1)
        def _():
            dk_ref[0] = dk_sc[...].astype(dk_ref.dtype)
            dv_ref[0] = dv_sc[...].astype(dv_ref.dtype)

    d_qcat, d_kcat, d_v = pl.pallas_call(
        attn_bwd_kern, name="attn_bwd", grid=(H, nq),
        in_specs=[q_blk, k_head, v_head, o_blk, o_blk, lse_blk],
        out_specs=[pl.BlockSpec((1, BQ, HW), lambda hh, i: (hh, i, 0)),
                   pl.BlockSpec((1, N, HW), lambda hh, i: (hh, 0, 0)),
                   pl.BlockSpec((1, N, LANES), lambda hh, i: (hh, 0, 0))],
        out_shape=[jax.ShapeDtypeStruct((H, N, HW), _BF), jax.ShapeDtypeStruct((H, N, HW), _BF),
                   jax.ShapeDtypeStruct((H, N, LANES), _BF)],
        scratch_shapes=[pltpu.VMEM((BQ, HW), F32), pltpu.VMEM((BQ, LANES), F32), pltpu.VMEM((N, HW), F32),
                        pltpu.VMEM((N, LANES), F32)],
        compiler_params=_cparams(("parallel", "arbitrary")),
    )(q_cat, k_cat, kv, mla_y, d_mlay, lse)

    def q_prep_bwd_kern(q_ref, dq_ref, t_ref, gn_ref, gr_ref, o_ref, dgn_ref, dgr_ref):
        @pl.when(jnp.logical_and(pl.program_id(0) == 0, pl.program_id(1) == 0))
        def _():
            dgn_ref[...] = jnp.zeros(dgn_ref.shape, F32)
            dgr_ref[...] = jnp.zeros(dgr_ref.shape, F32)

        for g in range(HG):
            blk = q_ref[:, g * HW:(g + 1) * HW].astype(F32)
            d = dq_ref[g].astype(F32) * c.scale
            nh, rn = _rms128(blk[:, :LANES], c.NOPE)
            rhat, rr = _rms128(blk[:, LANES:], c.ROPE)
            dn, dgn = _rms128_bwd(nh, rn, gn_ref[...], d[:, :LANES], c.NOPE)
            drot = _rope_t(d[:, LANES:], t_ref[0], t_ref[1], t_ref[2])
            dr, dgr = _rms128_bwd(rhat, rr, gr_ref[...], drot, c.ROPE)
            o_ref[:, g * HW:(g + 1) * HW] = jnp.concatenate([dn, dr], axis=1).astype(o_ref.dtype)
            dgn_ref[...] += jnp.sum(dgn, axis=0, keepdims=True)
            dgr_ref[...] += jnp.sum(dgr, axis=0, keepdims=True)

    d_qp, g_qn_nope, g_qn_rope = pl.pallas_call(
        q_prep_bwd_kern, name="q_prep_bwd", grid=(nrp, nhg),
        in_specs=[heads_in, heads_out, tabs_of(RP), fullb((1, LANES)), fullb((1, LANES))],
        out_specs=[heads_in, fullb((1, LANES)), fullb((1, LANES))],
        out_shape=[jax.ShapeDtypeStruct((N, H * HW), _BF), jax.ShapeDtypeStruct((1, LANES), F32),
                   jax.ShapeDtypeStruct((1, LANES), F32)],
        compiler_params=_cparams(("arbitrary", "arbitrary")),
    )(q_p, d_qcat, tabs, g_qn, g_qr)

    def k_prep_bwd_kern(kv_ref, dk_ref, dv_ref, gn_ref, o_ref, dkr_ref, dgn_ref):
        hg = pl.program_id(1)

        @pl.when(jnp.logical_and(pl.program_id(0) == 0, hg == 0))
        def _():
            dgn_ref[...] = jnp.zeros(dgn_ref.shape, F32)

        @pl.when(hg == 0)
        def _():
            dkr_ref[...] = jnp.zeros(dkr_ref.shape, F32)

        dkr = jnp.zeros((RP, LANES), F32)
        for g in range(HG):
            dk = dk_ref[g].astype(F32) * (1.0 / LOG2E)
            kn, r = _rms128(kv_ref[:, g * HW:g * HW + LANES].astype(F32), c.NOPE)
            dkn, dgn = _rms128_bwd(kn, r, gn_ref[...], dk[:, :LANES], c.NOPE)
            o_ref[:, g * HW:(g + 1) * HW] = jnp.concatenate([dkn.astype(o_ref.dtype), dv_ref[g]], axis=1)
            dgn_ref[...] += jnp.sum(dgn, axis=0, keepdims=True)
            dkr += dk[:, LANES:]
        dkr_ref[...] += dkr

    d_kv, d_krsum, g_kn_nope = pl.pallas_call(
        k_prep_bwd_kern, name="k_prep_bwd", grid=(nrp, nhg),
        in_specs=[heads_in, heads_out, pl.BlockSpec((HG, RP, LANES), lambda i, hg: (hg, i, 0)), fullb((1, LANES))],
        out_specs=[heads_in, pl.BlockSpec((RP, LANES), lambda i, hg: (i, 0)), fullb((1, LANES))],
        out_shape=[jax.ShapeDtypeStruct((N, H * HW), _BF), jax.ShapeDtypeStruct((N, LANES), F32),
                   jax.ShapeDtypeStruct((1, LANES), F32)],
        compiler_params=_cparams(("arbitrary", "arbitrary")),
    )(kv, d_kcat, d_v, g_kn)

    def krope_bwd_kern(p_ref, d_ref, t_ref, g_ref, o_ref, dg_ref):
        @pl.when(pl.program_id(0) == 0)
        def _():
            dg_ref[...] = jnp.zeros(dg_ref.shape, F32)

        xh, r = _rms128(p_ref[...].astype(F32), c.ROPE)
        drot = _rope_t(d_ref[...], t_ref[0], t_ref[1], t_ref[2])
        dx, dg = _rms128_bwd(xh, r, g_ref[...], drot, c.ROPE)
        o_ref[...] = dx.astype(o_ref.dtype)
        dg_ref[...] += jnp.sum(dg, axis=0, keepdims=True)

    d_kr, g_kn_rope = pl.pallas_call(
        krope_bwd_kern, name="krope_bwd", grid=(nr,),
        in_specs=[rowb(LANES, 0), rowb(LANES, 0), tabs_of(R), fullb((1, LANES))],
        out_specs=[rowb(LANES, 0), fullb((1, LANES))],
        out_shape=[jax.ShapeDtypeStruct((N, LANES), _BF), jax.ShapeDtypeStruct((1, LANES), F32)],
        compiler_params=_cparams(("arbitrary",)),
    )(kr_raw, d_krsum, tabs, g_kr)
    dproj5 = dproj4

    g_wuq_p = _mm(cqn, d_qp, ta=True, name="mm_dwuq", out_dtype=_BF, bk=4096)
    G['w_uq'] = g_wuq_p.reshape(QL, H, HW)[:, :, :c.NOPE + c.ROPE].reshape(QL, H * (c.NOPE + c.ROPE))
    d_cqn = _mm(d_qp, wuq_p, tb=True, name="mm_dcqn", out_dtype=F32, bk=4096)
    G['w_ukv'] = _mm(ckvn, d_kv, ta=True, name="mm_dwukv", out_dtype=_BF, bk=4096)
    d_ckvn = _mm(d_kv, wukv, tb=True, name="mm_dckvn", out_dtype=F32, bk=4096)

    def to_blocks(n, g):
        if n in COL_SHARDED:
            return jnp.transpose(g.reshape(g.shape[0], NDEV, -1), (1, 0, 2))
        return g.reshape(NDEV, -1, g.shape[1])

    def landing(b):
        own = lax.dynamic_index_in_dim(b, me, 0, keepdims=True)
        return lax.dynamic_update_index_in_dim(lax.empty(b.shape, b.dtype), own, me, 0)

    early = [n for n in BIG if n != 'w_in']
    blocks_e = [to_blocks(n, G[n]) for n in early]
    xe = _split_start(blocks_e, [landing(b) for b in blocks_e], False, "xchg_early_start")
    gq_after = mla_q_norm_g + xe[4][0:1, 0:1]

    def lora_bwd_kern(dp_any, p_ref, dq_ref, dkv_ref, gq_ref, gkv_ref, o_ref, dgq_ref, dgkv_ref):
        @pl.when(pl.program_id(0) == 0)
        def _():
            dgq_ref[...] = jnp.zeros(dgq_ref.shape, F32)
            dgkv_ref[...] = jnp.zeros(dgkv_ref.shape, F32)

        blk = p_ref[...].astype(F32)
        qh, rq = _rms(blk[:, :QL], QL)
        kh, rk = _rms(blk[:, QL:], KVL)
        dq, dgq = _rms_bwd(qh, rq, gq_ref[...], dq_ref[...], QL)
        dk, dgk = _rms_bwd(kh, rk, gkv_ref[...], dkv_ref[...], KVL)
        o_ref[:, :QL] = dq.astype(o_ref.dtype)
        o_ref[:, QL:] = dk.astype(o_ref.dtype)
        dgq_ref[...] += jnp.sum(dgq, axis=0, keepdims=True)
        dgkv_ref[...] += jnp.sum(dgk, axis=0, keepdims=True)

    dproj6, g_q_norm, g_kv_norm = pl.pallas_call(
        lora_bwd_kern, name="lora_bwd", grid=(nr,),
        in_specs=[any_spec, rowb(QL + KVL, lora_blk), rowb(QL, 0), rowb(KVL, 0), fullb((1, QL)), fullb((1, KVL))],
        out_specs=[rowb(QL + KVL, lora_blk), fullb((1, QL)), fullb((1, KVL))],
        out_shape=[jax.ShapeDtypeStruct((N, c.P), _BF), jax.ShapeDtypeStruct((1, QL), F32),
                   jax.ShapeDtypeStruct((1, KVL), F32)],
        input_output_aliases={0: 0}, compiler_params=_cparams(("arbitrary",)),
    )(dproj5, proj, d_cqn, d_ckvn, gq_after, mla_kv_norm_g)

    g_win_p = _mm(h, dproj6, ta=True, name="mm_dwin", out_dtype=_BF, bk=4096)
    g_wkr = _mm(h, d_kr, ta=True, name="mm_dwkr", out_dtype=_BF, bk=4096)
    blocks_w = [_win_blocks(g_win_p, g_wkr, c)]
    xw = _split_start(blocks_w, [landing(b) for b in blocks_w], False, "xchg_win_start")
    d_h = _mm(dproj6, win_p, tb=True, name="mm_dh", out_dtype=F32, bk=3072, after=xw[4], plus=(d_kr, w_krT))

    def final_bwd_kern(x_ref, g_ref, dh_ref, dy_ref, gx_ref, dg_ref):
        @pl.when(pl.program_id(0) == 0)
        def _():
            dg_ref[...] = jnp.zeros(dg_ref.shape, F32)

        xh, r = _rms(x_ref[...], D)
        dx, dg = _rms_bwd(xh, r, g_ref[...], dh_ref[...], D)
        gx_ref[...] = dy_ref[...] + dx
        dg_ref[...] += jnp.sum(dg, axis=0, keepdims=True)

    grad_x, g_norm = pl.pallas_call(
        final_bwd_kern, name="final_bwd", grid=(nr2,),
        in_specs=[rowb2(D, 0), fullb((1, D)), rowb2(D, 0), rowb2(D, 0)],
        out_specs=[rowb2(D, 0), fullb((1, D))],
        out_shape=[jax.ShapeDtypeStruct((N, D), F32), jax.ShapeDtypeStruct((1, D), F32)],
        compiler_params=_cparams(("arbitrary",)),
    )(xs, norm_g, d_h, dy)

    recv_e = _split_wait(xe, False, grad_x, "xchg_early_wait")
    recv_w = _split_wait(xw, False, grad_x, "xchg_win_wait")
    res = [{}, {}, {}, {}]
    for n, parts in zip(['w_in'] + early, list(recv_w) + list(recv_e)):
        outs = _adam(parts, W[n][0], Mo[n][0], Vo[n][0], "adam_" + n)
        for k in range(4):
            res[k][n] = outs[k][None]

    small_g = {'norm_g': g_norm, 'mla_q_norm_g': g_q_norm, 'mla_kv_norm_g': g_kv_norm,
               'mla_qn_nope_g': g_qn_nope, 'mla_qn_rope_g': g_qn_rope[:, :c.ROPE], 'mla_kn_nope_g': g_kn_nope,
               'mla_kn_rope_g': g_kn_rope[:, :c.ROPE], 'mem_norm_g': g_mem_norm, 'mem_qn_g': g_mem_qn,
               'mem_kn_g': g_mem_kn}
    small_part = _pack([small_g[n] for n in SMALL] + [g_convw[0:3, :]], 0)
    small_all = _all_gather([small_part], "ag_small_grads")[0]
    small_shapes = [W[n].shape for n in SMALL]
    pieces = _unpack(small_all, small_shapes + [(3, CW)])
    cw8 = CW // NDEV
    conv_mine = lax.dynamic_slice_in_dim(pieces[-1].reshape(NDEV, 3, NDEV, cw8), me, 1, axis=2)[:, :, 0, :]
    sm_parts = _pack(pieces[:-1] + [conv_mine], 1)
    sm_names = SMALL + ['conv_w']
    sm_shapes = small_shapes + [(3, cw8)]
    w_sm = _pack([W[n] for n in SMALL] + [conv_w[0]], 0)
    m_sm = _pack([Mo[n] for n in SMALL] + [m_conv_w[0]], 0)
    v_sm = _pack([Vo[n] for n in SMALL] + [v_conv_w[0]], 0)
    outs_sm = [_unpack(o, sm_shapes) for o in _adam(sm_parts, w_sm, m_sm, v_sm, "adam_small")]
    for k in range(4):
        for n, a in zip(sm_names, outs_sm[k]):
            res[k][n] = a[None] if n == 'conv_w' else a
    return (loss, grad_x[None], *[res[0][n] for n in WEIGHTS], *[res[1][n] for n in WEIGHTS],
            *[res[2][n] for n in WEIGHTS], *[res[3][n] for n in WEIGHTS])
```

```python
import math

import jax
import jax.numpy as jnp
from jax import lax
from jax.experimental import pallas as pl
from jax.experimental.pallas import tpu as pltpu

F32 = jnp.float32
_BF = jnp.bfloat16
EPS = 1e-6
CHUNK = 64
ROPE_THETA = 10000.0
ADAM_LR, ADAM_B1, ADAM_B2, ADAM_EPS, ADAM_WD, ADAM_STEP = 0.001, 0.9, 0.999, 1e-08, 0.01, 10
NDEV = 8
AXES = ("x", "y", "c")
MESH = pl.DeviceIdType.MESH
LANES = 128
NEG = -1e30
LOG2E = math.log2(math.e)
V7X_VMEM_LIMIT = 56 * 1024 * 1024
PACK_C = 1024
ATT_BLOCK = 512
ATT_UNROLL = 4
ATT_QBLOCKS = 2
ADAM_BLOCK_ELEMS = 256 * 1024

WEIGHTS = ['norm_g', 'w_in', 'conv_w', 'w_conv_out', 'mla_q_norm_g', 'w_uq', 'mla_kv_norm_g', 'w_ukv',
           'mla_qn_nope_g', 'mla_qn_rope_g', 'mla_kn_nope_g', 'mla_kn_rope_g', 'w_mla_out', 'mem_norm_g',
           'w_mem_kv', 'mem_qn_g', 'mem_kn_g', 'w_mem_out', 'w_o']
BIG = ['w_in', 'w_conv_out', 'w_uq', 'w_ukv', 'w_mla_out', 'w_mem_kv', 'w_mem_out', 'w_o']
COL_SHARDED = ('w_in', 'w_conv_out', 'w_uq', 'w_ukv', 'w_mem_out')
SMALL = ['norm_g', 'mla_q_norm_g', 'mla_kv_norm_g', 'mla_qn_nope_g', 'mla_qn_rope_g', 'mla_kn_nope_g',
         'mla_kn_rope_g', 'mem_norm_g', 'mem_qn_g', 'mem_kn_g']


def _tile(dim, target, align):
    if dim <= target:
        return dim
    t = target - target % align
    while t > 0:
        if dim % t == 0:
            return t
        t -= align
    raise ValueError(f"no tile for {dim} {target} {align}")


def _cparams(sem):
    return pltpu.CompilerParams(dimension_semantics=sem, vmem_limit_bytes=V7X_VMEM_LIMIT)


def _sig(x):
    return 1.0 / (1.0 + jnp.exp(-x))


def _rms(x, n):
    r = lax.rsqrt(jnp.sum(x * x, axis=-1, keepdims=True) * (1.0 / n) + EPS)
    return x * r, r


def _rms_bwd(xhat, r, g, dy, n):
    dxh = dy * g
    dx = r * (dxh - xhat * (jnp.sum(dxh * xhat, axis=-1, keepdims=True) * (1.0 / n)))
    return dx, dy * xhat


def _rowsum128(x):
    return jnp.dot(x.astype(_BF), jnp.ones((LANES, LANES), _BF), preferred_element_type=F32)


def _rms128(x, n):
    r = lax.rsqrt(_rowsum128(x * x) * (1.0 / n) + EPS)
    return x * r, r


def _rms128_bwd(xhat, r, g, dy, n):
    dxh = dy * g
    dx = r * (dxh - xhat * (_rowsum128(dxh * xhat) * (1.0 / n)))
    return dx, dy * xhat


def _rope(x, cosp, sina, sinb):
    return x * cosp + pltpu.roll(x, 96, 1) * sina + pltpu.roll(x, 32, 1) * sinb


def _rope_t(d, cosp, sina, sinb):
    return d * cosp + pltpu.roll(d * sina, 32, 1) + pltpu.roll(d * sinb, 96, 1)


def _dot_nt(a, b):
    return lax.dot_general(a, b, (((1,), (1,)), ((), ())), preferred_element_type=F32)


def _dot_tn(a, b):
    return lax.dot_general(a, b, (((0,), (0,)), ((), ())), preferred_element_type=F32)


def _dot(a, b):
    return jnp.dot(a, b, preferred_element_type=F32)


def _all_gather(shards, name):
    na = len(shards)
    nc = 9
    halves = [s.shape[0] // 2 if s.shape[0] % 32 == 0 else None for s in shards]

    def body(*refs):
        x_refs, out_refs = refs[:na], refs[na:2 * na]
        send_sems, recv_sems, local_sems = refs[2 * na:]
        x, y, c = lax.axis_index("x"), lax.axis_index("y"), lax.axis_index("c")
        me, sib = (x, y, c), (x, y, 1 - c)
        px, py, pd = (1 - x, y, c), (x, 1 - y, c), (1 - x, 1 - y, c)

        def other_core(p):
            return (p[0], p[1], 1 - p[2])

        def rows(a, blk, part=None):
            r = out_refs[a].at[4 * blk[0] + 2 * blk[1] + blk[2]]
            if part is None or halves[a] is None:
                return r
            return r.at[pl.ds(0, halves[a])] if part == 0 else r.at[pl.ds(halves[a], halves[a])]

        def copy(a, k, blk, to, part=None, src=None):
            dst = rows(a, blk, part)
            return pltpu.make_async_remote_copy(
                src_ref=dst if src is None else src, dst_ref=dst,
                send_sem=send_sems.at[nc * a + k], recv_sem=recv_sems.at[nc * a + k],
                device_id=to, device_id_type=MESH)

        mine = [pltpu.make_async_copy(x_refs[a], rows(a, me), local_sems.at[a]) for a in range(na)]
        for cp in mine:
            cp.start()
        started = []
        for a in range(na):
            started += [copy(a, 0, me, px, src=x_refs[a]), copy(a, 1, me, py, src=x_refs[a]),
                        copy(a, 2, me, sib, src=x_refs[a])]
        for cp in started:
            cp.start()

        def forward(cp):
            cp.start()
            started.append(cp)

        for a in range(na):
            copy(a, 0, px, me).wait_recv()
            forward(copy(a, 3, px, py, part=0))
            forward(copy(a, 4, px, sib))
        for a in range(na):
            copy(a, 1, py, me).wait_recv()
            if halves[a] is not None:
                forward(copy(a, 5, py, px, part=1))
            forward(copy(a, 6, py, sib))
        for a in range(na):
            copy(a, 3, pd, me, part=0).wait_recv()
            forward(copy(a, 7, pd, sib, part=0))
            if halves[a] is not None:
                copy(a, 5, pd, me, part=1).wait_recv()
                forward(copy(a, 8, pd, sib, part=1))
        for a in range(na):
            copy(a, 2, sib, me).wait_recv()
            copy(a, 4, other_core(px), me).wait_recv()
            copy(a, 6, other_core(py), me).wait_recv()
            copy(a, 7, other_core(pd), me, part=0).wait_recv()
            if halves[a] is not None:
                copy(a, 8, other_core(pd), me, part=1).wait_recv()
        for cp in started:
            cp.wait_send()
        for cp in mine:
            cp.wait()

    any_spec = pl.BlockSpec(memory_space=pl.ANY)
    return pl.pallas_call(
        body, name=name,
        out_shape=[jax.ShapeDtypeStruct((NDEV,) + s.shape, s.dtype) for s in shards],
        in_specs=[any_spec] * na, out_specs=[any_spec] * na,
        scratch_shapes=[pltpu.SemaphoreType.DMA((nc * na,)), pltpu.SemaphoreType.DMA((nc * na,)),
                        pltpu.SemaphoreType.DMA((na,))],
    )(*shards)


_HBM = pl.BlockSpec(memory_space=pltpu.HBM)
_SEM = pl.BlockSpec(memory_space=pltpu.SEMAPHORE)
_EFFECT = pltpu.SideEffectType.DATAFLOW_SIDE_EFFECTING


def _split_copy(a, k, src_refs, land_refs, send_sems, recv_sems, gather, receive_side):
    x, y, c = lax.axis_index("x"), lax.axis_index("y"), lax.axis_index("c")
    me = 4 * x + 2 * y + c
    tx, ty, tc = x ^ ((k + 1) >> 2 & 1), y ^ ((k + 1) >> 1 & 1), c ^ ((k + 1) & 1)
    peer = 4 * tx + 2 * ty + tc
    return pltpu.make_async_remote_copy(
        src_ref=src_refs[a] if gather else src_refs[a].at[peer],
        dst_ref=land_refs[a].at[peer if receive_side else me],
        send_sem=send_sems.at[7 * a + k], recv_sem=recv_sems.at[7 * a + k],
        device_id=(tx, ty, tc), device_id_type=MESH)


def _split_start(srcs, lands, gather, name, after=None):
    na = len(srcs)
    extra = [] if after is None else [after]

    def body(*refs):
        src_refs, land_refs = refs[:na], refs[na:2 * na]
        send_sems, recv_sems = refs[2 * na + len(extra)], refs[2 * na + len(extra) + 1]
        token = refs[-1]
        for k in range(7):
            for a in range(na):
                _split_copy(a, k, src_refs, land_refs, send_sems, recv_sems, gather, False).start()
        token[...] = jnp.zeros_like(token)

    hbm = [pltpu.HBM(b.shape, b.dtype) for b in list(srcs) + list(lands)]
    outs = pl.pallas_call(
        body, name=name,
        out_shape=(pltpu.SemaphoreType.DMA((7 * na,)), pltpu.SemaphoreType.DMA((7 * na,)), *hbm,
                   jax.ShapeDtypeStruct((8, LANES), F32)),
        in_specs=[_HBM] * (2 * na) + [pl.BlockSpec(memory_space=pl.ANY)] * len(extra),
        out_specs=(_SEM, _SEM, *[_HBM] * (2 * na), pl.BlockSpec(memory_space=pltpu.VMEM)),
        input_output_aliases={j: 2 + j for j in range(2 * na)},
        compiler_params=pltpu.CompilerParams(has_side_effects=_EFFECT),
    )(*[pltpu.with_memory_space_constraint(b, pltpu.HBM) for b in srcs],
      *[pltpu.with_memory_space_constraint(l, pltpu.HBM) for l in lands], *extra)
    return outs[0], outs[1], outs[2:2 + na], outs[2 + na:2 + 2 * na], outs[-1]


def _split_wait(started, gather, after, name):
    send_sems, recv_sems, srcs, lands, _ = started
    na = len(srcs)

    def body(*refs):
        src_refs, land_refs = refs[:na], refs[na:2 * na]
        send_s, recv_s = refs[2 * na], refs[2 * na + 1]
        for k in range(7):
            for a in range(na):
                cp = _split_copy(a, k, src_refs, land_refs, send_s, recv_s, gather, True)
                cp.wait_send()
                cp.wait_recv()

    hbm = [pltpu.HBM(b.shape, b.dtype) for b in list(srcs) + list(lands)]
    outs = pl.pallas_call(
        body, name=name, out_shape=tuple(hbm),
        in_specs=[_HBM] * (2 * na) + [_SEM, _SEM, pl.BlockSpec(memory_space=pl.ANY)],
        out_specs=tuple([_HBM] * (2 * na)),
        input_output_aliases={j: j for j in range(2 * na)},
        compiler_params=pltpu.CompilerParams(has_side_effects=_EFFECT),
    )(*srcs, *lands, send_sems, recv_sems, after)
    return outs[na:]


def _seg_rows(size):
    rows = -(-size // PACK_C)
    return -(-rows // 16) * 16


def _pack(arrs, lead):
    parts = []
    for a in arrs:
        lshape = a.shape[:lead]
        f = a.reshape(lshape + (-1,)).astype(F32)
        rows = _seg_rows(f.shape[-1])
        f = jnp.pad(f, [(0, 0)] * lead + [(0, rows * PACK_C - f.shape[-1])])
        parts.append(f.reshape(lshape + (rows, PACK_C)))
    return jnp.concatenate(parts, axis=lead)


def _unpack(buf, shapes):
    lshape = buf.shape[:-2]
    out, r = [], 0
    for shp in shapes:
        size = math.prod(shp)
        rows = _seg_rows(size)
        seg = buf[..., r:r + rows, :].reshape(lshape + (rows * PACK_C,))[..., :size]
        out.append(seg.reshape(lshape + tuple(shp)))
        r += rows
    return out


def _mm(a, b, *, name, out_dtype, ta=False, tb=False, bm=1024, bn=1024, bk=2048, after=None, plus=None):
    if ta:
        kdim, m = a.shape
    else:
        m, kdim = a.shape
    n, k2 = b.shape if tb else b.shape[::-1]
    assert kdim == k2 and not (ta and tb), (a.shape, b.shape)
    bm = _tile(m, bm, LANES if ta else 16)
    bn = _tile(n, bn, LANES)
    bk = _tile(kdim, bk, LANES)
    nk = kdim // bk
    n_after = 0 if after is None else 1
    n_plus = 0 if plus is None else 2

    def kern(a_ref, b_ref, *rest):
        plus_refs = rest[n_after:n_after + n_plus]
        o_ref, scratch = rest[n_after + n_plus], rest[n_after + n_plus + 1:]
        part = (_dot_tn if ta else _dot_nt if tb else _dot)(a_ref[...], b_ref[...])

        def first(p):
            return p + _dot(plus_refs[0][...], plus_refs[1][...]) if plus is not None else p

        if nk == 1:
            o_ref[...] = first(part).astype(o_ref.dtype)
        else:
            acc = scratch[0] if scratch else o_ref
            k = pl.program_id(2)

            @pl.when(k == 0)
            def _():
                acc[...] = first(jnp.zeros(acc.shape, F32))

            acc[...] += part
            if scratch:
                @pl.when(k == nk - 1)
                def _():
                    o_ref[...] = acc[...].astype(o_ref.dtype)

    a_spec = pl.BlockSpec((bk, bm), lambda i, j, k: (k, i)) if ta else pl.BlockSpec((bm, bk), lambda i, j, k: (i, k))
    b_spec = pl.BlockSpec((bn, bk), lambda i, j, k: (j, k)) if tb else pl.BlockSpec((bk, bn), lambda i, j, k: (k, j))
    extra_specs, extra_args = [], []
    if after is not None:
        extra_specs.append(pl.BlockSpec(after.shape, lambda i, j, k: (0, 0)))
        extra_args.append(after)
    if plus is not None:
        kk = plus[0].shape[1]
        extra_specs += [pl.BlockSpec((bm, kk), lambda i, j, k: (i, 0)), pl.BlockSpec((kk, bn), lambda i, j, k: (0, j))]
        extra_args += list(plus)
    return pl.pallas_call(
        kern, name=name, grid=(m // bm, n // bn, nk),
        in_specs=[a_spec, b_spec] + extra_specs,
        out_specs=pl.BlockSpec((bm, bn), lambda i, j, k: (i, j)),
        out_shape=jax.ShapeDtypeStruct((m, n), out_dtype),
        scratch_shapes=[pltpu.VMEM((bm, bn), F32)] if nk > 1 and out_dtype != F32 else [],
        compiler_params=_cparams(("parallel", "parallel", "arbitrary")),
    )(a, b, *extra_args)


def _adam(parts, w_a, m_a, v_a, name):
    rows, cols = w_a.shape
    rb = _tile(rows, max(8, ADAM_BLOCK_ELEMS // cols // 8 * 8), 8)
    bc1 = 1.0 - ADAM_B1 ** ADAM_STEP
    bc2 = 1.0 - ADAM_B2 ** ADAM_STEP

    def adam_kern(p_ref, w_ref, m_ref, v_ref, g_ref, d_ref, nm_ref, nv_ref):
        g = p_ref[0].astype(F32)
        for j in range(1, NDEV):
            g = g + p_ref[j].astype(F32)
        m_new = ADAM_B1 * m_ref[...] + (1.0 - ADAM_B1) * g
        v_new = ADAM_B2 * v_ref[...] + (1.0 - ADAM_B2) * (g * g)
        g_ref[...] = g
        nm_ref[...] = m_new
        nv_ref[...] = v_new
        d_ref[...] = -ADAM_LR * ((m_new / bc1) / (jnp.sqrt(v_new / bc2) + ADAM_EPS) + ADAM_WD * w_ref[...])

    blk = pl.BlockSpec((rb, cols), lambda i: (i, 0))
    return pl.pallas_call(
        adam_kern, name=name, grid=(rows // rb,),
        in_specs=[pl.BlockSpec((NDEV, rb, cols), lambda i: (0, i, 0)), blk, blk, blk],
        out_specs=[blk] * 4, out_shape=[jax.ShapeDtypeStruct((rows, cols), F32)] * 4,
        compiler_params=_cparams(("parallel",)),
    )(parts, w_a, m_a, v_a)


class _Cfg:
    pass


def _config(x, conv_w, w_uq, w_ukv, mla_qn_nope_g, mla_qn_rope_g, mem, mem_qn_g, w_mem_out, w_mla_out):
    c = _Cfg()
    c.N, c.D = x.shape[1], x.shape[2]
    c.CW = conv_w.shape[2] * NDEV
    c.QL, c.KVL = w_uq.shape[1], w_ukv.shape[1]
    c.NOPE, c.ROPE = mla_qn_nope_g.shape[1], mla_qn_rope_g.shape[1]
    c.H = w_uq.shape[2] * NDEV // (c.NOPE + c.ROPE)
    c.V = w_ukv.shape[2] * NDEV // c.H - c.NOPE
    assert c.NOPE == LANES and c.V == LANES and c.ROPE == LANES // 2
    c.HW = 2 * LANES
    c.HV = c.H * c.V
    assert w_mla_out.shape[1] * NDEV == c.HV
    c.M = mem.shape[1]
    c.MHD = mem_qn_g.shape[1]
    c.MW = w_mem_out.shape[1]
    c.MH = c.MW // c.MHD
    c.o_conv = 0
    c.o_mz = 4 * c.CW
    c.o_g = c.o_mz + c.HV
    c.o_mem = c.o_g + 3 * c.D
    c.o_lora = c.o_mem + 2 * c.MW
    c.P = c.o_lora + c.QL + c.KVL
    assert c.o_mz % c.HV == 0 and c.o_g % (3 * c.D) == 0 and c.o_mem % (2 * c.MW) == 0
    assert c.o_lora % (c.QL + c.KVL) == 0 and c.QL % LANES == 0 and c.KVL % LANES == 0
    c.IN = 4 * c.CW + c.QL + c.KVL + c.ROPE + c.HV + 2 * c.MW + 3 * c.D
    c.R = _tile(c.N, 256, 16)
    c.RP = _tile(c.N, 512, 16)
    c.HG = _tile(c.H, 4, 1)
    c.B = _tile(c.N, ATT_BLOCK, CHUNK)
    c.scale = float((c.NOPE + c.ROPE) ** -0.5)
    c.mscale = float(c.MHD ** -0.5)
    return c


def _win_segments(c):
    ref_order = (('conv', 4 * c.CW), ('lora', c.QL + c.KVL), ('kr', c.ROPE), ('mz', c.HV), ('mem', 2 * c.MW), ('g', 3 * c.D))
    mine = {'conv': c.o_conv, 'mz': c.o_mz, 'g': c.o_g, 'mem': c.o_mem, 'lora': c.o_lora, 'kr': 0}
    segs, o = [], 0
    for nm, wd in ref_order:
        segs.append((nm, o, wd, mine[nm]))
        o += wd
    return segs


def _win_split(g_win, c):
    n8 = g_win.shape[2]

    def columns(a, wd):
        return [g_win[j][:, max(a, j * n8) - j * n8:min(a + wd, (j + 1) * n8) - j * n8]
                for j in range(a // n8, (a + wd - 1) // n8 + 1)]

    segs = {nm: (a, wd) for nm, a, wd, _ in _win_segments(c)}
    main = [p for nm in ('conv', 'mz', 'g', 'mem', 'lora') for p in columns(*segs[nm])]
    kr = jnp.concatenate(columns(*segs['kr']) + [jnp.zeros((g_win.shape[1], LANES - c.ROPE), g_win.dtype)], axis=1)
    return jnp.concatenate(main, axis=1), kr


def _win_blocks(g, g_kr, c):
    n8 = c.IN // NDEV
    blocks = []
    for j in range(NDEV):
        lo, hi = j * n8, (j + 1) * n8
        parts = []
        for nm, a, wd, mine in _win_segments(c):
            s, e = max(a, lo), min(a + wd, hi)
            if s < e:
                parts.append((g_kr if nm == 'kr' else g)[:, mine + s - a:mine + e - a])
        blocks.append(jnp.concatenate(parts, axis=1))
    return jnp.stack(blocks, axis=0)


def kernel(x, positions, mem, norm_g, w_in, conv_w, w_conv_out, mla_q_norm_g, w_uq, mla_kv_norm_g, w_ukv, mla_qn_nope_g, mla_qn_rope_g, mla_kn_nope_g, mla_kn_rope_g, w_mla_out, mem_norm_g, w_mem_kv, mem_qn_g, mem_kn_g, w_mem_out, w_o, loss_target, m_norm_g, m_w_in, m_conv_w, m_w_conv_out, m_mla_q_norm_g, m_w_uq, m_mla_kv_norm_g, m_w_ukv, m_mla_qn_nope_g, m_mla_qn_rope_g, m_mla_kn_nope_g, m_mla_kn_rope_g, m_w_mla_out, m_mem_norm_g, m_w_mem_kv, m_mem_qn_g, m_mem_kn_g, m_w_mem_out, m_w_o, v_norm_g, v_w_in, v_conv_w, v_w_conv_out, v_mla_q_norm_g, v_w_uq, v_mla_kv_norm_g, v_w_ukv, v_mla_qn_nope_g, v_mla_qn_rope_g, v_mla_kn_nope_g, v_mla_kn_rope_g, v_w_mla_out, v_mem_norm_g, v_w_mem_kv, v_mem_qn_g, v_mem_kn_g, v_w_mem_out, v_w_o):
    args = dict(locals())
    W = {n: args[n] for n in WEIGHTS}
    Mo = {n: args['m_' + n] for n in WEIGHTS}
    Vo = {n: args['v_' + n] for n in WEIGHTS}
    c = _config(x, conv_w, w_uq, w_ukv, mla_qn_nope_g, mla_qn_rope_g, mem, mem_qn_g, w_mem_out, w_mla_out)
    N, D, R, B, H = c.N, c.D, c.R, c.B, c.H
    assert x.shape[0] == 1
    xs = x[0]
    tgt = loss_target[0]
    memx = mem[0]
    me = 4 * lax.axis_index("x") + 2 * lax.axis_index("y") + lax.axis_index("c")
    nr = N // R

    g_win, g_taps = _all_gather([W['w_in'][0].astype(_BF), conv_w[0]], "ag_w_in")
    rest = [n for n in BIG if n != 'w_in']
    shards_r = [W[n][0].astype(_BF) for n in rest]
    lands_r = [lax.dynamic_update_index_in_dim(lax.empty((NDEV,) + s.shape, s.dtype), s[None], me, 0)
               for s in shards_r]
    ag_rest = _split_start(shards_r, lands_r, True, "ag_rest_start", after=g_win)
    win_p, w_kr = _win_split(g_win, c)
    w_krT = w_kr.T
    convw = jnp.transpose(g_taps, (1, 0, 2)).reshape(3, c.CW)
    convw8 = jnp.pad(convw, ((0, 5), (0, 0)))

    def rowb(width, cidx, rows=R):
        return pl.BlockSpec((rows, width), lambda i, _c=cidx: (i, _c))

    R2 = c.RP
    nr2 = N // R2
    RM = _tile(N, 1024, 16)

    def rowb2(width, cidx):
        return rowb(width, cidx, R2)

    def fullb(shape):
        nd = len(shape)
        return pl.BlockSpec(shape, lambda *_: (0,) * nd)

    def pad_lanes(g, w=LANES):
        return jnp.pad(g, ((0, 0), (0, w - g.shape[1])))

    def tabs_of(rows):
        return pl.BlockSpec((3, rows, LANES), lambda i, *_: (0, i, 0))

    half = c.ROPE // 2
    inv_freq = jnp.power(ROPE_THETA, -jnp.arange(half, dtype=F32) / half)
    invf = jnp.concatenate([inv_freq, inv_freq, jnp.zeros((LANES - c.ROPE,), F32)])[None, :]
    pos_col = positions[0].astype(F32).reshape(N, 1)

    def rope_tab_kern(pos_ref, invf_ref, o_ref):
        ang = pos_ref[...] * invf_ref[...]
        co, si = jnp.cos(ang), jnp.sin(ang)
        lane = lax.broadcasted_iota(jnp.int32, ang.shape, 1)
        o_ref[0] = jnp.where(lane < c.ROPE, co, 0.0)
        o_ref[1] = jnp.where(lane < half, -si, 0.0)
        o_ref[2] = jnp.where(jnp.logical_and(lane >= half, lane < c.ROPE), si, 0.0)

    tabs = pl.pallas_call(
        rope_tab_kern, name="rope_tab", grid=(nr,),
        in_specs=[pl.BlockSpec((R, 1), lambda i: (i, 0)), fullb((1, LANES))],
        out_specs=tabs_of(R),
        out_shape=jax.ShapeDtypeStruct((3, N, LANES), F32),
        compiler_params=_cparams(("parallel",)),
    )(pos_col, invf)

    def make_rms_kern():
        def rms_fwd_kern(x_ref, g_ref, o_ref):
            xh, _ = _rms(x_ref[...].astype(F32), x_ref.shape[-1])
            o_ref[...] = (xh * g_ref[...]).astype(o_ref.dtype)
        return rms_fwd_kern

    h = pl.pallas_call(
        make_rms_kern(), name="rms_x", grid=(nr2,),
        in_specs=[rowb2(D, 0), fullb((1, D))], out_specs=rowb2(D, 0),
        out_shape=jax.ShapeDtypeStruct((N, D), _BF), compiler_params=_cparams(("parallel",)),
    )(xs, norm_g)

    proj = _mm(h, win_p, name="mm_proj", out_dtype=_BF, after=ag_rest[4])
    kr_raw = _mm(h, w_kr, name="mm_kr", out_dtype=_BF)

    Wf = {}
    for n, g in zip(rest, _split_wait(ag_rest, True, proj, "ag_rest_wait")):
        if n in COL_SHARDED:
            Wf[n] = jnp.transpose(g, (1, 0, 2)).reshape(g.shape[1], -1)
        else:
            Wf[n] = g.reshape(-1, g.shape[2])
    wuq = Wf['w_uq'].reshape(c.QL, H, c.NOPE + c.ROPE)
    wuq_p = jnp.pad(wuq, ((0, 0), (0, 0), (0, c.HW - c.NOPE - c.ROPE))).reshape(c.QL, H * c.HW)
    wukv = Wf['w_ukv']
    wco, wmo, wmkv, wmemo, wo = Wf['w_conv_out'], Wf['w_mla_out'], Wf['w_mem_kv'], Wf['w_mem_out'], Wf['w_o']

    CW = c.CW
    conv_blk = c.o_conv // (4 * CW)
    HALO = 16
    rh = R // HALO

    def conv_parts(blk):
        blk = blk.astype(F32)
        return blk[:, 0:CW], blk[:, CW:2 * CW], blk[:, 2 * CW:3 * CW], blk[:, 3 * CW:4 * CW]

    def shifted(cu, prev, i):
        prev = jnp.where(i > 0, prev, 0.0)
        rid = lax.broadcasted_iota(jnp.int32, cu.shape, 0)
        last, last2 = prev[HALO - 1:HALO, :], prev[HALO - 2:HALO - 1, :]
        sh1 = jnp.where(rid == 0, last, pltpu.roll(cu, 1, 0))
        sh2 = jnp.where(rid == 0, last2, jnp.where(rid == 1, last, pltpu.roll(cu, 2, 0)))
        return sh1, sh2

    def conv_fwd_kern(p_ref, prev_ref, w_ref, o_ref):
        i = pl.program_id(0)
        cg, bg, u, z = conv_parts(p_ref[...])
        pc, _, pu, _ = conv_parts(prev_ref[...])
        cu = cg * u
        sh1, sh2 = shifted(cu, pc * pu, i)
        w = w_ref[...]
        conv = w[0:1, :] * sh2 + w[1:2, :] * sh1 + w[2:3, :] * cu
        o_ref[...] = (bg * conv * (z * _sig(z))).astype(o_ref.dtype)

    prev_spec = pl.BlockSpec((HALO, 4 * CW), lambda i: (jnp.maximum(i * rh - 1, 0), conv_blk))
    a_conv = pl.pallas_call(
        conv_fwd_kern, name="conv_fwd", grid=(nr,),
        in_specs=[rowb(4 * CW, conv_blk), prev_spec, fullb((8, CW))],
        out_specs=rowb(CW, 0), out_shape=jax.ShapeDtypeStruct((N, CW), _BF),
        compiler_params=_cparams(("parallel",)),
    )(proj, proj, convw8)
    o_conv = _mm(a_conv, wco, name="mm_oconv", out_dtype=_BF)

    QL, KVL, HW = c.QL, c.KVL, c.HW
    lora_blk = c.o_lora // (QL + KVL)

    def lora_fwd_kern(p_ref, gq_ref, gkv_ref, q_ref, kv_ref):
        blk = p_ref[...].astype(F32)
        qh, _ = _rms(blk[:, :QL], QL)
        kh, _ = _rms(blk[:, QL:], KVL)
        q_ref[...] = (qh * gq_ref[...]).astype(q_ref.dtype)
        kv_ref[...] = (kh * gkv_ref[...]).astype(kv_ref.dtype)

    cqn, ckvn = pl.pallas_call(
        lora_fwd_kern, name="lora_fwd", grid=(nr,),
        in_specs=[rowb(QL + KVL, lora_blk), fullb((1, QL)), fullb((1, KVL))],
        out_specs=[rowb(QL, 0), rowb(KVL, 0)],
        out_shape=[jax.ShapeDtypeStruct((N, QL), _BF), jax.ShapeDtypeStruct((N, KVL), _BF)],
        compiler_params=_cparams(("parallel",)),
    )(proj, mla_q_norm_g, mla_kv_norm_g)
    q_p = _mm(cqn, wuq_p, name="mm_q", out_dtype=_BF, bn=2048)
    kv = _mm(ckvn, wukv, name="mm_kv", out_dtype=_BF, bn=2048)

    g_qn, g_qr = mla_qn_nope_g, pad_lanes(mla_qn_rope_g)
    g_kn, g_kr = mla_kn_nope_g, pad_lanes(mla_kn_rope_g)

    def krope_fwd_kern(p_ref, t_ref, g_ref, o_ref):
        xh, _ = _rms128(p_ref[...].astype(F32), c.ROPE)
        o_ref[...] = _rope(xh * g_ref[...], t_ref[0], t_ref[1], t_ref[2]).astype(o_ref.dtype)

    k_rope = pl.pallas_call(
        krope_fwd_kern, name="krope_fwd", grid=(nr,),
        in_specs=[rowb(LANES, 0), tabs_of(R), fullb((1, LANES))],
        out_specs=rowb(LANES, 0), out_shape=jax.ShapeDtypeStruct((N, LANES), _BF),
        compiler_params=_cparams(("parallel",)),
    )(kr_raw, tabs, g_kr)

    RP, HG = c.RP, c.HG
    nrp, nhg = N // RP, H // HG
    heads_in = pl.BlockSpec((RP, HG * HW), lambda i, hg: (i, hg))
    heads_out = pl.BlockSpec((HG, RP, HW), lambda i, hg: (hg, i, 0))

    def q_prep_kern(q_ref, t_ref, gn_ref, gr_ref, o_ref):
        for g in range(HG):
            blk = q_ref[:, g * HW:(g + 1) * HW].astype(F32)
            nh, _ = _rms128(blk[:, :LANES], c.NOPE)
            rhat, _ = _rms128(blk[:, LANES:], c.ROPE)
            rot = _rope(rhat * gr_ref[...], t_ref[0], t_ref[1], t_ref[2])
            o_ref[g] = (jnp.concatenate([nh * gn_ref[...], rot], axis=1) * (c.scale * LOG2E)).astype(o_ref.dtype)

    q_cat = pl.pallas_call(
        q_prep_kern, name="q_prep", grid=(nrp, nhg),
        in_specs=[heads_in, tabs_of(RP), fullb((1, LANES)), fullb((1, LANES))],
        out_specs=heads_out, out_shape=jax.ShapeDtypeStruct((H, N, HW), _BF),
        compiler_params=_cparams(("parallel", "parallel")),
    )(q_p, tabs, g_qn, g_qr)

    def k_prep_kern(kv_ref, kr_ref, gn_ref, o_ref):
        for g in range(HG):
            kn, _ = _rms128(kv_ref[:, g * HW:g * HW + LANES].astype(F32), c.NOPE)
            o_ref[g] = jnp.concatenate([(kn * gn_ref[...]).astype(o_ref.dtype), kr_ref[...]], axis=1)

    k_cat = pl.pallas_call(
        k_prep_kern, name="k_prep", grid=(nrp, nhg),
        in_specs=[heads_in, pl.BlockSpec((RP, LANES), lambda i, hg: (i, 0)), fullb((1, LANES))],
        out_specs=heads_out, out_shape=jax.ShapeDtypeStruct((H, N, HW), _BF),
        compiler_params=_cparams(("parallel", "parallel")),
    )(kv, k_rope, g_kn)

    QB = ATT_QBLOCKS if N % (ATT_QBLOCKS * B) == 0 else 1
    BQ = QB * B
    nq = N // BQ
    assert CHUNK & (CHUNK - 1) == 0 and B % CHUNK == 0 and ATT_UNROLL % QB == 0

    def live_rows(diag):
        return slice(0 if diag is None else diag * B, BQ)

    def diag_mask(s):
        row = lax.broadcasted_iota(jnp.int32, s.shape, 0)
        col = lax.broadcasted_iota(jnp.int32, s.shape, 1)
        shift = CHUNK.bit_length() - 1
        allowed = jnp.right_shift(col, shift) <= jnp.right_shift(row, shift)
        return jnp.where(allowed, s, NEG)

    k_head = pl.BlockSpec((1, N, HW), lambda hh, i: (hh, 0, 0))
    v_head = pl.BlockSpec((N, LANES), lambda hh, i: (0, 2 * hh + 1))
    q_blk = pl.BlockSpec((1, BQ, HW), lambda hh, i: (hh, i, 0))
    o_blk = pl.BlockSpec((BQ, LANES), lambda hh, i: (i, hh))
    lse_blk = pl.BlockSpec((1, BQ, LANES), lambda hh, i: (hh, i, 0))

    def key_block_plan(i, run):
        def unrolled(u, carry):
            run(ATT_UNROLL * u, ATT_UNROLL, 0)
            return carry

        n_full = QB * i
        lax.fori_loop(0, n_full // ATT_UNROLL, unrolled, 0)
        for rem in range(0, ATT_UNROLL, QB):
            @pl.when(n_full % ATT_UNROLL == rem)
            def _(rem=rem):
                run(n_full - rem, rem + QB, QB)

    def attn_fwd_kern(q_ref, k_ref, v_ref, o_ref, lse_ref, m_sc, acc_sc, s_sc):
        i = pl.program_id(1)
        m_sc[...] = jnp.full(m_sc.shape, NEG, F32)
        acc_sc[...] = jnp.zeros(acc_sc.shape, F32)

        def rows_of(t):
            return pl.ds(pl.multiple_of(t * B, B), B)

        def scores(t, slot, diag):
            rs = live_rows(diag)
            s_sc[slot, rs, :] = _dot_nt(q_ref[0, rs, :], k_ref[0, rows_of(t), :])

        def softmax_pv(t, slot, diag):
            rs = live_rows(diag)
            s = s_sc[slot, rs, :]
            if diag is not None:
                s = diag_mask(s)
            mt = s[:, 0:LANES]
            for cb in range(1, B // LANES):
                mt = jnp.maximum(mt, s[:, cb * LANES:(cb + 1) * LANES])
            m_prev = m_sc[rs, :]
            m_new = jnp.maximum(m_prev, jnp.max(mt, axis=1, keepdims=True))
            alpha = jnp.exp2(m_prev - m_new)
            p = jnp.concatenate([jnp.exp2(s[:, cb * LANES:(cb + 1) * LANES] - m_new).astype(_BF)
                                 for cb in range(B // LANES)], axis=1)
            v_ones = jnp.concatenate([v_ref[rows_of(t), :], jnp.ones((B, LANES), _BF)], axis=1)
            acc_sc[rs, :] = jnp.concatenate([alpha, alpha], axis=1) * acc_sc[rs, :] + _dot(p, v_ones)
            m_sc[rs, :] = m_new

        scores(0, 0, None)

        def run(first, count, n_diag):
            def diag_of(u):
                return u - (count - n_diag) if count - n_diag <= u < count else None

            for u in range(count):
                if u + 1 < count or n_diag == 0:
                    scores(first + u + 1, (u + 1) % 2, diag_of(u + 1))
                softmax_pv(first + u, u % 2, diag_of(u))

        key_block_plan(i, run)
        acc = acc_sc[...]
        o_ref[...] = (acc[:, :LANES] / acc[:, LANES:]).astype(o_ref.dtype)
        lse_ref[0] = m_sc[...] + jnp.log2(acc[:, LANES:])

    mla_y, lse = pl.pallas_call(
        attn_fwd_kern, name="attn_fwd", grid=(H, nq),
        in_specs=[q_blk, k_head, v_head], out_specs=[o_blk, lse_blk],
        out_shape=[jax.ShapeDtypeStruct((N, c.HV), _BF), jax.ShapeDtypeStruct((H, N, LANES), F32)],
        scratch_shapes=[pltpu.VMEM((BQ, LANES), F32), pltpu.VMEM((BQ, HW), F32), pltpu.VMEM((2, BQ, B), F32)],
        compiler_params=_cparams(("parallel", "arbitrary")),
    )(q_cat, k_cat, kv)

    HV = c.HV
    mz_blk = c.o_mz // HV

    def gate_fwd_kern(y_ref, z_ref, o_ref):
        z = z_ref[...].astype(F32)
        o_ref[...] = (y_ref[...].astype(F32) * (z * _sig(z))).astype(o_ref.dtype)

    a_mla = pl.pallas_call(
        gate_fwd_kern, name="gate_mla", grid=(nr2,),
        in_specs=[rowb2(HV, 0), rowb2(HV, mz_blk)], out_specs=rowb2(HV, 0),
        out_shape=jax.ShapeDtypeStruct((N, HV), _BF), compiler_params=_cparams(("parallel",)),
    )(mla_y, proj)
    o_mla = _mm(a_mla, wmo, name="mm_omla", out_dtype=_BF)

    M, MW, MH, MHD = c.M, c.MW, c.MH, c.MHD
    memn = pl.pallas_call(
        make_rms_kern(), name="rms_mem",
        grid=(1,), in_specs=[fullb((M, D)), fullb((1, D))], out_specs=fullb((M, D)),
        out_shape=jax.ShapeDtypeStruct((M, D), _BF), compiler_params=_cparams(("arbitrary",)),
    )(memx, mem_norm_g)
    kvm = _mm(memn, wmkv, name="mm_memkv", out_dtype=F32)

    def memk_fwd_kern(kv_ref, g_ref, k_ref, v_ref):
        for hh in range(MH):
            kh, _ = _rms(kv_ref[:, hh * MHD:(hh + 1) * MHD], MHD)
            k_ref[:, hh * MHD:(hh + 1) * MHD] = (kh * g_ref[...]).astype(k_ref.dtype)
        v_ref[...] = kv_ref[:, MW:].astype(v_ref.dtype)

    mem_k, mem_v = pl.pallas_call(
        memk_fwd_kern, name="memk_fwd", grid=(1,),
        in_specs=[fullb((M, 2 * MW)), fullb((1, MHD))], out_specs=[fullb((M, MW)), fullb((M, MW))],
        out_shape=[jax.ShapeDtypeStruct((M, MW), _BF)] * 2, compiler_params=_cparams(("arbitrary",)),
    )(kvm, mem_kn_g)

    mem_blk = c.o_mem // (2 * MW)

    def mem_head(qz_ref, k_ref, v_ref, g_ref, hh):
        sl = slice(hh * MHD, (hh + 1) * MHD)
        qh, r = _rms(qz_ref[:, sl].astype(F32), MHD)
        qn = (qh * g_ref[...]).astype(_BF)
        s = _dot_nt(qn, k_ref[:, sl]) * c.mscale
        e = jnp.exp(s - jnp.max(s, axis=1, keepdims=True))
        p = e / jnp.sum(e, axis=1, keepdims=True)
        y = _dot(p.astype(_BF), v_ref[:, sl])
        z = qz_ref[:, MW + hh * MHD:MW + (hh + 1) * MHD].astype(F32)
        return sl, qh, r, qn, p, y, z

    def mem_fwd_kern(qz_ref, k_ref, v_ref, g_ref, o_ref):
        for hh in range(MH):
            sl, _, _, _, _, y, z = mem_head(qz_ref, k_ref, v_ref, g_ref, hh)
            o_ref[:, sl] = (y * (z * _sig(z))).astype(o_ref.dtype)

    a_mem = pl.pallas_call(
        mem_fwd_kern, name="mem_fwd", grid=(N // RM,),
        in_specs=[rowb(2 * MW, mem_blk, RM), fullb((M, MW)), fullb((M, MW)), fullb((1, MHD))],
        out_specs=rowb(MW, 0, RM), out_shape=jax.ShapeDtypeStruct((N, MW), _BF),
        compiler_params=_cparams(("parallel",)),
    )(proj, mem_k, mem_v, mem_qn_g)
    o_mem = _mm(a_mem, wmemo, name="mm_omem", out_dtype=_BF)

    g_blk = c.o_g // (3 * D)

    def merge_fwd_kern(g_ref, oc_ref, om_ref, ome_ref, o_ref):
        g = g_ref[...].astype(F32)
        acc = _sig(g[:, :D]) * oc_ref[...].astype(F32)
        acc += _sig(g[:, D:2 * D]) * om_ref[...].astype(F32)
        acc += _sig(g[:, 2 * D:]) * ome_ref[...].astype(F32)
        o_ref[...] = acc.astype(o_ref.dtype)

    merged = pl.pallas_call(
        merge_fwd_kern, name="merge_fwd", grid=(nr,),
        in_specs=[rowb(3 * D, g_blk), rowb(D, 0), rowb(D, 0), rowb(D, 0)], out_specs=rowb(D, 0),
        out_shape=jax.ShapeDtypeStruct((N, D), _BF), compiler_params=_cparams(("parallel",)),
    )(proj, o_conv, o_mla, o_mem)
    obm, obn = _tile(N, 512, 16), _tile(D, 1024, LANES)

    def out_loss_kern(a_ref, w_ref, x_ref, t_ref, dy_ref, dyb_ref, l_ref):
        e = x_ref[...] + _dot(a_ref[...], w_ref[...]) - t_ref[...]
        dy = e * (1.0 / D)
        dy_ref[...] = dy
        dyb_ref[...] = dy.astype(dyb_ref.dtype)
        row = lax.broadcasted_iota(jnp.int32, l_ref.shape, 0)
        l_ref[...] = jnp.where(row == 0, jnp.sum(e * e, axis=0, keepdims=True), 0.0)

    oblk = pl.BlockSpec((obm, obn), lambda i, j: (i, j))
    dy, dyb, lpart = pl.pallas_call(
        out_loss_kern, name="mm_out_loss", grid=(N // obm, D // obn),
        in_specs=[pl.BlockSpec((obm, D), lambda i, j: (i, 0)), pl.BlockSpec((D, obn), lambda i, j: (0, j)), oblk, oblk],
        out_specs=[oblk, oblk, pl.BlockSpec((8, obn), lambda i, j: (i, j))],
        out_shape=[jax.ShapeDtypeStruct((N, D), F32), jax.ShapeDtypeStruct((N, D), _BF),
                   jax.ShapeDtypeStruct((8 * (N // obm), D), F32)],
        compiler_params=_cparams(("parallel", "parallel")),
    )(merged, wo, xs, tgt)
    loss = lax.psum(jnp.sum(lpart) * (0.5 / D), AXES)

    G = {}
    d_merged = _mm(dyb, wo, tb=True, name="mm_dmerged", out_dtype=_BF)
    G['w_o'] = _mm(merged, dyb, ta=True, name="mm_dwo", out_dtype=_BF, bk=4096)

    dproj0 = lax.empty((N, c.P), _BF)
    any_spec = pl.BlockSpec(memory_space=pl.ANY)

    def merge_bwd_kern(dp_any, g_ref, dm_ref, oc_ref, om_ref, ome_ref, dg_ref, doc_ref, dom_ref, dome_ref):
        g = g_ref[...].astype(F32)
        dm = dm_ref[...].astype(F32)
        for idx, (o_in, d_out) in enumerate(((oc_ref, doc_ref), (om_ref, dom_ref), (ome_ref, dome_ref))):
            sg = _sig(g[:, idx * D:(idx + 1) * D])
            d_out[...] = (sg * dm).astype(d_out.dtype)
            dg_ref[:, idx * D:(idx + 1) * D] = (dm * o_in[...].astype(F32) * sg * (1.0 - sg)).astype(dg_ref.dtype)

    dproj1, d_oconv, d_omla, d_omem = pl.pallas_call(
        merge_bwd_kern, name="merge_bwd", grid=(nr,),
        in_specs=[any_spec, rowb(3 * D, g_blk), rowb(D, 0), rowb(D, 0), rowb(D, 0), rowb(D, 0)],
        out_specs=[rowb(3 * D, g_blk), rowb(D, 0), rowb(D, 0), rowb(D, 0)],
        out_shape=[jax.ShapeDtypeStruct((N, c.P), _BF)] + [jax.ShapeDtypeStruct((N, D), _BF)] * 3,
        input_output_aliases={0: 0}, compiler_params=_cparams(("parallel",)),
    )(dproj0, proj, d_merged, o_conv, o_mla, o_mem)

    G['w_conv_out'] = _mm(a_conv, d_oconv, ta=True, name="mm_dwco", out_dtype=_BF, bk=4096)
    d_aconv = _mm(d_oconv, wco, tb=True, name="mm_daconv", out_dtype=_BF)
    G['w_mla_out'] = _mm(a_mla, d_omla, ta=True, name="mm_dwmo", out_dtype=_BF, bk=4096)
    d_amla = _mm(d_omla, wmo, tb=True, name="mm_damla", out_dtype=_BF)
    G['w_mem_out'] = _mm(a_mem, d_omem, ta=True, name="mm_dwmemo", out_dtype=_BF, bk=4096)
    d_amem = _mm(d_omem, wmemo, tb=True, name="mm_damem", out_dtype=_BF)

    def conv_bwd_kern(dp_any, p_ref, prev_ref, next_ref, da_ref, dan_ref, w_ref, o_ref, dw_ref):
        i = pl.program_id(0)
        cg, bg, u, z = conv_parts(p_ref[...])
        pc, _, pu, _ = conv_parts(prev_ref[...])
        _, nbg, _, nz = conv_parts(next_ref[...])
        cu = cg * u
        sh1, sh2 = shifted(cu, pc * pu, i)
        w = w_ref[...]
        conv = w[0:1, :] * sh2 + w[1:2, :] * sh1 + w[2:3, :] * cu
        sg = _sig(z)
        sz = z * sg
        da = da_ref[...].astype(F32)
        dcy = da * sz
        d_z = da * (bg * conv) * (sg * (1.0 + z * (1.0 - sg)))
        d_b = dcy * conv
        dconv = dcy * bg
        dnext = dan_ref[...].astype(F32) * (nz * _sig(nz)) * nbg
        dnext = jnp.where(i < nr - 1, dnext, 0.0)
        rid = lax.broadcasted_iota(jnp.int32, cu.shape, 0)
        up1 = jnp.where(rid == R - 1, dnext[0:1, :], pltpu.roll(dconv, R - 1, 0))
        up2 = jnp.where(rid == R - 2, dnext[0:1, :], jnp.where(rid == R - 1, dnext[1:2, :], pltpu.roll(dconv, R - 2, 0)))
        dcu = w[2:3, :] * dconv + w[1:2, :] * up1 + w[0:1, :] * up2
        o_ref[:, 0:CW] = (dcu * u).astype(o_ref.dtype)
        o_ref[:, CW:2 * CW] = d_b.astype(o_ref.dtype)
        o_ref[:, 2 * CW:3 * CW] = (dcu * cg).astype(o_ref.dtype)
        o_ref[:, 3 * CW:4 * CW] = d_z.astype(o_ref.dtype)

        @pl.when(i == 0)
        def _():
            dw_ref[...] = jnp.zeros(dw_ref.shape, F32)

        dw_ref[0:1, :] += jnp.sum(dconv * sh2, axis=0, keepdims=True)
        dw_ref[1:2, :] += jnp.sum(dconv * sh1, axis=0, keepdims=True)
        dw_ref[2:3, :] += jnp.sum(dconv * cu, axis=0, keepdims=True)

    next_spec = pl.BlockSpec((HALO, 4 * CW), lambda i: (jnp.minimum((i + 1) * rh, N // HALO - 1), conv_blk))
    dan_spec = pl.BlockSpec((HALO, CW), lambda i: (jnp.minimum((i + 1) * rh, N // HALO - 1), 0))
    dproj2, g_convw = pl.pallas_call(
        conv_bwd_kern, name="conv_bwd", grid=(nr,),
        in_specs=[any_spec, rowb(4 * CW, conv_blk), prev_spec, next_spec, rowb(CW, 0), dan_spec, fullb((8, CW))],
        out_specs=[rowb(4 * CW, conv_blk), fullb((8, CW))],
        out_shape=[jax.ShapeDtypeStruct((N, c.P), _BF), jax.ShapeDtypeStruct((8, CW), F32)],
        input_output_aliases={0: 0}, compiler_params=_cparams(("arbitrary",)),
    )(dproj1, proj, proj, proj, d_aconv, d_aconv, convw8)

    def mem_bwd_kern(dp_any, qz_ref, da_ref, k_ref, v_ref, g_ref, o_ref, dk_ref, dv_ref, dg_ref):
        @pl.when(pl.program_id(0) == 0)
        def _():
            dk_ref[...] = jnp.zeros(dk_ref.shape, F32)
            dv_ref[...] = jnp.zeros(dv_ref.shape, F32)
            dg_ref[...] = jnp.zeros(dg_ref.shape, F32)

        for hh in range(MH):
            sl, qh, r, qn, p, y, z = mem_head(qz_ref, k_ref, v_ref, g_ref, hh)
            da = da_ref[:, sl].astype(F32)
            sg = _sig(z)
            dyh = da * (z * sg)
            o_ref[:, MW + hh * MHD:MW + (hh + 1) * MHD] = (da * y * (sg * (1.0 + z * (1.0 - sg)))).astype(o_ref.dtype)
            dyb_h = dyh.astype(_BF)
            dpm = _dot_nt(dyb_h, v_ref[:, sl])
            ds = (p * (dpm - jnp.sum(dpm * p, axis=1, keepdims=True)) * c.mscale).astype(_BF)
            dqn = _dot(ds, k_ref[:, sl])
            dk_ref[:, sl] += _dot_tn(ds, qn)
            dv_ref[:, sl] += _dot_tn(p.astype(_BF), dyb_h)
            dq, dgp = _rms_bwd(qh, r, g_ref[...], dqn, MHD)
            o_ref[:, sl] = dq.astype(o_ref.dtype)
            dg_ref[...] += jnp.sum(dgp, axis=0, keepdims=True)

    dproj3, d_memk, d_memv, g_mem_qn = pl.pallas_call(
        mem_bwd_kern, name="mem_bwd", grid=(N // RM,),
        in_specs=[any_spec, rowb(2 * MW, mem_blk, RM), rowb(MW, 0, RM), fullb((M, MW)), fullb((M, MW)), fullb((1, MHD))],
        out_specs=[rowb(2 * MW, mem_blk, RM), fullb((M, MW)), fullb((M, MW)), fullb((1, MHD))],
        out_shape=[jax.ShapeDtypeStruct((N, c.P), _BF), jax.ShapeDtypeStruct((M, MW), F32),
                   jax.ShapeDtypeStruct((M, MW), F32), jax.ShapeDtypeStruct((1, MHD), F32)],
        input_output_aliases={0: 0}, compiler_params=_cparams(("arbitrary",)),
    )(dproj2, proj, d_amem, mem_k, mem_v, mem_qn_g)

    def memk_bwd_kern(kv_ref, dk_ref, dv_ref, g_ref, o_ref, dg_ref):
        dg = jnp.zeros((1, MHD), F32)
        for hh in range(MH):
            sl = slice(hh * MHD, (hh + 1) * MHD)
            kh, r = _rms(kv_ref[:, sl], MHD)
            dkr, dgp = _rms_bwd(kh, r, g_ref[...], dk_ref[:, sl], MHD)
            o_ref[:, sl] = dkr.astype(o_ref.dtype)
            dg += jnp.sum(dgp, axis=0, keepdims=True)
        o_ref[:, MW:] = dv_ref[...].astype(o_ref.dtype)
        dg_ref[...] = dg

    d_kvm, g_mem_kn = pl.pallas_call(
        memk_bwd_kern, name="memk_bwd", grid=(1,),
        in_specs=[fullb((M, 2 * MW)), fullb((M, MW)), fullb((M, MW)), fullb((1, MHD))],
        out_specs=[fullb((M, 2 * MW)), fullb((1, MHD))],
        out_shape=[jax.ShapeDtypeStruct((M, 2 * MW), _BF), jax.ShapeDtypeStruct((1, MHD), F32)],
        compiler_params=_cparams(("arbitrary",)),
    )(kvm, d_memk, d_memv, mem_kn_g)
    G['w_mem_kv'] = _mm(memn, d_kvm, ta=True, name="mm_dwmkv", out_dtype=_BF)
    d_memn = _mm(d_kvm, wmkv, tb=True, name="mm_dmemn", out_dtype=F32)

    def memnorm_bwd_kern(x_ref, d_ref, dg_ref):
        xh, _ = _rms(x_ref[...], D)
        dg_ref[...] = jnp.sum(d_ref[...] * xh, axis=0, keepdims=True)

    g_mem_norm = pl.pallas_call(
        memnorm_bwd_kern, name="memnorm_bwd", grid=(1,),
        in_specs=[fullb((M, D)), fullb((M, D))], out_specs=fullb((1, D)),
        out_shape=jax.ShapeDtypeStruct((1, D), F32), compiler_params=_cparams(("arbitrary",)),
    )(memx, d_memn)

    def gate_bwd_kern(dp_any, da_ref, y_ref, z_ref, dy_ref, dz_ref):
        z = z_ref[...].astype(F32)
        da = da_ref[...].astype(F32)
        sg = _sig(z)
        dy_ref[...] = (da * (z * sg)).astype(dy_ref.dtype)
        dz_ref[...] = (da * y_ref[...].astype(F32) * (sg * (1.0 + z * (1.0 - sg)))).astype(dz_ref.dtype)

    d_mlay, dproj4 = pl.pallas_call(
        gate_bwd_kern, name="gate_mla_bwd", grid=(nr2,),
        in_specs=[any_spec, rowb2(HV, 0), rowb2(HV, 0), rowb2(HV, mz_blk)],
        out_specs=[rowb2(HV, 0), rowb2(HV, mz_blk)],
        out_shape=[jax.ShapeDtypeStruct((N, HV), _BF), jax.ShapeDtypeStruct((N, c.P), _BF)],
        input_output_aliases={0: 1}, compiler_params=_cparams(("parallel",)),
    )(dproj3, d_amla, mla_y, proj)

    def attn_bwd_kern(q_ref, k_ref, v_ref, o_ref, do_ref, lse_ref, dq_ref, dk_ref, dv_ref, dq_sc, dl_sc, dk_sc, dv_sc):
        i = pl.program_id(1)
        delta = jnp.sum(do_ref[...].astype(F32) * o_ref[...].astype(F32), axis=1, keepdims=True)
        dl_sc[...] = jnp.broadcast_to(delta, dl_sc.shape)
        dq_sc[...] = jnp.zeros(dq_sc.shape, F32)

        def step(t, diag):
            rows = pl.ds(pl.multiple_of(t * B, B), B)
            rs = live_rows(diag)
            q, do = q_ref[0, rs, :], do_ref[rs, :]
            k = k_ref[0, rows, :]
            s = _dot_nt(q, k)
            if diag is not None:
                s = diag_mask(s)
            dpm = _dot_nt(do, v_ref[rows, :])
            lse_t, dl = lse_ref[0, rs, :], dl_sc[rs, :]
            ps, dss = [], []
            for cb in range(B // LANES):
                cols = slice(cb * LANES, (cb + 1) * LANES)
                p_cb = jnp.exp2(s[:, cols] - lse_t)
                ps.append(p_cb.astype(_BF))
                dss.append((p_cb * (dpm[:, cols] - dl)).astype(_BF))
            p, ds = jnp.concatenate(ps, axis=1), jnp.concatenate(dss, axis=1)
            dvp = _dot_tn(p, do)
            dkp = _dot_tn(ds, q)
            if diag is not None:
                dk_sc[rows, :] = dkp
                dv_sc[rows, :] = dvp
            else:
                dk_sc[rows, :] += dkp
                dv_sc[rows, :] += dvp
            dq_sc[rs, :] += _dot(ds, k)

        def run(first, count, n_diag):
            for u in range(count):
                step(first + u, u - (count - n_diag) if u >= count - n_diag else None)

        key_block_plan(i, run)
        dq_ref[0] = dq_sc[...].astype(dq_ref.dtype)

        @pl.when(i == nq - 1)
        def _():
            dk_ref[0] = dk_sc[...].astype(dk_ref.dtype)
            dv_ref[0] = dv_sc[...].astype(dv_ref.dtype)

    d_qcat, d_kcat, d_v = pl.pallas_call(
        attn_bwd_kern, name="attn_bwd", grid=(H, nq),
        in_specs=[q_blk, k_head, v_head, o_blk, o_blk, lse_blk],
        out_specs=[pl.BlockSpec((1, BQ, HW), lambda hh, i: (hh, i, 0)),
                   pl.BlockSpec((1, N, HW), lambda hh, i: (hh, 0, 0)),
                   pl.BlockSpec((1, N, LANES), lambda hh, i: (hh, 0, 0))],
        out_shape=[jax.ShapeDtypeStruct((H, N, HW), _BF), jax.ShapeDtypeStruct((H, N, HW), _BF),
                   jax.ShapeDtypeStruct((H, N, LANES), _BF)],
        scratch_shapes=[pltpu.VMEM((BQ, HW), F32), pltpu.VMEM((BQ, LANES), F32), pltpu.VMEM((N, HW), F32),
                        pltpu.VMEM((N, LANES), F32)],
        compiler_params=_cparams(("parallel", "arbitrary")),
    )(q_cat, k_cat, kv, mla_y, d_mlay, lse)

    def q_prep_bwd_kern(q_ref, dq_ref, t_ref, gn_ref, gr_ref, o_ref, dgn_ref, dgr_ref):
        @pl.when(jnp.logical_and(pl.program_id(0) == 0, pl.program_id(1) == 0))
        def _():
            dgn_ref[...] = jnp.zeros(dgn_ref.shape, F32)
            dgr_ref[...] = jnp.zeros(dgr_ref.shape, F32)

        for g in range(HG):
            blk = q_ref[:, g * HW:(g + 1) * HW].astype(F32)
            d = dq_ref[g].astype(F32) * c.scale
            nh, rn = _rms128(blk[:, :LANES], c.NOPE)
            rhat, rr = _rms128(blk[:, LANES:], c.ROPE)
            dn, dgn = _rms128_bwd(nh, rn, gn_ref[...], d[:, :LANES], c.NOPE)
            drot = _rope_t(d[:, LANES:], t_ref[0], t_ref[1], t_ref[2])
            dr, dgr = _rms128_bwd(rhat, rr, gr_ref[...], drot, c.ROPE)
            o_ref[:, g * HW:(g + 1) * HW] = jnp.concatenate([dn, dr], axis=1).astype(o_ref.dtype)
            dgn_ref[...] += jnp.sum(dgn, axis=0, keepdims=True)
            dgr_ref[...] += jnp.sum(dgr, axis=0, keepdims=True)

    d_qp, g_qn_nope, g_qn_rope = pl.pallas_call(
        q_prep_bwd_kern, name="q_prep_bwd", grid=(nrp, nhg),
        in_specs=[heads_in, heads_out, tabs_of(RP), fullb((1, LANES)), fullb((1, LANES))],
        out_specs=[heads_in, fullb((1, LANES)), fullb((1, LANES))],
        out_shape=[jax.ShapeDtypeStruct((N, H * HW), _BF), jax.ShapeDtypeStruct((1, LANES), F32),
                   jax.ShapeDtypeStruct((1, LANES), F32)],
        compiler_params=_cparams(("arbitrary", "arbitrary")),
    )(q_p, d_qcat, tabs, g_qn, g_qr)

    def k_prep_bwd_kern(kv_ref, dk_ref, dv_ref, gn_ref, o_ref, dkr_ref, dgn_ref):
        hg = pl.program_id(1)

        @pl.when(jnp.logical_and(pl.program_id(0) == 0, hg == 0))
        def _():
            dgn_ref[...] = jnp.zeros(dgn_ref.shape, F32)

        @pl.when(hg == 0)
        def _():
            dkr_ref[...] = jnp.zeros(dkr_ref.shape, F32)

        dkr = jnp.zeros((RP, LANES), F32)
        for g in range(HG):
            dk = dk_ref[g].astype(F32) * (1.0 / LOG2E)
            kn, r = _rms128(kv_ref[:, g * HW:g * HW + LANES].astype(F32), c.NOPE)
            dkn, dgn = _rms128_bwd(kn, r, gn_ref[...], dk[:, :LANES], c.NOPE)
            o_ref[:, g * HW:(g + 1) * HW] = jnp.concatenate([dkn.astype(o_ref.dtype), dv_ref[g]], axis=1)
            dgn_ref[...] += jnp.sum(dgn, axis=0, keepdims=True)
            dkr += dk[:, LANES:]
        dkr_ref[...] += dkr

    d_kv, d_krsum, g_kn_nope = pl.pallas_call(
        k_prep_bwd_kern, name="k_prep_bwd", grid=(nrp, nhg),
        in_specs=[heads_in, heads_out, pl.BlockSpec((HG, RP, LANES), lambda i, hg: (hg, i, 0)), fullb((1, LANES))],
        out_specs=[heads_in, pl.BlockSpec((RP, LANES), lambda i, hg: (i, 0)), fullb((1, LANES))],
        out_shape=[jax.ShapeDtypeStruct((N, H * HW), _BF), jax.ShapeDtypeStruct((N, LANES), F32),
                   jax.ShapeDtypeStruct((1, LANES), F32)],
        compiler_params=_cparams(("arbitrary", "arbitrary")),
    )(kv, d_kcat, d_v, g_kn)

    def krope_bwd_kern(p_ref, d_ref, t_ref, g_ref, o_ref, dg_ref):
        @pl.when(pl.program_id(0) == 0)
        def _():
            dg_ref[...] = jnp.zeros(dg_ref.shape, F32)

        xh, r = _rms128(p_ref[...].astype(F32), c.ROPE)
        drot = _rope_t(d_ref[...], t_ref[0], t_ref[1], t_ref[2])
        dx, dg = _rms128_bwd(xh, r, g_ref[...], drot, c.ROPE)
        o_ref[...] = dx.astype(o_ref.dtype)
        dg_ref[...] += jnp.sum(dg, axis=0, keepdims=True)

    d_kr, g_kn_rope = pl.pallas_call(
        krope_bwd_kern, name="krope_bwd", grid=(nr,),
        in_specs=[rowb(LANES, 0), rowb(LANES, 0), tabs_of(R), fullb((1, LANES))],
        out_specs=[rowb(LANES, 0), fullb((1, LANES))],
        out_shape=[jax.ShapeDtypeStruct((N, LANES), _BF), jax.ShapeDtypeStruct((1, LANES), F32)],
        compiler_params=_cparams(("arbitrary",)),
    )(kr_raw, d_krsum, tabs, g_kr)
    dproj5 = dproj4

    g_wuq_p = _mm(cqn, d_qp, ta=True, name="mm_dwuq", out_dtype=_BF, bk=4096)
    G['w_uq'] = g_wuq_p.reshape(QL, H, HW)[:, :, :c.NOPE + c.ROPE].reshape(QL, H * (c.NOPE + c.ROPE))
    d_cqn = _mm(d_qp, wuq_p, tb=True, name="mm_dcqn", out_dtype=F32, bk=4096)
    G['w_ukv'] = _mm(ckvn, d_kv, ta=True, name="mm_dwukv", out_dtype=_BF, bk=4096)
    d_ckvn = _mm(d_kv, wukv, tb=True, name="mm_dckvn", out_dtype=F32, bk=4096)

    def to_blocks(n, g):
        if n in COL_SHARDED:
            return jnp.transpose(g.reshape(g.shape[0], NDEV, -1), (1, 0, 2))
        return g.reshape(NDEV, -1, g.shape[1])

    def landing(b):
        own = lax.dynamic_index_in_dim(b, me, 0, keepdims=True)
        return lax.dynamic_update_index_in_dim(lax.empty(b.shape, b.dtype), own, me, 0)

    early = [n for n in BIG if n != 'w_in']
    blocks_e = [to_blocks(n, G[n]) for n in early]
    xe = _split_start(blocks_e, [landing(b) for b in blocks_e], False, "xchg_early_start")
    gq_after = mla_q_norm_g + xe[4][0:1, 0:1]

    def lora_bwd_kern(dp_any, p_ref, dq_ref, dkv_ref, gq_ref, gkv_ref, o_ref, dgq_ref, dgkv_ref):
        @pl.when(pl.program_id(0) == 0)
        def _():
            dgq_ref[...] = jnp.zeros(dgq_ref.shape, F32)
            dgkv_ref[...] = jnp.zeros(dgkv_ref.shape, F32)

        blk = p_ref[...].astype(F32)
        qh, rq = _rms(blk[:, :QL], QL)
        kh, rk = _rms(blk[:, QL:], KVL)
        dq, dgq = _rms_bwd(qh, rq, gq_ref[...], dq_ref[...], QL)
        dk, dgk = _rms_bwd(kh, rk, gkv_ref[...], dkv_ref[...], KVL)
        o_ref[:, :QL] = dq.astype(o_ref.dtype)
        o_ref[:, QL:] = dk.astype(o_ref.dtype)
        dgq_ref[...] += jnp.sum(dgq, axis=0, keepdims=True)
        dgkv_ref[...] += jnp.sum(dgk, axis=0, keepdims=True)

    dproj6, g_q_norm, g_kv_norm = pl.pallas_call(
        lora_bwd_kern, name="lora_bwd", grid=(nr,),
        in_specs=[any_spec, rowb(QL + KVL, lora_blk), rowb(QL, 0), rowb(KVL, 0), fullb((1, QL)), fullb((1, KVL))],
        out_specs=[rowb(QL + KVL, lora_blk), fullb((1, QL)), fullb((1, KVL))],
        out_shape=[jax.ShapeDtypeStruct((N, c.P), _BF), jax.ShapeDtypeStruct((1, QL), F32),
                   jax.ShapeDtypeStruct((1, KVL), F32)],
        input_output_aliases={0: 0}, compiler_params=_cparams(("arbitrary",)),
    )(dproj5, proj, d_cqn, d_ckvn, gq_after, mla_kv_norm_g)

    g_win_p = _mm(h, dproj6, ta=True, name="mm_dwin", out_dtype=_BF, bk=4096)
    g_wkr = _mm(h, d_kr, ta=True, name="mm_dwkr", out_dtype=_BF, bk=4096)
    blocks_w = [_win_blocks(g_win_p, g_wkr, c)]
    xw = _split_start(blocks_w, [landing(b) for b in blocks_w], False, "xchg_win_start")
    d_h = _mm(dproj6, win_p, tb=True, name="mm_dh", out_dtype=F32, bk=3072, after=xw[4], plus=(d_kr, w_krT))

    def final_bwd_kern(x_ref, g_ref, dh_ref, dy_ref, gx_ref, dg_ref):
        @pl.when(pl.program_id(0) == 0)
        def _():
            dg_ref[...] = jnp.zeros(dg_ref.shape, F32)

        xh, r = _rms(x_ref[...], D)
        dx, dg = _rms_bwd(xh, r, g_ref[...], dh_ref[...], D)
        gx_ref[...] = dy_ref[...] + dx
        dg_ref[...] += jnp.sum(dg, axis=0, keepdims=True)

    grad_x, g_norm = pl.pallas_call(
        final_bwd_kern, name="final_bwd", grid=(nr2,),
        in_specs=[rowb2(D, 0), fullb((1, D)), rowb2(D, 0), rowb2(D, 0)],
        out_specs=[rowb2(D, 0), fullb((1, D))],
        out_shape=[jax.ShapeDtypeStruct((N, D), F32), jax.ShapeDtypeStruct((1, D), F32)],
        compiler_params=_cparams(("arbitrary",)),
    )(xs, norm_g, d_h, dy)

    res = [{}, {}, {}, {}]

    def adam_into(n, parts):
        outs = _adam(parts, W[n][0], Mo[n][0], Vo[n][0], "adam_" + n)
        for k in range(4):
            res[k][n] = outs[k][None]
        return outs[0]

    recv_e = _split_wait(xe, False, grad_x, "xchg_early_wait")
    last = [adam_into(n, parts) for n, parts in zip(early, recv_e)][-1]
    recv_w = _split_wait(xw, False, last, "xchg_win_wait")
    adam_into('w_in', recv_w[0])

    small_g = {'norm_g': g_norm, 'mla_q_norm_g': g_q_norm, 'mla_kv_norm_g': g_kv_norm,
               'mla_qn_nope_g': g_qn_nope, 'mla_qn_rope_g': g_qn_rope[:, :c.ROPE], 'mla_kn_nope_g': g_kn_nope,
               'mla_kn_rope_g': g_kn_rope[:, :c.ROPE], 'mem_norm_g': g_mem_norm, 'mem_qn_g': g_mem_qn,
               'mem_kn_g': g_mem_kn}
    small_part = _pack([small_g[n] for n in SMALL] + [g_convw[0:3, :]], 0)
    small_all = _all_gather([small_part], "ag_small_grads")[0]
    small_shapes = [W[n].shape for n in SMALL]
    pieces = _unpack(small_all, small_shapes + [(3, CW)])
    cw8 = CW // NDEV
    conv_mine = lax.dynamic_slice_in_dim(pieces[-1].reshape(NDEV, 3, NDEV, cw8), me, 1, axis=2)[:, :, 0, :]
    sm_parts = _pack(pieces[:-1] + [conv_mine], 1)
    sm_names = SMALL + ['conv_w']
    sm_shapes = small_shapes + [(3, cw8)]
    w_sm = _pack([W[n] for n in SMALL] + [conv_w[0]], 0)
    m_sm = _pack([Mo[n] for n in SMALL] + [m_conv_w[0]], 0)
    v_sm = _pack([Vo[n] for n in SMALL] + [v_conv_w[0]], 0)
    outs_sm = [_unpack(o, sm_shapes) for o in _adam(sm_parts, w_sm, m_sm, v_sm, "adam_small")]
    for k in range(4):
        for n, a in zip(sm_names, outs_sm[k]):
            res[k][n] = a[None] if n == 'conv_w' else a
    return (loss, grad_x[None], *[res[0][n] for n in WEIGHTS], *[res[1][n] for n in WEIGHTS],
            *[res[2][n] for n in WEIGHTS], *[res[3][n] for n in WEIGHTS])
```

```python
import math

import jax
import jax.numpy as jnp
from jax import lax
from jax.experimental import pallas as pl
from jax.experimental.pallas import tpu as pltpu

F32 = jnp.float32
_BF = jnp.bfloat16
EPS = 1e-6
CHUNK = 64
ROPE_THETA = 10000.0
ADAM_LR, ADAM_B1, ADAM_B2, ADAM_EPS, ADAM_WD, ADAM_STEP = 0.001, 0.9, 0.999, 1e-08, 0.01, 10
NDEV = 8
AXES = ("x", "y", "c")
MESH = pl.DeviceIdType.MESH
LANES = 128
NEG = -1e30
LOG2E = math.log2(math.e)
V7X_VMEM_LIMIT = 56 * 1024 * 1024
PACK_C = 1024
ATT_BLOCK = 512
ATT_UNROLL = 4
ATT_QBLOCKS = 2
ADAM_BLOCK_ELEMS = 256 * 1024

WEIGHTS = ['norm_g', 'w_in', 'conv_w', 'w_conv_out', 'mla_q_norm_g', 'w_uq', 'mla_kv_norm_g', 'w_ukv',
           'mla_qn_nope_g', 'mla_qn_rope_g', 'mla_kn_nope_g', 'mla_kn_rope_g', 'w_mla_out', 'mem_norm_g',
           'w_mem_kv', 'mem_qn_g', 'mem_kn_g', 'w_mem_out', 'w_o']
BIG = ['w_in', 'w_conv_out', 'w_uq', 'w_ukv', 'w_mla_out', 'w_mem_kv', 'w_mem_out', 'w_o']
COL_SHARDED = ('w_in', 'w_conv_out', 'w_uq', 'w_ukv', 'w_mem_out')
SMALL = ['norm_g', 'mla_q_norm_g', 'mla_kv_norm_g', 'mla_qn_nope_g', 'mla_qn_rope_g', 'mla_kn_nope_g',
         'mla_kn_rope_g', 'mem_norm_g', 'mem_qn_g', 'mem_kn_g']


def _tile(dim, target, align):
    if dim <= target:
        return dim
    t = target - target % align
    while t > 0:
        if dim % t == 0:
            return t
        t -= align
    raise ValueError(f"no tile for {dim} {target} {align}")


def _cparams(sem):
    return pltpu.CompilerParams(dimension_semantics=sem, vmem_limit_bytes=V7X_VMEM_LIMIT)


def _sig(x):
    return 1.0 / (1.0 + jnp.exp(-x))


def _rms(x, n):
    r = lax.rsqrt(jnp.sum(x * x, axis=-1, keepdims=True) * (1.0 / n) + EPS)
    return x * r, r


def _rms_bwd(xhat, r, g, dy, n):
    dxh = dy * g
    dx = r * (dxh - xhat * (jnp.sum(dxh * xhat, axis=-1, keepdims=True) * (1.0 / n)))
    return dx, dy * xhat


def _rowsum128(x):
    return jnp.dot(x.astype(_BF), jnp.ones((LANES, LANES), _BF), preferred_element_type=F32)


def _rms128(x, n):
    r = lax.rsqrt(_rowsum128(x * x) * (1.0 / n) + EPS)
    return x * r, r


def _rms128_bwd(xhat, r, g, dy, n):
    dxh = dy * g
    dx = r * (dxh - xhat * (_rowsum128(dxh * xhat) * (1.0 / n)))
    return dx, dy * xhat


def _rope(x, cosp, sina, sinb):
    return x * cosp + pltpu.roll(x, 96, 1) * sina + pltpu.roll(x, 32, 1) * sinb


def _rope_t(d, cosp, sina, sinb):
    return d * cosp + pltpu.roll(d * sina, 32, 1) + pltpu.roll(d * sinb, 96, 1)


def _dot_nt(a, b):
    return lax.dot_general(a, b, (((1,), (1,)), ((), ())), preferred_element_type=F32)


def _dot_tn(a, b):
    return lax.dot_general(a, b, (((0,), (0,)), ((), ())), preferred_element_type=F32)


def _dot(a, b):
    return jnp.dot(a, b, preferred_element_type=F32)


def _all_gather(shards, name):
    na = len(shards)
    nc = 9
    halves = [s.shape[0] // 32 * 16 if s.shape[0] >= 256 else None for s in shards]

    def body(*refs):
        x_refs, out_refs = refs[:na], refs[na:2 * na]
        send_sems, recv_sems, local_sems = refs[2 * na:]
        x, y, c = lax.axis_index("x"), lax.axis_index("y"), lax.axis_index("c")
        me, sib = (x, y, c), (x, y, 1 - c)
        px, py, pd = (1 - x, y, c), (x, 1 - y, c), (1 - x, 1 - y, c)

        def other_core(p):
            return (p[0], p[1], 1 - p[2])

        def rows(a, blk, part=None):
            r = out_refs[a].at[4 * blk[0] + 2 * blk[1] + blk[2]]
            if part is None or halves[a] is None:
                return r
            rest = shards[a].shape[0] - halves[a]
            return r.at[pl.ds(0, halves[a])] if part == 0 else r.at[pl.ds(halves[a], rest)]

        def copy(a, k, blk, to, part=None, src=None):
            dst = rows(a, blk, part)
            return pltpu.make_async_remote_copy(
                src_ref=dst if src is None else src, dst_ref=dst,
                send_sem=send_sems.at[nc * a + k], recv_sem=recv_sems.at[nc * a + k],
                device_id=to, device_id_type=MESH)

        mine = [pltpu.make_async_copy(x_refs[a], rows(a, me), local_sems.at[a]) for a in range(na)]
        for cp in mine:
            cp.start()
        started = []
        for a in range(na):
            started += [copy(a, 0, me, px, src=x_refs[a]), copy(a, 1, me, py, src=x_refs[a]),
                        copy(a, 2, me, sib, src=x_refs[a])]
        for cp in started:
            cp.start()

        def forward(cp):
            cp.start()
            started.append(cp)

        for a in range(na):
            copy(a, 0, px, me).wait_recv()
            forward(copy(a, 3, px, py, part=0))
            forward(copy(a, 4, px, sib))
        for a in range(na):
            copy(a, 1, py, me).wait_recv()
            if halves[a] is not None:
                forward(copy(a, 5, py, px, part=1))
            forward(copy(a, 6, py, sib))
        for a in range(na):
            copy(a, 3, pd, me, part=0).wait_recv()
            forward(copy(a, 7, pd, sib, part=0))
            if halves[a] is not None:
                copy(a, 5, pd, me, part=1).wait_recv()
                forward(copy(a, 8, pd, sib, part=1))
        for a in range(na):
            copy(a, 2, sib, me).wait_recv()
            copy(a, 4, other_core(px), me).wait_recv()
            copy(a, 6, other_core(py), me).wait_recv()
            copy(a, 7, other_core(pd), me, part=0).wait_recv()
            if halves[a] is not None:
                copy(a, 8, other_core(pd), me, part=1).wait_recv()
        for cp in started:
            cp.wait_send()
        for cp in mine:
            cp.wait()

    any_spec = pl.BlockSpec(memory_space=pl.ANY)
    return pl.pallas_call(
        body, name=name,
        out_shape=[jax.ShapeDtypeStruct((NDEV,) + s.shape, s.dtype) for s in shards],
        in_specs=[any_spec] * na, out_specs=[any_spec] * na,
        scratch_shapes=[pltpu.SemaphoreType.DMA((nc * na,)), pltpu.SemaphoreType.DMA((nc * na,)),
                        pltpu.SemaphoreType.DMA((na,))],
    )(*shards)


_HBM = pl.BlockSpec(memory_space=pltpu.HBM)
_SEM = pl.BlockSpec(memory_space=pltpu.SEMAPHORE)
_EFFECT = pltpu.SideEffectType.DATAFLOW_SIDE_EFFECTING


def _split_copy(a, k, src_refs, land_refs, send_sems, recv_sems, gather, receive_side):
    x, y, c = lax.axis_index("x"), lax.axis_index("y"), lax.axis_index("c")
    me = 4 * x + 2 * y + c
    tx, ty, tc = x ^ ((k + 1) >> 2 & 1), y ^ ((k + 1) >> 1 & 1), c ^ ((k + 1) & 1)
    peer = 4 * tx + 2 * ty + tc
    return pltpu.make_async_remote_copy(
        src_ref=src_refs[a] if gather else src_refs[a].at[peer],
        dst_ref=land_refs[a].at[peer if receive_side else me],
        send_sem=send_sems.at[7 * a + k], recv_sem=recv_sems.at[7 * a + k],
        device_id=(tx, ty, tc), device_id_type=MESH)


def _split_start(srcs, lands, gather, name, after=None):
    na = len(srcs)
    extra = [] if after is None else [after]

    def body(*refs):
        src_refs, land_refs = refs[:na], refs[na:2 * na]
        send_sems, recv_sems = refs[2 * na + len(extra)], refs[2 * na + len(extra) + 1]
        token = refs[-1]
        for k in range(7):
            for a in range(na):
                _split_copy(a, k, src_refs, land_refs, send_sems, recv_sems, gather, False).start()
        token[...] = jnp.zeros_like(token)

    hbm = [pltpu.HBM(b.shape, b.dtype) for b in list(srcs) + list(lands)]
    outs = pl.pallas_call(
        body, name=name,
        out_shape=(pltpu.SemaphoreType.DMA((7 * na,)), pltpu.SemaphoreType.DMA((7 * na,)), *hbm,
                   jax.ShapeDtypeStruct((8, LANES), F32)),
        in_specs=[_HBM] * (2 * na) + [pl.BlockSpec(memory_space=pl.ANY)] * len(extra),
        out_specs=(_SEM, _SEM, *[_HBM] * (2 * na), pl.BlockSpec(memory_space=pltpu.VMEM)),
        input_output_aliases={j: 2 + j for j in range(2 * na)},
        compiler_params=pltpu.CompilerParams(has_side_effects=_EFFECT),
    )(*[pltpu.with_memory_space_constraint(b, pltpu.HBM) for b in srcs],
      *[pltpu.with_memory_space_constraint(l, pltpu.HBM) for l in lands], *extra)
    return outs[0], outs[1], outs[2:2 + na], outs[2 + na:2 + 2 * na], outs[-1]


def _split_wait(started, gather, after, name):
    send_sems, recv_sems, srcs, lands, _ = started
    na = len(srcs)

    def body(*refs):
        src_refs, land_refs = refs[:na], refs[na:2 * na]
        send_s, recv_s = refs[2 * na], refs[2 * na + 1]
        for k in range(7):
            for a in range(na):
                cp = _split_copy(a, k, src_refs, land_refs, send_s, recv_s, gather, True)
                cp.wait_send()
                cp.wait_recv()

    hbm = [pltpu.HBM(b.shape, b.dtype) for b in list(srcs) + list(lands)]
    outs = pl.pallas_call(
        body, name=name, out_shape=tuple(hbm),
        in_specs=[_HBM] * (2 * na) + [_SEM, _SEM, pl.BlockSpec(memory_space=pl.ANY)],
        out_specs=tuple([_HBM] * (2 * na)),
        input_output_aliases={j: j for j in range(2 * na)},
        compiler_params=pltpu.CompilerParams(has_side_effects=_EFFECT),
    )(*srcs, *lands, send_sems, recv_sems, after)
    return outs[na:]


def _seg_rows(size):
    rows = -(-size // PACK_C)
    return -(-rows // 16) * 16


def _pack(arrs, lead):
    parts = []
    for a in arrs:
        lshape = a.shape[:lead]
        f = a.reshape(lshape + (-1,)).astype(F32)
        rows = _seg_rows(f.shape[-1])
        f = jnp.pad(f, [(0, 0)] * lead + [(0, rows * PACK_C - f.shape[-1])])
        parts.append(f.reshape(lshape + (rows, PACK_C)))
    return jnp.concatenate(parts, axis=lead)


def _unpack(buf, shapes):
    lshape = buf.shape[:-2]
    out, r = [], 0
    for shp in shapes:
        size = math.prod(shp)
        rows = _seg_rows(size)
        seg = buf[..., r:r + rows, :].reshape(lshape + (rows * PACK_C,))[..., :size]
        out.append(seg.reshape(lshape + tuple(shp)))
        r += rows
    return out


def _mm(a, b, *, name, out_dtype, ta=False, tb=False, bm=1024, bn=1024, bk=2048, after=None, plus=None):
    if ta:
        kdim, m = a.shape
    else:
        m, kdim = a.shape
    n, k2 = b.shape if tb else b.shape[::-1]
    assert kdim == k2 and not (ta and tb), (a.shape, b.shape)
    bm = _tile(m, bm, LANES if ta else 16)
    bn = _tile(n, bn, LANES)
    bk = _tile(kdim, bk, LANES)
    nk = kdim // bk
    n_after = 0 if after is None else 1
    n_plus = 0 if plus is None else 2

    def kern(a_ref, b_ref, *rest):
        plus_refs = rest[n_after:n_after + n_plus]
        o_ref, scratch = rest[n_after + n_plus], rest[n_after + n_plus + 1:]
        part = (_dot_tn if ta else _dot_nt if tb else _dot)(a_ref[...], b_ref[...])

        def first(p):
            return p + _dot(plus_refs[0][...], plus_refs[1][...]) if plus is not None else p

        if nk == 1:
            o_ref[...] = first(part).astype(o_ref.dtype)
        else:
            acc = scratch[0] if scratch else o_ref
            k = pl.program_id(2)

            @pl.when(k == 0)
            def _():
                acc[...] = first(jnp.zeros(acc.shape, F32))

            acc[...] += part
            if scratch:
                @pl.when(k == nk - 1)
                def _():
                    o_ref[...] = acc[...].astype(o_ref.dtype)

    a_spec = pl.BlockSpec((bk, bm), lambda i, j, k: (k, i)) if ta else pl.BlockSpec((bm, bk), lambda i, j, k: (i, k))
    b_spec = pl.BlockSpec((bn, bk), lambda i, j, k: (j, k)) if tb else pl.BlockSpec((bk, bn), lambda i, j, k: (k, j))
    extra_specs, extra_args = [], []
    if after is not None:
        extra_specs.append(pl.BlockSpec(after.shape, lambda i, j, k: (0, 0)))
        extra_args.append(after)
    if plus is not None:
        kk = plus[0].shape[1]
        extra_specs += [pl.BlockSpec((bm, kk), lambda i, j, k: (i, 0)), pl.BlockSpec((kk, bn), lambda i, j, k: (0, j))]
        extra_args += list(plus)
    return pl.pallas_call(
        kern, name=name, grid=(m // bm, n // bn, nk),
        in_specs=[a_spec, b_spec] + extra_specs,
        out_specs=pl.BlockSpec((bm, bn), lambda i, j, k: (i, j)),
        out_shape=jax.ShapeDtypeStruct((m, n), out_dtype),
        scratch_shapes=[pltpu.VMEM((bm, bn), F32)] if nk > 1 and out_dtype != F32 else [],
        compiler_params=_cparams(("parallel", "parallel", "arbitrary")),
    )(a, b, *extra_args)


def _adam(parts, w_a, m_a, v_a, name):
    rows, cols = w_a.shape
    rb = _tile(rows, max(8, ADAM_BLOCK_ELEMS // cols // 8 * 8), 8)
    bc1 = 1.0 - ADAM_B1 ** ADAM_STEP
    bc2 = 1.0 - ADAM_B2 ** ADAM_STEP

    def adam_kern(p_ref, w_ref, m_ref, v_ref, g_ref, d_ref, nm_ref, nv_ref):
        g = p_ref[0].astype(F32)
        for j in range(1, NDEV):
            g = g + p_ref[j].astype(F32)
        m_new = ADAM_B1 * m_ref[...] + (1.0 - ADAM_B1) * g
        v_new = ADAM_B2 * v_ref[...] + (1.0 - ADAM_B2) * (g * g)
        g_ref[...] = g
        nm_ref[...] = m_new
        nv_ref[...] = v_new
        d_ref[...] = -ADAM_LR * ((m_new / bc1) / (jnp.sqrt(v_new / bc2) + ADAM_EPS) + ADAM_WD * w_ref[...])

    blk = pl.BlockSpec((rb, cols), lambda i: (i, 0))
    return pl.pallas_call(
        adam_kern, name=name, grid=(rows // rb,),
        in_specs=[pl.BlockSpec((NDEV, rb, cols), lambda i: (0, i, 0)), blk, blk, blk],
        out_specs=[blk] * 4, out_shape=[jax.ShapeDtypeStruct((rows, cols), F32)] * 4,
        compiler_params=_cparams(("parallel",)),
    )(parts, w_a, m_a, v_a)


class _Cfg:
    pass


def _config(x, conv_w, w_uq, w_ukv, mla_qn_nope_g, mla_qn_rope_g, mem, mem_qn_g, w_mem_out, w_mla_out):
    c = _Cfg()
    c.N, c.D = x.shape[1], x.shape[2]
    c.CW = conv_w.shape[2] * NDEV
    c.QL, c.KVL = w_uq.shape[1], w_ukv.shape[1]
    c.NOPE, c.ROPE = mla_qn_nope_g.shape[1], mla_qn_rope_g.shape[1]
    c.H = w_uq.shape[2] * NDEV // (c.NOPE + c.ROPE)
    c.V = w_ukv.shape[2] * NDEV // c.H - c.NOPE
    assert c.NOPE == LANES and c.V == LANES and c.ROPE == LANES // 2
    c.HW = 2 * LANES
    c.HV = c.H * c.V
    assert w_mla_out.shape[1] * NDEV == c.HV
    c.M = mem.shape[1]
    c.MHD = mem_qn_g.shape[1]
    c.MW = w_mem_out.shape[1]
    c.MH = c.MW // c.MHD
    c.o_conv = 0
    c.o_mz = 4 * c.CW
    c.o_g = c.o_mz + c.HV
    c.o_mem = c.o_g + 3 * c.D
    c.o_lora = c.o_mem + 2 * c.MW
    c.P = c.o_lora + c.QL + c.KVL
    assert c.o_mz % c.HV == 0 and c.o_g % (3 * c.D) == 0 and c.o_mem % (2 * c.MW) == 0
    assert c.o_lora % (c.QL + c.KVL) == 0 and c.QL % LANES == 0 and c.KVL % LANES == 0
    c.IN = 4 * c.CW + c.QL + c.KVL + c.ROPE + c.HV + 2 * c.MW + 3 * c.D
    c.R = _tile(c.N, 256, 16)
    c.RP = _tile(c.N, 512, 16)
    c.HG = _tile(c.H, 4, 1)
    c.B = _tile(c.N, ATT_BLOCK, CHUNK)
    c.scale = float((c.NOPE + c.ROPE) ** -0.5)
    c.mscale = float(c.MHD ** -0.5)
    return c


def _win_segments(c):
    ref_order = (('conv', 4 * c.CW), ('lora', c.QL + c.KVL), ('kr', c.ROPE), ('mz', c.HV), ('mem', 2 * c.MW), ('g', 3 * c.D))
    mine = {'conv': c.o_conv, 'mz': c.o_mz, 'g': c.o_g, 'mem': c.o_mem, 'lora': c.o_lora, 'kr': 0}
    segs, o = [], 0
    for nm, wd in ref_order:
        segs.append((nm, o, wd, mine[nm]))
        o += wd
    return segs


def _win_split(g_win_t, c):
    n8 = g_win_t.shape[1]

    def rows(a, wd):
        return [g_win_t[j][max(a, j * n8) - j * n8:min(a + wd, (j + 1) * n8) - j * n8]
                for j in range(a // n8, (a + wd - 1) // n8 + 1)]

    segs = {nm: (a, wd) for nm, a, wd, _ in _win_segments(c)}
    main = [p for nm in ('conv', 'mz', 'g', 'mem', 'lora') for p in rows(*segs[nm])]
    kr = jnp.concatenate(rows(*segs['kr']) + [jnp.zeros((LANES - c.ROPE, g_win_t.shape[2]), g_win_t.dtype)], axis=0)
    return jnp.concatenate(main, axis=0), kr


def _win_blocks(g, g_kr, c):
    n8 = c.IN // NDEV
    blocks = []
    for j in range(NDEV):
        lo, hi = j * n8, (j + 1) * n8
        parts = []
        for nm, a, wd, mine in _win_segments(c):
            s, e = max(a, lo), min(a + wd, hi)
            if s < e:
                parts.append((g_kr if nm == 'kr' else g)[:, mine + s - a:mine + e - a])
        blocks.append(jnp.concatenate(parts, axis=1))
    return jnp.stack(blocks, axis=0)


def kernel(x, positions, mem, norm_g, w_in, conv_w, w_conv_out, mla_q_norm_g, w_uq, mla_kv_norm_g, w_ukv, mla_qn_nope_g, mla_qn_rope_g, mla_kn_nope_g, mla_kn_rope_g, w_mla_out, mem_norm_g, w_mem_kv, mem_qn_g, mem_kn_g, w_mem_out, w_o, loss_target, m_norm_g, m_w_in, m_conv_w, m_w_conv_out, m_mla_q_norm_g, m_w_uq, m_mla_kv_norm_g, m_w_ukv, m_mla_qn_nope_g, m_mla_qn_rope_g, m_mla_kn_nope_g, m_mla_kn_rope_g, m_w_mla_out, m_mem_norm_g, m_w_mem_kv, m_mem_qn_g, m_mem_kn_g, m_w_mem_out, m_w_o, v_norm_g, v_w_in, v_conv_w, v_w_conv_out, v_mla_q_norm_g, v_w_uq, v_mla_kv_norm_g, v_w_ukv, v_mla_qn_nope_g, v_mla_qn_rope_g, v_mla_kn_nope_g, v_mla_kn_rope_g, v_w_mla_out, v_mem_norm_g, v_w_mem_kv, v_mem_qn_g, v_mem_kn_g, v_w_mem_out, v_w_o):
    args = dict(locals())
    W = {n: args[n] for n in WEIGHTS}
    Mo = {n: args['m_' + n] for n in WEIGHTS}
    Vo = {n: args['v_' + n] for n in WEIGHTS}
    c = _config(x, conv_w, w_uq, w_ukv, mla_qn_nope_g, mla_qn_rope_g, mem, mem_qn_g, w_mem_out, w_mla_out)
    N, D, R, B, H = c.N, c.D, c.R, c.B, c.H
    assert x.shape[0] == 1
    xs = x[0]
    tgt = loss_target[0]
    memx = mem[0]
    me = 4 * lax.axis_index("x") + 2 * lax.axis_index("y") + lax.axis_index("c")
    nr = N // R

    g_win, g_taps = _all_gather([W['w_in'][0].astype(_BF).T, conv_w[0]], "ag_w_in")
    rest = [n for n in BIG if n != 'w_in']
    shards_r = [W[n][0].astype(_BF) for n in rest]
    lands_r = [lax.dynamic_update_index_in_dim(lax.empty((NDEV,) + s.shape, s.dtype), s[None], me, 0)
               for s in shards_r]
    ag_rest = _split_start(shards_r, lands_r, True, "ag_rest_start", after=g_win)
    win_pT, w_krT = _win_split(g_win, c)
    convw = jnp.transpose(g_taps, (1, 0, 2)).reshape(3, c.CW)
    convw8 = jnp.pad(convw, ((0, 5), (0, 0)))

    def rowb(width, cidx, rows=R):
        return pl.BlockSpec((rows, width), lambda i, _c=cidx: (i, _c))

    R2 = c.RP
    nr2 = N // R2
    RM = _tile(N, 1024, 16)

    def rowb2(width, cidx):
        return rowb(width, cidx, R2)

    def fullb(shape):
        nd = len(shape)
        return pl.BlockSpec(shape, lambda *_: (0,) * nd)

    def pad_lanes(g, w=LANES):
        return jnp.pad(g, ((0, 0), (0, w - g.shape[1])))

    def tabs_of(rows):
        return pl.BlockSpec((3, rows, LANES), lambda i, *_: (0, i, 0))

    half = c.ROPE // 2
    inv_freq = jnp.power(ROPE_THETA, -jnp.arange(half, dtype=F32) / half)
    invf = jnp.concatenate([inv_freq, inv_freq, jnp.zeros((LANES - c.ROPE,), F32)])[None, :]
    pos_col = positions[0].astype(F32).reshape(N, 1)

    def rope_tab_kern(pos_ref, invf_ref, o_ref):
        ang = pos_ref[...] * invf_ref[...]
        co, si = jnp.cos(ang), jnp.sin(ang)
        lane = lax.broadcasted_iota(jnp.int32, ang.shape, 1)
        o_ref[0] = jnp.where(lane < c.ROPE, co, 0.0)
        o_ref[1] = jnp.where(lane < half, -si, 0.0)
        o_ref[2] = jnp.where(jnp.logical_and(lane >= half, lane < c.ROPE), si, 0.0)

    tabs = pl.pallas_call(
        rope_tab_kern, name="rope_tab", grid=(nr,),
        in_specs=[pl.BlockSpec((R, 1), lambda i: (i, 0)), fullb((1, LANES))],
        out_specs=tabs_of(R),
        out_shape=jax.ShapeDtypeStruct((3, N, LANES), F32),
        compiler_params=_cparams(("parallel",)),
    )(pos_col, invf)

    def make_rms_kern():
        def rms_fwd_kern(x_ref, g_ref, o_ref):
            xh, _ = _rms(x_ref[...].astype(F32), x_ref.shape[-1])
            o_ref[...] = (xh * g_ref[...]).astype(o_ref.dtype)
        return rms_fwd_kern

    h = pl.pallas_call(
        make_rms_kern(), name="rms_x", grid=(nr2,),
        in_specs=[rowb2(D, 0), fullb((1, D))], out_specs=rowb2(D, 0),
        out_shape=jax.ShapeDtypeStruct((N, D), _BF), compiler_params=_cparams(("parallel",)),
    )(xs, norm_g)

    proj = _mm(h, win_pT, tb=True, name="mm_proj", out_dtype=_BF, after=ag_rest[4])
    kr_raw = _mm(h, w_krT, tb=True, name="mm_kr", out_dtype=_BF)

    Wf = {}
    for n, g in zip(rest, _split_wait(ag_rest, True, proj, "ag_rest_wait")):
        if n in COL_SHARDED:
            Wf[n] = jnp.transpose(g, (1, 0, 2)).reshape(g.shape[1], -1)
        else:
            Wf[n] = g.reshape(-1, g.shape[2])
    wuq = Wf['w_uq'].reshape(c.QL, H, c.NOPE + c.ROPE)
    wuq_p = jnp.pad(wuq, ((0, 0), (0, 0), (0, c.HW - c.NOPE - c.ROPE))).reshape(c.QL, H * c.HW)
    wukv = Wf['w_ukv']
    wco, wmo, wmkv, wmemo, wo = Wf['w_conv_out'], Wf['w_mla_out'], Wf['w_mem_kv'], Wf['w_mem_out'], Wf['w_o']

    CW = c.CW
    conv_blk = c.o_conv // (4 * CW)
    HALO = 16
    rh = R // HALO

    def conv_parts(blk):
        blk = blk.astype(F32)
        return blk[:, 0:CW], blk[:, CW:2 * CW], blk[:, 2 * CW:3 * CW], blk[:, 3 * CW:4 * CW]

    def shifted(cu, prev, i):
        prev = jnp.where(i > 0, prev, 0.0)
        rid = lax.broadcasted_iota(jnp.int32, cu.shape, 0)
        last, last2 = prev[HALO - 1:HALO, :], prev[HALO - 2:HALO - 1, :]
        sh1 = jnp.where(rid == 0, last, pltpu.roll(cu, 1, 0))
        sh2 = jnp.where(rid == 0, last2, jnp.where(rid == 1, last, pltpu.roll(cu, 2, 0)))
        return sh1, sh2

    def conv_fwd_kern(p_ref, prev_ref, w_ref, o_ref):
        i = pl.program_id(0)
        cg, bg, u, z = conv_parts(p_ref[...])
        pc, _, pu, _ = conv_parts(prev_ref[...])
        cu = cg * u
        sh1, sh2 = shifted(cu, pc * pu, i)
        w = w_ref[...]
        conv = w[0:1, :] * sh2 + w[1:2, :] * sh1 + w[2:3, :] * cu
        o_ref[...] = (bg * conv * (z * _sig(z))).astype(o_ref.dtype)

    prev_spec = pl.BlockSpec((HALO, 4 * CW), lambda i: (jnp.maximum(i * rh - 1, 0), conv_blk))
    a_conv = pl.pallas_call(
        conv_fwd_kern, name="conv_fwd", grid=(nr,),
        in_specs=[rowb(4 * CW, conv_blk), prev_spec, fullb((8, CW))],
        out_specs=rowb(CW, 0), out_shape=jax.ShapeDtypeStruct((N, CW), _BF),
        compiler_params=_cparams(("parallel",)),
    )(proj, proj, convw8)
    o_conv = _mm(a_conv, wco, name="mm_oconv", out_dtype=_BF)

    QL, KVL, HW = c.QL, c.KVL, c.HW
    lora_blk = c.o_lora // (QL + KVL)

    def lora_fwd_kern(p_ref, gq_ref, gkv_ref, q_ref, kv_ref):
        blk = p_ref[...].astype(F32)
        qh, _ = _rms(blk[:, :QL], QL)
        kh, _ = _rms(blk[:, QL:], KVL)
        q_ref[...] = (qh * gq_ref[...]).astype(q_ref.dtype)
        kv_ref[...] = (kh * gkv_ref[...]).astype(kv_ref.dtype)

    cqn, ckvn = pl.pallas_call(
        lora_fwd_kern, name="lora_fwd", grid=(nr,),
        in_specs=[rowb(QL + KVL, lora_blk), fullb((1, QL)), fullb((1, KVL))],
        out_specs=[rowb(QL, 0), rowb(KVL, 0)],
        out_shape=[jax.ShapeDtypeStruct((N, QL), _BF), jax.ShapeDtypeStruct((N, KVL), _BF)],
        compiler_params=_cparams(("parallel",)),
    )(proj, mla_q_norm_g, mla_kv_norm_g)
    q_p = _mm(cqn, wuq_p, name="mm_q", out_dtype=_BF, bn=2048)
    kv = _mm(ckvn, wukv, name="mm_kv", out_dtype=_BF, bn=2048)

    g_qn, g_qr = mla_qn_nope_g, pad_lanes(mla_qn_rope_g)
    g_kn, g_kr = mla_kn_nope_g, pad_lanes(mla_kn_rope_g)

    def krope_fwd_kern(p_ref, t_ref, g_ref, o_ref):
        xh, _ = _rms128(p_ref[...].astype(F32), c.ROPE)
        o_ref[...] = _rope(xh * g_ref[...], t_ref[0], t_ref[1], t_ref[2]).astype(o_ref.dtype)

    k_rope = pl.pallas_call(
        krope_fwd_kern, name="krope_fwd", grid=(nr,),
        in_specs=[rowb(LANES, 0), tabs_of(R), fullb((1, LANES))],
        out_specs=rowb(LANES, 0), out_shape=jax.ShapeDtypeStruct((N, LANES), _BF),
        compiler_params=_cparams(("parallel",)),
    )(kr_raw, tabs, g_kr)

    RP, HG = c.RP, c.HG
    nrp, nhg = N // RP, H // HG
    heads_in = pl.BlockSpec((RP, HG * HW), lambda i, hg: (i, hg))
    heads_out = pl.BlockSpec((HG, RP, HW), lambda i, hg: (hg, i, 0))

    def q_prep_kern(q_ref, t_ref, gn_ref, gr_ref, o_ref):
        for g in range(HG):
            blk = q_ref[:, g * HW:(g + 1) * HW].astype(F32)
            nh, _ = _rms128(blk[:, :LANES], c.NOPE)
            rhat, _ = _rms128(blk[:, LANES:], c.ROPE)
            rot = _rope(rhat * gr_ref[...], t_ref[0], t_ref[1], t_ref[2])
            o_ref[g] = (jnp.concatenate([nh * gn_ref[...], rot], axis=1) * (c.scale * LOG2E)).astype(o_ref.dtype)

    q_cat = pl.pallas_call(
        q_prep_kern, name="q_prep", grid=(nrp, nhg),
        in_specs=[heads_in, tabs_of(RP), fullb((1, LANES)), fullb((1, LANES))],
        out_specs=heads_out, out_shape=jax.ShapeDtypeStruct((H, N, HW), _BF),
        compiler_params=_cparams(("parallel", "parallel")),
    )(q_p, tabs, g_qn, g_qr)

    def k_prep_kern(kv_ref, kr_ref, gn_ref, o_ref):
        for g in range(HG):
            kn, _ = _rms128(kv_ref[:, g * HW:g * HW + LANES].astype(F32), c.NOPE)
            o_ref[g] = jnp.concatenate([(kn * gn_ref[...]).astype(o_ref.dtype), kr_ref[...]], axis=1)

    k_cat = pl.pallas_call(
        k_prep_kern, name="k_prep", grid=(nrp, nhg),
        in_specs=[heads_in, pl.BlockSpec((RP, LANES), lambda i, hg: (i, 0)), fullb((1, LANES))],
        out_specs=heads_out, out_shape=jax.ShapeDtypeStruct((H, N, HW), _BF),
        compiler_params=_cparams(("parallel", "parallel")),
    )(kv, k_rope, g_kn)

    QB = ATT_QBLOCKS if N % (ATT_QBLOCKS * B) == 0 else 1
    BQ = QB * B
    nq = N // BQ
    assert CHUNK & (CHUNK - 1) == 0 and B % CHUNK == 0 and ATT_UNROLL % QB == 0

    def live_rows(diag):
        return slice(0 if diag is None else diag * B, BQ)

    def diag_mask(s):
        row = lax.broadcasted_iota(jnp.int32, s.shape, 0)
        col = lax.broadcasted_iota(jnp.int32, s.shape, 1)
        shift = CHUNK.bit_length() - 1
        allowed = jnp.right_shift(col, shift) <= jnp.right_shift(row, shift)
        return jnp.where(allowed, s, NEG)

    k_head = pl.BlockSpec((1, N, HW), lambda hh, i: (hh, 0, 0))
    v_head = pl.BlockSpec((N, LANES), lambda hh, i: (0, 2 * hh + 1))
    q_blk = pl.BlockSpec((1, BQ, HW), lambda hh, i: (hh, i, 0))
    o_blk = pl.BlockSpec((BQ, LANES), lambda hh, i: (i, hh))
    lse_blk = pl.BlockSpec((1, BQ, LANES), lambda hh, i: (hh, i, 0))

    def key_block_plan(i, run):
        def unrolled(u, carry):
            run(ATT_UNROLL * u, ATT_UNROLL, 0)
            return carry

        n_full = QB * i
        lax.fori_loop(0, n_full // ATT_UNROLL, unrolled, 0)
        for rem in range(0, ATT_UNROLL, QB):
            @pl.when(n_full % ATT_UNROLL == rem)
            def _(rem=rem):
                run(n_full - rem, rem + QB, QB)

    def attn_fwd_kern(q_ref, k_ref, v_ref, o_ref, lse_ref, m_sc, acc_sc, s_sc):
        i = pl.program_id(1)
        m_sc[...] = jnp.full(m_sc.shape, NEG, F32)
        acc_sc[...] = jnp.zeros(acc_sc.shape, F32)

        def rows_of(t):
            return pl.ds(pl.multiple_of(t * B, B), B)

        def scores(t, slot, diag):
            rs = live_rows(diag)
            s_sc[slot, rs, :] = _dot_nt(q_ref[0, rs, :], k_ref[0, rows_of(t), :])

        def softmax_pv(t, slot, diag):
            rs = live_rows(diag)
            s = s_sc[slot, rs, :]
            if diag is not None:
                s = diag_mask(s)
            mt = s[:, 0:LANES]
            for cb in range(1, B // LANES):
                mt = jnp.maximum(mt, s[:, cb * LANES:(cb + 1) * LANES])
            m_prev = m_sc[rs, :]
            m_new = jnp.maximum(m_prev, jnp.max(mt, axis=1, keepdims=True))
            alpha = jnp.exp2(m_prev - m_new)
            p = jnp.concatenate([jnp.exp2(s[:, cb * LANES:(cb + 1) * LANES] - m_new).astype(_BF)
                                 for cb in range(B // LANES)], axis=1)
            v_ones = jnp.concatenate([v_ref[rows_of(t), :], jnp.ones((B, LANES), _BF)], axis=1)
            acc_sc[rs, :] = jnp.concatenate([alpha, alpha], axis=1) * acc_sc[rs, :] + _dot(p, v_ones)
            m_sc[rs, :] = m_new

        scores(0, 0, None)

        def run(first, count, n_diag):
            def diag_of(u):
                return u - (count - n_diag) if count - n_diag <= u < count else None

            for u in range(count):
                if u + 1 < count or n_diag == 0:
                    scores(first + u + 1, (u + 1) % 2, diag_of(u + 1))
                softmax_pv(first + u, u % 2, diag_of(u))

        key_block_plan(i, run)
        acc = acc_sc[...]
        o_ref[...] = (acc[:, :LANES] / acc[:, LANES:]).astype(o_ref.dtype)
        lse_ref[0] = m_sc[...] + jnp.log2(acc[:, LANES:])

    mla_y, lse = pl.pallas_call(
        attn_fwd_kern, name="attn_fwd", grid=(H, nq),
        in_specs=[q_blk, k_head, v_head], out_specs=[o_blk, lse_blk],
        out_shape=[jax.ShapeDtypeStruct((N, c.HV), _BF), jax.ShapeDtypeStruct((H, N, LANES), F32)],
        scratch_shapes=[pltpu.VMEM((BQ, LANES), F32), pltpu.VMEM((BQ, HW), F32), pltpu.VMEM((2, BQ, B), F32)],
        compiler_params=_cparams(("parallel", "arbitrary")),
    )(q_cat, k_cat, kv)

    HV = c.HV
    mz_blk = c.o_mz // HV

    def gate_fwd_kern(y_ref, z_ref, o_ref):
        z = z_ref[...].astype(F32)
        o_ref[...] = (y_ref[...].astype(F32) * (z * _sig(z))).astype(o_ref.dtype)

    a_mla = pl.pallas_call(
        gate_fwd_kern, name="gate_mla", grid=(nr2,),
        in_specs=[rowb2(HV, 0), rowb2(HV, mz_blk)], out_specs=rowb2(HV, 0),
        out_shape=jax.ShapeDtypeStruct((N, HV), _BF), compiler_params=_cparams(("parallel",)),
    )(mla_y, proj)
    o_mla = _mm(a_mla, wmo, name="mm_omla", out_dtype=_BF)

    M, MW, MH, MHD = c.M, c.MW, c.MH, c.MHD
    memn = pl.pallas_call(
        make_rms_kern(), name="rms_mem",
        grid=(1,), in_specs=[fullb((M, D)), fullb((1, D))], out_specs=fullb((M, D)),
        out_shape=jax.ShapeDtypeStruct((M, D), _BF), compiler_params=_cparams(("arbitrary",)),
    )(memx, mem_norm_g)
    kvm = _mm(memn, wmkv, name="mm_memkv", out_dtype=F32)

    def memk_fwd_kern(kv_ref, g_ref, k_ref, v_ref):
        for hh in range(MH):
            kh, _ = _rms(kv_ref[:, hh * MHD:(hh + 1) * MHD], MHD)
            k_ref[:, hh * MHD:(hh + 1) * MHD] = (kh * g_ref[...]).astype(k_ref.dtype)
        v_ref[...] = kv_ref[:, MW:].astype(v_ref.dtype)

    mem_k, mem_v = pl.pallas_call(
        memk_fwd_kern, name="memk_fwd", grid=(1,),
        in_specs=[fullb((M, 2 * MW)), fullb((1, MHD))], out_specs=[fullb((M, MW)), fullb((M, MW))],
        out_shape=[jax.ShapeDtypeStruct((M, MW), _BF)] * 2, compiler_params=_cparams(("arbitrary",)),
    )(kvm, mem_kn_g)

    mem_blk = c.o_mem // (2 * MW)

    def mem_head(qz_ref, k_ref, v_ref, g_ref, hh):
        sl = slice(hh * MHD, (hh + 1) * MHD)
        qh, r = _rms(qz_ref[:, sl].astype(F32), MHD)
        qn = (qh * g_ref[...]).astype(_BF)
        s = _dot_nt(qn, k_ref[:, sl]) * c.mscale
        e = jnp.exp(s - jnp.max(s, axis=1, keepdims=True))
        p = e / jnp.sum(e, axis=1, keepdims=True)
        y = _dot(p.astype(_BF), v_ref[:, sl])
        z = qz_ref[:, MW + hh * MHD:MW + (hh + 1) * MHD].astype(F32)
        return sl, qh, r, qn, p, y, z

    def mem_fwd_kern(qz_ref, k_ref, v_ref, g_ref, o_ref):
        for hh in range(MH):
            sl, _, _, _, _, y, z = mem_head(qz_ref, k_ref, v_ref, g_ref, hh)
            o_ref[:, sl] = (y * (z * _sig(z))).astype(o_ref.dtype)

    a_mem = pl.pallas_call(
        mem_fwd_kern, name="mem_fwd", grid=(N // RM,),
        in_specs=[rowb(2 * MW, mem_blk, RM), fullb((M, MW)), fullb((M, MW)), fullb((1, MHD))],
        out_specs=rowb(MW, 0, RM), out_shape=jax.ShapeDtypeStruct((N, MW), _BF),
        compiler_params=_cparams(("parallel",)),
    )(proj, mem_k, mem_v, mem_qn_g)
    o_mem = _mm(a_mem, wmemo, name="mm_omem", out_dtype=_BF)

    g_blk = c.o_g // (3 * D)

    def merge_fwd_kern(g_ref, oc_ref, om_ref, ome_ref, o_ref):
        g = g_ref[...].astype(F32)
        acc = _sig(g[:, :D]) * oc_ref[...].astype(F32)
        acc += _sig(g[:, D:2 * D]) * om_ref[...].astype(F32)
        acc += _sig(g[:, 2 * D:]) * ome_ref[...].astype(F32)
        o_ref[...] = acc.astype(o_ref.dtype)

    merged = pl.pallas_call(
        merge_fwd_kern, name="merge_fwd", grid=(nr,),
        in_specs=[rowb(3 * D, g_blk), rowb(D, 0), rowb(D, 0), rowb(D, 0)], out_specs=rowb(D, 0),
        out_shape=jax.ShapeDtypeStruct((N, D), _BF), compiler_params=_cparams(("parallel",)),
    )(proj, o_conv, o_mla, o_mem)
    obm, obn = _tile(N, 512, 16), _tile(D, 1024, LANES)

    def out_loss_kern(a_ref, w_ref, x_ref, t_ref, dy_ref, dyb_ref, l_ref):
        e = x_ref[...] + _dot(a_ref[...], w_ref[...]) - t_ref[...]
        dy = e * (1.0 / D)
        dy_ref[...] = dy
        dyb_ref[...] = dy.astype(dyb_ref.dtype)
        row = lax.broadcasted_iota(jnp.int32, l_ref.shape, 0)
        l_ref[...] = jnp.where(row == 0, jnp.sum(e * e, axis=0, keepdims=True), 0.0)

    oblk = pl.BlockSpec((obm, obn), lambda i, j: (i, j))
    dy, dyb, lpart = pl.pallas_call(
        out_loss_kern, name="mm_out_loss", grid=(N // obm, D // obn),
        in_specs=[pl.BlockSpec((obm, D), lambda i, j: (i, 0)), pl.BlockSpec((D, obn), lambda i, j: (0, j)), oblk, oblk],
        out_specs=[oblk, oblk, pl.BlockSpec((8, obn), lambda i, j: (i, j))],
        out_shape=[jax.ShapeDtypeStruct((N, D), F32), jax.ShapeDtypeStruct((N, D), _BF),
                   jax.ShapeDtypeStruct((8 * (N // obm), D), F32)],
        compiler_params=_cparams(("parallel", "parallel")),
    )(merged, wo, xs, tgt)
    loss = lax.psum(jnp.sum(lpart) * (0.5 / D), AXES)

    G = {}
    d_merged = _mm(dyb, wo, tb=True, name="mm_dmerged", out_dtype=_BF)
    G['w_o'] = _mm(merged, dyb, ta=True, name="mm_dwo", out_dtype=_BF, bk=4096)

    dproj0 = lax.empty((N, c.P), _BF)
    any_spec = pl.BlockSpec(memory_space=pl.ANY)

    def merge_bwd_kern(dp_any, g_ref, dm_ref, oc_ref, om_ref, ome_ref, dg_ref, doc_ref, dom_ref, dome_ref):
        g = g_ref[...].astype(F32)
        dm = dm_ref[...].astype(F32)
        for idx, (o_in, d_out) in enumerate(((oc_ref, doc_ref), (om_ref, dom_ref), (ome_ref, dome_ref))):
            sg = _sig(g[:, idx * D:(idx + 1) * D])
            d_out[...] = (sg * dm).astype(d_out.dtype)
            dg_ref[:, idx * D:(idx + 1) * D] = (dm * o_in[...].astype(F32) * sg * (1.0 - sg)).astype(dg_ref.dtype)

    dproj1, d_oconv, d_omla, d_omem = pl.pallas_call(
        merge_bwd_kern, name="merge_bwd", grid=(nr,),
        in_specs=[any_spec, rowb(3 * D, g_blk), rowb(D, 0), rowb(D, 0), rowb(D, 0), rowb(D, 0)],
        out_specs=[rowb(3 * D, g_blk), rowb(D, 0), rowb(D, 0), rowb(D, 0)],
        out_shape=[jax.ShapeDtypeStruct((N, c.P), _BF)] + [jax.ShapeDtypeStruct((N, D), _BF)] * 3,
        input_output_aliases={0: 0}, compiler_params=_cparams(("parallel",)),
    )(dproj0, proj, d_merged, o_conv, o_mla, o_mem)

    G['w_conv_out'] = _mm(a_conv, d_oconv, ta=True, name="mm_dwco", out_dtype=_BF, bk=4096)
    d_aconv = _mm(d_oconv, wco, tb=True, name="mm_daconv", out_dtype=_BF)
    G['w_mla_out'] = _mm(a_mla, d_omla, ta=True, name="mm_dwmo", out_dtype=_BF, bk=4096)
    d_amla = _mm(d_omla, wmo, tb=True, name="mm_damla", out_dtype=_BF)
    G['w_mem_out'] = _mm(a_mem, d_omem, ta=True, name="mm_dwmemo", out_dtype=_BF, bk=4096)
    d_amem = _mm(d_omem, wmemo, tb=True, name="mm_damem", out_dtype=_BF)

    def conv_bwd_kern(dp_any, p_ref, prev_ref, next_ref, da_ref, dan_ref, w_ref, o_ref, dw_ref):
        i = pl.program_id(0)
        cg, bg, u, z = conv_parts(p_ref[...])
        pc, _, pu, _ = conv_parts(prev_ref[...])
        _, nbg, _, nz = conv_parts(next_ref[...])
        cu = cg * u
        sh1, sh2 = shifted(cu, pc * pu, i)
        w = w_ref[...]
        conv = w[0:1, :] * sh2 + w[1:2, :] * sh1 + w[2:3, :] * cu
        sg = _sig(z)
        sz = z * sg
        da = da_ref[...].astype(F32)
        dcy = da * sz
        d_z = da * (bg * conv) * (sg * (1.0 + z * (1.0 - sg)))
        d_b = dcy * conv
        dconv = dcy * bg
        dnext = dan_ref[...].astype(F32) * (nz * _sig(nz)) * nbg
        dnext = jnp.where(i < nr - 1, dnext, 0.0)
        rid = lax.broadcasted_iota(jnp.int32, cu.shape, 0)
        up1 = jnp.where(rid == R - 1, dnext[0:1, :], pltpu.roll(dconv, R - 1, 0))
        up2 = jnp.where(rid == R - 2, dnext[0:1, :], jnp.where(rid == R - 1, dnext[1:2, :], pltpu.roll(dconv, R - 2, 0)))
        dcu = w[2:3, :] * dconv + w[1:2, :] * up1 + w[0:1, :] * up2
        o_ref[:, 0:CW] = (dcu * u).astype(o_ref.dtype)
        o_ref[:, CW:2 * CW] = d_b.astype(o_ref.dtype)
        o_ref[:, 2 * CW:3 * CW] = (dcu * cg).astype(o_ref.dtype)
        o_ref[:, 3 * CW:4 * CW] = d_z.astype(o_ref.dtype)

        @pl.when(i == 0)
        def _():
            dw_ref[...] = jnp.zeros(dw_ref.shape, F32)

        dw_ref[0:1, :] += jnp.sum(dconv * sh2, axis=0, keepdims=True)
        dw_ref[1:2, :] += jnp.sum(dconv * sh1, axis=0, keepdims=True)
        dw_ref[2:3, :] += jnp.sum(dconv * cu, axis=0, keepdims=True)

    next_spec = pl.BlockSpec((HALO, 4 * CW), lambda i: (jnp.minimum((i + 1) * rh, N // HALO - 1), conv_blk))
    dan_spec = pl.BlockSpec((HALO, CW), lambda i: (jnp.minimum((i + 1) * rh, N // HALO - 1), 0))
    dproj2, g_convw = pl.pallas_call(
        conv_bwd_kern, name="conv_bwd", grid=(nr,),
        in_specs=[any_spec, rowb(4 * CW, conv_blk), prev_spec, next_spec, rowb(CW, 0), dan_spec, fullb((8, CW))],
        out_specs=[rowb(4 * CW, conv_blk), fullb((8, CW))],
        out_shape=[jax.ShapeDtypeStruct((N, c.P), _BF), jax.ShapeDtypeStruct((8, CW), F32)],
        input_output_aliases={0: 0}, compiler_params=_cparams(("arbitrary",)),
    )(dproj1, proj, proj, proj, d_aconv, d_aconv, convw8)

    def mem_bwd_kern(dp_any, qz_ref, da_ref, k_ref, v_ref, g_ref, o_ref, dk_ref, dv_ref, dg_ref):
        @pl.when(pl.program_id(0) == 0)
        def _():
            dk_ref[...] = jnp.zeros(dk_ref.shape, F32)
            dv_ref[...] = jnp.zeros(dv_ref.shape, F32)
            dg_ref[...] = jnp.zeros(dg_ref.shape, F32)

        for hh in range(MH):
            sl, qh, r, qn, p, y, z = mem_head(qz_ref, k_ref, v_ref, g_ref, hh)
            da = da_ref[:, sl].astype(F32)
            sg = _sig(z)
            dyh = da * (z * sg)
            o_ref[:, MW + hh * MHD:MW + (hh + 1) * MHD] = (da * y * (sg * (1.0 + z * (1.0 - sg)))).astype(o_ref.dtype)
            dyb_h = dyh.astype(_BF)
            dpm = _dot_nt(dyb_h, v_ref[:, sl])
            ds = (p * (dpm - jnp.sum(dpm * p, axis=1, keepdims=True)) * c.mscale).astype(_BF)
            dqn = _dot(ds, k_ref[:, sl])
            dk_ref[:, sl] += _dot_tn(ds, qn)
            dv_ref[:, sl] += _dot_tn(p.astype(_BF), dyb_h)
            dq, dgp = _rms_bwd(qh, r, g_ref[...], dqn, MHD)
            o_ref[:, sl] = dq.astype(o_ref.dtype)
            dg_ref[...] += jnp.sum(dgp, axis=0, keepdims=True)

    dproj3, d_memk, d_memv, g_mem_qn = pl.pallas_call(
        mem_bwd_kern, name="mem_bwd", grid=(N // RM,),
        in_specs=[any_spec, rowb(2 * MW, mem_blk, RM), rowb(MW, 0, RM), fullb((M, MW)), fullb((M, MW)), fullb((1, MHD))],
        out_specs=[rowb(2 * MW, mem_blk, RM), fullb((M, MW)), fullb((M, MW)), fullb((1, MHD))],
        out_shape=[jax.ShapeDtypeStruct((N, c.P), _BF), jax.ShapeDtypeStruct((M, MW), F32),
                   jax.ShapeDtypeStruct((M, MW), F32), jax.ShapeDtypeStruct((1, MHD), F32)],
        input_output_aliases={0: 0}, compiler_params=_cparams(("arbitrary",)),
    )(dproj2, proj, d_amem, mem_k, mem_v, mem_qn_g)

    def memk_bwd_kern(kv_ref, dk_ref, dv_ref, g_ref, o_ref, dg_ref):
        dg = jnp.zeros((1, MHD), F32)
        for hh in range(MH):
            sl = slice(hh * MHD, (hh + 1) * MHD)
            kh, r = _rms(kv_ref[:, sl], MHD)
            dkr, dgp = _rms_bwd(kh, r, g_ref[...], dk_ref[:, sl], MHD)
            o_ref[:, sl] = dkr.astype(o_ref.dtype)
            dg += jnp.sum(dgp, axis=0, keepdims=True)
        o_ref[:, MW:] = dv_ref[...].astype(o_ref.dtype)
        dg_ref[...] = dg

    d_kvm, g_mem_kn = pl.pallas_call(
        memk_bwd_kern, name="memk_bwd", grid=(1,),
        in_specs=[fullb((M, 2 * MW)), fullb((M, MW)), fullb((M, MW)), fullb((1, MHD))],
        out_specs=[fullb((M, 2 * MW)), fullb((1, MHD))],
        out_shape=[jax.ShapeDtypeStruct((M, 2 * MW), _BF), jax.ShapeDtypeStruct((1, MHD), F32)],
        compiler_params=_cparams(("arbitrary",)),
    )(kvm, d_memk, d_memv, mem_kn_g)
    G['w_mem_kv'] = _mm(memn, d_kvm, ta=True, name="mm_dwmkv", out_dtype=_BF)
    d_memn = _mm(d_kvm, wmkv, tb=True, name="mm_dmemn", out_dtype=F32)

    def memnorm_bwd_kern(x_ref, d_ref, dg_ref):
        xh, _ = _rms(x_ref[...], D)
        dg_ref[...] = jnp.sum(d_ref[...] * xh, axis=0, keepdims=True)

    g_mem_norm = pl.pallas_call(
        memnorm_bwd_kern, name="memnorm_bwd", grid=(1,),
        in_specs=[fullb((M, D)), fullb((M, D))], out_specs=fullb((1, D)),
        out_shape=jax.ShapeDtypeStruct((1, D), F32), compiler_params=_cparams(("arbitrary",)),
    )(memx, d_memn)

    def gate_bwd_kern(dp_any, da_ref, y_ref, z_ref, dy_ref, dz_ref):
        z = z_ref[...].astype(F32)
        da = da_ref[...].astype(F32)
        sg = _sig(z)
        dy_ref[...] = (da * (z * sg)).astype(dy_ref.dtype)
        dz_ref[...] = (da * y_ref[...].astype(F32) * (sg * (1.0 + z * (1.0 - sg)))).astype(dz_ref.dtype)

    d_mlay, dproj4 = pl.pallas_call(
        gate_bwd_kern, name="gate_mla_bwd", grid=(nr2,),
        in_specs=[any_spec, rowb2(HV, 0), rowb2(HV, 0), rowb2(HV, mz_blk)],
        out_specs=[rowb2(HV, 0), rowb2(HV, mz_blk)],
        out_shape=[jax.ShapeDtypeStruct((N, HV), _BF), jax.ShapeDtypeStruct((N, c.P), _BF)],
        input_output_aliases={0: 1}, compiler_params=_cparams(("parallel",)),
    )(dproj3, d_amla, mla_y, proj)

    def attn_bwd_kern(q_ref, k_ref, v_ref, o_ref, do_ref, lse_ref, dq_ref, dk_ref, dv_ref, dq_sc, dl_sc, dk_sc, dv_sc):
        i = pl.program_id(1)
        delta = jnp.sum(do_ref[...].astype(F32) * o_ref[...].astype(F32), axis=1, keepdims=True)
        dl_sc[...] = jnp.broadcast_to(delta, dl_sc.shape)
        dq_sc[...] = jnp.zeros(dq_sc.shape, F32)

        def step(t, diag):
            rows = pl.ds(pl.multiple_of(t * B, B), B)
            rs = live_rows(diag)
            q, do = q_ref[0, rs, :], do_ref[rs, :]
            k = k_ref[0, rows, :]
            s = _dot_nt(q, k)
            if diag is not None:
                s = diag_mask(s)
            dpm = _dot_nt(do, v_ref[rows, :])
            lse_t, dl = lse_ref[0, rs, :], dl_sc[rs, :]
            ps, dss = [], []
            for cb in range(B // LANES):
                cols = slice(cb * LANES, (cb + 1) * LANES)
                p_cb = jnp.exp2(s[:, cols] - lse_t)
                ps.append(p_cb.astype(_BF))
                dss.append((p_cb * (dpm[:, cols] - dl)).astype(_BF))
            p, ds = jnp.concatenate(ps, axis=1), jnp.concatenate(dss, axis=1)
            dvp = _dot_tn(p, do)
            dkp = _dot_tn(ds, q)
            if diag is not None:
                dk_sc[rows, :] = dkp
                dv_sc[rows, :] = dvp
            else:
                dk_sc[rows, :] += dkp
                dv_sc[rows, :] += dvp
            dq_sc[rs, :] += _dot(ds, k)

        def run(first, count, n_diag):
            for u in range(count):
                step(first + u, u - (count - n_diag) if u >= count - n_diag else None)

        key_block_plan(i, run)
        dq_ref[0] = dq_sc[...].astype(dq_ref.dtype)

        @pl.when(i == nq - 1)
        def _():
            dk_ref[0] = dk_sc[...].astype(dk_ref.dtype)
            dv_ref[0] = dv_sc[...].astype(dv_ref.dtype)

    d_qcat, d_kcat, d_v = pl.pallas_call(
        attn_bwd_kern, name="attn_bwd", grid=(H, nq),
        in_specs=[q_blk, k_head, v_head, o_blk, o_blk, lse_blk],
        out_specs=[pl.BlockSpec((1, BQ, HW), lambda hh, i: (hh, i, 0)),
                   pl.BlockSpec((1, N, HW), lambda hh, i: (hh, 0, 0)),
                   pl.BlockSpec((1, N, LANES), lambda hh, i: (hh, 0, 0))],
        out_shape=[jax.ShapeDtypeStruct((H, N, HW), _BF), jax.ShapeDtypeStruct((H, N, HW), _BF),
                   jax.ShapeDtypeStruct((H, N, LANES), _BF)],
        scratch_shapes=[pltpu.VMEM((BQ, HW), F32), pltpu.VMEM((BQ, LANES), F32), pltpu.VMEM((N, HW), F32),
                        pltpu.VMEM((N, LANES), F32)],
        compiler_params=_cparams(("parallel", "arbitrary")),
    )(q_cat, k_cat, kv, mla_y, d_mlay, lse)

    def q_prep_bwd_kern(q_ref, dq_ref, t_ref, gn_ref, gr_ref, o_ref, dgn_ref, dgr_ref):
        @pl.when(jnp.logical_and(pl.program_id(0) == 0, pl.program_id(1) == 0))
        def _():
            dgn_ref[...] = jnp.zeros(dgn_ref.shape, F32)
            dgr_ref[...] = jnp.zeros(dgr_ref.shape, F32)

        for g in range(HG):
            blk = q_ref[:, g * HW:(g + 1) * HW].astype(F32)
            d = dq_ref[g].astype(F32) * c.scale
            nh, rn = _rms128(blk[:, :LANES], c.NOPE)
            rhat, rr = _rms128(blk[:, LANES:], c.ROPE)
            dn, dgn = _rms128_bwd(nh, rn, gn_ref[...], d[:, :LANES], c.NOPE)
            drot = _rope_t(d[:, LANES:], t_ref[0], t_ref[1], t_ref[2])
            dr, dgr = _rms128_bwd(rhat, rr, gr_ref[...], drot, c.ROPE)
            o_ref[:, g * HW:(g + 1) * HW] = jnp.concatenate([dn, dr], axis=1).astype(o_ref.dtype)
            dgn_ref[...] += jnp.sum(dgn, axis=0, keepdims=True)
            dgr_ref[...] += jnp.sum(dgr, axis=0, keepdims=True)

    d_qp, g_qn_nope, g_qn_rope = pl.pallas_call(
        q_prep_bwd_kern, name="q_prep_bwd", grid=(nrp, nhg),
        in_specs=[heads_in, heads_out, tabs_of(RP), fullb((1, LANES)), fullb((1, LANES))],
        out_specs=[heads_in, fullb((1, LANES)), fullb((1, LANES))],
        out_shape=[jax.ShapeDtypeStruct((N, H * HW), _BF), jax.ShapeDtypeStruct((1, LANES), F32),
                   jax.ShapeDtypeStruct((1, LANES), F32)],
        compiler_params=_cparams(("arbitrary", "arbitrary")),
    )(q_p, d_qcat, tabs, g_qn, g_qr)

    def k_prep_bwd_kern(kv_ref, dk_ref, dv_ref, gn_ref, o_ref, dkr_ref, dgn_ref):
        hg = pl.program_id(1)

        @pl.when(jnp.logical_and(pl.program_id(0) == 0, hg == 0))
        def _():
            dgn_ref[...] = jnp.zeros(dgn_ref.shape, F32)

        @pl.when(hg == 0)
        def _():
            dkr_ref[...] = jnp.zeros(dkr_ref.shape, F32)

        dkr = jnp.zeros((RP, LANES), F32)
        for g in range(HG):
            dk = dk_ref[g].astype(F32) * (1.0 / LOG2E)
            kn, r = _rms128(kv_ref[:, g * HW:g * HW + LANES].astype(F32), c.NOPE)
            dkn, dgn = _rms128_bwd(kn, r, gn_ref[...], dk[:, :LANES], c.NOPE)
            o_ref[:, g * HW:(g + 1) * HW] = jnp.concatenate([dkn.astype(o_ref.dtype), dv_ref[g]], axis=1)
            dgn_ref[...] += jnp.sum(dgn, axis=0, keepdims=True)
            dkr += dk[:, LANES:]
        dkr_ref[...] += dkr

    d_kv, d_krsum, g_kn_nope = pl.pallas_call(
        k_prep_bwd_kern, name="k_prep_bwd", grid=(nrp, nhg),
        in_specs=[heads_in, heads_out, pl.BlockSpec((HG, RP, LANES), lambda i, hg: (hg, i, 0)), fullb((1, LANES))],
        out_specs=[heads_in, pl.BlockSpec((RP, LANES), lambda i, hg: (i, 0)), fullb((1, LANES))],
        out_shape=[jax.ShapeDtypeStruct((N, H * HW), _BF), jax.ShapeDtypeStruct((N, LANES), F32),
                   jax.ShapeDtypeStruct((1, LANES), F32)],
        compiler_params=_cparams(("arbitrary", "arbitrary")),
    )(kv, d_kcat, d_v, g_kn)

    def krope_bwd_kern(p_ref, d_ref, t_ref, g_ref, o_ref, dg_ref):
        @pl.when(pl.program_id(0) == 0)
        def _():
            dg_ref[...] = jnp.zeros(dg_ref.shape, F32)

        xh, r = _rms128(p_ref[...].astype(F32), c.ROPE)
        drot = _rope_t(d_ref[...], t_ref[0], t_ref[1], t_ref[2])
        dx, dg = _rms128_bwd(xh, r, g_ref[...], drot, c.ROPE)
        o_ref[...] = dx.astype(o_ref.dtype)
        dg_ref[...] += jnp.sum(dg, axis=0, keepdims=True)

    d_kr, g_kn_rope = pl.pallas_call(
        krope_bwd_kern, name="krope_bwd", grid=(nr,),
        in_specs=[rowb(LANES, 0), rowb(LANES, 0), tabs_of(R), fullb((1, LANES))],
        out_specs=[rowb(LANES, 0), fullb((1, LANES))],
        out_shape=[jax.ShapeDtypeStruct((N, LANES), _BF), jax.ShapeDtypeStruct((1, LANES), F32)],
        compiler_params=_cparams(("arbitrary",)),
    )(kr_raw, d_krsum, tabs, g_kr)
    dproj5 = dproj4

    g_wuq_p = _mm(cqn, d_qp, ta=True, name="mm_dwuq", out_dtype=_BF, bk=4096)
    G['w_uq'] = g_wuq_p.reshape(QL, H, HW)[:, :, :c.NOPE + c.ROPE].reshape(QL, H * (c.NOPE + c.ROPE))
    d_cqn = _mm(d_qp, wuq_p, tb=True, name="mm_dcqn", out_dtype=F32, bk=4096)
    G['w_ukv'] = _mm(ckvn, d_kv, ta=True, name="mm_dwukv", out_dtype=_BF, bk=4096)
    d_ckvn = _mm(d_kv, wukv, tb=True, name="mm_dckvn", out_dtype=F32, bk=4096)

    def to_blocks(n, g):
        if n in COL_SHARDED:
            return jnp.transpose(g.reshape(g.shape[0], NDEV, -1), (1, 0, 2))
        return g.reshape(NDEV, -1, g.shape[1])

    def landing(b):
        own = lax.dynamic_index_in_dim(b, me, 0, keepdims=True)
        return lax.dynamic_update_index_in_dim(lax.empty(b.shape, b.dtype), own, me, 0)

    early = [n for n in BIG if n != 'w_in']
    blocks_e = [to_blocks(n, G[n]) for n in early]
    xe = _split_start(blocks_e, [landing(b) for b in blocks_e], False, "xchg_early_start")
    gq_after = mla_q_norm_g + xe[4][0:1, 0:1]

    def lora_bwd_kern(dp_any, p_ref, dq_ref, dkv_ref, gq_ref, gkv_ref, o_ref, dgq_ref, dgkv_ref):
        @pl.when(pl.program_id(0) == 0)
        def _():
            dgq_ref[...] = jnp.zeros(dgq_ref.shape, F32)
            dgkv_ref[...] = jnp.zeros(dgkv_ref.shape, F32)

        blk = p_ref[...].astype(F32)
        qh, rq = _rms(blk[:, :QL], QL)
        kh, rk = _rms(blk[:, QL:], KVL)
        dq, dgq = _rms_bwd(qh, rq, gq_ref[...], dq_ref[...], QL)
        dk, dgk = _rms_bwd(kh, rk, gkv_ref[...], dkv_ref[...], KVL)
        o_ref[:, :QL] = dq.astype(o_ref.dtype)
        o_ref[:, QL:] = dk.astype(o_ref.dtype)
        dgq_ref[...] += jnp.sum(dgq, axis=0, keepdims=True)
        dgkv_ref[...] += jnp.sum(dgk, axis=0, keepdims=True)

    dproj6, g_q_norm, g_kv_norm = pl.pallas_call(
        lora_bwd_kern, name="lora_bwd", grid=(nr,),
        in_specs=[any_spec, rowb(QL + KVL, lora_blk), rowb(QL, 0), rowb(KVL, 0), fullb((1, QL)), fullb((1, KVL))],
        out_specs=[rowb(QL + KVL, lora_blk), fullb((1, QL)), fullb((1, KVL))],
        out_shape=[jax.ShapeDtypeStruct((N, c.P), _BF), jax.ShapeDtypeStruct((1, QL), F32),
                   jax.ShapeDtypeStruct((1, KVL), F32)],
        input_output_aliases={0: 0}, compiler_params=_cparams(("arbitrary",)),
    )(dproj5, proj, d_cqn, d_ckvn, gq_after, mla_kv_norm_g)

    g_win_p = _mm(h, dproj6, ta=True, name="mm_dwin", out_dtype=_BF, bk=4096)
    g_wkr = _mm(h, d_kr, ta=True, name="mm_dwkr", out_dtype=_BF, bk=4096)
    blocks_w = [_win_blocks(g_win_p, g_wkr, c)]
    xw = _split_start(blocks_w, [landing(b) for b in blocks_w], False, "xchg_win_start")
    d_h = _mm(dproj6, win_pT, name="mm_dh", out_dtype=F32, bk=3072, after=xw[4], plus=(d_kr, w_krT))

    def final_bwd_kern(x_ref, g_ref, dh_ref, dy_ref, gx_ref, dg_ref):
        @pl.when(pl.program_id(0) == 0)
        def _():
            dg_ref[...] = jnp.zeros(dg_ref.shape, F32)

        xh, r = _rms(x_ref[...], D)
        dx, dg = _rms_bwd(xh, r, g_ref[...], dh_ref[...], D)
        gx_ref[...] = dy_ref[...] + dx
        dg_ref[...] += jnp.sum(dg, axis=0, keepdims=True)

    grad_x, g_norm = pl.pallas_call(
        final_bwd_kern, name="final_bwd", grid=(nr2,),
        in_specs=[rowb2(D, 0), fullb((1, D)), rowb2(D, 0), rowb2(D, 0)],
        out_specs=[rowb2(D, 0), fullb((1, D))],
        out_shape=[jax.ShapeDtypeStruct((N, D), F32), jax.ShapeDtypeStruct((1, D), F32)],
        compiler_params=_cparams(("arbitrary",)),
    )(xs, norm_g, d_h, dy)

    res = [{}, {}, {}, {}]

    def adam_into(n, parts):
        outs = _adam(parts, W[n][0], Mo[n][0], Vo[n][0], "adam_" + n)
        for k in range(4):
            res[k][n] = outs[k][None]
        return outs[0]

    recv_e = _split_wait(xe, False, grad_x, "xchg_early_wait")
    last = [adam_into(n, parts) for n, parts in zip(early, recv_e)][-1]
    recv_w = _split_wait(xw, False, last, "xchg_win_wait")
    adam_into('w_in', recv_w[0])

    small_g = {'norm_g': g_norm, 'mla_q_norm_g': g_q_norm, 'mla_kv_norm_g': g_kv_norm,
               'mla_qn_nope_g': g_qn_nope, 'mla_qn_rope_g': g_qn_rope[:, :c.ROPE], 'mla_kn_nope_g': g_kn_nope,
               'mla_kn_rope_g': g_kn_rope[:, :c.ROPE], 'mem_norm_g': g_mem_norm, 'mem_qn_g': g_mem_qn,
               'mem_kn_g': g_mem_kn}
    small_part = _pack([small_g[n] for n in SMALL] + [g_convw[0:3, :]], 0)
    small_all = _all_gather([small_part], "ag_small_grads")[0]
    small_shapes = [W[n].shape for n in SMALL]
    pieces = _unpack(small_all, small_shapes + [(3, CW)])
    cw8 = CW // NDEV
    conv_mine = lax.dynamic_slice_in_dim(pieces[-1].reshape(NDEV, 3, NDEV, cw8), me, 1, axis=2)[:, :, 0, :]
    sm_parts = _pack(pieces[:-1] + [conv_mine], 1)
    sm_names = SMALL + ['conv_w']
    sm_shapes = small_shapes + [(3, cw8)]
    w_sm = _pack([W[n] for n in SMALL] + [conv_w[0]], 0)
    m_sm = _pack([Mo[n] for n in SMALL] + [m_conv_w[0]], 0)
    v_sm = _pack([Vo[n] for n in SMALL] + [v_conv_w[0]], 0)
    outs_sm = [_unpack(o, sm_shapes) for o in _adam(sm_parts, w_sm, m_sm, v_sm, "adam_small")]
    for k in range(4):
        for n, a in zip(sm_names, outs_sm[k]):
            res[k][n] = a[None] if n == 'conv_w' else a
    return (loss, grad_x[None], *[res[0][n] for n in WEIGHTS], *[res[1][n] for n in WEIGHTS],
            *[res[2][n] for n in WEIGHTS], *[res[3][n] for n in WEIGHTS])
```

```python
import math

import jax
import jax.numpy as jnp
from jax import lax
from jax.experimental import pallas as pl
from jax.experimental.pallas import tpu as pltpu

F32 = jnp.float32
_BF = jnp.bfloat16
EPS = 1e-6
CHUNK = 64
ROPE_THETA = 10000.0
ADAM_LR, ADAM_B1, ADAM_B2, ADAM_EPS, ADAM_WD, ADAM_STEP = 0.001, 0.9, 0.999, 1e-08, 0.01, 10
NDEV = 8
AXES = ("x", "y", "c")
MESH = pl.DeviceIdType.MESH
LANES = 128
NEG = -1e30
LOG2E = math.log2(math.e)
V7X_VMEM_LIMIT = 56 * 1024 * 1024
PACK_C = 1024
ATT_BLOCK = 512
ATT_UNROLL = 4
ATT_QBLOCKS = 2
ADAM_BLOCK_ELEMS = 256 * 1024

WEIGHTS = ['norm_g', 'w_in', 'conv_w', 'w_conv_out', 'mla_q_norm_g', 'w_uq', 'mla_kv_norm_g', 'w_ukv',
           'mla_qn_nope_g', 'mla_qn_rope_g', 'mla_kn_nope_g', 'mla_kn_rope_g', 'w_mla_out', 'mem_norm_g',
           'w_mem_kv', 'mem_qn_g', 'mem_kn_g', 'w_mem_out', 'w_o']
BIG = ['w_in', 'w_conv_out', 'w_uq', 'w_ukv', 'w_mla_out', 'w_mem_kv', 'w_mem_out', 'w_o']
COL_SHARDED = ('w_in', 'w_conv_out', 'w_uq', 'w_ukv', 'w_mem_out')
SMALL = ['norm_g', 'mla_q_norm_g', 'mla_kv_norm_g', 'mla_qn_nope_g', 'mla_qn_rope_g', 'mla_kn_nope_g',
         'mla_kn_rope_g', 'mem_norm_g', 'mem_qn_g', 'mem_kn_g']


def _tile(dim, target, align):
    if dim <= target:
        return dim
    t = target - target % align
    while t > 0:
        if dim % t == 0:
            return t
        t -= align
    raise ValueError(f"no tile for {dim} {target} {align}")


def _cparams(sem):
    return pltpu.CompilerParams(dimension_semantics=sem, vmem_limit_bytes=V7X_VMEM_LIMIT)


def _sig(x):
    return 1.0 / (1.0 + jnp.exp(-x))


def _rms(x, n):
    r = lax.rsqrt(jnp.sum(x * x, axis=-1, keepdims=True) * (1.0 / n) + EPS)
    return x * r, r


def _rms_bwd(xhat, r, g, dy, n):
    dxh = dy * g
    dx = r * (dxh - xhat * (jnp.sum(dxh * xhat, axis=-1, keepdims=True) * (1.0 / n)))
    return dx, dy * xhat


def _rowsum128(x):
    return jnp.dot(x.astype(_BF), jnp.ones((LANES, LANES), _BF), preferred_element_type=F32)


def _rms128(x, n):
    r = lax.rsqrt(_rowsum128(x * x) * (1.0 / n) + EPS)
    return x * r, r


def _rms128_bwd(xhat, r, g, dy, n):
    dxh = dy * g
    dx = r * (dxh - xhat * (_rowsum128(dxh * xhat) * (1.0 / n)))
    return dx, dy * xhat


def _rope(x, cosp, sina, sinb):
    return x * cosp + pltpu.roll(x, 96, 1) * sina + pltpu.roll(x, 32, 1) * sinb


def _rope_t(d, cosp, sina, sinb):
    return d * cosp + pltpu.roll(d * sina, 32, 1) + pltpu.roll(d * sinb, 96, 1)


def _dot_nt(a, b):
    return lax.dot_general(a, b, (((1,), (1,)), ((), ())), preferred_element_type=F32)


def _dot_tn(a, b):
    return lax.dot_general(a, b, (((0,), (0,)), ((), ())), preferred_element_type=F32)


def _dot(a, b):
    return jnp.dot(a, b, preferred_element_type=F32)


def _all_gather(shards, name):
    na = len(shards)
    nc = 9
    halves = [s.shape[0] // 32 * 16 if s.shape[0] >= 256 else None for s in shards]

    def body(*refs):
        x_refs, out_refs = refs[:na], refs[na:2 * na]
        send_sems, recv_sems, local_sems = refs[2 * na:]
        x, y, c = lax.axis_index("x"), lax.axis_index("y"), lax.axis_index("c")
        me, sib = (x, y, c), (x, y, 1 - c)
        px, py, pd = (1 - x, y, c), (x, 1 - y, c), (1 - x, 1 - y, c)

        def other_core(p):
            return (p[0], p[1], 1 - p[2])

        def rows(a, blk, part=None):
            r = out_refs[a].at[4 * blk[0] + 2 * blk[1] + blk[2]]
            if part is None or halves[a] is None:
                return r
            rest = shards[a].shape[0] - halves[a]
            return r.at[pl.ds(0, halves[a])] if part == 0 else r.at[pl.ds(halves[a], rest)]

        def copy(a, k, blk, to, part=None, src=None):
            dst = rows(a, blk, part)
            return pltpu.make_async_remote_copy(
                src_ref=dst if src is None else src, dst_ref=dst,
                send_sem=send_sems.at[nc * a + k], recv_sem=recv_sems.at[nc * a + k],
                device_id=to, device_id_type=MESH)

        mine = [pltpu.make_async_copy(x_refs[a], rows(a, me), local_sems.at[a]) for a in range(na)]
        for cp in mine:
            cp.start()
        started = []
        for a in range(na):
            started += [copy(a, 0, me, px, src=x_refs[a]), copy(a, 1, me, py, src=x_refs[a]),
                        copy(a, 2, me, sib, src=x_refs[a])]
        for cp in started:
            cp.start()

        def forward(cp):
            cp.start()
            started.append(cp)

        for a in range(na):
            copy(a, 0, px, me).wait_recv()
            forward(copy(a, 3, px, py, part=0))
            forward(copy(a, 4, px, sib))
        for a in range(na):
            copy(a, 1, py, me).wait_recv()
            if halves[a] is not None:
                forward(copy(a, 5, py, px, part=1))
            forward(copy(a, 6, py, sib))
        for a in range(na):
            copy(a, 3, pd, me, part=0).wait_recv()
            forward(copy(a, 7, pd, sib, part=0))
            if halves[a] is not None:
                copy(a, 5, pd, me, part=1).wait_recv()
                forward(copy(a, 8, pd, sib, part=1))
        for a in range(na):
            copy(a, 2, sib, me).wait_recv()
            copy(a, 4, other_core(px), me).wait_recv()
            copy(a, 6, other_core(py), me).wait_recv()
            copy(a, 7, other_core(pd), me, part=0).wait_recv()
            if halves[a] is not None:
                copy(a, 8, other_core(pd), me, part=1).wait_recv()
        for cp in started:
            cp.wait_send()
        for cp in mine:
            cp.wait()

    any_spec = pl.BlockSpec(memory_space=pl.ANY)
    return pl.pallas_call(
        body, name=name,
        out_shape=[jax.ShapeDtypeStruct((NDEV,) + s.shape, s.dtype) for s in shards],
        in_specs=[any_spec] * na, out_specs=[any_spec] * na,
        scratch_shapes=[pltpu.SemaphoreType.DMA((nc * na,)), pltpu.SemaphoreType.DMA((nc * na,)),
                        pltpu.SemaphoreType.DMA((na,))],
    )(*shards)


_HBM = pl.BlockSpec(memory_space=pltpu.HBM)
_SEM = pl.BlockSpec(memory_space=pltpu.SEMAPHORE)
_EFFECT = pltpu.SideEffectType.DATAFLOW_SIDE_EFFECTING


def _split_copy(a, k, src_refs, land_refs, send_sems, recv_sems, gather, receive_side):
    x, y, c = lax.axis_index("x"), lax.axis_index("y"), lax.axis_index("c")
    me = 4 * x + 2 * y + c
    tx, ty, tc = x ^ ((k + 1) >> 2 & 1), y ^ ((k + 1) >> 1 & 1), c ^ ((k + 1) & 1)
    peer = 4 * tx + 2 * ty + tc
    return pltpu.make_async_remote_copy(
        src_ref=src_refs[a] if gather else src_refs[a].at[peer],
        dst_ref=land_refs[a].at[peer if receive_side else me],
        send_sem=send_sems.at[7 * a + k], recv_sem=recv_sems.at[7 * a + k],
        device_id=(tx, ty, tc), device_id_type=MESH)


def _own_copy(a, na, src_refs, land_refs, send_sems, gather):
    me = 4 * lax.axis_index("x") + 2 * lax.axis_index("y") + lax.axis_index("c")
    return pltpu.make_async_copy(src_refs[a] if gather else src_refs[a].at[me], land_refs[a].at[me],
                                 send_sems.at[7 * na + a])


def _split_start(srcs, lands, gather, name, after=None):
    na = len(srcs)
    extra = [] if after is None else [after]

    def body(*refs):
        src_refs, land_refs = refs[:na], refs[na:2 * na]
        send_sems, recv_sems = refs[2 * na + len(extra)], refs[2 * na + len(extra) + 1]
        token = refs[-1]
        for k in range(7):
            for a in range(na):
                _split_copy(a, k, src_refs, land_refs, send_sems, recv_sems, gather, False).start()
        for a in range(na):
            _own_copy(a, na, src_refs, land_refs, send_sems, gather).start()
        token[...] = jnp.zeros_like(token)

    hbm = [pltpu.HBM(b.shape, b.dtype) for b in list(srcs) + list(lands)]
    outs = pl.pallas_call(
        body, name=name,
        out_shape=(pltpu.SemaphoreType.DMA((8 * na,)), pltpu.SemaphoreType.DMA((7 * na,)), *hbm,
                   jax.ShapeDtypeStruct((8, LANES), F32)),
        in_specs=[_HBM] * (2 * na) + [pl.BlockSpec(memory_space=pl.ANY)] * len(extra),
        out_specs=(_SEM, _SEM, *[_HBM] * (2 * na), pl.BlockSpec(memory_space=pltpu.VMEM)),
        input_output_aliases={j: 2 + j for j in range(2 * na)},
        compiler_params=pltpu.CompilerParams(has_side_effects=_EFFECT),
    )(*[pltpu.with_memory_space_constraint(b, pltpu.HBM) for b in srcs],
      *[pltpu.with_memory_space_constraint(l, pltpu.HBM) for l in lands], *extra)
    return outs[0], outs[1], outs[2:2 + na], outs[2 + na:2 + 2 * na], outs[-1]


def _split_wait(started, gather, after, name):
    send_sems, recv_sems, srcs, lands, _ = started
    na = len(srcs)

    def body(*refs):
        src_refs, land_refs = refs[:na], refs[na:2 * na]
        send_s, recv_s = refs[2 * na], refs[2 * na + 1]
        for k in range(7):
            for a in range(na):
                cp = _split_copy(a, k, src_refs, land_refs, send_s, recv_s, gather, True)
                cp.wait_send()
                cp.wait_recv()
        for a in range(na):
            _own_copy(a, na, src_refs, land_refs, send_s, gather).wait()

    hbm = [pltpu.HBM(b.shape, b.dtype) for b in list(srcs) + list(lands)]
    outs = pl.pallas_call(
        body, name=name, out_shape=tuple(hbm),
        in_specs=[_HBM] * (2 * na) + [_SEM, _SEM, pl.BlockSpec(memory_space=pl.ANY)],
        out_specs=tuple([_HBM] * (2 * na)),
        input_output_aliases={j: j for j in range(2 * na)},
        compiler_params=pltpu.CompilerParams(has_side_effects=_EFFECT),
    )(*srcs, *lands, send_sems, recv_sems, after)
    return outs[na:]


def _seg_rows(size):
    rows = -(-size // PACK_C)
    return -(-rows // 16) * 16


def _pack(arrs, lead):
    parts = []
    for a in arrs:
        lshape = a.shape[:lead]
        f = a.reshape(lshape + (-1,)).astype(F32)
        rows = _seg_rows(f.shape[-1])
        f = jnp.pad(f, [(0, 0)] * lead + [(0, rows * PACK_C - f.shape[-1])])
        parts.append(f.reshape(lshape + (rows, PACK_C)))
    return jnp.concatenate(parts, axis=lead)


def _unpack(buf, shapes):
    lshape = buf.shape[:-2]
    out, r = [], 0
    for shp in shapes:
        size = math.prod(shp)
        rows = _seg_rows(size)
        seg = buf[..., r:r + rows, :].reshape(lshape + (rows * PACK_C,))[..., :size]
        out.append(seg.reshape(lshape + tuple(shp)))
        r += rows
    return out


def _mm(a, b, *, name, out_dtype, ta=False, tb=False, bm=1024, bn=1024, bk=2048, after=None, plus=None):
    if ta:
        kdim, m = a.shape
    else:
        m, kdim = a.shape
    n, k2 = b.shape if tb else b.shape[::-1]
    assert kdim == k2 and not (ta and tb), (a.shape, b.shape)
    bm = _tile(m, bm, LANES if ta else 16)
    bn = _tile(n, bn, LANES)
    bk = _tile(kdim, bk, LANES)
    nk = kdim // bk
    n_after = 0 if after is None else 1
    n_plus = 0 if plus is None else 2

    def kern(a_ref, b_ref, *rest):
        plus_refs = rest[n_after:n_after + n_plus]
        o_ref, scratch = rest[n_after + n_plus], rest[n_after + n_plus + 1:]
        part = (_dot_tn if ta else _dot_nt if tb else _dot)(a_ref[...], b_ref[...])

        def first(p):
            return p + _dot(plus_refs[0][...], plus_refs[1][...]) if plus is not None else p

        if nk == 1:
            o_ref[...] = first(part).astype(o_ref.dtype)
        else:
            acc = scratch[0] if scratch else o_ref
            k = pl.program_id(2)

            @pl.when(k == 0)
            def _():
                acc[...] = first(jnp.zeros(acc.shape, F32))

            acc[...] += part
            if scratch:
                @pl.when(k == nk - 1)
                def _():
                    o_ref[...] = acc[...].astype(o_ref.dtype)

    a_spec = pl.BlockSpec((bk, bm), lambda i, j, k: (k, i)) if ta else pl.BlockSpec((bm, bk), lambda i, j, k: (i, k))
    b_spec = pl.BlockSpec((bn, bk), lambda i, j, k: (j, k)) if tb else pl.BlockSpec((bk, bn), lambda i, j, k: (k, j))
    extra_specs, extra_args = [], []
    if after is not None:
        extra_specs.append(pl.BlockSpec(after.shape, lambda i, j, k: (0, 0)))
        extra_args.append(after)
    if plus is not None:
        kk = plus[0].shape[1]
        extra_specs += [pl.BlockSpec((bm, kk), lambda i, j, k: (i, 0)), pl.BlockSpec((kk, bn), lambda i, j, k: (0, j))]
        extra_args += list(plus)
    return pl.pallas_call(
        kern, name=name, grid=(m // bm, n // bn, nk),
        in_specs=[a_spec, b_spec] + extra_specs,
        out_specs=pl.BlockSpec((bm, bn), lambda i, j, k: (i, j)),
        out_shape=jax.ShapeDtypeStruct((m, n), out_dtype),
        scratch_shapes=[pltpu.VMEM((bm, bn), F32)] if nk > 1 and out_dtype != F32 else [],
        compiler_params=_cparams(("parallel", "parallel", "arbitrary")),
    )(a, b, *extra_args)


def _adam(parts, w_a, m_a, v_a, name):
    rows, cols = w_a.shape
    rb = _tile(rows, max(8, ADAM_BLOCK_ELEMS // cols // 8 * 8), 8)
    bc1 = 1.0 - ADAM_B1 ** ADAM_STEP
    bc2 = 1.0 - ADAM_B2 ** ADAM_STEP

    def adam_kern(p_ref, w_ref, m_ref, v_ref, g_ref, d_ref, nm_ref, nv_ref):
        g = p_ref[0].astype(F32)
        for j in range(1, NDEV):
            g = g + p_ref[j].astype(F32)
        m_new = ADAM_B1 * m_ref[...] + (1.0 - ADAM_B1) * g
        v_new = ADAM_B2 * v_ref[...] + (1.0 - ADAM_B2) * (g * g)
        g_ref[...] = g
        nm_ref[...] = m_new
        nv_ref[...] = v_new
        d_ref[...] = -ADAM_LR * ((m_new / bc1) / (jnp.sqrt(v_new / bc2) + ADAM_EPS) + ADAM_WD * w_ref[...])

    blk = pl.BlockSpec((rb, cols), lambda i: (i, 0))
    return pl.pallas_call(
        adam_kern, name=name, grid=(rows // rb,),
        in_specs=[pl.BlockSpec((NDEV, rb, cols), lambda i: (0, i, 0)), blk, blk, blk],
        out_specs=[blk] * 4, out_shape=[jax.ShapeDtypeStruct((rows, cols), F32)] * 4,
        compiler_params=_cparams(("parallel",)),
    )(parts, w_a, m_a, v_a)


class _Cfg:
    pass


def _config(x, conv_w, w_uq, w_ukv, mla_qn_nope_g, mla_qn_rope_g, mem, mem_qn_g, w_mem_out, w_mla_out):
    c = _Cfg()
    c.N, c.D = x.shape[1], x.shape[2]
    c.CW = conv_w.shape[2] * NDEV
    c.QL, c.KVL = w_uq.shape[1], w_ukv.shape[1]
    c.NOPE, c.ROPE = mla_qn_nope_g.shape[1], mla_qn_rope_g.shape[1]
    c.H = w_uq.shape[2] * NDEV // (c.NOPE + c.ROPE)
    c.V = w_ukv.shape[2] * NDEV // c.H - c.NOPE
    assert c.NOPE == LANES and c.V == LANES and c.ROPE == LANES // 2
    c.HW = 2 * LANES
    c.HV = c.H * c.V
    assert w_mla_out.shape[1] * NDEV == c.HV
    c.M = mem.shape[1]
    c.MHD = mem_qn_g.shape[1]
    c.MW = w_mem_out.shape[1]
    c.MH = c.MW // c.MHD
    c.o_conv = 0
    c.o_mz = 4 * c.CW
    c.o_g = c.o_mz + c.HV
    c.o_mem = c.o_g + 3 * c.D
    c.o_lora = c.o_mem + 2 * c.MW
    c.P = c.o_lora + c.QL + c.KVL
    assert c.o_mz % c.HV == 0 and c.o_g % (3 * c.D) == 0 and c.o_mem % (2 * c.MW) == 0
    assert c.o_lora % (c.QL + c.KVL) == 0 and c.QL % LANES == 0 and c.KVL % LANES == 0
    c.IN = 4 * c.CW + c.QL + c.KVL + c.ROPE + c.HV + 2 * c.MW + 3 * c.D
    c.R = _tile(c.N, 256, 16)
    c.RP = _tile(c.N, 512, 16)
    c.HG = _tile(c.H, 4, 1)
    c.B = _tile(c.N, ATT_BLOCK, CHUNK)
    c.scale = float((c.NOPE + c.ROPE) ** -0.5)
    c.mscale = float(c.MHD ** -0.5)
    return c


def _win_segments(c):
    ref_order = (('conv', 4 * c.CW), ('lora', c.QL + c.KVL), ('kr', c.ROPE), ('mz', c.HV), ('mem', 2 * c.MW), ('g', 3 * c.D))
    mine = {'conv': c.o_conv, 'mz': c.o_mz, 'g': c.o_g, 'mem': c.o_mem, 'lora': c.o_lora, 'kr': 0}
    segs, o = [], 0
    for nm, wd in ref_order:
        segs.append((nm, o, wd, mine[nm]))
        o += wd
    return segs


def _win_split(g_win_t, c):
    n8 = g_win_t.shape[1]

    def rows(a, wd):
        return [g_win_t[j][max(a, j * n8) - j * n8:min(a + wd, (j + 1) * n8) - j * n8]
                for j in range(a // n8, (a + wd - 1) // n8 + 1)]

    segs = {nm: (a, wd) for nm, a, wd, _ in _win_segments(c)}
    main = [p for nm in ('conv', 'mz', 'g', 'mem', 'lora') for p in rows(*segs[nm])]
    kr = jnp.concatenate(rows(*segs['kr']) + [jnp.zeros((LANES - c.ROPE, g_win_t.shape[2]), g_win_t.dtype)], axis=0)
    return jnp.concatenate(main, axis=0), kr


def _win_blocks(g, g_kr, c):
    n8 = c.IN // NDEV
    blocks = []
    for j in range(NDEV):
        lo, hi = j * n8, (j + 1) * n8
        parts = []
        for nm, a, wd, mine in _win_segments(c):
            s, e = max(a, lo), min(a + wd, hi)
            if s < e:
                parts.append((g_kr if nm == 'kr' else g)[:, mine + s - a:mine + e - a])
        blocks.append(jnp.concatenate(parts, axis=1))
    return jnp.stack(blocks, axis=0)


def kernel(x, positions, mem, norm_g, w_in, conv_w, w_conv_out, mla_q_norm_g, w_uq, mla_kv_norm_g, w_ukv, mla_qn_nope_g, mla_qn_rope_g, mla_kn_nope_g, mla_kn_rope_g, w_mla_out, mem_norm_g, w_mem_kv, mem_qn_g, mem_kn_g, w_mem_out, w_o, loss_target, m_norm_g, m_w_in, m_conv_w, m_w_conv_out, m_mla_q_norm_g, m_w_uq, m_mla_kv_norm_g, m_w_ukv, m_mla_qn_nope_g, m_mla_qn_rope_g, m_mla_kn_nope_g, m_mla_kn_rope_g, m_w_mla_out, m_mem_norm_g, m_w_mem_kv, m_mem_qn_g, m_mem_kn_g, m_w_mem_out, m_w_o, v_norm_g, v_w_in, v_conv_w, v_w_conv_out, v_mla_q_norm_g, v_w_uq, v_mla_kv_norm_g, v_w_ukv, v_mla_qn_nope_g, v_mla_qn_rope_g, v_mla_kn_nope_g, v_mla_kn_rope_g, v_w_mla_out, v_mem_norm_g, v_w_mem_kv, v_mem_qn_g, v_mem_kn_g, v_w_mem_out, v_w_o):
    args = dict(locals())
    W = {n: args[n] for n in WEIGHTS}
    Mo = {n: args['m_' + n] for n in WEIGHTS}
    Vo = {n: args['v_' + n] for n in WEIGHTS}
    c = _config(x, conv_w, w_uq, w_ukv, mla_qn_nope_g, mla_qn_rope_g, mem, mem_qn_g, w_mem_out, w_mla_out)
    N, D, R, B, H = c.N, c.D, c.R, c.B, c.H
    assert x.shape[0] == 1
    xs = x[0]
    tgt = loss_target[0]
    memx = mem[0]
    me = 4 * lax.axis_index("x") + 2 * lax.axis_index("y") + lax.axis_index("c")
    nr = N // R

    g_win, g_taps = _all_gather([W['w_in'][0].astype(_BF).T, conv_w[0]], "ag_w_in")
    rest = [n for n in BIG if n != 'w_in']
    shards_r = [W[n][0].astype(_BF).T if n in COL_SHARDED else W[n][0].astype(_BF) for n in rest]
    lands_r = [lax.empty((NDEV,) + s.shape, s.dtype) for s in shards_r]
    ag_rest = _split_start(shards_r, lands_r, True, "ag_rest_start", after=g_win)
    win_pT, w_krT = _win_split(g_win, c)
    convw = jnp.transpose(g_taps, (1, 0, 2)).reshape(3, c.CW)
    convw8 = jnp.pad(convw, ((0, 5), (0, 0)))

    def rowb(width, cidx, rows=R):
        return pl.BlockSpec((rows, width), lambda i, _c=cidx: (i, _c))

    R2 = c.RP
    nr2 = N // R2
    RM = _tile(N, 1024, 16)

    def rowb2(width, cidx):
        return rowb(width, cidx, R2)

    def fullb(shape):
        nd = len(shape)
        return pl.BlockSpec(shape, lambda *_: (0,) * nd)

    def pad_lanes(g, w=LANES):
        return jnp.pad(g, ((0, 0), (0, w - g.shape[1])))

    def tabs_of(rows):
        return pl.BlockSpec((3, rows, LANES), lambda i, *_: (0, i, 0))

    half = c.ROPE // 2
    inv_freq = jnp.power(ROPE_THETA, -jnp.arange(half, dtype=F32) / half)
    invf = jnp.concatenate([inv_freq, inv_freq, jnp.zeros((LANES - c.ROPE,), F32)])[None, :]
    pos_col = positions[0].astype(F32).reshape(N, 1)

    def rope_tab_kern(pos_ref, invf_ref, o_ref):
        ang = pos_ref[...] * invf_ref[...]
        co, si = jnp.cos(ang), jnp.sin(ang)
        lane = lax.broadcasted_iota(jnp.int32, ang.shape, 1)
        o_ref[0] = jnp.where(lane < c.ROPE, co, 0.0)
        o_ref[1] = jnp.where(lane < half, -si, 0.0)
        o_ref[2] = jnp.where(jnp.logical_and(lane >= half, lane < c.ROPE), si, 0.0)

    tabs = pl.pallas_call(
        rope_tab_kern, name="rope_tab", grid=(nr,),
        in_specs=[pl.BlockSpec((R, 1), lambda i: (i, 0)), fullb((1, LANES))],
        out_specs=tabs_of(R),
        out_shape=jax.ShapeDtypeStruct((3, N, LANES), F32),
        compiler_params=_cparams(("parallel",)),
    )(pos_col, invf)

    def make_rms_kern():
        def rms_fwd_kern(x_ref, g_ref, o_ref):
            xh, _ = _rms(x_ref[...].astype(F32), x_ref.shape[-1])
            o_ref[...] = (xh * g_ref[...]).astype(o_ref.dtype)
        return rms_fwd_kern

    h = pl.pallas_call(
        make_rms_kern(), name="rms_x", grid=(nr2,),
        in_specs=[rowb2(D, 0), fullb((1, D))], out_specs=rowb2(D, 0),
        out_shape=jax.ShapeDtypeStruct((N, D), _BF), compiler_params=_cparams(("parallel",)),
    )(xs, norm_g)

    proj = _mm(h, win_pT, tb=True, name="mm_proj", out_dtype=_BF, after=ag_rest[4])
    kr_raw = _mm(h, w_krT, tb=True, name="mm_kr", out_dtype=_BF)

    Wf = {n: g.reshape(-1, g.shape[2]) for n, g in zip(rest, _split_wait(ag_rest, True, proj, "ag_rest_wait"))}
    wuqT = Wf['w_uq'].reshape(H, c.NOPE + c.ROPE, c.QL)
    wuq_pT = jnp.pad(wuqT, ((0, 0), (0, c.HW - c.NOPE - c.ROPE), (0, 0))).reshape(H * c.HW, c.QL)
    wukvT = Wf['w_ukv']
    wcoT, wmo, wmkv, wmemoT, wo = Wf['w_conv_out'], Wf['w_mla_out'], Wf['w_mem_kv'], Wf['w_mem_out'], Wf['w_o']

    CW = c.CW
    conv_blk = c.o_conv // (4 * CW)
    HALO = 16
    rh = R // HALO

    def conv_parts(blk):
        blk = blk.astype(F32)
        return blk[:, 0:CW], blk[:, CW:2 * CW], blk[:, 2 * CW:3 * CW], blk[:, 3 * CW:4 * CW]

    def shifted(cu, prev, i):
        prev = jnp.where(i > 0, prev, 0.0)
        rid = lax.broadcasted_iota(jnp.int32, cu.shape, 0)
        last, last2 = prev[HALO - 1:HALO, :], prev[HALO - 2:HALO - 1, :]
        sh1 = jnp.where(rid == 0, last, pltpu.roll(cu, 1, 0))
        sh2 = jnp.where(rid == 0, last2, jnp.where(rid == 1, last, pltpu.roll(cu, 2, 0)))
        return sh1, sh2

    def conv_fwd_kern(p_ref, prev_ref, w_ref, o_ref):
        i = pl.program_id(0)
        cg, bg, u, z = conv_parts(p_ref[...])
        pc, _, pu, _ = conv_parts(prev_ref[...])
        cu = cg * u
        sh1, sh2 = shifted(cu, pc * pu, i)
        w = w_ref[...]
        conv = w[0:1, :] * sh2 + w[1:2, :] * sh1 + w[2:3, :] * cu
        o_ref[...] = (bg * conv * (z * _sig(z))).astype(o_ref.dtype)

    prev_spec = pl.BlockSpec((HALO, 4 * CW), lambda i: (jnp.maximum(i * rh - 1, 0), conv_blk))
    a_conv = pl.pallas_call(
        conv_fwd_kern, name="conv_fwd", grid=(nr,),
        in_specs=[rowb(4 * CW, conv_blk), prev_spec, fullb((8, CW))],
        out_specs=rowb(CW, 0), out_shape=jax.ShapeDtypeStruct((N, CW), _BF),
        compiler_params=_cparams(("parallel",)),
    )(proj, proj, convw8)
    o_conv = _mm(a_conv, wcoT, tb=True, name="mm_oconv", out_dtype=_BF)

    QL, KVL, HW = c.QL, c.KVL, c.HW
    lora_blk = c.o_lora // (QL + KVL)

    def lora_fwd_kern(p_ref, gq_ref, gkv_ref, q_ref, kv_ref):
        blk = p_ref[...].astype(F32)
        qh, _ = _rms(blk[:, :QL], QL)
        kh, _ = _rms(blk[:, QL:], KVL)
        q_ref[...] = (qh * gq_ref[...]).astype(q_ref.dtype)
        kv_ref[...] = (kh * gkv_ref[...]).astype(kv_ref.dtype)

    cqn, ckvn = pl.pallas_call(
        lora_fwd_kern, name="lora_fwd", grid=(nr,),
        in_specs=[rowb(QL + KVL, lora_blk), fullb((1, QL)), fullb((1, KVL))],
        out_specs=[rowb(QL, 0), rowb(KVL, 0)],
        out_shape=[jax.ShapeDtypeStruct((N, QL), _BF), jax.ShapeDtypeStruct((N, KVL), _BF)],
        compiler_params=_cparams(("parallel",)),
    )(proj, mla_q_norm_g, mla_kv_norm_g)
    q_p = _mm(cqn, wuq_pT, tb=True, name="mm_q", out_dtype=_BF, bn=2048)
    kv = _mm(ckvn, wukvT, tb=True, name="mm_kv", out_dtype=_BF, bn=2048)

    g_qn, g_qr = mla_qn_nope_g, pad_lanes(mla_qn_rope_g)
    g_kn, g_kr = mla_kn_nope_g, pad_lanes(mla_kn_rope_g)

    def krope_fwd_kern(p_ref, t_ref, g_ref, o_ref):
        xh, _ = _rms128(p_ref[...].astype(F32), c.ROPE)
        o_ref[...] = _rope(xh * g_ref[...], t_ref[0], t_ref[1], t_ref[2]).astype(o_ref.dtype)

    k_rope = pl.pallas_call(
        krope_fwd_kern, name="krope_fwd", grid=(nr,),
        in_specs=[rowb(LANES, 0), tabs_of(R), fullb((1, LANES))],
        out_specs=rowb(LANES, 0), out_shape=jax.ShapeDtypeStruct((N, LANES), _BF),
        compiler_params=_cparams(("parallel",)),
    )(kr_raw, tabs, g_kr)

    RP, HG = c.RP, c.HG
    nrp, nhg = N // RP, H // HG
    heads_in = pl.BlockSpec((RP, HG * HW), lambda i, hg: (i, hg))
    heads_out = pl.BlockSpec((HG, RP, HW), lambda i, hg: (hg, i, 0))

    def q_prep_kern(q_ref, t_ref, gn_ref, gr_ref, o_ref):
        for g in range(HG):
            blk = q_ref[:, g * HW:(g + 1) * HW].astype(F32)
            nh, _ = _rms128(blk[:, :LANES], c.NOPE)
            rhat, _ = _rms128(blk[:, LANES:], c.ROPE)
            rot = _rope(rhat * gr_ref[...], t_ref[0], t_ref[1], t_ref[2])
            o_ref[g] = (jnp.concatenate([nh * gn_ref[...], rot], axis=1) * (c.scale * LOG2E)).astype(o_ref.dtype)

    q_cat = pl.pallas_call(
        q_prep_kern, name="q_prep", grid=(nrp, nhg),
        in_specs=[heads_in, tabs_of(RP), fullb((1, LANES)), fullb((1, LANES))],
        out_specs=heads_out, out_shape=jax.ShapeDtypeStruct((H, N, HW), _BF),
        compiler_params=_cparams(("parallel", "parallel")),
    )(q_p, tabs, g_qn, g_qr)

    def k_prep_kern(kv_ref, kr_ref, gn_ref, o_ref):
        for g in range(HG):
            kn, _ = _rms128(kv_ref[:, g * HW:g * HW + LANES].astype(F32), c.NOPE)
            o_ref[g] = jnp.concatenate([(kn * gn_ref[...]).astype(o_ref.dtype), kr_ref[...]], axis=1)

    k_cat = pl.pallas_call(
        k_prep_kern, name="k_prep", grid=(nrp, nhg),
        in_specs=[heads_in, pl.BlockSpec((RP, LANES), lambda i, hg: (i, 0)), fullb((1, LANES))],
        out_specs=heads_out, out_shape=jax.ShapeDtypeStruct((H, N, HW), _BF),
        compiler_params=_cparams(("parallel", "parallel")),
    )(kv, k_rope, g_kn)

    QB = ATT_QBLOCKS if N % (ATT_QBLOCKS * B) == 0 else 1
    BQ = QB * B
    nq = N // BQ
    assert CHUNK & (CHUNK - 1) == 0 and B % CHUNK == 0 and ATT_UNROLL % QB == 0

    def live_rows(diag):
        return slice(0 if diag is None else diag * B, BQ)

    def diag_mask(s):
        row = lax.broadcasted_iota(jnp.int32, s.shape, 0)
        col = lax.broadcasted_iota(jnp.int32, s.shape, 1)
        shift = CHUNK.bit_length() - 1
        allowed = jnp.right_shift(col, shift) <= jnp.right_shift(row, shift)
        return jnp.where(allowed, s, NEG)

    k_head = pl.BlockSpec((1, N, HW), lambda hh, i: (hh, 0, 0))
    v_head = pl.BlockSpec((N, LANES), lambda hh, i: (0, 2 * hh + 1))
    q_blk = pl.BlockSpec((1, BQ, HW), lambda hh, i: (hh, i, 0))
    o_blk = pl.BlockSpec((BQ, LANES), lambda hh, i: (i, hh))
    lse_blk = pl.BlockSpec((1, BQ, LANES), lambda hh, i: (hh, i, 0))

    def key_block_plan(i, run):
        def unrolled(u, carry):
            run(ATT_UNROLL * u, ATT_UNROLL, 0)
            return carry

        n_full = QB * i
        lax.fori_loop(0, n_full // ATT_UNROLL, unrolled, 0)
        for rem in range(0, ATT_UNROLL, QB):
            @pl.when(n_full % ATT_UNROLL == rem)
            def _(rem=rem):
                run(n_full - rem, rem + QB, QB)

    def attn_fwd_kern(q_ref, k_ref, v_ref, o_ref, lse_ref, m_sc, acc_sc, s_sc):
        i = pl.program_id(1)
        m_sc[...] = jnp.full(m_sc.shape, NEG, F32)
        acc_sc[...] = jnp.zeros(acc_sc.shape, F32)

        def rows_of(t):
            return pl.ds(pl.multiple_of(t * B, B), B)

        def scores(t, slot, diag):
            rs = live_rows(diag)
            s_sc[slot, rs, :] = _dot_nt(q_ref[0, rs, :], k_ref[0, rows_of(t), :])

        def softmax_pv(t, slot, diag):
            rs = live_rows(diag)
            s = s_sc[slot, rs, :]
            if diag is not None:
                s = diag_mask(s)
            mt = s[:, 0:LANES]
            for cb in range(1, B // LANES):
                mt = jnp.maximum(mt, s[:, cb * LANES:(cb + 1) * LANES])
            m_prev = m_sc[rs, :]
            m_new = jnp.maximum(m_prev, jnp.max(mt, axis=1, keepdims=True))
            alpha = jnp.exp2(m_prev - m_new)
            p = jnp.concatenate([jnp.exp2(s[:, cb * LANES:(cb + 1) * LANES] - m_new).astype(_BF)
                                 for cb in range(B // LANES)], axis=1)
            v_ones = jnp.concatenate([v_ref[rows_of(t), :], jnp.ones((B, LANES), _BF)], axis=1)
            acc_sc[rs, :] = jnp.concatenate([alpha, alpha], axis=1) * acc_sc[rs, :] + _dot(p, v_ones)
            m_sc[rs, :] = m_new

        scores(0, 0, None)

        def run(first, count, n_diag):
            def diag_of(u):
                return u - (count - n_diag) if count - n_diag <= u < count else None

            for u in range(count):
                if u + 1 < count or n_diag == 0:
                    scores(first + u + 1, (u + 1) % 2, diag_of(u + 1))
                softmax_pv(first + u, u % 2, diag_of(u))

        key_block_plan(i, run)
        acc = acc_sc[...]
        o_ref[...] = (acc[:, :LANES] / acc[:, LANES:]).astype(o_ref.dtype)
        lse_ref[0] = m_sc[...] + jnp.log2(acc[:, LANES:])

    mla_y, lse = pl.pallas_call(
        attn_fwd_kern, name="attn_fwd", grid=(H, nq),
        in_specs=[q_blk, k_head, v_head], out_specs=[o_blk, lse_blk],
        out_shape=[jax.ShapeDtypeStruct((N, c.HV), _BF), jax.ShapeDtypeStruct((H, N, LANES), F32)],
        scratch_shapes=[pltpu.VMEM((BQ, LANES), F32), pltpu.VMEM((BQ, HW), F32), pltpu.VMEM((2, BQ, B), F32)],
        compiler_params=_cparams(("parallel", "arbitrary")),
    )(q_cat, k_cat, kv)

    HV = c.HV
    mz_blk = c.o_mz // HV

    def gate_fwd_kern(y_ref, z_ref, o_ref):
        z = z_ref[...].astype(F32)
        o_ref[...] = (y_ref[...].astype(F32) * (z * _sig(z))).astype(o_ref.dtype)

    a_mla = pl.pallas_call(
        gate_fwd_kern, name="gate_mla", grid=(nr2,),
        in_specs=[rowb2(HV, 0), rowb2(HV, mz_blk)], out_specs=rowb2(HV, 0),
        out_shape=jax.ShapeDtypeStruct((N, HV), _BF), compiler_params=_cparams(("parallel",)),
    )(mla_y, proj)
    o_mla = _mm(a_mla, wmo, name="mm_omla", out_dtype=_BF)

    M, MW, MH, MHD = c.M, c.MW, c.MH, c.MHD
    memn = pl.pallas_call(
        make_rms_kern(), name="rms_mem",
        grid=(1,), in_specs=[fullb((M, D)), fullb((1, D))], out_specs=fullb((M, D)),
        out_shape=jax.ShapeDtypeStruct((M, D), _BF), compiler_params=_cparams(("arbitrary",)),
    )(memx, mem_norm_g)
    kvm = _mm(memn, wmkv, name="mm_memkv", out_dtype=F32)

    def memk_fwd_kern(kv_ref, g_ref, k_ref, v_ref):
        for hh in range(MH):
            kh, _ = _rms(kv_ref[:, hh * MHD:(hh + 1) * MHD], MHD)
            k_ref[:, hh * MHD:(hh + 1) * MHD] = (kh * g_ref[...]).astype(k_ref.dtype)
        v_ref[...] = kv_ref[:, MW:].astype(v_ref.dtype)

    mem_k, mem_v = pl.pallas_call(
        memk_fwd_kern, name="memk_fwd", grid=(1,),
        in_specs=[fullb((M, 2 * MW)), fullb((1, MHD))], out_specs=[fullb((M, MW)), fullb((M, MW))],
        out_shape=[jax.ShapeDtypeStruct((M, MW), _BF)] * 2, compiler_params=_cparams(("arbitrary",)),
    )(kvm, mem_kn_g)

    mem_blk = c.o_mem // (2 * MW)

    def mem_head(qz_ref, k_ref, v_ref, g_ref, hh):
        sl = slice(hh * MHD, (hh + 1) * MHD)
        qh, r = _rms(qz_ref[:, sl].astype(F32), MHD)
        qn = (qh * g_ref[...]).astype(_BF)
        s = _dot_nt(qn, k_ref[:, sl]) * c.mscale
        e = jnp.exp(s - jnp.max(s, axis=1, keepdims=True))
        p = e / jnp.sum(e, axis=1, keepdims=True)
        y = _dot(p.astype(_BF), v_ref[:, sl])
        z = qz_ref[:, MW + hh * MHD:MW + (hh + 1) * MHD].astype(F32)
        return sl, qh, r, qn, p, y, z

    def mem_fwd_kern(qz_ref, k_ref, v_ref, g_ref, o_ref):
        for hh in range(MH):
            sl, _, _, _, _, y, z = mem_head(qz_ref, k_ref, v_ref, g_ref, hh)
            o_ref[:, sl] = (y * (z * _sig(z))).astype(o_ref.dtype)

    a_mem = pl.pallas_call(
        mem_fwd_kern, name="mem_fwd", grid=(N // RM,),
        in_specs=[rowb(2 * MW, mem_blk, RM), fullb((M, MW)), fullb((M, MW)), fullb((1, MHD))],
        out_specs=rowb(MW, 0, RM), out_shape=jax.ShapeDtypeStruct((N, MW), _BF),
        compiler_params=_cparams(("parallel",)),
    )(proj, mem_k, mem_v, mem_qn_g)
    o_mem = _mm(a_mem, wmemoT, tb=True, name="mm_omem", out_dtype=_BF)

    g_blk = c.o_g // (3 * D)

    def merge_fwd_kern(g_ref, oc_ref, om_ref, ome_ref, o_ref):
        g = g_ref[...].astype(F32)
        acc = _sig(g[:, :D]) * oc_ref[...].astype(F32)
        acc += _sig(g[:, D:2 * D]) * om_ref[...].astype(F32)
        acc += _sig(g[:, 2 * D:]) * ome_ref[...].astype(F32)
        o_ref[...] = acc.astype(o_ref.dtype)

    merged = pl.pallas_call(
        merge_fwd_kern, name="merge_fwd", grid=(nr,),
        in_specs=[rowb(3 * D, g_blk), rowb(D, 0), rowb(D, 0), rowb(D, 0)], out_specs=rowb(D, 0),
        out_shape=jax.ShapeDtypeStruct((N, D), _BF), compiler_params=_cparams(("parallel",)),
    )(proj, o_conv, o_mla, o_mem)
    obm, obn = _tile(N, 512, 16), _tile(D, 1024, LANES)

    def out_loss_kern(a_ref, w_ref, x_ref, t_ref, dy_ref, dyb_ref, l_ref):
        e = x_ref[...] + _dot(a_ref[...], w_ref[...]) - t_ref[...]
        dy = e * (1.0 / D)
        dy_ref[...] = dy
        dyb_ref[...] = dy.astype(dyb_ref.dtype)
        row = lax.broadcasted_iota(jnp.int32, l_ref.shape, 0)
        l_ref[...] = jnp.where(row == 0, jnp.sum(e * e, axis=0, keepdims=True), 0.0)

    oblk = pl.BlockSpec((obm, obn), lambda i, j: (i, j))
    dy, dyb, lpart = pl.pallas_call(
        out_loss_kern, name="mm_out_loss", grid=(N // obm, D // obn),
        in_specs=[pl.BlockSpec((obm, D), lambda i, j: (i, 0)), pl.BlockSpec((D, obn), lambda i, j: (0, j)), oblk, oblk],
        out_specs=[oblk, oblk, pl.BlockSpec((8, obn), lambda i, j: (i, j))],
        out_shape=[jax.ShapeDtypeStruct((N, D), F32), jax.ShapeDtypeStruct((N, D), _BF),
                   jax.ShapeDtypeStruct((8 * (N // obm), D), F32)],
        compiler_params=_cparams(("parallel", "parallel")),
    )(merged, wo, xs, tgt)
    loss = lax.psum(jnp.sum(lpart) * (0.5 / D), AXES)

    G = {}
    d_merged = _mm(dyb, wo, tb=True, name="mm_dmerged", out_dtype=_BF)
    G['w_o'] = _mm(merged, dyb, ta=True, name="mm_dwo", out_dtype=_BF, bk=4096)

    dproj0 = lax.empty((N, c.P), _BF)
    any_spec = pl.BlockSpec(memory_space=pl.ANY)

    def merge_bwd_kern(dp_any, g_ref, dm_ref, oc_ref, om_ref, ome_ref, dg_ref, doc_ref, dom_ref, dome_ref):
        g = g_ref[...].astype(F32)
        dm = dm_ref[...].astype(F32)
        for idx, (o_in, d_out) in enumerate(((oc_ref, doc_ref), (om_ref, dom_ref), (ome_ref, dome_ref))):
            sg = _sig(g[:, idx * D:(idx + 1) * D])
            d_out[...] = (sg * dm).astype(d_out.dtype)
            dg_ref[:, idx * D:(idx + 1) * D] = (dm * o_in[...].astype(F32) * sg * (1.0 - sg)).astype(dg_ref.dtype)

    dproj1, d_oconv, d_omla, d_omem = pl.pallas_call(
        merge_bwd_kern, name="merge_bwd", grid=(nr,),
        in_specs=[any_spec, rowb(3 * D, g_blk), rowb(D, 0), rowb(D, 0), rowb(D, 0), rowb(D, 0)],
        out_specs=[rowb(3 * D, g_blk), rowb(D, 0), rowb(D, 0), rowb(D, 0)],
        out_shape=[jax.ShapeDtypeStruct((N, c.P), _BF)] + [jax.ShapeDtypeStruct((N, D), _BF)] * 3,
        input_output_aliases={0: 0}, compiler_params=_cparams(("parallel",)),
    )(dproj0, proj, d_merged, o_conv, o_mla, o_mem)

    G['w_conv_out'] = _mm(a_conv, d_oconv, ta=True, name="mm_dwco", out_dtype=_BF, bk=4096)
    d_aconv = _mm(d_oconv, wcoT, name="mm_daconv", out_dtype=_BF)
    G['w_mla_out'] = _mm(a_mla, d_omla, ta=True, name="mm_dwmo", out_dtype=_BF, bk=4096)
    d_amla = _mm(d_omla, wmo, tb=True, name="mm_damla", out_dtype=_BF)
    G['w_mem_out'] = _mm(a_mem, d_omem, ta=True, name="mm_dwmemo", out_dtype=_BF, bk=4096)
    d_amem = _mm(d_omem, wmemoT, name="mm_damem", out_dtype=_BF)

    def conv_bwd_kern(dp_any, p_ref, prev_ref, next_ref, da_ref, dan_ref, w_ref, o_ref, dw_ref):
        i = pl.program_id(0)
        cg, bg, u, z = conv_parts(p_ref[...])
        pc, _, pu, _ = conv_parts(prev_ref[...])
        _, nbg, _, nz = conv_parts(next_ref[...])
        cu = cg * u
        sh1, sh2 = shifted(cu, pc * pu, i)
        w = w_ref[...]
        conv = w[0:1, :] * sh2 + w[1:2, :] * sh1 + w[2:3, :] * cu
        sg = _sig(z)
        sz = z * sg
        da = da_ref[...].astype(F32)
        dcy = da * sz
        d_z = da * (bg * conv) * (sg * (1.0 + z * (1.0 - sg)))
        d_b = dcy * conv
        dconv = dcy * bg
        dnext = dan_ref[...].astype(F32) * (nz * _sig(nz)) * nbg
        dnext = jnp.where(i < nr - 1, dnext, 0.0)
        rid = lax.broadcasted_iota(jnp.int32, cu.shape, 0)
        up1 = jnp.where(rid == R - 1, dnext[0:1, :], pltpu.roll(dconv, R - 1, 0))
        up2 = jnp.where(rid == R - 2, dnext[0:1, :], jnp.where(rid == R - 1, dnext[1:2, :], pltpu.roll(dconv, R - 2, 0)))
        dcu = w[2:3, :] * dconv + w[1:2, :] * up1 + w[0:1, :] * up2
        o_ref[:, 0:CW] = (dcu * u).astype(o_ref.dtype)
        o_ref[:, CW:2 * CW] = d_b.astype(o_ref.dtype)
        o_ref[:, 2 * CW:3 * CW] = (dcu * cg).astype(o_ref.dtype)
        o_ref[:, 3 * CW:4 * CW] = d_z.astype(o_ref.dtype)

        @pl.when(i == 0)
        def _():
            dw_ref[...] = jnp.zeros(dw_ref.shape, F32)

        dw_ref[0:1, :] += jnp.sum(dconv * sh2, axis=0, keepdims=True)
        dw_ref[1:2, :] += jnp.sum(dconv * sh1, axis=0, keepdims=True)
        dw_ref[2:3, :] += jnp.sum(dconv * cu, axis=0, keepdims=True)

    next_spec = pl.BlockSpec((HALO, 4 * CW), lambda i: (jnp.minimum((i + 1) * rh, N // HALO - 1), conv_blk))
    dan_spec = pl.BlockSpec((HALO, CW), lambda i: (jnp.minimum((i + 1) * rh, N // HALO - 1), 0))
    dproj2, g_convw = pl.pallas_call(
        conv_bwd_kern, name="conv_bwd", grid=(nr,),
        in_specs=[any_spec, rowb(4 * CW, conv_blk), prev_spec, next_spec, rowb(CW, 0), dan_spec, fullb((8, CW))],
        out_specs=[rowb(4 * CW, conv_blk), fullb((8, CW))],
        out_shape=[jax.ShapeDtypeStruct((N, c.P), _BF), jax.ShapeDtypeStruct((8, CW), F32)],
        input_output_aliases={0: 0}, compiler_params=_cparams(("arbitrary",)),
    )(dproj1, proj, proj, proj, d_aconv, d_aconv, convw8)

    def mem_bwd_kern(dp_any, qz_ref, da_ref, k_ref, v_ref, g_ref, o_ref, dk_ref, dv_ref, dg_ref):
        @pl.when(pl.program_id(0) == 0)
        def _():
            dk_ref[...] = jnp.zeros(dk_ref.shape, F32)
            dv_ref[...] = jnp.zeros(dv_ref.shape, F32)
            dg_ref[...] = jnp.zeros(dg_ref.shape, F32)

        for hh in range(MH):
            sl, qh, r, qn, p, y, z = mem_head(qz_ref, k_ref, v_ref, g_ref, hh)
            da = da_ref[:, sl].astype(F32)
            sg = _sig(z)
            dyh = da * (z * sg)
            o_ref[:, MW + hh * MHD:MW + (hh + 1) * MHD] = (da * y * (sg * (1.0 + z * (1.0 - sg)))).astype(o_ref.dtype)
            dyb_h = dyh.astype(_BF)
            dpm = _dot_nt(dyb_h, v_ref[:, sl])
            ds = (p * (dpm - jnp.sum(dpm * p, axis=1, keepdims=True)) * c.mscale).astype(_BF)
            dqn = _dot(ds, k_ref[:, sl])
            dk_ref[:, sl] += _dot_tn(ds, qn)
            dv_ref[:, sl] += _dot_tn(p.astype(_BF), dyb_h)
            dq, dgp = _rms_bwd(qh, r, g_ref[...], dqn, MHD)
            o_ref[:, sl] = dq.astype(o_ref.dtype)
            dg_ref[...] += jnp.sum(dgp, axis=0, keepdims=True)

    dproj3, d_memk, d_memv, g_mem_qn = pl.pallas_call(
        mem_bwd_kern, name="mem_bwd", grid=(N // RM,),
        in_specs=[any_spec, rowb(2 * MW, mem_blk, RM), rowb(MW, 0, RM), fullb((M, MW)), fullb((M, MW)), fullb((1, MHD))],
        out_specs=[rowb(2 * MW, mem_blk, RM), fullb((M, MW)), fullb((M, MW)), fullb((1, MHD))],
        out_shape=[jax.ShapeDtypeStruct((N, c.P), _BF), jax.ShapeDtypeStruct((M, MW), F32),
                   jax.ShapeDtypeStruct((M, MW), F32), jax.ShapeDtypeStruct((1, MHD), F32)],
        input_output_aliases={0: 0}, compiler_params=_cparams(("arbitrary",)),
    )(dproj2, proj, d_amem, mem_k, mem_v, mem_qn_g)

    def memk_bwd_kern(kv_ref, dk_ref, dv_ref, g_ref, o_ref, dg_ref):
        dg = jnp.zeros((1, MHD), F32)
        for hh in range(MH):
            sl = slice(hh * MHD, (hh + 1) * MHD)
            kh, r = _rms(kv_ref[:, sl], MHD)
            dkr, dgp = _rms_bwd(kh, r, g_ref[...], dk_ref[:, sl], MHD)
            o_ref[:, sl] = dkr.astype(o_ref.dtype)
            dg += jnp.sum(dgp, axis=0, keepdims=True)
        o_ref[:, MW:] = dv_ref[...].astype(o_ref.dtype)
        dg_ref[...] = dg

    d_kvm, g_mem_kn = pl.pallas_call(
        memk_bwd_kern, name="memk_bwd", grid=(1,),
        in_specs=[fullb((M, 2 * MW)), fullb((M, MW)), fullb((M, MW)), fullb((1, MHD))],
        out_specs=[fullb((M, 2 * MW)), fullb((1, MHD))],
        out_shape=[jax.ShapeDtypeStruct((M, 2 * MW), _BF), jax.ShapeDtypeStruct((1, MHD), F32)],
        compiler_params=_cparams(("arbitrary",)),
    )(kvm, d_memk, d_memv, mem_kn_g)
    G['w_mem_kv'] = _mm(memn, d_kvm, ta=True, name="mm_dwmkv", out_dtype=_BF)
    d_memn = _mm(d_kvm, wmkv, tb=True, name="mm_dmemn", out_dtype=F32)

    def memnorm_bwd_kern(x_ref, d_ref, dg_ref):
        xh, _ = _rms(x_ref[...], D)
        dg_ref[...] = jnp.sum(d_ref[...] * xh, axis=0, keepdims=True)

    g_mem_norm = pl.pallas_call(
        memnorm_bwd_kern, name="memnorm_bwd", grid=(1,),
        in_specs=[fullb((M, D)), fullb((M, D))], out_specs=fullb((1, D)),
        out_shape=jax.ShapeDtypeStruct((1, D), F32), compiler_params=_cparams(("arbitrary",)),
    )(memx, d_memn)

    def gate_bwd_kern(dp_any, da_ref, y_ref, z_ref, dy_ref, dz_ref):
        z = z_ref[...].astype(F32)
        da = da_ref[...].astype(F32)
        sg = _sig(z)
        dy_ref[...] = (da * (z * sg)).astype(dy_ref.dtype)
        dz_ref[...] = (da * y_ref[...].astype(F32) * (sg * (1.0 + z * (1.0 - sg)))).astype(dz_ref.dtype)

    d_mlay, dproj4 = pl.pallas_call(
        gate_bwd_kern, name="gate_mla_bwd", grid=(nr2,),
        in_specs=[any_spec, rowb2(HV, 0), rowb2(HV, 0), rowb2(HV, mz_blk)],
        out_specs=[rowb2(HV, 0), rowb2(HV, mz_blk)],
        out_shape=[jax.ShapeDtypeStruct((N, HV), _BF), jax.ShapeDtypeStruct((N, c.P), _BF)],
        input_output_aliases={0: 1}, compiler_params=_cparams(("parallel",)),
    )(dproj3, d_amla, mla_y, proj)

    def attn_bwd_kern(q_ref, k_ref, v_ref, o_ref, do_ref, lse_ref, dq_ref, dk_ref, dv_ref, dq_sc, dl_sc, dk_sc, dv_sc):
        i = pl.program_id(1)
        delta = jnp.sum(do_ref[...].astype(F32) * o_ref[...].astype(F32), axis=1, keepdims=True)
        dl_sc[...] = jnp.broadcast_to(delta, dl_sc.shape)
        dq_sc[...] = jnp.zeros(dq_sc.shape, F32)

        def step(t, diag):
            rows = pl.ds(pl.multiple_of(t * B, B), B)
            rs = live_rows(diag)
            q, do = q_ref[0, rs, :], do_ref[rs, :]
            k = k_ref[0, rows, :]
            s = _dot_nt(q, k)
            if diag is not None:
                s = diag_mask(s)
            dpm = _dot_nt(do, v_ref[rows, :])
            lse_t, dl = lse_ref[0, rs, :], dl_sc[rs, :]
            ps, dss = [], []
            for cb in range(B // LANES):
                cols = slice(cb * LANES, (cb + 1) * LANES)
                p_cb = jnp.exp2(s[:, cols] - lse_t)
                ps.append(p_cb.astype(_BF))
                dss.append((p_cb * (dpm[:, cols] - dl)).astype(_BF))
            p, ds = jnp.concatenate(ps, axis=1), jnp.concatenate(dss, axis=1)
            dvp = _dot_tn(p, do)
            dkp = _dot_tn(ds, q)
            if diag is not None:
                dk_sc[rows, :] = dkp
                dv_sc[rows, :] = dvp
            else:
                dk_sc[rows, :] += dkp
                dv_sc[rows, :] += dvp
            dq_sc[rs, :] += _dot(ds, k)

        def run(first, count, n_diag):
            for u in range(count):
                step(first + u, u - (count - n_diag) if u >= count - n_diag else None)

        key_block_plan(i, run)
        dq_ref[0] = dq_sc[...].astype(dq_ref.dtype)

        @pl.when(i == nq - 1)
        def _():
            dk_ref[0] = dk_sc[...].astype(dk_ref.dtype)
            dv_ref[0] = dv_sc[...].astype(dv_ref.dtype)

    d_qcat, d_kcat, d_v = pl.pallas_call(
        attn_bwd_kern, name="attn_bwd", grid=(H, nq),
        in_specs=[q_blk, k_head, v_head, o_blk, o_blk, lse_blk],
        out_specs=[pl.BlockSpec((1, BQ, HW), lambda hh, i: (hh, i, 0)),
                   pl.BlockSpec((1, N, HW), lambda hh, i: (hh, 0, 0)),
                   pl.BlockSpec((1, N, LANES), lambda hh, i: (hh, 0, 0))],
        out_shape=[jax.ShapeDtypeStruct((H, N, HW), _BF), jax.ShapeDtypeStruct((H, N, HW), _BF),
                   jax.ShapeDtypeStruct((H, N, LANES), _BF)],
        scratch_shapes=[pltpu.VMEM((BQ, HW), F32), pltpu.VMEM((BQ, LANES), F32), pltpu.VMEM((N, HW), F32),
                        pltpu.VMEM((N, LANES), F32)],
        compiler_params=_cparams(("parallel", "arbitrary")),
    )(q_cat, k_cat, kv, mla_y, d_mlay, lse)

    def q_prep_bwd_kern(q_ref, dq_ref, t_ref, gn_ref, gr_ref, o_ref, dgn_ref, dgr_ref):
        @pl.when(jnp.logical_and(pl.program_id(0) == 0, pl.program_id(1) == 0))
        def _():
            dgn_ref[...] = jnp.zeros(dgn_ref.shape, F32)
            dgr_ref[...] = jnp.zeros(dgr_ref.shape, F32)

        for g in range(HG):
            blk = q_ref[:, g * HW:(g + 1) * HW].astype(F32)
            d = dq_ref[g].astype(F32) * c.scale
            nh, rn = _rms128(blk[:, :LANES], c.NOPE)
            rhat, rr = _rms128(blk[:, LANES:], c.ROPE)
            dn, dgn = _rms128_bwd(nh, rn, gn_ref[...], d[:, :LANES], c.NOPE)
            drot = _rope_t(d[:, LANES:], t_ref[0], t_ref[1], t_ref[2])
            dr, dgr = _rms128_bwd(rhat, rr, gr_ref[...], drot, c.ROPE)
            o_ref[:, g * HW:(g + 1) * HW] = jnp.concatenate([dn, dr], axis=1).astype(o_ref.dtype)
            dgn_ref[...] += jnp.sum(dgn, axis=0, keepdims=True)
            dgr_ref[...] += jnp.sum(dgr, axis=0, keepdims=True)

    d_qp, g_qn_nope, g_qn_rope = pl.pallas_call(
        q_prep_bwd_kern, name="q_prep_bwd", grid=(nrp, nhg),
        in_specs=[heads_in, heads_out, tabs_of(RP), fullb((1, LANES)), fullb((1, LANES))],
        out_specs=[heads_in, fullb((1, LANES)), fullb((1, LANES))],
        out_shape=[jax.ShapeDtypeStruct((N, H * HW), _BF), jax.ShapeDtypeStruct((1, LANES), F32),
                   jax.ShapeDtypeStruct((1, LANES), F32)],
        compiler_params=_cparams(("arbitrary", "arbitrary")),
    )(q_p, d_qcat, tabs, g_qn, g_qr)

    def k_prep_bwd_kern(kv_ref, dk_ref, dv_ref, gn_ref, o_ref, dkr_ref, dgn_ref):
        hg = pl.program_id(1)

        @pl.when(jnp.logical_and(pl.program_id(0) == 0, hg == 0))
        def _():
            dgn_ref[...] = jnp.zeros(dgn_ref.shape, F32)

        @pl.when(hg == 0)
        def _():
            dkr_ref[...] = jnp.zeros(dkr_ref.shape, F32)

        dkr = jnp.zeros((RP, LANES), F32)
        for g in range(HG):
            dk = dk_ref[g].astype(F32) * (1.0 / LOG2E)
            kn, r = _rms128(kv_ref[:, g * HW:g * HW + LANES].astype(F32), c.NOPE)
            dkn, dgn = _rms128_bwd(kn, r, gn_ref[...], dk[:, :LANES], c.NOPE)
            o_ref[:, g * HW:(g + 1) * HW] = jnp.concatenate([dkn.astype(o_ref.dtype), dv_ref[g]], axis=1)
            dgn_ref[...] += jnp.sum(dgn, axis=0, keepdims=True)
            dkr += dk[:, LANES:]
        dkr_ref[...] += dkr

    d_kv, d_krsum, g_kn_nope = pl.pallas_call(
        k_prep_bwd_kern, name="k_prep_bwd", grid=(nrp, nhg),
        in_specs=[heads_in, heads_out, pl.BlockSpec((HG, RP, LANES), lambda i, hg: (hg, i, 0)), fullb((1, LANES))],
        out_specs=[heads_in, pl.BlockSpec((RP, LANES), lambda i, hg: (i, 0)), fullb((1, LANES))],
        out_shape=[jax.ShapeDtypeStruct((N, H * HW), _BF), jax.ShapeDtypeStruct((N, LANES), F32),
                   jax.ShapeDtypeStruct((1, LANES), F32)],
        compiler_params=_cparams(("arbitrary", "arbitrary")),
    )(kv, d_kcat, d_v, g_kn)

    def krope_bwd_kern(p_ref, d_ref, t_ref, g_ref, o_ref, dg_ref):
        @pl.when(pl.program_id(0) == 0)
        def _():
            dg_ref[...] = jnp.zeros(dg_ref.shape, F32)

        xh, r = _rms128(p_ref[...].astype(F32), c.ROPE)
        drot = _rope_t(d_ref[...], t_ref[0], t_ref[1], t_ref[2])
        dx, dg = _rms128_bwd(xh, r, g_ref[...], drot, c.ROPE)
        o_ref[...] = dx.astype(o_ref.dtype)
        dg_ref[...] += jnp.sum(dg, axis=0, keepdims=True)

    d_kr, g_kn_rope = pl.pallas_call(
        krope_bwd_kern, name="krope_bwd", grid=(nr,),
        in_specs=[rowb(LANES, 0), rowb(LANES, 0), tabs_of(R), fullb((1, LANES))],
        out_specs=[rowb(LANES, 0), fullb((1, LANES))],
        out_shape=[jax.ShapeDtypeStruct((N, LANES), _BF), jax.ShapeDtypeStruct((1, LANES), F32)],
        compiler_params=_cparams(("arbitrary",)),
    )(kr_raw, d_krsum, tabs, g_kr)
    dproj5 = dproj4

    g_wuq_p = _mm(cqn, d_qp, ta=True, name="mm_dwuq", out_dtype=_BF, bk=4096)
    G['w_uq'] = g_wuq_p.reshape(QL, H, HW)[:, :, :c.NOPE + c.ROPE].reshape(QL, H * (c.NOPE + c.ROPE))
    d_cqn = _mm(d_qp, wuq_pT, name="mm_dcqn", out_dtype=F32, bk=4096)
    G['w_ukv'] = _mm(ckvn, d_kv, ta=True, name="mm_dwukv", out_dtype=_BF, bk=4096)
    d_ckvn = _mm(d_kv, wukvT, name="mm_dckvn", out_dtype=F32, bk=4096)

    def to_blocks(n, g):
        if n in COL_SHARDED:
            return jnp.transpose(g.reshape(g.shape[0], NDEV, -1), (1, 0, 2))
        return g.reshape(NDEV, -1, g.shape[1])

    def landing(b):
        return lax.empty(b.shape, b.dtype)

    early = [n for n in BIG if n != 'w_in']
    blocks_e = [to_blocks(n, G[n]) for n in early]
    xe = _split_start(blocks_e, [landing(b) for b in blocks_e], False, "xchg_early_start")
    gq_after = mla_q_norm_g + xe[4][0:1, 0:1]

    def lora_bwd_kern(dp_any, p_ref, dq_ref, dkv_ref, gq_ref, gkv_ref, o_ref, dgq_ref, dgkv_ref):
        @pl.when(pl.program_id(0) == 0)
        def _():
            dgq_ref[...] = jnp.zeros(dgq_ref.shape, F32)
            dgkv_ref[...] = jnp.zeros(dgkv_ref.shape, F32)

        blk = p_ref[...].astype(F32)
        qh, rq = _rms(blk[:, :QL], QL)
        kh, rk = _rms(blk[:, QL:], KVL)
        dq, dgq = _rms_bwd(qh, rq, gq_ref[...], dq_ref[...], QL)
        dk, dgk = _rms_bwd(kh, rk, gkv_ref[...], dkv_ref[...], KVL)
        o_ref[:, :QL] = dq.astype(o_ref.dtype)
        o_ref[:, QL:] = dk.astype(o_ref.dtype)
        dgq_ref[...] += jnp.sum(dgq, axis=0, keepdims=True)
        dgkv_ref[...] += jnp.sum(dgk, axis=0, keepdims=True)

    dproj6, g_q_norm, g_kv_norm = pl.pallas_call(
        lora_bwd_kern, name="lora_bwd", grid=(nr,),
        in_specs=[any_spec, rowb(QL + KVL, lora_blk), rowb(QL, 0), rowb(KVL, 0), fullb((1, QL)), fullb((1, KVL))],
        out_specs=[rowb(QL + KVL, lora_blk), fullb((1, QL)), fullb((1, KVL))],
        out_shape=[jax.ShapeDtypeStruct((N, c.P), _BF), jax.ShapeDtypeStruct((1, QL), F32),
                   jax.ShapeDtypeStruct((1, KVL), F32)],
        input_output_aliases={0: 0}, compiler_params=_cparams(("arbitrary",)),
    )(dproj5, proj, d_cqn, d_ckvn, gq_after, mla_kv_norm_g)

    g_win_p = _mm(h, dproj6, ta=True, name="mm_dwin", out_dtype=_BF, bk=4096)
    g_wkr = _mm(h, d_kr, ta=True, name="mm_dwkr", out_dtype=_BF, bk=4096)
    blocks_w = [_win_blocks(g_win_p, g_wkr, c)]
    xw = _split_start(blocks_w, [landing(b) for b in blocks_w], False, "xchg_win_start")
    d_h = _mm(dproj6, win_pT, name="mm_dh", out_dtype=F32, bk=3072, after=xw[4], plus=(d_kr, w_krT))

    def final_bwd_kern(x_ref, g_ref, dh_ref, dy_ref, gx_ref, dg_ref):
        @pl.when(pl.program_id(0) == 0)
        def _():
            dg_ref[...] = jnp.zeros(dg_ref.shape, F32)

        xh, r = _rms(x_ref[...], D)
        dx, dg = _rms_bwd(xh, r, g_ref[...], dh_ref[...], D)
        gx_ref[...] = dy_ref[...] + dx
        dg_ref[...] += jnp.sum(dg, axis=0, keepdims=True)

    grad_x, g_norm = pl.pallas_call(
        final_bwd_kern, name="final_bwd", grid=(nr2,),
        in_specs=[rowb2(D, 0), fullb((1, D)), rowb2(D, 0), rowb2(D, 0)],
        out_specs=[rowb2(D, 0), fullb((1, D))],
        out_shape=[jax.ShapeDtypeStruct((N, D), F32), jax.ShapeDtypeStruct((1, D), F32)],
        compiler_params=_cparams(("arbitrary",)),
    )(xs, norm_g, d_h, dy)

    res = [{}, {}, {}, {}]

    def adam_into(n, parts):
        outs = _adam(parts, W[n][0], Mo[n][0], Vo[n][0], "adam_" + n)
        for k in range(4):
            res[k][n] = outs[k][None]
        return outs[0]

    recv_e = _split_wait(xe, False, grad_x, "xchg_early_wait")
    last = [adam_into(n, parts) for n, parts in zip(early, recv_e)][-1]
    recv_w = _split_wait(xw, False, last, "xchg_win_wait")
    adam_into('w_in', recv_w[0])

    small_g = {'norm_g': g_norm, 'mla_q_norm_g': g_q_norm, 'mla_kv_norm_g': g_kv_norm,
               'mla_qn_nope_g': g_qn_nope, 'mla_qn_rope_g': g_qn_rope[:, :c.ROPE], 'mla_kn_nope_g': g_kn_nope,
               'mla_kn_rope_g': g_kn_rope[:, :c.ROPE], 'mem_norm_g': g_mem_norm, 'mem_qn_g': g_mem_qn,
               'mem_kn_g': g_mem_kn}
    small_part = _pack([small_g[n] for n in SMALL] + [g_convw[0:3, :]], 0)
    small_all = _all_gather([small_part], "ag_small_grads")[0]
    small_shapes = [W[n].shape for n in SMALL]
    pieces = _unpack(small_all, small_shapes + [(3, CW)])
    cw8 = CW // NDEV
    conv_mine = lax.dynamic_slice_in_dim(pieces[-1].reshape(NDEV, 3, NDEV, cw8), me, 1, axis=2)[:, :, 0, :]
    sm_parts = _pack(pieces[:-1] + [conv_mine], 1)
    sm_names = SMALL + ['conv_w']
    sm_shapes = small_shapes + [(3, cw8)]
    w_sm = _pack([W[n] for n in SMALL] + [conv_w[0]], 0)
    m_sm = _pack([Mo[n] for n in SMALL] + [m_conv_w[0]], 0)
    v_sm = _pack([Vo[n] for n in SMALL] + [v_conv_w[0]], 0)
    outs_sm = [_unpack(o, sm_shapes) for o in _adam(sm_parts, w_sm, m_sm, v_sm, "adam_small")]
    for k in range(4):
        for n, a in zip(sm_names, outs_sm[k]):
            res[k][n] = a[None] if n == 'conv_w' else a
    return (loss, grad_x[None], *[res[0][n] for n in WEIGHTS], *[res[1][n] for n in WEIGHTS],
            *[res[2][n] for n in WEIGHTS], *[res[3][n] for n in WEIGHTS])
```

```python
import math

import jax
import jax.numpy as jnp
from jax import lax
from jax.experimental import pallas as pl
from jax.experimental.pallas import tpu as pltpu

F32 = jnp.float32
_BF = jnp.bfloat16
EPS = 1e-6
CHUNK = 64
ROPE_THETA = 10000.0
ADAM_LR, ADAM_B1, ADAM_B2, ADAM_EPS, ADAM_WD, ADAM_STEP = 0.001, 0.9, 0.999, 1e-08, 0.01, 10
NDEV = 8
AXES = ("x", "y", "c")
MESH = pl.DeviceIdType.MESH
LANES = 128
NEG = -1e30
LOG2E = math.log2(math.e)
V7X_VMEM_LIMIT = 56 * 1024 * 1024
PACK_C = 1024
ATT_BLOCK = 512
ATT_UNROLL = 8
ATT_QBLOCKS = 2
ADAM_BLOCK_ELEMS = 256 * 1024

WEIGHTS = ['norm_g', 'w_in', 'conv_w', 'w_conv_out', 'mla_q_norm_g', 'w_uq', 'mla_kv_norm_g', 'w_ukv',
           'mla_qn_nope_g', 'mla_qn_rope_g', 'mla_kn_nope_g', 'mla_kn_rope_g', 'w_mla_out', 'mem_norm_g',
           'w_mem_kv', 'mem_qn_g', 'mem_kn_g', 'w_mem_out', 'w_o']
BIG = ['w_in', 'w_conv_out', 'w_uq', 'w_ukv', 'w_mla_out', 'w_mem_kv', 'w_mem_out', 'w_o']
COL_SHARDED = ('w_in', 'w_conv_out', 'w_uq', 'w_ukv', 'w_mem_out')
SMALL = ['norm_g', 'mla_q_norm_g', 'mla_kv_norm_g', 'mla_qn_nope_g', 'mla_qn_rope_g', 'mla_kn_nope_g',
         'mla_kn_rope_g', 'mem_norm_g', 'mem_qn_g', 'mem_kn_g']


def _tile(dim, target, align):
    if dim <= target:
        return dim
    t = target - target % align
    while t > 0:
        if dim % t == 0:
            return t
        t -= align
    raise ValueError(f"no tile for {dim} {target} {align}")


def _cparams(sem):
    return pltpu.CompilerParams(dimension_semantics=sem, vmem_limit_bytes=V7X_VMEM_LIMIT)


def _sig(x):
    return 1.0 / (1.0 + jnp.exp(-x))


def _rms(x, n):
    r = lax.rsqrt(jnp.sum(x * x, axis=-1, keepdims=True) * (1.0 / n) + EPS)
    return x * r, r


def _rms_bwd(xhat, r, g, dy, n):
    dxh = dy * g
    dx = r * (dxh - xhat * (jnp.sum(dxh * xhat, axis=-1, keepdims=True) * (1.0 / n)))
    return dx, dy * xhat


def _rowsum128(x):
    return jnp.dot(x.astype(_BF), jnp.ones((LANES, LANES), _BF), preferred_element_type=F32)


def _rms128(x, n):
    r = lax.rsqrt(_rowsum128(x * x) * (1.0 / n) + EPS)
    return x * r, r


def _rms128_bwd(xhat, r, g, dy, n):
    dxh = dy * g
    dx = r * (dxh - xhat * (_rowsum128(dxh * xhat) * (1.0 / n)))
    return dx, dy * xhat


def _rope(x, cosp, sina, sinb):
    return x * cosp + pltpu.roll(x, 96, 1) * sina + pltpu.roll(x, 32, 1) * sinb


def _rope_t(d, cosp, sina, sinb):
    return d * cosp + pltpu.roll(d * sina, 32, 1) + pltpu.roll(d * sinb, 96, 1)


def _dot_nt(a, b):
    return lax.dot_general(a, b, (((1,), (1,)), ((), ())), preferred_element_type=F32)


def _dot_tn(a, b):
    return lax.dot_general(a, b, (((0,), (0,)), ((), ())), preferred_element_type=F32)


def _dot(a, b):
    return jnp.dot(a, b, preferred_element_type=F32)


def _all_gather(shards, name):
    na = len(shards)
    nc = 9
    halves = [s.shape[0] // 32 * 16 if s.shape[0] >= 256 else None for s in shards]

    def body(*refs):
        x_refs, out_refs = refs[:na], refs[na:2 * na]
        send_sems, recv_sems, local_sems = refs[2 * na:]
        x, y, c = lax.axis_index("x"), lax.axis_index("y"), lax.axis_index("c")
        me, sib = (x, y, c), (x, y, 1 - c)
        px, py, pd = (1 - x, y, c), (x, 1 - y, c), (1 - x, 1 - y, c)

        def other_core(p):
            return (p[0], p[1], 1 - p[2])

        def rows(a, blk, part=None):
            r = out_refs[a].at[4 * blk[0] + 2 * blk[1] + blk[2]]
            if part is None or halves[a] is None:
                return r
            rest = shards[a].shape[0] - halves[a]
            return r.at[pl.ds(0, halves[a])] if part == 0 else r.at[pl.ds(halves[a], rest)]

        def copy(a, k, blk, to, part=None, src=None):
            dst = rows(a, blk, part)
            return pltpu.make_async_remote_copy(
                src_ref=dst if src is None else src, dst_ref=dst,
                send_sem=send_sems.at[nc * a + k], recv_sem=recv_sems.at[nc * a + k],
                device_id=to, device_id_type=MESH)

        mine = [pltpu.make_async_copy(x_refs[a], rows(a, me), local_sems.at[a]) for a in range(na)]
        for cp in mine:
            cp.start()
        started = []
        for a in range(na):
            started += [copy(a, 0, me, px, src=x_refs[a]), copy(a, 1, me, py, src=x_refs[a]),
                        copy(a, 2, me, sib, src=x_refs[a])]
        for cp in started:
            cp.start()

        def forward(cp):
            cp.start()
            started.append(cp)

        for a in range(na):
            copy(a, 0, px, me).wait_recv()
            forward(copy(a, 3, px, py, part=0))
            forward(copy(a, 4, px, sib))
        for a in range(na):
            copy(a, 1, py, me).wait_recv()
            if halves[a] is not None:
                forward(copy(a, 5, py, px, part=1))
            forward(copy(a, 6, py, sib))
        for a in range(na):
            copy(a, 3, pd, me, part=0).wait_recv()
            forward(copy(a, 7, pd, sib, part=0))
            if halves[a] is not None:
                copy(a, 5, pd, me, part=1).wait_recv()
                forward(copy(a, 8, pd, sib, part=1))
        for a in range(na):
            copy(a, 2, sib, me).wait_recv()
            copy(a, 4, other_core(px), me).wait_recv()
            copy(a, 6, other_core(py), me).wait_recv()
            copy(a, 7, other_core(pd), me, part=0).wait_recv()
            if halves[a] is not None:
                copy(a, 8, other_core(pd), me, part=1).wait_recv()
        for cp in started:
            cp.wait_send()
        for cp in mine:
            cp.wait()

    any_spec = pl.BlockSpec(memory_space=pl.ANY)
    return pl.pallas_call(
        body, name=name,
        out_shape=[jax.ShapeDtypeStruct((NDEV,) + s.shape, s.dtype) for s in shards],
        in_specs=[any_spec] * na, out_specs=[any_spec] * na,
        scratch_shapes=[pltpu.SemaphoreType.DMA((nc * na,)), pltpu.SemaphoreType.DMA((nc * na,)),
                        pltpu.SemaphoreType.DMA((na,))],
    )(*shards)


_HBM = pl.BlockSpec(memory_space=pltpu.HBM)
_SEM = pl.BlockSpec(memory_space=pltpu.SEMAPHORE)
_EFFECT = pltpu.SideEffectType.DATAFLOW_SIDE_EFFECTING


def _split_copy(a, k, src_refs, land_refs, send_sems, recv_sems, gather, receive_side):
    x, y, c = lax.axis_index("x"), lax.axis_index("y"), lax.axis_index("c")
    me = 4 * x + 2 * y + c
    tx, ty, tc = x ^ ((k + 1) >> 2 & 1), y ^ ((k + 1) >> 1 & 1), c ^ ((k + 1) & 1)
    peer = 4 * tx + 2 * ty + tc
    return pltpu.make_async_remote_copy(
        src_ref=src_refs[a] if gather else src_refs[a].at[peer],
        dst_ref=land_refs[a].at[peer if receive_side else me],
        send_sem=send_sems.at[7 * a + k], recv_sem=recv_sems.at[7 * a + k],
        device_id=(tx, ty, tc), device_id_type=MESH)


def _own_copy(a, na, src_refs, land_refs, send_sems, gather):
    me = 4 * lax.axis_index("x") + 2 * lax.axis_index("y") + lax.axis_index("c")
    return pltpu.make_async_copy(src_refs[a] if gather else src_refs[a].at[me], land_refs[a].at[me],
                                 send_sems.at[7 * na + a])


def _split_start(srcs, lands, gather, name, after=None):
    na = len(srcs)
    extra = [] if after is None else [after]

    def body(*refs):
        src_refs, land_refs = refs[:na], refs[na:2 * na]
        send_sems, recv_sems = refs[2 * na + len(extra)], refs[2 * na + len(extra) + 1]
        token = refs[-1]
        for k in range(7):
            for a in range(na):
                _split_copy(a, k, src_refs, land_refs, send_sems, recv_sems, gather, False).start()
        for a in range(na):
            _own_copy(a, na, src_refs, land_refs, send_sems, gather).start()
        token[...] = jnp.zeros_like(token)

    hbm = [pltpu.HBM(b.shape, b.dtype) for b in list(srcs) + list(lands)]
    outs = pl.pallas_call(
        body, name=name,
        out_shape=(pltpu.SemaphoreType.DMA((8 * na,)), pltpu.SemaphoreType.DMA((7 * na,)), *hbm,
                   jax.ShapeDtypeStruct((8, LANES), F32)),
        in_specs=[_HBM] * (2 * na) + [pl.BlockSpec(memory_space=pl.ANY)] * len(extra),
        out_specs=(_SEM, _SEM, *[_HBM] * (2 * na), pl.BlockSpec(memory_space=pltpu.VMEM)),
        input_output_aliases={j: 2 + j for j in range(2 * na)},
        compiler_params=pltpu.CompilerParams(has_side_effects=_EFFECT),
    )(*[pltpu.with_memory_space_constraint(b, pltpu.HBM) for b in srcs],
      *[pltpu.with_memory_space_constraint(l, pltpu.HBM) for l in lands], *extra)
    return outs[0], outs[1], outs[2:2 + na], outs[2 + na:2 + 2 * na], outs[-1]


def _split_wait(started, gather, after, name):
    send_sems, recv_sems, srcs, lands, _ = started
    na = len(srcs)

    def body(*refs):
        src_refs, land_refs = refs[:na], refs[na:2 * na]
        send_s, recv_s = refs[2 * na], refs[2 * na + 1]
        for k in range(7):
            for a in range(na):
                cp = _split_copy(a, k, src_refs, land_refs, send_s, recv_s, gather, True)
                cp.wait_send()
                cp.wait_recv()
        for a in range(na):
            _own_copy(a, na, src_refs, land_refs, send_s, gather).wait()

    hbm = [pltpu.HBM(b.shape, b.dtype) for b in list(srcs) + list(lands)]
    outs = pl.pallas_call(
        body, name=name, out_shape=tuple(hbm),
        in_specs=[_HBM] * (2 * na) + [_SEM, _SEM, pl.BlockSpec(memory_space=pl.ANY)],
        out_specs=tuple([_HBM] * (2 * na)),
        input_output_aliases={j: j for j in range(2 * na)},
        compiler_params=pltpu.CompilerParams(has_side_effects=_EFFECT),
    )(*srcs, *lands, send_sems, recv_sems, after)
    return outs[na:]


def _seg_rows(size):
    rows = -(-size // PACK_C)
    return -(-rows // 16) * 16


def _pack(arrs, lead):
    parts = []
    for a in arrs:
        lshape = a.shape[:lead]
        f = a.reshape(lshape + (-1,)).astype(F32)
        rows = _seg_rows(f.shape[-1])
        f = jnp.pad(f, [(0, 0)] * lead + [(0, rows * PACK_C - f.shape[-1])])
        parts.append(f.reshape(lshape + (rows, PACK_C)))
    return jnp.concatenate(parts, axis=lead)


def _unpack(buf, shapes):
    lshape = buf.shape[:-2]
    out, r = [], 0
    for shp in shapes:
        size = math.prod(shp)
        rows = _seg_rows(size)
        seg = buf[..., r:r + rows, :].reshape(lshape + (rows * PACK_C,))[..., :size]
        out.append(seg.reshape(lshape + tuple(shp)))
        r += rows
    return out


def _mm(a, b, *, name, out_dtype, ta=False, tb=False, bm=1024, bn=1024, bk=2048, after=None, plus=None):
    if ta:
        kdim, m = a.shape
    else:
        m, kdim = a.shape
    n, k2 = b.shape if tb else b.shape[::-1]
    assert kdim == k2 and not (ta and tb), (a.shape, b.shape)
    bm = _tile(m, bm, LANES if ta else 16)
    bn = _tile(n, bn, LANES)
    bk = _tile(kdim, bk, LANES)
    nk = kdim // bk
    n_after = 0 if after is None else 1
    n_plus = 0 if plus is None else 2

    def kern(a_ref, b_ref, *rest):
        plus_refs = rest[n_after:n_after + n_plus]
        o_ref, scratch = rest[n_after + n_plus], rest[n_after + n_plus + 1:]
        part = (_dot_tn if ta else _dot_nt if tb else _dot)(a_ref[...], b_ref[...])

        def first(p):
            return p + _dot(plus_refs[0][...], plus_refs[1][...]) if plus is not None else p

        if nk == 1:
            o_ref[...] = first(part).astype(o_ref.dtype)
        else:
            acc = scratch[0] if scratch else o_ref
            k = pl.program_id(2)

            @pl.when(k == 0)
            def _():
                acc[...] = first(jnp.zeros(acc.shape, F32))

            acc[...] += part
            if scratch:
                @pl.when(k == nk - 1)
                def _():
                    o_ref[...] = acc[...].astype(o_ref.dtype)

    a_spec = pl.BlockSpec((bk, bm), lambda i, j, k: (k, i)) if ta else pl.BlockSpec((bm, bk), lambda i, j, k: (i, k))
    b_spec = pl.BlockSpec((bn, bk), lambda i, j, k: (j, k)) if tb else pl.BlockSpec((bk, bn), lambda i, j, k: (k, j))
    extra_specs, extra_args = [], []
    if after is not None:
        extra_specs.append(pl.BlockSpec(after.shape, lambda i, j, k: (0, 0)))
        extra_args.append(after)
    if plus is not None:
        kk = plus[0].shape[1]
        extra_specs += [pl.BlockSpec((bm, kk), lambda i, j, k: (i, 0)), pl.BlockSpec((kk, bn), lambda i, j, k: (0, j))]
        extra_args += list(plus)
    return pl.pallas_call(
        kern, name=name, grid=(m // bm, n // bn, nk),
        in_specs=[a_spec, b_spec] + extra_specs,
        out_specs=pl.BlockSpec((bm, bn), lambda i, j, k: (i, j)),
        out_shape=jax.ShapeDtypeStruct((m, n), out_dtype),
        scratch_shapes=[pltpu.VMEM((bm, bn), F32)] if nk > 1 and out_dtype != F32 else [],
        compiler_params=_cparams(("parallel", "parallel", "arbitrary")),
    )(a, b, *extra_args)


def _adam(parts, w_a, m_a, v_a, name):
    rows, cols = w_a.shape
    rb = _tile(rows, max(8, ADAM_BLOCK_ELEMS // cols // 8 * 8), 8)
    bc1 = 1.0 - ADAM_B1 ** ADAM_STEP
    bc2 = 1.0 - ADAM_B2 ** ADAM_STEP

    def adam_kern(p_ref, w_ref, m_ref, v_ref, g_ref, d_ref, nm_ref, nv_ref):
        g = p_ref[0].astype(F32)
        for j in range(1, NDEV):
            g = g + p_ref[j].astype(F32)
        m_new = ADAM_B1 * m_ref[...] + (1.0 - ADAM_B1) * g
        v_new = ADAM_B2 * v_ref[...] + (1.0 - ADAM_B2) * (g * g)
        g_ref[...] = g
        nm_ref[...] = m_new
        nv_ref[...] = v_new
        d_ref[...] = -ADAM_LR * ((m_new / bc1) / (jnp.sqrt(v_new / bc2) + ADAM_EPS) + ADAM_WD * w_ref[...])

    blk = pl.BlockSpec((rb, cols), lambda i: (i, 0))
    return pl.pallas_call(
        adam_kern, name=name, grid=(rows // rb,),
        in_specs=[pl.BlockSpec((NDEV, rb, cols), lambda i: (0, i, 0)), blk, blk, blk],
        out_specs=[blk] * 4, out_shape=[jax.ShapeDtypeStruct((rows, cols), F32)] * 4,
        compiler_params=_cparams(("parallel",)),
    )(parts, w_a, m_a, v_a)


class _Cfg:
    pass


def _config(x, conv_w, w_uq, w_ukv, mla_qn_nope_g, mla_qn_rope_g, mem, mem_qn_g, w_mem_out, w_mla_out):
    c = _Cfg()
    c.N, c.D = x.shape[1], x.shape[2]
    c.CW = conv_w.shape[2] * NDEV
    c.QL, c.KVL = w_uq.shape[1], w_ukv.shape[1]
    c.NOPE, c.ROPE = mla_qn_nope_g.shape[1], mla_qn_rope_g.shape[1]
    c.H = w_uq.shape[2] * NDEV // (c.NOPE + c.ROPE)
    c.V = w_ukv.shape[2] * NDEV // c.H - c.NOPE
    assert c.NOPE == LANES and c.V == LANES and c.ROPE == LANES // 2
    c.HW = 2 * LANES
    c.HV = c.H * c.V
    assert w_mla_out.shape[1] * NDEV == c.HV
    c.M = mem.shape[1]
    c.MHD = mem_qn_g.shape[1]
    c.MW = w_mem_out.shape[1]
    c.MH = c.MW // c.MHD
    c.o_conv = 0
    c.o_mz = 4 * c.CW
    c.o_g = c.o_mz + c.HV
    c.o_mem = c.o_g + 3 * c.D
    c.o_lora = c.o_mem + 2 * c.MW
    c.P = c.o_lora + c.QL + c.KVL
    assert c.o_mz % c.HV == 0 and c.o_g % (3 * c.D) == 0 and c.o_mem % (2 * c.MW) == 0
    assert c.o_lora % (c.QL + c.KVL) == 0 and c.QL % LANES == 0 and c.KVL % LANES == 0
    c.IN = 4 * c.CW + c.QL + c.KVL + c.ROPE + c.HV + 2 * c.MW + 3 * c.D
    c.R = _tile(c.N, 256, 16)
    c.RP = _tile(c.N, 512, 16)
    c.HG = _tile(c.H, 4, 1)
    c.B = _tile(c.N, ATT_BLOCK, CHUNK)
    c.scale = float((c.NOPE + c.ROPE) ** -0.5)
    c.mscale = float(c.MHD ** -0.5)
    return c


def _win_segments(c):
    ref_order = (('conv', 4 * c.CW), ('lora', c.QL + c.KVL), ('kr', c.ROPE), ('mz', c.HV), ('mem', 2 * c.MW), ('g', 3 * c.D))
    mine = {'conv': c.o_conv, 'mz': c.o_mz, 'g': c.o_g, 'mem': c.o_mem, 'lora': c.o_lora, 'kr': 0}
    segs, o = [], 0
    for nm, wd in ref_order:
        segs.append((nm, o, wd, mine[nm]))
        o += wd
    return segs


def _win_split(g_win_t, c):
    n8 = g_win_t.shape[1]

    def rows(a, wd):
        return [g_win_t[j][max(a, j * n8) - j * n8:min(a + wd, (j + 1) * n8) - j * n8]
                for j in range(a // n8, (a + wd - 1) // n8 + 1)]

    segs = {nm: (a, wd) for nm, a, wd, _ in _win_segments(c)}
    main = [p for nm in ('conv', 'mz', 'g', 'mem', 'lora') for p in rows(*segs[nm])]
    kr = jnp.concatenate(rows(*segs['kr']) + [jnp.zeros((LANES - c.ROPE, g_win_t.shape[2]), g_win_t.dtype)], axis=0)
    return jnp.concatenate(main, axis=0), kr


def _win_blocks(g, g_kr, c):
    n8 = c.IN // NDEV
    blocks = []
    for j in range(NDEV):
        lo, hi = j * n8, (j + 1) * n8
        parts = []
        for nm, a, wd, mine in _win_segments(c):
            s, e = max(a, lo), min(a + wd, hi)
            if s < e:
                parts.append((g_kr if nm == 'kr' else g)[:, mine + s - a:mine + e - a])
        blocks.append(jnp.concatenate(parts, axis=1))
    return jnp.stack(blocks, axis=0)


def kernel(x, positions, mem, norm_g, w_in, conv_w, w_conv_out, mla_q_norm_g, w_uq, mla_kv_norm_g, w_ukv, mla_qn_nope_g, mla_qn_rope_g, mla_kn_nope_g, mla_kn_rope_g, w_mla_out, mem_norm_g, w_mem_kv, mem_qn_g, mem_kn_g, w_mem_out, w_o, loss_target, m_norm_g, m_w_in, m_conv_w, m_w_conv_out, m_mla_q_norm_g, m_w_uq, m_mla_kv_norm_g, m_w_ukv, m_mla_qn_nope_g, m_mla_qn_rope_g, m_mla_kn_nope_g, m_mla_kn_rope_g, m_w_mla_out, m_mem_norm_g, m_w_mem_kv, m_mem_qn_g, m_mem_kn_g, m_w_mem_out, m_w_o, v_norm_g, v_w_in, v_conv_w, v_w_conv_out, v_mla_q_norm_g, v_w_uq, v_mla_kv_norm_g, v_w_ukv, v_mla_qn_nope_g, v_mla_qn_rope_g, v_mla_kn_nope_g, v_mla_kn_rope_g, v_w_mla_out, v_mem_norm_g, v_w_mem_kv, v_mem_qn_g, v_mem_kn_g, v_w_mem_out, v_w_o):
    args = dict(locals())
    W = {n: args[n] for n in WEIGHTS}
    Mo = {n: args['m_' + n] for n in WEIGHTS}
    Vo = {n: args['v_' + n] for n in WEIGHTS}
    c = _config(x, conv_w, w_uq, w_ukv, mla_qn_nope_g, mla_qn_rope_g, mem, mem_qn_g, w_mem_out, w_mla_out)
    N, D, R, B, H = c.N, c.D, c.R, c.B, c.H
    assert x.shape[0] == 1
    xs = x[0]
    tgt = loss_target[0]
    memx = mem[0]
    me = 4 * lax.axis_index("x") + 2 * lax.axis_index("y") + lax.axis_index("c")
    nr = N // R

    g_win, g_taps = _all_gather([W['w_in'][0].astype(_BF).T, conv_w[0]], "ag_w_in")
    rest = [n for n in BIG if n != 'w_in']
    shards_r = [W[n][0].astype(_BF).T if n in COL_SHARDED else W[n][0].astype(_BF) for n in rest]
    lands_r = [lax.empty((NDEV,) + s.shape, s.dtype) for s in shards_r]
    ag_rest = _split_start(shards_r, lands_r, True, "ag_rest_start", after=g_win)
    win_pT, w_krT = _win_split(g_win, c)
    convw = jnp.transpose(g_taps, (1, 0, 2)).reshape(3, c.CW)
    convw8 = jnp.pad(convw, ((0, 5), (0, 0)))

    def rowb(width, cidx, rows=R):
        return pl.BlockSpec((rows, width), lambda i, _c=cidx: (i, _c))

    R2 = c.RP
    nr2 = N // R2
    RM = _tile(N, 1024, 16)

    def rowb2(width, cidx):
        return rowb(width, cidx, R2)

    def fullb(shape):
        nd = len(shape)
        return pl.BlockSpec(shape, lambda *_: (0,) * nd)

    def pad_lanes(g, w=LANES):
        return jnp.pad(g, ((0, 0), (0, w - g.shape[1])))

    def tabs_of(rows):
        return pl.BlockSpec((3, rows, LANES), lambda i, *_: (0, i, 0))

    half = c.ROPE // 2
    inv_freq = jnp.power(ROPE_THETA, -jnp.arange(half, dtype=F32) / half)
    invf = jnp.concatenate([inv_freq, inv_freq, jnp.zeros((LANES - c.ROPE,), F32)])[None, :]
    pos_col = positions[0].astype(F32).reshape(N, 1)

    def rope_tab_kern(pos_ref, invf_ref, o_ref):
        ang = pos_ref[...] * invf_ref[...]
        co, si = jnp.cos(ang), jnp.sin(ang)
        lane = lax.broadcasted_iota(jnp.int32, ang.shape, 1)
        o_ref[0] = jnp.where(lane < c.ROPE, co, 0.0)
        o_ref[1] = jnp.where(lane < half, -si, 0.0)
        o_ref[2] = jnp.where(jnp.logical_and(lane >= half, lane < c.ROPE), si, 0.0)

    tabs = pl.pallas_call(
        rope_tab_kern, name="rope_tab", grid=(nr,),
        in_specs=[pl.BlockSpec((R, 1), lambda i: (i, 0)), fullb((1, LANES))],
        out_specs=tabs_of(R),
        out_shape=jax.ShapeDtypeStruct((3, N, LANES), F32),
        compiler_params=_cparams(("parallel",)),
    )(pos_col, invf)

    def make_rms_kern():
        def rms_fwd_kern(x_ref, g_ref, o_ref):
            xh, _ = _rms(x_ref[...].astype(F32), x_ref.shape[-1])
            o_ref[...] = (xh * g_ref[...]).astype(o_ref.dtype)
        return rms_fwd_kern

    h = pl.pallas_call(
        make_rms_kern(), name="rms_x", grid=(nr2,),
        in_specs=[rowb2(D, 0), fullb((1, D))], out_specs=rowb2(D, 0),
        out_shape=jax.ShapeDtypeStruct((N, D), _BF), compiler_params=_cparams(("parallel",)),
    )(xs, norm_g)

    proj = _mm(h, win_pT, tb=True, name="mm_proj", out_dtype=_BF, after=ag_rest[4])
    kr_raw = _mm(h, w_krT, tb=True, name="mm_kr", out_dtype=_BF)

    Wf = {n: g.reshape(-1, g.shape[2]) for n, g in zip(rest, _split_wait(ag_rest, True, proj, "ag_rest_wait"))}
    wuqT = Wf['w_uq'].reshape(H, c.NOPE + c.ROPE, c.QL)
    wuq_pT = jnp.pad(wuqT, ((0, 0), (0, c.HW - c.NOPE - c.ROPE), (0, 0))).reshape(H * c.HW, c.QL)
    wukvT = Wf['w_ukv']
    wcoT, wmo, wmkv, wmemoT, wo = Wf['w_conv_out'], Wf['w_mla_out'], Wf['w_mem_kv'], Wf['w_mem_out'], Wf['w_o']

    CW = c.CW
    conv_blk = c.o_conv // (4 * CW)
    HALO = 16
    rh = R // HALO

    def conv_parts(blk):
        blk = blk.astype(F32)
        return blk[:, 0:CW], blk[:, CW:2 * CW], blk[:, 2 * CW:3 * CW], blk[:, 3 * CW:4 * CW]

    def shifted(cu, prev, i):
        prev = jnp.where(i > 0, prev, 0.0)
        rid = lax.broadcasted_iota(jnp.int32, cu.shape, 0)
        last, last2 = prev[HALO - 1:HALO, :], prev[HALO - 2:HALO - 1, :]
        sh1 = jnp.where(rid == 0, last, pltpu.roll(cu, 1, 0))
        sh2 = jnp.where(rid == 0, last2, jnp.where(rid == 1, last, pltpu.roll(cu, 2, 0)))
        return sh1, sh2

    def conv_fwd_kern(p_ref, prev_ref, w_ref, o_ref):
        i = pl.program_id(0)
        cg, bg, u, z = conv_parts(p_ref[...])
        pc, _, pu, _ = conv_parts(prev_ref[...])
        cu = cg * u
        sh1, sh2 = shifted(cu, pc * pu, i)
        w = w_ref[...]
        conv = w[0:1, :] * sh2 + w[1:2, :] * sh1 + w[2:3, :] * cu
        o_ref[...] = (bg * conv * (z * _sig(z))).astype(o_ref.dtype)

    prev_spec = pl.BlockSpec((HALO, 4 * CW), lambda i: (jnp.maximum(i * rh - 1, 0), conv_blk))
    a_conv = pl.pallas_call(
        conv_fwd_kern, name="conv_fwd", grid=(nr,),
        in_specs=[rowb(4 * CW, conv_blk), prev_spec, fullb((8, CW))],
        out_specs=rowb(CW, 0), out_shape=jax.ShapeDtypeStruct((N, CW), _BF),
        compiler_params=_cparams(("parallel",)),
    )(proj, proj, convw8)
    o_conv = _mm(a_conv, wcoT, tb=True, name="mm_oconv", out_dtype=_BF)

    QL, KVL, HW = c.QL, c.KVL, c.HW
    lora_blk = c.o_lora // (QL + KVL)

    def lora_fwd_kern(p_ref, gq_ref, gkv_ref, q_ref, kv_ref):
        blk = p_ref[...].astype(F32)
        qh, _ = _rms(blk[:, :QL], QL)
        kh, _ = _rms(blk[:, QL:], KVL)
        q_ref[...] = (qh * gq_ref[...]).astype(q_ref.dtype)
        kv_ref[...] = (kh * gkv_ref[...]).astype(kv_ref.dtype)

    cqn, ckvn = pl.pallas_call(
        lora_fwd_kern, name="lora_fwd", grid=(nr,),
        in_specs=[rowb(QL + KVL, lora_blk), fullb((1, QL)), fullb((1, KVL))],
        out_specs=[rowb(QL, 0), rowb(KVL, 0)],
        out_shape=[jax.ShapeDtypeStruct((N, QL), _BF), jax.ShapeDtypeStruct((N, KVL), _BF)],
        compiler_params=_cparams(("parallel",)),
    )(proj, mla_q_norm_g, mla_kv_norm_g)
    q_p = _mm(cqn, wuq_pT, tb=True, name="mm_q", out_dtype=_BF, bn=2048)
    kv = _mm(ckvn, wukvT, tb=True, name="mm_kv", out_dtype=_BF, bn=2048)

    g_qn, g_qr = mla_qn_nope_g, pad_lanes(mla_qn_rope_g)
    g_kn, g_kr = mla_kn_nope_g, pad_lanes(mla_kn_rope_g)

    def krope_fwd_kern(p_ref, t_ref, g_ref, o_ref):
        xh, _ = _rms128(p_ref[...].astype(F32), c.ROPE)
        o_ref[...] = _rope(xh * g_ref[...], t_ref[0], t_ref[1], t_ref[2]).astype(o_ref.dtype)

    k_rope = pl.pallas_call(
        krope_fwd_kern, name="krope_fwd", grid=(nr,),
        in_specs=[rowb(LANES, 0), tabs_of(R), fullb((1, LANES))],
        out_specs=rowb(LANES, 0), out_shape=jax.ShapeDtypeStruct((N, LANES), _BF),
        compiler_params=_cparams(("parallel",)),
    )(kr_raw, tabs, g_kr)

    RP, HG = c.RP, c.HG
    nrp, nhg = N // RP, H // HG
    heads_in = pl.BlockSpec((RP, HG * HW), lambda i, hg: (i, hg))
    heads_out = pl.BlockSpec((HG, RP, HW), lambda i, hg: (hg, i, 0))

    def q_prep_kern(q_ref, t_ref, gn_ref, gr_ref, o_ref):
        for g in range(HG):
            blk = q_ref[:, g * HW:(g + 1) * HW].astype(F32)
            nh, _ = _rms128(blk[:, :LANES], c.NOPE)
            rhat, _ = _rms128(blk[:, LANES:], c.ROPE)
            rot = _rope(rhat * gr_ref[...], t_ref[0], t_ref[1], t_ref[2])
            o_ref[g] = (jnp.concatenate([nh * gn_ref[...], rot], axis=1) * (c.scale * LOG2E)).astype(o_ref.dtype)

    q_cat = pl.pallas_call(
        q_prep_kern, name="q_prep", grid=(nrp, nhg),
        in_specs=[heads_in, tabs_of(RP), fullb((1, LANES)), fullb((1, LANES))],
        out_specs=heads_out, out_shape=jax.ShapeDtypeStruct((H, N, HW), _BF),
        compiler_params=_cparams(("parallel", "parallel")),
    )(q_p, tabs, g_qn, g_qr)

    def k_prep_kern(kv_ref, kr_ref, gn_ref, o_ref):
        for g in range(HG):
            kn, _ = _rms128(kv_ref[:, g * HW:g * HW + LANES].astype(F32), c.NOPE)
            o_ref[g] = jnp.concatenate([(kn * gn_ref[...]).astype(o_ref.dtype), kr_ref[...]], axis=1)

    k_cat = pl.pallas_call(
        k_prep_kern, name="k_prep", grid=(nrp, nhg),
        in_specs=[heads_in, pl.BlockSpec((RP, LANES), lambda i, hg: (i, 0)), fullb((1, LANES))],
        out_specs=heads_out, out_shape=jax.ShapeDtypeStruct((H, N, HW), _BF),
        compiler_params=_cparams(("parallel", "parallel")),
    )(kv, k_rope, g_kn)

    QB = ATT_QBLOCKS if N % (ATT_QBLOCKS * B) == 0 else 1
    BQ = QB * B
    nq = N // BQ
    assert CHUNK & (CHUNK - 1) == 0 and B % CHUNK == 0 and ATT_UNROLL % QB == 0

    def live_rows(diag):
        return slice(0 if diag is None else diag * B, BQ)

    def diag_mask(s):
        row = lax.broadcasted_iota(jnp.int32, s.shape, 0)
        col = lax.broadcasted_iota(jnp.int32, s.shape, 1)
        shift = CHUNK.bit_length() - 1
        allowed = jnp.right_shift(col, shift) <= jnp.right_shift(row, shift)
        return jnp.where(allowed, s, NEG)

    k_head = pl.BlockSpec((1, N, HW), lambda hh, i: (hh, 0, 0))
    v_head = pl.BlockSpec((N, LANES), lambda hh, i: (0, 2 * hh + 1))
    q_blk = pl.BlockSpec((1, BQ, HW), lambda hh, i: (hh, i, 0))
    o_blk = pl.BlockSpec((BQ, LANES), lambda hh, i: (i, hh))
    lse_blk = pl.BlockSpec((1, BQ, LANES), lambda hh, i: (hh, i, 0))

    def key_block_plan(i, run):
        def unrolled(u, carry):
            run(ATT_UNROLL * u, ATT_UNROLL, 0)
            return carry

        n_full = QB * i
        lax.fori_loop(0, n_full // ATT_UNROLL, unrolled, 0)
        for rem in range(0, ATT_UNROLL, QB):
            @pl.when(n_full % ATT_UNROLL == rem)
            def _(rem=rem):
                run(n_full - rem, rem + QB, QB)

    def attn_fwd_kern(q_ref, k_ref, v_ref, o_ref, lse_ref, m_sc, acc_sc, s_sc):
        i = pl.program_id(1)
        m_sc[...] = jnp.full(m_sc.shape, NEG, F32)
        acc_sc[...] = jnp.zeros(acc_sc.shape, F32)

        def rows_of(t):
            return pl.ds(pl.multiple_of(t * B, B), B)

        def scores(t, slot, diag):
            rs = live_rows(diag)
            s_sc[slot, rs, :] = _dot_nt(q_ref[0, rs, :], k_ref[0, rows_of(t), :])

        def softmax_pv(t, slot, diag):
            rs = live_rows(diag)
            s = s_sc[slot, rs, :]
            if diag is not None:
                s = diag_mask(s)
            mt = s[:, 0:LANES]
            for cb in range(1, B // LANES):
                mt = jnp.maximum(mt, s[:, cb * LANES:(cb + 1) * LANES])
            m_prev = m_sc[rs, :]
            m_new = jnp.maximum(m_prev, jnp.max(mt, axis=1, keepdims=True))
            alpha = jnp.exp2(m_prev - m_new)
            p = jnp.concatenate([jnp.exp2(s[:, cb * LANES:(cb + 1) * LANES] - m_new).astype(_BF)
                                 for cb in range(B // LANES)], axis=1)
            v_ones = jnp.concatenate([v_ref[rows_of(t), :], jnp.ones((B, LANES), _BF)], axis=1)
            acc_sc[rs, :] = jnp.concatenate([alpha, alpha], axis=1) * acc_sc[rs, :] + _dot(p, v_ones)
            m_sc[rs, :] = m_new

        scores(0, 0, None)

        def run(first, count, n_diag):
            def diag_of(u):
                return u - (count - n_diag) if count - n_diag <= u < count else None

            for u in range(count):
                if u + 1 < count or n_diag == 0:
                    scores(first + u + 1, (u + 1) % 2, diag_of(u + 1))
                softmax_pv(first + u, u % 2, diag_of(u))

        key_block_plan(i, run)
        acc = acc_sc[...]
        o_ref[...] = (acc[:, :LANES] / acc[:, LANES:]).astype(o_ref.dtype)
        lse_ref[0] = m_sc[...] + jnp.log2(acc[:, LANES:])

    mla_y, lse = pl.pallas_call(
        attn_fwd_kern, name="attn_fwd", grid=(H, nq),
        in_specs=[q_blk, k_head, v_head], out_specs=[o_blk, lse_blk],
        out_shape=[jax.ShapeDtypeStruct((N, c.HV), _BF), jax.ShapeDtypeStruct((H, N, LANES), F32)],
        scratch_shapes=[pltpu.VMEM((BQ, LANES), F32), pltpu.VMEM((BQ, HW), F32), pltpu.VMEM((2, BQ, B), F32)],
        compiler_params=_cparams(("parallel", "arbitrary")),
    )(q_cat, k_cat, kv)

    HV = c.HV
    mz_blk = c.o_mz // HV

    def gate_fwd_kern(y_ref, z_ref, o_ref):
        z = z_ref[...].astype(F32)
        o_ref[...] = (y_ref[...].astype(F32) * (z * _sig(z))).astype(o_ref.dtype)

    a_mla = pl.pallas_call(
        gate_fwd_kern, name="gate_mla", grid=(nr2,),
        in_specs=[rowb2(HV, 0), rowb2(HV, mz_blk)], out_specs=rowb2(HV, 0),
        out_shape=jax.ShapeDtypeStruct((N, HV), _BF), compiler_params=_cparams(("parallel",)),
    )(mla_y, proj)
    o_mla = _mm(a_mla, wmo, name="mm_omla", out_dtype=_BF)

    M, MW, MH, MHD = c.M, c.MW, c.MH, c.MHD
    memn = pl.pallas_call(
        make_rms_kern(), name="rms_mem",
        grid=(1,), in_specs=[fullb((M, D)), fullb((1, D))], out_specs=fullb((M, D)),
        out_shape=jax.ShapeDtypeStruct((M, D), _BF), compiler_params=_cparams(("arbitrary",)),
    )(memx, mem_norm_g)
    kvm = _mm(memn, wmkv, name="mm_memkv", out_dtype=F32)

    def memk_fwd_kern(kv_ref, g_ref, k_ref, v_ref):
        for hh in range(MH):
            kh, _ = _rms(kv_ref[:, hh * MHD:(hh + 1) * MHD], MHD)
            k_ref[:, hh * MHD:(hh + 1) * MHD] = (kh * g_ref[...]).astype(k_ref.dtype)
        v_ref[...] = kv_ref[:, MW:].astype(v_ref.dtype)

    mem_k, mem_v = pl.pallas_call(
        memk_fwd_kern, name="memk_fwd", grid=(1,),
        in_specs=[fullb((M, 2 * MW)), fullb((1, MHD))], out_specs=[fullb((M, MW)), fullb((M, MW))],
        out_shape=[jax.ShapeDtypeStruct((M, MW), _BF)] * 2, compiler_params=_cparams(("arbitrary",)),
    )(kvm, mem_kn_g)

    mem_blk = c.o_mem // (2 * MW)

    def mem_head(qz_ref, k_ref, v_ref, g_ref, hh):
        sl = slice(hh * MHD, (hh + 1) * MHD)
        qh, r = _rms(qz_ref[:, sl].astype(F32), MHD)
        qn = (qh * g_ref[...]).astype(_BF)
        s = _dot_nt(qn, k_ref[:, sl]) * c.mscale
        e = jnp.exp(s - jnp.max(s, axis=1, keepdims=True))
        p = e / jnp.sum(e, axis=1, keepdims=True)
        y = _dot(p.astype(_BF), v_ref[:, sl])
        z = qz_ref[:, MW + hh * MHD:MW + (hh + 1) * MHD].astype(F32)
        return sl, qh, r, qn, p, y, z

    def mem_fwd_kern(qz_ref, k_ref, v_ref, g_ref, o_ref):
        for hh in range(MH):
            sl, _, _, _, _, y, z = mem_head(qz_ref, k_ref, v_ref, g_ref, hh)
            o_ref[:, sl] = (y * (z * _sig(z))).astype(o_ref.dtype)

    a_mem = pl.pallas_call(
        mem_fwd_kern, name="mem_fwd", grid=(N // RM,),
        in_specs=[rowb(2 * MW, mem_blk, RM), fullb((M, MW)), fullb((M, MW)), fullb((1, MHD))],
        out_specs=rowb(MW, 0, RM), out_shape=jax.ShapeDtypeStruct((N, MW), _BF),
        compiler_params=_cparams(("parallel",)),
    )(proj, mem_k, mem_v, mem_qn_g)
    o_mem = _mm(a_mem, wmemoT, tb=True, name="mm_omem", out_dtype=_BF)

    g_blk = c.o_g // (3 * D)

    def merge_fwd_kern(g_ref, oc_ref, om_ref, ome_ref, o_ref):
        g = g_ref[...].astype(F32)
        acc = _sig(g[:, :D]) * oc_ref[...].astype(F32)
        acc += _sig(g[:, D:2 * D]) * om_ref[...].astype(F32)
        acc += _sig(g[:, 2 * D:]) * ome_ref[...].astype(F32)
        o_ref[...] = acc.astype(o_ref.dtype)

    merged = pl.pallas_call(
        merge_fwd_kern, name="merge_fwd", grid=(nr,),
        in_specs=[rowb(3 * D, g_blk), rowb(D, 0), rowb(D, 0), rowb(D, 0)], out_specs=rowb(D, 0),
        out_shape=jax.ShapeDtypeStruct((N, D), _BF), compiler_params=_cparams(("parallel",)),
    )(proj, o_conv, o_mla, o_mem)
    obm, obn = _tile(N, 512, 16), _tile(D, 1024, LANES)

    def out_loss_kern(a_ref, w_ref, x_ref, t_ref, dy_ref, dyb_ref, l_ref):
        e = x_ref[...] + _dot(a_ref[...], w_ref[...]) - t_ref[...]
        dy = e * (1.0 / D)
        dy_ref[...] = dy
        dyb_ref[...] = dy.astype(dyb_ref.dtype)
        row = lax.broadcasted_iota(jnp.int32, l_ref.shape, 0)
        l_ref[...] = jnp.where(row == 0, jnp.sum(e * e, axis=0, keepdims=True), 0.0)

    oblk = pl.BlockSpec((obm, obn), lambda i, j: (i, j))
    dy, dyb, lpart = pl.pallas_call(
        out_loss_kern, name="mm_out_loss", grid=(N // obm, D // obn),
        in_specs=[pl.BlockSpec((obm, D), lambda i, j: (i, 0)), pl.BlockSpec((D, obn), lambda i, j: (0, j)), oblk, oblk],
        out_specs=[oblk, oblk, pl.BlockSpec((8, obn), lambda i, j: (i, j))],
        out_shape=[jax.ShapeDtypeStruct((N, D), F32), jax.ShapeDtypeStruct((N, D), _BF),
                   jax.ShapeDtypeStruct((8 * (N // obm), D), F32)],
        compiler_params=_cparams(("parallel", "parallel")),
    )(merged, wo, xs, tgt)
    loss = lax.psum(jnp.sum(lpart) * (0.5 / D), AXES)

    G = {}
    d_merged = _mm(dyb, wo, tb=True, name="mm_dmerged", out_dtype=_BF)
    G['w_o'] = _mm(merged, dyb, ta=True, name="mm_dwo", out_dtype=_BF, bk=4096)

    dproj0 = lax.empty((N, c.P), _BF)
    any_spec = pl.BlockSpec(memory_space=pl.ANY)

    def merge_bwd_kern(dp_any, g_ref, dm_ref, oc_ref, om_ref, ome_ref, dg_ref, doc_ref, dom_ref, dome_ref):
        g = g_ref[...].astype(F32)
        dm = dm_ref[...].astype(F32)
        for idx, (o_in, d_out) in enumerate(((oc_ref, doc_ref), (om_ref, dom_ref), (ome_ref, dome_ref))):
            sg = _sig(g[:, idx * D:(idx + 1) * D])
            d_out[...] = (sg * dm).astype(d_out.dtype)
            dg_ref[:, idx * D:(idx + 1) * D] = (dm * o_in[...].astype(F32) * sg * (1.0 - sg)).astype(dg_ref.dtype)

    dproj1, d_oconv, d_omla, d_omem = pl.pallas_call(
        merge_bwd_kern, name="merge_bwd", grid=(nr,),
        in_specs=[any_spec, rowb(3 * D, g_blk), rowb(D, 0), rowb(D, 0), rowb(D, 0), rowb(D, 0)],
        out_specs=[rowb(3 * D, g_blk), rowb(D, 0), rowb(D, 0), rowb(D, 0)],
        out_shape=[jax.ShapeDtypeStruct((N, c.P), _BF)] + [jax.ShapeDtypeStruct((N, D), _BF)] * 3,
        input_output_aliases={0: 0}, compiler_params=_cparams(("parallel",)),
    )(dproj0, proj, d_merged, o_conv, o_mla, o_mem)

    G['w_conv_out'] = _mm(a_conv, d_oconv, ta=True, name="mm_dwco", out_dtype=_BF, bk=4096)
    d_aconv = _mm(d_oconv, wcoT, name="mm_daconv", out_dtype=_BF)
    G['w_mla_out'] = _mm(a_mla, d_omla, ta=True, name="mm_dwmo", out_dtype=_BF, bk=4096)
    d_amla = _mm(d_omla, wmo, tb=True, name="mm_damla", out_dtype=_BF)
    G['w_mem_out'] = _mm(a_mem, d_omem, ta=True, name="mm_dwmemo", out_dtype=_BF, bk=4096)
    d_amem = _mm(d_omem, wmemoT, name="mm_damem", out_dtype=_BF)

    def conv_bwd_kern(dp_any, p_ref, prev_ref, next_ref, da_ref, dan_ref, w_ref, o_ref, dw_ref):
        i = pl.program_id(0)
        cg, bg, u, z = conv_parts(p_ref[...])
        pc, _, pu, _ = conv_parts(prev_ref[...])
        _, nbg, _, nz = conv_parts(next_ref[...])
        cu = cg * u
        sh1, sh2 = shifted(cu, pc * pu, i)
        w = w_ref[...]
        conv = w[0:1, :] * sh2 + w[1:2, :] * sh1 + w[2:3, :] * cu
        sg = _sig(z)
        sz = z * sg
        da = da_ref[...].astype(F32)
        dcy = da * sz
        d_z = da * (bg * conv) * (sg * (1.0 + z * (1.0 - sg)))
        d_b = dcy * conv
        dconv = dcy * bg
        dnext = dan_ref[...].astype(F32) * (nz * _sig(nz)) * nbg
        dnext = jnp.where(i < nr - 1, dnext, 0.0)
        rid = lax.broadcasted_iota(jnp.int32, cu.shape, 0)
        up1 = jnp.where(rid == R - 1, dnext[0:1, :], pltpu.roll(dconv, R - 1, 0))
        up2 = jnp.where(rid == R - 2, dnext[0:1, :], jnp.where(rid == R - 1, dnext[1:2, :], pltpu.roll(dconv, R - 2, 0)))
        dcu = w[2:3, :] * dconv + w[1:2, :] * up1 + w[0:1, :] * up2
        o_ref[:, 0:CW] = (dcu * u).astype(o_ref.dtype)
        o_ref[:, CW:2 * CW] = d_b.astype(o_ref.dtype)
        o_ref[:, 2 * CW:3 * CW] = (dcu * cg).astype(o_ref.dtype)
        o_ref[:, 3 * CW:4 * CW] = d_z.astype(o_ref.dtype)

        @pl.when(i == 0)
        def _():
            dw_ref[...] = jnp.zeros(dw_ref.shape, F32)

        dw_ref[0:1, :] += jnp.sum(dconv * sh2, axis=0, keepdims=True)
        dw_ref[1:2, :] += jnp.sum(dconv * sh1, axis=0, keepdims=True)
        dw_ref[2:3, :] += jnp.sum(dconv * cu, axis=0, keepdims=True)

    next_spec = pl.BlockSpec((HALO, 4 * CW), lambda i: (jnp.minimum((i + 1) * rh, N // HALO - 1), conv_blk))
    dan_spec = pl.BlockSpec((HALO, CW), lambda i: (jnp.minimum((i + 1) * rh, N // HALO - 1), 0))
    dproj2, g_convw = pl.pallas_call(
        conv_bwd_kern, name="conv_bwd", grid=(nr,),
        in_specs=[any_spec, rowb(4 * CW, conv_blk), prev_spec, next_spec, rowb(CW, 0), dan_spec, fullb((8, CW))],
        out_specs=[rowb(4 * CW, conv_blk), fullb((8, CW))],
        out_shape=[jax.ShapeDtypeStruct((N, c.P), _BF), jax.ShapeDtypeStruct((8, CW), F32)],
        input_output_aliases={0: 0}, compiler_params=_cparams(("arbitrary",)),
    )(dproj1, proj, proj, proj, d_aconv, d_aconv, convw8)

    def mem_bwd_kern(dp_any, qz_ref, da_ref, k_ref, v_ref, g_ref, o_ref, dk_ref, dv_ref, dg_ref):
        @pl.when(pl.program_id(0) == 0)
        def _():
            dk_ref[...] = jnp.zeros(dk_ref.shape, F32)
            dv_ref[...] = jnp.zeros(dv_ref.shape, F32)
            dg_ref[...] = jnp.zeros(dg_ref.shape, F32)

        for hh in range(MH):
            sl, qh, r, qn, p, y, z = mem_head(qz_ref, k_ref, v_ref, g_ref, hh)
            da = da_ref[:, sl].astype(F32)
            sg = _sig(z)
            dyh = da * (z * sg)
            o_ref[:, MW + hh * MHD:MW + (hh + 1) * MHD] = (da * y * (sg * (1.0 + z * (1.0 - sg)))).astype(o_ref.dtype)
            dyb_h = dyh.astype(_BF)
            dpm = _dot_nt(dyb_h, v_ref[:, sl])
            ds = (p * (dpm - jnp.sum(dpm * p, axis=1, keepdims=True)) * c.mscale).astype(_BF)
            dqn = _dot(ds, k_ref[:, sl])
            dk_ref[:, sl] += _dot_tn(ds, qn)
            dv_ref[:, sl] += _dot_tn(p.astype(_BF), dyb_h)
            dq, dgp = _rms_bwd(qh, r, g_ref[...], dqn, MHD)
            o_ref[:, sl] = dq.astype(o_ref.dtype)
            dg_ref[...] += jnp.sum(dgp, axis=0, keepdims=True)

    dproj3, d_memk, d_memv, g_mem_qn = pl.pallas_call(
        mem_bwd_kern, name="mem_bwd", grid=(N // RM,),
        in_specs=[any_spec, rowb(2 * MW, mem_blk, RM), rowb(MW, 0, RM), fullb((M, MW)), fullb((M, MW)), fullb((1, MHD))],
        out_specs=[rowb(2 * MW, mem_blk, RM), fullb((M, MW)), fullb((M, MW)), fullb((1, MHD))],
        out_shape=[jax.ShapeDtypeStruct((N, c.P), _BF), jax.ShapeDtypeStruct((M, MW), F32),
                   jax.ShapeDtypeStruct((M, MW), F32), jax.ShapeDtypeStruct((1, MHD), F32)],
        input_output_aliases={0: 0}, compiler_params=_cparams(("arbitrary",)),
    )(dproj2, proj, d_amem, mem_k, mem_v, mem_qn_g)

    def memk_bwd_kern(kv_ref, dk_ref, dv_ref, g_ref, o_ref, dg_ref):
        dg = jnp.zeros((1, MHD), F32)
        for hh in range(MH):
            sl = slice(hh * MHD, (hh + 1) * MHD)
            kh, r = _rms(kv_ref[:, sl], MHD)
            dkr, dgp = _rms_bwd(kh, r, g_ref[...], dk_ref[:, sl], MHD)
            o_ref[:, sl] = dkr.astype(o_ref.dtype)
            dg += jnp.sum(dgp, axis=0, keepdims=True)
        o_ref[:, MW:] = dv_ref[...].astype(o_ref.dtype)
        dg_ref[...] = dg

    d_kvm, g_mem_kn = pl.pallas_call(
        memk_bwd_kern, name="memk_bwd", grid=(1,),
        in_specs=[fullb((M, 2 * MW)), fullb((M, MW)), fullb((M, MW)), fullb((1, MHD))],
        out_specs=[fullb((M, 2 * MW)), fullb((1, MHD))],
        out_shape=[jax.ShapeDtypeStruct((M, 2 * MW), _BF), jax.ShapeDtypeStruct((1, MHD), F32)],
        compiler_params=_cparams(("arbitrary",)),
    )(kvm, d_memk, d_memv, mem_kn_g)
    G['w_mem_kv'] = _mm(memn, d_kvm, ta=True, name="mm_dwmkv", out_dtype=_BF)
    d_memn = _mm(d_kvm, wmkv, tb=True, name="mm_dmemn", out_dtype=F32)

    def memnorm_bwd_kern(x_ref, d_ref, dg_ref):
        xh, _ = _rms(x_ref[...], D)
        dg_ref[...] = jnp.sum(d_ref[...] * xh, axis=0, keepdims=True)

    g_mem_norm = pl.pallas_call(
        memnorm_bwd_kern, name="memnorm_bwd", grid=(1,),
        in_specs=[fullb((M, D)), fullb((M, D))], out_specs=fullb((1, D)),
        out_shape=jax.ShapeDtypeStruct((1, D), F32), compiler_params=_cparams(("arbitrary",)),
    )(memx, d_memn)

    def gate_bwd_kern(dp_any, da_ref, y_ref, z_ref, dy_ref, dz_ref):
        z = z_ref[...].astype(F32)
        da = da_ref[...].astype(F32)
        sg = _sig(z)
        dy_ref[...] = (da * (z * sg)).astype(dy_ref.dtype)
        dz_ref[...] = (da * y_ref[...].astype(F32) * (sg * (1.0 + z * (1.0 - sg)))).astype(dz_ref.dtype)

    d_mlay, dproj4 = pl.pallas_call(
        gate_bwd_kern, name="gate_mla_bwd", grid=(nr2,),
        in_specs=[any_spec, rowb2(HV, 0), rowb2(HV, 0), rowb2(HV, mz_blk)],
        out_specs=[rowb2(HV, 0), rowb2(HV, mz_blk)],
        out_shape=[jax.ShapeDtypeStruct((N, HV), _BF), jax.ShapeDtypeStruct((N, c.P), _BF)],
        input_output_aliases={0: 1}, compiler_params=_cparams(("parallel",)),
    )(dproj3, d_amla, mla_y, proj)

    def attn_bwd_kern(q_ref, k_ref, v_ref, o_ref, do_ref, lse_ref, dq_ref, dk_ref, dv_ref, dq_sc, dl_sc, dk_sc, dv_sc):
        i = pl.program_id(1)
        delta = jnp.sum(do_ref[...].astype(F32) * o_ref[...].astype(F32), axis=1, keepdims=True)
        dl_sc[...] = jnp.broadcast_to(delta, dl_sc.shape)
        dq_sc[...] = jnp.zeros(dq_sc.shape, F32)

        def step(t, diag):
            rows = pl.ds(pl.multiple_of(t * B, B), B)
            rs = live_rows(diag)
            q, do = q_ref[0, rs, :], do_ref[rs, :]
            k = k_ref[0, rows, :]
            s = _dot_nt(q, k)
            if diag is not None:
                s = diag_mask(s)
            dpm = _dot_nt(do, v_ref[rows, :])
            lse_t, dl = lse_ref[0, rs, :], dl_sc[rs, :]
            ps, dss = [], []
            for cb in range(B // LANES):
                cols = slice(cb * LANES, (cb + 1) * LANES)
                p_cb = jnp.exp2(s[:, cols] - lse_t)
                ps.append(p_cb.astype(_BF))
                dss.append((p_cb * (dpm[:, cols] - dl)).astype(_BF))
            p, ds = jnp.concatenate(ps, axis=1), jnp.concatenate(dss, axis=1)
            dvp = _dot_tn(p, do)
            dkp = _dot_tn(ds, q)
            if diag is not None:
                dk_sc[rows, :] = dkp
                dv_sc[rows, :] = dvp
            else:
                dk_sc[rows, :] += dkp
                dv_sc[rows, :] += dvp
            dq_sc[rs, :] += _dot(ds, k)

        def run(first, count, n_diag):
            for u in range(count):
                step(first + u, u - (count - n_diag) if u >= count - n_diag else None)

        key_block_plan(i, run)
        dq_ref[0] = dq_sc[...].astype(dq_ref.dtype)

        @pl.when(i == nq - 1)
        def _():
            dk_ref[0] = dk_sc[...].astype(dk_ref.dtype)
            dv_ref[0] = dv_sc[...].astype(dv_ref.dtype)

    d_qcat, d_kcat, d_v = pl.pallas_call(
        attn_bwd_kern, name="attn_bwd", grid=(H, nq),
        in_specs=[q_blk, k_head, v_head, o_blk, o_blk, lse_blk],
        out_specs=[pl.BlockSpec((1, BQ, HW), lambda hh, i: (hh, i, 0)),
                   pl.BlockSpec((1, N, HW), lambda hh, i: (hh, 0, 0)),
                   pl.BlockSpec((1, N, LANES), lambda hh, i: (hh, 0, 0))],
        out_shape=[jax.ShapeDtypeStruct((H, N, HW), _BF), jax.ShapeDtypeStruct((H, N, HW), _BF),
                   jax.ShapeDtypeStruct((H, N, LANES), _BF)],
        scratch_shapes=[pltpu.VMEM((BQ, HW), F32), pltpu.VMEM((BQ, LANES), F32), pltpu.VMEM((N, HW), F32),
                        pltpu.VMEM((N, LANES), F32)],
        compiler_params=_cparams(("parallel", "arbitrary")),
    )(q_cat, k_cat, kv, mla_y, d_mlay, lse)

    def q_prep_bwd_kern(q_ref, dq_ref, t_ref, gn_ref, gr_ref, o_ref, dgn_ref, dgr_ref):
        @pl.when(jnp.logical_and(pl.program_id(0) == 0, pl.program_id(1) == 0))
        def _():
            dgn_ref[...] = jnp.zeros(dgn_ref.shape, F32)
            dgr_ref[...] = jnp.zeros(dgr_ref.shape, F32)

        for g in range(HG):
            blk = q_ref[:, g * HW:(g + 1) * HW].astype(F32)
            d = dq_ref[g].astype(F32) * c.scale
            nh, rn = _rms128(blk[:, :LANES], c.NOPE)
            rhat, rr = _rms128(blk[:, LANES:], c.ROPE)
            dn, dgn = _rms128_bwd(nh, rn, gn_ref[...], d[:, :LANES], c.NOPE)
            drot = _rope_t(d[:, LANES:], t_ref[0], t_ref[1], t_ref[2])
            dr, dgr = _rms128_bwd(rhat, rr, gr_ref[...], drot, c.ROPE)
            o_ref[:, g * HW:(g + 1) * HW] = jnp.concatenate([dn, dr], axis=1).astype(o_ref.dtype)
            dgn_ref[...] += jnp.sum(dgn, axis=0, keepdims=True)
            dgr_ref[...] += jnp.sum(dgr, axis=0, keepdims=True)

    d_qp, g_qn_nope, g_qn_rope = pl.pallas_call(
        q_prep_bwd_kern, name="q_prep_bwd", grid=(nrp, nhg),
        in_specs=[heads_in, heads_out, tabs_of(RP), fullb((1, LANES)), fullb((1, LANES))],
        out_specs=[heads_in, fullb((1, LANES)), fullb((1, LANES))],
        out_shape=[jax.ShapeDtypeStruct((N, H * HW), _BF), jax.ShapeDtypeStruct((1, LANES), F32),
                   jax.ShapeDtypeStruct((1, LANES), F32)],
        compiler_params=_cparams(("arbitrary", "arbitrary")),
    )(q_p, d_qcat, tabs, g_qn, g_qr)

    def k_prep_bwd_kern(kv_ref, dk_ref, dv_ref, gn_ref, o_ref, dkr_ref, dgn_ref):
        hg = pl.program_id(1)

        @pl.when(jnp.logical_and(pl.program_id(0) == 0, hg == 0))
        def _():
            dgn_ref[...] = jnp.zeros(dgn_ref.shape, F32)

        @pl.when(hg == 0)
        def _():
            dkr_ref[...] = jnp.zeros(dkr_ref.shape, F32)

        dkr = jnp.zeros((RP, LANES), F32)
        for g in range(HG):
            dk = dk_ref[g].astype(F32) * (1.0 / LOG2E)
            kn, r = _rms128(kv_ref[:, g * HW:g * HW + LANES].astype(F32), c.NOPE)
            dkn, dgn = _rms128_bwd(kn, r, gn_ref[...], dk[:, :LANES], c.NOPE)
            o_ref[:, g * HW:(g + 1) * HW] = jnp.concatenate([dkn.astype(o_ref.dtype), dv_ref[g]], axis=1)
            dgn_ref[...] += jnp.sum(dgn, axis=0, keepdims=True)
            dkr += dk[:, LANES:]
        dkr_ref[...] += dkr

    d_kv, d_krsum, g_kn_nope = pl.pallas_call(
        k_prep_bwd_kern, name="k_prep_bwd", grid=(nrp, nhg),
        in_specs=[heads_in, heads_out, pl.BlockSpec((HG, RP, LANES), lambda i, hg: (hg, i, 0)), fullb((1, LANES))],
        out_specs=[heads_in, pl.BlockSpec((RP, LANES), lambda i, hg: (i, 0)), fullb((1, LANES))],
        out_shape=[jax.ShapeDtypeStruct((N, H * HW), _BF), jax.ShapeDtypeStruct((N, LANES), F32),
                   jax.ShapeDtypeStruct((1, LANES), F32)],
        compiler_params=_cparams(("arbitrary", "arbitrary")),
    )(kv, d_kcat, d_v, g_kn)

    def krope_bwd_kern(p_ref, d_ref, t_ref, g_ref, o_ref, dg_ref):
        @pl.when(pl.program_id(0) == 0)
        def _():
            dg_ref[...] = jnp.zeros(dg_ref.shape, F32)

        xh, r = _rms128(p_ref[...].astype(F32), c.ROPE)
        drot = _rope_t(d_ref[...], t_ref[0], t_ref[1], t_ref[2])
        dx, dg = _rms128_bwd(xh, r, g_ref[...], drot, c.ROPE)
        o_ref[...] = dx.astype(o_ref.dtype)
        dg_ref[...] += jnp.sum(dg, axis=0, keepdims=True)

    d_kr, g_kn_rope = pl.pallas_call(
        krope_bwd_kern, name="krope_bwd", grid=(nr,),
        in_specs=[rowb(LANES, 0), rowb(LANES, 0), tabs_of(R), fullb((1, LANES))],
        out_specs=[rowb(LANES, 0), fullb((1, LANES))],
        out_shape=[jax.ShapeDtypeStruct((N, LANES), _BF), jax.ShapeDtypeStruct((1, LANES), F32)],
        compiler_params=_cparams(("arbitrary",)),
    )(kr_raw, d_krsum, tabs, g_kr)
    dproj5 = dproj4

    g_wuq_p = _mm(cqn, d_qp, ta=True, name="mm_dwuq", out_dtype=_BF, bk=4096)
    G['w_uq'] = g_wuq_p.reshape(QL, H, HW)[:, :, :c.NOPE + c.ROPE].reshape(QL, H * (c.NOPE + c.ROPE))
    d_cqn = _mm(d_qp, wuq_pT, name="mm_dcqn", out_dtype=F32, bk=4096)
    G['w_ukv'] = _mm(ckvn, d_kv, ta=True, name="mm_dwukv", out_dtype=_BF, bk=4096)
    d_ckvn = _mm(d_kv, wukvT, name="mm_dckvn", out_dtype=F32, bk=4096)

    def to_blocks(n, g):
        if n in COL_SHARDED:
            return jnp.transpose(g.reshape(g.shape[0], NDEV, -1), (1, 0, 2))
        return g.reshape(NDEV, -1, g.shape[1])

    def landing(b):
        return lax.empty(b.shape, b.dtype)

    early = [n for n in BIG if n != 'w_in']
    blocks_e = [to_blocks(n, G[n]) for n in early]
    xe = _split_start(blocks_e, [landing(b) for b in blocks_e], False, "xchg_early_start")
    gq_after = mla_q_norm_g + xe[4][0:1, 0:1]

    def lora_bwd_kern(dp_any, p_ref, dq_ref, dkv_ref, gq_ref, gkv_ref, o_ref, dgq_ref, dgkv_ref):
        @pl.when(pl.program_id(0) == 0)
        def _():
            dgq_ref[...] = jnp.zeros(dgq_ref.shape, F32)
            dgkv_ref[...] = jnp.zeros(dgkv_ref.shape, F32)

        blk = p_ref[...].astype(F32)
        qh, rq = _rms(blk[:, :QL], QL)
        kh, rk = _rms(blk[:, QL:], KVL)
        dq, dgq = _rms_bwd(qh, rq, gq_ref[...], dq_ref[...], QL)
        dk, dgk = _rms_bwd(kh, rk, gkv_ref[...], dkv_ref[...], KVL)
        o_ref[:, :QL] = dq.astype(o_ref.dtype)
        o_ref[:, QL:] = dk.astype(o_ref.dtype)
        dgq_ref[...] += jnp.sum(dgq, axis=0, keepdims=True)
        dgkv_ref[...] += jnp.sum(dgk, axis=0, keepdims=True)

    dproj6, g_q_norm, g_kv_norm = pl.pallas_call(
        lora_bwd_kern, name="lora_bwd", grid=(nr,),
        in_specs=[any_spec, rowb(QL + KVL, lora_blk), rowb(QL, 0), rowb(KVL, 0), fullb((1, QL)), fullb((1, KVL))],
        out_specs=[rowb(QL + KVL, lora_blk), fullb((1, QL)), fullb((1, KVL))],
        out_shape=[jax.ShapeDtypeStruct((N, c.P), _BF), jax.ShapeDtypeStruct((1, QL), F32),
                   jax.ShapeDtypeStruct((1, KVL), F32)],
        input_output_aliases={0: 0}, compiler_params=_cparams(("arbitrary",)),
    )(dproj5, proj, d_cqn, d_ckvn, gq_after, mla_kv_norm_g)

    g_win_p = _mm(h, dproj6, ta=True, name="mm_dwin", out_dtype=_BF, bk=4096)
    g_wkr = _mm(h, d_kr, ta=True, name="mm_dwkr", out_dtype=_BF, bk=4096)
    blocks_w = [_win_blocks(g_win_p, g_wkr, c)]
    xw = _split_start(blocks_w, [landing(b) for b in blocks_w], False, "xchg_win_start")
    d_h = _mm(dproj6, win_pT, name="mm_dh", out_dtype=F32, bk=3072, after=xw[4], plus=(d_kr, w_krT))

    def final_bwd_kern(x_ref, g_ref, dh_ref, dy_ref, gx_ref, dg_ref):
        @pl.when(pl.program_id(0) == 0)
        def _():
            dg_ref[...] = jnp.zeros(dg_ref.shape, F32)

        xh, r = _rms(x_ref[...], D)
        dx, dg = _rms_bwd(xh, r, g_ref[...], dh_ref[...], D)
        gx_ref[...] = dy_ref[...] + dx
        dg_ref[...] += jnp.sum(dg, axis=0, keepdims=True)

    grad_x, g_norm = pl.pallas_call(
        final_bwd_kern, name="final_bwd", grid=(nr2,),
        in_specs=[rowb2(D, 0), fullb((1, D)), rowb2(D, 0), rowb2(D, 0)],
        out_specs=[rowb2(D, 0), fullb((1, D))],
        out_shape=[jax.ShapeDtypeStruct((N, D), F32), jax.ShapeDtypeStruct((1, D), F32)],
        compiler_params=_cparams(("arbitrary",)),
    )(xs, norm_g, d_h, dy)

    res = [{}, {}, {}, {}]

    def adam_into(n, parts):
        outs = _adam(parts, W[n][0], Mo[n][0], Vo[n][0], "adam_" + n)
        for k in range(4):
            res[k][n] = outs[k][None]
        return outs[0]

    recv_e = _split_wait(xe, False, grad_x, "xchg_early_wait")
    last = [adam_into(n, parts) for n, parts in zip(early, recv_e)][-1]
    recv_w = _split_wait(xw, False, last, "xchg_win_wait")
    adam_into('w_in', recv_w[0])

    small_g = {'norm_g': g_norm, 'mla_q_norm_g': g_q_norm, 'mla_kv_norm_g': g_kv_norm,
               'mla_qn_nope_g': g_qn_nope, 'mla_qn_rope_g': g_qn_rope[:, :c.ROPE], 'mla_kn_nope_g': g_kn_nope,
               'mla_kn_rope_g': g_kn_rope[:, :c.ROPE], 'mem_norm_g': g_mem_norm, 'mem_qn_g': g_mem_qn,
               'mem_kn_g': g_mem_kn}
    small_part = _pack([small_g[n] for n in SMALL] + [g_convw[0:3, :]], 0)
    small_all = _all_gather([small_part], "ag_small_grads")[0]
    small_shapes = [W[n].shape for n in SMALL]
    pieces = _unpack(small_all, small_shapes + [(3, CW)])
    cw8 = CW // NDEV
    conv_mine = lax.dynamic_slice_in_dim(pieces[-1].reshape(NDEV, 3, NDEV, cw8), me, 1, axis=2)[:, :, 0, :]
    sm_parts = _pack(pieces[:-1] + [conv_mine], 1)
    sm_names = SMALL + ['conv_w']
    sm_shapes = small_shapes + [(3, cw8)]
    w_sm = _pack([W[n] for n in SMALL] + [conv_w[0]], 0)
    m_sm = _pack([Mo[n] for n in SMALL] + [m_conv_w[0]], 0)
    v_sm = _pack([Vo[n] for n in SMALL] + [v_conv_w[0]], 0)
    outs_sm = [_unpack(o, sm_shapes) for o in _adam(sm_parts, w_sm, m_sm, v_sm, "adam_small")]
    for k in range(4):
        for n, a in zip(sm_names, outs_sm[k]):
            res[k][n] = a[None] if n == 'conv_w' else a
    return (loss, grad_x[None], *[res[0][n] for n in WEIGHTS], *[res[1][n] for n in WEIGHTS],
            *[res[2][n] for n in WEIGHTS], *[res[3][n] for n in WEIGHTS])
```

```python
import math

import jax
import jax.numpy as jnp
from jax import lax
from jax.experimental import pallas as pl
from jax.experimental.pallas import tpu as pltpu

F32 = jnp.float32
_BF = jnp.bfloat16
EPS = 1e-6
CHUNK = 64
ROPE_THETA = 10000.0
ADAM_LR, ADAM_B1, ADAM_B2, ADAM_EPS, ADAM_WD, ADAM_STEP = 0.001, 0.9, 0.999, 1e-08, 0.01, 10
NDEV = 8
AXES = ("x", "y", "c")
MESH = pl.DeviceIdType.MESH
LANES = 128
NEG = -1e30
LOG2E = math.log2(math.e)
V7X_VMEM_LIMIT = 56 * 1024 * 1024
PACK_C = 1024
ATT_BLOCK = 512
ATT_UNROLL_FWD = 8
ATT_UNROLL_BWD = 4
ATT_QBLOCKS = 2
ADAM_BLOCK_ELEMS = 256 * 1024

WEIGHTS = ['norm_g', 'w_in', 'conv_w', 'w_conv_out', 'mla_q_norm_g', 'w_uq', 'mla_kv_norm_g', 'w_ukv',
           'mla_qn_nope_g', 'mla_qn_rope_g', 'mla_kn_nope_g', 'mla_kn_rope_g', 'w_mla_out', 'mem_norm_g',
           'w_mem_kv', 'mem_qn_g', 'mem_kn_g', 'w_mem_out', 'w_o']
BIG = ['w_in', 'w_conv_out', 'w_uq', 'w_ukv', 'w_mla_out', 'w_mem_kv', 'w_mem_out', 'w_o']
COL_SHARDED = ('w_in', 'w_conv_out', 'w_uq', 'w_ukv', 'w_mem_out')
SMALL = ['norm_g', 'mla_q_norm_g', 'mla_kv_norm_g', 'mla_qn_nope_g', 'mla_qn_rope_g', 'mla_kn_nope_g',
         'mla_kn_rope_g', 'mem_norm_g', 'mem_qn_g', 'mem_kn_g']


def _tile(dim, target, align):
    if dim <= target:
        return dim
    t = target - target % align
    while t > 0:
        if dim % t == 0:
            return t
        t -= align
    raise ValueError(f"no tile for {dim} {target} {align}")


def _cparams(sem):
    return pltpu.CompilerParams(dimension_semantics=sem, vmem_limit_bytes=V7X_VMEM_LIMIT)


def _sig(x):
    return 1.0 / (1.0 + jnp.exp(-x))


def _rms(x, n):
    r = lax.rsqrt(jnp.sum(x * x, axis=-1, keepdims=True) * (1.0 / n) + EPS)
    return x * r, r


def _rms_bwd(xhat, r, g, dy, n):
    dxh = dy * g
    dx = r * (dxh - xhat * (jnp.sum(dxh * xhat, axis=-1, keepdims=True) * (1.0 / n)))
    return dx, dy * xhat


def _rowsum128(x):
    return jnp.dot(x.astype(_BF), jnp.ones((LANES, LANES), _BF), preferred_element_type=F32)


def _rms128(x, n):
    r = lax.rsqrt(_rowsum128(x * x) * (1.0 / n) + EPS)
    return x * r, r


def _rms128_bwd(xhat, r, g, dy, n):
    dxh = dy * g
    dx = r * (dxh - xhat * (_rowsum128(dxh * xhat) * (1.0 / n)))
    return dx, dy * xhat


def _rope(x, cosp, sina, sinb):
    return x * cosp + pltpu.roll(x, 96, 1) * sina + pltpu.roll(x, 32, 1) * sinb


def _rope_t(d, cosp, sina, sinb):
    return d * cosp + pltpu.roll(d * sina, 32, 1) + pltpu.roll(d * sinb, 96, 1)


def _dot_nt(a, b):
    return lax.dot_general(a, b, (((1,), (1,)), ((), ())), preferred_element_type=F32)


def _dot_tn(a, b):
    return lax.dot_general(a, b, (((0,), (0,)), ((), ())), preferred_element_type=F32)


def _dot(a, b):
    return jnp.dot(a, b, preferred_element_type=F32)


def _all_gather(shards, name):
    na = len(shards)
    nc = 9
    halves = [s.shape[0] // 32 * 16 if s.shape[0] >= 256 else None for s in shards]

    def body(*refs):
        x_refs, out_refs = refs[:na], refs[na:2 * na]
        send_sems, recv_sems, local_sems = refs[2 * na:]
        x, y, c = lax.axis_index("x"), lax.axis_index("y"), lax.axis_index("c")
        me, sib = (x, y, c), (x, y, 1 - c)
        px, py, pd = (1 - x, y, c), (x, 1 - y, c), (1 - x, 1 - y, c)

        def other_core(p):
            return (p[0], p[1], 1 - p[2])

        def rows(a, blk, part=None):
            r = out_refs[a].at[4 * blk[0] + 2 * blk[1] + blk[2]]
            if part is None or halves[a] is None:
                return r
            rest = shards[a].shape[0] - halves[a]
            return r.at[pl.ds(0, halves[a])] if part == 0 else r.at[pl.ds(halves[a], rest)]

        def copy(a, k, blk, to, part=None, src=None):
            dst = rows(a, blk, part)
            return pltpu.make_async_remote_copy(
                src_ref=dst if src is None else src, dst_ref=dst,
                send_sem=send_sems.at[nc * a + k], recv_sem=recv_sems.at[nc * a + k],
                device_id=to, device_id_type=MESH)

        mine = [pltpu.make_async_copy(x_refs[a], rows(a, me), local_sems.at[a]) for a in range(na)]
        for cp in mine:
            cp.start()
        started = []
        for a in range(na):
            started += [copy(a, 0, me, px, src=x_refs[a]), copy(a, 1, me, py, src=x_refs[a]),
                        copy(a, 2, me, sib, src=x_refs[a])]
        for cp in started:
            cp.start()

        def forward(cp):
            cp.start()
            started.append(cp)

        for a in range(na):
            copy(a, 0, px, me).wait_recv()
            forward(copy(a, 3, px, py, part=0))
            forward(copy(a, 4, px, sib))
        for a in range(na):
            copy(a, 1, py, me).wait_recv()
            if halves[a] is not None:
                forward(copy(a, 5, py, px, part=1))
            forward(copy(a, 6, py, sib))
        for a in range(na):
            copy(a, 3, pd, me, part=0).wait_recv()
            forward(copy(a, 7, pd, sib, part=0))
            if halves[a] is not None:
                copy(a, 5, pd, me, part=1).wait_recv()
                forward(copy(a, 8, pd, sib, part=1))
        for a in range(na):
            copy(a, 2, sib, me).wait_recv()
            copy(a, 4, other_core(px), me).wait_recv()
            copy(a, 6, other_core(py), me).wait_recv()
            copy(a, 7, other_core(pd), me, part=0).wait_recv()
            if halves[a] is not None:
                copy(a, 8, other_core(pd), me, part=1).wait_recv()
        for cp in started:
            cp.wait_send()
        for cp in mine:
            cp.wait()

    any_spec = pl.BlockSpec(memory_space=pl.ANY)
    return pl.pallas_call(
        body, name=name,
        out_shape=[jax.ShapeDtypeStruct((NDEV,) + s.shape, s.dtype) for s in shards],
        in_specs=[any_spec] * na, out_specs=[any_spec] * na,
        scratch_shapes=[pltpu.SemaphoreType.DMA((nc * na,)), pltpu.SemaphoreType.DMA((nc * na,)),
                        pltpu.SemaphoreType.DMA((na,))],
    )(*shards)


_HBM = pl.BlockSpec(memory_space=pltpu.HBM)
_SEM = pl.BlockSpec(memory_space=pltpu.SEMAPHORE)
_EFFECT = pltpu.SideEffectType.DATAFLOW_SIDE_EFFECTING


def _split_copy(a, k, src_refs, land_refs, send_sems, recv_sems, gather, receive_side):
    x, y, c = lax.axis_index("x"), lax.axis_index("y"), lax.axis_index("c")
    me = 4 * x + 2 * y + c
    tx, ty, tc = x ^ ((k + 1) >> 2 & 1), y ^ ((k + 1) >> 1 & 1), c ^ ((k + 1) & 1)
    peer = 4 * tx + 2 * ty + tc
    return pltpu.make_async_remote_copy(
        src_ref=src_refs[a] if gather else src_refs[a].at[peer],
        dst_ref=land_refs[a].at[peer if receive_side else me],
        send_sem=send_sems.at[7 * a + k], recv_sem=recv_sems.at[7 * a + k],
        device_id=(tx, ty, tc), device_id_type=MESH)


def _own_copy(a, na, src_refs, land_refs, send_sems, gather):
    me = 4 * lax.axis_index("x") + 2 * lax.axis_index("y") + lax.axis_index("c")
    return pltpu.make_async_copy(src_refs[a] if gather else src_refs[a].at[me], land_refs[a].at[me],
                                 send_sems.at[7 * na + a])


def _split_start(srcs, lands, gather, name, after=None):
    na = len(srcs)
    extra = [] if after is None else [after]

    def body(*refs):
        src_refs, land_refs = refs[:na], refs[na:2 * na]
        send_sems, recv_sems = refs[2 * na + len(extra)], refs[2 * na + len(extra) + 1]
        token = refs[-1]
        for k in range(7):
            for a in range(na):
                _split_copy(a, k, src_refs, land_refs, send_sems, recv_sems, gather, False).start()
        for a in range(na):
            _own_copy(a, na, src_refs, land_refs, send_sems, gather).start()
        token[...] = jnp.zeros_like(token)

    hbm = [pltpu.HBM(b.shape, b.dtype) for b in list(srcs) + list(lands)]
    outs = pl.pallas_call(
        body, name=name,
        out_shape=(pltpu.SemaphoreType.DMA((8 * na,)), pltpu.SemaphoreType.DMA((7 * na,)), *hbm,
                   jax.ShapeDtypeStruct((8, LANES), F32)),
        in_specs=[_HBM] * (2 * na) + [pl.BlockSpec(memory_space=pl.ANY)] * len(extra),
        out_specs=(_SEM, _SEM, *[_HBM] * (2 * na), pl.BlockSpec(memory_space=pltpu.VMEM)),
        input_output_aliases={j: 2 + j for j in range(2 * na)},
        compiler_params=pltpu.CompilerParams(has_side_effects=_EFFECT),
    )(*[pltpu.with_memory_space_constraint(b, pltpu.HBM) for b in srcs],
      *[pltpu.with_memory_space_constraint(l, pltpu.HBM) for l in lands], *extra)
    return outs[0], outs[1], outs[2:2 + na], outs[2 + na:2 + 2 * na], outs[-1]


def _split_wait(started, gather, after, name):
    send_sems, recv_sems, srcs, lands, _ = started
    na = len(srcs)

    def body(*refs):
        src_refs, land_refs = refs[:na], refs[na:2 * na]
        send_s, recv_s = refs[2 * na], refs[2 * na + 1]
        for k in range(7):
            for a in range(na):
                cp = _split_copy(a, k, src_refs, land_refs, send_s, recv_s, gather, True)
                cp.wait_send()
                cp.wait_recv()
        for a in range(na):
            _own_copy(a, na, src_refs, land_refs, send_s, gather).wait()

    hbm = [pltpu.HBM(b.shape, b.dtype) for b in list(srcs) + list(lands)]
    outs = pl.pallas_call(
        body, name=name, out_shape=tuple(hbm),
        in_specs=[_HBM] * (2 * na) + [_SEM, _SEM, pl.BlockSpec(memory_space=pl.ANY)],
        out_specs=tuple([_HBM] * (2 * na)),
        input_output_aliases={j: j for j in range(2 * na)},
        compiler_params=pltpu.CompilerParams(has_side_effects=_EFFECT),
    )(*srcs, *lands, send_sems, recv_sems, after)
    return outs[na:]


def _seg_rows(size):
    rows = -(-size // PACK_C)
    return -(-rows // 16) * 16


def _pack(arrs, lead):
    parts = []
    for a in arrs:
        lshape = a.shape[:lead]
        f = a.reshape(lshape + (-1,)).astype(F32)
        rows = _seg_rows(f.shape[-1])
        f = jnp.pad(f, [(0, 0)] * lead + [(0, rows * PACK_C - f.shape[-1])])
        parts.append(f.reshape(lshape + (rows, PACK_C)))
    return jnp.concatenate(parts, axis=lead)


def _unpack(buf, shapes):
    lshape = buf.shape[:-2]
    out, r = [], 0
    for shp in shapes:
        size = math.prod(shp)
        rows = _seg_rows(size)
        seg = buf[..., r:r + rows, :].reshape(lshape + (rows * PACK_C,))[..., :size]
        out.append(seg.reshape(lshape + tuple(shp)))
        r += rows
    return out


def _mm(a, b, *, name, out_dtype, ta=False, tb=False, bm=1024, bn=1024, bk=2048, after=None, plus=None):
    if ta:
        kdim, m = a.shape
    else:
        m, kdim = a.shape
    n, k2 = b.shape if tb else b.shape[::-1]
    assert kdim == k2 and not (ta and tb), (a.shape, b.shape)
    bm = _tile(m, bm, LANES if ta else 16)
    bn = _tile(n, bn, LANES)
    bk = _tile(kdim, bk, LANES)
    nk = kdim // bk
    n_after = 0 if after is None else 1
    n_plus = 0 if plus is None else 2

    def kern(a_ref, b_ref, *rest):
        plus_refs = rest[n_after:n_after + n_plus]
        o_ref, scratch = rest[n_after + n_plus], rest[n_after + n_plus + 1:]
        part = (_dot_tn if ta else _dot_nt if tb else _dot)(a_ref[...], b_ref[...])

        def first(p):
            return p + _dot(plus_refs[0][...], plus_refs[1][...]) if plus is not None else p

        if nk == 1:
            o_ref[...] = first(part).astype(o_ref.dtype)
        else:
            acc = scratch[0] if scratch else o_ref
            k = pl.program_id(2)

            @pl.when(k == 0)
            def _():
                acc[...] = first(jnp.zeros(acc.shape, F32))

            acc[...] += part
            if scratch:
                @pl.when(k == nk - 1)
                def _():
                    o_ref[...] = acc[...].astype(o_ref.dtype)

    a_spec = pl.BlockSpec((bk, bm), lambda i, j, k: (k, i)) if ta else pl.BlockSpec((bm, bk), lambda i, j, k: (i, k))
    b_spec = pl.BlockSpec((bn, bk), lambda i, j, k: (j, k)) if tb else pl.BlockSpec((bk, bn), lambda i, j, k: (k, j))
    extra_specs, extra_args = [], []
    if after is not None:
        extra_specs.append(pl.BlockSpec(after.shape, lambda i, j, k: (0, 0)))
        extra_args.append(after)
    if plus is not None:
        kk = plus[0].shape[1]
        extra_specs += [pl.BlockSpec((bm, kk), lambda i, j, k: (i, 0)), pl.BlockSpec((kk, bn), lambda i, j, k: (0, j))]
        extra_args += list(plus)
    return pl.pallas_call(
        kern, name=name, grid=(m // bm, n // bn, nk),
        in_specs=[a_spec, b_spec] + extra_specs,
        out_specs=pl.BlockSpec((bm, bn), lambda i, j, k: (i, j)),
        out_shape=jax.ShapeDtypeStruct((m, n), out_dtype),
        scratch_shapes=[pltpu.VMEM((bm, bn), F32)] if nk > 1 and out_dtype != F32 else [],
        compiler_params=_cparams(("parallel", "parallel", "arbitrary")),
    )(a, b, *extra_args)


def _adam(parts, w_a, m_a, v_a, name):
    rows, cols = w_a.shape
    rb = _tile(rows, max(8, ADAM_BLOCK_ELEMS // cols // 8 * 8), 8)
    bc1 = 1.0 - ADAM_B1 ** ADAM_STEP
    bc2 = 1.0 - ADAM_B2 ** ADAM_STEP

    def adam_kern(p_ref, w_ref, m_ref, v_ref, g_ref, d_ref, nm_ref, nv_ref):
        g = p_ref[0].astype(F32)
        for j in range(1, NDEV):
            g = g + p_ref[j].astype(F32)
        m_new = ADAM_B1 * m_ref[...] + (1.0 - ADAM_B1) * g
        v_new = ADAM_B2 * v_ref[...] + (1.0 - ADAM_B2) * (g * g)
        g_ref[...] = g
        nm_ref[...] = m_new
        nv_ref[...] = v_new
        d_ref[...] = -ADAM_LR * ((m_new / bc1) / (jnp.sqrt(v_new / bc2) + ADAM_EPS) + ADAM_WD * w_ref[...])

    blk = pl.BlockSpec((rb, cols), lambda i: (i, 0))
    return pl.pallas_call(
        adam_kern, name=name, grid=(rows // rb,),
        in_specs=[pl.BlockSpec((NDEV, rb, cols), lambda i: (0, i, 0)), blk, blk, blk],
        out_specs=[blk] * 4, out_shape=[jax.ShapeDtypeStruct((rows, cols), F32)] * 4,
        compiler_params=_cparams(("parallel",)),
    )(parts, w_a, m_a, v_a)


class _Cfg:
    pass


def _config(x, conv_w, w_uq, w_ukv, mla_qn_nope_g, mla_qn_rope_g, mem, mem_qn_g, w_mem_out, w_mla_out):
    c = _Cfg()
    c.N, c.D = x.shape[1], x.shape[2]
    c.CW = conv_w.shape[2] * NDEV
    c.QL, c.KVL = w_uq.shape[1], w_ukv.shape[1]
    c.NOPE, c.ROPE = mla_qn_nope_g.shape[1], mla_qn_rope_g.shape[1]
    c.H = w_uq.shape[2] * NDEV // (c.NOPE + c.ROPE)
    c.V = w_ukv.shape[2] * NDEV // c.H - c.NOPE
    assert c.NOPE == LANES and c.V == LANES and c.ROPE == LANES // 2
    c.HW = 2 * LANES
    c.HV = c.H * c.V
    assert w_mla_out.shape[1] * NDEV == c.HV
    c.M = mem.shape[1]
    c.MHD = mem_qn_g.shape[1]
    c.MW = w_mem_out.shape[1]
    c.MH = c.MW // c.MHD
    c.o_conv = 0
    c.o_mz = 4 * c.CW
    c.o_g = c.o_mz + c.HV
    c.o_mem = c.o_g + 3 * c.D
    c.o_lora = c.o_mem + 2 * c.MW
    c.P = c.o_lora + c.QL + c.KVL
    assert c.o_mz % c.HV == 0 and c.o_g % (3 * c.D) == 0 and c.o_mem % (2 * c.MW) == 0
    assert c.o_lora % (c.QL + c.KVL) == 0 and c.QL % LANES == 0 and c.KVL % LANES == 0
    c.IN = 4 * c.CW + c.QL + c.KVL + c.ROPE + c.HV + 2 * c.MW + 3 * c.D
    c.R = _tile(c.N, 256, 16)
    c.RP = _tile(c.N, 512, 16)
    c.HG = _tile(c.H, 4, 1)
    c.B = _tile(c.N, ATT_BLOCK, CHUNK)
    c.scale = float((c.NOPE + c.ROPE) ** -0.5)
    c.mscale = float(c.MHD ** -0.5)
    return c


def _win_segments(c):
    ref_order = (('conv', 4 * c.CW), ('lora', c.QL + c.KVL), ('kr', c.ROPE), ('mz', c.HV), ('mem', 2 * c.MW), ('g', 3 * c.D))
    mine = {'conv': c.o_conv, 'mz': c.o_mz, 'g': c.o_g, 'mem': c.o_mem, 'lora': c.o_lora, 'kr': 0}
    segs, o = [], 0
    for nm, wd in ref_order:
        segs.append((nm, o, wd, mine[nm]))
        o += wd
    return segs


def _win_split(g_win_t, c):
    n8 = g_win_t.shape[1]

    def rows(a, wd):
        return [g_win_t[j][max(a, j * n8) - j * n8:min(a + wd, (j + 1) * n8) - j * n8]
                for j in range(a // n8, (a + wd - 1) // n8 + 1)]

    segs = {nm: (a, wd) for nm, a, wd, _ in _win_segments(c)}
    main = [p for nm in ('conv', 'mz', 'g', 'mem', 'lora') for p in rows(*segs[nm])]
    kr = jnp.concatenate(rows(*segs['kr']) + [jnp.zeros((LANES - c.ROPE, g_win_t.shape[2]), g_win_t.dtype)], axis=0)
    return jnp.concatenate(main, axis=0), kr


def _win_blocks(g, g_kr, c):
    n8 = c.IN // NDEV
    blocks = []
    for j in range(NDEV):
        lo, hi = j * n8, (j + 1) * n8
        parts = []
        for nm, a, wd, mine in _win_segments(c):
            s, e = max(a, lo), min(a + wd, hi)
            if s < e:
                parts.append((g_kr if nm == 'kr' else g)[:, mine + s - a:mine + e - a])
        blocks.append(jnp.concatenate(parts, axis=1))
    return jnp.stack(blocks, axis=0)


def kernel(x, positions, mem, norm_g, w_in, conv_w, w_conv_out, mla_q_norm_g, w_uq, mla_kv_norm_g, w_ukv, mla_qn_nope_g, mla_qn_rope_g, mla_kn_nope_g, mla_kn_rope_g, w_mla_out, mem_norm_g, w_mem_kv, mem_qn_g, mem_kn_g, w_mem_out, w_o, loss_target, m_norm_g, m_w_in, m_conv_w, m_w_conv_out, m_mla_q_norm_g, m_w_uq, m_mla_kv_norm_g, m_w_ukv, m_mla_qn_nope_g, m_mla_qn_rope_g, m_mla_kn_nope_g, m_mla_kn_rope_g, m_w_mla_out, m_mem_norm_g, m_w_mem_kv, m_mem_qn_g, m_mem_kn_g, m_w_mem_out, m_w_o, v_norm_g, v_w_in, v_conv_w, v_w_conv_out, v_mla_q_norm_g, v_w_uq, v_mla_kv_norm_g, v_w_ukv, v_mla_qn_nope_g, v_mla_qn_rope_g, v_mla_kn_nope_g, v_mla_kn_rope_g, v_w_mla_out, v_mem_norm_g, v_w_mem_kv, v_mem_qn_g, v_mem_kn_g, v_w_mem_out, v_w_o):
    args = dict(locals())
    W = {n: args[n] for n in WEIGHTS}
    Mo = {n: args['m_' + n] for n in WEIGHTS}
    Vo = {n: args['v_' + n] for n in WEIGHTS}
    c = _config(x, conv_w, w_uq, w_ukv, mla_qn_nope_g, mla_qn_rope_g, mem, mem_qn_g, w_mem_out, w_mla_out)
    N, D, R, B, H = c.N, c.D, c.R, c.B, c.H
    assert x.shape[0] == 1
    xs = x[0]
    tgt = loss_target[0]
    memx = mem[0]
    me = 4 * lax.axis_index("x") + 2 * lax.axis_index("y") + lax.axis_index("c")
    nr = N // R

    g_win, g_taps = _all_gather([W['w_in'][0].astype(_BF).T, conv_w[0]], "ag_w_in")
    rest = [n for n in BIG if n != 'w_in']
    shards_r = [W[n][0].astype(_BF).T if n in COL_SHARDED else W[n][0].astype(_BF) for n in rest]
    lands_r = [lax.empty((NDEV,) + s.shape, s.dtype) for s in shards_r]
    ag_rest = _split_start(shards_r, lands_r, True, "ag_rest_start", after=g_win)
    win_pT, w_krT = _win_split(g_win, c)
    convw = jnp.transpose(g_taps, (1, 0, 2)).reshape(3, c.CW)
    convw8 = jnp.pad(convw, ((0, 5), (0, 0)))

    def rowb(width, cidx, rows=R):
        return pl.BlockSpec((rows, width), lambda i, _c=cidx: (i, _c))

    R2 = c.RP
    nr2 = N // R2
    RM = _tile(N, 1024, 16)

    def rowb2(width, cidx):
        return rowb(width, cidx, R2)

    def fullb(shape):
        nd = len(shape)
        return pl.BlockSpec(shape, lambda *_: (0,) * nd)

    def pad_lanes(g, w=LANES):
        return jnp.pad(g, ((0, 0), (0, w - g.shape[1])))

    def tabs_of(rows):
        return pl.BlockSpec((3, rows, LANES), lambda i, *_: (0, i, 0))

    half = c.ROPE // 2
    inv_freq = jnp.power(ROPE_THETA, -jnp.arange(half, dtype=F32) / half)
    invf = jnp.concatenate([inv_freq, inv_freq, jnp.zeros((LANES - c.ROPE,), F32)])[None, :]
    pos_col = positions[0].astype(F32).reshape(N, 1)

    def rope_tab_kern(pos_ref, invf_ref, o_ref):
        ang = pos_ref[...] * invf_ref[...]
        co, si = jnp.cos(ang), jnp.sin(ang)
        lane = lax.broadcasted_iota(jnp.int32, ang.shape, 1)
        o_ref[0] = jnp.where(lane < c.ROPE, co, 0.0)
        o_ref[1] = jnp.where(lane < half, -si, 0.0)
        o_ref[2] = jnp.where(jnp.logical_and(lane >= half, lane < c.ROPE), si, 0.0)

    tabs = pl.pallas_call(
        rope_tab_kern, name="rope_tab", grid=(nr,),
        in_specs=[pl.BlockSpec((R, 1), lambda i: (i, 0)), fullb((1, LANES))],
        out_specs=tabs_of(R),
        out_shape=jax.ShapeDtypeStruct((3, N, LANES), F32),
        compiler_params=_cparams(("parallel",)),
    )(pos_col, invf)

    def make_rms_kern():
        def rms_fwd_kern(x_ref, g_ref, o_ref):
            xh, _ = _rms(x_ref[...].astype(F32), x_ref.shape[-1])
            o_ref[...] = (xh * g_ref[...]).astype(o_ref.dtype)
        return rms_fwd_kern

    h = pl.pallas_call(
        make_rms_kern(), name="rms_x", grid=(nr2,),
        in_specs=[rowb2(D, 0), fullb((1, D))], out_specs=rowb2(D, 0),
        out_shape=jax.ShapeDtypeStruct((N, D), _BF), compiler_params=_cparams(("parallel",)),
    )(xs, norm_g)

    proj = _mm(h, win_pT, tb=True, name="mm_proj", out_dtype=_BF, after=ag_rest[4])
    kr_raw = _mm(h, w_krT, tb=True, name="mm_kr", out_dtype=_BF)

    Wf = {n: g.reshape(-1, g.shape[2]) for n, g in zip(rest, _split_wait(ag_rest, True, proj, "ag_rest_wait"))}
    wuqT = Wf['w_uq'].reshape(H, c.NOPE + c.ROPE, c.QL)
    wuq_pT = jnp.pad(wuqT, ((0, 0), (0, c.HW - c.NOPE - c.ROPE), (0, 0))).reshape(H * c.HW, c.QL)
    wukvT = Wf['w_ukv']
    wcoT, wmo, wmkv, wmemoT, wo = Wf['w_conv_out'], Wf['w_mla_out'], Wf['w_mem_kv'], Wf['w_mem_out'], Wf['w_o']

    CW = c.CW
    conv_blk = c.o_conv // (4 * CW)
    HALO = 16
    rh = R // HALO

    def conv_parts(blk):
        blk = blk.astype(F32)
        return blk[:, 0:CW], blk[:, CW:2 * CW], blk[:, 2 * CW:3 * CW], blk[:, 3 * CW:4 * CW]

    def shifted(cu, prev, i):
        prev = jnp.where(i > 0, prev, 0.0)
        rid = lax.broadcasted_iota(jnp.int32, cu.shape, 0)
        last, last2 = prev[HALO - 1:HALO, :], prev[HALO - 2:HALO - 1, :]
        sh1 = jnp.where(rid == 0, last, pltpu.roll(cu, 1, 0))
        sh2 = jnp.where(rid == 0, last2, jnp.where(rid == 1, last, pltpu.roll(cu, 2, 0)))
        return sh1, sh2

    def conv_fwd_kern(p_ref, prev_ref, w_ref, o_ref):
        i = pl.program_id(0)
        cg, bg, u, z = conv_parts(p_ref[...])
        pc, _, pu, _ = conv_parts(prev_ref[...])
        cu = cg * u
        sh1, sh2 = shifted(cu, pc * pu, i)
        w = w_ref[...]
        conv = w[0:1, :] * sh2 + w[1:2, :] * sh1 + w[2:3, :] * cu
        o_ref[...] = (bg * conv * (z * _sig(z))).astype(o_ref.dtype)

    prev_spec = pl.BlockSpec((HALO, 4 * CW), lambda i: (jnp.maximum(i * rh - 1, 0), conv_blk))
    a_conv = pl.pallas_call(
        conv_fwd_kern, name="conv_fwd", grid=(nr,),
        in_specs=[rowb(4 * CW, conv_blk), prev_spec, fullb((8, CW))],
        out_specs=rowb(CW, 0), out_shape=jax.ShapeDtypeStruct((N, CW), _BF),
        compiler_params=_cparams(("parallel",)),
    )(proj, proj, convw8)
    o_conv = _mm(a_conv, wcoT, tb=True, name="mm_oconv", out_dtype=_BF)

    QL, KVL, HW = c.QL, c.KVL, c.HW
    lora_blk = c.o_lora // (QL + KVL)

    def lora_fwd_kern(p_ref, gq_ref, gkv_ref, q_ref, kv_ref):
        blk = p_ref[...].astype(F32)
        qh, _ = _rms(blk[:, :QL], QL)
        kh, _ = _rms(blk[:, QL:], KVL)
        q_ref[...] = (qh * gq_ref[...]).astype(q_ref.dtype)
        kv_ref[...] = (kh * gkv_ref[...]).astype(kv_ref.dtype)

    cqn, ckvn = pl.pallas_call(
        lora_fwd_kern, name="lora_fwd", grid=(nr,),
        in_specs=[rowb(QL + KVL, lora_blk), fullb((1, QL)), fullb((1, KVL))],
        out_specs=[rowb(QL, 0), rowb(KVL, 0)],
        out_shape=[jax.ShapeDtypeStruct((N, QL), _BF), jax.ShapeDtypeStruct((N, KVL), _BF)],
        compiler_params=_cparams(("parallel",)),
    )(proj, mla_q_norm_g, mla_kv_norm_g)
    q_p = _mm(cqn, wuq_pT, tb=True, name="mm_q", out_dtype=_BF, bn=2048)
    kv = _mm(ckvn, wukvT, tb=True, name="mm_kv", out_dtype=_BF, bn=2048)

    g_qn, g_qr = mla_qn_nope_g, pad_lanes(mla_qn_rope_g)
    g_kn, g_kr = mla_kn_nope_g, pad_lanes(mla_kn_rope_g)

    def krope_fwd_kern(p_ref, t_ref, g_ref, o_ref):
        xh, _ = _rms128(p_ref[...].astype(F32), c.ROPE)
        o_ref[...] = _rope(xh * g_ref[...], t_ref[0], t_ref[1], t_ref[2]).astype(o_ref.dtype)

    k_rope = pl.pallas_call(
        krope_fwd_kern, name="krope_fwd", grid=(nr,),
        in_specs=[rowb(LANES, 0), tabs_of(R), fullb((1, LANES))],
        out_specs=rowb(LANES, 0), out_shape=jax.ShapeDtypeStruct((N, LANES), _BF),
        compiler_params=_cparams(("parallel",)),
    )(kr_raw, tabs, g_kr)

    RP, HG = c.RP, c.HG
    nrp, nhg = N // RP, H // HG
    heads_in = pl.BlockSpec((RP, HG * HW), lambda i, hg: (i, hg))
    heads_out = pl.BlockSpec((HG, RP, HW), lambda i, hg: (hg, i, 0))

    def q_prep_kern(q_ref, t_ref, gn_ref, gr_ref, o_ref):
        for g in range(HG):
            blk = q_ref[:, g * HW:(g + 1) * HW].astype(F32)
            nh, _ = _rms128(blk[:, :LANES], c.NOPE)
            rhat, _ = _rms128(blk[:, LANES:], c.ROPE)
            rot = _rope(rhat * gr_ref[...], t_ref[0], t_ref[1], t_ref[2])
            o_ref[g] = (jnp.concatenate([nh * gn_ref[...], rot], axis=1) * (c.scale * LOG2E)).astype(o_ref.dtype)

    q_cat = pl.pallas_call(
        q_prep_kern, name="q_prep", grid=(nrp, nhg),
        in_specs=[heads_in, tabs_of(RP), fullb((1, LANES)), fullb((1, LANES))],
        out_specs=heads_out, out_shape=jax.ShapeDtypeStruct((H, N, HW), _BF),
        compiler_params=_cparams(("parallel", "parallel")),
    )(q_p, tabs, g_qn, g_qr)

    def k_prep_kern(kv_ref, kr_ref, gn_ref, o_ref):
        for g in range(HG):
            kn, _ = _rms128(kv_ref[:, g * HW:g * HW + LANES].astype(F32), c.NOPE)
            o_ref[g] = jnp.concatenate([(kn * gn_ref[...]).astype(o_ref.dtype), kr_ref[...]], axis=1)

    k_cat = pl.pallas_call(
        k_prep_kern, name="k_prep", grid=(nrp, nhg),
        in_specs=[heads_in, pl.BlockSpec((RP, LANES), lambda i, hg: (i, 0)), fullb((1, LANES))],
        out_specs=heads_out, out_shape=jax.ShapeDtypeStruct((H, N, HW), _BF),
        compiler_params=_cparams(("parallel", "parallel")),
    )(kv, k_rope, g_kn)

    QB = ATT_QBLOCKS if N % (ATT_QBLOCKS * B) == 0 else 1
    BQ = QB * B
    nq = N // BQ
    assert CHUNK & (CHUNK - 1) == 0 and B % CHUNK == 0 and ATT_UNROLL_FWD % QB == 0 and ATT_UNROLL_BWD % QB == 0

    def live_rows(diag):
        return slice(0 if diag is None else diag * B, BQ)

    def diag_mask(s):
        row = lax.broadcasted_iota(jnp.int32, s.shape, 0)
        col = lax.broadcasted_iota(jnp.int32, s.shape, 1)
        shift = CHUNK.bit_length() - 1
        allowed = jnp.right_shift(col, shift) <= jnp.right_shift(row, shift)
        return jnp.where(allowed, s, NEG)

    k_head = pl.BlockSpec((1, N, HW), lambda hh, i: (hh, 0, 0))
    v_head = pl.BlockSpec((N, LANES), lambda hh, i: (0, 2 * hh + 1))
    q_blk = pl.BlockSpec((1, BQ, HW), lambda hh, i: (hh, i, 0))
    o_blk = pl.BlockSpec((BQ, LANES), lambda hh, i: (i, hh))
    lse_blk = pl.BlockSpec((1, BQ, LANES), lambda hh, i: (hh, i, 0))

    def key_block_plan(i, run, unroll):
        def unrolled(u, carry):
            run(unroll * u, unroll, 0)
            return carry

        n_full = QB * i
        lax.fori_loop(0, n_full // unroll, unrolled, 0)
        for rem in range(0, unroll, QB):
            @pl.when(n_full % unroll == rem)
            def _(rem=rem):
                run(n_full - rem, rem + QB, QB)

    def attn_fwd_kern(q_ref, k_ref, v_ref, o_ref, lse_ref, m_sc, acc_sc, s_sc):
        i = pl.program_id(1)
        m_sc[...] = jnp.full(m_sc.shape, NEG, F32)
        acc_sc[...] = jnp.zeros(acc_sc.shape, F32)

        def rows_of(t):
            return pl.ds(pl.multiple_of(t * B, B), B)

        def scores(t, slot, diag):
            rs = live_rows(diag)
            s_sc[slot, rs, :] = _dot_nt(q_ref[0, rs, :], k_ref[0, rows_of(t), :])

        def softmax_pv(t, slot, diag):
            rs = live_rows(diag)
            s = s_sc[slot, rs, :]
            if diag is not None:
                s = diag_mask(s)
            mt = s[:, 0:LANES]
            for cb in range(1, B // LANES):
                mt = jnp.maximum(mt, s[:, cb * LANES:(cb + 1) * LANES])
            m_prev = m_sc[rs, :]
            m_new = jnp.maximum(m_prev, jnp.max(mt, axis=1, keepdims=True))
            alpha = jnp.exp2(m_prev - m_new)
            p = jnp.concatenate([jnp.exp2(s[:, cb * LANES:(cb + 1) * LANES] - m_new).astype(_BF)
                                 for cb in range(B // LANES)], axis=1)
            v_ones = jnp.concatenate([v_ref[rows_of(t), :], jnp.ones((B, LANES), _BF)], axis=1)
            acc_sc[rs, :] = jnp.concatenate([alpha, alpha], axis=1) * acc_sc[rs, :] + _dot(p, v_ones)
            m_sc[rs, :] = m_new

        scores(0, 0, None)

        def run(first, count, n_diag):
            def diag_of(u):
                return u - (count - n_diag) if count - n_diag <= u < count else None

            for u in range(count):
                if u + 1 < count or n_diag == 0:
                    scores(first + u + 1, (u + 1) % 2, diag_of(u + 1))
                softmax_pv(first + u, u % 2, diag_of(u))

        key_block_plan(i, run, ATT_UNROLL_FWD)
        acc = acc_sc[...]
        o_ref[...] = (acc[:, :LANES] / acc[:, LANES:]).astype(o_ref.dtype)
        lse_ref[0] = m_sc[...] + jnp.log2(acc[:, LANES:])

    mla_y, lse = pl.pallas_call(
        attn_fwd_kern, name="attn_fwd", grid=(H, nq),
        in_specs=[q_blk, k_head, v_head], out_specs=[o_blk, lse_blk],
        out_shape=[jax.ShapeDtypeStruct((N, c.HV), _BF), jax.ShapeDtypeStruct((H, N, LANES), F32)],
        scratch_shapes=[pltpu.VMEM((BQ, LANES), F32), pltpu.VMEM((BQ, HW), F32), pltpu.VMEM((2, BQ, B), F32)],
        compiler_params=_cparams(("parallel", "arbitrary")),
    )(q_cat, k_cat, kv)

    HV = c.HV
    mz_blk = c.o_mz // HV

    def gate_fwd_kern(y_ref, z_ref, o_ref):
        z = z_ref[...].astype(F32)
        o_ref[...] = (y_ref[...].astype(F32) * (z * _sig(z))).astype(o_ref.dtype)

    a_mla = pl.pallas_call(
        gate_fwd_kern, name="gate_mla", grid=(nr2,),
        in_specs=[rowb2(HV, 0), rowb2(HV, mz_blk)], out_specs=rowb2(HV, 0),
        out_shape=jax.ShapeDtypeStruct((N, HV), _BF), compiler_params=_cparams(("parallel",)),
    )(mla_y, proj)
    o_mla = _mm(a_mla, wmo, name="mm_omla", out_dtype=_BF)

    M, MW, MH, MHD = c.M, c.MW, c.MH, c.MHD
    memn = pl.pallas_call(
        make_rms_kern(), name="rms_mem",
        grid=(1,), in_specs=[fullb((M, D)), fullb((1, D))], out_specs=fullb((M, D)),
        out_shape=jax.ShapeDtypeStruct((M, D), _BF), compiler_params=_cparams(("arbitrary",)),
    )(memx, mem_norm_g)
    kvm = _mm(memn, wmkv, name="mm_memkv", out_dtype=F32)

    def memk_fwd_kern(kv_ref, g_ref, k_ref, v_ref):
        for hh in range(MH):
            kh, _ = _rms(kv_ref[:, hh * MHD:(hh + 1) * MHD], MHD)
            k_ref[:, hh * MHD:(hh + 1) * MHD] = (kh * g_ref[...]).astype(k_ref.dtype)
        v_ref[...] = kv_ref[:, MW:].astype(v_ref.dtype)

    mem_k, mem_v = pl.pallas_call(
        memk_fwd_kern, name="memk_fwd", grid=(1,),
        in_specs=[fullb((M, 2 * MW)), fullb((1, MHD))], out_specs=[fullb((M, MW)), fullb((M, MW))],
        out_shape=[jax.ShapeDtypeStruct((M, MW), _BF)] * 2, compiler_params=_cparams(("arbitrary",)),
    )(kvm, mem_kn_g)

    mem_blk = c.o_mem // (2 * MW)

    def mem_head(qz_ref, k_ref, v_ref, g_ref, hh):
        sl = slice(hh * MHD, (hh + 1) * MHD)
        qh, r = _rms(qz_ref[:, sl].astype(F32), MHD)
        qn = (qh * g_ref[...]).astype(_BF)
        s = _dot_nt(qn, k_ref[:, sl]) * c.mscale
        e = jnp.exp(s - jnp.max(s, axis=1, keepdims=True))
        p = e / jnp.sum(e, axis=1, keepdims=True)
        y = _dot(p.astype(_BF), v_ref[:, sl])
        z = qz_ref[:, MW + hh * MHD:MW + (hh + 1) * MHD].astype(F32)
        return sl, qh, r, qn, p, y, z

    def mem_fwd_kern(qz_ref, k_ref, v_ref, g_ref, o_ref):
        for hh in range(MH):
            sl, _, _, _, _, y, z = mem_head(qz_ref, k_ref, v_ref, g_ref, hh)
            o_ref[:, sl] = (y * (z * _sig(z))).astype(o_ref.dtype)

    a_mem = pl.pallas_call(
        mem_fwd_kern, name="mem_fwd", grid=(N // RM,),
        in_specs=[rowb(2 * MW, mem_blk, RM), fullb((M, MW)), fullb((M, MW)), fullb((1, MHD))],
        out_specs=rowb(MW, 0, RM), out_shape=jax.ShapeDtypeStruct((N, MW), _BF),
        compiler_params=_cparams(("parallel",)),
    )(proj, mem_k, mem_v, mem_qn_g)
    o_mem = _mm(a_mem, wmemoT, tb=True, name="mm_omem", out_dtype=_BF)

    g_blk = c.o_g // (3 * D)

    def merge_fwd_kern(g_ref, oc_ref, om_ref, ome_ref, o_ref):
        g = g_ref[...].astype(F32)
        acc = _sig(g[:, :D]) * oc_ref[...].astype(F32)
        acc += _sig(g[:, D:2 * D]) * om_ref[...].astype(F32)
        acc += _sig(g[:, 2 * D:]) * ome_ref[...].astype(F32)
        o_ref[...] = acc.astype(o_ref.dtype)

    merged = pl.pallas_call(
        merge_fwd_kern, name="merge_fwd", grid=(nr,),
        in_specs=[rowb(3 * D, g_blk), rowb(D, 0), rowb(D, 0), rowb(D, 0)], out_specs=rowb(D, 0),
        out_shape=jax.ShapeDtypeStruct((N, D), _BF), compiler_params=_cparams(("parallel",)),
    )(proj, o_conv, o_mla, o_mem)
    obm, obn = _tile(N, 512, 16), _tile(D, 1024, LANES)

    def out_loss_kern(a_ref, w_ref, x_ref, t_ref, dyb_ref, l_ref):
        e = x_ref[...] + _dot(a_ref[...], w_ref[...]) - t_ref[...]
        dyb_ref[...] = (e * (1.0 / D)).astype(dyb_ref.dtype)
        row = lax.broadcasted_iota(jnp.int32, l_ref.shape, 0)
        l_ref[...] = jnp.where(row == 0, jnp.sum(e * e, axis=0, keepdims=True), 0.0)

    oblk = pl.BlockSpec((obm, obn), lambda i, j: (i, j))
    dyb, lpart = pl.pallas_call(
        out_loss_kern, name="mm_out_loss", grid=(N // obm, D // obn),
        in_specs=[pl.BlockSpec((obm, D), lambda i, j: (i, 0)), pl.BlockSpec((D, obn), lambda i, j: (0, j)), oblk, oblk],
        out_specs=[oblk, pl.BlockSpec((8, obn), lambda i, j: (i, j))],
        out_shape=[jax.ShapeDtypeStruct((N, D), _BF), jax.ShapeDtypeStruct((8 * (N // obm), D), F32)],
        compiler_params=_cparams(("parallel", "parallel")),
    )(merged, wo, xs, tgt)
    loss = lax.psum(jnp.sum(lpart) * (0.5 / D), AXES)

    G = {}
    d_merged = _mm(dyb, wo, tb=True, name="mm_dmerged", out_dtype=_BF)
    G['w_o'] = _mm(merged, dyb, ta=True, name="mm_dwo", out_dtype=_BF, bk=4096)

    dproj0 = lax.empty((N, c.P), _BF)
    any_spec = pl.BlockSpec(memory_space=pl.ANY)

    def merge_bwd_kern(dp_any, g_ref, dm_ref, oc_ref, om_ref, ome_ref, dg_ref, doc_ref, dom_ref, dome_ref):
        g = g_ref[...].astype(F32)
        dm = dm_ref[...].astype(F32)
        for idx, (o_in, d_out) in enumerate(((oc_ref, doc_ref), (om_ref, dom_ref), (ome_ref, dome_ref))):
            sg = _sig(g[:, idx * D:(idx + 1) * D])
            d_out[...] = (sg * dm).astype(d_out.dtype)
            dg_ref[:, idx * D:(idx + 1) * D] = (dm * o_in[...].astype(F32) * sg * (1.0 - sg)).astype(dg_ref.dtype)

    dproj1, d_oconv, d_omla, d_omem = pl.pallas_call(
        merge_bwd_kern, name="merge_bwd", grid=(nr,),
        in_specs=[any_spec, rowb(3 * D, g_blk), rowb(D, 0), rowb(D, 0), rowb(D, 0), rowb(D, 0)],
        out_specs=[rowb(3 * D, g_blk), rowb(D, 0), rowb(D, 0), rowb(D, 0)],
        out_shape=[jax.ShapeDtypeStruct((N, c.P), _BF)] + [jax.ShapeDtypeStruct((N, D), _BF)] * 3,
        input_output_aliases={0: 0}, compiler_params=_cparams(("parallel",)),
    )(dproj0, proj, d_merged, o_conv, o_mla, o_mem)

    G['w_conv_out'] = _mm(a_conv, d_oconv, ta=True, name="mm_dwco", out_dtype=_BF, bk=4096)
    d_aconv = _mm(d_oconv, wcoT, name="mm_daconv", out_dtype=_BF)
    G['w_mla_out'] = _mm(a_mla, d_omla, ta=True, name="mm_dwmo", out_dtype=_BF, bk=4096)
    d_amla = _mm(d_omla, wmo, tb=True, name="mm_damla", out_dtype=_BF)
    G['w_mem_out'] = _mm(a_mem, d_omem, ta=True, name="mm_dwmemo", out_dtype=_BF, bk=4096)
    d_amem = _mm(d_omem, wmemoT, name="mm_damem", out_dtype=_BF)

    def conv_bwd_kern(dp_any, p_ref, prev_ref, next_ref, da_ref, dan_ref, w_ref, o_ref, dw_ref):
        i = pl.program_id(0)
        cg, bg, u, z = conv_parts(p_ref[...])
        pc, _, pu, _ = conv_parts(prev_ref[...])
        _, nbg, _, nz = conv_parts(next_ref[...])
        cu = cg * u
        sh1, sh2 = shifted(cu, pc * pu, i)
        w = w_ref[...]
        conv = w[0:1, :] * sh2 + w[1:2, :] * sh1 + w[2:3, :] * cu
        sg = _sig(z)
        sz = z * sg
        da = da_ref[...].astype(F32)
        dcy = da * sz
        d_z = da * (bg * conv) * (sg * (1.0 + z * (1.0 - sg)))
        d_b = dcy * conv
        dconv = dcy * bg
        dnext = dan_ref[...].astype(F32) * (nz * _sig(nz)) * nbg
        dnext = jnp.where(i < nr - 1, dnext, 0.0)
        rid = lax.broadcasted_iota(jnp.int32, cu.shape, 0)
        up1 = jnp.where(rid == R - 1, dnext[0:1, :], pltpu.roll(dconv, R - 1, 0))
        up2 = jnp.where(rid == R - 2, dnext[0:1, :], jnp.where(rid == R - 1, dnext[1:2, :], pltpu.roll(dconv, R - 2, 0)))
        dcu = w[2:3, :] * dconv + w[1:2, :] * up1 + w[0:1, :] * up2
        o_ref[:, 0:CW] = (dcu * u).astype(o_ref.dtype)
        o_ref[:, CW:2 * CW] = d_b.astype(o_ref.dtype)
        o_ref[:, 2 * CW:3 * CW] = (dcu * cg).astype(o_ref.dtype)
        o_ref[:, 3 * CW:4 * CW] = d_z.astype(o_ref.dtype)

        @pl.when(i == 0)
        def _():
            dw_ref[...] = jnp.zeros(dw_ref.shape, F32)

        dw_ref[0:1, :] += jnp.sum(dconv * sh2, axis=0, keepdims=True)
        dw_ref[1:2, :] += jnp.sum(dconv * sh1, axis=0, keepdims=True)
        dw_ref[2:3, :] += jnp.sum(dconv * cu, axis=0, keepdims=True)

    next_spec = pl.BlockSpec((HALO, 4 * CW), lambda i: (jnp.minimum((i + 1) * rh, N // HALO - 1), conv_blk))
    dan_spec = pl.BlockSpec((HALO, CW), lambda i: (jnp.minimum((i + 1) * rh, N // HALO - 1), 0))
    dproj2, g_convw = pl.pallas_call(
        conv_bwd_kern, name="conv_bwd", grid=(nr,),
        in_specs=[any_spec, rowb(4 * CW, conv_blk), prev_spec, next_spec, rowb(CW, 0), dan_spec, fullb((8, CW))],
        out_specs=[rowb(4 * CW, conv_blk), fullb((8, CW))],
        out_shape=[jax.ShapeDtypeStruct((N, c.P), _BF), jax.ShapeDtypeStruct((8, CW), F32)],
        input_output_aliases={0: 0}, compiler_params=_cparams(("arbitrary",)),
    )(dproj1, proj, proj, proj, d_aconv, d_aconv, convw8)

    def mem_bwd_kern(dp_any, qz_ref, da_ref, k_ref, v_ref, g_ref, o_ref, dk_ref, dv_ref, dg_ref):
        @pl.when(pl.program_id(0) == 0)
        def _():
            dk_ref[...] = jnp.zeros(dk_ref.shape, F32)
            dv_ref[...] = jnp.zeros(dv_ref.shape, F32)
            dg_ref[...] = jnp.zeros(dg_ref.shape, F32)

        for hh in range(MH):
            sl, qh, r, qn, p, y, z = mem_head(qz_ref, k_ref, v_ref, g_ref, hh)
            da = da_ref[:, sl].astype(F32)
            sg = _sig(z)
            dyh = da * (z * sg)
            o_ref[:, MW + hh * MHD:MW + (hh + 1) * MHD] = (da * y * (sg * (1.0 + z * (1.0 - sg)))).astype(o_ref.dtype)
            dyb_h = dyh.astype(_BF)
            dpm = _dot_nt(dyb_h, v_ref[:, sl])
            ds = (p * (dpm - jnp.sum(dpm * p, axis=1, keepdims=True)) * c.mscale).astype(_BF)
            dqn = _dot(ds, k_ref[:, sl])
            dk_ref[:, sl] += _dot_tn(ds, qn)
            dv_ref[:, sl] += _dot_tn(p.astype(_BF), dyb_h)
            dq, dgp = _rms_bwd(qh, r, g_ref[...], dqn, MHD)
            o_ref[:, sl] = dq.astype(o_ref.dtype)
            dg_ref[...] += jnp.sum(dgp, axis=0, keepdims=True)

    dproj3, d_memk, d_memv, g_mem_qn = pl.pallas_call(
        mem_bwd_kern, name="mem_bwd", grid=(N // RM,),
        in_specs=[any_spec, rowb(2 * MW, mem_blk, RM), rowb(MW, 0, RM), fullb((M, MW)), fullb((M, MW)), fullb((1, MHD))],
        out_specs=[rowb(2 * MW, mem_blk, RM), fullb((M, MW)), fullb((M, MW)), fullb((1, MHD))],
        out_shape=[jax.ShapeDtypeStruct((N, c.P), _BF), jax.ShapeDtypeStruct((M, MW), F32),
                   jax.ShapeDtypeStruct((M, MW), F32), jax.ShapeDtypeStruct((1, MHD), F32)],
        input_output_aliases={0: 0}, compiler_params=_cparams(("arbitrary",)),
    )(dproj2, proj, d_amem, mem_k, mem_v, mem_qn_g)

    def memk_bwd_kern(kv_ref, dk_ref, dv_ref, g_ref, o_ref, dg_ref):
        dg = jnp.zeros((1, MHD), F32)
        for hh in range(MH):
            sl = slice(hh * MHD, (hh + 1) * MHD)
            kh, r = _rms(kv_ref[:, sl], MHD)
            dkr, dgp = _rms_bwd(kh, r, g_ref[...], dk_ref[:, sl], MHD)
            o_ref[:, sl] = dkr.astype(o_ref.dtype)
            dg += jnp.sum(dgp, axis=0, keepdims=True)
        o_ref[:, MW:] = dv_ref[...].astype(o_ref.dtype)
        dg_ref[...] = dg

    d_kvm, g_mem_kn = pl.pallas_call(
        memk_bwd_kern, name="memk_bwd", grid=(1,),
        in_specs=[fullb((M, 2 * MW)), fullb((M, MW)), fullb((M, MW)), fullb((1, MHD))],
        out_specs=[fullb((M, 2 * MW)), fullb((1, MHD))],
        out_shape=[jax.ShapeDtypeStruct((M, 2 * MW), _BF), jax.ShapeDtypeStruct((1, MHD), F32)],
        compiler_params=_cparams(("arbitrary",)),
    )(kvm, d_memk, d_memv, mem_kn_g)
    G['w_mem_kv'] = _mm(memn, d_kvm, ta=True, name="mm_dwmkv", out_dtype=_BF)
    d_memn = _mm(d_kvm, wmkv, tb=True, name="mm_dmemn", out_dtype=F32)

    def memnorm_bwd_kern(x_ref, d_ref, dg_ref):
        xh, _ = _rms(x_ref[...], D)
        dg_ref[...] = jnp.sum(d_ref[...] * xh, axis=0, keepdims=True)

    g_mem_norm = pl.pallas_call(
        memnorm_bwd_kern, name="memnorm_bwd", grid=(1,),
        in_specs=[fullb((M, D)), fullb((M, D))], out_specs=fullb((1, D)),
        out_shape=jax.ShapeDtypeStruct((1, D), F32), compiler_params=_cparams(("arbitrary",)),
    )(memx, d_memn)

    def gate_bwd_kern(dp_any, da_ref, y_ref, z_ref, dy_ref, dz_ref):
        z = z_ref[...].astype(F32)
        da = da_ref[...].astype(F32)
        sg = _sig(z)
        dy_ref[...] = (da * (z * sg)).astype(dy_ref.dtype)
        dz_ref[...] = (da * y_ref[...].astype(F32) * (sg * (1.0 + z * (1.0 - sg)))).astype(dz_ref.dtype)

    d_mlay, dproj4 = pl.pallas_call(
        gate_bwd_kern, name="gate_mla_bwd", grid=(nr2,),
        in_specs=[any_spec, rowb2(HV, 0), rowb2(HV, 0), rowb2(HV, mz_blk)],
        out_specs=[rowb2(HV, 0), rowb2(HV, mz_blk)],
        out_shape=[jax.ShapeDtypeStruct((N, HV), _BF), jax.ShapeDtypeStruct((N, c.P), _BF)],
        input_output_aliases={0: 1}, compiler_params=_cparams(("parallel",)),
    )(dproj3, d_amla, mla_y, proj)

    def attn_bwd_kern(q_ref, k_ref, v_ref, o_ref, do_ref, lse_ref, dq_ref, dk_ref, dv_ref, dq_sc, dl_sc, dk_sc, dv_sc):
        i = pl.program_id(1)
        delta = jnp.sum(do_ref[...].astype(F32) * o_ref[...].astype(F32), axis=1, keepdims=True)
        dl_sc[...] = jnp.broadcast_to(delta, dl_sc.shape)
        dq_sc[...] = jnp.zeros(dq_sc.shape, F32)

        def step(t, diag):
            rows = pl.ds(pl.multiple_of(t * B, B), B)
            rs = live_rows(diag)
            q, do = q_ref[0, rs, :], do_ref[rs, :]
            k = k_ref[0, rows, :]
            s = _dot_nt(q, k)
            if diag is not None:
                s = diag_mask(s)
            dpm = _dot_nt(do, v_ref[rows, :])
            lse_t, dl = lse_ref[0, rs, :], dl_sc[rs, :]
            ps, dss = [], []
            for cb in range(B // LANES):
                cols = slice(cb * LANES, (cb + 1) * LANES)
                p_cb = jnp.exp2(s[:, cols] - lse_t)
                ps.append(p_cb.astype(_BF))
                dss.append((p_cb * (dpm[:, cols] - dl)).astype(_BF))
            p, ds = jnp.concatenate(ps, axis=1), jnp.concatenate(dss, axis=1)
            dvp = _dot_tn(p, do)
            dkp = _dot_tn(ds, q)
            if diag is not None:
                dk_sc[rows, :] = dkp
                dv_sc[rows, :] = dvp
            else:
                dk_sc[rows, :] += dkp
                dv_sc[rows, :] += dvp
            dq_sc[rs, :] += _dot(ds, k)

        def run(first, count, n_diag):
            for u in range(count):
                step(first + u, u - (count - n_diag) if u >= count - n_diag else None)

        key_block_plan(i, run, ATT_UNROLL_BWD)
        dq_ref[0] = dq_sc[...].astype(dq_ref.dtype)

        @pl.when(i == nq - 1)
        def _():
            dk_ref[0] = dk_sc[...].astype(dk_ref.dtype)
            dv_ref[0] = dv_sc[...].astype(dv_ref.dtype)

    d_qcat, d_kcat, d_v = pl.pallas_call(
        attn_bwd_kern, name="attn_bwd", grid=(H, nq),
        in_specs=[q_blk, k_head, v_head, o_blk, o_blk, lse_blk],
        out_specs=[pl.BlockSpec((1, BQ, HW), lambda hh, i: (hh, i, 0)),
                   pl.BlockSpec((1, N, HW), lambda hh, i: (hh, 0, 0)),
                   pl.BlockSpec((1, N, LANES), lambda hh, i: (hh, 0, 0))],
        out_shape=[jax.ShapeDtypeStruct((H, N, HW), _BF), jax.ShapeDtypeStruct((H, N, HW), _BF),
                   jax.ShapeDtypeStruct((H, N, LANES), _BF)],
        scratch_shapes=[pltpu.VMEM((BQ, HW), F32), pltpu.VMEM((BQ, LANES), F32), pltpu.VMEM((N, HW), F32),
                        pltpu.VMEM((N, LANES), F32)],
        compiler_params=_cparams(("parallel", "arbitrary")),
    )(q_cat, k_cat, kv, mla_y, d_mlay, lse)

    def q_prep_bwd_kern(q_ref, dq_ref, t_ref, gn_ref, gr_ref, o_ref, dgn_ref, dgr_ref):
        @pl.when(jnp.logical_and(pl.program_id(0) == 0, pl.program_id(1) == 0))
        def _():
            dgn_ref[...] = jnp.zeros(dgn_ref.shape, F32)
            dgr_ref[...] = jnp.zeros(dgr_ref.shape, F32)

        for g in range(HG):
            blk = q_ref[:, g * HW:(g + 1) * HW].astype(F32)
            d = dq_ref[g].astype(F32) * c.scale
            nh, rn = _rms128(blk[:, :LANES], c.NOPE)
            rhat, rr = _rms128(blk[:, LANES:], c.ROPE)
            dn, dgn = _rms128_bwd(nh, rn, gn_ref[...], d[:, :LANES], c.NOPE)
            drot = _rope_t(d[:, LANES:], t_ref[0], t_ref[1], t_ref[2])
            dr, dgr = _rms128_bwd(rhat, rr, gr_ref[...], drot, c.ROPE)
            o_ref[:, g * HW:(g + 1) * HW] = jnp.concatenate([dn, dr], axis=1).astype(o_ref.dtype)
            dgn_ref[...] += jnp.sum(dgn, axis=0, keepdims=True)
            dgr_ref[...] += jnp.sum(dgr, axis=0, keepdims=True)

    d_qp, g_qn_nope, g_qn_rope = pl.pallas_call(
        q_prep_bwd_kern, name="q_prep_bwd", grid=(nrp, nhg),
        in_specs=[heads_in, heads_out, tabs_of(RP), fullb((1, LANES)), fullb((1, LANES))],
        out_specs=[heads_in, fullb((1, LANES)), fullb((1, LANES))],
        out_shape=[jax.ShapeDtypeStruct((N, H * HW), _BF), jax.ShapeDtypeStruct((1, LANES), F32),
                   jax.ShapeDtypeStruct((1, LANES), F32)],
        compiler_params=_cparams(("arbitrary", "arbitrary")),
    )(q_p, d_qcat, tabs, g_qn, g_qr)

    def k_prep_bwd_kern(kv_ref, dk_ref, dv_ref, gn_ref, o_ref, dkr_ref, dgn_ref):
        hg = pl.program_id(1)

        @pl.when(jnp.logical_and(pl.program_id(0) == 0, hg == 0))
        def _():
            dgn_ref[...] = jnp.zeros(dgn_ref.shape, F32)

        @pl.when(hg == 0)
        def _():
            dkr_ref[...] = jnp.zeros(dkr_ref.shape, F32)

        dkr = jnp.zeros((RP, LANES), F32)
        for g in range(HG):
            dk = dk_ref[g].astype(F32) * (1.0 / LOG2E)
            kn, r = _rms128(kv_ref[:, g * HW:g * HW + LANES].astype(F32), c.NOPE)
            dkn, dgn = _rms128_bwd(kn, r, gn_ref[...], dk[:, :LANES], c.NOPE)
            o_ref[:, g * HW:(g + 1) * HW] = jnp.concatenate([dkn.astype(o_ref.dtype), dv_ref[g]], axis=1)
            dgn_ref[...] += jnp.sum(dgn, axis=0, keepdims=True)
            dkr += dk[:, LANES:]
        dkr_ref[...] += dkr

    d_kv, d_krsum, g_kn_nope = pl.pallas_call(
        k_prep_bwd_kern, name="k_prep_bwd", grid=(nrp, nhg),
        in_specs=[heads_in, heads_out, pl.BlockSpec((HG, RP, LANES), lambda i, hg: (hg, i, 0)), fullb((1, LANES))],
        out_specs=[heads_in, pl.BlockSpec((RP, LANES), lambda i, hg: (i, 0)), fullb((1, LANES))],
        out_shape=[jax.ShapeDtypeStruct((N, H * HW), _BF), jax.ShapeDtypeStruct((N, LANES), F32),
                   jax.ShapeDtypeStruct((1, LANES), F32)],
        compiler_params=_cparams(("arbitrary", "arbitrary")),
    )(kv, d_kcat, d_v, g_kn)

    def krope_bwd_kern(p_ref, d_ref, t_ref, g_ref, o_ref, dg_ref):
        @pl.when(pl.program_id(0) == 0)
        def _():
            dg_ref[...] = jnp.zeros(dg_ref.shape, F32)

        xh, r = _rms128(p_ref[...].astype(F32), c.ROPE)
        drot = _rope_t(d_ref[...], t_ref[0], t_ref[1], t_ref[2])
        dx, dg = _rms128_bwd(xh, r, g_ref[...], drot, c.ROPE)
        o_ref[...] = dx.astype(o_ref.dtype)
        dg_ref[...] += jnp.sum(dg, axis=0, keepdims=True)

    d_kr, g_kn_rope = pl.pallas_call(
        krope_bwd_kern, name="krope_bwd", grid=(nr,),
        in_specs=[rowb(LANES, 0), rowb(LANES, 0), tabs_of(R), fullb((1, LANES))],
        out_specs=[rowb(LANES, 0), fullb((1, LANES))],
        out_shape=[jax.ShapeDtypeStruct((N, LANES), _BF), jax.ShapeDtypeStruct((1, LANES), F32)],
        compiler_params=_cparams(("arbitrary",)),
    )(kr_raw, d_krsum, tabs, g_kr)
    dproj5 = dproj4

    g_wuq_p = _mm(cqn, d_qp, ta=True, name="mm_dwuq", out_dtype=_BF, bk=4096)
    G['w_uq'] = g_wuq_p.reshape(QL, H, HW)[:, :, :c.NOPE + c.ROPE].reshape(QL, H * (c.NOPE + c.ROPE))
    d_cqn = _mm(d_qp, wuq_pT, name="mm_dcqn", out_dtype=F32, bk=4096)
    G['w_ukv'] = _mm(ckvn, d_kv, ta=True, name="mm_dwukv", out_dtype=_BF, bk=4096)
    d_ckvn = _mm(d_kv, wukvT, name="mm_dckvn", out_dtype=F32, bk=4096)

    def to_blocks(n, g):
        if n in COL_SHARDED:
            return jnp.transpose(g.reshape(g.shape[0], NDEV, -1), (1, 0, 2))
        return g.reshape(NDEV, -1, g.shape[1])

    def landing(b):
        return lax.empty(b.shape, b.dtype)

    early = [n for n in BIG if n != 'w_in']
    blocks_e = [to_blocks(n, G[n]) for n in early]
    xe = _split_start(blocks_e, [landing(b) for b in blocks_e], False, "xchg_early_start")
    gq_after = mla_q_norm_g + xe[4][0:1, 0:1]

    def lora_bwd_kern(dp_any, p_ref, dq_ref, dkv_ref, gq_ref, gkv_ref, o_ref, dgq_ref, dgkv_ref):
        @pl.when(pl.program_id(0) == 0)
        def _():
            dgq_ref[...] = jnp.zeros(dgq_ref.shape, F32)
            dgkv_ref[...] = jnp.zeros(dgkv_ref.shape, F32)

        blk = p_ref[...].astype(F32)
        qh, rq = _rms(blk[:, :QL], QL)
        kh, rk = _rms(blk[:, QL:], KVL)
        dq, dgq = _rms_bwd(qh, rq, gq_ref[...], dq_ref[...], QL)
        dk, dgk = _rms_bwd(kh, rk, gkv_ref[...], dkv_ref[...], KVL)
        o_ref[:, :QL] = dq.astype(o_ref.dtype)
        o_ref[:, QL:] = dk.astype(o_ref.dtype)
        dgq_ref[...] += jnp.sum(dgq, axis=0, keepdims=True)
        dgkv_ref[...] += jnp.sum(dgk, axis=0, keepdims=True)

    dproj6, g_q_norm, g_kv_norm = pl.pallas_call(
        lora_bwd_kern, name="lora_bwd", grid=(nr,),
        in_specs=[any_spec, rowb(QL + KVL, lora_blk), rowb(QL, 0), rowb(KVL, 0), fullb((1, QL)), fullb((1, KVL))],
        out_specs=[rowb(QL + KVL, lora_blk), fullb((1, QL)), fullb((1, KVL))],
        out_shape=[jax.ShapeDtypeStruct((N, c.P), _BF), jax.ShapeDtypeStruct((1, QL), F32),
                   jax.ShapeDtypeStruct((1, KVL), F32)],
        input_output_aliases={0: 0}, compiler_params=_cparams(("arbitrary",)),
    )(dproj5, proj, d_cqn, d_ckvn, gq_after, mla_kv_norm_g)

    g_win_p = _mm(h, dproj6, ta=True, name="mm_dwin", out_dtype=_BF, bk=4096)
    g_wkr = _mm(h, d_kr, ta=True, name="mm_dwkr", out_dtype=_BF, bk=4096)
    blocks_w = [_win_blocks(g_win_p, g_wkr, c)]
    xw = _split_start(blocks_w, [landing(b) for b in blocks_w], False, "xchg_win_start")
    d_h = _mm(dproj6, win_pT, name="mm_dh", out_dtype=F32, bk=3072, after=xw[4], plus=(d_kr, w_krT))

    def final_bwd_kern(x_ref, g_ref, dh_ref, dy_ref, gx_ref, dg_ref):
        @pl.when(pl.program_id(0) == 0)
        def _():
            dg_ref[...] = jnp.zeros(dg_ref.shape, F32)

        xh, r = _rms(x_ref[...], D)
        dx, dg = _rms_bwd(xh, r, g_ref[...], dh_ref[...], D)
        gx_ref[...] = dy_ref[...].astype(F32) + dx
        dg_ref[...] += jnp.sum(dg, axis=0, keepdims=True)

    grad_x, g_norm = pl.pallas_call(
        final_bwd_kern, name="final_bwd", grid=(nr2,),
        in_specs=[rowb2(D, 0), fullb((1, D)), rowb2(D, 0), rowb2(D, 0)],
        out_specs=[rowb2(D, 0), fullb((1, D))],
        out_shape=[jax.ShapeDtypeStruct((N, D), F32), jax.ShapeDtypeStruct((1, D), F32)],
        compiler_params=_cparams(("arbitrary",)),
    )(xs, norm_g, d_h, dyb)

    res = [{}, {}, {}, {}]

    def adam_into(n, parts):
        outs = _adam(parts, W[n][0], Mo[n][0], Vo[n][0], "adam_" + n)
        for k in range(4):
            res[k][n] = outs[k][None]
        return outs[0]

    recv_e = _split_wait(xe, False, grad_x, "xchg_early_wait")
    last = [adam_into(n, parts) for n, parts in zip(early, recv_e)][-1]
    recv_w = _split_wait(xw, False, last, "xchg_win_wait")
    adam_into('w_in', recv_w[0])

    small_g = {'norm_g': g_norm, 'mla_q_norm_g': g_q_norm, 'mla_kv_norm_g': g_kv_norm,
               'mla_qn_nope_g': g_qn_nope, 'mla_qn_rope_g': g_qn_rope[:, :c.ROPE], 'mla_kn_nope_g': g_kn_nope,
               'mla_kn_rope_g': g_kn_rope[:, :c.ROPE], 'mem_norm_g': g_mem_norm, 'mem_qn_g': g_mem_qn,
               'mem_kn_g': g_mem_kn}
    small_part = _pack([small_g[n] for n in SMALL] + [g_convw[0:3, :]], 0)
    small_all = _all_gather([small_part], "ag_small_grads")[0]
    small_shapes = [W[n].shape for n in SMALL]
    pieces = _unpack(small_all, small_shapes + [(3, CW)])
    cw8 = CW // NDEV
    conv_mine = lax.dynamic_slice_in_dim(pieces[-1].reshape(NDEV, 3, NDEV, cw8), me, 1, axis=2)[:, :, 0, :]
    sm_parts = _pack(pieces[:-1] + [conv_mine], 1)
    sm_names = SMALL + ['conv_w']
    sm_shapes = small_shapes + [(3, cw8)]
    w_sm = _pack([W[n] for n in SMALL] + [conv_w[0]], 0)
    m_sm = _pack([Mo[n] for n in SMALL] + [m_conv_w[0]], 0)
    v_sm = _pack([Vo[n] for n in SMALL] + [v_conv_w[0]], 0)
    outs_sm = [_unpack(o, sm_shapes) for o in _adam(sm_parts, w_sm, m_sm, v_sm, "adam_small")]
    for k in range(4):
        for n, a in zip(sm_names, outs_sm[k]):
            res[k][n] = a[None] if n == 'conv_w' else a
    return (loss, grad_x[None], *[res[0][n] for n in WEIGHTS], *[res[1][n] for n in WEIGHTS],
            *[res[2][n] for n in WEIGHTS], *[res[3][n] for n in WEIGHTS])
```

```python
import math

import jax
import jax.numpy as jnp
from jax import lax
from jax.experimental import pallas as pl
from jax.experimental.pallas import tpu as pltpu

F32 = jnp.float32
_BF = jnp.bfloat16
EPS = 1e-6
CHUNK = 64
ROPE_THETA = 10000.0
ADAM_LR, ADAM_B1, ADAM_B2, ADAM_EPS, ADAM_WD, ADAM_STEP = 0.001, 0.9, 0.999, 1e-08, 0.01, 10
NDEV = 8
AXES = ("x", "y", "c")
MESH = pl.DeviceIdType.MESH
LANES = 128
NEG = -1e30
LOG2E = math.log2(math.e)
V7X_VMEM_LIMIT = 56 * 1024 * 1024
PACK_C = 1024
ATT_BLOCK = 512
ATT_UNROLL_FWD = 8
ATT_UNROLL_BWD = 4
ATT_QBLOCKS = 2
ADAM_BLOCK_ELEMS = 256 * 1024

WEIGHTS = ['norm_g', 'w_in', 'conv_w', 'w_conv_out', 'mla_q_norm_g', 'w_uq', 'mla_kv_norm_g', 'w_ukv',
           'mla_qn_nope_g', 'mla_qn_rope_g', 'mla_kn_nope_g', 'mla_kn_rope_g', 'w_mla_out', 'mem_norm_g',
           'w_mem_kv', 'mem_qn_g', 'mem_kn_g', 'w_mem_out', 'w_o']
BIG = ['w_in', 'w_conv_out', 'w_uq', 'w_ukv', 'w_mla_out', 'w_mem_kv', 'w_mem_out', 'w_o']
COL_SHARDED = ('w_in', 'w_conv_out', 'w_uq', 'w_ukv', 'w_mem_out')
SMALL = ['norm_g', 'mla_q_norm_g', 'mla_kv_norm_g', 'mla_qn_nope_g', 'mla_qn_rope_g', 'mla_kn_nope_g',
         'mla_kn_rope_g', 'mem_norm_g', 'mem_qn_g', 'mem_kn_g']


def _tile(dim, target, align):
    if dim <= target:
        return dim
    t = target - target % align
    while t > 0:
        if dim % t == 0:
            return t
        t -= align
    raise ValueError(f"no tile for {dim} {target} {align}")


def _cparams(sem):
    return pltpu.CompilerParams(dimension_semantics=sem, vmem_limit_bytes=V7X_VMEM_LIMIT)


def _sig(x):
    return 1.0 / (1.0 + jnp.exp(-x))


def _rms(x, n):
    r = lax.rsqrt(jnp.sum(x * x, axis=-1, keepdims=True) * (1.0 / n) + EPS)
    return x * r, r


def _rms_bwd(xhat, r, g, dy, n):
    dxh = dy * g
    dx = r * (dxh - xhat * (jnp.sum(dxh * xhat, axis=-1, keepdims=True) * (1.0 / n)))
    return dx, dy * xhat


def _rowmean128(x, n):
    return jnp.dot(x.astype(_BF), jnp.full((LANES, LANES), 1.0 / n, _BF), preferred_element_type=F32)


def _rms128(x, n):
    r = lax.rsqrt(_rowmean128(x * x, n) + EPS)
    return x * r, r


def _rms128_bwd(xhat, r, g, dy, n):
    dxh = dy * g
    dx = r * (dxh - xhat * _rowmean128(dxh * xhat, n))
    return dx, dy * xhat


def _rope(x, cosp, sina, sinb):
    return x * cosp + pltpu.roll(x, 96, 1) * sina + pltpu.roll(x, 32, 1) * sinb


def _rope_t(d, cosp, sina, sinb):
    return d * cosp + pltpu.roll(d * sina, 32, 1) + pltpu.roll(d * sinb, 96, 1)


def _dot_nt(a, b):
    return lax.dot_general(a, b, (((1,), (1,)), ((), ())), preferred_element_type=F32)


def _dot_tn(a, b):
    return lax.dot_general(a, b, (((0,), (0,)), ((), ())), preferred_element_type=F32)


def _dot(a, b):
    return jnp.dot(a, b, preferred_element_type=F32)


def _all_gather(shards, name):
    na = len(shards)
    nc = 9
    halves = [s.shape[0] // 32 * 16 if s.shape[0] >= 256 else None for s in shards]

    def body(*refs):
        x_refs, out_refs = refs[:na], refs[na:2 * na]
        send_sems, recv_sems, local_sems = refs[2 * na:]
        x, y, c = lax.axis_index("x"), lax.axis_index("y"), lax.axis_index("c")
        me, sib = (x, y, c), (x, y, 1 - c)
        px, py, pd = (1 - x, y, c), (x, 1 - y, c), (1 - x, 1 - y, c)

        def other_core(p):
            return (p[0], p[1], 1 - p[2])

        def rows(a, blk, part=None):
            r = out_refs[a].at[4 * blk[0] + 2 * blk[1] + blk[2]]
            if part is None or halves[a] is None:
                return r
            rest = shards[a].shape[0] - halves[a]
            return r.at[pl.ds(0, halves[a])] if part == 0 else r.at[pl.ds(halves[a], rest)]

        def copy(a, k, blk, to, part=None, src=None):
            dst = rows(a, blk, part)
            return pltpu.make_async_remote_copy(
                src_ref=dst if src is None else src, dst_ref=dst,
                send_sem=send_sems.at[nc * a + k], recv_sem=recv_sems.at[nc * a + k],
                device_id=to, device_id_type=MESH)

        mine = [pltpu.make_async_copy(x_refs[a], rows(a, me), local_sems.at[a]) for a in range(na)]
        for cp in mine:
            cp.start()
        started = []
        for a in range(na):
            started += [copy(a, 0, me, px, src=x_refs[a]), copy(a, 1, me, py, src=x_refs[a]),
                        copy(a, 2, me, sib, src=x_refs[a])]
        for cp in started:
            cp.start()

        def forward(cp):
            cp.start()
            started.append(cp)

        for a in range(na):
            copy(a, 0, px, me).wait_recv()
            forward(copy(a, 3, px, py, part=0))
            forward(copy(a, 4, px, sib))
        for a in range(na):
            copy(a, 1, py, me).wait_recv()
            if halves[a] is not None:
                forward(copy(a, 5, py, px, part=1))
            forward(copy(a, 6, py, sib))
        for a in range(na):
            copy(a, 3, pd, me, part=0).wait_recv()
            forward(copy(a, 7, pd, sib, part=0))
            if halves[a] is not None:
                copy(a, 5, pd, me, part=1).wait_recv()
                forward(copy(a, 8, pd, sib, part=1))
        for a in range(na):
            copy(a, 2, sib, me).wait_recv()
            copy(a, 4, other_core(px), me).wait_recv()
            copy(a, 6, other_core(py), me).wait_recv()
            copy(a, 7, other_core(pd), me, part=0).wait_recv()
            if halves[a] is not None:
                copy(a, 8, other_core(pd), me, part=1).wait_recv()
        for cp in started:
            cp.wait_send()
        for cp in mine:
            cp.wait()

    any_spec = pl.BlockSpec(memory_space=pl.ANY)
    return pl.pallas_call(
        body, name=name,
        out_shape=[jax.ShapeDtypeStruct((NDEV,) + s.shape, s.dtype) for s in shards],
        in_specs=[any_spec] * na, out_specs=[any_spec] * na,
        scratch_shapes=[pltpu.SemaphoreType.DMA((nc * na,)), pltpu.SemaphoreType.DMA((nc * na,)),
                        pltpu.SemaphoreType.DMA((na,))],
    )(*shards)


_HBM = pl.BlockSpec(memory_space=pltpu.HBM)
_SEM = pl.BlockSpec(memory_space=pltpu.SEMAPHORE)
_EFFECT = pltpu.SideEffectType.DATAFLOW_SIDE_EFFECTING


def _split_copy(a, k, src_refs, land_refs, send_sems, recv_sems, gather, receive_side):
    x, y, c = lax.axis_index("x"), lax.axis_index("y"), lax.axis_index("c")
    me = 4 * x + 2 * y + c
    tx, ty, tc = x ^ ((k + 1) >> 2 & 1), y ^ ((k + 1) >> 1 & 1), c ^ ((k + 1) & 1)
    peer = 4 * tx + 2 * ty + tc
    return pltpu.make_async_remote_copy(
        src_ref=src_refs[a] if gather else src_refs[a].at[peer],
        dst_ref=land_refs[a].at[peer if receive_side else me],
        send_sem=send_sems.at[7 * a + k], recv_sem=recv_sems.at[7 * a + k],
        device_id=(tx, ty, tc), device_id_type=MESH)


def _own_copy(a, na, src_refs, land_refs, send_sems, gather):
    me = 4 * lax.axis_index("x") + 2 * lax.axis_index("y") + lax.axis_index("c")
    return pltpu.make_async_copy(src_refs[a] if gather else src_refs[a].at[me], land_refs[a].at[me],
                                 send_sems.at[7 * na + a])


def _split_start(srcs, lands, gather, name, after=None):
    na = len(srcs)
    extra = [] if after is None else [after]

    def body(*refs):
        src_refs, land_refs = refs[:na], refs[na:2 * na]
        send_sems, recv_sems = refs[2 * na + len(extra)], refs[2 * na + len(extra) + 1]
        token = refs[-1]
        for k in range(7):
            for a in range(na):
                _split_copy(a, k, src_refs, land_refs, send_sems, recv_sems, gather, False).start()
        for a in range(na):
            _own_copy(a, na, src_refs, land_refs, send_sems, gather).start()
        token[...] = jnp.zeros_like(token)

    hbm = [pltpu.HBM(b.shape, b.dtype) for b in list(srcs) + list(lands)]
    outs = pl.pallas_call(
        body, name=name,
        out_shape=(pltpu.SemaphoreType.DMA((8 * na,)), pltpu.SemaphoreType.DMA((7 * na,)), *hbm,
                   jax.ShapeDtypeStruct((8, LANES), F32)),
        in_specs=[_HBM] * (2 * na) + [pl.BlockSpec(memory_space=pl.ANY)] * len(extra),
        out_specs=(_SEM, _SEM, *[_HBM] * (2 * na), pl.BlockSpec(memory_space=pltpu.VMEM)),
        input_output_aliases={j: 2 + j for j in range(2 * na)},
        compiler_params=pltpu.CompilerParams(has_side_effects=_EFFECT),
    )(*[pltpu.with_memory_space_constraint(b, pltpu.HBM) for b in srcs],
      *[pltpu.with_memory_space_constraint(l, pltpu.HBM) for l in lands], *extra)
    return outs[0], outs[1], outs[2:2 + na], outs[2 + na:2 + 2 * na], outs[-1]


def _split_wait(started, gather, after, name):
    send_sems, recv_sems, srcs, lands, _ = started
    na = len(srcs)

    def body(*refs):
        src_refs, land_refs = refs[:na], refs[na:2 * na]
        send_s, recv_s = refs[2 * na], refs[2 * na + 1]
        for k in range(7):
            for a in range(na):
                cp = _split_copy(a, k, src_refs, land_refs, send_s, recv_s, gather, True)
                cp.wait_send()
                cp.wait_recv()
        for a in range(na):
            _own_copy(a, na, src_refs, land_refs, send_s, gather).wait()

    hbm = [pltpu.HBM(b.shape, b.dtype) for b in list(srcs) + list(lands)]
    outs = pl.pallas_call(
        body, name=name, out_shape=tuple(hbm),
        in_specs=[_HBM] * (2 * na) + [_SEM, _SEM, pl.BlockSpec(memory_space=pl.ANY)],
        out_specs=tuple([_HBM] * (2 * na)),
        input_output_aliases={j: j for j in range(2 * na)},
        compiler_params=pltpu.CompilerParams(has_side_effects=_EFFECT),
    )(*srcs, *lands, send_sems, recv_sems, after)
    return outs[na:]


def _seg_rows(size):
    rows = -(-size // PACK_C)
    return -(-rows // 16) * 16


def _pack(arrs, lead):
    parts = []
    for a in arrs:
        lshape = a.shape[:lead]
        f = a.reshape(lshape + (-1,)).astype(F32)
        rows = _seg_rows(f.shape[-1])
        f = jnp.pad(f, [(0, 0)] * lead + [(0, rows * PACK_C - f.shape[-1])])
        parts.append(f.reshape(lshape + (rows, PACK_C)))
    return jnp.concatenate(parts, axis=lead)


def _unpack(buf, shapes):
    lshape = buf.shape[:-2]
    out, r = [], 0
    for shp in shapes:
        size = math.prod(shp)
        rows = _seg_rows(size)
        seg = buf[..., r:r + rows, :].reshape(lshape + (rows * PACK_C,))[..., :size]
        out.append(seg.reshape(lshape + tuple(shp)))
        r += rows
    return out


def _mm(a, b, *, name, out_dtype, ta=False, tb=False, bm=1024, bn=1024, bk=2048, after=None, plus=None):
    if ta:
        kdim, m = a.shape
    else:
        m, kdim = a.shape
    n, k2 = b.shape if tb else b.shape[::-1]
    assert kdim == k2 and not (ta and tb), (a.shape, b.shape)
    bm = _tile(m, bm, LANES if ta else 16)
    bn = _tile(n, bn, LANES)
    bk = _tile(kdim, bk, LANES)
    nk = kdim // bk
    n_after = 0 if after is None else 1
    n_plus = 0 if plus is None else 2

    def kern(a_ref, b_ref, *rest):
        plus_refs = rest[n_after:n_after + n_plus]
        o_ref, scratch = rest[n_after + n_plus], rest[n_after + n_plus + 1:]
        part = (_dot_tn if ta else _dot_nt if tb else _dot)(a_ref[...], b_ref[...])

        def first(p):
            return p + _dot(plus_refs[0][...], plus_refs[1][...]) if plus is not None else p

        if nk == 1:
            o_ref[...] = first(part).astype(o_ref.dtype)
        else:
            acc = scratch[0] if scratch else o_ref
            k = pl.program_id(2)

            @pl.when(k == 0)
            def _():
                acc[...] = first(jnp.zeros(acc.shape, F32))

            acc[...] += part
            if scratch:
                @pl.when(k == nk - 1)
                def _():
                    o_ref[...] = acc[...].astype(o_ref.dtype)

    a_spec = pl.BlockSpec((bk, bm), lambda i, j, k: (k, i)) if ta else pl.BlockSpec((bm, bk), lambda i, j, k: (i, k))
    b_spec = pl.BlockSpec((bn, bk), lambda i, j, k: (j, k)) if tb else pl.BlockSpec((bk, bn), lambda i, j, k: (k, j))
    extra_specs, extra_args = [], []
    if after is not None:
        extra_specs.append(pl.BlockSpec(after.shape, lambda i, j, k: (0, 0)))
        extra_args.append(after)
    if plus is not None:
        kk = plus[0].shape[1]
        extra_specs += [pl.BlockSpec((bm, kk), lambda i, j, k: (i, 0)), pl.BlockSpec((kk, bn), lambda i, j, k: (0, j))]
        extra_args += list(plus)
    return pl.pallas_call(
        kern, name=name, grid=(m // bm, n // bn, nk),
        in_specs=[a_spec, b_spec] + extra_specs,
        out_specs=pl.BlockSpec((bm, bn), lambda i, j, k: (i, j)),
        out_shape=jax.ShapeDtypeStruct((m, n), out_dtype),
        scratch_shapes=[pltpu.VMEM((bm, bn), F32)] if nk > 1 and out_dtype != F32 else [],
        compiler_params=_cparams(("parallel", "parallel", "arbitrary")),
    )(a, b, *extra_args)


def _adam(parts, w_a, m_a, v_a, name):
    rows, cols = w_a.shape
    rb = _tile(rows, max(8, ADAM_BLOCK_ELEMS // cols // 8 * 8), 8)
    bc1 = 1.0 - ADAM_B1 ** ADAM_STEP
    bc2 = 1.0 - ADAM_B2 ** ADAM_STEP

    def adam_kern(p_ref, w_ref, m_ref, v_ref, g_ref, d_ref, nm_ref, nv_ref):
        g = p_ref[0].astype(F32)
        for j in range(1, NDEV):
            g = g + p_ref[j].astype(F32)
        m_new = ADAM_B1 * m_ref[...] + (1.0 - ADAM_B1) * g
        v_new = ADAM_B2 * v_ref[...] + (1.0 - ADAM_B2) * (g * g)
        g_ref[...] = g
        nm_ref[...] = m_new
        nv_ref[...] = v_new
        d_ref[...] = -ADAM_LR * ((m_new / bc1) / (jnp.sqrt(v_new / bc2) + ADAM_EPS) + ADAM_WD * w_ref[...])

    blk = pl.BlockSpec((rb, cols), lambda i: (i, 0))
    return pl.pallas_call(
        adam_kern, name=name, grid=(rows // rb,),
        in_specs=[pl.BlockSpec((NDEV, rb, cols), lambda i: (0, i, 0)), blk, blk, blk],
        out_specs=[blk] * 4, out_shape=[jax.ShapeDtypeStruct((rows, cols), F32)] * 4,
        compiler_params=_cparams(("parallel",)),
    )(parts, w_a, m_a, v_a)


class _Cfg:
    pass


def _config(x, conv_w, w_uq, w_ukv, mla_qn_nope_g, mla_qn_rope_g, mem, mem_qn_g, w_mem_out, w_mla_out):
    c = _Cfg()
    c.N, c.D = x.shape[1], x.shape[2]
    c.CW = conv_w.shape[2] * NDEV
    c.QL, c.KVL = w_uq.shape[1], w_ukv.shape[1]
    c.NOPE, c.ROPE = mla_qn_nope_g.shape[1], mla_qn_rope_g.shape[1]
    c.H = w_uq.shape[2] * NDEV // (c.NOPE + c.ROPE)
    c.V = w_ukv.shape[2] * NDEV // c.H - c.NOPE
    assert c.NOPE == LANES and c.V == LANES and c.ROPE == LANES // 2
    c.HW = 2 * LANES
    c.HV = c.H * c.V
    assert w_mla_out.shape[1] * NDEV == c.HV
    c.M = mem.shape[1]
    c.MHD = mem_qn_g.shape[1]
    c.MW = w_mem_out.shape[1]
    c.MH = c.MW // c.MHD
    c.o_conv = 0
    c.o_mz = 4 * c.CW
    c.o_g = c.o_mz + c.HV
    c.o_mem = c.o_g + 3 * c.D
    c.o_lora = c.o_mem + 2 * c.MW
    c.P = c.o_lora + c.QL + c.KVL
    assert c.o_mz % c.HV == 0 and c.o_g % (3 * c.D) == 0 and c.o_mem % (2 * c.MW) == 0
    assert c.o_lora % (c.QL + c.KVL) == 0 and c.QL % LANES == 0 and c.KVL % LANES == 0
    c.IN = 4 * c.CW + c.QL + c.KVL + c.ROPE + c.HV + 2 * c.MW + 3 * c.D
    c.R = _tile(c.N, 256, 16)
    c.RP = _tile(c.N, 512, 16)
    c.HG = _tile(c.H, 4, 1)
    c.B = _tile(c.N, ATT_BLOCK, CHUNK)
    c.scale = float((c.NOPE + c.ROPE) ** -0.5)
    c.mscale = float(c.MHD ** -0.5)
    return c


def _win_segments(c):
    ref_order = (('conv', 4 * c.CW), ('lora', c.QL + c.KVL), ('kr', c.ROPE), ('mz', c.HV), ('mem', 2 * c.MW), ('g', 3 * c.D))
    mine = {'conv': c.o_conv, 'mz': c.o_mz, 'g': c.o_g, 'mem': c.o_mem, 'lora': c.o_lora, 'kr': 0}
    segs, o = [], 0
    for nm, wd in ref_order:
        segs.append((nm, o, wd, mine[nm]))
        o += wd
    return segs


def _win_split(g_win_t, c):
    n8 = g_win_t.shape[1]

    def rows(a, wd):
        return [g_win_t[j][max(a, j * n8) - j * n8:min(a + wd, (j + 1) * n8) - j * n8]
                for j in range(a // n8, (a + wd - 1) // n8 + 1)]

    segs = {nm: (a, wd) for nm, a, wd, _ in _win_segments(c)}
    main = [p for nm in ('conv', 'mz', 'g', 'mem', 'lora') for p in rows(*segs[nm])]
    kr = jnp.concatenate(rows(*segs['kr']) + [jnp.zeros((LANES - c.ROPE, g_win_t.shape[2]), g_win_t.dtype)], axis=0)
    return jnp.concatenate(main, axis=0), kr


def _win_blocks(g, g_kr, c):
    n8 = c.IN // NDEV
    blocks = []
    for j in range(NDEV):
        lo, hi = j * n8, (j + 1) * n8
        parts = []
        for nm, a, wd, mine in _win_segments(c):
            s, e = max(a, lo), min(a + wd, hi)
            if s < e:
                parts.append((g_kr if nm == 'kr' else g)[:, mine + s - a:mine + e - a])
        blocks.append(jnp.concatenate(parts, axis=1))
    return jnp.stack(blocks, axis=0)


def kernel(x, positions, mem, norm_g, w_in, conv_w, w_conv_out, mla_q_norm_g, w_uq, mla_kv_norm_g, w_ukv, mla_qn_nope_g, mla_qn_rope_g, mla_kn_nope_g, mla_kn_rope_g, w_mla_out, mem_norm_g, w_mem_kv, mem_qn_g, mem_kn_g, w_mem_out, w_o, loss_target, m_norm_g, m_w_in, m_conv_w, m_w_conv_out, m_mla_q_norm_g, m_w_uq, m_mla_kv_norm_g, m_w_ukv, m_mla_qn_nope_g, m_mla_qn_rope_g, m_mla_kn_nope_g, m_mla_kn_rope_g, m_w_mla_out, m_mem_norm_g, m_w_mem_kv, m_mem_qn_g, m_mem_kn_g, m_w_mem_out, m_w_o, v_norm_g, v_w_in, v_conv_w, v_w_conv_out, v_mla_q_norm_g, v_w_uq, v_mla_kv_norm_g, v_w_ukv, v_mla_qn_nope_g, v_mla_qn_rope_g, v_mla_kn_nope_g, v_mla_kn_rope_g, v_w_mla_out, v_mem_norm_g, v_w_mem_kv, v_mem_qn_g, v_mem_kn_g, v_w_mem_out, v_w_o):
    args = dict(locals())
    W = {n: args[n] for n in WEIGHTS}
    Mo = {n: args['m_' + n] for n in WEIGHTS}
    Vo = {n: args['v_' + n] for n in WEIGHTS}
    c = _config(x, conv_w, w_uq, w_ukv, mla_qn_nope_g, mla_qn_rope_g, mem, mem_qn_g, w_mem_out, w_mla_out)
    N, D, R, B, H = c.N, c.D, c.R, c.B, c.H
    assert x.shape[0] == 1
    xs = x[0]
    tgt = loss_target[0]
    memx = mem[0]
    me = 4 * lax.axis_index("x") + 2 * lax.axis_index("y") + lax.axis_index("c")
    nr = N // R

    g_win, g_taps = _all_gather([W['w_in'][0].astype(_BF).T, conv_w[0]], "ag_w_in")
    rest = [n for n in BIG if n != 'w_in']
    shards_r = [W[n][0].astype(_BF).T if n in COL_SHARDED else W[n][0].astype(_BF) for n in rest]
    lands_r = [lax.empty((NDEV,) + s.shape, s.dtype) for s in shards_r]
    ag_rest = _split_start(shards_r, lands_r, True, "ag_rest_start", after=g_win)
    win_pT, w_krT = _win_split(g_win, c)
    convw = jnp.transpose(g_taps, (1, 0, 2)).reshape(3, c.CW)
    convw8 = jnp.pad(convw, ((0, 5), (0, 0)))

    def rowb(width, cidx, rows=R):
        return pl.BlockSpec((rows, width), lambda i, _c=cidx: (i, _c))

    R2 = c.RP
    nr2 = N // R2
    RM = _tile(N, 1024, 16)

    def rowb2(width, cidx):
        return rowb(width, cidx, R2)

    def fullb(shape):
        nd = len(shape)
        return pl.BlockSpec(shape, lambda *_: (0,) * nd)

    def pad_lanes(g, w=LANES):
        return jnp.pad(g, ((0, 0), (0, w - g.shape[1])))

    def tabs_of(rows):
        return pl.BlockSpec((3, rows, LANES), lambda i, *_: (0, i, 0))

    half = c.ROPE // 2
    inv_freq = jnp.power(ROPE_THETA, -jnp.arange(half, dtype=F32) / half)
    invf = jnp.concatenate([inv_freq, inv_freq, jnp.zeros((LANES - c.ROPE,), F32)])[None, :]
    pos_col = positions[0].astype(F32).reshape(N, 1)

    def rope_tab_kern(pos_ref, invf_ref, o_ref):
        ang = pos_ref[...] * invf_ref[...]
        co, si = jnp.cos(ang), jnp.sin(ang)
        lane = lax.broadcasted_iota(jnp.int32, ang.shape, 1)
        o_ref[0] = jnp.where(lane < c.ROPE, co, 0.0)
        o_ref[1] = jnp.where(lane < half, -si, 0.0)
        o_ref[2] = jnp.where(jnp.logical_and(lane >= half, lane < c.ROPE), si, 0.0)

    tabs = pl.pallas_call(
        rope_tab_kern, name="rope_tab", grid=(nr,),
        in_specs=[pl.BlockSpec((R, 1), lambda i: (i, 0)), fullb((1, LANES))],
        out_specs=tabs_of(R),
        out_shape=jax.ShapeDtypeStruct((3, N, LANES), F32),
        compiler_params=_cparams(("parallel",)),
    )(pos_col, invf)

    def make_rms_kern():
        def rms_fwd_kern(x_ref, g_ref, o_ref):
            xh, _ = _rms(x_ref[...].astype(F32), x_ref.shape[-1])
            o_ref[...] = (xh * g_ref[...]).astype(o_ref.dtype)
        return rms_fwd_kern

    h = pl.pallas_call(
        make_rms_kern(), name="rms_x", grid=(nr2,),
        in_specs=[rowb2(D, 0), fullb((1, D))], out_specs=rowb2(D, 0),
        out_shape=jax.ShapeDtypeStruct((N, D), _BF), compiler_params=_cparams(("parallel",)),
    )(xs, norm_g)

    proj = _mm(h, win_pT, tb=True, name="mm_proj", out_dtype=_BF, after=ag_rest[4])
    kr_raw = _mm(h, w_krT, tb=True, name="mm_kr", out_dtype=_BF)

    Wf = {n: g.reshape(-1, g.shape[2]) for n, g in zip(rest, _split_wait(ag_rest, True, proj, "ag_rest_wait"))}
    wuqT = Wf['w_uq'].reshape(H, c.NOPE + c.ROPE, c.QL)
    wuq_pT = jnp.pad(wuqT, ((0, 0), (0, c.HW - c.NOPE - c.ROPE), (0, 0))).reshape(H * c.HW, c.QL)
    wukvT = Wf['w_ukv']
    wcoT, wmo, wmkv, wmemoT, wo = Wf['w_conv_out'], Wf['w_mla_out'], Wf['w_mem_kv'], Wf['w_mem_out'], Wf['w_o']

    CW = c.CW
    conv_blk = c.o_conv // (4 * CW)
    HALO = 16
    rh = R // HALO

    def conv_parts(blk):
        blk = blk.astype(F32)
        return blk[:, 0:CW], blk[:, CW:2 * CW], blk[:, 2 * CW:3 * CW], blk[:, 3 * CW:4 * CW]

    def shifted(cu, prev, i):
        prev = jnp.where(i > 0, prev, 0.0)
        rid = lax.broadcasted_iota(jnp.int32, cu.shape, 0)
        last, last2 = prev[HALO - 1:HALO, :], prev[HALO - 2:HALO - 1, :]
        sh1 = jnp.where(rid == 0, last, pltpu.roll(cu, 1, 0))
        sh2 = jnp.where(rid == 0, last2, jnp.where(rid == 1, last, pltpu.roll(cu, 2, 0)))
        return sh1, sh2

    def conv_fwd_kern(p_ref, prev_ref, w_ref, o_ref):
        i = pl.program_id(0)
        cg, bg, u, z = conv_parts(p_ref[...])
        pc, _, pu, _ = conv_parts(prev_ref[...])
        cu = cg * u
        sh1, sh2 = shifted(cu, pc * pu, i)
        w = w_ref[...]
        conv = w[0:1, :] * sh2 + w[1:2, :] * sh1 + w[2:3, :] * cu
        o_ref[...] = (bg * conv * (z * _sig(z))).astype(o_ref.dtype)

    prev_spec = pl.BlockSpec((HALO, 4 * CW), lambda i: (jnp.maximum(i * rh - 1, 0), conv_blk))
    a_conv = pl.pallas_call(
        conv_fwd_kern, name="conv_fwd", grid=(nr,),
        in_specs=[rowb(4 * CW, conv_blk), prev_spec, fullb((8, CW))],
        out_specs=rowb(CW, 0), out_shape=jax.ShapeDtypeStruct((N, CW), _BF),
        compiler_params=_cparams(("parallel",)),
    )(proj, proj, convw8)
    o_conv = _mm(a_conv, wcoT, tb=True, name="mm_oconv", out_dtype=_BF)

    QL, KVL, HW = c.QL, c.KVL, c.HW
    lora_blk = c.o_lora // (QL + KVL)

    def lora_fwd_kern(p_ref, gq_ref, gkv_ref, q_ref, kv_ref):
        blk = p_ref[...].astype(F32)
        qh, _ = _rms(blk[:, :QL], QL)
        kh, _ = _rms(blk[:, QL:], KVL)
        q_ref[...] = (qh * gq_ref[...]).astype(q_ref.dtype)
        kv_ref[...] = (kh * gkv_ref[...]).astype(kv_ref.dtype)

    cqn, ckvn = pl.pallas_call(
        lora_fwd_kern, name="lora_fwd", grid=(nr,),
        in_specs=[rowb(QL + KVL, lora_blk), fullb((1, QL)), fullb((1, KVL))],
        out_specs=[rowb(QL, 0), rowb(KVL, 0)],
        out_shape=[jax.ShapeDtypeStruct((N, QL), _BF), jax.ShapeDtypeStruct((N, KVL), _BF)],
        compiler_params=_cparams(("parallel",)),
    )(proj, mla_q_norm_g, mla_kv_norm_g)
    q_p = _mm(cqn, wuq_pT, tb=True, name="mm_q", out_dtype=_BF, bn=2048)
    kv = _mm(ckvn, wukvT, tb=True, name="mm_kv", out_dtype=_BF, bn=2048)

    g_qn, g_qr = mla_qn_nope_g, pad_lanes(mla_qn_rope_g)
    g_kn, g_kr = mla_kn_nope_g, pad_lanes(mla_kn_rope_g)

    def krope_fwd_kern(p_ref, t_ref, g_ref, o_ref):
        xh, _ = _rms128(p_ref[...].astype(F32), c.ROPE)
        o_ref[...] = _rope(xh * g_ref[...], t_ref[0], t_ref[1], t_ref[2]).astype(o_ref.dtype)

    k_rope = pl.pallas_call(
        krope_fwd_kern, name="krope_fwd", grid=(nr,),
        in_specs=[rowb(LANES, 0), tabs_of(R), fullb((1, LANES))],
        out_specs=rowb(LANES, 0), out_shape=jax.ShapeDtypeStruct((N, LANES), _BF),
        compiler_params=_cparams(("parallel",)),
    )(kr_raw, tabs, g_kr)

    RP, HG = c.RP, c.HG
    nrp, nhg = N // RP, H // HG
    heads_in = pl.BlockSpec((RP, HG * HW), lambda i, hg: (i, hg))
    heads_out = pl.BlockSpec((HG, RP, HW), lambda i, hg: (hg, i, 0))

    def q_prep_kern(q_ref, t_ref, gn_ref, gr_ref, o_ref):
        for g in range(HG):
            blk = q_ref[:, g * HW:(g + 1) * HW].astype(F32)
            nh, _ = _rms128(blk[:, :LANES], c.NOPE)
            rhat, _ = _rms128(blk[:, LANES:], c.ROPE)
            rot = _rope(rhat * gr_ref[...], t_ref[0], t_ref[1], t_ref[2])
            o_ref[g] = jnp.concatenate([nh * gn_ref[...], rot], axis=1).astype(o_ref.dtype)

    q_cat = pl.pallas_call(
        q_prep_kern, name="q_prep", grid=(nrp, nhg),
        in_specs=[heads_in, tabs_of(RP), fullb((1, LANES)), fullb((1, LANES))],
        out_specs=heads_out, out_shape=jax.ShapeDtypeStruct((H, N, HW), _BF),
        compiler_params=_cparams(("parallel", "parallel")),
    )(q_p, tabs, g_qn * (c.scale * LOG2E), g_qr * (c.scale * LOG2E))

    def k_prep_kern(kv_ref, kr_ref, gn_ref, o_ref):
        for g in range(HG):
            kn, _ = _rms128(kv_ref[:, g * HW:g * HW + LANES].astype(F32), c.NOPE)
            o_ref[g] = jnp.concatenate([(kn * gn_ref[...]).astype(o_ref.dtype), kr_ref[...]], axis=1)

    k_cat = pl.pallas_call(
        k_prep_kern, name="k_prep", grid=(nrp, nhg),
        in_specs=[heads_in, pl.BlockSpec((RP, LANES), lambda i, hg: (i, 0)), fullb((1, LANES))],
        out_specs=heads_out, out_shape=jax.ShapeDtypeStruct((H, N, HW), _BF),
        compiler_params=_cparams(("parallel", "parallel")),
    )(kv, k_rope, g_kn)

    QB = ATT_QBLOCKS if N % (ATT_QBLOCKS * B) == 0 else 1
    BQ = QB * B
    nq = N // BQ
    assert CHUNK & (CHUNK - 1) == 0 and B % CHUNK == 0 and ATT_UNROLL_FWD % QB == 0 and ATT_UNROLL_BWD % QB == 0

    def live_rows(diag):
        return slice(0 if diag is None else diag * B, BQ)

    def diag_mask(s):
        row = lax.broadcasted_iota(jnp.int32, s.shape, 0)
        col = lax.broadcasted_iota(jnp.int32, s.shape, 1)
        shift = CHUNK.bit_length() - 1
        allowed = jnp.right_shift(col, shift) <= jnp.right_shift(row, shift)
        return jnp.where(allowed, s, NEG)

    k_head = pl.BlockSpec((1, N, HW), lambda hh, i: (hh, 0, 0))
    v_head = pl.BlockSpec((N, LANES), lambda hh, i: (0, 2 * hh + 1))
    q_blk = pl.BlockSpec((1, BQ, HW), lambda hh, i: (hh, i, 0))
    o_blk = pl.BlockSpec((BQ, LANES), lambda hh, i: (i, hh))
    lse_blk = pl.BlockSpec((1, BQ, LANES), lambda hh, i: (hh, i, 0))

    def key_block_plan(i, run, unroll):
        def unrolled(u, carry):
            run(unroll * u, unroll, 0)
            return carry

        n_full = QB * i
        lax.fori_loop(0, n_full // unroll, unrolled, 0)
        for rem in range(0, unroll, QB):
            @pl.when(n_full % unroll == rem)
            def _(rem=rem):
                run(n_full - rem, rem + QB, QB)

    def attn_fwd_kern(q_ref, k_ref, v_ref, o_ref, lse_ref, m_sc, acc_sc, s_sc):
        i = pl.program_id(1)
        m_sc[...] = jnp.full(m_sc.shape, NEG, F32)
        acc_sc[...] = jnp.zeros(acc_sc.shape, F32)

        def rows_of(t):
            return pl.ds(pl.multiple_of(t * B, B), B)

        def scores(t, slot, diag):
            rs = live_rows(diag)
            s_sc[slot, rs, :] = _dot_nt(q_ref[0, rs, :], k_ref[0, rows_of(t), :])

        def softmax_pv(t, slot, diag):
            rs = live_rows(diag)
            s = s_sc[slot, rs, :]
            if diag is not None:
                s = diag_mask(s)
            mt = s[:, 0:LANES]
            for cb in range(1, B // LANES):
                mt = jnp.maximum(mt, s[:, cb * LANES:(cb + 1) * LANES])
            m_prev = m_sc[rs, :]
            m_new = jnp.maximum(m_prev, jnp.max(mt, axis=1, keepdims=True))
            alpha = jnp.exp2(m_prev - m_new)
            p = jnp.concatenate([jnp.exp2(s[:, cb * LANES:(cb + 1) * LANES] - m_new).astype(_BF)
                                 for cb in range(B // LANES)], axis=1)
            v_ones = jnp.concatenate([v_ref[rows_of(t), :], jnp.ones((B, LANES), _BF)], axis=1)
            acc_sc[rs, :] = jnp.concatenate([alpha, alpha], axis=1) * acc_sc[rs, :] + _dot(p, v_ones)
            m_sc[rs, :] = m_new

        scores(0, 0, None)

        def run(first, count, n_diag):
            def diag_of(u):
                return u - (count - n_diag) if count - n_diag <= u < count else None

            for u in range(count):
                if u + 1 < count or n_diag == 0:
                    scores(first + u + 1, (u + 1) % 2, diag_of(u + 1))
                softmax_pv(first + u, u % 2, diag_of(u))

        key_block_plan(i, run, ATT_UNROLL_FWD)
        acc = acc_sc[...]
        o_ref[...] = (acc[:, :LANES] / acc[:, LANES:]).astype(o_ref.dtype)
        lse_ref[0] = m_sc[...] + jnp.log2(acc[:, LANES:])

    mla_y, lse = pl.pallas_call(
        attn_fwd_kern, name="attn_fwd", grid=(H, nq),
        in_specs=[q_blk, k_head, v_head], out_specs=[o_blk, lse_blk],
        out_shape=[jax.ShapeDtypeStruct((N, c.HV), _BF), jax.ShapeDtypeStruct((H, N, LANES), F32)],
        scratch_shapes=[pltpu.VMEM((BQ, LANES), F32), pltpu.VMEM((BQ, HW), F32), pltpu.VMEM((2, BQ, B), F32)],
        compiler_params=_cparams(("parallel", "arbitrary")),
    )(q_cat, k_cat, kv)

    HV = c.HV
    mz_blk = c.o_mz // HV

    def gate_fwd_kern(y_ref, z_ref, o_ref):
        z = z_ref[...].astype(F32)
        o_ref[...] = (y_ref[...].astype(F32) * (z * _sig(z))).astype(o_ref.dtype)

    a_mla = pl.pallas_call(
        gate_fwd_kern, name="gate_mla", grid=(nr2,),
        in_specs=[rowb2(HV, 0), rowb2(HV, mz_blk)], out_specs=rowb2(HV, 0),
        out_shape=jax.ShapeDtypeStruct((N, HV), _BF), compiler_params=_cparams(("parallel",)),
    )(mla_y, proj)
    o_mla = _mm(a_mla, wmo, name="mm_omla", out_dtype=_BF)

    M, MW, MH, MHD = c.M, c.MW, c.MH, c.MHD
    memn = pl.pallas_call(
        make_rms_kern(), name="rms_mem",
        grid=(1,), in_specs=[fullb((M, D)), fullb((1, D))], out_specs=fullb((M, D)),
        out_shape=jax.ShapeDtypeStruct((M, D), _BF), compiler_params=_cparams(("arbitrary",)),
    )(memx, mem_norm_g)
    kvm = _mm(memn, wmkv, name="mm_memkv", out_dtype=F32)

    def memk_fwd_kern(kv_ref, g_ref, k_ref, v_ref):
        for hh in range(MH):
            kh, _ = _rms(kv_ref[:, hh * MHD:(hh + 1) * MHD], MHD)
            k_ref[:, hh * MHD:(hh + 1) * MHD] = (kh * g_ref[...]).astype(k_ref.dtype)
        v_ref[...] = kv_ref[:, MW:].astype(v_ref.dtype)

    mem_k, mem_v = pl.pallas_call(
        memk_fwd_kern, name="memk_fwd", grid=(1,),
        in_specs=[fullb((M, 2 * MW)), fullb((1, MHD))], out_specs=[fullb((M, MW)), fullb((M, MW))],
        out_shape=[jax.ShapeDtypeStruct((M, MW), _BF)] * 2, compiler_params=_cparams(("arbitrary",)),
    )(kvm, mem_kn_g)

    mem_blk = c.o_mem // (2 * MW)

    def mem_head(qz_ref, k_ref, v_ref, g_ref, hh):
        sl = slice(hh * MHD, (hh + 1) * MHD)
        qh, r = _rms(qz_ref[:, sl].astype(F32), MHD)
        qn = (qh * g_ref[...]).astype(_BF)
        s = _dot_nt(qn, k_ref[:, sl]) * c.mscale
        e = jnp.exp(s - jnp.max(s, axis=1, keepdims=True))
        p = e / jnp.sum(e, axis=1, keepdims=True)
        y = _dot(p.astype(_BF), v_ref[:, sl])
        z = qz_ref[:, MW + hh * MHD:MW + (hh + 1) * MHD].astype(F32)
        return sl, qh, r, qn, p, y, z

    def mem_fwd_kern(qz_ref, k_ref, v_ref, g_ref, o_ref):
        for hh in range(MH):
            sl, _, _, _, _, y, z = mem_head(qz_ref, k_ref, v_ref, g_ref, hh)
            o_ref[:, sl] = (y * (z * _sig(z))).astype(o_ref.dtype)

    a_mem = pl.pallas_call(
        mem_fwd_kern, name="mem_fwd", grid=(N // RM,),
        in_specs=[rowb(2 * MW, mem_blk, RM), fullb((M, MW)), fullb((M, MW)), fullb((1, MHD))],
        out_specs=rowb(MW, 0, RM), out_shape=jax.ShapeDtypeStruct((N, MW), _BF),
        compiler_params=_cparams(("parallel",)),
    )(proj, mem_k, mem_v, mem_qn_g)
    o_mem = _mm(a_mem, wmemoT, tb=True, name="mm_omem", out_dtype=_BF)

    g_blk = c.o_g // (3 * D)

    def merge_fwd_kern(g_ref, oc_ref, om_ref, ome_ref, o_ref):
        g = g_ref[...].astype(F32)
        acc = _sig(g[:, :D]) * oc_ref[...].astype(F32)
        acc += _sig(g[:, D:2 * D]) * om_ref[...].astype(F32)
        acc += _sig(g[:, 2 * D:]) * ome_ref[...].astype(F32)
        o_ref[...] = acc.astype(o_ref.dtype)

    merged = pl.pallas_call(
        merge_fwd_kern, name="merge_fwd", grid=(nr,),
        in_specs=[rowb(3 * D, g_blk), rowb(D, 0), rowb(D, 0), rowb(D, 0)], out_specs=rowb(D, 0),
        out_shape=jax.ShapeDtypeStruct((N, D), _BF), compiler_params=_cparams(("parallel",)),
    )(proj, o_conv, o_mla, o_mem)
    obm, obn = _tile(N, 512, 16), _tile(D, 1024, LANES)

    def out_loss_kern(a_ref, w_ref, x_ref, t_ref, dyb_ref, l_ref):
        e = x_ref[...] + _dot(a_ref[...], w_ref[...]) - t_ref[...]
        dyb_ref[...] = (e * (1.0 / D)).astype(dyb_ref.dtype)
        row = lax.broadcasted_iota(jnp.int32, l_ref.shape, 0)
        l_ref[...] = jnp.where(row == 0, jnp.sum(e * e, axis=0, keepdims=True), 0.0)

    oblk = pl.BlockSpec((obm, obn), lambda i, j: (i, j))
    dyb, lpart = pl.pallas_call(
        out_loss_kern, name="mm_out_loss", grid=(N // obm, D // obn),
        in_specs=[pl.BlockSpec((obm, D), lambda i, j: (i, 0)), pl.BlockSpec((D, obn), lambda i, j: (0, j)), oblk, oblk],
        out_specs=[oblk, pl.BlockSpec((8, obn), lambda i, j: (i, j))],
        out_shape=[jax.ShapeDtypeStruct((N, D), _BF), jax.ShapeDtypeStruct((8 * (N // obm), D), F32)],
        compiler_params=_cparams(("parallel", "parallel")),
    )(merged, wo, xs, tgt)
    loss = lax.psum(jnp.sum(lpart) * (0.5 / D), AXES)

    G = {}
    d_merged = _mm(dyb, wo, tb=True, name="mm_dmerged", out_dtype=_BF)
    G['w_o'] = _mm(merged, dyb, ta=True, name="mm_dwo", out_dtype=_BF, bk=4096)

    dproj0 = lax.empty((N, c.P), _BF)
    any_spec = pl.BlockSpec(memory_space=pl.ANY)

    def merge_bwd_kern(dp_any, g_ref, dm_ref, oc_ref, om_ref, ome_ref, dg_ref, doc_ref, dom_ref, dome_ref):
        g = g_ref[...].astype(F32)
        dm = dm_ref[...].astype(F32)
        for idx, (o_in, d_out) in enumerate(((oc_ref, doc_ref), (om_ref, dom_ref), (ome_ref, dome_ref))):
            sg = _sig(g[:, idx * D:(idx + 1) * D])
            d_out[...] = (sg * dm).astype(d_out.dtype)
            dg_ref[:, idx * D:(idx + 1) * D] = (dm * o_in[...].astype(F32) * sg * (1.0 - sg)).astype(dg_ref.dtype)

    dproj1, d_oconv, d_omla, d_omem = pl.pallas_call(
        merge_bwd_kern, name="merge_bwd", grid=(nr,),
        in_specs=[any_spec, rowb(3 * D, g_blk), rowb(D, 0), rowb(D, 0), rowb(D, 0), rowb(D, 0)],
        out_specs=[rowb(3 * D, g_blk), rowb(D, 0), rowb(D, 0), rowb(D, 0)],
        out_shape=[jax.ShapeDtypeStruct((N, c.P), _BF)] + [jax.ShapeDtypeStruct((N, D), _BF)] * 3,
        input_output_aliases={0: 0}, compiler_params=_cparams(("parallel",)),
    )(dproj0, proj, d_merged, o_conv, o_mla, o_mem)

    G['w_conv_out'] = _mm(a_conv, d_oconv, ta=True, name="mm_dwco", out_dtype=_BF, bk=4096)
    d_aconv = _mm(d_oconv, wcoT, name="mm_daconv", out_dtype=_BF)
    G['w_mla_out'] = _mm(a_mla, d_omla, ta=True, name="mm_dwmo", out_dtype=_BF, bk=4096)
    d_amla = _mm(d_omla, wmo, tb=True, name="mm_damla", out_dtype=_BF)
    G['w_mem_out'] = _mm(a_mem, d_omem, ta=True, name="mm_dwmemo", out_dtype=_BF, bk=4096)
    d_amem = _mm(d_omem, wmemoT, name="mm_damem", out_dtype=_BF)

    def conv_bwd_kern(dp_any, p_ref, prev_ref, next_ref, da_ref, dan_ref, w_ref, o_ref, dw_ref):
        i = pl.program_id(0)
        cg, bg, u, z = conv_parts(p_ref[...])
        pc, _, pu, _ = conv_parts(prev_ref[...])
        _, nbg, _, nz = conv_parts(next_ref[...])
        cu = cg * u
        sh1, sh2 = shifted(cu, pc * pu, i)
        w = w_ref[...]
        conv = w[0:1, :] * sh2 + w[1:2, :] * sh1 + w[2:3, :] * cu
        sg = _sig(z)
        sz = z * sg
        da = da_ref[...].astype(F32)
        dcy = da * sz
        d_z = da * (bg * conv) * (sg * (1.0 + z * (1.0 - sg)))
        d_b = dcy * conv
        dconv = dcy * bg
        dnext = dan_ref[...].astype(F32) * (nz * _sig(nz)) * nbg
        dnext = jnp.where(i < nr - 1, dnext, 0.0)
        rid = lax.broadcasted_iota(jnp.int32, cu.shape, 0)
        up1 = jnp.where(rid == R - 1, dnext[0:1, :], pltpu.roll(dconv, R - 1, 0))
        up2 = jnp.where(rid == R - 2, dnext[0:1, :], jnp.where(rid == R - 1, dnext[1:2, :], pltpu.roll(dconv, R - 2, 0)))
        dcu = w[2:3, :] * dconv + w[1:2, :] * up1 + w[0:1, :] * up2
        o_ref[:, 0:CW] = (dcu * u).astype(o_ref.dtype)
        o_ref[:, CW:2 * CW] = d_b.astype(o_ref.dtype)
        o_ref[:, 2 * CW:3 * CW] = (dcu * cg).astype(o_ref.dtype)
        o_ref[:, 3 * CW:4 * CW] = d_z.astype(o_ref.dtype)

        @pl.when(i == 0)
        def _():
            dw_ref[...] = jnp.zeros(dw_ref.shape, F32)

        dw_ref[0:1, :] += jnp.sum(dconv * sh2, axis=0, keepdims=True)
        dw_ref[1:2, :] += jnp.sum(dconv * sh1, axis=0, keepdims=True)
        dw_ref[2:3, :] += jnp.sum(dconv * cu, axis=0, keepdims=True)

    next_spec = pl.BlockSpec((HALO, 4 * CW), lambda i: (jnp.minimum((i + 1) * rh, N // HALO - 1), conv_blk))
    dan_spec = pl.BlockSpec((HALO, CW), lambda i: (jnp.minimum((i + 1) * rh, N // HALO - 1), 0))
    dproj2, g_convw = pl.pallas_call(
        conv_bwd_kern, name="conv_bwd", grid=(nr,),
        in_specs=[any_spec, rowb(4 * CW, conv_blk), prev_spec, next_spec, rowb(CW, 0), dan_spec, fullb((8, CW))],
        out_specs=[rowb(4 * CW, conv_blk), fullb((8, CW))],
        out_shape=[jax.ShapeDtypeStruct((N, c.P), _BF), jax.ShapeDtypeStruct((8, CW), F32)],
        input_output_aliases={0: 0}, compiler_params=_cparams(("arbitrary",)),
    )(dproj1, proj, proj, proj, d_aconv, d_aconv, convw8)

    def mem_bwd_kern(dp_any, qz_ref, da_ref, k_ref, v_ref, g_ref, o_ref, dk_ref, dv_ref, dg_ref):
        @pl.when(pl.program_id(0) == 0)
        def _():
            dk_ref[...] = jnp.zeros(dk_ref.shape, F32)
            dv_ref[...] = jnp.zeros(dv_ref.shape, F32)
            dg_ref[...] = jnp.zeros(dg_ref.shape, F32)

        for hh in range(MH):
            sl, qh, r, qn, p, y, z = mem_head(qz_ref, k_ref, v_ref, g_ref, hh)
            da = da_ref[:, sl].astype(F32)
            sg = _sig(z)
            dyh = da * (z * sg)
            o_ref[:, MW + hh * MHD:MW + (hh + 1) * MHD] = (da * y * (sg * (1.0 + z * (1.0 - sg)))).astype(o_ref.dtype)
            dyb_h = dyh.astype(_BF)
            dpm = _dot_nt(dyb_h, v_ref[:, sl])
            ds = (p * (dpm - jnp.sum(dpm * p, axis=1, keepdims=True)) * c.mscale).astype(_BF)
            dqn = _dot(ds, k_ref[:, sl])
            dk_ref[:, sl] += _dot_tn(ds, qn)
            dv_ref[:, sl] += _dot_tn(p.astype(_BF), dyb_h)
            dq, dgp = _rms_bwd(qh, r, g_ref[...], dqn, MHD)
            o_ref[:, sl] = dq.astype(o_ref.dtype)
            dg_ref[...] += jnp.sum(dgp, axis=0, keepdims=True)

    dproj3, d_memk, d_memv, g_mem_qn = pl.pallas_call(
        mem_bwd_kern, name="mem_bwd", grid=(N // RM,),
        in_specs=[any_spec, rowb(2 * MW, mem_blk, RM), rowb(MW, 0, RM), fullb((M, MW)), fullb((M, MW)), fullb((1, MHD))],
        out_specs=[rowb(2 * MW, mem_blk, RM), fullb((M, MW)), fullb((M, MW)), fullb((1, MHD))],
        out_shape=[jax.ShapeDtypeStruct((N, c.P), _BF), jax.ShapeDtypeStruct((M, MW), F32),
                   jax.ShapeDtypeStruct((M, MW), F32), jax.ShapeDtypeStruct((1, MHD), F32)],
        input_output_aliases={0: 0}, compiler_params=_cparams(("arbitrary",)),
    )(dproj2, proj, d_amem, mem_k, mem_v, mem_qn_g)

    def memk_bwd_kern(kv_ref, dk_ref, dv_ref, g_ref, o_ref, dg_ref):
        dg = jnp.zeros((1, MHD), F32)
        for hh in range(MH):
            sl = slice(hh * MHD, (hh + 1) * MHD)
            kh, r = _rms(kv_ref[:, sl], MHD)
            dkr, dgp = _rms_bwd(kh, r, g_ref[...], dk_ref[:, sl], MHD)
            o_ref[:, sl] = dkr.astype(o_ref.dtype)
            dg += jnp.sum(dgp, axis=0, keepdims=True)
        o_ref[:, MW:] = dv_ref[...].astype(o_ref.dtype)
        dg_ref[...] = dg

    d_kvm, g_mem_kn = pl.pallas_call(
        memk_bwd_kern, name="memk_bwd", grid=(1,),
        in_specs=[fullb((M, 2 * MW)), fullb((M, MW)), fullb((M, MW)), fullb((1, MHD))],
        out_specs=[fullb((M, 2 * MW)), fullb((1, MHD))],
        out_shape=[jax.ShapeDtypeStruct((M, 2 * MW), _BF), jax.ShapeDtypeStruct((1, MHD), F32)],
        compiler_params=_cparams(("arbitrary",)),
    )(kvm, d_memk, d_memv, mem_kn_g)
    G['w_mem_kv'] = _mm(memn, d_kvm, ta=True, name="mm_dwmkv", out_dtype=_BF)
    d_memn = _mm(d_kvm, wmkv, tb=True, name="mm_dmemn", out_dtype=F32)

    def memnorm_bwd_kern(x_ref, d_ref, dg_ref):
        xh, _ = _rms(x_ref[...], D)
        dg_ref[...] = jnp.sum(d_ref[...] * xh, axis=0, keepdims=True)

    g_mem_norm = pl.pallas_call(
        memnorm_bwd_kern, name="memnorm_bwd", grid=(1,),
        in_specs=[fullb((M, D)), fullb((M, D))], out_specs=fullb((1, D)),
        out_shape=jax.ShapeDtypeStruct((1, D), F32), compiler_params=_cparams(("arbitrary",)),
    )(memx, d_memn)

    def gate_bwd_kern(dp_any, da_ref, y_ref, z_ref, dy_ref, dz_ref):
        z = z_ref[...].astype(F32)
        da = da_ref[...].astype(F32)
        sg = _sig(z)
        dy_ref[...] = (da * (z * sg)).astype(dy_ref.dtype)
        dz_ref[...] = (da * y_ref[...].astype(F32) * (sg * (1.0 + z * (1.0 - sg)))).astype(dz_ref.dtype)

    d_mlay, dproj4 = pl.pallas_call(
        gate_bwd_kern, name="gate_mla_bwd", grid=(nr2,),
        in_specs=[any_spec, rowb2(HV, 0), rowb2(HV, 0), rowb2(HV, mz_blk)],
        out_specs=[rowb2(HV, 0), rowb2(HV, mz_blk)],
        out_shape=[jax.ShapeDtypeStruct((N, HV), _BF), jax.ShapeDtypeStruct((N, c.P), _BF)],
        input_output_aliases={0: 1}, compiler_params=_cparams(("parallel",)),
    )(dproj3, d_amla, mla_y, proj)

    def attn_bwd_kern(q_ref, k_ref, v_ref, o_ref, do_ref, lse_ref, dq_ref, dk_ref, dv_ref, dq_sc, dl_sc, dk_sc, dv_sc):
        i = pl.program_id(1)
        delta = jnp.sum(do_ref[...].astype(F32) * o_ref[...].astype(F32), axis=1, keepdims=True)
        dl_sc[...] = jnp.broadcast_to(delta, dl_sc.shape)
        dq_sc[...] = jnp.zeros(dq_sc.shape, F32)

        def step(t, diag):
            rows = pl.ds(pl.multiple_of(t * B, B), B)
            rs = live_rows(diag)
            q, do = q_ref[0, rs, :], do_ref[rs, :]
            k = k_ref[0, rows, :]
            s = _dot_nt(q, k)
            if diag is not None:
                s = diag_mask(s)
            dpm = _dot_nt(do, v_ref[rows, :])
            lse_t, dl = lse_ref[0, rs, :], dl_sc[rs, :]
            ps, dss = [], []
            for cb in range(B // LANES):
                cols = slice(cb * LANES, (cb + 1) * LANES)
                p_cb = jnp.exp2(s[:, cols] - lse_t)
                ps.append(p_cb.astype(_BF))
                dss.append((p_cb * (dpm[:, cols] - dl)).astype(_BF))
            p, ds = jnp.concatenate(ps, axis=1), jnp.concatenate(dss, axis=1)
            dvp = _dot_tn(p, do)
            dkp = _dot_tn(ds, q)
            if diag is not None:
                dk_sc[rows, :] = dkp
                dv_sc[rows, :] = dvp
            else:
                dk_sc[rows, :] += dkp
                dv_sc[rows, :] += dvp
            dq_sc[rs, :] += _dot(ds, k)

        def run(first, count, n_diag):
            for u in range(count):
                step(first + u, u - (count - n_diag) if u >= count - n_diag else None)

        key_block_plan(i, run, ATT_UNROLL_BWD)
        dq_ref[0] = dq_sc[...].astype(dq_ref.dtype)

        @pl.when(i == nq - 1)
        def _():
            dk_ref[0] = dk_sc[...].astype(dk_ref.dtype)
            dv_ref[0] = dv_sc[...].astype(dv_ref.dtype)

    d_qcat, d_kcat, d_v = pl.pallas_call(
        attn_bwd_kern, name="attn_bwd", grid=(H, nq),
        in_specs=[q_blk, k_head, v_head, o_blk, o_blk, lse_blk],
        out_specs=[pl.BlockSpec((1, BQ, HW), lambda hh, i: (hh, i, 0)),
                   pl.BlockSpec((1, N, HW), lambda hh, i: (hh, 0, 0)),
                   pl.BlockSpec((1, N, LANES), lambda hh, i: (hh, 0, 0))],
        out_shape=[jax.ShapeDtypeStruct((H, N, HW), _BF), jax.ShapeDtypeStruct((H, N, HW), _BF),
                   jax.ShapeDtypeStruct((H, N, LANES), _BF)],
        scratch_shapes=[pltpu.VMEM((BQ, HW), F32), pltpu.VMEM((BQ, LANES), F32), pltpu.VMEM((N, HW), F32),
                        pltpu.VMEM((N, LANES), F32)],
        compiler_params=_cparams(("parallel", "arbitrary")),
    )(q_cat, k_cat, kv, mla_y, d_mlay, lse)

    def q_prep_bwd_kern(q_ref, dq_ref, t_ref, gn_ref, gr_ref, o_ref, dgn_ref, dgr_ref):
        @pl.when(jnp.logical_and(pl.program_id(0) == 0, pl.program_id(1) == 0))
        def _():
            dgn_ref[...] = jnp.zeros(dgn_ref.shape, F32)
            dgr_ref[...] = jnp.zeros(dgr_ref.shape, F32)

        for g in range(HG):
            blk = q_ref[:, g * HW:(g + 1) * HW].astype(F32)
            d = dq_ref[g].astype(F32)
            nh, rn = _rms128(blk[:, :LANES], c.NOPE)
            rhat, rr = _rms128(blk[:, LANES:], c.ROPE)
            dn, dgn = _rms128_bwd(nh, rn, gn_ref[...], d[:, :LANES], c.NOPE)
            drot = _rope_t(d[:, LANES:], t_ref[0], t_ref[1], t_ref[2])
            dr, dgr = _rms128_bwd(rhat, rr, gr_ref[...], drot, c.ROPE)
            o_ref[:, g * HW:(g + 1) * HW] = jnp.concatenate([dn, dr], axis=1).astype(o_ref.dtype)
            dgn_ref[...] += jnp.sum(dgn, axis=0, keepdims=True)
            dgr_ref[...] += jnp.sum(dgr, axis=0, keepdims=True)

    d_qp, g_qn_nope, g_qn_rope = pl.pallas_call(
        q_prep_bwd_kern, name="q_prep_bwd", grid=(nrp, nhg),
        in_specs=[heads_in, heads_out, tabs_of(RP), fullb((1, LANES)), fullb((1, LANES))],
        out_specs=[heads_in, fullb((1, LANES)), fullb((1, LANES))],
        out_shape=[jax.ShapeDtypeStruct((N, H * HW), _BF), jax.ShapeDtypeStruct((1, LANES), F32),
                   jax.ShapeDtypeStruct((1, LANES), F32)],
        compiler_params=_cparams(("arbitrary", "arbitrary")),
    )(q_p, d_qcat, tabs, g_qn * c.scale, g_qr * c.scale)
    g_qn_nope, g_qn_rope = g_qn_nope * c.scale, g_qn_rope * c.scale

    def k_prep_bwd_kern(kv_ref, dk_ref, dv_ref, gn_ref, o_ref, dkr_ref, dgn_ref):
        hg = pl.program_id(1)

        @pl.when(jnp.logical_and(pl.program_id(0) == 0, hg == 0))
        def _():
            dgn_ref[...] = jnp.zeros(dgn_ref.shape, F32)

        @pl.when(hg == 0)
        def _():
            dkr_ref[...] = jnp.zeros(dkr_ref.shape, F32)

        dkr = jnp.zeros((RP, LANES), F32)
        for g in range(HG):
            dk = dk_ref[g].astype(F32) * (1.0 / LOG2E)
            kn, r = _rms128(kv_ref[:, g * HW:g * HW + LANES].astype(F32), c.NOPE)
            dkn, dgn = _rms128_bwd(kn, r, gn_ref[...], dk[:, :LANES], c.NOPE)
            o_ref[:, g * HW:(g + 1) * HW] = jnp.concatenate([dkn.astype(o_ref.dtype), dv_ref[g]], axis=1)
            dgn_ref[...] += jnp.sum(dgn, axis=0, keepdims=True)
            dkr += dk[:, LANES:]
        dkr_ref[...] += dkr

    d_kv, d_krsum, g_kn_nope = pl.pallas_call(
        k_prep_bwd_kern, name="k_prep_bwd", grid=(nrp, nhg),
        in_specs=[heads_in, heads_out, pl.BlockSpec((HG, RP, LANES), lambda i, hg: (hg, i, 0)), fullb((1, LANES))],
        out_specs=[heads_in, pl.BlockSpec((RP, LANES), lambda i, hg: (i, 0)), fullb((1, LANES))],
        out_shape=[jax.ShapeDtypeStruct((N, H * HW), _BF), jax.ShapeDtypeStruct((N, LANES), F32),
                   jax.ShapeDtypeStruct((1, LANES), F32)],
        compiler_params=_cparams(("arbitrary", "arbitrary")),
    )(kv, d_kcat, d_v, g_kn)

    def krope_bwd_kern(p_ref, d_ref, t_ref, g_ref, o_ref, dg_ref):
        @pl.when(pl.program_id(0) == 0)
        def _():
            dg_ref[...] = jnp.zeros(dg_ref.shape, F32)

        xh, r = _rms128(p_ref[...].astype(F32), c.ROPE)
        drot = _rope_t(d_ref[...], t_ref[0], t_ref[1], t_ref[2])
        dx, dg = _rms128_bwd(xh, r, g_ref[...], drot, c.ROPE)
        o_ref[...] = dx.astype(o_ref.dtype)
        dg_ref[...] += jnp.sum(dg, axis=0, keepdims=True)

    d_kr, g_kn_rope = pl.pallas_call(
        krope_bwd_kern, name="krope_bwd", grid=(nr,),
        in_specs=[rowb(LANES, 0), rowb(LANES, 0), tabs_of(R), fullb((1, LANES))],
        out_specs=[rowb(LANES, 0), fullb((1, LANES))],
        out_shape=[jax.ShapeDtypeStruct((N, LANES), _BF), jax.ShapeDtypeStruct((1, LANES), F32)],
        compiler_params=_cparams(("arbitrary",)),
    )(kr_raw, d_krsum, tabs, g_kr)
    dproj5 = dproj4

    g_wuq_p = _mm(cqn, d_qp, ta=True, name="mm_dwuq", out_dtype=_BF, bk=4096)
    G['w_uq'] = g_wuq_p.reshape(QL, H, HW)[:, :, :c.NOPE + c.ROPE].reshape(QL, H * (c.NOPE + c.ROPE))
    d_cqn = _mm(d_qp, wuq_pT, name="mm_dcqn", out_dtype=F32, bk=4096)
    G['w_ukv'] = _mm(ckvn, d_kv, ta=True, name="mm_dwukv", out_dtype=_BF, bk=4096)
    d_ckvn = _mm(d_kv, wukvT, name="mm_dckvn", out_dtype=F32, bk=4096)

    def to_blocks(n, g):
        if n in COL_SHARDED:
            return jnp.transpose(g.reshape(g.shape[0], NDEV, -1), (1, 0, 2))
        return g.reshape(NDEV, -1, g.shape[1])

    def landing(b):
        return lax.empty(b.shape, b.dtype)

    early = [n for n in BIG if n != 'w_in']
    blocks_e = [to_blocks(n, G[n]) for n in early]
    xe = _split_start(blocks_e, [landing(b) for b in blocks_e], False, "xchg_early_start")
    gq_after = mla_q_norm_g + xe[4][0:1, 0:1]

    def lora_bwd_kern(dp_any, p_ref, dq_ref, dkv_ref, gq_ref, gkv_ref, o_ref, dgq_ref, dgkv_ref):
        @pl.when(pl.program_id(0) == 0)
        def _():
            dgq_ref[...] = jnp.zeros(dgq_ref.shape, F32)
            dgkv_ref[...] = jnp.zeros(dgkv_ref.shape, F32)

        blk = p_ref[...].astype(F32)
        qh, rq = _rms(blk[:, :QL], QL)
        kh, rk = _rms(blk[:, QL:], KVL)
        dq, dgq = _rms_bwd(qh, rq, gq_ref[...], dq_ref[...], QL)
        dk, dgk = _rms_bwd(kh, rk, gkv_ref[...], dkv_ref[...], KVL)
        o_ref[:, :QL] = dq.astype(o_ref.dtype)
        o_ref[:, QL:] = dk.astype(o_ref.dtype)
        dgq_ref[...] += jnp.sum(dgq, axis=0, keepdims=True)
        dgkv_ref[...] += jnp.sum(dgk, axis=0, keepdims=True)

    dproj6, g_q_norm, g_kv_norm = pl.pallas_call(
        lora_bwd_kern, name="lora_bwd", grid=(nr,),
        in_specs=[any_spec, rowb(QL + KVL, lora_blk), rowb(QL, 0), rowb(KVL, 0), fullb((1, QL)), fullb((1, KVL))],
        out_specs=[rowb(QL + KVL, lora_blk), fullb((1, QL)), fullb((1, KVL))],
        out_shape=[jax.ShapeDtypeStruct((N, c.P), _BF), jax.ShapeDtypeStruct((1, QL), F32),
                   jax.ShapeDtypeStruct((1, KVL), F32)],
        input_output_aliases={0: 0}, compiler_params=_cparams(("arbitrary",)),
    )(dproj5, proj, d_cqn, d_ckvn, gq_after, mla_kv_norm_g)

    g_win_p = _mm(h, dproj6, ta=True, name="mm_dwin", out_dtype=_BF, bk=4096)
    g_wkr = _mm(h, d_kr, ta=True, name="mm_dwkr", out_dtype=_BF, bk=4096)
    blocks_w = [_win_blocks(g_win_p, g_wkr, c)]
    xw = _split_start(blocks_w, [landing(b) for b in blocks_w], False, "xchg_win_start")
    d_h = _mm(dproj6, win_pT, name="mm_dh", out_dtype=F32, bk=3072, after=xw[4], plus=(d_kr, w_krT))

    def final_bwd_kern(x_ref, g_ref, dh_ref, dy_ref, gx_ref, dg_ref):
        @pl.when(pl.program_id(0) == 0)
        def _():
            dg_ref[...] = jnp.zeros(dg_ref.shape, F32)

        xh, r = _rms(x_ref[...], D)
        dx, dg = _rms_bwd(xh, r, g_ref[...], dh_ref[...], D)
        gx_ref[...] = dy_ref[...].astype(F32) + dx
        dg_ref[...] += jnp.sum(dg, axis=0, keepdims=True)

    grad_x, g_norm = pl.pallas_call(
        final_bwd_kern, name="final_bwd", grid=(nr2,),
        in_specs=[rowb2(D, 0), fullb((1, D)), rowb2(D, 0), rowb2(D, 0)],
        out_specs=[rowb2(D, 0), fullb((1, D))],
        out_shape=[jax.ShapeDtypeStruct((N, D), F32), jax.ShapeDtypeStruct((1, D), F32)],
        compiler_params=_cparams(("arbitrary",)),
    )(xs, norm_g, d_h, dyb)

    res = [{}, {}, {}, {}]

    def adam_into(n, parts):
        outs = _adam(parts, W[n][0], Mo[n][0], Vo[n][0], "adam_" + n)
        for k in range(4):
            res[k][n] = outs[k][None]
        return outs[0]

    recv_e = _split_wait(xe, False, grad_x, "xchg_early_wait")
    last = [adam_into(n, parts) for n, parts in zip(early, recv_e)][-1]
    recv_w = _split_wait(xw, False, last, "xchg_win_wait")
    adam_into('w_in', recv_w[0])

    small_g = {'norm_g': g_norm, 'mla_q_norm_g': g_q_norm, 'mla_kv_norm_g': g_kv_norm,
               'mla_qn_nope_g': g_qn_nope, 'mla_qn_rope_g': g_qn_rope[:, :c.ROPE], 'mla_kn_nope_g': g_kn_nope,
               'mla_kn_rope_g': g_kn_rope[:, :c.ROPE], 'mem_norm_g': g_mem_norm, 'mem_qn_g': g_mem_qn,
               'mem_kn_g': g_mem_kn}
    small_part = _pack([small_g[n] for n in SMALL] + [g_convw[0:3, :]], 0)
    small_all = _all_gather([small_part], "ag_small_grads")[0]
    small_shapes = [W[n].shape for n in SMALL]
    pieces = _unpack(small_all, small_shapes + [(3, CW)])
    cw8 = CW // NDEV
    conv_mine = lax.dynamic_slice_in_dim(pieces[-1].reshape(NDEV, 3, NDEV, cw8), me, 1, axis=2)[:, :, 0, :]
    sm_parts = _pack(pieces[:-1] + [conv_mine], 1)
    sm_names = SMALL + ['conv_w']
    sm_shapes = small_shapes + [(3, cw8)]
    w_sm = _pack([W[n] for n in SMALL] + [conv_w[0]], 0)
    m_sm = _pack([Mo[n] for n in SMALL] + [m_conv_w[0]], 0)
    v_sm = _pack([Vo[n] for n in SMALL] + [v_conv_w[0]], 0)
    outs_sm = [_unpack(o, sm_shapes) for o in _adam(sm_parts, w_sm, m_sm, v_sm, "adam_small")]
    for k in range(4):
        for n, a in zip(sm_names, outs_sm[k]):
            res[k][n] = a[None] if n == 'conv_w' else a
    return (loss, grad_x[None], *[res[0][n] for n in WEIGHTS], *[res[1][n] for n in WEIGHTS],
            *[res[2][n] for n in WEIGHTS], *[res[3][n] for n in WEIGHTS])
```

```python
import math

import jax
import jax.numpy as jnp
from jax import lax
from jax.experimental import pallas as pl
from jax.experimental.pallas import tpu as pltpu

F32 = jnp.float32
_BF = jnp.bfloat16
EPS = 1e-6
CHUNK = 64
ROPE_THETA = 10000.0
ADAM_LR, ADAM_B1, ADAM_B2, ADAM_EPS, ADAM_WD, ADAM_STEP = 0.001, 0.9, 0.999, 1e-08, 0.01, 10
NDEV = 8
AXES = ("x", "y", "c")
MESH = pl.DeviceIdType.MESH
LANES = 128
NEG = -1e30
LOG2E = math.log2(math.e)
V7X_VMEM_LIMIT = 56 * 1024 * 1024
PACK_C = 1024
ATT_BLOCK = 512
ATT_UNROLL_FWD = 8
ATT_UNROLL_BWD = 4
ATT_QBLOCKS = 2
ADAM_BLOCK_ELEMS = 256 * 1024

WEIGHTS = ['norm_g', 'w_in', 'conv_w', 'w_conv_out', 'mla_q_norm_g', 'w_uq', 'mla_kv_norm_g', 'w_ukv',
           'mla_qn_nope_g', 'mla_qn_rope_g', 'mla_kn_nope_g', 'mla_kn_rope_g', 'w_mla_out', 'mem_norm_g',
           'w_mem_kv', 'mem_qn_g', 'mem_kn_g', 'w_mem_out', 'w_o']
BIG = ['w_in', 'w_conv_out', 'w_uq', 'w_ukv', 'w_mla_out', 'w_mem_kv', 'w_mem_out', 'w_o']
COL_SHARDED = ('w_in', 'w_conv_out', 'w_uq', 'w_ukv', 'w_mem_out')
SMALL = ['norm_g', 'mla_q_norm_g', 'mla_kv_norm_g', 'mla_qn_nope_g', 'mla_qn_rope_g', 'mla_kn_nope_g',
         'mla_kn_rope_g', 'mem_norm_g', 'mem_qn_g', 'mem_kn_g']


def _tile(dim, target, align):
    if dim <= target:
        return dim
    t = target - target % align
    while t > 0:
        if dim % t == 0:
            return t
        t -= align
    raise ValueError(f"no tile for {dim} {target} {align}")


def _cparams(sem):
    return pltpu.CompilerParams(dimension_semantics=sem, vmem_limit_bytes=V7X_VMEM_LIMIT)


def _sig(x):
    return 1.0 / (1.0 + jnp.exp(-x))


def _rms(x, n):
    r = lax.rsqrt(jnp.sum(x * x, axis=-1, keepdims=True) * (1.0 / n) + EPS)
    return x * r, r


def _rms_bwd(xhat, r, g, dy, n):
    dxh = dy * g
    dx = r * (dxh - xhat * (jnp.sum(dxh * xhat, axis=-1, keepdims=True) * (1.0 / n)))
    return dx, dy * xhat


def _rowmean128(x, n):
    return jnp.dot(x.astype(_BF), jnp.full((LANES, LANES), 1.0 / n, _BF), preferred_element_type=F32)


def _rms128(x, n):
    r = lax.rsqrt(_rowmean128(x * x, n) + EPS)
    return x * r, r


def _rms128_bwd(xhat, r, g, dy, n):
    dxh = dy * g
    dx = r * (dxh - xhat * _rowmean128(dxh * xhat, n))
    return dx, dy * xhat


def _rope(x, cosp, sina, sinb):
    return x * cosp + pltpu.roll(x, 96, 1) * sina + pltpu.roll(x, 32, 1) * sinb


def _rope_t(d, cosp, sina, sinb):
    return d * cosp + pltpu.roll(d * sina, 32, 1) + pltpu.roll(d * sinb, 96, 1)


def _dot_nt(a, b):
    return lax.dot_general(a, b, (((1,), (1,)), ((), ())), preferred_element_type=F32)


def _dot_tn(a, b):
    return lax.dot_general(a, b, (((0,), (0,)), ((), ())), preferred_element_type=F32)


def _dot(a, b):
    return jnp.dot(a, b, preferred_element_type=F32)


def _all_gather(shards, name):
    na = len(shards)
    nc = 9
    halves = [s.shape[0] // 32 * 16 if s.shape[0] >= 256 else None for s in shards]

    def body(*refs):
        x_refs, out_refs = refs[:na], refs[na:2 * na]
        send_sems, recv_sems, local_sems = refs[2 * na:]
        x, y, c = lax.axis_index("x"), lax.axis_index("y"), lax.axis_index("c")
        me, sib = (x, y, c), (x, y, 1 - c)
        px, py, pd = (1 - x, y, c), (x, 1 - y, c), (1 - x, 1 - y, c)

        def other_core(p):
            return (p[0], p[1], 1 - p[2])

        def rows(a, blk, part=None):
            r = out_refs[a].at[4 * blk[0] + 2 * blk[1] + blk[2]]
            if part is None or halves[a] is None:
                return r
            rest = shards[a].shape[0] - halves[a]
            return r.at[pl.ds(0, halves[a])] if part == 0 else r.at[pl.ds(halves[a], rest)]

        def copy(a, k, blk, to, part=None, src=None):
            dst = rows(a, blk, part)
            return pltpu.make_async_remote_copy(
                src_ref=dst if src is None else src, dst_ref=dst,
                send_sem=send_sems.at[nc * a + k], recv_sem=recv_sems.at[nc * a + k],
                device_id=to, device_id_type=MESH)

        mine = [pltpu.make_async_copy(x_refs[a], rows(a, me), local_sems.at[a]) for a in range(na)]
        for cp in mine:
            cp.start()
        started = []
        for a in range(na):
            started += [copy(a, 0, me, px, src=x_refs[a]), copy(a, 1, me, py, src=x_refs[a]),
                        copy(a, 2, me, sib, src=x_refs[a])]
        for cp in started:
            cp.start()

        def forward(cp):
            cp.start()
            started.append(cp)

        for a in range(na):
            copy(a, 0, px, me).wait_recv()
            forward(copy(a, 3, px, py, part=0))
            forward(copy(a, 4, px, sib))
        for a in range(na):
            copy(a, 1, py, me).wait_recv()
            if halves[a] is not None:
                forward(copy(a, 5, py, px, part=1))
            forward(copy(a, 6, py, sib))
        for a in range(na):
            copy(a, 3, pd, me, part=0).wait_recv()
            forward(copy(a, 7, pd, sib, part=0))
            if halves[a] is not None:
                copy(a, 5, pd, me, part=1).wait_recv()
                forward(copy(a, 8, pd, sib, part=1))
        for a in range(na):
            copy(a, 2, sib, me).wait_recv()
            copy(a, 4, other_core(px), me).wait_recv()
            copy(a, 6, other_core(py), me).wait_recv()
            copy(a, 7, other_core(pd), me, part=0).wait_recv()
            if halves[a] is not None:
                copy(a, 8, other_core(pd), me, part=1).wait_recv()
        for cp in started:
            cp.wait_send()
        for cp in mine:
            cp.wait()

    any_spec = pl.BlockSpec(memory_space=pl.ANY)
    return pl.pallas_call(
        body, name=name,
        out_shape=[jax.ShapeDtypeStruct((NDEV,) + s.shape, s.dtype) for s in shards],
        in_specs=[any_spec] * na, out_specs=[any_spec] * na,
        scratch_shapes=[pltpu.SemaphoreType.DMA((nc * na,)), pltpu.SemaphoreType.DMA((nc * na,)),
                        pltpu.SemaphoreType.DMA((na,))],
    )(*shards)


_HBM = pl.BlockSpec(memory_space=pltpu.HBM)
_SEM = pl.BlockSpec(memory_space=pltpu.SEMAPHORE)
_EFFECT = pltpu.SideEffectType.DATAFLOW_SIDE_EFFECTING


def _split_copy(a, k, src_refs, land_refs, send_sems, recv_sems, gather, receive_side):
    x, y, c = lax.axis_index("x"), lax.axis_index("y"), lax.axis_index("c")
    me = 4 * x + 2 * y + c
    tx, ty, tc = x ^ ((k + 1) >> 2 & 1), y ^ ((k + 1) >> 1 & 1), c ^ ((k + 1) & 1)
    peer = 4 * tx + 2 * ty + tc
    return pltpu.make_async_remote_copy(
        src_ref=src_refs[a] if gather else src_refs[a].at[peer],
        dst_ref=land_refs[a].at[peer if receive_side else me],
        send_sem=send_sems.at[7 * a + k], recv_sem=recv_sems.at[7 * a + k],
        device_id=(tx, ty, tc), device_id_type=MESH)


def _own_copy(a, na, src_refs, land_refs, send_sems, gather):
    me = 4 * lax.axis_index("x") + 2 * lax.axis_index("y") + lax.axis_index("c")
    return pltpu.make_async_copy(src_refs[a] if gather else src_refs[a].at[me], land_refs[a].at[me],
                                 send_sems.at[7 * na + a])


def _split_start(srcs, lands, gather, name, after=None):
    na = len(srcs)
    extra = [] if after is None else [after]

    def body(*refs):
        src_refs, land_refs = refs[:na], refs[na:2 * na]
        send_sems, recv_sems = refs[2 * na + len(extra)], refs[2 * na + len(extra) + 1]
        token = refs[-1]
        for k in range(7):
            for a in range(na):
                _split_copy(a, k, src_refs, land_refs, send_sems, recv_sems, gather, False).start()
        for a in range(na):
            _own_copy(a, na, src_refs, land_refs, send_sems, gather).start()
        token[...] = jnp.zeros_like(token)

    hbm = [pltpu.HBM(b.shape, b.dtype) for b in list(srcs) + list(lands)]
    outs = pl.pallas_call(
        body, name=name,
        out_shape=(pltpu.SemaphoreType.DMA((8 * na,)), pltpu.SemaphoreType.DMA((7 * na,)), *hbm,
                   jax.ShapeDtypeStruct((8, LANES), F32)),
        in_specs=[_HBM] * (2 * na) + [pl.BlockSpec(memory_space=pl.ANY)] * len(extra),
        out_specs=(_SEM, _SEM, *[_HBM] * (2 * na), pl.BlockSpec(memory_space=pltpu.VMEM)),
        input_output_aliases={j: 2 + j for j in range(2 * na)},
        compiler_params=pltpu.CompilerParams(has_side_effects=_EFFECT),
    )(*[pltpu.with_memory_space_constraint(b, pltpu.HBM) for b in srcs],
      *[pltpu.with_memory_space_constraint(l, pltpu.HBM) for l in lands], *extra)
    return outs[0], outs[1], outs[2:2 + na], outs[2 + na:2 + 2 * na], outs[-1]


def _split_wait(started, gather, after, name):
    send_sems, recv_sems, srcs, lands, _ = started
    na = len(srcs)

    def body(*refs):
        src_refs, land_refs = refs[:na], refs[na:2 * na]
        send_s, recv_s = refs[2 * na], refs[2 * na + 1]
        for k in range(7):
            for a in range(na):
                cp = _split_copy(a, k, src_refs, land_refs, send_s, recv_s, gather, True)
                cp.wait_send()
                cp.wait_recv()
        for a in range(na):
            _own_copy(a, na, src_refs, land_refs, send_s, gather).wait()

    hbm = [pltpu.HBM(b.shape, b.dtype) for b in list(srcs) + list(lands)]
    outs = pl.pallas_call(
        body, name=name, out_shape=tuple(hbm),
        in_specs=[_HBM] * (2 * na) + [_SEM, _SEM, pl.BlockSpec(memory_space=pl.ANY)],
        out_specs=tuple([_HBM] * (2 * na)),
        input_output_aliases={j: j for j in range(2 * na)},
        compiler_params=pltpu.CompilerParams(has_side_effects=_EFFECT),
    )(*srcs, *lands, send_sems, recv_sems, after)
    return outs[na:]


def _seg_rows(size):
    rows = -(-size // PACK_C)
    return -(-rows // 16) * 16


def _pack(arrs, lead):
    parts = []
    for a in arrs:
        lshape = a.shape[:lead]
        f = a.reshape(lshape + (-1,)).astype(F32)
        rows = _seg_rows(f.shape[-1])
        f = jnp.pad(f, [(0, 0)] * lead + [(0, rows * PACK_C - f.shape[-1])])
        parts.append(f.reshape(lshape + (rows, PACK_C)))
    return jnp.concatenate(parts, axis=lead)


def _unpack(buf, shapes):
    lshape = buf.shape[:-2]
    out, r = [], 0
    for shp in shapes:
        size = math.prod(shp)
        rows = _seg_rows(size)
        seg = buf[..., r:r + rows, :].reshape(lshape + (rows * PACK_C,))[..., :size]
        out.append(seg.reshape(lshape + tuple(shp)))
        r += rows
    return out


def _mm(a, b, *, name, out_dtype, ta=False, tb=False, bm=1024, bn=1024, bk=2048, after=None, plus=None):
    if ta:
        kdim, m = a.shape
    else:
        m, kdim = a.shape
    n, k2 = b.shape if tb else b.shape[::-1]
    assert kdim == k2 and not (ta and tb), (a.shape, b.shape)
    bm = _tile(m, bm, LANES if ta else 16)
    bn = _tile(n, bn, LANES)
    bk = _tile(kdim, bk, LANES)
    nk = kdim // bk
    n_after = 0 if after is None else 1
    n_plus = 0 if plus is None else 2

    def kern(a_ref, b_ref, *rest):
        plus_refs = rest[n_after:n_after + n_plus]
        o_ref, scratch = rest[n_after + n_plus], rest[n_after + n_plus + 1:]
        part = (_dot_tn if ta else _dot_nt if tb else _dot)(a_ref[...], b_ref[...])

        def first(p):
            return p + _dot(plus_refs[0][...], plus_refs[1][...]) if plus is not None else p

        if nk == 1:
            o_ref[...] = first(part).astype(o_ref.dtype)
        else:
            acc = scratch[0] if scratch else o_ref
            k = pl.program_id(2)

            @pl.when(k == 0)
            def _():
                acc[...] = first(jnp.zeros(acc.shape, F32))

            acc[...] += part
            if scratch:
                @pl.when(k == nk - 1)
                def _():
                    o_ref[...] = acc[...].astype(o_ref.dtype)

    a_spec = pl.BlockSpec((bk, bm), lambda i, j, k: (k, i)) if ta else pl.BlockSpec((bm, bk), lambda i, j, k: (i, k))
    b_spec = pl.BlockSpec((bn, bk), lambda i, j, k: (j, k)) if tb else pl.BlockSpec((bk, bn), lambda i, j, k: (k, j))
    extra_specs, extra_args = [], []
    if after is not None:
        extra_specs.append(pl.BlockSpec(after.shape, lambda i, j, k: (0, 0)))
        extra_args.append(after)
    if plus is not None:
        kk = plus[0].shape[1]
        extra_specs += [pl.BlockSpec((bm, kk), lambda i, j, k: (i, 0)), pl.BlockSpec((kk, bn), lambda i, j, k: (0, j))]
        extra_args += list(plus)
    return pl.pallas_call(
        kern, name=name, grid=(m // bm, n // bn, nk),
        in_specs=[a_spec, b_spec] + extra_specs,
        out_specs=pl.BlockSpec((bm, bn), lambda i, j, k: (i, j)),
        out_shape=jax.ShapeDtypeStruct((m, n), out_dtype),
        scratch_shapes=[pltpu.VMEM((bm, bn), F32)] if nk > 1 and out_dtype != F32 else [],
        compiler_params=_cparams(("parallel", "parallel", "arbitrary")),
    )(a, b, *extra_args)


def _adam(parts, w_a, m_a, v_a, name):
    rows, cols = w_a.shape
    rb = _tile(rows, max(8, ADAM_BLOCK_ELEMS // cols // 8 * 8), 8)
    bc1 = 1.0 - ADAM_B1 ** ADAM_STEP
    bc2 = 1.0 - ADAM_B2 ** ADAM_STEP

    def adam_kern(p_ref, w_ref, m_ref, v_ref, g_ref, d_ref, nm_ref, nv_ref):
        g = p_ref[0].astype(F32)
        for j in range(1, NDEV):
            g = g + p_ref[j].astype(F32)
        m_new = ADAM_B1 * m_ref[...] + (1.0 - ADAM_B1) * g
        v_new = ADAM_B2 * v_ref[...] + (1.0 - ADAM_B2) * (g * g)
        g_ref[...] = g
        nm_ref[...] = m_new
        nv_ref[...] = v_new
        d_ref[...] = -ADAM_LR * ((m_new / bc1) / (jnp.sqrt(v_new / bc2) + ADAM_EPS) + ADAM_WD * w_ref[...])

    blk = pl.BlockSpec((rb, cols), lambda i: (i, 0))
    return pl.pallas_call(
        adam_kern, name=name, grid=(rows // rb,),
        in_specs=[pl.BlockSpec((NDEV, rb, cols), lambda i: (0, i, 0)), blk, blk, blk],
        out_specs=[blk] * 4, out_shape=[jax.ShapeDtypeStruct((rows, cols), F32)] * 4,
        compiler_params=_cparams(("parallel",)),
    )(parts, w_a, m_a, v_a)


class _Cfg:
    pass


def _config(x, conv_w, w_uq, w_ukv, mla_qn_nope_g, mla_qn_rope_g, mem, mem_qn_g, w_mem_out, w_mla_out):
    c = _Cfg()
    c.N, c.D = x.shape[1], x.shape[2]
    c.CW = conv_w.shape[2] * NDEV
    c.QL, c.KVL = w_uq.shape[1], w_ukv.shape[1]
    c.NOPE, c.ROPE = mla_qn_nope_g.shape[1], mla_qn_rope_g.shape[1]
    c.H = w_uq.shape[2] * NDEV // (c.NOPE + c.ROPE)
    c.V = w_ukv.shape[2] * NDEV // c.H - c.NOPE
    assert c.NOPE == LANES and c.V == LANES and c.ROPE == LANES // 2
    c.HW = 2 * LANES
    c.HV = c.H * c.V
    assert w_mla_out.shape[1] * NDEV == c.HV
    c.M = mem.shape[1]
    c.MHD = mem_qn_g.shape[1]
    c.MW = w_mem_out.shape[1]
    c.MH = c.MW // c.MHD
    c.o_conv = 0
    c.o_mz = 4 * c.CW
    c.o_g = c.o_mz + c.HV
    c.o_mem = c.o_g + 3 * c.D
    c.o_lora = c.o_mem + 2 * c.MW
    c.P = c.o_lora + c.QL + c.KVL
    assert c.o_mz % c.HV == 0 and c.o_g % (3 * c.D) == 0 and c.o_mem % (2 * c.MW) == 0
    assert c.o_lora % (c.QL + c.KVL) == 0 and c.QL % LANES == 0 and c.KVL % LANES == 0
    c.IN = 4 * c.CW + c.QL + c.KVL + c.ROPE + c.HV + 2 * c.MW + 3 * c.D
    c.R = _tile(c.N, 256, 16)
    c.RP = _tile(c.N, 512, 16)
    c.HG = _tile(c.H, 4, 1)
    c.B = _tile(c.N, ATT_BLOCK, CHUNK)
    c.scale = float((c.NOPE + c.ROPE) ** -0.5)
    c.mscale = float(c.MHD ** -0.5)
    return c


def _win_segments(c):
    ref_order = (('conv', 4 * c.CW), ('lora', c.QL + c.KVL), ('kr', c.ROPE), ('mz', c.HV), ('mem', 2 * c.MW), ('g', 3 * c.D))
    mine = {'conv': c.o_conv, 'mz': c.o_mz, 'g': c.o_g, 'mem': c.o_mem, 'lora': c.o_lora, 'kr': 0}
    segs, o = [], 0
    for nm, wd in ref_order:
        segs.append((nm, o, wd, mine[nm]))
        o += wd
    return segs


def _win_split(g_win_t, c):
    n8 = g_win_t.shape[1]

    def rows(a, wd):
        return [g_win_t[j][max(a, j * n8) - j * n8:min(a + wd, (j + 1) * n8) - j * n8]
                for j in range(a // n8, (a + wd - 1) // n8 + 1)]

    segs = {nm: (a, wd) for nm, a, wd, _ in _win_segments(c)}
    main = [p for nm in ('conv', 'mz', 'g', 'mem', 'lora') for p in rows(*segs[nm])]
    kr = jnp.concatenate(rows(*segs['kr']) + [jnp.zeros((LANES - c.ROPE, g_win_t.shape[2]), g_win_t.dtype)], axis=0)
    return jnp.concatenate(main, axis=0), kr


def _win_blocks(g, g_kr, c):
    n8 = c.IN // NDEV
    blocks = []
    for j in range(NDEV):
        lo, hi = j * n8, (j + 1) * n8
        parts = []
        for nm, a, wd, mine in _win_segments(c):
            s, e = max(a, lo), min(a + wd, hi)
            if s < e:
                parts.append((g_kr if nm == 'kr' else g)[:, mine + s - a:mine + e - a])
        blocks.append(jnp.concatenate(parts, axis=1))
    return jnp.stack(blocks, axis=0)


def kernel(x, positions, mem, norm_g, w_in, conv_w, w_conv_out, mla_q_norm_g, w_uq, mla_kv_norm_g, w_ukv, mla_qn_nope_g, mla_qn_rope_g, mla_kn_nope_g, mla_kn_rope_g, w_mla_out, mem_norm_g, w_mem_kv, mem_qn_g, mem_kn_g, w_mem_out, w_o, loss_target, m_norm_g, m_w_in, m_conv_w, m_w_conv_out, m_mla_q_norm_g, m_w_uq, m_mla_kv_norm_g, m_w_ukv, m_mla_qn_nope_g, m_mla_qn_rope_g, m_mla_kn_nope_g, m_mla_kn_rope_g, m_w_mla_out, m_mem_norm_g, m_w_mem_kv, m_mem_qn_g, m_mem_kn_g, m_w_mem_out, m_w_o, v_norm_g, v_w_in, v_conv_w, v_w_conv_out, v_mla_q_norm_g, v_w_uq, v_mla_kv_norm_g, v_w_ukv, v_mla_qn_nope_g, v_mla_qn_rope_g, v_mla_kn_nope_g, v_mla_kn_rope_g, v_w_mla_out, v_mem_norm_g, v_w_mem_kv, v_mem_qn_g, v_mem_kn_g, v_w_mem_out, v_w_o):
    args = dict(locals())
    W = {n: args[n] for n in WEIGHTS}
    Mo = {n: args['m_' + n] for n in WEIGHTS}
    Vo = {n: args['v_' + n] for n in WEIGHTS}
    c = _config(x, conv_w, w_uq, w_ukv, mla_qn_nope_g, mla_qn_rope_g, mem, mem_qn_g, w_mem_out, w_mla_out)
    N, D, R, B, H = c.N, c.D, c.R, c.B, c.H
    assert x.shape[0] == 1
    xs = x[0]
    tgt = loss_target[0]
    memx = mem[0]
    me = 4 * lax.axis_index("x") + 2 * lax.axis_index("y") + lax.axis_index("c")
    nr = N // R

    g_win, g_taps = _all_gather([W['w_in'][0].astype(_BF).T, conv_w[0]], "ag_w_in")
    rest = [n for n in BIG if n != 'w_in']
    shards_r = [W[n][0].astype(_BF).T if n in COL_SHARDED else W[n][0].astype(_BF) for n in rest]
    lands_r = [lax.empty((NDEV,) + s.shape, s.dtype) for s in shards_r]
    ag_rest = _split_start(shards_r, lands_r, True, "ag_rest_start", after=g_win)
    win_pT, w_krT = _win_split(g_win, c)
    convw = jnp.transpose(g_taps, (1, 0, 2)).reshape(3, c.CW)
    convw8 = jnp.pad(convw, ((0, 5), (0, 0)))

    def rowb(width, cidx, rows=R):
        return pl.BlockSpec((rows, width), lambda i, _c=cidx: (i, _c))

    R2 = c.RP
    nr2 = N // R2
    RM = _tile(N, 1024, 16)

    def rowb2(width, cidx):
        return rowb(width, cidx, R2)

    def fullb(shape):
        nd = len(shape)
        return pl.BlockSpec(shape, lambda *_: (0,) * nd)

    def pad_lanes(g, w=LANES):
        return jnp.pad(g, ((0, 0), (0, w - g.shape[1])))

    def tabs_of(rows):
        return pl.BlockSpec((3, rows, LANES), lambda i, *_: (0, i, 0))

    half = c.ROPE // 2
    inv_freq = jnp.power(ROPE_THETA, -jnp.arange(half, dtype=F32) / half)
    invf = jnp.concatenate([inv_freq, inv_freq, jnp.zeros((LANES - c.ROPE,), F32)])[None, :]
    pos_col = positions[0].astype(F32).reshape(N, 1)

    def rope_tab_kern(pos_ref, invf_ref, o_ref):
        ang = pos_ref[...] * invf_ref[...]
        co, si = jnp.cos(ang), jnp.sin(ang)
        lane = lax.broadcasted_iota(jnp.int32, ang.shape, 1)
        o_ref[0] = jnp.where(lane < c.ROPE, co, 0.0)
        o_ref[1] = jnp.where(lane < half, -si, 0.0)
        o_ref[2] = jnp.where(jnp.logical_and(lane >= half, lane < c.ROPE), si, 0.0)

    tabs = pl.pallas_call(
        rope_tab_kern, name="rope_tab", grid=(nr,),
        in_specs=[pl.BlockSpec((R, 1), lambda i: (i, 0)), fullb((1, LANES))],
        out_specs=tabs_of(R),
        out_shape=jax.ShapeDtypeStruct((3, N, LANES), F32),
        compiler_params=_cparams(("parallel",)),
    )(pos_col, invf)

    def make_rms_kern():
        def rms_fwd_kern(x_ref, g_ref, o_ref):
            xh, _ = _rms(x_ref[...].astype(F32), x_ref.shape[-1])
            o_ref[...] = (xh * g_ref[...]).astype(o_ref.dtype)
        return rms_fwd_kern

    h = pl.pallas_call(
        make_rms_kern(), name="rms_x", grid=(nr2,),
        in_specs=[rowb2(D, 0), fullb((1, D))], out_specs=rowb2(D, 0),
        out_shape=jax.ShapeDtypeStruct((N, D), _BF), compiler_params=_cparams(("parallel",)),
    )(xs, norm_g)

    proj = _mm(h, win_pT, tb=True, name="mm_proj", out_dtype=_BF, after=ag_rest[4])
    kr_raw = _mm(h, w_krT, tb=True, name="mm_kr", out_dtype=_BF)

    Wf = {n: g.reshape(-1, g.shape[2]) for n, g in zip(rest, _split_wait(ag_rest, True, proj, "ag_rest_wait"))}
    wuqT = Wf['w_uq'].reshape(H, c.NOPE + c.ROPE, c.QL)
    wuq_pT = jnp.pad(wuqT, ((0, 0), (0, c.HW - c.NOPE - c.ROPE), (0, 0))).reshape(H * c.HW, c.QL)
    wukvT = Wf['w_ukv']
    wcoT, wmo, wmkv, wmemoT, wo = Wf['w_conv_out'], Wf['w_mla_out'], Wf['w_mem_kv'], Wf['w_mem_out'], Wf['w_o']

    CW = c.CW
    conv_blk = c.o_conv // (4 * CW)
    HALO = 16
    rh = R // HALO

    def conv_parts(blk):
        blk = blk.astype(F32)
        return blk[:, 0:CW], blk[:, CW:2 * CW], blk[:, 2 * CW:3 * CW], blk[:, 3 * CW:4 * CW]

    def shifted(cu, prev, i):
        prev = jnp.where(i > 0, prev, 0.0)
        rid = lax.broadcasted_iota(jnp.int32, cu.shape, 0)
        last, last2 = prev[HALO - 1:HALO, :], prev[HALO - 2:HALO - 1, :]
        sh1 = jnp.where(rid == 0, last, pltpu.roll(cu, 1, 0))
        sh2 = jnp.where(rid == 0, last2, jnp.where(rid == 1, last, pltpu.roll(cu, 2, 0)))
        return sh1, sh2

    def conv_fwd_kern(p_ref, prev_ref, w_ref, o_ref):
        i = pl.program_id(0)
        cg, bg, u, z = conv_parts(p_ref[...])
        pc, _, pu, _ = conv_parts(prev_ref[...])
        cu = cg * u
        sh1, sh2 = shifted(cu, pc * pu, i)
        w = w_ref[...]
        conv = w[0:1, :] * sh2 + w[1:2, :] * sh1 + w[2:3, :] * cu
        o_ref[...] = (bg * conv * (z * _sig(z))).astype(o_ref.dtype)

    prev_spec = pl.BlockSpec((HALO, 4 * CW), lambda i: (jnp.maximum(i * rh - 1, 0), conv_blk))
    a_conv = pl.pallas_call(
        conv_fwd_kern, name="conv_fwd", grid=(nr,),
        in_specs=[rowb(4 * CW, conv_blk), prev_spec, fullb((8, CW))],
        out_specs=rowb(CW, 0), out_shape=jax.ShapeDtypeStruct((N, CW), _BF),
        compiler_params=_cparams(("parallel",)),
    )(proj, proj, convw8)
    o_conv = _mm(a_conv, wcoT, tb=True, name="mm_oconv", out_dtype=_BF)

    QL, KVL, HW = c.QL, c.KVL, c.HW
    lora_blk = c.o_lora // (QL + KVL)

    def lora_fwd_kern(p_ref, gq_ref, gkv_ref, q_ref, kv_ref):
        blk = p_ref[...].astype(F32)
        qh, _ = _rms(blk[:, :QL], QL)
        kh, _ = _rms(blk[:, QL:], KVL)
        q_ref[...] = (qh * gq_ref[...]).astype(q_ref.dtype)
        kv_ref[...] = (kh * gkv_ref[...]).astype(kv_ref.dtype)

    cqn, ckvn = pl.pallas_call(
        lora_fwd_kern, name="lora_fwd", grid=(nr,),
        in_specs=[rowb(QL + KVL, lora_blk), fullb((1, QL)), fullb((1, KVL))],
        out_specs=[rowb(QL, 0), rowb(KVL, 0)],
        out_shape=[jax.ShapeDtypeStruct((N, QL), _BF), jax.ShapeDtypeStruct((N, KVL), _BF)],
        compiler_params=_cparams(("parallel",)),
    )(proj, mla_q_norm_g, mla_kv_norm_g)
    q_p = _mm(cqn, wuq_pT, tb=True, name="mm_q", out_dtype=_BF, bn=2048)
    kv = _mm(ckvn, wukvT, tb=True, name="mm_kv", out_dtype=_BF, bn=2048)

    g_qn, g_qr = mla_qn_nope_g, pad_lanes(mla_qn_rope_g)
    g_kn, g_kr = mla_kn_nope_g, pad_lanes(mla_kn_rope_g)

    def krope_fwd_kern(p_ref, t_ref, g_ref, o_ref):
        xh, _ = _rms128(p_ref[...].astype(F32), c.ROPE)
        o_ref[...] = _rope(xh * g_ref[...], t_ref[0], t_ref[1], t_ref[2]).astype(o_ref.dtype)

    k_rope = pl.pallas_call(
        krope_fwd_kern, name="krope_fwd", grid=(nr,),
        in_specs=[rowb(LANES, 0), tabs_of(R), fullb((1, LANES))],
        out_specs=rowb(LANES, 0), out_shape=jax.ShapeDtypeStruct((N, LANES), _BF),
        compiler_params=_cparams(("parallel",)),
    )(kr_raw, tabs, g_kr)

    RP, HG = c.RP, c.HG
    nrp, nhg = N // RP, H // HG
    heads_in = pl.BlockSpec((RP, HG * HW), lambda i, hg: (i, hg))
    heads_out = pl.BlockSpec((HG, RP, HW), lambda i, hg: (hg, i, 0))

    def q_prep_kern(q_ref, t_ref, gn_ref, gr_ref, o_ref):
        for g in range(HG):
            blk = q_ref[:, g * HW:(g + 1) * HW].astype(F32)
            nh, _ = _rms128(blk[:, :LANES], c.NOPE)
            rhat, _ = _rms128(blk[:, LANES:], c.ROPE)
            rot = _rope(rhat * gr_ref[...], t_ref[0], t_ref[1], t_ref[2])
            o_ref[g] = jnp.concatenate([nh * gn_ref[...], rot], axis=1).astype(o_ref.dtype)

    q_cat = pl.pallas_call(
        q_prep_kern, name="q_prep", grid=(nrp, nhg),
        in_specs=[heads_in, tabs_of(RP), fullb((1, LANES)), fullb((1, LANES))],
        out_specs=heads_out, out_shape=jax.ShapeDtypeStruct((H, N, HW), _BF),
        compiler_params=_cparams(("parallel", "parallel")),
    )(q_p, tabs, g_qn * (c.scale * LOG2E), g_qr * (c.scale * LOG2E))

    def k_prep_kern(kv_ref, kr_ref, gn_ref, o_ref):
        for g in range(HG):
            kn, _ = _rms128(kv_ref[:, g * HW:g * HW + LANES].astype(F32), c.NOPE)
            o_ref[g] = jnp.concatenate([(kn * gn_ref[...]).astype(o_ref.dtype), kr_ref[...]], axis=1)

    k_cat = pl.pallas_call(
        k_prep_kern, name="k_prep", grid=(nrp, nhg),
        in_specs=[heads_in, pl.BlockSpec((RP, LANES), lambda i, hg: (i, 0)), fullb((1, LANES))],
        out_specs=heads_out, out_shape=jax.ShapeDtypeStruct((H, N, HW), _BF),
        compiler_params=_cparams(("parallel", "parallel")),
    )(kv, k_rope, g_kn)

    QB = ATT_QBLOCKS if N % (ATT_QBLOCKS * B) == 0 else 1
    BQ = QB * B
    nq = N // BQ
    assert CHUNK & (CHUNK - 1) == 0 and B % CHUNK == 0 and ATT_UNROLL_FWD % QB == 0 and ATT_UNROLL_BWD % QB == 0

    def live_rows(diag):
        return slice(0 if diag is None else diag * B, BQ)

    def diag_mask(s):
        row = lax.broadcasted_iota(jnp.int32, s.shape, 0)
        col = lax.broadcasted_iota(jnp.int32, s.shape, 1)
        shift = CHUNK.bit_length() - 1
        allowed = jnp.right_shift(col, shift) <= jnp.right_shift(row, shift)
        return jnp.where(allowed, s, NEG)

    k_head = pl.BlockSpec((1, N, HW), lambda hh, i: (hh, 0, 0))
    v_head = pl.BlockSpec((N, LANES), lambda hh, i: (0, 2 * hh + 1))
    q_blk = pl.BlockSpec((1, BQ, HW), lambda hh, i: (hh, i, 0))
    o_blk = pl.BlockSpec((BQ, LANES), lambda hh, i: (i, hh))
    lse_blk = pl.BlockSpec((1, BQ, LANES), lambda hh, i: (hh, i, 0))

    def key_block_plan(i, run, unroll):
        def unrolled(u, carry):
            run(unroll * u, unroll, 0)
            return carry

        n_full = QB * i
        lax.fori_loop(0, n_full // unroll, unrolled, 0)
        for rem in range(0, unroll, QB):
            @pl.when(n_full % unroll == rem)
            def _(rem=rem):
                run(n_full - rem, rem + QB, QB)

    def attn_fwd_kern(q_ref, k_ref, v_ref, o_ref, lse_ref, m_sc, acc_sc, s_sc):
        i = pl.program_id(1)
        m_sc[...] = jnp.full(m_sc.shape, NEG, F32)
        acc_sc[...] = jnp.zeros(acc_sc.shape, F32)

        def rows_of(t):
            return pl.ds(pl.multiple_of(t * B, B), B)

        def scores(t, slot, diag):
            rs = live_rows(diag)
            s_sc[slot, rs, :] = _dot_nt(q_ref[0, rs, :], k_ref[0, rows_of(t), :])

        def softmax_pv(t, slot, diag):
            rs = live_rows(diag)
            s = s_sc[slot, rs, :]
            if diag is not None:
                s = diag_mask(s)
            mt = s[:, 0:LANES]
            for cb in range(1, B // LANES):
                mt = jnp.maximum(mt, s[:, cb * LANES:(cb + 1) * LANES])
            m_prev = m_sc[rs, :]
            m_new = jnp.maximum(m_prev, jnp.max(mt, axis=1, keepdims=True))
            alpha = jnp.exp2(m_prev - m_new)
            p = jnp.concatenate([jnp.exp2(s[:, cb * LANES:(cb + 1) * LANES] - m_new).astype(_BF)
                                 for cb in range(B // LANES)], axis=1)
            v_ones = jnp.concatenate([v_ref[rows_of(t), :], jnp.ones((B, LANES), _BF)], axis=1)
            acc_sc[rs, :] = jnp.concatenate([alpha, alpha], axis=1) * acc_sc[rs, :] + _dot(p, v_ones)
            m_sc[rs, :] = m_new

        scores(0, 0, None)

        def run(first, count, n_diag):
            def diag_of(u):
                return u - (count - n_diag) if count - n_diag <= u < count else None

            for u in range(count):
                if u + 1 < count or n_diag == 0:
                    scores(first + u + 1, (u + 1) % 2, diag_of(u + 1))
                softmax_pv(first + u, u % 2, diag_of(u))

        key_block_plan(i, run, ATT_UNROLL_FWD)
        acc = acc_sc[...]
        o_ref[...] = (acc[:, :LANES] / acc[:, LANES:]).astype(o_ref.dtype)
        lse_ref[0] = m_sc[...] + jnp.log2(acc[:, LANES:])

    mla_y, lse = pl.pallas_call(
        attn_fwd_kern, name="attn_fwd", grid=(H, nq),
        in_specs=[q_blk, k_head, v_head], out_specs=[o_blk, lse_blk],
        out_shape=[jax.ShapeDtypeStruct((N, c.HV), _BF), jax.ShapeDtypeStruct((H, N, LANES), F32)],
        scratch_shapes=[pltpu.VMEM((BQ, LANES), F32), pltpu.VMEM((BQ, HW), F32), pltpu.VMEM((2, BQ, B), F32)],
        compiler_params=_cparams(("parallel", "arbitrary")),
    )(q_cat, k_cat, kv)

    HV = c.HV
    mz_blk = c.o_mz // HV

    def gate_fwd_kern(y_ref, z_ref, o_ref):
        z = z_ref[...].astype(F32)
        o_ref[...] = (y_ref[...].astype(F32) * (z * _sig(z))).astype(o_ref.dtype)

    a_mla = pl.pallas_call(
        gate_fwd_kern, name="gate_mla", grid=(nr2,),
        in_specs=[rowb2(HV, 0), rowb2(HV, mz_blk)], out_specs=rowb2(HV, 0),
        out_shape=jax.ShapeDtypeStruct((N, HV), _BF), compiler_params=_cparams(("parallel",)),
    )(mla_y, proj)
    o_mla = _mm(a_mla, wmo, name="mm_omla", out_dtype=_BF)

    M, MW, MH, MHD = c.M, c.MW, c.MH, c.MHD
    memn = pl.pallas_call(
        make_rms_kern(), name="rms_mem",
        grid=(1,), in_specs=[fullb((M, D)), fullb((1, D))], out_specs=fullb((M, D)),
        out_shape=jax.ShapeDtypeStruct((M, D), _BF), compiler_params=_cparams(("arbitrary",)),
    )(memx, mem_norm_g)
    kvm = _mm(memn, wmkv, name="mm_memkv", out_dtype=F32)

    def memk_fwd_kern(kv_ref, g_ref, k_ref, v_ref):
        for hh in range(MH):
            kh, _ = _rms(kv_ref[:, hh * MHD:(hh + 1) * MHD], MHD)
            k_ref[:, hh * MHD:(hh + 1) * MHD] = (kh * g_ref[...]).astype(k_ref.dtype)
        v_ref[...] = kv_ref[:, MW:].astype(v_ref.dtype)

    mem_k, mem_v = pl.pallas_call(
        memk_fwd_kern, name="memk_fwd", grid=(1,),
        in_specs=[fullb((M, 2 * MW)), fullb((1, MHD))], out_specs=[fullb((M, MW)), fullb((M, MW))],
        out_shape=[jax.ShapeDtypeStruct((M, MW), _BF)] * 2, compiler_params=_cparams(("arbitrary",)),
    )(kvm, mem_kn_g)

    mem_blk = c.o_mem // (2 * MW)

    def mem_head(qz_ref, k_ref, v_ref, g_ref, hh):
        sl = slice(hh * MHD, (hh + 1) * MHD)
        qh, r = _rms(qz_ref[:, sl].astype(F32), MHD)
        qn = (qh * g_ref[...]).astype(_BF)
        s = _dot_nt(qn, k_ref[:, sl]) * c.mscale
        e = jnp.exp(s - jnp.max(s, axis=1, keepdims=True))
        p = e / jnp.sum(e, axis=1, keepdims=True)
        y = _dot(p.astype(_BF), v_ref[:, sl])
        z = qz_ref[:, MW + hh * MHD:MW + (hh + 1) * MHD].astype(F32)
        return sl, qh, r, qn, p, y, z

    def mem_fwd_kern(qz_ref, k_ref, v_ref, g_ref, o_ref):
        for hh in range(MH):
            sl, _, _, _, _, y, z = mem_head(qz_ref, k_ref, v_ref, g_ref, hh)
            o_ref[:, sl] = (y * (z * _sig(z))).astype(o_ref.dtype)

    a_mem = pl.pallas_call(
        mem_fwd_kern, name="mem_fwd", grid=(N // RM,),
        in_specs=[rowb(2 * MW, mem_blk, RM), fullb((M, MW)), fullb((M, MW)), fullb((1, MHD))],
        out_specs=rowb(MW, 0, RM), out_shape=jax.ShapeDtypeStruct((N, MW), _BF),
        compiler_params=_cparams(("parallel",)),
    )(proj, mem_k, mem_v, mem_qn_g)
    o_mem = _mm(a_mem, wmemoT, tb=True, name="mm_omem", out_dtype=_BF)

    g_blk = c.o_g // (3 * D)

    def merge_fwd_kern(g_ref, oc_ref, om_ref, ome_ref, o_ref):
        g = g_ref[...].astype(F32)
        acc = _sig(g[:, :D]) * oc_ref[...].astype(F32)
        acc += _sig(g[:, D:2 * D]) * om_ref[...].astype(F32)
        acc += _sig(g[:, 2 * D:]) * ome_ref[...].astype(F32)
        o_ref[...] = acc.astype(o_ref.dtype)

    merged = pl.pallas_call(
        merge_fwd_kern, name="merge_fwd", grid=(nr,),
        in_specs=[rowb(3 * D, g_blk), rowb(D, 0), rowb(D, 0), rowb(D, 0)], out_specs=rowb(D, 0),
        out_shape=jax.ShapeDtypeStruct((N, D), _BF), compiler_params=_cparams(("parallel",)),
    )(proj, o_conv, o_mla, o_mem)
    obm, obn = _tile(N, 512, 16), _tile(D, 1024, LANES)

    def out_loss_kern(a_ref, w_ref, x_ref, t_ref, dyb_ref, l_ref):
        e = x_ref[...] + _dot(a_ref[...], w_ref[...]) - t_ref[...]
        dyb_ref[...] = (e * (1.0 / D)).astype(dyb_ref.dtype)
        row = lax.broadcasted_iota(jnp.int32, l_ref.shape, 0)
        l_ref[...] = jnp.where(row == 0, jnp.sum(e * e, axis=0, keepdims=True), 0.0)

    oblk = pl.BlockSpec((obm, obn), lambda i, j: (i, j))
    dyb, lpart = pl.pallas_call(
        out_loss_kern, name="mm_out_loss", grid=(N // obm, D // obn),
        in_specs=[pl.BlockSpec((obm, D), lambda i, j: (i, 0)), pl.BlockSpec((D, obn), lambda i, j: (0, j)), oblk, oblk],
        out_specs=[oblk, pl.BlockSpec((8, obn), lambda i, j: (i, j))],
        out_shape=[jax.ShapeDtypeStruct((N, D), _BF), jax.ShapeDtypeStruct((8 * (N // obm), D), F32)],
        compiler_params=_cparams(("parallel", "parallel")),
    )(merged, wo, xs, tgt)
    loss = lax.psum(jnp.sum(lpart) * (0.5 / D), AXES)

    G = {}
    d_merged = _mm(dyb, wo, tb=True, name="mm_dmerged", out_dtype=_BF)
    G['w_o'] = _mm(merged, dyb, ta=True, name="mm_dwo", out_dtype=_BF, bk=4096)

    dproj0 = lax.empty((N, c.P), _BF)
    any_spec = pl.BlockSpec(memory_space=pl.ANY)

    def merge_bwd_kern(dp_any, g_ref, dm_ref, oc_ref, om_ref, ome_ref, dg_ref, doc_ref, dom_ref, dome_ref):
        g = g_ref[...].astype(F32)
        dm = dm_ref[...].astype(F32)
        for idx, (o_in, d_out) in enumerate(((oc_ref, doc_ref), (om_ref, dom_ref), (ome_ref, dome_ref))):
            sg = _sig(g[:, idx * D:(idx + 1) * D])
            d_out[...] = (sg * dm).astype(d_out.dtype)
            dg_ref[:, idx * D:(idx + 1) * D] = (dm * o_in[...].astype(F32) * sg * (1.0 - sg)).astype(dg_ref.dtype)

    dproj1, d_oconv, d_omla, d_omem = pl.pallas_call(
        merge_bwd_kern, name="merge_bwd", grid=(nr,),
        in_specs=[any_spec, rowb(3 * D, g_blk), rowb(D, 0), rowb(D, 0), rowb(D, 0), rowb(D, 0)],
        out_specs=[rowb(3 * D, g_blk), rowb(D, 0), rowb(D, 0), rowb(D, 0)],
        out_shape=[jax.ShapeDtypeStruct((N, c.P), _BF)] + [jax.ShapeDtypeStruct((N, D), _BF)] * 3,
        input_output_aliases={0: 0}, compiler_params=_cparams(("parallel",)),
    )(dproj0, proj, d_merged, o_conv, o_mla, o_mem)

    G['w_conv_out'] = _mm(a_conv, d_oconv, ta=True, name="mm_dwco", out_dtype=_BF, bk=4096)
    d_aconv = _mm(d_oconv, wcoT, name="mm_daconv", out_dtype=_BF)
    G['w_mla_out'] = _mm(a_mla, d_omla, ta=True, name="mm_dwmo", out_dtype=_BF, bk=4096)
    d_amla = _mm(d_omla, wmo, tb=True, name="mm_damla", out_dtype=_BF)
    G['w_mem_out'] = _mm(a_mem, d_omem, ta=True, name="mm_dwmemo", out_dtype=_BF, bk=4096)
    d_amem = _mm(d_omem, wmemoT, name="mm_damem", out_dtype=_BF)

    def conv_bwd_kern(dp_any, p_ref, prev_ref, next_ref, da_ref, dan_ref, w_ref, o_ref, dw_ref):
        i = pl.program_id(0)
        cg, bg, u, z = conv_parts(p_ref[...])
        pc, _, pu, _ = conv_parts(prev_ref[...])
        _, nbg, _, nz = conv_parts(next_ref[...])
        cu = cg * u
        sh1, sh2 = shifted(cu, pc * pu, i)
        w = w_ref[...]
        conv = w[0:1, :] * sh2 + w[1:2, :] * sh1 + w[2:3, :] * cu
        sg = _sig(z)
        sz = z * sg
        da = da_ref[...].astype(F32)
        dcy = da * sz
        d_z = da * (bg * conv) * (sg * (1.0 + z * (1.0 - sg)))
        d_b = dcy * conv
        dconv = dcy * bg
        dnext = dan_ref[...].astype(F32) * (nz * _sig(nz)) * nbg
        dnext = jnp.where(i < nr - 1, dnext, 0.0)
        rid = lax.broadcasted_iota(jnp.int32, cu.shape, 0)
        up1 = jnp.where(rid == R - 1, dnext[0:1, :], pltpu.roll(dconv, R - 1, 0))
        up2 = jnp.where(rid == R - 2, dnext[0:1, :], jnp.where(rid == R - 1, dnext[1:2, :], pltpu.roll(dconv, R - 2, 0)))
        dcu = w[2:3, :] * dconv + w[1:2, :] * up1 + w[0:1, :] * up2
        o_ref[:, 0:CW] = (dcu * u).astype(o_ref.dtype)
        o_ref[:, CW:2 * CW] = d_b.astype(o_ref.dtype)
        o_ref[:, 2 * CW:3 * CW] = (dcu * cg).astype(o_ref.dtype)
        o_ref[:, 3 * CW:4 * CW] = d_z.astype(o_ref.dtype)

        @pl.when(i == 0)
        def _():
            dw_ref[...] = jnp.zeros(dw_ref.shape, F32)

        dw_ref[0:1, :] += jnp.sum(dconv * sh2, axis=0, keepdims=True)
        dw_ref[1:2, :] += jnp.sum(dconv * sh1, axis=0, keepdims=True)
        dw_ref[2:3, :] += jnp.sum(dconv * cu, axis=0, keepdims=True)

    next_spec = pl.BlockSpec((HALO, 4 * CW), lambda i: (jnp.minimum((i + 1) * rh, N // HALO - 1), conv_blk))
    dan_spec = pl.BlockSpec((HALO, CW), lambda i: (jnp.minimum((i + 1) * rh, N // HALO - 1), 0))
    dproj2, g_convw = pl.pallas_call(
        conv_bwd_kern, name="conv_bwd", grid=(nr,),
        in_specs=[any_spec, rowb(4 * CW, conv_blk), prev_spec, next_spec, rowb(CW, 0), dan_spec, fullb((8, CW))],
        out_specs=[rowb(4 * CW, conv_blk), fullb((8, CW))],
        out_shape=[jax.ShapeDtypeStruct((N, c.P), _BF), jax.ShapeDtypeStruct((8, CW), F32)],
        input_output_aliases={0: 0}, compiler_params=_cparams(("arbitrary",)),
    )(dproj1, proj, proj, proj, d_aconv, d_aconv, convw8)

    def mem_bwd_kern(dp_any, qz_ref, da_ref, k_ref, v_ref, g_ref, o_ref, dk_ref, dv_ref, dg_ref):
        @pl.when(pl.program_id(0) == 0)
        def _():
            dk_ref[...] = jnp.zeros(dk_ref.shape, F32)
            dv_ref[...] = jnp.zeros(dv_ref.shape, F32)
            dg_ref[...] = jnp.zeros(dg_ref.shape, F32)

        for hh in range(MH):
            sl, qh, r, qn, p, y, z = mem_head(qz_ref, k_ref, v_ref, g_ref, hh)
            da = da_ref[:, sl].astype(F32)
            sg = _sig(z)
            dyh = da * (z * sg)
            o_ref[:, MW + hh * MHD:MW + (hh + 1) * MHD] = (da * y * (sg * (1.0 + z * (1.0 - sg)))).astype(o_ref.dtype)
            dyb_h = dyh.astype(_BF)
            dpm = _dot_nt(dyb_h, v_ref[:, sl])
            ds = (p * (dpm - jnp.sum(dpm * p, axis=1, keepdims=True)) * c.mscale).astype(_BF)
            dqn = _dot(ds, k_ref[:, sl])
            dk_ref[:, sl] += _dot_tn(ds, qn)
            dv_ref[:, sl] += _dot_tn(p.astype(_BF), dyb_h)
            dq, dgp = _rms_bwd(qh, r, g_ref[...], dqn, MHD)
            o_ref[:, sl] = dq.astype(o_ref.dtype)
            dg_ref[...] += jnp.sum(dgp, axis=0, keepdims=True)

    dproj3, d_memk, d_memv, g_mem_qn = pl.pallas_call(
        mem_bwd_kern, name="mem_bwd", grid=(N // RM,),
        in_specs=[any_spec, rowb(2 * MW, mem_blk, RM), rowb(MW, 0, RM), fullb((M, MW)), fullb((M, MW)), fullb((1, MHD))],
        out_specs=[rowb(2 * MW, mem_blk, RM), fullb((M, MW)), fullb((M, MW)), fullb((1, MHD))],
        out_shape=[jax.ShapeDtypeStruct((N, c.P), _BF), jax.ShapeDtypeStruct((M, MW), F32),
                   jax.ShapeDtypeStruct((M, MW), F32), jax.ShapeDtypeStruct((1, MHD), F32)],
        input_output_aliases={0: 0}, compiler_params=_cparams(("arbitrary",)),
    )(dproj2, proj, d_amem, mem_k, mem_v, mem_qn_g)

    def memk_bwd_kern(kv_ref, dk_ref, dv_ref, g_ref, o_ref, dg_ref):
        dg = jnp.zeros((1, MHD), F32)
        for hh in range(MH):
            sl = slice(hh * MHD, (hh + 1) * MHD)
            kh, r = _rms(kv_ref[:, sl], MHD)
            dkr, dgp = _rms_bwd(kh, r, g_ref[...], dk_ref[:, sl], MHD)
            o_ref[:, sl] = dkr.astype(o_ref.dtype)
            dg += jnp.sum(dgp, axis=0, keepdims=True)
        o_ref[:, MW:] = dv_ref[...].astype(o_ref.dtype)
        dg_ref[...] = dg

    d_kvm, g_mem_kn = pl.pallas_call(
        memk_bwd_kern, name="memk_bwd", grid=(1,),
        in_specs=[fullb((M, 2 * MW)), fullb((M, MW)), fullb((M, MW)), fullb((1, MHD))],
        out_specs=[fullb((M, 2 * MW)), fullb((1, MHD))],
        out_shape=[jax.ShapeDtypeStruct((M, 2 * MW), _BF), jax.ShapeDtypeStruct((1, MHD), F32)],
        compiler_params=_cparams(("arbitrary",)),
    )(kvm, d_memk, d_memv, mem_kn_g)
    G['w_mem_kv'] = _mm(memn, d_kvm, ta=True, name="mm_dwmkv", out_dtype=_BF)
    d_memn = _mm(d_kvm, wmkv, tb=True, name="mm_dmemn", out_dtype=F32)

    def memnorm_bwd_kern(x_ref, d_ref, dg_ref):
        xh, _ = _rms(x_ref[...], D)
        dg_ref[...] = jnp.sum(d_ref[...] * xh, axis=0, keepdims=True)

    g_mem_norm = pl.pallas_call(
        memnorm_bwd_kern, name="memnorm_bwd", grid=(1,),
        in_specs=[fullb((M, D)), fullb((M, D))], out_specs=fullb((1, D)),
        out_shape=jax.ShapeDtypeStruct((1, D), F32), compiler_params=_cparams(("arbitrary",)),
    )(memx, d_memn)

    def gate_bwd_kern(dp_any, da_ref, y_ref, z_ref, dy_ref, dz_ref):
        z = z_ref[...].astype(F32)
        da = da_ref[...].astype(F32)
        sg = _sig(z)
        dy_ref[...] = (da * (z * sg)).astype(dy_ref.dtype)
        dz_ref[...] = (da * y_ref[...].astype(F32) * (sg * (1.0 + z * (1.0 - sg)))).astype(dz_ref.dtype)

    d_mlay, dproj4 = pl.pallas_call(
        gate_bwd_kern, name="gate_mla_bwd", grid=(nr2,),
        in_specs=[any_spec, rowb2(HV, 0), rowb2(HV, 0), rowb2(HV, mz_blk)],
        out_specs=[rowb2(HV, 0), rowb2(HV, mz_blk)],
        out_shape=[jax.ShapeDtypeStruct((N, HV), _BF), jax.ShapeDtypeStruct((N, c.P), _BF)],
        input_output_aliases={0: 1}, compiler_params=_cparams(("parallel",)),
    )(dproj3, d_amla, mla_y, proj)

    def attn_bwd_kern(q_ref, k_ref, v_ref, o_ref, do_ref, lse_ref, dq_ref, dk_ref, dv_ref, dq_sc, dl_sc, dk_sc, dv_sc):
        i = pl.program_id(1)
        delta = jnp.sum(do_ref[...].astype(F32) * o_ref[...].astype(F32), axis=1, keepdims=True)
        dl_sc[...] = jnp.broadcast_to(delta, dl_sc.shape)
        dq_sc[...] = jnp.zeros(dq_sc.shape, F32)

        def step(t, diag):
            rows = pl.ds(pl.multiple_of(t * B, B), B)
            rs = live_rows(diag)
            q, do = q_ref[0, rs, :], do_ref[rs, :]
            k = k_ref[0, rows, :]
            s = _dot_nt(q, k)
            if diag is not None:
                s = diag_mask(s)
            dpm = _dot_nt(do, v_ref[rows, :])
            lse_t, dl = lse_ref[0, rs, :], dl_sc[rs, :]
            ps, dss = [], []
            for cb in range(B // LANES):
                cols = slice(cb * LANES, (cb + 1) * LANES)
                p_cb = jnp.exp2(s[:, cols] - lse_t)
                ps.append(p_cb.astype(_BF))
                dss.append((p_cb * (dpm[:, cols] - dl)).astype(_BF))
            p, ds = jnp.concatenate(ps, axis=1), jnp.concatenate(dss, axis=1)
            dvp = _dot_tn(p, do)
            dkp = _dot_tn(ds, q)
            if diag is not None:
                dk_sc[rows, :] = dkp
                dv_sc[rows, :] = dvp
            else:
                dk_sc[rows, :] += dkp
                dv_sc[rows, :] += dvp
            dq_sc[rs, :] += _dot(ds, k)

        def run(first, count, n_diag):
            for u in range(count):
                step(first + u, u - (count - n_diag) if u >= count - n_diag else None)

        key_block_plan(i, run, ATT_UNROLL_BWD)
        dq_ref[0] = dq_sc[...].astype(dq_ref.dtype)

        @pl.when(i == nq - 1)
        def _():
            dk_ref[0] = dk_sc[...].astype(dk_ref.dtype)
            dv_ref[0] = dv_sc[...].astype(dv_ref.dtype)

    d_qcat, d_kcat, d_v = pl.pallas_call(
        attn_bwd_kern, name="attn_bwd", grid=(H, nq),
        in_specs=[q_blk, k_head, v_head, o_blk, o_blk, lse_blk],
        out_specs=[pl.BlockSpec((1, BQ, HW), lambda hh, i: (hh, i, 0)),
                   pl.BlockSpec((1, N, HW), lambda hh, i: (hh, 0, 0)),
                   pl.BlockSpec((1, N, LANES), lambda hh, i: (hh, 0, 0))],
        out_shape=[jax.ShapeDtypeStruct((H, N, HW), _BF), jax.ShapeDtypeStruct((H, N, HW), _BF),
                   jax.ShapeDtypeStruct((H, N, LANES), _BF)],
        scratch_shapes=[pltpu.VMEM((BQ, HW), F32), pltpu.VMEM((BQ, LANES), F32), pltpu.VMEM((N, HW), F32),
                        pltpu.VMEM((N, LANES), F32)],
        compiler_params=_cparams(("parallel", "arbitrary")),
    )(q_cat, k_cat, kv, mla_y, d_mlay, lse)

    def q_prep_bwd_kern(q_ref, dq_ref, t_ref, gn_ref, gr_ref, o_ref, dgn_ref, dgr_ref):
        @pl.when(jnp.logical_and(pl.program_id(0) == 0, pl.program_id(1) == 0))
        def _():
            dgn_ref[...] = jnp.zeros(dgn_ref.shape, F32)
            dgr_ref[...] = jnp.zeros(dgr_ref.shape, F32)

        for g in range(HG):
            blk = q_ref[:, g * HW:(g + 1) * HW].astype(F32)
            d = dq_ref[g].astype(F32)
            nh, rn = _rms128(blk[:, :LANES], c.NOPE)
            rhat, rr = _rms128(blk[:, LANES:], c.ROPE)
            dn, dgn = _rms128_bwd(nh, rn, gn_ref[...], d[:, :LANES], c.NOPE)
            drot = _rope_t(d[:, LANES:], t_ref[0], t_ref[1], t_ref[2])
            dr, dgr = _rms128_bwd(rhat, rr, gr_ref[...], drot, c.ROPE)
            o_ref[:, g * HW:(g + 1) * HW] = jnp.concatenate([dn, dr], axis=1).astype(o_ref.dtype)
            dgn_ref[...] += jnp.sum(dgn, axis=0, keepdims=True)
            dgr_ref[...] += jnp.sum(dgr, axis=0, keepdims=True)

    d_qp, g_qn_nope, g_qn_rope = pl.pallas_call(
        q_prep_bwd_kern, name="q_prep_bwd", grid=(nrp, nhg),
        in_specs=[heads_in, heads_out, tabs_of(RP), fullb((1, LANES)), fullb((1, LANES))],
        out_specs=[heads_in, fullb((1, LANES)), fullb((1, LANES))],
        out_shape=[jax.ShapeDtypeStruct((N, H * HW), _BF), jax.ShapeDtypeStruct((1, LANES), F32),
                   jax.ShapeDtypeStruct((1, LANES), F32)],
        compiler_params=_cparams(("arbitrary", "arbitrary")),
    )(q_p, d_qcat, tabs, g_qn * c.scale, g_qr * c.scale)
    g_qn_nope, g_qn_rope = g_qn_nope * c.scale, g_qn_rope * c.scale

    def k_prep_bwd_kern(kv_ref, dk_ref, dv_ref, gn_ref, o_ref, dkr_ref, dgn_ref):
        hg = pl.program_id(1)

        @pl.when(jnp.logical_and(pl.program_id(0) == 0, hg == 0))
        def _():
            dgn_ref[...] = jnp.zeros(dgn_ref.shape, F32)

        @pl.when(hg == 0)
        def _():
            dkr_ref[...] = jnp.zeros(dkr_ref.shape, F32)

        dkr = jnp.zeros((RP, LANES), F32)
        for g in range(HG):
            dk = dk_ref[g].astype(F32) * (1.0 / LOG2E)
            kn, r = _rms128(kv_ref[:, g * HW:g * HW + LANES].astype(F32), c.NOPE)
            dkn, dgn = _rms128_bwd(kn, r, gn_ref[...], dk[:, :LANES], c.NOPE)
            o_ref[:, g * HW:(g + 1) * HW] = jnp.concatenate([dkn.astype(o_ref.dtype), dv_ref[g]], axis=1)
            dgn_ref[...] += jnp.sum(dgn, axis=0, keepdims=True)
            dkr += dk[:, LANES:]
        dkr_ref[...] += dkr

    d_kv, d_krsum, g_kn_nope = pl.pallas_call(
        k_prep_bwd_kern, name="k_prep_bwd", grid=(nrp, nhg),
        in_specs=[heads_in, heads_out, pl.BlockSpec((HG, RP, LANES), lambda i, hg: (hg, i, 0)), fullb((1, LANES))],
        out_specs=[heads_in, pl.BlockSpec((RP, LANES), lambda i, hg: (i, 0)), fullb((1, LANES))],
        out_shape=[jax.ShapeDtypeStruct((N, H * HW), _BF), jax.ShapeDtypeStruct((N, LANES), F32),
                   jax.ShapeDtypeStruct((1, LANES), F32)],
        compiler_params=_cparams(("arbitrary", "arbitrary")),
    )(kv, d_kcat, d_v, g_kn)

    def krope_bwd_kern(p_ref, d_ref, t_ref, g_ref, o_ref, dg_ref):
        @pl.when(pl.program_id(0) == 0)
        def _():
            dg_ref[...] = jnp.zeros(dg_ref.shape, F32)

        xh, r = _rms128(p_ref[...].astype(F32), c.ROPE)
        drot = _rope_t(d_ref[...], t_ref[0], t_ref[1], t_ref[2])
        dx, dg = _rms128_bwd(xh, r, g_ref[...], drot, c.ROPE)
        o_ref[...] = dx.astype(o_ref.dtype)
        dg_ref[...] += jnp.sum(dg, axis=0, keepdims=True)

    d_kr, g_kn_rope = pl.pallas_call(
        krope_bwd_kern, name="krope_bwd", grid=(nr,),
        in_specs=[rowb(LANES, 0), rowb(LANES, 0), tabs_of(R), fullb((1, LANES))],
        out_specs=[rowb(LANES, 0), fullb((1, LANES))],
        out_shape=[jax.ShapeDtypeStruct((N, LANES), _BF), jax.ShapeDtypeStruct((1, LANES), F32)],
        compiler_params=_cparams(("arbitrary",)),
    )(kr_raw, d_krsum, tabs, g_kr)
    dproj5 = dproj4

    g_wuq_p = _mm(cqn, d_qp, ta=True, name="mm_dwuq", out_dtype=_BF, bk=4096)
    G['w_uq'] = g_wuq_p.reshape(QL, H, HW)[:, :, :c.NOPE + c.ROPE].reshape(QL, H * (c.NOPE + c.ROPE))
    d_cqn = _mm(d_qp, wuq_pT, name="mm_dcqn", out_dtype=F32, bk=4096)
    G['w_ukv'] = _mm(ckvn, d_kv, ta=True, name="mm_dwukv", out_dtype=_BF, bk=4096)
    d_ckvn = _mm(d_kv, wukvT, name="mm_dckvn", out_dtype=F32, bk=4096)

    def to_blocks(n, g):
        if n in COL_SHARDED:
            return jnp.transpose(g.reshape(g.shape[0], NDEV, -1), (1, 0, 2))
        return g.reshape(NDEV, -1, g.shape[1])

    def landing(b):
        return lax.empty(b.shape, b.dtype)

    early = [n for n in BIG if n != 'w_in']
    blocks_e = [to_blocks(n, G[n]) for n in early]
    xe = _split_start(blocks_e, [landing(b) for b in blocks_e], False, "xchg_early_start")
    gq_after = mla_q_norm_g + xe[4][0:1, 0:1]

    def lora_bwd_kern(dp_any, p_ref, dq_ref, dkv_ref, gq_ref, gkv_ref, o_ref, dgq_ref, dgkv_ref):
        @pl.when(pl.program_id(0) == 0)
        def _():
            dgq_ref[...] = jnp.zeros(dgq_ref.shape, F32)
            dgkv_ref[...] = jnp.zeros(dgkv_ref.shape, F32)

        blk = p_ref[...].astype(F32)
        qh, rq = _rms(blk[:, :QL], QL)
        kh, rk = _rms(blk[:, QL:], KVL)
        dq, dgq = _rms_bwd(qh, rq, gq_ref[...], dq_ref[...], QL)
        dk, dgk = _rms_bwd(kh, rk, gkv_ref[...], dkv_ref[...], KVL)
        o_ref[:, :QL] = dq.astype(o_ref.dtype)
        o_ref[:, QL:] = dk.astype(o_ref.dtype)
        dgq_ref[...] += jnp.sum(dgq, axis=0, keepdims=True)
        dgkv_ref[...] += jnp.sum(dgk, axis=0, keepdims=True)

    dproj6, g_q_norm, g_kv_norm = pl.pallas_call(
        lora_bwd_kern, name="lora_bwd", grid=(nr,),
        in_specs=[any_spec, rowb(QL + KVL, lora_blk), rowb(QL, 0), rowb(KVL, 0), fullb((1, QL)), fullb((1, KVL))],
        out_specs=[rowb(QL + KVL, lora_blk), fullb((1, QL)), fullb((1, KVL))],
        out_shape=[jax.ShapeDtypeStruct((N, c.P), _BF), jax.ShapeDtypeStruct((1, QL), F32),
                   jax.ShapeDtypeStruct((1, KVL), F32)],
        input_output_aliases={0: 0}, compiler_params=_cparams(("arbitrary",)),
    )(dproj5, proj, d_cqn, d_ckvn, gq_after, mla_kv_norm_g)

    g_win_p = _mm(h, dproj6, ta=True, name="mm_dwin", out_dtype=_BF, bk=4096)
    g_wkr = _mm(h, d_kr, ta=True, name="mm_dwkr", out_dtype=_BF, bk=4096)
    blocks_w = [_win_blocks(g_win_p, g_wkr, c)]
    xw = _split_start(blocks_w, [landing(b) for b in blocks_w], False, "xchg_win_start")
    d_h = _mm(dproj6, win_pT, name="mm_dh", out_dtype=F32, bk=3072, after=xw[4], plus=(d_kr, w_krT))

    def final_bwd_kern(x_ref, g_ref, dh_ref, dy_ref, gx_ref, dg_ref):
        @pl.when(pl.program_id(0) == 0)
        def _():
            dg_ref[...] = jnp.zeros(dg_ref.shape, F32)

        xh, r = _rms(x_ref[...], D)
        dx, dg = _rms_bwd(xh, r, g_ref[...], dh_ref[...], D)
        gx_ref[...] = dy_ref[...].astype(F32) + dx
        dg_ref[...] += jnp.sum(dg, axis=0, keepdims=True)

    grad_x, g_norm = pl.pallas_call(
        final_bwd_kern, name="final_bwd", grid=(nr2,),
        in_specs=[rowb2(D, 0), fullb((1, D)), rowb2(D, 0), rowb2(D, 0)],
        out_specs=[rowb2(D, 0), fullb((1, D))],
        out_shape=[jax.ShapeDtypeStruct((N, D), F32), jax.ShapeDtypeStruct((1, D), F32)],
        compiler_params=_cparams(("arbitrary",)),
    )(xs, norm_g, d_h, dyb)

    res = [{}, {}, {}, {}]

    def adam_into(n, parts):
        outs = _adam(parts, W[n][0], Mo[n][0], Vo[n][0], "adam_" + n)
        for k in range(4):
            res[k][n] = outs[k][None]
        return outs[0]

    small_g = {'norm_g': g_norm, 'mla_q_norm_g': g_q_norm, 'mla_kv_norm_g': g_kv_norm,
               'mla_qn_nope_g': g_qn_nope, 'mla_qn_rope_g': g_qn_rope[:, :c.ROPE], 'mla_kn_nope_g': g_kn_nope,
               'mla_kn_rope_g': g_kn_rope[:, :c.ROPE], 'mem_norm_g': g_mem_norm, 'mem_qn_g': g_mem_qn,
               'mem_kn_g': g_mem_kn}
    small_part = _pack([small_g[n] for n in SMALL] + [g_convw[0:3, :]], 0)
    xs_small = _split_start([small_part], [lax.empty((NDEV,) + small_part.shape, F32)], True, "ag_small_start",
                            after=grad_x)

    recv_e = _split_wait(xe, False, xs_small[4], "xchg_early_wait")
    last = [adam_into(n, parts) for n, parts in zip(early, recv_e)][-1]
    small_all = _split_wait(xs_small, True, last, "ag_small_wait")[0]
    small_shapes = [W[n].shape for n in SMALL]
    pieces = _unpack(small_all, small_shapes + [(3, CW)])
    cw8 = CW // NDEV
    conv_mine = lax.dynamic_slice_in_dim(pieces[-1].reshape(NDEV, 3, NDEV, cw8), me, 1, axis=2)[:, :, 0, :]
    sm_parts = _pack(pieces[:-1] + [conv_mine], 1)
    sm_names = SMALL + ['conv_w']
    sm_shapes = small_shapes + [(3, cw8)]
    w_sm = _pack([W[n] for n in SMALL] + [conv_w[0]], 0)
    m_sm = _pack([Mo[n] for n in SMALL] + [m_conv_w[0]], 0)
    v_sm = _pack([Vo[n] for n in SMALL] + [v_conv_w[0]], 0)
    packed_sm = _adam(sm_parts, w_sm, m_sm, v_sm, "adam_small")
    outs_sm = [_unpack(o, sm_shapes) for o in packed_sm]
    for k in range(4):
        for n, a in zip(sm_names, outs_sm[k]):
            res[k][n] = a[None] if n == 'conv_w' else a
    recv_w = _split_wait(xw, False, packed_sm[0], "xchg_win_wait")
    adam_into('w_in', recv_w[0])
    return (loss, grad_x[None], *[res[0][n] for n in WEIGHTS], *[res[1][n] for n in WEIGHTS],
            *[res[2][n] for n in WEIGHTS], *[res[3][n] for n in WEIGHTS])
```

```python
import math

import jax
import jax.numpy as jnp
from jax import lax
from jax.experimental import pallas as pl
from jax.experimental.pallas import tpu as pltpu

F32 = jnp.float32
_BF = jnp.bfloat16
EPS = 1e-6
CHUNK = 64
ROPE_THETA = 10000.0
ADAM_LR, ADAM_B1, ADAM_B2, ADAM_EPS, ADAM_WD, ADAM_STEP = 0.001, 0.9, 0.999, 1e-08, 0.01, 10
NDEV = 8
AXES = ("x", "y", "c")
MESH = pl.DeviceIdType.MESH
LANES = 128
NEG = -1e30
LOG2E = math.log2(math.e)
V7X_VMEM_LIMIT = 56 * 1024 * 1024
PACK_C = 1024
ATT_BLOCK = 512
ATT_UNROLL_FWD = 8
ATT_UNROLL_BWD = 6
ATT_QBLOCKS = 2
ADAM_BLOCK_ELEMS = 256 * 1024

WEIGHTS = ['norm_g', 'w_in', 'conv_w', 'w_conv_out', 'mla_q_norm_g', 'w_uq', 'mla_kv_norm_g', 'w_ukv',
           'mla_qn_nope_g', 'mla_qn_rope_g', 'mla_kn_nope_g', 'mla_kn_rope_g', 'w_mla_out', 'mem_norm_g',
           'w_mem_kv', 'mem_qn_g', 'mem_kn_g', 'w_mem_out', 'w_o']
BIG = ['w_in', 'w_conv_out', 'w_uq', 'w_ukv', 'w_mla_out', 'w_mem_kv', 'w_mem_out', 'w_o']
COL_SHARDED = ('w_in', 'w_conv_out', 'w_uq', 'w_ukv', 'w_mem_out')
SMALL = ['norm_g', 'mla_q_norm_g', 'mla_kv_norm_g', 'mla_qn_nope_g', 'mla_qn_rope_g', 'mla_kn_nope_g',
         'mla_kn_rope_g', 'mem_norm_g', 'mem_qn_g', 'mem_kn_g']


def _tile(dim, target, align):
    if dim <= target:
        return dim
    t = target - target % align
    while t > 0:
        if dim % t == 0:
            return t
        t -= align
    raise ValueError(f"no tile for {dim} {target} {align}")


def _cparams(sem):
    return pltpu.CompilerParams(dimension_semantics=sem, vmem_limit_bytes=V7X_VMEM_LIMIT)


def _sig(x):
    return 1.0 / (1.0 + jnp.exp(-x))


def _rms(x, n):
    r = lax.rsqrt(jnp.sum(x * x, axis=-1, keepdims=True) * (1.0 / n) + EPS)
    return x * r, r


def _rms_bwd(xhat, r, g, dy, n):
    dxh = dy * g
    dx = r * (dxh - xhat * (jnp.sum(dxh * xhat, axis=-1, keepdims=True) * (1.0 / n)))
    return dx, dy * xhat


def _rowmean128(x, n):
    return jnp.dot(x.astype(_BF), jnp.full((LANES, LANES), 1.0 / n, _BF), preferred_element_type=F32)


def _rms128(x, n):
    r = lax.rsqrt(_rowmean128(x * x, n) + EPS)
    return x * r, r


def _rms128_bwd(xhat, r, g, dy, n):
    dxh = dy * g
    dx = r * (dxh - xhat * _rowmean128(dxh * xhat, n))
    return dx, dy * xhat


def _rope(x, cosp, sina, sinb):
    return x * cosp + pltpu.roll(x, 96, 1) * sina + pltpu.roll(x, 32, 1) * sinb


def _rope_t(d, cosp, sina, sinb):
    return d * cosp + pltpu.roll(d * sina, 32, 1) + pltpu.roll(d * sinb, 96, 1)


def _dot_nt(a, b):
    return lax.dot_general(a, b, (((1,), (1,)), ((), ())), preferred_element_type=F32)


def _dot_tn(a, b):
    return lax.dot_general(a, b, (((0,), (0,)), ((), ())), preferred_element_type=F32)


def _dot(a, b):
    return jnp.dot(a, b, preferred_element_type=F32)


def _all_gather(shards, name):
    na = len(shards)
    nc = 9
    halves = [s.shape[0] // 32 * 16 if s.shape[0] >= 256 else None for s in shards]

    def body(*refs):
        x_refs, out_refs = refs[:na], refs[na:2 * na]
        send_sems, recv_sems, local_sems = refs[2 * na:]
        x, y, c = lax.axis_index("x"), lax.axis_index("y"), lax.axis_index("c")
        me, sib = (x, y, c), (x, y, 1 - c)
        px, py, pd = (1 - x, y, c), (x, 1 - y, c), (1 - x, 1 - y, c)

        def other_core(p):
            return (p[0], p[1], 1 - p[2])

        def rows(a, blk, part=None):
            r = out_refs[a].at[4 * blk[0] + 2 * blk[1] + blk[2]]
            if part is None or halves[a] is None:
                return r
            rest = shards[a].shape[0] - halves[a]
            return r.at[pl.ds(0, halves[a])] if part == 0 else r.at[pl.ds(halves[a], rest)]

        def copy(a, k, blk, to, part=None, src=None):
            dst = rows(a, blk, part)
            return pltpu.make_async_remote_copy(
                src_ref=dst if src is None else src, dst_ref=dst,
                send_sem=send_sems.at[nc * a + k], recv_sem=recv_sems.at[nc * a + k],
                device_id=to, device_id_type=MESH)

        mine = [pltpu.make_async_copy(x_refs[a], rows(a, me), local_sems.at[a]) for a in range(na)]
        for cp in mine:
            cp.start()
        started = []
        for a in range(na):
            started += [copy(a, 0, me, px, src=x_refs[a]), copy(a, 1, me, py, src=x_refs[a]),
                        copy(a, 2, me, sib, src=x_refs[a])]
        for cp in started:
            cp.start()

        def forward(cp):
            cp.start()
            started.append(cp)

        for a in range(na):
            copy(a, 0, px, me).wait_recv()
            forward(copy(a, 3, px, py, part=0))
            forward(copy(a, 4, px, sib))
        for a in range(na):
            copy(a, 1, py, me).wait_recv()
            if halves[a] is not None:
                forward(copy(a, 5, py, px, part=1))
            forward(copy(a, 6, py, sib))
        for a in range(na):
            copy(a, 3, pd, me, part=0).wait_recv()
            forward(copy(a, 7, pd, sib, part=0))
            if halves[a] is not None:
                copy(a, 5, pd, me, part=1).wait_recv()
                forward(copy(a, 8, pd, sib, part=1))
        for a in range(na):
            copy(a, 2, sib, me).wait_recv()
            copy(a, 4, other_core(px), me).wait_recv()
            copy(a, 6, other_core(py), me).wait_recv()
            copy(a, 7, other_core(pd), me, part=0).wait_recv()
            if halves[a] is not None:
                copy(a, 8, other_core(pd), me, part=1).wait_recv()
        for cp in started:
            cp.wait_send()
        for cp in mine:
            cp.wait()

    any_spec = pl.BlockSpec(memory_space=pl.ANY)
    return pl.pallas_call(
        body, name=name,
        out_shape=[jax.ShapeDtypeStruct((NDEV,) + s.shape, s.dtype) for s in shards],
        in_specs=[any_spec] * na, out_specs=[any_spec] * na,
        scratch_shapes=[pltpu.SemaphoreType.DMA((nc * na,)), pltpu.SemaphoreType.DMA((nc * na,)),
                        pltpu.SemaphoreType.DMA((na,))],
    )(*shards)


_HBM = pl.BlockSpec(memory_space=pltpu.HBM)
_SEM = pl.BlockSpec(memory_space=pltpu.SEMAPHORE)
_EFFECT = pltpu.SideEffectType.DATAFLOW_SIDE_EFFECTING


def _split_copy(a, k, src_refs, land_refs, send_sems, recv_sems, gather, receive_side):
    x, y, c = lax.axis_index("x"), lax.axis_index("y"), lax.axis_index("c")
    me = 4 * x + 2 * y + c
    tx, ty, tc = x ^ ((k + 1) >> 2 & 1), y ^ ((k + 1) >> 1 & 1), c ^ ((k + 1) & 1)
    peer = 4 * tx + 2 * ty + tc
    return pltpu.make_async_remote_copy(
        src_ref=src_refs[a] if gather else src_refs[a].at[peer],
        dst_ref=land_refs[a].at[peer if receive_side else me],
        send_sem=send_sems.at[7 * a + k], recv_sem=recv_sems.at[7 * a + k],
        device_id=(tx, ty, tc), device_id_type=MESH)


def _own_copy(a, na, src_refs, land_refs, send_sems, gather):
    me = 4 * lax.axis_index("x") + 2 * lax.axis_index("y") + lax.axis_index("c")
    return pltpu.make_async_copy(src_refs[a] if gather else src_refs[a].at[me], land_refs[a].at[me],
                                 send_sems.at[7 * na + a])


def _split_start(srcs, lands, gather, name, after=None):
    na = len(srcs)
    extra = [] if after is None else [after]

    def body(*refs):
        src_refs, land_refs = refs[:na], refs[na:2 * na]
        send_sems, recv_sems = refs[2 * na + len(extra)], refs[2 * na + len(extra) + 1]
        token = refs[-1]
        for k in range(7):
            for a in range(na):
                _split_copy(a, k, src_refs, land_refs, send_sems, recv_sems, gather, False).start()
        for a in range(na):
            _own_copy(a, na, src_refs, land_refs, send_sems, gather).start()
        token[...] = jnp.zeros_like(token)

    hbm = [pltpu.HBM(b.shape, b.dtype) for b in list(srcs) + list(lands)]
    outs = pl.pallas_call(
        body, name=name,
        out_shape=(pltpu.SemaphoreType.DMA((8 * na,)), pltpu.SemaphoreType.DMA((7 * na,)), *hbm,
                   jax.ShapeDtypeStruct((8, LANES), F32)),
        in_specs=[_HBM] * (2 * na) + [pl.BlockSpec(memory_space=pl.ANY)] * len(extra),
        out_specs=(_SEM, _SEM, *[_HBM] * (2 * na), pl.BlockSpec(memory_space=pltpu.VMEM)),
        input_output_aliases={j: 2 + j for j in range(2 * na)},
        compiler_params=pltpu.CompilerParams(has_side_effects=_EFFECT),
    )(*[pltpu.with_memory_space_constraint(b, pltpu.HBM) for b in srcs],
      *[pltpu.with_memory_space_constraint(l, pltpu.HBM) for l in lands], *extra)
    return outs[0], outs[1], outs[2:2 + na], outs[2 + na:2 + 2 * na], outs[-1]


def _split_wait(started, gather, after, name):
    send_sems, recv_sems, srcs, lands, _ = started
    na = len(srcs)

    def body(*refs):
        src_refs, land_refs = refs[:na], refs[na:2 * na]
        send_s, recv_s = refs[2 * na], refs[2 * na + 1]
        for k in range(7):
            for a in range(na):
                cp = _split_copy(a, k, src_refs, land_refs, send_s, recv_s, gather, True)
                cp.wait_send()
                cp.wait_recv()
        for a in range(na):
            _own_copy(a, na, src_refs, land_refs, send_s, gather).wait()

    hbm = [pltpu.HBM(b.shape, b.dtype) for b in list(srcs) + list(lands)]
    outs = pl.pallas_call(
        body, name=name, out_shape=tuple(hbm),
        in_specs=[_HBM] * (2 * na) + [_SEM, _SEM, pl.BlockSpec(memory_space=pl.ANY)],
        out_specs=tuple([_HBM] * (2 * na)),
        input_output_aliases={j: j for j in range(2 * na)},
        compiler_params=pltpu.CompilerParams(has_side_effects=_EFFECT),
    )(*srcs, *lands, send_sems, recv_sems, after)
    return outs[na:]


def _seg_rows(size):
    rows = -(-size // PACK_C)
    return -(-rows // 16) * 16


def _pack(arrs, lead):
    parts = []
    for a in arrs:
        lshape = a.shape[:lead]
        f = a.reshape(lshape + (-1,)).astype(F32)
        rows = _seg_rows(f.shape[-1])
        f = jnp.pad(f, [(0, 0)] * lead + [(0, rows * PACK_C - f.shape[-1])])
        parts.append(f.reshape(lshape + (rows, PACK_C)))
    return jnp.concatenate(parts, axis=lead)


def _unpack(buf, shapes):
    lshape = buf.shape[:-2]
    out, r = [], 0
    for shp in shapes:
        size = math.prod(shp)
        rows = _seg_rows(size)
        seg = buf[..., r:r + rows, :].reshape(lshape + (rows * PACK_C,))[..., :size]
        out.append(seg.reshape(lshape + tuple(shp)))
        r += rows
    return out


def _mm(a, b, *, name, out_dtype, ta=False, tb=False, bm=1024, bn=1024, bk=2048, after=None, plus=None):
    if ta:
        kdim, m = a.shape
    else:
        m, kdim = a.shape
    n, k2 = b.shape if tb else b.shape[::-1]
    assert kdim == k2 and not (ta and tb), (a.shape, b.shape)
    bm = _tile(m, bm, LANES if ta else 16)
    bn = _tile(n, bn, LANES)
    bk = _tile(kdim, bk, LANES)
    nk = kdim // bk
    n_after = 0 if after is None else 1
    n_plus = 0 if plus is None else 2

    def kern(a_ref, b_ref, *rest):
        plus_refs = rest[n_after:n_after + n_plus]
        o_ref, scratch = rest[n_after + n_plus], rest[n_after + n_plus + 1:]
        part = (_dot_tn if ta else _dot_nt if tb else _dot)(a_ref[...], b_ref[...])

        def first(p):
            return p + _dot(plus_refs[0][...], plus_refs[1][...]) if plus is not None else p

        if nk == 1:
            o_ref[...] = first(part).astype(o_ref.dtype)
        else:
            acc = scratch[0] if scratch else o_ref
            k = pl.program_id(2)

            @pl.when(k == 0)
            def _():
                acc[...] = first(jnp.zeros(acc.shape, F32))

            acc[...] += part
            if scratch:
                @pl.when(k == nk - 1)
                def _():
                    o_ref[...] = acc[...].astype(o_ref.dtype)

    a_spec = pl.BlockSpec((bk, bm), lambda i, j, k: (k, i)) if ta else pl.BlockSpec((bm, bk), lambda i, j, k: (i, k))
    b_spec = pl.BlockSpec((bn, bk), lambda i, j, k: (j, k)) if tb else pl.BlockSpec((bk, bn), lambda i, j, k: (k, j))
    extra_specs, extra_args = [], []
    if after is not None:
        extra_specs.append(pl.BlockSpec(after.shape, lambda i, j, k: (0, 0)))
        extra_args.append(after)
    if plus is not None:
        kk = plus[0].shape[1]
        extra_specs += [pl.BlockSpec((bm, kk), lambda i, j, k: (i, 0)), pl.BlockSpec((kk, bn), lambda i, j, k: (0, j))]
        extra_args += list(plus)
    return pl.pallas_call(
        kern, name=name, grid=(m // bm, n // bn, nk),
        in_specs=[a_spec, b_spec] + extra_specs,
        out_specs=pl.BlockSpec((bm, bn), lambda i, j, k: (i, j)),
        out_shape=jax.ShapeDtypeStruct((m, n), out_dtype),
        scratch_shapes=[pltpu.VMEM((bm, bn), F32)] if nk > 1 and out_dtype != F32 else [],
        compiler_params=_cparams(("parallel", "parallel", "arbitrary")),
    )(a, b, *extra_args)


def _adam(parts, w_a, m_a, v_a, name):
    rows, cols = w_a.shape
    rb = _tile(rows, max(8, ADAM_BLOCK_ELEMS // cols // 8 * 8), 8)
    bc1 = 1.0 - ADAM_B1 ** ADAM_STEP
    bc2 = 1.0 - ADAM_B2 ** ADAM_STEP

    def adam_kern(p_ref, w_ref, m_ref, v_ref, g_ref, d_ref, nm_ref, nv_ref):
        g = p_ref[0].astype(F32)
        for j in range(1, NDEV):
            g = g + p_ref[j].astype(F32)
        m_new = ADAM_B1 * m_ref[...] + (1.0 - ADAM_B1) * g
        v_new = ADAM_B2 * v_ref[...] + (1.0 - ADAM_B2) * (g * g)
        g_ref[...] = g
        nm_ref[...] = m_new
        nv_ref[...] = v_new
        d_ref[...] = -ADAM_LR * ((m_new / bc1) / (jnp.sqrt(v_new / bc2) + ADAM_EPS) + ADAM_WD * w_ref[...])

    blk = pl.BlockSpec((rb, cols), lambda i: (i, 0))
    return pl.pallas_call(
        adam_kern, name=name, grid=(rows // rb,),
        in_specs=[pl.BlockSpec((NDEV, rb, cols), lambda i: (0, i, 0)), blk, blk, blk],
        out_specs=[blk] * 4, out_shape=[jax.ShapeDtypeStruct((rows, cols), F32)] * 4,
        compiler_params=_cparams(("parallel",)),
    )(parts, w_a, m_a, v_a)


class _Cfg:
    pass


def _config(x, conv_w, w_uq, w_ukv, mla_qn_nope_g, mla_qn_rope_g, mem, mem_qn_g, w_mem_out, w_mla_out):
    c = _Cfg()
    c.N, c.D = x.shape[1], x.shape[2]
    c.CW = conv_w.shape[2] * NDEV
    c.QL, c.KVL = w_uq.shape[1], w_ukv.shape[1]
    c.NOPE, c.ROPE = mla_qn_nope_g.shape[1], mla_qn_rope_g.shape[1]
    c.H = w_uq.shape[2] * NDEV // (c.NOPE + c.ROPE)
    c.V = w_ukv.shape[2] * NDEV // c.H - c.NOPE
    assert c.NOPE == LANES and c.V == LANES and c.ROPE == LANES // 2
    c.HW = 2 * LANES
    c.HV = c.H * c.V
    assert w_mla_out.shape[1] * NDEV == c.HV
    c.M = mem.shape[1]
    c.MHD = mem_qn_g.shape[1]
    c.MW = w_mem_out.shape[1]
    c.MH = c.MW // c.MHD
    c.o_conv = 0
    c.o_mz = 4 * c.CW
    c.o_g = c.o_mz + c.HV
    c.o_mem = c.o_g + 3 * c.D
    c.o_lora = c.o_mem + 2 * c.MW
    c.P = c.o_lora + c.QL + c.KVL
    assert c.o_mz % c.HV == 0 and c.o_g % (3 * c.D) == 0 and c.o_mem % (2 * c.MW) == 0
    assert c.o_lora % (c.QL + c.KVL) == 0 and c.QL % LANES == 0 and c.KVL % LANES == 0
    c.IN = 4 * c.CW + c.QL + c.KVL + c.ROPE + c.HV + 2 * c.MW + 3 * c.D
    c.R = _tile(c.N, 256, 16)
    c.RP = _tile(c.N, 512, 16)
    c.HG = _tile(c.H, 4, 1)
    c.B = _tile(c.N, ATT_BLOCK, CHUNK)
    c.scale = float((c.NOPE + c.ROPE) ** -0.5)
    c.mscale = float(c.MHD ** -0.5)
    return c


def _win_segments(c):
    ref_order = (('conv', 4 * c.CW), ('lora', c.QL + c.KVL), ('kr', c.ROPE), ('mz', c.HV), ('mem', 2 * c.MW), ('g', 3 * c.D))
    mine = {'conv': c.o_conv, 'mz': c.o_mz, 'g': c.o_g, 'mem': c.o_mem, 'lora': c.o_lora, 'kr': 0}
    segs, o = [], 0
    for nm, wd in ref_order:
        segs.append((nm, o, wd, mine[nm]))
        o += wd
    return segs


def _win_split(g_win_t, c):
    n8 = g_win_t.shape[1]

    def rows(a, wd):
        return [g_win_t[j][max(a, j * n8) - j * n8:min(a + wd, (j + 1) * n8) - j * n8]
                for j in range(a // n8, (a + wd - 1) // n8 + 1)]

    segs = {nm: (a, wd) for nm, a, wd, _ in _win_segments(c)}
    main = [p for nm in ('conv', 'mz', 'g', 'mem', 'lora') for p in rows(*segs[nm])]
    kr = jnp.concatenate(rows(*segs['kr']) + [jnp.zeros((LANES - c.ROPE, g_win_t.shape[2]), g_win_t.dtype)], axis=0)
    return jnp.concatenate(main, axis=0), kr


def _win_blocks(g, g_kr, c):
    n8 = c.IN // NDEV
    blocks = []
    for j in range(NDEV):
        lo, hi = j * n8, (j + 1) * n8
        parts = []
        for nm, a, wd, mine in _win_segments(c):
            s, e = max(a, lo), min(a + wd, hi)
            if s < e:
                parts.append((g_kr if nm == 'kr' else g)[:, mine + s - a:mine + e - a])
        blocks.append(jnp.concatenate(parts, axis=1))
    return jnp.stack(blocks, axis=0)


def kernel(x, positions, mem, norm_g, w_in, conv_w, w_conv_out, mla_q_norm_g, w_uq, mla_kv_norm_g, w_ukv, mla_qn_nope_g, mla_qn_rope_g, mla_kn_nope_g, mla_kn_rope_g, w_mla_out, mem_norm_g, w_mem_kv, mem_qn_g, mem_kn_g, w_mem_out, w_o, loss_target, m_norm_g, m_w_in, m_conv_w, m_w_conv_out, m_mla_q_norm_g, m_w_uq, m_mla_kv_norm_g, m_w_ukv, m_mla_qn_nope_g, m_mla_qn_rope_g, m_mla_kn_nope_g, m_mla_kn_rope_g, m_w_mla_out, m_mem_norm_g, m_w_mem_kv, m_mem_qn_g, m_mem_kn_g, m_w_mem_out, m_w_o, v_norm_g, v_w_in, v_conv_w, v_w_conv_out, v_mla_q_norm_g, v_w_uq, v_mla_kv_norm_g, v_w_ukv, v_mla_qn_nope_g, v_mla_qn_rope_g, v_mla_kn_nope_g, v_mla_kn_rope_g, v_w_mla_out, v_mem_norm_g, v_w_mem_kv, v_mem_qn_g, v_mem_kn_g, v_w_mem_out, v_w_o):
    args = dict(locals())
    W = {n: args[n] for n in WEIGHTS}
    Mo = {n: args['m_' + n] for n in WEIGHTS}
    Vo = {n: args['v_' + n] for n in WEIGHTS}
    c = _config(x, conv_w, w_uq, w_ukv, mla_qn_nope_g, mla_qn_rope_g, mem, mem_qn_g, w_mem_out, w_mla_out)
    N, D, R, B, H = c.N, c.D, c.R, c.B, c.H
    assert x.shape[0] == 1
    xs = x[0]
    tgt = loss_target[0]
    memx = mem[0]
    me = 4 * lax.axis_index("x") + 2 * lax.axis_index("y") + lax.axis_index("c")
    nr = N // R

    g_win, g_taps = _all_gather([W['w_in'][0].astype(_BF).T, conv_w[0]], "ag_w_in")
    rest = [n for n in BIG if n != 'w_in']
    shards_r = [W[n][0].astype(_BF).T if n in COL_SHARDED else W[n][0].astype(_BF) for n in rest]
    lands_r = [lax.empty((NDEV,) + s.shape, s.dtype) for s in shards_r]
    ag_rest = _split_start(shards_r, lands_r, True, "ag_rest_start", after=g_win)
    win_pT, w_krT = _win_split(g_win, c)
    convw = jnp.transpose(g_taps, (1, 0, 2)).reshape(3, c.CW)
    convw8 = jnp.pad(convw, ((0, 5), (0, 0)))

    def rowb(width, cidx, rows=R):
        return pl.BlockSpec((rows, width), lambda i, _c=cidx: (i, _c))

    R2 = c.RP
    nr2 = N // R2
    RM = _tile(N, 1024, 16)

    def rowb2(width, cidx):
        return rowb(width, cidx, R2)

    def fullb(shape):
        nd = len(shape)
        return pl.BlockSpec(shape, lambda *_: (0,) * nd)

    def pad_lanes(g, w=LANES):
        return jnp.pad(g, ((0, 0), (0, w - g.shape[1])))

    def tabs_of(rows):
        return pl.BlockSpec((3, rows, LANES), lambda i, *_: (0, i, 0))

    half = c.ROPE // 2
    inv_freq = jnp.power(ROPE_THETA, -jnp.arange(half, dtype=F32) / half)
    invf = jnp.concatenate([inv_freq, inv_freq, jnp.zeros((LANES - c.ROPE,), F32)])[None, :]
    pos_col = positions[0].astype(F32).reshape(N, 1)

    def rope_tab_kern(pos_ref, invf_ref, o_ref):
        ang = pos_ref[...] * invf_ref[...]
        co, si = jnp.cos(ang), jnp.sin(ang)
        lane = lax.broadcasted_iota(jnp.int32, ang.shape, 1)
        o_ref[0] = jnp.where(lane < c.ROPE, co, 0.0)
        o_ref[1] = jnp.where(lane < half, -si, 0.0)
        o_ref[2] = jnp.where(jnp.logical_and(lane >= half, lane < c.ROPE), si, 0.0)

    tabs = pl.pallas_call(
        rope_tab_kern, name="rope_tab", grid=(nr,),
        in_specs=[pl.BlockSpec((R, 1), lambda i: (i, 0)), fullb((1, LANES))],
        out_specs=tabs_of(R),
        out_shape=jax.ShapeDtypeStruct((3, N, LANES), F32),
        compiler_params=_cparams(("parallel",)),
    )(pos_col, invf)

    def make_rms_kern():
        def rms_fwd_kern(x_ref, g_ref, o_ref):
            xh, _ = _rms(x_ref[...].astype(F32), x_ref.shape[-1])
            o_ref[...] = (xh * g_ref[...]).astype(o_ref.dtype)
        return rms_fwd_kern

    h = pl.pallas_call(
        make_rms_kern(), name="rms_x", grid=(nr2,),
        in_specs=[rowb2(D, 0), fullb((1, D))], out_specs=rowb2(D, 0),
        out_shape=jax.ShapeDtypeStruct((N, D), _BF), compiler_params=_cparams(("parallel",)),
    )(xs, norm_g)

    proj = _mm(h, win_pT, tb=True, name="mm_proj", out_dtype=_BF, after=ag_rest[4])
    kr_raw = _mm(h, w_krT, tb=True, name="mm_kr", out_dtype=_BF)

    Wf = {n: g.reshape(-1, g.shape[2]) for n, g in zip(rest, _split_wait(ag_rest, True, proj, "ag_rest_wait"))}
    wuqT = Wf['w_uq'].reshape(H, c.NOPE + c.ROPE, c.QL)
    wuq_pT = jnp.pad(wuqT, ((0, 0), (0, c.HW - c.NOPE - c.ROPE), (0, 0))).reshape(H * c.HW, c.QL)
    wukvT = Wf['w_ukv']
    wcoT, wmo, wmkv, wmemoT, wo = Wf['w_conv_out'], Wf['w_mla_out'], Wf['w_mem_kv'], Wf['w_mem_out'], Wf['w_o']

    CW = c.CW
    conv_blk = c.o_conv // (4 * CW)
    HALO = 16
    rh = R // HALO

    def conv_parts(blk):
        blk = blk.astype(F32)
        return blk[:, 0:CW], blk[:, CW:2 * CW], blk[:, 2 * CW:3 * CW], blk[:, 3 * CW:4 * CW]

    def shifted(cu, prev, i):
        prev = jnp.where(i > 0, prev, 0.0)
        rid = lax.broadcasted_iota(jnp.int32, cu.shape, 0)
        last, last2 = prev[HALO - 1:HALO, :], prev[HALO - 2:HALO - 1, :]
        sh1 = jnp.where(rid == 0, last, pltpu.roll(cu, 1, 0))
        sh2 = jnp.where(rid == 0, last2, jnp.where(rid == 1, last, pltpu.roll(cu, 2, 0)))
        return sh1, sh2

    def conv_fwd_kern(p_ref, prev_ref, w_ref, o_ref):
        i = pl.program_id(0)
        cg, bg, u, z = conv_parts(p_ref[...])
        pc, _, pu, _ = conv_parts(prev_ref[...])
        cu = cg * u
        sh1, sh2 = shifted(cu, pc * pu, i)
        w = w_ref[...]
        conv = w[0:1, :] * sh2 + w[1:2, :] * sh1 + w[2:3, :] * cu
        o_ref[...] = (bg * conv * (z * _sig(z))).astype(o_ref.dtype)

    prev_spec = pl.BlockSpec((HALO, 4 * CW), lambda i: (jnp.maximum(i * rh - 1, 0), conv_blk))
    a_conv = pl.pallas_call(
        conv_fwd_kern, name="conv_fwd", grid=(nr,),
        in_specs=[rowb(4 * CW, conv_blk), prev_spec, fullb((8, CW))],
        out_specs=rowb(CW, 0), out_shape=jax.ShapeDtypeStruct((N, CW), _BF),
        compiler_params=_cparams(("parallel",)),
    )(proj, proj, convw8)
    o_conv = _mm(a_conv, wcoT, tb=True, name="mm_oconv", out_dtype=_BF)

    QL, KVL, HW = c.QL, c.KVL, c.HW
    lora_blk = c.o_lora // (QL + KVL)

    def lora_fwd_kern(p_ref, gq_ref, gkv_ref, q_ref, kv_ref):
        blk = p_ref[...].astype(F32)
        qh, _ = _rms(blk[:, :QL], QL)
        kh, _ = _rms(blk[:, QL:], KVL)
        q_ref[...] = (qh * gq_ref[...]).astype(q_ref.dtype)
        kv_ref[...] = (kh * gkv_ref[...]).astype(kv_ref.dtype)

    cqn, ckvn = pl.pallas_call(
        lora_fwd_kern, name="lora_fwd", grid=(nr,),
        in_specs=[rowb(QL + KVL, lora_blk), fullb((1, QL)), fullb((1, KVL))],
        out_specs=[rowb(QL, 0), rowb(KVL, 0)],
        out_shape=[jax.ShapeDtypeStruct((N, QL), _BF), jax.ShapeDtypeStruct((N, KVL), _BF)],
        compiler_params=_cparams(("parallel",)),
    )(proj, mla_q_norm_g, mla_kv_norm_g)
    q_p = _mm(cqn, wuq_pT, tb=True, name="mm_q", out_dtype=_BF, bn=2048)
    kv = _mm(ckvn, wukvT, tb=True, name="mm_kv", out_dtype=_BF, bn=2048)

    g_qn, g_qr = mla_qn_nope_g, pad_lanes(mla_qn_rope_g)
    g_kn, g_kr = mla_kn_nope_g, pad_lanes(mla_kn_rope_g)

    def krope_fwd_kern(p_ref, t_ref, g_ref, o_ref):
        xh, _ = _rms128(p_ref[...].astype(F32), c.ROPE)
        o_ref[...] = _rope(xh * g_ref[...], t_ref[0], t_ref[1], t_ref[2]).astype(o_ref.dtype)

    k_rope = pl.pallas_call(
        krope_fwd_kern, name="krope_fwd", grid=(nr,),
        in_specs=[rowb(LANES, 0), tabs_of(R), fullb((1, LANES))],
        out_specs=rowb(LANES, 0), out_shape=jax.ShapeDtypeStruct((N, LANES), _BF),
        compiler_params=_cparams(("parallel",)),
    )(kr_raw, tabs, g_kr)

    RP, HG = c.RP, c.HG
    nrp, nhg = N // RP, H // HG
    heads_in = pl.BlockSpec((RP, HG * HW), lambda i, hg: (i, hg))
    heads_out = pl.BlockSpec((HG, RP, HW), lambda i, hg: (hg, i, 0))

    def q_prep_kern(q_ref, t_ref, gn_ref, gr_ref, o_ref):
        for g in range(HG):
            blk = q_ref[:, g * HW:(g + 1) * HW].astype(F32)
            nh, _ = _rms128(blk[:, :LANES], c.NOPE)
            rhat, _ = _rms128(blk[:, LANES:], c.ROPE)
            rot = _rope(rhat * gr_ref[...], t_ref[0], t_ref[1], t_ref[2])
            o_ref[g] = jnp.concatenate([nh * gn_ref[...], rot], axis=1).astype(o_ref.dtype)

    q_cat = pl.pallas_call(
        q_prep_kern, name="q_prep", grid=(nrp, nhg),
        in_specs=[heads_in, tabs_of(RP), fullb((1, LANES)), fullb((1, LANES))],
        out_specs=heads_out, out_shape=jax.ShapeDtypeStruct((H, N, HW), _BF),
        compiler_params=_cparams(("parallel", "parallel")),
    )(q_p, tabs, g_qn * (c.scale * LOG2E), g_qr * (c.scale * LOG2E))

    def k_prep_kern(kv_ref, kr_ref, gn_ref, o_ref):
        for g in range(HG):
            kn, _ = _rms128(kv_ref[:, g * HW:g * HW + LANES].astype(F32), c.NOPE)
            o_ref[g] = jnp.concatenate([(kn * gn_ref[...]).astype(o_ref.dtype), kr_ref[...]], axis=1)

    k_cat = pl.pallas_call(
        k_prep_kern, name="k_prep", grid=(nrp, nhg),
        in_specs=[heads_in, pl.BlockSpec((RP, LANES), lambda i, hg: (i, 0)), fullb((1, LANES))],
        out_specs=heads_out, out_shape=jax.ShapeDtypeStruct((H, N, HW), _BF),
        compiler_params=_cparams(("parallel", "parallel")),
    )(kv, k_rope, g_kn)

    QB = ATT_QBLOCKS if N % (ATT_QBLOCKS * B) == 0 else 1
    BQ = QB * B
    nq = N // BQ
    assert CHUNK & (CHUNK - 1) == 0 and B % CHUNK == 0 and ATT_UNROLL_FWD % QB == 0 and ATT_UNROLL_BWD % QB == 0

    def live_rows(diag):
        return slice(0 if diag is None else diag * B, BQ)

    def diag_mask(s):
        row = lax.broadcasted_iota(jnp.int32, s.shape, 0)
        col = lax.broadcasted_iota(jnp.int32, s.shape, 1)
        shift = CHUNK.bit_length() - 1
        allowed = jnp.right_shift(col, shift) <= jnp.right_shift(row, shift)
        return jnp.where(allowed, s, NEG)

    k_head = pl.BlockSpec((1, N, HW), lambda hh, i: (hh, 0, 0))
    v_head = pl.BlockSpec((N, LANES), lambda hh, i: (0, 2 * hh + 1))
    q_blk = pl.BlockSpec((1, BQ, HW), lambda hh, i: (hh, i, 0))
    o_blk = pl.BlockSpec((BQ, LANES), lambda hh, i: (i, hh))
    lse_blk = pl.BlockSpec((1, BQ, LANES), lambda hh, i: (hh, i, 0))

    def key_block_plan(i, run, unroll):
        def unrolled(u, carry):
            run(unroll * u, unroll, 0)
            return carry

        n_full = QB * i
        lax.fori_loop(0, n_full // unroll, unrolled, 0)
        for rem in range(0, unroll, QB):
            @pl.when(n_full % unroll == rem)
            def _(rem=rem):
                run(n_full - rem, rem + QB, QB)

    def attn_fwd_kern(q_ref, k_ref, v_ref, o_ref, lse_ref, m_sc, acc_sc, s_sc):
        i = pl.program_id(1)
        m_sc[...] = jnp.full(m_sc.shape, NEG, F32)
        acc_sc[...] = jnp.zeros(acc_sc.shape, F32)

        def rows_of(t):
            return pl.ds(pl.multiple_of(t * B, B), B)

        def scores(t, slot, diag):
            rs = live_rows(diag)
            s_sc[slot, rs, :] = _dot_nt(q_ref[0, rs, :], k_ref[0, rows_of(t), :])

        def softmax_pv(t, slot, diag):
            rs = live_rows(diag)
            s = s_sc[slot, rs, :]
            if diag is not None:
                s = diag_mask(s)
            mt = s[:, 0:LANES]
            for cb in range(1, B // LANES):
                mt = jnp.maximum(mt, s[:, cb * LANES:(cb + 1) * LANES])
            m_prev = m_sc[rs, :]
            m_new = jnp.maximum(m_prev, jnp.max(mt, axis=1, keepdims=True))
            alpha = jnp.exp2(m_prev - m_new)
            p = jnp.concatenate([jnp.exp2(s[:, cb * LANES:(cb + 1) * LANES] - m_new).astype(_BF)
                                 for cb in range(B // LANES)], axis=1)
            v_ones = jnp.concatenate([v_ref[rows_of(t), :], jnp.ones((B, LANES), _BF)], axis=1)
            acc_sc[rs, :] = jnp.concatenate([alpha, alpha], axis=1) * acc_sc[rs, :] + _dot(p, v_ones)
            m_sc[rs, :] = m_new

        scores(0, 0, None)

        def run(first, count, n_diag):
            def diag_of(u):
                return u - (count - n_diag) if count - n_diag <= u < count else None

            for u in range(count):
                if u + 1 < count or n_diag == 0:
                    scores(first + u + 1, (u + 1) % 2, diag_of(u + 1))
                softmax_pv(first + u, u % 2, diag_of(u))

        key_block_plan(i, run, ATT_UNROLL_FWD)
        acc = acc_sc[...]
        o_ref[...] = (acc[:, :LANES] / acc[:, LANES:]).astype(o_ref.dtype)
        lse_ref[0] = m_sc[...] + jnp.log2(acc[:, LANES:])

    mla_y, lse = pl.pallas_call(
        attn_fwd_kern, name="attn_fwd", grid=(H, nq),
        in_specs=[q_blk, k_head, v_head], out_specs=[o_blk, lse_blk],
        out_shape=[jax.ShapeDtypeStruct((N, c.HV), _BF), jax.ShapeDtypeStruct((H, N, LANES), F32)],
        scratch_shapes=[pltpu.VMEM((BQ, LANES), F32), pltpu.VMEM((BQ, HW), F32), pltpu.VMEM((2, BQ, B), F32)],
        compiler_params=_cparams(("parallel", "arbitrary")),
    )(q_cat, k_cat, kv)

    HV = c.HV
    mz_blk = c.o_mz // HV

    def gate_fwd_kern(y_ref, z_ref, o_ref):
        z = z_ref[...].astype(F32)
        o_ref[...] = (y_ref[...].astype(F32) * (z * _sig(z))).astype(o_ref.dtype)

    a_mla = pl.pallas_call(
        gate_fwd_kern, name="gate_mla", grid=(nr2,),
        in_specs=[rowb2(HV, 0), rowb2(HV, mz_blk)], out_specs=rowb2(HV, 0),
        out_shape=jax.ShapeDtypeStruct((N, HV), _BF), compiler_params=_cparams(("parallel",)),
    )(mla_y, proj)
    o_mla = _mm(a_mla, wmo, name="mm_omla", out_dtype=_BF)

    M, MW, MH, MHD = c.M, c.MW, c.MH, c.MHD
    memn = pl.pallas_call(
        make_rms_kern(), name="rms_mem",
        grid=(1,), in_specs=[fullb((M, D)), fullb((1, D))], out_specs=fullb((M, D)),
        out_shape=jax.ShapeDtypeStruct((M, D), _BF), compiler_params=_cparams(("arbitrary",)),
    )(memx, mem_norm_g)
    kvm = _mm(memn, wmkv, name="mm_memkv", out_dtype=F32)

    def memk_fwd_kern(kv_ref, g_ref, k_ref, v_ref):
        for hh in range(MH):
            kh, _ = _rms(kv_ref[:, hh * MHD:(hh + 1) * MHD], MHD)
            k_ref[:, hh * MHD:(hh + 1) * MHD] = (kh * g_ref[...]).astype(k_ref.dtype)
        v_ref[...] = kv_ref[:, MW:].astype(v_ref.dtype)

    mem_k, mem_v = pl.pallas_call(
        memk_fwd_kern, name="memk_fwd", grid=(1,),
        in_specs=[fullb((M, 2 * MW)), fullb((1, MHD))], out_specs=[fullb((M, MW)), fullb((M, MW))],
        out_shape=[jax.ShapeDtypeStruct((M, MW), _BF)] * 2, compiler_params=_cparams(("arbitrary",)),
    )(kvm, mem_kn_g)

    mem_blk = c.o_mem // (2 * MW)

    def mem_head(qz_ref, k_ref, v_ref, g_ref, hh):
        sl = slice(hh * MHD, (hh + 1) * MHD)
        qh, r = _rms(qz_ref[:, sl].astype(F32), MHD)
        qn = (qh * g_ref[...]).astype(_BF)
        s = _dot_nt(qn, k_ref[:, sl]) * c.mscale
        e = jnp.exp(s - jnp.max(s, axis=1, keepdims=True))
        p = e / jnp.sum(e, axis=1, keepdims=True)
        y = _dot(p.astype(_BF), v_ref[:, sl])
        z = qz_ref[:, MW + hh * MHD:MW + (hh + 1) * MHD].astype(F32)
        return sl, qh, r, qn, p, y, z

    def mem_fwd_kern(qz_ref, k_ref, v_ref, g_ref, o_ref):
        for hh in range(MH):
            sl, _, _, _, _, y, z = mem_head(qz_ref, k_ref, v_ref, g_ref, hh)
            o_ref[:, sl] = (y * (z * _sig(z))).astype(o_ref.dtype)

    a_mem = pl.pallas_call(
        mem_fwd_kern, name="mem_fwd", grid=(N // RM,),
        in_specs=[rowb(2 * MW, mem_blk, RM), fullb((M, MW)), fullb((M, MW)), fullb((1, MHD))],
        out_specs=rowb(MW, 0, RM), out_shape=jax.ShapeDtypeStruct((N, MW), _BF),
        compiler_params=_cparams(("parallel",)),
    )(proj, mem_k, mem_v, mem_qn_g)
    o_mem = _mm(a_mem, wmemoT, tb=True, name="mm_omem", out_dtype=_BF)

    g_blk = c.o_g // (3 * D)

    def merge_fwd_kern(g_ref, oc_ref, om_ref, ome_ref, o_ref):
        g = g_ref[...].astype(F32)
        acc = _sig(g[:, :D]) * oc_ref[...].astype(F32)
        acc += _sig(g[:, D:2 * D]) * om_ref[...].astype(F32)
        acc += _sig(g[:, 2 * D:]) * ome_ref[...].astype(F32)
        o_ref[...] = acc.astype(o_ref.dtype)

    merged = pl.pallas_call(
        merge_fwd_kern, name="merge_fwd", grid=(nr,),
        in_specs=[rowb(3 * D, g_blk), rowb(D, 0), rowb(D, 0), rowb(D, 0)], out_specs=rowb(D, 0),
        out_shape=jax.ShapeDtypeStruct((N, D), _BF), compiler_params=_cparams(("parallel",)),
    )(proj, o_conv, o_mla, o_mem)
    obm, obn = _tile(N, 512, 16), _tile(D, 1024, LANES)

    def out_loss_kern(a_ref, w_ref, x_ref, t_ref, dyb_ref, l_ref):
        e = x_ref[...] + _dot(a_ref[...], w_ref[...]) - t_ref[...]
        dyb_ref[...] = (e * (1.0 / D)).astype(dyb_ref.dtype)
        row = lax.broadcasted_iota(jnp.int32, l_ref.shape, 0)
        l_ref[...] = jnp.where(row == 0, jnp.sum(e * e, axis=0, keepdims=True), 0.0)

    oblk = pl.BlockSpec((obm, obn), lambda i, j: (i, j))
    dyb, lpart = pl.pallas_call(
        out_loss_kern, name="mm_out_loss", grid=(N // obm, D // obn),
        in_specs=[pl.BlockSpec((obm, D), lambda i, j: (i, 0)), pl.BlockSpec((D, obn), lambda i, j: (0, j)), oblk, oblk],
        out_specs=[oblk, pl.BlockSpec((8, obn), lambda i, j: (i, j))],
        out_shape=[jax.ShapeDtypeStruct((N, D), _BF), jax.ShapeDtypeStruct((8 * (N // obm), D), F32)],
        compiler_params=_cparams(("parallel", "parallel")),
    )(merged, wo, xs, tgt)
    loss = lax.psum(jnp.sum(lpart) * (0.5 / D), AXES)

    G = {}
    d_merged = _mm(dyb, wo, tb=True, name="mm_dmerged", out_dtype=_BF)
    G['w_o'] = _mm(merged, dyb, ta=True, name="mm_dwo", out_dtype=_BF, bk=4096)

    dproj0 = lax.empty((N, c.P), _BF)
    any_spec = pl.BlockSpec(memory_space=pl.ANY)

    def merge_bwd_kern(dp_any, g_ref, dm_ref, oc_ref, om_ref, ome_ref, dg_ref, doc_ref, dom_ref, dome_ref):
        g = g_ref[...].astype(F32)
        dm = dm_ref[...].astype(F32)
        for idx, (o_in, d_out) in enumerate(((oc_ref, doc_ref), (om_ref, dom_ref), (ome_ref, dome_ref))):
            sg = _sig(g[:, idx * D:(idx + 1) * D])
            d_out[...] = (sg * dm).astype(d_out.dtype)
            dg_ref[:, idx * D:(idx + 1) * D] = (dm * o_in[...].astype(F32) * sg * (1.0 - sg)).astype(dg_ref.dtype)

    dproj1, d_oconv, d_omla, d_omem = pl.pallas_call(
        merge_bwd_kern, name="merge_bwd", grid=(nr,),
        in_specs=[any_spec, rowb(3 * D, g_blk), rowb(D, 0), rowb(D, 0), rowb(D, 0), rowb(D, 0)],
        out_specs=[rowb(3 * D, g_blk), rowb(D, 0), rowb(D, 0), rowb(D, 0)],
        out_shape=[jax.ShapeDtypeStruct((N, c.P), _BF)] + [jax.ShapeDtypeStruct((N, D), _BF)] * 3,
        input_output_aliases={0: 0}, compiler_params=_cparams(("parallel",)),
    )(dproj0, proj, d_merged, o_conv, o_mla, o_mem)

    G['w_conv_out'] = _mm(a_conv, d_oconv, ta=True, name="mm_dwco", out_dtype=_BF, bk=4096)
    d_aconv = _mm(d_oconv, wcoT, name="mm_daconv", out_dtype=_BF)
    G['w_mla_out'] = _mm(a_mla, d_omla, ta=True, name="mm_dwmo", out_dtype=_BF, bk=4096)
    d_amla = _mm(d_omla, wmo, tb=True, name="mm_damla", out_dtype=_BF)
    G['w_mem_out'] = _mm(a_mem, d_omem, ta=True, name="mm_dwmemo", out_dtype=_BF, bk=4096)
    d_amem = _mm(d_omem, wmemoT, name="mm_damem", out_dtype=_BF)

    def conv_bwd_kern(dp_any, p_ref, prev_ref, next_ref, da_ref, dan_ref, w_ref, o_ref, dw_ref):
        i = pl.program_id(0)
        cg, bg, u, z = conv_parts(p_ref[...])
        pc, _, pu, _ = conv_parts(prev_ref[...])
        _, nbg, _, nz = conv_parts(next_ref[...])
        cu = cg * u
        sh1, sh2 = shifted(cu, pc * pu, i)
        w = w_ref[...]
        conv = w[0:1, :] * sh2 + w[1:2, :] * sh1 + w[2:3, :] * cu
        sg = _sig(z)
        sz = z * sg
        da = da_ref[...].astype(F32)
        dcy = da * sz
        d_z = da * (bg * conv) * (sg * (1.0 + z * (1.0 - sg)))
        d_b = dcy * conv
        dconv = dcy * bg
        dnext = dan_ref[...].astype(F32) * (nz * _sig(nz)) * nbg
        dnext = jnp.where(i < nr - 1, dnext, 0.0)
        rid = lax.broadcasted_iota(jnp.int32, cu.shape, 0)
        up1 = jnp.where(rid == R - 1, dnext[0:1, :], pltpu.roll(dconv, R - 1, 0))
        up2 = jnp.where(rid == R - 2, dnext[0:1, :], jnp.where(rid == R - 1, dnext[1:2, :], pltpu.roll(dconv, R - 2, 0)))
        dcu = w[2:3, :] * dconv + w[1:2, :] * up1 + w[0:1, :] * up2
        o_ref[:, 0:CW] = (dcu * u).astype(o_ref.dtype)
        o_ref[:, CW:2 * CW] = d_b.astype(o_ref.dtype)
        o_ref[:, 2 * CW:3 * CW] = (dcu * cg).astype(o_ref.dtype)
        o_ref[:, 3 * CW:4 * CW] = d_z.astype(o_ref.dtype)

        @pl.when(i == 0)
        def _():
            dw_ref[...] = jnp.zeros(dw_ref.shape, F32)

        dw_ref[0:1, :] += jnp.sum(dconv * sh2, axis=0, keepdims=True)
        dw_ref[1:2, :] += jnp.sum(dconv * sh1, axis=0, keepdims=True)
        dw_ref[2:3, :] += jnp.sum(dconv * cu, axis=0, keepdims=True)

    next_spec = pl.BlockSpec((HALO, 4 * CW), lambda i: (jnp.minimum((i + 1) * rh, N // HALO - 1), conv_blk))
    dan_spec = pl.BlockSpec((HALO, CW), lambda i: (jnp.minimum((i + 1) * rh, N // HALO - 1), 0))
    dproj2, g_convw = pl.pallas_call(
        conv_bwd_kern, name="conv_bwd", grid=(nr,),
        in_specs=[any_spec, rowb(4 * CW, conv_blk), prev_spec, next_spec, rowb(CW, 0), dan_spec, fullb((8, CW))],
        out_specs=[rowb(4 * CW, conv_blk), fullb((8, CW))],
        out_shape=[jax.ShapeDtypeStruct((N, c.P), _BF), jax.ShapeDtypeStruct((8, CW), F32)],
        input_output_aliases={0: 0}, compiler_params=_cparams(("arbitrary",)),
    )(dproj1, proj, proj, proj, d_aconv, d_aconv, convw8)

    def mem_bwd_kern(dp_any, qz_ref, da_ref, k_ref, v_ref, g_ref, o_ref, dk_ref, dv_ref, dg_ref):
        @pl.when(pl.program_id(0) == 0)
        def _():
            dk_ref[...] = jnp.zeros(dk_ref.shape, F32)
            dv_ref[...] = jnp.zeros(dv_ref.shape, F32)
            dg_ref[...] = jnp.zeros(dg_ref.shape, F32)

        for hh in range(MH):
            sl, qh, r, qn, p, y, z = mem_head(qz_ref, k_ref, v_ref, g_ref, hh)
            da = da_ref[:, sl].astype(F32)
            sg = _sig(z)
            dyh = da * (z * sg)
            o_ref[:, MW + hh * MHD:MW + (hh + 1) * MHD] = (da * y * (sg * (1.0 + z * (1.0 - sg)))).astype(o_ref.dtype)
            dyb_h = dyh.astype(_BF)
            dpm = _dot_nt(dyb_h, v_ref[:, sl])
            ds = (p * (dpm - jnp.sum(dpm * p, axis=1, keepdims=True)) * c.mscale).astype(_BF)
            dqn = _dot(ds, k_ref[:, sl])
            dk_ref[:, sl] += _dot_tn(ds, qn)
            dv_ref[:, sl] += _dot_tn(p.astype(_BF), dyb_h)
            dq, dgp = _rms_bwd(qh, r, g_ref[...], dqn, MHD)
            o_ref[:, sl] = dq.astype(o_ref.dtype)
            dg_ref[...] += jnp.sum(dgp, axis=0, keepdims=True)

    dproj3, d_memk, d_memv, g_mem_qn = pl.pallas_call(
        mem_bwd_kern, name="mem_bwd", grid=(N // RM,),
        in_specs=[any_spec, rowb(2 * MW, mem_blk, RM), rowb(MW, 0, RM), fullb((M, MW)), fullb((M, MW)), fullb((1, MHD))],
        out_specs=[rowb(2 * MW, mem_blk, RM), fullb((M, MW)), fullb((M, MW)), fullb((1, MHD))],
        out_shape=[jax.ShapeDtypeStruct((N, c.P), _BF), jax.ShapeDtypeStruct((M, MW), F32),
                   jax.ShapeDtypeStruct((M, MW), F32), jax.ShapeDtypeStruct((1, MHD), F32)],
        input_output_aliases={0: 0}, compiler_params=_cparams(("arbitrary",)),
    )(dproj2, proj, d_amem, mem_k, mem_v, mem_qn_g)

    def memk_bwd_kern(kv_ref, dk_ref, dv_ref, g_ref, o_ref, dg_ref):
        dg = jnp.zeros((1, MHD), F32)
        for hh in range(MH):
            sl = slice(hh * MHD, (hh + 1) * MHD)
            kh, r = _rms(kv_ref[:, sl], MHD)
            dkr, dgp = _rms_bwd(kh, r, g_ref[...], dk_ref[:, sl], MHD)
            o_ref[:, sl] = dkr.astype(o_ref.dtype)
            dg += jnp.sum(dgp, axis=0, keepdims=True)
        o_ref[:, MW:] = dv_ref[...].astype(o_ref.dtype)
        dg_ref[...] = dg

    d_kvm, g_mem_kn = pl.pallas_call(
        memk_bwd_kern, name="memk_bwd", grid=(1,),
        in_specs=[fullb((M, 2 * MW)), fullb((M, MW)), fullb((M, MW)), fullb((1, MHD))],
        out_specs=[fullb((M, 2 * MW)), fullb((1, MHD))],
        out_shape=[jax.ShapeDtypeStruct((M, 2 * MW), _BF), jax.ShapeDtypeStruct((1, MHD), F32)],
        compiler_params=_cparams(("arbitrary",)),
    )(kvm, d_memk, d_memv, mem_kn_g)
    G['w_mem_kv'] = _mm(memn, d_kvm, ta=True, name="mm_dwmkv", out_dtype=_BF)
    d_memn = _mm(d_kvm, wmkv, tb=True, name="mm_dmemn", out_dtype=F32)

    def memnorm_bwd_kern(x_ref, d_ref, dg_ref):
        xh, _ = _rms(x_ref[...], D)
        dg_ref[...] = jnp.sum(d_ref[...] * xh, axis=0, keepdims=True)

    g_mem_norm = pl.pallas_call(
        memnorm_bwd_kern, name="memnorm_bwd", grid=(1,),
        in_specs=[fullb((M, D)), fullb((M, D))], out_specs=fullb((1, D)),
        out_shape=jax.ShapeDtypeStruct((1, D), F32), compiler_params=_cparams(("arbitrary",)),
    )(memx, d_memn)

    def gate_bwd_kern(dp_any, da_ref, y_ref, z_ref, dy_ref, dz_ref):
        z = z_ref[...].astype(F32)
        da = da_ref[...].astype(F32)
        sg = _sig(z)
        dy_ref[...] = (da * (z * sg)).astype(dy_ref.dtype)
        dz_ref[...] = (da * y_ref[...].astype(F32) * (sg * (1.0 + z * (1.0 - sg)))).astype(dz_ref.dtype)

    d_mlay, dproj4 = pl.pallas_call(
        gate_bwd_kern, name="gate_mla_bwd", grid=(nr2,),
        in_specs=[any_spec, rowb2(HV, 0), rowb2(HV, 0), rowb2(HV, mz_blk)],
        out_specs=[rowb2(HV, 0), rowb2(HV, mz_blk)],
        out_shape=[jax.ShapeDtypeStruct((N, HV), _BF), jax.ShapeDtypeStruct((N, c.P), _BF)],
        input_output_aliases={0: 1}, compiler_params=_cparams(("parallel",)),
    )(dproj3, d_amla, mla_y, proj)

    def attn_bwd_kern(q_ref, k_ref, v_ref, o_ref, do_ref, lse_ref, dq_ref, dk_ref, dv_ref, dq_sc, dl_sc, dk_sc, dv_sc):
        i = pl.program_id(1)
        delta = jnp.sum(do_ref[...].astype(F32) * o_ref[...].astype(F32), axis=1, keepdims=True)
        dl_sc[...] = jnp.broadcast_to(delta, dl_sc.shape)
        dq_sc[...] = jnp.zeros(dq_sc.shape, F32)

        def step(t, diag):
            rows = pl.ds(pl.multiple_of(t * B, B), B)
            rs = live_rows(diag)
            q, do = q_ref[0, rs, :], do_ref[rs, :]
            k = k_ref[0, rows, :]
            s = _dot_nt(q, k)
            if diag is not None:
                s = diag_mask(s)
            dpm = _dot_nt(do, v_ref[rows, :])
            lse_t, dl = lse_ref[0, rs, :], dl_sc[rs, :]
            ps, dss = [], []
            for cb in range(B // LANES):
                cols = slice(cb * LANES, (cb + 1) * LANES)
                p_cb = jnp.exp2(s[:, cols] - lse_t)
                ps.append(p_cb.astype(_BF))
                dss.append((p_cb * (dpm[:, cols] - dl)).astype(_BF))
            p, ds = jnp.concatenate(ps, axis=1), jnp.concatenate(dss, axis=1)
            dvp = _dot_tn(p, do)
            dkp = _dot_tn(ds, q)
            if diag is not None:
                dk_sc[rows, :] = dkp
                dv_sc[rows, :] = dvp
            else:
                dk_sc[rows, :] += dkp
                dv_sc[rows, :] += dvp
            dq_sc[rs, :] += _dot(ds, k)

        def run(first, count, n_diag):
            for u in range(count):
                step(first + u, u - (count - n_diag) if u >= count - n_diag else None)

        key_block_plan(i, run, ATT_UNROLL_BWD)
        dq_ref[0] = dq_sc[...].astype(dq_ref.dtype)

        @pl.when(i == nq - 1)
        def _():
            dk_ref[0] = dk_sc[...].astype(dk_ref.dtype)
            dv_ref[0] = dv_sc[...].astype(dv_ref.dtype)

    d_qcat, d_kcat, d_v = pl.pallas_call(
        attn_bwd_kern, name="attn_bwd", grid=(H, nq),
        in_specs=[q_blk, k_head, v_head, o_blk, o_blk, lse_blk],
        out_specs=[pl.BlockSpec((1, BQ, HW), lambda hh, i: (hh, i, 0)),
                   pl.BlockSpec((1, N, HW), lambda hh, i: (hh, 0, 0)),
                   pl.BlockSpec((1, N, LANES), lambda hh, i: (hh, 0, 0))],
        out_shape=[jax.ShapeDtypeStruct((H, N, HW), _BF), jax.ShapeDtypeStruct((H, N, HW), _BF),
                   jax.ShapeDtypeStruct((H, N, LANES), _BF)],
        scratch_shapes=[pltpu.VMEM((BQ, HW), F32), pltpu.VMEM((BQ, LANES), F32), pltpu.VMEM((N, HW), F32),
                        pltpu.VMEM((N, LANES), F32)],
        compiler_params=_cparams(("parallel", "arbitrary")),
    )(q_cat, k_cat, kv, mla_y, d_mlay, lse)

    def q_prep_bwd_kern(q_ref, dq_ref, t_ref, gn_ref, gr_ref, o_ref, dgn_ref, dgr_ref):
        @pl.when(jnp.logical_and(pl.program_id(0) == 0, pl.program_id(1) == 0))
        def _():
            dgn_ref[...] = jnp.zeros(dgn_ref.shape, F32)
            dgr_ref[...] = jnp.zeros(dgr_ref.shape, F32)

        for g in range(HG):
            blk = q_ref[:, g * HW:(g + 1) * HW].astype(F32)
            d = dq_ref[g].astype(F32)
            nh, rn = _rms128(blk[:, :LANES], c.NOPE)
            rhat, rr = _rms128(blk[:, LANES:], c.ROPE)
            dn, dgn = _rms128_bwd(nh, rn, gn_ref[...], d[:, :LANES], c.NOPE)
            drot = _rope_t(d[:, LANES:], t_ref[0], t_ref[1], t_ref[2])
            dr, dgr = _rms128_bwd(rhat, rr, gr_ref[...], drot, c.ROPE)
            o_ref[:, g * HW:(g + 1) * HW] = jnp.concatenate([dn, dr], axis=1).astype(o_ref.dtype)
            dgn_ref[...] += jnp.sum(dgn, axis=0, keepdims=True)
            dgr_ref[...] += jnp.sum(dgr, axis=0, keepdims=True)

    d_qp, g_qn_nope, g_qn_rope = pl.pallas_call(
        q_prep_bwd_kern, name="q_prep_bwd", grid=(nrp, nhg),
        in_specs=[heads_in, heads_out, tabs_of(RP), fullb((1, LANES)), fullb((1, LANES))],
        out_specs=[heads_in, fullb((1, LANES)), fullb((1, LANES))],
        out_shape=[jax.ShapeDtypeStruct((N, H * HW), _BF), jax.ShapeDtypeStruct((1, LANES), F32),
                   jax.ShapeDtypeStruct((1, LANES), F32)],
        compiler_params=_cparams(("arbitrary", "arbitrary")),
    )(q_p, d_qcat, tabs, g_qn * c.scale, g_qr * c.scale)
    g_qn_nope, g_qn_rope = g_qn_nope * c.scale, g_qn_rope * c.scale

    def k_prep_bwd_kern(kv_ref, dk_ref, dv_ref, gn_ref, o_ref, dkr_ref, dgn_ref):
        hg = pl.program_id(1)

        @pl.when(jnp.logical_and(pl.program_id(0) == 0, hg == 0))
        def _():
            dgn_ref[...] = jnp.zeros(dgn_ref.shape, F32)

        @pl.when(hg == 0)
        def _():
            dkr_ref[...] = jnp.zeros(dkr_ref.shape, F32)

        dkr = jnp.zeros((RP, LANES), F32)
        for g in range(HG):
            dk = dk_ref[g].astype(F32) * (1.0 / LOG2E)
            kn, r = _rms128(kv_ref[:, g * HW:g * HW + LANES].astype(F32), c.NOPE)
            dkn, dgn = _rms128_bwd(kn, r, gn_ref[...], dk[:, :LANES], c.NOPE)
            o_ref[:, g * HW:(g + 1) * HW] = jnp.concatenate([dkn.astype(o_ref.dtype), dv_ref[g]], axis=1)
            dgn_ref[...] += jnp.sum(dgn, axis=0, keepdims=True)
            dkr += dk[:, LANES:]
        dkr_ref[...] += dkr

    d_kv, d_krsum, g_kn_nope = pl.pallas_call(
        k_prep_bwd_kern, name="k_prep_bwd", grid=(nrp, nhg),
        in_specs=[heads_in, heads_out, pl.BlockSpec((HG, RP, LANES), lambda i, hg: (hg, i, 0)), fullb((1, LANES))],
        out_specs=[heads_in, pl.BlockSpec((RP, LANES), lambda i, hg: (i, 0)), fullb((1, LANES))],
        out_shape=[jax.ShapeDtypeStruct((N, H * HW), _BF), jax.ShapeDtypeStruct((N, LANES), F32),
                   jax.ShapeDtypeStruct((1, LANES), F32)],
        compiler_params=_cparams(("arbitrary", "arbitrary")),
    )(kv, d_kcat, d_v, g_kn)

    def krope_bwd_kern(p_ref, d_ref, t_ref, g_ref, o_ref, dg_ref):
        @pl.when(pl.program_id(0) == 0)
        def _():
            dg_ref[...] = jnp.zeros(dg_ref.shape, F32)

        xh, r = _rms128(p_ref[...].astype(F32), c.ROPE)
        drot = _rope_t(d_ref[...], t_ref[0], t_ref[1], t_ref[2])
        dx, dg = _rms128_bwd(xh, r, g_ref[...], drot, c.ROPE)
        o_ref[...] = dx.astype(o_ref.dtype)
        dg_ref[...] += jnp.sum(dg, axis=0, keepdims=True)

    d_kr, g_kn_rope = pl.pallas_call(
        krope_bwd_kern, name="krope_bwd", grid=(nr,),
        in_specs=[rowb(LANES, 0), rowb(LANES, 0), tabs_of(R), fullb((1, LANES))],
        out_specs=[rowb(LANES, 0), fullb((1, LANES))],
        out_shape=[jax.ShapeDtypeStruct((N, LANES), _BF), jax.ShapeDtypeStruct((1, LANES), F32)],
        compiler_params=_cparams(("arbitrary",)),
    )(kr_raw, d_krsum, tabs, g_kr)
    dproj5 = dproj4

    g_wuq_p = _mm(cqn, d_qp, ta=True, name="mm_dwuq", out_dtype=_BF, bk=4096)
    G['w_uq'] = g_wuq_p.reshape(QL, H, HW)[:, :, :c.NOPE + c.ROPE].reshape(QL, H * (c.NOPE + c.ROPE))
    d_cqn = _mm(d_qp, wuq_pT, name="mm_dcqn", out_dtype=F32, bk=4096)
    G['w_ukv'] = _mm(ckvn, d_kv, ta=True, name="mm_dwukv", out_dtype=_BF, bk=4096)
    d_ckvn = _mm(d_kv, wukvT, name="mm_dckvn", out_dtype=F32, bk=4096)

    def to_blocks(n, g):
        if n in COL_SHARDED:
            return jnp.transpose(g.reshape(g.shape[0], NDEV, -1), (1, 0, 2))
        return g.reshape(NDEV, -1, g.shape[1])

    def landing(b):
        return lax.empty(b.shape, b.dtype)

    early = [n for n in BIG if n != 'w_in']
    blocks_e = [to_blocks(n, G[n]) for n in early]
    xe = _split_start(blocks_e, [landing(b) for b in blocks_e], False, "xchg_early_start")
    gq_after = mla_q_norm_g + xe[4][0:1, 0:1]

    def lora_bwd_kern(dp_any, p_ref, dq_ref, dkv_ref, gq_ref, gkv_ref, o_ref, dgq_ref, dgkv_ref):
        @pl.when(pl.program_id(0) == 0)
        def _():
            dgq_ref[...] = jnp.zeros(dgq_ref.shape, F32)
            dgkv_ref[...] = jnp.zeros(dgkv_ref.shape, F32)

        blk = p_ref[...].astype(F32)
        qh, rq = _rms(blk[:, :QL], QL)
        kh, rk = _rms(blk[:, QL:], KVL)
        dq, dgq = _rms_bwd(qh, rq, gq_ref[...], dq_ref[...], QL)
        dk, dgk = _rms_bwd(kh, rk, gkv_ref[...], dkv_ref[...], KVL)
        o_ref[:, :QL] = dq.astype(o_ref.dtype)
        o_ref[:, QL:] = dk.astype(o_ref.dtype)
        dgq_ref[...] += jnp.sum(dgq, axis=0, keepdims=True)
        dgkv_ref[...] += jnp.sum(dgk, axis=0, keepdims=True)

    dproj6, g_q_norm, g_kv_norm = pl.pallas_call(
        lora_bwd_kern, name="lora_bwd", grid=(nr,),
        in_specs=[any_spec, rowb(QL + KVL, lora_blk), rowb(QL, 0), rowb(KVL, 0), fullb((1, QL)), fullb((1, KVL))],
        out_specs=[rowb(QL + KVL, lora_blk), fullb((1, QL)), fullb((1, KVL))],
        out_shape=[jax.ShapeDtypeStruct((N, c.P), _BF), jax.ShapeDtypeStruct((1, QL), F32),
                   jax.ShapeDtypeStruct((1, KVL), F32)],
        input_output_aliases={0: 0}, compiler_params=_cparams(("arbitrary",)),
    )(dproj5, proj, d_cqn, d_ckvn, gq_after, mla_kv_norm_g)

    g_win_p = _mm(h, dproj6, ta=True, name="mm_dwin", out_dtype=_BF, bk=4096)
    g_wkr = _mm(h, d_kr, ta=True, name="mm_dwkr", out_dtype=_BF, bk=4096)
    blocks_w = [_win_blocks(g_win_p, g_wkr, c)]
    xw = _split_start(blocks_w, [landing(b) for b in blocks_w], False, "xchg_win_start")
    d_h = _mm(dproj6, win_pT, name="mm_dh", out_dtype=F32, bk=3072, after=xw[4], plus=(d_kr, w_krT))

    def final_bwd_kern(x_ref, g_ref, dh_ref, dy_ref, gx_ref, dg_ref):
        @pl.when(pl.program_id(0) == 0)
        def _():
            dg_ref[...] = jnp.zeros(dg_ref.shape, F32)

        xh, r = _rms(x_ref[...], D)
        dx, dg = _rms_bwd(xh, r, g_ref[...], dh_ref[...], D)
        gx_ref[...] = dy_ref[...].astype(F32) + dx
        dg_ref[...] += jnp.sum(dg, axis=0, keepdims=True)

    grad_x, g_norm = pl.pallas_call(
        final_bwd_kern, name="final_bwd", grid=(nr2,),
        in_specs=[rowb2(D, 0), fullb((1, D)), rowb2(D, 0), rowb2(D, 0)],
        out_specs=[rowb2(D, 0), fullb((1, D))],
        out_shape=[jax.ShapeDtypeStruct((N, D), F32), jax.ShapeDtypeStruct((1, D), F32)],
        compiler_params=_cparams(("arbitrary",)),
    )(xs, norm_g, d_h, dyb)

    res = [{}, {}, {}, {}]

    def adam_into(n, parts):
        outs = _adam(parts, W[n][0], Mo[n][0], Vo[n][0], "adam_" + n)
        for k in range(4):
            res[k][n] = outs[k][None]
        return outs[0]

    recv_e = _split_wait(xe, False, grad_x, "xchg_early_wait")
    last = [adam_into(n, parts) for n, parts in zip(early, recv_e)][-1]
    recv_w = _split_wait(xw, False, last, "xchg_win_wait")
    adam_into('w_in', recv_w[0])

    small_g = {'norm_g': g_norm, 'mla_q_norm_g': g_q_norm, 'mla_kv_norm_g': g_kv_norm,
               'mla_qn_nope_g': g_qn_nope, 'mla_qn_rope_g': g_qn_rope[:, :c.ROPE], 'mla_kn_nope_g': g_kn_nope,
               'mla_kn_rope_g': g_kn_rope[:, :c.ROPE], 'mem_norm_g': g_mem_norm, 'mem_qn_g': g_mem_qn,
               'mem_kn_g': g_mem_kn}
    small_part = _pack([small_g[n] for n in SMALL] + [g_convw[0:3, :]], 0)
    small_all = _all_gather([small_part], "ag_small_grads")[0]
    small_shapes = [W[n].shape for n in SMALL]
    pieces = _unpack(small_all, small_shapes + [(3, CW)])
    cw8 = CW // NDEV
    conv_mine = lax.dynamic_slice_in_dim(pieces[-1].reshape(NDEV, 3, NDEV, cw8), me, 1, axis=2)[:, :, 0, :]
    sm_parts = _pack(pieces[:-1] + [conv_mine], 1)
    sm_names = SMALL + ['conv_w']
    sm_shapes = small_shapes + [(3, cw8)]
    w_sm = _pack([W[n] for n in SMALL] + [conv_w[0]], 0)
    m_sm = _pack([Mo[n] for n in SMALL] + [m_conv_w[0]], 0)
    v_sm = _pack([Vo[n] for n in SMALL] + [v_conv_w[0]], 0)
    outs_sm = [_unpack(o, sm_shapes) for o in _adam(sm_parts, w_sm, m_sm, v_sm, "adam_small")]
    for k in range(4):
        for n, a in zip(sm_names, outs_sm[k]):
            res[k][n] = a[None] if n == 'conv_w' else a
    return (loss, grad_x[None], *[res[0][n] for n in WEIGHTS], *[res[1][n] for n in WEIGHTS],
            *[res[2][n] for n in WEIGHTS], *[res[3][n] for n in WEIGHTS])
```

```python
import math

import jax
import jax.numpy as jnp
from jax import lax
from jax.experimental import pallas as pl
from jax.experimental.pallas import tpu as pltpu

F32 = jnp.float32
_BF = jnp.bfloat16
EPS = 1e-6
CHUNK = 64
ROPE_THETA = 10000.0
ADAM_LR, ADAM_B1, ADAM_B2, ADAM_EPS, ADAM_WD, ADAM_STEP = 0.001, 0.9, 0.999, 1e-08, 0.01, 10
NDEV = 8
AXES = ("x", "y", "c")
MESH = pl.DeviceIdType.MESH
LANES = 128
NEG = -1e30
LOG2E = math.log2(math.e)
V7X_VMEM_LIMIT = 56 * 1024 * 1024
PACK_C = 1024
ATT_BLOCK = 512
ATT_UNROLL_FWD = 8
ATT_UNROLL_BWD = 6
ATT_QBLOCKS = 2
ADAM_BLOCK_ELEMS = 256 * 1024

WEIGHTS = ['norm_g', 'w_in', 'conv_w', 'w_conv_out', 'mla_q_norm_g', 'w_uq', 'mla_kv_norm_g', 'w_ukv',
           'mla_qn_nope_g', 'mla_qn_rope_g', 'mla_kn_nope_g', 'mla_kn_rope_g', 'w_mla_out', 'mem_norm_g',
           'w_mem_kv', 'mem_qn_g', 'mem_kn_g', 'w_mem_out', 'w_o']
BIG = ['w_in', 'w_conv_out', 'w_uq', 'w_ukv', 'w_mla_out', 'w_mem_kv', 'w_mem_out', 'w_o']
COL_SHARDED = ('w_in', 'w_conv_out', 'w_uq', 'w_ukv', 'w_mem_out')
SMALL = ['norm_g', 'mla_q_norm_g', 'mla_kv_norm_g', 'mla_qn_nope_g', 'mla_qn_rope_g', 'mla_kn_nope_g',
         'mla_kn_rope_g', 'mem_norm_g', 'mem_qn_g', 'mem_kn_g']


def _tile(dim, target, align):
    if dim <= target:
        return dim
    t = target - target % align
    while t > 0:
        if dim % t == 0:
            return t
        t -= align
    raise ValueError(f"no tile for {dim} {target} {align}")


def _cparams(sem):
    return pltpu.CompilerParams(dimension_semantics=sem, vmem_limit_bytes=V7X_VMEM_LIMIT)


def _sig(x):
    return 1.0 / (1.0 + jnp.exp(-x))


def _rms(x, n):
    r = lax.rsqrt(jnp.sum(x * x, axis=-1, keepdims=True) * (1.0 / n) + EPS)
    return x * r, r


def _rms_bwd(xhat, r, g, dy, n):
    dxh = dy * g
    dx = r * (dxh - xhat * (jnp.sum(dxh * xhat, axis=-1, keepdims=True) * (1.0 / n)))
    return dx, dy * xhat


def _rowmean128(x, n):
    return jnp.dot(x.astype(_BF), jnp.full((LANES, LANES), 1.0 / n, _BF), preferred_element_type=F32)


def _rms128(x, n):
    r = lax.rsqrt(_rowmean128(x * x, n) + EPS)
    return x * r, r


def _rms128_bwd(xhat, r, g, dy, n):
    dxh = dy * g
    dx = r * (dxh - xhat * _rowmean128(dxh * xhat, n))
    return dx, dy * xhat


def _rope(x, cosp, sina, sinb):
    return x * cosp + pltpu.roll(x, 96, 1) * sina + pltpu.roll(x, 32, 1) * sinb


def _rope_t(d, cosp, sina, sinb):
    return d * cosp + pltpu.roll(d * sina, 32, 1) + pltpu.roll(d * sinb, 96, 1)


def _dot_nt(a, b):
    return lax.dot_general(a, b, (((1,), (1,)), ((), ())), preferred_element_type=F32)


def _dot_tn(a, b):
    return lax.dot_general(a, b, (((0,), (0,)), ((), ())), preferred_element_type=F32)


def _dot(a, b):
    return jnp.dot(a, b, preferred_element_type=F32)


def _all_gather(shards, name):
    na = len(shards)
    nc = 9
    halves = [s.shape[0] // 32 * 16 if s.shape[0] >= 256 else None for s in shards]

    def body(*refs):
        x_refs, out_refs = refs[:na], refs[na:2 * na]
        send_sems, recv_sems, local_sems = refs[2 * na:]
        x, y, c = lax.axis_index("x"), lax.axis_index("y"), lax.axis_index("c")
        me, sib = (x, y, c), (x, y, 1 - c)
        px, py, pd = (1 - x, y, c), (x, 1 - y, c), (1 - x, 1 - y, c)

        def other_core(p):
            return (p[0], p[1], 1 - p[2])

        def rows(a, blk, part=None):
            r = out_refs[a].at[4 * blk[0] + 2 * blk[1] + blk[2]]
            if part is None or halves[a] is None:
                return r
            rest = shards[a].shape[0] - halves[a]
            return r.at[pl.ds(0, halves[a])] if part == 0 else r.at[pl.ds(halves[a], rest)]

        def copy(a, k, blk, to, part=None, src=None):
            dst = rows(a, blk, part)
            return pltpu.make_async_remote_copy(
                src_ref=dst if src is None else src, dst_ref=dst,
                send_sem=send_sems.at[nc * a + k], recv_sem=recv_sems.at[nc * a + k],
                device_id=to, device_id_type=MESH)

        mine = [pltpu.make_async_copy(x_refs[a], rows(a, me), local_sems.at[a]) for a in range(na)]
        for cp in mine:
            cp.start()
        started = []
        for a in range(na):
            started += [copy(a, 0, me, px, src=x_refs[a]), copy(a, 1, me, py, src=x_refs[a]),
                        copy(a, 2, me, sib, src=x_refs[a])]
        for cp in started:
            cp.start()

        def forward(cp):
            cp.start()
            started.append(cp)

        for a in range(na):
            copy(a, 0, px, me).wait_recv()
            forward(copy(a, 3, px, py, part=0))
            forward(copy(a, 4, px, sib))
        for a in range(na):
            copy(a, 1, py, me).wait_recv()
            if halves[a] is not None:
                forward(copy(a, 5, py, px, part=1))
            forward(copy(a, 6, py, sib))
        for a in range(na):
            copy(a, 3, pd, me, part=0).wait_recv()
            forward(copy(a, 7, pd, sib, part=0))
            if halves[a] is not None:
                copy(a, 5, pd, me, part=1).wait_recv()
                forward(copy(a, 8, pd, sib, part=1))
        for a in range(na):
            copy(a, 2, sib, me).wait_recv()
            copy(a, 4, other_core(px), me).wait_recv()
            copy(a, 6, other_core(py), me).wait_recv()
            copy(a, 7, other_core(pd), me, part=0).wait_recv()
            if halves[a] is not None:
                copy(a, 8, other_core(pd), me, part=1).wait_recv()
        for cp in started:
            cp.wait_send()
        for cp in mine:
            cp.wait()

    any_spec = pl.BlockSpec(memory_space=pl.ANY)
    return pl.pallas_call(
        body, name=name,
        out_shape=[jax.ShapeDtypeStruct((NDEV,) + s.shape, s.dtype) for s in shards],
        in_specs=[any_spec] * na, out_specs=[any_spec] * na,
        scratch_shapes=[pltpu.SemaphoreType.DMA((nc * na,)), pltpu.SemaphoreType.DMA((nc * na,)),
                        pltpu.SemaphoreType.DMA((na,))],
    )(*shards)


_HBM = pl.BlockSpec(memory_space=pltpu.HBM)
_SEM = pl.BlockSpec(memory_space=pltpu.SEMAPHORE)
_EFFECT = pltpu.SideEffectType.DATAFLOW_SIDE_EFFECTING


def _split_copy(a, k, src_refs, land_refs, send_sems, recv_sems, gather, receive_side):
    x, y, c = lax.axis_index("x"), lax.axis_index("y"), lax.axis_index("c")
    me = 4 * x + 2 * y + c
    tx, ty, tc = x ^ ((k + 1) >> 2 & 1), y ^ ((k + 1) >> 1 & 1), c ^ ((k + 1) & 1)
    peer = 4 * tx + 2 * ty + tc
    return pltpu.make_async_remote_copy(
        src_ref=src_refs[a] if gather else src_refs[a].at[peer],
        dst_ref=land_refs[a].at[peer if receive_side else me],
        send_sem=send_sems.at[7 * a + k], recv_sem=recv_sems.at[7 * a + k],
        device_id=(tx, ty, tc), device_id_type=MESH)


def _own_copy(a, na, src_refs, land_refs, send_sems, gather):
    me = 4 * lax.axis_index("x") + 2 * lax.axis_index("y") + lax.axis_index("c")
    return pltpu.make_async_copy(src_refs[a] if gather else src_refs[a].at[me], land_refs[a].at[me],
                                 send_sems.at[7 * na + a])


def _split_start(srcs, lands, gather, name, after=None):
    na = len(srcs)
    extra = [] if after is None else [after]

    def body(*refs):
        src_refs, land_refs = refs[:na], refs[na:2 * na]
        send_sems, recv_sems = refs[2 * na + len(extra)], refs[2 * na + len(extra) + 1]
        token = refs[-1]
        for k in range(7):
            for a in range(na):
                _split_copy(a, k, src_refs, land_refs, send_sems, recv_sems, gather, False).start()
        for a in range(na):
            _own_copy(a, na, src_refs, land_refs, send_sems, gather).start()
        token[...] = jnp.zeros_like(token)

    hbm = [pltpu.HBM(b.shape, b.dtype) for b in list(srcs) + list(lands)]
    outs = pl.pallas_call(
        body, name=name,
        out_shape=(pltpu.SemaphoreType.DMA((8 * na,)), pltpu.SemaphoreType.DMA((7 * na,)), *hbm,
                   jax.ShapeDtypeStruct((8, LANES), F32)),
        in_specs=[_HBM] * (2 * na) + [pl.BlockSpec(memory_space=pl.ANY)] * len(extra),
        out_specs=(_SEM, _SEM, *[_HBM] * (2 * na), pl.BlockSpec(memory_space=pltpu.VMEM)),
        input_output_aliases={j: 2 + j for j in range(2 * na)},
        compiler_params=pltpu.CompilerParams(has_side_effects=_EFFECT),
    )(*[pltpu.with_memory_space_constraint(b, pltpu.HBM) for b in srcs],
      *[pltpu.with_memory_space_constraint(l, pltpu.HBM) for l in lands], *extra)
    return outs[0], outs[1], outs[2:2 + na], outs[2 + na:2 + 2 * na], outs[-1]


def _split_wait(started, gather, after, name):
    send_sems, recv_sems, srcs, lands, _ = started
    na = len(srcs)

    def body(*refs):
        src_refs, land_refs = refs[:na], refs[na:2 * na]
        send_s, recv_s = refs[2 * na], refs[2 * na + 1]
        for k in range(7):
            for a in range(na):
                cp = _split_copy(a, k, src_refs, land_refs, send_s, recv_s, gather, True)
                cp.wait_send()
                cp.wait_recv()
        for a in range(na):
            _own_copy(a, na, src_refs, land_refs, send_s, gather).wait()

    hbm = [pltpu.HBM(b.shape, b.dtype) for b in list(srcs) + list(lands)]
    outs = pl.pallas_call(
        body, name=name, out_shape=tuple(hbm),
        in_specs=[_HBM] * (2 * na) + [_SEM, _SEM, pl.BlockSpec(memory_space=pl.ANY)],
        out_specs=tuple([_HBM] * (2 * na)),
        input_output_aliases={j: j for j in range(2 * na)},
        compiler_params=pltpu.CompilerParams(has_side_effects=_EFFECT),
    )(*srcs, *lands, send_sems, recv_sems, after)
    return outs[na:]


def _seg_rows(size):
    rows = -(-size // PACK_C)
    return -(-rows // 16) * 16


def _pack(arrs, lead):
    parts = []
    for a in arrs:
        lshape = a.shape[:lead]
        f = a.reshape(lshape + (-1,)).astype(F32)
        rows = _seg_rows(f.shape[-1])
        f = jnp.pad(f, [(0, 0)] * lead + [(0, rows * PACK_C - f.shape[-1])])
        parts.append(f.reshape(lshape + (rows, PACK_C)))
    return jnp.concatenate(parts, axis=lead)


def _unpack(buf, shapes):
    lshape = buf.shape[:-2]
    out, r = [], 0
    for shp in shapes:
        size = math.prod(shp)
        rows = _seg_rows(size)
        seg = buf[..., r:r + rows, :].reshape(lshape + (rows * PACK_C,))[..., :size]
        out.append(seg.reshape(lshape + tuple(shp)))
        r += rows
    return out


def _mm(a, b, *, name, out_dtype, ta=False, tb=False, bm=1024, bn=1024, bk=2048, after=None, plus=None):
    if ta:
        kdim, m = a.shape
    else:
        m, kdim = a.shape
    n, k2 = b.shape if tb else b.shape[::-1]
    assert kdim == k2 and not (ta and tb), (a.shape, b.shape)
    bm = _tile(m, bm, LANES if ta else 16)
    bn = _tile(n, bn, LANES)
    bk = _tile(kdim, bk, LANES)
    nk = kdim // bk
    n_after = 0 if after is None else 1
    n_plus = 0 if plus is None else 2

    def kern(a_ref, b_ref, *rest):
        plus_refs = rest[n_after:n_after + n_plus]
        o_ref, scratch = rest[n_after + n_plus], rest[n_after + n_plus + 1:]
        part = (_dot_tn if ta else _dot_nt if tb else _dot)(a_ref[...], b_ref[...])

        def first(p):
            return p + _dot(plus_refs[0][...], plus_refs[1][...]) if plus is not None else p

        if nk == 1:
            o_ref[...] = first(part).astype(o_ref.dtype)
        else:
            acc = scratch[0] if scratch else o_ref
            k = pl.program_id(2)

            @pl.when(k == 0)
            def _():
                acc[...] = first(jnp.zeros(acc.shape, F32))

            acc[...] += part
            if scratch:
                @pl.when(k == nk - 1)
                def _():
                    o_ref[...] = acc[...].astype(o_ref.dtype)

    a_spec = pl.BlockSpec((bk, bm), lambda i, j, k: (k, i)) if ta else pl.BlockSpec((bm, bk), lambda i, j, k: (i, k))
    b_spec = pl.BlockSpec((bn, bk), lambda i, j, k: (j, k)) if tb else pl.BlockSpec((bk, bn), lambda i, j, k: (k, j))
    extra_specs, extra_args = [], []
    if after is not None:
        extra_specs.append(pl.BlockSpec(after.shape, lambda i, j, k: (0, 0)))
        extra_args.append(after)
    if plus is not None:
        kk = plus[0].shape[1]
        extra_specs += [pl.BlockSpec((bm, kk), lambda i, j, k: (i, 0)), pl.BlockSpec((kk, bn), lambda i, j, k: (0, j))]
        extra_args += list(plus)
    return pl.pallas_call(
        kern, name=name, grid=(m // bm, n // bn, nk),
        in_specs=[a_spec, b_spec] + extra_specs,
        out_specs=pl.BlockSpec((bm, bn), lambda i, j, k: (i, j)),
        out_shape=jax.ShapeDtypeStruct((m, n), out_dtype),
        scratch_shapes=[pltpu.VMEM((bm, bn), F32)] if nk > 1 and out_dtype != F32 else [],
        compiler_params=_cparams(("parallel", "parallel", "arbitrary")),
    )(a, b, *extra_args)


def _adam(parts, w_a, m_a, v_a, name):
    rows, cols = w_a.shape
    rb = _tile(rows, max(8, ADAM_BLOCK_ELEMS // cols // 8 * 8), 8)
    bc1 = 1.0 - ADAM_B1 ** ADAM_STEP
    bc2 = 1.0 - ADAM_B2 ** ADAM_STEP

    def adam_kern(p_ref, w_ref, m_ref, v_ref, g_ref, d_ref, nm_ref, nv_ref):
        g = p_ref[0].astype(F32)
        for j in range(1, NDEV):
            g = g + p_ref[j].astype(F32)
        m_new = ADAM_B1 * m_ref[...] + (1.0 - ADAM_B1) * g
        v_new = ADAM_B2 * v_ref[...] + (1.0 - ADAM_B2) * (g * g)
        g_ref[...] = g
        nm_ref[...] = m_new
        nv_ref[...] = v_new
        d_ref[...] = -ADAM_LR * ((m_new / bc1) / (jnp.sqrt(v_new / bc2) + ADAM_EPS) + ADAM_WD * w_ref[...])

    blk = pl.BlockSpec((rb, cols), lambda i: (i, 0))
    return pl.pallas_call(
        adam_kern, name=name, grid=(rows // rb,),
        in_specs=[pl.BlockSpec((NDEV, rb, cols), lambda i: (0, i, 0)), blk, blk, blk],
        out_specs=[blk] * 4, out_shape=[jax.ShapeDtypeStruct((rows, cols), F32)] * 4,
        compiler_params=_cparams(("parallel",)),
    )(parts, w_a, m_a, v_a)


class _Cfg:
    pass


def _config(x, conv_w, w_uq, w_ukv, mla_qn_nope_g, mla_qn_rope_g, mem, mem_qn_g, w_mem_out, w_mla_out):
    c = _Cfg()
    c.N, c.D = x.shape[1], x.shape[2]
    c.CW = conv_w.shape[2] * NDEV
    c.QL, c.KVL = w_uq.shape[1], w_ukv.shape[1]
    c.NOPE, c.ROPE = mla_qn_nope_g.shape[1], mla_qn_rope_g.shape[1]
    c.H = w_uq.shape[2] * NDEV // (c.NOPE + c.ROPE)
    c.V = w_ukv.shape[2] * NDEV // c.H - c.NOPE
    assert c.NOPE == LANES and c.V == LANES and c.ROPE == LANES // 2
    c.HW = 2 * LANES
    c.HV = c.H * c.V
    assert w_mla_out.shape[1] * NDEV == c.HV
    c.M = mem.shape[1]
    c.MHD = mem_qn_g.shape[1]
    c.MW = w_mem_out.shape[1]
    c.MH = c.MW // c.MHD
    c.o_conv = 0
    c.o_mz = 4 * c.CW
    c.o_g = c.o_mz + c.HV
    c.o_mem = c.o_g + 3 * c.D
    c.o_lora = c.o_mem + 2 * c.MW
    c.P = c.o_lora + c.QL + c.KVL
    assert c.o_mz % c.HV == 0 and c.o_g % (3 * c.D) == 0 and c.o_mem % (2 * c.MW) == 0
    assert c.o_lora % (c.QL + c.KVL) == 0 and c.QL % LANES == 0 and c.KVL % LANES == 0
    c.IN = 4 * c.CW + c.QL + c.KVL + c.ROPE + c.HV + 2 * c.MW + 3 * c.D
    c.R = _tile(c.N, 256, 16)
    c.RP = _tile(c.N, 512, 16)
    c.HG = _tile(c.H, 4, 1)
    c.B = _tile(c.N, ATT_BLOCK, CHUNK)
    c.scale = float((c.NOPE + c.ROPE) ** -0.5)
    c.mscale = float(c.MHD ** -0.5)
    return c


def _win_segments(c):
    ref_order = (('conv', 4 * c.CW), ('lora', c.QL + c.KVL), ('kr', c.ROPE), ('mz', c.HV), ('mem', 2 * c.MW), ('g', 3 * c.D))
    mine = {'conv': c.o_conv, 'mz': c.o_mz, 'g': c.o_g, 'mem': c.o_mem, 'lora': c.o_lora, 'kr': 0}
    segs, o = [], 0
    for nm, wd in ref_order:
        segs.append((nm, o, wd, mine[nm]))
        o += wd
    return segs


def _win_split(g_win_t, c):
    n8 = g_win_t.shape[1]

    def rows(a, wd):
        return [g_win_t[j][max(a, j * n8) - j * n8:min(a + wd, (j + 1) * n8) - j * n8]
                for j in range(a // n8, (a + wd - 1) // n8 + 1)]

    segs = {nm: (a, wd) for nm, a, wd, _ in _win_segments(c)}
    main = [p for nm in ('conv', 'mz', 'g', 'mem', 'lora') for p in rows(*segs[nm])]
    kr = jnp.concatenate(rows(*segs['kr']) + [jnp.zeros((LANES - c.ROPE, g_win_t.shape[2]), g_win_t.dtype)], axis=0)
    return jnp.concatenate(main, axis=0), kr


def _win_blocks(g, g_kr, c):
    n8 = c.IN // NDEV
    blocks = []
    for j in range(NDEV):
        lo, hi = j * n8, (j + 1) * n8
        parts = []
        for nm, a, wd, mine in _win_segments(c):
            s, e = max(a, lo), min(a + wd, hi)
            if s < e:
                parts.append((g_kr if nm == 'kr' else g)[:, mine + s - a:mine + e - a])
        blocks.append(jnp.concatenate(parts, axis=1))
    return jnp.stack(blocks, axis=0)


def kernel(x, positions, mem, norm_g, w_in, conv_w, w_conv_out, mla_q_norm_g, w_uq, mla_kv_norm_g, w_ukv, mla_qn_nope_g, mla_qn_rope_g, mla_kn_nope_g, mla_kn_rope_g, w_mla_out, mem_norm_g, w_mem_kv, mem_qn_g, mem_kn_g, w_mem_out, w_o, loss_target, m_norm_g, m_w_in, m_conv_w, m_w_conv_out, m_mla_q_norm_g, m_w_uq, m_mla_kv_norm_g, m_w_ukv, m_mla_qn_nope_g, m_mla_qn_rope_g, m_mla_kn_nope_g, m_mla_kn_rope_g, m_w_mla_out, m_mem_norm_g, m_w_mem_kv, m_mem_qn_g, m_mem_kn_g, m_w_mem_out, m_w_o, v_norm_g, v_w_in, v_conv_w, v_w_conv_out, v_mla_q_norm_g, v_w_uq, v_mla_kv_norm_g, v_w_ukv, v_mla_qn_nope_g, v_mla_qn_rope_g, v_mla_kn_nope_g, v_mla_kn_rope_g, v_w_mla_out, v_mem_norm_g, v_w_mem_kv, v_mem_qn_g, v_mem_kn_g, v_w_mem_out, v_w_o):
    args = dict(locals())
    W = {n: args[n] for n in WEIGHTS}
    Mo = {n: args['m_' + n] for n in WEIGHTS}
    Vo = {n: args['v_' + n] for n in WEIGHTS}
    c = _config(x, conv_w, w_uq, w_ukv, mla_qn_nope_g, mla_qn_rope_g, mem, mem_qn_g, w_mem_out, w_mla_out)
    N, D, R, B, H = c.N, c.D, c.R, c.B, c.H
    assert x.shape[0] == 1
    xs = x[0]
    tgt = loss_target[0]
    memx = mem[0]
    me = 4 * lax.axis_index("x") + 2 * lax.axis_index("y") + lax.axis_index("c")
    nr = N // R

    g_win, g_taps = _all_gather([W['w_in'][0].astype(_BF).T, conv_w[0]], "ag_w_in")
    rest = [n for n in BIG if n != 'w_in']
    shards_r = [W[n][0].astype(_BF).T if n in COL_SHARDED else W[n][0].astype(_BF) for n in rest]
    lands_r = [lax.empty((NDEV,) + s.shape, s.dtype) for s in shards_r]
    ag_rest = _split_start(shards_r, lands_r, True, "ag_rest_start", after=g_win)
    win_pT, w_krT = _win_split(g_win, c)
    convw = jnp.transpose(g_taps, (1, 0, 2)).reshape(3, c.CW)
    convw8 = jnp.pad(convw, ((0, 5), (0, 0)))

    def rowb(width, cidx, rows=R):
        return pl.BlockSpec((rows, width), lambda i, _c=cidx: (i, _c))

    R2 = c.RP
    nr2 = N // R2
    RM = _tile(N, 1024, 16)

    def rowb2(width, cidx):
        return rowb(width, cidx, R2)

    def fullb(shape):
        nd = len(shape)
        return pl.BlockSpec(shape, lambda *_: (0,) * nd)

    def pad_lanes(g, w=LANES):
        return jnp.pad(g, ((0, 0), (0, w - g.shape[1])))

    def tabs_of(rows):
        return pl.BlockSpec((3, rows, LANES), lambda i, *_: (0, i, 0))

    half = c.ROPE // 2
    inv_freq = jnp.power(ROPE_THETA, -jnp.arange(half, dtype=F32) / half)
    invf = jnp.concatenate([inv_freq, inv_freq, jnp.zeros((LANES - c.ROPE,), F32)])[None, :]
    pos_col = positions[0].astype(F32).reshape(N, 1)

    def rope_tab_kern(pos_ref, invf_ref, o_ref):
        ang = pos_ref[...] * invf_ref[...]
        co, si = jnp.cos(ang), jnp.sin(ang)
        lane = lax.broadcasted_iota(jnp.int32, ang.shape, 1)
        o_ref[0] = jnp.where(lane < c.ROPE, co, 0.0)
        o_ref[1] = jnp.where(lane < half, -si, 0.0)
        o_ref[2] = jnp.where(jnp.logical_and(lane >= half, lane < c.ROPE), si, 0.0)

    tabs = pl.pallas_call(
        rope_tab_kern, name="rope_tab", grid=(nr,),
        in_specs=[pl.BlockSpec((R, 1), lambda i: (i, 0)), fullb((1, LANES))],
        out_specs=tabs_of(R),
        out_shape=jax.ShapeDtypeStruct((3, N, LANES), F32),
        compiler_params=_cparams(("parallel",)),
    )(pos_col, invf)

    def make_rms_kern():
        def rms_fwd_kern(x_ref, g_ref, o_ref):
            xh, _ = _rms(x_ref[...].astype(F32), x_ref.shape[-1])
            o_ref[...] = (xh * g_ref[...]).astype(o_ref.dtype)
        return rms_fwd_kern

    h = pl.pallas_call(
        make_rms_kern(), name="rms_x", grid=(nr2,),
        in_specs=[rowb2(D, 0), fullb((1, D))], out_specs=rowb2(D, 0),
        out_shape=jax.ShapeDtypeStruct((N, D), _BF), compiler_params=_cparams(("parallel",)),
    )(xs, norm_g)

    proj = _mm(h, win_pT, tb=True, name="mm_proj", out_dtype=_BF, after=ag_rest[4])
    kr_raw = _mm(h, w_krT, tb=True, name="mm_kr", out_dtype=_BF)

    Wf = {n: g.reshape(-1, g.shape[2]) for n, g in zip(rest, _split_wait(ag_rest, True, proj, "ag_rest_wait"))}
    wuqT = Wf['w_uq'].reshape(H, c.NOPE + c.ROPE, c.QL)
    wuq_pT = jnp.pad(wuqT, ((0, 0), (0, c.HW - c.NOPE - c.ROPE), (0, 0))).reshape(H * c.HW, c.QL)
    wukvT = Wf['w_ukv']
    wcoT, wmo, wmkv, wmemoT, wo = Wf['w_conv_out'], Wf['w_mla_out'], Wf['w_mem_kv'], Wf['w_mem_out'], Wf['w_o']

    CW = c.CW
    conv_blk = c.o_conv // (4 * CW)
    HALO = 16
    rh = R // HALO

    def conv_parts(blk):
        blk = blk.astype(F32)
        return blk[:, 0:CW], blk[:, CW:2 * CW], blk[:, 2 * CW:3 * CW], blk[:, 3 * CW:4 * CW]

    def shifted(cu, prev, i):
        prev = jnp.where(i > 0, prev, 0.0)
        rid = lax.broadcasted_iota(jnp.int32, cu.shape, 0)
        last, last2 = prev[HALO - 1:HALO, :], prev[HALO - 2:HALO - 1, :]
        sh1 = jnp.where(rid == 0, last, pltpu.roll(cu, 1, 0))
        sh2 = jnp.where(rid == 0, last2, jnp.where(rid == 1, last, pltpu.roll(cu, 2, 0)))
        return sh1, sh2

    def conv_fwd_kern(p_ref, prev_ref, w_ref, o_ref):
        i = pl.program_id(0)
        cg, bg, u, z = conv_parts(p_ref[...])
        pc, _, pu, _ = conv_parts(prev_ref[...])
        cu = cg * u
        sh1, sh2 = shifted(cu, pc * pu, i)
        w = w_ref[...]
        conv = w[0:1, :] * sh2 + w[1:2, :] * sh1 + w[2:3, :] * cu
        o_ref[...] = (bg * conv * (z * _sig(z))).astype(o_ref.dtype)

    prev_spec = pl.BlockSpec((HALO, 4 * CW), lambda i: (jnp.maximum(i * rh - 1, 0), conv_blk))
    a_conv = pl.pallas_call(
        conv_fwd_kern, name="conv_fwd", grid=(nr,),
        in_specs=[rowb(4 * CW, conv_blk), prev_spec, fullb((8, CW))],
        out_specs=rowb(CW, 0), out_shape=jax.ShapeDtypeStruct((N, CW), _BF),
        compiler_params=_cparams(("parallel",)),
    )(proj, proj, convw8)
    o_conv = _mm(a_conv, wcoT, tb=True, name="mm_oconv", out_dtype=_BF)

    QL, KVL, HW = c.QL, c.KVL, c.HW
    lora_blk = c.o_lora // (QL + KVL)

    def lora_fwd_kern(p_ref, gq_ref, gkv_ref, q_ref, kv_ref):
        blk = p_ref[...].astype(F32)
        qh, _ = _rms(blk[:, :QL], QL)
        kh, _ = _rms(blk[:, QL:], KVL)
        q_ref[...] = (qh * gq_ref[...]).astype(q_ref.dtype)
        kv_ref[...] = (kh * gkv_ref[...]).astype(kv_ref.dtype)

    cqn, ckvn = pl.pallas_call(
        lora_fwd_kern, name="lora_fwd", grid=(nr,),
        in_specs=[rowb(QL + KVL, lora_blk), fullb((1, QL)), fullb((1, KVL))],
        out_specs=[rowb(QL, 0), rowb(KVL, 0)],
        out_shape=[jax.ShapeDtypeStruct((N, QL), _BF), jax.ShapeDtypeStruct((N, KVL), _BF)],
        compiler_params=_cparams(("parallel",)),
    )(proj, mla_q_norm_g, mla_kv_norm_g)
    q_p = _mm(cqn, wuq_pT, tb=True, name="mm_q", out_dtype=_BF, bn=2048)
    kv = _mm(ckvn, wukvT, tb=True, name="mm_kv", out_dtype=_BF, bn=2048)

    g_qn, g_qr = mla_qn_nope_g, pad_lanes(mla_qn_rope_g)
    g_kn, g_kr = mla_kn_nope_g, pad_lanes(mla_kn_rope_g)

    def krope_fwd_kern(p_ref, t_ref, g_ref, o_ref):
        xh, _ = _rms128(p_ref[...].astype(F32), c.ROPE)
        o_ref[...] = _rope(xh * g_ref[...], t_ref[0], t_ref[1], t_ref[2]).astype(o_ref.dtype)

    k_rope = pl.pallas_call(
        krope_fwd_kern, name="krope_fwd", grid=(nr,),
        in_specs=[rowb(LANES, 0), tabs_of(R), fullb((1, LANES))],
        out_specs=rowb(LANES, 0), out_shape=jax.ShapeDtypeStruct((N, LANES), _BF),
        compiler_params=_cparams(("parallel",)),
    )(kr_raw, tabs, g_kr)

    RP, HG = c.RP, c.HG
    nrp, nhg = N // RP, H // HG
    heads_in = pl.BlockSpec((RP, HG * HW), lambda i, hg: (i, hg))
    heads_out = pl.BlockSpec((HG, RP, HW), lambda i, hg: (hg, i, 0))

    def q_prep_kern(q_ref, t_ref, gn_ref, gr_ref, o_ref):
        for g in range(HG):
            blk = q_ref[:, g * HW:(g + 1) * HW].astype(F32)
            nh, _ = _rms128(blk[:, :LANES], c.NOPE)
            rhat, _ = _rms128(blk[:, LANES:], c.ROPE)
            rot = _rope(rhat * gr_ref[...], t_ref[0], t_ref[1], t_ref[2])
            o_ref[g] = jnp.concatenate([nh * gn_ref[...], rot], axis=1).astype(o_ref.dtype)

    q_cat = pl.pallas_call(
        q_prep_kern, name="q_prep", grid=(nrp, nhg),
        in_specs=[heads_in, tabs_of(RP), fullb((1, LANES)), fullb((1, LANES))],
        out_specs=heads_out, out_shape=jax.ShapeDtypeStruct((H, N, HW), _BF),
        compiler_params=_cparams(("parallel", "parallel")),
    )(q_p, tabs, g_qn * (c.scale * LOG2E), g_qr * (c.scale * LOG2E))

    def k_prep_kern(kv_ref, kr_ref, gn_ref, o_ref):
        for g in range(HG):
            kn, _ = _rms128(kv_ref[:, g * HW:g * HW + LANES].astype(F32), c.NOPE)
            o_ref[g] = jnp.concatenate([(kn * gn_ref[...]).astype(o_ref.dtype), kr_ref[...]], axis=1)

    k_cat = pl.pallas_call(
        k_prep_kern, name="k_prep", grid=(nrp, nhg),
        in_specs=[heads_in, pl.BlockSpec((RP, LANES), lambda i, hg: (i, 0)), fullb((1, LANES))],
        out_specs=heads_out, out_shape=jax.ShapeDtypeStruct((H, N, HW), _BF),
        compiler_params=_cparams(("parallel", "parallel")),
    )(kv, k_rope, g_kn)

    QB = ATT_QBLOCKS if N % (ATT_QBLOCKS * B) == 0 else 1
    BQ = QB * B
    nq = N // BQ
    assert CHUNK & (CHUNK - 1) == 0 and B % CHUNK == 0 and ATT_UNROLL_FWD % QB == 0 and ATT_UNROLL_BWD % QB == 0

    def live_rows(diag):
        return slice(0 if diag is None else diag * B, BQ)

    def diag_mask(s):
        row = lax.broadcasted_iota(jnp.int32, s.shape, 0)
        col = lax.broadcasted_iota(jnp.int32, s.shape, 1)
        shift = CHUNK.bit_length() - 1
        allowed = jnp.right_shift(col, shift) <= jnp.right_shift(row, shift)
        return jnp.where(allowed, s, NEG)

    k_head = pl.BlockSpec((1, N, HW), lambda hh, i: (hh, 0, 0))
    v_head = pl.BlockSpec((N, LANES), lambda hh, i: (0, 2 * hh + 1))
    q_blk = pl.BlockSpec((1, BQ, HW), lambda hh, i: (hh, i, 0))
    o_blk = pl.BlockSpec((BQ, LANES), lambda hh, i: (i, hh))
    lse_blk = pl.BlockSpec((1, BQ, LANES), lambda hh, i: (hh, i, 0))

    def key_block_plan(i, run, unroll):
        def unrolled(u, carry):
            run(unroll * u, unroll, 0)
            return carry

        n_full = QB * i
        lax.fori_loop(0, n_full // unroll, unrolled, 0)
        for rem in range(0, unroll, QB):
            @pl.when(n_full % unroll == rem)
            def _(rem=rem):
                run(n_full - rem, rem + QB, QB)

    def attn_fwd_kern(q_ref, k_ref, v_ref, o_ref, lse_ref, m_sc, acc_sc, s_sc):
        i = pl.program_id(1)
        m_sc[...] = jnp.full(m_sc.shape, NEG, F32)
        acc_sc[...] = jnp.zeros(acc_sc.shape, F32)

        def rows_of(t):
            return pl.ds(pl.multiple_of(t * B, B), B)

        def scores(t, slot, diag):
            rs = live_rows(diag)
            s_sc[slot, rs, :] = _dot_nt(q_ref[0, rs, :], k_ref[0, rows_of(t), :])

        def softmax_pv(t, slot, diag):
            rs = live_rows(diag)
            s = s_sc[slot, rs, :]
            if diag is not None:
                s = diag_mask(s)
            mt = s[:, 0:LANES]
            for cb in range(1, B // LANES):
                mt = jnp.maximum(mt, s[:, cb * LANES:(cb + 1) * LANES])
            m_prev = m_sc[rs, :]
            m_new = jnp.maximum(m_prev, jnp.max(mt, axis=1, keepdims=True))
            alpha = jnp.exp2(m_prev - m_new)
            p = jnp.concatenate([jnp.exp2(s[:, cb * LANES:(cb + 1) * LANES] - m_new).astype(_BF)
                                 for cb in range(B // LANES)], axis=1)
            v_ones = jnp.concatenate([v_ref[rows_of(t), :], jnp.ones((B, LANES), _BF)], axis=1)
            acc_sc[rs, :] = jnp.concatenate([alpha, alpha], axis=1) * acc_sc[rs, :] + _dot(p, v_ones)
            m_sc[rs, :] = m_new

        scores(0, 0, None)

        def run(first, count, n_diag):
            def diag_of(u):
                return u - (count - n_diag) if count - n_diag <= u < count else None

            for u in range(count):
                if u + 1 < count or n_diag == 0:
                    scores(first + u + 1, (u + 1) % 2, diag_of(u + 1))
                softmax_pv(first + u, u % 2, diag_of(u))

        key_block_plan(i, run, ATT_UNROLL_FWD)
        acc = acc_sc[...]
        o_ref[...] = (acc[:, :LANES] / acc[:, LANES:]).astype(o_ref.dtype)
        lse_ref[0] = m_sc[...] + jnp.log2(acc[:, LANES:])

    mla_y, lse = pl.pallas_call(
        attn_fwd_kern, name="attn_fwd", grid=(H, nq),
        in_specs=[q_blk, k_head, v_head], out_specs=[o_blk, lse_blk],
        out_shape=[jax.ShapeDtypeStruct((N, c.HV), _BF), jax.ShapeDtypeStruct((H, N, LANES), F32)],
        scratch_shapes=[pltpu.VMEM((BQ, LANES), F32), pltpu.VMEM((BQ, HW), F32), pltpu.VMEM((2, BQ, B), F32)],
        compiler_params=_cparams(("parallel", "arbitrary")),
    )(q_cat, k_cat, kv)

    HV = c.HV
    mz_blk = c.o_mz // HV

    def gate_fwd_kern(y_ref, z_ref, o_ref):
        z = z_ref[...].astype(F32)
        o_ref[...] = (y_ref[...].astype(F32) * (z * _sig(z))).astype(o_ref.dtype)

    a_mla = pl.pallas_call(
        gate_fwd_kern, name="gate_mla", grid=(nr2,),
        in_specs=[rowb2(HV, 0), rowb2(HV, mz_blk)], out_specs=rowb2(HV, 0),
        out_shape=jax.ShapeDtypeStruct((N, HV), _BF), compiler_params=_cparams(("parallel",)),
    )(mla_y, proj)
    o_mla = _mm(a_mla, wmo, name="mm_omla", out_dtype=_BF)

    M, MW, MH, MHD = c.M, c.MW, c.MH, c.MHD
    memn = pl.pallas_call(
        make_rms_kern(), name="rms_mem",
        grid=(1,), in_specs=[fullb((M, D)), fullb((1, D))], out_specs=fullb((M, D)),
        out_shape=jax.ShapeDtypeStruct((M, D), _BF), compiler_params=_cparams(("arbitrary",)),
    )(memx, mem_norm_g)
    kvm = _mm(memn, wmkv, name="mm_memkv", out_dtype=F32)

    def memk_fwd_kern(kv_ref, g_ref, k_ref, v_ref):
        for hh in range(MH):
            kh, _ = _rms(kv_ref[:, hh * MHD:(hh + 1) * MHD], MHD)
            k_ref[:, hh * MHD:(hh + 1) * MHD] = (kh * g_ref[...]).astype(k_ref.dtype)
        v_ref[...] = kv_ref[:, MW:].astype(v_ref.dtype)

    mem_k, mem_v = pl.pallas_call(
        memk_fwd_kern, name="memk_fwd", grid=(1,),
        in_specs=[fullb((M, 2 * MW)), fullb((1, MHD))], out_specs=[fullb((M, MW)), fullb((M, MW))],
        out_shape=[jax.ShapeDtypeStruct((M, MW), _BF)] * 2, compiler_params=_cparams(("arbitrary",)),
    )(kvm, mem_kn_g)

    mem_blk = c.o_mem // (2 * MW)

    def mem_head(qz_ref, k_ref, v_ref, g_ref, hh):
        sl = slice(hh * MHD, (hh + 1) * MHD)
        qh, r = _rms(qz_ref[:, sl].astype(F32), MHD)
        qn = (qh * g_ref[...]).astype(_BF)
        s = _dot_nt(qn, k_ref[:, sl]) * c.mscale
        e = jnp.exp(s - jnp.max(s, axis=1, keepdims=True))
        p = e / jnp.sum(e, axis=1, keepdims=True)
        y = _dot(p.astype(_BF), v_ref[:, sl])
        z = qz_ref[:, MW + hh * MHD:MW + (hh + 1) * MHD].astype(F32)
        return sl, qh, r, qn, p, y, z

    def mem_fwd_kern(qz_ref, k_ref, v_ref, g_ref, o_ref):
        for hh in range(MH):
            sl, _, _, _, _, y, z = mem_head(qz_ref, k_ref, v_ref, g_ref, hh)
            o_ref[:, sl] = (y * (z * _sig(z))).astype(o_ref.dtype)

    a_mem = pl.pallas_call(
        mem_fwd_kern, name="mem_fwd", grid=(N // RM,),
        in_specs=[rowb(2 * MW, mem_blk, RM), fullb((M, MW)), fullb((M, MW)), fullb((1, MHD))],
        out_specs=rowb(MW, 0, RM), out_shape=jax.ShapeDtypeStruct((N, MW), _BF),
        compiler_params=_cparams(("parallel",)),
    )(proj, mem_k, mem_v, mem_qn_g)
    o_mem = _mm(a_mem, wmemoT, tb=True, name="mm_omem", out_dtype=_BF)

    g_blk = c.o_g // (3 * D)

    def merge_fwd_kern(g_ref, oc_ref, om_ref, ome_ref, o_ref):
        g = g_ref[...].astype(F32)
        acc = _sig(g[:, :D]) * oc_ref[...].astype(F32)
        acc += _sig(g[:, D:2 * D]) * om_ref[...].astype(F32)
        acc += _sig(g[:, 2 * D:]) * ome_ref[...].astype(F32)
        o_ref[...] = acc.astype(o_ref.dtype)

    merged = pl.pallas_call(
        merge_fwd_kern, name="merge_fwd", grid=(nr,),
        in_specs=[rowb(3 * D, g_blk), rowb(D, 0), rowb(D, 0), rowb(D, 0)], out_specs=rowb(D, 0),
        out_shape=jax.ShapeDtypeStruct((N, D), _BF), compiler_params=_cparams(("parallel",)),
    )(proj, o_conv, o_mla, o_mem)
    obm, obn = _tile(N, 1024, 16), _tile(D, 1024, LANES)

    def out_loss_kern(a_ref, w_ref, x_ref, t_ref, dyb_ref, l_ref):
        e = x_ref[...] + _dot(a_ref[...], w_ref[...]) - t_ref[...]
        dyb_ref[...] = (e * (1.0 / D)).astype(dyb_ref.dtype)
        row = lax.broadcasted_iota(jnp.int32, l_ref.shape, 0)
        l_ref[...] = jnp.where(row == 0, jnp.sum(e * e, axis=0, keepdims=True), 0.0)

    oblk = pl.BlockSpec((obm, obn), lambda i, j: (i, j))
    dyb, lpart = pl.pallas_call(
        out_loss_kern, name="mm_out_loss", grid=(N // obm, D // obn),
        in_specs=[pl.BlockSpec((obm, D), lambda i, j: (i, 0)), pl.BlockSpec((D, obn), lambda i, j: (0, j)), oblk, oblk],
        out_specs=[oblk, pl.BlockSpec((8, obn), lambda i, j: (i, j))],
        out_shape=[jax.ShapeDtypeStruct((N, D), _BF), jax.ShapeDtypeStruct((8 * (N // obm), D), F32)],
        compiler_params=_cparams(("parallel", "parallel")),
    )(merged, wo, xs, tgt)
    loss = lax.psum(jnp.sum(lpart) * (0.5 / D), AXES)

    G = {}
    d_merged = _mm(dyb, wo, tb=True, name="mm_dmerged", out_dtype=_BF)
    G['w_o'] = _mm(merged, dyb, ta=True, name="mm_dwo", out_dtype=_BF, bk=4096)

    dproj0 = lax.empty((N, c.P), _BF)
    any_spec = pl.BlockSpec(memory_space=pl.ANY)

    def merge_bwd_kern(dp_any, g_ref, dm_ref, oc_ref, om_ref, ome_ref, dg_ref, doc_ref, dom_ref, dome_ref):
        g = g_ref[...].astype(F32)
        dm = dm_ref[...].astype(F32)
        for idx, (o_in, d_out) in enumerate(((oc_ref, doc_ref), (om_ref, dom_ref), (ome_ref, dome_ref))):
            sg = _sig(g[:, idx * D:(idx + 1) * D])
            d_out[...] = (sg * dm).astype(d_out.dtype)
            dg_ref[:, idx * D:(idx + 1) * D] = (dm * o_in[...].astype(F32) * sg * (1.0 - sg)).astype(dg_ref.dtype)

    dproj1, d_oconv, d_omla, d_omem = pl.pallas_call(
        merge_bwd_kern, name="merge_bwd", grid=(nr,),
        in_specs=[any_spec, rowb(3 * D, g_blk), rowb(D, 0), rowb(D, 0), rowb(D, 0), rowb(D, 0)],
        out_specs=[rowb(3 * D, g_blk), rowb(D, 0), rowb(D, 0), rowb(D, 0)],
        out_shape=[jax.ShapeDtypeStruct((N, c.P), _BF)] + [jax.ShapeDtypeStruct((N, D), _BF)] * 3,
        input_output_aliases={0: 0}, compiler_params=_cparams(("parallel",)),
    )(dproj0, proj, d_merged, o_conv, o_mla, o_mem)

    G['w_conv_out'] = _mm(a_conv, d_oconv, ta=True, name="mm_dwco", out_dtype=_BF, bk=4096)
    d_aconv = _mm(d_oconv, wcoT, name="mm_daconv", out_dtype=_BF)
    G['w_mla_out'] = _mm(a_mla, d_omla, ta=True, name="mm_dwmo", out_dtype=_BF, bk=4096)
    d_amla = _mm(d_omla, wmo, tb=True, name="mm_damla", out_dtype=_BF)
    G['w_mem_out'] = _mm(a_mem, d_omem, ta=True, name="mm_dwmemo", out_dtype=_BF, bk=4096)
    d_amem = _mm(d_omem, wmemoT, name="mm_damem", out_dtype=_BF)

    def conv_bwd_kern(dp_any, p_ref, prev_ref, next_ref, da_ref, dan_ref, w_ref, o_ref, dw_ref):
        i = pl.program_id(0)
        cg, bg, u, z = conv_parts(p_ref[...])
        pc, _, pu, _ = conv_parts(prev_ref[...])
        _, nbg, _, nz = conv_parts(next_ref[...])
        cu = cg * u
        sh1, sh2 = shifted(cu, pc * pu, i)
        w = w_ref[...]
        conv = w[0:1, :] * sh2 + w[1:2, :] * sh1 + w[2:3, :] * cu
        sg = _sig(z)
        sz = z * sg
        da = da_ref[...].astype(F32)
        dcy = da * sz
        d_z = da * (bg * conv) * (sg * (1.0 + z * (1.0 - sg)))
        d_b = dcy * conv
        dconv = dcy * bg
        dnext = dan_ref[...].astype(F32) * (nz * _sig(nz)) * nbg
        dnext = jnp.where(i < nr - 1, dnext, 0.0)
        rid = lax.broadcasted_iota(jnp.int32, cu.shape, 0)
        up1 = jnp.where(rid == R - 1, dnext[0:1, :], pltpu.roll(dconv, R - 1, 0))
        up2 = jnp.where(rid == R - 2, dnext[0:1, :], jnp.where(rid == R - 1, dnext[1:2, :], pltpu.roll(dconv, R - 2, 0)))
        dcu = w[2:3, :] * dconv + w[1:2, :] * up1 + w[0:1, :] * up2
        o_ref[:, 0:CW] = (dcu * u).astype(o_ref.dtype)
        o_ref[:, CW:2 * CW] = d_b.astype(o_ref.dtype)
        o_ref[:, 2 * CW:3 * CW] = (dcu * cg).astype(o_ref.dtype)
        o_ref[:, 3 * CW:4 * CW] = d_z.astype(o_ref.dtype)

        @pl.when(i == 0)
        def _():
            dw_ref[...] = jnp.zeros(dw_ref.shape, F32)

        dw_ref[0:1, :] += jnp.sum(dconv * sh2, axis=0, keepdims=True)
        dw_ref[1:2, :] += jnp.sum(dconv * sh1, axis=0, keepdims=True)
        dw_ref[2:3, :] += jnp.sum(dconv * cu, axis=0, keepdims=True)

    next_spec = pl.BlockSpec((HALO, 4 * CW), lambda i: (jnp.minimum((i + 1) * rh, N // HALO - 1), conv_blk))
    dan_spec = pl.BlockSpec((HALO, CW), lambda i: (jnp.minimum((i + 1) * rh, N // HALO - 1), 0))
    dproj2, g_convw = pl.pallas_call(
        conv_bwd_kern, name="conv_bwd", grid=(nr,),
        in_specs=[any_spec, rowb(4 * CW, conv_blk), prev_spec, next_spec, rowb(CW, 0), dan_spec, fullb((8, CW))],
        out_specs=[rowb(4 * CW, conv_blk), fullb((8, CW))],
        out_shape=[jax.ShapeDtypeStruct((N, c.P), _BF), jax.ShapeDtypeStruct((8, CW), F32)],
        input_output_aliases={0: 0}, compiler_params=_cparams(("arbitrary",)),
    )(dproj1, proj, proj, proj, d_aconv, d_aconv, convw8)

    def mem_bwd_kern(dp_any, qz_ref, da_ref, k_ref, v_ref, g_ref, o_ref, dk_ref, dv_ref, dg_ref):
        @pl.when(pl.program_id(0) == 0)
        def _():
            dk_ref[...] = jnp.zeros(dk_ref.shape, F32)
            dv_ref[...] = jnp.zeros(dv_ref.shape, F32)
            dg_ref[...] = jnp.zeros(dg_ref.shape, F32)

        for hh in range(MH):
            sl, qh, r, qn, p, y, z = mem_head(qz_ref, k_ref, v_ref, g_ref, hh)
            da = da_ref[:, sl].astype(F32)
            sg = _sig(z)
            dyh = da * (z * sg)
            o_ref[:, MW + hh * MHD:MW + (hh + 1) * MHD] = (da * y * (sg * (1.0 + z * (1.0 - sg)))).astype(o_ref.dtype)
            dyb_h = dyh.astype(_BF)
            dpm = _dot_nt(dyb_h, v_ref[:, sl])
            ds = (p * (dpm - jnp.sum(dpm * p, axis=1, keepdims=True)) * c.mscale).astype(_BF)
            dqn = _dot(ds, k_ref[:, sl])
            dk_ref[:, sl] += _dot_tn(ds, qn)
            dv_ref[:, sl] += _dot_tn(p.astype(_BF), dyb_h)
            dq, dgp = _rms_bwd(qh, r, g_ref[...], dqn, MHD)
            o_ref[:, sl] = dq.astype(o_ref.dtype)
            dg_ref[...] += jnp.sum(dgp, axis=0, keepdims=True)

    dproj3, d_memk, d_memv, g_mem_qn = pl.pallas_call(
        mem_bwd_kern, name="mem_bwd", grid=(N // RM,),
        in_specs=[any_spec, rowb(2 * MW, mem_blk, RM), rowb(MW, 0, RM), fullb((M, MW)), fullb((M, MW)), fullb((1, MHD))],
        out_specs=[rowb(2 * MW, mem_blk, RM), fullb((M, MW)), fullb((M, MW)), fullb((1, MHD))],
        out_shape=[jax.ShapeDtypeStruct((N, c.P), _BF), jax.ShapeDtypeStruct((M, MW), F32),
                   jax.ShapeDtypeStruct((M, MW), F32), jax.ShapeDtypeStruct((1, MHD), F32)],
        input_output_aliases={0: 0}, compiler_params=_cparams(("arbitrary",)),
    )(dproj2, proj, d_amem, mem_k, mem_v, mem_qn_g)

    def memk_bwd_kern(kv_ref, dk_ref, dv_ref, g_ref, o_ref, dg_ref):
        dg = jnp.zeros((1, MHD), F32)
        for hh in range(MH):
            sl = slice(hh * MHD, (hh + 1) * MHD)
            kh, r = _rms(kv_ref[:, sl], MHD)
            dkr, dgp = _rms_bwd(kh, r, g_ref[...], dk_ref[:, sl], MHD)
            o_ref[:, sl] = dkr.astype(o_ref.dtype)
            dg += jnp.sum(dgp, axis=0, keepdims=True)
        o_ref[:, MW:] = dv_ref[...].astype(o_ref.dtype)
        dg_ref[...] = dg

    d_kvm, g_mem_kn = pl.pallas_call(
        memk_bwd_kern, name="memk_bwd", grid=(1,),
        in_specs=[fullb((M, 2 * MW)), fullb((M, MW)), fullb((M, MW)), fullb((1, MHD))],
        out_specs=[fullb((M, 2 * MW)), fullb((1, MHD))],
        out_shape=[jax.ShapeDtypeStruct((M, 2 * MW), _BF), jax.ShapeDtypeStruct((1, MHD), F32)],
        compiler_params=_cparams(("arbitrary",)),
    )(kvm, d_memk, d_memv, mem_kn_g)
    G['w_mem_kv'] = _mm(memn, d_kvm, ta=True, name="mm_dwmkv", out_dtype=_BF)
    d_memn = _mm(d_kvm, wmkv, tb=True, name="mm_dmemn", out_dtype=F32)

    def memnorm_bwd_kern(x_ref, d_ref, dg_ref):
        xh, _ = _rms(x_ref[...], D)
        dg_ref[...] = jnp.sum(d_ref[...] * xh, axis=0, keepdims=True)

    g_mem_norm = pl.pallas_call(
        memnorm_bwd_kern, name="memnorm_bwd", grid=(1,),
        in_specs=[fullb((M, D)), fullb((M, D))], out_specs=fullb((1, D)),
        out_shape=jax.ShapeDtypeStruct((1, D), F32), compiler_params=_cparams(("arbitrary",)),
    )(memx, d_memn)

    def gate_bwd_kern(dp_any, da_ref, y_ref, z_ref, dy_ref, dz_ref):
        z = z_ref[...].astype(F32)
        da = da_ref[...].astype(F32)
        sg = _sig(z)
        dy_ref[...] = (da * (z * sg)).astype(dy_ref.dtype)
        dz_ref[...] = (da * y_ref[...].astype(F32) * (sg * (1.0 + z * (1.0 - sg)))).astype(dz_ref.dtype)

    d_mlay, dproj4 = pl.pallas_call(
        gate_bwd_kern, name="gate_mla_bwd", grid=(nr2,),
        in_specs=[any_spec, rowb2(HV, 0), rowb2(HV, 0), rowb2(HV, mz_blk)],
        out_specs=[rowb2(HV, 0), rowb2(HV, mz_blk)],
        out_shape=[jax.ShapeDtypeStruct((N, HV), _BF), jax.ShapeDtypeStruct((N, c.P), _BF)],
        input_output_aliases={0: 1}, compiler_params=_cparams(("parallel",)),
    )(dproj3, d_amla, mla_y, proj)

    def attn_bwd_kern(q_ref, k_ref, v_ref, o_ref, do_ref, lse_ref, dq_ref, dk_ref, dv_ref, dq_sc, dl_sc, dk_sc, dv_sc):
        i = pl.program_id(1)
        delta = jnp.sum(do_ref[...].astype(F32) * o_ref[...].astype(F32), axis=1, keepdims=True)
        dl_sc[...] = jnp.broadcast_to(delta, dl_sc.shape)
        dq_sc[...] = jnp.zeros(dq_sc.shape, F32)

        def step(t, diag):
            rows = pl.ds(pl.multiple_of(t * B, B), B)
            rs = live_rows(diag)
            q, do = q_ref[0, rs, :], do_ref[rs, :]
            k = k_ref[0, rows, :]
            s = _dot_nt(q, k)
            if diag is not None:
                s = diag_mask(s)
            dpm = _dot_nt(do, v_ref[rows, :])
            lse_t, dl = lse_ref[0, rs, :], dl_sc[rs, :]
            ps, dss = [], []
            for cb in range(B // LANES):
                cols = slice(cb * LANES, (cb + 1) * LANES)
                p_cb = jnp.exp2(s[:, cols] - lse_t)
                ps.append(p_cb.astype(_BF))
                dss.append((p_cb * (dpm[:, cols] - dl)).astype(_BF))
            p, ds = jnp.concatenate(ps, axis=1), jnp.concatenate(dss, axis=1)
            dvp = _dot_tn(p, do)
            dkp = _dot_tn(ds, q)
            if diag is not None:
                dk_sc[rows, :] = dkp
                dv_sc[rows, :] = dvp
            else:
                dk_sc[rows, :] += dkp
                dv_sc[rows, :] += dvp
            dq_sc[rs, :] += _dot(ds, k)

        def run(first, count, n_diag):
            for u in range(count):
                step(first + u, u - (count - n_diag) if u >= count - n_diag else None)

        key_block_plan(i, run, ATT_UNROLL_BWD)
        dq_ref[0] = dq_sc[...].astype(dq_ref.dtype)

        @pl.when(i == nq - 1)
        def _():
            dk_ref[0] = dk_sc[...].astype(dk_ref.dtype)
            dv_ref[0] = dv_sc[...].astype(dv_ref.dtype)

    d_qcat, d_kcat, d_v = pl.pallas_call(
        attn_bwd_kern, name="attn_bwd", grid=(H, nq),
        in_specs=[q_blk, k_head, v_head, o_blk, o_blk, lse_blk],
        out_specs=[pl.BlockSpec((1, BQ, HW), lambda hh, i: (hh, i, 0)),
                   pl.BlockSpec((1, N, HW), lambda hh, i: (hh, 0, 0)),
                   pl.BlockSpec((1, N, LANES), lambda hh, i: (hh, 0, 0))],
        out_shape=[jax.ShapeDtypeStruct((H, N, HW), _BF), jax.ShapeDtypeStruct((H, N, HW), _BF),
                   jax.ShapeDtypeStruct((H, N, LANES), _BF)],
        scratch_shapes=[pltpu.VMEM((BQ, HW), F32), pltpu.VMEM((BQ, LANES), F32), pltpu.VMEM((N, HW), F32),
                        pltpu.VMEM((N, LANES), F32)],
        compiler_params=_cparams(("parallel", "arbitrary")),
    )(q_cat, k_cat, kv, mla_y, d_mlay, lse)

    def q_prep_bwd_kern(q_ref, dq_ref, t_ref, gn_ref, gr_ref, o_ref, dgn_ref, dgr_ref):
        @pl.when(jnp.logical_and(pl.program_id(0) == 0, pl.program_id(1) == 0))
        def _():
            dgn_ref[...] = jnp.zeros(dgn_ref.shape, F32)
            dgr_ref[...] = jnp.zeros(dgr_ref.shape, F32)

        for g in range(HG):
            blk = q_ref[:, g * HW:(g + 1) * HW].astype(F32)
            d = dq_ref[g].astype(F32)
            nh, rn = _rms128(blk[:, :LANES], c.NOPE)
            rhat, rr = _rms128(blk[:, LANES:], c.ROPE)
            dn, dgn = _rms128_bwd(nh, rn, gn_ref[...], d[:, :LANES], c.NOPE)
            drot = _rope_t(d[:, LANES:], t_ref[0], t_ref[1], t_ref[2])
            dr, dgr = _rms128_bwd(rhat, rr, gr_ref[...], drot, c.ROPE)
            o_ref[:, g * HW:(g + 1) * HW] = jnp.concatenate([dn, dr], axis=1).astype(o_ref.dtype)
            dgn_ref[...] += jnp.sum(dgn, axis=0, keepdims=True)
            dgr_ref[...] += jnp.sum(dgr, axis=0, keepdims=True)

    d_qp, g_qn_nope, g_qn_rope = pl.pallas_call(
        q_prep_bwd_kern, name="q_prep_bwd", grid=(nrp, nhg),
        in_specs=[heads_in, heads_out, tabs_of(RP), fullb((1, LANES)), fullb((1, LANES))],
        out_specs=[heads_in, fullb((1, LANES)), fullb((1, LANES))],
        out_shape=[jax.ShapeDtypeStruct((N, H * HW), _BF), jax.ShapeDtypeStruct((1, LANES), F32),
                   jax.ShapeDtypeStruct((1, LANES), F32)],
        compiler_params=_cparams(("arbitrary", "arbitrary")),
    )(q_p, d_qcat, tabs, g_qn * c.scale, g_qr * c.scale)
    g_qn_nope, g_qn_rope = g_qn_nope * c.scale, g_qn_rope * c.scale

    def k_prep_bwd_kern(kv_ref, dk_ref, dv_ref, gn_ref, o_ref, dkr_ref, dgn_ref):
        hg = pl.program_id(1)

        @pl.when(jnp.logical_and(pl.program_id(0) == 0, hg == 0))
        def _():
            dgn_ref[...] = jnp.zeros(dgn_ref.shape, F32)

        @pl.when(hg == 0)
        def _():
            dkr_ref[...] = jnp.zeros(dkr_ref.shape, F32)

        dkr = jnp.zeros((RP, LANES), F32)
        for g in range(HG):
            dk = dk_ref[g].astype(F32) * (1.0 / LOG2E)
            kn, r = _rms128(kv_ref[:, g * HW:g * HW + LANES].astype(F32), c.NOPE)
            dkn, dgn = _rms128_bwd(kn, r, gn_ref[...], dk[:, :LANES], c.NOPE)
            o_ref[:, g * HW:(g + 1) * HW] = jnp.concatenate([dkn.astype(o_ref.dtype), dv_ref[g]], axis=1)
            dgn_ref[...] += jnp.sum(dgn, axis=0, keepdims=True)
            dkr += dk[:, LANES:]
        dkr_ref[...] += dkr

    d_kv, d_krsum, g_kn_nope = pl.pallas_call(
        k_prep_bwd_kern, name="k_prep_bwd", grid=(nrp, nhg),
        in_specs=[heads_in, heads_out, pl.BlockSpec((HG, RP, LANES), lambda i, hg: (hg, i, 0)), fullb((1, LANES))],
        out_specs=[heads_in, pl.BlockSpec((RP, LANES), lambda i, hg: (i, 0)), fullb((1, LANES))],
        out_shape=[jax.ShapeDtypeStruct((N, H * HW), _BF), jax.ShapeDtypeStruct((N, LANES), F32),
                   jax.ShapeDtypeStruct((1, LANES), F32)],
        compiler_params=_cparams(("arbitrary", "arbitrary")),
    )(kv, d_kcat, d_v, g_kn)

    def krope_bwd_kern(p_ref, d_ref, t_ref, g_ref, o_ref, dg_ref):
        @pl.when(pl.program_id(0) == 0)
        def _():
            dg_ref[...] = jnp.zeros(dg_ref.shape, F32)

        xh, r = _rms128(p_ref[...].astype(F32), c.ROPE)
        drot = _rope_t(d_ref[...], t_ref[0], t_ref[1], t_ref[2])
        dx, dg = _rms128_bwd(xh, r, g_ref[...], drot, c.ROPE)
        o_ref[...] = dx.astype(o_ref.dtype)
        dg_ref[...] += jnp.sum(dg, axis=0, keepdims=True)

    d_kr, g_kn_rope = pl.pallas_call(
        krope_bwd_kern, name="krope_bwd", grid=(nr,),
        in_specs=[rowb(LANES, 0), rowb(LANES, 0), tabs_of(R), fullb((1, LANES))],
        out_specs=[rowb(LANES, 0), fullb((1, LANES))],
        out_shape=[jax.ShapeDtypeStruct((N, LANES), _BF), jax.ShapeDtypeStruct((1, LANES), F32)],
        compiler_params=_cparams(("arbitrary",)),
    )(kr_raw, d_krsum, tabs, g_kr)
    dproj5 = dproj4

    g_wuq_p = _mm(cqn, d_qp, ta=True, name="mm_dwuq", out_dtype=_BF, bk=4096)
    G['w_uq'] = g_wuq_p.reshape(QL, H, HW)[:, :, :c.NOPE + c.ROPE].reshape(QL, H * (c.NOPE + c.ROPE))
    d_cqn = _mm(d_qp, wuq_pT, name="mm_dcqn", out_dtype=F32, bk=4096)
    G['w_ukv'] = _mm(ckvn, d_kv, ta=True, name="mm_dwukv", out_dtype=_BF, bk=4096)
    d_ckvn = _mm(d_kv, wukvT, name="mm_dckvn", out_dtype=F32, bk=4096)

    def to_blocks(n, g):
        if n in COL_SHARDED:
            return jnp.transpose(g.reshape(g.shape[0], NDEV, -1), (1, 0, 2))
        return g.reshape(NDEV, -1, g.shape[1])

    def landing(b):
        return lax.empty(b.shape, b.dtype)

    early = [n for n in BIG if n != 'w_in']
    blocks_e = [to_blocks(n, G[n]) for n in early]
    xe = _split_start(blocks_e, [landing(b) for b in blocks_e], False, "xchg_early_start")
    gq_after = mla_q_norm_g + xe[4][0:1, 0:1]

    def lora_bwd_kern(dp_any, p_ref, dq_ref, dkv_ref, gq_ref, gkv_ref, o_ref, dgq_ref, dgkv_ref):
        @pl.when(pl.program_id(0) == 0)
        def _():
            dgq_ref[...] = jnp.zeros(dgq_ref.shape, F32)
            dgkv_ref[...] = jnp.zeros(dgkv_ref.shape, F32)

        blk = p_ref[...].astype(F32)
        qh, rq = _rms(blk[:, :QL], QL)
        kh, rk = _rms(blk[:, QL:], KVL)
        dq, dgq = _rms_bwd(qh, rq, gq_ref[...], dq_ref[...], QL)
        dk, dgk = _rms_bwd(kh, rk, gkv_ref[...], dkv_ref[...], KVL)
        o_ref[:, :QL] = dq.astype(o_ref.dtype)
        o_ref[:, QL:] = dk.astype(o_ref.dtype)
        dgq_ref[...] += jnp.sum(dgq, axis=0, keepdims=True)
        dgkv_ref[...] += jnp.sum(dgk, axis=0, keepdims=True)

    dproj6, g_q_norm, g_kv_norm = pl.pallas_call(
        lora_bwd_kern, name="lora_bwd", grid=(nr,),
        in_specs=[any_spec, rowb(QL + KVL, lora_blk), rowb(QL, 0), rowb(KVL, 0), fullb((1, QL)), fullb((1, KVL))],
        out_specs=[rowb(QL + KVL, lora_blk), fullb((1, QL)), fullb((1, KVL))],
        out_shape=[jax.ShapeDtypeStruct((N, c.P), _BF), jax.ShapeDtypeStruct((1, QL), F32),
                   jax.ShapeDtypeStruct((1, KVL), F32)],
        input_output_aliases={0: 0}, compiler_params=_cparams(("arbitrary",)),
    )(dproj5, proj, d_cqn, d_ckvn, gq_after, mla_kv_norm_g)

    g_win_p = _mm(h, dproj6, ta=True, name="mm_dwin", out_dtype=_BF, bk=4096)
    g_wkr = _mm(h, d_kr, ta=True, name="mm_dwkr", out_dtype=_BF, bk=4096)
    blocks_w = [_win_blocks(g_win_p, g_wkr, c)]
    xw = _split_start(blocks_w, [landing(b) for b in blocks_w], False, "xchg_win_start")
    d_h = _mm(dproj6, win_pT, name="mm_dh", out_dtype=F32, bk=3072, after=xw[4], plus=(d_kr, w_krT))

    def final_bwd_kern(x_ref, g_ref, dh_ref, dy_ref, gx_ref, dg_ref):
        @pl.when(pl.program_id(0) == 0)
        def _():
            dg_ref[...] = jnp.zeros(dg_ref.shape, F32)

        xh, r = _rms(x_ref[...], D)
        dx, dg = _rms_bwd(xh, r, g_ref[...], dh_ref[...], D)
        gx_ref[...] = dy_ref[...].astype(F32) + dx
        dg_ref[...] += jnp.sum(dg, axis=0, keepdims=True)

    grad_x, g_norm = pl.pallas_call(
        final_bwd_kern, name="final_bwd", grid=(nr2,),
        in_specs=[rowb2(D, 0), fullb((1, D)), rowb2(D, 0), rowb2(D, 0)],
        out_specs=[rowb2(D, 0), fullb((1, D))],
        out_shape=[jax.ShapeDtypeStruct((N, D), F32), jax.ShapeDtypeStruct((1, D), F32)],
        compiler_params=_cparams(("arbitrary",)),
    )(xs, norm_g, d_h, dyb)

    res = [{}, {}, {}, {}]

    def adam_into(n, parts):
        outs = _adam(parts, W[n][0], Mo[n][0], Vo[n][0], "adam_" + n)
        for k in range(4):
            res[k][n] = outs[k][None]
        return outs[0]

    recv_e = _split_wait(xe, False, grad_x, "xchg_early_wait")
    last = [adam_into(n, parts) for n, parts in zip(early, recv_e)][-1]
    recv_w = _split_wait(xw, False, last, "xchg_win_wait")
    adam_into('w_in', recv_w[0])

    small_g = {'norm_g': g_norm, 'mla_q_norm_g': g_q_norm, 'mla_kv_norm_g': g_kv_norm,
               'mla_qn_nope_g': g_qn_nope, 'mla_qn_rope_g': g_qn_rope[:, :c.ROPE], 'mla_kn_nope_g': g_kn_nope,
               'mla_kn_rope_g': g_kn_rope[:, :c.ROPE], 'mem_norm_g': g_mem_norm, 'mem_qn_g': g_mem_qn,
               'mem_kn_g': g_mem_kn}
    small_part = _pack([small_g[n] for n in SMALL] + [g_convw[0:3, :]], 0)
    small_all = _all_gather([small_part], "ag_small_grads")[0]
    small_shapes = [W[n].shape for n in SMALL]
    pieces = _unpack(small_all, small_shapes + [(3, CW)])
    cw8 = CW // NDEV
    conv_mine = lax.dynamic_slice_in_dim(pieces[-1].reshape(NDEV, 3, NDEV, cw8), me, 1, axis=2)[:, :, 0, :]
    sm_parts = _pack(pieces[:-1] + [conv_mine], 1)
    sm_names = SMALL + ['conv_w']
    sm_shapes = small_shapes + [(3, cw8)]
    w_sm = _pack([W[n] for n in SMALL] + [conv_w[0]], 0)
    m_sm = _pack([Mo[n] for n in SMALL] + [m_conv_w[0]], 0)
    v_sm = _pack([Vo[n] for n in SMALL] + [v_conv_w[0]], 0)
    outs_sm = [_unpack(o, sm_shapes) for o in _adam(sm_parts, w_sm, m_sm, v_sm, "adam_small")]
    for k in range(4):
        for n, a in zip(sm_names, outs_sm[k]):
            res[k][n] = a[None] if n == 'conv_w' else a
    return (loss, grad_x[None], *[res[0][n] for n in WEIGHTS], *[res[1][n] for n in WEIGHTS],
            *[res[2][n] for n in WEIGHTS], *[res[3][n] for n in WEIGHTS])
```

```python
import math

import jax
import jax.numpy as jnp
from jax import lax
from jax.experimental import pallas as pl
from jax.experimental.pallas import tpu as pltpu

F32 = jnp.float32
_BF = jnp.bfloat16
EPS = 1e-6
CHUNK = 64
ROPE_THETA = 10000.0
ADAM_LR, ADAM_B1, ADAM_B2, ADAM_EPS, ADAM_WD, ADAM_STEP = 0.001, 0.9, 0.999, 1e-08, 0.01, 10
NDEV = 8
AXES = ("x", "y", "c")
MESH = pl.DeviceIdType.MESH
LANES = 128
NEG = -1e30
LOG2E = math.log2(math.e)
V7X_VMEM_LIMIT = 56 * 1024 * 1024
PACK_C = 1024
ATT_BLOCK = 512
ATT_UNROLL_FWD = 8
ATT_UNROLL_BWD = 6
ATT_QBLOCKS = 2
ADAM_BLOCK_ELEMS = 256 * 1024

WEIGHTS = ['norm_g', 'w_in', 'conv_w', 'w_conv_out', 'mla_q_norm_g', 'w_uq', 'mla_kv_norm_g', 'w_ukv',
           'mla_qn_nope_g', 'mla_qn_rope_g', 'mla_kn_nope_g', 'mla_kn_rope_g', 'w_mla_out', 'mem_norm_g',
           'w_mem_kv', 'mem_qn_g', 'mem_kn_g', 'w_mem_out', 'w_o']
BIG = ['w_in', 'w_conv_out', 'w_uq', 'w_ukv', 'w_mla_out', 'w_mem_kv', 'w_mem_out', 'w_o']
COL_SHARDED = ('w_in', 'w_conv_out', 'w_uq', 'w_ukv', 'w_mem_out')
SMALL = ['norm_g', 'mla_q_norm_g', 'mla_kv_norm_g', 'mla_qn_nope_g', 'mla_qn_rope_g', 'mla_kn_nope_g',
         'mla_kn_rope_g', 'mem_norm_g', 'mem_qn_g', 'mem_kn_g']


def _tile(dim, target, align):
    if dim <= target:
        return dim
    t = target - target % align
    while t > 0:
        if dim % t == 0:
            return t
        t -= align
    raise ValueError(f"no tile for {dim} {target} {align}")


def _cparams(sem):
    return pltpu.CompilerParams(dimension_semantics=sem, vmem_limit_bytes=V7X_VMEM_LIMIT)


def _sig(x):
    return 1.0 / (1.0 + jnp.exp(-x))


def _rms(x, n):
    r = lax.rsqrt(jnp.sum(x * x, axis=-1, keepdims=True) * (1.0 / n) + EPS)
    return x * r, r


def _rms_bwd(xhat, r, g, dy, n):
    dxh = dy * g
    dx = r * (dxh - xhat * (jnp.sum(dxh * xhat, axis=-1, keepdims=True) * (1.0 / n)))
    return dx, dy * xhat


def _rowmean128(x, n):
    return jnp.dot(x.astype(_BF), jnp.full((LANES, LANES), 1.0 / n, _BF), preferred_element_type=F32)


def _rms128(x, n):
    r = lax.rsqrt(_rowmean128(x * x, n) + EPS)
    return x * r, r


def _rms128_bwd(xhat, r, g, dy, n):
    dxh = dy * g
    dx = r * (dxh - xhat * _rowmean128(dxh * xhat, n))
    return dx, dy * xhat


def _rope(x, cosp, sina, sinb):
    return x * cosp + pltpu.roll(x, 96, 1) * sina + pltpu.roll(x, 32, 1) * sinb


def _rope_t(d, cosp, sina, sinb):
    return d * cosp + pltpu.roll(d * sina, 32, 1) + pltpu.roll(d * sinb, 96, 1)


def _dot_nt(a, b):
    return lax.dot_general(a, b, (((1,), (1,)), ((), ())), preferred_element_type=F32)


def _dot_tn(a, b):
    return lax.dot_general(a, b, (((0,), (0,)), ((), ())), preferred_element_type=F32)


def _dot(a, b):
    return jnp.dot(a, b, preferred_element_type=F32)


def _all_gather(shards, name):
    na = len(shards)
    nc = 9
    halves = [s.shape[0] // 32 * 16 if s.shape[0] >= 256 else None for s in shards]

    def body(*refs):
        x_refs, out_refs = refs[:na], refs[na:2 * na]
        send_sems, recv_sems, local_sems = refs[2 * na:]
        x, y, c = lax.axis_index("x"), lax.axis_index("y"), lax.axis_index("c")
        me, sib = (x, y, c), (x, y, 1 - c)
        px, py, pd = (1 - x, y, c), (x, 1 - y, c), (1 - x, 1 - y, c)

        def other_core(p):
            return (p[0], p[1], 1 - p[2])

        def rows(a, blk, part=None):
            r = out_refs[a].at[4 * blk[0] + 2 * blk[1] + blk[2]]
            if part is None or halves[a] is None:
                return r
            rest = shards[a].shape[0] - halves[a]
            return r.at[pl.ds(0, halves[a])] if part == 0 else r.at[pl.ds(halves[a], rest)]

        def copy(a, k, blk, to, part=None, src=None):
            dst = rows(a, blk, part)
            return pltpu.make_async_remote_copy(
                src_ref=dst if src is None else src, dst_ref=dst,
                send_sem=send_sems.at[nc * a + k], recv_sem=recv_sems.at[nc * a + k],
                device_id=to, device_id_type=MESH)

        mine = [pltpu.make_async_copy(x_refs[a], rows(a, me), local_sems.at[a]) for a in range(na)]
        for cp in mine:
            cp.start()
        started = []
        for a in range(na):
            started += [copy(a, 0, me, px, src=x_refs[a]), copy(a, 1, me, py, src=x_refs[a]),
                        copy(a, 2, me, sib, src=x_refs[a])]
        for cp in started:
            cp.start()

        def forward(cp):
            cp.start()
            started.append(cp)

        for a in range(na):
            copy(a, 0, px, me).wait_recv()
            forward(copy(a, 3, px, py, part=0))
            forward(copy(a, 4, px, sib))
        for a in range(na):
            copy(a, 1, py, me).wait_recv()
            if halves[a] is not None:
                forward(copy(a, 5, py, px, part=1))
            forward(copy(a, 6, py, sib))
        for a in range(na):
            copy(a, 3, pd, me, part=0).wait_recv()
            forward(copy(a, 7, pd, sib, part=0))
            if halves[a] is not None:
                copy(a, 5, pd, me, part=1).wait_recv()
                forward(copy(a, 8, pd, sib, part=1))
        for a in range(na):
            copy(a, 2, sib, me).wait_recv()
            copy(a, 4, other_core(px), me).wait_recv()
            copy(a, 6, other_core(py), me).wait_recv()
            copy(a, 7, other_core(pd), me, part=0).wait_recv()
            if halves[a] is not None:
                copy(a, 8, other_core(pd), me, part=1).wait_recv()
        for cp in started:
            cp.wait_send()
        for cp in mine:
            cp.wait()

    any_spec = pl.BlockSpec(memory_space=pl.ANY)
    return pl.pallas_call(
        body, name=name,
        out_shape=[jax.ShapeDtypeStruct((NDEV,) + s.shape, s.dtype) for s in shards],
        in_specs=[any_spec] * na, out_specs=[any_spec] * na,
        scratch_shapes=[pltpu.SemaphoreType.DMA((nc * na,)), pltpu.SemaphoreType.DMA((nc * na,)),
                        pltpu.SemaphoreType.DMA((na,))],
    )(*shards)


_HBM = pl.BlockSpec(memory_space=pltpu.HBM)
_SEM = pl.BlockSpec(memory_space=pltpu.SEMAPHORE)
_EFFECT = pltpu.SideEffectType.DATAFLOW_SIDE_EFFECTING


def _split_copy(a, k, src_refs, land_refs, send_sems, recv_sems, gather, receive_side):
    x, y, c = lax.axis_index("x"), lax.axis_index("y"), lax.axis_index("c")
    me = 4 * x + 2 * y + c
    tx, ty, tc = x ^ ((k + 1) >> 2 & 1), y ^ ((k + 1) >> 1 & 1), c ^ ((k + 1) & 1)
    peer = 4 * tx + 2 * ty + tc
    return pltpu.make_async_remote_copy(
        src_ref=src_refs[a] if gather else src_refs[a].at[peer],
        dst_ref=land_refs[a].at[peer if receive_side else me],
        send_sem=send_sems.at[7 * a + k], recv_sem=recv_sems.at[7 * a + k],
        device_id=(tx, ty, tc), device_id_type=MESH)


def _own_copy(a, na, src_refs, land_refs, send_sems, gather):
    me = 4 * lax.axis_index("x") + 2 * lax.axis_index("y") + lax.axis_index("c")
    return pltpu.make_async_copy(src_refs[a] if gather else src_refs[a].at[me], land_refs[a].at[me],
                                 send_sems.at[7 * na + a])


def _split_start(srcs, lands, gather, name, after=None):
    na = len(srcs)
    extra = [] if after is None else [after]

    def body(*refs):
        src_refs, land_refs = refs[:na], refs[na:2 * na]
        send_sems, recv_sems = refs[2 * na + len(extra)], refs[2 * na + len(extra) + 1]
        token = refs[-1]
        for k in range(7):
            for a in range(na):
                _split_copy(a, k, src_refs, land_refs, send_sems, recv_sems, gather, False).start()
        for a in range(na):
            _own_copy(a, na, src_refs, land_refs, send_sems, gather).start()
        token[...] = jnp.zeros_like(token)

    hbm = [pltpu.HBM(b.shape, b.dtype) for b in list(srcs) + list(lands)]
    outs = pl.pallas_call(
        body, name=name,
        out_shape=(pltpu.SemaphoreType.DMA((8 * na,)), pltpu.SemaphoreType.DMA((7 * na,)), *hbm,
                   jax.ShapeDtypeStruct((8, LANES), F32)),
        in_specs=[_HBM] * (2 * na) + [pl.BlockSpec(memory_space=pl.ANY)] * len(extra),
        out_specs=(_SEM, _SEM, *[_HBM] * (2 * na), pl.BlockSpec(memory_space=pltpu.VMEM)),
        input_output_aliases={j: 2 + j for j in range(2 * na)},
        compiler_params=pltpu.CompilerParams(has_side_effects=_EFFECT),
    )(*[pltpu.with_memory_space_constraint(b, pltpu.HBM) for b in srcs],
      *[pltpu.with_memory_space_constraint(l, pltpu.HBM) for l in lands], *extra)
    return outs[0], outs[1], outs[2:2 + na], outs[2 + na:2 + 2 * na], outs[-1]


def _split_wait(started, gather, after, name):
    send_sems, recv_sems, srcs, lands, _ = started
    na = len(srcs)

    def body(*refs):
        src_refs, land_refs = refs[:na], refs[na:2 * na]
        send_s, recv_s = refs[2 * na], refs[2 * na + 1]
        for k in range(7):
            for a in range(na):
                cp = _split_copy(a, k, src_refs, land_refs, send_s, recv_s, gather, True)
                cp.wait_send()
                cp.wait_recv()
        for a in range(na):
            _own_copy(a, na, src_refs, land_refs, send_s, gather).wait()

    hbm = [pltpu.HBM(b.shape, b.dtype) for b in list(srcs) + list(lands)]
    outs = pl.pallas_call(
        body, name=name, out_shape=tuple(hbm),
        in_specs=[_HBM] * (2 * na) + [_SEM, _SEM, pl.BlockSpec(memory_space=pl.ANY)],
        out_specs=tuple([_HBM] * (2 * na)),
        input_output_aliases={j: j for j in range(2 * na)},
        compiler_params=pltpu.CompilerParams(has_side_effects=_EFFECT),
    )(*srcs, *lands, send_sems, recv_sems, after)
    return outs[na:]


def _seg_rows(size):
    rows = -(-size // PACK_C)
    return -(-rows // 16) * 16


def _pack(arrs, lead):
    parts = []
    for a in arrs:
        lshape = a.shape[:lead]
        f = a.reshape(lshape + (-1,)).astype(F32)
        rows = _seg_rows(f.shape[-1])
        f = jnp.pad(f, [(0, 0)] * lead + [(0, rows * PACK_C - f.shape[-1])])
        parts.append(f.reshape(lshape + (rows, PACK_C)))
    return jnp.concatenate(parts, axis=lead)


def _unpack(buf, shapes):
    lshape = buf.shape[:-2]
    out, r = [], 0
    for shp in shapes:
        size = math.prod(shp)
        rows = _seg_rows(size)
        seg = buf[..., r:r + rows, :].reshape(lshape + (rows * PACK_C,))[..., :size]
        out.append(seg.reshape(lshape + tuple(shp)))
        r += rows
    return out


def _mm(a, b, *, name, out_dtype, ta=False, tb=False, bm=1024, bn=1024, bk=2048, after=None, plus=None):
    if ta:
        kdim, m = a.shape
    else:
        m, kdim = a.shape
    n, k2 = b.shape if tb else b.shape[::-1]
    assert kdim == k2 and not (ta and tb), (a.shape, b.shape)
    bm = _tile(m, bm, LANES if ta else 16)
    bn = _tile(n, bn, LANES)
    bk = _tile(kdim, bk, LANES)
    nk = kdim // bk
    n_after = 0 if after is None else 1
    n_plus = 0 if plus is None else 2

    def kern(a_ref, b_ref, *rest):
        plus_refs = rest[n_after:n_after + n_plus]
        o_ref, scratch = rest[n_after + n_plus], rest[n_after + n_plus + 1:]
        part = (_dot_tn if ta else _dot_nt if tb else _dot)(a_ref[...], b_ref[...])

        def first(p):
            return p + _dot(plus_refs[0][...], plus_refs[1][...]) if plus is not None else p

        if nk == 1:
            o_ref[...] = first(part).astype(o_ref.dtype)
        else:
            acc = scratch[0] if scratch else o_ref
            k = pl.program_id(2)

            @pl.when(k == 0)
            def _():
                acc[...] = first(jnp.zeros(acc.shape, F32))

            acc[...] += part
            if scratch:
                @pl.when(k == nk - 1)
                def _():
                    o_ref[...] = acc[...].astype(o_ref.dtype)

    a_spec = pl.BlockSpec((bk, bm), lambda i, j, k: (k, i)) if ta else pl.BlockSpec((bm, bk), lambda i, j, k: (i, k))
    b_spec = pl.BlockSpec((bn, bk), lambda i, j, k: (j, k)) if tb else pl.BlockSpec((bk, bn), lambda i, j, k: (k, j))
    extra_specs, extra_args = [], []
    if after is not None:
        extra_specs.append(pl.BlockSpec(after.shape, lambda i, j, k: (0, 0)))
        extra_args.append(after)
    if plus is not None:
        kk = plus[0].shape[1]
        extra_specs += [pl.BlockSpec((bm, kk), lambda i, j, k: (i, 0)), pl.BlockSpec((kk, bn), lambda i, j, k: (0, j))]
        extra_args += list(plus)
    return pl.pallas_call(
        kern, name=name, grid=(m // bm, n // bn, nk),
        in_specs=[a_spec, b_spec] + extra_specs,
        out_specs=pl.BlockSpec((bm, bn), lambda i, j, k: (i, j)),
        out_shape=jax.ShapeDtypeStruct((m, n), out_dtype),
        scratch_shapes=[pltpu.VMEM((bm, bn), F32)] if nk > 1 and out_dtype != F32 else [],
        compiler_params=_cparams(("parallel", "parallel", "arbitrary")),
    )(a, b, *extra_args)


def _adam(parts, w_a, m_a, v_a, name):
    rows, cols = w_a.shape
    rb = _tile(rows, max(8, ADAM_BLOCK_ELEMS // cols // 8 * 8), 8)
    bc1 = 1.0 - ADAM_B1 ** ADAM_STEP
    bc2 = 1.0 - ADAM_B2 ** ADAM_STEP

    def adam_kern(p_ref, w_ref, m_ref, v_ref, g_ref, d_ref, nm_ref, nv_ref):
        g = p_ref[0].astype(F32)
        for j in range(1, NDEV):
            g = g + p_ref[j].astype(F32)
        m_new = ADAM_B1 * m_ref[...] + (1.0 - ADAM_B1) * g
        v_new = ADAM_B2 * v_ref[...] + (1.0 - ADAM_B2) * (g * g)
        g_ref[...] = g
        nm_ref[...] = m_new
        nv_ref[...] = v_new
        d_ref[...] = -ADAM_LR * ((m_new / bc1) / (jnp.sqrt(v_new / bc2) + ADAM_EPS) + ADAM_WD * w_ref[...])

    blk = pl.BlockSpec((rb, cols), lambda i: (i, 0))
    return pl.pallas_call(
        adam_kern, name=name, grid=(rows // rb,),
        in_specs=[pl.BlockSpec((NDEV, rb, cols), lambda i: (0, i, 0)), blk, blk, blk],
        out_specs=[blk] * 4, out_shape=[jax.ShapeDtypeStruct((rows, cols), F32)] * 4,
        compiler_params=_cparams(("parallel",)),
    )(parts, w_a, m_a, v_a)


class _Cfg:
    pass


def _config(x, conv_w, w_uq, w_ukv, mla_qn_nope_g, mla_qn_rope_g, mem, mem_qn_g, w_mem_out, w_mla_out):
    c = _Cfg()
    c.N, c.D = x.shape[1], x.shape[2]
    c.CW = conv_w.shape[2] * NDEV
    c.QL, c.KVL = w_uq.shape[1], w_ukv.shape[1]
    c.NOPE, c.ROPE = mla_qn_nope_g.shape[1], mla_qn_rope_g.shape[1]
    c.H = w_uq.shape[2] * NDEV // (c.NOPE + c.ROPE)
    c.V = w_ukv.shape[2] * NDEV // c.H - c.NOPE
    assert c.NOPE == LANES and c.V == LANES and c.ROPE == LANES // 2
    c.HW = 2 * LANES
    c.HV = c.H * c.V
    assert w_mla_out.shape[1] * NDEV == c.HV
    c.M = mem.shape[1]
    c.MHD = mem_qn_g.shape[1]
    c.MW = w_mem_out.shape[1]
    c.MH = c.MW // c.MHD
    c.o_conv = 0
    c.o_mz = 4 * c.CW
    c.o_g = c.o_mz + c.HV
    c.o_mem = c.o_g + 3 * c.D
    c.o_lora = c.o_mem + 2 * c.MW
    c.P = c.o_lora + c.QL + c.KVL
    assert c.o_mz % c.HV == 0 and c.o_g % (3 * c.D) == 0 and c.o_mem % (2 * c.MW) == 0
    assert c.o_lora % (c.QL + c.KVL) == 0 and c.QL % LANES == 0 and c.KVL % LANES == 0
    c.IN = 4 * c.CW + c.QL + c.KVL + c.ROPE + c.HV + 2 * c.MW + 3 * c.D
    c.R = _tile(c.N, 256, 16)
    c.RP = _tile(c.N, 512, 16)
    c.HG = _tile(c.H, 4, 1)
    c.B = _tile(c.N, ATT_BLOCK, CHUNK)
    c.scale = float((c.NOPE + c.ROPE) ** -0.5)
    c.mscale = float(c.MHD ** -0.5)
    return c


def _win_segments(c):
    ref_order = (('conv', 4 * c.CW), ('lora', c.QL + c.KVL), ('kr', c.ROPE), ('mz', c.HV), ('mem', 2 * c.MW), ('g', 3 * c.D))
    mine = {'conv': c.o_conv, 'mz': c.o_mz, 'g': c.o_g, 'mem': c.o_mem, 'lora': c.o_lora, 'kr': 0}
    segs, o = [], 0
    for nm, wd in ref_order:
        segs.append((nm, o, wd, mine[nm]))
        o += wd
    return segs


def _win_split(g_win_t, c):
    n8 = g_win_t.shape[1]

    def rows(a, wd):
        return [g_win_t[j][max(a, j * n8) - j * n8:min(a + wd, (j + 1) * n8) - j * n8]
                for j in range(a // n8, (a + wd - 1) // n8 + 1)]

    segs = {nm: (a, wd) for nm, a, wd, _ in _win_segments(c)}
    main = [p for nm in ('conv', 'mz', 'g', 'mem', 'lora') for p in rows(*segs[nm])]
    kr = jnp.concatenate(rows(*segs['kr']) + [jnp.zeros((LANES - c.ROPE, g_win_t.shape[2]), g_win_t.dtype)], axis=0)
    return jnp.concatenate(main, axis=0), kr


def _win_blocks(g, g_kr, c):
    n8 = c.IN // NDEV
    blocks = []
    for j in range(NDEV):
        lo, hi = j * n8, (j + 1) * n8
        parts = []
        for nm, a, wd, mine in _win_segments(c):
            s, e = max(a, lo), min(a + wd, hi)
            if s < e:
                parts.append((g_kr if nm == 'kr' else g)[:, mine + s - a:mine + e - a])
        blocks.append(jnp.concatenate(parts, axis=1))
    return jnp.stack(blocks, axis=0)


def kernel(x, positions, mem, norm_g, w_in, conv_w, w_conv_out, mla_q_norm_g, w_uq, mla_kv_norm_g, w_ukv, mla_qn_nope_g, mla_qn_rope_g, mla_kn_nope_g, mla_kn_rope_g, w_mla_out, mem_norm_g, w_mem_kv, mem_qn_g, mem_kn_g, w_mem_out, w_o, loss_target, m_norm_g, m_w_in, m_conv_w, m_w_conv_out, m_mla_q_norm_g, m_w_uq, m_mla_kv_norm_g, m_w_ukv, m_mla_qn_nope_g, m_mla_qn_rope_g, m_mla_kn_nope_g, m_mla_kn_rope_g, m_w_mla_out, m_mem_norm_g, m_w_mem_kv, m_mem_qn_g, m_mem_kn_g, m_w_mem_out, m_w_o, v_norm_g, v_w_in, v_conv_w, v_w_conv_out, v_mla_q_norm_g, v_w_uq, v_mla_kv_norm_g, v_w_ukv, v_mla_qn_nope_g, v_mla_qn_rope_g, v_mla_kn_nope_g, v_mla_kn_rope_g, v_w_mla_out, v_mem_norm_g, v_w_mem_kv, v_mem_qn_g, v_mem_kn_g, v_w_mem_out, v_w_o):
    args = dict(locals())
    W = {n: args[n] for n in WEIGHTS}
    Mo = {n: args['m_' + n] for n in WEIGHTS}
    Vo = {n: args['v_' + n] for n in WEIGHTS}
    c = _config(x, conv_w, w_uq, w_ukv, mla_qn_nope_g, mla_qn_rope_g, mem, mem_qn_g, w_mem_out, w_mla_out)
    N, D, R, B, H = c.N, c.D, c.R, c.B, c.H
    assert x.shape[0] == 1
    xs = x[0]
    tgt = loss_target[0]
    memx = mem[0]
    me = 4 * lax.axis_index("x") + 2 * lax.axis_index("y") + lax.axis_index("c")
    nr = N // R

    g_win, g_taps = _all_gather([W['w_in'][0].astype(_BF).T, conv_w[0]], "ag_w_in")
    rest = [n for n in BIG if n != 'w_in']
    shards_r = [W[n][0].astype(_BF).T if n in COL_SHARDED else W[n][0].astype(_BF) for n in rest]
    lands_r = [lax.empty((NDEV,) + s.shape, s.dtype) for s in shards_r]
    ag_rest = _split_start(shards_r, lands_r, True, "ag_rest_start", after=g_win)
    win_pT, w_krT = _win_split(g_win, c)
    convw = jnp.transpose(g_taps, (1, 0, 2)).reshape(3, c.CW)
    convw8 = jnp.pad(convw, ((0, 5), (0, 0)))

    def rowb(width, cidx, rows=R):
        return pl.BlockSpec((rows, width), lambda i, _c=cidx: (i, _c))

    R2 = c.RP
    nr2 = N // R2
    RM = _tile(N, 1024, 16)

    def rowb2(width, cidx):
        return rowb(width, cidx, R2)

    def fullb(shape):
        nd = len(shape)
        return pl.BlockSpec(shape, lambda *_: (0,) * nd)

    def pad_lanes(g, w=LANES):
        return jnp.pad(g, ((0, 0), (0, w - g.shape[1])))

    def tabs_of(rows):
        return pl.BlockSpec((3, rows, LANES), lambda i, *_: (0, i, 0))

    half = c.ROPE // 2
    inv_freq = jnp.power(ROPE_THETA, -jnp.arange(half, dtype=F32) / half)
    invf = jnp.concatenate([inv_freq, inv_freq, jnp.zeros((LANES - c.ROPE,), F32)])[None, :]
    pos_col = positions[0].astype(F32).reshape(N, 1)

    def rope_tab_kern(pos_ref, invf_ref, o_ref):
        ang = pos_ref[...] * invf_ref[...]
        co, si = jnp.cos(ang), jnp.sin(ang)
        lane = lax.broadcasted_iota(jnp.int32, ang.shape, 1)
        o_ref[0] = jnp.where(lane < c.ROPE, co, 0.0)
        o_ref[1] = jnp.where(lane < half, -si, 0.0)
        o_ref[2] = jnp.where(jnp.logical_and(lane >= half, lane < c.ROPE), si, 0.0)

    tabs = pl.pallas_call(
        rope_tab_kern, name="rope_tab", grid=(nr,),
        in_specs=[pl.BlockSpec((R, 1), lambda i: (i, 0)), fullb((1, LANES))],
        out_specs=tabs_of(R),
        out_shape=jax.ShapeDtypeStruct((3, N, LANES), F32),
        compiler_params=_cparams(("parallel",)),
    )(pos_col, invf)

    def make_rms_kern():
        def rms_fwd_kern(x_ref, g_ref, o_ref):
            xh, _ = _rms(x_ref[...].astype(F32), x_ref.shape[-1])
            o_ref[...] = (xh * g_ref[...]).astype(o_ref.dtype)
        return rms_fwd_kern

    h = pl.pallas_call(
        make_rms_kern(), name="rms_x", grid=(nr2,),
        in_specs=[rowb2(D, 0), fullb((1, D))], out_specs=rowb2(D, 0),
        out_shape=jax.ShapeDtypeStruct((N, D), _BF), compiler_params=_cparams(("parallel",)),
    )(xs, norm_g)

    proj = _mm(h, win_pT, tb=True, name="mm_proj", out_dtype=_BF, after=ag_rest[4])
    kr_raw = _mm(h, w_krT, tb=True, name="mm_kr", out_dtype=_BF)

    Wf = {n: g.reshape(-1, g.shape[2]) for n, g in zip(rest, _split_wait(ag_rest, True, proj, "ag_rest_wait"))}
    wuqT = Wf['w_uq'].reshape(H, c.NOPE + c.ROPE, c.QL)
    wuq_pT = jnp.pad(wuqT, ((0, 0), (0, c.HW - c.NOPE - c.ROPE), (0, 0))).reshape(H * c.HW, c.QL)
    wukvT = Wf['w_ukv']
    wcoT, wmo, wmkv, wmemoT, wo = Wf['w_conv_out'], Wf['w_mla_out'], Wf['w_mem_kv'], Wf['w_mem_out'], Wf['w_o']

    CW = c.CW
    conv_blk = c.o_conv // (4 * CW)
    HALO = 16
    rh = R // HALO

    def conv_parts(blk):
        blk = blk.astype(F32)
        return blk[:, 0:CW], blk[:, CW:2 * CW], blk[:, 2 * CW:3 * CW], blk[:, 3 * CW:4 * CW]

    def shifted(cu, prev, i):
        prev = jnp.where(i > 0, prev, 0.0)
        rid = lax.broadcasted_iota(jnp.int32, cu.shape, 0)
        last, last2 = prev[HALO - 1:HALO, :], prev[HALO - 2:HALO - 1, :]
        sh1 = jnp.where(rid == 0, last, pltpu.roll(cu, 1, 0))
        sh2 = jnp.where(rid == 0, last2, jnp.where(rid == 1, last, pltpu.roll(cu, 2, 0)))
        return sh1, sh2

    def conv_fwd_kern(p_ref, prev_ref, w_ref, o_ref):
        i = pl.program_id(0)
        cg, bg, u, z = conv_parts(p_ref[...])
        pc, _, pu, _ = conv_parts(prev_ref[...])
        cu = cg * u
        sh1, sh2 = shifted(cu, pc * pu, i)
        w = w_ref[...]
        conv = w[0:1, :] * sh2 + w[1:2, :] * sh1 + w[2:3, :] * cu
        o_ref[...] = (bg * conv * (z * _sig(z))).astype(o_ref.dtype)

    prev_spec = pl.BlockSpec((HALO, 4 * CW), lambda i: (jnp.maximum(i * rh - 1, 0), conv_blk))
    a_conv = pl.pallas_call(
        conv_fwd_kern, name="conv_fwd", grid=(nr,),
        in_specs=[rowb(4 * CW, conv_blk), prev_spec, fullb((8, CW))],
        out_specs=rowb(CW, 0), out_shape=jax.ShapeDtypeStruct((N, CW), _BF),
        compiler_params=_cparams(("parallel",)),
    )(proj, proj, convw8)
    o_conv = _mm(a_conv, wcoT, tb=True, name="mm_oconv", out_dtype=_BF)

    QL, KVL, HW = c.QL, c.KVL, c.HW
    lora_blk = c.o_lora // (QL + KVL)

    def lora_fwd_kern(p_ref, gq_ref, gkv_ref, q_ref, kv_ref):
        blk = p_ref[...].astype(F32)
        qh, _ = _rms(blk[:, :QL], QL)
        kh, _ = _rms(blk[:, QL:], KVL)
        q_ref[...] = (qh * gq_ref[...]).astype(q_ref.dtype)
        kv_ref[...] = (kh * gkv_ref[...]).astype(kv_ref.dtype)

    cqn, ckvn = pl.pallas_call(
        lora_fwd_kern, name="lora_fwd", grid=(nr2,),
        in_specs=[rowb2(QL + KVL, lora_blk), fullb((1, QL)), fullb((1, KVL))],
        out_specs=[rowb2(QL, 0), rowb2(KVL, 0)],
        out_shape=[jax.ShapeDtypeStruct((N, QL), _BF), jax.ShapeDtypeStruct((N, KVL), _BF)],
        compiler_params=_cparams(("parallel",)),
    )(proj, mla_q_norm_g, mla_kv_norm_g)
    q_p = _mm(cqn, wuq_pT, tb=True, name="mm_q", out_dtype=_BF, bn=2048)
    kv = _mm(ckvn, wukvT, tb=True, name="mm_kv", out_dtype=_BF, bn=2048)

    g_qn, g_qr = mla_qn_nope_g, pad_lanes(mla_qn_rope_g)
    g_kn, g_kr = mla_kn_nope_g, pad_lanes(mla_kn_rope_g)

    def krope_fwd_kern(p_ref, t_ref, g_ref, o_ref):
        xh, _ = _rms128(p_ref[...].astype(F32), c.ROPE)
        o_ref[...] = _rope(xh * g_ref[...], t_ref[0], t_ref[1], t_ref[2]).astype(o_ref.dtype)

    k_rope = pl.pallas_call(
        krope_fwd_kern, name="krope_fwd", grid=(nr2,),
        in_specs=[rowb2(LANES, 0), tabs_of(R2), fullb((1, LANES))],
        out_specs=rowb2(LANES, 0), out_shape=jax.ShapeDtypeStruct((N, LANES), _BF),
        compiler_params=_cparams(("parallel",)),
    )(kr_raw, tabs, g_kr)

    RP, HG = c.RP, c.HG
    nrp, nhg = N // RP, H // HG
    heads_in = pl.BlockSpec((RP, HG * HW), lambda i, hg: (i, hg))
    heads_out = pl.BlockSpec((HG, RP, HW), lambda i, hg: (hg, i, 0))

    def q_prep_kern(q_ref, t_ref, gn_ref, gr_ref, o_ref):
        for g in range(HG):
            blk = q_ref[:, g * HW:(g + 1) * HW].astype(F32)
            nh, _ = _rms128(blk[:, :LANES], c.NOPE)
            rhat, _ = _rms128(blk[:, LANES:], c.ROPE)
            rot = _rope(rhat * gr_ref[...], t_ref[0], t_ref[1], t_ref[2])
            o_ref[g] = jnp.concatenate([nh * gn_ref[...], rot], axis=1).astype(o_ref.dtype)

    q_cat = pl.pallas_call(
        q_prep_kern, name="q_prep", grid=(nrp, nhg),
        in_specs=[heads_in, tabs_of(RP), fullb((1, LANES)), fullb((1, LANES))],
        out_specs=heads_out, out_shape=jax.ShapeDtypeStruct((H, N, HW), _BF),
        compiler_params=_cparams(("parallel", "parallel")),
    )(q_p, tabs, g_qn * (c.scale * LOG2E), g_qr * (c.scale * LOG2E))

    def k_prep_kern(kv_ref, kr_ref, gn_ref, o_ref):
        for g in range(HG):
            kn, _ = _rms128(kv_ref[:, g * HW:g * HW + LANES].astype(F32), c.NOPE)
            o_ref[g] = jnp.concatenate([(kn * gn_ref[...]).astype(o_ref.dtype), kr_ref[...]], axis=1)

    k_cat = pl.pallas_call(
        k_prep_kern, name="k_prep", grid=(nrp, nhg),
        in_specs=[heads_in, pl.BlockSpec((RP, LANES), lambda i, hg: (i, 0)), fullb((1, LANES))],
        out_specs=heads_out, out_shape=jax.ShapeDtypeStruct((H, N, HW), _BF),
        compiler_params=_cparams(("parallel", "parallel")),
    )(kv, k_rope, g_kn)

    QB = ATT_QBLOCKS if N % (ATT_QBLOCKS * B) == 0 else 1
    BQ = QB * B
    nq = N // BQ
    assert CHUNK & (CHUNK - 1) == 0 and B % CHUNK == 0 and ATT_UNROLL_FWD % QB == 0 and ATT_UNROLL_BWD % QB == 0

    def live_rows(diag):
        return slice(0 if diag is None else diag * B, BQ)

    def diag_mask(s):
        row = lax.broadcasted_iota(jnp.int32, s.shape, 0)
        col = lax.broadcasted_iota(jnp.int32, s.shape, 1)
        shift = CHUNK.bit_length() - 1
        allowed = jnp.right_shift(col, shift) <= jnp.right_shift(row, shift)
        return jnp.where(allowed, s, NEG)

    k_head = pl.BlockSpec((1, N, HW), lambda hh, i: (hh, 0, 0))
    v_head = pl.BlockSpec((N, LANES), lambda hh, i: (0, 2 * hh + 1))
    q_blk = pl.BlockSpec((1, BQ, HW), lambda hh, i: (hh, i, 0))
    o_blk = pl.BlockSpec((BQ, LANES), lambda hh, i: (i, hh))
    lse_blk = pl.BlockSpec((1, BQ, LANES), lambda hh, i: (hh, i, 0))

    def key_block_plan(i, run, unroll):
        def unrolled(u, carry):
            run(unroll * u, unroll, 0)
            return carry

        n_full = QB * i
        lax.fori_loop(0, n_full // unroll, unrolled, 0)
        for rem in range(0, unroll, QB):
            @pl.when(n_full % unroll == rem)
            def _(rem=rem):
                run(n_full - rem, rem + QB, QB)

    def attn_fwd_kern(q_ref, k_ref, v_ref, o_ref, lse_ref, m_sc, acc_sc, s_sc):
        i = pl.program_id(1)
        m_sc[...] = jnp.full(m_sc.shape, NEG, F32)
        acc_sc[...] = jnp.zeros(acc_sc.shape, F32)

        def rows_of(t):
            return pl.ds(pl.multiple_of(t * B, B), B)

        def scores(t, slot, diag):
            rs = live_rows(diag)
            s_sc[slot, rs, :] = _dot_nt(q_ref[0, rs, :], k_ref[0, rows_of(t), :])

        def softmax_pv(t, slot, diag):
            rs = live_rows(diag)
            s = s_sc[slot, rs, :]
            if diag is not None:
                s = diag_mask(s)
            mt = s[:, 0:LANES]
            for cb in range(1, B // LANES):
                mt = jnp.maximum(mt, s[:, cb * LANES:(cb + 1) * LANES])
            m_prev = m_sc[rs, :]
            m_new = jnp.maximum(m_prev, jnp.max(mt, axis=1, keepdims=True))
            alpha = jnp.exp2(m_prev - m_new)
            p = jnp.concatenate([jnp.exp2(s[:, cb * LANES:(cb + 1) * LANES] - m_new).astype(_BF)
                                 for cb in range(B // LANES)], axis=1)
            v_ones = jnp.concatenate([v_ref[rows_of(t), :], jnp.ones((B, LANES), _BF)], axis=1)
            acc_sc[rs, :] = jnp.concatenate([alpha, alpha], axis=1) * acc_sc[rs, :] + _dot(p, v_ones)
            m_sc[rs, :] = m_new

        scores(0, 0, None)

        def run(first, count, n_diag):
            def diag_of(u):
                return u - (count - n_diag) if count - n_diag <= u < count else None

            for u in range(count):
                if u + 1 < count or n_diag == 0:
                    scores(first + u + 1, (u + 1) % 2, diag_of(u + 1))
                softmax_pv(first + u, u % 2, diag_of(u))

        key_block_plan(i, run, ATT_UNROLL_FWD)
        acc = acc_sc[...]
        o_ref[...] = (acc[:, :LANES] / acc[:, LANES:]).astype(o_ref.dtype)
        lse_ref[0] = m_sc[...] + jnp.log2(acc[:, LANES:])

    mla_y, lse = pl.pallas_call(
        attn_fwd_kern, name="attn_fwd", grid=(H, nq),
        in_specs=[q_blk, k_head, v_head], out_specs=[o_blk, lse_blk],
        out_shape=[jax.ShapeDtypeStruct((N, c.HV), _BF), jax.ShapeDtypeStruct((H, N, LANES), F32)],
        scratch_shapes=[pltpu.VMEM((BQ, LANES), F32), pltpu.VMEM((BQ, HW), F32), pltpu.VMEM((2, BQ, B), F32)],
        compiler_params=_cparams(("parallel", "arbitrary")),
    )(q_cat, k_cat, kv)

    HV = c.HV
    mz_blk = c.o_mz // HV

    def gate_fwd_kern(y_ref, z_ref, o_ref):
        z = z_ref[...].astype(F32)
        o_ref[...] = (y_ref[...].astype(F32) * (z * _sig(z))).astype(o_ref.dtype)

    a_mla = pl.pallas_call(
        gate_fwd_kern, name="gate_mla", grid=(nr2,),
        in_specs=[rowb2(HV, 0), rowb2(HV, mz_blk)], out_specs=rowb2(HV, 0),
        out_shape=jax.ShapeDtypeStruct((N, HV), _BF), compiler_params=_cparams(("parallel",)),
    )(mla_y, proj)
    o_mla = _mm(a_mla, wmo, name="mm_omla", out_dtype=_BF)

    M, MW, MH, MHD = c.M, c.MW, c.MH, c.MHD
    memn = pl.pallas_call(
        make_rms_kern(), name="rms_mem",
        grid=(1,), in_specs=[fullb((M, D)), fullb((1, D))], out_specs=fullb((M, D)),
        out_shape=jax.ShapeDtypeStruct((M, D), _BF), compiler_params=_cparams(("arbitrary",)),
    )(memx, mem_norm_g)
    kvm = _mm(memn, wmkv, name="mm_memkv", out_dtype=F32)

    def memk_fwd_kern(kv_ref, g_ref, k_ref, v_ref):
        for hh in range(MH):
            kh, _ = _rms(kv_ref[:, hh * MHD:(hh + 1) * MHD], MHD)
            k_ref[:, hh * MHD:(hh + 1) * MHD] = (kh * g_ref[...]).astype(k_ref.dtype)
        v_ref[...] = kv_ref[:, MW:].astype(v_ref.dtype)

    mem_k, mem_v = pl.pallas_call(
        memk_fwd_kern, name="memk_fwd", grid=(1,),
        in_specs=[fullb((M, 2 * MW)), fullb((1, MHD))], out_specs=[fullb((M, MW)), fullb((M, MW))],
        out_shape=[jax.ShapeDtypeStruct((M, MW), _BF)] * 2, compiler_params=_cparams(("arbitrary",)),
    )(kvm, mem_kn_g)

    mem_blk = c.o_mem // (2 * MW)

    def mem_head(qz_ref, k_ref, v_ref, g_ref, hh):
        sl = slice(hh * MHD, (hh + 1) * MHD)
        qh, r = _rms(qz_ref[:, sl].astype(F32), MHD)
        qn = (qh * g_ref[...]).astype(_BF)
        s = _dot_nt(qn, k_ref[:, sl]) * c.mscale
        e = jnp.exp(s - jnp.max(s, axis=1, keepdims=True))
        p = e / jnp.sum(e, axis=1, keepdims=True)
        y = _dot(p.astype(_BF), v_ref[:, sl])
        z = qz_ref[:, MW + hh * MHD:MW + (hh + 1) * MHD].astype(F32)
        return sl, qh, r, qn, p, y, z

    def mem_fwd_kern(qz_ref, k_ref, v_ref, g_ref, o_ref):
        for hh in range(MH):
            sl, _, _, _, _, y, z = mem_head(qz_ref, k_ref, v_ref, g_ref, hh)
            o_ref[:, sl] = (y * (z * _sig(z))).astype(o_ref.dtype)

    a_mem = pl.pallas_call(
        mem_fwd_kern, name="mem_fwd", grid=(N // RM,),
        in_specs=[rowb(2 * MW, mem_blk, RM), fullb((M, MW)), fullb((M, MW)), fullb((1, MHD))],
        out_specs=rowb(MW, 0, RM), out_shape=jax.ShapeDtypeStruct((N, MW), _BF),
        compiler_params=_cparams(("parallel",)),
    )(proj, mem_k, mem_v, mem_qn_g)
    o_mem = _mm(a_mem, wmemoT, tb=True, name="mm_omem", out_dtype=_BF)

    g_blk = c.o_g // (3 * D)

    def merge_fwd_kern(g_ref, oc_ref, om_ref, ome_ref, o_ref):
        g = g_ref[...].astype(F32)
        acc = _sig(g[:, :D]) * oc_ref[...].astype(F32)
        acc += _sig(g[:, D:2 * D]) * om_ref[...].astype(F32)
        acc += _sig(g[:, 2 * D:]) * ome_ref[...].astype(F32)
        o_ref[...] = acc.astype(o_ref.dtype)

    merged = pl.pallas_call(
        merge_fwd_kern, name="merge_fwd", grid=(nr2,),
        in_specs=[rowb2(3 * D, g_blk), rowb2(D, 0), rowb2(D, 0), rowb2(D, 0)], out_specs=rowb2(D, 0),
        out_shape=jax.ShapeDtypeStruct((N, D), _BF), compiler_params=_cparams(("parallel",)),
    )(proj, o_conv, o_mla, o_mem)
    obm, obn = _tile(N, 1024, 16), _tile(D, 1024, LANES)

    def out_loss_kern(a_ref, w_ref, x_ref, t_ref, dyb_ref, l_ref):
        e = x_ref[...] + _dot(a_ref[...], w_ref[...]) - t_ref[...]
        dyb_ref[...] = (e * (1.0 / D)).astype(dyb_ref.dtype)
        row = lax.broadcasted_iota(jnp.int32, l_ref.shape, 0)
        l_ref[...] = jnp.where(row == 0, jnp.sum(e * e, axis=0, keepdims=True), 0.0)

    oblk = pl.BlockSpec((obm, obn), lambda i, j: (i, j))
    dyb, lpart = pl.pallas_call(
        out_loss_kern, name="mm_out_loss", grid=(N // obm, D // obn),
        in_specs=[pl.BlockSpec((obm, D), lambda i, j: (i, 0)), pl.BlockSpec((D, obn), lambda i, j: (0, j)), oblk, oblk],
        out_specs=[oblk, pl.BlockSpec((8, obn), lambda i, j: (i, j))],
        out_shape=[jax.ShapeDtypeStruct((N, D), _BF), jax.ShapeDtypeStruct((8 * (N // obm), D), F32)],
        compiler_params=_cparams(("parallel", "parallel")),
    )(merged, wo, xs, tgt)
    loss = lax.psum(jnp.sum(lpart) * (0.5 / D), AXES)

    G = {}
    d_merged = _mm(dyb, wo, tb=True, name="mm_dmerged", out_dtype=_BF)
    G['w_o'] = _mm(merged, dyb, ta=True, name="mm_dwo", out_dtype=_BF, bk=4096)

    dproj0 = lax.empty((N, c.P), _BF)
    any_spec = pl.BlockSpec(memory_space=pl.ANY)

    def merge_bwd_kern(dp_any, g_ref, dm_ref, oc_ref, om_ref, ome_ref, dg_ref, doc_ref, dom_ref, dome_ref):
        g = g_ref[...].astype(F32)
        dm = dm_ref[...].astype(F32)
        for idx, (o_in, d_out) in enumerate(((oc_ref, doc_ref), (om_ref, dom_ref), (ome_ref, dome_ref))):
            sg = _sig(g[:, idx * D:(idx + 1) * D])
            d_out[...] = (sg * dm).astype(d_out.dtype)
            dg_ref[:, idx * D:(idx + 1) * D] = (dm * o_in[...].astype(F32) * sg * (1.0 - sg)).astype(dg_ref.dtype)

    dproj1, d_oconv, d_omla, d_omem = pl.pallas_call(
        merge_bwd_kern, name="merge_bwd", grid=(nr,),
        in_specs=[any_spec, rowb(3 * D, g_blk), rowb(D, 0), rowb(D, 0), rowb(D, 0), rowb(D, 0)],
        out_specs=[rowb(3 * D, g_blk), rowb(D, 0), rowb(D, 0), rowb(D, 0)],
        out_shape=[jax.ShapeDtypeStruct((N, c.P), _BF)] + [jax.ShapeDtypeStruct((N, D), _BF)] * 3,
        input_output_aliases={0: 0}, compiler_params=_cparams(("parallel",)),
    )(dproj0, proj, d_merged, o_conv, o_mla, o_mem)

    G['w_conv_out'] = _mm(a_conv, d_oconv, ta=True, name="mm_dwco", out_dtype=_BF, bk=4096)
    d_aconv = _mm(d_oconv, wcoT, name="mm_daconv", out_dtype=_BF)
    G['w_mla_out'] = _mm(a_mla, d_omla, ta=True, name="mm_dwmo", out_dtype=_BF, bk=4096)
    d_amla = _mm(d_omla, wmo, tb=True, name="mm_damla", out_dtype=_BF)
    G['w_mem_out'] = _mm(a_mem, d_omem, ta=True, name="mm_dwmemo", out_dtype=_BF, bk=4096)
    d_amem = _mm(d_omem, wmemoT, name="mm_damem", out_dtype=_BF)

    def conv_bwd_kern(dp_any, p_ref, prev_ref, next_ref, da_ref, dan_ref, w_ref, o_ref, dw_ref):
        i = pl.program_id(0)
        cg, bg, u, z = conv_parts(p_ref[...])
        pc, _, pu, _ = conv_parts(prev_ref[...])
        _, nbg, _, nz = conv_parts(next_ref[...])
        cu = cg * u
        sh1, sh2 = shifted(cu, pc * pu, i)
        w = w_ref[...]
        conv = w[0:1, :] * sh2 + w[1:2, :] * sh1 + w[2:3, :] * cu
        sg = _sig(z)
        sz = z * sg
        da = da_ref[...].astype(F32)
        dcy = da * sz
        d_z = da * (bg * conv) * (sg * (1.0 + z * (1.0 - sg)))
        d_b = dcy * conv
        dconv = dcy * bg
        dnext = dan_ref[...].astype(F32) * (nz * _sig(nz)) * nbg
        dnext = jnp.where(i < nr - 1, dnext, 0.0)
        rid = lax.broadcasted_iota(jnp.int32, cu.shape, 0)
        up1 = jnp.where(rid == R - 1, dnext[0:1, :], pltpu.roll(dconv, R - 1, 0))
        up2 = jnp.where(rid == R - 2, dnext[0:1, :], jnp.where(rid == R - 1, dnext[1:2, :], pltpu.roll(dconv, R - 2, 0)))
        dcu = w[2:3, :] * dconv + w[1:2, :] * up1 + w[0:1, :] * up2
        o_ref[:, 0:CW] = (dcu * u).astype(o_ref.dtype)
        o_ref[:, CW:2 * CW] = d_b.astype(o_ref.dtype)
        o_ref[:, 2 * CW:3 * CW] = (dcu * cg).astype(o_ref.dtype)
        o_ref[:, 3 * CW:4 * CW] = d_z.astype(o_ref.dtype)

        @pl.when(i == 0)
        def _():
            dw_ref[...] = jnp.zeros(dw_ref.shape, F32)

        dw_ref[0:1, :] += jnp.sum(dconv * sh2, axis=0, keepdims=True)
        dw_ref[1:2, :] += jnp.sum(dconv * sh1, axis=0, keepdims=True)
        dw_ref[2:3, :] += jnp.sum(dconv * cu, axis=0, keepdims=True)

    next_spec = pl.BlockSpec((HALO, 4 * CW), lambda i: (jnp.minimum((i + 1) * rh, N // HALO - 1), conv_blk))
    dan_spec = pl.BlockSpec((HALO, CW), lambda i: (jnp.minimum((i + 1) * rh, N // HALO - 1), 0))
    dproj2, g_convw = pl.pallas_call(
        conv_bwd_kern, name="conv_bwd", grid=(nr,),
        in_specs=[any_spec, rowb(4 * CW, conv_blk), prev_spec, next_spec, rowb(CW, 0), dan_spec, fullb((8, CW))],
        out_specs=[rowb(4 * CW, conv_blk), fullb((8, CW))],
        out_shape=[jax.ShapeDtypeStruct((N, c.P), _BF), jax.ShapeDtypeStruct((8, CW), F32)],
        input_output_aliases={0: 0}, compiler_params=_cparams(("arbitrary",)),
    )(dproj1, proj, proj, proj, d_aconv, d_aconv, convw8)

    def mem_bwd_kern(dp_any, qz_ref, da_ref, k_ref, v_ref, g_ref, o_ref, dk_ref, dv_ref, dg_ref):
        @pl.when(pl.program_id(0) == 0)
        def _():
            dk_ref[...] = jnp.zeros(dk_ref.shape, F32)
            dv_ref[...] = jnp.zeros(dv_ref.shape, F32)
            dg_ref[...] = jnp.zeros(dg_ref.shape, F32)

        for hh in range(MH):
            sl, qh, r, qn, p, y, z = mem_head(qz_ref, k_ref, v_ref, g_ref, hh)
            da = da_ref[:, sl].astype(F32)
            sg = _sig(z)
            dyh = da * (z * sg)
            o_ref[:, MW + hh * MHD:MW + (hh + 1) * MHD] = (da * y * (sg * (1.0 + z * (1.0 - sg)))).astype(o_ref.dtype)
            dyb_h = dyh.astype(_BF)
            dpm = _dot_nt(dyb_h, v_ref[:, sl])
            ds = (p * (dpm - jnp.sum(dpm * p, axis=1, keepdims=True)) * c.mscale).astype(_BF)
            dqn = _dot(ds, k_ref[:, sl])
            dk_ref[:, sl] += _dot_tn(ds, qn)
            dv_ref[:, sl] += _dot_tn(p.astype(_BF), dyb_h)
            dq, dgp = _rms_bwd(qh, r, g_ref[...], dqn, MHD)
            o_ref[:, sl] = dq.astype(o_ref.dtype)
            dg_ref[...] += jnp.sum(dgp, axis=0, keepdims=True)

    dproj3, d_memk, d_memv, g_mem_qn = pl.pallas_call(
        mem_bwd_kern, name="mem_bwd", grid=(N // RM,),
        in_specs=[any_spec, rowb(2 * MW, mem_blk, RM), rowb(MW, 0, RM), fullb((M, MW)), fullb((M, MW)), fullb((1, MHD))],
        out_specs=[rowb(2 * MW, mem_blk, RM), fullb((M, MW)), fullb((M, MW)), fullb((1, MHD))],
        out_shape=[jax.ShapeDtypeStruct((N, c.P), _BF), jax.ShapeDtypeStruct((M, MW), F32),
                   jax.ShapeDtypeStruct((M, MW), F32), jax.ShapeDtypeStruct((1, MHD), F32)],
        input_output_aliases={0: 0}, compiler_params=_cparams(("arbitrary",)),
    )(dproj2, proj, d_amem, mem_k, mem_v, mem_qn_g)

    def memk_bwd_kern(kv_ref, dk_ref, dv_ref, g_ref, o_ref, dg_ref):
        dg = jnp.zeros((1, MHD), F32)
        for hh in range(MH):
            sl = slice(hh * MHD, (hh + 1) * MHD)
            kh, r = _rms(kv_ref[:, sl], MHD)
            dkr, dgp = _rms_bwd(kh, r, g_ref[...], dk_ref[:, sl], MHD)
            o_ref[:, sl] = dkr.astype(o_ref.dtype)
            dg += jnp.sum(dgp, axis=0, keepdims=True)
        o_ref[:, MW:] = dv_ref[...].astype(o_ref.dtype)
        dg_ref[...] = dg

    d_kvm, g_mem_kn = pl.pallas_call(
        memk_bwd_kern, name="memk_bwd", grid=(1,),
        in_specs=[fullb((M, 2 * MW)), fullb((M, MW)), fullb((M, MW)), fullb((1, MHD))],
        out_specs=[fullb((M, 2 * MW)), fullb((1, MHD))],
        out_shape=[jax.ShapeDtypeStruct((M, 2 * MW), _BF), jax.ShapeDtypeStruct((1, MHD), F32)],
        compiler_params=_cparams(("arbitrary",)),
    )(kvm, d_memk, d_memv, mem_kn_g)
    G['w_mem_kv'] = _mm(memn, d_kvm, ta=True, name="mm_dwmkv", out_dtype=_BF)
    d_memn = _mm(d_kvm, wmkv, tb=True, name="mm_dmemn", out_dtype=F32)

    def memnorm_bwd_kern(x_ref, d_ref, dg_ref):
        xh, _ = _rms(x_ref[...], D)
        dg_ref[...] = jnp.sum(d_ref[...] * xh, axis=0, keepdims=True)

    g_mem_norm = pl.pallas_call(
        memnorm_bwd_kern, name="memnorm_bwd", grid=(1,),
        in_specs=[fullb((M, D)), fullb((M, D))], out_specs=fullb((1, D)),
        out_shape=jax.ShapeDtypeStruct((1, D), F32), compiler_params=_cparams(("arbitrary",)),
    )(memx, d_memn)

    def gate_bwd_kern(dp_any, da_ref, y_ref, z_ref, dy_ref, dz_ref):
        z = z_ref[...].astype(F32)
        da = da_ref[...].astype(F32)
        sg = _sig(z)
        dy_ref[...] = (da * (z * sg)).astype(dy_ref.dtype)
        dz_ref[...] = (da * y_ref[...].astype(F32) * (sg * (1.0 + z * (1.0 - sg)))).astype(dz_ref.dtype)

    d_mlay, dproj4 = pl.pallas_call(
        gate_bwd_kern, name="gate_mla_bwd", grid=(nr2,),
        in_specs=[any_spec, rowb2(HV, 0), rowb2(HV, 0), rowb2(HV, mz_blk)],
        out_specs=[rowb2(HV, 0), rowb2(HV, mz_blk)],
        out_shape=[jax.ShapeDtypeStruct((N, HV), _BF), jax.ShapeDtypeStruct((N, c.P), _BF)],
        input_output_aliases={0: 1}, compiler_params=_cparams(("parallel",)),
    )(dproj3, d_amla, mla_y, proj)

    def attn_bwd_kern(q_ref, k_ref, v_ref, o_ref, do_ref, lse_ref, dq_ref, dk_ref, dv_ref, dq_sc, dl_sc, dk_sc, dv_sc):
        i = pl.program_id(1)
        delta = jnp.sum(do_ref[...].astype(F32) * o_ref[...].astype(F32), axis=1, keepdims=True)
        dl_sc[...] = jnp.broadcast_to(delta, dl_sc.shape)
        dq_sc[...] = jnp.zeros(dq_sc.shape, F32)

        def step(t, diag):
            rows = pl.ds(pl.multiple_of(t * B, B), B)
            rs = live_rows(diag)
            q, do = q_ref[0, rs, :], do_ref[rs, :]
            k = k_ref[0, rows, :]
            s = _dot_nt(q, k)
            if diag is not None:
                s = diag_mask(s)
            dpm = _dot_nt(do, v_ref[rows, :])
            lse_t, dl = lse_ref[0, rs, :], dl_sc[rs, :]
            ps, dss = [], []
            for cb in range(B // LANES):
                cols = slice(cb * LANES, (cb + 1) * LANES)
                p_cb = jnp.exp2(s[:, cols] - lse_t)
                ps.append(p_cb.astype(_BF))
                dss.append((p_cb * (dpm[:, cols] - dl)).astype(_BF))
            p, ds = jnp.concatenate(ps, axis=1), jnp.concatenate(dss, axis=1)
            dvp = _dot_tn(p, do)
            dkp = _dot_tn(ds, q)
            if diag is not None:
                dk_sc[rows, :] = dkp
                dv_sc[rows, :] = dvp
            else:
                dk_sc[rows, :] += dkp
                dv_sc[rows, :] += dvp
            dq_sc[rs, :] += _dot(ds, k)

        def run(first, count, n_diag):
            for u in range(count):
                step(first + u, u - (count - n_diag) if u >= count - n_diag else None)

        key_block_plan(i, run, ATT_UNROLL_BWD)
        dq_ref[0] = dq_sc[...].astype(dq_ref.dtype)

        @pl.when(i == nq - 1)
        def _():
            dk_ref[0] = dk_sc[...].astype(dk_ref.dtype)
            dv_ref[0] = dv_sc[...].astype(dv_ref.dtype)

    d_qcat, d_kcat, d_v = pl.pallas_call(
        attn_bwd_kern, name="attn_bwd", grid=(H, nq),
        in_specs=[q_blk, k_head, v_head, o_blk, o_blk, lse_blk],
        out_specs=[pl.BlockSpec((1, BQ, HW), lambda hh, i: (hh, i, 0)),
                   pl.BlockSpec((1, N, HW), lambda hh, i: (hh, 0, 0)),
                   pl.BlockSpec((1, N, LANES), lambda hh, i: (hh, 0, 0))],
        out_shape=[jax.ShapeDtypeStruct((H, N, HW), _BF), jax.ShapeDtypeStruct((H, N, HW), _BF),
                   jax.ShapeDtypeStruct((H, N, LANES), _BF)],
        scratch_shapes=[pltpu.VMEM((BQ, HW), F32), pltpu.VMEM((BQ, LANES), F32), pltpu.VMEM((N, HW), F32),
                        pltpu.VMEM((N, LANES), F32)],
        compiler_params=_cparams(("parallel", "arbitrary")),
    )(q_cat, k_cat, kv, mla_y, d_mlay, lse)

    def q_prep_bwd_kern(q_ref, dq_ref, t_ref, gn_ref, gr_ref, o_ref, dgn_ref, dgr_ref):
        @pl.when(jnp.logical_and(pl.program_id(0) == 0, pl.program_id(1) == 0))
        def _():
            dgn_ref[...] = jnp.zeros(dgn_ref.shape, F32)
            dgr_ref[...] = jnp.zeros(dgr_ref.shape, F32)

        for g in range(HG):
            blk = q_ref[:, g * HW:(g + 1) * HW].astype(F32)
            d = dq_ref[g].astype(F32)
            nh, rn = _rms128(blk[:, :LANES], c.NOPE)
            rhat, rr = _rms128(blk[:, LANES:], c.ROPE)
            dn, dgn = _rms128_bwd(nh, rn, gn_ref[...], d[:, :LANES], c.NOPE)
            drot = _rope_t(d[:, LANES:], t_ref[0], t_ref[1], t_ref[2])
            dr, dgr = _rms128_bwd(rhat, rr, gr_ref[...], drot, c.ROPE)
            o_ref[:, g * HW:(g + 1) * HW] = jnp.concatenate([dn, dr], axis=1).astype(o_ref.dtype)
            dgn_ref[...] += jnp.sum(dgn, axis=0, keepdims=True)
            dgr_ref[...] += jnp.sum(dgr, axis=0, keepdims=True)

    d_qp, g_qn_nope, g_qn_rope = pl.pallas_call(
        q_prep_bwd_kern, name="q_prep_bwd", grid=(nrp, nhg),
        in_specs=[heads_in, heads_out, tabs_of(RP), fullb((1, LANES)), fullb((1, LANES))],
        out_specs=[heads_in, fullb((1, LANES)), fullb((1, LANES))],
        out_shape=[jax.ShapeDtypeStruct((N, H * HW), _BF), jax.ShapeDtypeStruct((1, LANES), F32),
                   jax.ShapeDtypeStruct((1, LANES), F32)],
        compiler_params=_cparams(("arbitrary", "arbitrary")),
    )(q_p, d_qcat, tabs, g_qn * c.scale, g_qr * c.scale)
    g_qn_nope, g_qn_rope = g_qn_nope * c.scale, g_qn_rope * c.scale

    def k_prep_bwd_kern(kv_ref, dk_ref, dv_ref, gn_ref, o_ref, dkr_ref, dgn_ref):
        hg = pl.program_id(1)

        @pl.when(jnp.logical_and(pl.program_id(0) == 0, hg == 0))
        def _():
            dgn_ref[...] = jnp.zeros(dgn_ref.shape, F32)

        @pl.when(hg == 0)
        def _():
            dkr_ref[...] = jnp.zeros(dkr_ref.shape, F32)

        dkr = jnp.zeros((RP, LANES), F32)
        for g in range(HG):
            dk = dk_ref[g].astype(F32) * (1.0 / LOG2E)
            kn, r = _rms128(kv_ref[:, g * HW:g * HW + LANES].astype(F32), c.NOPE)
            dkn, dgn = _rms128_bwd(kn, r, gn_ref[...], dk[:, :LANES], c.NOPE)
            o_ref[:, g * HW:(g + 1) * HW] = jnp.concatenate([dkn.astype(o_ref.dtype), dv_ref[g]], axis=1)
            dgn_ref[...] += jnp.sum(dgn, axis=0, keepdims=True)
            dkr += dk[:, LANES:]
        dkr_ref[...] += dkr

    d_kv, d_krsum, g_kn_nope = pl.pallas_call(
        k_prep_bwd_kern, name="k_prep_bwd", grid=(nrp, nhg),
        in_specs=[heads_in, heads_out, pl.BlockSpec((HG, RP, LANES), lambda i, hg: (hg, i, 0)), fullb((1, LANES))],
        out_specs=[heads_in, pl.BlockSpec((RP, LANES), lambda i, hg: (i, 0)), fullb((1, LANES))],
        out_shape=[jax.ShapeDtypeStruct((N, H * HW), _BF), jax.ShapeDtypeStruct((N, LANES), F32),
                   jax.ShapeDtypeStruct((1, LANES), F32)],
        compiler_params=_cparams(("arbitrary", "arbitrary")),
    )(kv, d_kcat, d_v, g_kn)

    def krope_bwd_kern(p_ref, d_ref, t_ref, g_ref, o_ref, dg_ref):
        @pl.when(pl.program_id(0) == 0)
        def _():
            dg_ref[...] = jnp.zeros(dg_ref.shape, F32)

        xh, r = _rms128(p_ref[...].astype(F32), c.ROPE)
        drot = _rope_t(d_ref[...], t_ref[0], t_ref[1], t_ref[2])
        dx, dg = _rms128_bwd(xh, r, g_ref[...], drot, c.ROPE)
        o_ref[...] = dx.astype(o_ref.dtype)
        dg_ref[...] += jnp.sum(dg, axis=0, keepdims=True)

    d_kr, g_kn_rope = pl.pallas_call(
        krope_bwd_kern, name="krope_bwd", grid=(nr2,),
        in_specs=[rowb2(LANES, 0), rowb2(LANES, 0), tabs_of(R2), fullb((1, LANES))],
        out_specs=[rowb2(LANES, 0), fullb((1, LANES))],
        out_shape=[jax.ShapeDtypeStruct((N, LANES), _BF), jax.ShapeDtypeStruct((1, LANES), F32)],
        compiler_params=_cparams(("arbitrary",)),
    )(kr_raw, d_krsum, tabs, g_kr)
    dproj5 = dproj4

    g_wuq_p = _mm(cqn, d_qp, ta=True, name="mm_dwuq", out_dtype=_BF, bk=4096)
    G['w_uq'] = g_wuq_p.reshape(QL, H, HW)[:, :, :c.NOPE + c.ROPE].reshape(QL, H * (c.NOPE + c.ROPE))
    d_cqn = _mm(d_qp, wuq_pT, name="mm_dcqn", out_dtype=F32, bk=4096)
    G['w_ukv'] = _mm(ckvn, d_kv, ta=True, name="mm_dwukv", out_dtype=_BF, bk=4096)
    d_ckvn = _mm(d_kv, wukvT, name="mm_dckvn", out_dtype=F32, bk=4096)

    def to_blocks(n, g):
        if n in COL_SHARDED:
            return jnp.transpose(g.reshape(g.shape[0], NDEV, -1), (1, 0, 2))
        return g.reshape(NDEV, -1, g.shape[1])

    def landing(b):
        return lax.empty(b.shape, b.dtype)

    early = [n for n in BIG if n != 'w_in']
    blocks_e = [to_blocks(n, G[n]) for n in early]
    xe = _split_start(blocks_e, [landing(b) for b in blocks_e], False, "xchg_early_start")
    gq_after = mla_q_norm_g + xe[4][0:1, 0:1]

    def lora_bwd_kern(dp_any, p_ref, dq_ref, dkv_ref, gq_ref, gkv_ref, o_ref, dgq_ref, dgkv_ref):
        @pl.when(pl.program_id(0) == 0)
        def _():
            dgq_ref[...] = jnp.zeros(dgq_ref.shape, F32)
            dgkv_ref[...] = jnp.zeros(dgkv_ref.shape, F32)

        blk = p_ref[...].astype(F32)
        qh, rq = _rms(blk[:, :QL], QL)
        kh, rk = _rms(blk[:, QL:], KVL)
        dq, dgq = _rms_bwd(qh, rq, gq_ref[...], dq_ref[...], QL)
        dk, dgk = _rms_bwd(kh, rk, gkv_ref[...], dkv_ref[...], KVL)
        o_ref[:, :QL] = dq.astype(o_ref.dtype)
        o_ref[:, QL:] = dk.astype(o_ref.dtype)
        dgq_ref[...] += jnp.sum(dgq, axis=0, keepdims=True)
        dgkv_ref[...] += jnp.sum(dgk, axis=0, keepdims=True)

    dproj6, g_q_norm, g_kv_norm = pl.pallas_call(
        lora_bwd_kern, name="lora_bwd", grid=(nr2,),
        in_specs=[any_spec, rowb2(QL + KVL, lora_blk), rowb2(QL, 0), rowb2(KVL, 0), fullb((1, QL)), fullb((1, KVL))],
        out_specs=[rowb2(QL + KVL, lora_blk), fullb((1, QL)), fullb((1, KVL))],
        out_shape=[jax.ShapeDtypeStruct((N, c.P), _BF), jax.ShapeDtypeStruct((1, QL), F32),
                   jax.ShapeDtypeStruct((1, KVL), F32)],
        input_output_aliases={0: 0}, compiler_params=_cparams(("arbitrary",)),
    )(dproj5, proj, d_cqn, d_ckvn, gq_after, mla_kv_norm_g)

    g_win_p = _mm(h, dproj6, ta=True, name="mm_dwin", out_dtype=_BF, bk=4096)
    g_wkr = _mm(h, d_kr, ta=True, name="mm_dwkr", out_dtype=_BF, bk=4096)
    blocks_w = [_win_blocks(g_win_p, g_wkr, c)]
    xw = _split_start(blocks_w, [landing(b) for b in blocks_w], False, "xchg_win_start")
    d_h = _mm(dproj6, win_pT, name="mm_dh", out_dtype=F32, bk=3072, after=xw[4], plus=(d_kr, w_krT))

    def final_bwd_kern(x_ref, g_ref, dh_ref, dy_ref, gx_ref, dg_ref):
        @pl.when(pl.program_id(0) == 0)
        def _():
            dg_ref[...] = jnp.zeros(dg_ref.shape, F32)

        xh, r = _rms(x_ref[...], D)
        dx, dg = _rms_bwd(xh, r, g_ref[...], dh_ref[...], D)
        gx_ref[...] = dy_ref[...].astype(F32) + dx
        dg_ref[...] += jnp.sum(dg, axis=0, keepdims=True)

    grad_x, g_norm = pl.pallas_call(
        final_bwd_kern, name="final_bwd", grid=(nr2,),
        in_specs=[rowb2(D, 0), fullb((1, D)), rowb2(D, 0), rowb2(D, 0)],
        out_specs=[rowb2(D, 0), fullb((1, D))],
        out_shape=[jax.ShapeDtypeStruct((N, D), F32), jax.ShapeDtypeStruct((1, D), F32)],
        compiler_params=_cparams(("arbitrary",)),
    )(xs, norm_g, d_h, dyb)

    res = [{}, {}, {}, {}]

    def adam_into(n, parts):
        outs = _adam(parts, W[n][0], Mo[n][0], Vo[n][0], "adam_" + n)
        for k in range(4):
            res[k][n] = outs[k][None]
        return outs[0]

    recv_e = _split_wait(xe, False, grad_x, "xchg_early_wait")
    last = [adam_into(n, parts) for n, parts in zip(early, recv_e)][-1]
    recv_w = _split_wait(xw, False, last, "xchg_win_wait")
    adam_into('w_in', recv_w[0])

    small_g = {'norm_g': g_norm, 'mla_q_norm_g': g_q_norm, 'mla_kv_norm_g': g_kv_norm,
               'mla_qn_nope_g': g_qn_nope, 'mla_qn_rope_g': g_qn_rope[:, :c.ROPE], 'mla_kn_nope_g': g_kn_nope,
               'mla_kn_rope_g': g_kn_rope[:, :c.ROPE], 'mem_norm_g': g_mem_norm, 'mem_qn_g': g_mem_qn,
               'mem_kn_g': g_mem_kn}
    small_part = _pack([small_g[n] for n in SMALL] + [g_convw[0:3, :]], 0)
    small_all = _all_gather([small_part], "ag_small_grads")[0]
    small_shapes = [W[n].shape for n in SMALL]
    pieces = _unpack(small_all, small_shapes + [(3, CW)])
    cw8 = CW // NDEV
    conv_mine = lax.dynamic_slice_in_dim(pieces[-1].reshape(NDEV, 3, NDEV, cw8), me, 1, axis=2)[:, :, 0, :]
    sm_parts = _pack(pieces[:-1] + [conv_mine], 1)
    sm_names = SMALL + ['conv_w']
    sm_shapes = small_shapes + [(3, cw8)]
    w_sm = _pack([W[n] for n in SMALL] + [conv_w[0]], 0)
    m_sm = _pack([Mo[n] for n in SMALL] + [m_conv_w[0]], 0)
    v_sm = _pack([Vo[n] for n in SMALL] + [v_conv_w[0]], 0)
    outs_sm = [_unpack(o, sm_shapes) for o in _adam(sm_parts, w_sm, m_sm, v_sm, "adam_small")]
    for k in range(4):
        for n, a in zip(sm_names, outs_sm[k]):
            res[k][n] = a[None] if n == 'conv_w' else a
    return (loss, grad_x[None], *[res[0][n] for n in WEIGHTS], *[res[1][n] for n in WEIGHTS],
            *[res[2][n] for n in WEIGHTS], *[res[3][n] for n in WEIGHTS])
```

```python
import math

import jax
import jax.numpy as jnp
from jax import lax
from jax.experimental import pallas as pl
from jax.experimental.pallas import tpu as pltpu

F32 = jnp.float32
_BF = jnp.bfloat16
EPS = 1e-6
CHUNK = 64
ROPE_THETA = 10000.0
ADAM_LR, ADAM_B1, ADAM_B2, ADAM_EPS, ADAM_WD, ADAM_STEP = 0.001, 0.9, 0.999, 1e-08, 0.01, 10
NDEV = 8
AXES = ("x", "y", "c")
MESH = pl.DeviceIdType.MESH
LANES = 128
NEG = -1e30
LOG2E = math.log2(math.e)
V7X_VMEM_LIMIT = 56 * 1024 * 1024
PACK_C = 1024
ATT_BLOCK = 512
ATT_UNROLL_FWD = 8
ATT_UNROLL_BWD = 6
ATT_QBLOCKS = 2
ADAM_BLOCK_ELEMS = 256 * 1024

WEIGHTS = ['norm_g', 'w_in', 'conv_w', 'w_conv_out', 'mla_q_norm_g', 'w_uq', 'mla_kv_norm_g', 'w_ukv',
           'mla_qn_nope_g', 'mla_qn_rope_g', 'mla_kn_nope_g', 'mla_kn_rope_g', 'w_mla_out', 'mem_norm_g',
           'w_mem_kv', 'mem_qn_g', 'mem_kn_g', 'w_mem_out', 'w_o']
BIG = ['w_in', 'w_conv_out', 'w_uq', 'w_ukv', 'w_mla_out', 'w_mem_kv', 'w_mem_out', 'w_o']
COL_SHARDED = ('w_in', 'w_conv_out', 'w_uq', 'w_ukv', 'w_mem_out')
SMALL = ['norm_g', 'mla_q_norm_g', 'mla_kv_norm_g', 'mla_qn_nope_g', 'mla_qn_rope_g', 'mla_kn_nope_g',
         'mla_kn_rope_g', 'mem_norm_g', 'mem_qn_g', 'mem_kn_g']


def _tile(dim, target, align):
    if dim <= target:
        return dim
    t = target - target % align
    while t > 0:
        if dim % t == 0:
            return t
        t -= align
    raise ValueError(f"no tile for {dim} {target} {align}")


def _cparams(sem):
    return pltpu.CompilerParams(dimension_semantics=sem, vmem_limit_bytes=V7X_VMEM_LIMIT)


def _sig(x):
    return 1.0 / (1.0 + jnp.exp(-x))


def _rms(x, n):
    r = lax.rsqrt(jnp.sum(x * x, axis=-1, keepdims=True) * (1.0 / n) + EPS)
    return x * r, r


def _rms_bwd(xhat, r, g, dy, n):
    dxh = dy * g
    dx = r * (dxh - xhat * (jnp.sum(dxh * xhat, axis=-1, keepdims=True) * (1.0 / n)))
    return dx, dy * xhat


def _rowmean128(x, n):
    return jnp.dot(x.astype(_BF), jnp.full((LANES, LANES), 1.0 / n, _BF), preferred_element_type=F32)


def _rms128(x, n):
    r = lax.rsqrt(_rowmean128(x * x, n) + EPS)
    return x * r, r


def _rms128_bwd(xhat, r, g, dy, n):
    dxh = dy * g
    dx = r * (dxh - xhat * _rowmean128(dxh * xhat, n))
    return dx, dy * xhat


def _rope(x, cosp, sina, sinb):
    return x * cosp + pltpu.roll(x, 96, 1) * sina + pltpu.roll(x, 32, 1) * sinb


def _rope_t(d, cosp, sina, sinb):
    return d * cosp + pltpu.roll(d * sina, 32, 1) + pltpu.roll(d * sinb, 96, 1)


def _dot_nt(a, b):
    return lax.dot_general(a, b, (((1,), (1,)), ((), ())), preferred_element_type=F32)


def _dot_tn(a, b):
    return lax.dot_general(a, b, (((0,), (0,)), ((), ())), preferred_element_type=F32)


def _dot(a, b):
    return jnp.dot(a, b, preferred_element_type=F32)


def _all_gather(shards, name):
    na = len(shards)
    nc = 9
    halves = [s.shape[0] // 32 * 16 if s.shape[0] >= 256 else None for s in shards]

    def body(*refs):
        x_refs, out_refs = refs[:na], refs[na:2 * na]
        send_sems, recv_sems, local_sems = refs[2 * na:]
        x, y, c = lax.axis_index("x"), lax.axis_index("y"), lax.axis_index("c")
        me, sib = (x, y, c), (x, y, 1 - c)
        px, py, pd = (1 - x, y, c), (x, 1 - y, c), (1 - x, 1 - y, c)

        def other_core(p):
            return (p[0], p[1], 1 - p[2])

        def rows(a, blk, part=None):
            r = out_refs[a].at[4 * blk[0] + 2 * blk[1] + blk[2]]
            if part is None or halves[a] is None:
                return r
            rest = shards[a].shape[0] - halves[a]
            return r.at[pl.ds(0, halves[a])] if part == 0 else r.at[pl.ds(halves[a], rest)]

        def copy(a, k, blk, to, part=None, src=None):
            dst = rows(a, blk, part)
            return pltpu.make_async_remote_copy(
                src_ref=dst if src is None else src, dst_ref=dst,
                send_sem=send_sems.at[nc * a + k], recv_sem=recv_sems.at[nc * a + k],
                device_id=to, device_id_type=MESH)

        mine = [pltpu.make_async_copy(x_refs[a], rows(a, me), local_sems.at[a]) for a in range(na)]
        for cp in mine:
            cp.start()
        started = []
        for a in range(na):
            started += [copy(a, 0, me, px, src=x_refs[a]), copy(a, 1, me, py, src=x_refs[a]),
                        copy(a, 2, me, sib, src=x_refs[a])]
        for cp in started:
            cp.start()

        def forward(cp):
            cp.start()
            started.append(cp)

        for a in range(na):
            copy(a, 0, px, me).wait_recv()
            forward(copy(a, 3, px, py, part=0))
            forward(copy(a, 4, px, sib))
        for a in range(na):
            copy(a, 1, py, me).wait_recv()
            if halves[a] is not None:
                forward(copy(a, 5, py, px, part=1))
            forward(copy(a, 6, py, sib))
        for a in range(na):
            copy(a, 3, pd, me, part=0).wait_recv()
            forward(copy(a, 7, pd, sib, part=0))
            if halves[a] is not None:
                copy(a, 5, pd, me, part=1).wait_recv()
                forward(copy(a, 8, pd, sib, part=1))
        for a in range(na):
            copy(a, 2, sib, me).wait_recv()
            copy(a, 4, other_core(px), me).wait_recv()
            copy(a, 6, other_core(py), me).wait_recv()
            copy(a, 7, other_core(pd), me, part=0).wait_recv()
            if halves[a] is not None:
                copy(a, 8, other_core(pd), me, part=1).wait_recv()
        for cp in started:
            cp.wait_send()
        for cp in mine:
            cp.wait()

    any_spec = pl.BlockSpec(memory_space=pl.ANY)
    return pl.pallas_call(
        body, name=name,
        out_shape=[jax.ShapeDtypeStruct((NDEV,) + s.shape, s.dtype) for s in shards],
        in_specs=[any_spec] * na, out_specs=[any_spec] * na,
        scratch_shapes=[pltpu.SemaphoreType.DMA((nc * na,)), pltpu.SemaphoreType.DMA((nc * na,)),
                        pltpu.SemaphoreType.DMA((na,))],
    )(*shards)


_HBM = pl.BlockSpec(memory_space=pltpu.HBM)
_SEM = pl.BlockSpec(memory_space=pltpu.SEMAPHORE)
_EFFECT = pltpu.SideEffectType.DATAFLOW_SIDE_EFFECTING


def _split_copy(a, k, src_refs, land_refs, send_sems, recv_sems, gather, receive_side):
    x, y, c = lax.axis_index("x"), lax.axis_index("y"), lax.axis_index("c")
    me = 4 * x + 2 * y + c
    tx, ty, tc = x ^ ((k + 1) >> 2 & 1), y ^ ((k + 1) >> 1 & 1), c ^ ((k + 1) & 1)
    peer = 4 * tx + 2 * ty + tc
    return pltpu.make_async_remote_copy(
        src_ref=src_refs[a] if gather else src_refs[a].at[peer],
        dst_ref=land_refs[a].at[peer if receive_side else me],
        send_sem=send_sems.at[7 * a + k], recv_sem=recv_sems.at[7 * a + k],
        device_id=(tx, ty, tc), device_id_type=MESH)


def _own_copy(a, na, src_refs, land_refs, send_sems, gather):
    me = 4 * lax.axis_index("x") + 2 * lax.axis_index("y") + lax.axis_index("c")
    return pltpu.make_async_copy(src_refs[a] if gather else src_refs[a].at[me], land_refs[a].at[me],
                                 send_sems.at[7 * na + a])


def _split_start(srcs, lands, gather, name, after=None):
    na = len(srcs)
    extra = [] if after is None else [after]

    def body(*refs):
        src_refs, land_refs = refs[:na], refs[na:2 * na]
        send_sems, recv_sems = refs[2 * na + len(extra)], refs[2 * na + len(extra) + 1]
        token = refs[-1]
        for k in range(7):
            for a in range(na):
                _split_copy(a, k, src_refs, land_refs, send_sems, recv_sems, gather, False).start()
        for a in range(na):
            _own_copy(a, na, src_refs, land_refs, send_sems, gather).start()
        token[...] = jnp.zeros_like(token)

    hbm = [pltpu.HBM(b.shape, b.dtype) for b in list(srcs) + list(lands)]
    outs = pl.pallas_call(
        body, name=name,
        out_shape=(pltpu.SemaphoreType.DMA((8 * na,)), pltpu.SemaphoreType.DMA((7 * na,)), *hbm,
                   jax.ShapeDtypeStruct((8, LANES), F32)),
        in_specs=[_HBM] * (2 * na) + [pl.BlockSpec(memory_space=pl.ANY)] * len(extra),
        out_specs=(_SEM, _SEM, *[_HBM] * (2 * na), pl.BlockSpec(memory_space=pltpu.VMEM)),
        input_output_aliases={j: 2 + j for j in range(2 * na)},
        compiler_params=pltpu.CompilerParams(has_side_effects=_EFFECT),
    )(*[pltpu.with_memory_space_constraint(b, pltpu.HBM) for b in srcs],
      *[pltpu.with_memory_space_constraint(l, pltpu.HBM) for l in lands], *extra)
    return outs[0], outs[1], outs[2:2 + na], outs[2 + na:2 + 2 * na], outs[-1]


def _split_wait(started, gather, after, name):
    send_sems, recv_sems, srcs, lands, _ = started
    na = len(srcs)

    def body(*refs):
        src_refs, land_refs = refs[:na], refs[na:2 * na]
        send_s, recv_s = refs[2 * na], refs[2 * na + 1]
        for k in range(7):
            for a in range(na):
                cp = _split_copy(a, k, src_refs, land_refs, send_s, recv_s, gather, True)
                cp.wait_send()
                cp.wait_recv()
        for a in range(na):
            _own_copy(a, na, src_refs, land_refs, send_s, gather).wait()

    hbm = [pltpu.HBM(b.shape, b.dtype) for b in list(srcs) + list(lands)]
    outs = pl.pallas_call(
        body, name=name, out_shape=tuple(hbm),
        in_specs=[_HBM] * (2 * na) + [_SEM, _SEM, pl.BlockSpec(memory_space=pl.ANY)],
        out_specs=tuple([_HBM] * (2 * na)),
        input_output_aliases={j: j for j in range(2 * na)},
        compiler_params=pltpu.CompilerParams(has_side_effects=_EFFECT),
    )(*srcs, *lands, send_sems, recv_sems, after)
    return outs[na:]


def _seg_rows(size):
    rows = -(-size // PACK_C)
    return -(-rows // 16) * 16


def _pack(arrs, lead):
    parts = []
    for a in arrs:
        lshape = a.shape[:lead]
        f = a.reshape(lshape + (-1,)).astype(F32)
        rows = _seg_rows(f.shape[-1])
        f = jnp.pad(f, [(0, 0)] * lead + [(0, rows * PACK_C - f.shape[-1])])
        parts.append(f.reshape(lshape + (rows, PACK_C)))
    return jnp.concatenate(parts, axis=lead)


def _unpack(buf, shapes):
    lshape = buf.shape[:-2]
    out, r = [], 0
    for shp in shapes:
        size = math.prod(shp)
        rows = _seg_rows(size)
        seg = buf[..., r:r + rows, :].reshape(lshape + (rows * PACK_C,))[..., :size]
        out.append(seg.reshape(lshape + tuple(shp)))
        r += rows
    return out


def _mm(a, b, *, name, out_dtype, ta=False, tb=False, bm=1024, bn=1024, bk=2048, after=None, plus=None):
    if ta:
        kdim, m = a.shape
    else:
        m, kdim = a.shape
    n, k2 = b.shape if tb else b.shape[::-1]
    assert kdim == k2 and not (ta and tb), (a.shape, b.shape)
    bm = _tile(m, bm, LANES if ta else 16)
    bn = _tile(n, bn, LANES)
    bk = _tile(kdim, bk, LANES)
    nk = kdim // bk
    n_after = 0 if after is None else 1
    n_plus = 0 if plus is None else 2

    def kern(a_ref, b_ref, *rest):
        plus_refs = rest[n_after:n_after + n_plus]
        o_ref, scratch = rest[n_after + n_plus], rest[n_after + n_plus + 1:]
        part = (_dot_tn if ta else _dot_nt if tb else _dot)(a_ref[...], b_ref[...])

        def first(p):
            return p + _dot(plus_refs[0][...], plus_refs[1][...]) if plus is not None else p

        if nk == 1:
            o_ref[...] = first(part).astype(o_ref.dtype)
        else:
            acc = scratch[0] if scratch else o_ref
            k = pl.program_id(2)

            @pl.when(k == 0)
            def _():
                acc[...] = first(jnp.zeros(acc.shape, F32))

            acc[...] += part
            if scratch:
                @pl.when(k == nk - 1)
                def _():
                    o_ref[...] = acc[...].astype(o_ref.dtype)

    a_spec = pl.BlockSpec((bk, bm), lambda i, j, k: (k, i)) if ta else pl.BlockSpec((bm, bk), lambda i, j, k: (i, k))
    b_spec = pl.BlockSpec((bn, bk), lambda i, j, k: (j, k)) if tb else pl.BlockSpec((bk, bn), lambda i, j, k: (k, j))
    extra_specs, extra_args = [], []
    if after is not None:
        extra_specs.append(pl.BlockSpec(after.shape, lambda i, j, k: (0, 0)))
        extra_args.append(after)
    if plus is not None:
        kk = plus[0].shape[1]
        extra_specs += [pl.BlockSpec((bm, kk), lambda i, j, k: (i, 0)), pl.BlockSpec((kk, bn), lambda i, j, k: (0, j))]
        extra_args += list(plus)
    return pl.pallas_call(
        kern, name=name, grid=(m // bm, n // bn, nk),
        in_specs=[a_spec, b_spec] + extra_specs,
        out_specs=pl.BlockSpec((bm, bn), lambda i, j, k: (i, j)),
        out_shape=jax.ShapeDtypeStruct((m, n), out_dtype),
        scratch_shapes=[pltpu.VMEM((bm, bn), F32)] if nk > 1 and out_dtype != F32 else [],
        compiler_params=_cparams(("parallel", "parallel", "arbitrary")),
    )(a, b, *extra_args)


def _adam(parts, w_a, m_a, v_a, name):
    rows, cols = w_a.shape
    rb = _tile(rows, max(8, ADAM_BLOCK_ELEMS // cols // 8 * 8), 8)
    bc1 = 1.0 - ADAM_B1 ** ADAM_STEP
    bc2 = 1.0 - ADAM_B2 ** ADAM_STEP

    def adam_kern(p_ref, w_ref, m_ref, v_ref, g_ref, d_ref, nm_ref, nv_ref):
        g = p_ref[0].astype(F32)
        for j in range(1, NDEV):
            g = g + p_ref[j].astype(F32)
        m_new = ADAM_B1 * m_ref[...] + (1.0 - ADAM_B1) * g
        v_new = ADAM_B2 * v_ref[...] + (1.0 - ADAM_B2) * (g * g)
        g_ref[...] = g
        nm_ref[...] = m_new
        nv_ref[...] = v_new
        d_ref[...] = -ADAM_LR * ((m_new / bc1) / (jnp.sqrt(v_new / bc2) + ADAM_EPS) + ADAM_WD * w_ref[...])

    blk = pl.BlockSpec((rb, cols), lambda i: (i, 0))
    return pl.pallas_call(
        adam_kern, name=name, grid=(rows // rb,),
        in_specs=[pl.BlockSpec((NDEV, rb, cols), lambda i: (0, i, 0)), blk, blk, blk],
        out_specs=[blk] * 4, out_shape=[jax.ShapeDtypeStruct((rows, cols), F32)] * 4,
        compiler_params=_cparams(("parallel",)),
    )(parts, w_a, m_a, v_a)


class _Cfg:
    pass


def _config(x, conv_w, w_uq, w_ukv, mla_qn_nope_g, mla_qn_rope_g, mem, mem_qn_g, w_mem_out, w_mla_out):
    c = _Cfg()
    c.N, c.D = x.shape[1], x.shape[2]
    c.CW = conv_w.shape[2] * NDEV
    c.QL, c.KVL = w_uq.shape[1], w_ukv.shape[1]
    c.NOPE, c.ROPE = mla_qn_nope_g.shape[1], mla_qn_rope_g.shape[1]
    c.H = w_uq.shape[2] * NDEV // (c.NOPE + c.ROPE)
    c.V = w_ukv.shape[2] * NDEV // c.H - c.NOPE
    assert c.NOPE == LANES and c.V == LANES and c.ROPE == LANES // 2
    c.HW = 2 * LANES
    c.HV = c.H * c.V
    assert w_mla_out.shape[1] * NDEV == c.HV
    c.M = mem.shape[1]
    c.MHD = mem_qn_g.shape[1]
    c.MW = w_mem_out.shape[1]
    c.MH = c.MW // c.MHD
    c.o_conv = 0
    c.o_mz = 4 * c.CW
    c.o_g = c.o_mz + c.HV
    c.o_mem = c.o_g + 3 * c.D
    c.o_lora = c.o_mem + 2 * c.MW
    c.P = c.o_lora + c.QL + c.KVL
    assert c.o_mz % c.HV == 0 and c.o_g % (3 * c.D) == 0 and c.o_mem % (2 * c.MW) == 0
    assert c.o_lora % (c.QL + c.KVL) == 0 and c.QL % LANES == 0 and c.KVL % LANES == 0
    c.IN = 4 * c.CW + c.QL + c.KVL + c.ROPE + c.HV + 2 * c.MW + 3 * c.D
    c.R = _tile(c.N, 256, 16)
    c.RP = _tile(c.N, 1024, 16)
    c.HG = _tile(c.H, 4, 1)
    c.B = _tile(c.N, ATT_BLOCK, CHUNK)
    c.scale = float((c.NOPE + c.ROPE) ** -0.5)
    c.mscale = float(c.MHD ** -0.5)
    return c


def _win_segments(c):
    ref_order = (('conv', 4 * c.CW), ('lora', c.QL + c.KVL), ('kr', c.ROPE), ('mz', c.HV), ('mem', 2 * c.MW), ('g', 3 * c.D))
    mine = {'conv': c.o_conv, 'mz': c.o_mz, 'g': c.o_g, 'mem': c.o_mem, 'lora': c.o_lora, 'kr': 0}
    segs, o = [], 0
    for nm, wd in ref_order:
        segs.append((nm, o, wd, mine[nm]))
        o += wd
    return segs


def _win_split(g_win_t, c):
    n8 = g_win_t.shape[1]

    def rows(a, wd):
        return [g_win_t[j][max(a, j * n8) - j * n8:min(a + wd, (j + 1) * n8) - j * n8]
                for j in range(a // n8, (a + wd - 1) // n8 + 1)]

    segs = {nm: (a, wd) for nm, a, wd, _ in _win_segments(c)}
    main = [p for nm in ('conv', 'mz', 'g', 'mem', 'lora') for p in rows(*segs[nm])]
    kr = jnp.concatenate(rows(*segs['kr']) + [jnp.zeros((LANES - c.ROPE, g_win_t.shape[2]), g_win_t.dtype)], axis=0)
    return jnp.concatenate(main, axis=0), kr


def _win_blocks(g, g_kr, c):
    n8 = c.IN // NDEV
    blocks = []
    for j in range(NDEV):
        lo, hi = j * n8, (j + 1) * n8
        parts = []
        for nm, a, wd, mine in _win_segments(c):
            s, e = max(a, lo), min(a + wd, hi)
            if s < e:
                parts.append((g_kr if nm == 'kr' else g)[:, mine + s - a:mine + e - a])
        blocks.append(jnp.concatenate(parts, axis=1))
    return jnp.stack(blocks, axis=0)


def kernel(x, positions, mem, norm_g, w_in, conv_w, w_conv_out, mla_q_norm_g, w_uq, mla_kv_norm_g, w_ukv, mla_qn_nope_g, mla_qn_rope_g, mla_kn_nope_g, mla_kn_rope_g, w_mla_out, mem_norm_g, w_mem_kv, mem_qn_g, mem_kn_g, w_mem_out, w_o, loss_target, m_norm_g, m_w_in, m_conv_w, m_w_conv_out, m_mla_q_norm_g, m_w_uq, m_mla_kv_norm_g, m_w_ukv, m_mla_qn_nope_g, m_mla_qn_rope_g, m_mla_kn_nope_g, m_mla_kn_rope_g, m_w_mla_out, m_mem_norm_g, m_w_mem_kv, m_mem_qn_g, m_mem_kn_g, m_w_mem_out, m_w_o, v_norm_g, v_w_in, v_conv_w, v_w_conv_out, v_mla_q_norm_g, v_w_uq, v_mla_kv_norm_g, v_w_ukv, v_mla_qn_nope_g, v_mla_qn_rope_g, v_mla_kn_nope_g, v_mla_kn_rope_g, v_w_mla_out, v_mem_norm_g, v_w_mem_kv, v_mem_qn_g, v_mem_kn_g, v_w_mem_out, v_w_o):
    args = dict(locals())
    W = {n: args[n] for n in WEIGHTS}
    Mo = {n: args['m_' + n] for n in WEIGHTS}
    Vo = {n: args['v_' + n] for n in WEIGHTS}
    c = _config(x, conv_w, w_uq, w_ukv, mla_qn_nope_g, mla_qn_rope_g, mem, mem_qn_g, w_mem_out, w_mla_out)
    N, D, R, B, H = c.N, c.D, c.R, c.B, c.H
    assert x.shape[0] == 1
    xs = x[0]
    tgt = loss_target[0]
    memx = mem[0]
    me = 4 * lax.axis_index("x") + 2 * lax.axis_index("y") + lax.axis_index("c")
    nr = N // R

    g_win, g_taps = _all_gather([W['w_in'][0].astype(_BF).T, conv_w[0]], "ag_w_in")
    rest = [n for n in BIG if n != 'w_in']
    shards_r = [W[n][0].astype(_BF).T if n in COL_SHARDED else W[n][0].astype(_BF) for n in rest]
    lands_r = [lax.empty((NDEV,) + s.shape, s.dtype) for s in shards_r]
    ag_rest = _split_start(shards_r, lands_r, True, "ag_rest_start", after=g_win)
    win_pT, w_krT = _win_split(g_win, c)
    convw = jnp.transpose(g_taps, (1, 0, 2)).reshape(3, c.CW)
    convw8 = jnp.pad(convw, ((0, 5), (0, 0)))

    def rowb(width, cidx, rows=R):
        return pl.BlockSpec((rows, width), lambda i, _c=cidx: (i, _c))

    R2 = _tile(N, 512, 16)
    nr2 = N // R2
    RM = _tile(N, 1024, 16)

    def rowb2(width, cidx):
        return rowb(width, cidx, R2)

    def fullb(shape):
        nd = len(shape)
        return pl.BlockSpec(shape, lambda *_: (0,) * nd)

    def pad_lanes(g, w=LANES):
        return jnp.pad(g, ((0, 0), (0, w - g.shape[1])))

    def tabs_of(rows):
        return pl.BlockSpec((3, rows, LANES), lambda i, *_: (0, i, 0))

    half = c.ROPE // 2
    inv_freq = jnp.power(ROPE_THETA, -jnp.arange(half, dtype=F32) / half)
    invf = jnp.concatenate([inv_freq, inv_freq, jnp.zeros((LANES - c.ROPE,), F32)])[None, :]
    pos_col = positions[0].astype(F32).reshape(N, 1)

    def rope_tab_kern(pos_ref, invf_ref, o_ref):
        ang = pos_ref[...] * invf_ref[...]
        co, si = jnp.cos(ang), jnp.sin(ang)
        lane = lax.broadcasted_iota(jnp.int32, ang.shape, 1)
        o_ref[0] = jnp.where(lane < c.ROPE, co, 0.0)
        o_ref[1] = jnp.where(lane < half, -si, 0.0)
        o_ref[2] = jnp.where(jnp.logical_and(lane >= half, lane < c.ROPE), si, 0.0)

    tabs = pl.pallas_call(
        rope_tab_kern, name="rope_tab", grid=(nr,),
        in_specs=[pl.BlockSpec((R, 1), lambda i: (i, 0)), fullb((1, LANES))],
        out_specs=tabs_of(R),
        out_shape=jax.ShapeDtypeStruct((3, N, LANES), F32),
        compiler_params=_cparams(("parallel",)),
    )(pos_col, invf)

    def make_rms_kern():
        def rms_fwd_kern(x_ref, g_ref, o_ref):
            xh, _ = _rms(x_ref[...].astype(F32), x_ref.shape[-1])
            o_ref[...] = (xh * g_ref[...]).astype(o_ref.dtype)
        return rms_fwd_kern

    h = pl.pallas_call(
        make_rms_kern(), name="rms_x", grid=(nr2,),
        in_specs=[rowb2(D, 0), fullb((1, D))], out_specs=rowb2(D, 0),
        out_shape=jax.ShapeDtypeStruct((N, D), _BF), compiler_params=_cparams(("parallel",)),
    )(xs, norm_g)

    proj = _mm(h, win_pT, tb=True, name="mm_proj", out_dtype=_BF, after=ag_rest[4])
    kr_raw = _mm(h, w_krT, tb=True, name="mm_kr", out_dtype=_BF)

    Wf = {n: g.reshape(-1, g.shape[2]) for n, g in zip(rest, _split_wait(ag_rest, True, proj, "ag_rest_wait"))}
    wuqT = Wf['w_uq'].reshape(H, c.NOPE + c.ROPE, c.QL)
    wuq_pT = jnp.pad(wuqT, ((0, 0), (0, c.HW - c.NOPE - c.ROPE), (0, 0))).reshape(H * c.HW, c.QL)
    wukvT = Wf['w_ukv']
    wcoT, wmo, wmkv, wmemoT, wo = Wf['w_conv_out'], Wf['w_mla_out'], Wf['w_mem_kv'], Wf['w_mem_out'], Wf['w_o']

    CW = c.CW
    conv_blk = c.o_conv // (4 * CW)
    HALO = 16
    rh = R // HALO

    def conv_parts(blk):
        blk = blk.astype(F32)
        return blk[:, 0:CW], blk[:, CW:2 * CW], blk[:, 2 * CW:3 * CW], blk[:, 3 * CW:4 * CW]

    def shifted(cu, prev, i):
        prev = jnp.where(i > 0, prev, 0.0)
        rid = lax.broadcasted_iota(jnp.int32, cu.shape, 0)
        last, last2 = prev[HALO - 1:HALO, :], prev[HALO - 2:HALO - 1, :]
        sh1 = jnp.where(rid == 0, last, pltpu.roll(cu, 1, 0))
        sh2 = jnp.where(rid == 0, last2, jnp.where(rid == 1, last, pltpu.roll(cu, 2, 0)))
        return sh1, sh2

    def conv_fwd_kern(p_ref, prev_ref, w_ref, o_ref):
        i = pl.program_id(0)
        cg, bg, u, z = conv_parts(p_ref[...])
        pc, _, pu, _ = conv_parts(prev_ref[...])
        cu = cg * u
        sh1, sh2 = shifted(cu, pc * pu, i)
        w = w_ref[...]
        conv = w[0:1, :] * sh2 + w[1:2, :] * sh1 + w[2:3, :] * cu
        o_ref[...] = (bg * conv * (z * _sig(z))).astype(o_ref.dtype)

    prev_spec = pl.BlockSpec((HALO, 4 * CW), lambda i: (jnp.maximum(i * rh - 1, 0), conv_blk))
    a_conv = pl.pallas_call(
        conv_fwd_kern, name="conv_fwd", grid=(nr,),
        in_specs=[rowb(4 * CW, conv_blk), prev_spec, fullb((8, CW))],
        out_specs=rowb(CW, 0), out_shape=jax.ShapeDtypeStruct((N, CW), _BF),
        compiler_params=_cparams(("parallel",)),
    )(proj, proj, convw8)
    o_conv = _mm(a_conv, wcoT, tb=True, name="mm_oconv", out_dtype=_BF)

    QL, KVL, HW = c.QL, c.KVL, c.HW
    lora_blk = c.o_lora // (QL + KVL)

    def lora_fwd_kern(p_ref, gq_ref, gkv_ref, q_ref, kv_ref):
        blk = p_ref[...].astype(F32)
        qh, _ = _rms(blk[:, :QL], QL)
        kh, _ = _rms(blk[:, QL:], KVL)
        q_ref[...] = (qh * gq_ref[...]).astype(q_ref.dtype)
        kv_ref[...] = (kh * gkv_ref[...]).astype(kv_ref.dtype)

    cqn, ckvn = pl.pallas_call(
        lora_fwd_kern, name="lora_fwd", grid=(nr2,),
        in_specs=[rowb2(QL + KVL, lora_blk), fullb((1, QL)), fullb((1, KVL))],
        out_specs=[rowb2(QL, 0), rowb2(KVL, 0)],
        out_shape=[jax.ShapeDtypeStruct((N, QL), _BF), jax.ShapeDtypeStruct((N, KVL), _BF)],
        compiler_params=_cparams(("parallel",)),
    )(proj, mla_q_norm_g, mla_kv_norm_g)
    q_p = _mm(cqn, wuq_pT, tb=True, name="mm_q", out_dtype=_BF, bn=2048)
    kv = _mm(ckvn, wukvT, tb=True, name="mm_kv", out_dtype=_BF, bn=2048)

    g_qn, g_qr = mla_qn_nope_g, pad_lanes(mla_qn_rope_g)
    g_kn, g_kr = mla_kn_nope_g, pad_lanes(mla_kn_rope_g)

    def krope_fwd_kern(p_ref, t_ref, g_ref, o_ref):
        xh, _ = _rms128(p_ref[...].astype(F32), c.ROPE)
        o_ref[...] = _rope(xh * g_ref[...], t_ref[0], t_ref[1], t_ref[2]).astype(o_ref.dtype)

    k_rope = pl.pallas_call(
        krope_fwd_kern, name="krope_fwd", grid=(nr2,),
        in_specs=[rowb2(LANES, 0), tabs_of(R2), fullb((1, LANES))],
        out_specs=rowb2(LANES, 0), out_shape=jax.ShapeDtypeStruct((N, LANES), _BF),
        compiler_params=_cparams(("parallel",)),
    )(kr_raw, tabs, g_kr)

    RP, HG = c.RP, c.HG
    nrp, nhg = N // RP, H // HG
    heads_in = pl.BlockSpec((RP, HG * HW), lambda i, hg: (i, hg))
    heads_out = pl.BlockSpec((HG, RP, HW), lambda i, hg: (hg, i, 0))

    def q_prep_kern(q_ref, t_ref, gn_ref, gr_ref, o_ref):
        for g in range(HG):
            blk = q_ref[:, g * HW:(g + 1) * HW].astype(F32)
            nh, _ = _rms128(blk[:, :LANES], c.NOPE)
            rhat, _ = _rms128(blk[:, LANES:], c.ROPE)
            rot = _rope(rhat * gr_ref[...], t_ref[0], t_ref[1], t_ref[2])
            o_ref[g] = jnp.concatenate([nh * gn_ref[...], rot], axis=1).astype(o_ref.dtype)

    q_cat = pl.pallas_call(
        q_prep_kern, name="q_prep", grid=(nrp, nhg),
        in_specs=[heads_in, tabs_of(RP), fullb((1, LANES)), fullb((1, LANES))],
        out_specs=heads_out, out_shape=jax.ShapeDtypeStruct((H, N, HW), _BF),
        compiler_params=_cparams(("parallel", "parallel")),
    )(q_p, tabs, g_qn * (c.scale * LOG2E), g_qr * (c.scale * LOG2E))

    def k_prep_kern(kv_ref, kr_ref, gn_ref, o_ref):
        for g in range(HG):
            kn, _ = _rms128(kv_ref[:, g * HW:g * HW + LANES].astype(F32), c.NOPE)
            o_ref[g] = jnp.concatenate([(kn * gn_ref[...]).astype(o_ref.dtype), kr_ref[...]], axis=1)

    k_cat = pl.pallas_call(
        k_prep_kern, name="k_prep", grid=(nrp, nhg),
        in_specs=[heads_in, pl.BlockSpec((RP, LANES), lambda i, hg: (i, 0)), fullb((1, LANES))],
        out_specs=heads_out, out_shape=jax.ShapeDtypeStruct((H, N, HW), _BF),
        compiler_params=_cparams(("parallel", "parallel")),
    )(kv, k_rope, g_kn)

    QB = ATT_QBLOCKS if N % (ATT_QBLOCKS * B) == 0 else 1
    BQ = QB * B
    nq = N // BQ
    assert CHUNK & (CHUNK - 1) == 0 and B % CHUNK == 0 and ATT_UNROLL_FWD % QB == 0 and ATT_UNROLL_BWD % QB == 0

    def live_rows(diag):
        return slice(0 if diag is None else diag * B, BQ)

    def diag_mask(s):
        row = lax.broadcasted_iota(jnp.int32, s.shape, 0)
        col = lax.broadcasted_iota(jnp.int32, s.shape, 1)
        shift = CHUNK.bit_length() - 1
        allowed = jnp.right_shift(col, shift) <= jnp.right_shift(row, shift)
        return jnp.where(allowed, s, NEG)

    k_head = pl.BlockSpec((1, N, HW), lambda hh, i: (hh, 0, 0))
    v_head = pl.BlockSpec((N, LANES), lambda hh, i: (0, 2 * hh + 1))
    q_blk = pl.BlockSpec((1, BQ, HW), lambda hh, i: (hh, i, 0))
    o_blk = pl.BlockSpec((BQ, LANES), lambda hh, i: (i, hh))
    lse_blk = pl.BlockSpec((1, BQ, LANES), lambda hh, i: (hh, i, 0))

    def key_block_plan(i, run, unroll):
        def unrolled(u, carry):
            run(unroll * u, unroll, 0)
            return carry

        n_full = QB * i
        lax.fori_loop(0, n_full // unroll, unrolled, 0)
        for rem in range(0, unroll, QB):
            @pl.when(n_full % unroll == rem)
            def _(rem=rem):
                run(n_full - rem, rem + QB, QB)

    def attn_fwd_kern(q_ref, k_ref, v_ref, o_ref, lse_ref, m_sc, acc_sc, s_sc):
        i = pl.program_id(1)
        m_sc[...] = jnp.full(m_sc.shape, NEG, F32)
        acc_sc[...] = jnp.zeros(acc_sc.shape, F32)

        def rows_of(t):
            return pl.ds(pl.multiple_of(t * B, B), B)

        def scores(t, slot, diag):
            rs = live_rows(diag)
            s_sc[slot, rs, :] = _dot_nt(q_ref[0, rs, :], k_ref[0, rows_of(t), :])

        def softmax_pv(t, slot, diag):
            rs = live_rows(diag)
            s = s_sc[slot, rs, :]
            if diag is not None:
                s = diag_mask(s)
            mt = s[:, 0:LANES]
            for cb in range(1, B // LANES):
                mt = jnp.maximum(mt, s[:, cb * LANES:(cb + 1) * LANES])
            m_prev = m_sc[rs, :]
            m_new = jnp.maximum(m_prev, jnp.max(mt, axis=1, keepdims=True))
            alpha = jnp.exp2(m_prev - m_new)
            p = jnp.concatenate([jnp.exp2(s[:, cb * LANES:(cb + 1) * LANES] - m_new).astype(_BF)
                                 for cb in range(B // LANES)], axis=1)
            v_ones = jnp.concatenate([v_ref[rows_of(t), :], jnp.ones((B, LANES), _BF)], axis=1)
            acc_sc[rs, :] = jnp.concatenate([alpha, alpha], axis=1) * acc_sc[rs, :] + _dot(p, v_ones)
            m_sc[rs, :] = m_new

        scores(0, 0, None)

        def run(first, count, n_diag):
            def diag_of(u):
                return u - (count - n_diag) if count - n_diag <= u < count else None

            for u in range(count):
                if u + 1 < count or n_diag == 0:
                    scores(first + u + 1, (u + 1) % 2, diag_of(u + 1))
                softmax_pv(first + u, u % 2, diag_of(u))

        key_block_plan(i, run, ATT_UNROLL_FWD)
        acc = acc_sc[...]
        o_ref[...] = (acc[:, :LANES] / acc[:, LANES:]).astype(o_ref.dtype)
        lse_ref[0] = m_sc[...] + jnp.log2(acc[:, LANES:])

    mla_y, lse = pl.pallas_call(
        attn_fwd_kern, name="attn_fwd", grid=(H, nq),
        in_specs=[q_blk, k_head, v_head], out_specs=[o_blk, lse_blk],
        out_shape=[jax.ShapeDtypeStruct((N, c.HV), _BF), jax.ShapeDtypeStruct((H, N, LANES), F32)],
        scratch_shapes=[pltpu.VMEM((BQ, LANES), F32), pltpu.VMEM((BQ, HW), F32), pltpu.VMEM((2, BQ, B), F32)],
        compiler_params=_cparams(("parallel", "arbitrary")),
    )(q_cat, k_cat, kv)

    HV = c.HV
    mz_blk = c.o_mz // HV

    def gate_fwd_kern(y_ref, z_ref, o_ref):
        z = z_ref[...].astype(F32)
        o_ref[...] = (y_ref[...].astype(F32) * (z * _sig(z))).astype(o_ref.dtype)

    a_mla = pl.pallas_call(
        gate_fwd_kern, name="gate_mla", grid=(nr2,),
        in_specs=[rowb2(HV, 0), rowb2(HV, mz_blk)], out_specs=rowb2(HV, 0),
        out_shape=jax.ShapeDtypeStruct((N, HV), _BF), compiler_params=_cparams(("parallel",)),
    )(mla_y, proj)
    o_mla = _mm(a_mla, wmo, name="mm_omla", out_dtype=_BF)

    M, MW, MH, MHD = c.M, c.MW, c.MH, c.MHD
    memn = pl.pallas_call(
        make_rms_kern(), name="rms_mem",
        grid=(1,), in_specs=[fullb((M, D)), fullb((1, D))], out_specs=fullb((M, D)),
        out_shape=jax.ShapeDtypeStruct((M, D), _BF), compiler_params=_cparams(("arbitrary",)),
    )(memx, mem_norm_g)
    kvm = _mm(memn, wmkv, name="mm_memkv", out_dtype=F32)

    def memk_fwd_kern(kv_ref, g_ref, k_ref, v_ref):
        for hh in range(MH):
            kh, _ = _rms(kv_ref[:, hh * MHD:(hh + 1) * MHD], MHD)
            k_ref[:, hh * MHD:(hh + 1) * MHD] = (kh * g_ref[...]).astype(k_ref.dtype)
        v_ref[...] = kv_ref[:, MW:].astype(v_ref.dtype)

    mem_k, mem_v = pl.pallas_call(
        memk_fwd_kern, name="memk_fwd", grid=(1,),
        in_specs=[fullb((M, 2 * MW)), fullb((1, MHD))], out_specs=[fullb((M, MW)), fullb((M, MW))],
        out_shape=[jax.ShapeDtypeStruct((M, MW), _BF)] * 2, compiler_params=_cparams(("arbitrary",)),
    )(kvm, mem_kn_g)

    mem_blk = c.o_mem // (2 * MW)

    def mem_head(qz_ref, k_ref, v_ref, g_ref, hh):
        sl = slice(hh * MHD, (hh + 1) * MHD)
        qh, r = _rms(qz_ref[:, sl].astype(F32), MHD)
        qn = (qh * g_ref[...]).astype(_BF)
        s = _dot_nt(qn, k_ref[:, sl]) * c.mscale
        e = jnp.exp(s - jnp.max(s, axis=1, keepdims=True))
        p = e / jnp.sum(e, axis=1, keepdims=True)
        y = _dot(p.astype(_BF), v_ref[:, sl])
        z = qz_ref[:, MW + hh * MHD:MW + (hh + 1) * MHD].astype(F32)
        return sl, qh, r, qn, p, y, z

    def mem_fwd_kern(qz_ref, k_ref, v_ref, g_ref, o_ref):
        for hh in range(MH):
            sl, _, _, _, _, y, z = mem_head(qz_ref, k_ref, v_ref, g_ref, hh)
            o_ref[:, sl] = (y * (z * _sig(z))).astype(o_ref.dtype)

    a_mem = pl.pallas_call(
        mem_fwd_kern, name="mem_fwd", grid=(N // RM,),
        in_specs=[rowb(2 * MW, mem_blk, RM), fullb((M, MW)), fullb((M, MW)), fullb((1, MHD))],
        out_specs=rowb(MW, 0, RM), out_shape=jax.ShapeDtypeStruct((N, MW), _BF),
        compiler_params=_cparams(("parallel",)),
    )(proj, mem_k, mem_v, mem_qn_g)
    o_mem = _mm(a_mem, wmemoT, tb=True, name="mm_omem", out_dtype=_BF)

    g_blk = c.o_g // (3 * D)

    def merge_fwd_kern(g_ref, oc_ref, om_ref, ome_ref, o_ref):
        g = g_ref[...].astype(F32)
        acc = _sig(g[:, :D]) * oc_ref[...].astype(F32)
        acc += _sig(g[:, D:2 * D]) * om_ref[...].astype(F32)
        acc += _sig(g[:, 2 * D:]) * ome_ref[...].astype(F32)
        o_ref[...] = acc.astype(o_ref.dtype)

    merged = pl.pallas_call(
        merge_fwd_kern, name="merge_fwd", grid=(nr2,),
        in_specs=[rowb2(3 * D, g_blk), rowb2(D, 0), rowb2(D, 0), rowb2(D, 0)], out_specs=rowb2(D, 0),
        out_shape=jax.ShapeDtypeStruct((N, D), _BF), compiler_params=_cparams(("parallel",)),
    )(proj, o_conv, o_mla, o_mem)
    obm, obn = _tile(N, 1024, 16), _tile(D, 1024, LANES)

    def out_loss_kern(a_ref, w_ref, x_ref, t_ref, dyb_ref, l_ref):
        e = x_ref[...] + _dot(a_ref[...], w_ref[...]) - t_ref[...]
        dyb_ref[...] = (e * (1.0 / D)).astype(dyb_ref.dtype)
        row = lax.broadcasted_iota(jnp.int32, l_ref.shape, 0)
        l_ref[...] = jnp.where(row == 0, jnp.sum(e * e, axis=0, keepdims=True), 0.0)

    oblk = pl.BlockSpec((obm, obn), lambda i, j: (i, j))
    dyb, lpart = pl.pallas_call(
        out_loss_kern, name="mm_out_loss", grid=(N // obm, D // obn),
        in_specs=[pl.BlockSpec((obm, D), lambda i, j: (i, 0)), pl.BlockSpec((D, obn), lambda i, j: (0, j)), oblk, oblk],
        out_specs=[oblk, pl.BlockSpec((8, obn), lambda i, j: (i, j))],
        out_shape=[jax.ShapeDtypeStruct((N, D), _BF), jax.ShapeDtypeStruct((8 * (N // obm), D), F32)],
        compiler_params=_cparams(("parallel", "parallel")),
    )(merged, wo, xs, tgt)
    loss = lax.psum(jnp.sum(lpart) * (0.5 / D), AXES)

    G = {}
    d_merged = _mm(dyb, wo, tb=True, name="mm_dmerged", out_dtype=_BF)
    G['w_o'] = _mm(merged, dyb, ta=True, name="mm_dwo", out_dtype=_BF, bk=4096)

    dproj0 = lax.empty((N, c.P), _BF)
    any_spec = pl.BlockSpec(memory_space=pl.ANY)

    def merge_bwd_kern(dp_any, g_ref, dm_ref, oc_ref, om_ref, ome_ref, dg_ref, doc_ref, dom_ref, dome_ref):
        g = g_ref[...].astype(F32)
        dm = dm_ref[...].astype(F32)
        for idx, (o_in, d_out) in enumerate(((oc_ref, doc_ref), (om_ref, dom_ref), (ome_ref, dome_ref))):
            sg = _sig(g[:, idx * D:(idx + 1) * D])
            d_out[...] = (sg * dm).astype(d_out.dtype)
            dg_ref[:, idx * D:(idx + 1) * D] = (dm * o_in[...].astype(F32) * sg * (1.0 - sg)).astype(dg_ref.dtype)

    dproj1, d_oconv, d_omla, d_omem = pl.pallas_call(
        merge_bwd_kern, name="merge_bwd", grid=(nr,),
        in_specs=[any_spec, rowb(3 * D, g_blk), rowb(D, 0), rowb(D, 0), rowb(D, 0), rowb(D, 0)],
        out_specs=[rowb(3 * D, g_blk), rowb(D, 0), rowb(D, 0), rowb(D, 0)],
        out_shape=[jax.ShapeDtypeStruct((N, c.P), _BF)] + [jax.ShapeDtypeStruct((N, D), _BF)] * 3,
        input_output_aliases={0: 0}, compiler_params=_cparams(("parallel",)),
    )(dproj0, proj, d_merged, o_conv, o_mla, o_mem)

    G['w_conv_out'] = _mm(a_conv, d_oconv, ta=True, name="mm_dwco", out_dtype=_BF, bk=4096)
    d_aconv = _mm(d_oconv, wcoT, name="mm_daconv", out_dtype=_BF)
    G['w_mla_out'] = _mm(a_mla, d_omla, ta=True, name="mm_dwmo", out_dtype=_BF, bk=4096)
    d_amla = _mm(d_omla, wmo, tb=True, name="mm_damla", out_dtype=_BF)
    G['w_mem_out'] = _mm(a_mem, d_omem, ta=True, name="mm_dwmemo", out_dtype=_BF, bk=4096)
    d_amem = _mm(d_omem, wmemoT, name="mm_damem", out_dtype=_BF)

    def conv_bwd_kern(dp_any, p_ref, prev_ref, next_ref, da_ref, dan_ref, w_ref, o_ref, dw_ref):
        i = pl.program_id(0)
        cg, bg, u, z = conv_parts(p_ref[...])
        pc, _, pu, _ = conv_parts(prev_ref[...])
        _, nbg, _, nz = conv_parts(next_ref[...])
        cu = cg * u
        sh1, sh2 = shifted(cu, pc * pu, i)
        w = w_ref[...]
        conv = w[0:1, :] * sh2 + w[1:2, :] * sh1 + w[2:3, :] * cu
        sg = _sig(z)
        sz = z * sg
        da = da_ref[...].astype(F32)
        dcy = da * sz
        d_z = da * (bg * conv) * (sg * (1.0 + z * (1.0 - sg)))
        d_b = dcy * conv
        dconv = dcy * bg
        dnext = dan_ref[...].astype(F32) * (nz * _sig(nz)) * nbg
        dnext = jnp.where(i < nr - 1, dnext, 0.0)
        rid = lax.broadcasted_iota(jnp.int32, cu.shape, 0)
        up1 = jnp.where(rid == R - 1, dnext[0:1, :], pltpu.roll(dconv, R - 1, 0))
        up2 = jnp.where(rid == R - 2, dnext[0:1, :], jnp.where(rid == R - 1, dnext[1:2, :], pltpu.roll(dconv, R - 2, 0)))
        dcu = w[2:3, :] * dconv + w[1:2, :] * up1 + w[0:1, :] * up2
        o_ref[:, 0:CW] = (dcu * u).astype(o_ref.dtype)
        o_ref[:, CW:2 * CW] = d_b.astype(o_ref.dtype)
        o_ref[:, 2 * CW:3 * CW] = (dcu * cg).astype(o_ref.dtype)
        o_ref[:, 3 * CW:4 * CW] = d_z.astype(o_ref.dtype)

        @pl.when(i == 0)
        def _():
            dw_ref[...] = jnp.zeros(dw_ref.shape, F32)

        dw_ref[0:1, :] += jnp.sum(dconv * sh2, axis=0, keepdims=True)
        dw_ref[1:2, :] += jnp.sum(dconv * sh1, axis=0, keepdims=True)
        dw_ref[2:3, :] += jnp.sum(dconv * cu, axis=0, keepdims=True)

    next_spec = pl.BlockSpec((HALO, 4 * CW), lambda i: (jnp.minimum((i + 1) * rh, N // HALO - 1), conv_blk))
    dan_spec = pl.BlockSpec((HALO, CW), lambda i: (jnp.minimum((i + 1) * rh, N // HALO - 1), 0))
    dproj2, g_convw = pl.pallas_call(
        conv_bwd_kern, name="conv_bwd", grid=(nr,),
        in_specs=[any_spec, rowb(4 * CW, conv_blk), prev_spec, next_spec, rowb(CW, 0), dan_spec, fullb((8, CW))],
        out_specs=[rowb(4 * CW, conv_blk), fullb((8, CW))],
        out_shape=[jax.ShapeDtypeStruct((N, c.P), _BF), jax.ShapeDtypeStruct((8, CW), F32)],
        input_output_aliases={0: 0}, compiler_params=_cparams(("arbitrary",)),
    )(dproj1, proj, proj, proj, d_aconv, d_aconv, convw8)

    def mem_bwd_kern(dp_any, qz_ref, da_ref, k_ref, v_ref, g_ref, o_ref, dk_ref, dv_ref, dg_ref):
        @pl.when(pl.program_id(0) == 0)
        def _():
            dk_ref[...] = jnp.zeros(dk_ref.shape, F32)
            dv_ref[...] = jnp.zeros(dv_ref.shape, F32)
            dg_ref[...] = jnp.zeros(dg_ref.shape, F32)

        for hh in range(MH):
            sl, qh, r, qn, p, y, z = mem_head(qz_ref, k_ref, v_ref, g_ref, hh)
            da = da_ref[:, sl].astype(F32)
            sg = _sig(z)
            dyh = da * (z * sg)
            o_ref[:, MW + hh * MHD:MW + (hh + 1) * MHD] = (da * y * (sg * (1.0 + z * (1.0 - sg)))).astype(o_ref.dtype)
            dyb_h = dyh.astype(_BF)
            dpm = _dot_nt(dyb_h, v_ref[:, sl])
            ds = (p * (dpm - jnp.sum(dpm * p, axis=1, keepdims=True)) * c.mscale).astype(_BF)
            dqn = _dot(ds, k_ref[:, sl])
            dk_ref[:, sl] += _dot_tn(ds, qn)
            dv_ref[:, sl] += _dot_tn(p.astype(_BF), dyb_h)
            dq, dgp = _rms_bwd(qh, r, g_ref[...], dqn, MHD)
            o_ref[:, sl] = dq.astype(o_ref.dtype)
            dg_ref[...] += jnp.sum(dgp, axis=0, keepdims=True)

    dproj3, d_memk, d_memv, g_mem_qn = pl.pallas_call(
        mem_bwd_kern, name="mem_bwd", grid=(N // RM,),
        in_specs=[any_spec, rowb(2 * MW, mem_blk, RM), rowb(MW, 0, RM), fullb((M, MW)), fullb((M, MW)), fullb((1, MHD))],
        out_specs=[rowb(2 * MW, mem_blk, RM), fullb((M, MW)), fullb((M, MW)), fullb((1, MHD))],
        out_shape=[jax.ShapeDtypeStruct((N, c.P), _BF), jax.ShapeDtypeStruct((M, MW), F32),
                   jax.ShapeDtypeStruct((M, MW), F32), jax.ShapeDtypeStruct((1, MHD), F32)],
        input_output_aliases={0: 0}, compiler_params=_cparams(("arbitrary",)),
    )(dproj2, proj, d_amem, mem_k, mem_v, mem_qn_g)

    def memk_bwd_kern(kv_ref, dk_ref, dv_ref, g_ref, o_ref, dg_ref):
        dg = jnp.zeros((1, MHD), F32)
        for hh in range(MH):
            sl = slice(hh * MHD, (hh + 1) * MHD)
            kh, r = _rms(kv_ref[:, sl], MHD)
            dkr, dgp = _rms_bwd(kh, r, g_ref[...], dk_ref[:, sl], MHD)
            o_ref[:, sl] = dkr.astype(o_ref.dtype)
            dg += jnp.sum(dgp, axis=0, keepdims=True)
        o_ref[:, MW:] = dv_ref[...].astype(o_ref.dtype)
        dg_ref[...] = dg

    d_kvm, g_mem_kn = pl.pallas_call(
        memk_bwd_kern, name="memk_bwd", grid=(1,),
        in_specs=[fullb((M, 2 * MW)), fullb((M, MW)), fullb((M, MW)), fullb((1, MHD))],
        out_specs=[fullb((M, 2 * MW)), fullb((1, MHD))],
        out_shape=[jax.ShapeDtypeStruct((M, 2 * MW), _BF), jax.ShapeDtypeStruct((1, MHD), F32)],
        compiler_params=_cparams(("arbitrary",)),
    )(kvm, d_memk, d_memv, mem_kn_g)
    G['w_mem_kv'] = _mm(memn, d_kvm, ta=True, name="mm_dwmkv", out_dtype=_BF)
    d_memn = _mm(d_kvm, wmkv, tb=True, name="mm_dmemn", out_dtype=F32)

    def memnorm_bwd_kern(x_ref, d_ref, dg_ref):
        xh, _ = _rms(x_ref[...], D)
        dg_ref[...] = jnp.sum(d_ref[...] * xh, axis=0, keepdims=True)

    g_mem_norm = pl.pallas_call(
        memnorm_bwd_kern, name="memnorm_bwd", grid=(1,),
        in_specs=[fullb((M, D)), fullb((M, D))], out_specs=fullb((1, D)),
        out_shape=jax.ShapeDtypeStruct((1, D), F32), compiler_params=_cparams(("arbitrary",)),
    )(memx, d_memn)

    def gate_bwd_kern(dp_any, da_ref, y_ref, z_ref, dy_ref, dz_ref):
        z = z_ref[...].astype(F32)
        da = da_ref[...].astype(F32)
        sg = _sig(z)
        dy_ref[...] = (da * (z * sg)).astype(dy_ref.dtype)
        dz_ref[...] = (da * y_ref[...].astype(F32) * (sg * (1.0 + z * (1.0 - sg)))).astype(dz_ref.dtype)

    d_mlay, dproj4 = pl.pallas_call(
        gate_bwd_kern, name="gate_mla_bwd", grid=(nr2,),
        in_specs=[any_spec, rowb2(HV, 0), rowb2(HV, 0), rowb2(HV, mz_blk)],
        out_specs=[rowb2(HV, 0), rowb2(HV, mz_blk)],
        out_shape=[jax.ShapeDtypeStruct((N, HV), _BF), jax.ShapeDtypeStruct((N, c.P), _BF)],
        input_output_aliases={0: 1}, compiler_params=_cparams(("parallel",)),
    )(dproj3, d_amla, mla_y, proj)

    def attn_bwd_kern(q_ref, k_ref, v_ref, o_ref, do_ref, lse_ref, dq_ref, dk_ref, dv_ref, dq_sc, dl_sc, dk_sc, dv_sc):
        i = pl.program_id(1)
        delta = jnp.sum(do_ref[...].astype(F32) * o_ref[...].astype(F32), axis=1, keepdims=True)
        dl_sc[...] = jnp.broadcast_to(delta, dl_sc.shape)
        dq_sc[...] = jnp.zeros(dq_sc.shape, F32)

        def step(t, diag):
            rows = pl.ds(pl.multiple_of(t * B, B), B)
            rs = live_rows(diag)
            q, do = q_ref[0, rs, :], do_ref[rs, :]
            k = k_ref[0, rows, :]
            s = _dot_nt(q, k)
            if diag is not None:
                s = diag_mask(s)
            dpm = _dot_nt(do, v_ref[rows, :])
            lse_t, dl = lse_ref[0, rs, :], dl_sc[rs, :]
            ps, dss = [], []
            for cb in range(B // LANES):
                cols = slice(cb * LANES, (cb + 1) * LANES)
                p_cb = jnp.exp2(s[:, cols] - lse_t)
                ps.append(p_cb.astype(_BF))
                dss.append((p_cb * (dpm[:, cols] - dl)).astype(_BF))
            p, ds = jnp.concatenate(ps, axis=1), jnp.concatenate(dss, axis=1)
            dvp = _dot_tn(p, do)
            dkp = _dot_tn(ds, q)
            if diag is not None:
                dk_sc[rows, :] = dkp
                dv_sc[rows, :] = dvp
            else:
                dk_sc[rows, :] += dkp
                dv_sc[rows, :] += dvp
            dq_sc[rs, :] += _dot(ds, k)

        def run(first, count, n_diag):
            for u in range(count):
                step(first + u, u - (count - n_diag) if u >= count - n_diag else None)

        key_block_plan(i, run, ATT_UNROLL_BWD)
        dq_ref[0] = dq_sc[...].astype(dq_ref.dtype)

        @pl.when(i == nq - 1)
        def _():
            dk_ref[0] = dk_sc[...].astype(dk_ref.dtype)
            dv_ref[0] = dv_sc[...].astype(dv_ref.dtype)

    d_qcat, d_kcat, d_v = pl.pallas_call(
        attn_bwd_kern, name="attn_bwd", grid=(H, nq),
        in_specs=[q_blk, k_head, v_head, o_blk, o_blk, lse_blk],
        out_specs=[pl.BlockSpec((1, BQ, HW), lambda hh, i: (hh, i, 0)),
                   pl.BlockSpec((1, N, HW), lambda hh, i: (hh, 0, 0)),
                   pl.BlockSpec((1, N, LANES), lambda hh, i: (hh, 0, 0))],
        out_shape=[jax.ShapeDtypeStruct((H, N, HW), _BF), jax.ShapeDtypeStruct((H, N, HW), _BF),
                   jax.ShapeDtypeStruct((H, N, LANES), _BF)],
        scratch_shapes=[pltpu.VMEM((BQ, HW), F32), pltpu.VMEM((BQ, LANES), F32), pltpu.VMEM((N, HW), F32),
                        pltpu.VMEM((N, LANES), F32)],
        compiler_params=_cparams(("parallel", "arbitrary")),
    )(q_cat, k_cat, kv, mla_y, d_mlay, lse)

    def q_prep_bwd_kern(q_ref, dq_ref, t_ref, gn_ref, gr_ref, o_ref, dgn_ref, dgr_ref):
        @pl.when(jnp.logical_and(pl.program_id(0) == 0, pl.program_id(1) == 0))
        def _():
            dgn_ref[...] = jnp.zeros(dgn_ref.shape, F32)
            dgr_ref[...] = jnp.zeros(dgr_ref.shape, F32)

        for g in range(HG):
            blk = q_ref[:, g * HW:(g + 1) * HW].astype(F32)
            d = dq_ref[g].astype(F32)
            nh, rn = _rms128(blk[:, :LANES], c.NOPE)
            rhat, rr = _rms128(blk[:, LANES:], c.ROPE)
            dn, dgn = _rms128_bwd(nh, rn, gn_ref[...], d[:, :LANES], c.NOPE)
            drot = _rope_t(d[:, LANES:], t_ref[0], t_ref[1], t_ref[2])
            dr, dgr = _rms128_bwd(rhat, rr, gr_ref[...], drot, c.ROPE)
            o_ref[:, g * HW:(g + 1) * HW] = jnp.concatenate([dn, dr], axis=1).astype(o_ref.dtype)
            dgn_ref[...] += jnp.sum(dgn, axis=0, keepdims=True)
            dgr_ref[...] += jnp.sum(dgr, axis=0, keepdims=True)

    d_qp, g_qn_nope, g_qn_rope = pl.pallas_call(
        q_prep_bwd_kern, name="q_prep_bwd", grid=(nrp, nhg),
        in_specs=[heads_in, heads_out, tabs_of(RP), fullb((1, LANES)), fullb((1, LANES))],
        out_specs=[heads_in, fullb((1, LANES)), fullb((1, LANES))],
        out_shape=[jax.ShapeDtypeStruct((N, H * HW), _BF), jax.ShapeDtypeStruct((1, LANES), F32),
                   jax.ShapeDtypeStruct((1, LANES), F32)],
        compiler_params=_cparams(("arbitrary", "arbitrary")),
    )(q_p, d_qcat, tabs, g_qn * c.scale, g_qr * c.scale)
    g_qn_nope, g_qn_rope = g_qn_nope * c.scale, g_qn_rope * c.scale

    def k_prep_bwd_kern(kv_ref, dk_ref, dv_ref, gn_ref, o_ref, dkr_ref, dgn_ref):
        hg = pl.program_id(1)

        @pl.when(jnp.logical_and(pl.program_id(0) == 0, hg == 0))
        def _():
            dgn_ref[...] = jnp.zeros(dgn_ref.shape, F32)

        @pl.when(hg == 0)
        def _():
            dkr_ref[...] = jnp.zeros(dkr_ref.shape, F32)

        dkr = jnp.zeros((RP, LANES), F32)
        for g in range(HG):
            dk = dk_ref[g].astype(F32) * (1.0 / LOG2E)
            kn, r = _rms128(kv_ref[:, g * HW:g * HW + LANES].astype(F32), c.NOPE)
            dkn, dgn = _rms128_bwd(kn, r, gn_ref[...], dk[:, :LANES], c.NOPE)
            o_ref[:, g * HW:(g + 1) * HW] = jnp.concatenate([dkn.astype(o_ref.dtype), dv_ref[g]], axis=1)
            dgn_ref[...] += jnp.sum(dgn, axis=0, keepdims=True)
            dkr += dk[:, LANES:]
        dkr_ref[...] += dkr

    d_kv, d_krsum, g_kn_nope = pl.pallas_call(
        k_prep_bwd_kern, name="k_prep_bwd", grid=(nrp, nhg),
        in_specs=[heads_in, heads_out, pl.BlockSpec((HG, RP, LANES), lambda i, hg: (hg, i, 0)), fullb((1, LANES))],
        out_specs=[heads_in, pl.BlockSpec((RP, LANES), lambda i, hg: (i, 0)), fullb((1, LANES))],
        out_shape=[jax.ShapeDtypeStruct((N, H * HW), _BF), jax.ShapeDtypeStruct((N, LANES), F32),
                   jax.ShapeDtypeStruct((1, LANES), F32)],
        compiler_params=_cparams(("arbitrary", "arbitrary")),
    )(kv, d_kcat, d_v, g_kn)

    def krope_bwd_kern(p_ref, d_ref, t_ref, g_ref, o_ref, dg_ref):
        @pl.when(pl.program_id(0) == 0)
        def _():
            dg_ref[...] = jnp.zeros(dg_ref.shape, F32)

        xh, r = _rms128(p_ref[...].astype(F32), c.ROPE)
        drot = _rope_t(d_ref[...], t_ref[0], t_ref[1], t_ref[2])
        dx, dg = _rms128_bwd(xh, r, g_ref[...], drot, c.ROPE)
        o_ref[...] = dx.astype(o_ref.dtype)
        dg_ref[...] += jnp.sum(dg, axis=0, keepdims=True)

    d_kr, g_kn_rope = pl.pallas_call(
        krope_bwd_kern, name="krope_bwd", grid=(nr2,),
        in_specs=[rowb2(LANES, 0), rowb2(LANES, 0), tabs_of(R2), fullb((1, LANES))],
        out_specs=[rowb2(LANES, 0), fullb((1, LANES))],
        out_shape=[jax.ShapeDtypeStruct((N, LANES), _BF), jax.ShapeDtypeStruct((1, LANES), F32)],
        compiler_params=_cparams(("arbitrary",)),
    )(kr_raw, d_krsum, tabs, g_kr)
    dproj5 = dproj4

    g_wuq_p = _mm(cqn, d_qp, ta=True, name="mm_dwuq", out_dtype=_BF, bk=4096)
    G['w_uq'] = g_wuq_p.reshape(QL, H, HW)[:, :, :c.NOPE + c.ROPE].reshape(QL, H * (c.NOPE + c.ROPE))
    d_cqn = _mm(d_qp, wuq_pT, name="mm_dcqn", out_dtype=F32, bk=4096)
    G['w_ukv'] = _mm(ckvn, d_kv, ta=True, name="mm_dwukv", out_dtype=_BF, bk=4096)
    d_ckvn = _mm(d_kv, wukvT, name="mm_dckvn", out_dtype=F32, bk=4096)

    def to_blocks(n, g):
        if n in COL_SHARDED:
            return jnp.transpose(g.reshape(g.shape[0], NDEV, -1), (1, 0, 2))
        return g.reshape(NDEV, -1, g.shape[1])

    def landing(b):
        return lax.empty(b.shape, b.dtype)

    early = [n for n in BIG if n != 'w_in']
    blocks_e = [to_blocks(n, G[n]) for n in early]
    xe = _split_start(blocks_e, [landing(b) for b in blocks_e], False, "xchg_early_start")
    gq_after = mla_q_norm_g + xe[4][0:1, 0:1]

    def lora_bwd_kern(dp_any, p_ref, dq_ref, dkv_ref, gq_ref, gkv_ref, o_ref, dgq_ref, dgkv_ref):
        @pl.when(pl.program_id(0) == 0)
        def _():
            dgq_ref[...] = jnp.zeros(dgq_ref.shape, F32)
            dgkv_ref[...] = jnp.zeros(dgkv_ref.shape, F32)

        blk = p_ref[...].astype(F32)
        qh, rq = _rms(blk[:, :QL], QL)
        kh, rk = _rms(blk[:, QL:], KVL)
        dq, dgq = _rms_bwd(qh, rq, gq_ref[...], dq_ref[...], QL)
        dk, dgk = _rms_bwd(kh, rk, gkv_ref[...], dkv_ref[...], KVL)
        o_ref[:, :QL] = dq.astype(o_ref.dtype)
        o_ref[:, QL:] = dk.astype(o_ref.dtype)
        dgq_ref[...] += jnp.sum(dgq, axis=0, keepdims=True)
        dgkv_ref[...] += jnp.sum(dgk, axis=0, keepdims=True)

    dproj6, g_q_norm, g_kv_norm = pl.pallas_call(
        lora_bwd_kern, name="lora_bwd", grid=(nr2,),
        in_specs=[any_spec, rowb2(QL + KVL, lora_blk), rowb2(QL, 0), rowb2(KVL, 0), fullb((1, QL)), fullb((1, KVL))],
        out_specs=[rowb2(QL + KVL, lora_blk), fullb((1, QL)), fullb((1, KVL))],
        out_shape=[jax.ShapeDtypeStruct((N, c.P), _BF), jax.ShapeDtypeStruct((1, QL), F32),
                   jax.ShapeDtypeStruct((1, KVL), F32)],
        input_output_aliases={0: 0}, compiler_params=_cparams(("arbitrary",)),
    )(dproj5, proj, d_cqn, d_ckvn, gq_after, mla_kv_norm_g)

    g_win_p = _mm(h, dproj6, ta=True, name="mm_dwin", out_dtype=_BF, bk=4096)
    g_wkr = _mm(h, d_kr, ta=True, name="mm_dwkr", out_dtype=_BF, bk=4096)
    blocks_w = [_win_blocks(g_win_p, g_wkr, c)]
    xw = _split_start(blocks_w, [landing(b) for b in blocks_w], False, "xchg_win_start")
    d_h = _mm(dproj6, win_pT, name="mm_dh", out_dtype=F32, bk=3072, after=xw[4], plus=(d_kr, w_krT))

    def final_bwd_kern(x_ref, g_ref, dh_ref, dy_ref, gx_ref, dg_ref):
        @pl.when(pl.program_id(0) == 0)
        def _():
            dg_ref[...] = jnp.zeros(dg_ref.shape, F32)

        xh, r = _rms(x_ref[...], D)
        dx, dg = _rms_bwd(xh, r, g_ref[...], dh_ref[...], D)
        gx_ref[...] = dy_ref[...].astype(F32) + dx
        dg_ref[...] += jnp.sum(dg, axis=0, keepdims=True)

    grad_x, g_norm = pl.pallas_call(
        final_bwd_kern, name="final_bwd", grid=(nr2,),
        in_specs=[rowb2(D, 0), fullb((1, D)), rowb2(D, 0), rowb2(D, 0)],
        out_specs=[rowb2(D, 0), fullb((1, D))],
        out_shape=[jax.ShapeDtypeStruct((N, D), F32), jax.ShapeDtypeStruct((1, D), F32)],
        compiler_params=_cparams(("arbitrary",)),
    )(xs, norm_g, d_h, dyb)

    res = [{}, {}, {}, {}]

    def adam_into(n, parts):
        outs = _adam(parts, W[n][0], Mo[n][0], Vo[n][0], "adam_" + n)
        for k in range(4):
            res[k][n] = outs[k][None]
        return outs[0]

    recv_e = _split_wait(xe, False, grad_x, "xchg_early_wait")
    last = [adam_into(n, parts) for n, parts in zip(early, recv_e)][-1]
    recv_w = _split_wait(xw, False, last, "xchg_win_wait")
    adam_into('w_in', recv_w[0])

    small_g = {'norm_g': g_norm, 'mla_q_norm_g': g_q_norm, 'mla_kv_norm_g': g_kv_norm,
               'mla_qn_nope_g': g_qn_nope, 'mla_qn_rope_g': g_qn_rope[:, :c.ROPE], 'mla_kn_nope_g': g_kn_nope,
               'mla_kn_rope_g': g_kn_rope[:, :c.ROPE], 'mem_norm_g': g_mem_norm, 'mem_qn_g': g_mem_qn,
               'mem_kn_g': g_mem_kn}
    small_part = _pack([small_g[n] for n in SMALL] + [g_convw[0:3, :]], 0)
    small_all = _all_gather([small_part], "ag_small_grads")[0]
    small_shapes = [W[n].shape for n in SMALL]
    pieces = _unpack(small_all, small_shapes + [(3, CW)])
    cw8 = CW // NDEV
    conv_mine = lax.dynamic_slice_in_dim(pieces[-1].reshape(NDEV, 3, NDEV, cw8), me, 1, axis=2)[:, :, 0, :]
    sm_parts = _pack(pieces[:-1] + [conv_mine], 1)
    sm_names = SMALL + ['conv_w']
    sm_shapes = small_shapes + [(3, cw8)]
    w_sm = _pack([W[n] for n in SMALL] + [conv_w[0]], 0)
    m_sm = _pack([Mo[n] for n in SMALL] + [m_conv_w[0]], 0)
    v_sm = _pack([Vo[n] for n in SMALL] + [v_conv_w[0]], 0)
    outs_sm = [_unpack(o, sm_shapes) for o in _adam(sm_parts, w_sm, m_sm, v_sm, "adam_small")]
    for k in range(4):
        for n, a in zip(sm_names, outs_sm[k]):
            res[k][n] = a[None] if n == 'conv_w' else a
    return (loss, grad_x[None], *[res[0][n] for n in WEIGHTS], *[res[1][n] for n in WEIGHTS],
            *[res[2][n] for n in WEIGHTS], *[res[3][n] for n in WEIGHTS])
```

```python
import math

import jax
import jax.numpy as jnp
from jax import lax
from jax.experimental import pallas as pl
from jax.experimental.pallas import tpu as pltpu

F32 = jnp.float32
_BF = jnp.bfloat16
EPS = 1e-6
CHUNK = 64
ROPE_THETA = 10000.0
ADAM_LR, ADAM_B1, ADAM_B2, ADAM_EPS, ADAM_WD, ADAM_STEP = 0.001, 0.9, 0.999, 1e-08, 0.01, 10
NDEV = 8
AXES = ("x", "y", "c")
MESH = pl.DeviceIdType.MESH
LANES = 128
NEG = -1e30
LOG2E = math.log2(math.e)
V7X_VMEM_LIMIT = 56 * 1024 * 1024
PACK_C = 1024
ATT_BLOCK = 512
ATT_UNROLL_FWD = 8
ATT_UNROLL_BWD = 6
ATT_QBLOCKS = 2
ADAM_BLOCK_ELEMS = 256 * 1024

WEIGHTS = ['norm_g', 'w_in', 'conv_w', 'w_conv_out', 'mla_q_norm_g', 'w_uq', 'mla_kv_norm_g', 'w_ukv',
           'mla_qn_nope_g', 'mla_qn_rope_g', 'mla_kn_nope_g', 'mla_kn_rope_g', 'w_mla_out', 'mem_norm_g',
           'w_mem_kv', 'mem_qn_g', 'mem_kn_g', 'w_mem_out', 'w_o']
BIG = ['w_in', 'w_conv_out', 'w_uq', 'w_ukv', 'w_mla_out', 'w_mem_kv', 'w_mem_out', 'w_o']
COL_SHARDED = ('w_in', 'w_conv_out', 'w_uq', 'w_ukv', 'w_mem_out')
SMALL = ['norm_g', 'mla_q_norm_g', 'mla_kv_norm_g', 'mla_qn_nope_g', 'mla_qn_rope_g', 'mla_kn_nope_g',
         'mla_kn_rope_g', 'mem_norm_g', 'mem_qn_g', 'mem_kn_g']


def _tile(dim, target, align):
    if dim <= target:
        return dim
    t = target - target % align
    while t > 0:
        if dim % t == 0:
            return t
        t -= align
    raise ValueError(f"no tile for {dim} {target} {align}")


def _cparams(sem):
    return pltpu.CompilerParams(dimension_semantics=sem, vmem_limit_bytes=V7X_VMEM_LIMIT)


def _sig(x):
    return 1.0 / (1.0 + jnp.exp(-x))


def _rms(x, n):
    r = lax.rsqrt(jnp.sum(x * x, axis=-1, keepdims=True) * (1.0 / n) + EPS)
    return x * r, r


def _rms_bwd(xhat, r, g, dy, n):
    dxh = dy * g
    dx = r * (dxh - xhat * (jnp.sum(dxh * xhat, axis=-1, keepdims=True) * (1.0 / n)))
    return dx, dy * xhat


def _rowmean128(x, n):
    return jnp.dot(x.astype(_BF), jnp.full((LANES, LANES), 1.0 / n, _BF), preferred_element_type=F32)


def _rms128(x, n):
    r = lax.rsqrt(_rowmean128(x * x, n) + EPS)
    return x * r, r


def _rms128_bwd(xhat, r, g, dy, n):
    dxh = dy * g
    dx = r * (dxh - xhat * _rowmean128(dxh * xhat, n))
    return dx, dy * xhat


def _rope(x, cosp, sina, sinb):
    return x * cosp + pltpu.roll(x, 96, 1) * sina + pltpu.roll(x, 32, 1) * sinb


def _rope_t(d, cosp, sina, sinb):
    return d * cosp + pltpu.roll(d * sina, 32, 1) + pltpu.roll(d * sinb, 96, 1)


def _dot_nt(a, b):
    return lax.dot_general(a, b, (((1,), (1,)), ((), ())), preferred_element_type=F32)


def _dot_tn(a, b):
    return lax.dot_general(a, b, (((0,), (0,)), ((), ())), preferred_element_type=F32)


def _dot(a, b):
    return jnp.dot(a, b, preferred_element_type=F32)


def _all_gather(shards, name):
    na = len(shards)
    nc = 9
    halves = [s.shape[0] // 32 * 16 if s.shape[0] >= 256 else None for s in shards]

    def body(*refs):
        x_refs, out_refs = refs[:na], refs[na:2 * na]
        send_sems, recv_sems, local_sems = refs[2 * na:]
        x, y, c = lax.axis_index("x"), lax.axis_index("y"), lax.axis_index("c")
        me, sib = (x, y, c), (x, y, 1 - c)
        px, py, pd = (1 - x, y, c), (x, 1 - y, c), (1 - x, 1 - y, c)

        def other_core(p):
            return (p[0], p[1], 1 - p[2])

        def rows(a, blk, part=None):
            r = out_refs[a].at[4 * blk[0] + 2 * blk[1] + blk[2]]
            if part is None or halves[a] is None:
                return r
            rest = shards[a].shape[0] - halves[a]
            return r.at[pl.ds(0, halves[a])] if part == 0 else r.at[pl.ds(halves[a], rest)]

        def copy(a, k, blk, to, part=None, src=None):
            dst = rows(a, blk, part)
            return pltpu.make_async_remote_copy(
                src_ref=dst if src is None else src, dst_ref=dst,
                send_sem=send_sems.at[nc * a + k], recv_sem=recv_sems.at[nc * a + k],
                device_id=to, device_id_type=MESH)

        mine = [pltpu.make_async_copy(x_refs[a], rows(a, me), local_sems.at[a]) for a in range(na)]
        for cp in mine:
            cp.start()
        started = []
        for a in range(na):
            started += [copy(a, 0, me, px, src=x_refs[a]), copy(a, 1, me, py, src=x_refs[a]),
                        copy(a, 2, me, sib, src=x_refs[a])]
        for cp in started:
            cp.start()

        def forward(cp):
            cp.start()
            started.append(cp)

        for a in range(na):
            copy(a, 0, px, me).wait_recv()
            forward(copy(a, 3, px, py, part=0))
            forward(copy(a, 4, px, sib))
        for a in range(na):
            copy(a, 1, py, me).wait_recv()
            if halves[a] is not None:
                forward(copy(a, 5, py, px, part=1))
            forward(copy(a, 6, py, sib))
        for a in range(na):
            copy(a, 3, pd, me, part=0).wait_recv()
            forward(copy(a, 7, pd, sib, part=0))
            if halves[a] is not None:
                copy(a, 5, pd, me, part=1).wait_recv()
                forward(copy(a, 8, pd, sib, part=1))
        for a in range(na):
            copy(a, 2, sib, me).wait_recv()
            copy(a, 4, other_core(px), me).wait_recv()
            copy(a, 6, other_core(py), me).wait_recv()
            copy(a, 7, other_core(pd), me, part=0).wait_recv()
            if halves[a] is not None:
                copy(a, 8, other_core(pd), me, part=1).wait_recv()
        for cp in started:
            cp.wait_send()
        for cp in mine:
            cp.wait()

    any_spec = pl.BlockSpec(memory_space=pl.ANY)
    return pl.pallas_call(
        body, name=name,
        out_shape=[jax.ShapeDtypeStruct((NDEV,) + s.shape, s.dtype) for s in shards],
        in_specs=[any_spec] * na, out_specs=[any_spec] * na,
        scratch_shapes=[pltpu.SemaphoreType.DMA((nc * na,)), pltpu.SemaphoreType.DMA((nc * na,)),
                        pltpu.SemaphoreType.DMA((na,))],
    )(*shards)


_HBM = pl.BlockSpec(memory_space=pltpu.HBM)
_SEM = pl.BlockSpec(memory_space=pltpu.SEMAPHORE)
_EFFECT = pltpu.SideEffectType.DATAFLOW_SIDE_EFFECTING


def _split_copy(a, k, src_refs, land_refs, send_sems, recv_sems, gather, receive_side):
    x, y, c = lax.axis_index("x"), lax.axis_index("y"), lax.axis_index("c")
    me = 4 * x + 2 * y + c
    tx, ty, tc = x ^ ((k + 1) >> 2 & 1), y ^ ((k + 1) >> 1 & 1), c ^ ((k + 1) & 1)
    peer = 4 * tx + 2 * ty + tc
    return pltpu.make_async_remote_copy(
        src_ref=src_refs[a] if gather else src_refs[a].at[peer],
        dst_ref=land_refs[a].at[peer if receive_side else me],
        send_sem=send_sems.at[7 * a + k], recv_sem=recv_sems.at[7 * a + k],
        device_id=(tx, ty, tc), device_id_type=MESH)


def _own_copy(a, na, src_refs, land_refs, send_sems, gather):
    me = 4 * lax.axis_index("x") + 2 * lax.axis_index("y") + lax.axis_index("c")
    return pltpu.make_async_copy(src_refs[a] if gather else src_refs[a].at[me], land_refs[a].at[me],
                                 send_sems.at[7 * na + a])


def _split_start(srcs, lands, gather, name, after=None):
    na = len(srcs)
    extra = [] if after is None else [after]

    def body(*refs):
        src_refs, land_refs = refs[:na], refs[na:2 * na]
        send_sems, recv_sems = refs[2 * na + len(extra)], refs[2 * na + len(extra) + 1]
        token = refs[-1]
        for k in range(7):
            for a in range(na):
                _split_copy(a, k, src_refs, land_refs, send_sems, recv_sems, gather, False).start()
        for a in range(na):
            _own_copy(a, na, src_refs, land_refs, send_sems, gather).start()
        token[...] = jnp.zeros_like(token)

    hbm = [pltpu.HBM(b.shape, b.dtype) for b in list(srcs) + list(lands)]
    outs = pl.pallas_call(
        body, name=name,
        out_shape=(pltpu.SemaphoreType.DMA((8 * na,)), pltpu.SemaphoreType.DMA((7 * na,)), *hbm,
                   jax.ShapeDtypeStruct((8, LANES), F32)),
        in_specs=[_HBM] * (2 * na) + [pl.BlockSpec(memory_space=pl.ANY)] * len(extra),
        out_specs=(_SEM, _SEM, *[_HBM] * (2 * na), pl.BlockSpec(memory_space=pltpu.VMEM)),
        input_output_aliases={j: 2 + j for j in range(2 * na)},
        compiler_params=pltpu.CompilerParams(has_side_effects=_EFFECT),
    )(*[pltpu.with_memory_space_constraint(b, pltpu.HBM) for b in srcs],
      *[pltpu.with_memory_space_constraint(l, pltpu.HBM) for l in lands], *extra)
    return outs[0], outs[1], outs[2:2 + na], outs[2 + na:2 + 2 * na], outs[-1]


def _split_wait(started, gather, after, name):
    send_sems, recv_sems, srcs, lands, _ = started
    na = len(srcs)

    def body(*refs):
        src_refs, land_refs = refs[:na], refs[na:2 * na]
        send_s, recv_s = refs[2 * na], refs[2 * na + 1]
        for k in range(7):
            for a in range(na):
                cp = _split_copy(a, k, src_refs, land_refs, send_s, recv_s, gather, True)
                cp.wait_send()
                cp.wait_recv()
        for a in range(na):
            _own_copy(a, na, src_refs, land_refs, send_s, gather).wait()

    hbm = [pltpu.HBM(b.shape, b.dtype) for b in list(srcs) + list(lands)]
    outs = pl.pallas_call(
        body, name=name, out_shape=tuple(hbm),
        in_specs=[_HBM] * (2 * na) + [_SEM, _SEM, pl.BlockSpec(memory_space=pl.ANY)],
        out_specs=tuple([_HBM] * (2 * na)),
        input_output_aliases={j: j for j in range(2 * na)},
        compiler_params=pltpu.CompilerParams(has_side_effects=_EFFECT),
    )(*srcs, *lands, send_sems, recv_sems, after)
    return outs[na:]


def _seg_rows(size):
    rows = -(-size // PACK_C)
    return -(-rows // 16) * 16


def _pack(arrs, lead):
    parts = []
    for a in arrs:
        lshape = a.shape[:lead]
        f = a.reshape(lshape + (-1,)).astype(F32)
        rows = _seg_rows(f.shape[-1])
        f = jnp.pad(f, [(0, 0)] * lead + [(0, rows * PACK_C - f.shape[-1])])
        parts.append(f.reshape(lshape + (rows, PACK_C)))
    return jnp.concatenate(parts, axis=lead)


def _unpack(buf, shapes):
    lshape = buf.shape[:-2]
    out, r = [], 0
    for shp in shapes:
        size = math.prod(shp)
        rows = _seg_rows(size)
        seg = buf[..., r:r + rows, :].reshape(lshape + (rows * PACK_C,))[..., :size]
        out.append(seg.reshape(lshape + tuple(shp)))
        r += rows
    return out


def _mm(a, b, *, name, out_dtype, ta=False, tb=False, bm=1024, bn=1024, bk=2048, after=None, plus=None):
    if ta:
        kdim, m = a.shape
    else:
        m, kdim = a.shape
    n, k2 = b.shape if tb else b.shape[::-1]
    assert kdim == k2 and not (ta and tb), (a.shape, b.shape)
    bm = _tile(m, bm, LANES if ta else 16)
    bn = _tile(n, bn, LANES)
    bk = _tile(kdim, bk, LANES)
    nk = kdim // bk
    n_after = 0 if after is None else 1
    n_plus = 0 if plus is None else 2

    def kern(a_ref, b_ref, *rest):
        plus_refs = rest[n_after:n_after + n_plus]
        o_ref, scratch = rest[n_after + n_plus], rest[n_after + n_plus + 1:]
        part = (_dot_tn if ta else _dot_nt if tb else _dot)(a_ref[...], b_ref[...])

        def first(p):
            return p + _dot(plus_refs[0][...], plus_refs[1][...]) if plus is not None else p

        if nk == 1:
            o_ref[...] = first(part).astype(o_ref.dtype)
        else:
            acc = scratch[0] if scratch else o_ref
            k = pl.program_id(2)

            @pl.when(k == 0)
            def _():
                acc[...] = first(jnp.zeros(acc.shape, F32))

            acc[...] += part
            if scratch:
                @pl.when(k == nk - 1)
                def _():
                    o_ref[...] = acc[...].astype(o_ref.dtype)

    a_spec = pl.BlockSpec((bk, bm), lambda i, j, k: (k, i)) if ta else pl.BlockSpec((bm, bk), lambda i, j, k: (i, k))
    b_spec = pl.BlockSpec((bn, bk), lambda i, j, k: (j, k)) if tb else pl.BlockSpec((bk, bn), lambda i, j, k: (k, j))
    extra_specs, extra_args = [], []
    if after is not None:
        extra_specs.append(pl.BlockSpec(after.shape, lambda i, j, k: (0, 0)))
        extra_args.append(after)
    if plus is not None:
        kk = plus[0].shape[1]
        extra_specs += [pl.BlockSpec((bm, kk), lambda i, j, k: (i, 0)), pl.BlockSpec((kk, bn), lambda i, j, k: (0, j))]
        extra_args += list(plus)
    return pl.pallas_call(
        kern, name=name, grid=(m // bm, n // bn, nk),
        in_specs=[a_spec, b_spec] + extra_specs,
        out_specs=pl.BlockSpec((bm, bn), lambda i, j, k: (i, j)),
        out_shape=jax.ShapeDtypeStruct((m, n), out_dtype),
        scratch_shapes=[pltpu.VMEM((bm, bn), F32)] if nk > 1 and out_dtype != F32 else [],
        compiler_params=_cparams(("parallel", "parallel", "arbitrary")),
    )(a, b, *extra_args)


def _adam(parts, w_a, m_a, v_a, name):
    rows, cols = w_a.shape
    rb = _tile(rows, max(8, ADAM_BLOCK_ELEMS // cols // 8 * 8), 8)
    bc1 = 1.0 - ADAM_B1 ** ADAM_STEP
    bc2 = 1.0 - ADAM_B2 ** ADAM_STEP

    def adam_kern(p_ref, w_ref, m_ref, v_ref, g_ref, d_ref, nm_ref, nv_ref):
        g = p_ref[0].astype(F32)
        for j in range(1, NDEV):
            g = g + p_ref[j].astype(F32)
        m_new = ADAM_B1 * m_ref[...] + (1.0 - ADAM_B1) * g
        v_new = ADAM_B2 * v_ref[...] + (1.0 - ADAM_B2) * (g * g)
        g_ref[...] = g
        nm_ref[...] = m_new
        nv_ref[...] = v_new
        d_ref[...] = -ADAM_LR * ((m_new / bc1) / (jnp.sqrt(v_new / bc2) + ADAM_EPS) + ADAM_WD * w_ref[...])

    blk = pl.BlockSpec((rb, cols), lambda i: (i, 0))
    return pl.pallas_call(
        adam_kern, name=name, grid=(rows // rb,),
        in_specs=[pl.BlockSpec((NDEV, rb, cols), lambda i: (0, i, 0)), blk, blk, blk],
        out_specs=[blk] * 4, out_shape=[jax.ShapeDtypeStruct((rows, cols), F32)] * 4,
        compiler_params=_cparams(("parallel",)),
    )(parts, w_a, m_a, v_a)


class _Cfg:
    pass


def _config(x, conv_w, w_uq, w_ukv, mla_qn_nope_g, mla_qn_rope_g, mem, mem_qn_g, w_mem_out, w_mla_out):
    c = _Cfg()
    c.N, c.D = x.shape[1], x.shape[2]
    c.CW = conv_w.shape[2] * NDEV
    c.QL, c.KVL = w_uq.shape[1], w_ukv.shape[1]
    c.NOPE, c.ROPE = mla_qn_nope_g.shape[1], mla_qn_rope_g.shape[1]
    c.H = w_uq.shape[2] * NDEV // (c.NOPE + c.ROPE)
    c.V = w_ukv.shape[2] * NDEV // c.H - c.NOPE
    assert c.NOPE == LANES and c.V == LANES and c.ROPE == LANES // 2
    c.HW = 2 * LANES
    c.HV = c.H * c.V
    assert w_mla_out.shape[1] * NDEV == c.HV
    c.M = mem.shape[1]
    c.MHD = mem_qn_g.shape[1]
    c.MW = w_mem_out.shape[1]
    c.MH = c.MW // c.MHD
    c.o_conv = 0
    c.o_mz = 4 * c.CW
    c.o_g = c.o_mz + c.HV
    c.o_mem = c.o_g + 3 * c.D
    c.o_lora = c.o_mem + 2 * c.MW
    c.P = c.o_lora + c.QL + c.KVL
    assert c.o_mz % c.HV == 0 and c.o_g % (3 * c.D) == 0 and c.o_mem % (2 * c.MW) == 0
    assert c.o_lora % (c.QL + c.KVL) == 0 and c.QL % LANES == 0 and c.KVL % LANES == 0
    c.IN = 4 * c.CW + c.QL + c.KVL + c.ROPE + c.HV + 2 * c.MW + 3 * c.D
    c.R = _tile(c.N, 256, 16)
    c.RP = _tile(c.N, 1024, 16)
    c.HG = _tile(c.H, 8, 1)
    c.B = _tile(c.N, ATT_BLOCK, CHUNK)
    c.scale = float((c.NOPE + c.ROPE) ** -0.5)
    c.mscale = float(c.MHD ** -0.5)
    return c


def _win_segments(c):
    ref_order = (('conv', 4 * c.CW), ('lora', c.QL + c.KVL), ('kr', c.ROPE), ('mz', c.HV), ('mem', 2 * c.MW), ('g', 3 * c.D))
    mine = {'conv': c.o_conv, 'mz': c.o_mz, 'g': c.o_g, 'mem': c.o_mem, 'lora': c.o_lora, 'kr': 0}
    segs, o = [], 0
    for nm, wd in ref_order:
        segs.append((nm, o, wd, mine[nm]))
        o += wd
    return segs


def _win_split(g_win_t, c):
    n8 = g_win_t.shape[1]

    def rows(a, wd):
        return [g_win_t[j][max(a, j * n8) - j * n8:min(a + wd, (j + 1) * n8) - j * n8]
                for j in range(a // n8, (a + wd - 1) // n8 + 1)]

    segs = {nm: (a, wd) for nm, a, wd, _ in _win_segments(c)}
    main = [p for nm in ('conv', 'mz', 'g', 'mem', 'lora') for p in rows(*segs[nm])]
    kr = jnp.concatenate(rows(*segs['kr']) + [jnp.zeros((LANES - c.ROPE, g_win_t.shape[2]), g_win_t.dtype)], axis=0)
    return jnp.concatenate(main, axis=0), kr


def _win_blocks(g, g_kr, c):
    n8 = c.IN // NDEV
    blocks = []
    for j in range(NDEV):
        lo, hi = j * n8, (j + 1) * n8
        parts = []
        for nm, a, wd, mine in _win_segments(c):
            s, e = max(a, lo), min(a + wd, hi)
            if s < e:
                parts.append((g_kr if nm == 'kr' else g)[:, mine + s - a:mine + e - a])
        blocks.append(jnp.concatenate(parts, axis=1))
    return jnp.stack(blocks, axis=0)


def kernel(x, positions, mem, norm_g, w_in, conv_w, w_conv_out, mla_q_norm_g, w_uq, mla_kv_norm_g, w_ukv, mla_qn_nope_g, mla_qn_rope_g, mla_kn_nope_g, mla_kn_rope_g, w_mla_out, mem_norm_g, w_mem_kv, mem_qn_g, mem_kn_g, w_mem_out, w_o, loss_target, m_norm_g, m_w_in, m_conv_w, m_w_conv_out, m_mla_q_norm_g, m_w_uq, m_mla_kv_norm_g, m_w_ukv, m_mla_qn_nope_g, m_mla_qn_rope_g, m_mla_kn_nope_g, m_mla_kn_rope_g, m_w_mla_out, m_mem_norm_g, m_w_mem_kv, m_mem_qn_g, m_mem_kn_g, m_w_mem_out, m_w_o, v_norm_g, v_w_in, v_conv_w, v_w_conv_out, v_mla_q_norm_g, v_w_uq, v_mla_kv_norm_g, v_w_ukv, v_mla_qn_nope_g, v_mla_qn_rope_g, v_mla_kn_nope_g, v_mla_kn_rope_g, v_w_mla_out, v_mem_norm_g, v_w_mem_kv, v_mem_qn_g, v_mem_kn_g, v_w_mem_out, v_w_o):
    args = dict(locals())
    W = {n: args[n] for n in WEIGHTS}
    Mo = {n: args['m_' + n] for n in WEIGHTS}
    Vo = {n: args['v_' + n] for n in WEIGHTS}
    c = _config(x, conv_w, w_uq, w_ukv, mla_qn_nope_g, mla_qn_rope_g, mem, mem_qn_g, w_mem_out, w_mla_out)
    N, D, R, B, H = c.N, c.D, c.R, c.B, c.H
    assert x.shape[0] == 1
    xs = x[0]
    tgt = loss_target[0]
    memx = mem[0]
    me = 4 * lax.axis_index("x") + 2 * lax.axis_index("y") + lax.axis_index("c")
    nr = N // R

    g_win, g_taps = _all_gather([W['w_in'][0].astype(_BF).T, conv_w[0]], "ag_w_in")
    rest = [n for n in BIG if n != 'w_in']
    shards_r = [W[n][0].astype(_BF).T if n in COL_SHARDED else W[n][0].astype(_BF) for n in rest]
    lands_r = [lax.empty((NDEV,) + s.shape, s.dtype) for s in shards_r]
    ag_rest = _split_start(shards_r, lands_r, True, "ag_rest_start", after=g_win)
    win_pT, w_krT = _win_split(g_win, c)
    convw = jnp.transpose(g_taps, (1, 0, 2)).reshape(3, c.CW)
    convw8 = jnp.pad(convw, ((0, 5), (0, 0)))

    def rowb(width, cidx, rows=R):
        return pl.BlockSpec((rows, width), lambda i, _c=cidx: (i, _c))

    R2 = _tile(N, 512, 16)
    nr2 = N // R2
    RM = _tile(N, 1024, 16)

    def rowb2(width, cidx):
        return rowb(width, cidx, R2)

    def fullb(shape):
        nd = len(shape)
        return pl.BlockSpec(shape, lambda *_: (0,) * nd)

    def pad_lanes(g, w=LANES):
        return jnp.pad(g, ((0, 0), (0, w - g.shape[1])))

    def tabs_of(rows):
        return pl.BlockSpec((3, rows, LANES), lambda i, *_: (0, i, 0))

    half = c.ROPE // 2
    inv_freq = jnp.power(ROPE_THETA, -jnp.arange(half, dtype=F32) / half)
    invf = jnp.concatenate([inv_freq, inv_freq, jnp.zeros((LANES - c.ROPE,), F32)])[None, :]
    pos_col = positions[0].astype(F32).reshape(N, 1)

    def rope_tab_kern(pos_ref, invf_ref, o_ref):
        ang = pos_ref[...] * invf_ref[...]
        co, si = jnp.cos(ang), jnp.sin(ang)
        lane = lax.broadcasted_iota(jnp.int32, ang.shape, 1)
        o_ref[0] = jnp.where(lane < c.ROPE, co, 0.0)
        o_ref[1] = jnp.where(lane < half, -si, 0.0)
        o_ref[2] = jnp.where(jnp.logical_and(lane >= half, lane < c.ROPE), si, 0.0)

    tabs = pl.pallas_call(
        rope_tab_kern, name="rope_tab", grid=(nr,),
        in_specs=[pl.BlockSpec((R, 1), lambda i: (i, 0)), fullb((1, LANES))],
        out_specs=tabs_of(R),
        out_shape=jax.ShapeDtypeStruct((3, N, LANES), F32),
        compiler_params=_cparams(("parallel",)),
    )(pos_col, invf)

    def make_rms_kern():
        def rms_fwd_kern(x_ref, g_ref, o_ref):
            xh, _ = _rms(x_ref[...].astype(F32), x_ref.shape[-1])
            o_ref[...] = (xh * g_ref[...]).astype(o_ref.dtype)
        return rms_fwd_kern

    h = pl.pallas_call(
        make_rms_kern(), name="rms_x", grid=(nr2,),
        in_specs=[rowb2(D, 0), fullb((1, D))], out_specs=rowb2(D, 0),
        out_shape=jax.ShapeDtypeStruct((N, D), _BF), compiler_params=_cparams(("parallel",)),
    )(xs, norm_g)

    proj = _mm(h, win_pT, tb=True, name="mm_proj", out_dtype=_BF, after=ag_rest[4])
    kr_raw = _mm(h, w_krT, tb=True, name="mm_kr", out_dtype=_BF)

    Wf = {n: g.reshape(-1, g.shape[2]) for n, g in zip(rest, _split_wait(ag_rest, True, proj, "ag_rest_wait"))}
    wuqT = Wf['w_uq'].reshape(H, c.NOPE + c.ROPE, c.QL)
    wuq_pT = jnp.pad(wuqT, ((0, 0), (0, c.HW - c.NOPE - c.ROPE), (0, 0))).reshape(H * c.HW, c.QL)
    wukvT = Wf['w_ukv']
    wcoT, wmo, wmkv, wmemoT, wo = Wf['w_conv_out'], Wf['w_mla_out'], Wf['w_mem_kv'], Wf['w_mem_out'], Wf['w_o']

    CW = c.CW
    conv_blk = c.o_conv // (4 * CW)
    HALO = 16
    rh = R // HALO

    def conv_parts(blk):
        blk = blk.astype(F32)
        return blk[:, 0:CW], blk[:, CW:2 * CW], blk[:, 2 * CW:3 * CW], blk[:, 3 * CW:4 * CW]

    def shifted(cu, prev, i):
        prev = jnp.where(i > 0, prev, 0.0)
        rid = lax.broadcasted_iota(jnp.int32, cu.shape, 0)
        last, last2 = prev[HALO - 1:HALO, :], prev[HALO - 2:HALO - 1, :]
        sh1 = jnp.where(rid == 0, last, pltpu.roll(cu, 1, 0))
        sh2 = jnp.where(rid == 0, last2, jnp.where(rid == 1, last, pltpu.roll(cu, 2, 0)))
        return sh1, sh2

    def conv_fwd_kern(p_ref, prev_ref, w_ref, o_ref):
        i = pl.program_id(0)
        cg, bg, u, z = conv_parts(p_ref[...])
        pc, _, pu, _ = conv_parts(prev_ref[...])
        cu = cg * u
        sh1, sh2 = shifted(cu, pc * pu, i)
        w = w_ref[...]
        conv = w[0:1, :] * sh2 + w[1:2, :] * sh1 + w[2:3, :] * cu
        o_ref[...] = (bg * conv * (z * _sig(z))).astype(o_ref.dtype)

    prev_spec = pl.BlockSpec((HALO, 4 * CW), lambda i: (jnp.maximum(i * rh - 1, 0), conv_blk))
    a_conv = pl.pallas_call(
        conv_fwd_kern, name="conv_fwd", grid=(nr,),
        in_specs=[rowb(4 * CW, conv_blk), prev_spec, fullb((8, CW))],
        out_specs=rowb(CW, 0), out_shape=jax.ShapeDtypeStruct((N, CW), _BF),
        compiler_params=_cparams(("parallel",)),
    )(proj, proj, convw8)
    o_conv = _mm(a_conv, wcoT, tb=True, name="mm_oconv", out_dtype=_BF)

    QL, KVL, HW = c.QL, c.KVL, c.HW
    lora_blk = c.o_lora // (QL + KVL)

    def lora_fwd_kern(p_ref, gq_ref, gkv_ref, q_ref, kv_ref):
        blk = p_ref[...].astype(F32)
        qh, _ = _rms(blk[:, :QL], QL)
        kh, _ = _rms(blk[:, QL:], KVL)
        q_ref[...] = (qh * gq_ref[...]).astype(q_ref.dtype)
        kv_ref[...] = (kh * gkv_ref[...]).astype(kv_ref.dtype)

    cqn, ckvn = pl.pallas_call(
        lora_fwd_kern, name="lora_fwd", grid=(nr2,),
        in_specs=[rowb2(QL + KVL, lora_blk), fullb((1, QL)), fullb((1, KVL))],
        out_specs=[rowb2(QL, 0), rowb2(KVL, 0)],
        out_shape=[jax.ShapeDtypeStruct((N, QL), _BF), jax.ShapeDtypeStruct((N, KVL), _BF)],
        compiler_params=_cparams(("parallel",)),
    )(proj, mla_q_norm_g, mla_kv_norm_g)
    q_p = _mm(cqn, wuq_pT, tb=True, name="mm_q", out_dtype=_BF, bn=2048)
    kv = _mm(ckvn, wukvT, tb=True, name="mm_kv", out_dtype=_BF, bn=2048)

    g_qn, g_qr = mla_qn_nope_g, pad_lanes(mla_qn_rope_g)
    g_kn, g_kr = mla_kn_nope_g, pad_lanes(mla_kn_rope_g)

    def krope_fwd_kern(p_ref, t_ref, g_ref, o_ref):
        xh, _ = _rms128(p_ref[...].astype(F32), c.ROPE)
        o_ref[...] = _rope(xh * g_ref[...], t_ref[0], t_ref[1], t_ref[2]).astype(o_ref.dtype)

    k_rope = pl.pallas_call(
        krope_fwd_kern, name="krope_fwd", grid=(nr2,),
        in_specs=[rowb2(LANES, 0), tabs_of(R2), fullb((1, LANES))],
        out_specs=rowb2(LANES, 0), out_shape=jax.ShapeDtypeStruct((N, LANES), _BF),
        compiler_params=_cparams(("parallel",)),
    )(kr_raw, tabs, g_kr)

    RP, HG = c.RP, c.HG
    nrp, nhg = N // RP, H // HG
    heads_in = pl.BlockSpec((RP, HG * HW), lambda i, hg: (i, hg))
    heads_out = pl.BlockSpec((HG, RP, HW), lambda i, hg: (hg, i, 0))

    def q_prep_kern(q_ref, t_ref, gn_ref, gr_ref, o_ref):
        for g in range(HG):
            blk = q_ref[:, g * HW:(g + 1) * HW].astype(F32)
            nh, _ = _rms128(blk[:, :LANES], c.NOPE)
            rhat, _ = _rms128(blk[:, LANES:], c.ROPE)
            rot = _rope(rhat * gr_ref[...], t_ref[0], t_ref[1], t_ref[2])
            o_ref[g] = jnp.concatenate([nh * gn_ref[...], rot], axis=1).astype(o_ref.dtype)

    q_cat = pl.pallas_call(
        q_prep_kern, name="q_prep", grid=(nrp, nhg),
        in_specs=[heads_in, tabs_of(RP), fullb((1, LANES)), fullb((1, LANES))],
        out_specs=heads_out, out_shape=jax.ShapeDtypeStruct((H, N, HW), _BF),
        compiler_params=_cparams(("parallel", "parallel")),
    )(q_p, tabs, g_qn * (c.scale * LOG2E), g_qr * (c.scale * LOG2E))

    def k_prep_kern(kv_ref, kr_ref, gn_ref, o_ref):
        for g in range(HG):
            kn, _ = _rms128(kv_ref[:, g * HW:g * HW + LANES].astype(F32), c.NOPE)
            o_ref[g] = jnp.concatenate([(kn * gn_ref[...]).astype(o_ref.dtype), kr_ref[...]], axis=1)

    k_cat = pl.pallas_call(
        k_prep_kern, name="k_prep", grid=(nrp, nhg),
        in_specs=[heads_in, pl.BlockSpec((RP, LANES), lambda i, hg: (i, 0)), fullb((1, LANES))],
        out_specs=heads_out, out_shape=jax.ShapeDtypeStruct((H, N, HW), _BF),
        compiler_params=_cparams(("parallel", "parallel")),
    )(kv, k_rope, g_kn)

    QB = ATT_QBLOCKS if N % (ATT_QBLOCKS * B) == 0 else 1
    BQ = QB * B
    nq = N // BQ
    assert CHUNK & (CHUNK - 1) == 0 and B % CHUNK == 0 and ATT_UNROLL_FWD % QB == 0 and ATT_UNROLL_BWD % QB == 0

    def live_rows(diag):
        return slice(0 if diag is None else diag * B, BQ)

    def diag_mask(s):
        row = lax.broadcasted_iota(jnp.int32, s.shape, 0)
        col = lax.broadcasted_iota(jnp.int32, s.shape, 1)
        shift = CHUNK.bit_length() - 1
        allowed = jnp.right_shift(col, shift) <= jnp.right_shift(row, shift)
        return jnp.where(allowed, s, NEG)

    k_head = pl.BlockSpec((1, N, HW), lambda hh, i: (hh, 0, 0))
    v_head = pl.BlockSpec((N, LANES), lambda hh, i: (0, 2 * hh + 1))
    q_blk = pl.BlockSpec((1, BQ, HW), lambda hh, i: (hh, i, 0))
    o_blk = pl.BlockSpec((BQ, LANES), lambda hh, i: (i, hh))
    lse_blk = pl.BlockSpec((1, BQ, LANES), lambda hh, i: (hh, i, 0))

    def key_block_plan(i, run, unroll):
        def unrolled(u, carry):
            run(unroll * u, unroll, 0)
            return carry

        n_full = QB * i
        lax.fori_loop(0, n_full // unroll, unrolled, 0)
        for rem in range(0, unroll, QB):
            @pl.when(n_full % unroll == rem)
            def _(rem=rem):
                run(n_full - rem, rem + QB, QB)

    def attn_fwd_kern(q_ref, k_ref, v_ref, o_ref, lse_ref, m_sc, acc_sc, s_sc):
        i = pl.program_id(1)
        m_sc[...] = jnp.full(m_sc.shape, NEG, F32)
        acc_sc[...] = jnp.zeros(acc_sc.shape, F32)

        def rows_of(t):
            return pl.ds(pl.multiple_of(t * B, B), B)

        def scores(t, slot, diag):
            rs = live_rows(diag)
            s_sc[slot, rs, :] = _dot_nt(q_ref[0, rs, :], k_ref[0, rows_of(t), :])

        def softmax_pv(t, slot, diag):
            rs = live_rows(diag)
            s = s_sc[slot, rs, :]
            if diag is not None:
                s = diag_mask(s)
            mt = s[:, 0:LANES]
            for cb in range(1, B // LANES):
                mt = jnp.maximum(mt, s[:, cb * LANES:(cb + 1) * LANES])
            m_prev = m_sc[rs, :]
            m_new = jnp.maximum(m_prev, jnp.max(mt, axis=1, keepdims=True))
            alpha = jnp.exp2(m_prev - m_new)
            p = jnp.concatenate([jnp.exp2(s[:, cb * LANES:(cb + 1) * LANES] - m_new).astype(_BF)
                                 for cb in range(B // LANES)], axis=1)
            v_ones = jnp.concatenate([v_ref[rows_of(t), :], jnp.ones((B, LANES), _BF)], axis=1)
            acc_sc[rs, :] = jnp.concatenate([alpha, alpha], axis=1) * acc_sc[rs, :] + _dot(p, v_ones)
            m_sc[rs, :] = m_new

        scores(0, 0, None)

        def run(first, count, n_diag):
            def diag_of(u):
                return u - (count - n_diag) if count - n_diag <= u < count else None

            for u in range(count):
                if u + 1 < count or n_diag == 0:
                    scores(first + u + 1, (u + 1) % 2, diag_of(u + 1))
                softmax_pv(first + u, u % 2, diag_of(u))

        key_block_plan(i, run, ATT_UNROLL_FWD)
        acc = acc_sc[...]
        o_ref[...] = (acc[:, :LANES] / acc[:, LANES:]).astype(o_ref.dtype)
        lse_ref[0] = m_sc[...] + jnp.log2(acc[:, LANES:])

    mla_y, lse = pl.pallas_call(
        attn_fwd_kern, name="attn_fwd", grid=(H, nq),
        in_specs=[q_blk, k_head, v_head], out_specs=[o_blk, lse_blk],
        out_shape=[jax.ShapeDtypeStruct((N, c.HV), _BF), jax.ShapeDtypeStruct((H, N, LANES), F32)],
        scratch_shapes=[pltpu.VMEM((BQ, LANES), F32), pltpu.VMEM((BQ, HW), F32), pltpu.VMEM((2, BQ, B), F32)],
        compiler_params=_cparams(("parallel", "arbitrary")),
    )(q_cat, k_cat, kv)

    HV = c.HV
    mz_blk = c.o_mz // HV

    def gate_fwd_kern(y_ref, z_ref, o_ref):
        z = z_ref[...].astype(F32)
        o_ref[...] = (y_ref[...].astype(F32) * (z * _sig(z))).astype(o_ref.dtype)

    a_mla = pl.pallas_call(
        gate_fwd_kern, name="gate_mla", grid=(nr2,),
        in_specs=[rowb2(HV, 0), rowb2(HV, mz_blk)], out_specs=rowb2(HV, 0),
        out_shape=jax.ShapeDtypeStruct((N, HV), _BF), compiler_params=_cparams(("parallel",)),
    )(mla_y, proj)
    o_mla = _mm(a_mla, wmo, name="mm_omla", out_dtype=_BF)

    M, MW, MH, MHD = c.M, c.MW, c.MH, c.MHD
    memn = pl.pallas_call(
        make_rms_kern(), name="rms_mem",
        grid=(1,), in_specs=[fullb((M, D)), fullb((1, D))], out_specs=fullb((M, D)),
        out_shape=jax.ShapeDtypeStruct((M, D), _BF), compiler_params=_cparams(("arbitrary",)),
    )(memx, mem_norm_g)
    kvm = _mm(memn, wmkv, name="mm_memkv", out_dtype=F32)

    def memk_fwd_kern(kv_ref, g_ref, k_ref, v_ref):
        for hh in range(MH):
            kh, _ = _rms(kv_ref[:, hh * MHD:(hh + 1) * MHD], MHD)
            k_ref[:, hh * MHD:(hh + 1) * MHD] = (kh * g_ref[...]).astype(k_ref.dtype)
        v_ref[...] = kv_ref[:, MW:].astype(v_ref.dtype)

    mem_k, mem_v = pl.pallas_call(
        memk_fwd_kern, name="memk_fwd", grid=(1,),
        in_specs=[fullb((M, 2 * MW)), fullb((1, MHD))], out_specs=[fullb((M, MW)), fullb((M, MW))],
        out_shape=[jax.ShapeDtypeStruct((M, MW), _BF)] * 2, compiler_params=_cparams(("arbitrary",)),
    )(kvm, mem_kn_g)

    mem_blk = c.o_mem // (2 * MW)

    def mem_head(qz_ref, k_ref, v_ref, g_ref, hh):
        sl = slice(hh * MHD, (hh + 1) * MHD)
        qh, r = _rms(qz_ref[:, sl].astype(F32), MHD)
        qn = (qh * g_ref[...]).astype(_BF)
        s = _dot_nt(qn, k_ref[:, sl]) * c.mscale
        e = jnp.exp(s - jnp.max(s, axis=1, keepdims=True))
        p = e / jnp.sum(e, axis=1, keepdims=True)
        y = _dot(p.astype(_BF), v_ref[:, sl])
        z = qz_ref[:, MW + hh * MHD:MW + (hh + 1) * MHD].astype(F32)
        return sl, qh, r, qn, p, y, z

    def mem_fwd_kern(qz_ref, k_ref, v_ref, g_ref, o_ref):
        for hh in range(MH):
            sl, _, _, _, _, y, z = mem_head(qz_ref, k_ref, v_ref, g_ref, hh)
            o_ref[:, sl] = (y * (z * _sig(z))).astype(o_ref.dtype)

    a_mem = pl.pallas_call(
        mem_fwd_kern, name="mem_fwd", grid=(N // RM,),
        in_specs=[rowb(2 * MW, mem_blk, RM), fullb((M, MW)), fullb((M, MW)), fullb((1, MHD))],
        out_specs=rowb(MW, 0, RM), out_shape=jax.ShapeDtypeStruct((N, MW), _BF),
        compiler_params=_cparams(("parallel",)),
    )(proj, mem_k, mem_v, mem_qn_g)
    o_mem = _mm(a_mem, wmemoT, tb=True, name="mm_omem", out_dtype=_BF)

    g_blk = c.o_g // (3 * D)

    def merge_fwd_kern(g_ref, oc_ref, om_ref, ome_ref, o_ref):
        g = g_ref[...].astype(F32)
        acc = _sig(g[:, :D]) * oc_ref[...].astype(F32)
        acc += _sig(g[:, D:2 * D]) * om_ref[...].astype(F32)
        acc += _sig(g[:, 2 * D:]) * ome_ref[...].astype(F32)
        o_ref[...] = acc.astype(o_ref.dtype)

    merged = pl.pallas_call(
        merge_fwd_kern, name="merge_fwd", grid=(nr2,),
        in_specs=[rowb2(3 * D, g_blk), rowb2(D, 0), rowb2(D, 0), rowb2(D, 0)], out_specs=rowb2(D, 0),
        out_shape=jax.ShapeDtypeStruct((N, D), _BF), compiler_params=_cparams(("parallel",)),
    )(proj, o_conv, o_mla, o_mem)
    obm, obn = _tile(N, 1024, 16), _tile(D, 1024, LANES)

    def out_loss_kern(a_ref, w_ref, x_ref, t_ref, dyb_ref, l_ref):
        e = x_ref[...] + _dot(a_ref[...], w_ref[...]) - t_ref[...]
        dyb_ref[...] = (e * (1.0 / D)).astype(dyb_ref.dtype)
        row = lax.broadcasted_iota(jnp.int32, l_ref.shape, 0)
        l_ref[...] = jnp.where(row == 0, jnp.sum(e * e, axis=0, keepdims=True), 0.0)

    oblk = pl.BlockSpec((obm, obn), lambda i, j: (i, j))
    dyb, lpart = pl.pallas_call(
        out_loss_kern, name="mm_out_loss", grid=(N // obm, D // obn),
        in_specs=[pl.BlockSpec((obm, D), lambda i, j: (i, 0)), pl.BlockSpec((D, obn), lambda i, j: (0, j)), oblk, oblk],
        out_specs=[oblk, pl.BlockSpec((8, obn), lambda i, j: (i, j))],
        out_shape=[jax.ShapeDtypeStruct((N, D), _BF), jax.ShapeDtypeStruct((8 * (N // obm), D), F32)],
        compiler_params=_cparams(("parallel", "parallel")),
    )(merged, wo, xs, tgt)
    loss = lax.psum(jnp.sum(lpart) * (0.5 / D), AXES)

    G = {}
    d_merged = _mm(dyb, wo, tb=True, name="mm_dmerged", out_dtype=_BF)
    G['w_o'] = _mm(merged, dyb, ta=True, name="mm_dwo", out_dtype=_BF, bk=4096)

    dproj0 = lax.empty((N, c.P), _BF)
    any_spec = pl.BlockSpec(memory_space=pl.ANY)

    def merge_bwd_kern(dp_any, g_ref, dm_ref, oc_ref, om_ref, ome_ref, dg_ref, doc_ref, dom_ref, dome_ref):
        g = g_ref[...].astype(F32)
        dm = dm_ref[...].astype(F32)
        for idx, (o_in, d_out) in enumerate(((oc_ref, doc_ref), (om_ref, dom_ref), (ome_ref, dome_ref))):
            sg = _sig(g[:, idx * D:(idx + 1) * D])
            d_out[...] = (sg * dm).astype(d_out.dtype)
            dg_ref[:, idx * D:(idx + 1) * D] = (dm * o_in[...].astype(F32) * sg * (1.0 - sg)).astype(dg_ref.dtype)

    dproj1, d_oconv, d_omla, d_omem = pl.pallas_call(
        merge_bwd_kern, name="merge_bwd", grid=(nr,),
        in_specs=[any_spec, rowb(3 * D, g_blk), rowb(D, 0), rowb(D, 0), rowb(D, 0), rowb(D, 0)],
        out_specs=[rowb(3 * D, g_blk), rowb(D, 0), rowb(D, 0), rowb(D, 0)],
        out_shape=[jax.ShapeDtypeStruct((N, c.P), _BF)] + [jax.ShapeDtypeStruct((N, D), _BF)] * 3,
        input_output_aliases={0: 0}, compiler_params=_cparams(("parallel",)),
    )(dproj0, proj, d_merged, o_conv, o_mla, o_mem)

    G['w_conv_out'] = _mm(a_conv, d_oconv, ta=True, name="mm_dwco", out_dtype=_BF, bk=4096)
    d_aconv = _mm(d_oconv, wcoT, name="mm_daconv", out_dtype=_BF)
    G['w_mla_out'] = _mm(a_mla, d_omla, ta=True, name="mm_dwmo", out_dtype=_BF, bk=4096)
    d_amla = _mm(d_omla, wmo, tb=True, name="mm_damla", out_dtype=_BF)
    G['w_mem_out'] = _mm(a_mem, d_omem, ta=True, name="mm_dwmemo", out_dtype=_BF, bk=4096)
    d_amem = _mm(d_omem, wmemoT, name="mm_damem", out_dtype=_BF)

    def conv_bwd_kern(dp_any, p_ref, prev_ref, next_ref, da_ref, dan_ref, w_ref, o_ref, dw_ref):
        i = pl.program_id(0)
        cg, bg, u, z = conv_parts(p_ref[...])
        pc, _, pu, _ = conv_parts(prev_ref[...])
        _, nbg, _, nz = conv_parts(next_ref[...])
        cu = cg * u
        sh1, sh2 = shifted(cu, pc * pu, i)
        w = w_ref[...]
        conv = w[0:1, :] * sh2 + w[1:2, :] * sh1 + w[2:3, :] * cu
        sg = _sig(z)
        sz = z * sg
        da = da_ref[...].astype(F32)
        dcy = da * sz
        d_z = da * (bg * conv) * (sg * (1.0 + z * (1.0 - sg)))
        d_b = dcy * conv
        dconv = dcy * bg
        dnext = dan_ref[...].astype(F32) * (nz * _sig(nz)) * nbg
        dnext = jnp.where(i < nr - 1, dnext, 0.0)
        rid = lax.broadcasted_iota(jnp.int32, cu.shape, 0)
        up1 = jnp.where(rid == R - 1, dnext[0:1, :], pltpu.roll(dconv, R - 1, 0))
        up2 = jnp.where(rid == R - 2, dnext[0:1, :], jnp.where(rid == R - 1, dnext[1:2, :], pltpu.roll(dconv, R - 2, 0)))
        dcu = w[2:3, :] * dconv + w[1:2, :] * up1 + w[0:1, :] * up2
        o_ref[:, 0:CW] = (dcu * u).astype(o_ref.dtype)
        o_ref[:, CW:2 * CW] = d_b.astype(o_ref.dtype)
        o_ref[:, 2 * CW:3 * CW] = (dcu * cg).astype(o_ref.dtype)
        o_ref[:, 3 * CW:4 * CW] = d_z.astype(o_ref.dtype)

        @pl.when(i == 0)
        def _():
            dw_ref[...] = jnp.zeros(dw_ref.shape, F32)

        dw_ref[0:1, :] += jnp.sum(dconv * sh2, axis=0, keepdims=True)
        dw_ref[1:2, :] += jnp.sum(dconv * sh1, axis=0, keepdims=True)
        dw_ref[2:3, :] += jnp.sum(dconv * cu, axis=0, keepdims=True)

    next_spec = pl.BlockSpec((HALO, 4 * CW), lambda i: (jnp.minimum((i + 1) * rh, N // HALO - 1), conv_blk))
    dan_spec = pl.BlockSpec((HALO, CW), lambda i: (jnp.minimum((i + 1) * rh, N // HALO - 1), 0))
    dproj2, g_convw = pl.pallas_call(
        conv_bwd_kern, name="conv_bwd", grid=(nr,),
        in_specs=[any_spec, rowb(4 * CW, conv_blk), prev_spec, next_spec, rowb(CW, 0), dan_spec, fullb((8, CW))],
        out_specs=[rowb(4 * CW, conv_blk), fullb((8, CW))],
        out_shape=[jax.ShapeDtypeStruct((N, c.P), _BF), jax.ShapeDtypeStruct((8, CW), F32)],
        input_output_aliases={0: 0}, compiler_params=_cparams(("arbitrary",)),
    )(dproj1, proj, proj, proj, d_aconv, d_aconv, convw8)

    def mem_bwd_kern(dp_any, qz_ref, da_ref, k_ref, v_ref, g_ref, o_ref, dk_ref, dv_ref, dg_ref):
        @pl.when(pl.program_id(0) == 0)
        def _():
            dk_ref[...] = jnp.zeros(dk_ref.shape, F32)
            dv_ref[...] = jnp.zeros(dv_ref.shape, F32)
            dg_ref[...] = jnp.zeros(dg_ref.shape, F32)

        for hh in range(MH):
            sl, qh, r, qn, p, y, z = mem_head(qz_ref, k_ref, v_ref, g_ref, hh)
            da = da_ref[:, sl].astype(F32)
            sg = _sig(z)
            dyh = da * (z * sg)
            o_ref[:, MW + hh * MHD:MW + (hh + 1) * MHD] = (da * y * (sg * (1.0 + z * (1.0 - sg)))).astype(o_ref.dtype)
            dyb_h = dyh.astype(_BF)
            dpm = _dot_nt(dyb_h, v_ref[:, sl])
            ds = (p * (dpm - jnp.sum(dpm * p, axis=1, keepdims=True)) * c.mscale).astype(_BF)
            dqn = _dot(ds, k_ref[:, sl])
            dk_ref[:, sl] += _dot_tn(ds, qn)
            dv_ref[:, sl] += _dot_tn(p.astype(_BF), dyb_h)
            dq, dgp = _rms_bwd(qh, r, g_ref[...], dqn, MHD)
            o_ref[:, sl] = dq.astype(o_ref.dtype)
            dg_ref[...] += jnp.sum(dgp, axis=0, keepdims=True)

    dproj3, d_memk, d_memv, g_mem_qn = pl.pallas_call(
        mem_bwd_kern, name="mem_bwd", grid=(N // RM,),
        in_specs=[any_spec, rowb(2 * MW, mem_blk, RM), rowb(MW, 0, RM), fullb((M, MW)), fullb((M, MW)), fullb((1, MHD))],
        out_specs=[rowb(2 * MW, mem_blk, RM), fullb((M, MW)), fullb((M, MW)), fullb((1, MHD))],
        out_shape=[jax.ShapeDtypeStruct((N, c.P), _BF), jax.ShapeDtypeStruct((M, MW), F32),
                   jax.ShapeDtypeStruct((M, MW), F32), jax.ShapeDtypeStruct((1, MHD), F32)],
        input_output_aliases={0: 0}, compiler_params=_cparams(("arbitrary",)),
    )(dproj2, proj, d_amem, mem_k, mem_v, mem_qn_g)

    def memk_bwd_kern(kv_ref, dk_ref, dv_ref, g_ref, o_ref, dg_ref):
        dg = jnp.zeros((1, MHD), F32)
        for hh in range(MH):
            sl = slice(hh * MHD, (hh + 1) * MHD)
            kh, r = _rms(kv_ref[:, sl], MHD)
            dkr, dgp = _rms_bwd(kh, r, g_ref[...], dk_ref[:, sl], MHD)
            o_ref[:, sl] = dkr.astype(o_ref.dtype)
            dg += jnp.sum(dgp, axis=0, keepdims=True)
        o_ref[:, MW:] = dv_ref[...].astype(o_ref.dtype)
        dg_ref[...] = dg

    d_kvm, g_mem_kn = pl.pallas_call(
        memk_bwd_kern, name="memk_bwd", grid=(1,),
        in_specs=[fullb((M, 2 * MW)), fullb((M, MW)), fullb((M, MW)), fullb((1, MHD))],
        out_specs=[fullb((M, 2 * MW)), fullb((1, MHD))],
        out_shape=[jax.ShapeDtypeStruct((M, 2 * MW), _BF), jax.ShapeDtypeStruct((1, MHD), F32)],
        compiler_params=_cparams(("arbitrary",)),
    )(kvm, d_memk, d_memv, mem_kn_g)
    G['w_mem_kv'] = _mm(memn, d_kvm, ta=True, name="mm_dwmkv", out_dtype=_BF)
    d_memn = _mm(d_kvm, wmkv, tb=True, name="mm_dmemn", out_dtype=F32)

    def memnorm_bwd_kern(x_ref, d_ref, dg_ref):
        xh, _ = _rms(x_ref[...], D)
        dg_ref[...] = jnp.sum(d_ref[...] * xh, axis=0, keepdims=True)

    g_mem_norm = pl.pallas_call(
        memnorm_bwd_kern, name="memnorm_bwd", grid=(1,),
        in_specs=[fullb((M, D)), fullb((M, D))], out_specs=fullb((1, D)),
        out_shape=jax.ShapeDtypeStruct((1, D), F32), compiler_params=_cparams(("arbitrary",)),
    )(memx, d_memn)

    def gate_bwd_kern(dp_any, da_ref, y_ref, z_ref, dy_ref, dz_ref):
        z = z_ref[...].astype(F32)
        da = da_ref[...].astype(F32)
        sg = _sig(z)
        dy_ref[...] = (da * (z * sg)).astype(dy_ref.dtype)
        dz_ref[...] = (da * y_ref[...].astype(F32) * (sg * (1.0 + z * (1.0 - sg)))).astype(dz_ref.dtype)

    d_mlay, dproj4 = pl.pallas_call(
        gate_bwd_kern, name="gate_mla_bwd", grid=(nr2,),
        in_specs=[any_spec, rowb2(HV, 0), rowb2(HV, 0), rowb2(HV, mz_blk)],
        out_specs=[rowb2(HV, 0), rowb2(HV, mz_blk)],
        out_shape=[jax.ShapeDtypeStruct((N, HV), _BF), jax.ShapeDtypeStruct((N, c.P), _BF)],
        input_output_aliases={0: 1}, compiler_params=_cparams(("parallel",)),
    )(dproj3, d_amla, mla_y, proj)

    def attn_bwd_kern(q_ref, k_ref, v_ref, o_ref, do_ref, lse_ref, dq_ref, dk_ref, dv_ref, dq_sc, dl_sc, dk_sc, dv_sc):
        i = pl.program_id(1)
        delta = jnp.sum(do_ref[...].astype(F32) * o_ref[...].astype(F32), axis=1, keepdims=True)
        dl_sc[...] = jnp.broadcast_to(delta, dl_sc.shape)
        dq_sc[...] = jnp.zeros(dq_sc.shape, F32)

        def step(t, diag):
            rows = pl.ds(pl.multiple_of(t * B, B), B)
            rs = live_rows(diag)
            q, do = q_ref[0, rs, :], do_ref[rs, :]
            k = k_ref[0, rows, :]
            s = _dot_nt(q, k)
            if diag is not None:
                s = diag_mask(s)
            dpm = _dot_nt(do, v_ref[rows, :])
            lse_t, dl = lse_ref[0, rs, :], dl_sc[rs, :]
            ps, dss = [], []
            for cb in range(B // LANES):
                cols = slice(cb * LANES, (cb + 1) * LANES)
                p_cb = jnp.exp2(s[:, cols] - lse_t)
                ps.append(p_cb.astype(_BF))
                dss.append((p_cb * (dpm[:, cols] - dl)).astype(_BF))
            p, ds = jnp.concatenate(ps, axis=1), jnp.concatenate(dss, axis=1)
            dvp = _dot_tn(p, do)
            dkp = _dot_tn(ds, q)
            if diag is not None:
                dk_sc[rows, :] = dkp
                dv_sc[rows, :] = dvp
            else:
                dk_sc[rows, :] += dkp
                dv_sc[rows, :] += dvp
            dq_sc[rs, :] += _dot(ds, k)

        def run(first, count, n_diag):
            for u in range(count):
                step(first + u, u - (count - n_diag) if u >= count - n_diag else None)

        key_block_plan(i, run, ATT_UNROLL_BWD)
        dq_ref[0] = dq_sc[...].astype(dq_ref.dtype)

        @pl.when(i == nq - 1)
        def _():
            dk_ref[0] = dk_sc[...].astype(dk_ref.dtype)
            dv_ref[0] = dv_sc[...].astype(dv_ref.dtype)

    d_qcat, d_kcat, d_v = pl.pallas_call(
        attn_bwd_kern, name="attn_bwd", grid=(H, nq),
        in_specs=[q_blk, k_head, v_head, o_blk, o_blk, lse_blk],
        out_specs=[pl.BlockSpec((1, BQ, HW), lambda hh, i: (hh, i, 0)),
                   pl.BlockSpec((1, N, HW), lambda hh, i: (hh, 0, 0)),
                   pl.BlockSpec((1, N, LANES), lambda hh, i: (hh, 0, 0))],
        out_shape=[jax.ShapeDtypeStruct((H, N, HW), _BF), jax.ShapeDtypeStruct((H, N, HW), _BF),
                   jax.ShapeDtypeStruct((H, N, LANES), _BF)],
        scratch_shapes=[pltpu.VMEM((BQ, HW), F32), pltpu.VMEM((BQ, LANES), F32), pltpu.VMEM((N, HW), F32),
                        pltpu.VMEM((N, LANES), F32)],
        compiler_params=_cparams(("parallel", "arbitrary")),
    )(q_cat, k_cat, kv, mla_y, d_mlay, lse)

    def q_prep_bwd_kern(q_ref, dq_ref, t_ref, gn_ref, gr_ref, o_ref, dgn_ref, dgr_ref):
        @pl.when(jnp.logical_and(pl.program_id(0) == 0, pl.program_id(1) == 0))
        def _():
            dgn_ref[...] = jnp.zeros(dgn_ref.shape, F32)
            dgr_ref[...] = jnp.zeros(dgr_ref.shape, F32)

        for g in range(HG):
            blk = q_ref[:, g * HW:(g + 1) * HW].astype(F32)
            d = dq_ref[g].astype(F32)
            nh, rn = _rms128(blk[:, :LANES], c.NOPE)
            rhat, rr = _rms128(blk[:, LANES:], c.ROPE)
            dn, dgn = _rms128_bwd(nh, rn, gn_ref[...], d[:, :LANES], c.NOPE)
            drot = _rope_t(d[:, LANES:], t_ref[0], t_ref[1], t_ref[2])
            dr, dgr = _rms128_bwd(rhat, rr, gr_ref[...], drot, c.ROPE)
            o_ref[:, g * HW:(g + 1) * HW] = jnp.concatenate([dn, dr], axis=1).astype(o_ref.dtype)
            dgn_ref[...] += jnp.sum(dgn, axis=0, keepdims=True)
            dgr_ref[...] += jnp.sum(dgr, axis=0, keepdims=True)

    d_qp, g_qn_nope, g_qn_rope = pl.pallas_call(
        q_prep_bwd_kern, name="q_prep_bwd", grid=(nrp, nhg),
        in_specs=[heads_in, heads_out, tabs_of(RP), fullb((1, LANES)), fullb((1, LANES))],
        out_specs=[heads_in, fullb((1, LANES)), fullb((1, LANES))],
        out_shape=[jax.ShapeDtypeStruct((N, H * HW), _BF), jax.ShapeDtypeStruct((1, LANES), F32),
                   jax.ShapeDtypeStruct((1, LANES), F32)],
        compiler_params=_cparams(("arbitrary", "arbitrary")),
    )(q_p, d_qcat, tabs, g_qn * c.scale, g_qr * c.scale)
    g_qn_nope, g_qn_rope = g_qn_nope * c.scale, g_qn_rope * c.scale

    def k_prep_bwd_kern(kv_ref, dk_ref, dv_ref, gn_ref, o_ref, dkr_ref, dgn_ref):
        hg = pl.program_id(1)

        @pl.when(jnp.logical_and(pl.program_id(0) == 0, hg == 0))
        def _():
            dgn_ref[...] = jnp.zeros(dgn_ref.shape, F32)

        @pl.when(hg == 0)
        def _():
            dkr_ref[...] = jnp.zeros(dkr_ref.shape, F32)

        dkr = jnp.zeros((RP, LANES), F32)
        for g in range(HG):
            dk = dk_ref[g].astype(F32) * (1.0 / LOG2E)
            kn, r = _rms128(kv_ref[:, g * HW:g * HW + LANES].astype(F32), c.NOPE)
            dkn, dgn = _rms128_bwd(kn, r, gn_ref[...], dk[:, :LANES], c.NOPE)
            o_ref[:, g * HW:(g + 1) * HW] = jnp.concatenate([dkn.astype(o_ref.dtype), dv_ref[g]], axis=1)
            dgn_ref[...] += jnp.sum(dgn, axis=0, keepdims=True)
            dkr += dk[:, LANES:]
        dkr_ref[...] += dkr

    d_kv, d_krsum, g_kn_nope = pl.pallas_call(
        k_prep_bwd_kern, name="k_prep_bwd", grid=(nrp, nhg),
        in_specs=[heads_in, heads_out, pl.BlockSpec((HG, RP, LANES), lambda i, hg: (hg, i, 0)), fullb((1, LANES))],
        out_specs=[heads_in, pl.BlockSpec((RP, LANES), lambda i, hg: (i, 0)), fullb((1, LANES))],
        out_shape=[jax.ShapeDtypeStruct((N, H * HW), _BF), jax.ShapeDtypeStruct((N, LANES), F32),
                   jax.ShapeDtypeStruct((1, LANES), F32)],
        compiler_params=_cparams(("arbitrary", "arbitrary")),
    )(kv, d_kcat, d_v, g_kn)

    def krope_bwd_kern(p_ref, d_ref, t_ref, g_ref, o_ref, dg_ref):
        @pl.when(pl.program_id(0) == 0)
        def _():
            dg_ref[...] = jnp.zeros(dg_ref.shape, F32)

        xh, r = _rms128(p_ref[...].astype(F32), c.ROPE)
        drot = _rope_t(d_ref[...], t_ref[0], t_ref[1], t_ref[2])
        dx, dg = _rms128_bwd(xh, r, g_ref[...], drot, c.ROPE)
        o_ref[...] = dx.astype(o_ref.dtype)
        dg_ref[...] += jnp.sum(dg, axis=0, keepdims=True)

    d_kr, g_kn_rope = pl.pallas_call(
        krope_bwd_kern, name="krope_bwd", grid=(nr2,),
        in_specs=[rowb2(LANES, 0), rowb2(LANES, 0), tabs_of(R2), fullb((1, LANES))],
        out_specs=[rowb2(LANES, 0), fullb((1, LANES))],
        out_shape=[jax.ShapeDtypeStruct((N, LANES), _BF), jax.ShapeDtypeStruct((1, LANES), F32)],
        compiler_params=_cparams(("arbitrary",)),
    )(kr_raw, d_krsum, tabs, g_kr)
    dproj5 = dproj4

    g_wuq_p = _mm(cqn, d_qp, ta=True, name="mm_dwuq", out_dtype=_BF, bk=4096)
    G['w_uq'] = g_wuq_p.reshape(QL, H, HW)[:, :, :c.NOPE + c.ROPE].reshape(QL, H * (c.NOPE + c.ROPE))
    d_cqn = _mm(d_qp, wuq_pT, name="mm_dcqn", out_dtype=F32, bk=4096)
    G['w_ukv'] = _mm(ckvn, d_kv, ta=True, name="mm_dwukv", out_dtype=_BF, bk=4096)
    d_ckvn = _mm(d_kv, wukvT, name="mm_dckvn", out_dtype=F32, bk=4096)

    def to_blocks(n, g):
        if n in COL_SHARDED:
            return jnp.transpose(g.reshape(g.shape[0], NDEV, -1), (1, 0, 2))
        return g.reshape(NDEV, -1, g.shape[1])

    def landing(b):
        return lax.empty(b.shape, b.dtype)

    early = [n for n in BIG if n != 'w_in']
    blocks_e = [to_blocks(n, G[n]) for n in early]
    xe = _split_start(blocks_e, [landing(b) for b in blocks_e], False, "xchg_early_start")
    gq_after = mla_q_norm_g + xe[4][0:1, 0:1]

    def lora_bwd_kern(dp_any, p_ref, dq_ref, dkv_ref, gq_ref, gkv_ref, o_ref, dgq_ref, dgkv_ref):
        @pl.when(pl.program_id(0) == 0)
        def _():
            dgq_ref[...] = jnp.zeros(dgq_ref.shape, F32)
            dgkv_ref[...] = jnp.zeros(dgkv_ref.shape, F32)

        blk = p_ref[...].astype(F32)
        qh, rq = _rms(blk[:, :QL], QL)
        kh, rk = _rms(blk[:, QL:], KVL)
        dq, dgq = _rms_bwd(qh, rq, gq_ref[...], dq_ref[...], QL)
        dk, dgk = _rms_bwd(kh, rk, gkv_ref[...], dkv_ref[...], KVL)
        o_ref[:, :QL] = dq.astype(o_ref.dtype)
        o_ref[:, QL:] = dk.astype(o_ref.dtype)
        dgq_ref[...] += jnp.sum(dgq, axis=0, keepdims=True)
        dgkv_ref[...] += jnp.sum(dgk, axis=0, keepdims=True)

    dproj6, g_q_norm, g_kv_norm = pl.pallas_call(
        lora_bwd_kern, name="lora_bwd", grid=(nr2,),
        in_specs=[any_spec, rowb2(QL + KVL, lora_blk), rowb2(QL, 0), rowb2(KVL, 0), fullb((1, QL)), fullb((1, KVL))],
        out_specs=[rowb2(QL + KVL, lora_blk), fullb((1, QL)), fullb((1, KVL))],
        out_shape=[jax.ShapeDtypeStruct((N, c.P), _BF), jax.ShapeDtypeStruct((1, QL), F32),
                   jax.ShapeDtypeStruct((1, KVL), F32)],
        input_output_aliases={0: 0}, compiler_params=_cparams(("arbitrary",)),
    )(dproj5, proj, d_cqn, d_ckvn, gq_after, mla_kv_norm_g)

    g_win_p = _mm(h, dproj6, ta=True, name="mm_dwin", out_dtype=_BF, bk=4096)
    g_wkr = _mm(h, d_kr, ta=True, name="mm_dwkr", out_dtype=_BF, bk=4096)
    blocks_w = [_win_blocks(g_win_p, g_wkr, c)]
    xw = _split_start(blocks_w, [landing(b) for b in blocks_w], False, "xchg_win_start")
    d_h = _mm(dproj6, win_pT, name="mm_dh", out_dtype=F32, bk=3072, after=xw[4], plus=(d_kr, w_krT))

    def final_bwd_kern(x_ref, g_ref, dh_ref, dy_ref, gx_ref, dg_ref):
        @pl.when(pl.program_id(0) == 0)
        def _():
            dg_ref[...] = jnp.zeros(dg_ref.shape, F32)

        xh, r = _rms(x_ref[...], D)
        dx, dg = _rms_bwd(xh, r, g_ref[...], dh_ref[...], D)
        gx_ref[...] = dy_ref[...].astype(F32) + dx
        dg_ref[...] += jnp.sum(dg, axis=0, keepdims=True)

    grad_x, g_norm = pl.pallas_call(
        final_bwd_kern, name="final_bwd", grid=(nr2,),
        in_specs=[rowb2(D, 0), fullb((1, D)), rowb2(D, 0), rowb2(D, 0)],
        out_specs=[rowb2(D, 0), fullb((1, D))],
        out_shape=[jax.ShapeDtypeStruct((N, D), F32), jax.ShapeDtypeStruct((1, D), F32)],
        compiler_params=_cparams(("arbitrary",)),
    )(xs, norm_g, d_h, dyb)

    res = [{}, {}, {}, {}]

    def adam_into(n, parts):
        outs = _adam(parts, W[n][0], Mo[n][0], Vo[n][0], "adam_" + n)
        for k in range(4):
            res[k][n] = outs[k][None]
        return outs[0]

    recv_e = _split_wait(xe, False, grad_x, "xchg_early_wait")
    last = [adam_into(n, parts) for n, parts in zip(early, recv_e)][-1]
    recv_w = _split_wait(xw, False, last, "xchg_win_wait")
    adam_into('w_in', recv_w[0])

    small_g = {'norm_g': g_norm, 'mla_q_norm_g': g_q_norm, 'mla_kv_norm_g': g_kv_norm,
               'mla_qn_nope_g': g_qn_nope, 'mla_qn_rope_g': g_qn_rope[:, :c.ROPE], 'mla_kn_nope_g': g_kn_nope,
               'mla_kn_rope_g': g_kn_rope[:, :c.ROPE], 'mem_norm_g': g_mem_norm, 'mem_qn_g': g_mem_qn,
               'mem_kn_g': g_mem_kn}
    small_part = _pack([small_g[n] for n in SMALL] + [g_convw[0:3, :]], 0)
    small_all = _all_gather([small_part], "ag_small_grads")[0]
    small_shapes = [W[n].shape for n in SMALL]
    pieces = _unpack(small_all, small_shapes + [(3, CW)])
    cw8 = CW // NDEV
    conv_mine = lax.dynamic_slice_in_dim(pieces[-1].reshape(NDEV, 3, NDEV, cw8), me, 1, axis=2)[:, :, 0, :]
    sm_parts = _pack(pieces[:-1] + [conv_mine], 1)
    sm_names = SMALL + ['conv_w']
    sm_shapes = small_shapes + [(3, cw8)]
    w_sm = _pack([W[n] for n in SMALL] + [conv_w[0]], 0)
    m_sm = _pack([Mo[n] for n in SMALL] + [m_conv_w[0]], 0)
    v_sm = _pack([Vo[n] for n in SMALL] + [v_conv_w[0]], 0)
    outs_sm = [_unpack(o, sm_shapes) for o in _adam(sm_parts, w_sm, m_sm, v_sm, "adam_small")]
    for k in range(4):
        for n, a in zip(sm_names, outs_sm[k]):
            res[k][n] = a[None] if n == 'conv_w' else a
    return (loss, grad_x[None], *[res[0][n] for n in WEIGHTS], *[res[1][n] for n in WEIGHTS],
            *[res[2][n] for n in WEIGHTS], *[res[3][n] for n in WEIGHTS])
```
